```python
import jax, jax.numpy as jnp
from jax import lax
import numpy as np

D_MODEL = 1024
BATCH = 8
SEQ = 8192
DEPTH = 1

CHUNK = 64
PLE_DIM = 256
HG_HEADS = 4
HG_DK = 128
HG_DV = (D_MODEL // 2) // HG_HEADS
HG_WIDTH = HG_HEADS * HG_DV
ML_HEADS = 4
ML_DV = (D_MODEL // 2) // ML_HEADS
ML_DQK = ML_DV // 2
ML_WIDTH = ML_HEADS * ML_DV
MIX_WIDTH = HG_WIDTH + ML_WIDTH
CONV_K = 4
D_FF = ((8 * D_MODEL + 3 * 256 - 1) // (3 * 256)) * 256
ALPHA = float((2 * DEPTH) ** 0.25)
BETA = float((8 * DEPTH) ** -0.25)
LN_EPS = 1e-5
RMS_EPS = 1e-6
ML_I_BIAS = -2.0
ML_F_BIAS_LO = 3.0
ML_F_BIAS_HI = 6.0
PROJ_SIZES = (
    HG_HEADS * HG_DK,
    HG_HEADS * HG_DK,
    HG_WIDTH,
    HG_WIDTH,
    ML_HEADS * ML_DQK,
    ML_HEADS * ML_DQK,
    ML_WIDTH,
    ML_WIDTH,
    ML_HEADS,
    ML_HEADS,
)
PROJ_WIDTH = sum(PROJ_SIZES)

kernel_name = "hymba_hgrn2_mlstm_deepnorm_block"


def _split_cols(u):
    idx, acc = [], 0
    for s in PROJ_SIZES[:-1]:
        acc += s
        idx.append(acc)
    return jnp.split(u, idx, axis=-1)


def layer_norm(x, g, b):
    xf = x.astype(jnp.float32)
    mu = jnp.mean(xf, -1, keepdims=True)
    var = jnp.mean(jnp.square(xf - mu), -1, keepdims=True)
    return ((xf - mu) * lax.rsqrt(var + LN_EPS)).astype(x.dtype) * g + b


def head_rms_norm(h, g):
    hf = h.astype(jnp.float32)
    hf = hf * lax.rsqrt(jnp.mean(hf * hf, -1, keepdims=True) + RMS_EPS)
    B, S, H, Dh = h.shape
    return hf.reshape(B, S, H * Dh).astype(g.dtype) * g


def causal_conv(x, w, b):
    S = x.shape[1]
    xp = jnp.pad(x, ((0, 0), (CONV_K - 1, 0), (0, 0)))
    out = xp[:, 0:S] * w[0]
    for k in range(1, CONV_K):
        out = out + xp[:, k:k + S] * w[k]
    return out + b


def to_chunks(t):
    B, S, H, D = t.shape
    return t.reshape(B, S // CHUNK, CHUNK, H, D).transpose(1, 0, 3, 2, 4)


def gate_chunks(t):
    B, S, H = t.shape
    return t.reshape(B, S // CHUNK, CHUNK, H).transpose(1, 0, 3, 2)


def from_chunks(t):
    N, B, H, C, D = t.shape
    return t.transpose(1, 0, 3, 2, 4).reshape(B, N * C, H, D)


def hgrn2_mixer(q, log_f, k, v):
    B, S, H, DK = q.shape
    DV = v.shape[-1]
    mask = jnp.tril(jnp.ones((CHUNK, CHUNK), dtype=bool))[:, :, None]

    def step(state, inp):
        q_, g_, k_, v_ = inp
        b = jnp.cumsum(g_, axis=2)
        diff = b[:, :, :, None, :] - b[:, :, None, :, :]
        decay = jnp.exp(jnp.where(mask, diff, -jnp.inf))
        scores = jnp.einsum('bhtd,bhsd,bhtsd->bhts', q_, k_, decay)
        o_intra = jnp.einsum('bhts,bhsv->bhtv', scores, v_)
        o_inter = jnp.einsum('bhtd,bhdv->bhtv', q_ * jnp.exp(b), state)
        b_last = b[:, :, -1:, :]
        k_dec = k_ * jnp.exp(b_last - b)
        new_state = jnp.exp(b_last[:, :, 0, :])[..., None] * state + jnp.einsum('bhsd,bhsv->bhdv', k_dec, v_)
        return new_state, o_intra + o_inter

    state0 = jnp.zeros((B, H, DK, DV), jnp.float32)
    _, o = lax.scan(step, state0, (to_chunks(q), to_chunks(log_f), to_chunks(k), to_chunks(v)))
    return from_chunks(o).astype(v.dtype)


def mlstm_mixer(q, k, v, i_pre, log_f):
    B, S, H, DQK = q.shape
    DV = v.shape[-1]
    q = q * (DQK ** -0.5)
    mask = jnp.tril(jnp.ones((CHUNK, CHUNK), dtype=bool))

    def step(carry, inp):
        C_st, n_st, m_st = carry
        q_, k_, v_, ig, lf = inp
        g = jnp.cumsum(lf, axis=-1)
        dmat = g[..., :, None] - g[..., None, :] + ig[..., None, :]
        dmat = jnp.where(mask, dmat, -jnp.inf)
        m_inter = g + m_st[..., None]
        m_t = jnp.maximum(m_inter, jnp.max(dmat, -1))
        w_intra = jnp.exp(dmat - m_t[..., None])
        w_inter = jnp.exp(m_inter - m_t)
        qk = jnp.einsum('bhtd,bhsd->bhts', q_, k_) * w_intra
        num = jnp.einsum('bhts,bhsv->bhtv', qk, v_) + w_inter[..., None] * jnp.einsum('bhtd,bhdv->bhtv', q_, C_st)
        den = jnp.sum(qk, -1) + w_inter * jnp.einsum('bhtd,bhd->bht', q_, n_st)
        h = num / jnp.maximum(jnp.abs(den), jnp.exp(-m_t))[..., None]
        g_last = g[..., -1]
        a = g_last[..., None] - g + ig
        m_new = jnp.maximum(g_last + m_st, jnp.max(a, -1))
        ws = jnp.exp(a - m_new[..., None])
        w_old = jnp.exp(g_last + m_st - m_new)
        C_new = w_old[..., None, None] * C_st + jnp.einsum('bhs,bhsd,bhsv->bhdv', ws, k_, v_)
        n_new = w_old[..., None] * n_st + jnp.einsum('bhs,bhsd->bhd', ws, k_)
        return (C_new, n_new, m_new), h

    carry0 = (jnp.zeros((B, H, DQK, DV), jnp.float32),
              jnp.zeros((B, H, DQK), jnp.float32),
              jnp.zeros((B, H), jnp.float32))
    _, h = lax.scan(step, carry0, (to_chunks(q), to_chunks(k), to_chunks(v),
                                   gate_chunks(i_pre), gate_chunks(log_f)))
    return from_chunks(h).astype(v.dtype)


def _fwd_setup_inputs(seed: int = 0) -> dict:
    key = jax.random.key(seed)
    ks = jax.random.split(key, 20)
    f32 = jnp.float32
    nrm = lambda k, shape, scale: jax.random.normal(k, shape, f32) * scale
    x = nrm(ks[0], (BATCH, SEQ, D_MODEL), 1.0)
    p = nrm(ks[1], (DEPTH, BATCH, SEQ, PLE_DIM), 1.0)
    w_in = nrm(ks[2], (DEPTH, D_MODEL, PROJ_WIDTH), D_MODEL ** -0.5)
    b_in = nrm(ks[3], (DEPTH, PROJ_WIDTH), 0.02)
    ig_off = PROJ_WIDTH - 2 * ML_HEADS
    fg_off = PROJ_WIDTH - ML_HEADS
    b_in = b_in.at[:, ig_off:fg_off].add(ML_I_BIAS)
    b_in = b_in.at[:, fg_off:].add(jnp.linspace(ML_F_BIAS_LO, ML_F_BIAS_HI, ML_HEADS, dtype=f32))
    hg_lb_logits = nrm(ks[4], (DEPTH + 1, HG_HEADS * HG_DK), 0.5)
    ml_conv_w = nrm(ks[5], (DEPTH, CONV_K, 2 * ML_HEADS * ML_DQK), CONV_K ** -0.5)
    ml_conv_b = nrm(ks[6], (DEPTH, 2 * ML_HEADS * ML_DQK), 0.02)
    hg_norm_g = 1.0 + nrm(ks[7], (DEPTH, HG_WIDTH), 0.02)
    ml_norm_g = 1.0 + nrm(ks[8], (DEPTH, ML_WIDTH), 0.02)
    w_out = nrm(ks[9], (DEPTH, MIX_WIDTH, D_MODEL), BETA * MIX_WIDTH ** -0.5)
    ln1_g = 1.0 + nrm(ks[10], (DEPTH, D_MODEL), 0.02)
    ln1_b = nrm(ks[11], (DEPTH, D_MODEL), 0.02)
    w_ffn_gate = nrm(ks[12], (DEPTH, D_MODEL, D_FF), D_MODEL ** -0.5)
    w_ffn_up = nrm(ks[13], (DEPTH, D_MODEL, D_FF), D_MODEL ** -0.5)
    w_ffn_down = nrm(ks[14], (DEPTH, D_FF, D_MODEL), BETA * D_FF ** -0.5)
    ln2_g = 1.0 + nrm(ks[15], (DEPTH, D_MODEL), 0.02)
    ln2_b = nrm(ks[16], (DEPTH, D_MODEL), 0.02)
    ple_w_proj = nrm(ks[17], (DEPTH, PLE_DIM, D_MODEL), PLE_DIM ** -0.5)
    ple_w_gate = nrm(ks[18], (DEPTH, D_MODEL, D_MODEL), D_MODEL ** -0.5)
    ple_b_gate = nrm(ks[19], (DEPTH, D_MODEL), 0.02)
    return {"x": x, "p": p, "w_in": w_in, "b_in": b_in, "hg_lb_logits": hg_lb_logits,
            "ml_conv_w": ml_conv_w, "ml_conv_b": ml_conv_b, "hg_norm_g": hg_norm_g,
            "ml_norm_g": ml_norm_g, "w_out": w_out, "ln1_g": ln1_g, "ln1_b": ln1_b,
            "w_ffn_gate": w_ffn_gate, "w_ffn_up": w_ffn_up, "w_ffn_down": w_ffn_down,
            "ln2_g": ln2_g, "ln2_b": ln2_b, "ple_w_proj": ple_w_proj,
            "ple_w_gate": ple_w_gate, "ple_b_gate": ple_b_gate}


def _fwd_reference(x, p, w_in, b_in, hg_lb_logits, ml_conv_w, ml_conv_b, hg_norm_g, ml_norm_g,
              w_out, ln1_g, ln1_b, w_ffn_gate, w_ffn_up, w_ffn_down, ln2_g, ln2_b,
              ple_w_proj, ple_w_gate, ple_b_gate):
    B, S, _ = x.shape
    lower_bounds = jnp.cumsum(jax.nn.softmax(hg_lb_logits.astype(jnp.float32), axis=0), axis=0)
    for i in range(DEPTH):
        u = x @ w_in[i] + b_in[i]
        hq, hf, hv, hgate, mq, mk, mv, mo, mig, mfg = _split_cols(u)

        lb = lower_bounds[i]
        log_f = jnp.logaddexp(jnp.log(lb), jnp.log1p(-lb) + jax.nn.log_sigmoid(hf.astype(jnp.float32)))
        k_hg = -jnp.expm1(log_f)
        o_hg = hgrn2_mixer(jax.nn.silu(hq).reshape(B, S, HG_HEADS, HG_DK),
                           log_f.reshape(B, S, HG_HEADS, HG_DK),
                           k_hg.reshape(B, S, HG_HEADS, HG_DK),
                           hv.reshape(B, S, HG_HEADS, HG_DV))
        o_hg = head_rms_norm(o_hg, hg_norm_g[i]) * jax.nn.silu(hgate)

        qk_c = jax.nn.silu(causal_conv(jnp.concatenate([mq, mk], -1), ml_conv_w[i], ml_conv_b[i]))
        mq_c, mk_c = jnp.split(qk_c, 2, axis=-1)
        h_ml = mlstm_mixer(mq_c.reshape(B, S, ML_HEADS, ML_DQK),
                           mk_c.reshape(B, S, ML_HEADS, ML_DQK),
                           mv.reshape(B, S, ML_HEADS, ML_DV),
                           mig.astype(jnp.float32),
                           jax.nn.log_sigmoid(mfg.astype(jnp.float32)))
        o_ml = head_rms_norm(h_ml, ml_norm_g[i]) * jax.nn.sigmoid(mo)

        mix = jnp.concatenate([o_hg, o_ml], -1) @ w_out[i]
        x = layer_norm(ALPHA * x + mix, ln1_g[i], ln1_b[i])

        ffn = (jax.nn.silu(x @ w_ffn_gate[i]) * (x @ w_ffn_up[i])) @ w_ffn_down[i]
        x = layer_norm(ALPHA * x + ffn, ln2_g[i], ln2_b[i])

        x = x + jax.nn.sigmoid(x @ ple_w_gate[i] + ple_b_gate[i]) * (p[i] @ ple_w_proj[i])
    return x


import jax as _jax
import jax.numpy as _jnp

TWIN_FORMAT = 'train_step'
FWD_PARAMS = ['x', 'p', 'w_in', 'b_in', 'hg_lb_logits', 'ml_conv_w', 'ml_conv_b', 'hg_norm_g', 'ml_norm_g', 'w_out', 'ln1_g', 'ln1_b', 'w_ffn_gate', 'w_ffn_up', 'w_ffn_down', 'ln2_g', 'ln2_b', 'ple_w_proj', 'ple_w_gate', 'ple_b_gate']
TWIN_WEIGHTS = ['w_in', 'b_in', 'hg_lb_logits', 'ml_conv_w', 'ml_conv_b', 'hg_norm_g', 'ml_norm_g', 'w_out', 'ln1_g', 'ln1_b', 'w_ffn_gate', 'w_ffn_up', 'w_ffn_down', 'ln2_g', 'ln2_b', 'ple_w_proj', 'ple_w_gate', 'ple_b_gate']
TWIN_DIFF_INPUT = 'x'
TWIN_INPUTS = ['x', 'p', 'w_in', 'b_in', 'hg_lb_logits', 'ml_conv_w', 'ml_conv_b', 'hg_norm_g', 'ml_norm_g', 'w_out', 'ln1_g', 'ln1_b', 'w_ffn_gate', 'w_ffn_up', 'w_ffn_down', 'ln2_g', 'ln2_b', 'ple_w_proj', 'ple_w_gate', 'ple_b_gate', 'loss_target', 'm_w_in', 'm_b_in', 'm_hg_lb_logits', 'm_ml_conv_w', 'm_ml_conv_b', 'm_hg_norm_g', 'm_ml_norm_g', 'm_w_out', 'm_ln1_g', 'm_ln1_b', 'm_w_ffn_gate', 'm_w_ffn_up', 'm_w_ffn_down', 'm_ln2_g', 'm_ln2_b', 'm_ple_w_proj', 'm_ple_w_gate', 'm_ple_b_gate', 'v_w_in', 'v_b_in', 'v_hg_lb_logits', 'v_ml_conv_w', 'v_ml_conv_b', 'v_hg_norm_g', 'v_ml_norm_g', 'v_w_out', 'v_ln1_g', 'v_ln1_b', 'v_w_ffn_gate', 'v_w_ffn_up', 'v_w_ffn_down', 'v_ln2_g', 'v_ln2_b', 'v_ple_w_proj', 'v_ple_w_gate', 'v_ple_b_gate']
TWIN_OUTPUTS = ['loss', 'grad_x', 'grad_w_in', 'grad_b_in', 'grad_hg_lb_logits', 'grad_ml_conv_w', 'grad_ml_conv_b', 'grad_hg_norm_g', 'grad_ml_norm_g', 'grad_w_out', 'grad_ln1_g', 'grad_ln1_b', 'grad_w_ffn_gate', 'grad_w_ffn_up', 'grad_w_ffn_down', 'grad_ln2_g', 'grad_ln2_b', 'grad_ple_w_proj', 'grad_ple_w_gate', 'grad_ple_b_gate', 'delta_w_in', 'delta_b_in', 'delta_hg_lb_logits', 'delta_ml_conv_w', 'delta_ml_conv_b', 'delta_hg_norm_g', 'delta_ml_norm_g', 'delta_w_out', 'delta_ln1_g', 'delta_ln1_b', 'delta_w_ffn_gate', 'delta_w_ffn_up', 'delta_w_ffn_down', 'delta_ln2_g', 'delta_ln2_b', 'delta_ple_w_proj', 'delta_ple_w_gate', 'delta_ple_b_gate', 'new_m_w_in', 'new_m_b_in', 'new_m_hg_lb_logits', 'new_m_ml_conv_w', 'new_m_ml_conv_b', 'new_m_hg_norm_g', 'new_m_ml_norm_g', 'new_m_w_out', 'new_m_ln1_g', 'new_m_ln1_b', 'new_m_w_ffn_gate', 'new_m_w_ffn_up', 'new_m_w_ffn_down', 'new_m_ln2_g', 'new_m_ln2_b', 'new_m_ple_w_proj', 'new_m_ple_w_gate', 'new_m_ple_b_gate', 'new_v_w_in', 'new_v_b_in', 'new_v_hg_lb_logits', 'new_v_ml_conv_w', 'new_v_ml_conv_b', 'new_v_hg_norm_g', 'new_v_ml_norm_g', 'new_v_w_out', 'new_v_ln1_g', 'new_v_ln1_b', 'new_v_w_ffn_gate', 'new_v_w_ffn_up', 'new_v_w_ffn_down', 'new_v_ln2_g', 'new_v_ln2_b', 'new_v_ple_w_proj', 'new_v_ple_w_gate', 'new_v_ple_b_gate']
TWIN_LEAF_KINDS = {'loss': 'loss', 'grad_x': 'grad_x', 'grad_w_in': 'grad_w', 'grad_b_in': 'grad_w', 'grad_hg_lb_logits': 'grad_w', 'grad_ml_conv_w': 'grad_w', 'grad_ml_conv_b': 'grad_w', 'grad_hg_norm_g': 'grad_w', 'grad_ml_norm_g': 'grad_w', 'grad_w_out': 'grad_w', 'grad_ln1_g': 'grad_w', 'grad_ln1_b': 'grad_w', 'grad_w_ffn_gate': 'grad_w', 'grad_w_ffn_up': 'grad_w', 'grad_w_ffn_down': 'grad_w', 'grad_ln2_g': 'grad_w', 'grad_ln2_b': 'grad_w', 'grad_ple_w_proj': 'grad_w', 'grad_ple_w_gate': 'grad_w', 'grad_ple_b_gate': 'grad_w', 'delta_w_in': 'delta_w', 'delta_b_in': 'delta_w', 'delta_hg_lb_logits': 'delta_w', 'delta_ml_conv_w': 'delta_w', 'delta_ml_conv_b': 'delta_w', 'delta_hg_norm_g': 'delta_w', 'delta_ml_norm_g': 'delta_w', 'delta_w_out': 'delta_w', 'delta_ln1_g': 'delta_w', 'delta_ln1_b': 'delta_w', 'delta_w_ffn_gate': 'delta_w', 'delta_w_ffn_up': 'delta_w', 'delta_w_ffn_down': 'delta_w', 'delta_ln2_g': 'delta_w', 'delta_ln2_b': 'delta_w', 'delta_ple_w_proj': 'delta_w', 'delta_ple_w_gate': 'delta_w', 'delta_ple_b_gate': 'delta_w', 'new_m_w_in': 'new_m', 'new_m_b_in': 'new_m', 'new_m_hg_lb_logits': 'new_m', 'new_m_ml_conv_w': 'new_m', 'new_m_ml_conv_b': 'new_m', 'new_m_hg_norm_g': 'new_m', 'new_m_ml_norm_g': 'new_m', 'new_m_w_out': 'new_m', 'new_m_ln1_g': 'new_m', 'new_m_ln1_b': 'new_m', 'new_m_w_ffn_gate': 'new_m', 'new_m_w_ffn_up': 'new_m', 'new_m_w_ffn_down': 'new_m', 'new_m_ln2_g': 'new_m', 'new_m_ln2_b': 'new_m', 'new_m_ple_w_proj': 'new_m', 'new_m_ple_w_gate': 'new_m', 'new_m_ple_b_gate': 'new_m', 'new_v_w_in': 'new_v', 'new_v_b_in': 'new_v', 'new_v_hg_lb_logits': 'new_v', 'new_v_ml_conv_w': 'new_v', 'new_v_ml_conv_b': 'new_v', 'new_v_hg_norm_g': 'new_v', 'new_v_ml_norm_g': 'new_v', 'new_v_w_out': 'new_v', 'new_v_ln1_g': 'new_v', 'new_v_ln1_b': 'new_v', 'new_v_w_ffn_gate': 'new_v', 'new_v_w_ffn_up': 'new_v', 'new_v_w_ffn_down': 'new_v', 'new_v_ln2_g': 'new_v', 'new_v_ln2_b': 'new_v', 'new_v_ple_w_proj': 'new_v', 'new_v_ple_w_gate': 'new_v', 'new_v_ple_b_gate': 'new_v'}


def _forward(args):
    return _fwd_reference(*[args[k] for k in FWD_PARAMS])


def _output_shape():
    def fwd():
        inp = _fwd_setup_inputs(0)
        return _fwd_reference(*[inp[k] for k in FWD_PARAMS])
    out = _jax.eval_shape(fwd)
    return out.shape, out.dtype

N_MICROBATCH = 1
ADAM_LR = 0.001
ADAM_B1 = 0.9
ADAM_B2 = 0.999
ADAM_EPS = 1e-08
ADAM_WD = 0.01
ADAM_STEP = 10
PER_EXAMPLE_BATCH_AXIS = {'x': 0, 'p': 1, 'loss_target': 0}
SHARED_INPUTS = []
_WEIGHT_DTYPES = {'w_in': _jnp.float32, 'b_in': _jnp.float32, 'hg_lb_logits': _jnp.float32, 'ml_conv_w': _jnp.float32, 'ml_conv_b': _jnp.float32, 'hg_norm_g': _jnp.float32, 'ml_norm_g': _jnp.float32, 'w_out': _jnp.float32, 'ln1_g': _jnp.float32, 'ln1_b': _jnp.float32, 'w_ffn_gate': _jnp.float32, 'w_ffn_up': _jnp.float32, 'w_ffn_down': _jnp.float32, 'ln2_g': _jnp.float32, 'ln2_b': _jnp.float32, 'ple_w_proj': _jnp.float32, 'ple_w_gate': _jnp.float32, 'ple_b_gate': _jnp.float32}
MOMENT_SCALE = {'w_in': 1.172553e-01, 'b_in': 2.570770e+00, 'hg_lb_logits': 8.072808e-03, 'ml_conv_w': 1.108862e-01, 'ml_conv_b': 1.195008e-01, 'hg_norm_g': 9.192105e-02, 'ml_norm_g': 2.443201e-01, 'w_out': 2.808007e-01, 'ln1_g': 2.034220e+00, 'ln1_b': 6.365389e+00, 'w_ffn_gate': 5.548969e-02, 'w_ffn_up': 5.467355e-02, 'w_ffn_down': 1.521878e-01, 'ln2_g': 6.592736e+01, 'ln2_b': 6.899133e+00, 'ple_w_proj': 9.172822e-01, 'ple_w_gate': 2.054810e-01, 'ple_b_gate': 6.632431e+00}


def _to_microbatches(a, axis):
    t = _jnp.moveaxis(a, axis, 0)
    t = t.reshape((N_MICROBATCH, t.shape[0] // N_MICROBATCH) + t.shape[1:])
    return _jnp.moveaxis(t, 1, axis + 1)


def setup_inputs(seed: int = 0) -> dict:
    inp = _fwd_setup_inputs(seed)
    key = _jax.random.fold_in(_jax.random.key(seed), 7919)
    shape, _ = _output_shape()
    out = dict(inp)
    out["loss_target"] = _jax.random.normal(_jax.random.fold_in(key, 0), shape, _jnp.float32)
    for i, name in enumerate(TWIN_WEIGHTS):
        w = inp[name].astype(_jnp.float32)
        if MOMENT_SCALE is None:
            s = _jnp.sqrt(_jnp.mean(_jnp.square(w)) + 1e-30)
        else:
            s = MOMENT_SCALE[name]
        km, kv = _jax.random.split(_jax.random.fold_in(key, i + 1))
        out[name] = w
        out["m_" + name] = s * _jax.random.normal(km, w.shape, _jnp.float32)
        out["v_" + name] = (s * s) * _jax.random.uniform(kv, w.shape, _jnp.float32, 0.5, 1.5)
    if N_MICROBATCH > 1:
        for name, axis in PER_EXAMPLE_BATCH_AXIS.items():
            out[name] = _to_microbatches(out[name], axis)
    return {'x': out['x'], 'p': out['p'], 'w_in': out['w_in'], 'b_in': out['b_in'], 'hg_lb_logits': out['hg_lb_logits'], 'ml_conv_w': out['ml_conv_w'], 'ml_conv_b': out['ml_conv_b'], 'hg_norm_g': out['hg_norm_g'], 'ml_norm_g': out['ml_norm_g'], 'w_out': out['w_out'], 'ln1_g': out['ln1_g'], 'ln1_b': out['ln1_b'], 'w_ffn_gate': out['w_ffn_gate'], 'w_ffn_up': out['w_ffn_up'], 'w_ffn_down': out['w_ffn_down'], 'ln2_g': out['ln2_g'], 'ln2_b': out['ln2_b'], 'ple_w_proj': out['ple_w_proj'], 'ple_w_gate': out['ple_w_gate'], 'ple_b_gate': out['ple_b_gate'], 'loss_target': out['loss_target'], 'm_w_in': out['m_w_in'], 'm_b_in': out['m_b_in'], 'm_hg_lb_logits': out['m_hg_lb_logits'], 'm_ml_conv_w': out['m_ml_conv_w'], 'm_ml_conv_b': out['m_ml_conv_b'], 'm_hg_norm_g': out['m_hg_norm_g'], 'm_ml_norm_g': out['m_ml_norm_g'], 'm_w_out': out['m_w_out'], 'm_ln1_g': out['m_ln1_g'], 'm_ln1_b': out['m_ln1_b'], 'm_w_ffn_gate': out['m_w_ffn_gate'], 'm_w_ffn_up': out['m_w_ffn_up'], 'm_w_ffn_down': out['m_w_ffn_down'], 'm_ln2_g': out['m_ln2_g'], 'm_ln2_b': out['m_ln2_b'], 'm_ple_w_proj': out['m_ple_w_proj'], 'm_ple_w_gate': out['m_ple_w_gate'], 'm_ple_b_gate': out['m_ple_b_gate'], 'v_w_in': out['v_w_in'], 'v_b_in': out['v_b_in'], 'v_hg_lb_logits': out['v_hg_lb_logits'], 'v_ml_conv_w': out['v_ml_conv_w'], 'v_ml_conv_b': out['v_ml_conv_b'], 'v_hg_norm_g': out['v_hg_norm_g'], 'v_ml_norm_g': out['v_ml_norm_g'], 'v_w_out': out['v_w_out'], 'v_ln1_g': out['v_ln1_g'], 'v_ln1_b': out['v_ln1_b'], 'v_w_ffn_gate': out['v_w_ffn_gate'], 'v_w_ffn_up': out['v_w_ffn_up'], 'v_w_ffn_down': out['v_w_ffn_down'], 'v_ln2_g': out['v_ln2_g'], 'v_ln2_b': out['v_ln2_b'], 'v_ple_w_proj': out['v_ple_w_proj'], 'v_ple_w_gate': out['v_ple_w_gate'], 'v_ple_b_gate': out['v_ple_b_gate']}


def _loss(weights, diff, rest, loss_target):
    with _jax.named_scope("forward"):
        args = {**rest, TWIN_DIFF_INPUT: diff, **{k: w.astype(_WEIGHT_DTYPES[k]) for k, w in weights.items()}}
        y = _forward(args)
    with _jax.named_scope("loss_head"):
        err = _jnp.square(y.astype(_jnp.float32) - loss_target)
        return 0.5 * _jnp.sum(_jnp.mean(err, axis=-1)) if err.ndim else 0.5 * err


def _adamw(w, g, m, v):
    m = ADAM_B1 * m + (1.0 - ADAM_B1) * g
    v = ADAM_B2 * v + (1.0 - ADAM_B2) * _jnp.square(g)
    m_hat = m / (1.0 - ADAM_B1 ** ADAM_STEP)
    v_hat = v / (1.0 - ADAM_B2 ** ADAM_STEP)
    delta = -ADAM_LR * (m_hat / (_jnp.sqrt(v_hat) + ADAM_EPS) + ADAM_WD * w)
    return delta, m, v


def reference(x, p, w_in, b_in, hg_lb_logits, ml_conv_w, ml_conv_b, hg_norm_g, ml_norm_g, w_out, ln1_g, ln1_b, w_ffn_gate, w_ffn_up, w_ffn_down, ln2_g, ln2_b, ple_w_proj, ple_w_gate, ple_b_gate, loss_target, m_w_in, m_b_in, m_hg_lb_logits, m_ml_conv_w, m_ml_conv_b, m_hg_norm_g, m_ml_norm_g, m_w_out, m_ln1_g, m_ln1_b, m_w_ffn_gate, m_w_ffn_up, m_w_ffn_down, m_ln2_g, m_ln2_b, m_ple_w_proj, m_ple_w_gate, m_ple_b_gate, v_w_in, v_b_in, v_hg_lb_logits, v_ml_conv_w, v_ml_conv_b, v_hg_norm_g, v_ml_norm_g, v_w_out, v_ln1_g, v_ln1_b, v_w_ffn_gate, v_w_ffn_up, v_w_ffn_down, v_ln2_g, v_ln2_b, v_ple_w_proj, v_ple_w_gate, v_ple_b_gate):
    given = dict(x=x, p=p, w_in=w_in, b_in=b_in, hg_lb_logits=hg_lb_logits, ml_conv_w=ml_conv_w, ml_conv_b=ml_conv_b, hg_norm_g=hg_norm_g, ml_norm_g=ml_norm_g, w_out=w_out, ln1_g=ln1_g, ln1_b=ln1_b, w_ffn_gate=w_ffn_gate, w_ffn_up=w_ffn_up, w_ffn_down=w_ffn_down, ln2_g=ln2_g, ln2_b=ln2_b, ple_w_proj=ple_w_proj, ple_w_gate=ple_w_gate, ple_b_gate=ple_b_gate, loss_target=loss_target, m_w_in=m_w_in, m_b_in=m_b_in, m_hg_lb_logits=m_hg_lb_logits, m_ml_conv_w=m_ml_conv_w, m_ml_conv_b=m_ml_conv_b, m_hg_norm_g=m_hg_norm_g, m_ml_norm_g=m_ml_norm_g, m_w_out=m_w_out, m_ln1_g=m_ln1_g, m_ln1_b=m_ln1_b, m_w_ffn_gate=m_w_ffn_gate, m_w_ffn_up=m_w_ffn_up, m_w_ffn_down=m_w_ffn_down, m_ln2_g=m_ln2_g, m_ln2_b=m_ln2_b, m_ple_w_proj=m_ple_w_proj, m_ple_w_gate=m_ple_w_gate, m_ple_b_gate=m_ple_b_gate, v_w_in=v_w_in, v_b_in=v_b_in, v_hg_lb_logits=v_hg_lb_logits, v_ml_conv_w=v_ml_conv_w, v_ml_conv_b=v_ml_conv_b, v_hg_norm_g=v_hg_norm_g, v_ml_norm_g=v_ml_norm_g, v_w_out=v_w_out, v_ln1_g=v_ln1_g, v_ln1_b=v_ln1_b, v_w_ffn_gate=v_w_ffn_gate, v_w_ffn_up=v_w_ffn_up, v_w_ffn_down=v_w_ffn_down, v_ln2_g=v_ln2_g, v_ln2_b=v_ln2_b, v_ple_w_proj=v_ple_w_proj, v_ple_w_gate=v_ple_w_gate, v_ple_b_gate=v_ple_b_gate)
    weights = {n: given[n] for n in TWIN_WEIGHTS}
    shared = {n: given[n] for n in SHARED_INPUTS}
    per_example = {n: given[n] for n in ['x', 'p']}
    grad_fn = _jax.value_and_grad(_loss, argnums=(0, 1))

    def one_microbatch(ex, loss_target):
        ex = dict(ex)
        diff = ex.pop(TWIN_DIFF_INPUT)
        return grad_fn(weights, diff, {**shared, **ex}, loss_target)

    if N_MICROBATCH == 1:
        loss, (grad_w, grad_x) = one_microbatch(per_example, given["loss_target"])
    else:
        def body(carry, xs):
            loss_sum, grad_sum = carry
            l_k, (gw_k, gx_k) = one_microbatch(xs[0], xs[1])
            with _jax.named_scope("update"):
                return (loss_sum + l_k, _jax.tree.map(_jnp.add, grad_sum, gw_k)), gx_k

        init = (_jnp.zeros((), _jnp.float32), _jax.tree.map(_jnp.zeros_like, weights))
        (loss, grad_w), grad_x = _jax.lax.scan(body, init, (per_example, given["loss_target"]))
    with _jax.named_scope("update"):
        delta_w, new_m, new_v = {}, {}, {}
        for n in TWIN_WEIGHTS:
            delta_w[n], new_m[n], new_v[n] = _adamw(weights[n], grad_w[n], given["m_" + n], given["v_" + n])
    return (loss, grad_x, *[grad_w[n] for n in TWIN_WEIGHTS], *[delta_w[n] for n in TWIN_WEIGHTS],
            *[new_m[n] for n in TWIN_WEIGHTS], *[new_v[n] for n in TWIN_WEIGHTS])
```

```python
import functools

import jax
import jax.numpy as jnp
from jax import lax
from jax.experimental import pallas as pl
from jax.experimental.pallas import tpu as pltpu

f32 = jnp.float32
bf16 = jnp.bfloat16
HI = lax.Precision.HIGHEST

D_MODEL = 1024
HEADS = 4
HEAD_W = 128
MIX_W = HEADS * HEAD_W
ML_DQK = 64
PROJ_W = 3592
U_HG = 4 * MIX_W
U_ML = 3 * MIX_W + 128
D_FF = 2816
PLE = 256
CHUNK = 64
SUB = 16
EXP_CAP = 80.0
CONV_K = 4
HALO = 8
ALPHA = float(2.0 ** 0.25)
LN_EPS = 1e-5
RMS_EPS = 1e-6
NEG = -1e30
LR, B1, B2, EPS_ADAM, WD, STEP = 0.001, 0.9, 0.999, 1e-08, 0.01, 10
VMEM_LIMIT = 56 * 1024 * 1024


def _cparams(n_axes, arbitrary=True):
    sem = ("arbitrary",) * n_axes if arbitrary else ("parallel",) * n_axes
    return pltpu.CompilerParams(dimension_semantics=sem, vmem_limit_bytes=VMEM_LIMIT)


ACT = bf16


def _mx(a):
    return a.astype(ACT)


def _bdot(a, b):
    return jnp.dot(_mx(a), _mx(b), preferred_element_type=f32)


def _bdot_nt(a, b):
    return lax.dot_general(_mx(a), _mx(b), (((1,), (1,)), ((), ())), preferred_element_type=f32)


def _bdot_tn(a, b):
    return lax.dot_general(_mx(a), _mx(b), (((0,), (0,)), ((), ())), preferred_element_type=f32)


def _hdot(a, b):
    return jnp.dot(a, b, precision=HI, preferred_element_type=f32)


def _hdot_nt(a, b):
    return lax.dot_general(a, b, (((1,), (1,)), ((), ())), precision=HI, preferred_element_type=f32)


def _hdot_tn(a, b):
    return lax.dot_general(a, b, (((0,), (0,)), ((), ())), precision=HI, preferred_element_type=f32)


def _sigmoid(x):
    return 1.0 / (1.0 + jnp.exp(-x))


def _log_sigmoid(x):
    return jnp.minimum(x, 0.0) - jnp.log(1.0 + jnp.exp(-jnp.abs(x)))


def _tri(n, upper=False):
    r = lax.broadcasted_iota(jnp.int32, (n, n), 0)
    c = lax.broadcasted_iota(jnp.int32, (n, n), 1)
    return (c >= r) if upper else (c <= r)


def _rows(tm, n, col=0):
    return pl.BlockSpec((tm, n), lambda i, _c=col: (i, _c))


def _rows_rev(tm, n, nb, col=0):
    return pl.BlockSpec((tm, n), lambda i, _c=col, _nb=nb: (_nb - 1 - i, _c))


def _const(shape):
    return pl.BlockSpec(shape, lambda i, _n=len(shape): (0,) * _n)


def _tile(t, want):
    return want if t % want == 0 else t


def _inproj(x, w_hg, w_ml, b_hg, b_ml):
    t = x.shape[0]
    tm = _tile(t, 256)

    def body(x_ref, whg_ref, wml_ref, bhg_ref, bml_ref, uhg_ref, uml_ref):
        xv = x_ref[...]
        uhg_ref[...] = _bdot(xv, whg_ref[...]) + bhg_ref[...]
        uml_ref[...] = _bdot(xv, wml_ref[...]) + bml_ref[...]

    return pl.pallas_call(
        body, name="inproj", grid=(t // tm,),
        in_specs=[_rows(tm, D_MODEL), _const((D_MODEL, U_HG)), _const((D_MODEL, U_ML)), _const((1, U_HG)), _const((1, U_ML))],
        out_specs=[_rows(tm, U_HG), _rows(tm, U_ML)],
        out_shape=[jax.ShapeDtypeStruct((t, U_HG), f32), jax.ShapeDtypeStruct((t, U_ML), f32)],
        compiler_params=_cparams(1, arbitrary=False),
    )(x, w_hg, w_ml, b_hg, b_ml)


def _hg_gates(hq, hf, lb, tri):
    s = _sigmoid(hf)
    om = 1.0 - lb
    f = lb + om * s
    g = jnp.log(f)
    k = om * (1.0 - s)
    sq = _sigmoid(hq)
    q = hq * sq
    b = _hdot(tri, g)
    return q, sq, s, f, k, b


def _hg_scores(q, k, b, tril_mask):
    qts, kts, eqs, eks, rows = [], [], [], [], []
    for i in range(CHUNK // SUB):
        lo = i * SUB
        ref = jnp.zeros_like(b[0:1]) if i == 0 else b[lo - 1:lo]
        eq = jnp.exp(b[lo:lo + SUB] - ref)
        ek = jnp.exp(jnp.minimum(ref - b, EXP_CAP))
        qt = q[lo:lo + SUB] * eq
        kt = k * ek
        rows.append(_hdot_nt(qt, kt))
        qts.append(qt); kts.append(kt); eqs.append(eq); eks.append(ek)
    a = jnp.where(tril_mask, jnp.concatenate(rows, axis=0), 0.0)
    return a, qts, kts, eqs, eks


def _head_rms(o, gn):
    rstd = lax.rsqrt(jnp.mean(o * o, axis=-1, keepdims=True) + RMS_EPS)
    oh = o * rstd
    return oh, rstd, oh * gn


def _lower_bound(logit_ref):
    lg = logit_ref[...]
    return _sigmoid(lg[0:1] - lg[1:2])


def _hgrn2_fwd(u_hg, logits, gn):
    t = u_hg.shape[0]
    tb = _tile(t, 256)
    nc_blk = tb // CHUNK

    def body(u_ref, lg_ref, gn_ref, og_ref, sst_ref, st_ref):
        @pl.when(pl.program_id(0) == 0)
        def _():
            st_ref[...] = jnp.zeros_like(st_ref)

        lb_all = _lower_bound(lg_ref)
        tril_mask = _tri(CHUNK)
        tri = tril_mask.astype(f32)

        def chunk(c, carry):
            r0 = pl.multiple_of(c * CHUNK, CHUNK)
            rows = pl.ds(r0, CHUNK)
            for h in range(HEADS):
                cs = slice(h * HEAD_W, (h + 1) * HEAD_W)
                hq = u_ref[rows, h * HEAD_W:(h + 1) * HEAD_W]
                hf = u_ref[rows, MIX_W + h * HEAD_W:MIX_W + (h + 1) * HEAD_W]
                hv = u_ref[rows, 2 * MIX_W + h * HEAD_W:2 * MIX_W + (h + 1) * HEAD_W]
                hgate = u_ref[rows, 3 * MIX_W + h * HEAD_W:3 * MIX_W + (h + 1) * HEAD_W]
                q, _, _, _, k, b = _hg_gates(hq, hf, lb_all[:, cs], tri)
                a, _, _, _, _ = _hg_scores(q, k, b, tril_mask)
                st = st_ref[h]
                sst_ref[c, h] = st
                bl = b[CHUNK - 1:CHUNK]
                o = _bdot(a, hv) + _bdot_nt(q * jnp.exp(b), st)
                st_ref[h] = st * jnp.exp(bl) + _bdot_tn(hv, k * jnp.exp(bl - b))
                _, _, y = _head_rms(o, gn_ref[:, cs])
                og_ref[rows, cs] = (y * (hgate * _sigmoid(hgate))).astype(ACT)
            return carry

        lax.fori_loop(0, nc_blk, chunk, 0)

    return pl.pallas_call(
        body, name="hgrn2_fwd", grid=(t // tb,),
        in_specs=[_rows(tb, U_HG), _const((2, MIX_W)), _const((1, MIX_W))],
        out_specs=[_rows(tb, MIX_W), pl.BlockSpec((nc_blk, HEADS, HEAD_W, HEAD_W), lambda i: (i, 0, 0, 0))],
        out_shape=[jax.ShapeDtypeStruct((t, MIX_W), ACT), jax.ShapeDtypeStruct((t // CHUNK, HEADS, HEAD_W, HEAD_W), f32)],
        scratch_shapes=[pltpu.VMEM((HEADS, HEAD_W, HEAD_W), f32)],
        compiler_params=_cparams(1),
    )(u_hg, logits, gn)


def _hgrn2_bwd(u_hg, logits, gn, sst, dog):
    t = u_hg.shape[0]
    tb = _tile(t, 256)
    nb = t // tb
    nc_blk = tb // CHUNK

    def body(u_ref, lg_ref, gn_ref, sst_ref, dog_ref, du_ref, dlg_ref, dgn_ref, dst_ref):
        @pl.when(pl.program_id(0) == 0)
        def _():
            dst_ref[...] = jnp.zeros_like(dst_ref)
            dlg_ref[...] = jnp.zeros_like(dlg_ref)
            dgn_ref[...] = jnp.zeros_like(dgn_ref)

        lb_all = _lower_bound(lg_ref)
        tril_mask = _tri(CHUNK)
        tri = tril_mask.astype(f32)
        triu = _tri(CHUNK, upper=True).astype(f32)

        def chunk(j, carry):
            c = nc_blk - 1 - j
            r0 = pl.multiple_of(c * CHUNK, CHUNK)
            rows = pl.ds(r0, CHUNK)
            for h in range(HEADS):
                cs = slice(h * HEAD_W, (h + 1) * HEAD_W)
                hq = u_ref[rows, h * HEAD_W:(h + 1) * HEAD_W]
                hf = u_ref[rows, MIX_W + h * HEAD_W:MIX_W + (h + 1) * HEAD_W]
                hv = u_ref[rows, 2 * MIX_W + h * HEAD_W:2 * MIX_W + (h + 1) * HEAD_W]
                hgate = u_ref[rows, 3 * MIX_W + h * HEAD_W:3 * MIX_W + (h + 1) * HEAD_W]
                lb = lb_all[:, cs]
                gnh = gn_ref[:, cs]
                q, sq, s, f, k, b = _hg_gates(hq, hf, lb, tri)
                a, qts, kts, eqs, eks = _hg_scores(q, k, b, tril_mask)
                st = sst_ref[c, h]
                dst = dst_ref[h]
                bl = b[CHUNK - 1:CHUNK]
                eb = jnp.exp(b)
                qh = q * eb
                ekl = jnp.exp(bl - b)
                kh = k * ekl
                o = _bdot(a, hv) + _bdot_nt(qh, st)
                oh, rstd, y = _head_rms(o, gnh)
                sg = _sigmoid(hgate)
                dogh = dog_ref[rows, cs]
                dy = dogh * (hgate * sg)
                du_ref[rows, 3 * MIX_W + h * HEAD_W:3 * MIX_W + (h + 1) * HEAD_W] = dogh * y * (sg * (1.0 + hgate * (1.0 - sg)))
                dgn_ref[:, cs] += jnp.sum(dy * oh, axis=0, keepdims=True)
                doh = dy * gnh
                do = rstd * (doh - oh * jnp.mean(doh * oh, axis=-1, keepdims=True))
                da = jnp.where(tril_mask, _bdot_nt(do, hv), 0.0)
                dv = _bdot_tn(a, do) + _bdot_nt(kh, dst)
                dq = _bdot(do, st) * eb
                dk = _bdot(hv, dst) * ekl
                d_last = jnp.sum(k * dk, axis=0, keepdims=True) + jnp.exp(bl) * jnp.sum(dst * st, axis=0, keepdims=True)
                dqs = []
                for i in range(CHUNK // SUB):
                    da_i = da[i * SUB:(i + 1) * SUB]
                    dqs.append(_hdot(da_i, kts[i]) * eqs[i])
                    dk = dk + _hdot_tn(da_i, qts[i]) * eks[i]
                dq = dq + jnp.concatenate(dqs, axis=0)
                dst_ref[h] = dst * jnp.exp(bl) + _bdot_tn(do, qh)
                dg = _hdot(triu, q * dq - k * dk) + d_last
                df = dg / f
                dfk = df - dk
                du_ref[rows, h * HEAD_W:(h + 1) * HEAD_W] = dq * (sq * (1.0 + hq * (1.0 - sq)))
                du_ref[rows, MIX_W + h * HEAD_W:MIX_W + (h + 1) * HEAD_W] = (1.0 - lb) * dfk * s * (1.0 - s)
                du_ref[rows, 2 * MIX_W + h * HEAD_W:2 * MIX_W + (h + 1) * HEAD_W] = dv
                dlb = jnp.sum((1.0 - s) * dfk, axis=0, keepdims=True) * (lb * (1.0 - lb))
                dlg_ref[0:1, cs] += dlb
                dlg_ref[1:2, cs] -= dlb
            return carry

        lax.fori_loop(0, nc_blk, chunk, 0)

    return pl.pallas_call(
        body, name="hgrn2_bwd", grid=(nb,),
        in_specs=[_rows_rev(tb, U_HG, nb), _const((2, MIX_W)), _const((1, MIX_W)),
                  pl.BlockSpec((nc_blk, HEADS, HEAD_W, HEAD_W), lambda i: (nb - 1 - i, 0, 0, 0)), _rows_rev(tb, MIX_W, nb)],
        out_specs=[_rows_rev(tb, U_HG, nb), _const((2, MIX_W)), _const((1, MIX_W))],
        out_shape=[jax.ShapeDtypeStruct((t, U_HG), f32), jax.ShapeDtypeStruct((2, MIX_W), f32), jax.ShapeDtypeStruct((1, MIX_W), f32)],
        scratch_shapes=[pltpu.VMEM((HEADS, HEAD_W, HEAD_W), f32)],
        compiler_params=_cparams(1),
    )(u_hg, logits, gn, sst, dog)


def _conv_fwd(u_ml, w, b):
    t = u_ml.shape[0]
    tm = _tile(t, 512)

    def body(x_ref, w_ref, b_ref, pre_ref, act_ref, xbuf):
        @pl.when(pl.program_id(0) == 0)
        def _():
            xbuf[...] = jnp.zeros_like(xbuf)

        xbuf[0:HALO, :] = xbuf[tm:tm + HALO, :]
        xbuf[HALO:HALO + tm, :] = x_ref[...]
        pre = b_ref[...] + jnp.zeros((tm, MIX_W), f32)
        for kk in range(CONV_K):
            off = HALO - (CONV_K - 1) + kk
            pre = pre + w_ref[kk:kk + 1, :] * xbuf[off:off + tm, :]
        pre_ref[...] = pre
        act_ref[...] = pre * _sigmoid(pre)

    return pl.pallas_call(
        body, name="conv_fwd", grid=(t // tm,),
        in_specs=[_rows(tm, MIX_W), _const((CONV_K, MIX_W)), _const((1, MIX_W))],
        out_specs=[_rows(tm, MIX_W), _rows(tm, MIX_W)],
        out_shape=[jax.ShapeDtypeStruct((t, MIX_W), f32)] * 2,
        scratch_shapes=[pltpu.VMEM((tm + HALO, MIX_W), f32)],
        compiler_params=_cparams(1),
    )(u_ml, w, b)


def _conv_bwd(u_ml, w, pre, dact):
    t = u_ml.shape[0]
    tm = _tile(t, 512)
    nb = t // tm
    hb = tm // HALO

    def body(x_ref, halo_ref, w_ref, pre_ref, dact_ref, dx_ref, dw_ref, db_ref, dbuf, xbuf):
        i = pl.program_id(0)

        @pl.when(i == 0)
        def _():
            dbuf[...] = jnp.zeros_like(dbuf)
            dw_ref[...] = jnp.zeros_like(dw_ref)
            db_ref[...] = jnp.zeros_like(db_ref)

        p = pre_ref[...]
        sg = _sigmoid(p)
        dpre = dact_ref[...] * (sg * (1.0 + p * (1.0 - sg)))
        dbuf[tm:tm + HALO, :] = dbuf[0:HALO, :]
        dbuf[0:tm, :] = dpre
        has_prev = (i < nb - 1).astype(f32)
        xbuf[0:HALO, :] = halo_ref[...] * has_prev
        xbuf[HALO:HALO + tm, :] = x_ref[...]
        dx = jnp.zeros((tm, MIX_W), f32)
        for kk in range(CONV_K):
            back = CONV_K - 1 - kk
            dx = dx + w_ref[kk:kk + 1, :] * dbuf[back:back + tm, :]
            off = HALO - (CONV_K - 1) + kk
            dw_ref[kk:kk + 1, :] += jnp.sum(dpre * xbuf[off:off + tm, :], axis=0, keepdims=True)
        dx_ref[...] = dx
        db_ref[...] += jnp.sum(dpre, axis=0, keepdims=True)

    return pl.pallas_call(
        body, name="conv_bwd", grid=(nb,),
        in_specs=[_rows_rev(tm, MIX_W, nb),
                  pl.BlockSpec((HALO, MIX_W), lambda i: (jnp.maximum((nb - 1 - i) * hb - 1, 0), 0)),
                  _const((CONV_K, MIX_W)), _rows_rev(tm, MIX_W, nb), _rows_rev(tm, MIX_W, nb)],
        out_specs=[_rows_rev(tm, MIX_W, nb), _const((CONV_K, MIX_W)), _const((1, MIX_W))],
        out_shape=[jax.ShapeDtypeStruct((t, MIX_W), f32), jax.ShapeDtypeStruct((CONV_K, MIX_W), f32), jax.ShapeDtypeStruct((1, MIX_W), f32)],
        scratch_shapes=[pltpu.VMEM((tm + HALO, MIX_W), f32), pltpu.VMEM((tm + HALO, MIX_W), f32)],
        compiler_params=_cparams(1),
    )(u_ml, u_ml, w, pre, dact)


def _lane_pick(x, lane):
    idx = lax.broadcasted_iota(jnp.int32, x.shape, 1)
    return jnp.sum(jnp.where(idx == lane, x, 0.0), axis=-1, keepdims=True)


def _ml_gate_forms(gates, tri):
    lf = _log_sigmoid(gates)
    gc = _hdot(tri, lf)
    lane = lax.broadcasted_iota(jnp.int32, gates.shape, 1)
    mixed = jnp.where(lane < HEADS, gates, gc)
    sel = (lax.broadcasted_iota(jnp.int32, (8, 128), 0) == lax.broadcasted_iota(jnp.int32, (8, 128), 1)).astype(f32)
    rowsf = _hdot_nt(sel, mixed)
    return gc, rowsf


def _ml_chunk(q, k, v, gates, gc, rowsf, h, c_st, n_st, m_st, tril_mask):
    g_col = _lane_pick(gc, HEADS + h)
    ig_col = _lane_pick(gates, h)
    ig_row = rowsf[h:h + 1, :]
    g_row = rowsf[HEADS + h:HEADS + h + 1, :]
    dmat = jnp.where(tril_mask, g_col - g_row + ig_row, NEG)
    m_inter = g_col + m_st
    m_t = jnp.maximum(m_inter, jnp.max(dmat, axis=-1, keepdims=True))
    wi = jnp.exp(dmat - m_t)
    wo = jnp.exp(m_inter - m_t)
    qk = _bdot_nt(q, k) * wi
    num = _bdot(qk, v) + wo * _bdot(q, c_st)
    den = jnp.sum(qk, axis=-1, keepdims=True) + wo * jnp.sum(q * n_st, axis=-1, keepdims=True)
    floor = jnp.exp(-m_t)
    z = jnp.maximum(jnp.abs(den), floor)
    g_last = g_col[CHUNK - 1:CHUNK]
    a_col = g_last - g_col + ig_col
    m_new = jnp.maximum(g_last + m_st, jnp.max(a_col, axis=0, keepdims=True))
    ws = jnp.exp(a_col - m_new)
    w_old = jnp.exp(g_last + m_st - m_new)
    return dict(wi=wi, wo=wo, qk=qk, num=num, den=den, z=z, floor=floor, ws=ws, w_old=w_old, m_new=m_new)


def _mlstm_fwd(qkc, u_ml, gn):
    t = qkc.shape[0]
    tb = _tile(t, 256)
    nc_blk = tb // CHUNK

    def body(qk_ref, v_ref, mo_ref, gt_ref, gn_ref, og_ref, cst_ref, nst_ref, mst_ref, c_sc, n_sc, m_sc):
        @pl.when(pl.program_id(0) == 0)
        def _():
            c_sc[...] = jnp.zeros_like(c_sc)
            n_sc[...] = jnp.zeros_like(n_sc)
            m_sc[...] = jnp.zeros_like(m_sc)

        tril_mask = _tri(CHUNK)
        tri = tril_mask.astype(f32)

        def chunk(c, carry):
            r0 = pl.multiple_of(c * CHUNK, CHUNK)
            rows = pl.ds(r0, CHUNK)
            gates = gt_ref[rows, :]
            gc, rowsf = _ml_gate_forms(gates, tri)
            for h in range(HEADS):
                cs = slice(h * HEAD_W, (h + 1) * HEAD_W)
                q = qk_ref[rows, h * ML_DQK:(h + 1) * ML_DQK] * (ML_DQK ** -0.5)
                k = qk_ref[rows, HEADS * ML_DQK + h * ML_DQK:HEADS * ML_DQK + (h + 1) * ML_DQK]
                v = v_ref[rows, h * HEAD_W:(h + 1) * HEAD_W]
                mo = mo_ref[rows, h * HEAD_W:(h + 1) * HEAD_W]
                c_st = c_sc[h]
                n_st = n_sc[h]
                m_st = m_sc[h][:, 0:1]
                cst_ref[c, h] = c_st
                nst_ref[c, h] = n_st
                mst_ref[c, h] = m_sc[h]
                r = _ml_chunk(q, k, v, gates, gc, rowsf, h, c_st, n_st, m_st, tril_mask)
                hh = r["num"] / r["z"]
                ksc = k * r["ws"]
                c_sc[h] = r["w_old"] * c_st + _bdot_tn(ksc, v)
                n_sc[h] = r["w_old"] * n_st + jnp.sum(ksc, axis=0, keepdims=True)
                m_sc[h] = r["m_new"] + jnp.zeros((1, 128), f32)
                _, _, y = _head_rms(hh, gn_ref[:, cs])
                og_ref[rows, cs] = (y * _sigmoid(mo)).astype(ACT)
            return carry

        lax.fori_loop(0, nc_blk, chunk, 0)

    nchunks = t // CHUNK
    return pl.pallas_call(
        body, name="mlstm_fwd", grid=(t // tb,),
        in_specs=[_rows(tb, MIX_W), _rows(tb, MIX_W, 1), _rows(tb, MIX_W, 2), _rows(tb, 128, 12), _const((1, MIX_W))],
        out_specs=[_rows(tb, MIX_W),
                   pl.BlockSpec((nc_blk, HEADS, ML_DQK, HEAD_W), lambda i: (i, 0, 0, 0)),
                   pl.BlockSpec((nc_blk, HEADS, 1, ML_DQK), lambda i: (i, 0, 0, 0)),
                   pl.BlockSpec((nc_blk, HEADS, 1, 128), lambda i: (i, 0, 0, 0))],
        out_shape=[jax.ShapeDtypeStruct((t, MIX_W), ACT),
                   jax.ShapeDtypeStruct((nchunks, HEADS, ML_DQK, HEAD_W), f32),
                   jax.ShapeDtypeStruct((nchunks, HEADS, 1, ML_DQK), f32),
                   jax.ShapeDtypeStruct((nchunks, HEADS, 1, 128), f32)],
        scratch_shapes=[pltpu.VMEM((HEADS, ML_DQK, HEAD_W), f32), pltpu.VMEM((HEADS, 1, ML_DQK), f32), pltpu.VMEM((HEADS, 1, 128), f32)],
        compiler_params=_cparams(1),
    )(qkc, u_ml, u_ml, u_ml, gn)


def _mlstm_bwd(qkc, u_ml, gn, cst, nst, mst, dog):
    t = qkc.shape[0]
    tb = _tile(t, 256)
    nb = t // tb
    nc_blk = tb // CHUNK

    def body(qk_ref, v_ref, mo_ref, gt_ref, gn_ref, cst_ref, nst_ref, mst_ref, dog_ref,
             dqk_ref, dv_ref, dmo_ref, dgt_ref, dgn_ref, dc_sc, dn_sc):
        @pl.when(pl.program_id(0) == 0)
        def _():
            dc_sc[...] = jnp.zeros_like(dc_sc)
            dn_sc[...] = jnp.zeros_like(dn_sc)
            dgn_ref[...] = jnp.zeros_like(dgn_ref)

        tril_mask = _tri(CHUNK)
        tri = tril_mask.astype(f32)
        triu = _tri(CHUNK, upper=True).astype(f32)
        lane = lax.broadcasted_iota(jnp.int32, (CHUNK, 128), 1)

        def chunk(j, carry):
            c = nc_blk - 1 - j
            r0 = pl.multiple_of(c * CHUNK, CHUNK)
            rows = pl.ds(r0, CHUNK)
            gates = gt_ref[rows, :]
            gc, rowsf = _ml_gate_forms(gates, tri)
            dg_mat = jnp.zeros((CHUNK, 128), f32)
            dig_mat = jnp.zeros((CHUNK, 128), f32)
            dlast_row = jnp.zeros((1, 128), f32)
            for h in range(HEADS):
                cs = slice(h * HEAD_W, (h + 1) * HEAD_W)
                q = qk_ref[rows, h * ML_DQK:(h + 1) * ML_DQK] * (ML_DQK ** -0.5)
                k = qk_ref[rows, HEADS * ML_DQK + h * ML_DQK:HEADS * ML_DQK + (h + 1) * ML_DQK]
                v = v_ref[rows, h * HEAD_W:(h + 1) * HEAD_W]
                mo = mo_ref[rows, h * HEAD_W:(h + 1) * HEAD_W]
                gnh = gn_ref[:, cs]
                c_st = cst_ref[c, h]
                n_st = nst_ref[c, h]
                m_st = mst_ref[c, h][:, 0:1]
                dc = dc_sc[h]
                dn = dn_sc[h]
                r = _ml_chunk(q, k, v, gates, gc, rowsf, h, c_st, n_st, m_st, tril_mask)
                z = r["z"]
                hh = r["num"] / z
                oh, rstd, y = _head_rms(hh, gnh)
                sg = _sigmoid(mo)
                dogh = dog_ref[rows, cs]
                dy = dogh * sg
                dmo_ref[rows, cs] = dogh * y * (sg * (1.0 - sg))
                dgn_ref[:, cs] += jnp.sum(dy * oh, axis=0, keepdims=True)
                doh = dy * gnh
                dh = rstd * (doh - oh * jnp.mean(doh * oh, axis=-1, keepdims=True))
                dnum = dh / z
                dz = -jnp.sum(dh * hh, axis=-1, keepdims=True) / z
                den = r["den"]
                dden = jnp.where(jnp.abs(den) > r["floor"], dz * jnp.sign(den), 0.0)
                dsw = (_bdot_nt(dnum, v) + dden) * r["wi"]
                wo = r["wo"]
                ws = r["ws"]
                dq = _bdot(dsw, k) + wo * (_bdot_nt(dnum, c_st) + dden * n_st)
                dk_state = ws * (_bdot_nt(v, dc) + dn)
                dk = _bdot_tn(dsw, q) + dk_state
                dv_ref[rows, cs] = _bdot_tn(r["qk"], dnum) + ws * _bdot(k, dc)
                woq = wo * q
                w_old = r["w_old"]
                dc_sc[h] = w_old * dc + _bdot_tn(woq, dnum)
                dn_sc[h] = w_old * dn + jnp.sum(woq * dden, axis=0, keepdims=True)
                d_last = (jnp.sum(jnp.sum(k * dk_state, axis=-1, keepdims=True), axis=0, keepdims=True)
                          + w_old * (jnp.sum(jnp.sum(dc * c_st, axis=-1, keepdims=True), axis=0, keepdims=True)
                                     + jnp.sum(dn * n_st, axis=-1, keepdims=True)))
                kdk = jnp.sum(k * dk, axis=-1, keepdims=True)
                qdq = jnp.sum(q * dq, axis=-1, keepdims=True)
                dg_mat = dg_mat + jnp.where(lane == HEADS + h, qdq - kdk, 0.0)
                dlast_row = dlast_row + jnp.where(lane[0:1] == HEADS + h, d_last, 0.0)
                dig_mat = dig_mat + jnp.where(lane == h, kdk, 0.0)
                dqk_ref[rows, h * ML_DQK:(h + 1) * ML_DQK] = dq * (ML_DQK ** -0.5)
                dqk_ref[rows, HEADS * ML_DQK + h * ML_DQK:HEADS * ML_DQK + (h + 1) * ML_DQK] = dk
            dlf = _hdot(triu, dg_mat) + dlast_row
            dgt_ref[rows, :] = dig_mat + dlf * _sigmoid(-gates)
            return carry

        lax.fori_loop(0, nc_blk, chunk, 0)

    st4 = lambda a, b: pl.BlockSpec((nc_blk, HEADS, a, b), lambda i: (nb - 1 - i, 0, 0, 0))
    return pl.pallas_call(
        body, name="mlstm_bwd", grid=(nb,),
        in_specs=[_rows_rev(tb, MIX_W, nb), _rows_rev(tb, MIX_W, nb, 1), _rows_rev(tb, MIX_W, nb, 2), _rows_rev(tb, 128, nb, 12),
                  _const((1, MIX_W)), st4(ML_DQK, HEAD_W), st4(1, ML_DQK), st4(1, 128), _rows_rev(tb, MIX_W, nb)],
        out_specs=[_rows_rev(tb, MIX_W, nb), _rows_rev(tb, MIX_W, nb), _rows_rev(tb, MIX_W, nb), _rows_rev(tb, 128, nb), _const((1, MIX_W))],
        out_shape=[jax.ShapeDtypeStruct((t, MIX_W), f32)] * 3 + [jax.ShapeDtypeStruct((t, 128), f32), jax.ShapeDtypeStruct((1, MIX_W), f32)],
        scratch_shapes=[pltpu.VMEM((HEADS, ML_DQK, HEAD_W), f32), pltpu.VMEM((HEADS, 1, ML_DQK), f32)],
        compiler_params=_cparams(1),
    )(qkc, u_ml, u_ml, u_ml, gn, cst, nst, mst, dog)


def _ln_fwd(r, g, b):
    mu = jnp.mean(r, axis=-1, keepdims=True)
    xc = r - mu
    rstd = lax.rsqrt(jnp.mean(xc * xc, axis=-1, keepdims=True) + LN_EPS)
    xh = xc * rstd
    return xh * g + b, xh, rstd


def _ln_bwd(dy, xh, rstd, g):
    dxh = dy * g
    return rstd * (dxh - jnp.mean(dxh, axis=-1, keepdims=True) - xh * jnp.mean(dxh * xh, axis=-1, keepdims=True))


def _outproj_ln1(og_hg, og_ml, x, w_out, g, b):
    t = x.shape[0]
    tm = _tile(t, 256)

    def body(a_ref, b_ref, x_ref, w_ref, g_ref, bb_ref, x1_ref, xh_ref, rs_ref):
        mix = _bdot(a_ref[...], w_ref[0:MIX_W, :]) + _bdot(b_ref[...], w_ref[MIX_W:2 * MIX_W, :])
        y, xh, rstd = _ln_fwd(ALPHA * x_ref[...] + mix, g_ref[...], bb_ref[...])
        x1_ref[...] = y
        xh_ref[...] = xh
        rs_ref[...] = rstd

    return pl.pallas_call(
        body, name="outproj_ln1", grid=(t // tm,),
        in_specs=[_rows(tm, MIX_W), _rows(tm, MIX_W), _rows(tm, D_MODEL), _const((D_MODEL, D_MODEL)), _const((1, D_MODEL)), _const((1, D_MODEL))],
        out_specs=[_rows(tm, D_MODEL), _rows(tm, D_MODEL), _rows(tm, 1)],
        out_shape=[jax.ShapeDtypeStruct((t, D_MODEL), f32), jax.ShapeDtypeStruct((t, D_MODEL), f32), jax.ShapeDtypeStruct((t, 1), f32)],
        compiler_params=_cparams(1, arbitrary=False),
    )(og_hg, og_ml, x, w_out, g, b)


def _ffn_up(x1, wg, wu):
    t = x1.shape[0]
    tm = _tile(t, 256)

    def body(x_ref, wg_ref, wu_ref, hg_ref, up_ref, a_ref):
        xv = x_ref[...]
        hg = _bdot(xv, wg_ref[...])
        up = _bdot(xv, wu_ref[...])
        hg_ref[...] = hg
        up_ref[...] = up
        a_ref[...] = (hg * _sigmoid(hg) * up).astype(ACT)

    return pl.pallas_call(
        body, name="ffn_up", grid=(t // tm,),
        in_specs=[_rows(tm, D_MODEL), _const((D_MODEL, D_FF)), _const((D_MODEL, D_FF))],
        out_specs=[_rows(tm, D_FF), _rows(tm, D_FF), _rows(tm, D_FF)],
        out_shape=[jax.ShapeDtypeStruct((t, D_FF), f32), jax.ShapeDtypeStruct((t, D_FF), f32), jax.ShapeDtypeStruct((t, D_FF), ACT)],
        compiler_params=_cparams(1, arbitrary=False),
    )(x1, wg, wu)


def _ffn_down_ln2(a, x1, wd, g, b):
    t = x1.shape[0]
    tm = _tile(t, 256)

    def body(a_ref, x_ref, w_ref, g_ref, bb_ref, x2_ref, xh_ref, rs_ref):
        ffn = _bdot(a_ref[...], w_ref[...])
        y, xh, rstd = _ln_fwd(ALPHA * x_ref[...] + ffn, g_ref[...], bb_ref[...])
        x2_ref[...] = y
        xh_ref[...] = xh
        rs_ref[...] = rstd

    return pl.pallas_call(
        body, name="ffn_down_ln2", grid=(t // tm,),
        in_specs=[_rows(tm, D_FF), _rows(tm, D_MODEL), _const((D_FF, D_MODEL)), _const((1, D_MODEL)), _const((1, D_MODEL))],
        out_specs=[_rows(tm, D_MODEL), _rows(tm, D_MODEL), _rows(tm, 1)],
        out_shape=[jax.ShapeDtypeStruct((t, D_MODEL), f32), jax.ShapeDtypeStruct((t, D_MODEL), f32), jax.ShapeDtypeStruct((t, 1), f32)],
        compiler_params=_cparams(1, arbitrary=False),
    )(a, x1, wd, g, b)


def _head_loss_bwd(x2, xh2, rs2, p, tgt, w_pg, b_pg, w_pp, g2):
    t = x2.shape[0]
    tm = _tile(t, 256)

    def body(x_ref, xh_ref, rs_ref, p_ref, t_ref, wg_ref, bg_ref, wp_ref, g_ref,
             dr_ref, de_ref, dz_ref, loss_ref, dbg_ref, dg2_ref, db2_ref):
        @pl.when(pl.program_id(0) == 0)
        def _():
            loss_ref[...] = jnp.zeros_like(loss_ref)
            dbg_ref[...] = jnp.zeros_like(dbg_ref)
            dg2_ref[...] = jnp.zeros_like(dg2_ref)
            db2_ref[...] = jnp.zeros_like(db2_ref)

        x2v = x_ref[...]
        z = _bdot(x2v, wg_ref[...]) + bg_ref[...]
        e = _bdot(p_ref[...], wp_ref[...])
        sg = _sigmoid(z)
        diff = x2v + sg * e - t_ref[...]
        loss_ref[...] += 0.5 * jnp.sum(jnp.mean(diff * diff, axis=-1, keepdims=True), axis=0, keepdims=True)
        dy = diff * (1.0 / D_MODEL)
        de_ref[...] = (dy * sg).astype(ACT)
        dz = dy * e * (sg * (1.0 - sg))
        dz_ref[...] = dz.astype(ACT)
        dbg_ref[...] += jnp.sum(dz, axis=0, keepdims=True)
        dx2 = dy + _bdot_nt(dz, wg_ref[...])
        xh = xh_ref[...]
        dg2_ref[...] += jnp.sum(dx2 * xh, axis=0, keepdims=True)
        db2_ref[...] += jnp.sum(dx2, axis=0, keepdims=True)
        dr_ref[...] = _ln_bwd(dx2, xh, rs_ref[...], g_ref[...])

    row = jax.ShapeDtypeStruct((1, D_MODEL), f32)
    return pl.pallas_call(
        body, name="head_loss_bwd", grid=(t // tm,),
        in_specs=[_rows(tm, D_MODEL), _rows(tm, D_MODEL), _rows(tm, 1), _rows(tm, PLE), _rows(tm, D_MODEL),
                  _const((D_MODEL, D_MODEL)), _const((1, D_MODEL)), _const((PLE, D_MODEL)), _const((1, D_MODEL))],
        out_specs=[_rows(tm, D_MODEL), _rows(tm, D_MODEL), _rows(tm, D_MODEL), _const((1, 1)), _const((1, D_MODEL)), _const((1, D_MODEL)), _const((1, D_MODEL))],
        out_shape=[jax.ShapeDtypeStruct((t, D_MODEL), f32), jax.ShapeDtypeStruct((t, D_MODEL), ACT), jax.ShapeDtypeStruct((t, D_MODEL), ACT),
                   jax.ShapeDtypeStruct((1, 1), f32), row, row, row],
        compiler_params=_cparams(1),
    )(x2, xh2, rs2, p, tgt, w_pg, b_pg, w_pp, g2)


def _ffn_bwd(dr2, hg, up, xh1, rs1, wd, wg, wu, g1):
    t = dr2.shape[0]
    tm = _tile(t, 128)

    def body(dr_ref, hg_ref, up_ref, xh_ref, rs_ref, wd_ref, wg_ref, wu_ref, g_ref,
             dr1_ref, dhg_ref, dup_ref, dg1_ref, db1_ref):
        @pl.when(pl.program_id(0) == 0)
        def _():
            dg1_ref[...] = jnp.zeros_like(dg1_ref)
            db1_ref[...] = jnp.zeros_like(db1_ref)

        dr2v = dr_ref[...]
        da = _bdot_nt(dr2v, wd_ref[...])
        hgv = hg_ref[...]
        sg = _sigmoid(hgv)
        dhg = da * up_ref[...] * (sg * (1.0 + hgv * (1.0 - sg)))
        dup = da * (hgv * sg)
        dhg_ref[...] = dhg.astype(ACT)
        dup_ref[...] = dup.astype(ACT)
        dx1 = ALPHA * dr2v + _bdot_nt(dhg, wg_ref[...]) + _bdot_nt(dup, wu_ref[...])
        xh = xh_ref[...]
        dg1_ref[...] += jnp.sum(dx1 * xh, axis=0, keepdims=True)
        db1_ref[...] += jnp.sum(dx1, axis=0, keepdims=True)
        dr1_ref[...] = _ln_bwd(dx1, xh, rs_ref[...], g_ref[...])

    row = jax.ShapeDtypeStruct((1, D_MODEL), f32)
    return pl.pallas_call(
        body, name="ffn_bwd", grid=(t // tm,),
        in_specs=[_rows(tm, D_MODEL), _rows(tm, D_FF), _rows(tm, D_FF), _rows(tm, D_MODEL), _rows(tm, 1),
                  _const((D_FF, D_MODEL)), _const((D_MODEL, D_FF)), _const((D_MODEL, D_FF)), _const((1, D_MODEL))],
        out_specs=[_rows(tm, D_MODEL), _rows(tm, D_FF), _rows(tm, D_FF), _const((1, D_MODEL)), _const((1, D_MODEL))],
        out_shape=[jax.ShapeDtypeStruct((t, D_MODEL), f32), jax.ShapeDtypeStruct((t, D_FF), ACT), jax.ShapeDtypeStruct((t, D_FF), ACT), row, row],
        compiler_params=_cparams(1),
    )(dr2, hg, up, xh1, rs1, wd, wg, wu, g1)


def _outproj_bwd(dr1, w_out):
    t = dr1.shape[0]
    tm = _tile(t, 256)

    def body(dr_ref, w_ref, dhg_ref, dml_ref):
        d = _bdot_nt(dr_ref[...], w_ref[...])
        dhg_ref[...] = d[:, 0:MIX_W]
        dml_ref[...] = d[:, MIX_W:2 * MIX_W]

    return pl.pallas_call(
        body, name="outproj_bwd", grid=(t // tm,),
        in_specs=[_rows(tm, D_MODEL), _const((D_MODEL, D_MODEL))],
        out_specs=[_rows(tm, MIX_W), _rows(tm, MIX_W)],
        out_shape=[jax.ShapeDtypeStruct((t, MIX_W), f32)] * 2,
        compiler_params=_cparams(1, arbitrary=False),
    )(dr1, w_out)


def _inproj_bwd(dr1, du_hg, dqk, dmv, dmo, dgt, w_hg, w_ml):
    t = dr1.shape[0]
    tm = _tile(t, 256)

    def body(dr_ref, dhg_ref, dqk_ref, dmv_ref, dmo_ref, dgt_ref, whg_ref, wml_ref, gx_ref, dml_ref):
        dml = jnp.concatenate([dqk_ref[...], dmv_ref[...], dmo_ref[...], dgt_ref[...]], axis=-1).astype(ACT)
        dml_ref[...] = dml
        gx_ref[...] = ALPHA * dr_ref[...] + _bdot_nt(dhg_ref[...], whg_ref[...]) + _bdot_nt(dml, wml_ref[...])

    return pl.pallas_call(
        body, name="inproj_bwd", grid=(t // tm,),
        in_specs=[_rows(tm, D_MODEL), _rows(tm, U_HG), _rows(tm, MIX_W), _rows(tm, MIX_W), _rows(tm, MIX_W), _rows(tm, 128),
                  _const((D_MODEL, U_HG)), _const((D_MODEL, U_ML))],
        out_specs=[_rows(tm, D_MODEL), _rows(tm, U_ML)],
        out_shape=[jax.ShapeDtypeStruct((t, D_MODEL), f32), jax.ShapeDtypeStruct((t, U_ML), ACT)],
        compiler_params=_cparams(1, arbitrary=False),
    )(dr1, du_hg, dqk, dmv, dmo, dgt, w_hg, w_ml)


def _wgrad(a, b, name, tk=None, tn=None):
    t, kdim = a.shape
    n = b.shape[1]
    tk = tk or kdim
    tn = tn or n
    tt = _tile(t, 512)

    def body(a_ref, b_ref, o_ref):
        @pl.when(pl.program_id(2) == 0)
        def _():
            o_ref[...] = jnp.zeros_like(o_ref)

        o_ref[...] += _bdot_tn(a_ref[...], b_ref[...])

    return pl.pallas_call(
        body, name=name, grid=(kdim // tk, n // tn, t // tt),
        in_specs=[pl.BlockSpec((tt, tk), lambda i, j, s: (s, i)), pl.BlockSpec((tt, tn), lambda i, j, s: (s, j))],
        out_specs=pl.BlockSpec((tk, tn), lambda i, j, s: (i, j)),
        out_shape=jax.ShapeDtypeStruct((kdim, n), f32),
        compiler_params=_cparams(3),
    )(a, b)


def _colsum(parts, name):
    t = parts[0].shape[0]
    tt = _tile(t, 512)
    widths = [a.shape[1] for a in parts]

    def body(*refs):
        o_ref = refs[-1]

        @pl.when(pl.program_id(0) == 0)
        def _():
            o_ref[...] = jnp.zeros_like(o_ref)

        off = 0
        for r, w in zip(refs[:-1], widths):
            o_ref[:, off:off + w] += jnp.sum(r[...].astype(f32), axis=0, keepdims=True)
            off += w

    return pl.pallas_call(
        body, name=name, grid=(t // tt,),
        in_specs=[_rows(tt, w) for w in widths],
        out_specs=_const((1, sum(widths))),
        out_shape=jax.ShapeDtypeStruct((1, sum(widths)), f32),
        compiler_params=_cparams(1),
    )(*parts)


def _local_step(x, p, tgt, w_in_b, b_in, logits, conv_w, conv_b, hg_gn, ml_gn, w_out_b, ln1_g, ln1_b,
                wg_b, wu_b, wd_b, ln2_g, ln2_b, w_pp_b, w_pg_b, b_pg):
    pad_w = U_HG + U_ML - PROJ_W
    w_hg = w_in_b[:, :U_HG]
    w_ml = jnp.pad(w_in_b[:, U_HG:], ((0, 0), (0, pad_w)))
    bb_hg = b_in[:, :U_HG]
    bb_ml = jnp.pad(b_in[:, U_HG:], ((0, 0), (0, pad_w)))

    u_hg, u_ml = _inproj(x, w_hg, w_ml, bb_hg, bb_ml)
    og_hg, sst = _hgrn2_fwd(u_hg, logits, hg_gn)
    pre, qkc = _conv_fwd(u_ml, conv_w, conv_b)
    og_ml, cst, nst, mst = _mlstm_fwd(qkc, u_ml, ml_gn)
    x1, xh1, rs1 = _outproj_ln1(og_hg, og_ml, x, w_out_b, ln1_g, ln1_b)
    hgp, up, act = _ffn_up(x1, wg_b, wu_b)
    x2, xh2, rs2 = _ffn_down_ln2(act, x1, wd_b, ln2_g, ln2_b)
    dr2, de, dz, loss, d_bpg, d_ln2g, d_ln2b = _head_loss_bwd(x2, xh2, rs2, p, tgt, w_pg_b, b_pg, w_pp_b, ln2_g)
    dr1, dhg, dup, d_ln1g, d_ln1b = _ffn_bwd(dr2, hgp, up, xh1, rs1, wd_b, wg_b, wu_b, ln1_g)
    dog_hg, dog_ml = _outproj_bwd(dr1, w_out_b)
    du_hg, d_logits, d_hg_gn = _hgrn2_bwd(u_hg, logits, hg_gn, sst, dog_hg)
    dqkc, dmv, dmo, dgt, d_ml_gn = _mlstm_bwd(qkc, u_ml, ml_gn, cst, nst, mst, dog_ml)
    dqk, d_conv_w, d_conv_b = _conv_bwd(u_ml, conv_w, pre, dqkc)
    grad_x, du_ml = _inproj_bwd(dr1, du_hg, dqk, dmv, dmo, dgt, w_hg, w_ml)

    dw_hg = _wgrad(x, du_hg, "wgrad_in_hg", tn=1024)
    dw_ml = _wgrad(x, du_ml, "wgrad_in_ml")
    d_w_in = jnp.concatenate([dw_hg, dw_ml[:, :PROJ_W - U_HG]], axis=1)
    d_b_in = _colsum([du_hg, du_ml], "colsum_du")[:, :PROJ_W]
    d_wo_a = _wgrad(og_hg, dr1, "wgrad_out_hg")
    d_wo_b = _wgrad(og_ml, dr1, "wgrad_out_ml")
    d_w_out = jnp.concatenate([d_wo_a, d_wo_b], axis=0)
    d_wg = _wgrad(x1, dhg, "wgrad_ffn_gate", tn=D_FF // 2)
    d_wu = _wgrad(x1, dup, "wgrad_ffn_up", tn=D_FF // 2)
    d_wd = _wgrad(act, dr2, "wgrad_ffn_down", tk=D_FF // 2)
    d_wpp = _wgrad(p, de, "wgrad_ple_proj")
    d_wpg = _wgrad(x2, dz, "wgrad_ple_gate")

    grads = dict(w_in=d_w_in, b_in=d_b_in, hg_lb_logits=d_logits, ml_conv_w=d_conv_w, ml_conv_b=d_conv_b,
                 hg_norm_g=d_hg_gn, ml_norm_g=d_ml_gn, w_out=d_w_out, ln1_g=d_ln1g, ln1_b=d_ln1b,
                 w_ffn_gate=d_wg, w_ffn_up=d_wu, w_ffn_down=d_wd, ln2_g=d_ln2g, ln2_b=d_ln2b,
                 ple_w_proj=d_wpp, ple_w_gate=d_wpg, ple_b_gate=d_bpg)
    return loss, grad_x, grads


_ANY = pl.BlockSpec(memory_space=pl.ANY)
_MESH = pl.DeviceIdType.MESH


def _my_place():
    return lax.axis_index("x"), lax.axis_index("y"), lax.axis_index("c")


def _other_chips(x, y):
    return [(1 - x, y), (x, 1 - y), (1 - x, 1 - y)]


def _allgather_chips(shards, name):
    n = len(shards)

    def body(*refs):
        ins, outs = refs[:n], refs[n:2 * n]
        send_sems, recv_sems, local_sems = refs[2 * n:]
        x, y, c = _my_place()
        me = 2 * x + y
        chips = _other_chips(x, y)
        local, waits = [], []
        for a in range(n):
            cp = pltpu.make_async_copy(ins[a], outs[a].at[me], local_sems.at[a])
            cp.start()
            local.append(cp)
            for j, (px, py) in enumerate(chips):
                pltpu.make_async_remote_copy(src_ref=ins[a], dst_ref=outs[a].at[me], send_sem=send_sems.at[3 * a + j],
                                             recv_sem=recv_sems.at[3 * a + j], device_id=(px, py, c), device_id_type=_MESH).start()
        for a in range(n):
            for j, (px, py) in enumerate(chips):
                pltpu.make_async_remote_copy(src_ref=ins[a], dst_ref=outs[a].at[2 * px + py], send_sem=send_sems.at[3 * a + j],
                                             recv_sem=recv_sems.at[3 * a + j], device_id=(px, py, c), device_id_type=_MESH).wait()
            local[a].wait()

    return pl.pallas_call(
        body, name=name,
        in_specs=[_ANY] * n, out_specs=[_ANY] * n,
        out_shape=[jax.ShapeDtypeStruct((4,) + s.shape, s.dtype) for s in shards],
        scratch_shapes=[pltpu.SemaphoreType.DMA((3 * n,)), pltpu.SemaphoreType.DMA((3 * n,)), pltpu.SemaphoreType.DMA((n,))],
    )(*shards)


def _scatter_chips(pieces, name):
    n = len(pieces)

    def body(*refs):
        ins, outs = refs[:n], refs[n:2 * n]
        send_sems, recv_sems, local_sems = refs[2 * n:]
        x, y, c = _my_place()
        me = 2 * x + y
        chips = _other_chips(x, y)
        local = []
        for a in range(n):
            cp = pltpu.make_async_copy(ins[a].at[me], outs[a].at[me], local_sems.at[a])
            cp.start()
            local.append(cp)
            for j, (px, py) in enumerate(chips):
                pltpu.make_async_remote_copy(src_ref=ins[a].at[2 * px + py], dst_ref=outs[a].at[me], send_sem=send_sems.at[3 * a + j],
                                             recv_sem=recv_sems.at[3 * a + j], device_id=(px, py, c), device_id_type=_MESH).start()
        for a in range(n):
            for j, (px, py) in enumerate(chips):
                pltpu.make_async_remote_copy(src_ref=ins[a].at[2 * px + py], dst_ref=outs[a].at[2 * px + py], send_sem=send_sems.at[3 * a + j],
                                             recv_sem=recv_sems.at[3 * a + j], device_id=(px, py, c), device_id_type=_MESH).wait()
            local[a].wait()

    return pl.pallas_call(
        body, name=name,
        in_specs=[_ANY] * n, out_specs=[_ANY] * n,
        out_shape=[jax.ShapeDtypeStruct(s.shape, s.dtype) for s in pieces],
        scratch_shapes=[pltpu.SemaphoreType.DMA((3 * n,)), pltpu.SemaphoreType.DMA((3 * n,)), pltpu.SemaphoreType.DMA((n,))],
    )(*pieces)


def _swap_cores(blocks, name):
    n = len(blocks)

    def body(*refs):
        ins, outs = refs[:n], refs[n:2 * n]
        send_sems, recv_sems, local_sems = refs[2 * n:]
        x, y, c = _my_place()
        local = []
        for a in range(n):
            cp = pltpu.make_async_copy(ins[a], outs[a].at[c], local_sems.at[a])
            cp.start()
            local.append(cp)
            pltpu.make_async_remote_copy(src_ref=ins[a], dst_ref=outs[a].at[c], send_sem=send_sems.at[a], recv_sem=recv_sems.at[a],
                                         device_id=(x, y, 1 - c), device_id_type=_MESH).start()
        for a in range(n):
            pltpu.make_async_remote_copy(src_ref=ins[a], dst_ref=outs[a].at[1 - c], send_sem=send_sems.at[a], recv_sem=recv_sems.at[a],
                                         device_id=(x, y, 1 - c), device_id_type=_MESH).wait()
            local[a].wait()

    return pl.pallas_call(
        body, name=name,
        in_specs=[_ANY] * n, out_specs=[_ANY] * n,
        out_shape=[jax.ShapeDtypeStruct((2,) + s.shape, s.dtype) for s in blocks],
        scratch_shapes=[pltpu.SemaphoreType.DMA((n,)), pltpu.SemaphoreType.DMA((n,)), pltpu.SemaphoreType.DMA((n,))],
    )(*blocks)


def _gather_all(block, name):
    def body(in_ref, out_ref, send_sems, recv_sems, local_sem):
        x, y, c = _my_place()
        me = 4 * x + 2 * y + c
        cp = pltpu.make_async_copy(in_ref, out_ref.at[me], local_sem)
        cp.start()
        peers = []
        for dx in range(2):
            for dy in range(2):
                for dc in range(2):
                    if dx or dy or dc:
                        peers.append((1 - x if dx else x, 1 - y if dy else y, 1 - c if dc else c))
        for j, pr in enumerate(peers):
            pltpu.make_async_remote_copy(src_ref=in_ref, dst_ref=out_ref.at[me], send_sem=send_sems.at[j], recv_sem=recv_sems.at[j],
                                         device_id=pr, device_id_type=_MESH).start()
        for j, (px, py, pc) in enumerate(peers):
            pltpu.make_async_remote_copy(src_ref=in_ref, dst_ref=out_ref.at[4 * px + 2 * py + pc], send_sem=send_sems.at[j], recv_sem=recv_sems.at[j],
                                         device_id=(px, py, pc), device_id_type=_MESH).wait()
        cp.wait()

    return pl.pallas_call(
        body, name=name,
        in_specs=[_ANY], out_specs=_ANY,
        out_shape=jax.ShapeDtypeStruct((8,) + block.shape, block.dtype),
        scratch_shapes=[pltpu.SemaphoreType.DMA((7,)), pltpu.SemaphoreType.DMA((7,)), pltpu.SemaphoreType.DMA],
    )(block)


def _row_tile(r, c):
    best = r
    for cand in range(8, r + 1, 8):
        if r % cand == 0 and cand * c * 4 <= (1 << 20):
            best = cand
    return best if best * c * 4 <= (4 << 20) else r


def _sum_slots(parts, name):
    n, r, c = parts.shape
    tr = _row_tile(r, c)

    def body(p_ref, o_ref):
        acc = p_ref[0]
        for s in range(1, n):
            acc = acc + p_ref[s]
        o_ref[...] = acc

    return pl.pallas_call(
        body, name=name, grid=(r // tr,),
        in_specs=[pl.BlockSpec((n, tr, c), lambda i: (0, i, 0))],
        out_specs=pl.BlockSpec((tr, c), lambda i: (i, 0)),
        out_shape=jax.ShapeDtypeStruct((r, c), f32),
        compiler_params=_cparams(1, arbitrary=False),
    )(parts)


def _adamw(parts, w, m, v, name):
    n, r, c = parts.shape
    tr = _row_tile(r, c)

    def body(p_ref, w_ref, m_ref, v_ref, g_ref, d_ref, nm_ref, nv_ref):
        g = p_ref[0]
        for s in range(1, n):
            g = g + p_ref[s]
        nm = B1 * m_ref[...] + (1.0 - B1) * g
        nv = B2 * v_ref[...] + (1.0 - B2) * (g * g)
        m_hat = nm / (1.0 - B1 ** STEP)
        v_hat = nv / (1.0 - B2 ** STEP)
        g_ref[...] = g
        nm_ref[...] = nm
        nv_ref[...] = nv
        d_ref[...] = -LR * (m_hat / (jnp.sqrt(v_hat) + EPS_ADAM) + WD * w_ref[...])

    blk = pl.BlockSpec((tr, c), lambda i: (i, 0))
    return pl.pallas_call(
        body, name=name, grid=(r // tr,),
        in_specs=[pl.BlockSpec((n, tr, c), lambda i: (0, i, 0)), blk, blk, blk],
        out_specs=[blk] * 4,
        out_shape=[jax.ShapeDtypeStruct((r, c), f32)] * 4,
        compiler_params=_cparams(1, arbitrary=False),
    )(parts, w, m, v)


_BIG = ["w_in", "w_out", "w_ffn_gate", "w_ffn_up", "w_ffn_down", "ple_w_proj", "ple_w_gate"]
_COL_SPLIT = {"w_in", "w_ffn_gate", "w_ffn_up", "ple_w_proj"}
_SMALL = ["b_in", "hg_lb_logits", "ml_conv_w", "ml_conv_b", "hg_norm_g", "ml_norm_g", "ln1_g", "ln1_b", "ln2_g", "ln2_b", "ple_b_gate"]
_ORDER = ["w_in", "b_in", "hg_lb_logits", "ml_conv_w", "ml_conv_b", "hg_norm_g", "ml_norm_g", "w_out", "ln1_g", "ln1_b",
          "w_ffn_gate", "w_ffn_up", "w_ffn_down", "ln2_g", "ln2_b", "ple_w_proj", "ple_w_gate", "ple_b_gate"]
_PACK_ROWS, _PACK_COLS = 16, 1024


def _pack(arrays):
    flat = jnp.concatenate([a.reshape(-1) for a in arrays])
    return jnp.pad(flat, (0, _PACK_ROWS * _PACK_COLS - flat.shape[0])).reshape(_PACK_ROWS, _PACK_COLS)


def _unpack(pack, shapes):
    flat = pack.reshape(-1)
    out, off = [], 0
    for s in shapes:
        size = 1
        for d in s:
            size *= d
        out.append(flat[off:off + size].reshape(s))
        off += size
    return out


def _to_chip_major(g, col_split):
    if col_split:
        k, n = g.shape
        return g.reshape(k, 4, n // 4).transpose(1, 0, 2)
    k, n = g.shape
    return g.reshape(4, k // 4, n)


def _from_chip_major(a, col_split):
    if col_split:
        return a.transpose(1, 0, 2).reshape(a.shape[1], 4 * a.shape[2])
    return a.reshape(4 * a.shape[1], a.shape[2])


def kernel(x, p, w_in, b_in, hg_lb_logits, ml_conv_w, ml_conv_b, hg_norm_g, ml_norm_g, w_out, ln1_g, ln1_b, w_ffn_gate, w_ffn_up, w_ffn_down, ln2_g, ln2_b, ple_w_proj, ple_w_gate, ple_b_gate, loss_target, m_w_in, m_b_in, m_hg_lb_logits, m_ml_conv_w, m_ml_conv_b, m_hg_norm_g, m_ml_norm_g, m_w_out, m_ln1_g, m_ln1_b, m_w_ffn_gate, m_w_ffn_up, m_w_ffn_down, m_ln2_g, m_ln2_b, m_ple_w_proj, m_ple_w_gate, m_ple_b_gate, v_w_in, v_b_in, v_hg_lb_logits, v_ml_conv_w, v_ml_conv_b, v_hg_norm_g, v_ml_norm_g, v_w_out, v_ln1_g, v_ln1_b, v_w_ffn_gate, v_w_ffn_up, v_w_ffn_down, v_ln2_g, v_ln2_b, v_ple_w_proj, v_ple_w_gate, v_ple_b_gate):
    args = dict(locals())
    wts = {k: args[k] for k in _ORDER}
    mom = {k: args["m_" + k] for k in _ORDER}
    var = {k: args["v_" + k] for k in _ORDER}
    two_d = lambda a: a.reshape(a.shape[-2], a.shape[-1])

    shards = [two_d(wts[k]).astype(bf16) for k in _BIG] + [two_d(ml_conv_w)]
    gathered = _allgather_chips(shards, "allgather_weights")
    full = {k: _from_chip_major(g, k in _COL_SPLIT) for k, g in zip(_BIG, gathered[:-1])}
    conv_w_full = _from_chip_major(gathered[-1], True)

    loss, grad_x, grads = _local_step(
        x[0], p[0, 0], loss_target[0], full["w_in"], b_in, hg_lb_logits, conv_w_full, ml_conv_b, hg_norm_g, ml_norm_g,
        full["w_out"], ln1_g, ln1_b, full["w_ffn_gate"], full["w_ffn_up"], full["w_ffn_down"], ln2_g, ln2_b,
        full["ple_w_proj"], full["ple_w_gate"], ple_b_gate)

    pieces = [_to_chip_major(grads[k], k in _COL_SPLIT) for k in _BIG]
    received = _scatter_chips(pieces, "scatter_grads")
    chip_sums = [_sum_slots(r, "sum_chips_" + k) for k, r in zip(_BIG, received)]
    both = _swap_cores(chip_sums, "swap_cores")
    out_g, out_d, out_m, out_v = {}, {}, {}, {}
    for k, parts in zip(_BIG, both):
        g, d, nm, nv = _adamw(parts, two_d(wts[k]), two_d(mom[k]), two_d(var[k]), "adamw_" + k)
        shp = wts[k].shape
        out_g[k], out_d[k], out_m[k], out_v[k] = g.reshape(shp), d.reshape(shp), nm.reshape(shp), nv.reshape(shp)

    small_shapes = [(1, PROJ_W), (2, MIX_W), (CONV_K, MIX_W)] + [(1, MIX_W)] * 3 + [(1, D_MODEL)] * 5 + [(1, 1)]
    contrib = _pack([grads[k] for k in _SMALL] + [loss])
    summed = _sum_slots(_gather_all(contrib, "gather_small"), "sum_small")
    small = _unpack(summed, small_shapes)
    loss_total = small[-1].reshape(())
    gsm = dict(zip(_SMALL, small[:-1]))
    place = 2 * lax.axis_index("x") + lax.axis_index("y")
    conv_cols = ml_conv_w.shape[-1]
    gsm["ml_conv_w"] = lax.dynamic_slice(gsm["ml_conv_w"], (0, place * conv_cols), (CONV_K, conv_cols))
    own_shapes = [wts[k].shape for k in _SMALL]
    g_pack = _pack([gsm[k] for k in _SMALL])
    res = _adamw(g_pack[None], _pack([wts[k] for k in _SMALL]), _pack([mom[k] for k in _SMALL]), _pack([var[k] for k in _SMALL]), "adamw_small")
    for dst, pack in zip((out_g, out_d, out_m, out_v), res):
        for k, a in zip(_SMALL, _unpack(pack, own_shapes)):
            dst[k] = a

    outs = [loss_total, grad_x[None]]
    for group in (out_g, out_d, out_m, out_v):
        outs += [group[k] for k in _ORDER]
    return tuple(outs)
```

```python
import functools

import jax
import jax.numpy as jnp
from jax import lax
from jax.experimental import pallas as pl
from jax.experimental.pallas import tpu as pltpu

f32 = jnp.float32
bf16 = jnp.bfloat16
HI = lax.Precision.HIGHEST

D_MODEL = 1024
HEADS = 4
HEAD_W = 128
MIX_W = HEADS * HEAD_W
ML_DQK = 64
PROJ_W = 3592
U_HG = 4 * MIX_W
U_ML = 3 * MIX_W + 128
D_FF = 2816
PLE = 256
CHUNK = 64
SUB = 16
EXP_CAP = 80.0
CONV_K = 4
HALO = 8
ALPHA = float(2.0 ** 0.25)
LN_EPS = 1e-5
RMS_EPS = 1e-6
NEG = -1e30
LR, B1, B2, EPS_ADAM, WD, STEP = 0.001, 0.9, 0.999, 1e-08, 0.01, 10
VMEM_LIMIT = 56 * 1024 * 1024


def _cparams(n_axes, arbitrary=True):
    sem = ("arbitrary",) * n_axes if arbitrary else ("parallel",) * n_axes
    return pltpu.CompilerParams(dimension_semantics=sem, vmem_limit_bytes=VMEM_LIMIT)


ACT = bf16


def _mx(a):
    return a.astype(ACT)


def _bdot(a, b):
    return jnp.dot(_mx(a), _mx(b), preferred_element_type=f32)


def _bdot_nt(a, b):
    return lax.dot_general(_mx(a), _mx(b), (((1,), (1,)), ((), ())), preferred_element_type=f32)


def _bdot_tn(a, b):
    return lax.dot_general(_mx(a), _mx(b), (((0,), (0,)), ((), ())), preferred_element_type=f32)


def _hdot(a, b):
    return jnp.dot(a, b, precision=HI, preferred_element_type=f32)


def _hdot_nt(a, b):
    return lax.dot_general(a, b, (((1,), (1,)), ((), ())), precision=HI, preferred_element_type=f32)


def _hdot_tn(a, b):
    return lax.dot_general(a, b, (((0,), (0,)), ((), ())), precision=HI, preferred_element_type=f32)


def _sigmoid(x):
    return 1.0 / (1.0 + jnp.exp(-x))


def _log_sigmoid(x):
    return jnp.minimum(x, 0.0) - jnp.log(1.0 + jnp.exp(-jnp.abs(x)))


def _tri(n, upper=False):
    r = lax.broadcasted_iota(jnp.int32, (n, n), 0)
    c = lax.broadcasted_iota(jnp.int32, (n, n), 1)
    return (c >= r) if upper else (c <= r)


def _rows(tm, n, col=0):
    return pl.BlockSpec((tm, n), lambda i, _c=col: (i, _c))


def _rows_rev(tm, n, nb, col=0):
    return pl.BlockSpec((tm, n), lambda i, _c=col, _nb=nb: (_nb - 1 - i, _c))


def _const(shape):
    return pl.BlockSpec(shape, lambda i, _n=len(shape): (0,) * _n)


def _tile(t, want):
    return want if t % want == 0 else t


def _inproj(x, w_hg, w_ml, b_hg, b_ml):
    t = x.shape[0]
    tm = _tile(t, 256)

    def body(x_ref, whg_ref, wml_ref, bhg_ref, bml_ref, uhg_ref, uml_ref):
        xv = x_ref[...]
        uhg_ref[...] = _bdot(xv, whg_ref[...]) + bhg_ref[...]
        uml_ref[...] = _bdot(xv, wml_ref[...]) + bml_ref[...]

    return pl.pallas_call(
        body, name="inproj", grid=(t // tm,),
        in_specs=[_rows(tm, D_MODEL), _const((D_MODEL, U_HG)), _const((D_MODEL, U_ML)), _const((1, U_HG)), _const((1, U_ML))],
        out_specs=[_rows(tm, U_HG), _rows(tm, U_ML)],
        out_shape=[jax.ShapeDtypeStruct((t, U_HG), f32), jax.ShapeDtypeStruct((t, U_ML), f32)],
        compiler_params=_cparams(1, arbitrary=False),
    )(x, w_hg, w_ml, b_hg, b_ml)


def _hg_gates(hq, hf, lb, tri):
    s = _sigmoid(hf)
    om = 1.0 - lb
    f = lb + om * s
    g = jnp.log(f)
    k = om * (1.0 - s)
    sq = _sigmoid(hq)
    q = hq * sq
    b = _hdot(tri, g)
    return q, sq, s, f, k, b


def _hg_scores(q, k, b, tril_mask):
    qts, kts, eqs, eks, rows = [], [], [], [], []
    for i in range(CHUNK // SUB):
        lo = i * SUB
        ref = jnp.zeros_like(b[0:1]) if i == 0 else b[lo - 1:lo]
        eq = jnp.exp(b[lo:lo + SUB] - ref)
        ek = jnp.exp(jnp.minimum(ref - b, EXP_CAP))
        qt = q[lo:lo + SUB] * eq
        kt = k * ek
        rows.append(_hdot_nt(qt, kt))
        qts.append(qt); kts.append(kt); eqs.append(eq); eks.append(ek)
    a = jnp.where(tril_mask, jnp.concatenate(rows, axis=0), 0.0)
    return a, qts, kts, eqs, eks


def _head_rms(o, gn):
    rstd = lax.rsqrt(jnp.mean(o * o, axis=-1, keepdims=True) + RMS_EPS)
    oh = o * rstd
    return oh, rstd, oh * gn


def _lower_bound(logit_ref):
    lg = logit_ref[...]
    return _sigmoid(lg[0:1] - lg[1:2])


def _hgrn2_fwd(u_hg, logits, gn):
    t = u_hg.shape[0]
    tb = _tile(t, 256)
    nc_blk = tb // CHUNK

    def body(u_ref, lg_ref, gn_ref, og_ref, sst_ref, st_ref):
        @pl.when(pl.program_id(0) == 0)
        def _():
            st_ref[...] = jnp.zeros_like(st_ref)

        lb_all = _lower_bound(lg_ref)
        tril_mask = _tri(CHUNK)
        tri = tril_mask.astype(f32)

        def chunk(c, carry):
            r0 = pl.multiple_of(c * CHUNK, CHUNK)
            rows = pl.ds(r0, CHUNK)
            for h in range(HEADS):
                cs = slice(h * HEAD_W, (h + 1) * HEAD_W)
                hq = u_ref[rows, h * HEAD_W:(h + 1) * HEAD_W]
                hf = u_ref[rows, MIX_W + h * HEAD_W:MIX_W + (h + 1) * HEAD_W]
                hv = u_ref[rows, 2 * MIX_W + h * HEAD_W:2 * MIX_W + (h + 1) * HEAD_W]
                hgate = u_ref[rows, 3 * MIX_W + h * HEAD_W:3 * MIX_W + (h + 1) * HEAD_W]
                q, _, _, _, k, b = _hg_gates(hq, hf, lb_all[:, cs], tri)
                a, _, _, _, _ = _hg_scores(q, k, b, tril_mask)
                st = st_ref[h]
                sst_ref[c, h] = st
                bl = b[CHUNK - 1:CHUNK]
                o = _bdot(a, hv) + _bdot_nt(q * jnp.exp(b), st)
                st_ref[h] = st * jnp.exp(bl) + _bdot_tn(hv, k * jnp.exp(bl - b))
                _, _, y = _head_rms(o, gn_ref[:, cs])
                og_ref[rows, cs] = (y * (hgate * _sigmoid(hgate))).astype(ACT)
            return carry

        lax.fori_loop(0, nc_blk, chunk, 0)

    return pl.pallas_call(
        body, name="hgrn2_fwd", grid=(t // tb,),
        in_specs=[_rows(tb, U_HG), _const((2, MIX_W)), _const((1, MIX_W))],
        out_specs=[_rows(tb, MIX_W), pl.BlockSpec((nc_blk, HEADS, HEAD_W, HEAD_W), lambda i: (i, 0, 0, 0))],
        out_shape=[jax.ShapeDtypeStruct((t, MIX_W), ACT), jax.ShapeDtypeStruct((t // CHUNK, HEADS, HEAD_W, HEAD_W), f32)],
        scratch_shapes=[pltpu.VMEM((HEADS, HEAD_W, HEAD_W), f32)],
        compiler_params=_cparams(1),
    )(u_hg, logits, gn)


def _hgrn2_bwd(u_hg, logits, gn, sst, dog):
    t = u_hg.shape[0]
    tb = _tile(t, 256)
    nb = t // tb
    nc_blk = tb // CHUNK

    def body(u_ref, lg_ref, gn_ref, sst_ref, dog_ref, du_ref, dlg_ref, dgn_ref, dst_ref):
        @pl.when(pl.program_id(0) == 0)
        def _():
            dst_ref[...] = jnp.zeros_like(dst_ref)
            dlg_ref[...] = jnp.zeros_like(dlg_ref)
            dgn_ref[...] = jnp.zeros_like(dgn_ref)

        lb_all = _lower_bound(lg_ref)
        tril_mask = _tri(CHUNK)
        tri = tril_mask.astype(f32)
        triu = _tri(CHUNK, upper=True).astype(f32)

        def chunk(j, carry):
            c = nc_blk - 1 - j
            r0 = pl.multiple_of(c * CHUNK, CHUNK)
            rows = pl.ds(r0, CHUNK)
            for h in range(HEADS):
                cs = slice(h * HEAD_W, (h + 1) * HEAD_W)
                hq = u_ref[rows, h * HEAD_W:(h + 1) * HEAD_W]
                hf = u_ref[rows, MIX_W + h * HEAD_W:MIX_W + (h + 1) * HEAD_W]
                hv = u_ref[rows, 2 * MIX_W + h * HEAD_W:2 * MIX_W + (h + 1) * HEAD_W]
                hgate = u_ref[rows, 3 * MIX_W + h * HEAD_W:3 * MIX_W + (h + 1) * HEAD_W]
                lb = lb_all[:, cs]
                gnh = gn_ref[:, cs]
                q, sq, s, f, k, b = _hg_gates(hq, hf, lb, tri)
                a, qts, kts, eqs, eks = _hg_scores(q, k, b, tril_mask)
                st = sst_ref[c, h]
                dst = dst_ref[h]
                bl = b[CHUNK - 1:CHUNK]
                eb = jnp.exp(b)
                qh = q * eb
                ekl = jnp.exp(bl - b)
                kh = k * ekl
                o = _bdot(a, hv) + _bdot_nt(qh, st)
                oh, rstd, y = _head_rms(o, gnh)
                sg = _sigmoid(hgate)
                dogh = dog_ref[rows, cs]
                dy = dogh * (hgate * sg)
                du_ref[rows, 3 * MIX_W + h * HEAD_W:3 * MIX_W + (h + 1) * HEAD_W] = dogh * y * (sg * (1.0 + hgate * (1.0 - sg)))
                dgn_ref[:, cs] += jnp.sum(dy * oh, axis=0, keepdims=True)
                doh = dy * gnh
                do = rstd * (doh - oh * jnp.mean(doh * oh, axis=-1, keepdims=True))
                da = jnp.where(tril_mask, _bdot_nt(do, hv), 0.0)
                dv = _bdot_tn(a, do) + _bdot_nt(kh, dst)
                dq = _bdot(do, st) * eb
                dk = _bdot(hv, dst) * ekl
                d_last = jnp.sum(k * dk, axis=0, keepdims=True) + jnp.exp(bl) * jnp.sum(dst * st, axis=0, keepdims=True)
                dqs = []
                for i in range(CHUNK // SUB):
                    da_i = da[i * SUB:(i + 1) * SUB]
                    dqs.append(_hdot(da_i, kts[i]) * eqs[i])
                    dk = dk + _hdot_tn(da_i, qts[i]) * eks[i]
                dq = dq + jnp.concatenate(dqs, axis=0)
                dst_ref[h] = dst * jnp.exp(bl) + _bdot_tn(do, qh)
                dg = _hdot(triu, q * dq - k * dk) + d_last
                df = dg / f
                dfk = df - dk
                du_ref[rows, h * HEAD_W:(h + 1) * HEAD_W] = dq * (sq * (1.0 + hq * (1.0 - sq)))
                du_ref[rows, MIX_W + h * HEAD_W:MIX_W + (h + 1) * HEAD_W] = (1.0 - lb) * dfk * s * (1.0 - s)
                du_ref[rows, 2 * MIX_W + h * HEAD_W:2 * MIX_W + (h + 1) * HEAD_W] = dv
                dlb = jnp.sum((1.0 - s) * dfk, axis=0, keepdims=True) * (lb * (1.0 - lb))
                dlg_ref[0:1, cs] += dlb
                dlg_ref[1:2, cs] -= dlb
            return carry

        lax.fori_loop(0, nc_blk, chunk, 0)

    return pl.pallas_call(
        body, name="hgrn2_bwd", grid=(nb,),
        in_specs=[_rows_rev(tb, U_HG, nb), _const((2, MIX_W)), _const((1, MIX_W)),
                  pl.BlockSpec((nc_blk, HEADS, HEAD_W, HEAD_W), lambda i: (nb - 1 - i, 0, 0, 0)), _rows_rev(tb, MIX_W, nb)],
        out_specs=[_rows_rev(tb, U_HG, nb), _const((2, MIX_W)), _const((1, MIX_W))],
        out_shape=[jax.ShapeDtypeStruct((t, U_HG), f32), jax.ShapeDtypeStruct((2, MIX_W), f32), jax.ShapeDtypeStruct((1, MIX_W), f32)],
        scratch_shapes=[pltpu.VMEM((HEADS, HEAD_W, HEAD_W), f32)],
        compiler_params=_cparams(1),
    )(u_hg, logits, gn, sst, dog)


def _conv_fwd(u_ml, w, b):
    t = u_ml.shape[0]
    tm = _tile(t, 512)

    def body(x_ref, w_ref, b_ref, pre_ref, act_ref, xbuf):
        @pl.when(pl.program_id(0) == 0)
        def _():
            xbuf[...] = jnp.zeros_like(xbuf)

        xbuf[0:HALO, :] = xbuf[tm:tm + HALO, :]
        xbuf[HALO:HALO + tm, :] = x_ref[...]
        pre = b_ref[...] + jnp.zeros((tm, MIX_W), f32)
        for kk in range(CONV_K):
            off = HALO - (CONV_K - 1) + kk
            pre = pre + w_ref[kk:kk + 1, :] * xbuf[off:off + tm, :]
        pre_ref[...] = pre
        act_ref[...] = pre * _sigmoid(pre)

    return pl.pallas_call(
        body, name="conv_fwd", grid=(t // tm,),
        in_specs=[_rows(tm, MIX_W), _const((CONV_K, MIX_W)), _const((1, MIX_W))],
        out_specs=[_rows(tm, MIX_W), _rows(tm, MIX_W)],
        out_shape=[jax.ShapeDtypeStruct((t, MIX_W), f32)] * 2,
        scratch_shapes=[pltpu.VMEM((tm + HALO, MIX_W), f32)],
        compiler_params=_cparams(1),
    )(u_ml, w, b)


def _conv_bwd(u_ml, w, pre, dact):
    t = u_ml.shape[0]
    tm = _tile(t, 512)
    nb = t // tm
    hb = tm // HALO

    def body(x_ref, halo_ref, w_ref, pre_ref, dact_ref, dx_ref, dw_ref, db_ref, dbuf, xbuf):
        i = pl.program_id(0)

        @pl.when(i == 0)
        def _():
            dbuf[...] = jnp.zeros_like(dbuf)
            dw_ref[...] = jnp.zeros_like(dw_ref)
            db_ref[...] = jnp.zeros_like(db_ref)

        p = pre_ref[...]
        sg = _sigmoid(p)
        dpre = dact_ref[...] * (sg * (1.0 + p * (1.0 - sg)))
        dbuf[tm:tm + HALO, :] = dbuf[0:HALO, :]
        dbuf[0:tm, :] = dpre
        has_prev = (i < nb - 1).astype(f32)
        xbuf[0:HALO, :] = halo_ref[...] * has_prev
        xbuf[HALO:HALO + tm, :] = x_ref[...]
        dx = jnp.zeros((tm, MIX_W), f32)
        for kk in range(CONV_K):
            back = CONV_K - 1 - kk
            dx = dx + w_ref[kk:kk + 1, :] * dbuf[back:back + tm, :]
            off = HALO - (CONV_K - 1) + kk
            dw_ref[kk:kk + 1, :] += jnp.sum(dpre * xbuf[off:off + tm, :], axis=0, keepdims=True)
        dx_ref[...] = dx
        db_ref[...] += jnp.sum(dpre, axis=0, keepdims=True)

    return pl.pallas_call(
        body, name="conv_bwd", grid=(nb,),
        in_specs=[_rows_rev(tm, MIX_W, nb),
                  pl.BlockSpec((HALO, MIX_W), lambda i: (jnp.maximum((nb - 1 - i) * hb - 1, 0), 0)),
                  _const((CONV_K, MIX_W)), _rows_rev(tm, MIX_W, nb), _rows_rev(tm, MIX_W, nb)],
        out_specs=[_rows_rev(tm, MIX_W, nb), _const((CONV_K, MIX_W)), _const((1, MIX_W))],
        out_shape=[jax.ShapeDtypeStruct((t, MIX_W), f32), jax.ShapeDtypeStruct((CONV_K, MIX_W), f32), jax.ShapeDtypeStruct((1, MIX_W), f32)],
        scratch_shapes=[pltpu.VMEM((tm + HALO, MIX_W), f32), pltpu.VMEM((tm + HALO, MIX_W), f32)],
        compiler_params=_cparams(1),
    )(u_ml, u_ml, w, pre, dact)


def _lane_pick(x, lane):
    idx = lax.broadcasted_iota(jnp.int32, x.shape, 1)
    return jnp.sum(jnp.where(idx == lane, x, 0.0), axis=-1, keepdims=True)


def _ml_gate_forms(gates, tri):
    lf = _log_sigmoid(gates)
    gc = _hdot(tri, lf)
    lane = lax.broadcasted_iota(jnp.int32, gates.shape, 1)
    mixed = jnp.where(lane < HEADS, gates, gc)
    sel = (lax.broadcasted_iota(jnp.int32, (8, 128), 0) == lax.broadcasted_iota(jnp.int32, (8, 128), 1)).astype(f32)
    rowsf = _hdot_nt(sel, mixed)
    return gc, rowsf


def _ml_chunk(q, k, v, gates, gc, rowsf, h, c_st, n_st, m_st, tril_mask):
    g_col = _lane_pick(gc, HEADS + h)
    ig_col = _lane_pick(gates, h)
    ig_row = rowsf[h:h + 1, :]
    g_row = rowsf[HEADS + h:HEADS + h + 1, :]
    dmat = jnp.where(tril_mask, g_col - g_row + ig_row, NEG)
    m_inter = g_col + m_st
    m_t = jnp.maximum(m_inter, jnp.max(dmat, axis=-1, keepdims=True))
    wi = jnp.exp(dmat - m_t)
    wo = jnp.exp(m_inter - m_t)
    qk = _bdot_nt(q, k) * wi
    num = _bdot(qk, v) + wo * _bdot(q, c_st)
    den = jnp.sum(qk, axis=-1, keepdims=True) + wo * jnp.sum(q * n_st, axis=-1, keepdims=True)
    floor = jnp.exp(-m_t)
    z = jnp.maximum(jnp.abs(den), floor)
    g_last = g_col[CHUNK - 1:CHUNK]
    a_col = g_last - g_col + ig_col
    m_new = jnp.maximum(g_last + m_st, jnp.max(a_col, axis=0, keepdims=True))
    ws = jnp.exp(a_col - m_new)
    w_old = jnp.exp(g_last + m_st - m_new)
    return dict(wi=wi, wo=wo, qk=qk, num=num, den=den, z=z, floor=floor, ws=ws, w_old=w_old, m_new=m_new)


def _mlstm_fwd(qkc, u_ml, gn):
    t = qkc.shape[0]
    tb = _tile(t, 256)
    nc_blk = tb // CHUNK

    def body(qk_ref, v_ref, mo_ref, gt_ref, gn_ref, og_ref, cst_ref, nst_ref, mst_ref, c_sc, n_sc, m_sc):
        @pl.when(pl.program_id(0) == 0)
        def _():
            c_sc[...] = jnp.zeros_like(c_sc)
            n_sc[...] = jnp.zeros_like(n_sc)
            m_sc[...] = jnp.zeros_like(m_sc)

        tril_mask = _tri(CHUNK)
        tri = tril_mask.astype(f32)

        def chunk(c, carry):
            r0 = pl.multiple_of(c * CHUNK, CHUNK)
            rows = pl.ds(r0, CHUNK)
            gates = gt_ref[rows, :]
            gc, rowsf = _ml_gate_forms(gates, tri)
            for h in range(HEADS):
                cs = slice(h * HEAD_W, (h + 1) * HEAD_W)
                q = qk_ref[rows, h * ML_DQK:(h + 1) * ML_DQK] * (ML_DQK ** -0.5)
                k = qk_ref[rows, HEADS * ML_DQK + h * ML_DQK:HEADS * ML_DQK + (h + 1) * ML_DQK]
                v = v_ref[rows, h * HEAD_W:(h + 1) * HEAD_W]
                mo = mo_ref[rows, h * HEAD_W:(h + 1) * HEAD_W]
                c_st = c_sc[h]
                n_st = n_sc[h]
                m_st = m_sc[h][:, 0:1]
                cst_ref[c, h] = c_st
                nst_ref[c, h] = n_st
                mst_ref[c, h] = m_sc[h]
                r = _ml_chunk(q, k, v, gates, gc, rowsf, h, c_st, n_st, m_st, tril_mask)
                hh = r["num"] / r["z"]
                ksc = k * r["ws"]
                c_sc[h] = r["w_old"] * c_st + _bdot_tn(ksc, v)
                n_sc[h] = r["w_old"] * n_st + jnp.sum(ksc, axis=0, keepdims=True)
                m_sc[h] = r["m_new"] + jnp.zeros((1, 128), f32)
                _, _, y = _head_rms(hh, gn_ref[:, cs])
                og_ref[rows, cs] = (y * _sigmoid(mo)).astype(ACT)
            return carry

        lax.fori_loop(0, nc_blk, chunk, 0)

    nchunks = t // CHUNK
    return pl.pallas_call(
        body, name="mlstm_fwd", grid=(t // tb,),
        in_specs=[_rows(tb, MIX_W), _rows(tb, MIX_W, 1), _rows(tb, MIX_W, 2), _rows(tb, 128, 12), _const((1, MIX_W))],
        out_specs=[_rows(tb, MIX_W),
                   pl.BlockSpec((nc_blk, HEADS, ML_DQK, HEAD_W), lambda i: (i, 0, 0, 0)),
                   pl.BlockSpec((nc_blk, HEADS, 1, ML_DQK), lambda i: (i, 0, 0, 0)),
                   pl.BlockSpec((nc_blk, HEADS, 1, 128), lambda i: (i, 0, 0, 0))],
        out_shape=[jax.ShapeDtypeStruct((t, MIX_W), ACT),
                   jax.ShapeDtypeStruct((nchunks, HEADS, ML_DQK, HEAD_W), f32),
                   jax.ShapeDtypeStruct((nchunks, HEADS, 1, ML_DQK), f32),
                   jax.ShapeDtypeStruct((nchunks, HEADS, 1, 128), f32)],
        scratch_shapes=[pltpu.VMEM((HEADS, ML_DQK, HEAD_W), f32), pltpu.VMEM((HEADS, 1, ML_DQK), f32), pltpu.VMEM((HEADS, 1, 128), f32)],
        compiler_params=_cparams(1),
    )(qkc, u_ml, u_ml, u_ml, gn)


def _mlstm_bwd(qkc, u_ml, gn, cst, nst, mst, dog):
    t = qkc.shape[0]
    tb = _tile(t, 256)
    nb = t // tb
    nc_blk = tb // CHUNK

    def body(qk_ref, v_ref, mo_ref, gt_ref, gn_ref, cst_ref, nst_ref, mst_ref, dog_ref,
             dqk_ref, dv_ref, dmo_ref, dgt_ref, dgn_ref, dc_sc, dn_sc):
        @pl.when(pl.program_id(0) == 0)
        def _():
            dc_sc[...] = jnp.zeros_like(dc_sc)
            dn_sc[...] = jnp.zeros_like(dn_sc)
            dgn_ref[...] = jnp.zeros_like(dgn_ref)

        tril_mask = _tri(CHUNK)
        tri = tril_mask.astype(f32)
        triu = _tri(CHUNK, upper=True).astype(f32)
        lane = lax.broadcasted_iota(jnp.int32, (CHUNK, 128), 1)

        def chunk(j, carry):
            c = nc_blk - 1 - j
            r0 = pl.multiple_of(c * CHUNK, CHUNK)
            rows = pl.ds(r0, CHUNK)
            gates = gt_ref[rows, :]
            gc, rowsf = _ml_gate_forms(gates, tri)
            dg_mat = jnp.zeros((CHUNK, 128), f32)
            dig_mat = jnp.zeros((CHUNK, 128), f32)
            dlast_row = jnp.zeros((1, 128), f32)
            for h in range(HEADS):
                cs = slice(h * HEAD_W, (h + 1) * HEAD_W)
                q = qk_ref[rows, h * ML_DQK:(h + 1) * ML_DQK] * (ML_DQK ** -0.5)
                k = qk_ref[rows, HEADS * ML_DQK + h * ML_DQK:HEADS * ML_DQK + (h + 1) * ML_DQK]
                v = v_ref[rows, h * HEAD_W:(h + 1) * HEAD_W]
                mo = mo_ref[rows, h * HEAD_W:(h + 1) * HEAD_W]
                gnh = gn_ref[:, cs]
                c_st = cst_ref[c, h]
                n_st = nst_ref[c, h]
                m_st = mst_ref[c, h][:, 0:1]
                dc = dc_sc[h]
                dn = dn_sc[h]
                r = _ml_chunk(q, k, v, gates, gc, rowsf, h, c_st, n_st, m_st, tril_mask)
                z = r["z"]
                hh = r["num"] / z
                oh, rstd, y = _head_rms(hh, gnh)
                sg = _sigmoid(mo)
                dogh = dog_ref[rows, cs]
                dy = dogh * sg
                dmo_ref[rows, cs] = dogh * y * (sg * (1.0 - sg))
                dgn_ref[:, cs] += jnp.sum(dy * oh, axis=0, keepdims=True)
                doh = dy * gnh
                dh = rstd * (doh - oh * jnp.mean(doh * oh, axis=-1, keepdims=True))
                dnum = dh / z
                dz = -jnp.sum(dh * hh, axis=-1, keepdims=True) / z
                den = r["den"]
                dden = jnp.where(jnp.abs(den) > r["floor"], dz * jnp.sign(den), 0.0)
                dsw = (_bdot_nt(dnum, v) + dden) * r["wi"]
                wo = r["wo"]
                ws = r["ws"]
                dq = _bdot(dsw, k) + wo * (_bdot_nt(dnum, c_st) + dden * n_st)
                dk_state = ws * (_bdot_nt(v, dc) + dn)
                dk = _bdot_tn(dsw, q) + dk_state
                dv_ref[rows, cs] = _bdot_tn(r["qk"], dnum) + ws * _bdot(k, dc)
                woq = wo * q
                w_old = r["w_old"]
                dc_sc[h] = w_old * dc + _bdot_tn(woq, dnum)
                dn_sc[h] = w_old * dn + jnp.sum(woq * dden, axis=0, keepdims=True)
                d_last = (jnp.sum(jnp.sum(k * dk_state, axis=-1, keepdims=True), axis=0, keepdims=True)
                          + w_old * (jnp.sum(jnp.sum(dc * c_st, axis=-1, keepdims=True), axis=0, keepdims=True)
                                     + jnp.sum(dn * n_st, axis=-1, keepdims=True)))
                kdk = jnp.sum(k * dk, axis=-1, keepdims=True)
                qdq = jnp.sum(q * dq, axis=-1, keepdims=True)
                dg_mat = dg_mat + jnp.where(lane == HEADS + h, qdq - kdk, 0.0)
                dlast_row = dlast_row + jnp.where(lane[0:1] == HEADS + h, d_last, 0.0)
                dig_mat = dig_mat + jnp.where(lane == h, kdk, 0.0)
                dqk_ref[rows, h * ML_DQK:(h + 1) * ML_DQK] = dq * (ML_DQK ** -0.5)
                dqk_ref[rows, HEADS * ML_DQK + h * ML_DQK:HEADS * ML_DQK + (h + 1) * ML_DQK] = dk
            dlf = _hdot(triu, dg_mat) + dlast_row
            dgt_ref[rows, :] = dig_mat + dlf * _sigmoid(-gates)
            return carry

        lax.fori_loop(0, nc_blk, chunk, 0)

    st4 = lambda a, b: pl.BlockSpec((nc_blk, HEADS, a, b), lambda i: (nb - 1 - i, 0, 0, 0))
    return pl.pallas_call(
        body, name="mlstm_bwd", grid=(nb,),
        in_specs=[_rows_rev(tb, MIX_W, nb), _rows_rev(tb, MIX_W, nb, 1), _rows_rev(tb, MIX_W, nb, 2), _rows_rev(tb, 128, nb, 12),
                  _const((1, MIX_W)), st4(ML_DQK, HEAD_W), st4(1, ML_DQK), st4(1, 128), _rows_rev(tb, MIX_W, nb)],
        out_specs=[_rows_rev(tb, MIX_W, nb), _rows_rev(tb, MIX_W, nb), _rows_rev(tb, MIX_W, nb), _rows_rev(tb, 128, nb), _const((1, MIX_W))],
        out_shape=[jax.ShapeDtypeStruct((t, MIX_W), f32)] * 3 + [jax.ShapeDtypeStruct((t, 128), f32), jax.ShapeDtypeStruct((1, MIX_W), f32)],
        scratch_shapes=[pltpu.VMEM((HEADS, ML_DQK, HEAD_W), f32), pltpu.VMEM((HEADS, 1, ML_DQK), f32)],
        compiler_params=_cparams(1),
    )(qkc, u_ml, u_ml, u_ml, gn, cst, nst, mst, dog)


def _ln_fwd(r, g, b):
    mu = jnp.mean(r, axis=-1, keepdims=True)
    xc = r - mu
    rstd = lax.rsqrt(jnp.mean(xc * xc, axis=-1, keepdims=True) + LN_EPS)
    xh = xc * rstd
    return xh * g + b, xh, rstd


def _ln_bwd(dy, xh, rstd, g):
    dxh = dy * g
    return rstd * (dxh - jnp.mean(dxh, axis=-1, keepdims=True) - xh * jnp.mean(dxh * xh, axis=-1, keepdims=True))


def _outproj_ln1(og_hg, og_ml, x, w_out, g, b):
    t = x.shape[0]
    tm = _tile(t, 256)

    def body(a_ref, b_ref, x_ref, w_ref, g_ref, bb_ref, x1_ref, xh_ref, rs_ref):
        mix = _bdot(a_ref[...], w_ref[0:MIX_W, :]) + _bdot(b_ref[...], w_ref[MIX_W:2 * MIX_W, :])
        y, xh, rstd = _ln_fwd(ALPHA * x_ref[...] + mix, g_ref[...], bb_ref[...])
        x1_ref[...] = y
        xh_ref[...] = xh
        rs_ref[...] = rstd

    return pl.pallas_call(
        body, name="outproj_ln1", grid=(t // tm,),
        in_specs=[_rows(tm, MIX_W), _rows(tm, MIX_W), _rows(tm, D_MODEL), _const((D_MODEL, D_MODEL)), _const((1, D_MODEL)), _const((1, D_MODEL))],
        out_specs=[_rows(tm, D_MODEL), _rows(tm, D_MODEL), _rows(tm, 1)],
        out_shape=[jax.ShapeDtypeStruct((t, D_MODEL), f32), jax.ShapeDtypeStruct((t, D_MODEL), f32), jax.ShapeDtypeStruct((t, 1), f32)],
        compiler_params=_cparams(1, arbitrary=False),
    )(og_hg, og_ml, x, w_out, g, b)


def _ffn_up(x1, wg, wu):
    t = x1.shape[0]
    tm = _tile(t, 256)

    def body(x_ref, wg_ref, wu_ref, hg_ref, up_ref, a_ref):
        xv = x_ref[...]
        hg = _bdot(xv, wg_ref[...])
        up = _bdot(xv, wu_ref[...])
        hg_ref[...] = hg
        up_ref[...] = up
        a_ref[...] = (hg * _sigmoid(hg) * up).astype(ACT)

    return pl.pallas_call(
        body, name="ffn_up", grid=(t // tm,),
        in_specs=[_rows(tm, D_MODEL), _const((D_MODEL, D_FF)), _const((D_MODEL, D_FF))],
        out_specs=[_rows(tm, D_FF), _rows(tm, D_FF), _rows(tm, D_FF)],
        out_shape=[jax.ShapeDtypeStruct((t, D_FF), f32), jax.ShapeDtypeStruct((t, D_FF), f32), jax.ShapeDtypeStruct((t, D_FF), ACT)],
        compiler_params=_cparams(1, arbitrary=False),
    )(x1, wg, wu)


def _ffn_down_ln2(a, x1, wd, g, b):
    t = x1.shape[0]
    tm = _tile(t, 256)

    def body(a_ref, x_ref, w_ref, g_ref, bb_ref, x2_ref, xh_ref, rs_ref):
        ffn = _bdot(a_ref[...], w_ref[...])
        y, xh, rstd = _ln_fwd(ALPHA * x_ref[...] + ffn, g_ref[...], bb_ref[...])
        x2_ref[...] = y
        xh_ref[...] = xh
        rs_ref[...] = rstd

    return pl.pallas_call(
        body, name="ffn_down_ln2", grid=(t // tm,),
        in_specs=[_rows(tm, D_FF), _rows(tm, D_MODEL), _const((D_FF, D_MODEL)), _const((1, D_MODEL)), _const((1, D_MODEL))],
        out_specs=[_rows(tm, D_MODEL), _rows(tm, D_MODEL), _rows(tm, 1)],
        out_shape=[jax.ShapeDtypeStruct((t, D_MODEL), f32), jax.ShapeDtypeStruct((t, D_MODEL), f32), jax.ShapeDtypeStruct((t, 1), f32)],
        compiler_params=_cparams(1, arbitrary=False),
    )(a, x1, wd, g, b)


def _head_loss_bwd(x2, xh2, rs2, p, tgt, w_pg, b_pg, w_pp, g2):
    t = x2.shape[0]
    tm = _tile(t, 256)

    def body(x_ref, xh_ref, rs_ref, p_ref, t_ref, wg_ref, bg_ref, wp_ref, g_ref,
             dr_ref, de_ref, dz_ref, loss_ref, dbg_ref, dg2_ref, db2_ref):
        @pl.when(pl.program_id(0) == 0)
        def _():
            loss_ref[...] = jnp.zeros_like(loss_ref)
            dbg_ref[...] = jnp.zeros_like(dbg_ref)
            dg2_ref[...] = jnp.zeros_like(dg2_ref)
            db2_ref[...] = jnp.zeros_like(db2_ref)

        x2v = x_ref[...]
        z = _bdot(x2v, wg_ref[...]) + bg_ref[...]
        e = _bdot(p_ref[...], wp_ref[...])
        sg = _sigmoid(z)
        diff = x2v + sg * e - t_ref[...]
        loss_ref[...] += 0.5 * jnp.sum(jnp.mean(diff * diff, axis=-1, keepdims=True), axis=0, keepdims=True)
        dy = diff * (1.0 / D_MODEL)
        de_ref[...] = (dy * sg).astype(ACT)
        dz = dy * e * (sg * (1.0 - sg))
        dz_ref[...] = dz.astype(ACT)
        dbg_ref[...] += jnp.sum(dz, axis=0, keepdims=True)
        dx2 = dy + _bdot_nt(dz, wg_ref[...])
        xh = xh_ref[...]
        dg2_ref[...] += jnp.sum(dx2 * xh, axis=0, keepdims=True)
        db2_ref[...] += jnp.sum(dx2, axis=0, keepdims=True)
        dr_ref[...] = _ln_bwd(dx2, xh, rs_ref[...], g_ref[...])

    row = jax.ShapeDtypeStruct((1, D_MODEL), f32)
    return pl.pallas_call(
        body, name="head_loss_bwd", grid=(t // tm,),
        in_specs=[_rows(tm, D_MODEL), _rows(tm, D_MODEL), _rows(tm, 1), _rows(tm, PLE), _rows(tm, D_MODEL),
                  _const((D_MODEL, D_MODEL)), _const((1, D_MODEL)), _const((PLE, D_MODEL)), _const((1, D_MODEL))],
        out_specs=[_rows(tm, D_MODEL), _rows(tm, D_MODEL), _rows(tm, D_MODEL), _const((1, 1)), _const((1, D_MODEL)), _const((1, D_MODEL)), _const((1, D_MODEL))],
        out_shape=[jax.ShapeDtypeStruct((t, D_MODEL), f32), jax.ShapeDtypeStruct((t, D_MODEL), ACT), jax.ShapeDtypeStruct((t, D_MODEL), ACT),
                   jax.ShapeDtypeStruct((1, 1), f32), row, row, row],
        compiler_params=_cparams(1),
    )(x2, xh2, rs2, p, tgt, w_pg, b_pg, w_pp, g2)


def _ffn_bwd(dr2, hg, up, xh1, rs1, wd, wg, wu, g1):
    t = dr2.shape[0]
    tm = _tile(t, 128)

    def body(dr_ref, hg_ref, up_ref, xh_ref, rs_ref, wd_ref, wg_ref, wu_ref, g_ref,
             dr1_ref, dhg_ref, dup_ref, dg1_ref, db1_ref):
        @pl.when(pl.program_id(0) == 0)
        def _():
            dg1_ref[...] = jnp.zeros_like(dg1_ref)
            db1_ref[...] = jnp.zeros_like(db1_ref)

        dr2v = dr_ref[...]
        da = _bdot_nt(dr2v, wd_ref[...])
        hgv = hg_ref[...]
        sg = _sigmoid(hgv)
        dhg = da * up_ref[...] * (sg * (1.0 + hgv * (1.0 - sg)))
        dup = da * (hgv * sg)
        dhg_ref[...] = dhg.astype(ACT)
        dup_ref[...] = dup.astype(ACT)
        dx1 = ALPHA * dr2v + _bdot_nt(dhg, wg_ref[...]) + _bdot_nt(dup, wu_ref[...])
        xh = xh_ref[...]
        dg1_ref[...] += jnp.sum(dx1 * xh, axis=0, keepdims=True)
        db1_ref[...] += jnp.sum(dx1, axis=0, keepdims=True)
        dr1_ref[...] = _ln_bwd(dx1, xh, rs_ref[...], g_ref[...])

    row = jax.ShapeDtypeStruct((1, D_MODEL), f32)
    return pl.pallas_call(
        body, name="ffn_bwd", grid=(t // tm,),
        in_specs=[_rows(tm, D_MODEL), _rows(tm, D_FF), _rows(tm, D_FF), _rows(tm, D_MODEL), _rows(tm, 1),
                  _const((D_FF, D_MODEL)), _const((D_MODEL, D_FF)), _const((D_MODEL, D_FF)), _const((1, D_MODEL))],
        out_specs=[_rows(tm, D_MODEL), _rows(tm, D_FF), _rows(tm, D_FF), _const((1, D_MODEL)), _const((1, D_MODEL))],
        out_shape=[jax.ShapeDtypeStruct((t, D_MODEL), f32), jax.ShapeDtypeStruct((t, D_FF), ACT), jax.ShapeDtypeStruct((t, D_FF), ACT), row, row],
        compiler_params=_cparams(1),
    )(dr2, hg, up, xh1, rs1, wd, wg, wu, g1)


def _outproj_bwd(dr1, w_out):
    t = dr1.shape[0]
    tm = _tile(t, 256)

    def body(dr_ref, w_ref, dhg_ref, dml_ref):
        d = _bdot_nt(dr_ref[...], w_ref[...])
        dhg_ref[...] = d[:, 0:MIX_W]
        dml_ref[...] = d[:, MIX_W:2 * MIX_W]

    return pl.pallas_call(
        body, name="outproj_bwd", grid=(t // tm,),
        in_specs=[_rows(tm, D_MODEL), _const((D_MODEL, D_MODEL))],
        out_specs=[_rows(tm, MIX_W), _rows(tm, MIX_W)],
        out_shape=[jax.ShapeDtypeStruct((t, MIX_W), f32)] * 2,
        compiler_params=_cparams(1, arbitrary=False),
    )(dr1, w_out)


def _inproj_bwd(dr1, du_hg, dqk, dmv, dmo, dgt, w_hg, w_ml):
    t = dr1.shape[0]
    tm = _tile(t, 256)

    def body(dr_ref, dhg_ref, dqk_ref, dmv_ref, dmo_ref, dgt_ref, whg_ref, wml_ref, gx_ref, dml_ref):
        dml = jnp.concatenate([dqk_ref[...], dmv_ref[...], dmo_ref[...], dgt_ref[...]], axis=-1).astype(ACT)
        dml_ref[...] = dml
        gx_ref[...] = ALPHA * dr_ref[...] + _bdot_nt(dhg_ref[...], whg_ref[...]) + _bdot_nt(dml, wml_ref[...])

    return pl.pallas_call(
        body, name="inproj_bwd", grid=(t // tm,),
        in_specs=[_rows(tm, D_MODEL), _rows(tm, U_HG), _rows(tm, MIX_W), _rows(tm, MIX_W), _rows(tm, MIX_W), _rows(tm, 128),
                  _const((D_MODEL, U_HG)), _const((D_MODEL, U_ML))],
        out_specs=[_rows(tm, D_MODEL), _rows(tm, U_ML)],
        out_shape=[jax.ShapeDtypeStruct((t, D_MODEL), f32), jax.ShapeDtypeStruct((t, U_ML), ACT)],
        compiler_params=_cparams(1, arbitrary=False),
    )(dr1, du_hg, dqk, dmv, dmo, dgt, w_hg, w_ml)


def _wgrad(a, b, name, tk=None, tn=None):
    t, kdim = a.shape
    n = b.shape[1]
    tk = tk or kdim
    tn = tn or n
    tt = _tile(t, 512)

    def body(a_ref, b_ref, o_ref):
        @pl.when(pl.program_id(2) == 0)
        def _():
            o_ref[...] = jnp.zeros_like(o_ref)

        o_ref[...] += _bdot_tn(a_ref[...], b_ref[...])

    return pl.pallas_call(
        body, name=name, grid=(kdim // tk, n // tn, t // tt),
        in_specs=[pl.BlockSpec((tt, tk), lambda i, j, s: (s, i)), pl.BlockSpec((tt, tn), lambda i, j, s: (s, j))],
        out_specs=pl.BlockSpec((tk, tn), lambda i, j, s: (i, j)),
        out_shape=jax.ShapeDtypeStruct((kdim, n), f32),
        compiler_params=_cparams(3),
    )(a, b)


def _colsum(parts, name):
    t = parts[0].shape[0]
    tt = _tile(t, 512)
    widths = [a.shape[1] for a in parts]

    def body(*refs):
        o_ref = refs[-1]

        @pl.when(pl.program_id(0) == 0)
        def _():
            o_ref[...] = jnp.zeros_like(o_ref)

        off = 0
        for r, w in zip(refs[:-1], widths):
            o_ref[:, off:off + w] += jnp.sum(r[...].astype(f32), axis=0, keepdims=True)
            off += w

    return pl.pallas_call(
        body, name=name, grid=(t // tt,),
        in_specs=[_rows(tt, w) for w in widths],
        out_specs=_const((1, sum(widths))),
        out_shape=jax.ShapeDtypeStruct((1, sum(widths)), f32),
        compiler_params=_cparams(1),
    )(*parts)


def _local_step(x, p, tgt, w_in_b, b_in, logits, conv_w, conv_b, hg_gn, ml_gn, w_out_b, ln1_g, ln1_b,
                wg_b, wu_b, wd_b, ln2_g, ln2_b, w_pp_b, w_pg_b, b_pg):
    pad_w = U_HG + U_ML - PROJ_W
    w_hg = w_in_b[:, :U_HG]
    w_ml = jnp.pad(w_in_b[:, U_HG:], ((0, 0), (0, pad_w)))
    bb_hg = b_in[:, :U_HG]
    bb_ml = jnp.pad(b_in[:, U_HG:], ((0, 0), (0, pad_w)))

    u_hg, u_ml = _inproj(x, w_hg, w_ml, bb_hg, bb_ml)
    og_hg, sst = _hgrn2_fwd(u_hg, logits, hg_gn)
    pre, qkc = _conv_fwd(u_ml, conv_w, conv_b)
    og_ml, cst, nst, mst = _mlstm_fwd(qkc, u_ml, ml_gn)
    x1, xh1, rs1 = _outproj_ln1(og_hg, og_ml, x, w_out_b, ln1_g, ln1_b)
    hgp, up, act = _ffn_up(x1, wg_b, wu_b)
    x2, xh2, rs2 = _ffn_down_ln2(act, x1, wd_b, ln2_g, ln2_b)
    dr2, de, dz, loss, d_bpg, d_ln2g, d_ln2b = _head_loss_bwd(x2, xh2, rs2, p, tgt, w_pg_b, b_pg, w_pp_b, ln2_g)
    dr1, dhg, dup, d_ln1g, d_ln1b = _ffn_bwd(dr2, hgp, up, xh1, rs1, wd_b, wg_b, wu_b, ln1_g)
    dog_hg, dog_ml = _outproj_bwd(dr1, w_out_b)
    du_hg, d_logits, d_hg_gn = _hgrn2_bwd(u_hg, logits, hg_gn, sst, dog_hg)
    dqkc, dmv, dmo, dgt, d_ml_gn = _mlstm_bwd(qkc, u_ml, ml_gn, cst, nst, mst, dog_ml)
    dqk, d_conv_w, d_conv_b = _conv_bwd(u_ml, conv_w, pre, dqkc)
    grad_x, du_ml = _inproj_bwd(dr1, du_hg, dqk, dmv, dmo, dgt, w_hg, w_ml)

    dw_hg = _wgrad(x, du_hg, "wgrad_in_hg", tn=1024)
    dw_ml = _wgrad(x, du_ml, "wgrad_in_ml")
    d_w_in = jnp.concatenate([dw_hg, dw_ml[:, :PROJ_W - U_HG]], axis=1)
    d_b_in = _colsum([du_hg, du_ml], "colsum_du")[:, :PROJ_W]
    d_wo_a = _wgrad(og_hg, dr1, "wgrad_out_hg")
    d_wo_b = _wgrad(og_ml, dr1, "wgrad_out_ml")
    d_w_out = jnp.concatenate([d_wo_a, d_wo_b], axis=0)
    d_wg = _wgrad(x1, dhg, "wgrad_ffn_gate", tn=D_FF // 2)
    d_wu = _wgrad(x1, dup, "wgrad_ffn_up", tn=D_FF // 2)
    d_wd = _wgrad(act, dr2, "wgrad_ffn_down", tk=D_FF // 2)
    d_wpp = _wgrad(p, de, "wgrad_ple_proj")
    d_wpg = _wgrad(x2, dz, "wgrad_ple_gate")

    grads = dict(w_in=d_w_in, b_in=d_b_in, hg_lb_logits=d_logits, ml_conv_w=d_conv_w, ml_conv_b=d_conv_b,
                 hg_norm_g=d_hg_gn, ml_norm_g=d_ml_gn, w_out=d_w_out, ln1_g=d_ln1g, ln1_b=d_ln1b,
                 w_ffn_gate=d_wg, w_ffn_up=d_wu, w_ffn_down=d_wd, ln2_g=d_ln2g, ln2_b=d_ln2b,
                 ple_w_proj=d_wpp, ple_w_gate=d_wpg, ple_b_gate=d_bpg)
    return loss, grad_x, grads


_ANY = pl.BlockSpec(memory_space=pl.ANY)
_MESH = pl.DeviceIdType.MESH


def _my_place():
    return lax.axis_index("x"), lax.axis_index("y"), lax.axis_index("c")


def _other_chips(x, y):
    return [(1 - x, y), (x, 1 - y), (1 - x, 1 - y)]


def _allgather_weights(shards, taps, name):
    n = len(shards)
    halves = [s.shape[0] // 2 for s in shards]

    def body(*refs):
        ins, tap_in = refs[:n], refs[n]
        outs, tap_out = refs[n + 1:2 * n + 1], refs[2 * n + 1]
        send_sems, recv_sems, local_sems = refs[2 * n + 2:]
        x, y, c = _my_place()
        me = 2 * x + y
        sibling = (x, y, 1 - c)
        chips = _other_chips(x, y)

        def ici(a, j, block_chip):
            px, py = chips[j]
            src = ins[a].at[pl.ds(pl.multiple_of(c * halves[a], 16), halves[a])] if block_chip is None else outs[a].at[block_chip, c]
            dst = outs[a].at[me if block_chip is None else block_chip, c]
            return pltpu.make_async_remote_copy(src_ref=src, dst_ref=dst, send_sem=send_sems.at[6 * a + j], recv_sem=recv_sems.at[6 * a + j],
                                                device_id=(px, py, c), device_id_type=_MESH)

        def d2d(a, j, half):
            px, py = chips[j]
            blk = outs[a].at[2 * px + py, half]
            return pltpu.make_async_remote_copy(src_ref=blk, dst_ref=blk, send_sem=send_sems.at[6 * a + 3 + j], recv_sem=recv_sems.at[6 * a + 3 + j],
                                                device_id=sibling, device_id_type=_MESH)

        local = []
        for a in range(n):
            for h in range(2):
                cp = pltpu.make_async_copy(ins[a].at[pl.ds(h * halves[a], halves[a])], outs[a].at[me, h], local_sems.at[2 * a + h])
                cp.start()
                local.append(cp)
            for j in range(3):
                ici(a, j, None).start()
        tap_local = pltpu.make_async_copy(tap_in, tap_out.at[me], local_sems.at[2 * n])
        tap_local.start()
        tap_copies = []
        for j, (px, py) in enumerate(chips):
            cp = pltpu.make_async_remote_copy(src_ref=tap_in, dst_ref=tap_out.at[me], send_sem=send_sems.at[6 * n + j], recv_sem=recv_sems.at[6 * n + j],
                                              device_id=(px, py, c), device_id_type=_MESH)
            cp.start()
            tap_copies.append(cp)
        for a in range(n):
            for j, (px, py) in enumerate(chips):
                ici(a, j, 2 * px + py).wait_recv()
                d2d(a, j, c).start()
        for a in range(n):
            for j in range(3):
                d2d(a, j, 1 - c).wait_recv()
        for a in range(n):
            for j in range(3):
                ici(a, j, None).wait_send()
                d2d(a, j, c).wait_send()
        for j, (px, py) in enumerate(chips):
            pltpu.make_async_remote_copy(src_ref=tap_in, dst_ref=tap_out.at[2 * px + py], send_sem=send_sems.at[6 * n + j], recv_sem=recv_sems.at[6 * n + j],
                                         device_id=(px, py, c), device_id_type=_MESH).wait()
        for cp in local:
            cp.wait()
        tap_local.wait()

    res = pl.pallas_call(
        body, name=name,
        in_specs=[_ANY] * (n + 1), out_specs=[_ANY] * (n + 1),
        out_shape=[jax.ShapeDtypeStruct((4, 2, s.shape[0] // 2, s.shape[1]), s.dtype) for s in shards]
        + [jax.ShapeDtypeStruct((4,) + taps.shape, taps.dtype)],
        scratch_shapes=[pltpu.SemaphoreType.DMA((6 * n + 3,)), pltpu.SemaphoreType.DMA((6 * n + 3,)), pltpu.SemaphoreType.DMA((2 * n + 1,))],
    )(*shards, taps)
    return [w.reshape((4,) + s.shape) for w, s in zip(res[:n], shards)], res[n]


def _swap_halves(pieces, name):
    n = len(pieces)
    halves = [p.shape[1] // 2 for p in pieces]

    def body(*refs):
        ins, own, other = refs[:n], refs[n:2 * n], refs[2 * n:3 * n]
        send_sems, recv_sems, local_sems = refs[3 * n:]
        x, y, c = _my_place()

        def half_of(a, which):
            return ins[a].at[pl.ds(0, 4), pl.ds(pl.multiple_of(which * halves[a], 16), halves[a])]

        def to_sibling(a):
            return pltpu.make_async_remote_copy(src_ref=half_of(a, 1 - c), dst_ref=other[a], send_sem=send_sems.at[a], recv_sem=recv_sems.at[a],
                                                device_id=(x, y, 1 - c), device_id_type=_MESH)

        local = []
        for a in range(n):
            cp = pltpu.make_async_copy(half_of(a, c), own[a], local_sems.at[a])
            cp.start()
            local.append(cp)
            to_sibling(a).start()
        for a in range(n):
            to_sibling(a).wait()
            local[a].wait()

    shapes = [jax.ShapeDtypeStruct((4, p.shape[1] // 2, p.shape[2]), p.dtype) for p in pieces]
    res = pl.pallas_call(
        body, name=name,
        in_specs=[_ANY] * n, out_specs=[_ANY] * (2 * n), out_shape=shapes + shapes,
        scratch_shapes=[pltpu.SemaphoreType.DMA((n,)), pltpu.SemaphoreType.DMA((n,)), pltpu.SemaphoreType.DMA((n,))],
    )(*pieces)
    return res[:n], res[n:]


def _add_cast(a, b, name):
    s, r, c = a.shape
    tr = _row_tile(r, c)

    def body(a_ref, b_ref, o_ref):
        o_ref[...] = (a_ref[...] + b_ref[...]).astype(bf16)

    blk = pl.BlockSpec((1, tr, c), lambda i, j: (i, j, 0))
    return pl.pallas_call(
        body, name=name, grid=(s, r // tr), in_specs=[blk, blk], out_specs=blk,
        out_shape=jax.ShapeDtypeStruct(a.shape, bf16),
        compiler_params=_cparams(2, arbitrary=False),
    )(a, b)


def _scatter_chips(pieces, name):
    n = len(pieces)

    def body(*refs):
        ins, outs = refs[:n], refs[n:2 * n]
        send_sems, recv_sems, local_sems = refs[2 * n:]
        x, y, c = _my_place()
        me = 2 * x + y
        chips = _other_chips(x, y)
        local = []
        for a in range(n):
            cp = pltpu.make_async_copy(ins[a].at[me], outs[a].at[me], local_sems.at[a])
            cp.start()
            local.append(cp)
            for j, (px, py) in enumerate(chips):
                pltpu.make_async_remote_copy(src_ref=ins[a].at[2 * px + py], dst_ref=outs[a].at[me], send_sem=send_sems.at[3 * a + j],
                                             recv_sem=recv_sems.at[3 * a + j], device_id=(px, py, c), device_id_type=_MESH).start()
        for a in range(n):
            for j, (px, py) in enumerate(chips):
                pltpu.make_async_remote_copy(src_ref=ins[a].at[2 * px + py], dst_ref=outs[a].at[2 * px + py], send_sem=send_sems.at[3 * a + j],
                                             recv_sem=recv_sems.at[3 * a + j], device_id=(px, py, c), device_id_type=_MESH).wait()
            local[a].wait()

    return pl.pallas_call(
        body, name=name,
        in_specs=[_ANY] * n, out_specs=[_ANY] * n,
        out_shape=[jax.ShapeDtypeStruct(s.shape, s.dtype) for s in pieces],
        scratch_shapes=[pltpu.SemaphoreType.DMA((3 * n,)), pltpu.SemaphoreType.DMA((3 * n,)), pltpu.SemaphoreType.DMA((n,))],
    )(*pieces)


def _swap_cores(blocks, name):
    n = len(blocks)

    def body(*refs):
        ins, outs = refs[:n], refs[n:2 * n]
        send_sems, recv_sems, local_sems = refs[2 * n:]
        x, y, c = _my_place()
        local = []
        for a in range(n):
            cp = pltpu.make_async_copy(ins[a], outs[a].at[c], local_sems.at[a])
            cp.start()
            local.append(cp)
            pltpu.make_async_remote_copy(src_ref=ins[a], dst_ref=outs[a].at[c], send_sem=send_sems.at[a], recv_sem=recv_sems.at[a],
                                         device_id=(x, y, 1 - c), device_id_type=_MESH).start()
        for a in range(n):
            pltpu.make_async_remote_copy(src_ref=ins[a], dst_ref=outs[a].at[1 - c], send_sem=send_sems.at[a], recv_sem=recv_sems.at[a],
                                         device_id=(x, y, 1 - c), device_id_type=_MESH).wait()
            local[a].wait()

    return pl.pallas_call(
        body, name=name,
        in_specs=[_ANY] * n, out_specs=[_ANY] * n,
        out_shape=[jax.ShapeDtypeStruct((2,) + s.shape, s.dtype) for s in blocks],
        scratch_shapes=[pltpu.SemaphoreType.DMA((n,)), pltpu.SemaphoreType.DMA((n,)), pltpu.SemaphoreType.DMA((n,))],
    )(*blocks)


def _gather_all(block, name):
    def body(in_ref, out_ref, send_sems, recv_sems, local_sem):
        x, y, c = _my_place()
        me = 4 * x + 2 * y + c
        cp = pltpu.make_async_copy(in_ref, out_ref.at[me], local_sem)
        cp.start()
        peers = []
        for dx in range(2):
            for dy in range(2):
                for dc in range(2):
                    if dx or dy or dc:
                        peers.append((1 - x if dx else x, 1 - y if dy else y, 1 - c if dc else c))
        for j, pr in enumerate(peers):
            pltpu.make_async_remote_copy(src_ref=in_ref, dst_ref=out_ref.at[me], send_sem=send_sems.at[j], recv_sem=recv_sems.at[j],
                                         device_id=pr, device_id_type=_MESH).start()
        for j, (px, py, pc) in enumerate(peers):
            pltpu.make_async_remote_copy(src_ref=in_ref, dst_ref=out_ref.at[4 * px + 2 * py + pc], send_sem=send_sems.at[j], recv_sem=recv_sems.at[j],
                                         device_id=(px, py, pc), device_id_type=_MESH).wait()
        cp.wait()

    return pl.pallas_call(
        body, name=name,
        in_specs=[_ANY], out_specs=_ANY,
        out_shape=jax.ShapeDtypeStruct((8,) + block.shape, block.dtype),
        scratch_shapes=[pltpu.SemaphoreType.DMA((7,)), pltpu.SemaphoreType.DMA((7,)), pltpu.SemaphoreType.DMA],
    )(block)


def _row_tile(r, c):
    best = r
    for cand in range(16, r + 1, 16):
        if r % cand == 0 and cand * c * 4 <= (1 << 20):
            best = cand
    return best if best * c * 4 <= (4 << 20) else r


def _sum_slots(parts, name):
    n, r, c = parts.shape
    tr = _row_tile(r, c)

    def body(p_ref, o_ref):
        acc = p_ref[0].astype(f32)
        for s in range(1, n):
            acc = acc + p_ref[s].astype(f32)
        o_ref[...] = acc

    return pl.pallas_call(
        body, name=name, grid=(r // tr,),
        in_specs=[pl.BlockSpec((n, tr, c), lambda i: (0, i, 0))],
        out_specs=pl.BlockSpec((tr, c), lambda i: (i, 0)),
        out_shape=jax.ShapeDtypeStruct((r, c), f32),
        compiler_params=_cparams(1, arbitrary=False),
    )(parts)


def _adamw(parts, w, m, v, name):
    n, r, c = parts.shape
    tr = _row_tile(r, c)

    def body(p_ref, w_ref, m_ref, v_ref, g_ref, d_ref, nm_ref, nv_ref):
        g = p_ref[0]
        for s in range(1, n):
            g = g + p_ref[s]
        nm = B1 * m_ref[...] + (1.0 - B1) * g
        nv = B2 * v_ref[...] + (1.0 - B2) * (g * g)
        m_hat = nm / (1.0 - B1 ** STEP)
        v_hat = nv / (1.0 - B2 ** STEP)
        g_ref[...] = g
        nm_ref[...] = nm
        nv_ref[...] = nv
        d_ref[...] = -LR * (m_hat / (jnp.sqrt(v_hat) + EPS_ADAM) + WD * w_ref[...])

    blk = pl.BlockSpec((tr, c), lambda i: (i, 0))
    return pl.pallas_call(
        body, name=name, grid=(r // tr,),
        in_specs=[pl.BlockSpec((n, tr, c), lambda i: (0, i, 0)), blk, blk, blk],
        out_specs=[blk] * 4,
        out_shape=[jax.ShapeDtypeStruct((r, c), f32)] * 4,
        compiler_params=_cparams(1, arbitrary=False),
    )(parts, w, m, v)


_BIG = ["w_in", "w_out", "w_ffn_gate", "w_ffn_up", "w_ffn_down", "ple_w_proj", "ple_w_gate"]
_COL_SPLIT = {"w_in", "w_ffn_gate", "w_ffn_up", "ple_w_proj"}
_SMALL = ["b_in", "hg_lb_logits", "ml_conv_w", "ml_conv_b", "hg_norm_g", "ml_norm_g", "ln1_g", "ln1_b", "ln2_g", "ln2_b", "ple_b_gate"]
_ORDER = ["w_in", "b_in", "hg_lb_logits", "ml_conv_w", "ml_conv_b", "hg_norm_g", "ml_norm_g", "w_out", "ln1_g", "ln1_b",
          "w_ffn_gate", "w_ffn_up", "w_ffn_down", "ln2_g", "ln2_b", "ple_w_proj", "ple_w_gate", "ple_b_gate"]
_PACK_ROWS, _PACK_COLS = 16, 1024


def _pack(arrays):
    flat = jnp.concatenate([a.reshape(-1) for a in arrays])
    return jnp.pad(flat, (0, _PACK_ROWS * _PACK_COLS - flat.shape[0])).reshape(_PACK_ROWS, _PACK_COLS)


def _unpack(pack, shapes):
    flat = pack.reshape(-1)
    out, off = [], 0
    for s in shapes:
        size = 1
        for d in s:
            size *= d
        out.append(flat[off:off + size].reshape(s))
        off += size
    return out


def _to_chip_major(g, col_split):
    if col_split:
        k, n = g.shape
        return g.reshape(k, 4, n // 4).transpose(1, 0, 2)
    k, n = g.shape
    return g.reshape(4, k // 4, n)


def _from_chip_major(a, col_split):
    if col_split:
        return a.transpose(1, 0, 2).reshape(a.shape[1], 4 * a.shape[2])
    return a.reshape(4 * a.shape[1], a.shape[2])


def kernel(x, p, w_in, b_in, hg_lb_logits, ml_conv_w, ml_conv_b, hg_norm_g, ml_norm_g, w_out, ln1_g, ln1_b, w_ffn_gate, w_ffn_up, w_ffn_down, ln2_g, ln2_b, ple_w_proj, ple_w_gate, ple_b_gate, loss_target, m_w_in, m_b_in, m_hg_lb_logits, m_ml_conv_w, m_ml_conv_b, m_hg_norm_g, m_ml_norm_g, m_w_out, m_ln1_g, m_ln1_b, m_w_ffn_gate, m_w_ffn_up, m_w_ffn_down, m_ln2_g, m_ln2_b, m_ple_w_proj, m_ple_w_gate, m_ple_b_gate, v_w_in, v_b_in, v_hg_lb_logits, v_ml_conv_w, v_ml_conv_b, v_hg_norm_g, v_ml_norm_g, v_w_out, v_ln1_g, v_ln1_b, v_w_ffn_gate, v_w_ffn_up, v_w_ffn_down, v_ln2_g, v_ln2_b, v_ple_w_proj, v_ple_w_gate, v_ple_b_gate):
    args = dict(locals())
    wts = {k: args[k] for k in _ORDER}
    mom = {k: args["m_" + k] for k in _ORDER}
    var = {k: args["v_" + k] for k in _ORDER}
    two_d = lambda a: a.reshape(a.shape[-2], a.shape[-1])

    shards = [two_d(wts[k]).astype(bf16) for k in _BIG]
    gathered, taps = _allgather_weights(shards, two_d(ml_conv_w), "allgather_weights")
    full = {k: _from_chip_major(g, k in _COL_SPLIT) for k, g in zip(_BIG, gathered)}
    conv_w_full = _from_chip_major(taps, True)

    loss, grad_x, grads = _local_step(
        x[0], p[0, 0], loss_target[0], full["w_in"], b_in, hg_lb_logits, conv_w_full, ml_conv_b, hg_norm_g, ml_norm_g,
        full["w_out"], ln1_g, ln1_b, full["w_ffn_gate"], full["w_ffn_up"], full["w_ffn_down"], ln2_g, ln2_b,
        full["ple_w_proj"], full["ple_w_gate"], ple_b_gate)

    pieces = [_to_chip_major(grads[k], k in _COL_SPLIT) for k in _BIG]
    own, other = _swap_halves(pieces, "swap_halves")
    core_sums = [_add_cast(a, b, "add_cores_" + k) for k, a, b in zip(_BIG, own, other)]
    received = _scatter_chips(core_sums, "scatter_grads")
    chip_sums = [_sum_slots(r, "sum_chips_" + k) for k, r in zip(_BIG, received)]
    both = _swap_cores(chip_sums, "swap_cores")
    out_g, out_d, out_m, out_v = {}, {}, {}, {}
    for k, parts in zip(_BIG, both):
        whole = parts.reshape(1, 2 * parts.shape[1], parts.shape[2])
        g, d, nm, nv = _adamw(whole, two_d(wts[k]), two_d(mom[k]), two_d(var[k]), "adamw_" + k)
        shp = wts[k].shape
        out_g[k], out_d[k], out_m[k], out_v[k] = g.reshape(shp), d.reshape(shp), nm.reshape(shp), nv.reshape(shp)

    small_shapes = [(1, PROJ_W), (2, MIX_W), (CONV_K, MIX_W)] + [(1, MIX_W)] * 3 + [(1, D_MODEL)] * 5 + [(1, 1)]
    contrib = _pack([grads[k] for k in _SMALL] + [loss])
    summed = _sum_slots(_gather_all(contrib, "gather_small"), "sum_small")
    small = _unpack(summed, small_shapes)
    loss_total = small[-1].reshape(())
    gsm = dict(zip(_SMALL, small[:-1]))
    place = 2 * lax.axis_index("x") + lax.axis_index("y")
    conv_cols = ml_conv_w.shape[-1]
    gsm["ml_conv_w"] = lax.dynamic_slice(gsm["ml_conv_w"], (0, place * conv_cols), (CONV_K, conv_cols))
    own_shapes = [wts[k].shape for k in _SMALL]
    g_pack = _pack([gsm[k] for k in _SMALL])
    res = _adamw(g_pack[None], _pack([wts[k] for k in _SMALL]), _pack([mom[k] for k in _SMALL]), _pack([var[k] for k in _SMALL]), "adamw_small")
    for dst, pack in zip((out_g, out_d, out_m, out_v), res):
        for k, a in zip(_SMALL, _unpack(pack, own_shapes)):
            dst[k] = a

    outs = [loss_total, grad_x[None]]
    for group in (out_g, out_d, out_m, out_v):
        outs += [group[k] for k in _ORDER]
    return tuple(outs)
```

```python
import functools

import jax
import jax.numpy as jnp
from jax import lax
from jax.experimental import pallas as pl
from jax.experimental.pallas import tpu as pltpu

f32 = jnp.float32
bf16 = jnp.bfloat16
HI = lax.Precision.HIGHEST

D_MODEL = 1024
HEADS = 4
HEAD_W = 128
MIX_W = HEADS * HEAD_W
ML_DQK = 64
PROJ_W = 3592
U_HG = 4 * MIX_W
U_ML = 3 * MIX_W + 128
D_FF = 2816
PLE = 256
CHUNK = 64
SUB = 16
EXP_CAP = 80.0
CONV_K = 4
HALO = 8
ALPHA = float(2.0 ** 0.25)
LN_EPS = 1e-5
RMS_EPS = 1e-6
NEG = -1e30
LR, B1, B2, EPS_ADAM, WD, STEP = 0.001, 0.9, 0.999, 1e-08, 0.01, 10
VMEM_LIMIT = 56 * 1024 * 1024


def _cparams(n_axes, arbitrary=True):
    sem = ("arbitrary",) * n_axes if arbitrary else ("parallel",) * n_axes
    return pltpu.CompilerParams(dimension_semantics=sem, vmem_limit_bytes=VMEM_LIMIT)


ACT = bf16


def _mx(a):
    return a.astype(ACT)


def _bdot(a, b):
    return jnp.dot(_mx(a), _mx(b), preferred_element_type=f32)


def _bdot_nt(a, b):
    return lax.dot_general(_mx(a), _mx(b), (((1,), (1,)), ((), ())), preferred_element_type=f32)


def _bdot_tn(a, b):
    return lax.dot_general(_mx(a), _mx(b), (((0,), (0,)), ((), ())), preferred_element_type=f32)


def _hdot(a, b):
    return jnp.dot(a, b, precision=HI, preferred_element_type=f32)


def _hdot_nt(a, b):
    return lax.dot_general(a, b, (((1,), (1,)), ((), ())), precision=HI, preferred_element_type=f32)


def _hdot_tn(a, b):
    return lax.dot_general(a, b, (((0,), (0,)), ((), ())), precision=HI, preferred_element_type=f32)


def _sigmoid(x):
    return 1.0 / (1.0 + jnp.exp(-x))


def _log_sigmoid(x):
    return jnp.minimum(x, 0.0) - jnp.log(1.0 + jnp.exp(-jnp.abs(x)))


def _tri(n, upper=False):
    r = lax.broadcasted_iota(jnp.int32, (n, n), 0)
    c = lax.broadcasted_iota(jnp.int32, (n, n), 1)
    return (c >= r) if upper else (c <= r)


def _rows(tm, n, col=0):
    return pl.BlockSpec((tm, n), lambda i, _c=col: (i, _c))


def _rows_rev(tm, n, nb, col=0):
    return pl.BlockSpec((tm, n), lambda i, _c=col, _nb=nb: (_nb - 1 - i, _c))


def _const(shape):
    return pl.BlockSpec(shape, lambda i, _n=len(shape): (0,) * _n)


def _tile(t, want):
    return want if t % want == 0 else t


def _inproj(x, w_hg, w_ml, b_hg, b_ml):
    t = x.shape[0]
    tm = _tile(t, 256)

    def body(x_ref, whg_ref, wml_ref, bhg_ref, bml_ref, uhg_ref, uml_ref):
        xv = x_ref[...]
        uhg_ref[...] = _bdot(xv, whg_ref[...]) + bhg_ref[...]
        uml_ref[...] = _bdot(xv, wml_ref[...]) + bml_ref[...]

    return pl.pallas_call(
        body, name="inproj", grid=(t // tm,),
        in_specs=[_rows(tm, D_MODEL), _const((D_MODEL, U_HG)), _const((D_MODEL, U_ML)), _const((1, U_HG)), _const((1, U_ML))],
        out_specs=[_rows(tm, U_HG), _rows(tm, U_ML)],
        out_shape=[jax.ShapeDtypeStruct((t, U_HG), f32), jax.ShapeDtypeStruct((t, U_ML), f32)],
        compiler_params=_cparams(1, arbitrary=False),
    )(x, w_hg, w_ml, b_hg, b_ml)


def _hg_gates(hq, hf, lb, tri):
    s = _sigmoid(hf)
    om = 1.0 - lb
    f = lb + om * s
    g = jnp.log(f)
    k = om * (1.0 - s)
    sq = _sigmoid(hq)
    q = hq * sq
    b = _hdot(tri, g)
    return q, sq, s, f, k, b


def _hg_scores(q, k, b, tril_mask):
    qts, kts, eqs, eks, rows = [], [], [], [], []
    for i in range(CHUNK // SUB):
        lo = i * SUB
        ref = jnp.zeros_like(b[0:1]) if i == 0 else b[lo - 1:lo]
        eq = jnp.exp(b[lo:lo + SUB] - ref)
        ek = jnp.exp(jnp.minimum(ref - b, EXP_CAP))
        qt = q[lo:lo + SUB] * eq
        kt = k * ek
        rows.append(_hdot_nt(qt, kt))
        qts.append(qt); kts.append(kt); eqs.append(eq); eks.append(ek)
    a = jnp.where(tril_mask, jnp.concatenate(rows, axis=0), 0.0)
    return a, qts, kts, eqs, eks


def _head_rms(o, gn):
    rstd = lax.rsqrt(jnp.mean(o * o, axis=-1, keepdims=True) + RMS_EPS)
    oh = o * rstd
    return oh, rstd, oh * gn


def _lower_bound(logit_ref):
    lg = logit_ref[...]
    return _sigmoid(lg[0:1] - lg[1:2])


def _hgrn2_fwd(u_hg, logits, gn):
    t = u_hg.shape[0]
    tb = _tile(t, 256)
    nc_blk = tb // CHUNK

    def body(u_ref, lg_ref, gn_ref, og_ref, sst_ref, st_ref):
        @pl.when(pl.program_id(0) == 0)
        def _():
            st_ref[...] = jnp.zeros_like(st_ref)

        lb_all = _lower_bound(lg_ref)
        tril_mask = _tri(CHUNK)
        tri = tril_mask.astype(f32)

        def chunk(c, carry):
            r0 = pl.multiple_of(c * CHUNK, CHUNK)
            rows = pl.ds(r0, CHUNK)
            for h in range(HEADS):
                cs = slice(h * HEAD_W, (h + 1) * HEAD_W)
                hq = u_ref[rows, h * HEAD_W:(h + 1) * HEAD_W]
                hf = u_ref[rows, MIX_W + h * HEAD_W:MIX_W + (h + 1) * HEAD_W]
                hv = u_ref[rows, 2 * MIX_W + h * HEAD_W:2 * MIX_W + (h + 1) * HEAD_W]
                hgate = u_ref[rows, 3 * MIX_W + h * HEAD_W:3 * MIX_W + (h + 1) * HEAD_W]
                q, _, _, _, k, b = _hg_gates(hq, hf, lb_all[:, cs], tri)
                a, _, _, _, _ = _hg_scores(q, k, b, tril_mask)
                st = st_ref[h]
                sst_ref[c, h] = st
                bl = b[CHUNK - 1:CHUNK]
                o = _bdot(a, hv) + _bdot_nt(q * jnp.exp(b), st)
                st_ref[h] = st * jnp.exp(bl) + _bdot_tn(hv, k * jnp.exp(bl - b))
                _, _, y = _head_rms(o, gn_ref[:, cs])
                og_ref[rows, cs] = (y * (hgate * _sigmoid(hgate))).astype(ACT)
            return carry

        lax.fori_loop(0, nc_blk, chunk, 0)

    return pl.pallas_call(
        body, name="hgrn2_fwd", grid=(t // tb,),
        in_specs=[_rows(tb, U_HG), _const((2, MIX_W)), _const((1, MIX_W))],
        out_specs=[_rows(tb, MIX_W), pl.BlockSpec((nc_blk, HEADS, HEAD_W, HEAD_W), lambda i: (i, 0, 0, 0))],
        out_shape=[jax.ShapeDtypeStruct((t, MIX_W), ACT), jax.ShapeDtypeStruct((t // CHUNK, HEADS, HEAD_W, HEAD_W), f32)],
        scratch_shapes=[pltpu.VMEM((HEADS, HEAD_W, HEAD_W), f32)],
        compiler_params=_cparams(1),
    )(u_hg, logits, gn)


def _hgrn2_bwd(u_hg, logits, gn, sst, dog):
    t = u_hg.shape[0]
    tb = _tile(t, 256)
    nb = t // tb
    nc_blk = tb // CHUNK

    def body(u_ref, lg_ref, gn_ref, sst_ref, dog_ref, du_ref, dlg_ref, dgn_ref, dst_ref):
        @pl.when(pl.program_id(0) == 0)
        def _():
            dst_ref[...] = jnp.zeros_like(dst_ref)
            dlg_ref[...] = jnp.zeros_like(dlg_ref)
            dgn_ref[...] = jnp.zeros_like(dgn_ref)

        lb_all = _lower_bound(lg_ref)
        tril_mask = _tri(CHUNK)
        tri = tril_mask.astype(f32)
        triu = _tri(CHUNK, upper=True).astype(f32)

        def chunk(j, carry):
            c = nc_blk - 1 - j
            r0 = pl.multiple_of(c * CHUNK, CHUNK)
            rows = pl.ds(r0, CHUNK)
            for h in range(HEADS):
                cs = slice(h * HEAD_W, (h + 1) * HEAD_W)
                hq = u_ref[rows, h * HEAD_W:(h + 1) * HEAD_W]
                hf = u_ref[rows, MIX_W + h * HEAD_W:MIX_W + (h + 1) * HEAD_W]
                hv = u_ref[rows, 2 * MIX_W + h * HEAD_W:2 * MIX_W + (h + 1) * HEAD_W]
                hgate = u_ref[rows, 3 * MIX_W + h * HEAD_W:3 * MIX_W + (h + 1) * HEAD_W]
                lb = lb_all[:, cs]
                gnh = gn_ref[:, cs]
                q, sq, s, f, k, b = _hg_gates(hq, hf, lb, tri)
                a, qts, kts, eqs, eks = _hg_scores(q, k, b, tril_mask)
                st = sst_ref[c, h]
                dst = dst_ref[h]
                bl = b[CHUNK - 1:CHUNK]
                eb = jnp.exp(b)
                qh = q * eb
                ekl = jnp.exp(bl - b)
                kh = k * ekl
                o = _bdot(a, hv) + _bdot_nt(qh, st)
                oh, rstd, y = _head_rms(o, gnh)
                sg = _sigmoid(hgate)
                dogh = dog_ref[rows, cs]
                dy = dogh * (hgate * sg)
                du_ref[rows, 3 * MIX_W + h * HEAD_W:3 * MIX_W + (h + 1) * HEAD_W] = dogh * y * (sg * (1.0 + hgate * (1.0 - sg)))
                dgn_ref[:, cs] += jnp.sum(dy * oh, axis=0, keepdims=True)
                doh = dy * gnh
                do = rstd * (doh - oh * jnp.mean(doh * oh, axis=-1, keepdims=True))
                da = jnp.where(tril_mask, _bdot_nt(do, hv), 0.0)
                dv = _bdot_tn(a, do) + _bdot_nt(kh, dst)
                dq = _bdot(do, st) * eb
                dk = _bdot(hv, dst) * ekl
                d_last = jnp.sum(k * dk, axis=0, keepdims=True) + jnp.exp(bl) * jnp.sum(dst * st, axis=0, keepdims=True)
                dqs = []
                for i in range(CHUNK // SUB):
                    da_i = da[i * SUB:(i + 1) * SUB]
                    dqs.append(_hdot(da_i, kts[i]) * eqs[i])
                    dk = dk + _hdot_tn(da_i, qts[i]) * eks[i]
                dq = dq + jnp.concatenate(dqs, axis=0)
                dst_ref[h] = dst * jnp.exp(bl) + _bdot_tn(do, qh)
                dg = _hdot(triu, q * dq - k * dk) + d_last
                df = dg / f
                dfk = df - dk
                du_ref[rows, h * HEAD_W:(h + 1) * HEAD_W] = dq * (sq * (1.0 + hq * (1.0 - sq)))
                du_ref[rows, MIX_W + h * HEAD_W:MIX_W + (h + 1) * HEAD_W] = (1.0 - lb) * dfk * s * (1.0 - s)
                du_ref[rows, 2 * MIX_W + h * HEAD_W:2 * MIX_W + (h + 1) * HEAD_W] = dv
                dlb = jnp.sum((1.0 - s) * dfk, axis=0, keepdims=True) * (lb * (1.0 - lb))
                dlg_ref[0:1, cs] += dlb
                dlg_ref[1:2, cs] -= dlb
            return carry

        lax.fori_loop(0, nc_blk, chunk, 0)

    return pl.pallas_call(
        body, name="hgrn2_bwd", grid=(nb,),
        in_specs=[_rows_rev(tb, U_HG, nb), _const((2, MIX_W)), _const((1, MIX_W)),
                  pl.BlockSpec((nc_blk, HEADS, HEAD_W, HEAD_W), lambda i: (nb - 1 - i, 0, 0, 0)), _rows_rev(tb, MIX_W, nb)],
        out_specs=[_rows_rev(tb, U_HG, nb), _const((2, MIX_W)), _const((1, MIX_W))],
        out_shape=[jax.ShapeDtypeStruct((t, U_HG), f32), jax.ShapeDtypeStruct((2, MIX_W), f32), jax.ShapeDtypeStruct((1, MIX_W), f32)],
        scratch_shapes=[pltpu.VMEM((HEADS, HEAD_W, HEAD_W), f32)],
        compiler_params=_cparams(1),
    )(u_hg, logits, gn, sst, dog)


def _conv_fwd(u_ml, w, b):
    t = u_ml.shape[0]
    tm = _tile(t, 512)

    def body(x_ref, w_ref, b_ref, pre_ref, act_ref, xbuf):
        @pl.when(pl.program_id(0) == 0)
        def _():
            xbuf[...] = jnp.zeros_like(xbuf)

        xbuf[0:HALO, :] = xbuf[tm:tm + HALO, :]
        xbuf[HALO:HALO + tm, :] = x_ref[...]
        pre = b_ref[...] + jnp.zeros((tm, MIX_W), f32)
        for kk in range(CONV_K):
            off = HALO - (CONV_K - 1) + kk
            pre = pre + w_ref[kk:kk + 1, :] * xbuf[off:off + tm, :]
        pre_ref[...] = pre
        act_ref[...] = pre * _sigmoid(pre)

    return pl.pallas_call(
        body, name="conv_fwd", grid=(t // tm,),
        in_specs=[_rows(tm, MIX_W), _const((CONV_K, MIX_W)), _const((1, MIX_W))],
        out_specs=[_rows(tm, MIX_W), _rows(tm, MIX_W)],
        out_shape=[jax.ShapeDtypeStruct((t, MIX_W), f32)] * 2,
        scratch_shapes=[pltpu.VMEM((tm + HALO, MIX_W), f32)],
        compiler_params=_cparams(1),
    )(u_ml, w, b)


def _conv_bwd(u_ml, w, pre, dact):
    t = u_ml.shape[0]
    tm = _tile(t, 512)
    nb = t // tm
    hb = tm // HALO

    def body(x_ref, halo_ref, w_ref, pre_ref, dact_ref, dx_ref, dw_ref, db_ref, dbuf, xbuf):
        i = pl.program_id(0)

        @pl.when(i == 0)
        def _():
            dbuf[...] = jnp.zeros_like(dbuf)
            dw_ref[...] = jnp.zeros_like(dw_ref)
            db_ref[...] = jnp.zeros_like(db_ref)

        p = pre_ref[...]
        sg = _sigmoid(p)
        dpre = dact_ref[...] * (sg * (1.0 + p * (1.0 - sg)))
        dbuf[tm:tm + HALO, :] = dbuf[0:HALO, :]
        dbuf[0:tm, :] = dpre
        has_prev = (i < nb - 1).astype(f32)
        xbuf[0:HALO, :] = halo_ref[...] * has_prev
        xbuf[HALO:HALO + tm, :] = x_ref[...]
        dx = jnp.zeros((tm, MIX_W), f32)
        for kk in range(CONV_K):
            back = CONV_K - 1 - kk
            dx = dx + w_ref[kk:kk + 1, :] * dbuf[back:back + tm, :]
            off = HALO - (CONV_K - 1) + kk
            dw_ref[kk:kk + 1, :] += jnp.sum(dpre * xbuf[off:off + tm, :], axis=0, keepdims=True)
        dx_ref[...] = dx
        db_ref[...] += jnp.sum(dpre, axis=0, keepdims=True)

    return pl.pallas_call(
        body, name="conv_bwd", grid=(nb,),
        in_specs=[_rows_rev(tm, MIX_W, nb),
                  pl.BlockSpec((HALO, MIX_W), lambda i: (jnp.maximum((nb - 1 - i) * hb - 1, 0), 0)),
                  _const((CONV_K, MIX_W)), _rows_rev(tm, MIX_W, nb), _rows_rev(tm, MIX_W, nb)],
        out_specs=[_rows_rev(tm, MIX_W, nb), _const((CONV_K, MIX_W)), _const((1, MIX_W))],
        out_shape=[jax.ShapeDtypeStruct((t, MIX_W), f32), jax.ShapeDtypeStruct((CONV_K, MIX_W), f32), jax.ShapeDtypeStruct((1, MIX_W), f32)],
        scratch_shapes=[pltpu.VMEM((tm + HALO, MIX_W), f32), pltpu.VMEM((tm + HALO, MIX_W), f32)],
        compiler_params=_cparams(1),
    )(u_ml, u_ml, w, pre, dact)


def _lane_pick(x, lane):
    idx = lax.broadcasted_iota(jnp.int32, x.shape, 1)
    return jnp.sum(jnp.where(idx == lane, x, 0.0), axis=-1, keepdims=True)


def _ml_gate_forms(gates, tri):
    lf = _log_sigmoid(gates)
    gc = _hdot(tri, lf)
    lane = lax.broadcasted_iota(jnp.int32, gates.shape, 1)
    mixed = jnp.where(lane < HEADS, gates, gc)
    sel = (lax.broadcasted_iota(jnp.int32, (8, 128), 0) == lax.broadcasted_iota(jnp.int32, (8, 128), 1)).astype(f32)
    rowsf = _hdot_nt(sel, mixed)
    return gc, rowsf


def _ml_chunk(q, k, v, gates, gc, rowsf, h, c_st, n_st, m_st, tril_mask):
    g_col = _lane_pick(gc, HEADS + h)
    ig_col = _lane_pick(gates, h)
    ig_row = rowsf[h:h + 1, :]
    g_row = rowsf[HEADS + h:HEADS + h + 1, :]
    dmat = jnp.where(tril_mask, g_col - g_row + ig_row, NEG)
    m_inter = g_col + m_st
    m_t = jnp.maximum(m_inter, jnp.max(dmat, axis=-1, keepdims=True))
    wi = jnp.exp(dmat - m_t)
    wo = jnp.exp(m_inter - m_t)
    qk = _bdot_nt(q, k) * wi
    num = _bdot(qk, v) + wo * _bdot(q, c_st)
    den = jnp.sum(qk, axis=-1, keepdims=True) + wo * jnp.sum(q * n_st, axis=-1, keepdims=True)
    floor = jnp.exp(-m_t)
    z = jnp.maximum(jnp.abs(den), floor)
    g_last = g_col[CHUNK - 1:CHUNK]
    a_col = g_last - g_col + ig_col
    m_new = jnp.maximum(g_last + m_st, jnp.max(a_col, axis=0, keepdims=True))
    ws = jnp.exp(a_col - m_new)
    w_old = jnp.exp(g_last + m_st - m_new)
    return dict(wi=wi, wo=wo, qk=qk, num=num, den=den, z=z, floor=floor, ws=ws, w_old=w_old, m_new=m_new)


def _mlstm_fwd(qkc, u_ml, gn):
    t = qkc.shape[0]
    tb = _tile(t, 256)
    nc_blk = tb // CHUNK

    def body(qk_ref, v_ref, mo_ref, gt_ref, gn_ref, og_ref, cst_ref, nst_ref, mst_ref, c_sc, n_sc, m_sc):
        @pl.when(pl.program_id(0) == 0)
        def _():
            c_sc[...] = jnp.zeros_like(c_sc)
            n_sc[...] = jnp.zeros_like(n_sc)
            m_sc[...] = jnp.zeros_like(m_sc)

        tril_mask = _tri(CHUNK)
        tri = tril_mask.astype(f32)

        def chunk(c, carry):
            r0 = pl.multiple_of(c * CHUNK, CHUNK)
            rows = pl.ds(r0, CHUNK)
            gates = gt_ref[rows, :]
            gc, rowsf = _ml_gate_forms(gates, tri)
            for h in range(HEADS):
                cs = slice(h * HEAD_W, (h + 1) * HEAD_W)
                q = qk_ref[rows, h * ML_DQK:(h + 1) * ML_DQK] * (ML_DQK ** -0.5)
                k = qk_ref[rows, HEADS * ML_DQK + h * ML_DQK:HEADS * ML_DQK + (h + 1) * ML_DQK]
                v = v_ref[rows, h * HEAD_W:(h + 1) * HEAD_W]
                mo = mo_ref[rows, h * HEAD_W:(h + 1) * HEAD_W]
                c_st = c_sc[h]
                n_st = n_sc[h]
                m_st = m_sc[h][:, 0:1]
                cst_ref[c, h] = c_st
                nst_ref[c, h] = n_st
                mst_ref[c, h] = m_sc[h]
                r = _ml_chunk(q, k, v, gates, gc, rowsf, h, c_st, n_st, m_st, tril_mask)
                hh = r["num"] / r["z"]
                ksc = k * r["ws"]
                c_sc[h] = r["w_old"] * c_st + _bdot_tn(ksc, v)
                n_sc[h] = r["w_old"] * n_st + jnp.sum(ksc, axis=0, keepdims=True)
                m_sc[h] = r["m_new"] + jnp.zeros((1, 128), f32)
                _, _, y = _head_rms(hh, gn_ref[:, cs])
                og_ref[rows, cs] = (y * _sigmoid(mo)).astype(ACT)
            return carry

        lax.fori_loop(0, nc_blk, chunk, 0)

    nchunks = t // CHUNK
    return pl.pallas_call(
        body, name="mlstm_fwd", grid=(t // tb,),
        in_specs=[_rows(tb, MIX_W), _rows(tb, MIX_W, 1), _rows(tb, MIX_W, 2), _rows(tb, 128, 12), _const((1, MIX_W))],
        out_specs=[_rows(tb, MIX_W),
                   pl.BlockSpec((nc_blk, HEADS, ML_DQK, HEAD_W), lambda i: (i, 0, 0, 0)),
                   pl.BlockSpec((nc_blk, HEADS, 1, ML_DQK), lambda i: (i, 0, 0, 0)),
                   pl.BlockSpec((nc_blk, HEADS, 1, 128), lambda i: (i, 0, 0, 0))],
        out_shape=[jax.ShapeDtypeStruct((t, MIX_W), ACT),
                   jax.ShapeDtypeStruct((nchunks, HEADS, ML_DQK, HEAD_W), f32),
                   jax.ShapeDtypeStruct((nchunks, HEADS, 1, ML_DQK), f32),
                   jax.ShapeDtypeStruct((nchunks, HEADS, 1, 128), f32)],
        scratch_shapes=[pltpu.VMEM((HEADS, ML_DQK, HEAD_W), f32), pltpu.VMEM((HEADS, 1, ML_DQK), f32), pltpu.VMEM((HEADS, 1, 128), f32)],
        compiler_params=_cparams(1),
    )(qkc, u_ml, u_ml, u_ml, gn)


def _mlstm_bwd(qkc, u_ml, gn, cst, nst, mst, dog):
    t = qkc.shape[0]
    tb = _tile(t, 256)
    nb = t // tb
    nc_blk = tb // CHUNK

    def body(qk_ref, v_ref, mo_ref, gt_ref, gn_ref, cst_ref, nst_ref, mst_ref, dog_ref,
             dqk_ref, dv_ref, dmo_ref, dgt_ref, dgn_ref, dc_sc, dn_sc):
        @pl.when(pl.program_id(0) == 0)
        def _():
            dc_sc[...] = jnp.zeros_like(dc_sc)
            dn_sc[...] = jnp.zeros_like(dn_sc)
            dgn_ref[...] = jnp.zeros_like(dgn_ref)

        tril_mask = _tri(CHUNK)
        tri = tril_mask.astype(f32)
        triu = _tri(CHUNK, upper=True).astype(f32)
        lane = lax.broadcasted_iota(jnp.int32, (CHUNK, 128), 1)

        def chunk(j, carry):
            c = nc_blk - 1 - j
            r0 = pl.multiple_of(c * CHUNK, CHUNK)
            rows = pl.ds(r0, CHUNK)
            gates = gt_ref[rows, :]
            gc, rowsf = _ml_gate_forms(gates, tri)
            dg_mat = jnp.zeros((CHUNK, 128), f32)
            dig_mat = jnp.zeros((CHUNK, 128), f32)
            dlast_row = jnp.zeros((1, 128), f32)
            for h in range(HEADS):
                cs = slice(h * HEAD_W, (h + 1) * HEAD_W)
                q = qk_ref[rows, h * ML_DQK:(h + 1) * ML_DQK] * (ML_DQK ** -0.5)
                k = qk_ref[rows, HEADS * ML_DQK + h * ML_DQK:HEADS * ML_DQK + (h + 1) * ML_DQK]
                v = v_ref[rows, h * HEAD_W:(h + 1) * HEAD_W]
                mo = mo_ref[rows, h * HEAD_W:(h + 1) * HEAD_W]
                gnh = gn_ref[:, cs]
                c_st = cst_ref[c, h]
                n_st = nst_ref[c, h]
                m_st = mst_ref[c, h][:, 0:1]
                dc = dc_sc[h]
                dn = dn_sc[h]
                r = _ml_chunk(q, k, v, gates, gc, rowsf, h, c_st, n_st, m_st, tril_mask)
                z = r["z"]
                hh = r["num"] / z
                oh, rstd, y = _head_rms(hh, gnh)
                sg = _sigmoid(mo)
                dogh = dog_ref[rows, cs]
                dy = dogh * sg
                dmo_ref[rows, cs] = dogh * y * (sg * (1.0 - sg))
                dgn_ref[:, cs] += jnp.sum(dy * oh, axis=0, keepdims=True)
                doh = dy * gnh
                dh = rstd * (doh - oh * jnp.mean(doh * oh, axis=-1, keepdims=True))
                dnum = dh / z
                dz = -jnp.sum(dh * hh, axis=-1, keepdims=True) / z
                den = r["den"]
                dden = jnp.where(jnp.abs(den) > r["floor"], dz * jnp.sign(den), 0.0)
                dsw = (_bdot_nt(dnum, v) + dden) * r["wi"]
                wo = r["wo"]
                ws = r["ws"]
                dq = _bdot(dsw, k) + wo * (_bdot_nt(dnum, c_st) + dden * n_st)
                dk_state = ws * (_bdot_nt(v, dc) + dn)
                dk = _bdot_tn(dsw, q) + dk_state
                dv_ref[rows, cs] = _bdot_tn(r["qk"], dnum) + ws * _bdot(k, dc)
                woq = wo * q
                w_old = r["w_old"]
                dc_sc[h] = w_old * dc + _bdot_tn(woq, dnum)
                dn_sc[h] = w_old * dn + jnp.sum(woq * dden, axis=0, keepdims=True)
                d_last = (jnp.sum(jnp.sum(k * dk_state, axis=-1, keepdims=True), axis=0, keepdims=True)
                          + w_old * (jnp.sum(jnp.sum(dc * c_st, axis=-1, keepdims=True), axis=0, keepdims=True)
                                     + jnp.sum(dn * n_st, axis=-1, keepdims=True)))
                kdk = jnp.sum(k * dk, axis=-1, keepdims=True)
                qdq = jnp.sum(q * dq, axis=-1, keepdims=True)
                dg_mat = dg_mat + jnp.where(lane == HEADS + h, qdq - kdk, 0.0)
                dlast_row = dlast_row + jnp.where(lane[0:1] == HEADS + h, d_last, 0.0)
                dig_mat = dig_mat + jnp.where(lane == h, kdk, 0.0)
                dqk_ref[rows, h * ML_DQK:(h + 1) * ML_DQK] = dq * (ML_DQK ** -0.5)
                dqk_ref[rows, HEADS * ML_DQK + h * ML_DQK:HEADS * ML_DQK + (h + 1) * ML_DQK] = dk
            dlf = _hdot(triu, dg_mat) + dlast_row
            dgt_ref[rows, :] = dig_mat + dlf * _sigmoid(-gates)
            return carry

        lax.fori_loop(0, nc_blk, chunk, 0)

    st4 = lambda a, b: pl.BlockSpec((nc_blk, HEADS, a, b), lambda i: (nb - 1 - i, 0, 0, 0))
    return pl.pallas_call(
        body, name="mlstm_bwd", grid=(nb,),
        in_specs=[_rows_rev(tb, MIX_W, nb), _rows_rev(tb, MIX_W, nb, 1), _rows_rev(tb, MIX_W, nb, 2), _rows_rev(tb, 128, nb, 12),
                  _const((1, MIX_W)), st4(ML_DQK, HEAD_W), st4(1, ML_DQK), st4(1, 128), _rows_rev(tb, MIX_W, nb)],
        out_specs=[_rows_rev(tb, MIX_W, nb), _rows_rev(tb, MIX_W, nb), _rows_rev(tb, MIX_W, nb), _rows_rev(tb, 128, nb), _const((1, MIX_W))],
        out_shape=[jax.ShapeDtypeStruct((t, MIX_W), f32)] * 3 + [jax.ShapeDtypeStruct((t, 128), f32), jax.ShapeDtypeStruct((1, MIX_W), f32)],
        scratch_shapes=[pltpu.VMEM((HEADS, ML_DQK, HEAD_W), f32), pltpu.VMEM((HEADS, 1, ML_DQK), f32)],
        compiler_params=_cparams(1),
    )(qkc, u_ml, u_ml, u_ml, gn, cst, nst, mst, dog)


def _ln_fwd(r, g, b):
    mu = jnp.mean(r, axis=-1, keepdims=True)
    xc = r - mu
    rstd = lax.rsqrt(jnp.mean(xc * xc, axis=-1, keepdims=True) + LN_EPS)
    xh = xc * rstd
    return xh * g + b, xh, rstd


def _ln_bwd(dy, xh, rstd, g):
    dxh = dy * g
    return rstd * (dxh - jnp.mean(dxh, axis=-1, keepdims=True) - xh * jnp.mean(dxh * xh, axis=-1, keepdims=True))


def _outproj_ln1(og_hg, og_ml, x, w_out, g, b):
    t = x.shape[0]
    tm = _tile(t, 256)

    def body(a_ref, b_ref, x_ref, w_ref, g_ref, bb_ref, x1_ref, xh_ref, rs_ref):
        mix = _bdot(a_ref[...], w_ref[0:MIX_W, :]) + _bdot(b_ref[...], w_ref[MIX_W:2 * MIX_W, :])
        y, xh, rstd = _ln_fwd(ALPHA * x_ref[...] + mix, g_ref[...], bb_ref[...])
        x1_ref[...] = y
        xh_ref[...] = xh
        rs_ref[...] = rstd

    return pl.pallas_call(
        body, name="outproj_ln1", grid=(t // tm,),
        in_specs=[_rows(tm, MIX_W), _rows(tm, MIX_W), _rows(tm, D_MODEL), _const((D_MODEL, D_MODEL)), _const((1, D_MODEL)), _const((1, D_MODEL))],
        out_specs=[_rows(tm, D_MODEL), _rows(tm, D_MODEL), _rows(tm, 1)],
        out_shape=[jax.ShapeDtypeStruct((t, D_MODEL), f32), jax.ShapeDtypeStruct((t, D_MODEL), f32), jax.ShapeDtypeStruct((t, 1), f32)],
        compiler_params=_cparams(1, arbitrary=False),
    )(og_hg, og_ml, x, w_out, g, b)


def _ffn_up(x1, wg, wu):
    t = x1.shape[0]
    tm = _tile(t, 256)

    def body(x_ref, wg_ref, wu_ref, hg_ref, up_ref, a_ref):
        xv = x_ref[...]
        hg = _bdot(xv, wg_ref[...])
        up = _bdot(xv, wu_ref[...])
        hg_ref[...] = hg
        up_ref[...] = up
        a_ref[...] = (hg * _sigmoid(hg) * up).astype(ACT)

    return pl.pallas_call(
        body, name="ffn_up", grid=(t // tm,),
        in_specs=[_rows(tm, D_MODEL), _const((D_MODEL, D_FF)), _const((D_MODEL, D_FF))],
        out_specs=[_rows(tm, D_FF), _rows(tm, D_FF), _rows(tm, D_FF)],
        out_shape=[jax.ShapeDtypeStruct((t, D_FF), f32), jax.ShapeDtypeStruct((t, D_FF), f32), jax.ShapeDtypeStruct((t, D_FF), ACT)],
        compiler_params=_cparams(1, arbitrary=False),
    )(x1, wg, wu)


def _ffn_down_ln2(a, x1, wd, g, b):
    t = x1.shape[0]
    tm = _tile(t, 256)

    def body(a_ref, x_ref, w_ref, g_ref, bb_ref, x2_ref, xh_ref, rs_ref):
        ffn = _bdot(a_ref[...], w_ref[...])
        y, xh, rstd = _ln_fwd(ALPHA * x_ref[...] + ffn, g_ref[...], bb_ref[...])
        x2_ref[...] = y
        xh_ref[...] = xh
        rs_ref[...] = rstd

    return pl.pallas_call(
        body, name="ffn_down_ln2", grid=(t // tm,),
        in_specs=[_rows(tm, D_FF), _rows(tm, D_MODEL), _const((D_FF, D_MODEL)), _const((1, D_MODEL)), _const((1, D_MODEL))],
        out_specs=[_rows(tm, D_MODEL), _rows(tm, D_MODEL), _rows(tm, 1)],
        out_shape=[jax.ShapeDtypeStruct((t, D_MODEL), f32), jax.ShapeDtypeStruct((t, D_MODEL), f32), jax.ShapeDtypeStruct((t, 1), f32)],
        compiler_params=_cparams(1, arbitrary=False),
    )(a, x1, wd, g, b)


def _head_loss_bwd(x2, xh2, rs2, p, tgt, w_pg, b_pg, w_pp, g2):
    t = x2.shape[0]
    tm = _tile(t, 256)

    def body(x_ref, xh_ref, rs_ref, p_ref, t_ref, wg_ref, bg_ref, wp_ref, g_ref,
             dr_ref, de_ref, dz_ref, loss_ref, dbg_ref, dg2_ref, db2_ref):
        @pl.when(pl.program_id(0) == 0)
        def _():
            loss_ref[...] = jnp.zeros_like(loss_ref)
            dbg_ref[...] = jnp.zeros_like(dbg_ref)
            dg2_ref[...] = jnp.zeros_like(dg2_ref)
            db2_ref[...] = jnp.zeros_like(db2_ref)

        x2v = x_ref[...]
        z = _bdot(x2v, wg_ref[...]) + bg_ref[...]
        e = _bdot(p_ref[...], wp_ref[...])
        sg = _sigmoid(z)
        diff = x2v + sg * e - t_ref[...]
        loss_ref[...] += 0.5 * jnp.sum(jnp.mean(diff * diff, axis=-1, keepdims=True), axis=0, keepdims=True)
        dy = diff * (1.0 / D_MODEL)
        de_ref[...] = (dy * sg).astype(ACT)
        dz = dy * e * (sg * (1.0 - sg))
        dz_ref[...] = dz.astype(ACT)
        dbg_ref[...] += jnp.sum(dz, axis=0, keepdims=True)
        dx2 = dy + _bdot_nt(dz, wg_ref[...])
        xh = xh_ref[...]
        dg2_ref[...] += jnp.sum(dx2 * xh, axis=0, keepdims=True)
        db2_ref[...] += jnp.sum(dx2, axis=0, keepdims=True)
        dr_ref[...] = _ln_bwd(dx2, xh, rs_ref[...], g_ref[...])

    row = jax.ShapeDtypeStruct((1, D_MODEL), f32)
    return pl.pallas_call(
        body, name="head_loss_bwd", grid=(t // tm,),
        in_specs=[_rows(tm, D_MODEL), _rows(tm, D_MODEL), _rows(tm, 1), _rows(tm, PLE), _rows(tm, D_MODEL),
                  _const((D_MODEL, D_MODEL)), _const((1, D_MODEL)), _const((PLE, D_MODEL)), _const((1, D_MODEL))],
        out_specs=[_rows(tm, D_MODEL), _rows(tm, D_MODEL), _rows(tm, D_MODEL), _const((1, 1)), _const((1, D_MODEL)), _const((1, D_MODEL)), _const((1, D_MODEL))],
        out_shape=[jax.ShapeDtypeStruct((t, D_MODEL), f32), jax.ShapeDtypeStruct((t, D_MODEL), ACT), jax.ShapeDtypeStruct((t, D_MODEL), ACT),
                   jax.ShapeDtypeStruct((1, 1), f32), row, row, row],
        compiler_params=_cparams(1),
    )(x2, xh2, rs2, p, tgt, w_pg, b_pg, w_pp, g2)


def _ffn_bwd(dr2, hg, up, xh1, rs1, wd, wg, wu, g1):
    t = dr2.shape[0]
    tm = _tile(t, 128)

    def body(dr_ref, hg_ref, up_ref, xh_ref, rs_ref, wd_ref, wg_ref, wu_ref, g_ref,
             dr1_ref, dhg_ref, dup_ref, dg1_ref, db1_ref):
        @pl.when(pl.program_id(0) == 0)
        def _():
            dg1_ref[...] = jnp.zeros_like(dg1_ref)
            db1_ref[...] = jnp.zeros_like(db1_ref)

        dr2v = dr_ref[...]
        da = _bdot_nt(dr2v, wd_ref[...])
        hgv = hg_ref[...]
        sg = _sigmoid(hgv)
        dhg = da * up_ref[...] * (sg * (1.0 + hgv * (1.0 - sg)))
        dup = da * (hgv * sg)
        dhg_ref[...] = dhg.astype(ACT)
        dup_ref[...] = dup.astype(ACT)
        dx1 = ALPHA * dr2v + _bdot_nt(dhg, wg_ref[...]) + _bdot_nt(dup, wu_ref[...])
        xh = xh_ref[...]
        dg1_ref[...] += jnp.sum(dx1 * xh, axis=0, keepdims=True)
        db1_ref[...] += jnp.sum(dx1, axis=0, keepdims=True)
        dr1_ref[...] = _ln_bwd(dx1, xh, rs_ref[...], g_ref[...])

    row = jax.ShapeDtypeStruct((1, D_MODEL), f32)
    return pl.pallas_call(
        body, name="ffn_bwd", grid=(t // tm,),
        in_specs=[_rows(tm, D_MODEL), _rows(tm, D_FF), _rows(tm, D_FF), _rows(tm, D_MODEL), _rows(tm, 1),
                  _const((D_FF, D_MODEL)), _const((D_MODEL, D_FF)), _const((D_MODEL, D_FF)), _const((1, D_MODEL))],
        out_specs=[_rows(tm, D_MODEL), _rows(tm, D_FF), _rows(tm, D_FF), _const((1, D_MODEL)), _const((1, D_MODEL))],
        out_shape=[jax.ShapeDtypeStruct((t, D_MODEL), f32), jax.ShapeDtypeStruct((t, D_FF), ACT), jax.ShapeDtypeStruct((t, D_FF), ACT), row, row],
        compiler_params=_cparams(1),
    )(dr2, hg, up, xh1, rs1, wd, wg, wu, g1)


def _outproj_bwd(dr1, w_out):
    t = dr1.shape[0]
    tm = _tile(t, 256)

    def body(dr_ref, w_ref, dhg_ref, dml_ref):
        d = _bdot_nt(dr_ref[...], w_ref[...])
        dhg_ref[...] = d[:, 0:MIX_W]
        dml_ref[...] = d[:, MIX_W:2 * MIX_W]

    return pl.pallas_call(
        body, name="outproj_bwd", grid=(t // tm,),
        in_specs=[_rows(tm, D_MODEL), _const((D_MODEL, D_MODEL))],
        out_specs=[_rows(tm, MIX_W), _rows(tm, MIX_W)],
        out_shape=[jax.ShapeDtypeStruct((t, MIX_W), f32)] * 2,
        compiler_params=_cparams(1, arbitrary=False),
    )(dr1, w_out)


def _inproj_bwd(dr1, du_hg, dqk, dmv, dmo, dgt, w_hg, w_ml):
    t = dr1.shape[0]
    tm = _tile(t, 256)

    def body(dr_ref, dhg_ref, dqk_ref, dmv_ref, dmo_ref, dgt_ref, whg_ref, wml_ref, gx_ref, dml_ref):
        dml = jnp.concatenate([dqk_ref[...], dmv_ref[...], dmo_ref[...], dgt_ref[...]], axis=-1).astype(ACT)
        dml_ref[...] = dml
        gx_ref[...] = ALPHA * dr_ref[...] + _bdot_nt(dhg_ref[...], whg_ref[...]) + _bdot_nt(dml, wml_ref[...])

    return pl.pallas_call(
        body, name="inproj_bwd", grid=(t // tm,),
        in_specs=[_rows(tm, D_MODEL), _rows(tm, U_HG), _rows(tm, MIX_W), _rows(tm, MIX_W), _rows(tm, MIX_W), _rows(tm, 128),
                  _const((D_MODEL, U_HG)), _const((D_MODEL, U_ML))],
        out_specs=[_rows(tm, D_MODEL), _rows(tm, U_ML)],
        out_shape=[jax.ShapeDtypeStruct((t, D_MODEL), f32), jax.ShapeDtypeStruct((t, U_ML), ACT)],
        compiler_params=_cparams(1, arbitrary=False),
    )(dr1, du_hg, dqk, dmv, dmo, dgt, w_hg, w_ml)


def _wgrad(a, b, name, tk=None, tn=None):
    t, kdim = a.shape
    n = b.shape[1]
    tk = tk or kdim
    tn = tn or n
    tt = _tile(t, 512)

    def body(a_ref, b_ref, o_ref):
        @pl.when(pl.program_id(2) == 0)
        def _():
            o_ref[...] = jnp.zeros_like(o_ref)

        o_ref[...] += _bdot_tn(a_ref[...], b_ref[...])

    return pl.pallas_call(
        body, name=name, grid=(kdim // tk, n // tn, t // tt),
        in_specs=[pl.BlockSpec((tt, tk), lambda i, j, s: (s, i)), pl.BlockSpec((tt, tn), lambda i, j, s: (s, j))],
        out_specs=pl.BlockSpec((tk, tn), lambda i, j, s: (i, j)),
        out_shape=jax.ShapeDtypeStruct((kdim, n), f32),
        compiler_params=_cparams(3),
    )(a, b)


def _colsum(parts, name):
    t = parts[0].shape[0]
    tt = _tile(t, 512)
    widths = [a.shape[1] for a in parts]

    def body(*refs):
        o_ref = refs[-1]

        @pl.when(pl.program_id(0) == 0)
        def _():
            o_ref[...] = jnp.zeros_like(o_ref)

        off = 0
        for r, w in zip(refs[:-1], widths):
            o_ref[:, off:off + w] += jnp.sum(r[...].astype(f32), axis=0, keepdims=True)
            off += w

    return pl.pallas_call(
        body, name=name, grid=(t // tt,),
        in_specs=[_rows(tt, w) for w in widths],
        out_specs=_const((1, sum(widths))),
        out_shape=jax.ShapeDtypeStruct((1, sum(widths)), f32),
        compiler_params=_cparams(1),
    )(*parts)


def _local_step(x, p, tgt, w_in_b, b_in, logits, conv_w, conv_b, hg_gn, ml_gn, w_out_b, ln1_g, ln1_b,
                wg_b, wu_b, wd_b, ln2_g, ln2_b, w_pp_b, w_pg_b, b_pg):
    pad_w = U_HG + U_ML - PROJ_W
    w_hg = w_in_b[:, :U_HG]
    w_ml = jnp.pad(w_in_b[:, U_HG:], ((0, 0), (0, pad_w)))
    bb_hg = b_in[:, :U_HG]
    bb_ml = jnp.pad(b_in[:, U_HG:], ((0, 0), (0, pad_w)))

    u_hg, u_ml = _inproj(x, w_hg, w_ml, bb_hg, bb_ml)
    og_hg, sst = _hgrn2_fwd(u_hg, logits, hg_gn)
    pre, qkc = _conv_fwd(u_ml, conv_w, conv_b)
    og_ml, cst, nst, mst = _mlstm_fwd(qkc, u_ml, ml_gn)
    x1, xh1, rs1 = _outproj_ln1(og_hg, og_ml, x, w_out_b, ln1_g, ln1_b)
    hgp, up, act = _ffn_up(x1, wg_b, wu_b)
    x2, xh2, rs2 = _ffn_down_ln2(act, x1, wd_b, ln2_g, ln2_b)
    dr2, de, dz, loss, d_bpg, d_ln2g, d_ln2b = _head_loss_bwd(x2, xh2, rs2, p, tgt, w_pg_b, b_pg, w_pp_b, ln2_g)
    dr1, dhg, dup, d_ln1g, d_ln1b = _ffn_bwd(dr2, hgp, up, xh1, rs1, wd_b, wg_b, wu_b, ln1_g)
    dog_hg, dog_ml = _outproj_bwd(dr1, w_out_b)
    du_hg, d_logits, d_hg_gn = _hgrn2_bwd(u_hg, logits, hg_gn, sst, dog_hg)
    dqkc, dmv, dmo, dgt, d_ml_gn = _mlstm_bwd(qkc, u_ml, ml_gn, cst, nst, mst, dog_ml)
    dqk, d_conv_w, d_conv_b = _conv_bwd(u_ml, conv_w, pre, dqkc)
    grad_x, du_ml = _inproj_bwd(dr1, du_hg, dqk, dmv, dmo, dgt, w_hg, w_ml)

    dw_hg = _wgrad(x, du_hg, "wgrad_in_hg", tn=1024)
    dw_ml = _wgrad(x, du_ml, "wgrad_in_ml")
    d_w_in = jnp.concatenate([dw_hg, dw_ml[:, :PROJ_W - U_HG]], axis=1)
    d_b_in = _colsum([du_hg, du_ml], "colsum_du")[:, :PROJ_W]
    d_wo_a = _wgrad(og_hg, dr1, "wgrad_out_hg")
    d_wo_b = _wgrad(og_ml, dr1, "wgrad_out_ml")
    d_w_out = jnp.concatenate([d_wo_a, d_wo_b], axis=0)
    d_wg = _wgrad(x1, dhg, "wgrad_ffn_gate", tn=D_FF // 2)
    d_wu = _wgrad(x1, dup, "wgrad_ffn_up", tn=D_FF // 2)
    d_wd = _wgrad(act, dr2, "wgrad_ffn_down", tk=D_FF // 2)
    d_wpp = _wgrad(p, de, "wgrad_ple_proj")
    d_wpg = _wgrad(x2, dz, "wgrad_ple_gate")

    grads = dict(w_in=d_w_in, b_in=d_b_in, hg_lb_logits=d_logits, ml_conv_w=d_conv_w, ml_conv_b=d_conv_b,
                 hg_norm_g=d_hg_gn, ml_norm_g=d_ml_gn, w_out=d_w_out, ln1_g=d_ln1g, ln1_b=d_ln1b,
                 w_ffn_gate=d_wg, w_ffn_up=d_wu, w_ffn_down=d_wd, ln2_g=d_ln2g, ln2_b=d_ln2b,
                 ple_w_proj=d_wpp, ple_w_gate=d_wpg, ple_b_gate=d_bpg)
    return loss, grad_x, grads


_ANY = pl.BlockSpec(memory_space=pltpu.HBM)
_MESH = pl.DeviceIdType.MESH


def _my_place():
    return lax.axis_index("x"), lax.axis_index("y"), lax.axis_index("c")


def _other_chips(x, y):
    return [(1 - x, y), (x, 1 - y), (1 - x, 1 - y)]


def _allgather_weights(shards, taps, name):
    n = len(shards)
    halves = [s.shape[0] // 2 for s in shards]

    def body(*refs):
        ins, tap_in = refs[:n], refs[n]
        outs, tap_out = refs[n + 1:2 * n + 1], refs[2 * n + 1]
        send_sems, recv_sems, local_sems = refs[2 * n + 2:]
        x, y, c = _my_place()
        me = 2 * x + y
        sibling = (x, y, 1 - c)
        chips = _other_chips(x, y)

        def ici(a, j, block_chip):
            px, py = chips[j]
            src = ins[a].at[pl.ds(pl.multiple_of(c * halves[a], 16), halves[a])] if block_chip is None else outs[a].at[block_chip, c]
            dst = outs[a].at[me if block_chip is None else block_chip, c]
            return pltpu.make_async_remote_copy(src_ref=src, dst_ref=dst, send_sem=send_sems.at[6 * a + j], recv_sem=recv_sems.at[6 * a + j],
                                                device_id=(px, py, c), device_id_type=_MESH)

        def d2d(a, j, half):
            px, py = chips[j]
            blk = outs[a].at[2 * px + py, half]
            return pltpu.make_async_remote_copy(src_ref=blk, dst_ref=blk, send_sem=send_sems.at[6 * a + 3 + j], recv_sem=recv_sems.at[6 * a + 3 + j],
                                                device_id=sibling, device_id_type=_MESH)

        local = []
        for a in range(n):
            for h in range(2):
                cp = pltpu.make_async_copy(ins[a].at[pl.ds(h * halves[a], halves[a])], outs[a].at[me, h], local_sems.at[2 * a + h])
                cp.start()
                local.append(cp)
            for j in range(3):
                ici(a, j, None).start()
        tap_local = pltpu.make_async_copy(tap_in, tap_out.at[me], local_sems.at[2 * n])
        tap_local.start()
        tap_copies = []
        for j, (px, py) in enumerate(chips):
            cp = pltpu.make_async_remote_copy(src_ref=tap_in, dst_ref=tap_out.at[me], send_sem=send_sems.at[6 * n + j], recv_sem=recv_sems.at[6 * n + j],
                                              device_id=(px, py, c), device_id_type=_MESH)
            cp.start()
            tap_copies.append(cp)
        for a in range(n):
            for j, (px, py) in enumerate(chips):
                ici(a, j, 2 * px + py).wait_recv()
                d2d(a, j, c).start()
        for a in range(n):
            for j in range(3):
                d2d(a, j, 1 - c).wait_recv()
        for a in range(n):
            for j in range(3):
                ici(a, j, None).wait_send()
                d2d(a, j, c).wait_send()
        for j, (px, py) in enumerate(chips):
            pltpu.make_async_remote_copy(src_ref=tap_in, dst_ref=tap_out.at[2 * px + py], send_sem=send_sems.at[6 * n + j], recv_sem=recv_sems.at[6 * n + j],
                                         device_id=(px, py, c), device_id_type=_MESH).wait()
        for cp in local:
            cp.wait()
        tap_local.wait()

    res = pl.pallas_call(
        body, name=name,
        in_specs=[_ANY] * (n + 1), out_specs=[_ANY] * (n + 1),
        out_shape=[jax.ShapeDtypeStruct((4, 2, s.shape[0] // 2, s.shape[1]), s.dtype) for s in shards]
        + [jax.ShapeDtypeStruct((4,) + taps.shape, taps.dtype)],
        scratch_shapes=[pltpu.SemaphoreType.DMA((6 * n + 3,)), pltpu.SemaphoreType.DMA((6 * n + 3,)), pltpu.SemaphoreType.DMA((2 * n + 1,))],
    )(*shards, taps)
    return [w.reshape((4,) + s.shape) for w, s in zip(res[:n], shards)], res[n]


def _swap_halves(pieces, name):
    n = len(pieces)
    halves = [p.shape[1] // 2 for p in pieces]

    def body(*refs):
        ins, own, other = refs[:n], refs[n:2 * n], refs[2 * n:3 * n]
        send_sems, recv_sems, local_sems = refs[3 * n:]
        x, y, c = _my_place()

        def half_of(a, which):
            return ins[a].at[pl.ds(0, 4), pl.ds(pl.multiple_of(which * halves[a], 16), halves[a])]

        def to_sibling(a):
            return pltpu.make_async_remote_copy(src_ref=half_of(a, 1 - c), dst_ref=other[a], send_sem=send_sems.at[a], recv_sem=recv_sems.at[a],
                                                device_id=(x, y, 1 - c), device_id_type=_MESH)

        local = []
        for a in range(n):
            cp = pltpu.make_async_copy(half_of(a, c), own[a], local_sems.at[a])
            cp.start()
            local.append(cp)
            to_sibling(a).start()
        for a in range(n):
            to_sibling(a).wait()
            local[a].wait()

    shapes = [jax.ShapeDtypeStruct((4, p.shape[1] // 2, p.shape[2]), p.dtype) for p in pieces]
    res = pl.pallas_call(
        body, name=name,
        in_specs=[_ANY] * n, out_specs=[_ANY] * (2 * n), out_shape=shapes + shapes,
        scratch_shapes=[pltpu.SemaphoreType.DMA((n,)), pltpu.SemaphoreType.DMA((n,)), pltpu.SemaphoreType.DMA((n,))],
    )(*pieces)
    return res[:n], res[n:]


_VMEM = pl.BlockSpec(memory_space=pltpu.VMEM)
_EX_ROWS = 32


def _pair_reduce(p, name):
    s, r, c = p.shape
    half = r // 2

    def body(p_ref, o_ref, other, send_sem, recv_sem):
        x, y, cc = _my_place()
        theirs = pl.multiple_of((1 - cc) * half, 16)
        mine = pl.multiple_of(cc * half, 16)
        cp = pltpu.make_async_remote_copy(src_ref=p_ref.at[pl.ds(0, s), pl.ds(theirs, half)], dst_ref=other, send_sem=send_sem, recv_sem=recv_sem,
                                          device_id=(x, y, 1 - cc), device_id_type=_MESH)
        cp.start()
        cp.wait()

        def step(i, carry):
            r0 = pl.multiple_of(i * _EX_ROWS, _EX_ROWS)
            for slot in range(s):
                own_rows = pl.ds(pl.multiple_of(mine + r0, 16), _EX_ROWS)
                o_ref[slot, pl.ds(r0, _EX_ROWS), :] = (p_ref[slot, own_rows, :] + other[slot, pl.ds(r0, _EX_ROWS), :]).astype(bf16)
            return carry

        lax.fori_loop(0, half // _EX_ROWS, step, 0)

    return pl.pallas_call(
        body, name=name, in_specs=[_VMEM], out_specs=_VMEM,
        out_shape=jax.ShapeDtypeStruct((s, half, c), bf16),
        scratch_shapes=[pltpu.VMEM((s, half, c), f32), pltpu.SemaphoreType.DMA, pltpu.SemaphoreType.DMA],
        compiler_params=pltpu.CompilerParams(vmem_limit_bytes=VMEM_LIMIT),
    )(p)


def _chip_reduce_swap(rcv, name):
    s, h, c = rcv.shape

    def body(r_ref, g_ref, send_sem, recv_sem):
        x, y, cc = _my_place()

        def step(i, carry):
            r0 = pl.multiple_of(i * _EX_ROWS, _EX_ROWS)
            acc = r_ref[0, pl.ds(r0, _EX_ROWS), :].astype(f32)
            for slot in range(1, s):
                acc = acc + r_ref[slot, pl.ds(r0, _EX_ROWS), :].astype(f32)
            g_ref[cc, pl.ds(r0, _EX_ROWS), :] = acc
            return carry

        lax.fori_loop(0, h // _EX_ROWS, step, 0)
        cp = pltpu.make_async_remote_copy(src_ref=g_ref.at[cc], dst_ref=g_ref.at[cc], send_sem=send_sem, recv_sem=recv_sem,
                                          device_id=(x, y, 1 - cc), device_id_type=_MESH)
        cp.start()
        cp.wait()

    return pl.pallas_call(
        body, name=name, in_specs=[_VMEM], out_specs=_VMEM,
        out_shape=jax.ShapeDtypeStruct((2, h, c), f32),
        scratch_shapes=[pltpu.SemaphoreType.DMA, pltpu.SemaphoreType.DMA],
        compiler_params=pltpu.CompilerParams(vmem_limit_bytes=VMEM_LIMIT),
    )(rcv)


def _add_cast(a, b, name):
    s, r, c = a.shape
    tr = _row_tile(r, c)

    def body(a_ref, b_ref, o_ref):
        o_ref[...] = (a_ref[...] + b_ref[...]).astype(bf16)

    blk = pl.BlockSpec((1, tr, c), lambda i, j: (i, j, 0))
    return pl.pallas_call(
        body, name=name, grid=(s, r // tr), in_specs=[blk, blk], out_specs=blk,
        out_shape=jax.ShapeDtypeStruct(a.shape, bf16),
        compiler_params=_cparams(2, arbitrary=False),
    )(a, b)


def _scatter_chips(pieces, name):
    n = len(pieces)

    def body(*refs):
        ins, outs = refs[:n], refs[n:2 * n]
        send_sems, recv_sems, local_sems = refs[2 * n:]
        x, y, c = _my_place()
        me = 2 * x + y
        chips = _other_chips(x, y)
        local = []
        for a in range(n):
            cp = pltpu.make_async_copy(ins[a].at[me], outs[a].at[me], local_sems.at[a])
            cp.start()
            local.append(cp)
            for j, (px, py) in enumerate(chips):
                pltpu.make_async_remote_copy(src_ref=ins[a].at[2 * px + py], dst_ref=outs[a].at[me], send_sem=send_sems.at[3 * a + j],
                                             recv_sem=recv_sems.at[3 * a + j], device_id=(px, py, c), device_id_type=_MESH).start()
        for a in range(n):
            for j, (px, py) in enumerate(chips):
                pltpu.make_async_remote_copy(src_ref=ins[a].at[2 * px + py], dst_ref=outs[a].at[2 * px + py], send_sem=send_sems.at[3 * a + j],
                                             recv_sem=recv_sems.at[3 * a + j], device_id=(px, py, c), device_id_type=_MESH).wait()
            local[a].wait()

    return pl.pallas_call(
        body, name=name,
        in_specs=[_ANY] * n, out_specs=[_ANY] * n,
        out_shape=[jax.ShapeDtypeStruct(s.shape, s.dtype) for s in pieces],
        scratch_shapes=[pltpu.SemaphoreType.DMA((3 * n,)), pltpu.SemaphoreType.DMA((3 * n,)), pltpu.SemaphoreType.DMA((n,))],
    )(*pieces)


def _swap_cores(blocks, name):
    n = len(blocks)
    parts = 4
    rows = [b.shape[0] // parts for b in blocks]

    def body(*refs):
        ins, outs = refs[:n], refs[n:2 * n]
        send_sems, recv_sems, local_sems = refs[2 * n:]
        x, y, c = _my_place()

        def remote(a, k, slot):
            rs = pl.ds(k * rows[a], rows[a])
            return pltpu.make_async_remote_copy(src_ref=ins[a].at[rs], dst_ref=outs[a].at[slot, rs], send_sem=send_sems.at[parts * a + k],
                                                recv_sem=recv_sems.at[parts * a + k], device_id=(x, y, 1 - c), device_id_type=_MESH)

        local = []
        for a in range(n):
            cp = pltpu.make_async_copy(ins[a], outs[a].at[c], local_sems.at[a])
            cp.start()
            local.append(cp)
            for k in range(parts):
                remote(a, k, c).start()
        for a in range(n):
            for k in range(parts):
                remote(a, k, 1 - c).wait()
            local[a].wait()

    return pl.pallas_call(
        body, name=name,
        in_specs=[_ANY] * n, out_specs=[_ANY] * n,
        out_shape=[jax.ShapeDtypeStruct((2,) + s.shape, s.dtype) for s in blocks],
        scratch_shapes=[pltpu.SemaphoreType.DMA((parts * n,)), pltpu.SemaphoreType.DMA((parts * n,)), pltpu.SemaphoreType.DMA((n,))],
    )(*blocks)


def _gather_all(block, name):
    def body(in_ref, out_ref, send_sems, recv_sems, local_sem):
        x, y, c = _my_place()
        me = 4 * x + 2 * y + c
        cp = pltpu.make_async_copy(in_ref, out_ref.at[me], local_sem)
        cp.start()
        peers = []
        for dx in range(2):
            for dy in range(2):
                for dc in range(2):
                    if dx or dy or dc:
                        peers.append((1 - x if dx else x, 1 - y if dy else y, 1 - c if dc else c))
        for j, pr in enumerate(peers):
            pltpu.make_async_remote_copy(src_ref=in_ref, dst_ref=out_ref.at[me], send_sem=send_sems.at[j], recv_sem=recv_sems.at[j],
                                         device_id=pr, device_id_type=_MESH).start()
        for j, (px, py, pc) in enumerate(peers):
            pltpu.make_async_remote_copy(src_ref=in_ref, dst_ref=out_ref.at[4 * px + 2 * py + pc], send_sem=send_sems.at[j], recv_sem=recv_sems.at[j],
                                         device_id=(px, py, pc), device_id_type=_MESH).wait()
        cp.wait()

    return pl.pallas_call(
        body, name=name,
        in_specs=[_ANY], out_specs=_ANY,
        out_shape=jax.ShapeDtypeStruct((8,) + block.shape, block.dtype),
        scratch_shapes=[pltpu.SemaphoreType.DMA((7,)), pltpu.SemaphoreType.DMA((7,)), pltpu.SemaphoreType.DMA],
    )(block)


def _row_tile(r, c):
    best = r
    for cand in range(16, r + 1, 16):
        if r % cand == 0 and cand * c * 4 <= (1 << 20):
            best = cand
    return best if best * c * 4 <= (4 << 20) else r


def _sum_slots(parts, name):
    n, r, c = parts.shape
    tr = _row_tile(r, c)

    def body(p_ref, o_ref):
        acc = p_ref[0].astype(f32)
        for s in range(1, n):
            acc = acc + p_ref[s].astype(f32)
        o_ref[...] = acc

    return pl.pallas_call(
        body, name=name, grid=(r // tr,),
        in_specs=[pl.BlockSpec((n, tr, c), lambda i: (0, i, 0))],
        out_specs=pl.BlockSpec((tr, c), lambda i: (i, 0)),
        out_shape=jax.ShapeDtypeStruct((r, c), f32),
        compiler_params=_cparams(1, arbitrary=False),
    )(parts)


def _adamw(parts, w, m, v, name):
    n, r, c = parts.shape
    tr = _row_tile(r, c)

    def body(p_ref, w_ref, m_ref, v_ref, g_ref, d_ref, nm_ref, nv_ref):
        g = p_ref[0]
        for s in range(1, n):
            g = g + p_ref[s]
        nm = B1 * m_ref[...] + (1.0 - B1) * g
        nv = B2 * v_ref[...] + (1.0 - B2) * (g * g)
        m_hat = nm / (1.0 - B1 ** STEP)
        v_hat = nv / (1.0 - B2 ** STEP)
        g_ref[...] = g
        nm_ref[...] = nm
        nv_ref[...] = nv
        d_ref[...] = -LR * (m_hat / (jnp.sqrt(v_hat) + EPS_ADAM) + WD * w_ref[...])

    blk = pl.BlockSpec((tr, c), lambda i: (i, 0))
    return pl.pallas_call(
        body, name=name, grid=(r // tr,),
        in_specs=[pl.BlockSpec((n, tr, c), lambda i: (0, i, 0)), blk, blk, blk],
        out_specs=[blk] * 4,
        out_shape=[jax.ShapeDtypeStruct((r, c), f32)] * 4,
        compiler_params=_cparams(1, arbitrary=False),
    )(parts, w, m, v)


_BIG = ["w_in", "w_out", "w_ffn_gate", "w_ffn_up", "w_ffn_down", "ple_w_proj", "ple_w_gate"]
_COL_SPLIT = {"w_in", "w_ffn_gate", "w_ffn_up", "ple_w_proj"}
_SMALL = ["b_in", "hg_lb_logits", "ml_conv_w", "ml_conv_b", "hg_norm_g", "ml_norm_g", "ln1_g", "ln1_b", "ln2_g", "ln2_b", "ple_b_gate"]
_ORDER = ["w_in", "b_in", "hg_lb_logits", "ml_conv_w", "ml_conv_b", "hg_norm_g", "ml_norm_g", "w_out", "ln1_g", "ln1_b",
          "w_ffn_gate", "w_ffn_up", "w_ffn_down", "ln2_g", "ln2_b", "ple_w_proj", "ple_w_gate", "ple_b_gate"]
_PACK_ROWS, _PACK_COLS = 16, 1024


def _pack(arrays):
    flat = jnp.concatenate([a.reshape(-1) for a in arrays])
    return jnp.pad(flat, (0, _PACK_ROWS * _PACK_COLS - flat.shape[0])).reshape(_PACK_ROWS, _PACK_COLS)


def _unpack(pack, shapes):
    flat = pack.reshape(-1)
    out, off = [], 0
    for s in shapes:
        size = 1
        for d in s:
            size *= d
        out.append(flat[off:off + size].reshape(s))
        off += size
    return out


def _to_chip_major(g, col_split):
    if col_split:
        k, n = g.shape
        return g.reshape(k, 4, n // 4).transpose(1, 0, 2)
    k, n = g.shape
    return g.reshape(4, k // 4, n)


def _from_chip_major(a, col_split):
    if col_split:
        return a.transpose(1, 0, 2).reshape(a.shape[1], 4 * a.shape[2])
    return a.reshape(4 * a.shape[1], a.shape[2])


def kernel(x, p, w_in, b_in, hg_lb_logits, ml_conv_w, ml_conv_b, hg_norm_g, ml_norm_g, w_out, ln1_g, ln1_b, w_ffn_gate, w_ffn_up, w_ffn_down, ln2_g, ln2_b, ple_w_proj, ple_w_gate, ple_b_gate, loss_target, m_w_in, m_b_in, m_hg_lb_logits, m_ml_conv_w, m_ml_conv_b, m_hg_norm_g, m_ml_norm_g, m_w_out, m_ln1_g, m_ln1_b, m_w_ffn_gate, m_w_ffn_up, m_w_ffn_down, m_ln2_g, m_ln2_b, m_ple_w_proj, m_ple_w_gate, m_ple_b_gate, v_w_in, v_b_in, v_hg_lb_logits, v_ml_conv_w, v_ml_conv_b, v_hg_norm_g, v_ml_norm_g, v_w_out, v_ln1_g, v_ln1_b, v_w_ffn_gate, v_w_ffn_up, v_w_ffn_down, v_ln2_g, v_ln2_b, v_ple_w_proj, v_ple_w_gate, v_ple_b_gate):
    args = dict(locals())
    wts = {k: args[k] for k in _ORDER}
    mom = {k: args["m_" + k] for k in _ORDER}
    var = {k: args["v_" + k] for k in _ORDER}
    two_d = lambda a: a.reshape(a.shape[-2], a.shape[-1])

    shards = [two_d(wts[k]).astype(bf16) for k in _BIG]
    gathered, taps = _allgather_weights(shards, two_d(ml_conv_w), "allgather_weights")
    full = {k: _from_chip_major(g, k in _COL_SPLIT) for k, g in zip(_BIG, gathered)}
    conv_w_full = _from_chip_major(taps, True)

    loss, grad_x, grads = _local_step(
        x[0], p[0, 0], loss_target[0], full["w_in"], b_in, hg_lb_logits, conv_w_full, ml_conv_b, hg_norm_g, ml_norm_g,
        full["w_out"], ln1_g, ln1_b, full["w_ffn_gate"], full["w_ffn_up"], full["w_ffn_down"], ln2_g, ln2_b,
        full["ple_w_proj"], full["ple_w_gate"], ple_b_gate)

    pieces = [_to_chip_major(grads[k], k in _COL_SPLIT) for k in _BIG]
    core_sums = [_pair_reduce(pc, "pair_reduce_" + k) for k, pc in zip(_BIG, pieces)]
    received = _scatter_chips(core_sums, "scatter_grads")
    both = [_chip_reduce_swap(r, "chip_reduce_" + k) for k, r in zip(_BIG, received)]
    out_g, out_d, out_m, out_v = {}, {}, {}, {}
    for k, parts in zip(_BIG, both):
        whole = parts.reshape(1, 2 * parts.shape[1], parts.shape[2])
        g, d, nm, nv = _adamw(whole, two_d(wts[k]), two_d(mom[k]), two_d(var[k]), "adamw_" + k)
        shp = wts[k].shape
        out_g[k], out_d[k], out_m[k], out_v[k] = g.reshape(shp), d.reshape(shp), nm.reshape(shp), nv.reshape(shp)

    small_shapes = [(1, PROJ_W), (2, MIX_W), (CONV_K, MIX_W)] + [(1, MIX_W)] * 3 + [(1, D_MODEL)] * 5 + [(1, 1)]
    contrib = _pack([grads[k] for k in _SMALL] + [loss])
    summed = _sum_slots(_gather_all(contrib, "gather_small"), "sum_small")
    small = _unpack(summed, small_shapes)
    loss_total = small[-1].reshape(())
    gsm = dict(zip(_SMALL, small[:-1]))
    place = 2 * lax.axis_index("x") + lax.axis_index("y")
    conv_cols = ml_conv_w.shape[-1]
    gsm["ml_conv_w"] = lax.dynamic_slice(gsm["ml_conv_w"], (0, place * conv_cols), (CONV_K, conv_cols))
    own_shapes = [wts[k].shape for k in _SMALL]
    g_pack = _pack([gsm[k] for k in _SMALL])
    res = _adamw(g_pack[None], _pack([wts[k] for k in _SMALL]), _pack([mom[k] for k in _SMALL]), _pack([var[k] for k in _SMALL]), "adamw_small")
    for dst, pack in zip((out_g, out_d, out_m, out_v), res):
        for k, a in zip(_SMALL, _unpack(pack, own_shapes)):
            dst[k] = a

    outs = [loss_total, grad_x[None]]
    for group in (out_g, out_d, out_m, out_v):
        outs += [group[k] for k in _ORDER]
    return tuple(outs)
```

```python
import functools

import jax
import jax.numpy as jnp
from jax import lax
from jax.experimental import pallas as pl
from jax.experimental.pallas import tpu as pltpu

f32 = jnp.float32
bf16 = jnp.bfloat16
HI = lax.Precision.HIGHEST

D_MODEL = 1024
HEADS = 4
HEAD_W = 128
MIX_W = HEADS * HEAD_W
ML_DQK = 64
PROJ_W = 3592
U_HG = 4 * MIX_W
U_ML = 3 * MIX_W + 128
D_FF = 2816
PLE = 256
CHUNK = 128
SUB = 16
EXP_CAP = 80.0
CONV_K = 4
HALO = 8
ALPHA = float(2.0 ** 0.25)
LN_EPS = 1e-5
RMS_EPS = 1e-6
NEG = -1e30
LR, B1, B2, EPS_ADAM, WD, STEP = 0.001, 0.9, 0.999, 1e-08, 0.01, 10
VMEM_LIMIT = 56 * 1024 * 1024
DENSE_ROWS = 512


def _cparams(n_axes, arbitrary=True):
    sem = ("arbitrary",) * n_axes if arbitrary else ("parallel",) * n_axes
    return pltpu.CompilerParams(dimension_semantics=sem, vmem_limit_bytes=VMEM_LIMIT)


ACT = bf16


def _mx(a):
    return a.astype(ACT)


def _bdot(a, b):
    return jnp.dot(_mx(a), _mx(b), preferred_element_type=f32)


def _bdot_nt(a, b):
    return lax.dot_general(_mx(a), _mx(b), (((1,), (1,)), ((), ())), preferred_element_type=f32)


def _bdot_tn(a, b):
    return lax.dot_general(_mx(a), _mx(b), (((0,), (0,)), ((), ())), preferred_element_type=f32)


def _split3(x):
    hi = x.astype(bf16)
    r1 = x - hi.astype(f32)
    mid = r1.astype(bf16)
    lo = (r1 - mid.astype(f32)).astype(bf16)
    return hi, mid, lo


def _dot3(a, b, dims):
    a_hi = a.astype(bf16)
    a_lo = (a - a_hi.astype(f32)).astype(bf16)
    b_hi = b.astype(bf16)
    b_lo = (b - b_hi.astype(f32)).astype(bf16)
    dn = (dims, ((), ()))
    return (lax.dot_general(a_hi, b_hi, dn, preferred_element_type=f32) + lax.dot_general(a_hi, b_lo, dn, preferred_element_type=f32)
            + lax.dot_general(a_lo, b_hi, dn, preferred_element_type=f32))


def _sel_dot(sel, x):
    sb = sel.astype(bf16)
    return sum(jnp.dot(sb, part, preferred_element_type=f32) for part in _split3(x))


def _sel_dot_nt(sel, x):
    sb = sel.astype(bf16)
    return sum(lax.dot_general(sb, part, (((1,), (1,)), ((), ())), preferred_element_type=f32) for part in _split3(x))


def _sigmoid(x):
    return 1.0 / (1.0 + jnp.exp(-x))


def _log_sigmoid(x):
    return jnp.minimum(x, 0.0) - jnp.log(1.0 + jnp.exp(-jnp.abs(x)))


def _tri(n, upper=False):
    r = lax.broadcasted_iota(jnp.int32, (n, n), 0)
    c = lax.broadcasted_iota(jnp.int32, (n, n), 1)
    return (c >= r) if upper else (c <= r)


def _rows(tm, n, col=0):
    return pl.BlockSpec((tm, n), lambda i, _c=col: (i, _c))


def _rows_rev(tm, n, nb, col=0):
    return pl.BlockSpec((tm, n), lambda i, _c=col, _nb=nb: (_nb - 1 - i, _c))


def _const(shape):
    return pl.BlockSpec(shape, lambda i, _n=len(shape): (0,) * _n)


def _resident(shape):
    return pl.BlockSpec(shape, lambda i, _n=len(shape): (0,) * _n, pipeline_mode=pl.Buffered(1))


def _tile(t, want):
    return want if t % want == 0 else t


def _inproj(x, w_hg, w_ml, b_hg, b_ml):
    t = x.shape[0]
    tm = _tile(t, DENSE_ROWS)

    def body(x_ref, whg_ref, wml_ref, bhg_ref, bml_ref, uhg_ref, uml_ref):
        xv = x_ref[...]
        uhg_ref[...] = _bdot(xv, whg_ref[...]) + bhg_ref[...]
        uml_ref[...] = _bdot(xv, wml_ref[...]) + bml_ref[...]

    return pl.pallas_call(
        body, name="inproj", grid=(t // tm,),
        in_specs=[_rows(tm, D_MODEL), _resident((D_MODEL, U_HG)), _resident((D_MODEL, U_ML)), _const((1, U_HG)), _const((1, U_ML))],
        out_specs=[_rows(tm, U_HG), _rows(tm, U_ML)],
        out_shape=[jax.ShapeDtypeStruct((t, U_HG), f32), jax.ShapeDtypeStruct((t, U_ML), f32)],
        compiler_params=_cparams(1, arbitrary=False),
    )(x, w_hg, w_ml, b_hg, b_ml)


def _hg_gates(hq, hf, lb, tri):
    s = _sigmoid(hf)
    om = 1.0 - lb
    f = lb + om * s
    g = jnp.log(f)
    k = om * (1.0 - s)
    sq = _sigmoid(hq)
    q = hq * sq
    b = _sel_dot(tri, g)
    return q, sq, s, f, k, b


def _hg_scores(q, k, b, tril_mask):
    qts, kts, eqs, eks, rows = [], [], [], [], []
    for i in range(CHUNK // SUB):
        lo = i * SUB
        ref = jnp.zeros_like(b[0:1]) if i == 0 else b[lo - 1:lo]
        eq = jnp.exp(b[lo:lo + SUB] - ref)
        ek = jnp.exp(jnp.minimum(ref - b, EXP_CAP))
        qt = q[lo:lo + SUB] * eq
        kt = k * ek
        rows.append(_bdot_nt(qt, kt))
        qts.append(qt); kts.append(kt); eqs.append(eq); eks.append(ek)
    a = jnp.where(tril_mask, jnp.concatenate(rows, axis=0), 0.0)
    return a, qts, kts, eqs, eks


def _head_rms(o, gn):
    rstd = lax.rsqrt(jnp.mean(o * o, axis=-1, keepdims=True) + RMS_EPS)
    oh = o * rstd
    return oh, rstd, oh * gn


def _lower_bound(logit_ref):
    lg = logit_ref[...]
    return _sigmoid(lg[0:1] - lg[1:2])


def _hgrn2_fwd(u_hg, logits, gn):
    t = u_hg.shape[0]
    tb = _tile(t, 256)
    nc_blk = tb // CHUNK

    def body(u_ref, lg_ref, gn_ref, og_ref, sst_ref, st_ref):
        @pl.when(pl.program_id(0) == 0)
        def _():
            st_ref[...] = jnp.zeros_like(st_ref)

        lb_all = _lower_bound(lg_ref)
        tril_mask = _tri(CHUNK)
        tri = tril_mask.astype(f32)

        def chunk(c, carry):
            r0 = pl.multiple_of(c * CHUNK, CHUNK)
            rows = pl.ds(r0, CHUNK)
            for h in range(HEADS):
                cs = slice(h * HEAD_W, (h + 1) * HEAD_W)
                hq = u_ref[rows, h * HEAD_W:(h + 1) * HEAD_W]
                hf = u_ref[rows, MIX_W + h * HEAD_W:MIX_W + (h + 1) * HEAD_W]
                hv = u_ref[rows, 2 * MIX_W + h * HEAD_W:2 * MIX_W + (h + 1) * HEAD_W]
                hgate = u_ref[rows, 3 * MIX_W + h * HEAD_W:3 * MIX_W + (h + 1) * HEAD_W]
                q, _, _, _, k, b = _hg_gates(hq, hf, lb_all[:, cs], tri)
                a, _, _, _, _ = _hg_scores(q, k, b, tril_mask)
                st = st_ref[h]
                sst_ref[c, h] = st
                bl = b[CHUNK - 1:CHUNK]
                o = _bdot(a, hv) + _bdot_nt(q * jnp.exp(b), st)
                st_ref[h] = st * jnp.exp(bl) + _bdot_tn(hv, k * jnp.exp(bl - b))
                _, _, y = _head_rms(o, gn_ref[:, cs])
                og_ref[rows, cs] = (y * (hgate * _sigmoid(hgate))).astype(ACT)
            return carry

        lax.fori_loop(0, nc_blk, chunk, 0)

    return pl.pallas_call(
        body, name="hgrn2_fwd", grid=(t // tb,),
        in_specs=[_rows(tb, U_HG), _const((2, MIX_W)), _const((1, MIX_W))],
        out_specs=[_rows(tb, MIX_W), pl.BlockSpec((nc_blk, HEADS, HEAD_W, HEAD_W), lambda i: (i, 0, 0, 0))],
        out_shape=[jax.ShapeDtypeStruct((t, MIX_W), ACT), jax.ShapeDtypeStruct((t // CHUNK, HEADS, HEAD_W, HEAD_W), f32)],
        scratch_shapes=[pltpu.VMEM((HEADS, HEAD_W, HEAD_W), f32)],
        compiler_params=_cparams(1),
    )(u_hg, logits, gn)


def _hgrn2_bwd(u_hg, logits, gn, sst, dog):
    t = u_hg.shape[0]
    tb = _tile(t, 256)
    nb = t // tb
    nc_blk = tb // CHUNK

    def body(u_ref, lg_ref, gn_ref, sst_ref, dog_ref, du_ref, dlg_ref, dgn_ref, dst_ref):
        @pl.when(pl.program_id(0) == 0)
        def _():
            dst_ref[...] = jnp.zeros_like(dst_ref)
            dlg_ref[...] = jnp.zeros_like(dlg_ref)
            dgn_ref[...] = jnp.zeros_like(dgn_ref)

        lb_all = _lower_bound(lg_ref)
        tril_mask = _tri(CHUNK)
        tri = tril_mask.astype(f32)
        triu = _tri(CHUNK, upper=True).astype(f32)

        def chunk(j, carry):
            c = nc_blk - 1 - j
            r0 = pl.multiple_of(c * CHUNK, CHUNK)
            rows = pl.ds(r0, CHUNK)
            for h in range(HEADS):
                cs = slice(h * HEAD_W, (h + 1) * HEAD_W)
                hq = u_ref[rows, h * HEAD_W:(h + 1) * HEAD_W]
                hf = u_ref[rows, MIX_W + h * HEAD_W:MIX_W + (h + 1) * HEAD_W]
                hv = u_ref[rows, 2 * MIX_W + h * HEAD_W:2 * MIX_W + (h + 1) * HEAD_W]
                hgate = u_ref[rows, 3 * MIX_W + h * HEAD_W:3 * MIX_W + (h + 1) * HEAD_W]
                lb = lb_all[:, cs]
                gnh = gn_ref[:, cs]
                q, sq, s, f, k, b = _hg_gates(hq, hf, lb, tri)
                a, qts, kts, eqs, eks = _hg_scores(q, k, b, tril_mask)
                st = sst_ref[c, h]
                dst = dst_ref[h]
                bl = b[CHUNK - 1:CHUNK]
                eb = jnp.exp(b)
                qh = q * eb
                ekl = jnp.exp(bl - b)
                kh = k * ekl
                o = _bdot(a, hv) + _bdot_nt(qh, st)
                oh, rstd, y = _head_rms(o, gnh)
                sg = _sigmoid(hgate)
                dogh = dog_ref[rows, cs]
                dy = dogh * (hgate * sg)
                du_ref[rows, 3 * MIX_W + h * HEAD_W:3 * MIX_W + (h + 1) * HEAD_W] = dogh * y * (sg * (1.0 + hgate * (1.0 - sg)))
                dgn_ref[:, cs] += jnp.sum(dy * oh, axis=0, keepdims=True)
                doh = dy * gnh
                do = rstd * (doh - oh * jnp.mean(doh * oh, axis=-1, keepdims=True))
                da = jnp.where(tril_mask, _bdot_nt(do, hv), 0.0)
                dv = _bdot_tn(a, do) + _bdot_nt(kh, dst)
                dq = _bdot(do, st) * eb
                dk = _bdot(hv, dst) * ekl
                d_last = jnp.sum(k * dk, axis=0, keepdims=True) + jnp.exp(bl) * jnp.sum(dst * st, axis=0, keepdims=True)
                dqs = []
                for i in range(CHUNK // SUB):
                    da_i = da[i * SUB:(i + 1) * SUB]
                    dqs.append(_dot3(da_i, kts[i], ((1,), (0,))) * eqs[i])
                    dk = dk + _dot3(da_i, qts[i], ((0,), (0,))) * eks[i]
                dq = dq + jnp.concatenate(dqs, axis=0)
                dst_ref[h] = dst * jnp.exp(bl) + _bdot_tn(do, qh)
                dg = _sel_dot(triu, q * dq - k * dk) + d_last
                df = dg / f
                dfk = df - dk
                du_ref[rows, h * HEAD_W:(h + 1) * HEAD_W] = dq * (sq * (1.0 + hq * (1.0 - sq)))
                du_ref[rows, MIX_W + h * HEAD_W:MIX_W + (h + 1) * HEAD_W] = (1.0 - lb) * dfk * s * (1.0 - s)
                du_ref[rows, 2 * MIX_W + h * HEAD_W:2 * MIX_W + (h + 1) * HEAD_W] = dv
                dlb = jnp.sum((1.0 - s) * dfk, axis=0, keepdims=True) * (lb * (1.0 - lb))
                dlg_ref[0:1, cs] += dlb
                dlg_ref[1:2, cs] -= dlb
            return carry

        lax.fori_loop(0, nc_blk, chunk, 0)

    return pl.pallas_call(
        body, name="hgrn2_bwd", grid=(nb,),
        in_specs=[_rows_rev(tb, U_HG, nb), _const((2, MIX_W)), _const((1, MIX_W)),
                  pl.BlockSpec((nc_blk, HEADS, HEAD_W, HEAD_W), lambda i: (nb - 1 - i, 0, 0, 0)), _rows_rev(tb, MIX_W, nb)],
        out_specs=[_rows_rev(tb, U_HG, nb), _const((2, MIX_W)), _const((1, MIX_W))],
        out_shape=[jax.ShapeDtypeStruct((t, U_HG), f32), jax.ShapeDtypeStruct((2, MIX_W), f32), jax.ShapeDtypeStruct((1, MIX_W), f32)],
        scratch_shapes=[pltpu.VMEM((HEADS, HEAD_W, HEAD_W), f32)],
        compiler_params=_cparams(1),
    )(u_hg, logits, gn, sst, dog)


def _conv_fwd(u_ml, w, b):
    t = u_ml.shape[0]
    tm = _tile(t, 512)

    def body(x_ref, w_ref, b_ref, pre_ref, act_ref, xbuf):
        @pl.when(pl.program_id(0) == 0)
        def _():
            xbuf[...] = jnp.zeros_like(xbuf)

        xbuf[0:HALO, :] = xbuf[tm:tm + HALO, :]
        xbuf[HALO:HALO + tm, :] = x_ref[...]
        pre = b_ref[...] + jnp.zeros((tm, MIX_W), f32)
        for kk in range(CONV_K):
            off = HALO - (CONV_K - 1) + kk
            pre = pre + w_ref[kk:kk + 1, :] * xbuf[off:off + tm, :]
        pre_ref[...] = pre
        act_ref[...] = pre * _sigmoid(pre)

    return pl.pallas_call(
        body, name="conv_fwd", grid=(t // tm,),
        in_specs=[_rows(tm, MIX_W), _const((CONV_K, MIX_W)), _const((1, MIX_W))],
        out_specs=[_rows(tm, MIX_W), _rows(tm, MIX_W)],
        out_shape=[jax.ShapeDtypeStruct((t, MIX_W), f32)] * 2,
        scratch_shapes=[pltpu.VMEM((tm + HALO, MIX_W), f32)],
        compiler_params=_cparams(1),
    )(u_ml, w, b)


def _conv_bwd(u_ml, w, pre, dact):
    t = u_ml.shape[0]
    tm = _tile(t, 512)
    nb = t // tm
    hb = tm // HALO

    def body(x_ref, halo_ref, w_ref, pre_ref, dact_ref, dx_ref, dw_ref, db_ref, dbuf, xbuf):
        i = pl.program_id(0)

        @pl.when(i == 0)
        def _():
            dbuf[...] = jnp.zeros_like(dbuf)
            dw_ref[...] = jnp.zeros_like(dw_ref)
            db_ref[...] = jnp.zeros_like(db_ref)

        p = pre_ref[...]
        sg = _sigmoid(p)
        dpre = dact_ref[...] * (sg * (1.0 + p * (1.0 - sg)))
        dbuf[tm:tm + HALO, :] = dbuf[0:HALO, :]
        dbuf[0:tm, :] = dpre
        has_prev = (i < nb - 1).astype(f32)
        xbuf[0:HALO, :] = halo_ref[...] * has_prev
        xbuf[HALO:HALO + tm, :] = x_ref[...]
        dx = jnp.zeros((tm, MIX_W), f32)
        for kk in range(CONV_K):
            back = CONV_K - 1 - kk
            dx = dx + w_ref[kk:kk + 1, :] * dbuf[back:back + tm, :]
            off = HALO - (CONV_K - 1) + kk
            dw_ref[kk:kk + 1, :] += jnp.sum(dpre * xbuf[off:off + tm, :], axis=0, keepdims=True)
        dx_ref[...] = dx
        db_ref[...] += jnp.sum(dpre, axis=0, keepdims=True)

    return pl.pallas_call(
        body, name="conv_bwd", grid=(nb,),
        in_specs=[_rows_rev(tm, MIX_W, nb),
                  pl.BlockSpec((HALO, MIX_W), lambda i: (jnp.maximum((nb - 1 - i) * hb - 1, 0), 0)),
                  _const((CONV_K, MIX_W)), _rows_rev(tm, MIX_W, nb), _rows_rev(tm, MIX_W, nb)],
        out_specs=[_rows_rev(tm, MIX_W, nb), _const((CONV_K, MIX_W)), _const((1, MIX_W))],
        out_shape=[jax.ShapeDtypeStruct((t, MIX_W), f32), jax.ShapeDtypeStruct((CONV_K, MIX_W), f32), jax.ShapeDtypeStruct((1, MIX_W), f32)],
        scratch_shapes=[pltpu.VMEM((tm + HALO, MIX_W), f32), pltpu.VMEM((tm + HALO, MIX_W), f32)],
        compiler_params=_cparams(1),
    )(u_ml, u_ml, w, pre, dact)


def _lane_pick(x, lane):
    idx = lax.broadcasted_iota(jnp.int32, x.shape, 1)
    return jnp.sum(jnp.where(idx == lane, x, 0.0), axis=-1, keepdims=True)


def _ml_gate_forms(gates, tri):
    lf = _log_sigmoid(gates)
    gc = _sel_dot(tri, lf)
    lane = lax.broadcasted_iota(jnp.int32, gates.shape, 1)
    mixed = jnp.where(lane < HEADS, gates, gc)
    sel = (lax.broadcasted_iota(jnp.int32, (8, 128), 0) == lax.broadcasted_iota(jnp.int32, (8, 128), 1)).astype(f32)
    rowsf = _sel_dot_nt(sel, mixed)
    return gc, rowsf


def _ml_chunk(q, k, v, gates, gc, rowsf, h, c_st, n_st, m_st, tril_mask):
    g_col = _lane_pick(gc, HEADS + h)
    ig_col = _lane_pick(gates, h)
    ig_row = rowsf[h:h + 1, :]
    g_row = rowsf[HEADS + h:HEADS + h + 1, :]
    dmat = jnp.where(tril_mask, g_col - g_row + ig_row, NEG)
    m_inter = g_col + m_st
    m_t = jnp.maximum(m_inter, jnp.max(dmat, axis=-1, keepdims=True))
    wi = jnp.exp(dmat - m_t)
    wo = jnp.exp(m_inter - m_t)
    qk = _bdot_nt(q, k) * wi
    num = _bdot(qk, v) + wo * _bdot(q, c_st)
    den = jnp.sum(qk, axis=-1, keepdims=True) + wo * jnp.sum(q * n_st, axis=-1, keepdims=True)
    floor = jnp.exp(-m_t)
    z = jnp.maximum(jnp.abs(den), floor)
    g_last = g_col[CHUNK - 1:CHUNK]
    a_col = g_last - g_col + ig_col
    m_new = jnp.maximum(g_last + m_st, jnp.max(a_col, axis=0, keepdims=True))
    ws = jnp.exp(a_col - m_new)
    w_old = jnp.exp(g_last + m_st - m_new)
    return dict(wi=wi, wo=wo, qk=qk, num=num, den=den, z=z, floor=floor, ws=ws, w_old=w_old, m_new=m_new)


def _mlstm_fwd(qkc, u_ml, gn):
    t = qkc.shape[0]
    tb = _tile(t, 256)
    nc_blk = tb // CHUNK

    def body(qk_ref, v_ref, mo_ref, gt_ref, gn_ref, og_ref, cst_ref, nst_ref, mst_ref, c_sc, n_sc, m_sc):
        @pl.when(pl.program_id(0) == 0)
        def _():
            c_sc[...] = jnp.zeros_like(c_sc)
            n_sc[...] = jnp.zeros_like(n_sc)
            m_sc[...] = jnp.zeros_like(m_sc)

        tril_mask = _tri(CHUNK)
        tri = tril_mask.astype(f32)

        def chunk(c, carry):
            r0 = pl.multiple_of(c * CHUNK, CHUNK)
            rows = pl.ds(r0, CHUNK)
            gates = gt_ref[rows, :]
            gc, rowsf = _ml_gate_forms(gates, tri)
            for h in range(HEADS):
                cs = slice(h * HEAD_W, (h + 1) * HEAD_W)
                q = qk_ref[rows, h * ML_DQK:(h + 1) * ML_DQK] * (ML_DQK ** -0.5)
                k = qk_ref[rows, HEADS * ML_DQK + h * ML_DQK:HEADS * ML_DQK + (h + 1) * ML_DQK]
                v = v_ref[rows, h * HEAD_W:(h + 1) * HEAD_W]
                mo = mo_ref[rows, h * HEAD_W:(h + 1) * HEAD_W]
                c_st = c_sc[h]
                n_st = n_sc[h]
                m_st = m_sc[h][:, 0:1]
                cst_ref[c, h] = c_st
                nst_ref[c, h] = n_st
                mst_ref[c, h] = m_sc[h]
                r = _ml_chunk(q, k, v, gates, gc, rowsf, h, c_st, n_st, m_st, tril_mask)
                hh = r["num"] / r["z"]
                ksc = k * r["ws"]
                c_sc[h] = r["w_old"] * c_st + _bdot_tn(ksc, v)
                n_sc[h] = r["w_old"] * n_st + jnp.sum(ksc, axis=0, keepdims=True)
                m_sc[h] = r["m_new"] + jnp.zeros((1, 128), f32)
                _, _, y = _head_rms(hh, gn_ref[:, cs])
                og_ref[rows, cs] = (y * _sigmoid(mo)).astype(ACT)
            return carry

        lax.fori_loop(0, nc_blk, chunk, 0)

    nchunks = t // CHUNK
    return pl.pallas_call(
        body, name="mlstm_fwd", grid=(t // tb,),
        in_specs=[_rows(tb, MIX_W), _rows(tb, MIX_W, 1), _rows(tb, MIX_W, 2), _rows(tb, 128, 12), _const((1, MIX_W))],
        out_specs=[_rows(tb, MIX_W),
                   pl.BlockSpec((nc_blk, HEADS, ML_DQK, HEAD_W), lambda i: (i, 0, 0, 0)),
                   pl.BlockSpec((nc_blk, HEADS, 1, ML_DQK), lambda i: (i, 0, 0, 0)),
                   pl.BlockSpec((nc_blk, HEADS, 1, 128), lambda i: (i, 0, 0, 0))],
        out_shape=[jax.ShapeDtypeStruct((t, MIX_W), ACT),
                   jax.ShapeDtypeStruct((nchunks, HEADS, ML_DQK, HEAD_W), f32),
                   jax.ShapeDtypeStruct((nchunks, HEADS, 1, ML_DQK), f32),
                   jax.ShapeDtypeStruct((nchunks, HEADS, 1, 128), f32)],
        scratch_shapes=[pltpu.VMEM((HEADS, ML_DQK, HEAD_W), f32), pltpu.VMEM((HEADS, 1, ML_DQK), f32), pltpu.VMEM((HEADS, 1, 128), f32)],
        compiler_params=_cparams(1),
    )(qkc, u_ml, u_ml, u_ml, gn)


def _mlstm_bwd(qkc, u_ml, gn, cst, nst, mst, dog):
    t = qkc.shape[0]
    tb = _tile(t, 256)
    nb = t // tb
    nc_blk = tb // CHUNK

    def body(qk_ref, v_ref, mo_ref, gt_ref, gn_ref, cst_ref, nst_ref, mst_ref, dog_ref,
             dqk_ref, dv_ref, dmo_ref, dgt_ref, dgn_ref, dc_sc, dn_sc):
        @pl.when(pl.program_id(0) == 0)
        def _():
            dc_sc[...] = jnp.zeros_like(dc_sc)
            dn_sc[...] = jnp.zeros_like(dn_sc)
            dgn_ref[...] = jnp.zeros_like(dgn_ref)

        tril_mask = _tri(CHUNK)
        tri = tril_mask.astype(f32)
        triu = _tri(CHUNK, upper=True).astype(f32)
        lane = lax.broadcasted_iota(jnp.int32, (CHUNK, 128), 1)

        def chunk(j, carry):
            c = nc_blk - 1 - j
            r0 = pl.multiple_of(c * CHUNK, CHUNK)
            rows = pl.ds(r0, CHUNK)
            gates = gt_ref[rows, :]
            gc, rowsf = _ml_gate_forms(gates, tri)
            dg_mat = jnp.zeros((CHUNK, 128), f32)
            dig_mat = jnp.zeros((CHUNK, 128), f32)
            dlast_row = jnp.zeros((1, 128), f32)
            for h in range(HEADS):
                cs = slice(h * HEAD_W, (h + 1) * HEAD_W)
                q = qk_ref[rows, h * ML_DQK:(h + 1) * ML_DQK] * (ML_DQK ** -0.5)
                k = qk_ref[rows, HEADS * ML_DQK + h * ML_DQK:HEADS * ML_DQK + (h + 1) * ML_DQK]
                v = v_ref[rows, h * HEAD_W:(h + 1) * HEAD_W]
                mo = mo_ref[rows, h * HEAD_W:(h + 1) * HEAD_W]
                gnh = gn_ref[:, cs]
                c_st = cst_ref[c, h]
                n_st = nst_ref[c, h]
                m_st = mst_ref[c, h][:, 0:1]
                dc = dc_sc[h]
                dn = dn_sc[h]
                r = _ml_chunk(q, k, v, gates, gc, rowsf, h, c_st, n_st, m_st, tril_mask)
                z = r["z"]
                hh = r["num"] / z
                oh, rstd, y = _head_rms(hh, gnh)
                sg = _sigmoid(mo)
                dogh = dog_ref[rows, cs]
                dy = dogh * sg
                dmo_ref[rows, cs] = dogh * y * (sg * (1.0 - sg))
                dgn_ref[:, cs] += jnp.sum(dy * oh, axis=0, keepdims=True)
                doh = dy * gnh
                dh = rstd * (doh - oh * jnp.mean(doh * oh, axis=-1, keepdims=True))
                dnum = dh / z
                dz = -jnp.sum(dh * hh, axis=-1, keepdims=True) / z
                den = r["den"]
                dden = jnp.where(jnp.abs(den) > r["floor"], dz * jnp.sign(den), 0.0)
                dsw = (_bdot_nt(dnum, v) + dden) * r["wi"]
                wo = r["wo"]
                ws = r["ws"]
                dq = _bdot(dsw, k) + wo * (_bdot_nt(dnum, c_st) + dden * n_st)
                dk_state = ws * (_bdot_nt(v, dc) + dn)
                dk = _bdot_tn(dsw, q) + dk_state
                dv_ref[rows, cs] = _bdot_tn(r["qk"], dnum) + ws * _bdot(k, dc)
                woq = wo * q
                w_old = r["w_old"]
                dc_sc[h] = w_old * dc + _bdot_tn(woq, dnum)
                dn_sc[h] = w_old * dn + jnp.sum(woq * dden, axis=0, keepdims=True)
                d_last = (jnp.sum(jnp.sum(k * dk_state, axis=-1, keepdims=True), axis=0, keepdims=True)
                          + w_old * (jnp.sum(jnp.sum(dc * c_st, axis=-1, keepdims=True), axis=0, keepdims=True)
                                     + jnp.sum(dn * n_st, axis=-1, keepdims=True)))
                kdk = jnp.sum(k * dk, axis=-1, keepdims=True)
                qdq = jnp.sum(q * dq, axis=-1, keepdims=True)
                dg_mat = dg_mat + jnp.where(lane == HEADS + h, qdq - kdk, 0.0)
                dlast_row = dlast_row + jnp.where(lane[0:1] == HEADS + h, d_last, 0.0)
                dig_mat = dig_mat + jnp.where(lane == h, kdk, 0.0)
                dqk_ref[rows, h * ML_DQK:(h + 1) * ML_DQK] = dq * (ML_DQK ** -0.5)
                dqk_ref[rows, HEADS * ML_DQK + h * ML_DQK:HEADS * ML_DQK + (h + 1) * ML_DQK] = dk
            dlf = _sel_dot(triu, dg_mat) + dlast_row
            dgt_ref[rows, :] = dig_mat + dlf * _sigmoid(-gates)
            return carry

        lax.fori_loop(0, nc_blk, chunk, 0)

    st4 = lambda a, b: pl.BlockSpec((nc_blk, HEADS, a, b), lambda i: (nb - 1 - i, 0, 0, 0))
    return pl.pallas_call(
        body, name="mlstm_bwd", grid=(nb,),
        in_specs=[_rows_rev(tb, MIX_W, nb), _rows_rev(tb, MIX_W, nb, 1), _rows_rev(tb, MIX_W, nb, 2), _rows_rev(tb, 128, nb, 12),
                  _const((1, MIX_W)), st4(ML_DQK, HEAD_W), st4(1, ML_DQK), st4(1, 128), _rows_rev(tb, MIX_W, nb)],
        out_specs=[_rows_rev(tb, MIX_W, nb), _rows_rev(tb, MIX_W, nb), _rows_rev(tb, MIX_W, nb), _rows_rev(tb, 128, nb), _const((1, MIX_W))],
        out_shape=[jax.ShapeDtypeStruct((t, MIX_W), f32)] * 3 + [jax.ShapeDtypeStruct((t, 128), f32), jax.ShapeDtypeStruct((1, MIX_W), f32)],
        scratch_shapes=[pltpu.VMEM((HEADS, ML_DQK, HEAD_W), f32), pltpu.VMEM((HEADS, 1, ML_DQK), f32)],
        compiler_params=_cparams(1),
    )(qkc, u_ml, u_ml, u_ml, gn, cst, nst, mst, dog)


def _ln_fwd(r, g, b):
    mu = jnp.mean(r, axis=-1, keepdims=True)
    xc = r - mu
    rstd = lax.rsqrt(jnp.mean(xc * xc, axis=-1, keepdims=True) + LN_EPS)
    xh = xc * rstd
    return xh * g + b, xh, rstd


def _ln_bwd(dy, xh, rstd, g):
    dxh = dy * g
    return rstd * (dxh - jnp.mean(dxh, axis=-1, keepdims=True) - xh * jnp.mean(dxh * xh, axis=-1, keepdims=True))


def _outproj_ln1(og_hg, og_ml, x, w_out, g, b):
    t = x.shape[0]
    tm = _tile(t, DENSE_ROWS)

    def body(a_ref, b_ref, x_ref, w_ref, g_ref, bb_ref, x1_ref, xh_ref, rs_ref):
        mix = _bdot(a_ref[...], w_ref[0:MIX_W, :]) + _bdot(b_ref[...], w_ref[MIX_W:2 * MIX_W, :])
        y, xh, rstd = _ln_fwd(ALPHA * x_ref[...] + mix, g_ref[...], bb_ref[...])
        x1_ref[...] = y
        xh_ref[...] = xh
        rs_ref[...] = rstd

    return pl.pallas_call(
        body, name="outproj_ln1", grid=(t // tm,),
        in_specs=[_rows(tm, MIX_W), _rows(tm, MIX_W), _rows(tm, D_MODEL), _resident((D_MODEL, D_MODEL)), _const((1, D_MODEL)), _const((1, D_MODEL))],
        out_specs=[_rows(tm, D_MODEL), _rows(tm, D_MODEL), _rows(tm, 1)],
        out_shape=[jax.ShapeDtypeStruct((t, D_MODEL), f32), jax.ShapeDtypeStruct((t, D_MODEL), f32), jax.ShapeDtypeStruct((t, 1), f32)],
        compiler_params=_cparams(1, arbitrary=False),
    )(og_hg, og_ml, x, w_out, g, b)


def _ffn_up(x1, wg, wu):
    t = x1.shape[0]
    tm = _tile(t, DENSE_ROWS)

    def body(x_ref, wg_ref, wu_ref, hg_ref, up_ref, a_ref):
        xv = x_ref[...]
        hg = _bdot(xv, wg_ref[...])
        up = _bdot(xv, wu_ref[...])
        hg_ref[...] = hg
        up_ref[...] = up
        a_ref[...] = (hg * _sigmoid(hg) * up).astype(ACT)

    return pl.pallas_call(
        body, name="ffn_up", grid=(t // tm,),
        in_specs=[_rows(tm, D_MODEL), _resident((D_MODEL, D_FF)), _resident((D_MODEL, D_FF))],
        out_specs=[_rows(tm, D_FF), _rows(tm, D_FF), _rows(tm, D_FF)],
        out_shape=[jax.ShapeDtypeStruct((t, D_FF), f32), jax.ShapeDtypeStruct((t, D_FF), f32), jax.ShapeDtypeStruct((t, D_FF), ACT)],
        compiler_params=_cparams(1, arbitrary=False),
    )(x1, wg, wu)


def _ffn_down_ln2(a, x1, wd, g, b):
    t = x1.shape[0]
    tm = _tile(t, DENSE_ROWS)

    def body(a_ref, x_ref, w_ref, g_ref, bb_ref, x2_ref, xh_ref, rs_ref):
        ffn = _bdot(a_ref[...], w_ref[...])
        y, xh, rstd = _ln_fwd(ALPHA * x_ref[...] + ffn, g_ref[...], bb_ref[...])
        x2_ref[...] = y
        xh_ref[...] = xh
        rs_ref[...] = rstd

    return pl.pallas_call(
        body, name="ffn_down_ln2", grid=(t // tm,),
        in_specs=[_rows(tm, D_FF), _rows(tm, D_MODEL), _resident((D_FF, D_MODEL)), _const((1, D_MODEL)), _const((1, D_MODEL))],
        out_specs=[_rows(tm, D_MODEL), _rows(tm, D_MODEL), _rows(tm, 1)],
        out_shape=[jax.ShapeDtypeStruct((t, D_MODEL), f32), jax.ShapeDtypeStruct((t, D_MODEL), f32), jax.ShapeDtypeStruct((t, 1), f32)],
        compiler_params=_cparams(1, arbitrary=False),
    )(a, x1, wd, g, b)


def _head_loss_bwd(x2, xh2, rs2, p, tgt, w_pg, b_pg, w_pp, g2):
    t = x2.shape[0]
    tm = _tile(t, DENSE_ROWS)

    def body(x_ref, xh_ref, rs_ref, p_ref, t_ref, wg_ref, bg_ref, wp_ref, g_ref,
             dr_ref, de_ref, dz_ref, loss_ref, dbg_ref, dg2_ref, db2_ref):
        @pl.when(pl.program_id(0) == 0)
        def _():
            loss_ref[...] = jnp.zeros_like(loss_ref)
            dbg_ref[...] = jnp.zeros_like(dbg_ref)
            dg2_ref[...] = jnp.zeros_like(dg2_ref)
            db2_ref[...] = jnp.zeros_like(db2_ref)

        x2v = x_ref[...]
        z = _bdot(x2v, wg_ref[...]) + bg_ref[...]
        e = _bdot(p_ref[...], wp_ref[...])
        sg = _sigmoid(z)
        diff = x2v + sg * e - t_ref[...]
        loss_ref[...] += 0.5 * jnp.sum(jnp.mean(diff * diff, axis=-1, keepdims=True), axis=0, keepdims=True)
        dy = diff * (1.0 / D_MODEL)
        de_ref[...] = (dy * sg).astype(ACT)
        dz = dy * e * (sg * (1.0 - sg))
        dz_ref[...] = dz.astype(ACT)
        dbg_ref[...] += jnp.sum(dz, axis=0, keepdims=True)
        dx2 = dy + _bdot_nt(dz, wg_ref[...])
        xh = xh_ref[...]
        dg2_ref[...] += jnp.sum(dx2 * xh, axis=0, keepdims=True)
        db2_ref[...] += jnp.sum(dx2, axis=0, keepdims=True)
        dr_ref[...] = _ln_bwd(dx2, xh, rs_ref[...], g_ref[...])

    row = jax.ShapeDtypeStruct((1, D_MODEL), f32)
    return pl.pallas_call(
        body, name="head_loss_bwd", grid=(t // tm,),
        in_specs=[_rows(tm, D_MODEL), _rows(tm, D_MODEL), _rows(tm, 1), _rows(tm, PLE), _rows(tm, D_MODEL),
                  _resident((D_MODEL, D_MODEL)), _const((1, D_MODEL)), _resident((PLE, D_MODEL)), _const((1, D_MODEL))],
        out_specs=[_rows(tm, D_MODEL), _rows(tm, D_MODEL), _rows(tm, D_MODEL), _const((1, 1)), _const((1, D_MODEL)), _const((1, D_MODEL)), _const((1, D_MODEL))],
        out_shape=[jax.ShapeDtypeStruct((t, D_MODEL), f32), jax.ShapeDtypeStruct((t, D_MODEL), ACT), jax.ShapeDtypeStruct((t, D_MODEL), ACT),
                   jax.ShapeDtypeStruct((1, 1), f32), row, row, row],
        compiler_params=_cparams(1),
    )(x2, xh2, rs2, p, tgt, w_pg, b_pg, w_pp, g2)


def _ffn_bwd(dr2, hg, up, xh1, rs1, wd, wg, wu, g1):
    t = dr2.shape[0]
    tm = _tile(t, DENSE_ROWS // 2)

    def body(dr_ref, hg_ref, up_ref, xh_ref, rs_ref, wd_ref, wg_ref, wu_ref, g_ref,
             dr1_ref, dhg_ref, dup_ref, dg1_ref, db1_ref):
        @pl.when(pl.program_id(0) == 0)
        def _():
            dg1_ref[...] = jnp.zeros_like(dg1_ref)
            db1_ref[...] = jnp.zeros_like(db1_ref)

        dr2v = dr_ref[...]
        da = _bdot_nt(dr2v, wd_ref[...])
        hgv = hg_ref[...]
        sg = _sigmoid(hgv)
        dhg = da * up_ref[...] * (sg * (1.0 + hgv * (1.0 - sg)))
        dup = da * (hgv * sg)
        dhg_ref[...] = dhg.astype(ACT)
        dup_ref[...] = dup.astype(ACT)
        dx1 = ALPHA * dr2v + _bdot_nt(dhg, wg_ref[...]) + _bdot_nt(dup, wu_ref[...])
        xh = xh_ref[...]
        dg1_ref[...] += jnp.sum(dx1 * xh, axis=0, keepdims=True)
        db1_ref[...] += jnp.sum(dx1, axis=0, keepdims=True)
        dr1_ref[...] = _ln_bwd(dx1, xh, rs_ref[...], g_ref[...])

    row = jax.ShapeDtypeStruct((1, D_MODEL), f32)
    return pl.pallas_call(
        body, name="ffn_bwd", grid=(t // tm,),
        in_specs=[_rows(tm, D_MODEL), _rows(tm, D_FF), _rows(tm, D_FF), _rows(tm, D_MODEL), _rows(tm, 1),
                  _resident((D_FF, D_MODEL)), _resident((D_MODEL, D_FF)), _resident((D_MODEL, D_FF)), _const((1, D_MODEL))],
        out_specs=[_rows(tm, D_MODEL), _rows(tm, D_FF), _rows(tm, D_FF), _const((1, D_MODEL)), _const((1, D_MODEL))],
        out_shape=[jax.ShapeDtypeStruct((t, D_MODEL), f32), jax.ShapeDtypeStruct((t, D_FF), ACT), jax.ShapeDtypeStruct((t, D_FF), ACT), row, row],
        compiler_params=_cparams(1),
    )(dr2, hg, up, xh1, rs1, wd, wg, wu, g1)


def _outproj_bwd(dr1, w_out):
    t = dr1.shape[0]
    tm = _tile(t, DENSE_ROWS)

    def body(dr_ref, w_ref, dhg_ref, dml_ref):
        d = _bdot_nt(dr_ref[...], w_ref[...])
        dhg_ref[...] = d[:, 0:MIX_W]
        dml_ref[...] = d[:, MIX_W:2 * MIX_W]

    return pl.pallas_call(
        body, name="outproj_bwd", grid=(t // tm,),
        in_specs=[_rows(tm, D_MODEL), _resident((D_MODEL, D_MODEL))],
        out_specs=[_rows(tm, MIX_W), _rows(tm, MIX_W)],
        out_shape=[jax.ShapeDtypeStruct((t, MIX_W), f32)] * 2,
        compiler_params=_cparams(1, arbitrary=False),
    )(dr1, w_out)


def _inproj_bwd(dr1, du_hg, dqk, dmv, dmo, dgt, w_hg, w_ml):
    t = dr1.shape[0]
    tm = _tile(t, DENSE_ROWS)

    def body(dr_ref, dhg_ref, dqk_ref, dmv_ref, dmo_ref, dgt_ref, whg_ref, wml_ref, gx_ref, dml_ref):
        dml = jnp.concatenate([dqk_ref[...], dmv_ref[...], dmo_ref[...], dgt_ref[...]], axis=-1).astype(ACT)
        dml_ref[...] = dml
        gx_ref[...] = ALPHA * dr_ref[...] + _bdot_nt(dhg_ref[...], whg_ref[...]) + _bdot_nt(dml, wml_ref[...])

    return pl.pallas_call(
        body, name="inproj_bwd", grid=(t // tm,),
        in_specs=[_rows(tm, D_MODEL), _rows(tm, U_HG), _rows(tm, MIX_W), _rows(tm, MIX_W), _rows(tm, MIX_W), _rows(tm, 128),
                  _resident((D_MODEL, U_HG)), _resident((D_MODEL, U_ML))],
        out_specs=[_rows(tm, D_MODEL), _rows(tm, U_ML)],
        out_shape=[jax.ShapeDtypeStruct((t, D_MODEL), f32), jax.ShapeDtypeStruct((t, U_ML), ACT)],
        compiler_params=_cparams(1, arbitrary=False),
    )(dr1, du_hg, dqk, dmv, dmo, dgt, w_hg, w_ml)


def _wgrad(a, b, name, tk=None, tn=None):
    t, kdim = a.shape
    n = b.shape[1]
    tk = tk or kdim
    tn = tn or n
    tt = _tile(t, 512)

    def body(a_ref, b_ref, o_ref):
        @pl.when(pl.program_id(2) == 0)
        def _():
            o_ref[...] = jnp.zeros_like(o_ref)

        o_ref[...] += _bdot_tn(a_ref[...], b_ref[...])

    return pl.pallas_call(
        body, name=name, grid=(kdim // tk, n // tn, t // tt),
        in_specs=[pl.BlockSpec((tt, tk), lambda i, j, s: (s, i)), pl.BlockSpec((tt, tn), lambda i, j, s: (s, j))],
        out_specs=pl.BlockSpec((tk, tn), lambda i, j, s: (i, j)),
        out_shape=jax.ShapeDtypeStruct((kdim, n), f32),
        compiler_params=_cparams(3),
    )(a, b)


def _colsum(parts, name):
    t = parts[0].shape[0]
    tt = _tile(t, 512)
    widths = [a.shape[1] for a in parts]

    def body(*refs):
        o_ref = refs[-1]

        @pl.when(pl.program_id(0) == 0)
        def _():
            o_ref[...] = jnp.zeros_like(o_ref)

        off = 0
        for r, w in zip(refs[:-1], widths):
            o_ref[:, off:off + w] += jnp.sum(r[...].astype(f32), axis=0, keepdims=True)
            off += w

    return pl.pallas_call(
        body, name=name, grid=(t // tt,),
        in_specs=[_rows(tt, w) for w in widths],
        out_specs=_const((1, sum(widths))),
        out_shape=jax.ShapeDtypeStruct((1, sum(widths)), f32),
        compiler_params=_cparams(1),
    )(*parts)


def _local_step(x, p, tgt, w_in_b, b_in, logits, conv_w, conv_b, hg_gn, ml_gn, w_out_b, ln1_g, ln1_b,
                wg_b, wu_b, wd_b, ln2_g, ln2_b, w_pp_b, w_pg_b, b_pg):
    pad_w = U_HG + U_ML - PROJ_W
    w_hg = w_in_b[:, :U_HG]
    w_ml = jnp.pad(w_in_b[:, U_HG:], ((0, 0), (0, pad_w)))
    bb_hg = b_in[:, :U_HG]
    bb_ml = jnp.pad(b_in[:, U_HG:], ((0, 0), (0, pad_w)))

    u_hg, u_ml = _inproj(x, w_hg, w_ml, bb_hg, bb_ml)
    og_hg, sst = _hgrn2_fwd(u_hg, logits, hg_gn)
    pre, qkc = _conv_fwd(u_ml, conv_w, conv_b)
    og_ml, cst, nst, mst = _mlstm_fwd(qkc, u_ml, ml_gn)
    x1, xh1, rs1 = _outproj_ln1(og_hg, og_ml, x, w_out_b, ln1_g, ln1_b)
    hgp, up, act = _ffn_up(x1, wg_b, wu_b)
    x2, xh2, rs2 = _ffn_down_ln2(act, x1, wd_b, ln2_g, ln2_b)
    dr2, de, dz, loss, d_bpg, d_ln2g, d_ln2b = _head_loss_bwd(x2, xh2, rs2, p, tgt, w_pg_b, b_pg, w_pp_b, ln2_g)
    dr1, dhg, dup, d_ln1g, d_ln1b = _ffn_bwd(dr2, hgp, up, xh1, rs1, wd_b, wg_b, wu_b, ln1_g)
    dog_hg, dog_ml = _outproj_bwd(dr1, w_out_b)
    du_hg, d_logits, d_hg_gn = _hgrn2_bwd(u_hg, logits, hg_gn, sst, dog_hg)
    dqkc, dmv, dmo, dgt, d_ml_gn = _mlstm_bwd(qkc, u_ml, ml_gn, cst, nst, mst, dog_ml)
    dqk, d_conv_w, d_conv_b = _conv_bwd(u_ml, conv_w, pre, dqkc)
    grad_x, du_ml = _inproj_bwd(dr1, du_hg, dqk, dmv, dmo, dgt, w_hg, w_ml)

    dw_hg = _wgrad(x, du_hg, "wgrad_in_hg", tn=1024)
    dw_ml = _wgrad(x, du_ml, "wgrad_in_ml")
    d_w_in = jnp.concatenate([dw_hg, dw_ml[:, :PROJ_W - U_HG]], axis=1)
    d_b_in = _colsum([du_hg, du_ml], "colsum_du")[:, :PROJ_W]
    d_wo_a = _wgrad(og_hg, dr1, "wgrad_out_hg")
    d_wo_b = _wgrad(og_ml, dr1, "wgrad_out_ml")
    d_w_out = jnp.concatenate([d_wo_a, d_wo_b], axis=0)
    d_wg = _wgrad(x1, dhg, "wgrad_ffn_gate", tn=D_FF // 2)
    d_wu = _wgrad(x1, dup, "wgrad_ffn_up", tn=D_FF // 2)
    d_wd = _wgrad(act, dr2, "wgrad_ffn_down", tk=D_FF // 2)
    d_wpp = _wgrad(p, de, "wgrad_ple_proj")
    d_wpg = _wgrad(x2, dz, "wgrad_ple_gate")

    grads = dict(w_in=d_w_in, b_in=d_b_in, hg_lb_logits=d_logits, ml_conv_w=d_conv_w, ml_conv_b=d_conv_b,
                 hg_norm_g=d_hg_gn, ml_norm_g=d_ml_gn, w_out=d_w_out, ln1_g=d_ln1g, ln1_b=d_ln1b,
                 w_ffn_gate=d_wg, w_ffn_up=d_wu, w_ffn_down=d_wd, ln2_g=d_ln2g, ln2_b=d_ln2b,
                 ple_w_proj=d_wpp, ple_w_gate=d_wpg, ple_b_gate=d_bpg)
    return loss, grad_x, grads


_ANY = pl.BlockSpec(memory_space=pltpu.HBM)
_MESH = pl.DeviceIdType.MESH


def _my_place():
    return lax.axis_index("x"), lax.axis_index("y"), lax.axis_index("c")


def _other_chips(x, y):
    return [(1 - x, y), (x, 1 - y), (1 - x, 1 - y)]


def _allgather_weights(shards, taps, name):
    n = len(shards)
    halves = [s.shape[0] // 2 for s in shards]

    def body(*refs):
        ins, tap_in = refs[:n], refs[n]
        outs, tap_out = refs[n + 1:2 * n + 1], refs[2 * n + 1]
        send_sems, recv_sems, local_sems = refs[2 * n + 2:]
        x, y, c = _my_place()
        me = 2 * x + y
        sibling = (x, y, 1 - c)
        chips = _other_chips(x, y)

        def ici(a, j, block_chip):
            px, py = chips[j]
            src = ins[a].at[pl.ds(pl.multiple_of(c * halves[a], 16), halves[a])] if block_chip is None else outs[a].at[block_chip, c]
            dst = outs[a].at[me if block_chip is None else block_chip, c]
            return pltpu.make_async_remote_copy(src_ref=src, dst_ref=dst, send_sem=send_sems.at[6 * a + j], recv_sem=recv_sems.at[6 * a + j],
                                                device_id=(px, py, c), device_id_type=_MESH)

        def d2d(a, j, half):
            px, py = chips[j]
            blk = outs[a].at[2 * px + py, half]
            return pltpu.make_async_remote_copy(src_ref=blk, dst_ref=blk, send_sem=send_sems.at[6 * a + 3 + j], recv_sem=recv_sems.at[6 * a + 3 + j],
                                                device_id=sibling, device_id_type=_MESH)

        local = []
        for a in range(n):
            for h in range(2):
                cp = pltpu.make_async_copy(ins[a].at[pl.ds(h * halves[a], halves[a])], outs[a].at[me, h], local_sems.at[2 * a + h])
                cp.start()
                local.append(cp)
            for j in range(3):
                ici(a, j, None).start()
        tap_local = pltpu.make_async_copy(tap_in, tap_out.at[me], local_sems.at[2 * n])
        tap_local.start()
        tap_copies = []
        for j, (px, py) in enumerate(chips):
            cp = pltpu.make_async_remote_copy(src_ref=tap_in, dst_ref=tap_out.at[me], send_sem=send_sems.at[6 * n + j], recv_sem=recv_sems.at[6 * n + j],
                                              device_id=(px, py, c), device_id_type=_MESH)
            cp.start()
            tap_copies.append(cp)
        for a in range(n):
            for j, (px, py) in enumerate(chips):
                ici(a, j, 2 * px + py).wait_recv()
                d2d(a, j, c).start()
        for a in range(n):
            for j in range(3):
                d2d(a, j, 1 - c).wait_recv()
        for a in range(n):
            for j in range(3):
                ici(a, j, None).wait_send()
                d2d(a, j, c).wait_send()
        for j, (px, py) in enumerate(chips):
            pltpu.make_async_remote_copy(src_ref=tap_in, dst_ref=tap_out.at[2 * px + py], send_sem=send_sems.at[6 * n + j], recv_sem=recv_sems.at[6 * n + j],
                                         device_id=(px, py, c), device_id_type=_MESH).wait()
        for cp in local:
            cp.wait()
        tap_local.wait()

    res = pl.pallas_call(
        body, name=name,
        in_specs=[_ANY] * (n + 1), out_specs=[_ANY] * (n + 1),
        out_shape=[jax.ShapeDtypeStruct((4, 2, s.shape[0] // 2, s.shape[1]), s.dtype) for s in shards]
        + [jax.ShapeDtypeStruct((4,) + taps.shape, taps.dtype)],
        scratch_shapes=[pltpu.SemaphoreType.DMA((6 * n + 3,)), pltpu.SemaphoreType.DMA((6 * n + 3,)), pltpu.SemaphoreType.DMA((2 * n + 1,))],
    )(*shards, taps)
    return [w.reshape((4,) + s.shape) for w, s in zip(res[:n], shards)], res[n]


def _swap_halves(pieces, name):
    n = len(pieces)
    halves = [p.shape[1] // 2 for p in pieces]

    def body(*refs):
        ins, own, other = refs[:n], refs[n:2 * n], refs[2 * n:3 * n]
        send_sems, recv_sems, local_sems = refs[3 * n:]
        x, y, c = _my_place()

        def half_of(a, which):
            return ins[a].at[pl.ds(0, 4), pl.ds(pl.multiple_of(which * halves[a], 16), halves[a])]

        def to_sibling(a):
            return pltpu.make_async_remote_copy(src_ref=half_of(a, 1 - c), dst_ref=other[a], send_sem=send_sems.at[a], recv_sem=recv_sems.at[a],
                                                device_id=(x, y, 1 - c), device_id_type=_MESH)

        local = []
        for a in range(n):
            cp = pltpu.make_async_copy(half_of(a, c), own[a], local_sems.at[a])
            cp.start()
            local.append(cp)
            to_sibling(a).start()
        for a in range(n):
            to_sibling(a).wait()
            local[a].wait()

    shapes = [jax.ShapeDtypeStruct((4, p.shape[1] // 2, p.shape[2]), p.dtype) for p in pieces]
    res = pl.pallas_call(
        body, name=name,
        in_specs=[_ANY] * n, out_specs=[_ANY] * (2 * n), out_shape=shapes + shapes,
        scratch_shapes=[pltpu.SemaphoreType.DMA((n,)), pltpu.SemaphoreType.DMA((n,)), pltpu.SemaphoreType.DMA((n,))],
    )(*pieces)
    return res[:n], res[n:]


_VMEM = pl.BlockSpec(memory_space=pltpu.VMEM)
_EX_ROWS = 32


def _pair_reduce(p, name):
    s, r, c = p.shape
    half = r // 2

    def body(p_ref, o_ref, other, send_sem, recv_sem):
        x, y, cc = _my_place()
        theirs = pl.multiple_of((1 - cc) * half, 16)
        mine = pl.multiple_of(cc * half, 16)
        cp = pltpu.make_async_remote_copy(src_ref=p_ref.at[pl.ds(0, s), pl.ds(theirs, half)], dst_ref=other, send_sem=send_sem, recv_sem=recv_sem,
                                          device_id=(x, y, 1 - cc), device_id_type=_MESH)
        cp.start()
        cp.wait()

        def step(i, carry):
            r0 = pl.multiple_of(i * _EX_ROWS, _EX_ROWS)
            for slot in range(s):
                own_rows = pl.ds(pl.multiple_of(mine + r0, 16), _EX_ROWS)
                o_ref[slot, pl.ds(r0, _EX_ROWS), :] = (p_ref[slot, own_rows, :] + other[slot, pl.ds(r0, _EX_ROWS), :]).astype(bf16)
            return carry

        lax.fori_loop(0, half // _EX_ROWS, step, 0)

    return pl.pallas_call(
        body, name=name, in_specs=[_VMEM], out_specs=_VMEM,
        out_shape=jax.ShapeDtypeStruct((s, half, c), bf16),
        scratch_shapes=[pltpu.VMEM((s, half, c), f32), pltpu.SemaphoreType.DMA, pltpu.SemaphoreType.DMA],
        compiler_params=pltpu.CompilerParams(vmem_limit_bytes=VMEM_LIMIT),
    )(p)


def _chip_reduce_swap(rcv, name):
    s, h, c = rcv.shape

    def body(r_ref, g_ref, send_sem, recv_sem):
        x, y, cc = _my_place()

        def step(i, carry):
            r0 = pl.multiple_of(i * _EX_ROWS, _EX_ROWS)
            acc = r_ref[0, pl.ds(r0, _EX_ROWS), :].astype(f32)
            for slot in range(1, s):
                acc = acc + r_ref[slot, pl.ds(r0, _EX_ROWS), :].astype(f32)
            g_ref[cc, pl.ds(r0, _EX_ROWS), :] = acc
            return carry

        lax.fori_loop(0, h // _EX_ROWS, step, 0)
        cp = pltpu.make_async_remote_copy(src_ref=g_ref.at[cc], dst_ref=g_ref.at[cc], send_sem=send_sem, recv_sem=recv_sem,
                                          device_id=(x, y, 1 - cc), device_id_type=_MESH)
        cp.start()
        cp.wait()

    return pl.pallas_call(
        body, name=name, in_specs=[_VMEM], out_specs=_VMEM,
        out_shape=jax.ShapeDtypeStruct((2, h, c), f32),
        scratch_shapes=[pltpu.SemaphoreType.DMA, pltpu.SemaphoreType.DMA],
        compiler_params=pltpu.CompilerParams(vmem_limit_bytes=VMEM_LIMIT),
    )(rcv)


def _add_cast(a, b, name):
    s, r, c = a.shape
    tr = _row_tile(r, c)

    def body(a_ref, b_ref, o_ref):
        o_ref[...] = (a_ref[...] + b_ref[...]).astype(bf16)

    blk = pl.BlockSpec((1, tr, c), lambda i, j: (i, j, 0))
    return pl.pallas_call(
        body, name=name, grid=(s, r // tr), in_specs=[blk, blk], out_specs=blk,
        out_shape=jax.ShapeDtypeStruct(a.shape, bf16),
        compiler_params=_cparams(2, arbitrary=False),
    )(a, b)


def _scatter_chips(pieces, name):
    n = len(pieces)

    def body(*refs):
        ins, outs = refs[:n], refs[n:2 * n]
        send_sems, recv_sems, local_sems = refs[2 * n:]
        x, y, c = _my_place()
        me = 2 * x + y
        chips = _other_chips(x, y)
        local = []
        for a in range(n):
            cp = pltpu.make_async_copy(ins[a].at[me], outs[a].at[me], local_sems.at[a])
            cp.start()
            local.append(cp)
            for j, (px, py) in enumerate(chips):
                pltpu.make_async_remote_copy(src_ref=ins[a].at[2 * px + py], dst_ref=outs[a].at[me], send_sem=send_sems.at[3 * a + j],
                                             recv_sem=recv_sems.at[3 * a + j], device_id=(px, py, c), device_id_type=_MESH).start()
        for a in range(n):
            for j, (px, py) in enumerate(chips):
                pltpu.make_async_remote_copy(src_ref=ins[a].at[2 * px + py], dst_ref=outs[a].at[2 * px + py], send_sem=send_sems.at[3 * a + j],
                                             recv_sem=recv_sems.at[3 * a + j], device_id=(px, py, c), device_id_type=_MESH).wait()
            local[a].wait()

    return pl.pallas_call(
        body, name=name,
        in_specs=[_ANY] * n, out_specs=[_ANY] * n,
        out_shape=[jax.ShapeDtypeStruct(s.shape, s.dtype) for s in pieces],
        scratch_shapes=[pltpu.SemaphoreType.DMA((3 * n,)), pltpu.SemaphoreType.DMA((3 * n,)), pltpu.SemaphoreType.DMA((n,))],
    )(*pieces)


def _swap_cores(blocks, name):
    n = len(blocks)
    parts = 4
    rows = [b.shape[0] // parts for b in blocks]

    def body(*refs):
        ins, outs = refs[:n], refs[n:2 * n]
        send_sems, recv_sems, local_sems = refs[2 * n:]
        x, y, c = _my_place()

        def remote(a, k, slot):
            rs = pl.ds(k * rows[a], rows[a])
            return pltpu.make_async_remote_copy(src_ref=ins[a].at[rs], dst_ref=outs[a].at[slot, rs], send_sem=send_sems.at[parts * a + k],
                                                recv_sem=recv_sems.at[parts * a + k], device_id=(x, y, 1 - c), device_id_type=_MESH)

        local = []
        for a in range(n):
            cp = pltpu.make_async_copy(ins[a], outs[a].at[c], local_sems.at[a])
            cp.start()
            local.append(cp)
            for k in range(parts):
                remote(a, k, c).start()
        for a in range(n):
            for k in range(parts):
                remote(a, k, 1 - c).wait()
            local[a].wait()

    return pl.pallas_call(
        body, name=name,
        in_specs=[_ANY] * n, out_specs=[_ANY] * n,
        out_shape=[jax.ShapeDtypeStruct((2,) + s.shape, s.dtype) for s in blocks],
        scratch_shapes=[pltpu.SemaphoreType.DMA((parts * n,)), pltpu.SemaphoreType.DMA((parts * n,)), pltpu.SemaphoreType.DMA((n,))],
    )(*blocks)


def _gather_all(block, name):
    def body(in_ref, out_ref, send_sems, recv_sems, local_sem):
        x, y, c = _my_place()
        me = 4 * x + 2 * y + c
        cp = pltpu.make_async_copy(in_ref, out_ref.at[me], local_sem)
        cp.start()
        peers = []
        for dx in range(2):
            for dy in range(2):
                for dc in range(2):
                    if dx or dy or dc:
                        peers.append((1 - x if dx else x, 1 - y if dy else y, 1 - c if dc else c))
        for j, pr in enumerate(peers):
            pltpu.make_async_remote_copy(src_ref=in_ref, dst_ref=out_ref.at[me], send_sem=send_sems.at[j], recv_sem=recv_sems.at[j],
                                         device_id=pr, device_id_type=_MESH).start()
        for j, (px, py, pc) in enumerate(peers):
            pltpu.make_async_remote_copy(src_ref=in_ref, dst_ref=out_ref.at[4 * px + 2 * py + pc], send_sem=send_sems.at[j], recv_sem=recv_sems.at[j],
                                         device_id=(px, py, pc), device_id_type=_MESH).wait()
        cp.wait()

    return pl.pallas_call(
        body, name=name,
        in_specs=[_ANY], out_specs=_ANY,
        out_shape=jax.ShapeDtypeStruct((8,) + block.shape, block.dtype),
        scratch_shapes=[pltpu.SemaphoreType.DMA((7,)), pltpu.SemaphoreType.DMA((7,)), pltpu.SemaphoreType.DMA],
    )(block)


def _row_tile(r, c):
    best = r
    for cand in range(16, r + 1, 16):
        if r % cand == 0 and cand * c * 4 <= (1 << 20):
            best = cand
    return best if best * c * 4 <= (4 << 20) else r


def _sum_slots(parts, name):
    n, r, c = parts.shape
    tr = _row_tile(r, c)

    def body(p_ref, o_ref):
        acc = p_ref[0].astype(f32)
        for s in range(1, n):
            acc = acc + p_ref[s].astype(f32)
        o_ref[...] = acc

    return pl.pallas_call(
        body, name=name, grid=(r // tr,),
        in_specs=[pl.BlockSpec((n, tr, c), lambda i: (0, i, 0))],
        out_specs=pl.BlockSpec((tr, c), lambda i: (i, 0)),
        out_shape=jax.ShapeDtypeStruct((r, c), f32),
        compiler_params=_cparams(1, arbitrary=False),
    )(parts)


def _adamw(parts, w, m, v, name):
    n, r, c = parts.shape
    tr = _row_tile(r, c)

    def body(p_ref, w_ref, m_ref, v_ref, g_ref, d_ref, nm_ref, nv_ref):
        g = p_ref[0]
        for s in range(1, n):
            g = g + p_ref[s]
        nm = B1 * m_ref[...] + (1.0 - B1) * g
        nv = B2 * v_ref[...] + (1.0 - B2) * (g * g)
        m_hat = nm / (1.0 - B1 ** STEP)
        v_hat = nv / (1.0 - B2 ** STEP)
        g_ref[...] = g
        nm_ref[...] = nm
        nv_ref[...] = nv
        d_ref[...] = -LR * (m_hat / (jnp.sqrt(v_hat) + EPS_ADAM) + WD * w_ref[...])

    blk = pl.BlockSpec((tr, c), lambda i: (i, 0))
    return pl.pallas_call(
        body, name=name, grid=(r // tr,),
        in_specs=[pl.BlockSpec((n, tr, c), lambda i: (0, i, 0)), blk, blk, blk],
        out_specs=[blk] * 4,
        out_shape=[jax.ShapeDtypeStruct((r, c), f32)] * 4,
        compiler_params=_cparams(1, arbitrary=False),
    )(parts, w, m, v)


_BIG = ["w_in", "w_out", "w_ffn_gate", "w_ffn_up", "w_ffn_down", "ple_w_proj", "ple_w_gate"]
_COL_SPLIT = {"w_in", "w_ffn_gate", "w_ffn_up", "ple_w_proj"}
_SMALL = ["b_in", "hg_lb_logits", "ml_conv_w", "ml_conv_b", "hg_norm_g", "ml_norm_g", "ln1_g", "ln1_b", "ln2_g", "ln2_b", "ple_b_gate"]
_ORDER = ["w_in", "b_in", "hg_lb_logits", "ml_conv_w", "ml_conv_b", "hg_norm_g", "ml_norm_g", "w_out", "ln1_g", "ln1_b",
          "w_ffn_gate", "w_ffn_up", "w_ffn_down", "ln2_g", "ln2_b", "ple_w_proj", "ple_w_gate", "ple_b_gate"]
_PACK_ROWS, _PACK_COLS = 16, 1024


def _pack(arrays):
    flat = jnp.concatenate([a.reshape(-1) for a in arrays])
    return jnp.pad(flat, (0, _PACK_ROWS * _PACK_COLS - flat.shape[0])).reshape(_PACK_ROWS, _PACK_COLS)


def _unpack(pack, shapes):
    flat = pack.reshape(-1)
    out, off = [], 0
    for s in shapes:
        size = 1
        for d in s:
            size *= d
        out.append(flat[off:off + size].reshape(s))
        off += size
    return out


def _to_chip_major(g, col_split):
    if col_split:
        k, n = g.shape
        return g.reshape(k, 4, n // 4).transpose(1, 0, 2)
    k, n = g.shape
    return g.reshape(4, k // 4, n)


def _from_chip_major(a, col_split):
    if col_split:
        return a.transpose(1, 0, 2).reshape(a.shape[1], 4 * a.shape[2])
    return a.reshape(4 * a.shape[1], a.shape[2])


def kernel(x, p, w_in, b_in, hg_lb_logits, ml_conv_w, ml_conv_b, hg_norm_g, ml_norm_g, w_out, ln1_g, ln1_b, w_ffn_gate, w_ffn_up, w_ffn_down, ln2_g, ln2_b, ple_w_proj, ple_w_gate, ple_b_gate, loss_target, m_w_in, m_b_in, m_hg_lb_logits, m_ml_conv_w, m_ml_conv_b, m_hg_norm_g, m_ml_norm_g, m_w_out, m_ln1_g, m_ln1_b, m_w_ffn_gate, m_w_ffn_up, m_w_ffn_down, m_ln2_g, m_ln2_b, m_ple_w_proj, m_ple_w_gate, m_ple_b_gate, v_w_in, v_b_in, v_hg_lb_logits, v_ml_conv_w, v_ml_conv_b, v_hg_norm_g, v_ml_norm_g, v_w_out, v_ln1_g, v_ln1_b, v_w_ffn_gate, v_w_ffn_up, v_w_ffn_down, v_ln2_g, v_ln2_b, v_ple_w_proj, v_ple_w_gate, v_ple_b_gate):
    args = dict(locals())
    wts = {k: args[k] for k in _ORDER}
    mom = {k: args["m_" + k] for k in _ORDER}
    var = {k: args["v_" + k] for k in _ORDER}
    two_d = lambda a: a.reshape(a.shape[-2], a.shape[-1])

    shards = [two_d(wts[k]).astype(bf16) for k in _BIG]
    gathered, taps = _allgather_weights(shards, two_d(ml_conv_w), "allgather_weights")
    full = {k: _from_chip_major(g, k in _COL_SPLIT) for k, g in zip(_BIG, gathered)}
    conv_w_full = _from_chip_major(taps, True)

    loss, grad_x, grads = _local_step(
        x[0], p[0, 0], loss_target[0], full["w_in"], b_in, hg_lb_logits, conv_w_full, ml_conv_b, hg_norm_g, ml_norm_g,
        full["w_out"], ln1_g, ln1_b, full["w_ffn_gate"], full["w_ffn_up"], full["w_ffn_down"], ln2_g, ln2_b,
        full["ple_w_proj"], full["ple_w_gate"], ple_b_gate)

    pieces = [_to_chip_major(grads[k], k in _COL_SPLIT) for k in _BIG]
    core_sums = [_pair_reduce(pc, "pair_reduce_" + k) for k, pc in zip(_BIG, pieces)]
    received = _scatter_chips(core_sums, "scatter_grads")
    both = [_chip_reduce_swap(r, "chip_reduce_" + k) for k, r in zip(_BIG, received)]
    out_g, out_d, out_m, out_v = {}, {}, {}, {}
    for k, parts in zip(_BIG, both):
        whole = parts.reshape(1, 2 * parts.shape[1], parts.shape[2])
        g, d, nm, nv = _adamw(whole, two_d(wts[k]), two_d(mom[k]), two_d(var[k]), "adamw_" + k)
        shp = wts[k].shape
        out_g[k], out_d[k], out_m[k], out_v[k] = g.reshape(shp), d.reshape(shp), nm.reshape(shp), nv.reshape(shp)

    small_shapes = [(1, PROJ_W), (2, MIX_W), (CONV_K, MIX_W)] + [(1, MIX_W)] * 3 + [(1, D_MODEL)] * 5 + [(1, 1)]
    contrib = _pack([grads[k] for k in _SMALL] + [loss])
    summed = _sum_slots(_gather_all(contrib, "gather_small"), "sum_small")
    small = _unpack(summed, small_shapes)
    loss_total = small[-1].reshape(())
    gsm = dict(zip(_SMALL, small[:-1]))
    place = 2 * lax.axis_index("x") + lax.axis_index("y")
    conv_cols = ml_conv_w.shape[-1]
    gsm["ml_conv_w"] = lax.dynamic_slice(gsm["ml_conv_w"], (0, place * conv_cols), (CONV_K, conv_cols))
    own_shapes = [wts[k].shape for k in _SMALL]
    g_pack = _pack([gsm[k] for k in _SMALL])
    res = _adamw(g_pack[None], _pack([wts[k] for k in _SMALL]), _pack([mom[k] for k in _SMALL]), _pack([var[k] for k in _SMALL]), "adamw_small")
    for dst, pack in zip((out_g, out_d, out_m, out_v), res):
        for k, a in zip(_SMALL, _unpack(pack, own_shapes)):
            dst[k] = a

    outs = [loss_total, grad_x[None]]
    for group in (out_g, out_d, out_m, out_v):
        outs += [group[k] for k in _ORDER]
    return tuple(outs)
```

```python
import functools

import jax
import jax.numpy as jnp
from jax import lax
from jax.experimental import pallas as pl
from jax.experimental.pallas import tpu as pltpu

f32 = jnp.float32
bf16 = jnp.bfloat16
HI = lax.Precision.HIGHEST

D_MODEL = 1024
HEADS = 4
HEAD_W = 128
MIX_W = HEADS * HEAD_W
ML_DQK = 64
PROJ_W = 3592
U_HG = 4 * MIX_W
U_ML = 3 * MIX_W + 128
D_FF = 2816
PLE = 256
CHUNK = 128
SUB = 16
EXP_CAP = 80.0
CONV_K = 4
HALO = 8
ALPHA = float(2.0 ** 0.25)
LN_EPS = 1e-5
RMS_EPS = 1e-6
NEG = -1e30
LR, B1, B2, EPS_ADAM, WD, STEP = 0.001, 0.9, 0.999, 1e-08, 0.01, 10
VMEM_LIMIT = 56 * 1024 * 1024
DENSE_ROWS = 512
WGRAD_ROWS = 2048


def _cparams(n_axes, arbitrary=True):
    sem = ("arbitrary",) * n_axes if arbitrary else ("parallel",) * n_axes
    return pltpu.CompilerParams(dimension_semantics=sem, vmem_limit_bytes=VMEM_LIMIT)


ACT = bf16


def _mx(a):
    return a.astype(ACT)


def _bdot(a, b):
    return jnp.dot(_mx(a), _mx(b), preferred_element_type=f32)


def _bdot_nt(a, b):
    return lax.dot_general(_mx(a), _mx(b), (((1,), (1,)), ((), ())), preferred_element_type=f32)


def _bdot_tn(a, b):
    return lax.dot_general(_mx(a), _mx(b), (((0,), (0,)), ((), ())), preferred_element_type=f32)


def _split3(x):
    hi = x.astype(bf16)
    r1 = x - hi.astype(f32)
    mid = r1.astype(bf16)
    lo = (r1 - mid.astype(f32)).astype(bf16)
    return hi, mid, lo


def _dot3(a, b, dims):
    a_hi = a.astype(bf16)
    a_lo = (a - a_hi.astype(f32)).astype(bf16)
    b_hi = b.astype(bf16)
    b_lo = (b - b_hi.astype(f32)).astype(bf16)
    dn = (dims, ((), ()))
    return (lax.dot_general(a_hi, b_hi, dn, preferred_element_type=f32) + lax.dot_general(a_hi, b_lo, dn, preferred_element_type=f32)
            + lax.dot_general(a_lo, b_hi, dn, preferred_element_type=f32))


def _sel_dot(sel, x):
    sb = sel.astype(bf16)
    return sum(jnp.dot(sb, part, preferred_element_type=f32) for part in _split3(x))


def _sel_dot_nt(sel, x):
    sb = sel.astype(bf16)
    return sum(lax.dot_general(sb, part, (((1,), (1,)), ((), ())), preferred_element_type=f32) for part in _split3(x))


def _sigmoid(x):
    return 1.0 / (1.0 + jnp.exp(-x))


def _log_sigmoid(x):
    return jnp.minimum(x, 0.0) - jnp.log(1.0 + jnp.exp(-jnp.abs(x)))


def _tri(n, upper=False):
    r = lax.broadcasted_iota(jnp.int32, (n, n), 0)
    c = lax.broadcasted_iota(jnp.int32, (n, n), 1)
    return (c >= r) if upper else (c <= r)


def _rows(tm, n, col=0):
    return pl.BlockSpec((tm, n), lambda i, _c=col: (i, _c))


def _rows_rev(tm, n, nb, col=0):
    return pl.BlockSpec((tm, n), lambda i, _c=col, _nb=nb: (_nb - 1 - i, _c))


def _const(shape):
    return pl.BlockSpec(shape, lambda i, _n=len(shape): (0,) * _n)


def _resident(shape):
    return pl.BlockSpec(shape, lambda i, _n=len(shape): (0,) * _n, pipeline_mode=pl.Buffered(1))


def _tile(t, want):
    return want if t % want == 0 else t


def _inproj(x, w_hg, w_ml, b_hg, b_ml):
    t = x.shape[0]
    tm = _tile(t, DENSE_ROWS)

    def body(x_ref, whg_ref, wml_ref, bhg_ref, bml_ref, uhg_ref, uml_ref, xb_ref):
        xb = _mx(x_ref[...])
        xb_ref[...] = xb
        uhg_ref[...] = _bdot(xb, whg_ref[...]) + bhg_ref[...]
        uml_ref[...] = _bdot(xb, wml_ref[...]) + bml_ref[...]

    return pl.pallas_call(
        body, name="inproj", grid=(t // tm,),
        in_specs=[_rows(tm, D_MODEL), _resident((D_MODEL, U_HG)), _resident((D_MODEL, U_ML)), _const((1, U_HG)), _const((1, U_ML))],
        out_specs=[_rows(tm, U_HG), _rows(tm, U_ML), _rows(tm, D_MODEL)],
        out_shape=[jax.ShapeDtypeStruct((t, U_HG), f32), jax.ShapeDtypeStruct((t, U_ML), f32), jax.ShapeDtypeStruct((t, D_MODEL), ACT)],
        compiler_params=_cparams(1, arbitrary=False),
    )(x, w_hg, w_ml, b_hg, b_ml)


def _hg_gates(hq, hf, lb, tri):
    s = _sigmoid(hf)
    om = 1.0 - lb
    f = lb + om * s
    g = jnp.log(f)
    k = om * (1.0 - s)
    sq = _sigmoid(hq)
    q = hq * sq
    b = _sel_dot(tri, g)
    return q, sq, s, f, k, b


def _hg_scores(q, k, b, tril_mask):
    qts, kts, eqs, eks, rows = [], [], [], [], []
    for i in range(CHUNK // SUB):
        lo = i * SUB
        ref = jnp.zeros_like(b[0:1]) if i == 0 else b[lo - 1:lo]
        eq = jnp.exp(b[lo:lo + SUB] - ref)
        ek = jnp.exp(jnp.minimum(ref - b, EXP_CAP))
        qt = q[lo:lo + SUB] * eq
        kt = k * ek
        rows.append(_bdot_nt(qt, kt))
        qts.append(qt); kts.append(kt); eqs.append(eq); eks.append(ek)
    a = jnp.where(tril_mask, jnp.concatenate(rows, axis=0), 0.0)
    return a, qts, kts, eqs, eks


def _head_rms(o, gn):
    rstd = lax.rsqrt(jnp.mean(o * o, axis=-1, keepdims=True) + RMS_EPS)
    oh = o * rstd
    return oh, rstd, oh * gn


def _lower_bound(logit_ref):
    lg = logit_ref[...]
    return _sigmoid(lg[0:1] - lg[1:2])


def _hgrn2_fwd(u_hg, logits, gn):
    t = u_hg.shape[0]
    tb = _tile(t, 256)
    nc_blk = tb // CHUNK

    def body(u_ref, lg_ref, gn_ref, og_ref, sst_ref, st_ref):
        @pl.when(pl.program_id(0) == 0)
        def _():
            st_ref[...] = jnp.zeros_like(st_ref)

        lb_all = _lower_bound(lg_ref)
        tril_mask = _tri(CHUNK)
        tri = tril_mask.astype(f32)

        def chunk(c, carry):
            r0 = pl.multiple_of(c * CHUNK, CHUNK)
            rows = pl.ds(r0, CHUNK)
            for h in range(HEADS):
                cs = slice(h * HEAD_W, (h + 1) * HEAD_W)
                hq = u_ref[rows, h * HEAD_W:(h + 1) * HEAD_W]
                hf = u_ref[rows, MIX_W + h * HEAD_W:MIX_W + (h + 1) * HEAD_W]
                hv = u_ref[rows, 2 * MIX_W + h * HEAD_W:2 * MIX_W + (h + 1) * HEAD_W]
                hgate = u_ref[rows, 3 * MIX_W + h * HEAD_W:3 * MIX_W + (h + 1) * HEAD_W]
                q, _, _, _, k, b = _hg_gates(hq, hf, lb_all[:, cs], tri)
                a, _, _, _, _ = _hg_scores(q, k, b, tril_mask)
                st = st_ref[h]
                sst_ref[c, h] = st
                bl = b[CHUNK - 1:CHUNK]
                o = _bdot(a, hv) + _bdot_nt(q * jnp.exp(b), st)
                st_ref[h] = st * jnp.exp(bl) + _bdot_tn(hv, k * jnp.exp(bl - b))
                _, _, y = _head_rms(o, gn_ref[:, cs])
                og_ref[rows, cs] = (y * (hgate * _sigmoid(hgate))).astype(ACT)
            return carry

        lax.fori_loop(0, nc_blk, chunk, 0)

    return pl.pallas_call(
        body, name="hgrn2_fwd", grid=(t // tb,),
        in_specs=[_rows(tb, U_HG), _const((2, MIX_W)), _const((1, MIX_W))],
        out_specs=[_rows(tb, MIX_W), pl.BlockSpec((nc_blk, HEADS, HEAD_W, HEAD_W), lambda i: (i, 0, 0, 0))],
        out_shape=[jax.ShapeDtypeStruct((t, MIX_W), ACT), jax.ShapeDtypeStruct((t // CHUNK, HEADS, HEAD_W, HEAD_W), f32)],
        scratch_shapes=[pltpu.VMEM((HEADS, HEAD_W, HEAD_W), f32)],
        compiler_params=_cparams(1),
    )(u_hg, logits, gn)


def _hgrn2_bwd(u_hg, logits, gn, sst, dog, riders=()):
    t = u_hg.shape[0]
    tb = _tile(t, 256)
    nb = t // tb
    nc_blk = tb // CHUNK
    nr = len(riders)

    def body(*refs):
        u_ref, lg_ref, gn_ref, sst_ref, dog_ref = refs[:5]
        ride_in = refs[5:5 + nr]
        du_ref, dlg_ref, dgn_ref = refs[5 + nr:8 + nr]
        ride_out = refs[8 + nr:8 + 2 * nr]
        dst_ref = refs[8 + 2 * nr]
        ride_sems = refs[9 + 2 * nr:]

        @pl.when(pl.program_id(0) == 0)
        def _():
            dst_ref[...] = jnp.zeros_like(dst_ref)
            dlg_ref[...] = jnp.zeros_like(dlg_ref)
            dgn_ref[...] = jnp.zeros_like(dgn_ref)
            if nr:
                _scatter_start(ride_in, ride_out, *ride_sems)

        lb_all = _lower_bound(lg_ref)
        tril_mask = _tri(CHUNK)
        tri = tril_mask.astype(f32)
        triu = _tri(CHUNK, upper=True).astype(f32)

        def chunk(j, carry):
            c = nc_blk - 1 - j
            r0 = pl.multiple_of(c * CHUNK, CHUNK)
            rows = pl.ds(r0, CHUNK)
            for h in range(HEADS):
                cs = slice(h * HEAD_W, (h + 1) * HEAD_W)
                hq = u_ref[rows, h * HEAD_W:(h + 1) * HEAD_W]
                hf = u_ref[rows, MIX_W + h * HEAD_W:MIX_W + (h + 1) * HEAD_W]
                hv = u_ref[rows, 2 * MIX_W + h * HEAD_W:2 * MIX_W + (h + 1) * HEAD_W]
                hgate = u_ref[rows, 3 * MIX_W + h * HEAD_W:3 * MIX_W + (h + 1) * HEAD_W]
                lb = lb_all[:, cs]
                gnh = gn_ref[:, cs]
                q, sq, s, f, k, b = _hg_gates(hq, hf, lb, tri)
                a, qts, kts, eqs, eks = _hg_scores(q, k, b, tril_mask)
                st = sst_ref[c, h]
                dst = dst_ref[h]
                bl = b[CHUNK - 1:CHUNK]
                eb = jnp.exp(b)
                qh = q * eb
                ekl = jnp.exp(bl - b)
                kh = k * ekl
                o = _bdot(a, hv) + _bdot_nt(qh, st)
                oh, rstd, y = _head_rms(o, gnh)
                sg = _sigmoid(hgate)
                dogh = dog_ref[rows, cs]
                dy = dogh * (hgate * sg)
                du_ref[rows, 3 * MIX_W + h * HEAD_W:3 * MIX_W + (h + 1) * HEAD_W] = (dogh * y * (sg * (1.0 + hgate * (1.0 - sg)))).astype(ACT)
                dgn_ref[:, cs] += jnp.sum(dy * oh, axis=0, keepdims=True)
                doh = dy * gnh
                do = rstd * (doh - oh * jnp.mean(doh * oh, axis=-1, keepdims=True))
                da = jnp.where(tril_mask, _bdot_nt(do, hv), 0.0)
                dv = _bdot_tn(a, do) + _bdot_nt(kh, dst)
                dq = _bdot(do, st) * eb
                dk = _bdot(hv, dst) * ekl
                d_last = jnp.sum(k * dk, axis=0, keepdims=True) + jnp.exp(bl) * jnp.sum(dst * st, axis=0, keepdims=True)
                dqs = []
                for i in range(CHUNK // SUB):
                    da_i = da[i * SUB:(i + 1) * SUB]
                    dqs.append(_dot3(da_i, kts[i], ((1,), (0,))) * eqs[i])
                    dk = dk + _dot3(da_i, qts[i], ((0,), (0,))) * eks[i]
                dq = dq + jnp.concatenate(dqs, axis=0)
                dst_ref[h] = dst * jnp.exp(bl) + _bdot_tn(do, qh)
                dg = _sel_dot(triu, q * dq - k * dk) + d_last
                df = dg / f
                dfk = df - dk
                du_ref[rows, h * HEAD_W:(h + 1) * HEAD_W] = (dq * (sq * (1.0 + hq * (1.0 - sq)))).astype(ACT)
                du_ref[rows, MIX_W + h * HEAD_W:MIX_W + (h + 1) * HEAD_W] = ((1.0 - lb) * dfk * s * (1.0 - s)).astype(ACT)
                du_ref[rows, 2 * MIX_W + h * HEAD_W:2 * MIX_W + (h + 1) * HEAD_W] = dv.astype(ACT)
                dlb = jnp.sum((1.0 - s) * dfk, axis=0, keepdims=True) * (lb * (1.0 - lb))
                dlg_ref[0:1, cs] += dlb
                dlg_ref[1:2, cs] -= dlb
            return carry

        lax.fori_loop(0, nc_blk, chunk, 0)

        if nr:
            @pl.when(pl.program_id(0) == nb - 1)
            def _():
                _scatter_wait(ride_in, ride_out, *ride_sems)

    hbm = pl.BlockSpec(memory_space=pltpu.HBM)
    ride_scratch = [pltpu.SemaphoreType.DMA((3 * nr,)), pltpu.SemaphoreType.DMA((3 * nr,)), pltpu.SemaphoreType.DMA((nr,))] if nr else []
    return pl.pallas_call(
        body, name="hgrn2_bwd", grid=(nb,),
        in_specs=[_rows_rev(tb, U_HG, nb), _const((2, MIX_W)), _const((1, MIX_W)),
                  pl.BlockSpec((nc_blk, HEADS, HEAD_W, HEAD_W), lambda i: (nb - 1 - i, 0, 0, 0)), _rows_rev(tb, MIX_W, nb)] + [hbm] * nr,
        out_specs=[_rows_rev(tb, U_HG, nb), _const((2, MIX_W)), _const((1, MIX_W))] + [hbm] * nr,
        out_shape=[jax.ShapeDtypeStruct((t, U_HG), ACT), jax.ShapeDtypeStruct((2, MIX_W), f32), jax.ShapeDtypeStruct((1, MIX_W), f32)]
        + [jax.ShapeDtypeStruct(r.shape, r.dtype) for r in riders],
        scratch_shapes=[pltpu.VMEM((HEADS, HEAD_W, HEAD_W), f32)] + ride_scratch,
        compiler_params=_cparams(1),
    )(u_hg, logits, gn, sst, dog, *riders)


def _conv_fwd(u_ml, w, b):
    t = u_ml.shape[0]
    tm = _tile(t, 512)

    def body(x_ref, w_ref, b_ref, pre_ref, act_ref, xbuf):
        @pl.when(pl.program_id(0) == 0)
        def _():
            xbuf[...] = jnp.zeros_like(xbuf)

        xbuf[0:HALO, :] = xbuf[tm:tm + HALO, :]
        xbuf[HALO:HALO + tm, :] = x_ref[...]
        pre = b_ref[...] + jnp.zeros((tm, MIX_W), f32)
        for kk in range(CONV_K):
            off = HALO - (CONV_K - 1) + kk
            pre = pre + w_ref[kk:kk + 1, :] * xbuf[off:off + tm, :]
        pre_ref[...] = pre
        act_ref[...] = pre * _sigmoid(pre)

    return pl.pallas_call(
        body, name="conv_fwd", grid=(t // tm,),
        in_specs=[_rows(tm, MIX_W), _const((CONV_K, MIX_W)), _const((1, MIX_W))],
        out_specs=[_rows(tm, MIX_W), _rows(tm, MIX_W)],
        out_shape=[jax.ShapeDtypeStruct((t, MIX_W), f32)] * 2,
        scratch_shapes=[pltpu.VMEM((tm + HALO, MIX_W), f32)],
        compiler_params=_cparams(1),
    )(u_ml, w, b)


def _conv_bwd(u_ml, w, pre, dact):
    t = u_ml.shape[0]
    tm = _tile(t, 512)
    nb = t // tm
    hb = tm // HALO

    def body(x_ref, halo_ref, w_ref, pre_ref, dact_ref, dx_ref, dw_ref, db_ref, dbuf, xbuf):
        i = pl.program_id(0)

        @pl.when(i == 0)
        def _():
            dbuf[...] = jnp.zeros_like(dbuf)
            dw_ref[...] = jnp.zeros_like(dw_ref)
            db_ref[...] = jnp.zeros_like(db_ref)

        p = pre_ref[...]
        sg = _sigmoid(p)
        dpre = dact_ref[...] * (sg * (1.0 + p * (1.0 - sg)))
        dbuf[tm:tm + HALO, :] = dbuf[0:HALO, :]
        dbuf[0:tm, :] = dpre
        has_prev = (i < nb - 1).astype(f32)
        xbuf[0:HALO, :] = halo_ref[...] * has_prev
        xbuf[HALO:HALO + tm, :] = x_ref[...]
        dx = jnp.zeros((tm, MIX_W), f32)
        for kk in range(CONV_K):
            back = CONV_K - 1 - kk
            dx = dx + w_ref[kk:kk + 1, :] * dbuf[back:back + tm, :]
            off = HALO - (CONV_K - 1) + kk
            dw_ref[kk:kk + 1, :] += jnp.sum(dpre * xbuf[off:off + tm, :], axis=0, keepdims=True)
        dx_ref[...] = dx.astype(ACT)
        db_ref[...] += jnp.sum(dpre, axis=0, keepdims=True)

    return pl.pallas_call(
        body, name="conv_bwd", grid=(nb,),
        in_specs=[_rows_rev(tm, MIX_W, nb),
                  pl.BlockSpec((HALO, MIX_W), lambda i: (jnp.maximum((nb - 1 - i) * hb - 1, 0), 0)),
                  _const((CONV_K, MIX_W)), _rows_rev(tm, MIX_W, nb), _rows_rev(tm, MIX_W, nb)],
        out_specs=[_rows_rev(tm, MIX_W, nb), _const((CONV_K, MIX_W)), _const((1, MIX_W))],
        out_shape=[jax.ShapeDtypeStruct((t, MIX_W), ACT), jax.ShapeDtypeStruct((CONV_K, MIX_W), f32), jax.ShapeDtypeStruct((1, MIX_W), f32)],
        scratch_shapes=[pltpu.VMEM((tm + HALO, MIX_W), f32), pltpu.VMEM((tm + HALO, MIX_W), f32)],
        compiler_params=_cparams(1),
    )(u_ml, u_ml, w, pre, dact)


def _lane_pick(x, lane):
    idx = lax.broadcasted_iota(jnp.int32, x.shape, 1)
    return jnp.sum(jnp.where(idx == lane, x, 0.0), axis=-1, keepdims=True)


def _ml_gate_forms(gates, tri):
    lf = _log_sigmoid(gates)
    gc = _sel_dot(tri, lf)
    lane = lax.broadcasted_iota(jnp.int32, gates.shape, 1)
    mixed = jnp.where(lane < HEADS, gates, gc)
    sel = (lax.broadcasted_iota(jnp.int32, (8, 128), 0) == lax.broadcasted_iota(jnp.int32, (8, 128), 1)).astype(f32)
    rowsf = _sel_dot_nt(sel, mixed)
    return gc, rowsf


def _ml_chunk(q, k, v, gates, gc, rowsf, h, c_st, n_st, m_st, tril_mask):
    g_col = _lane_pick(gc, HEADS + h)
    ig_col = _lane_pick(gates, h)
    ig_row = rowsf[h:h + 1, :]
    g_row = rowsf[HEADS + h:HEADS + h + 1, :]
    dmat = jnp.where(tril_mask, g_col - g_row + ig_row, NEG)
    m_inter = g_col + m_st
    m_t = jnp.maximum(m_inter, jnp.max(dmat, axis=-1, keepdims=True))
    wi = jnp.exp(dmat - m_t)
    wo = jnp.exp(m_inter - m_t)
    qk = _bdot_nt(q, k) * wi
    num = _bdot(qk, v) + wo * _bdot(q, c_st)
    den = jnp.sum(qk, axis=-1, keepdims=True) + wo * jnp.sum(q * n_st, axis=-1, keepdims=True)
    floor = jnp.exp(-m_t)
    z = jnp.maximum(jnp.abs(den), floor)
    g_last = g_col[CHUNK - 1:CHUNK]
    a_col = g_last - g_col + ig_col
    m_new = jnp.maximum(g_last + m_st, jnp.max(a_col, axis=0, keepdims=True))
    ws = jnp.exp(a_col - m_new)
    w_old = jnp.exp(g_last + m_st - m_new)
    return dict(wi=wi, wo=wo, qk=qk, num=num, den=den, z=z, floor=floor, ws=ws, w_old=w_old, m_new=m_new)


def _mlstm_fwd(qkc, u_ml, gn):
    t = qkc.shape[0]
    tb = _tile(t, 256)
    nc_blk = tb // CHUNK

    def body(qk_ref, v_ref, mo_ref, gt_ref, gn_ref, og_ref, cst_ref, nst_ref, mst_ref, c_sc, n_sc, m_sc):
        @pl.when(pl.program_id(0) == 0)
        def _():
            c_sc[...] = jnp.zeros_like(c_sc)
            n_sc[...] = jnp.zeros_like(n_sc)
            m_sc[...] = jnp.zeros_like(m_sc)

        tril_mask = _tri(CHUNK)
        tri = tril_mask.astype(f32)

        def chunk(c, carry):
            r0 = pl.multiple_of(c * CHUNK, CHUNK)
            rows = pl.ds(r0, CHUNK)
            gates = gt_ref[rows, :]
            gc, rowsf = _ml_gate_forms(gates, tri)
            for h in range(HEADS):
                cs = slice(h * HEAD_W, (h + 1) * HEAD_W)
                q = qk_ref[rows, h * ML_DQK:(h + 1) * ML_DQK] * (ML_DQK ** -0.5)
                k = qk_ref[rows, HEADS * ML_DQK + h * ML_DQK:HEADS * ML_DQK + (h + 1) * ML_DQK]
                v = v_ref[rows, h * HEAD_W:(h + 1) * HEAD_W]
                mo = mo_ref[rows, h * HEAD_W:(h + 1) * HEAD_W]
                c_st = c_sc[h]
                n_st = n_sc[h]
                m_st = m_sc[h][:, 0:1]
                cst_ref[c, h] = c_st
                nst_ref[c, h] = n_st
                mst_ref[c, h] = m_sc[h]
                r = _ml_chunk(q, k, v, gates, gc, rowsf, h, c_st, n_st, m_st, tril_mask)
                hh = r["num"] / r["z"]
                ksc = k * r["ws"]
                c_sc[h] = r["w_old"] * c_st + _bdot_tn(ksc, v)
                n_sc[h] = r["w_old"] * n_st + jnp.sum(ksc, axis=0, keepdims=True)
                m_sc[h] = r["m_new"] + jnp.zeros((1, 128), f32)
                _, _, y = _head_rms(hh, gn_ref[:, cs])
                og_ref[rows, cs] = (y * _sigmoid(mo)).astype(ACT)
            return carry

        lax.fori_loop(0, nc_blk, chunk, 0)

    nchunks = t // CHUNK
    return pl.pallas_call(
        body, name="mlstm_fwd", grid=(t // tb,),
        in_specs=[_rows(tb, MIX_W), _rows(tb, MIX_W, 1), _rows(tb, MIX_W, 2), _rows(tb, 128, 12), _const((1, MIX_W))],
        out_specs=[_rows(tb, MIX_W),
                   pl.BlockSpec((nc_blk, HEADS, ML_DQK, HEAD_W), lambda i: (i, 0, 0, 0)),
                   pl.BlockSpec((nc_blk, HEADS, 1, ML_DQK), lambda i: (i, 0, 0, 0)),
                   pl.BlockSpec((nc_blk, HEADS, 1, 128), lambda i: (i, 0, 0, 0))],
        out_shape=[jax.ShapeDtypeStruct((t, MIX_W), ACT),
                   jax.ShapeDtypeStruct((nchunks, HEADS, ML_DQK, HEAD_W), f32),
                   jax.ShapeDtypeStruct((nchunks, HEADS, 1, ML_DQK), f32),
                   jax.ShapeDtypeStruct((nchunks, HEADS, 1, 128), f32)],
        scratch_shapes=[pltpu.VMEM((HEADS, ML_DQK, HEAD_W), f32), pltpu.VMEM((HEADS, 1, ML_DQK), f32), pltpu.VMEM((HEADS, 1, 128), f32)],
        compiler_params=_cparams(1),
    )(qkc, u_ml, u_ml, u_ml, gn)


def _mlstm_bwd(qkc, u_ml, gn, cst, nst, mst, dog):
    t = qkc.shape[0]
    tb = _tile(t, 256)
    nb = t // tb
    nc_blk = tb // CHUNK

    def body(qk_ref, v_ref, mo_ref, gt_ref, gn_ref, cst_ref, nst_ref, mst_ref, dog_ref,
             dqk_ref, dv_ref, dmo_ref, dgt_ref, dgn_ref, dc_sc, dn_sc):
        @pl.when(pl.program_id(0) == 0)
        def _():
            dc_sc[...] = jnp.zeros_like(dc_sc)
            dn_sc[...] = jnp.zeros_like(dn_sc)
            dgn_ref[...] = jnp.zeros_like(dgn_ref)

        tril_mask = _tri(CHUNK)
        tri = tril_mask.astype(f32)
        triu = _tri(CHUNK, upper=True).astype(f32)
        lane = lax.broadcasted_iota(jnp.int32, (CHUNK, 128), 1)

        def chunk(j, carry):
            c = nc_blk - 1 - j
            r0 = pl.multiple_of(c * CHUNK, CHUNK)
            rows = pl.ds(r0, CHUNK)
            gates = gt_ref[rows, :]
            gc, rowsf = _ml_gate_forms(gates, tri)
            dg_mat = jnp.zeros((CHUNK, 128), f32)
            dig_mat = jnp.zeros((CHUNK, 128), f32)
            dlast_row = jnp.zeros((1, 128), f32)
            for h in range(HEADS):
                cs = slice(h * HEAD_W, (h + 1) * HEAD_W)
                q = qk_ref[rows, h * ML_DQK:(h + 1) * ML_DQK] * (ML_DQK ** -0.5)
                k = qk_ref[rows, HEADS * ML_DQK + h * ML_DQK:HEADS * ML_DQK + (h + 1) * ML_DQK]
                v = v_ref[rows, h * HEAD_W:(h + 1) * HEAD_W]
                mo = mo_ref[rows, h * HEAD_W:(h + 1) * HEAD_W]
                gnh = gn_ref[:, cs]
                c_st = cst_ref[c, h]
                n_st = nst_ref[c, h]
                m_st = mst_ref[c, h][:, 0:1]
                dc = dc_sc[h]
                dn = dn_sc[h]
                r = _ml_chunk(q, k, v, gates, gc, rowsf, h, c_st, n_st, m_st, tril_mask)
                z = r["z"]
                hh = r["num"] / z
                oh, rstd, y = _head_rms(hh, gnh)
                sg = _sigmoid(mo)
                dogh = dog_ref[rows, cs]
                dy = dogh * sg
                dmo_ref[rows, cs] = (dogh * y * (sg * (1.0 - sg))).astype(ACT)
                dgn_ref[:, cs] += jnp.sum(dy * oh, axis=0, keepdims=True)
                doh = dy * gnh
                dh = rstd * (doh - oh * jnp.mean(doh * oh, axis=-1, keepdims=True))
                dnum = dh / z
                dz = -jnp.sum(dh * hh, axis=-1, keepdims=True) / z
                den = r["den"]
                dden = jnp.where(jnp.abs(den) > r["floor"], dz * jnp.sign(den), 0.0)
                dsw = (_bdot_nt(dnum, v) + dden) * r["wi"]
                wo = r["wo"]
                ws = r["ws"]
                dq = _bdot(dsw, k) + wo * (_bdot_nt(dnum, c_st) + dden * n_st)
                dk_state = ws * (_bdot_nt(v, dc) + dn)
                dk = _bdot_tn(dsw, q) + dk_state
                dv_ref[rows, cs] = (_bdot_tn(r["qk"], dnum) + ws * _bdot(k, dc)).astype(ACT)
                woq = wo * q
                w_old = r["w_old"]
                dc_sc[h] = w_old * dc + _bdot_tn(woq, dnum)
                dn_sc[h] = w_old * dn + jnp.sum(woq * dden, axis=0, keepdims=True)
                d_last = (jnp.sum(jnp.sum(k * dk_state, axis=-1, keepdims=True), axis=0, keepdims=True)
                          + w_old * (jnp.sum(jnp.sum(dc * c_st, axis=-1, keepdims=True), axis=0, keepdims=True)
                                     + jnp.sum(dn * n_st, axis=-1, keepdims=True)))
                kdk = jnp.sum(k * dk, axis=-1, keepdims=True)
                qdq = jnp.sum(q * dq, axis=-1, keepdims=True)
                dg_mat = dg_mat + jnp.where(lane == HEADS + h, qdq - kdk, 0.0)
                dlast_row = dlast_row + jnp.where(lane[0:1] == HEADS + h, d_last, 0.0)
                dig_mat = dig_mat + jnp.where(lane == h, kdk, 0.0)
                dqk_ref[rows, h * ML_DQK:(h + 1) * ML_DQK] = dq * (ML_DQK ** -0.5)
                dqk_ref[rows, HEADS * ML_DQK + h * ML_DQK:HEADS * ML_DQK + (h + 1) * ML_DQK] = dk
            dlf = _sel_dot(triu, dg_mat) + dlast_row
            dgt_ref[rows, :] = (dig_mat + dlf * _sigmoid(-gates)).astype(ACT)
            return carry

        lax.fori_loop(0, nc_blk, chunk, 0)

    st4 = lambda a, b: pl.BlockSpec((nc_blk, HEADS, a, b), lambda i: (nb - 1 - i, 0, 0, 0))
    return pl.pallas_call(
        body, name="mlstm_bwd", grid=(nb,),
        in_specs=[_rows_rev(tb, MIX_W, nb), _rows_rev(tb, MIX_W, nb, 1), _rows_rev(tb, MIX_W, nb, 2), _rows_rev(tb, 128, nb, 12),
                  _const((1, MIX_W)), st4(ML_DQK, HEAD_W), st4(1, ML_DQK), st4(1, 128), _rows_rev(tb, MIX_W, nb)],
        out_specs=[_rows_rev(tb, MIX_W, nb), _rows_rev(tb, MIX_W, nb), _rows_rev(tb, MIX_W, nb), _rows_rev(tb, 128, nb), _const((1, MIX_W))],
        out_shape=[jax.ShapeDtypeStruct((t, MIX_W), f32), jax.ShapeDtypeStruct((t, MIX_W), ACT), jax.ShapeDtypeStruct((t, MIX_W), ACT),
                   jax.ShapeDtypeStruct((t, 128), ACT), jax.ShapeDtypeStruct((1, MIX_W), f32)],
        scratch_shapes=[pltpu.VMEM((HEADS, ML_DQK, HEAD_W), f32), pltpu.VMEM((HEADS, 1, ML_DQK), f32)],
        compiler_params=_cparams(1),
    )(qkc, u_ml, u_ml, u_ml, gn, cst, nst, mst, dog)


def _ln_fwd(r, g, b):
    mu = jnp.mean(r, axis=-1, keepdims=True)
    xc = r - mu
    rstd = lax.rsqrt(jnp.mean(xc * xc, axis=-1, keepdims=True) + LN_EPS)
    xh = xc * rstd
    return xh * g + b, xh, rstd


def _ln_bwd(dy, xh, rstd, g):
    dxh = dy * g
    return rstd * (dxh - jnp.mean(dxh, axis=-1, keepdims=True) - xh * jnp.mean(dxh * xh, axis=-1, keepdims=True))


def _outproj_ln1(og_hg, og_ml, x, w_out, g, b):
    t = x.shape[0]
    tm = _tile(t, DENSE_ROWS)

    def body(a_ref, b_ref, x_ref, w_ref, g_ref, bb_ref, x1_ref, xh_ref, rs_ref, x1b_ref):
        mix = _bdot(a_ref[...], w_ref[0:MIX_W, :]) + _bdot(b_ref[...], w_ref[MIX_W:2 * MIX_W, :])
        y, xh, rstd = _ln_fwd(ALPHA * x_ref[...] + mix, g_ref[...], bb_ref[...])
        x1_ref[...] = y
        x1b_ref[...] = y.astype(ACT)
        xh_ref[...] = xh
        rs_ref[...] = rstd

    return pl.pallas_call(
        body, name="outproj_ln1", grid=(t // tm,),
        in_specs=[_rows(tm, MIX_W), _rows(tm, MIX_W), _rows(tm, D_MODEL), _resident((D_MODEL, D_MODEL)), _const((1, D_MODEL)), _const((1, D_MODEL))],
        out_specs=[_rows(tm, D_MODEL), _rows(tm, D_MODEL), _rows(tm, 1), _rows(tm, D_MODEL)],
        out_shape=[jax.ShapeDtypeStruct((t, D_MODEL), f32), jax.ShapeDtypeStruct((t, D_MODEL), f32), jax.ShapeDtypeStruct((t, 1), f32),
                   jax.ShapeDtypeStruct((t, D_MODEL), ACT)],
        compiler_params=_cparams(1, arbitrary=False),
    )(og_hg, og_ml, x, w_out, g, b)


def _ffn_up(x1, wg, wu):
    t = x1.shape[0]
    tm = _tile(t, DENSE_ROWS)

    def body(x_ref, wg_ref, wu_ref, hg_ref, up_ref, a_ref):
        xv = x_ref[...]
        hg = _bdot(xv, wg_ref[...])
        up = _bdot(xv, wu_ref[...])
        hg_ref[...] = hg
        up_ref[...] = up
        a_ref[...] = (hg * _sigmoid(hg) * up).astype(ACT)

    return pl.pallas_call(
        body, name="ffn_up", grid=(t // tm,),
        in_specs=[_rows(tm, D_MODEL), _resident((D_MODEL, D_FF)), _resident((D_MODEL, D_FF))],
        out_specs=[_rows(tm, D_FF), _rows(tm, D_FF), _rows(tm, D_FF)],
        out_shape=[jax.ShapeDtypeStruct((t, D_FF), f32), jax.ShapeDtypeStruct((t, D_FF), f32), jax.ShapeDtypeStruct((t, D_FF), ACT)],
        compiler_params=_cparams(1, arbitrary=False),
    )(x1, wg, wu)


def _ffn_down_ln2(a, x1, wd, g, b):
    t = x1.shape[0]
    tm = _tile(t, DENSE_ROWS)

    def body(a_ref, x_ref, w_ref, g_ref, bb_ref, x2_ref, xh_ref, rs_ref, x2b_ref):
        ffn = _bdot(a_ref[...], w_ref[...])
        y, xh, rstd = _ln_fwd(ALPHA * x_ref[...] + ffn, g_ref[...], bb_ref[...])
        x2_ref[...] = y
        x2b_ref[...] = y.astype(ACT)
        xh_ref[...] = xh
        rs_ref[...] = rstd

    return pl.pallas_call(
        body, name="ffn_down_ln2", grid=(t // tm,),
        in_specs=[_rows(tm, D_FF), _rows(tm, D_MODEL), _resident((D_FF, D_MODEL)), _const((1, D_MODEL)), _const((1, D_MODEL))],
        out_specs=[_rows(tm, D_MODEL), _rows(tm, D_MODEL), _rows(tm, 1), _rows(tm, D_MODEL)],
        out_shape=[jax.ShapeDtypeStruct((t, D_MODEL), f32), jax.ShapeDtypeStruct((t, D_MODEL), f32), jax.ShapeDtypeStruct((t, 1), f32),
                   jax.ShapeDtypeStruct((t, D_MODEL), ACT)],
        compiler_params=_cparams(1, arbitrary=False),
    )(a, x1, wd, g, b)


def _head_loss_bwd(x2, xh2, rs2, p, tgt, w_pg, b_pg, w_pp, g2):
    t = x2.shape[0]
    tm = _tile(t, DENSE_ROWS)

    def body(x_ref, xh_ref, rs_ref, p_ref, t_ref, wg_ref, bg_ref, wp_ref, g_ref,
             dr_ref, de_ref, dz_ref, loss_ref, dbg_ref, dg2_ref, db2_ref):
        @pl.when(pl.program_id(0) == 0)
        def _():
            loss_ref[...] = jnp.zeros_like(loss_ref)
            dbg_ref[...] = jnp.zeros_like(dbg_ref)
            dg2_ref[...] = jnp.zeros_like(dg2_ref)
            db2_ref[...] = jnp.zeros_like(db2_ref)

        x2v = x_ref[...]
        z = _bdot(x2v, wg_ref[...]) + bg_ref[...]
        e = _bdot(p_ref[...], wp_ref[...])
        sg = _sigmoid(z)
        diff = x2v + sg * e - t_ref[...]
        loss_ref[...] += 0.5 * jnp.sum(jnp.mean(diff * diff, axis=-1, keepdims=True), axis=0, keepdims=True)
        dy = diff * (1.0 / D_MODEL)
        de_ref[...] = (dy * sg).astype(ACT)
        dz = dy * e * (sg * (1.0 - sg))
        dz_ref[...] = dz.astype(ACT)
        dbg_ref[...] += jnp.sum(dz, axis=0, keepdims=True)
        dx2 = dy + _bdot_nt(dz, wg_ref[...])
        xh = xh_ref[...]
        dg2_ref[...] += jnp.sum(dx2 * xh, axis=0, keepdims=True)
        db2_ref[...] += jnp.sum(dx2, axis=0, keepdims=True)
        dr_ref[...] = _ln_bwd(dx2, xh, rs_ref[...], g_ref[...])

    row = jax.ShapeDtypeStruct((1, D_MODEL), f32)
    return pl.pallas_call(
        body, name="head_loss_bwd", grid=(t // tm,),
        in_specs=[_rows(tm, D_MODEL), _rows(tm, D_MODEL), _rows(tm, 1), _rows(tm, PLE), _rows(tm, D_MODEL),
                  _resident((D_MODEL, D_MODEL)), _const((1, D_MODEL)), _resident((PLE, D_MODEL)), _const((1, D_MODEL))],
        out_specs=[_rows(tm, D_MODEL), _rows(tm, D_MODEL), _rows(tm, D_MODEL), _const((1, 1)), _const((1, D_MODEL)), _const((1, D_MODEL)), _const((1, D_MODEL))],
        out_shape=[jax.ShapeDtypeStruct((t, D_MODEL), f32), jax.ShapeDtypeStruct((t, D_MODEL), ACT), jax.ShapeDtypeStruct((t, D_MODEL), ACT),
                   jax.ShapeDtypeStruct((1, 1), f32), row, row, row],
        compiler_params=_cparams(1),
    )(x2, xh2, rs2, p, tgt, w_pg, b_pg, w_pp, g2)


def _ffn_bwd(dr2, hg, up, xh1, rs1, wd, wg, wu, g1):
    t = dr2.shape[0]
    tm = _tile(t, DENSE_ROWS // 2)

    def body(dr_ref, hg_ref, up_ref, xh_ref, rs_ref, wd_ref, wg_ref, wu_ref, g_ref,
             dr1_ref, dhg_ref, dup_ref, dg1_ref, db1_ref):
        @pl.when(pl.program_id(0) == 0)
        def _():
            dg1_ref[...] = jnp.zeros_like(dg1_ref)
            db1_ref[...] = jnp.zeros_like(db1_ref)

        dr2v = dr_ref[...]
        da = _bdot_nt(dr2v, wd_ref[...])
        hgv = hg_ref[...]
        sg = _sigmoid(hgv)
        dhg = da * up_ref[...] * (sg * (1.0 + hgv * (1.0 - sg)))
        dup = da * (hgv * sg)
        dhg_ref[...] = dhg.astype(ACT)
        dup_ref[...] = dup.astype(ACT)
        dx1 = ALPHA * dr2v + _bdot_nt(dhg, wg_ref[...]) + _bdot_nt(dup, wu_ref[...])
        xh = xh_ref[...]
        dg1_ref[...] += jnp.sum(dx1 * xh, axis=0, keepdims=True)
        db1_ref[...] += jnp.sum(dx1, axis=0, keepdims=True)
        dr1_ref[...] = _ln_bwd(dx1, xh, rs_ref[...], g_ref[...])

    row = jax.ShapeDtypeStruct((1, D_MODEL), f32)
    return pl.pallas_call(
        body, name="ffn_bwd", grid=(t // tm,),
        in_specs=[_rows(tm, D_MODEL), _rows(tm, D_FF), _rows(tm, D_FF), _rows(tm, D_MODEL), _rows(tm, 1),
                  _resident((D_FF, D_MODEL)), _resident((D_MODEL, D_FF)), _resident((D_MODEL, D_FF)), _const((1, D_MODEL))],
        out_specs=[_rows(tm, D_MODEL), _rows(tm, D_FF), _rows(tm, D_FF), _const((1, D_MODEL)), _const((1, D_MODEL))],
        out_shape=[jax.ShapeDtypeStruct((t, D_MODEL), f32), jax.ShapeDtypeStruct((t, D_FF), ACT), jax.ShapeDtypeStruct((t, D_FF), ACT), row, row],
        compiler_params=_cparams(1),
    )(dr2, hg, up, xh1, rs1, wd, wg, wu, g1)


def _outproj_bwd(dr1, w_out):
    t = dr1.shape[0]
    tm = _tile(t, DENSE_ROWS)

    def body(dr_ref, w_ref, dhg_ref, dml_ref):
        d = _bdot_nt(dr_ref[...], w_ref[...])
        dhg_ref[...] = d[:, 0:MIX_W]
        dml_ref[...] = d[:, MIX_W:2 * MIX_W]

    return pl.pallas_call(
        body, name="outproj_bwd", grid=(t // tm,),
        in_specs=[_rows(tm, D_MODEL), _resident((D_MODEL, D_MODEL))],
        out_specs=[_rows(tm, MIX_W), _rows(tm, MIX_W)],
        out_shape=[jax.ShapeDtypeStruct((t, MIX_W), f32)] * 2,
        compiler_params=_cparams(1, arbitrary=False),
    )(dr1, w_out)


def _inproj_bwd(dr1, du_hg, dqk, dmv, dmo, dgt, w_hg, w_ml):
    t = dr1.shape[0]
    tm = _tile(t, DENSE_ROWS)

    def body(dr_ref, dhg_ref, dqk_ref, dmv_ref, dmo_ref, dgt_ref, whg_ref, wml_ref, gx_ref, dml_ref):
        dml = jnp.concatenate([dqk_ref[...], dmv_ref[...], dmo_ref[...], dgt_ref[...]], axis=-1).astype(ACT)
        dml_ref[...] = dml
        gx_ref[...] = ALPHA * dr_ref[...] + _bdot_nt(dhg_ref[...], whg_ref[...]) + _bdot_nt(dml, wml_ref[...])

    return pl.pallas_call(
        body, name="inproj_bwd", grid=(t // tm,),
        in_specs=[_rows(tm, D_MODEL), _rows(tm, U_HG), _rows(tm, MIX_W), _rows(tm, MIX_W), _rows(tm, MIX_W), _rows(tm, 128),
                  _resident((D_MODEL, U_HG)), _resident((D_MODEL, U_ML))],
        out_specs=[_rows(tm, D_MODEL), _rows(tm, U_ML)],
        out_shape=[jax.ShapeDtypeStruct((t, D_MODEL), f32), jax.ShapeDtypeStruct((t, U_ML), ACT)],
        compiler_params=_cparams(1, arbitrary=False),
    )(dr1, du_hg, dqk, dmv, dmo, dgt, w_hg, w_ml)


def _wgrad(a, b, name, tk=None, tn=None):
    t, kdim = a.shape
    n = b.shape[1]
    tk = tk or kdim
    tn = tn or n
    tt = _tile(t, WGRAD_ROWS)

    def body(a_ref, b_ref, o_ref):
        @pl.when(pl.program_id(2) == 0)
        def _():
            o_ref[...] = jnp.zeros_like(o_ref)

        o_ref[...] += _bdot_tn(a_ref[...], b_ref[...])

    return pl.pallas_call(
        body, name=name, grid=(kdim // tk, n // tn, t // tt),
        in_specs=[pl.BlockSpec((tt, tk), lambda i, j, s: (s, i)), pl.BlockSpec((tt, tn), lambda i, j, s: (s, j))],
        out_specs=pl.BlockSpec((tk, tn), lambda i, j, s: (i, j)),
        out_shape=jax.ShapeDtypeStruct((kdim, n), f32),
        compiler_params=_cparams(3),
    )(a, b)


def _colsum(parts, name):
    t = parts[0].shape[0]
    tt = _tile(t, 512)
    widths = [a.shape[1] for a in parts]

    def body(*refs):
        o_ref = refs[-1]

        @pl.when(pl.program_id(0) == 0)
        def _():
            o_ref[...] = jnp.zeros_like(o_ref)

        off = 0
        for r, w in zip(refs[:-1], widths):
            o_ref[:, off:off + w] += jnp.sum(r[...].astype(f32), axis=0, keepdims=True)
            off += w

    return pl.pallas_call(
        body, name=name, grid=(t // tt,),
        in_specs=[_rows(tt, w) for w in widths],
        out_specs=_const((1, sum(widths))),
        out_shape=jax.ShapeDtypeStruct((1, sum(widths)), f32),
        compiler_params=_cparams(1),
    )(*parts)


def _local_step(x, p, tgt, w_in_b, b_in, logits, conv_w, conv_b, hg_gn, ml_gn, w_out_b, ln1_g, ln1_b,
                wg_b, wu_b, wd_b, ln2_g, ln2_b, w_pp_b, w_pg_b, b_pg, early_hook=None):
    pad_w = U_HG + U_ML - PROJ_W
    w_hg = w_in_b[:, :U_HG]
    w_ml = jnp.pad(w_in_b[:, U_HG:], ((0, 0), (0, pad_w)))
    bb_hg = b_in[:, :U_HG]
    bb_ml = jnp.pad(b_in[:, U_HG:], ((0, 0), (0, pad_w)))

    u_hg, u_ml, xb = _inproj(x, w_hg, w_ml, bb_hg, bb_ml)
    og_hg, sst = _hgrn2_fwd(u_hg, logits, hg_gn)
    pre, qkc = _conv_fwd(u_ml, conv_w, conv_b)
    og_ml, cst, nst, mst = _mlstm_fwd(qkc, u_ml, ml_gn)
    x1, xh1, rs1, x1b = _outproj_ln1(og_hg, og_ml, x, w_out_b, ln1_g, ln1_b)
    hgp, up, act = _ffn_up(x1b, wg_b, wu_b)
    x2, xh2, rs2, x2b = _ffn_down_ln2(act, x1, wd_b, ln2_g, ln2_b)
    dr2, de, dz, loss, d_bpg, d_ln2g, d_ln2b = _head_loss_bwd(x2, xh2, rs2, p, tgt, w_pg_b, b_pg, w_pp_b, ln2_g)
    dr1, dhg, dup, d_ln1g, d_ln1b = _ffn_bwd(dr2, hgp, up, xh1, rs1, wd_b, wg_b, wu_b, ln1_g)

    d_wo_a = _wgrad(og_hg, dr1, "wgrad_out_hg")
    d_wo_b = _wgrad(og_ml, dr1, "wgrad_out_ml")
    d_w_out = jnp.concatenate([d_wo_a, d_wo_b], axis=0)
    d_wg = _wgrad(x1b, dhg, "wgrad_ffn_gate", tn=D_FF // 2)
    d_wu = _wgrad(x1b, dup, "wgrad_ffn_up", tn=D_FF // 2)
    d_wd = _wgrad(act, dr2, "wgrad_ffn_down", tk=D_FF // 2)
    d_wpp = _wgrad(p, de, "wgrad_ple_proj")
    d_wpg = _wgrad(x2b, dz, "wgrad_ple_gate")
    early = dict(w_out=d_w_out, w_ffn_gate=d_wg, w_ffn_up=d_wu, w_ffn_down=d_wd, ple_w_proj=d_wpp, ple_w_gate=d_wpg)
    riders = early_hook(early) if early_hook is not None else ()

    dog_hg, dog_ml = _outproj_bwd(dr1, w_out_b)
    res = _hgrn2_bwd(u_hg, logits, hg_gn, sst, dog_hg, riders)
    du_hg, d_logits, d_hg_gn = res[:3]
    dqkc, dmv, dmo, dgt, d_ml_gn = _mlstm_bwd(qkc, u_ml, ml_gn, cst, nst, mst, dog_ml)
    dqk, d_conv_w, d_conv_b = _conv_bwd(u_ml, conv_w, pre, dqkc)
    grad_x, du_ml = _inproj_bwd(dr1, du_hg, dqk, dmv, dmo, dgt, w_hg, w_ml)

    dw_hg = _wgrad(xb, du_hg, "wgrad_in_hg", tn=1024)
    dw_ml = _wgrad(xb, du_ml, "wgrad_in_ml")
    d_w_in = jnp.concatenate([dw_hg, dw_ml[:, :PROJ_W - U_HG]], axis=1)
    d_b_in = _colsum([du_hg, du_ml], "colsum_du")[:, :PROJ_W]

    grads = dict(w_in=d_w_in, b_in=d_b_in, hg_lb_logits=d_logits, ml_conv_w=d_conv_w, ml_conv_b=d_conv_b,
                 hg_norm_g=d_hg_gn, ml_norm_g=d_ml_gn, ln1_g=d_ln1g, ln1_b=d_ln1b, ln2_g=d_ln2g, ln2_b=d_ln2b,
                 ple_b_gate=d_bpg, **early)
    return loss, grad_x, grads, list(res[3:])


_ANY = pl.BlockSpec(memory_space=pltpu.HBM)
_MESH = pl.DeviceIdType.MESH


def _my_place():
    return lax.axis_index("x"), lax.axis_index("y"), lax.axis_index("c")


def _other_chips(x, y):
    return [(1 - x, y), (x, 1 - y), (1 - x, 1 - y)]


def _allgather_weights(shards, taps, name):
    n = len(shards)
    halves = [s.shape[0] // 2 for s in shards]

    def body(*refs):
        ins, tap_in = refs[:n], refs[n]
        outs, tap_out = refs[n + 1:2 * n + 1], refs[2 * n + 1]
        send_sems, recv_sems, local_sems = refs[2 * n + 2:]
        x, y, c = _my_place()
        me = 2 * x + y
        sibling = (x, y, 1 - c)
        chips = _other_chips(x, y)

        def ici(a, j, block_chip):
            px, py = chips[j]
            src = ins[a].at[pl.ds(pl.multiple_of(c * halves[a], 16), halves[a])] if block_chip is None else outs[a].at[block_chip, c]
            dst = outs[a].at[me if block_chip is None else block_chip, c]
            return pltpu.make_async_remote_copy(src_ref=src, dst_ref=dst, send_sem=send_sems.at[6 * a + j], recv_sem=recv_sems.at[6 * a + j],
                                                device_id=(px, py, c), device_id_type=_MESH)

        def d2d(a, j, half):
            px, py = chips[j]
            blk = outs[a].at[2 * px + py, half]
            return pltpu.make_async_remote_copy(src_ref=blk, dst_ref=blk, send_sem=send_sems.at[6 * a + 3 + j], recv_sem=recv_sems.at[6 * a + 3 + j],
                                                device_id=sibling, device_id_type=_MESH)

        local = []
        for a in range(n):
            for h in range(2):
                cp = pltpu.make_async_copy(ins[a].at[pl.ds(h * halves[a], halves[a])], outs[a].at[me, h], local_sems.at[2 * a + h])
                cp.start()
                local.append(cp)
            for j in range(3):
                ici(a, j, None).start()
        tap_local = pltpu.make_async_copy(tap_in, tap_out.at[me], local_sems.at[2 * n])
        tap_local.start()
        tap_copies = []
        for j, (px, py) in enumerate(chips):
            cp = pltpu.make_async_remote_copy(src_ref=tap_in, dst_ref=tap_out.at[me], send_sem=send_sems.at[6 * n + j], recv_sem=recv_sems.at[6 * n + j],
                                              device_id=(px, py, c), device_id_type=_MESH)
            cp.start()
            tap_copies.append(cp)
        for a in range(n):
            for j, (px, py) in enumerate(chips):
                ici(a, j, 2 * px + py).wait_recv()
                d2d(a, j, c).start()
        for a in range(n):
            for j in range(3):
                d2d(a, j, 1 - c).wait_recv()
        for a in range(n):
            for j in range(3):
                ici(a, j, None).wait_send()
                d2d(a, j, c).wait_send()
        for j, (px, py) in enumerate(chips):
            pltpu.make_async_remote_copy(src_ref=tap_in, dst_ref=tap_out.at[2 * px + py], send_sem=send_sems.at[6 * n + j], recv_sem=recv_sems.at[6 * n + j],
                                         device_id=(px, py, c), device_id_type=_MESH).wait()
        for cp in local:
            cp.wait()
        tap_local.wait()

    res = pl.pallas_call(
        body, name=name,
        in_specs=[_ANY] * (n + 1), out_specs=[_ANY] * (n + 1),
        out_shape=[jax.ShapeDtypeStruct((4, 2, s.shape[0] // 2, s.shape[1]), s.dtype) for s in shards]
        + [jax.ShapeDtypeStruct((4,) + taps.shape, taps.dtype)],
        scratch_shapes=[pltpu.SemaphoreType.DMA((6 * n + 3,)), pltpu.SemaphoreType.DMA((6 * n + 3,)), pltpu.SemaphoreType.DMA((2 * n + 1,))],
    )(*shards, taps)
    return [w.reshape((4,) + s.shape) for w, s in zip(res[:n], shards)], res[n]


def _swap_halves(pieces, name):
    n = len(pieces)
    halves = [p.shape[1] // 2 for p in pieces]

    def body(*refs):
        ins, own, other = refs[:n], refs[n:2 * n], refs[2 * n:3 * n]
        send_sems, recv_sems, local_sems = refs[3 * n:]
        x, y, c = _my_place()

        def half_of(a, which):
            return ins[a].at[pl.ds(0, 4), pl.ds(pl.multiple_of(which * halves[a], 16), halves[a])]

        def to_sibling(a):
            return pltpu.make_async_remote_copy(src_ref=half_of(a, 1 - c), dst_ref=other[a], send_sem=send_sems.at[a], recv_sem=recv_sems.at[a],
                                                device_id=(x, y, 1 - c), device_id_type=_MESH)

        local = []
        for a in range(n):
            cp = pltpu.make_async_copy(half_of(a, c), own[a], local_sems.at[a])
            cp.start()
            local.append(cp)
            to_sibling(a).start()
        for a in range(n):
            to_sibling(a).wait()
            local[a].wait()

    shapes = [jax.ShapeDtypeStruct((4, p.shape[1] // 2, p.shape[2]), p.dtype) for p in pieces]
    res = pl.pallas_call(
        body, name=name,
        in_specs=[_ANY] * n, out_specs=[_ANY] * (2 * n), out_shape=shapes + shapes,
        scratch_shapes=[pltpu.SemaphoreType.DMA((n,)), pltpu.SemaphoreType.DMA((n,)), pltpu.SemaphoreType.DMA((n,))],
    )(*pieces)
    return res[:n], res[n:]


_VMEM = pl.BlockSpec(memory_space=pltpu.VMEM)
_EX_ROWS = 32


def _pair_reduce(p, name):
    s, r, c = p.shape
    half = r // 2

    def body(p_ref, o_ref, other, send_sem, recv_sem):
        x, y, cc = _my_place()
        theirs = pl.multiple_of((1 - cc) * half, 16)
        mine = pl.multiple_of(cc * half, 16)
        cp = pltpu.make_async_remote_copy(src_ref=p_ref.at[pl.ds(0, s), pl.ds(theirs, half)], dst_ref=other, send_sem=send_sem, recv_sem=recv_sem,
                                          device_id=(x, y, 1 - cc), device_id_type=_MESH)
        cp.start()
        cp.wait()

        def step(i, carry):
            r0 = pl.multiple_of(i * _EX_ROWS, _EX_ROWS)
            for slot in range(s):
                own_rows = pl.ds(pl.multiple_of(mine + r0, 16), _EX_ROWS)
                o_ref[slot, pl.ds(r0, _EX_ROWS), :] = (p_ref[slot, own_rows, :] + other[slot, pl.ds(r0, _EX_ROWS), :]).astype(bf16)
            return carry

        lax.fori_loop(0, half // _EX_ROWS, step, 0)

    return pl.pallas_call(
        body, name=name, in_specs=[_VMEM], out_specs=_VMEM,
        out_shape=jax.ShapeDtypeStruct((s, half, c), bf16),
        scratch_shapes=[pltpu.VMEM((s, half, c), f32), pltpu.SemaphoreType.DMA, pltpu.SemaphoreType.DMA],
        compiler_params=pltpu.CompilerParams(vmem_limit_bytes=VMEM_LIMIT),
    )(p)


def _chip_reduce_swap(rcv, name):
    s, h, c = rcv.shape

    def body(r_ref, g_ref, send_sem, recv_sem):
        x, y, cc = _my_place()

        def step(i, carry):
            r0 = pl.multiple_of(i * _EX_ROWS, _EX_ROWS)
            acc = r_ref[0, pl.ds(r0, _EX_ROWS), :].astype(f32)
            for slot in range(1, s):
                acc = acc + r_ref[slot, pl.ds(r0, _EX_ROWS), :].astype(f32)
            g_ref[cc, pl.ds(r0, _EX_ROWS), :] = acc
            return carry

        lax.fori_loop(0, h // _EX_ROWS, step, 0)
        cp = pltpu.make_async_remote_copy(src_ref=g_ref.at[cc], dst_ref=g_ref.at[cc], send_sem=send_sem, recv_sem=recv_sem,
                                          device_id=(x, y, 1 - cc), device_id_type=_MESH)
        cp.start()
        cp.wait()

    return pl.pallas_call(
        body, name=name, in_specs=[_VMEM], out_specs=_VMEM,
        out_shape=jax.ShapeDtypeStruct((2, h, c), f32),
        scratch_shapes=[pltpu.SemaphoreType.DMA, pltpu.SemaphoreType.DMA],
        compiler_params=pltpu.CompilerParams(vmem_limit_bytes=VMEM_LIMIT),
    )(rcv)


def _add_cast(a, b, name):
    s, r, c = a.shape
    tr = _row_tile(r, c)

    def body(a_ref, b_ref, o_ref):
        o_ref[...] = (a_ref[...] + b_ref[...]).astype(bf16)

    blk = pl.BlockSpec((1, tr, c), lambda i, j: (i, j, 0))
    return pl.pallas_call(
        body, name=name, grid=(s, r // tr), in_specs=[blk, blk], out_specs=blk,
        out_shape=jax.ShapeDtypeStruct(a.shape, bf16),
        compiler_params=_cparams(2, arbitrary=False),
    )(a, b)


def _scatter_copies(ins, outs, send_sems, recv_sems, local_sems):
    x, y, c = _my_place()
    me = 2 * x + y
    local, outgoing, incoming = [], [], []
    for a in range(len(ins)):
        local.append(pltpu.make_async_copy(ins[a].at[me], outs[a].at[me], local_sems.at[a]))
        for j, (px, py) in enumerate(_other_chips(x, y)):
            sems = dict(send_sem=send_sems.at[3 * a + j], recv_sem=recv_sems.at[3 * a + j], device_id=(px, py, c), device_id_type=_MESH)
            outgoing.append(pltpu.make_async_remote_copy(src_ref=ins[a].at[2 * px + py], dst_ref=outs[a].at[me], **sems))
            incoming.append(pltpu.make_async_remote_copy(src_ref=ins[a].at[2 * px + py], dst_ref=outs[a].at[2 * px + py], **sems))
    return local, outgoing, incoming


def _scatter_start(ins, outs, send_sems, recv_sems, local_sems):
    local, outgoing, _ = _scatter_copies(ins, outs, send_sems, recv_sems, local_sems)
    for cp in local + outgoing:
        cp.start()


def _scatter_wait(ins, outs, send_sems, recv_sems, local_sems):
    local, outgoing, incoming = _scatter_copies(ins, outs, send_sems, recv_sems, local_sems)
    for cp in incoming:
        cp.wait_recv()
    for cp in outgoing:
        cp.wait_send()
    for cp in local:
        cp.wait()


def _scatter_chips(pieces, name):
    n = len(pieces)

    def body(*refs):
        ins, outs = refs[:n], refs[n:2 * n]
        _scatter_start(ins, outs, *refs[2 * n:])
        _scatter_wait(ins, outs, *refs[2 * n:])

    return pl.pallas_call(
        body, name=name,
        in_specs=[_ANY] * n, out_specs=[_ANY] * n,
        out_shape=[jax.ShapeDtypeStruct(s.shape, s.dtype) for s in pieces],
        scratch_shapes=[pltpu.SemaphoreType.DMA((3 * n,)), pltpu.SemaphoreType.DMA((3 * n,)), pltpu.SemaphoreType.DMA((n,))],
    )(*pieces)


def _swap_cores(blocks, name):
    n = len(blocks)
    parts = 4
    rows = [b.shape[0] // parts for b in blocks]

    def body(*refs):
        ins, outs = refs[:n], refs[n:2 * n]
        send_sems, recv_sems, local_sems = refs[2 * n:]
        x, y, c = _my_place()

        def remote(a, k, slot):
            rs = pl.ds(k * rows[a], rows[a])
            return pltpu.make_async_remote_copy(src_ref=ins[a].at[rs], dst_ref=outs[a].at[slot, rs], send_sem=send_sems.at[parts * a + k],
                                                recv_sem=recv_sems.at[parts * a + k], device_id=(x, y, 1 - c), device_id_type=_MESH)

        local = []
        for a in range(n):
            cp = pltpu.make_async_copy(ins[a], outs[a].at[c], local_sems.at[a])
            cp.start()
            local.append(cp)
            for k in range(parts):
                remote(a, k, c).start()
        for a in range(n):
            for k in range(parts):
                remote(a, k, 1 - c).wait()
            local[a].wait()

    return pl.pallas_call(
        body, name=name,
        in_specs=[_ANY] * n, out_specs=[_ANY] * n,
        out_shape=[jax.ShapeDtypeStruct((2,) + s.shape, s.dtype) for s in blocks],
        scratch_shapes=[pltpu.SemaphoreType.DMA((parts * n,)), pltpu.SemaphoreType.DMA((parts * n,)), pltpu.SemaphoreType.DMA((n,))],
    )(*blocks)


def _gather_all(block, name):
    def body(in_ref, out_ref, send_sems, recv_sems, local_sem):
        x, y, c = _my_place()
        me = 4 * x + 2 * y + c
        cp = pltpu.make_async_copy(in_ref, out_ref.at[me], local_sem)
        cp.start()
        peers = []
        for dx in range(2):
            for dy in range(2):
                for dc in range(2):
                    if dx or dy or dc:
                        peers.append((1 - x if dx else x, 1 - y if dy else y, 1 - c if dc else c))
        for j, pr in enumerate(peers):
            pltpu.make_async_remote_copy(src_ref=in_ref, dst_ref=out_ref.at[me], send_sem=send_sems.at[j], recv_sem=recv_sems.at[j],
                                         device_id=pr, device_id_type=_MESH).start()
        for j, (px, py, pc) in enumerate(peers):
            pltpu.make_async_remote_copy(src_ref=in_ref, dst_ref=out_ref.at[4 * px + 2 * py + pc], send_sem=send_sems.at[j], recv_sem=recv_sems.at[j],
                                         device_id=(px, py, pc), device_id_type=_MESH).wait()
        cp.wait()

    return pl.pallas_call(
        body, name=name,
        in_specs=[_ANY], out_specs=_ANY,
        out_shape=jax.ShapeDtypeStruct((8,) + block.shape, block.dtype),
        scratch_shapes=[pltpu.SemaphoreType.DMA((7,)), pltpu.SemaphoreType.DMA((7,)), pltpu.SemaphoreType.DMA],
    )(block)


def _row_tile(r, c):
    best = r
    for cand in range(16, r + 1, 16):
        if r % cand == 0 and cand * c * 4 <= (1 << 20):
            best = cand
    return best if best * c * 4 <= (4 << 20) else r


def _sum_slots(parts, name):
    n, r, c = parts.shape
    tr = _row_tile(r, c)

    def body(p_ref, o_ref):
        acc = p_ref[0].astype(f32)
        for s in range(1, n):
            acc = acc + p_ref[s].astype(f32)
        o_ref[...] = acc

    return pl.pallas_call(
        body, name=name, grid=(r // tr,),
        in_specs=[pl.BlockSpec((n, tr, c), lambda i: (0, i, 0))],
        out_specs=pl.BlockSpec((tr, c), lambda i: (i, 0)),
        out_shape=jax.ShapeDtypeStruct((r, c), f32),
        compiler_params=_cparams(1, arbitrary=False),
    )(parts)


def _adamw(parts, w, m, v, name):
    n, r, c = parts.shape
    tr = _row_tile(r, c)

    def body(p_ref, w_ref, m_ref, v_ref, g_ref, d_ref, nm_ref, nv_ref):
        g = p_ref[0]
        for s in range(1, n):
            g = g + p_ref[s]
        nm = B1 * m_ref[...] + (1.0 - B1) * g
        nv = B2 * v_ref[...] + (1.0 - B2) * (g * g)
        m_hat = nm / (1.0 - B1 ** STEP)
        v_hat = nv / (1.0 - B2 ** STEP)
        g_ref[...] = g
        nm_ref[...] = nm
        nv_ref[...] = nv
        d_ref[...] = -LR * (m_hat / (jnp.sqrt(v_hat) + EPS_ADAM) + WD * w_ref[...])

    blk = pl.BlockSpec((tr, c), lambda i: (i, 0))
    return pl.pallas_call(
        body, name=name, grid=(r // tr,),
        in_specs=[pl.BlockSpec((n, tr, c), lambda i: (0, i, 0)), blk, blk, blk],
        out_specs=[blk] * 4,
        out_shape=[jax.ShapeDtypeStruct((r, c), f32)] * 4,
        compiler_params=_cparams(1, arbitrary=False),
    )(parts, w, m, v)


_BIG = ["w_in", "w_out", "w_ffn_gate", "w_ffn_up", "w_ffn_down", "ple_w_proj", "ple_w_gate"]
_COL_SPLIT = {"w_in", "w_ffn_gate", "w_ffn_up", "ple_w_proj"}
_SMALL = ["b_in", "hg_lb_logits", "ml_conv_w", "ml_conv_b", "hg_norm_g", "ml_norm_g", "ln1_g", "ln1_b", "ln2_g", "ln2_b", "ple_b_gate"]
_ORDER = ["w_in", "b_in", "hg_lb_logits", "ml_conv_w", "ml_conv_b", "hg_norm_g", "ml_norm_g", "w_out", "ln1_g", "ln1_b",
          "w_ffn_gate", "w_ffn_up", "w_ffn_down", "ln2_g", "ln2_b", "ple_w_proj", "ple_w_gate", "ple_b_gate"]
_PACK_ROWS, _PACK_COLS = 16, 1024


def _pack(arrays):
    flat = jnp.concatenate([a.reshape(-1) for a in arrays])
    return jnp.pad(flat, (0, _PACK_ROWS * _PACK_COLS - flat.shape[0])).reshape(_PACK_ROWS, _PACK_COLS)


def _unpack(pack, shapes):
    flat = pack.reshape(-1)
    out, off = [], 0
    for s in shapes:
        size = 1
        for d in s:
            size *= d
        out.append(flat[off:off + size].reshape(s))
        off += size
    return out


def _to_chip_major(g, col_split):
    if col_split:
        k, n = g.shape
        return g.reshape(k, 4, n // 4).transpose(1, 0, 2)
    k, n = g.shape
    return g.reshape(4, k // 4, n)


def _from_chip_major(a, col_split):
    if col_split:
        return a.transpose(1, 0, 2).reshape(a.shape[1], 4 * a.shape[2])
    return a.reshape(4 * a.shape[1], a.shape[2])


def kernel(x, p, w_in, b_in, hg_lb_logits, ml_conv_w, ml_conv_b, hg_norm_g, ml_norm_g, w_out, ln1_g, ln1_b, w_ffn_gate, w_ffn_up, w_ffn_down, ln2_g, ln2_b, ple_w_proj, ple_w_gate, ple_b_gate, loss_target, m_w_in, m_b_in, m_hg_lb_logits, m_ml_conv_w, m_ml_conv_b, m_hg_norm_g, m_ml_norm_g, m_w_out, m_ln1_g, m_ln1_b, m_w_ffn_gate, m_w_ffn_up, m_w_ffn_down, m_ln2_g, m_ln2_b, m_ple_w_proj, m_ple_w_gate, m_ple_b_gate, v_w_in, v_b_in, v_hg_lb_logits, v_ml_conv_w, v_ml_conv_b, v_hg_norm_g, v_ml_norm_g, v_w_out, v_ln1_g, v_ln1_b, v_w_ffn_gate, v_w_ffn_up, v_w_ffn_down, v_ln2_g, v_ln2_b, v_ple_w_proj, v_ple_w_gate, v_ple_b_gate):
    args = dict(locals())
    wts = {k: args[k] for k in _ORDER}
    mom = {k: args["m_" + k] for k in _ORDER}
    var = {k: args["v_" + k] for k in _ORDER}
    two_d = lambda a: a.reshape(a.shape[-2], a.shape[-1])

    shards = [two_d(wts[k]).astype(bf16) for k in _BIG]
    gathered, taps = _allgather_weights(shards, two_d(ml_conv_w), "allgather_weights")
    full = {k: _from_chip_major(g, k in _COL_SPLIT) for k, g in zip(_BIG, gathered)}
    conv_w_full = _from_chip_major(taps, True)

    def core_sum(k, g):
        return _pair_reduce(_to_chip_major(g, k in _COL_SPLIT), "pair_reduce_" + k)

    early_keys = _BIG[1:]
    loss, grad_x, grads, early_received = _local_step(
        x[0], p[0, 0], loss_target[0], full["w_in"], b_in, hg_lb_logits, conv_w_full, ml_conv_b, hg_norm_g, ml_norm_g,
        full["w_out"], ln1_g, ln1_b, full["w_ffn_gate"], full["w_ffn_up"], full["w_ffn_down"], ln2_g, ln2_b,
        full["ple_w_proj"], full["ple_w_gate"], ple_b_gate,
        early_hook=lambda early: [core_sum(k, early[k]) for k in early_keys])

    received = list(_scatter_chips([core_sum("w_in", grads["w_in"])], "scatter_grad_w_in")) + list(early_received)
    both = [_chip_reduce_swap(r, "chip_reduce_" + k) for k, r in zip(_BIG, received)]
    out_g, out_d, out_m, out_v = {}, {}, {}, {}
    for k, parts in zip(_BIG, both):
        whole = parts.reshape(1, 2 * parts.shape[1], parts.shape[2])
        g, d, nm, nv = _adamw(whole, two_d(wts[k]), two_d(mom[k]), two_d(var[k]), "adamw_" + k)
        shp = wts[k].shape
        out_g[k], out_d[k], out_m[k], out_v[k] = g.reshape(shp), d.reshape(shp), nm.reshape(shp), nv.reshape(shp)

    small_shapes = [(1, PROJ_W), (2, MIX_W), (CONV_K, MIX_W)] + [(1, MIX_W)] * 3 + [(1, D_MODEL)] * 5 + [(1, 1)]
    contrib = _pack([grads[k] for k in _SMALL] + [loss])
    summed = _sum_slots(_gather_all(contrib, "gather_small"), "sum_small")
    small = _unpack(summed, small_shapes)
    loss_total = small[-1].reshape(())
    gsm = dict(zip(_SMALL, small[:-1]))
    place = 2 * lax.axis_index("x") + lax.axis_index("y")
    conv_cols = ml_conv_w.shape[-1]
    gsm["ml_conv_w"] = lax.dynamic_slice(gsm["ml_conv_w"], (0, place * conv_cols), (CONV_K, conv_cols))
    own_shapes = [wts[k].shape for k in _SMALL]
    g_pack = _pack([gsm[k] for k in _SMALL])
    res = _adamw(g_pack[None], _pack([wts[k] for k in _SMALL]), _pack([mom[k] for k in _SMALL]), _pack([var[k] for k in _SMALL]), "adamw_small")
    for dst, pack in zip((out_g, out_d, out_m, out_v), res):
        for k, a in zip(_SMALL, _unpack(pack, own_shapes)):
            dst[k] = a

    outs = [loss_total, grad_x[None]]
    for group in (out_g, out_d, out_m, out_v):
        outs += [group[k] for k in _ORDER]
    return tuple(outs)
```

```python
import functools

import jax
import jax.numpy as jnp
from jax import lax
from jax.experimental import pallas as pl
from jax.experimental.pallas import tpu as pltpu

f32 = jnp.float32
bf16 = jnp.bfloat16
HI = lax.Precision.HIGHEST

D_MODEL = 1024
HEADS = 4
HEAD_W = 128
MIX_W = HEADS * HEAD_W
ML_DQK = 64
PROJ_W = 3592
U_HG = 4 * MIX_W
U_ML = 3 * MIX_W + 128
D_FF = 2816
PLE = 256
CHUNK = 128
SUB = 16
EXP_CAP = 80.0
CONV_K = 4
HALO = 8
ALPHA = float(2.0 ** 0.25)
LN_EPS = 1e-5
RMS_EPS = 1e-6
NEG = -1e30
LR, B1, B2, EPS_ADAM, WD, STEP = 0.001, 0.9, 0.999, 1e-08, 0.01, 10
VMEM_LIMIT = 56 * 1024 * 1024
DENSE_ROWS = 512
WGRAD_ROWS = 2048


def _cparams(n_axes, arbitrary=True):
    sem = ("arbitrary",) * n_axes if arbitrary else ("parallel",) * n_axes
    return pltpu.CompilerParams(dimension_semantics=sem, vmem_limit_bytes=VMEM_LIMIT)


ACT = bf16


def _mx(a):
    return a.astype(ACT)


def _bdot(a, b):
    return jnp.dot(_mx(a), _mx(b), preferred_element_type=f32)


def _bdot_nt(a, b):
    return lax.dot_general(_mx(a), _mx(b), (((1,), (1,)), ((), ())), preferred_element_type=f32)


def _bdot_tn(a, b):
    return lax.dot_general(_mx(a), _mx(b), (((0,), (0,)), ((), ())), preferred_element_type=f32)


def _split3(x):
    hi = x.astype(bf16)
    r1 = x - hi.astype(f32)
    mid = r1.astype(bf16)
    lo = (r1 - mid.astype(f32)).astype(bf16)
    return hi, mid, lo


def _dot3(a, b, dims):
    a_hi = a.astype(bf16)
    a_lo = (a - a_hi.astype(f32)).astype(bf16)
    b_hi = b.astype(bf16)
    b_lo = (b - b_hi.astype(f32)).astype(bf16)
    dn = (dims, ((), ()))
    return (lax.dot_general(a_hi, b_hi, dn, preferred_element_type=f32) + lax.dot_general(a_hi, b_lo, dn, preferred_element_type=f32)
            + lax.dot_general(a_lo, b_hi, dn, preferred_element_type=f32))


def _sel_dot(sel, x):
    sb = sel.astype(bf16)
    return sum(jnp.dot(sb, part, preferred_element_type=f32) for part in _split3(x))


def _sel_dot_nt(sel, x):
    sb = sel.astype(bf16)
    return sum(lax.dot_general(sb, part, (((1,), (1,)), ((), ())), preferred_element_type=f32) for part in _split3(x))


def _sigmoid(x):
    return 1.0 / (1.0 + jnp.exp(-x))


def _log_sigmoid(x):
    return jnp.minimum(x, 0.0) - jnp.log(1.0 + jnp.exp(-jnp.abs(x)))


def _tri(n, upper=False):
    r = lax.broadcasted_iota(jnp.int32, (n, n), 0)
    c = lax.broadcasted_iota(jnp.int32, (n, n), 1)
    return (c >= r) if upper else (c <= r)


def _rows(tm, n, col=0):
    return pl.BlockSpec((tm, n), lambda i, _c=col: (i, _c))


def _rows_rev(tm, n, nb, col=0):
    return pl.BlockSpec((tm, n), lambda i, _c=col, _nb=nb: (_nb - 1 - i, _c))


def _const(shape):
    return pl.BlockSpec(shape, lambda i, _n=len(shape): (0,) * _n)


def _resident(shape):
    return pl.BlockSpec(shape, lambda i, _n=len(shape): (0,) * _n, pipeline_mode=pl.Buffered(1))


def _tile(t, want):
    return want if t % want == 0 else t


def _inproj(x, w_hg, w_ml, b_hg, b_ml):
    t = x.shape[0]
    tm = _tile(t, DENSE_ROWS)

    def body(x_ref, whg_ref, wml_ref, bhg_ref, bml_ref, uhg_ref, uml_ref, xb_ref):
        xb = _mx(x_ref[...])
        xb_ref[...] = xb
        uhg_ref[...] = _bdot(xb, whg_ref[...]) + bhg_ref[...]
        uml_ref[...] = _bdot(xb, wml_ref[...]) + bml_ref[...]

    return pl.pallas_call(
        body, name="inproj", grid=(t // tm,),
        in_specs=[_rows(tm, D_MODEL), _resident((D_MODEL, U_HG)), _resident((D_MODEL, U_ML)), _const((1, U_HG)), _const((1, U_ML))],
        out_specs=[_rows(tm, U_HG), _rows(tm, U_ML), _rows(tm, D_MODEL)],
        out_shape=[jax.ShapeDtypeStruct((t, U_HG), f32), jax.ShapeDtypeStruct((t, U_ML), f32), jax.ShapeDtypeStruct((t, D_MODEL), ACT)],
        compiler_params=_cparams(1, arbitrary=False),
    )(x, w_hg, w_ml, b_hg, b_ml)


def _hg_gates(hq, hf, lb, tri):
    s = _sigmoid(hf)
    om = 1.0 - lb
    f = lb + om * s
    g = jnp.log(f)
    k = om * (1.0 - s)
    sq = _sigmoid(hq)
    q = hq * sq
    b = _sel_dot(tri, g)
    return q, sq, s, f, k, b


def _hg_scores(q, k, b, tril_mask):
    qts, kts, eqs, eks, rows = [], [], [], [], []
    for i in range(CHUNK // SUB):
        lo = i * SUB
        ref = jnp.zeros_like(b[0:1]) if i == 0 else b[lo - 1:lo]
        eq = jnp.exp(b[lo:lo + SUB] - ref)
        ek = jnp.exp(jnp.minimum(ref - b, EXP_CAP))
        qt = q[lo:lo + SUB] * eq
        kt = k * ek
        rows.append(_bdot_nt(qt, kt))
        qts.append(qt); kts.append(kt); eqs.append(eq); eks.append(ek)
    a = jnp.where(tril_mask, jnp.concatenate(rows, axis=0), 0.0)
    return a, qts, kts, eqs, eks


def _head_rms(o, gn):
    rstd = lax.rsqrt(jnp.mean(o * o, axis=-1, keepdims=True) + RMS_EPS)
    oh = o * rstd
    return oh, rstd, oh * gn


def _lower_bound(logit_ref):
    lg = logit_ref[...]
    return _sigmoid(lg[0:1] - lg[1:2])


def _hgrn2_fwd(u_hg, logits, gn):
    t = u_hg.shape[0]
    tb = _tile(t, 256)
    nc_blk = tb // CHUNK

    def body(u_ref, lg_ref, gn_ref, og_ref, sst_ref, st_ref):
        @pl.when(pl.program_id(0) == 0)
        def _():
            st_ref[...] = jnp.zeros_like(st_ref)

        lb_all = _lower_bound(lg_ref)
        tril_mask = _tri(CHUNK)
        tri = tril_mask.astype(f32)

        def chunk(c, carry):
            r0 = pl.multiple_of(c * CHUNK, CHUNK)
            rows = pl.ds(r0, CHUNK)
            heads = range(HEADS)
            cols = [slice(h * HEAD_W, (h + 1) * HEAD_W) for h in heads]
            hv = [u_ref[rows, 2 * MIX_W + h * HEAD_W:2 * MIX_W + (h + 1) * HEAD_W] for h in heads]
            gts = [_hg_gates(u_ref[rows, h * HEAD_W:(h + 1) * HEAD_W], u_ref[rows, MIX_W + h * HEAD_W:MIX_W + (h + 1) * HEAD_W],
                             lb_all[:, cols[h]], tri) for h in heads]
            q = [g[0] for g in gts]
            k = [g[4] for g in gts]
            b = [g[5] for g in gts]
            a = [_hg_scores(q[h], k[h], b[h], tril_mask)[0] for h in heads]
            st = [st_ref[h] for h in heads]
            bl = [b[h][CHUNK - 1:CHUNK] for h in heads]
            o = [_bdot(a[h], hv[h]) + _bdot_nt(q[h] * jnp.exp(b[h]), st[h]) for h in heads]
            new_st = [st[h] * jnp.exp(bl[h]) + _bdot_tn(hv[h], k[h] * jnp.exp(bl[h] - b[h])) for h in heads]
            for h in heads:
                sst_ref[c, h] = st[h]
                st_ref[h] = new_st[h]
                hgate = u_ref[rows, 3 * MIX_W + h * HEAD_W:3 * MIX_W + (h + 1) * HEAD_W]
                _, _, y = _head_rms(o[h], gn_ref[:, cols[h]])
                og_ref[rows, cols[h]] = (y * (hgate * _sigmoid(hgate))).astype(ACT)
            return carry

        lax.fori_loop(0, nc_blk, chunk, 0)

    return pl.pallas_call(
        body, name="hgrn2_fwd", grid=(t // tb,),
        in_specs=[_rows(tb, U_HG), _const((2, MIX_W)), _const((1, MIX_W))],
        out_specs=[_rows(tb, MIX_W), pl.BlockSpec((nc_blk, HEADS, HEAD_W, HEAD_W), lambda i: (i, 0, 0, 0))],
        out_shape=[jax.ShapeDtypeStruct((t, MIX_W), ACT), jax.ShapeDtypeStruct((t // CHUNK, HEADS, HEAD_W, HEAD_W), f32)],
        scratch_shapes=[pltpu.VMEM((HEADS, HEAD_W, HEAD_W), f32)],
        compiler_params=_cparams(1),
    )(u_hg, logits, gn)


def _hgrn2_bwd(u_hg, logits, gn, sst, dog, riders=()):
    t = u_hg.shape[0]
    tb = _tile(t, 256)
    nb = t // tb
    nc_blk = tb // CHUNK
    nr = len(riders)

    def body(*refs):
        u_ref, lg_ref, gn_ref, sst_ref, dog_ref = refs[:5]
        ride_in = refs[5:5 + nr]
        du_ref, dlg_ref, dgn_ref = refs[5 + nr:8 + nr]
        ride_out = refs[8 + nr:8 + 2 * nr]
        dst_ref = refs[8 + 2 * nr]
        ride_sems = refs[9 + 2 * nr:]

        @pl.when(pl.program_id(0) == 0)
        def _():
            dst_ref[...] = jnp.zeros_like(dst_ref)
            dlg_ref[...] = jnp.zeros_like(dlg_ref)
            dgn_ref[...] = jnp.zeros_like(dgn_ref)
            if nr:
                _scatter_start(ride_in, ride_out, *ride_sems)

        lb_all = _lower_bound(lg_ref)
        tril_mask = _tri(CHUNK)
        tri = tril_mask.astype(f32)
        triu = _tri(CHUNK, upper=True).astype(f32)

        def chunk(j, carry):
            c = nc_blk - 1 - j
            r0 = pl.multiple_of(c * CHUNK, CHUNK)
            rows = pl.ds(r0, CHUNK)
            heads = range(HEADS)
            nsub = CHUNK // SUB
            cols = [slice(h * HEAD_W, (h + 1) * HEAD_W) for h in heads]
            hq = [u_ref[rows, h * HEAD_W:(h + 1) * HEAD_W] for h in heads]
            hf = [u_ref[rows, MIX_W + h * HEAD_W:MIX_W + (h + 1) * HEAD_W] for h in heads]
            hv = [u_ref[rows, 2 * MIX_W + h * HEAD_W:2 * MIX_W + (h + 1) * HEAD_W] for h in heads]
            lb = [lb_all[:, cols[h]] for h in heads]
            gts = [_hg_gates(hq[h], hf[h], lb[h], tri) for h in heads]
            q, sq, s, f, k, b = ([g[n] for g in gts] for n in range(6))
            scs = [_hg_scores(q[h], k[h], b[h], tril_mask) for h in heads]
            a, qts, kts, eqs, eks = ([sc[n] for sc in scs] for n in range(5))
            st = [sst_ref[c, h] for h in heads]
            dst = [dst_ref[h] for h in heads]
            bl = [b[h][CHUNK - 1:CHUNK] for h in heads]
            eb = [jnp.exp(b[h]) for h in heads]
            qh = [q[h] * eb[h] for h in heads]
            ekl = [jnp.exp(bl[h] - b[h]) for h in heads]
            kh = [k[h] * ekl[h] for h in heads]
            o = [_bdot(a[h], hv[h]) + _bdot_nt(qh[h], st[h]) for h in heads]
            do = []
            for h in heads:
                hgate = u_ref[rows, 3 * MIX_W + h * HEAD_W:3 * MIX_W + (h + 1) * HEAD_W]
                gnh = gn_ref[:, cols[h]]
                oh, rstd, y = _head_rms(o[h], gnh)
                sg = _sigmoid(hgate)
                dogh = dog_ref[rows, cols[h]]
                dy = dogh * (hgate * sg)
                du_ref[rows, 3 * MIX_W + h * HEAD_W:3 * MIX_W + (h + 1) * HEAD_W] = (dogh * y * (sg * (1.0 + hgate * (1.0 - sg)))).astype(ACT)
                dgn_ref[:, cols[h]] += jnp.sum(dy * oh, axis=0, keepdims=True)
                doh = dy * gnh
                do.append(rstd * (doh - oh * jnp.mean(doh * oh, axis=-1, keepdims=True)))
            da = [jnp.where(tril_mask, _bdot_nt(do[h], hv[h]), 0.0) for h in heads]
            dv = [_bdot_tn(a[h], do[h]) + _bdot_nt(kh[h], dst[h]) for h in heads]
            dq = [_bdot(do[h], st[h]) * eb[h] for h in heads]
            dk = [_bdot(hv[h], dst[h]) * ekl[h] for h in heads]
            d_last = [jnp.sum(k[h] * dk[h], axis=0, keepdims=True) + jnp.exp(bl[h]) * jnp.sum(dst[h] * st[h], axis=0, keepdims=True)
                      for h in heads]
            dqs = [[] for _ in heads]
            for i in range(nsub):
                for h in heads:
                    da_i = da[h][i * SUB:(i + 1) * SUB]
                    dqs[h].append(_dot3(da_i, kts[h][i], ((1,), (0,))) * eqs[h][i])
                    dk[h] = dk[h] + _dot3(da_i, qts[h][i], ((0,), (0,))) * eks[h][i]
            for h in heads:
                dq[h] = dq[h] + jnp.concatenate(dqs[h], axis=0)
                dst_ref[h] = dst[h] * jnp.exp(bl[h]) + _bdot_tn(do[h], qh[h])
            dg = [_sel_dot(triu, q[h] * dq[h] - k[h] * dk[h]) + d_last[h] for h in heads]
            for h in heads:
                dfk = dg[h] / f[h] - dk[h]
                du_ref[rows, h * HEAD_W:(h + 1) * HEAD_W] = (dq[h] * (sq[h] * (1.0 + hq[h] * (1.0 - sq[h])))).astype(ACT)
                du_ref[rows, MIX_W + h * HEAD_W:MIX_W + (h + 1) * HEAD_W] = ((1.0 - lb[h]) * dfk * s[h] * (1.0 - s[h])).astype(ACT)
                du_ref[rows, 2 * MIX_W + h * HEAD_W:2 * MIX_W + (h + 1) * HEAD_W] = dv[h].astype(ACT)
                dlb = jnp.sum((1.0 - s[h]) * dfk, axis=0, keepdims=True) * (lb[h] * (1.0 - lb[h]))
                dlg_ref[0:1, cols[h]] += dlb
                dlg_ref[1:2, cols[h]] -= dlb
            return carry

        lax.fori_loop(0, nc_blk, chunk, 0)

        if nr:
            @pl.when(pl.program_id(0) == nb - 1)
            def _():
                _scatter_wait(ride_in, ride_out, *ride_sems)

    hbm = pl.BlockSpec(memory_space=pltpu.HBM)
    ride_scratch = [pltpu.SemaphoreType.DMA((3 * nr,)), pltpu.SemaphoreType.DMA((3 * nr,)), pltpu.SemaphoreType.DMA((nr,))] if nr else []
    return pl.pallas_call(
        body, name="hgrn2_bwd", grid=(nb,),
        in_specs=[_rows_rev(tb, U_HG, nb), _const((2, MIX_W)), _const((1, MIX_W)),
                  pl.BlockSpec((nc_blk, HEADS, HEAD_W, HEAD_W), lambda i: (nb - 1 - i, 0, 0, 0)), _rows_rev(tb, MIX_W, nb)] + [hbm] * nr,
        out_specs=[_rows_rev(tb, U_HG, nb), _const((2, MIX_W)), _const((1, MIX_W))] + [hbm] * nr,
        out_shape=[jax.ShapeDtypeStruct((t, U_HG), ACT), jax.ShapeDtypeStruct((2, MIX_W), f32), jax.ShapeDtypeStruct((1, MIX_W), f32)]
        + [jax.ShapeDtypeStruct(r.shape, r.dtype) for r in riders],
        scratch_shapes=[pltpu.VMEM((HEADS, HEAD_W, HEAD_W), f32)] + ride_scratch,
        compiler_params=_cparams(1),
    )(u_hg, logits, gn, sst, dog, *riders)


def _conv_fwd(u_ml, w, b):
    t = u_ml.shape[0]
    tm = _tile(t, 512)

    def body(x_ref, w_ref, b_ref, pre_ref, act_ref, xbuf):
        @pl.when(pl.program_id(0) == 0)
        def _():
            xbuf[...] = jnp.zeros_like(xbuf)

        xbuf[0:HALO, :] = xbuf[tm:tm + HALO, :]
        xbuf[HALO:HALO + tm, :] = x_ref[...]
        pre = b_ref[...] + jnp.zeros((tm, MIX_W), f32)
        for kk in range(CONV_K):
            off = HALO - (CONV_K - 1) + kk
            pre = pre + w_ref[kk:kk + 1, :] * xbuf[off:off + tm, :]
        pre_ref[...] = pre
        act_ref[...] = pre * _sigmoid(pre)

    return pl.pallas_call(
        body, name="conv_fwd", grid=(t // tm,),
        in_specs=[_rows(tm, MIX_W), _const((CONV_K, MIX_W)), _const((1, MIX_W))],
        out_specs=[_rows(tm, MIX_W), _rows(tm, MIX_W)],
        out_shape=[jax.ShapeDtypeStruct((t, MIX_W), f32)] * 2,
        scratch_shapes=[pltpu.VMEM((tm + HALO, MIX_W), f32)],
        compiler_params=_cparams(1),
    )(u_ml, w, b)


def _conv_bwd(u_ml, w, pre, dact):
    t = u_ml.shape[0]
    tm = _tile(t, 512)
    nb = t // tm
    hb = tm // HALO

    def body(x_ref, halo_ref, w_ref, pre_ref, dact_ref, dx_ref, dw_ref, db_ref, dbuf, xbuf):
        i = pl.program_id(0)

        @pl.when(i == 0)
        def _():
            dbuf[...] = jnp.zeros_like(dbuf)
            dw_ref[...] = jnp.zeros_like(dw_ref)
            db_ref[...] = jnp.zeros_like(db_ref)

        p = pre_ref[...]
        sg = _sigmoid(p)
        dpre = dact_ref[...] * (sg * (1.0 + p * (1.0 - sg)))
        dbuf[tm:tm + HALO, :] = dbuf[0:HALO, :]
        dbuf[0:tm, :] = dpre
        has_prev = (i < nb - 1).astype(f32)
        xbuf[0:HALO, :] = halo_ref[...] * has_prev
        xbuf[HALO:HALO + tm, :] = x_ref[...]
        dx = jnp.zeros((tm, MIX_W), f32)
        for kk in range(CONV_K):
            back = CONV_K - 1 - kk
            dx = dx + w_ref[kk:kk + 1, :] * dbuf[back:back + tm, :]
            off = HALO - (CONV_K - 1) + kk
            dw_ref[kk:kk + 1, :] += jnp.sum(dpre * xbuf[off:off + tm, :], axis=0, keepdims=True)
        dx_ref[...] = dx.astype(ACT)
        db_ref[...] += jnp.sum(dpre, axis=0, keepdims=True)

    return pl.pallas_call(
        body, name="conv_bwd", grid=(nb,),
        in_specs=[_rows_rev(tm, MIX_W, nb),
                  pl.BlockSpec((HALO, MIX_W), lambda i: (jnp.maximum((nb - 1 - i) * hb - 1, 0), 0)),
                  _const((CONV_K, MIX_W)), _rows_rev(tm, MIX_W, nb), _rows_rev(tm, MIX_W, nb)],
        out_specs=[_rows_rev(tm, MIX_W, nb), _const((CONV_K, MIX_W)), _const((1, MIX_W))],
        out_shape=[jax.ShapeDtypeStruct((t, MIX_W), ACT), jax.ShapeDtypeStruct((CONV_K, MIX_W), f32), jax.ShapeDtypeStruct((1, MIX_W), f32)],
        scratch_shapes=[pltpu.VMEM((tm + HALO, MIX_W), f32), pltpu.VMEM((tm + HALO, MIX_W), f32)],
        compiler_params=_cparams(1),
    )(u_ml, u_ml, w, pre, dact)


def _lane_pick(x, lane):
    idx = lax.broadcasted_iota(jnp.int32, x.shape, 1)
    return jnp.sum(jnp.where(idx == lane, x, 0.0), axis=-1, keepdims=True)


def _ml_gate_forms(gates, tri):
    lf = _log_sigmoid(gates)
    gc = _sel_dot(tri, lf)
    lane = lax.broadcasted_iota(jnp.int32, gates.shape, 1)
    mixed = jnp.where(lane < HEADS, gates, gc)
    sel = (lax.broadcasted_iota(jnp.int32, (8, 128), 0) == lax.broadcasted_iota(jnp.int32, (8, 128), 1)).astype(f32)
    rowsf = _sel_dot_nt(sel, mixed)
    return gc, rowsf


def _ml_chunk(q, k, v, gates, gc, rowsf, c_st, n_st, m_st, tril_mask):
    hs = range(HEADS)
    g_col = [_lane_pick(gc, HEADS + h) for h in hs]
    ig_col = [_lane_pick(gates, h) for h in hs]
    dmat = [jnp.where(tril_mask, g_col[h] - rowsf[HEADS + h:HEADS + h + 1, :] + rowsf[h:h + 1, :], NEG) for h in hs]
    m_inter = [g_col[h] + m_st[h] for h in hs]
    m_t = [jnp.maximum(m_inter[h], jnp.max(dmat[h], axis=-1, keepdims=True)) for h in hs]
    wi = [jnp.exp(dmat[h] - m_t[h]) for h in hs]
    wo = [jnp.exp(m_inter[h] - m_t[h]) for h in hs]
    qk = [_bdot_nt(q[h], k[h]) * wi[h] for h in hs]
    num = [_bdot(qk[h], v[h]) + wo[h] * _bdot(q[h], c_st[h]) for h in hs]
    den = [jnp.sum(qk[h], axis=-1, keepdims=True) + wo[h] * jnp.sum(q[h] * n_st[h], axis=-1, keepdims=True) for h in hs]
    floor = [jnp.exp(-m_t[h]) for h in hs]
    z = [jnp.maximum(jnp.abs(den[h]), floor[h]) for h in hs]
    g_last = [g_col[h][CHUNK - 1:CHUNK] for h in hs]
    a_col = [g_last[h] - g_col[h] + ig_col[h] for h in hs]
    m_new = [jnp.maximum(g_last[h] + m_st[h], jnp.max(a_col[h], axis=0, keepdims=True)) for h in hs]
    ws = [jnp.exp(a_col[h] - m_new[h]) for h in hs]
    w_old = [jnp.exp(g_last[h] + m_st[h] - m_new[h]) for h in hs]
    return dict(wi=wi, wo=wo, qk=qk, num=num, den=den, z=z, floor=floor, ws=ws, w_old=w_old, m_new=m_new)


def _mlstm_fwd(qkc, u_ml, gn):
    t = qkc.shape[0]
    tb = _tile(t, 256)
    nc_blk = tb // CHUNK

    def body(qk_ref, v_ref, mo_ref, gt_ref, gn_ref, og_ref, cst_ref, nst_ref, mst_ref, c_sc, n_sc, m_sc):
        @pl.when(pl.program_id(0) == 0)
        def _():
            c_sc[...] = jnp.zeros_like(c_sc)
            n_sc[...] = jnp.zeros_like(n_sc)
            m_sc[...] = jnp.zeros_like(m_sc)

        tril_mask = _tri(CHUNK)
        tri = tril_mask.astype(f32)

        def chunk(c, carry):
            r0 = pl.multiple_of(c * CHUNK, CHUNK)
            rows = pl.ds(r0, CHUNK)
            gates = gt_ref[rows, :]
            gc, rowsf = _ml_gate_forms(gates, tri)
            hs = range(HEADS)
            q = [qk_ref[rows, h * ML_DQK:(h + 1) * ML_DQK] * (ML_DQK ** -0.5) for h in hs]
            k = [qk_ref[rows, HEADS * ML_DQK + h * ML_DQK:HEADS * ML_DQK + (h + 1) * ML_DQK] for h in hs]
            v = [v_ref[rows, h * HEAD_W:(h + 1) * HEAD_W] for h in hs]
            c_st = [c_sc[h] for h in hs]
            n_st = [n_sc[h] for h in hs]
            m_full = [m_sc[h] for h in hs]
            r = _ml_chunk(q, k, v, gates, gc, rowsf, c_st, n_st, [m[:, 0:1] for m in m_full], tril_mask)
            ksc = [k[h] * r["ws"][h] for h in hs]
            new_c = [r["w_old"][h] * c_st[h] + _bdot_tn(ksc[h], v[h]) for h in hs]
            for h in hs:
                cs = slice(h * HEAD_W, (h + 1) * HEAD_W)
                cst_ref[c, h] = c_st[h]
                nst_ref[c, h] = n_st[h]
                mst_ref[c, h] = m_full[h]
                c_sc[h] = new_c[h]
                n_sc[h] = r["w_old"][h] * n_st[h] + jnp.sum(ksc[h], axis=0, keepdims=True)
                m_sc[h] = r["m_new"][h] + jnp.zeros((1, 128), f32)
                _, _, y = _head_rms(r["num"][h] / r["z"][h], gn_ref[:, cs])
                og_ref[rows, cs] = (y * _sigmoid(mo_ref[rows, h * HEAD_W:(h + 1) * HEAD_W])).astype(ACT)
            return carry

        lax.fori_loop(0, nc_blk, chunk, 0)

    nchunks = t // CHUNK
    return pl.pallas_call(
        body, name="mlstm_fwd", grid=(t // tb,),
        in_specs=[_rows(tb, MIX_W), _rows(tb, MIX_W, 1), _rows(tb, MIX_W, 2), _rows(tb, 128, 12), _const((1, MIX_W))],
        out_specs=[_rows(tb, MIX_W),
                   pl.BlockSpec((nc_blk, HEADS, ML_DQK, HEAD_W), lambda i: (i, 0, 0, 0)),
                   pl.BlockSpec((nc_blk, HEADS, 1, ML_DQK), lambda i: (i, 0, 0, 0)),
                   pl.BlockSpec((nc_blk, HEADS, 1, 128), lambda i: (i, 0, 0, 0))],
        out_shape=[jax.ShapeDtypeStruct((t, MIX_W), ACT),
                   jax.ShapeDtypeStruct((nchunks, HEADS, ML_DQK, HEAD_W), f32),
                   jax.ShapeDtypeStruct((nchunks, HEADS, 1, ML_DQK), f32),
                   jax.ShapeDtypeStruct((nchunks, HEADS, 1, 128), f32)],
        scratch_shapes=[pltpu.VMEM((HEADS, ML_DQK, HEAD_W), f32), pltpu.VMEM((HEADS, 1, ML_DQK), f32), pltpu.VMEM((HEADS, 1, 128), f32)],
        compiler_params=_cparams(1),
    )(qkc, u_ml, u_ml, u_ml, gn)


def _mlstm_bwd(qkc, u_ml, gn, cst, nst, mst, dog):
    t = qkc.shape[0]
    tb = _tile(t, 256)
    nb = t // tb
    nc_blk = tb // CHUNK

    def body(qk_ref, v_ref, mo_ref, gt_ref, gn_ref, cst_ref, nst_ref, mst_ref, dog_ref,
             dqk_ref, dv_ref, dmo_ref, dgt_ref, dgn_ref, dc_sc, dn_sc):
        @pl.when(pl.program_id(0) == 0)
        def _():
            dc_sc[...] = jnp.zeros_like(dc_sc)
            dn_sc[...] = jnp.zeros_like(dn_sc)
            dgn_ref[...] = jnp.zeros_like(dgn_ref)

        tril_mask = _tri(CHUNK)
        tri = tril_mask.astype(f32)
        triu = _tri(CHUNK, upper=True).astype(f32)
        lane = lax.broadcasted_iota(jnp.int32, (CHUNK, 128), 1)

        def chunk(j, carry):
            c = nc_blk - 1 - j
            r0 = pl.multiple_of(c * CHUNK, CHUNK)
            rows = pl.ds(r0, CHUNK)
            gates = gt_ref[rows, :]
            gc, rowsf = _ml_gate_forms(gates, tri)
            dg_mat = jnp.zeros((CHUNK, 128), f32)
            dig_mat = jnp.zeros((CHUNK, 128), f32)
            dlast_row = jnp.zeros((1, 128), f32)
            hs = range(HEADS)
            cols = [slice(h * HEAD_W, (h + 1) * HEAD_W) for h in hs]
            q = [qk_ref[rows, h * ML_DQK:(h + 1) * ML_DQK] * (ML_DQK ** -0.5) for h in hs]
            k = [qk_ref[rows, HEADS * ML_DQK + h * ML_DQK:HEADS * ML_DQK + (h + 1) * ML_DQK] for h in hs]
            v = [v_ref[rows, h * HEAD_W:(h + 1) * HEAD_W] for h in hs]
            c_st = [cst_ref[c, h] for h in hs]
            n_st = [nst_ref[c, h] for h in hs]
            m_st = [mst_ref[c, h][:, 0:1] for h in hs]
            dc = [dc_sc[h] for h in hs]
            dn = [dn_sc[h] for h in hs]
            r = _ml_chunk(q, k, v, gates, gc, rowsf, c_st, n_st, m_st, tril_mask)
            z, wi, wo, ws, w_old, den = r["z"], r["wi"], r["wo"], r["ws"], r["w_old"], r["den"]
            hh = [r["num"][h] / z[h] for h in hs]
            dh = []
            for h in hs:
                gnh = gn_ref[:, cols[h]]
                oh, rstd, y = _head_rms(hh[h], gnh)
                sg = _sigmoid(mo_ref[rows, h * HEAD_W:(h + 1) * HEAD_W])
                dogh = dog_ref[rows, cols[h]]
                dy = dogh * sg
                dmo_ref[rows, cols[h]] = (dogh * y * (sg * (1.0 - sg))).astype(ACT)
                dgn_ref[:, cols[h]] += jnp.sum(dy * oh, axis=0, keepdims=True)
                doh = dy * gnh
                dh.append(rstd * (doh - oh * jnp.mean(doh * oh, axis=-1, keepdims=True)))
            dnum = [dh[h] / z[h] for h in hs]
            dz = [-jnp.sum(dh[h] * hh[h], axis=-1, keepdims=True) / z[h] for h in hs]
            dden = [jnp.where(jnp.abs(den[h]) > r["floor"][h], dz[h] * jnp.sign(den[h]), 0.0) for h in hs]
            dsw = [(_bdot_nt(dnum[h], v[h]) + dden[h]) * wi[h] for h in hs]
            dq = [_bdot(dsw[h], k[h]) + wo[h] * (_bdot_nt(dnum[h], c_st[h]) + dden[h] * n_st[h]) for h in hs]
            dk_state = [ws[h] * (_bdot_nt(v[h], dc[h]) + dn[h]) for h in hs]
            dk = [_bdot_tn(dsw[h], q[h]) + dk_state[h] for h in hs]
            dv = [_bdot_tn(r["qk"][h], dnum[h]) + ws[h] * _bdot(k[h], dc[h]) for h in hs]
            woq = [wo[h] * q[h] for h in hs]
            new_dc = [w_old[h] * dc[h] + _bdot_tn(woq[h], dnum[h]) for h in hs]
            for h in hs:
                dv_ref[rows, cols[h]] = dv[h].astype(ACT)
                dc_sc[h] = new_dc[h]
                dn_sc[h] = w_old[h] * dn[h] + jnp.sum(woq[h] * dden[h], axis=0, keepdims=True)
                d_last = (jnp.sum(jnp.sum(k[h] * dk_state[h], axis=-1, keepdims=True), axis=0, keepdims=True)
                          + w_old[h] * (jnp.sum(jnp.sum(dc[h] * c_st[h], axis=-1, keepdims=True), axis=0, keepdims=True)
                                        + jnp.sum(dn[h] * n_st[h], axis=-1, keepdims=True)))
                kdk = jnp.sum(k[h] * dk[h], axis=-1, keepdims=True)
                qdq = jnp.sum(q[h] * dq[h], axis=-1, keepdims=True)
                dg_mat = dg_mat + jnp.where(lane == HEADS + h, qdq - kdk, 0.0)
                dlast_row = dlast_row + jnp.where(lane[0:1] == HEADS + h, d_last, 0.0)
                dig_mat = dig_mat + jnp.where(lane == h, kdk, 0.0)
                dqk_ref[rows, h * ML_DQK:(h + 1) * ML_DQK] = dq[h] * (ML_DQK ** -0.5)
                dqk_ref[rows, HEADS * ML_DQK + h * ML_DQK:HEADS * ML_DQK + (h + 1) * ML_DQK] = dk[h]
            dlf = _sel_dot(triu, dg_mat) + dlast_row
            dgt_ref[rows, :] = (dig_mat + dlf * _sigmoid(-gates)).astype(ACT)
            return carry

        lax.fori_loop(0, nc_blk, chunk, 0)

    st4 = lambda a, b: pl.BlockSpec((nc_blk, HEADS, a, b), lambda i: (nb - 1 - i, 0, 0, 0))
    return pl.pallas_call(
        body, name="mlstm_bwd", grid=(nb,),
        in_specs=[_rows_rev(tb, MIX_W, nb), _rows_rev(tb, MIX_W, nb, 1), _rows_rev(tb, MIX_W, nb, 2), _rows_rev(tb, 128, nb, 12),
                  _const((1, MIX_W)), st4(ML_DQK, HEAD_W), st4(1, ML_DQK), st4(1, 128), _rows_rev(tb, MIX_W, nb)],
        out_specs=[_rows_rev(tb, MIX_W, nb), _rows_rev(tb, MIX_W, nb), _rows_rev(tb, MIX_W, nb), _rows_rev(tb, 128, nb), _const((1, MIX_W))],
        out_shape=[jax.ShapeDtypeStruct((t, MIX_W), f32), jax.ShapeDtypeStruct((t, MIX_W), ACT), jax.ShapeDtypeStruct((t, MIX_W), ACT),
                   jax.ShapeDtypeStruct((t, 128), ACT), jax.ShapeDtypeStruct((1, MIX_W), f32)],
        scratch_shapes=[pltpu.VMEM((HEADS, ML_DQK, HEAD_W), f32), pltpu.VMEM((HEADS, 1, ML_DQK), f32)],
        compiler_params=_cparams(1),
    )(qkc, u_ml, u_ml, u_ml, gn, cst, nst, mst, dog)


def _ln_fwd(r, g, b):
    mu = jnp.mean(r, axis=-1, keepdims=True)
    xc = r - mu
    rstd = lax.rsqrt(jnp.mean(xc * xc, axis=-1, keepdims=True) + LN_EPS)
    xh = xc * rstd
    return xh * g + b, xh, rstd


def _ln_bwd(dy, xh, rstd, g):
    dxh = dy * g
    return rstd * (dxh - jnp.mean(dxh, axis=-1, keepdims=True) - xh * jnp.mean(dxh * xh, axis=-1, keepdims=True))


def _outproj_ln1(og_hg, og_ml, x, w_out, g, b):
    t = x.shape[0]
    tm = _tile(t, DENSE_ROWS)

    def body(a_ref, b_ref, x_ref, w_ref, g_ref, bb_ref, x1_ref, xh_ref, rs_ref, x1b_ref):
        mix = _bdot(a_ref[...], w_ref[0:MIX_W, :]) + _bdot(b_ref[...], w_ref[MIX_W:2 * MIX_W, :])
        y, xh, rstd = _ln_fwd(ALPHA * x_ref[...] + mix, g_ref[...], bb_ref[...])
        x1_ref[...] = y
        x1b_ref[...] = y.astype(ACT)
        xh_ref[...] = xh
        rs_ref[...] = rstd

    return pl.pallas_call(
        body, name="outproj_ln1", grid=(t // tm,),
        in_specs=[_rows(tm, MIX_W), _rows(tm, MIX_W), _rows(tm, D_MODEL), _resident((D_MODEL, D_MODEL)), _const((1, D_MODEL)), _const((1, D_MODEL))],
        out_specs=[_rows(tm, D_MODEL), _rows(tm, D_MODEL), _rows(tm, 1), _rows(tm, D_MODEL)],
        out_shape=[jax.ShapeDtypeStruct((t, D_MODEL), f32), jax.ShapeDtypeStruct((t, D_MODEL), f32), jax.ShapeDtypeStruct((t, 1), f32),
                   jax.ShapeDtypeStruct((t, D_MODEL), ACT)],
        compiler_params=_cparams(1, arbitrary=False),
    )(og_hg, og_ml, x, w_out, g, b)


def _ffn_up(x1, wg, wu):
    t = x1.shape[0]
    tm = _tile(t, DENSE_ROWS)

    def body(x_ref, wg_ref, wu_ref, hg_ref, up_ref, a_ref):
        xv = x_ref[...]
        hg = _bdot(xv, wg_ref[...])
        up = _bdot(xv, wu_ref[...])
        hg_ref[...] = hg
        up_ref[...] = up
        a_ref[...] = (hg * _sigmoid(hg) * up).astype(ACT)

    return pl.pallas_call(
        body, name="ffn_up", grid=(t // tm,),
        in_specs=[_rows(tm, D_MODEL), _resident((D_MODEL, D_FF)), _resident((D_MODEL, D_FF))],
        out_specs=[_rows(tm, D_FF), _rows(tm, D_FF), _rows(tm, D_FF)],
        out_shape=[jax.ShapeDtypeStruct((t, D_FF), f32), jax.ShapeDtypeStruct((t, D_FF), f32), jax.ShapeDtypeStruct((t, D_FF), ACT)],
        compiler_params=_cparams(1, arbitrary=False),
    )(x1, wg, wu)


def _ffn_down_ln2(a, x1, wd, g, b):
    t = x1.shape[0]
    tm = _tile(t, DENSE_ROWS)

    def body(a_ref, x_ref, w_ref, g_ref, bb_ref, x2_ref, xh_ref, rs_ref, x2b_ref):
        ffn = _bdot(a_ref[...], w_ref[...])
        y, xh, rstd = _ln_fwd(ALPHA * x_ref[...] + ffn, g_ref[...], bb_ref[...])
        x2_ref[...] = y
        x2b_ref[...] = y.astype(ACT)
        xh_ref[...] = xh
        rs_ref[...] = rstd

    return pl.pallas_call(
        body, name="ffn_down_ln2", grid=(t // tm,),
        in_specs=[_rows(tm, D_FF), _rows(tm, D_MODEL), _resident((D_FF, D_MODEL)), _const((1, D_MODEL)), _const((1, D_MODEL))],
        out_specs=[_rows(tm, D_MODEL), _rows(tm, D_MODEL), _rows(tm, 1), _rows(tm, D_MODEL)],
        out_shape=[jax.ShapeDtypeStruct((t, D_MODEL), f32), jax.ShapeDtypeStruct((t, D_MODEL), f32), jax.ShapeDtypeStruct((t, 1), f32),
                   jax.ShapeDtypeStruct((t, D_MODEL), ACT)],
        compiler_params=_cparams(1, arbitrary=False),
    )(a, x1, wd, g, b)


def _head_loss_bwd(x2, xh2, rs2, p, tgt, w_pg, b_pg, w_pp, g2):
    t = x2.shape[0]
    tm = _tile(t, DENSE_ROWS)

    def body(x_ref, xh_ref, rs_ref, p_ref, t_ref, wg_ref, bg_ref, wp_ref, g_ref,
             dr_ref, de_ref, dz_ref, loss_ref, dbg_ref, dg2_ref, db2_ref):
        @pl.when(pl.program_id(0) == 0)
        def _():
            loss_ref[...] = jnp.zeros_like(loss_ref)
            dbg_ref[...] = jnp.zeros_like(dbg_ref)
            dg2_ref[...] = jnp.zeros_like(dg2_ref)
            db2_ref[...] = jnp.zeros_like(db2_ref)

        x2v = x_ref[...]
        z = _bdot(x2v, wg_ref[...]) + bg_ref[...]
        e = _bdot(p_ref[...], wp_ref[...])
        sg = _sigmoid(z)
        diff = x2v + sg * e - t_ref[...]
        loss_ref[...] += 0.5 * jnp.sum(jnp.mean(diff * diff, axis=-1, keepdims=True), axis=0, keepdims=True)
        dy = diff * (1.0 / D_MODEL)
        de_ref[...] = (dy * sg).astype(ACT)
        dz = dy * e * (sg * (1.0 - sg))
        dz_ref[...] = dz.astype(ACT)
        dbg_ref[...] += jnp.sum(dz, axis=0, keepdims=True)
        dx2 = dy + _bdot_nt(dz, wg_ref[...])
        xh = xh_ref[...]
        dg2_ref[...] += jnp.sum(dx2 * xh, axis=0, keepdims=True)
        db2_ref[...] += jnp.sum(dx2, axis=0, keepdims=True)
        dr_ref[...] = _ln_bwd(dx2, xh, rs_ref[...], g_ref[...])

    row = jax.ShapeDtypeStruct((1, D_MODEL), f32)
    return pl.pallas_call(
        body, name="head_loss_bwd", grid=(t // tm,),
        in_specs=[_rows(tm, D_MODEL), _rows(tm, D_MODEL), _rows(tm, 1), _rows(tm, PLE), _rows(tm, D_MODEL),
                  _resident((D_MODEL, D_MODEL)), _const((1, D_MODEL)), _resident((PLE, D_MODEL)), _const((1, D_MODEL))],
        out_specs=[_rows(tm, D_MODEL), _rows(tm, D_MODEL), _rows(tm, D_MODEL), _const((1, 1)), _const((1, D_MODEL)), _const((1, D_MODEL)), _const((1, D_MODEL))],
        out_shape=[jax.ShapeDtypeStruct((t, D_MODEL), f32), jax.ShapeDtypeStruct((t, D_MODEL), ACT), jax.ShapeDtypeStruct((t, D_MODEL), ACT),
                   jax.ShapeDtypeStruct((1, 1), f32), row, row, row],
        compiler_params=_cparams(1),
    )(x2, xh2, rs2, p, tgt, w_pg, b_pg, w_pp, g2)


def _ffn_bwd(dr2, hg, up, xh1, rs1, wd, wg, wu, g1):
    t = dr2.shape[0]
    tm = _tile(t, DENSE_ROWS // 2)

    def body(dr_ref, hg_ref, up_ref, xh_ref, rs_ref, wd_ref, wg_ref, wu_ref, g_ref,
             dr1_ref, dhg_ref, dup_ref, dg1_ref, db1_ref):
        @pl.when(pl.program_id(0) == 0)
        def _():
            dg1_ref[...] = jnp.zeros_like(dg1_ref)
            db1_ref[...] = jnp.zeros_like(db1_ref)

        dr2v = dr_ref[...]
        da = _bdot_nt(dr2v, wd_ref[...])
        hgv = hg_ref[...]
        sg = _sigmoid(hgv)
        dhg = da * up_ref[...] * (sg * (1.0 + hgv * (1.0 - sg)))
        dup = da * (hgv * sg)
        dhg_ref[...] = dhg.astype(ACT)
        dup_ref[...] = dup.astype(ACT)
        dx1 = ALPHA * dr2v + _bdot_nt(dhg, wg_ref[...]) + _bdot_nt(dup, wu_ref[...])
        xh = xh_ref[...]
        dg1_ref[...] += jnp.sum(dx1 * xh, axis=0, keepdims=True)
        db1_ref[...] += jnp.sum(dx1, axis=0, keepdims=True)
        dr1_ref[...] = _ln_bwd(dx1, xh, rs_ref[...], g_ref[...])

    row = jax.ShapeDtypeStruct((1, D_MODEL), f32)
    return pl.pallas_call(
        body, name="ffn_bwd", grid=(t // tm,),
        in_specs=[_rows(tm, D_MODEL), _rows(tm, D_FF), _rows(tm, D_FF), _rows(tm, D_MODEL), _rows(tm, 1),
                  _resident((D_FF, D_MODEL)), _resident((D_MODEL, D_FF)), _resident((D_MODEL, D_FF)), _const((1, D_MODEL))],
        out_specs=[_rows(tm, D_MODEL), _rows(tm, D_FF), _rows(tm, D_FF), _const((1, D_MODEL)), _const((1, D_MODEL))],
        out_shape=[jax.ShapeDtypeStruct((t, D_MODEL), f32), jax.ShapeDtypeStruct((t, D_FF), ACT), jax.ShapeDtypeStruct((t, D_FF), ACT), row, row],
        compiler_params=_cparams(1),
    )(dr2, hg, up, xh1, rs1, wd, wg, wu, g1)


def _outproj_bwd(dr1, w_out):
    t = dr1.shape[0]
    tm = _tile(t, DENSE_ROWS)

    def body(dr_ref, w_ref, dhg_ref, dml_ref):
        d = _bdot_nt(dr_ref[...], w_ref[...])
        dhg_ref[...] = d[:, 0:MIX_W]
        dml_ref[...] = d[:, MIX_W:2 * MIX_W]

    return pl.pallas_call(
        body, name="outproj_bwd", grid=(t // tm,),
        in_specs=[_rows(tm, D_MODEL), _resident((D_MODEL, D_MODEL))],
        out_specs=[_rows(tm, MIX_W), _rows(tm, MIX_W)],
        out_shape=[jax.ShapeDtypeStruct((t, MIX_W), f32)] * 2,
        compiler_params=_cparams(1, arbitrary=False),
    )(dr1, w_out)


def _inproj_bwd(dr1, du_hg, dqk, dmv, dmo, dgt, w_hg, w_ml):
    t = dr1.shape[0]
    tm = _tile(t, DENSE_ROWS)

    def body(dr_ref, dhg_ref, dqk_ref, dmv_ref, dmo_ref, dgt_ref, whg_ref, wml_ref, gx_ref, dml_ref):
        dml = jnp.concatenate([dqk_ref[...], dmv_ref[...], dmo_ref[...], dgt_ref[...]], axis=-1).astype(ACT)
        dml_ref[...] = dml
        gx_ref[...] = ALPHA * dr_ref[...] + _bdot_nt(dhg_ref[...], whg_ref[...]) + _bdot_nt(dml, wml_ref[...])

    return pl.pallas_call(
        body, name="inproj_bwd", grid=(t // tm,),
        in_specs=[_rows(tm, D_MODEL), _rows(tm, U_HG), _rows(tm, MIX_W), _rows(tm, MIX_W), _rows(tm, MIX_W), _rows(tm, 128),
                  _resident((D_MODEL, U_HG)), _resident((D_MODEL, U_ML))],
        out_specs=[_rows(tm, D_MODEL), _rows(tm, U_ML)],
        out_shape=[jax.ShapeDtypeStruct((t, D_MODEL), f32), jax.ShapeDtypeStruct((t, U_ML), ACT)],
        compiler_params=_cparams(1, arbitrary=False),
    )(dr1, du_hg, dqk, dmv, dmo, dgt, w_hg, w_ml)


def _wgrad(a, b, name, tk=None, tn=None):
    t, kdim = a.shape
    n = b.shape[1]
    tk = tk or kdim
    tn = tn or n
    tt = _tile(t, WGRAD_ROWS)

    def body(a_ref, b_ref, o_ref):
        @pl.when(pl.program_id(2) == 0)
        def _():
            o_ref[...] = jnp.zeros_like(o_ref)

        o_ref[...] += _bdot_tn(a_ref[...], b_ref[...])

    return pl.pallas_call(
        body, name=name, grid=(kdim // tk, n // tn, t // tt),
        in_specs=[pl.BlockSpec((tt, tk), lambda i, j, s: (s, i)), pl.BlockSpec((tt, tn), lambda i, j, s: (s, j))],
        out_specs=pl.BlockSpec((tk, tn), lambda i, j, s: (i, j)),
        out_shape=jax.ShapeDtypeStruct((kdim, n), f32),
        compiler_params=_cparams(3),
    )(a, b)


def _colsum(parts, name):
    t = parts[0].shape[0]
    tt = _tile(t, 512)
    widths = [a.shape[1] for a in parts]

    def body(*refs):
        o_ref = refs[-1]

        @pl.when(pl.program_id(0) == 0)
        def _():
            o_ref[...] = jnp.zeros_like(o_ref)

        off = 0
        for r, w in zip(refs[:-1], widths):
            o_ref[:, off:off + w] += jnp.sum(r[...].astype(f32), axis=0, keepdims=True)
            off += w

    return pl.pallas_call(
        body, name=name, grid=(t // tt,),
        in_specs=[_rows(tt, w) for w in widths],
        out_specs=_const((1, sum(widths))),
        out_shape=jax.ShapeDtypeStruct((1, sum(widths)), f32),
        compiler_params=_cparams(1),
    )(*parts)


def _local_step(x, p, tgt, w_in_b, b_in, logits, conv_w, conv_b, hg_gn, ml_gn, w_out_b, ln1_g, ln1_b,
                wg_b, wu_b, wd_b, ln2_g, ln2_b, w_pp_b, w_pg_b, b_pg, early_hook=None):
    pad_w = U_HG + U_ML - PROJ_W
    w_hg = w_in_b[:, :U_HG]
    w_ml = jnp.pad(w_in_b[:, U_HG:], ((0, 0), (0, pad_w)))
    bb_hg = b_in[:, :U_HG]
    bb_ml = jnp.pad(b_in[:, U_HG:], ((0, 0), (0, pad_w)))

    u_hg, u_ml, xb = _inproj(x, w_hg, w_ml, bb_hg, bb_ml)
    og_hg, sst = _hgrn2_fwd(u_hg, logits, hg_gn)
    pre, qkc = _conv_fwd(u_ml, conv_w, conv_b)
    og_ml, cst, nst, mst = _mlstm_fwd(qkc, u_ml, ml_gn)
    x1, xh1, rs1, x1b = _outproj_ln1(og_hg, og_ml, x, w_out_b, ln1_g, ln1_b)
    hgp, up, act = _ffn_up(x1b, wg_b, wu_b)
    x2, xh2, rs2, x2b = _ffn_down_ln2(act, x1, wd_b, ln2_g, ln2_b)
    dr2, de, dz, loss, d_bpg, d_ln2g, d_ln2b = _head_loss_bwd(x2, xh2, rs2, p, tgt, w_pg_b, b_pg, w_pp_b, ln2_g)
    dr1, dhg, dup, d_ln1g, d_ln1b = _ffn_bwd(dr2, hgp, up, xh1, rs1, wd_b, wg_b, wu_b, ln1_g)

    d_wo_a = _wgrad(og_hg, dr1, "wgrad_out_hg")
    d_wo_b = _wgrad(og_ml, dr1, "wgrad_out_ml")
    d_w_out = jnp.concatenate([d_wo_a, d_wo_b], axis=0)
    d_wg = _wgrad(x1b, dhg, "wgrad_ffn_gate", tn=D_FF // 2)
    d_wu = _wgrad(x1b, dup, "wgrad_ffn_up", tn=D_FF // 2)
    d_wd = _wgrad(act, dr2, "wgrad_ffn_down", tk=D_FF // 2)
    d_wpp = _wgrad(p, de, "wgrad_ple_proj")
    d_wpg = _wgrad(x2b, dz, "wgrad_ple_gate")
    early = dict(w_out=d_w_out, w_ffn_gate=d_wg, w_ffn_up=d_wu, w_ffn_down=d_wd, ple_w_proj=d_wpp, ple_w_gate=d_wpg)
    riders = early_hook(early) if early_hook is not None else ()

    dog_hg, dog_ml = _outproj_bwd(dr1, w_out_b)
    res = _hgrn2_bwd(u_hg, logits, hg_gn, sst, dog_hg, riders)
    du_hg, d_logits, d_hg_gn = res[:3]
    dqkc, dmv, dmo, dgt, d_ml_gn = _mlstm_bwd(qkc, u_ml, ml_gn, cst, nst, mst, dog_ml)
    dqk, d_conv_w, d_conv_b = _conv_bwd(u_ml, conv_w, pre, dqkc)
    grad_x, du_ml = _inproj_bwd(dr1, du_hg, dqk, dmv, dmo, dgt, w_hg, w_ml)

    dw_hg = _wgrad(xb, du_hg, "wgrad_in_hg", tn=1024)
    dw_ml = _wgrad(xb, du_ml, "wgrad_in_ml")
    d_w_in = jnp.concatenate([dw_hg, dw_ml[:, :PROJ_W - U_HG]], axis=1)
    d_b_in = _colsum([du_hg, du_ml], "colsum_du")[:, :PROJ_W]

    grads = dict(w_in=d_w_in, b_in=d_b_in, hg_lb_logits=d_logits, ml_conv_w=d_conv_w, ml_conv_b=d_conv_b,
                 hg_norm_g=d_hg_gn, ml_norm_g=d_ml_gn, ln1_g=d_ln1g, ln1_b=d_ln1b, ln2_g=d_ln2g, ln2_b=d_ln2b,
                 ple_b_gate=d_bpg, **early)
    return loss, grad_x, grads, list(res[3:])


_ANY = pl.BlockSpec(memory_space=pltpu.HBM)
_MESH = pl.DeviceIdType.MESH


def _my_place():
    return lax.axis_index("x"), lax.axis_index("y"), lax.axis_index("c")


def _other_chips(x, y):
    return [(1 - x, y), (x, 1 - y), (1 - x, 1 - y)]


def _allgather_weights(shards, taps, name):
    n = len(shards)
    halves = [s.shape[0] // 2 for s in shards]

    def body(*refs):
        ins, tap_in = refs[:n], refs[n]
        outs, tap_out = refs[n + 1:2 * n + 1], refs[2 * n + 1]
        send_sems, recv_sems, local_sems = refs[2 * n + 2:]
        x, y, c = _my_place()
        me = 2 * x + y
        sibling = (x, y, 1 - c)
        chips = _other_chips(x, y)

        def ici(a, j, block_chip):
            px, py = chips[j]
            src = ins[a].at[pl.ds(pl.multiple_of(c * halves[a], 16), halves[a])] if block_chip is None else outs[a].at[block_chip, c]
            dst = outs[a].at[me if block_chip is None else block_chip, c]
            return pltpu.make_async_remote_copy(src_ref=src, dst_ref=dst, send_sem=send_sems.at[6 * a + j], recv_sem=recv_sems.at[6 * a + j],
                                                device_id=(px, py, c), device_id_type=_MESH)

        def d2d(a, j, half):
            px, py = chips[j]
            blk = outs[a].at[2 * px + py, half]
            return pltpu.make_async_remote_copy(src_ref=blk, dst_ref=blk, send_sem=send_sems.at[6 * a + 3 + j], recv_sem=recv_sems.at[6 * a + 3 + j],
                                                device_id=sibling, device_id_type=_MESH)

        local = []
        for a in range(n):
            for h in range(2):
                cp = pltpu.make_async_copy(ins[a].at[pl.ds(h * halves[a], halves[a])], outs[a].at[me, h], local_sems.at[2 * a + h])
                cp.start()
                local.append(cp)
            for j in range(3):
                ici(a, j, None).start()
        tap_local = pltpu.make_async_copy(tap_in, tap_out.at[me], local_sems.at[2 * n])
        tap_local.start()
        tap_copies = []
        for j, (px, py) in enumerate(chips):
            cp = pltpu.make_async_remote_copy(src_ref=tap_in, dst_ref=tap_out.at[me], send_sem=send_sems.at[6 * n + j], recv_sem=recv_sems.at[6 * n + j],
                                              device_id=(px, py, c), device_id_type=_MESH)
            cp.start()
            tap_copies.append(cp)
        for a in range(n):
            for j, (px, py) in enumerate(chips):
                ici(a, j, 2 * px + py).wait_recv()
                d2d(a, j, c).start()
        for a in range(n):
            for j in range(3):
                d2d(a, j, 1 - c).wait_recv()
        for a in range(n):
            for j in range(3):
                ici(a, j, None).wait_send()
                d2d(a, j, c).wait_send()
        for j, (px, py) in enumerate(chips):
            pltpu.make_async_remote_copy(src_ref=tap_in, dst_ref=tap_out.at[2 * px + py], send_sem=send_sems.at[6 * n + j], recv_sem=recv_sems.at[6 * n + j],
                                         device_id=(px, py, c), device_id_type=_MESH).wait()
        for cp in local:
            cp.wait()
        tap_local.wait()

    res = pl.pallas_call(
        body, name=name,
        in_specs=[_ANY] * (n + 1), out_specs=[_ANY] * (n + 1),
        out_shape=[jax.ShapeDtypeStruct((4, 2, s.shape[0] // 2, s.shape[1]), s.dtype) for s in shards]
        + [jax.ShapeDtypeStruct((4,) + taps.shape, taps.dtype)],
        scratch_shapes=[pltpu.SemaphoreType.DMA((6 * n + 3,)), pltpu.SemaphoreType.DMA((6 * n + 3,)), pltpu.SemaphoreType.DMA((2 * n + 1,))],
    )(*shards, taps)
    return [w.reshape((4,) + s.shape) for w, s in zip(res[:n], shards)], res[n]


def _swap_halves(pieces, name):
    n = len(pieces)
    halves = [p.shape[1] // 2 for p in pieces]

    def body(*refs):
        ins, own, other = refs[:n], refs[n:2 * n], refs[2 * n:3 * n]
        send_sems, recv_sems, local_sems = refs[3 * n:]
        x, y, c = _my_place()

        def half_of(a, which):
            return ins[a].at[pl.ds(0, 4), pl.ds(pl.multiple_of(which * halves[a], 16), halves[a])]

        def to_sibling(a):
            return pltpu.make_async_remote_copy(src_ref=half_of(a, 1 - c), dst_ref=other[a], send_sem=send_sems.at[a], recv_sem=recv_sems.at[a],
                                                device_id=(x, y, 1 - c), device_id_type=_MESH)

        local = []
        for a in range(n):
            cp = pltpu.make_async_copy(half_of(a, c), own[a], local_sems.at[a])
            cp.start()
            local.append(cp)
            to_sibling(a).start()
        for a in range(n):
            to_sibling(a).wait()
            local[a].wait()

    shapes = [jax.ShapeDtypeStruct((4, p.shape[1] // 2, p.shape[2]), p.dtype) for p in pieces]
    res = pl.pallas_call(
        body, name=name,
        in_specs=[_ANY] * n, out_specs=[_ANY] * (2 * n), out_shape=shapes + shapes,
        scratch_shapes=[pltpu.SemaphoreType.DMA((n,)), pltpu.SemaphoreType.DMA((n,)), pltpu.SemaphoreType.DMA((n,))],
    )(*pieces)
    return res[:n], res[n:]


_VMEM = pl.BlockSpec(memory_space=pltpu.VMEM)
_EX_ROWS = 32


def _pair_reduce(p, name):
    s, r, c = p.shape
    half = r // 2

    def body(p_ref, o_ref, other, send_sem, recv_sem):
        x, y, cc = _my_place()
        theirs = pl.multiple_of((1 - cc) * half, 16)
        mine = pl.multiple_of(cc * half, 16)
        cp = pltpu.make_async_remote_copy(src_ref=p_ref.at[pl.ds(0, s), pl.ds(theirs, half)], dst_ref=other, send_sem=send_sem, recv_sem=recv_sem,
                                          device_id=(x, y, 1 - cc), device_id_type=_MESH)
        cp.start()
        cp.wait()

        def step(i, carry):
            r0 = pl.multiple_of(i * _EX_ROWS, _EX_ROWS)
            for slot in range(s):
                own_rows = pl.ds(pl.multiple_of(mine + r0, 16), _EX_ROWS)
                o_ref[slot, pl.ds(r0, _EX_ROWS), :] = (p_ref[slot, own_rows, :] + other[slot, pl.ds(r0, _EX_ROWS), :]).astype(bf16)
            return carry

        lax.fori_loop(0, half // _EX_ROWS, step, 0)

    return pl.pallas_call(
        body, name=name, in_specs=[_VMEM], out_specs=_VMEM,
        out_shape=jax.ShapeDtypeStruct((s, half, c), bf16),
        scratch_shapes=[pltpu.VMEM((s, half, c), f32), pltpu.SemaphoreType.DMA, pltpu.SemaphoreType.DMA],
        compiler_params=pltpu.CompilerParams(vmem_limit_bytes=VMEM_LIMIT),
    )(p)


def _chip_reduce_swap(rcv, name):
    s, h, c = rcv.shape

    def body(r_ref, g_ref, send_sem, recv_sem):
        x, y, cc = _my_place()

        def step(i, carry):
            r0 = pl.multiple_of(i * _EX_ROWS, _EX_ROWS)
            acc = r_ref[0, pl.ds(r0, _EX_ROWS), :].astype(f32)
            for slot in range(1, s):
                acc = acc + r_ref[slot, pl.ds(r0, _EX_ROWS), :].astype(f32)
            g_ref[cc, pl.ds(r0, _EX_ROWS), :] = acc
            return carry

        lax.fori_loop(0, h // _EX_ROWS, step, 0)
        cp = pltpu.make_async_remote_copy(src_ref=g_ref.at[cc], dst_ref=g_ref.at[cc], send_sem=send_sem, recv_sem=recv_sem,
                                          device_id=(x, y, 1 - cc), device_id_type=_MESH)
        cp.start()
        cp.wait()

    return pl.pallas_call(
        body, name=name, in_specs=[_VMEM], out_specs=_VMEM,
        out_shape=jax.ShapeDtypeStruct((2, h, c), f32),
        scratch_shapes=[pltpu.SemaphoreType.DMA, pltpu.SemaphoreType.DMA],
        compiler_params=pltpu.CompilerParams(vmem_limit_bytes=VMEM_LIMIT),
    )(rcv)


def _add_cast(a, b, name):
    s, r, c = a.shape
    tr = _row_tile(r, c)

    def body(a_ref, b_ref, o_ref):
        o_ref[...] = (a_ref[...] + b_ref[...]).astype(bf16)

    blk = pl.BlockSpec((1, tr, c), lambda i, j: (i, j, 0))
    return pl.pallas_call(
        body, name=name, grid=(s, r // tr), in_specs=[blk, blk], out_specs=blk,
        out_shape=jax.ShapeDtypeStruct(a.shape, bf16),
        compiler_params=_cparams(2, arbitrary=False),
    )(a, b)


def _scatter_copies(ins, outs, send_sems, recv_sems, local_sems):
    x, y, c = _my_place()
    me = 2 * x + y
    local, outgoing, incoming = [], [], []
    for a in range(len(ins)):
        local.append(pltpu.make_async_copy(ins[a].at[me], outs[a].at[me], local_sems.at[a]))
        for j, (px, py) in enumerate(_other_chips(x, y)):
            sems = dict(send_sem=send_sems.at[3 * a + j], recv_sem=recv_sems.at[3 * a + j], device_id=(px, py, c), device_id_type=_MESH)
            outgoing.append(pltpu.make_async_remote_copy(src_ref=ins[a].at[2 * px + py], dst_ref=outs[a].at[me], **sems))
            incoming.append(pltpu.make_async_remote_copy(src_ref=ins[a].at[2 * px + py], dst_ref=outs[a].at[2 * px + py], **sems))
    return local, outgoing, incoming


def _scatter_start(ins, outs, send_sems, recv_sems, local_sems):
    local, outgoing, _ = _scatter_copies(ins, outs, send_sems, recv_sems, local_sems)
    for cp in local + outgoing:
        cp.start()


def _scatter_wait(ins, outs, send_sems, recv_sems, local_sems):
    local, outgoing, incoming = _scatter_copies(ins, outs, send_sems, recv_sems, local_sems)
    for cp in incoming:
        cp.wait_recv()
    for cp in outgoing:
        cp.wait_send()
    for cp in local:
        cp.wait()


def _scatter_chips(pieces, name):
    n = len(pieces)

    def body(*refs):
        ins, outs = refs[:n], refs[n:2 * n]
        _scatter_start(ins, outs, *refs[2 * n:])
        _scatter_wait(ins, outs, *refs[2 * n:])

    return pl.pallas_call(
        body, name=name,
        in_specs=[_ANY] * n, out_specs=[_ANY] * n,
        out_shape=[jax.ShapeDtypeStruct(s.shape, s.dtype) for s in pieces],
        scratch_shapes=[pltpu.SemaphoreType.DMA((3 * n,)), pltpu.SemaphoreType.DMA((3 * n,)), pltpu.SemaphoreType.DMA((n,))],
    )(*pieces)


def _swap_cores(blocks, name):
    n = len(blocks)
    parts = 4
    rows = [b.shape[0] // parts for b in blocks]

    def body(*refs):
        ins, outs = refs[:n], refs[n:2 * n]
        send_sems, recv_sems, local_sems = refs[2 * n:]
        x, y, c = _my_place()

        def remote(a, k, slot):
            rs = pl.ds(k * rows[a], rows[a])
            return pltpu.make_async_remote_copy(src_ref=ins[a].at[rs], dst_ref=outs[a].at[slot, rs], send_sem=send_sems.at[parts * a + k],
                                                recv_sem=recv_sems.at[parts * a + k], device_id=(x, y, 1 - c), device_id_type=_MESH)

        local = []
        for a in range(n):
            cp = pltpu.make_async_copy(ins[a], outs[a].at[c], local_sems.at[a])
            cp.start()
            local.append(cp)
            for k in range(parts):
                remote(a, k, c).start()
        for a in range(n):
            for k in range(parts):
                remote(a, k, 1 - c).wait()
            local[a].wait()

    return pl.pallas_call(
        body, name=name,
        in_specs=[_ANY] * n, out_specs=[_ANY] * n,
        out_shape=[jax.ShapeDtypeStruct((2,) + s.shape, s.dtype) for s in blocks],
        scratch_shapes=[pltpu.SemaphoreType.DMA((parts * n,)), pltpu.SemaphoreType.DMA((parts * n,)), pltpu.SemaphoreType.DMA((n,))],
    )(*blocks)


def _gather_all(block, name):
    def body(in_ref, out_ref, send_sems, recv_sems, local_sem):
        x, y, c = _my_place()
        me = 4 * x + 2 * y + c
        cp = pltpu.make_async_copy(in_ref, out_ref.at[me], local_sem)
        cp.start()
        peers = []
        for dx in range(2):
            for dy in range(2):
                for dc in range(2):
                    if dx or dy or dc:
                        peers.append((1 - x if dx else x, 1 - y if dy else y, 1 - c if dc else c))
        for j, pr in enumerate(peers):
            pltpu.make_async_remote_copy(src_ref=in_ref, dst_ref=out_ref.at[me], send_sem=send_sems.at[j], recv_sem=recv_sems.at[j],
                                         device_id=pr, device_id_type=_MESH).start()
        for j, (px, py, pc) in enumerate(peers):
            pltpu.make_async_remote_copy(src_ref=in_ref, dst_ref=out_ref.at[4 * px + 2 * py + pc], send_sem=send_sems.at[j], recv_sem=recv_sems.at[j],
                                         device_id=(px, py, pc), device_id_type=_MESH).wait()
        cp.wait()

    return pl.pallas_call(
        body, name=name,
        in_specs=[_ANY], out_specs=_ANY,
        out_shape=jax.ShapeDtypeStruct((8,) + block.shape, block.dtype),
        scratch_shapes=[pltpu.SemaphoreType.DMA((7,)), pltpu.SemaphoreType.DMA((7,)), pltpu.SemaphoreType.DMA],
    )(block)


def _row_tile(r, c):
    best = r
    for cand in range(16, r + 1, 16):
        if r % cand == 0 and cand * c * 4 <= (1 << 20):
            best = cand
    return best if best * c * 4 <= (4 << 20) else r


def _sum_slots(parts, name):
    n, r, c = parts.shape
    tr = _row_tile(r, c)

    def body(p_ref, o_ref):
        acc = p_ref[0].astype(f32)
        for s in range(1, n):
            acc = acc + p_ref[s].astype(f32)
        o_ref[...] = acc

    return pl.pallas_call(
        body, name=name, grid=(r // tr,),
        in_specs=[pl.BlockSpec((n, tr, c), lambda i: (0, i, 0))],
        out_specs=pl.BlockSpec((tr, c), lambda i: (i, 0)),
        out_shape=jax.ShapeDtypeStruct((r, c), f32),
        compiler_params=_cparams(1, arbitrary=False),
    )(parts)


def _adamw(parts, w, m, v, name):
    n, r, c = parts.shape
    tr = _row_tile(r, c)

    def body(p_ref, w_ref, m_ref, v_ref, g_ref, d_ref, nm_ref, nv_ref):
        g = p_ref[0]
        for s in range(1, n):
            g = g + p_ref[s]
        nm = B1 * m_ref[...] + (1.0 - B1) * g
        nv = B2 * v_ref[...] + (1.0 - B2) * (g * g)
        m_hat = nm / (1.0 - B1 ** STEP)
        v_hat = nv / (1.0 - B2 ** STEP)
        g_ref[...] = g
        nm_ref[...] = nm
        nv_ref[...] = nv
        d_ref[...] = -LR * (m_hat / (jnp.sqrt(v_hat) + EPS_ADAM) + WD * w_ref[...])

    blk = pl.BlockSpec((tr, c), lambda i: (i, 0))
    return pl.pallas_call(
        body, name=name, grid=(r // tr,),
        in_specs=[pl.BlockSpec((n, tr, c), lambda i: (0, i, 0)), blk, blk, blk],
        out_specs=[blk] * 4,
        out_shape=[jax.ShapeDtypeStruct((r, c), f32)] * 4,
        compiler_params=_cparams(1, arbitrary=False),
    )(parts, w, m, v)


_BIG = ["w_in", "w_out", "w_ffn_gate", "w_ffn_up", "w_ffn_down", "ple_w_proj", "ple_w_gate"]
_COL_SPLIT = {"w_in", "w_ffn_gate", "w_ffn_up", "ple_w_proj"}
_SMALL = ["b_in", "hg_lb_logits", "ml_conv_w", "ml_conv_b", "hg_norm_g", "ml_norm_g", "ln1_g", "ln1_b", "ln2_g", "ln2_b", "ple_b_gate"]
_ORDER = ["w_in", "b_in", "hg_lb_logits", "ml_conv_w", "ml_conv_b", "hg_norm_g", "ml_norm_g", "w_out", "ln1_g", "ln1_b",
          "w_ffn_gate", "w_ffn_up", "w_ffn_down", "ln2_g", "ln2_b", "ple_w_proj", "ple_w_gate", "ple_b_gate"]
_PACK_ROWS, _PACK_COLS = 16, 1024


def _pack(arrays):
    flat = jnp.concatenate([a.reshape(-1) for a in arrays])
    return jnp.pad(flat, (0, _PACK_ROWS * _PACK_COLS - flat.shape[0])).reshape(_PACK_ROWS, _PACK_COLS)


def _unpack(pack, shapes):
    flat = pack.reshape(-1)
    out, off = [], 0
    for s in shapes:
        size = 1
        for d in s:
            size *= d
        out.append(flat[off:off + size].reshape(s))
        off += size
    return out


def _to_chip_major(g, col_split):
    if col_split:
        k, n = g.shape
        return g.reshape(k, 4, n // 4).transpose(1, 0, 2)
    k, n = g.shape
    return g.reshape(4, k // 4, n)


def _from_chip_major(a, col_split):
    if col_split:
        return a.transpose(1, 0, 2).reshape(a.shape[1], 4 * a.shape[2])
    return a.reshape(4 * a.shape[1], a.shape[2])


def kernel(x, p, w_in, b_in, hg_lb_logits, ml_conv_w, ml_conv_b, hg_norm_g, ml_norm_g, w_out, ln1_g, ln1_b, w_ffn_gate, w_ffn_up, w_ffn_down, ln2_g, ln2_b, ple_w_proj, ple_w_gate, ple_b_gate, loss_target, m_w_in, m_b_in, m_hg_lb_logits, m_ml_conv_w, m_ml_conv_b, m_hg_norm_g, m_ml_norm_g, m_w_out, m_ln1_g, m_ln1_b, m_w_ffn_gate, m_w_ffn_up, m_w_ffn_down, m_ln2_g, m_ln2_b, m_ple_w_proj, m_ple_w_gate, m_ple_b_gate, v_w_in, v_b_in, v_hg_lb_logits, v_ml_conv_w, v_ml_conv_b, v_hg_norm_g, v_ml_norm_g, v_w_out, v_ln1_g, v_ln1_b, v_w_ffn_gate, v_w_ffn_up, v_w_ffn_down, v_ln2_g, v_ln2_b, v_ple_w_proj, v_ple_w_gate, v_ple_b_gate):
    args = dict(locals())
    wts = {k: args[k] for k in _ORDER}
    mom = {k: args["m_" + k] for k in _ORDER}
    var = {k: args["v_" + k] for k in _ORDER}
    two_d = lambda a: a.reshape(a.shape[-2], a.shape[-1])

    shards = [two_d(wts[k]).astype(bf16) for k in _BIG]
    gathered, taps = _allgather_weights(shards, two_d(ml_conv_w), "allgather_weights")
    full = {k: _from_chip_major(g, k in _COL_SPLIT) for k, g in zip(_BIG, gathered)}
    conv_w_full = _from_chip_major(taps, True)

    def core_sum(k, g):
        return _pair_reduce(_to_chip_major(g, k in _COL_SPLIT), "pair_reduce_" + k)

    early_keys = _BIG[1:]
    loss, grad_x, grads, early_received = _local_step(
        x[0], p[0, 0], loss_target[0], full["w_in"], b_in, hg_lb_logits, conv_w_full, ml_conv_b, hg_norm_g, ml_norm_g,
        full["w_out"], ln1_g, ln1_b, full["w_ffn_gate"], full["w_ffn_up"], full["w_ffn_down"], ln2_g, ln2_b,
        full["ple_w_proj"], full["ple_w_gate"], ple_b_gate,
        early_hook=lambda early: [core_sum(k, early[k]) for k in early_keys])

    received = list(_scatter_chips([core_sum("w_in", grads["w_in"])], "scatter_grad_w_in")) + list(early_received)
    both = [_chip_reduce_swap(r, "chip_reduce_" + k) for k, r in zip(_BIG, received)]
    out_g, out_d, out_m, out_v = {}, {}, {}, {}
    for k, parts in zip(_BIG, both):
        whole = parts.reshape(1, 2 * parts.shape[1], parts.shape[2])
        g, d, nm, nv = _adamw(whole, two_d(wts[k]), two_d(mom[k]), two_d(var[k]), "adamw_" + k)
        shp = wts[k].shape
        out_g[k], out_d[k], out_m[k], out_v[k] = g.reshape(shp), d.reshape(shp), nm.reshape(shp), nv.reshape(shp)

    small_shapes = [(1, PROJ_W), (2, MIX_W), (CONV_K, MIX_W)] + [(1, MIX_W)] * 3 + [(1, D_MODEL)] * 5 + [(1, 1)]
    contrib = _pack([grads[k] for k in _SMALL] + [loss])
    summed = _sum_slots(_gather_all(contrib, "gather_small"), "sum_small")
    small = _unpack(summed, small_shapes)
    loss_total = small[-1].reshape(())
    gsm = dict(zip(_SMALL, small[:-1]))
    place = 2 * lax.axis_index("x") + lax.axis_index("y")
    conv_cols = ml_conv_w.shape[-1]
    gsm["ml_conv_w"] = lax.dynamic_slice(gsm["ml_conv_w"], (0, place * conv_cols), (CONV_K, conv_cols))
    own_shapes = [wts[k].shape for k in _SMALL]
    g_pack = _pack([gsm[k] for k in _SMALL])
    res = _adamw(g_pack[None], _pack([wts[k] for k in _SMALL]), _pack([mom[k] for k in _SMALL]), _pack([var[k] for k in _SMALL]), "adamw_small")
    for dst, pack in zip((out_g, out_d, out_m, out_v), res):
        for k, a in zip(_SMALL, _unpack(pack, own_shapes)):
            dst[k] = a

    outs = [loss_total, grad_x[None]]
    for group in (out_g, out_d, out_m, out_v):
        outs += [group[k] for k in _ORDER]
    return tuple(outs)
```

```python
import functools

import jax
import jax.numpy as jnp
from jax import lax
from jax.experimental import pallas as pl
from jax.experimental.pallas import tpu as pltpu

f32 = jnp.float32
bf16 = jnp.bfloat16
HI = lax.Precision.HIGHEST

D_MODEL = 1024
HEADS = 4
HEAD_W = 128
MIX_W = HEADS * HEAD_W
ML_DQK = 64
PROJ_W = 3592
U_HG = 4 * MIX_W
U_ML = 3 * MIX_W + 128
D_FF = 2816
PLE = 256
CHUNK = 128
SUB = 16
EXP_CAP = 80.0
CONV_K = 4
HALO = 8
ALPHA = float(2.0 ** 0.25)
LN_EPS = 1e-5
RMS_EPS = 1e-6
NEG = -1e30
LR, B1, B2, EPS_ADAM, WD, STEP = 0.001, 0.9, 0.999, 1e-08, 0.01, 10
VMEM_LIMIT = 56 * 1024 * 1024
DENSE_ROWS = 512
WGRAD_ROWS = 2048


def _cparams(n_axes, arbitrary=True):
    sem = ("arbitrary",) * n_axes if arbitrary else ("parallel",) * n_axes
    return pltpu.CompilerParams(dimension_semantics=sem, vmem_limit_bytes=VMEM_LIMIT)


ACT = bf16


def _mx(a):
    return a.astype(ACT)


def _bdot(a, b):
    return jnp.dot(_mx(a), _mx(b), preferred_element_type=f32)


def _bdot_nt(a, b):
    return lax.dot_general(_mx(a), _mx(b), (((1,), (1,)), ((), ())), preferred_element_type=f32)


def _bdot_tn(a, b):
    return lax.dot_general(_mx(a), _mx(b), (((0,), (0,)), ((), ())), preferred_element_type=f32)


def _split3(x):
    hi = x.astype(bf16)
    r1 = x - hi.astype(f32)
    mid = r1.astype(bf16)
    lo = (r1 - mid.astype(f32)).astype(bf16)
    return hi, mid, lo


def _dot3(a, b, dims):
    a_hi = a.astype(bf16)
    a_lo = (a - a_hi.astype(f32)).astype(bf16)
    b_hi = b.astype(bf16)
    b_lo = (b - b_hi.astype(f32)).astype(bf16)
    dn = (dims, ((), ()))
    return (lax.dot_general(a_hi, b_hi, dn, preferred_element_type=f32) + lax.dot_general(a_hi, b_lo, dn, preferred_element_type=f32)
            + lax.dot_general(a_lo, b_hi, dn, preferred_element_type=f32))


def _sel_dot(sel, x):
    sb = sel.astype(bf16)
    return sum(jnp.dot(sb, part, preferred_element_type=f32) for part in _split3(x))


def _sel_dot_nt(sel, x):
    sb = sel.astype(bf16)
    return sum(lax.dot_general(sb, part, (((1,), (1,)), ((), ())), preferred_element_type=f32) for part in _split3(x))


def _sigmoid(x):
    return 1.0 / (1.0 + jnp.exp(-x))


def _log_sigmoid(x):
    return jnp.minimum(x, 0.0) - jnp.log(1.0 + jnp.exp(-jnp.abs(x)))


def _tri(n, upper=False):
    r = lax.broadcasted_iota(jnp.int32, (n, n), 0)
    c = lax.broadcasted_iota(jnp.int32, (n, n), 1)
    return (c >= r) if upper else (c <= r)


def _rows(tm, n, col=0):
    return pl.BlockSpec((tm, n), lambda i, _c=col: (i, _c))


def _rows_rev(tm, n, nb, col=0):
    return pl.BlockSpec((tm, n), lambda i, _c=col, _nb=nb: (_nb - 1 - i, _c))


def _const(shape):
    return pl.BlockSpec(shape, lambda i, _n=len(shape): (0,) * _n)


def _resident(shape):
    return pl.BlockSpec(shape, lambda i, _n=len(shape): (0,) * _n, pipeline_mode=pl.Buffered(1))


def _tile(t, want):
    return want if t % want == 0 else t


def _inproj(x, w_hg, w_ml, b_hg, b_ml, riders=()):
    t = x.shape[0]
    tm = _tile(t, DENSE_ROWS)

    def body(x_ref, whg_ref, wml_ref, bhg_ref, bml_ref, uhg_ref, uml_ref, xb_ref):
        xb = _mx(x_ref[...])
        xb_ref[...] = xb
        uhg_ref[...] = _bdot(xb, whg_ref[...]) + bhg_ref[...]
        uml_ref[...] = _bdot(xb, wml_ref[...]) + bml_ref[...]

    return _riding_call(
        body, "inproj", t // tm,
        in_specs=[_rows(tm, D_MODEL), _resident((D_MODEL, U_HG)), _resident((D_MODEL, U_ML)), _const((1, U_HG)), _const((1, U_ML))],
        out_specs=[_rows(tm, U_HG), _rows(tm, U_ML), _rows(tm, D_MODEL)],
        out_shape=[jax.ShapeDtypeStruct((t, U_HG), f32), jax.ShapeDtypeStruct((t, U_ML), f32), jax.ShapeDtypeStruct((t, D_MODEL), ACT)],
        scratch_shapes=[], operands=(x, w_hg, w_ml, b_hg, b_ml), riders=riders, copies=_gather_copies, ride_shapes=_gather_shapes(riders))


def _hg_gates(hq, hf, lb, tri):
    s = _sigmoid(hf)
    om = 1.0 - lb
    f = lb + om * s
    g = jnp.log(f)
    k = om * (1.0 - s)
    sq = _sigmoid(hq)
    q = hq * sq
    b = _sel_dot(tri, g)
    return q, sq, s, f, k, b


def _hg_scores(q, k, b, tril_mask):
    qts, kts, eqs, eks, rows = [], [], [], [], []
    for i in range(CHUNK // SUB):
        lo = i * SUB
        ref = jnp.zeros_like(b[0:1]) if i == 0 else b[lo - 1:lo]
        eq = jnp.exp(b[lo:lo + SUB] - ref)
        ek = jnp.exp(jnp.minimum(ref - b, EXP_CAP))
        qt = q[lo:lo + SUB] * eq
        kt = k * ek
        rows.append(_bdot_nt(qt, kt))
        qts.append(qt); kts.append(kt); eqs.append(eq); eks.append(ek)
    a = jnp.where(tril_mask, jnp.concatenate(rows, axis=0), 0.0)
    return a, qts, kts, eqs, eks


def _head_rms(o, gn):
    rstd = lax.rsqrt(jnp.mean(o * o, axis=-1, keepdims=True) + RMS_EPS)
    oh = o * rstd
    return oh, rstd, oh * gn


def _lower_bound(logit_ref):
    lg = logit_ref[...]
    return _sigmoid(lg[0:1] - lg[1:2])


def _hgrn2_fwd(u_hg, logits, gn, riders=()):
    t = u_hg.shape[0]
    tb = _tile(t, 256)
    nc_blk = tb // CHUNK

    def body(u_ref, lg_ref, gn_ref, og_ref, sst_ref, st_ref):
        @pl.when(pl.program_id(0) == 0)
        def _():
            st_ref[...] = jnp.zeros_like(st_ref)

        lb_all = _lower_bound(lg_ref)
        tril_mask = _tri(CHUNK)
        tri = tril_mask.astype(f32)

        def chunk(c, carry):
            r0 = pl.multiple_of(c * CHUNK, CHUNK)
            rows = pl.ds(r0, CHUNK)
            heads = range(HEADS)
            cols = [slice(h * HEAD_W, (h + 1) * HEAD_W) for h in heads]
            hv = [u_ref[rows, 2 * MIX_W + h * HEAD_W:2 * MIX_W + (h + 1) * HEAD_W] for h in heads]
            gts = [_hg_gates(u_ref[rows, h * HEAD_W:(h + 1) * HEAD_W], u_ref[rows, MIX_W + h * HEAD_W:MIX_W + (h + 1) * HEAD_W],
                             lb_all[:, cols[h]], tri) for h in heads]
            q = [g[0] for g in gts]
            k = [g[4] for g in gts]
            b = [g[5] for g in gts]
            a = [_hg_scores(q[h], k[h], b[h], tril_mask)[0] for h in heads]
            st = [st_ref[h] for h in heads]
            bl = [b[h][CHUNK - 1:CHUNK] for h in heads]
            o = [_bdot(a[h], hv[h]) + _bdot_nt(q[h] * jnp.exp(b[h]), st[h]) for h in heads]
            new_st = [st[h] * jnp.exp(bl[h]) + _bdot_tn(hv[h], k[h] * jnp.exp(bl[h] - b[h])) for h in heads]
            for h in heads:
                sst_ref[c, h] = st[h]
                st_ref[h] = new_st[h]
                hgate = u_ref[rows, 3 * MIX_W + h * HEAD_W:3 * MIX_W + (h + 1) * HEAD_W]
                _, _, y = _head_rms(o[h], gn_ref[:, cols[h]])
                og_ref[rows, cols[h]] = (y * (hgate * _sigmoid(hgate))).astype(ACT)
            return carry

        lax.fori_loop(0, nc_blk, chunk, 0)

    return _riding_call(
        body, "hgrn2_fwd", t // tb,
        in_specs=[_rows(tb, U_HG), _const((2, MIX_W)), _const((1, MIX_W))],
        out_specs=[_rows(tb, MIX_W), pl.BlockSpec((nc_blk, HEADS, HEAD_W, HEAD_W), lambda i: (i, 0, 0, 0))],
        out_shape=[jax.ShapeDtypeStruct((t, MIX_W), ACT), jax.ShapeDtypeStruct((t // CHUNK, HEADS, HEAD_W, HEAD_W), f32)],
        scratch_shapes=[pltpu.VMEM((HEADS, HEAD_W, HEAD_W), f32)],
        operands=(u_hg, logits, gn), riders=riders, copies=_gather_copies, ride_shapes=_gather_shapes(riders))


def _hgrn2_bwd(u_hg, logits, gn, sst, dog, riders=()):
    t = u_hg.shape[0]
    tb = _tile(t, 256)
    nb = t // tb
    nc_blk = tb // CHUNK
    nr = len(riders)

    def body(*refs):
        u_ref, lg_ref, gn_ref, sst_ref, dog_ref = refs[:5]
        ride_in = refs[5:5 + nr]
        du_ref, dlg_ref, dgn_ref = refs[5 + nr:8 + nr]
        ride_out = refs[8 + nr:8 + 2 * nr]
        dst_ref = refs[8 + 2 * nr]
        ride_sems = refs[9 + 2 * nr:]

        @pl.when(pl.program_id(0) == 0)
        def _():
            dst_ref[...] = jnp.zeros_like(dst_ref)
            dlg_ref[...] = jnp.zeros_like(dlg_ref)
            dgn_ref[...] = jnp.zeros_like(dgn_ref)
            if nr:
                _scatter_start(ride_in, ride_out, *ride_sems)

        lb_all = _lower_bound(lg_ref)
        tril_mask = _tri(CHUNK)
        tri = tril_mask.astype(f32)
        triu = _tri(CHUNK, upper=True).astype(f32)

        def chunk(j, carry):
            c = nc_blk - 1 - j
            r0 = pl.multiple_of(c * CHUNK, CHUNK)
            rows = pl.ds(r0, CHUNK)
            heads = range(HEADS)
            nsub = CHUNK // SUB
            cols = [slice(h * HEAD_W, (h + 1) * HEAD_W) for h in heads]
            hq = [u_ref[rows, h * HEAD_W:(h + 1) * HEAD_W] for h in heads]
            hf = [u_ref[rows, MIX_W + h * HEAD_W:MIX_W + (h + 1) * HEAD_W] for h in heads]
            hv = [u_ref[rows, 2 * MIX_W + h * HEAD_W:2 * MIX_W + (h + 1) * HEAD_W] for h in heads]
            lb = [lb_all[:, cols[h]] for h in heads]
            gts = [_hg_gates(hq[h], hf[h], lb[h], tri) for h in heads]
            q, sq, s, f, k, b = ([g[n] for g in gts] for n in range(6))
            scs = [_hg_scores(q[h], k[h], b[h], tril_mask) for h in heads]
            a, qts, kts, eqs, eks = ([sc[n] for sc in scs] for n in range(5))
            st = [sst_ref[c, h] for h in heads]
            dst = [dst_ref[h] for h in heads]
            bl = [b[h][CHUNK - 1:CHUNK] for h in heads]
            eb = [jnp.exp(b[h]) for h in heads]
            qh = [q[h] * eb[h] for h in heads]
            ekl = [jnp.exp(bl[h] - b[h]) for h in heads]
            kh = [k[h] * ekl[h] for h in heads]
            o = [_bdot(a[h], hv[h]) + _bdot_nt(qh[h], st[h]) for h in heads]
            do = []
            for h in heads:
                hgate = u_ref[rows, 3 * MIX_W + h * HEAD_W:3 * MIX_W + (h + 1) * HEAD_W]
                gnh = gn_ref[:, cols[h]]
                oh, rstd, y = _head_rms(o[h], gnh)
                sg = _sigmoid(hgate)
                dogh = dog_ref[rows, cols[h]]
                dy = dogh * (hgate * sg)
                du_ref[rows, 3 * MIX_W + h * HEAD_W:3 * MIX_W + (h + 1) * HEAD_W] = (dogh * y * (sg * (1.0 + hgate * (1.0 - sg)))).astype(ACT)
                dgn_ref[:, cols[h]] += jnp.sum(dy * oh, axis=0, keepdims=True)
                doh = dy * gnh
                do.append(rstd * (doh - oh * jnp.mean(doh * oh, axis=-1, keepdims=True)))
            da = [jnp.where(tril_mask, _bdot_nt(do[h], hv[h]), 0.0) for h in heads]
            dv = [_bdot_tn(a[h], do[h]) + _bdot_nt(kh[h], dst[h]) for h in heads]
            dq = [_bdot(do[h], st[h]) * eb[h] for h in heads]
            dk = [_bdot(hv[h], dst[h]) * ekl[h] for h in heads]
            d_last = [jnp.sum(k[h] * dk[h], axis=0, keepdims=True) + jnp.exp(bl[h]) * jnp.sum(dst[h] * st[h], axis=0, keepdims=True)
                      for h in heads]
            dqs = [[] for _ in heads]
            for i in range(nsub):
                for h in heads:
                    da_i = da[h][i * SUB:(i + 1) * SUB]
                    dqs[h].append(_dot3(da_i, kts[h][i], ((1,), (0,))) * eqs[h][i])
                    dk[h] = dk[h] + _dot3(da_i, qts[h][i], ((0,), (0,))) * eks[h][i]
            for h in heads:
                dq[h] = dq[h] + jnp.concatenate(dqs[h], axis=0)
                dst_ref[h] = dst[h] * jnp.exp(bl[h]) + _bdot_tn(do[h], qh[h])
            dg = [_sel_dot(triu, q[h] * dq[h] - k[h] * dk[h]) + d_last[h] for h in heads]
            for h in heads:
                dfk = dg[h] / f[h] - dk[h]
                du_ref[rows, h * HEAD_W:(h + 1) * HEAD_W] = (dq[h] * (sq[h] * (1.0 + hq[h] * (1.0 - sq[h])))).astype(ACT)
                du_ref[rows, MIX_W + h * HEAD_W:MIX_W + (h + 1) * HEAD_W] = ((1.0 - lb[h]) * dfk * s[h] * (1.0 - s[h])).astype(ACT)
                du_ref[rows, 2 * MIX_W + h * HEAD_W:2 * MIX_W + (h + 1) * HEAD_W] = dv[h].astype(ACT)
                dlb = jnp.sum((1.0 - s[h]) * dfk, axis=0, keepdims=True) * (lb[h] * (1.0 - lb[h]))
                dlg_ref[0:1, cols[h]] += dlb
                dlg_ref[1:2, cols[h]] -= dlb
            return carry

        lax.fori_loop(0, nc_blk, chunk, 0)

        if nr:
            @pl.when(pl.program_id(0) == nb - 1)
            def _():
                _scatter_wait(ride_in, ride_out, *ride_sems)

    hbm = pl.BlockSpec(memory_space=pltpu.HBM)
    ride_scratch = [pltpu.SemaphoreType.DMA((3 * nr,)), pltpu.SemaphoreType.DMA((3 * nr,)), pltpu.SemaphoreType.DMA((nr,))] if nr else []
    return pl.pallas_call(
        body, name="hgrn2_bwd", grid=(nb,),
        in_specs=[_rows_rev(tb, U_HG, nb), _const((2, MIX_W)), _const((1, MIX_W)),
                  pl.BlockSpec((nc_blk, HEADS, HEAD_W, HEAD_W), lambda i: (nb - 1 - i, 0, 0, 0)), _rows_rev(tb, MIX_W, nb)] + [hbm] * nr,
        out_specs=[_rows_rev(tb, U_HG, nb), _const((2, MIX_W)), _const((1, MIX_W))] + [hbm] * nr,
        out_shape=[jax.ShapeDtypeStruct((t, U_HG), ACT), jax.ShapeDtypeStruct((2, MIX_W), f32), jax.ShapeDtypeStruct((1, MIX_W), f32)]
        + [jax.ShapeDtypeStruct(r.shape, r.dtype) for r in riders],
        scratch_shapes=[pltpu.VMEM((HEADS, HEAD_W, HEAD_W), f32)] + ride_scratch,
        compiler_params=_cparams(1),
    )(u_hg, logits, gn, sst, dog, *riders)


def _conv_fwd(u_ml, w, b):
    t = u_ml.shape[0]
    tm = _tile(t, 512)

    def body(x_ref, w_ref, b_ref, pre_ref, act_ref, xbuf):
        @pl.when(pl.program_id(0) == 0)
        def _():
            xbuf[...] = jnp.zeros_like(xbuf)

        xbuf[0:HALO, :] = xbuf[tm:tm + HALO, :]
        xbuf[HALO:HALO + tm, :] = x_ref[...]
        pre = b_ref[...] + jnp.zeros((tm, MIX_W), f32)
        for kk in range(CONV_K):
            off = HALO - (CONV_K - 1) + kk
            pre = pre + w_ref[kk:kk + 1, :] * xbuf[off:off + tm, :]
        pre_ref[...] = pre
        act_ref[...] = pre * _sigmoid(pre)

    return pl.pallas_call(
        body, name="conv_fwd", grid=(t // tm,),
        in_specs=[_rows(tm, MIX_W), _const((CONV_K, MIX_W)), _const((1, MIX_W))],
        out_specs=[_rows(tm, MIX_W), _rows(tm, MIX_W)],
        out_shape=[jax.ShapeDtypeStruct((t, MIX_W), f32)] * 2,
        scratch_shapes=[pltpu.VMEM((tm + HALO, MIX_W), f32)],
        compiler_params=_cparams(1),
    )(u_ml, w, b)


def _conv_bwd(u_ml, w, pre, dact):
    t = u_ml.shape[0]
    tm = _tile(t, 512)
    nb = t // tm
    hb = tm // HALO

    def body(x_ref, halo_ref, w_ref, pre_ref, dact_ref, dx_ref, dw_ref, db_ref, dbuf, xbuf):
        i = pl.program_id(0)

        @pl.when(i == 0)
        def _():
            dbuf[...] = jnp.zeros_like(dbuf)
            dw_ref[...] = jnp.zeros_like(dw_ref)
            db_ref[...] = jnp.zeros_like(db_ref)

        p = pre_ref[...]
        sg = _sigmoid(p)
        dpre = dact_ref[...] * (sg * (1.0 + p * (1.0 - sg)))
        dbuf[tm:tm + HALO, :] = dbuf[0:HALO, :]
        dbuf[0:tm, :] = dpre
        has_prev = (i < nb - 1).astype(f32)
        xbuf[0:HALO, :] = halo_ref[...] * has_prev
        xbuf[HALO:HALO + tm, :] = x_ref[...]
        dx = jnp.zeros((tm, MIX_W), f32)
        for kk in range(CONV_K):
            back = CONV_K - 1 - kk
            dx = dx + w_ref[kk:kk + 1, :] * dbuf[back:back + tm, :]
            off = HALO - (CONV_K - 1) + kk
            dw_ref[kk:kk + 1, :] += jnp.sum(dpre * xbuf[off:off + tm, :], axis=0, keepdims=True)
        dx_ref[...] = dx.astype(ACT)
        db_ref[...] += jnp.sum(dpre, axis=0, keepdims=True)

    return pl.pallas_call(
        body, name="conv_bwd", grid=(nb,),
        in_specs=[_rows_rev(tm, MIX_W, nb),
                  pl.BlockSpec((HALO, MIX_W), lambda i: (jnp.maximum((nb - 1 - i) * hb - 1, 0), 0)),
                  _const((CONV_K, MIX_W)), _rows_rev(tm, MIX_W, nb), _rows_rev(tm, MIX_W, nb)],
        out_specs=[_rows_rev(tm, MIX_W, nb), _const((CONV_K, MIX_W)), _const((1, MIX_W))],
        out_shape=[jax.ShapeDtypeStruct((t, MIX_W), ACT), jax.ShapeDtypeStruct((CONV_K, MIX_W), f32), jax.ShapeDtypeStruct((1, MIX_W), f32)],
        scratch_shapes=[pltpu.VMEM((tm + HALO, MIX_W), f32), pltpu.VMEM((tm + HALO, MIX_W), f32)],
        compiler_params=_cparams(1),
    )(u_ml, u_ml, w, pre, dact)


def _lane_pick(x, lane):
    idx = lax.broadcasted_iota(jnp.int32, x.shape, 1)
    return jnp.sum(jnp.where(idx == lane, x, 0.0), axis=-1, keepdims=True)


def _ml_gate_forms(gates, tri):
    lf = _log_sigmoid(gates)
    gc = _sel_dot(tri, lf)
    lane = lax.broadcasted_iota(jnp.int32, gates.shape, 1)
    mixed = jnp.where(lane < HEADS, gates, gc)
    sel = (lax.broadcasted_iota(jnp.int32, (8, 128), 0) == lax.broadcasted_iota(jnp.int32, (8, 128), 1)).astype(f32)
    rowsf = _sel_dot_nt(sel, mixed)
    return gc, rowsf


def _ml_chunk(q, k, v, gates, gc, rowsf, c_st, n_st, m_st, tril_mask):
    hs = range(HEADS)
    g_col = [_lane_pick(gc, HEADS + h) for h in hs]
    ig_col = [_lane_pick(gates, h) for h in hs]
    dmat = [jnp.where(tril_mask, g_col[h] - rowsf[HEADS + h:HEADS + h + 1, :] + rowsf[h:h + 1, :], NEG) for h in hs]
    m_inter = [g_col[h] + m_st[h] for h in hs]
    m_t = [jnp.maximum(m_inter[h], jnp.max(dmat[h], axis=-1, keepdims=True)) for h in hs]
    wi = [jnp.exp(dmat[h] - m_t[h]) for h in hs]
    wo = [jnp.exp(m_inter[h] - m_t[h]) for h in hs]
    qk = [_bdot_nt(q[h], k[h]) * wi[h] for h in hs]
    num = [_bdot(qk[h], v[h]) + wo[h] * _bdot(q[h], c_st[h]) for h in hs]
    den = [jnp.sum(qk[h], axis=-1, keepdims=True) + wo[h] * jnp.sum(q[h] * n_st[h], axis=-1, keepdims=True) for h in hs]
    floor = [jnp.exp(-m_t[h]) for h in hs]
    z = [jnp.maximum(jnp.abs(den[h]), floor[h]) for h in hs]
    g_last = [g_col[h][CHUNK - 1:CHUNK] for h in hs]
    a_col = [g_last[h] - g_col[h] + ig_col[h] for h in hs]
    m_new = [jnp.maximum(g_last[h] + m_st[h], jnp.max(a_col[h], axis=0, keepdims=True)) for h in hs]
    ws = [jnp.exp(a_col[h] - m_new[h]) for h in hs]
    w_old = [jnp.exp(g_last[h] + m_st[h] - m_new[h]) for h in hs]
    return dict(wi=wi, wo=wo, qk=qk, num=num, den=den, z=z, floor=floor, ws=ws, w_old=w_old, m_new=m_new)


def _mlstm_fwd(qkc, u_ml, gn, riders=()):
    t = qkc.shape[0]
    tb = _tile(t, 256)
    nc_blk = tb // CHUNK

    def body(qk_ref, v_ref, mo_ref, gt_ref, gn_ref, og_ref, cst_ref, nst_ref, mst_ref, c_sc, n_sc, m_sc):
        @pl.when(pl.program_id(0) == 0)
        def _():
            c_sc[...] = jnp.zeros_like(c_sc)
            n_sc[...] = jnp.zeros_like(n_sc)
            m_sc[...] = jnp.zeros_like(m_sc)

        tril_mask = _tri(CHUNK)
        tri = tril_mask.astype(f32)

        def chunk(c, carry):
            r0 = pl.multiple_of(c * CHUNK, CHUNK)
            rows = pl.ds(r0, CHUNK)
            gates = gt_ref[rows, :]
            gc, rowsf = _ml_gate_forms(gates, tri)
            hs = range(HEADS)
            q = [qk_ref[rows, h * ML_DQK:(h + 1) * ML_DQK] * (ML_DQK ** -0.5) for h in hs]
            k = [qk_ref[rows, HEADS * ML_DQK + h * ML_DQK:HEADS * ML_DQK + (h + 1) * ML_DQK] for h in hs]
            v = [v_ref[rows, h * HEAD_W:(h + 1) * HEAD_W] for h in hs]
            c_st = [c_sc[h] for h in hs]
            n_st = [n_sc[h] for h in hs]
            m_full = [m_sc[h] for h in hs]
            r = _ml_chunk(q, k, v, gates, gc, rowsf, c_st, n_st, [m[:, 0:1] for m in m_full], tril_mask)
            ksc = [k[h] * r["ws"][h] for h in hs]
            new_c = [r["w_old"][h] * c_st[h] + _bdot_tn(ksc[h], v[h]) for h in hs]
            for h in hs:
                cs = slice(h * HEAD_W, (h + 1) * HEAD_W)
                cst_ref[c, h] = c_st[h]
                nst_ref[c, h] = n_st[h]
                mst_ref[c, h] = m_full[h]
                c_sc[h] = new_c[h]
                n_sc[h] = r["w_old"][h] * n_st[h] + jnp.sum(ksc[h], axis=0, keepdims=True)
                m_sc[h] = r["m_new"][h] + jnp.zeros((1, 128), f32)
                _, _, y = _head_rms(r["num"][h] / r["z"][h], gn_ref[:, cs])
                og_ref[rows, cs] = (y * _sigmoid(mo_ref[rows, h * HEAD_W:(h + 1) * HEAD_W])).astype(ACT)
            return carry

        lax.fori_loop(0, nc_blk, chunk, 0)

    nchunks = t // CHUNK
    return _riding_call(
        body, "mlstm_fwd", t // tb,
        in_specs=[_rows(tb, MIX_W), _rows(tb, MIX_W, 1), _rows(tb, MIX_W, 2), _rows(tb, 128, 12), _const((1, MIX_W))],
        out_specs=[_rows(tb, MIX_W),
                   pl.BlockSpec((nc_blk, HEADS, ML_DQK, HEAD_W), lambda i: (i, 0, 0, 0)),
                   pl.BlockSpec((nc_blk, HEADS, 1, ML_DQK), lambda i: (i, 0, 0, 0)),
                   pl.BlockSpec((nc_blk, HEADS, 1, 128), lambda i: (i, 0, 0, 0))],
        out_shape=[jax.ShapeDtypeStruct((t, MIX_W), ACT),
                   jax.ShapeDtypeStruct((nchunks, HEADS, ML_DQK, HEAD_W), f32),
                   jax.ShapeDtypeStruct((nchunks, HEADS, 1, ML_DQK), f32),
                   jax.ShapeDtypeStruct((nchunks, HEADS, 1, 128), f32)],
        scratch_shapes=[pltpu.VMEM((HEADS, ML_DQK, HEAD_W), f32), pltpu.VMEM((HEADS, 1, ML_DQK), f32), pltpu.VMEM((HEADS, 1, 128), f32)],
        operands=(qkc, u_ml, u_ml, u_ml, gn), riders=riders, copies=_gather_copies, ride_shapes=_gather_shapes(riders))


def _mlstm_bwd(qkc, u_ml, gn, cst, nst, mst, dog):
    t = qkc.shape[0]
    tb = _tile(t, 256)
    nb = t // tb
    nc_blk = tb // CHUNK

    def body(qk_ref, v_ref, mo_ref, gt_ref, gn_ref, cst_ref, nst_ref, mst_ref, dog_ref,
             dqk_ref, dv_ref, dmo_ref, dgt_ref, dgn_ref, dc_sc, dn_sc):
        @pl.when(pl.program_id(0) == 0)
        def _():
            dc_sc[...] = jnp.zeros_like(dc_sc)
            dn_sc[...] = jnp.zeros_like(dn_sc)
            dgn_ref[...] = jnp.zeros_like(dgn_ref)

        tril_mask = _tri(CHUNK)
        tri = tril_mask.astype(f32)
        triu = _tri(CHUNK, upper=True).astype(f32)
        lane = lax.broadcasted_iota(jnp.int32, (CHUNK, 128), 1)

        def chunk(j, carry):
            c = nc_blk - 1 - j
            r0 = pl.multiple_of(c * CHUNK, CHUNK)
            rows = pl.ds(r0, CHUNK)
            gates = gt_ref[rows, :]
            gc, rowsf = _ml_gate_forms(gates, tri)
            dg_mat = jnp.zeros((CHUNK, 128), f32)
            dig_mat = jnp.zeros((CHUNK, 128), f32)
            dlast_row = jnp.zeros((1, 128), f32)
            hs = range(HEADS)
            cols = [slice(h * HEAD_W, (h + 1) * HEAD_W) for h in hs]
            q = [qk_ref[rows, h * ML_DQK:(h + 1) * ML_DQK] * (ML_DQK ** -0.5) for h in hs]
            k = [qk_ref[rows, HEADS * ML_DQK + h * ML_DQK:HEADS * ML_DQK + (h + 1) * ML_DQK] for h in hs]
            v = [v_ref[rows, h * HEAD_W:(h + 1) * HEAD_W] for h in hs]
            c_st = [cst_ref[c, h] for h in hs]
            n_st = [nst_ref[c, h] for h in hs]
            m_st = [mst_ref[c, h][:, 0:1] for h in hs]
            dc = [dc_sc[h] for h in hs]
            dn = [dn_sc[h] for h in hs]
            r = _ml_chunk(q, k, v, gates, gc, rowsf, c_st, n_st, m_st, tril_mask)
            z, wi, wo, ws, w_old, den = r["z"], r["wi"], r["wo"], r["ws"], r["w_old"], r["den"]
            hh = [r["num"][h] / z[h] for h in hs]
            dh = []
            for h in hs:
                gnh = gn_ref[:, cols[h]]
                oh, rstd, y = _head_rms(hh[h], gnh)
                sg = _sigmoid(mo_ref[rows, h * HEAD_W:(h + 1) * HEAD_W])
                dogh = dog_ref[rows, cols[h]]
                dy = dogh * sg
                dmo_ref[rows, cols[h]] = (dogh * y * (sg * (1.0 - sg))).astype(ACT)
                dgn_ref[:, cols[h]] += jnp.sum(dy * oh, axis=0, keepdims=True)
                doh = dy * gnh
                dh.append(rstd * (doh - oh * jnp.mean(doh * oh, axis=-1, keepdims=True)))
            dnum = [dh[h] / z[h] for h in hs]
            dz = [-jnp.sum(dh[h] * hh[h], axis=-1, keepdims=True) / z[h] for h in hs]
            dden = [jnp.where(jnp.abs(den[h]) > r["floor"][h], dz[h] * jnp.sign(den[h]), 0.0) for h in hs]
            dsw = [(_bdot_nt(dnum[h], v[h]) + dden[h]) * wi[h] for h in hs]
            dq = [_bdot(dsw[h], k[h]) + wo[h] * (_bdot_nt(dnum[h], c_st[h]) + dden[h] * n_st[h]) for h in hs]
            dk_state = [ws[h] * (_bdot_nt(v[h], dc[h]) + dn[h]) for h in hs]
            dk = [_bdot_tn(dsw[h], q[h]) + dk_state[h] for h in hs]
            dv = [_bdot_tn(r["qk"][h], dnum[h]) + ws[h] * _bdot(k[h], dc[h]) for h in hs]
            woq = [wo[h] * q[h] for h in hs]
            new_dc = [w_old[h] * dc[h] + _bdot_tn(woq[h], dnum[h]) for h in hs]
            for h in hs:
                dv_ref[rows, cols[h]] = dv[h].astype(ACT)
                dc_sc[h] = new_dc[h]
                dn_sc[h] = w_old[h] * dn[h] + jnp.sum(woq[h] * dden[h], axis=0, keepdims=True)
                d_last = (jnp.sum(jnp.sum(k[h] * dk_state[h], axis=-1, keepdims=True), axis=0, keepdims=True)
                          + w_old[h] * (jnp.sum(jnp.sum(dc[h] * c_st[h], axis=-1, keepdims=True), axis=0, keepdims=True)
                                        + jnp.sum(dn[h] * n_st[h], axis=-1, keepdims=True)))
                kdk = jnp.sum(k[h] * dk[h], axis=-1, keepdims=True)
                qdq = jnp.sum(q[h] * dq[h], axis=-1, keepdims=True)
                dg_mat = dg_mat + jnp.where(lane == HEADS + h, qdq - kdk, 0.0)
                dlast_row = dlast_row + jnp.where(lane[0:1] == HEADS + h, d_last, 0.0)
                dig_mat = dig_mat + jnp.where(lane == h, kdk, 0.0)
                dqk_ref[rows, h * ML_DQK:(h + 1) * ML_DQK] = dq[h] * (ML_DQK ** -0.5)
                dqk_ref[rows, HEADS * ML_DQK + h * ML_DQK:HEADS * ML_DQK + (h + 1) * ML_DQK] = dk[h]
            dlf = _sel_dot(triu, dg_mat) + dlast_row
            dgt_ref[rows, :] = (dig_mat + dlf * _sigmoid(-gates)).astype(ACT)
            return carry

        lax.fori_loop(0, nc_blk, chunk, 0)

    st4 = lambda a, b: pl.BlockSpec((nc_blk, HEADS, a, b), lambda i: (nb - 1 - i, 0, 0, 0))
    return pl.pallas_call(
        body, name="mlstm_bwd", grid=(nb,),
        in_specs=[_rows_rev(tb, MIX_W, nb), _rows_rev(tb, MIX_W, nb, 1), _rows_rev(tb, MIX_W, nb, 2), _rows_rev(tb, 128, nb, 12),
                  _const((1, MIX_W)), st4(ML_DQK, HEAD_W), st4(1, ML_DQK), st4(1, 128), _rows_rev(tb, MIX_W, nb)],
        out_specs=[_rows_rev(tb, MIX_W, nb), _rows_rev(tb, MIX_W, nb), _rows_rev(tb, MIX_W, nb), _rows_rev(tb, 128, nb), _const((1, MIX_W))],
        out_shape=[jax.ShapeDtypeStruct((t, MIX_W), f32), jax.ShapeDtypeStruct((t, MIX_W), ACT), jax.ShapeDtypeStruct((t, MIX_W), ACT),
                   jax.ShapeDtypeStruct((t, 128), ACT), jax.ShapeDtypeStruct((1, MIX_W), f32)],
        scratch_shapes=[pltpu.VMEM((HEADS, ML_DQK, HEAD_W), f32), pltpu.VMEM((HEADS, 1, ML_DQK), f32)],
        compiler_params=_cparams(1),
    )(qkc, u_ml, u_ml, u_ml, gn, cst, nst, mst, dog)


def _ln_fwd(r, g, b):
    mu = jnp.mean(r, axis=-1, keepdims=True)
    xc = r - mu
    rstd = lax.rsqrt(jnp.mean(xc * xc, axis=-1, keepdims=True) + LN_EPS)
    xh = xc * rstd
    return xh * g + b, xh, rstd


def _ln_bwd(dy, xh, rstd, g):
    dxh = dy * g
    return rstd * (dxh - jnp.mean(dxh, axis=-1, keepdims=True) - xh * jnp.mean(dxh * xh, axis=-1, keepdims=True))


def _outproj_ln1(og_hg, og_ml, x, w_out, g, b):
    t = x.shape[0]
    tm = _tile(t, DENSE_ROWS)

    def body(a_ref, b_ref, x_ref, w_ref, g_ref, bb_ref, x1_ref, xh_ref, rs_ref, x1b_ref):
        mix = _bdot(a_ref[...], w_ref[0:MIX_W, :]) + _bdot(b_ref[...], w_ref[MIX_W:2 * MIX_W, :])
        y, xh, rstd = _ln_fwd(ALPHA * x_ref[...] + mix, g_ref[...], bb_ref[...])
        x1_ref[...] = y
        x1b_ref[...] = y.astype(ACT)
        xh_ref[...] = xh
        rs_ref[...] = rstd

    return pl.pallas_call(
        body, name="outproj_ln1", grid=(t // tm,),
        in_specs=[_rows(tm, MIX_W), _rows(tm, MIX_W), _rows(tm, D_MODEL), _resident((D_MODEL, D_MODEL)), _const((1, D_MODEL)), _const((1, D_MODEL))],
        out_specs=[_rows(tm, D_MODEL), _rows(tm, D_MODEL), _rows(tm, 1), _rows(tm, D_MODEL)],
        out_shape=[jax.ShapeDtypeStruct((t, D_MODEL), f32), jax.ShapeDtypeStruct((t, D_MODEL), f32), jax.ShapeDtypeStruct((t, 1), f32),
                   jax.ShapeDtypeStruct((t, D_MODEL), ACT)],
        compiler_params=_cparams(1, arbitrary=False),
    )(og_hg, og_ml, x, w_out, g, b)


def _ffn_up(x1, wg, wu):
    t = x1.shape[0]
    tm = _tile(t, DENSE_ROWS)

    def body(x_ref, wg_ref, wu_ref, hg_ref, up_ref, a_ref):
        xv = x_ref[...]
        hg = _bdot(xv, wg_ref[...])
        up = _bdot(xv, wu_ref[...])
        hg_ref[...] = hg
        up_ref[...] = up
        a_ref[...] = (hg * _sigmoid(hg) * up).astype(ACT)

    return pl.pallas_call(
        body, name="ffn_up", grid=(t // tm,),
        in_specs=[_rows(tm, D_MODEL), _resident((D_MODEL, D_FF)), _resident((D_MODEL, D_FF))],
        out_specs=[_rows(tm, D_FF), _rows(tm, D_FF), _rows(tm, D_FF)],
        out_shape=[jax.ShapeDtypeStruct((t, D_FF), f32), jax.ShapeDtypeStruct((t, D_FF), f32), jax.ShapeDtypeStruct((t, D_FF), ACT)],
        compiler_params=_cparams(1, arbitrary=False),
    )(x1, wg, wu)


def _ffn_down_ln2(a, x1, wd, g, b):
    t = x1.shape[0]
    tm = _tile(t, DENSE_ROWS)

    def body(a_ref, x_ref, w_ref, g_ref, bb_ref, x2_ref, xh_ref, rs_ref, x2b_ref):
        ffn = _bdot(a_ref[...], w_ref[...])
        y, xh, rstd = _ln_fwd(ALPHA * x_ref[...] + ffn, g_ref[...], bb_ref[...])
        x2_ref[...] = y
        x2b_ref[...] = y.astype(ACT)
        xh_ref[...] = xh
        rs_ref[...] = rstd

    return pl.pallas_call(
        body, name="ffn_down_ln2", grid=(t // tm,),
        in_specs=[_rows(tm, D_FF), _rows(tm, D_MODEL), _resident((D_FF, D_MODEL)), _const((1, D_MODEL)), _const((1, D_MODEL))],
        out_specs=[_rows(tm, D_MODEL), _rows(tm, D_MODEL), _rows(tm, 1), _rows(tm, D_MODEL)],
        out_shape=[jax.ShapeDtypeStruct((t, D_MODEL), f32), jax.ShapeDtypeStruct((t, D_MODEL), f32), jax.ShapeDtypeStruct((t, 1), f32),
                   jax.ShapeDtypeStruct((t, D_MODEL), ACT)],
        compiler_params=_cparams(1, arbitrary=False),
    )(a, x1, wd, g, b)


def _head_loss_bwd(x2, xh2, rs2, p, tgt, w_pg, b_pg, w_pp, g2):
    t = x2.shape[0]
    tm = _tile(t, DENSE_ROWS)

    def body(x_ref, xh_ref, rs_ref, p_ref, t_ref, wg_ref, bg_ref, wp_ref, g_ref,
             dr_ref, de_ref, dz_ref, loss_ref, dbg_ref, dg2_ref, db2_ref):
        @pl.when(pl.program_id(0) == 0)
        def _():
            loss_ref[...] = jnp.zeros_like(loss_ref)
            dbg_ref[...] = jnp.zeros_like(dbg_ref)
            dg2_ref[...] = jnp.zeros_like(dg2_ref)
            db2_ref[...] = jnp.zeros_like(db2_ref)

        x2v = x_ref[...]
        z = _bdot(x2v, wg_ref[...]) + bg_ref[...]
        e = _bdot(p_ref[...], wp_ref[...])
        sg = _sigmoid(z)
        diff = x2v + sg * e - t_ref[...]
        loss_ref[...] += 0.5 * jnp.sum(jnp.mean(diff * diff, axis=-1, keepdims=True), axis=0, keepdims=True)
        dy = diff * (1.0 / D_MODEL)
        de_ref[...] = (dy * sg).astype(ACT)
        dz = dy * e * (sg * (1.0 - sg))
        dz_ref[...] = dz.astype(ACT)
        dbg_ref[...] += jnp.sum(dz, axis=0, keepdims=True)
        dx2 = dy + _bdot_nt(dz, wg_ref[...])
        xh = xh_ref[...]
        dg2_ref[...] += jnp.sum(dx2 * xh, axis=0, keepdims=True)
        db2_ref[...] += jnp.sum(dx2, axis=0, keepdims=True)
        dr_ref[...] = _ln_bwd(dx2, xh, rs_ref[...], g_ref[...])

    row = jax.ShapeDtypeStruct((1, D_MODEL), f32)
    return pl.pallas_call(
        body, name="head_loss_bwd", grid=(t // tm,),
        in_specs=[_rows(tm, D_MODEL), _rows(tm, D_MODEL), _rows(tm, 1), _rows(tm, PLE), _rows(tm, D_MODEL),
                  _resident((D_MODEL, D_MODEL)), _const((1, D_MODEL)), _resident((PLE, D_MODEL)), _const((1, D_MODEL))],
        out_specs=[_rows(tm, D_MODEL), _rows(tm, D_MODEL), _rows(tm, D_MODEL), _const((1, 1)), _const((1, D_MODEL)), _const((1, D_MODEL)), _const((1, D_MODEL))],
        out_shape=[jax.ShapeDtypeStruct((t, D_MODEL), f32), jax.ShapeDtypeStruct((t, D_MODEL), ACT), jax.ShapeDtypeStruct((t, D_MODEL), ACT),
                   jax.ShapeDtypeStruct((1, 1), f32), row, row, row],
        compiler_params=_cparams(1),
    )(x2, xh2, rs2, p, tgt, w_pg, b_pg, w_pp, g2)


def _ffn_bwd(dr2, hg, up, xh1, rs1, wd, wg, wu, g1):
    t = dr2.shape[0]
    tm = _tile(t, DENSE_ROWS // 2)

    def body(dr_ref, hg_ref, up_ref, xh_ref, rs_ref, wd_ref, wg_ref, wu_ref, g_ref,
             dr1_ref, dhg_ref, dup_ref, dg1_ref, db1_ref):
        @pl.when(pl.program_id(0) == 0)
        def _():
            dg1_ref[...] = jnp.zeros_like(dg1_ref)
            db1_ref[...] = jnp.zeros_like(db1_ref)

        dr2v = dr_ref[...]
        da = _bdot_nt(dr2v, wd_ref[...])
        hgv = hg_ref[...]
        sg = _sigmoid(hgv)
        dhg = da * up_ref[...] * (sg * (1.0 + hgv * (1.0 - sg)))
        dup = da * (hgv * sg)
        dhg_ref[...] = dhg.astype(ACT)
        dup_ref[...] = dup.astype(ACT)
        dx1 = ALPHA * dr2v + _bdot_nt(dhg, wg_ref[...]) + _bdot_nt(dup, wu_ref[...])
        xh = xh_ref[...]
        dg1_ref[...] += jnp.sum(dx1 * xh, axis=0, keepdims=True)
        db1_ref[...] += jnp.sum(dx1, axis=0, keepdims=True)
        dr1_ref[...] = _ln_bwd(dx1, xh, rs_ref[...], g_ref[...])

    row = jax.ShapeDtypeStruct((1, D_MODEL), f32)
    return pl.pallas_call(
        body, name="ffn_bwd", grid=(t // tm,),
        in_specs=[_rows(tm, D_MODEL), _rows(tm, D_FF), _rows(tm, D_FF), _rows(tm, D_MODEL), _rows(tm, 1),
                  _resident((D_FF, D_MODEL)), _resident((D_MODEL, D_FF)), _resident((D_MODEL, D_FF)), _const((1, D_MODEL))],
        out_specs=[_rows(tm, D_MODEL), _rows(tm, D_FF), _rows(tm, D_FF), _const((1, D_MODEL)), _const((1, D_MODEL))],
        out_shape=[jax.ShapeDtypeStruct((t, D_MODEL), f32), jax.ShapeDtypeStruct((t, D_FF), ACT), jax.ShapeDtypeStruct((t, D_FF), ACT), row, row],
        compiler_params=_cparams(1),
    )(dr2, hg, up, xh1, rs1, wd, wg, wu, g1)


def _outproj_bwd(dr1, w_out):
    t = dr1.shape[0]
    tm = _tile(t, DENSE_ROWS)

    def body(dr_ref, w_ref, dhg_ref, dml_ref):
        d = _bdot_nt(dr_ref[...], w_ref[...])
        dhg_ref[...] = d[:, 0:MIX_W]
        dml_ref[...] = d[:, MIX_W:2 * MIX_W]

    return pl.pallas_call(
        body, name="outproj_bwd", grid=(t // tm,),
        in_specs=[_rows(tm, D_MODEL), _resident((D_MODEL, D_MODEL))],
        out_specs=[_rows(tm, MIX_W), _rows(tm, MIX_W)],
        out_shape=[jax.ShapeDtypeStruct((t, MIX_W), f32)] * 2,
        compiler_params=_cparams(1, arbitrary=False),
    )(dr1, w_out)


def _inproj_bwd(dr1, du_hg, dqk, dmv, dmo, dgt, w_hg, w_ml):
    t = dr1.shape[0]
    tm = _tile(t, DENSE_ROWS)

    def body(dr_ref, dhg_ref, dqk_ref, dmv_ref, dmo_ref, dgt_ref, whg_ref, wml_ref, gx_ref, dml_ref):
        dml = jnp.concatenate([dqk_ref[...], dmv_ref[...], dmo_ref[...], dgt_ref[...]], axis=-1).astype(ACT)
        dml_ref[...] = dml
        gx_ref[...] = ALPHA * dr_ref[...] + _bdot_nt(dhg_ref[...], whg_ref[...]) + _bdot_nt(dml, wml_ref[...])

    return pl.pallas_call(
        body, name="inproj_bwd", grid=(t // tm,),
        in_specs=[_rows(tm, D_MODEL), _rows(tm, U_HG), _rows(tm, MIX_W), _rows(tm, MIX_W), _rows(tm, MIX_W), _rows(tm, 128),
                  _resident((D_MODEL, U_HG)), _resident((D_MODEL, U_ML))],
        out_specs=[_rows(tm, D_MODEL), _rows(tm, U_ML)],
        out_shape=[jax.ShapeDtypeStruct((t, D_MODEL), f32), jax.ShapeDtypeStruct((t, U_ML), ACT)],
        compiler_params=_cparams(1, arbitrary=False),
    )(dr1, du_hg, dqk, dmv, dmo, dgt, w_hg, w_ml)


def _wgrad(a, b, name, tk=None, tn=None):
    t, kdim = a.shape
    n = b.shape[1]
    tk = tk or kdim
    tn = tn or n
    tt = _tile(t, WGRAD_ROWS)

    def body(a_ref, b_ref, o_ref):
        @pl.when(pl.program_id(2) == 0)
        def _():
            o_ref[...] = jnp.zeros_like(o_ref)

        o_ref[...] += _bdot_tn(a_ref[...], b_ref[...])

    return pl.pallas_call(
        body, name=name, grid=(kdim // tk, n // tn, t // tt),
        in_specs=[pl.BlockSpec((tt, tk), lambda i, j, s: (s, i)), pl.BlockSpec((tt, tn), lambda i, j, s: (s, j))],
        out_specs=pl.BlockSpec((tk, tn), lambda i, j, s: (i, j)),
        out_shape=jax.ShapeDtypeStruct((kdim, n), f32),
        compiler_params=_cparams(3),
    )(a, b)


def _colsum(parts, name):
    t = parts[0].shape[0]
    tt = _tile(t, 512)
    widths = [a.shape[1] for a in parts]

    def body(*refs):
        o_ref = refs[-1]

        @pl.when(pl.program_id(0) == 0)
        def _():
            o_ref[...] = jnp.zeros_like(o_ref)

        off = 0
        for r, w in zip(refs[:-1], widths):
            o_ref[:, off:off + w] += jnp.sum(r[...].astype(f32), axis=0, keepdims=True)
            off += w

    return pl.pallas_call(
        body, name=name, grid=(t // tt,),
        in_specs=[_rows(tt, w) for w in widths],
        out_specs=_const((1, sum(widths))),
        out_shape=jax.ShapeDtypeStruct((1, sum(widths)), f32),
        compiler_params=_cparams(1),
    )(*parts)


_COL_SPLIT = {"w_in", "w_ffn_gate", "w_ffn_up", "ple_w_proj"}
_RIDE_PLAN = (("w_out", "ple_w_gate", "ple_w_proj"), ("w_ffn_down",), ("w_ffn_gate", "w_ffn_up"))


def _from_chip_major(a, col_split):
    if col_split:
        return a.transpose(1, 0, 2).reshape(a.shape[1], 4 * a.shape[2])
    return a.reshape(4 * a.shape[1], a.shape[2])


def _local_step(x, p, tgt, w_in_b, b_in, logits, conv_w, conv_b, hg_gn, ml_gn, w_out_b, ln1_g, ln1_b,
                wg_b, wu_b, wd_b, ln2_g, ln2_b, w_pp_b, w_pg_b, b_pg, early_hook=None, late_shards=None):
    pad_w = U_HG + U_ML - PROJ_W
    w_hg = w_in_b[:, :U_HG]
    w_ml = jnp.pad(w_in_b[:, U_HG:], ((0, 0), (0, pad_w)))
    bb_hg = b_in[:, :U_HG]
    bb_ml = jnp.pad(b_in[:, U_HG:], ((0, 0), (0, pad_w)))

    ride = [[late_shards[k] for k in names] for names in _RIDE_PLAN] if late_shards is not None else [(), (), ()]
    (u_hg, u_ml, xb), got0 = _inproj(x, w_hg, w_ml, bb_hg, bb_ml, ride[0])
    (og_hg, sst), got1 = _hgrn2_fwd(u_hg, logits, hg_gn, ride[1])
    pre, qkc = _conv_fwd(u_ml, conv_w, conv_b)
    (og_ml, cst, nst, mst), got2 = _mlstm_fwd(qkc, u_ml, ml_gn, ride[2])
    if late_shards is not None:
        late = {k: _from_chip_major(g, k in _COL_SPLIT) for names, got in zip(_RIDE_PLAN, (got0, got1, got2)) for k, g in zip(names, got)}
        w_out_b, wg_b, wu_b, wd_b = late["w_out"], late["w_ffn_gate"], late["w_ffn_up"], late["w_ffn_down"]
        w_pp_b, w_pg_b = late["ple_w_proj"], late["ple_w_gate"]
    x1, xh1, rs1, x1b = _outproj_ln1(og_hg, og_ml, x, w_out_b, ln1_g, ln1_b)
    hgp, up, act = _ffn_up(x1b, wg_b, wu_b)
    x2, xh2, rs2, x2b = _ffn_down_ln2(act, x1, wd_b, ln2_g, ln2_b)
    dr2, de, dz, loss, d_bpg, d_ln2g, d_ln2b = _head_loss_bwd(x2, xh2, rs2, p, tgt, w_pg_b, b_pg, w_pp_b, ln2_g)
    dr1, dhg, dup, d_ln1g, d_ln1b = _ffn_bwd(dr2, hgp, up, xh1, rs1, wd_b, wg_b, wu_b, ln1_g)

    d_wo_a = _wgrad(og_hg, dr1, "wgrad_out_hg")
    d_wo_b = _wgrad(og_ml, dr1, "wgrad_out_ml")
    d_w_out = jnp.concatenate([d_wo_a, d_wo_b], axis=0)
    d_wg = _wgrad(x1b, dhg, "wgrad_ffn_gate", tn=D_FF // 2)
    d_wu = _wgrad(x1b, dup, "wgrad_ffn_up", tn=D_FF // 2)
    d_wd = _wgrad(act, dr2, "wgrad_ffn_down", tk=D_FF // 2)
    d_wpp = _wgrad(p, de, "wgrad_ple_proj")
    d_wpg = _wgrad(x2b, dz, "wgrad_ple_gate")
    early = dict(w_out=d_w_out, w_ffn_gate=d_wg, w_ffn_up=d_wu, w_ffn_down=d_wd, ple_w_proj=d_wpp, ple_w_gate=d_wpg)
    riders = early_hook(early) if early_hook is not None else ()

    dog_hg, dog_ml = _outproj_bwd(dr1, w_out_b)
    res = _hgrn2_bwd(u_hg, logits, hg_gn, sst, dog_hg, riders)
    du_hg, d_logits, d_hg_gn = res[:3]
    dqkc, dmv, dmo, dgt, d_ml_gn = _mlstm_bwd(qkc, u_ml, ml_gn, cst, nst, mst, dog_ml)
    dqk, d_conv_w, d_conv_b = _conv_bwd(u_ml, conv_w, pre, dqkc)
    grad_x, du_ml = _inproj_bwd(dr1, du_hg, dqk, dmv, dmo, dgt, w_hg, w_ml)

    dw_hg = _wgrad(xb, du_hg, "wgrad_in_hg", tn=1024)
    dw_ml = _wgrad(xb, du_ml, "wgrad_in_ml")
    d_w_in = jnp.concatenate([dw_hg, dw_ml[:, :PROJ_W - U_HG]], axis=1)
    d_b_in = _colsum([du_hg, du_ml], "colsum_du")[:, :PROJ_W]

    grads = dict(w_in=d_w_in, b_in=d_b_in, hg_lb_logits=d_logits, ml_conv_w=d_conv_w, ml_conv_b=d_conv_b,
                 hg_norm_g=d_hg_gn, ml_norm_g=d_ml_gn, ln1_g=d_ln1g, ln1_b=d_ln1b, ln2_g=d_ln2g, ln2_b=d_ln2b,
                 ple_b_gate=d_bpg, **early)
    return loss, grad_x, grads, list(res[3:])


_ANY = pl.BlockSpec(memory_space=pltpu.HBM)
_MESH = pl.DeviceIdType.MESH


def _my_place():
    return lax.axis_index("x"), lax.axis_index("y"), lax.axis_index("c")


def _other_chips(x, y):
    return [(1 - x, y), (x, 1 - y), (1 - x, 1 - y)]


def _allgather_weights(shards, taps, name):
    n = len(shards)
    halves = [s.shape[0] // 2 for s in shards]

    def body(*refs):
        ins, tap_in = refs[:n], refs[n]
        outs, tap_out = refs[n + 1:2 * n + 1], refs[2 * n + 1]
        send_sems, recv_sems, local_sems = refs[2 * n + 2:]
        x, y, c = _my_place()
        me = 2 * x + y
        sibling = (x, y, 1 - c)
        chips = _other_chips(x, y)

        def ici(a, j, block_chip):
            px, py = chips[j]
            src = ins[a].at[pl.ds(pl.multiple_of(c * halves[a], 16), halves[a])] if block_chip is None else outs[a].at[block_chip, c]
            dst = outs[a].at[me if block_chip is None else block_chip, c]
            return pltpu.make_async_remote_copy(src_ref=src, dst_ref=dst, send_sem=send_sems.at[6 * a + j], recv_sem=recv_sems.at[6 * a + j],
                                                device_id=(px, py, c), device_id_type=_MESH)

        def d2d(a, j, half):
            px, py = chips[j]
            blk = outs[a].at[2 * px + py, half]
            return pltpu.make_async_remote_copy(src_ref=blk, dst_ref=blk, send_sem=send_sems.at[6 * a + 3 + j], recv_sem=recv_sems.at[6 * a + 3 + j],
                                                device_id=sibling, device_id_type=_MESH)

        local = []
        for a in range(n):
            for h in range(2):
                cp = pltpu.make_async_copy(ins[a].at[pl.ds(h * halves[a], halves[a])], outs[a].at[me, h], local_sems.at[2 * a + h])
                cp.start()
                local.append(cp)
            for j in range(3):
                ici(a, j, None).start()
        tap_local = pltpu.make_async_copy(tap_in, tap_out.at[me], local_sems.at[2 * n])
        tap_local.start()
        tap_copies = []
        for j, (px, py) in enumerate(chips):
            cp = pltpu.make_async_remote_copy(src_ref=tap_in, dst_ref=tap_out.at[me], send_sem=send_sems.at[6 * n + j], recv_sem=recv_sems.at[6 * n + j],
                                              device_id=(px, py, c), device_id_type=_MESH)
            cp.start()
            tap_copies.append(cp)
        for a in range(n):
            for j, (px, py) in enumerate(chips):
                ici(a, j, 2 * px + py).wait_recv()
                d2d(a, j, c).start()
        for a in range(n):
            for j in range(3):
                d2d(a, j, 1 - c).wait_recv()
        for a in range(n):
            for j in range(3):
                ici(a, j, None).wait_send()
                d2d(a, j, c).wait_send()
        for j, (px, py) in enumerate(chips):
            pltpu.make_async_remote_copy(src_ref=tap_in, dst_ref=tap_out.at[2 * px + py], send_sem=send_sems.at[6 * n + j], recv_sem=recv_sems.at[6 * n + j],
                                         device_id=(px, py, c), device_id_type=_MESH).wait()
        for cp in local:
            cp.wait()
        tap_local.wait()

    res = pl.pallas_call(
        body, name=name,
        in_specs=[_ANY] * (n + 1), out_specs=[_ANY] * (n + 1),
        out_shape=[jax.ShapeDtypeStruct((4, 2, s.shape[0] // 2, s.shape[1]), s.dtype) for s in shards]
        + [jax.ShapeDtypeStruct((4,) + taps.shape, taps.dtype)],
        scratch_shapes=[pltpu.SemaphoreType.DMA((6 * n + 3,)), pltpu.SemaphoreType.DMA((6 * n + 3,)), pltpu.SemaphoreType.DMA((2 * n + 1,))],
    )(*shards, taps)
    return [w.reshape((4,) + s.shape) for w, s in zip(res[:n], shards)], res[n]


def _swap_halves(pieces, name):
    n = len(pieces)
    halves = [p.shape[1] // 2 for p in pieces]

    def body(*refs):
        ins, own, other = refs[:n], refs[n:2 * n], refs[2 * n:3 * n]
        send_sems, recv_sems, local_sems = refs[3 * n:]
        x, y, c = _my_place()

        def half_of(a, which):
            return ins[a].at[pl.ds(0, 4), pl.ds(pl.multiple_of(which * halves[a], 16), halves[a])]

        def to_sibling(a):
            return pltpu.make_async_remote_copy(src_ref=half_of(a, 1 - c), dst_ref=other[a], send_sem=send_sems.at[a], recv_sem=recv_sems.at[a],
                                                device_id=(x, y, 1 - c), device_id_type=_MESH)

        local = []
        for a in range(n):
            cp = pltpu.make_async_copy(half_of(a, c), own[a], local_sems.at[a])
            cp.start()
            local.append(cp)
            to_sibling(a).start()
        for a in range(n):
            to_sibling(a).wait()
            local[a].wait()

    shapes = [jax.ShapeDtypeStruct((4, p.shape[1] // 2, p.shape[2]), p.dtype) for p in pieces]
    res = pl.pallas_call(
        body, name=name,
        in_specs=[_ANY] * n, out_specs=[_ANY] * (2 * n), out_shape=shapes + shapes,
        scratch_shapes=[pltpu.SemaphoreType.DMA((n,)), pltpu.SemaphoreType.DMA((n,)), pltpu.SemaphoreType.DMA((n,))],
    )(*pieces)
    return res[:n], res[n:]


_VMEM = pl.BlockSpec(memory_space=pltpu.VMEM)
_EX_ROWS = 32


def _pair_reduce(p, name):
    s, r, c = p.shape
    half = r // 2

    def body(p_ref, o_ref, other, send_sem, recv_sem):
        x, y, cc = _my_place()
        theirs = pl.multiple_of((1 - cc) * half, 16)
        mine = pl.multiple_of(cc * half, 16)
        cp = pltpu.make_async_remote_copy(src_ref=p_ref.at[pl.ds(0, s), pl.ds(theirs, half)], dst_ref=other, send_sem=send_sem, recv_sem=recv_sem,
                                          device_id=(x, y, 1 - cc), device_id_type=_MESH)
        cp.start()
        cp.wait()

        def step(i, carry):
            r0 = pl.multiple_of(i * _EX_ROWS, _EX_ROWS)
            for slot in range(s):
                own_rows = pl.ds(pl.multiple_of(mine + r0, 16), _EX_ROWS)
                o_ref[slot, pl.ds(r0, _EX_ROWS), :] = (p_ref[slot, own_rows, :] + other[slot, pl.ds(r0, _EX_ROWS), :]).astype(bf16)
            return carry

        lax.fori_loop(0, half // _EX_ROWS, step, 0)

    return pl.pallas_call(
        body, name=name, in_specs=[_VMEM], out_specs=_VMEM,
        out_shape=jax.ShapeDtypeStruct((s, half, c), bf16),
        scratch_shapes=[pltpu.VMEM((s, half, c), f32), pltpu.SemaphoreType.DMA, pltpu.SemaphoreType.DMA],
        compiler_params=pltpu.CompilerParams(vmem_limit_bytes=VMEM_LIMIT),
    )(p)


def _chip_reduce_swap(rcv, name):
    s, h, c = rcv.shape

    def body(r_ref, g_ref, send_sem, recv_sem):
        x, y, cc = _my_place()

        def step(i, carry):
            r0 = pl.multiple_of(i * _EX_ROWS, _EX_ROWS)
            acc = r_ref[0, pl.ds(r0, _EX_ROWS), :].astype(f32)
            for slot in range(1, s):
                acc = acc + r_ref[slot, pl.ds(r0, _EX_ROWS), :].astype(f32)
            g_ref[cc, pl.ds(r0, _EX_ROWS), :] = acc
            return carry

        lax.fori_loop(0, h // _EX_ROWS, step, 0)
        cp = pltpu.make_async_remote_copy(src_ref=g_ref.at[cc], dst_ref=g_ref.at[cc], send_sem=send_sem, recv_sem=recv_sem,
                                          device_id=(x, y, 1 - cc), device_id_type=_MESH)
        cp.start()
        cp.wait()

    return pl.pallas_call(
        body, name=name, in_specs=[_VMEM], out_specs=_VMEM,
        out_shape=jax.ShapeDtypeStruct((2, h, c), f32),
        scratch_shapes=[pltpu.SemaphoreType.DMA, pltpu.SemaphoreType.DMA],
        compiler_params=pltpu.CompilerParams(vmem_limit_bytes=VMEM_LIMIT),
    )(rcv)


def _add_cast(a, b, name):
    s, r, c = a.shape
    tr = _row_tile(r, c)

    def body(a_ref, b_ref, o_ref):
        o_ref[...] = (a_ref[...] + b_ref[...]).astype(bf16)

    blk = pl.BlockSpec((1, tr, c), lambda i, j: (i, j, 0))
    return pl.pallas_call(
        body, name=name, grid=(s, r // tr), in_specs=[blk, blk], out_specs=blk,
        out_shape=jax.ShapeDtypeStruct(a.shape, bf16),
        compiler_params=_cparams(2, arbitrary=False),
    )(a, b)


def _gather_copies(ins, outs, send_sems, recv_sems, local_sems):
    x, y, c = _my_place()
    me = 2 * x + y
    local, outgoing, incoming = [], [], []
    for a in range(len(ins)):
        local.append(pltpu.make_async_copy(ins[a], outs[a].at[me], local_sems.at[a]))
        for j, (px, py) in enumerate(_other_chips(x, y)):
            sems = dict(send_sem=send_sems.at[3 * a + j], recv_sem=recv_sems.at[3 * a + j], device_id=(px, py, c), device_id_type=_MESH)
            outgoing.append(pltpu.make_async_remote_copy(src_ref=ins[a], dst_ref=outs[a].at[me], **sems))
            incoming.append(pltpu.make_async_remote_copy(src_ref=ins[a], dst_ref=outs[a].at[2 * px + py], **sems))
    return local, outgoing, incoming


def _riding_call(body, name, nsteps, in_specs, out_specs, out_shape, scratch_shapes, operands, riders, copies, ride_shapes):
    nr, n_in, n_out, n_scr = len(riders), len(in_specs), len(out_specs), len(scratch_shapes)

    def wrapped(*refs):
        ins, ride_in = refs[:n_in], refs[n_in:n_in + nr]
        outs, ride_out = refs[n_in + nr:n_in + nr + n_out], refs[n_in + nr + n_out:n_in + 2 * nr + n_out]
        scratch, sems = refs[n_in + 2 * nr + n_out:n_in + 2 * nr + n_out + n_scr], refs[n_in + 2 * nr + n_out + n_scr:]
        if nr:
            @pl.when(pl.program_id(0) == 0)
            def _():
                local, outgoing, _ = copies(ride_in, ride_out, *sems)
                for cp in local + outgoing:
                    cp.start()

        body(*ins, *outs, *scratch)
        if nr:
            @pl.when(pl.program_id(0) == nsteps - 1)
            def _():
                local, outgoing, incoming = copies(ride_in, ride_out, *sems)
                for cp in incoming:
                    cp.wait_recv()
                for cp in outgoing:
                    cp.wait_send()
                for cp in local:
                    cp.wait()

    hbm = pl.BlockSpec(memory_space=pltpu.HBM)
    sems = [pltpu.SemaphoreType.DMA((3 * nr,)), pltpu.SemaphoreType.DMA((3 * nr,)), pltpu.SemaphoreType.DMA((nr,))] if nr else []
    res = pl.pallas_call(
        wrapped, name=name, grid=(nsteps,),
        in_specs=list(in_specs) + [hbm] * nr, out_specs=list(out_specs) + [hbm] * nr,
        out_shape=list(out_shape) + list(ride_shapes),
        scratch_shapes=list(scratch_shapes) + sems,
        compiler_params=_cparams(1),
    )(*operands, *riders)
    return list(res[:n_out]), list(res[n_out:])


def _gather_shapes(riders):
    return [jax.ShapeDtypeStruct((4,) + r.shape, r.dtype) for r in riders]


def _scatter_copies(ins, outs, send_sems, recv_sems, local_sems):
    x, y, c = _my_place()
    me = 2 * x + y
    local, outgoing, incoming = [], [], []
    for a in range(len(ins)):
        local.append(pltpu.make_async_copy(ins[a].at[me], outs[a].at[me], local_sems.at[a]))
        for j, (px, py) in enumerate(_other_chips(x, y)):
            sems = dict(send_sem=send_sems.at[3 * a + j], recv_sem=recv_sems.at[3 * a + j], device_id=(px, py, c), device_id_type=_MESH)
            outgoing.append(pltpu.make_async_remote_copy(src_ref=ins[a].at[2 * px + py], dst_ref=outs[a].at[me], **sems))
            incoming.append(pltpu.make_async_remote_copy(src_ref=ins[a].at[2 * px + py], dst_ref=outs[a].at[2 * px + py], **sems))
    return local, outgoing, incoming


def _scatter_start(ins, outs, send_sems, recv_sems, local_sems):
    local, outgoing, _ = _scatter_copies(ins, outs, send_sems, recv_sems, local_sems)
    for cp in local + outgoing:
        cp.start()


def _scatter_wait(ins, outs, send_sems, recv_sems, local_sems):
    local, outgoing, incoming = _scatter_copies(ins, outs, send_sems, recv_sems, local_sems)
    for cp in incoming:
        cp.wait_recv()
    for cp in outgoing:
        cp.wait_send()
    for cp in local:
        cp.wait()


def _scatter_chips(pieces, name):
    n = len(pieces)

    def body(*refs):
        ins, outs = refs[:n], refs[n:2 * n]
        _scatter_start(ins, outs, *refs[2 * n:])
        _scatter_wait(ins, outs, *refs[2 * n:])

    return pl.pallas_call(
        body, name=name,
        in_specs=[_ANY] * n, out_specs=[_ANY] * n,
        out_shape=[jax.ShapeDtypeStruct(s.shape, s.dtype) for s in pieces],
        scratch_shapes=[pltpu.SemaphoreType.DMA((3 * n,)), pltpu.SemaphoreType.DMA((3 * n,)), pltpu.SemaphoreType.DMA((n,))],
    )(*pieces)


def _swap_cores(blocks, name):
    n = len(blocks)
    parts = 4
    rows = [b.shape[0] // parts for b in blocks]

    def body(*refs):
        ins, outs = refs[:n], refs[n:2 * n]
        send_sems, recv_sems, local_sems = refs[2 * n:]
        x, y, c = _my_place()

        def remote(a, k, slot):
            rs = pl.ds(k * rows[a], rows[a])
            return pltpu.make_async_remote_copy(src_ref=ins[a].at[rs], dst_ref=outs[a].at[slot, rs], send_sem=send_sems.at[parts * a + k],
                                                recv_sem=recv_sems.at[parts * a + k], device_id=(x, y, 1 - c), device_id_type=_MESH)

        local = []
        for a in range(n):
            cp = pltpu.make_async_copy(ins[a], outs[a].at[c], local_sems.at[a])
            cp.start()
            local.append(cp)
            for k in range(parts):
                remote(a, k, c).start()
        for a in range(n):
            for k in range(parts):
                remote(a, k, 1 - c).wait()
            local[a].wait()

    return pl.pallas_call(
        body, name=name,
        in_specs=[_ANY] * n, out_specs=[_ANY] * n,
        out_shape=[jax.ShapeDtypeStruct((2,) + s.shape, s.dtype) for s in blocks],
        scratch_shapes=[pltpu.SemaphoreType.DMA((parts * n,)), pltpu.SemaphoreType.DMA((parts * n,)), pltpu.SemaphoreType.DMA((n,))],
    )(*blocks)


def _gather_all(block, name):
    def body(in_ref, out_ref, send_sems, recv_sems, local_sem):
        x, y, c = _my_place()
        me = 4 * x + 2 * y + c
        cp = pltpu.make_async_copy(in_ref, out_ref.at[me], local_sem)
        cp.start()
        peers = []
        for dx in range(2):
            for dy in range(2):
                for dc in range(2):
                    if dx or dy or dc:
                        peers.append((1 - x if dx else x, 1 - y if dy else y, 1 - c if dc else c))
        for j, pr in enumerate(peers):
            pltpu.make_async_remote_copy(src_ref=in_ref, dst_ref=out_ref.at[me], send_sem=send_sems.at[j], recv_sem=recv_sems.at[j],
                                         device_id=pr, device_id_type=_MESH).start()
        for j, (px, py, pc) in enumerate(peers):
            pltpu.make_async_remote_copy(src_ref=in_ref, dst_ref=out_ref.at[4 * px + 2 * py + pc], send_sem=send_sems.at[j], recv_sem=recv_sems.at[j],
                                         device_id=(px, py, pc), device_id_type=_MESH).wait()
        cp.wait()

    return pl.pallas_call(
        body, name=name,
        in_specs=[_ANY], out_specs=_ANY,
        out_shape=jax.ShapeDtypeStruct((8,) + block.shape, block.dtype),
        scratch_shapes=[pltpu.SemaphoreType.DMA((7,)), pltpu.SemaphoreType.DMA((7,)), pltpu.SemaphoreType.DMA],
    )(block)


def _row_tile(r, c):
    best = r
    for cand in range(16, r + 1, 16):
        if r % cand == 0 and cand * c * 4 <= (1 << 20):
            best = cand
    return best if best * c * 4 <= (4 << 20) else r


def _sum_slots(parts, name):
    n, r, c = parts.shape
    tr = _row_tile(r, c)

    def body(p_ref, o_ref):
        acc = p_ref[0].astype(f32)
        for s in range(1, n):
            acc = acc + p_ref[s].astype(f32)
        o_ref[...] = acc

    return pl.pallas_call(
        body, name=name, grid=(r // tr,),
        in_specs=[pl.BlockSpec((n, tr, c), lambda i: (0, i, 0))],
        out_specs=pl.BlockSpec((tr, c), lambda i: (i, 0)),
        out_shape=jax.ShapeDtypeStruct((r, c), f32),
        compiler_params=_cparams(1, arbitrary=False),
    )(parts)


def _adamw(parts, w, m, v, name):
    n, r, c = parts.shape
    tr = _row_tile(r, c)

    def body(p_ref, w_ref, m_ref, v_ref, g_ref, d_ref, nm_ref, nv_ref):
        g = p_ref[0]
        for s in range(1, n):
            g = g + p_ref[s]
        nm = B1 * m_ref[...] + (1.0 - B1) * g
        nv = B2 * v_ref[...] + (1.0 - B2) * (g * g)
        m_hat = nm / (1.0 - B1 ** STEP)
        v_hat = nv / (1.0 - B2 ** STEP)
        g_ref[...] = g
        nm_ref[...] = nm
        nv_ref[...] = nv
        d_ref[...] = -LR * (m_hat / (jnp.sqrt(v_hat) + EPS_ADAM) + WD * w_ref[...])

    blk = pl.BlockSpec((tr, c), lambda i: (i, 0))
    return pl.pallas_call(
        body, name=name, grid=(r // tr,),
        in_specs=[pl.BlockSpec((n, tr, c), lambda i: (0, i, 0)), blk, blk, blk],
        out_specs=[blk] * 4,
        out_shape=[jax.ShapeDtypeStruct((r, c), f32)] * 4,
        compiler_params=_cparams(1, arbitrary=False),
    )(parts, w, m, v)


_BIG = ["w_in", "w_out", "w_ffn_gate", "w_ffn_up", "w_ffn_down", "ple_w_proj", "ple_w_gate"]
_SMALL = ["b_in", "hg_lb_logits", "ml_conv_w", "ml_conv_b", "hg_norm_g", "ml_norm_g", "ln1_g", "ln1_b", "ln2_g", "ln2_b", "ple_b_gate"]
_ORDER = ["w_in", "b_in", "hg_lb_logits", "ml_conv_w", "ml_conv_b", "hg_norm_g", "ml_norm_g", "w_out", "ln1_g", "ln1_b",
          "w_ffn_gate", "w_ffn_up", "w_ffn_down", "ln2_g", "ln2_b", "ple_w_proj", "ple_w_gate", "ple_b_gate"]
_PACK_ROWS, _PACK_COLS = 16, 1024


def _pack(arrays):
    flat = jnp.concatenate([a.reshape(-1) for a in arrays])
    return jnp.pad(flat, (0, _PACK_ROWS * _PACK_COLS - flat.shape[0])).reshape(_PACK_ROWS, _PACK_COLS)


def _unpack(pack, shapes):
    flat = pack.reshape(-1)
    out, off = [], 0
    for s in shapes:
        size = 1
        for d in s:
            size *= d
        out.append(flat[off:off + size].reshape(s))
        off += size
    return out


def _to_chip_major(g, col_split):
    if col_split:
        k, n = g.shape
        return g.reshape(k, 4, n // 4).transpose(1, 0, 2)
    k, n = g.shape
    return g.reshape(4, k // 4, n)


def kernel(x, p, w_in, b_in, hg_lb_logits, ml_conv_w, ml_conv_b, hg_norm_g, ml_norm_g, w_out, ln1_g, ln1_b, w_ffn_gate, w_ffn_up, w_ffn_down, ln2_g, ln2_b, ple_w_proj, ple_w_gate, ple_b_gate, loss_target, m_w_in, m_b_in, m_hg_lb_logits, m_ml_conv_w, m_ml_conv_b, m_hg_norm_g, m_ml_norm_g, m_w_out, m_ln1_g, m_ln1_b, m_w_ffn_gate, m_w_ffn_up, m_w_ffn_down, m_ln2_g, m_ln2_b, m_ple_w_proj, m_ple_w_gate, m_ple_b_gate, v_w_in, v_b_in, v_hg_lb_logits, v_ml_conv_w, v_ml_conv_b, v_hg_norm_g, v_ml_norm_g, v_w_out, v_ln1_g, v_ln1_b, v_w_ffn_gate, v_w_ffn_up, v_w_ffn_down, v_ln2_g, v_ln2_b, v_ple_w_proj, v_ple_w_gate, v_ple_b_gate):
    args = dict(locals())
    wts = {k: args[k] for k in _ORDER}
    mom = {k: args["m_" + k] for k in _ORDER}
    var = {k: args["v_" + k] for k in _ORDER}
    two_d = lambda a: a.reshape(a.shape[-2], a.shape[-1])

    shards = {k: two_d(wts[k]).astype(bf16) for k in _BIG}
    gathered, taps = _allgather_weights([shards["w_in"]], two_d(ml_conv_w), "allgather_w_in")
    w_in_full = _from_chip_major(gathered[0], True)
    conv_w_full = _from_chip_major(taps, True)

    def core_sum(k, g):
        return _pair_reduce(_to_chip_major(g, k in _COL_SPLIT), "pair_reduce_" + k)

    early_keys = _BIG[1:]
    loss, grad_x, grads, early_received = _local_step(
        x[0], p[0, 0], loss_target[0], w_in_full, b_in, hg_lb_logits, conv_w_full, ml_conv_b, hg_norm_g, ml_norm_g,
        None, ln1_g, ln1_b, None, None, None, ln2_g, ln2_b, None, None, ple_b_gate,
        early_hook=lambda early: [core_sum(k, early[k]) for k in early_keys],
        late_shards={k: shards[k] for k in early_keys})

    received = list(_scatter_chips([core_sum("w_in", grads["w_in"])], "scatter_grad_w_in")) + list(early_received)
    both = [_chip_reduce_swap(r, "chip_reduce_" + k) for k, r in zip(_BIG, received)]
    out_g, out_d, out_m, out_v = {}, {}, {}, {}
    for k, parts in zip(_BIG, both):
        whole = parts.reshape(1, 2 * parts.shape[1], parts.shape[2])
        g, d, nm, nv = _adamw(whole, two_d(wts[k]), two_d(mom[k]), two_d(var[k]), "adamw_" + k)
        shp = wts[k].shape
        out_g[k], out_d[k], out_m[k], out_v[k] = g.reshape(shp), d.reshape(shp), nm.reshape(shp), nv.reshape(shp)

    small_shapes = [(1, PROJ_W), (2, MIX_W), (CONV_K, MIX_W)] + [(1, MIX_W)] * 3 + [(1, D_MODEL)] * 5 + [(1, 1)]
    contrib = _pack([grads[k] for k in _SMALL] + [loss])
    summed = _sum_slots(_gather_all(contrib, "gather_small"), "sum_small")
    small = _unpack(summed, small_shapes)
    loss_total = small[-1].reshape(())
    gsm = dict(zip(_SMALL, small[:-1]))
    place = 2 * lax.axis_index("x") + lax.axis_index("y")
    conv_cols = ml_conv_w.shape[-1]
    gsm["ml_conv_w"] = lax.dynamic_slice(gsm["ml_conv_w"], (0, place * conv_cols), (CONV_K, conv_cols))
    own_shapes = [wts[k].shape for k in _SMALL]
    g_pack = _pack([gsm[k] for k in _SMALL])
    res = _adamw(g_pack[None], _pack([wts[k] for k in _SMALL]), _pack([mom[k] for k in _SMALL]), _pack([var[k] for k in _SMALL]), "adamw_small")
    for dst, pack in zip((out_g, out_d, out_m, out_v), res):
        for k, a in zip(_SMALL, _unpack(pack, own_shapes)):
            dst[k] = a

    outs = [loss_total, grad_x[None]]
    for group in (out_g, out_d, out_m, out_v):
        outs += [group[k] for k in _ORDER]
    return tuple(outs)
```

```python
import functools

import jax
import jax.numpy as jnp
from jax import lax
from jax.experimental import pallas as pl
from jax.experimental.pallas import tpu as pltpu

f32 = jnp.float32
bf16 = jnp.bfloat16
HI = lax.Precision.HIGHEST

D_MODEL = 1024
HEADS = 4
HEAD_W = 128
MIX_W = HEADS * HEAD_W
ML_DQK = 64
PROJ_W = 3592
U_HG = 4 * MIX_W
U_ML = 3 * MIX_W + 128
D_FF = 2816
PLE = 256
CHUNK = 128
SUB = 16
EXP_CAP = 80.0
CONV_K = 4
HALO = 8
ALPHA = float(2.0 ** 0.25)
LN_EPS = 1e-5
RMS_EPS = 1e-6
NEG = -1e30
LR, B1, B2, EPS_ADAM, WD, STEP = 0.001, 0.9, 0.999, 1e-08, 0.01, 10
VMEM_LIMIT = 56 * 1024 * 1024
DENSE_ROWS = 512
WGRAD_ROWS = 2048


def _cparams(n_axes, arbitrary=True):
    sem = ("arbitrary",) * n_axes if arbitrary else ("parallel",) * n_axes
    return pltpu.CompilerParams(dimension_semantics=sem, vmem_limit_bytes=VMEM_LIMIT)


ACT = bf16


def _mx(a):
    return a.astype(ACT)


def _bdot(a, b):
    return jnp.dot(_mx(a), _mx(b), preferred_element_type=f32)


def _bdot_nt(a, b):
    return lax.dot_general(_mx(a), _mx(b), (((1,), (1,)), ((), ())), preferred_element_type=f32)


def _bdot_tn(a, b):
    return lax.dot_general(_mx(a), _mx(b), (((0,), (0,)), ((), ())), preferred_element_type=f32)


def _split3(x):
    hi = x.astype(bf16)
    r1 = x - hi.astype(f32)
    mid = r1.astype(bf16)
    lo = (r1 - mid.astype(f32)).astype(bf16)
    return hi, mid, lo


def _dot3(a, b, dims):
    a_hi = a.astype(bf16)
    a_lo = (a - a_hi.astype(f32)).astype(bf16)
    b_hi = b.astype(bf16)
    b_lo = (b - b_hi.astype(f32)).astype(bf16)
    dn = (dims, ((), ()))
    return (lax.dot_general(a_hi, b_hi, dn, preferred_element_type=f32) + lax.dot_general(a_hi, b_lo, dn, preferred_element_type=f32)
            + lax.dot_general(a_lo, b_hi, dn, preferred_element_type=f32))


def _sel_dot(sel, x):
    sb = sel.astype(bf16)
    return sum(jnp.dot(sb, part, preferred_element_type=f32) for part in _split3(x))


def _sel_dot_nt(sel, x):
    sb = sel.astype(bf16)
    return sum(lax.dot_general(sb, part, (((1,), (1,)), ((), ())), preferred_element_type=f32) for part in _split3(x))


def _sigmoid(x):
    return 1.0 / (1.0 + jnp.exp(-x))


def _log_sigmoid(x):
    return jnp.minimum(x, 0.0) - jnp.log(1.0 + jnp.exp(-jnp.abs(x)))


def _tri(n, upper=False):
    r = lax.broadcasted_iota(jnp.int32, (n, n), 0)
    c = lax.broadcasted_iota(jnp.int32, (n, n), 1)
    return (c >= r) if upper else (c <= r)


def _rows(tm, n, col=0):
    return pl.BlockSpec((tm, n), lambda i, _c=col: (i, _c))


def _rows_rev(tm, n, nb, col=0):
    return pl.BlockSpec((tm, n), lambda i, _c=col, _nb=nb: (_nb - 1 - i, _c))


def _const(shape):
    return pl.BlockSpec(shape, lambda i, _n=len(shape): (0,) * _n)


def _resident(shape):
    return pl.BlockSpec(shape, lambda i, _n=len(shape): (0,) * _n, pipeline_mode=pl.Buffered(1))


def _tile(t, want):
    return want if t % want == 0 else t


def _inproj(x, w_hg, w_ml, b_hg, b_ml, riders=()):
    t = x.shape[0]
    tm = _tile(t, DENSE_ROWS)

    def body(x_ref, whg_ref, wml_ref, bhg_ref, bml_ref, uhg_ref, uml_ref, xb_ref):
        xb = _mx(x_ref[...])
        xb_ref[...] = xb
        uhg_ref[...] = _bdot_nt(xb, whg_ref[...]) + bhg_ref[...]
        uml_ref[...] = _bdot_nt(xb, wml_ref[...]) + bml_ref[...]

    return _riding_call(
        body, "inproj", t // tm,
        in_specs=[_rows(tm, D_MODEL), _resident((U_HG, D_MODEL)), _resident((U_ML, D_MODEL)), _const((1, U_HG)), _const((1, U_ML))],
        out_specs=[_rows(tm, U_HG), _rows(tm, U_ML), _rows(tm, D_MODEL)],
        out_shape=[jax.ShapeDtypeStruct((t, U_HG), f32), jax.ShapeDtypeStruct((t, U_ML), f32), jax.ShapeDtypeStruct((t, D_MODEL), ACT)],
        scratch_shapes=[], operands=(x, w_hg, w_ml, b_hg, b_ml), riders=riders, copies=_gather_copies, ride_shapes=_gather_shapes(riders))


def _hg_gates(hq, hf, lb, tri):
    s = _sigmoid(hf)
    om = 1.0 - lb
    f = lb + om * s
    g = jnp.log(f)
    k = om * (1.0 - s)
    sq = _sigmoid(hq)
    q = hq * sq
    b = _sel_dot(tri, g)
    return q, sq, s, f, k, b


def _hg_scores(q, k, b, tril_mask):
    qts, kts, eqs, eks, rows = [], [], [], [], []
    for i in range(CHUNK // SUB):
        lo = i * SUB
        ref = jnp.zeros_like(b[0:1]) if i == 0 else b[lo - 1:lo]
        eq = jnp.exp(b[lo:lo + SUB] - ref)
        ek = jnp.exp(jnp.minimum(ref - b, EXP_CAP))
        qt = q[lo:lo + SUB] * eq
        kt = k * ek
        rows.append(_bdot_nt(qt, kt))
        qts.append(qt); kts.append(kt); eqs.append(eq); eks.append(ek)
    a = jnp.where(tril_mask, jnp.concatenate(rows, axis=0), 0.0)
    return a, qts, kts, eqs, eks


def _head_rms(o, gn):
    rstd = lax.rsqrt(jnp.mean(o * o, axis=-1, keepdims=True) + RMS_EPS)
    oh = o * rstd
    return oh, rstd, oh * gn


def _lower_bound(logit_ref):
    lg = logit_ref[...]
    return _sigmoid(lg[0:1] - lg[1:2])


def _hgrn2_fwd(u_hg, logits, gn, riders=()):
    t = u_hg.shape[0]
    tb = _tile(t, 256)
    nc_blk = tb // CHUNK

    def body(u_ref, lg_ref, gn_ref, og_ref, sst_ref, st_ref):
        @pl.when(pl.program_id(0) == 0)
        def _():
            st_ref[...] = jnp.zeros_like(st_ref)

        lb_all = _lower_bound(lg_ref)
        tril_mask = _tri(CHUNK)
        tri = tril_mask.astype(f32)

        def chunk(c, carry):
            r0 = pl.multiple_of(c * CHUNK, CHUNK)
            rows = pl.ds(r0, CHUNK)
            heads = range(HEADS)
            cols = [slice(h * HEAD_W, (h + 1) * HEAD_W) for h in heads]
            hv = [u_ref[rows, 2 * MIX_W + h * HEAD_W:2 * MIX_W + (h + 1) * HEAD_W] for h in heads]
            gts = [_hg_gates(u_ref[rows, h * HEAD_W:(h + 1) * HEAD_W], u_ref[rows, MIX_W + h * HEAD_W:MIX_W + (h + 1) * HEAD_W],
                             lb_all[:, cols[h]], tri) for h in heads]
            q = [g[0] for g in gts]
            k = [g[4] for g in gts]
            b = [g[5] for g in gts]
            a = [_hg_scores(q[h], k[h], b[h], tril_mask)[0] for h in heads]
            st = [st_ref[h] for h in heads]
            bl = [b[h][CHUNK - 1:CHUNK] for h in heads]
            o = [_bdot(a[h], hv[h]) + _bdot_nt(q[h] * jnp.exp(b[h]), st[h]) for h in heads]
            new_st = [st[h] * jnp.exp(bl[h]) + _bdot_tn(hv[h], k[h] * jnp.exp(bl[h] - b[h])) for h in heads]
            for h in heads:
                sst_ref[c, h] = st[h]
                st_ref[h] = new_st[h]
                hgate = u_ref[rows, 3 * MIX_W + h * HEAD_W:3 * MIX_W + (h + 1) * HEAD_W]
                _, _, y = _head_rms(o[h], gn_ref[:, cols[h]])
                og_ref[rows, cols[h]] = (y * (hgate * _sigmoid(hgate))).astype(ACT)
            return carry

        lax.fori_loop(0, nc_blk, chunk, 0)

    return _riding_call(
        body, "hgrn2_fwd", t // tb,
        in_specs=[_rows(tb, U_HG), _const((2, MIX_W)), _const((1, MIX_W))],
        out_specs=[_rows(tb, MIX_W), pl.BlockSpec((nc_blk, HEADS, HEAD_W, HEAD_W), lambda i: (i, 0, 0, 0))],
        out_shape=[jax.ShapeDtypeStruct((t, MIX_W), ACT), jax.ShapeDtypeStruct((t // CHUNK, HEADS, HEAD_W, HEAD_W), f32)],
        scratch_shapes=[pltpu.VMEM((HEADS, HEAD_W, HEAD_W), f32)],
        operands=(u_hg, logits, gn), riders=riders, copies=_gather_copies, ride_shapes=_gather_shapes(riders))


def _hgrn2_bwd(u_hg, logits, gn, sst, dog, riders=()):
    t = u_hg.shape[0]
    tb = _tile(t, 256)
    nb = t // tb
    nc_blk = tb // CHUNK
    nr = len(riders)

    def body(*refs):
        u_ref, lg_ref, gn_ref, sst_ref, dog_ref = refs[:5]
        ride_in = refs[5:5 + nr]
        du_ref, dlg_ref, dgn_ref = refs[5 + nr:8 + nr]
        ride_out = refs[8 + nr:8 + 2 * nr]
        dst_ref = refs[8 + 2 * nr]
        ride_sems = refs[9 + 2 * nr:]

        @pl.when(pl.program_id(0) == 0)
        def _():
            dst_ref[...] = jnp.zeros_like(dst_ref)
            dlg_ref[...] = jnp.zeros_like(dlg_ref)
            dgn_ref[...] = jnp.zeros_like(dgn_ref)
            if nr:
                _scatter_start(ride_in, ride_out, *ride_sems)

        lb_all = _lower_bound(lg_ref)
        tril_mask = _tri(CHUNK)
        tri = tril_mask.astype(f32)
        triu = _tri(CHUNK, upper=True).astype(f32)

        def chunk(j, carry):
            c = nc_blk - 1 - j
            r0 = pl.multiple_of(c * CHUNK, CHUNK)
            rows = pl.ds(r0, CHUNK)
            heads = range(HEADS)
            nsub = CHUNK // SUB
            cols = [slice(h * HEAD_W, (h + 1) * HEAD_W) for h in heads]
            hq = [u_ref[rows, h * HEAD_W:(h + 1) * HEAD_W] for h in heads]
            hf = [u_ref[rows, MIX_W + h * HEAD_W:MIX_W + (h + 1) * HEAD_W] for h in heads]
            hv = [u_ref[rows, 2 * MIX_W + h * HEAD_W:2 * MIX_W + (h + 1) * HEAD_W] for h in heads]
            lb = [lb_all[:, cols[h]] for h in heads]
            gts = [_hg_gates(hq[h], hf[h], lb[h], tri) for h in heads]
            q, sq, s, f, k, b = ([g[n] for g in gts] for n in range(6))
            scs = [_hg_scores(q[h], k[h], b[h], tril_mask) for h in heads]
            a, qts, kts, eqs, eks = ([sc[n] for sc in scs] for n in range(5))
            st = [sst_ref[c, h] for h in heads]
            dst = [dst_ref[h] for h in heads]
            bl = [b[h][CHUNK - 1:CHUNK] for h in heads]
            eb = [jnp.exp(b[h]) for h in heads]
            qh = [q[h] * eb[h] for h in heads]
            ekl = [jnp.exp(bl[h] - b[h]) for h in heads]
            kh = [k[h] * ekl[h] for h in heads]
            o = [_bdot(a[h], hv[h]) + _bdot_nt(qh[h], st[h]) for h in heads]
            do = []
            for h in heads:
                hgate = u_ref[rows, 3 * MIX_W + h * HEAD_W:3 * MIX_W + (h + 1) * HEAD_W]
                gnh = gn_ref[:, cols[h]]
                oh, rstd, y = _head_rms(o[h], gnh)
                sg = _sigmoid(hgate)
                dogh = dog_ref[rows, cols[h]]
                dy = dogh * (hgate * sg)
                du_ref[rows, 3 * MIX_W + h * HEAD_W:3 * MIX_W + (h + 1) * HEAD_W] = (dogh * y * (sg * (1.0 + hgate * (1.0 - sg)))).astype(ACT)
                dgn_ref[:, cols[h]] += jnp.sum(dy * oh, axis=0, keepdims=True)
                doh = dy * gnh
                do.append(rstd * (doh - oh * jnp.mean(doh * oh, axis=-1, keepdims=True)))
            da = [jnp.where(tril_mask, _bdot_nt(do[h], hv[h]), 0.0) for h in heads]
            dv = [_bdot_tn(a[h], do[h]) + _bdot_nt(kh[h], dst[h]) for h in heads]
            dq = [_bdot(do[h], st[h]) * eb[h] for h in heads]
            dk = [_bdot(hv[h], dst[h]) * ekl[h] for h in heads]
            d_last = [jnp.sum(k[h] * dk[h], axis=0, keepdims=True) + jnp.exp(bl[h]) * jnp.sum(dst[h] * st[h], axis=0, keepdims=True)
                      for h in heads]
            dqs = [[] for _ in heads]
            for i in range(nsub):
                for h in heads:
                    da_i = da[h][i * SUB:(i + 1) * SUB]
                    dqs[h].append(_dot3(da_i, kts[h][i], ((1,), (0,))) * eqs[h][i])
                    dk[h] = dk[h] + _dot3(da_i, qts[h][i], ((0,), (0,))) * eks[h][i]
            for h in heads:
                dq[h] = dq[h] + jnp.concatenate(dqs[h], axis=0)
                dst_ref[h] = dst[h] * jnp.exp(bl[h]) + _bdot_tn(do[h], qh[h])
            dg = [_sel_dot(triu, q[h] * dq[h] - k[h] * dk[h]) + d_last[h] for h in heads]
            for h in heads:
                dfk = dg[h] / f[h] - dk[h]
                du_ref[rows, h * HEAD_W:(h + 1) * HEAD_W] = (dq[h] * (sq[h] * (1.0 + hq[h] * (1.0 - sq[h])))).astype(ACT)
                du_ref[rows, MIX_W + h * HEAD_W:MIX_W + (h + 1) * HEAD_W] = ((1.0 - lb[h]) * dfk * s[h] * (1.0 - s[h])).astype(ACT)
                du_ref[rows, 2 * MIX_W + h * HEAD_W:2 * MIX_W + (h + 1) * HEAD_W] = dv[h].astype(ACT)
                dlb = jnp.sum((1.0 - s[h]) * dfk, axis=0, keepdims=True) * (lb[h] * (1.0 - lb[h]))
                dlg_ref[0:1, cols[h]] += dlb
                dlg_ref[1:2, cols[h]] -= dlb
            return carry

        lax.fori_loop(0, nc_blk, chunk, 0)

        if nr:
            @pl.when(pl.program_id(0) == nb - 1)
            def _():
                _scatter_wait(ride_in, ride_out, *ride_sems)

    hbm = pl.BlockSpec(memory_space=pltpu.HBM)
    ride_scratch = [pltpu.SemaphoreType.DMA((3 * nr,)), pltpu.SemaphoreType.DMA((3 * nr,)), pltpu.SemaphoreType.DMA((nr,))] if nr else []
    return pl.pallas_call(
        body, name="hgrn2_bwd", grid=(nb,),
        in_specs=[_rows_rev(tb, U_HG, nb), _const((2, MIX_W)), _const((1, MIX_W)),
                  pl.BlockSpec((nc_blk, HEADS, HEAD_W, HEAD_W), lambda i: (nb - 1 - i, 0, 0, 0)), _rows_rev(tb, MIX_W, nb)] + [hbm] * nr,
        out_specs=[_rows_rev(tb, U_HG, nb), _const((2, MIX_W)), _const((1, MIX_W))] + [hbm] * nr,
        out_shape=[jax.ShapeDtypeStruct((t, U_HG), ACT), jax.ShapeDtypeStruct((2, MIX_W), f32), jax.ShapeDtypeStruct((1, MIX_W), f32)]
        + [jax.ShapeDtypeStruct(r.shape, r.dtype) for r in riders],
        scratch_shapes=[pltpu.VMEM((HEADS, HEAD_W, HEAD_W), f32)] + ride_scratch,
        compiler_params=_cparams(1),
    )(u_hg, logits, gn, sst, dog, *riders)


def _conv_fwd(u_ml, w, b):
    t = u_ml.shape[0]
    tm = _tile(t, 512)

    def body(x_ref, w_ref, b_ref, pre_ref, act_ref, xbuf):
        @pl.when(pl.program_id(0) == 0)
        def _():
            xbuf[...] = jnp.zeros_like(xbuf)

        xbuf[0:HALO, :] = xbuf[tm:tm + HALO, :]
        xbuf[HALO:HALO + tm, :] = x_ref[...]
        pre = b_ref[...] + jnp.zeros((tm, MIX_W), f32)
        for kk in range(CONV_K):
            off = HALO - (CONV_K - 1) + kk
            pre = pre + w_ref[kk:kk + 1, :] * xbuf[off:off + tm, :]
        pre_ref[...] = pre
        act_ref[...] = pre * _sigmoid(pre)

    return pl.pallas_call(
        body, name="conv_fwd", grid=(t // tm,),
        in_specs=[_rows(tm, MIX_W), _const((CONV_K, MIX_W)), _const((1, MIX_W))],
        out_specs=[_rows(tm, MIX_W), _rows(tm, MIX_W)],
        out_shape=[jax.ShapeDtypeStruct((t, MIX_W), f32)] * 2,
        scratch_shapes=[pltpu.VMEM((tm + HALO, MIX_W), f32)],
        compiler_params=_cparams(1),
    )(u_ml, w, b)


def _conv_bwd(u_ml, w, pre, dact):
    t = u_ml.shape[0]
    tm = _tile(t, 512)
    nb = t // tm
    hb = tm // HALO

    def body(x_ref, halo_ref, w_ref, pre_ref, dact_ref, dx_ref, dw_ref, db_ref, dbuf, xbuf):
        i = pl.program_id(0)

        @pl.when(i == 0)
        def _():
            dbuf[...] = jnp.zeros_like(dbuf)
            dw_ref[...] = jnp.zeros_like(dw_ref)
            db_ref[...] = jnp.zeros_like(db_ref)

        p = pre_ref[...]
        sg = _sigmoid(p)
        dpre = dact_ref[...] * (sg * (1.0 + p * (1.0 - sg)))
        dbuf[tm:tm + HALO, :] = dbuf[0:HALO, :]
        dbuf[0:tm, :] = dpre
        has_prev = (i < nb - 1).astype(f32)
        xbuf[0:HALO, :] = halo_ref[...] * has_prev
        xbuf[HALO:HALO + tm, :] = x_ref[...]
        dx = jnp.zeros((tm, MIX_W), f32)
        for kk in range(CONV_K):
            back = CONV_K - 1 - kk
            dx = dx + w_ref[kk:kk + 1, :] * dbuf[back:back + tm, :]
            off = HALO - (CONV_K - 1) + kk
            dw_ref[kk:kk + 1, :] += jnp.sum(dpre * xbuf[off:off + tm, :], axis=0, keepdims=True)
        dx_ref[...] = dx.astype(ACT)
        db_ref[...] += jnp.sum(dpre, axis=0, keepdims=True)

    return pl.pallas_call(
        body, name="conv_bwd", grid=(nb,),
        in_specs=[_rows_rev(tm, MIX_W, nb),
                  pl.BlockSpec((HALO, MIX_W), lambda i: (jnp.maximum((nb - 1 - i) * hb - 1, 0), 0)),
                  _const((CONV_K, MIX_W)), _rows_rev(tm, MIX_W, nb), _rows_rev(tm, MIX_W, nb)],
        out_specs=[_rows_rev(tm, MIX_W, nb), _const((CONV_K, MIX_W)), _const((1, MIX_W))],
        out_shape=[jax.ShapeDtypeStruct((t, MIX_W), ACT), jax.ShapeDtypeStruct((CONV_K, MIX_W), f32), jax.ShapeDtypeStruct((1, MIX_W), f32)],
        scratch_shapes=[pltpu.VMEM((tm + HALO, MIX_W), f32), pltpu.VMEM((tm + HALO, MIX_W), f32)],
        compiler_params=_cparams(1),
    )(u_ml, u_ml, w, pre, dact)


def _lane_pick(x, lane):
    idx = lax.broadcasted_iota(jnp.int32, x.shape, 1)
    return jnp.sum(jnp.where(idx == lane, x, 0.0), axis=-1, keepdims=True)


def _ml_gate_forms(gates, tri):
    lf = _log_sigmoid(gates)
    gc = _sel_dot(tri, lf)
    lane = lax.broadcasted_iota(jnp.int32, gates.shape, 1)
    mixed = jnp.where(lane < HEADS, gates, gc)
    sel = (lax.broadcasted_iota(jnp.int32, (8, 128), 0) == lax.broadcasted_iota(jnp.int32, (8, 128), 1)).astype(f32)
    rowsf = _sel_dot_nt(sel, mixed)
    return gc, rowsf


def _ml_chunk(q, k, v, gates, gc, rowsf, c_st, n_st, m_st, tril_mask):
    hs = range(HEADS)
    g_col = [_lane_pick(gc, HEADS + h) for h in hs]
    ig_col = [_lane_pick(gates, h) for h in hs]
    dmat = [jnp.where(tril_mask, g_col[h] - rowsf[HEADS + h:HEADS + h + 1, :] + rowsf[h:h + 1, :], NEG) for h in hs]
    m_inter = [g_col[h] + m_st[h] for h in hs]
    m_t = [jnp.maximum(m_inter[h], jnp.max(dmat[h], axis=-1, keepdims=True)) for h in hs]
    wi = [jnp.exp(dmat[h] - m_t[h]) for h in hs]
    wo = [jnp.exp(m_inter[h] - m_t[h]) for h in hs]
    qk = [_bdot_nt(q[h], k[h]) * wi[h] for h in hs]
    num = [_bdot(qk[h], v[h]) + wo[h] * _bdot(q[h], c_st[h]) for h in hs]
    den = [jnp.sum(qk[h], axis=-1, keepdims=True) + wo[h] * jnp.sum(q[h] * n_st[h], axis=-1, keepdims=True) for h in hs]
    floor = [jnp.exp(-m_t[h]) for h in hs]
    z = [jnp.maximum(jnp.abs(den[h]), floor[h]) for h in hs]
    g_last = [g_col[h][CHUNK - 1:CHUNK] for h in hs]
    a_col = [g_last[h] - g_col[h] + ig_col[h] for h in hs]
    m_new = [jnp.maximum(g_last[h] + m_st[h], jnp.max(a_col[h], axis=0, keepdims=True)) for h in hs]
    ws = [jnp.exp(a_col[h] - m_new[h]) for h in hs]
    w_old = [jnp.exp(g_last[h] + m_st[h] - m_new[h]) for h in hs]
    return dict(wi=wi, wo=wo, qk=qk, num=num, den=den, z=z, floor=floor, ws=ws, w_old=w_old, m_new=m_new)


def _mlstm_fwd(qkc, u_ml, gn, riders=()):
    t = qkc.shape[0]
    tb = _tile(t, 256)
    nc_blk = tb // CHUNK

    def body(qk_ref, v_ref, mo_ref, gt_ref, gn_ref, og_ref, cst_ref, nst_ref, mst_ref, c_sc, n_sc, m_sc):
        @pl.when(pl.program_id(0) == 0)
        def _():
            c_sc[...] = jnp.zeros_like(c_sc)
            n_sc[...] = jnp.zeros_like(n_sc)
            m_sc[...] = jnp.zeros_like(m_sc)

        tril_mask = _tri(CHUNK)
        tri = tril_mask.astype(f32)

        def chunk(c, carry):
            r0 = pl.multiple_of(c * CHUNK, CHUNK)
            rows = pl.ds(r0, CHUNK)
            gates = gt_ref[rows, :]
            gc, rowsf = _ml_gate_forms(gates, tri)
            hs = range(HEADS)
            q = [qk_ref[rows, h * ML_DQK:(h + 1) * ML_DQK] * (ML_DQK ** -0.5) for h in hs]
            k = [qk_ref[rows, HEADS * ML_DQK + h * ML_DQK:HEADS * ML_DQK + (h + 1) * ML_DQK] for h in hs]
            v = [v_ref[rows, h * HEAD_W:(h + 1) * HEAD_W] for h in hs]
            c_st = [c_sc[h] for h in hs]
            n_st = [n_sc[h] for h in hs]
            m_full = [m_sc[h] for h in hs]
            r = _ml_chunk(q, k, v, gates, gc, rowsf, c_st, n_st, [m[:, 0:1] for m in m_full], tril_mask)
            ksc = [k[h] * r["ws"][h] for h in hs]
            new_c = [r["w_old"][h] * c_st[h] + _bdot_tn(ksc[h], v[h]) for h in hs]
            for h in hs:
                cs = slice(h * HEAD_W, (h + 1) * HEAD_W)
                cst_ref[c, h] = c_st[h]
                nst_ref[c, h] = n_st[h]
                mst_ref[c, h] = m_full[h]
                c_sc[h] = new_c[h]
                n_sc[h] = r["w_old"][h] * n_st[h] + jnp.sum(ksc[h], axis=0, keepdims=True)
                m_sc[h] = r["m_new"][h] + jnp.zeros((1, 128), f32)
                _, _, y = _head_rms(r["num"][h] / r["z"][h], gn_ref[:, cs])
                og_ref[rows, cs] = (y * _sigmoid(mo_ref[rows, h * HEAD_W:(h + 1) * HEAD_W])).astype(ACT)
            return carry

        lax.fori_loop(0, nc_blk, chunk, 0)

    nchunks = t // CHUNK
    return _riding_call(
        body, "mlstm_fwd", t // tb,
        in_specs=[_rows(tb, MIX_W), _rows(tb, MIX_W, 1), _rows(tb, MIX_W, 2), _rows(tb, 128, 12), _const((1, MIX_W))],
        out_specs=[_rows(tb, MIX_W),
                   pl.BlockSpec((nc_blk, HEADS, ML_DQK, HEAD_W), lambda i: (i, 0, 0, 0)),
                   pl.BlockSpec((nc_blk, HEADS, 1, ML_DQK), lambda i: (i, 0, 0, 0)),
                   pl.BlockSpec((nc_blk, HEADS, 1, 128), lambda i: (i, 0, 0, 0))],
        out_shape=[jax.ShapeDtypeStruct((t, MIX_W), ACT),
                   jax.ShapeDtypeStruct((nchunks, HEADS, ML_DQK, HEAD_W), f32),
                   jax.ShapeDtypeStruct((nchunks, HEADS, 1, ML_DQK), f32),
                   jax.ShapeDtypeStruct((nchunks, HEADS, 1, 128), f32)],
        scratch_shapes=[pltpu.VMEM((HEADS, ML_DQK, HEAD_W), f32), pltpu.VMEM((HEADS, 1, ML_DQK), f32), pltpu.VMEM((HEADS, 1, 128), f32)],
        operands=(qkc, u_ml, u_ml, u_ml, gn), riders=riders, copies=_gather_copies, ride_shapes=_gather_shapes(riders))


def _mlstm_bwd(qkc, u_ml, gn, cst, nst, mst, dog):
    t = qkc.shape[0]
    tb = _tile(t, 256)
    nb = t // tb
    nc_blk = tb // CHUNK

    def body(qk_ref, v_ref, mo_ref, gt_ref, gn_ref, cst_ref, nst_ref, mst_ref, dog_ref,
             dqk_ref, dv_ref, dmo_ref, dgt_ref, dgn_ref, dc_sc, dn_sc):
        @pl.when(pl.program_id(0) == 0)
        def _():
            dc_sc[...] = jnp.zeros_like(dc_sc)
            dn_sc[...] = jnp.zeros_like(dn_sc)
            dgn_ref[...] = jnp.zeros_like(dgn_ref)

        tril_mask = _tri(CHUNK)
        tri = tril_mask.astype(f32)
        triu = _tri(CHUNK, upper=True).astype(f32)
        lane = lax.broadcasted_iota(jnp.int32, (CHUNK, 128), 1)

        def chunk(j, carry):
            c = nc_blk - 1 - j
            r0 = pl.multiple_of(c * CHUNK, CHUNK)
            rows = pl.ds(r0, CHUNK)
            gates = gt_ref[rows, :]
            gc, rowsf = _ml_gate_forms(gates, tri)
            dg_mat = jnp.zeros((CHUNK, 128), f32)
            dig_mat = jnp.zeros((CHUNK, 128), f32)
            dlast_row = jnp.zeros((1, 128), f32)
            hs = range(HEADS)
            cols = [slice(h * HEAD_W, (h + 1) * HEAD_W) for h in hs]
            q = [qk_ref[rows, h * ML_DQK:(h + 1) * ML_DQK] * (ML_DQK ** -0.5) for h in hs]
            k = [qk_ref[rows, HEADS * ML_DQK + h * ML_DQK:HEADS * ML_DQK + (h + 1) * ML_DQK] for h in hs]
            v = [v_ref[rows, h * HEAD_W:(h + 1) * HEAD_W] for h in hs]
            c_st = [cst_ref[c, h] for h in hs]
            n_st = [nst_ref[c, h] for h in hs]
            m_st = [mst_ref[c, h][:, 0:1] for h in hs]
            dc = [dc_sc[h] for h in hs]
            dn = [dn_sc[h] for h in hs]
            r = _ml_chunk(q, k, v, gates, gc, rowsf, c_st, n_st, m_st, tril_mask)
            z, wi, wo, ws, w_old, den = r["z"], r["wi"], r["wo"], r["ws"], r["w_old"], r["den"]
            hh = [r["num"][h] / z[h] for h in hs]
            dh = []
            for h in hs:
                gnh = gn_ref[:, cols[h]]
                oh, rstd, y = _head_rms(hh[h], gnh)
                sg = _sigmoid(mo_ref[rows, h * HEAD_W:(h + 1) * HEAD_W])
                dogh = dog_ref[rows, cols[h]]
                dy = dogh * sg
                dmo_ref[rows, cols[h]] = (dogh * y * (sg * (1.0 - sg))).astype(ACT)
                dgn_ref[:, cols[h]] += jnp.sum(dy * oh, axis=0, keepdims=True)
                doh = dy * gnh
                dh.append(rstd * (doh - oh * jnp.mean(doh * oh, axis=-1, keepdims=True)))
            dnum = [dh[h] / z[h] for h in hs]
            dz = [-jnp.sum(dh[h] * hh[h], axis=-1, keepdims=True) / z[h] for h in hs]
            dden = [jnp.where(jnp.abs(den[h]) > r["floor"][h], dz[h] * jnp.sign(den[h]), 0.0) for h in hs]
            dsw = [(_bdot_nt(dnum[h], v[h]) + dden[h]) * wi[h] for h in hs]
            dq = [_bdot(dsw[h], k[h]) + wo[h] * (_bdot_nt(dnum[h], c_st[h]) + dden[h] * n_st[h]) for h in hs]
            dk_state = [ws[h] * (_bdot_nt(v[h], dc[h]) + dn[h]) for h in hs]
            dk = [_bdot_tn(dsw[h], q[h]) + dk_state[h] for h in hs]
            dv = [_bdot_tn(r["qk"][h], dnum[h]) + ws[h] * _bdot(k[h], dc[h]) for h in hs]
            woq = [wo[h] * q[h] for h in hs]
            new_dc = [w_old[h] * dc[h] + _bdot_tn(woq[h], dnum[h]) for h in hs]
            for h in hs:
                dv_ref[rows, cols[h]] = dv[h].astype(ACT)
                dc_sc[h] = new_dc[h]
                dn_sc[h] = w_old[h] * dn[h] + jnp.sum(woq[h] * dden[h], axis=0, keepdims=True)
                d_last = (jnp.sum(jnp.sum(k[h] * dk_state[h], axis=-1, keepdims=True), axis=0, keepdims=True)
                          + w_old[h] * (jnp.sum(jnp.sum(dc[h] * c_st[h], axis=-1, keepdims=True), axis=0, keepdims=True)
                                        + jnp.sum(dn[h] * n_st[h], axis=-1, keepdims=True)))
                kdk = jnp.sum(k[h] * dk[h], axis=-1, keepdims=True)
                qdq = jnp.sum(q[h] * dq[h], axis=-1, keepdims=True)
                dg_mat = dg_mat + jnp.where(lane == HEADS + h, qdq - kdk, 0.0)
                dlast_row = dlast_row + jnp.where(lane[0:1] == HEADS + h, d_last, 0.0)
                dig_mat = dig_mat + jnp.where(lane == h, kdk, 0.0)
                dqk_ref[rows, h * ML_DQK:(h + 1) * ML_DQK] = dq[h] * (ML_DQK ** -0.5)
                dqk_ref[rows, HEADS * ML_DQK + h * ML_DQK:HEADS * ML_DQK + (h + 1) * ML_DQK] = dk[h]
            dlf = _sel_dot(triu, dg_mat) + dlast_row
            dgt_ref[rows, :] = (dig_mat + dlf * _sigmoid(-gates)).astype(ACT)
            return carry

        lax.fori_loop(0, nc_blk, chunk, 0)

    st4 = lambda a, b: pl.BlockSpec((nc_blk, HEADS, a, b), lambda i: (nb - 1 - i, 0, 0, 0))
    return pl.pallas_call(
        body, name="mlstm_bwd", grid=(nb,),
        in_specs=[_rows_rev(tb, MIX_W, nb), _rows_rev(tb, MIX_W, nb, 1), _rows_rev(tb, MIX_W, nb, 2), _rows_rev(tb, 128, nb, 12),
                  _const((1, MIX_W)), st4(ML_DQK, HEAD_W), st4(1, ML_DQK), st4(1, 128), _rows_rev(tb, MIX_W, nb)],
        out_specs=[_rows_rev(tb, MIX_W, nb), _rows_rev(tb, MIX_W, nb), _rows_rev(tb, MIX_W, nb), _rows_rev(tb, 128, nb), _const((1, MIX_W))],
        out_shape=[jax.ShapeDtypeStruct((t, MIX_W), f32), jax.ShapeDtypeStruct((t, MIX_W), ACT), jax.ShapeDtypeStruct((t, MIX_W), ACT),
                   jax.ShapeDtypeStruct((t, 128), ACT), jax.ShapeDtypeStruct((1, MIX_W), f32)],
        scratch_shapes=[pltpu.VMEM((HEADS, ML_DQK, HEAD_W), f32), pltpu.VMEM((HEADS, 1, ML_DQK), f32)],
        compiler_params=_cparams(1),
    )(qkc, u_ml, u_ml, u_ml, gn, cst, nst, mst, dog)


def _ln_fwd(r, g, b):
    mu = jnp.mean(r, axis=-1, keepdims=True)
    xc = r - mu
    rstd = lax.rsqrt(jnp.mean(xc * xc, axis=-1, keepdims=True) + LN_EPS)
    xh = xc * rstd
    return xh * g + b, xh, rstd


def _ln_bwd(dy, xh, rstd, g):
    dxh = dy * g
    return rstd * (dxh - jnp.mean(dxh, axis=-1, keepdims=True) - xh * jnp.mean(dxh * xh, axis=-1, keepdims=True))


def _outproj_ln1(og_hg, og_ml, x, w_out, g, b):
    t = x.shape[0]
    tm = _tile(t, DENSE_ROWS)

    def body(a_ref, b_ref, x_ref, w_ref, g_ref, bb_ref, x1_ref, xh_ref, rs_ref, x1b_ref):
        mix = _bdot(a_ref[...], w_ref[0:MIX_W, :]) + _bdot(b_ref[...], w_ref[MIX_W:2 * MIX_W, :])
        y, xh, rstd = _ln_fwd(ALPHA * x_ref[...] + mix, g_ref[...], bb_ref[...])
        x1_ref[...] = y
        x1b_ref[...] = y.astype(ACT)
        xh_ref[...] = xh
        rs_ref[...] = rstd

    return pl.pallas_call(
        body, name="outproj_ln1", grid=(t // tm,),
        in_specs=[_rows(tm, MIX_W), _rows(tm, MIX_W), _rows(tm, D_MODEL), _resident((D_MODEL, D_MODEL)), _const((1, D_MODEL)), _const((1, D_MODEL))],
        out_specs=[_rows(tm, D_MODEL), _rows(tm, D_MODEL), _rows(tm, 1), _rows(tm, D_MODEL)],
        out_shape=[jax.ShapeDtypeStruct((t, D_MODEL), f32), jax.ShapeDtypeStruct((t, D_MODEL), f32), jax.ShapeDtypeStruct((t, 1), f32),
                   jax.ShapeDtypeStruct((t, D_MODEL), ACT)],
        compiler_params=_cparams(1, arbitrary=False),
    )(og_hg, og_ml, x, w_out, g, b)


def _ffn_up(x1, wg, wu):
    t = x1.shape[0]
    tm = _tile(t, DENSE_ROWS)

    def body(x_ref, wg_ref, wu_ref, hg_ref, up_ref, a_ref):
        xv = x_ref[...]
        hg = _bdot_nt(xv, wg_ref[...])
        up = _bdot_nt(xv, wu_ref[...])
        hg_ref[...] = hg
        up_ref[...] = up
        a_ref[...] = (hg * _sigmoid(hg) * up).astype(ACT)

    return pl.pallas_call(
        body, name="ffn_up", grid=(t // tm,),
        in_specs=[_rows(tm, D_MODEL), _resident((D_FF, D_MODEL)), _resident((D_FF, D_MODEL))],
        out_specs=[_rows(tm, D_FF), _rows(tm, D_FF), _rows(tm, D_FF)],
        out_shape=[jax.ShapeDtypeStruct((t, D_FF), f32), jax.ShapeDtypeStruct((t, D_FF), f32), jax.ShapeDtypeStruct((t, D_FF), ACT)],
        compiler_params=_cparams(1, arbitrary=False),
    )(x1, wg, wu)


def _ffn_down_ln2(a, x1, wd, g, b):
    t = x1.shape[0]
    tm = _tile(t, DENSE_ROWS)

    def body(a_ref, x_ref, w_ref, g_ref, bb_ref, x2_ref, xh_ref, rs_ref, x2b_ref):
        ffn = _bdot(a_ref[...], w_ref[...])
        y, xh, rstd = _ln_fwd(ALPHA * x_ref[...] + ffn, g_ref[...], bb_ref[...])
        x2_ref[...] = y
        x2b_ref[...] = y.astype(ACT)
        xh_ref[...] = xh
        rs_ref[...] = rstd

    return pl.pallas_call(
        body, name="ffn_down_ln2", grid=(t // tm,),
        in_specs=[_rows(tm, D_FF), _rows(tm, D_MODEL), _resident((D_FF, D_MODEL)), _const((1, D_MODEL)), _const((1, D_MODEL))],
        out_specs=[_rows(tm, D_MODEL), _rows(tm, D_MODEL), _rows(tm, 1), _rows(tm, D_MODEL)],
        out_shape=[jax.ShapeDtypeStruct((t, D_MODEL), f32), jax.ShapeDtypeStruct((t, D_MODEL), f32), jax.ShapeDtypeStruct((t, 1), f32),
                   jax.ShapeDtypeStruct((t, D_MODEL), ACT)],
        compiler_params=_cparams(1, arbitrary=False),
    )(a, x1, wd, g, b)


def _head_loss_bwd(x2, xh2, rs2, p, tgt, w_pg, b_pg, w_pp, g2):
    t = x2.shape[0]
    tm = _tile(t, DENSE_ROWS)

    def body(x_ref, xh_ref, rs_ref, p_ref, t_ref, wg_ref, bg_ref, wp_ref, g_ref,
             dr_ref, de_ref, dz_ref, loss_ref, dbg_ref, dg2_ref, db2_ref):
        @pl.when(pl.program_id(0) == 0)
        def _():
            loss_ref[...] = jnp.zeros_like(loss_ref)
            dbg_ref[...] = jnp.zeros_like(dbg_ref)
            dg2_ref[...] = jnp.zeros_like(dg2_ref)
            db2_ref[...] = jnp.zeros_like(db2_ref)

        x2v = x_ref[...]
        z = _bdot(x2v, wg_ref[...]) + bg_ref[...]
        e = _bdot(p_ref[...], wp_ref[...])
        sg = _sigmoid(z)
        diff = x2v + sg * e - t_ref[...]
        loss_ref[...] += 0.5 * jnp.sum(jnp.mean(diff * diff, axis=-1, keepdims=True), axis=0, keepdims=True)
        dy = diff * (1.0 / D_MODEL)
        de_ref[...] = (dy * sg).astype(ACT)
        dz = dy * e * (sg * (1.0 - sg))
        dz_ref[...] = dz.astype(ACT)
        dbg_ref[...] += jnp.sum(dz, axis=0, keepdims=True)
        dx2 = dy + _bdot_nt(dz, wg_ref[...])
        xh = xh_ref[...]
        dg2_ref[...] += jnp.sum(dx2 * xh, axis=0, keepdims=True)
        db2_ref[...] += jnp.sum(dx2, axis=0, keepdims=True)
        dr_ref[...] = _ln_bwd(dx2, xh, rs_ref[...], g_ref[...])

    row = jax.ShapeDtypeStruct((1, D_MODEL), f32)
    return pl.pallas_call(
        body, name="head_loss_bwd", grid=(t // tm,),
        in_specs=[_rows(tm, D_MODEL), _rows(tm, D_MODEL), _rows(tm, 1), _rows(tm, PLE), _rows(tm, D_MODEL),
                  _resident((D_MODEL, D_MODEL)), _const((1, D_MODEL)), _resident((PLE, D_MODEL)), _const((1, D_MODEL))],
        out_specs=[_rows(tm, D_MODEL), _rows(tm, D_MODEL), _rows(tm, D_MODEL), _const((1, 1)), _const((1, D_MODEL)), _const((1, D_MODEL)), _const((1, D_MODEL))],
        out_shape=[jax.ShapeDtypeStruct((t, D_MODEL), f32), jax.ShapeDtypeStruct((t, D_MODEL), ACT), jax.ShapeDtypeStruct((t, D_MODEL), ACT),
                   jax.ShapeDtypeStruct((1, 1), f32), row, row, row],
        compiler_params=_cparams(1),
    )(x2, xh2, rs2, p, tgt, w_pg, b_pg, w_pp, g2)


def _ffn_bwd(dr2, hg, up, xh1, rs1, wd, wg, wu, g1):
    t = dr2.shape[0]
    tm = _tile(t, DENSE_ROWS // 2)

    def body(dr_ref, hg_ref, up_ref, xh_ref, rs_ref, wd_ref, wg_ref, wu_ref, g_ref,
             dr1_ref, dhg_ref, dup_ref, dg1_ref, db1_ref):
        @pl.when(pl.program_id(0) == 0)
        def _():
            dg1_ref[...] = jnp.zeros_like(dg1_ref)
            db1_ref[...] = jnp.zeros_like(db1_ref)

        dr2v = dr_ref[...]
        da = _bdot_nt(dr2v, wd_ref[...])
        hgv = hg_ref[...]
        sg = _sigmoid(hgv)
        dhg = da * up_ref[...] * (sg * (1.0 + hgv * (1.0 - sg)))
        dup = da * (hgv * sg)
        dhg_ref[...] = dhg.astype(ACT)
        dup_ref[...] = dup.astype(ACT)
        dx1 = ALPHA * dr2v + _bdot(dhg, wg_ref[...]) + _bdot(dup, wu_ref[...])
        xh = xh_ref[...]
        dg1_ref[...] += jnp.sum(dx1 * xh, axis=0, keepdims=True)
        db1_ref[...] += jnp.sum(dx1, axis=0, keepdims=True)
        dr1_ref[...] = _ln_bwd(dx1, xh, rs_ref[...], g_ref[...])

    row = jax.ShapeDtypeStruct((1, D_MODEL), f32)
    return pl.pallas_call(
        body, name="ffn_bwd", grid=(t // tm,),
        in_specs=[_rows(tm, D_MODEL), _rows(tm, D_FF), _rows(tm, D_FF), _rows(tm, D_MODEL), _rows(tm, 1),
                  _resident((D_FF, D_MODEL)), _resident((D_FF, D_MODEL)), _resident((D_FF, D_MODEL)), _const((1, D_MODEL))],
        out_specs=[_rows(tm, D_MODEL), _rows(tm, D_FF), _rows(tm, D_FF), _const((1, D_MODEL)), _const((1, D_MODEL))],
        out_shape=[jax.ShapeDtypeStruct((t, D_MODEL), f32), jax.ShapeDtypeStruct((t, D_FF), ACT), jax.ShapeDtypeStruct((t, D_FF), ACT), row, row],
        compiler_params=_cparams(1),
    )(dr2, hg, up, xh1, rs1, wd, wg, wu, g1)


def _outproj_bwd(dr1, w_out):
    t = dr1.shape[0]
    tm = _tile(t, DENSE_ROWS)

    def body(dr_ref, w_ref, dhg_ref, dml_ref):
        d = _bdot_nt(dr_ref[...], w_ref[...])
        dhg_ref[...] = d[:, 0:MIX_W]
        dml_ref[...] = d[:, MIX_W:2 * MIX_W]

    return pl.pallas_call(
        body, name="outproj_bwd", grid=(t // tm,),
        in_specs=[_rows(tm, D_MODEL), _resident((D_MODEL, D_MODEL))],
        out_specs=[_rows(tm, MIX_W), _rows(tm, MIX_W)],
        out_shape=[jax.ShapeDtypeStruct((t, MIX_W), f32)] * 2,
        compiler_params=_cparams(1, arbitrary=False),
    )(dr1, w_out)


def _inproj_bwd(dr1, du_hg, dqk, dmv, dmo, dgt, w_hg, w_ml):
    t = dr1.shape[0]
    tm = _tile(t, DENSE_ROWS)

    def body(dr_ref, dhg_ref, dqk_ref, dmv_ref, dmo_ref, dgt_ref, whg_ref, wml_ref, gx_ref, dml_ref):
        dml = jnp.concatenate([dqk_ref[...], dmv_ref[...], dmo_ref[...], dgt_ref[...]], axis=-1).astype(ACT)
        dml_ref[...] = dml
        gx_ref[...] = ALPHA * dr_ref[...] + _bdot(dhg_ref[...], whg_ref[...]) + _bdot(dml, wml_ref[...])

    return pl.pallas_call(
        body, name="inproj_bwd", grid=(t // tm,),
        in_specs=[_rows(tm, D_MODEL), _rows(tm, U_HG), _rows(tm, MIX_W), _rows(tm, MIX_W), _rows(tm, MIX_W), _rows(tm, 128),
                  _resident((U_HG, D_MODEL)), _resident((U_ML, D_MODEL))],
        out_specs=[_rows(tm, D_MODEL), _rows(tm, U_ML)],
        out_shape=[jax.ShapeDtypeStruct((t, D_MODEL), f32), jax.ShapeDtypeStruct((t, U_ML), ACT)],
        compiler_params=_cparams(1, arbitrary=False),
    )(dr1, du_hg, dqk, dmv, dmo, dgt, w_hg, w_ml)


def _wgrad(a, b, name, tk=None, tn=None):
    t, kdim = a.shape
    n = b.shape[1]
    tk = tk or kdim
    tn = tn or n
    tt = _tile(t, WGRAD_ROWS)

    def body(a_ref, b_ref, o_ref):
        @pl.when(pl.program_id(2) == 0)
        def _():
            o_ref[...] = jnp.zeros_like(o_ref)

        o_ref[...] += _bdot_tn(a_ref[...], b_ref[...])

    return pl.pallas_call(
        body, name=name, grid=(kdim // tk, n // tn, t // tt),
        in_specs=[pl.BlockSpec((tt, tk), lambda i, j, s: (s, i)), pl.BlockSpec((tt, tn), lambda i, j, s: (s, j))],
        out_specs=pl.BlockSpec((tk, tn), lambda i, j, s: (i, j)),
        out_shape=jax.ShapeDtypeStruct((kdim, n), f32),
        compiler_params=_cparams(3),
    )(a, b)


def _colsum(parts, name):
    t = parts[0].shape[0]
    tt = _tile(t, 512)
    widths = [a.shape[1] for a in parts]

    def body(*refs):
        o_ref = refs[-1]

        @pl.when(pl.program_id(0) == 0)
        def _():
            o_ref[...] = jnp.zeros_like(o_ref)

        off = 0
        for r, w in zip(refs[:-1], widths):
            o_ref[:, off:off + w] += jnp.sum(r[...].astype(f32), axis=0, keepdims=True)
            off += w

    return pl.pallas_call(
        body, name=name, grid=(t // tt,),
        in_specs=[_rows(tt, w) for w in widths],
        out_specs=_const((1, sum(widths))),
        out_shape=jax.ShapeDtypeStruct((1, sum(widths)), f32),
        compiler_params=_cparams(1),
    )(*parts)


_TRANSPOSED = {"w_in", "w_ffn_gate", "w_ffn_up"}
_COL_SPLIT = {"ple_w_proj"}
_RIDE_PLAN = (("w_out", "ple_w_gate", "ple_w_proj"), ("w_ffn_down",), ("w_ffn_gate", "w_ffn_up"))


def _from_chip_major(a, col_split):
    if col_split:
        return a.transpose(1, 0, 2).reshape(a.shape[1], 4 * a.shape[2])
    return a.reshape(4 * a.shape[1], a.shape[2])


def _local_step(x, p, tgt, w_in_b, b_in, logits, conv_w, conv_b, hg_gn, ml_gn, w_out_b, ln1_g, ln1_b,
                wg_b, wu_b, wd_b, ln2_g, ln2_b, w_pp_b, w_pg_b, b_pg, early_hook=None, late_shards=None):
    pad_w = U_HG + U_ML - PROJ_W
    w_hg = w_in_b[:U_HG]
    w_ml = jnp.pad(w_in_b[U_HG:], ((0, pad_w), (0, 0)))
    bb_hg = b_in[:, :U_HG]
    bb_ml = jnp.pad(b_in[:, U_HG:], ((0, 0), (0, pad_w)))

    ride = [[late_shards[k] for k in names] for names in _RIDE_PLAN] if late_shards is not None else [(), (), ()]
    (u_hg, u_ml, xb), got0 = _inproj(x, w_hg, w_ml, bb_hg, bb_ml, ride[0])
    (og_hg, sst), got1 = _hgrn2_fwd(u_hg, logits, hg_gn, ride[1])
    pre, qkc = _conv_fwd(u_ml, conv_w, conv_b)
    (og_ml, cst, nst, mst), got2 = _mlstm_fwd(qkc, u_ml, ml_gn, ride[2])
    if late_shards is not None:
        late = {k: _from_chip_major(g, k in _COL_SPLIT) for names, got in zip(_RIDE_PLAN, (got0, got1, got2)) for k, g in zip(names, got)}
        w_out_b, wg_b, wu_b, wd_b = late["w_out"], late["w_ffn_gate"], late["w_ffn_up"], late["w_ffn_down"]
        w_pp_b, w_pg_b = late["ple_w_proj"], late["ple_w_gate"]
    x1, xh1, rs1, x1b = _outproj_ln1(og_hg, og_ml, x, w_out_b, ln1_g, ln1_b)
    hgp, up, act = _ffn_up(x1b, wg_b, wu_b)
    x2, xh2, rs2, x2b = _ffn_down_ln2(act, x1, wd_b, ln2_g, ln2_b)
    dr2, de, dz, loss, d_bpg, d_ln2g, d_ln2b = _head_loss_bwd(x2, xh2, rs2, p, tgt, w_pg_b, b_pg, w_pp_b, ln2_g)
    dr1, dhg, dup, d_ln1g, d_ln1b = _ffn_bwd(dr2, hgp, up, xh1, rs1, wd_b, wg_b, wu_b, ln1_g)

    d_wo_a = _wgrad(og_hg, dr1, "wgrad_out_hg")
    d_wo_b = _wgrad(og_ml, dr1, "wgrad_out_ml")
    d_w_out = jnp.concatenate([d_wo_a, d_wo_b], axis=0)
    d_wg = _wgrad(dhg, x1b, "wgrad_ffn_gate", tk=D_FF // 2)
    d_wu = _wgrad(dup, x1b, "wgrad_ffn_up", tk=D_FF // 2)
    d_wd = _wgrad(act, dr2, "wgrad_ffn_down", tk=D_FF // 2)
    d_wpp = _wgrad(p, de, "wgrad_ple_proj")
    d_wpg = _wgrad(x2b, dz, "wgrad_ple_gate")
    early = dict(w_out=d_w_out, w_ffn_gate=d_wg, w_ffn_up=d_wu, w_ffn_down=d_wd, ple_w_proj=d_wpp, ple_w_gate=d_wpg)
    riders = early_hook(early) if early_hook is not None else ()

    dog_hg, dog_ml = _outproj_bwd(dr1, w_out_b)
    res = _hgrn2_bwd(u_hg, logits, hg_gn, sst, dog_hg, riders)
    du_hg, d_logits, d_hg_gn = res[:3]
    dqkc, dmv, dmo, dgt, d_ml_gn = _mlstm_bwd(qkc, u_ml, ml_gn, cst, nst, mst, dog_ml)
    dqk, d_conv_w, d_conv_b = _conv_bwd(u_ml, conv_w, pre, dqkc)
    grad_x, du_ml = _inproj_bwd(dr1, du_hg, dqk, dmv, dmo, dgt, w_hg, w_ml)

    dw_hg = _wgrad(du_hg, xb, "wgrad_in_hg", tk=U_HG // 2)
    dw_ml = _wgrad(du_ml, xb, "wgrad_in_ml")
    d_w_in = jnp.concatenate([dw_hg, dw_ml[:PROJ_W - U_HG]], axis=0)
    d_b_in = _colsum([du_hg, du_ml], "colsum_du")[:, :PROJ_W]

    grads = dict(w_in=d_w_in, b_in=d_b_in, hg_lb_logits=d_logits, ml_conv_w=d_conv_w, ml_conv_b=d_conv_b,
                 hg_norm_g=d_hg_gn, ml_norm_g=d_ml_gn, ln1_g=d_ln1g, ln1_b=d_ln1b, ln2_g=d_ln2g, ln2_b=d_ln2b,
                 ple_b_gate=d_bpg, **early)
    return loss, grad_x, grads, list(res[3:])


_ANY = pl.BlockSpec(memory_space=pltpu.HBM)
_MESH = pl.DeviceIdType.MESH


def _my_place():
    return lax.axis_index("x"), lax.axis_index("y"), lax.axis_index("c")


def _other_chips(x, y):
    return [(1 - x, y), (x, 1 - y), (1 - x, 1 - y)]


def _allgather_weights(shards, taps, name):
    n = len(shards)
    halves = [s.shape[0] // 2 for s in shards]

    def body(*refs):
        ins, tap_in = refs[:n], refs[n]
        outs, tap_out = refs[n + 1:2 * n + 1], refs[2 * n + 1]
        send_sems, recv_sems, local_sems = refs[2 * n + 2:]
        x, y, c = _my_place()
        me = 2 * x + y
        sibling = (x, y, 1 - c)
        chips = _other_chips(x, y)

        def ici(a, j, block_chip):
            px, py = chips[j]
            src = ins[a].at[pl.ds(pl.multiple_of(c * halves[a], 16), halves[a])] if block_chip is None else outs[a].at[block_chip, c]
            dst = outs[a].at[me if block_chip is None else block_chip, c]
            return pltpu.make_async_remote_copy(src_ref=src, dst_ref=dst, send_sem=send_sems.at[6 * a + j], recv_sem=recv_sems.at[6 * a + j],
                                                device_id=(px, py, c), device_id_type=_MESH)

        def d2d(a, j, half):
            px, py = chips[j]
            blk = outs[a].at[2 * px + py, half]
            return pltpu.make_async_remote_copy(src_ref=blk, dst_ref=blk, send_sem=send_sems.at[6 * a + 3 + j], recv_sem=recv_sems.at[6 * a + 3 + j],
                                                device_id=sibling, device_id_type=_MESH)

        local = []
        for a in range(n):
            for h in range(2):
                cp = pltpu.make_async_copy(ins[a].at[pl.ds(h * halves[a], halves[a])], outs[a].at[me, h], local_sems.at[2 * a + h])
                cp.start()
                local.append(cp)
            for j in range(3):
                ici(a, j, None).start()
        tap_local = pltpu.make_async_copy(tap_in, tap_out.at[me], local_sems.at[2 * n])
        tap_local.start()
        tap_copies = []
        for j, (px, py) in enumerate(chips):
            cp = pltpu.make_async_remote_copy(src_ref=tap_in, dst_ref=tap_out.at[me], send_sem=send_sems.at[6 * n + j], recv_sem=recv_sems.at[6 * n + j],
                                              device_id=(px, py, c), device_id_type=_MESH)
            cp.start()
            tap_copies.append(cp)
        for a in range(n):
            for j, (px, py) in enumerate(chips):
                ici(a, j, 2 * px + py).wait_recv()
                d2d(a, j, c).start()
        for a in range(n):
            for j in range(3):
                d2d(a, j, 1 - c).wait_recv()
        for a in range(n):
            for j in range(3):
                ici(a, j, None).wait_send()
                d2d(a, j, c).wait_send()
        for j, (px, py) in enumerate(chips):
            pltpu.make_async_remote_copy(src_ref=tap_in, dst_ref=tap_out.at[2 * px + py], send_sem=send_sems.at[6 * n + j], recv_sem=recv_sems.at[6 * n + j],
                                         device_id=(px, py, c), device_id_type=_MESH).wait()
        for cp in local:
            cp.wait()
        tap_local.wait()

    res = pl.pallas_call(
        body, name=name,
        in_specs=[_ANY] * (n + 1), out_specs=[_ANY] * (n + 1),
        out_shape=[jax.ShapeDtypeStruct((4, 2, s.shape[0] // 2, s.shape[1]), s.dtype) for s in shards]
        + [jax.ShapeDtypeStruct((4,) + taps.shape, taps.dtype)],
        scratch_shapes=[pltpu.SemaphoreType.DMA((6 * n + 3,)), pltpu.SemaphoreType.DMA((6 * n + 3,)), pltpu.SemaphoreType.DMA((2 * n + 1,))],
    )(*shards, taps)
    return [w.reshape((4,) + s.shape) for w, s in zip(res[:n], shards)], res[n]


def _swap_halves(pieces, name):
    n = len(pieces)
    halves = [p.shape[1] // 2 for p in pieces]

    def body(*refs):
        ins, own, other = refs[:n], refs[n:2 * n], refs[2 * n:3 * n]
        send_sems, recv_sems, local_sems = refs[3 * n:]
        x, y, c = _my_place()

        def half_of(a, which):
            return ins[a].at[pl.ds(0, 4), pl.ds(pl.multiple_of(which * halves[a], 16), halves[a])]

        def to_sibling(a):
            return pltpu.make_async_remote_copy(src_ref=half_of(a, 1 - c), dst_ref=other[a], send_sem=send_sems.at[a], recv_sem=recv_sems.at[a],
                                                device_id=(x, y, 1 - c), device_id_type=_MESH)

        local = []
        for a in range(n):
            cp = pltpu.make_async_copy(half_of(a, c), own[a], local_sems.at[a])
            cp.start()
            local.append(cp)
            to_sibling(a).start()
        for a in range(n):
            to_sibling(a).wait()
            local[a].wait()

    shapes = [jax.ShapeDtypeStruct((4, p.shape[1] // 2, p.shape[2]), p.dtype) for p in pieces]
    res = pl.pallas_call(
        body, name=name,
        in_specs=[_ANY] * n, out_specs=[_ANY] * (2 * n), out_shape=shapes + shapes,
        scratch_shapes=[pltpu.SemaphoreType.DMA((n,)), pltpu.SemaphoreType.DMA((n,)), pltpu.SemaphoreType.DMA((n,))],
    )(*pieces)
    return res[:n], res[n:]


_VMEM = pl.BlockSpec(memory_space=pltpu.VMEM)
_EX_ROWS = 32


def _pair_reduce(p, name):
    s, r, c = p.shape
    half = r // 2

    def body(p_ref, o_ref, other, send_sem, recv_sem):
        x, y, cc = _my_place()
        theirs = pl.multiple_of((1 - cc) * half, 16)
        mine = pl.multiple_of(cc * half, 16)
        cp = pltpu.make_async_remote_copy(src_ref=p_ref.at[pl.ds(0, s), pl.ds(theirs, half)], dst_ref=other, send_sem=send_sem, recv_sem=recv_sem,
                                          device_id=(x, y, 1 - cc), device_id_type=_MESH)
        cp.start()
        cp.wait()

        def step(i, carry):
            r0 = pl.multiple_of(i * _EX_ROWS, _EX_ROWS)
            for slot in range(s):
                own_rows = pl.ds(pl.multiple_of(mine + r0, 16), _EX_ROWS)
                o_ref[slot, pl.ds(r0, _EX_ROWS), :] = (p_ref[slot, own_rows, :] + other[slot, pl.ds(r0, _EX_ROWS), :]).astype(bf16)
            return carry

        lax.fori_loop(0, half // _EX_ROWS, step, 0)

    return pl.pallas_call(
        body, name=name, in_specs=[_VMEM], out_specs=_VMEM,
        out_shape=jax.ShapeDtypeStruct((s, half, c), bf16),
        scratch_shapes=[pltpu.VMEM((s, half, c), f32), pltpu.SemaphoreType.DMA, pltpu.SemaphoreType.DMA],
        compiler_params=pltpu.CompilerParams(vmem_limit_bytes=VMEM_LIMIT),
    )(p)


def _chip_reduce_swap(rcv, name):
    s, h, c = rcv.shape

    def body(r_ref, g_ref, send_sem, recv_sem):
        x, y, cc = _my_place()

        def step(i, carry):
            r0 = pl.multiple_of(i * _EX_ROWS, _EX_ROWS)
            acc = r_ref[0, pl.ds(r0, _EX_ROWS), :].astype(f32)
            for slot in range(1, s):
                acc = acc + r_ref[slot, pl.ds(r0, _EX_ROWS), :].astype(f32)
            g_ref[cc, pl.ds(r0, _EX_ROWS), :] = acc
            return carry

        lax.fori_loop(0, h // _EX_ROWS, step, 0)
        cp = pltpu.make_async_remote_copy(src_ref=g_ref.at[cc], dst_ref=g_ref.at[cc], send_sem=send_sem, recv_sem=recv_sem,
                                          device_id=(x, y, 1 - cc), device_id_type=_MESH)
        cp.start()
        cp.wait()

    return pl.pallas_call(
        body, name=name, in_specs=[_VMEM], out_specs=_VMEM,
        out_shape=jax.ShapeDtypeStruct((2, h, c), f32),
        scratch_shapes=[pltpu.SemaphoreType.DMA, pltpu.SemaphoreType.DMA],
        compiler_params=pltpu.CompilerParams(vmem_limit_bytes=VMEM_LIMIT),
    )(rcv)


def _pair_reduce_cols(p, name):
    s, r, c = p.shape
    hc = c // 2

    def body(p_ref, o_ref, other, send_sem, recv_sem):
        x, y, cc = _my_place()

        def run(mine_lo, theirs_lo):
            cp = pltpu.make_async_remote_copy(src_ref=p_ref.at[pl.ds(0, s), pl.ds(0, r), pl.ds(theirs_lo, hc)], dst_ref=other,
                                              send_sem=send_sem, recv_sem=recv_sem, device_id=(x, y, 1 - cc), device_id_type=_MESH)
            cp.start()
            cp.wait()
            for slot in range(s):
                o_ref[slot] = (p_ref[slot, :, mine_lo:mine_lo + hc] + other[slot]).astype(bf16)

        @pl.when(cc == 0)
        def _():
            run(0, hc)

        @pl.when(cc == 1)
        def _():
            run(hc, 0)

    return pl.pallas_call(
        body, name=name, in_specs=[_VMEM], out_specs=_VMEM,
        out_shape=jax.ShapeDtypeStruct((s, r, hc), bf16),
        scratch_shapes=[pltpu.VMEM((s, r, hc), f32), pltpu.SemaphoreType.DMA, pltpu.SemaphoreType.DMA],
        compiler_params=pltpu.CompilerParams(vmem_limit_bytes=VMEM_LIMIT),
    )(p)


def _chip_reduce_swap_cols(rcv, name):
    s, r, hc = rcv.shape

    def body(r_ref, g_ref, send_sem, recv_sem):
        x, y, cc = _my_place()
        acc = r_ref[0].astype(f32)
        for slot in range(1, s):
            acc = acc + r_ref[slot].astype(f32)
        g_ref[cc] = acc
        cp = pltpu.make_async_remote_copy(src_ref=g_ref.at[cc], dst_ref=g_ref.at[cc], send_sem=send_sem, recv_sem=recv_sem,
                                          device_id=(x, y, 1 - cc), device_id_type=_MESH)
        cp.start()
        cp.wait()

    both = pl.pallas_call(
        body, name=name, in_specs=[_VMEM], out_specs=_VMEM,
        out_shape=jax.ShapeDtypeStruct((2, r, hc), f32),
        scratch_shapes=[pltpu.SemaphoreType.DMA, pltpu.SemaphoreType.DMA],
        compiler_params=pltpu.CompilerParams(vmem_limit_bytes=VMEM_LIMIT),
    )(rcv)
    return both.transpose(1, 0, 2).reshape(r, 2 * hc)


def _add_cast(a, b, name):
    s, r, c = a.shape
    tr = _row_tile(r, c)

    def body(a_ref, b_ref, o_ref):
        o_ref[...] = (a_ref[...] + b_ref[...]).astype(bf16)

    blk = pl.BlockSpec((1, tr, c), lambda i, j: (i, j, 0))
    return pl.pallas_call(
        body, name=name, grid=(s, r // tr), in_specs=[blk, blk], out_specs=blk,
        out_shape=jax.ShapeDtypeStruct(a.shape, bf16),
        compiler_params=_cparams(2, arbitrary=False),
    )(a, b)


def _gather_copies(ins, outs, send_sems, recv_sems, local_sems):
    x, y, c = _my_place()
    me = 2 * x + y
    local, outgoing, incoming = [], [], []
    for a in range(len(ins)):
        local.append(pltpu.make_async_copy(ins[a], outs[a].at[me], local_sems.at[a]))
        for j, (px, py) in enumerate(_other_chips(x, y)):
            sems = dict(send_sem=send_sems.at[3 * a + j], recv_sem=recv_sems.at[3 * a + j], device_id=(px, py, c), device_id_type=_MESH)
            outgoing.append(pltpu.make_async_remote_copy(src_ref=ins[a], dst_ref=outs[a].at[me], **sems))
            incoming.append(pltpu.make_async_remote_copy(src_ref=ins[a], dst_ref=outs[a].at[2 * px + py], **sems))
    return local, outgoing, incoming


def _gather_chips(blocks, name):
    n = len(blocks)

    def body(*refs):
        local, outgoing, incoming = _gather_copies(refs[:n], refs[n:2 * n], *refs[2 * n:])
        for cp in local + outgoing:
            cp.start()
        for cp in incoming:
            cp.wait_recv()
        for cp in outgoing:
            cp.wait_send()
        for cp in local:
            cp.wait()

    return pl.pallas_call(
        body, name=name, in_specs=[_ANY] * n, out_specs=[_ANY] * n, out_shape=_gather_shapes(blocks),
        scratch_shapes=[pltpu.SemaphoreType.DMA((3 * n,)), pltpu.SemaphoreType.DMA((3 * n,)), pltpu.SemaphoreType.DMA((n,))],
    )(*blocks)


def _riding_call(body, name, nsteps, in_specs, out_specs, out_shape, scratch_shapes, operands, riders, copies, ride_shapes):
    nr, n_in, n_out, n_scr = len(riders), len(in_specs), len(out_specs), len(scratch_shapes)

    def wrapped(*refs):
        ins, ride_in = refs[:n_in], refs[n_in:n_in + nr]
        outs, ride_out = refs[n_in + nr:n_in + nr + n_out], refs[n_in + nr + n_out:n_in + 2 * nr + n_out]
        scratch, sems = refs[n_in + 2 * nr + n_out:n_in + 2 * nr + n_out + n_scr], refs[n_in + 2 * nr + n_out + n_scr:]
        if nr:
            @pl.when(pl.program_id(0) == 0)
            def _():
                local, outgoing, _ = copies(ride_in, ride_out, *sems)
                for cp in local + outgoing:
                    cp.start()

        body(*ins, *outs, *scratch)
        if nr:
            @pl.when(pl.program_id(0) == nsteps - 1)
            def _():
                local, outgoing, incoming = copies(ride_in, ride_out, *sems)
                for cp in incoming:
                    cp.wait_recv()
                for cp in outgoing:
                    cp.wait_send()
                for cp in local:
                    cp.wait()

    hbm = pl.BlockSpec(memory_space=pltpu.HBM)
    sems = [pltpu.SemaphoreType.DMA((3 * nr,)), pltpu.SemaphoreType.DMA((3 * nr,)), pltpu.SemaphoreType.DMA((nr,))] if nr else []
    res = pl.pallas_call(
        wrapped, name=name, grid=(nsteps,),
        in_specs=list(in_specs) + [hbm] * nr, out_specs=list(out_specs) + [hbm] * nr,
        out_shape=list(out_shape) + list(ride_shapes),
        scratch_shapes=list(scratch_shapes) + sems,
        compiler_params=_cparams(1),
    )(*operands, *riders)
    return list(res[:n_out]), list(res[n_out:])


def _gather_shapes(riders):
    return [jax.ShapeDtypeStruct((4,) + r.shape, r.dtype) for r in riders]


def _scatter_copies(ins, outs, send_sems, recv_sems, local_sems):
    x, y, c = _my_place()
    me = 2 * x + y
    local, outgoing, incoming = [], [], []
    for a in range(len(ins)):
        local.append(pltpu.make_async_copy(ins[a].at[me], outs[a].at[me], local_sems.at[a]))
        for j, (px, py) in enumerate(_other_chips(x, y)):
            sems = dict(send_sem=send_sems.at[3 * a + j], recv_sem=recv_sems.at[3 * a + j], device_id=(px, py, c), device_id_type=_MESH)
            outgoing.append(pltpu.make_async_remote_copy(src_ref=ins[a].at[2 * px + py], dst_ref=outs[a].at[me], **sems))
            incoming.append(pltpu.make_async_remote_copy(src_ref=ins[a].at[2 * px + py], dst_ref=outs[a].at[2 * px + py], **sems))
    return local, outgoing, incoming


def _scatter_start(ins, outs, send_sems, recv_sems, local_sems):
    local, outgoing, _ = _scatter_copies(ins, outs, send_sems, recv_sems, local_sems)
    for cp in local + outgoing:
        cp.start()


def _scatter_wait(ins, outs, send_sems, recv_sems, local_sems):
    local, outgoing, incoming = _scatter_copies(ins, outs, send_sems, recv_sems, local_sems)
    for cp in incoming:
        cp.wait_recv()
    for cp in outgoing:
        cp.wait_send()
    for cp in local:
        cp.wait()


def _scatter_chips(pieces, name):
    n = len(pieces)

    def body(*refs):
        ins, outs = refs[:n], refs[n:2 * n]
        _scatter_start(ins, outs, *refs[2 * n:])
        _scatter_wait(ins, outs, *refs[2 * n:])

    return pl.pallas_call(
        body, name=name,
        in_specs=[_ANY] * n, out_specs=[_ANY] * n,
        out_shape=[jax.ShapeDtypeStruct(s.shape, s.dtype) for s in pieces],
        scratch_shapes=[pltpu.SemaphoreType.DMA((3 * n,)), pltpu.SemaphoreType.DMA((3 * n,)), pltpu.SemaphoreType.DMA((n,))],
    )(*pieces)


def _swap_cores(blocks, name):
    n = len(blocks)
    parts = 4
    rows = [b.shape[0] // parts for b in blocks]

    def body(*refs):
        ins, outs = refs[:n], refs[n:2 * n]
        send_sems, recv_sems, local_sems = refs[2 * n:]
        x, y, c = _my_place()

        def remote(a, k, slot):
            rs = pl.ds(k * rows[a], rows[a])
            return pltpu.make_async_remote_copy(src_ref=ins[a].at[rs], dst_ref=outs[a].at[slot, rs], send_sem=send_sems.at[parts * a + k],
                                                recv_sem=recv_sems.at[parts * a + k], device_id=(x, y, 1 - c), device_id_type=_MESH)

        local = []
        for a in range(n):
            cp = pltpu.make_async_copy(ins[a], outs[a].at[c], local_sems.at[a])
            cp.start()
            local.append(cp)
            for k in range(parts):
                remote(a, k, c).start()
        for a in range(n):
            for k in range(parts):
                remote(a, k, 1 - c).wait()
            local[a].wait()

    return pl.pallas_call(
        body, name=name,
        in_specs=[_ANY] * n, out_specs=[_ANY] * n,
        out_shape=[jax.ShapeDtypeStruct((2,) + s.shape, s.dtype) for s in blocks],
        scratch_shapes=[pltpu.SemaphoreType.DMA((parts * n,)), pltpu.SemaphoreType.DMA((parts * n,)), pltpu.SemaphoreType.DMA((n,))],
    )(*blocks)


def _gather_all(block, name):
    def body(in_ref, out_ref, send_sems, recv_sems, local_sem):
        x, y, c = _my_place()
        me = 4 * x + 2 * y + c
        cp = pltpu.make_async_copy(in_ref, out_ref.at[me], local_sem)
        cp.start()
        peers = []
        for dx in range(2):
            for dy in range(2):
                for dc in range(2):
                    if dx or dy or dc:
                        peers.append((1 - x if dx else x, 1 - y if dy else y, 1 - c if dc else c))
        for j, pr in enumerate(peers):
            pltpu.make_async_remote_copy(src_ref=in_ref, dst_ref=out_ref.at[me], send_sem=send_sems.at[j], recv_sem=recv_sems.at[j],
                                         device_id=pr, device_id_type=_MESH).start()
        for j, (px, py, pc) in enumerate(peers):
            pltpu.make_async_remote_copy(src_ref=in_ref, dst_ref=out_ref.at[4 * px + 2 * py + pc], send_sem=send_sems.at[j], recv_sem=recv_sems.at[j],
                                         device_id=(px, py, pc), device_id_type=_MESH).wait()
        cp.wait()

    return pl.pallas_call(
        body, name=name,
        in_specs=[_ANY], out_specs=_ANY,
        out_shape=jax.ShapeDtypeStruct((8,) + block.shape, block.dtype),
        scratch_shapes=[pltpu.SemaphoreType.DMA((7,)), pltpu.SemaphoreType.DMA((7,)), pltpu.SemaphoreType.DMA],
    )(block)


def _row_tile(r, c):
    best = r
    for cand in range(16, r + 1, 16):
        if r % cand == 0 and cand * c * 4 <= (1 << 20):
            best = cand
    return best if best * c * 4 <= (4 << 20) else r


def _sum_slots(parts, name):
    n, r, c = parts.shape
    tr = _row_tile(r, c)

    def body(p_ref, o_ref):
        acc = p_ref[0].astype(f32)
        for s in range(1, n):
            acc = acc + p_ref[s].astype(f32)
        o_ref[...] = acc

    return pl.pallas_call(
        body, name=name, grid=(r // tr,),
        in_specs=[pl.BlockSpec((n, tr, c), lambda i: (0, i, 0))],
        out_specs=pl.BlockSpec((tr, c), lambda i: (i, 0)),
        out_shape=jax.ShapeDtypeStruct((r, c), f32),
        compiler_params=_cparams(1, arbitrary=False),
    )(parts)


def _adamw(parts, w, m, v, name):
    n, r, c = parts.shape
    tr = _row_tile(r, c)
    tc = c
    if tr == r and r * c * 4 > (1 << 20) and c % 256 == 0:
        tc = 256

    def body(p_ref, w_ref, m_ref, v_ref, g_ref, d_ref, nm_ref, nv_ref):
        g = p_ref[0]
        for s in range(1, n):
            g = g + p_ref[s]
        nm = B1 * m_ref[...] + (1.0 - B1) * g
        nv = B2 * v_ref[...] + (1.0 - B2) * (g * g)
        m_hat = nm / (1.0 - B1 ** STEP)
        v_hat = nv / (1.0 - B2 ** STEP)
        g_ref[...] = g
        nm_ref[...] = nm
        nv_ref[...] = nv
        d_ref[...] = -LR * (m_hat / (jnp.sqrt(v_hat) + EPS_ADAM) + WD * w_ref[...])

    blk = pl.BlockSpec((tr, tc), lambda i, j: (i, j))
    return pl.pallas_call(
        body, name=name, grid=(r // tr, c // tc),
        in_specs=[pl.BlockSpec((n, tr, tc), lambda i, j: (0, i, j)), blk, blk, blk],
        out_specs=[blk] * 4,
        out_shape=[jax.ShapeDtypeStruct((r, c), f32)] * 4,
        compiler_params=_cparams(2, arbitrary=False),
    )(parts, w, m, v)


_BIG = ["w_in", "w_out", "w_ffn_gate", "w_ffn_up", "w_ffn_down", "ple_w_proj", "ple_w_gate"]
_SMALL = ["b_in", "hg_lb_logits", "ml_conv_w", "ml_conv_b", "hg_norm_g", "ml_norm_g", "ln1_g", "ln1_b", "ln2_g", "ln2_b", "ple_b_gate"]
_ORDER = ["w_in", "b_in", "hg_lb_logits", "ml_conv_w", "ml_conv_b", "hg_norm_g", "ml_norm_g", "w_out", "ln1_g", "ln1_b",
          "w_ffn_gate", "w_ffn_up", "w_ffn_down", "ln2_g", "ln2_b", "ple_w_proj", "ple_w_gate", "ple_b_gate"]
_PACK_ROWS, _PACK_COLS = 16, 1024


def _pack(arrays):
    flat = jnp.concatenate([a.reshape(-1) for a in arrays])
    return jnp.pad(flat, (0, _PACK_ROWS * _PACK_COLS - flat.shape[0])).reshape(_PACK_ROWS, _PACK_COLS)


def _unpack(pack, shapes):
    flat = pack.reshape(-1)
    out, off = [], 0
    for s in shapes:
        size = 1
        for d in s:
            size *= d
        out.append(flat[off:off + size].reshape(s))
        off += size
    return out


def _to_chip_major(g, col_split):
    if col_split:
        k, n = g.shape
        return g.reshape(k, 4, n // 4).transpose(1, 0, 2)
    k, n = g.shape
    return g.reshape(4, k // 4, n)


def kernel(x, p, w_in, b_in, hg_lb_logits, ml_conv_w, ml_conv_b, hg_norm_g, ml_norm_g, w_out, ln1_g, ln1_b, w_ffn_gate, w_ffn_up, w_ffn_down, ln2_g, ln2_b, ple_w_proj, ple_w_gate, ple_b_gate, loss_target, m_w_in, m_b_in, m_hg_lb_logits, m_ml_conv_w, m_ml_conv_b, m_hg_norm_g, m_ml_norm_g, m_w_out, m_ln1_g, m_ln1_b, m_w_ffn_gate, m_w_ffn_up, m_w_ffn_down, m_ln2_g, m_ln2_b, m_ple_w_proj, m_ple_w_gate, m_ple_b_gate, v_w_in, v_b_in, v_hg_lb_logits, v_ml_conv_w, v_ml_conv_b, v_hg_norm_g, v_ml_norm_g, v_w_out, v_ln1_g, v_ln1_b, v_w_ffn_gate, v_w_ffn_up, v_w_ffn_down, v_ln2_g, v_ln2_b, v_ple_w_proj, v_ple_w_gate, v_ple_b_gate):
    args = dict(locals())
    wts = {k: args[k] for k in _ORDER}
    mom = {k: args["m_" + k] for k in _ORDER}
    var = {k: args["v_" + k] for k in _ORDER}
    two_d = lambda a: a.reshape(a.shape[-2], a.shape[-1])
    block = lambda k, a: jnp.swapaxes(two_d(a), 0, 1) if k in _TRANSPOSED else two_d(a)
    unblock = lambda k, a: (jnp.swapaxes(a, 0, 1) if k in _TRANSPOSED else a).reshape(wts[k].shape)

    shards = {k: block(k, wts[k]).astype(bf16) for k in _BIG}
    w_in_blocks, taps = _gather_chips([shards["w_in"], two_d(ml_conv_w)], "gather_w_in")
    w_in_full = _from_chip_major(w_in_blocks, False)
    conv_w_full = _from_chip_major(taps, True)

    def core_sum(k, g):
        pieces = _to_chip_major(g, k in _COL_SPLIT)
        if pieces.shape[1] % (2 * _EX_ROWS):
            return _pair_reduce_cols(pieces, "pair_reduce_" + k)
        return _pair_reduce(pieces, "pair_reduce_" + k)

    early_keys = _BIG[1:]
    loss, grad_x, grads, early_received = _local_step(
        x[0], p[0, 0], loss_target[0], w_in_full, b_in, hg_lb_logits, conv_w_full, ml_conv_b, hg_norm_g, ml_norm_g,
        None, ln1_g, ln1_b, None, None, None, ln2_g, ln2_b, None, None, ple_b_gate,
        early_hook=lambda early: [core_sum(k, early[k]) for k in early_keys],
        late_shards={k: shards[k] for k in early_keys})

    received = list(_scatter_chips([core_sum("w_in", grads["w_in"])], "scatter_grad_w_in")) + list(early_received)
    out_g, out_d, out_m, out_v = {}, {}, {}, {}
    for k, rcv in zip(_BIG, received):
        own = block(k, wts[k])
        if rcv.shape[1] == own.shape[0]:
            whole = _chip_reduce_swap_cols(rcv, "chip_reduce_" + k)
        else:
            parts = _chip_reduce_swap(rcv, "chip_reduce_" + k)
            whole = parts.reshape(2 * parts.shape[1], parts.shape[2])
        g, d, nm, nv = _adamw(whole[None], own, block(k, mom[k]), block(k, var[k]), "adamw_" + k)
        out_g[k], out_d[k], out_m[k], out_v[k] = unblock(k, g), unblock(k, d), unblock(k, nm), unblock(k, nv)

    small_shapes = [(1, PROJ_W), (2, MIX_W), (CONV_K, MIX_W)] + [(1, MIX_W)] * 3 + [(1, D_MODEL)] * 5 + [(1, 1)]
    contrib = _pack([grads[k] for k in _SMALL] + [loss])
    summed = _sum_slots(_gather_all(contrib, "gather_small"), "sum_small")
    small = _unpack(summed, small_shapes)
    loss_total = small[-1].reshape(())
    gsm = dict(zip(_SMALL, small[:-1]))
    place = 2 * lax.axis_index("x") + lax.axis_index("y")
    conv_cols = ml_conv_w.shape[-1]
    gsm["ml_conv_w"] = lax.dynamic_slice(gsm["ml_conv_w"], (0, place * conv_cols), (CONV_K, conv_cols))
    own_shapes = [wts[k].shape for k in _SMALL]
    g_pack = _pack([gsm[k] for k in _SMALL])
    res = _adamw(g_pack[None], _pack([wts[k] for k in _SMALL]), _pack([mom[k] for k in _SMALL]), _pack([var[k] for k in _SMALL]), "adamw_small")
    for dst, pack in zip((out_g, out_d, out_m, out_v), res):
        for k, a in zip(_SMALL, _unpack(pack, own_shapes)):
            dst[k] = a

    outs = [loss_total, grad_x[None]]
    for group in (out_g, out_d, out_m, out_v):
        outs += [group[k] for k in _ORDER]
    return tuple(outs)
```

```python
import functools

import jax
import jax.numpy as jnp
from jax import lax
from jax.experimental import pallas as pl
from jax.experimental.pallas import tpu as pltpu

f32 = jnp.float32
bf16 = jnp.bfloat16
HI = lax.Precision.HIGHEST

D_MODEL = 1024
HEADS = 4
HEAD_W = 128
MIX_W = HEADS * HEAD_W
ML_DQK = 64
PROJ_W = 3592
U_HG = 4 * MIX_W
U_ML = 3 * MIX_W + 128
D_FF = 2816
PLE = 256
CHUNK = 128
SUB = 16
EXP_CAP = 80.0
CONV_K = 4
HALO = 8
ALPHA = float(2.0 ** 0.25)
LN_EPS = 1e-5
RMS_EPS = 1e-6
NEG = -1e30
LR, B1, B2, EPS_ADAM, WD, STEP = 0.001, 0.9, 0.999, 1e-08, 0.01, 10
VMEM_LIMIT = 56 * 1024 * 1024
DENSE_ROWS = 512
WGRAD_ROWS = 2048


def _cparams(n_axes, arbitrary=True):
    sem = ("arbitrary",) * n_axes if arbitrary else ("parallel",) * n_axes
    return pltpu.CompilerParams(dimension_semantics=sem, vmem_limit_bytes=VMEM_LIMIT)


ACT = bf16


def _mx(a):
    return a.astype(ACT)


def _bdot(a, b):
    return jnp.dot(_mx(a), _mx(b), preferred_element_type=f32)


def _bdot_nt(a, b):
    return lax.dot_general(_mx(a), _mx(b), (((1,), (1,)), ((), ())), preferred_element_type=f32)


def _bdot_tn(a, b):
    return lax.dot_general(_mx(a), _mx(b), (((0,), (0,)), ((), ())), preferred_element_type=f32)


def _split3(x):
    hi = x.astype(bf16)
    r1 = x - hi.astype(f32)
    mid = r1.astype(bf16)
    lo = (r1 - mid.astype(f32)).astype(bf16)
    return hi, mid, lo


def _dot3(a, b, dims):
    a_hi = a.astype(bf16)
    a_lo = (a - a_hi.astype(f32)).astype(bf16)
    b_hi = b.astype(bf16)
    b_lo = (b - b_hi.astype(f32)).astype(bf16)
    dn = (dims, ((), ()))
    return (lax.dot_general(a_hi, b_hi, dn, preferred_element_type=f32) + lax.dot_general(a_hi, b_lo, dn, preferred_element_type=f32)
            + lax.dot_general(a_lo, b_hi, dn, preferred_element_type=f32))


def _sel_dot(sel, x):
    sb = sel.astype(bf16)
    return sum(jnp.dot(sb, part, preferred_element_type=f32) for part in _split3(x))


def _sel_dot_nt(sel, x):
    sb = sel.astype(bf16)
    return sum(lax.dot_general(sb, part, (((1,), (1,)), ((), ())), preferred_element_type=f32) for part in _split3(x))


def _sigmoid(x):
    return 1.0 / (1.0 + jnp.exp(-x))


def _log_sigmoid(x):
    return jnp.minimum(x, 0.0) - jnp.log(1.0 + jnp.exp(-jnp.abs(x)))


def _tri(n, upper=False):
    r = lax.broadcasted_iota(jnp.int32, (n, n), 0)
    c = lax.broadcasted_iota(jnp.int32, (n, n), 1)
    return (c >= r) if upper else (c <= r)


def _rows(tm, n, col=0):
    return pl.BlockSpec((tm, n), lambda i, _c=col: (i, _c))


def _rows_rev(tm, n, nb, col=0):
    return pl.BlockSpec((tm, n), lambda i, _c=col, _nb=nb: (_nb - 1 - i, _c))


def _const(shape):
    return pl.BlockSpec(shape, lambda i, _n=len(shape): (0,) * _n)


def _resident(shape):
    return pl.BlockSpec(shape, lambda i, _n=len(shape): (0,) * _n, pipeline_mode=pl.Buffered(1))


def _tile(t, want):
    return want if t % want == 0 else t


def _inproj(x, w_hg, w_ml, b_hg, b_ml, riders=()):
    t = x.shape[0]
    tm = _tile(t, DENSE_ROWS)

    def body(x_ref, whg_ref, wml_ref, bhg_ref, bml_ref, uhg_ref, uml_ref, xb_ref):
        xb = _mx(x_ref[...])
        xb_ref[...] = xb
        uhg_ref[...] = _bdot_nt(xb, whg_ref[...]) + bhg_ref[...]
        uml_ref[...] = _bdot_nt(xb, wml_ref[...]) + bml_ref[...]

    return _riding_call(
        body, "inproj", t // tm,
        in_specs=[_rows(tm, D_MODEL), _resident((U_HG, D_MODEL)), _resident((U_ML, D_MODEL)), _const((1, U_HG)), _const((1, U_ML))],
        out_specs=[_rows(tm, U_HG), _rows(tm, U_ML), _rows(tm, D_MODEL)],
        out_shape=[jax.ShapeDtypeStruct((t, U_HG), f32), jax.ShapeDtypeStruct((t, U_ML), f32), jax.ShapeDtypeStruct((t, D_MODEL), ACT)],
        scratch_shapes=[], operands=(x, w_hg, w_ml, b_hg, b_ml), riders=riders, copies=_gather_copies, ride_shapes=_gather_shapes(riders))


def _hg_gates(hq, hf, lb, tri):
    s = _sigmoid(hf)
    om = 1.0 - lb
    f = lb + om * s
    g = jnp.log(f)
    k = om * (1.0 - s)
    sq = _sigmoid(hq)
    q = hq * sq
    b = _sel_dot(tri, g)
    return q, sq, s, f, k, b


def _hg_scores(q, k, b, tril_mask):
    qts, kts, eqs, eks, rows = [], [], [], [], []
    for i in range(CHUNK // SUB):
        lo = i * SUB
        ref = jnp.zeros_like(b[0:1]) if i == 0 else b[lo - 1:lo]
        eq = jnp.exp(b[lo:lo + SUB] - ref)
        ek = jnp.exp(jnp.minimum(ref - b, EXP_CAP))
        qt = q[lo:lo + SUB] * eq
        kt = k * ek
        rows.append(_bdot_nt(qt, kt))
        qts.append(qt); kts.append(kt); eqs.append(eq); eks.append(ek)
    a = jnp.where(tril_mask, jnp.concatenate(rows, axis=0), 0.0)
    return a, qts, kts, eqs, eks


def _head_rms(o, gn):
    rstd = lax.rsqrt(jnp.mean(o * o, axis=-1, keepdims=True) + RMS_EPS)
    oh = o * rstd
    return oh, rstd, oh * gn


def _lower_bound(logit_ref):
    lg = logit_ref[...]
    return _sigmoid(lg[0:1] - lg[1:2])


def _hgrn2_fwd(u_hg, logits, gn, riders=()):
    t = u_hg.shape[0]
    tb = _tile(t, 256)
    nc_blk = tb // CHUNK

    def body(u_ref, lg_ref, gn_ref, og_ref, sst_ref, st_ref):
        @pl.when(pl.program_id(0) == 0)
        def _():
            st_ref[...] = jnp.zeros_like(st_ref)

        lb_all = _lower_bound(lg_ref)
        tril_mask = _tri(CHUNK)
        tri = tril_mask.astype(f32)

        def chunk(c, carry):
            r0 = pl.multiple_of(c * CHUNK, CHUNK)
            rows = pl.ds(r0, CHUNK)
            heads = range(HEADS)
            cols = [slice(h * HEAD_W, (h + 1) * HEAD_W) for h in heads]
            hv = [u_ref[rows, 2 * MIX_W + h * HEAD_W:2 * MIX_W + (h + 1) * HEAD_W] for h in heads]
            gts = [_hg_gates(u_ref[rows, h * HEAD_W:(h + 1) * HEAD_W], u_ref[rows, MIX_W + h * HEAD_W:MIX_W + (h + 1) * HEAD_W],
                             lb_all[:, cols[h]], tri) for h in heads]
            q = [g[0] for g in gts]
            k = [g[4] for g in gts]
            b = [g[5] for g in gts]
            a = [_hg_scores(q[h], k[h], b[h], tril_mask)[0] for h in heads]
            st = [st_ref[h] for h in heads]
            bl = [b[h][CHUNK - 1:CHUNK] for h in heads]
            o = [_bdot(a[h], hv[h]) + _bdot_nt(q[h] * jnp.exp(b[h]), st[h]) for h in heads]
            new_st = [st[h] * jnp.exp(bl[h]) + _bdot_tn(hv[h], k[h] * jnp.exp(bl[h] - b[h])) for h in heads]
            for h in heads:
                sst_ref[c, h] = st[h]
                st_ref[h] = new_st[h]
                hgate = u_ref[rows, 3 * MIX_W + h * HEAD_W:3 * MIX_W + (h + 1) * HEAD_W]
                _, _, y = _head_rms(o[h], gn_ref[:, cols[h]])
                og_ref[rows, cols[h]] = (y * (hgate * _sigmoid(hgate))).astype(ACT)
            return carry

        lax.fori_loop(0, nc_blk, chunk, 0)

    return _riding_call(
        body, "hgrn2_fwd", t // tb,
        in_specs=[_rows(tb, U_HG), _const((2, MIX_W)), _const((1, MIX_W))],
        out_specs=[_rows(tb, MIX_W), pl.BlockSpec((nc_blk, HEADS, HEAD_W, HEAD_W), lambda i: (i, 0, 0, 0))],
        out_shape=[jax.ShapeDtypeStruct((t, MIX_W), ACT), jax.ShapeDtypeStruct((t // CHUNK, HEADS, HEAD_W, HEAD_W), f32)],
        scratch_shapes=[pltpu.VMEM((HEADS, HEAD_W, HEAD_W), f32)],
        operands=(u_hg, logits, gn), riders=riders, copies=_gather_copies, ride_shapes=_gather_shapes(riders))


def _hgrn2_bwd(u_hg, logits, gn, sst, dog, riders=()):
    t = u_hg.shape[0]
    tb = _tile(t, 256)
    nb = t // tb
    nc_blk = tb // CHUNK
    nr = len(riders)

    def body(*refs):
        u_ref, lg_ref, gn_ref, sst_ref, dog_ref = refs[:5]
        ride_in = refs[5:5 + nr]
        du_ref, dlg_ref, dgn_ref = refs[5 + nr:8 + nr]
        ride_out = refs[8 + nr:8 + 2 * nr]
        dst_ref = refs[8 + 2 * nr]
        ride_sems = refs[9 + 2 * nr:]

        @pl.when(pl.program_id(0) == 0)
        def _():
            dst_ref[...] = jnp.zeros_like(dst_ref)
            dlg_ref[...] = jnp.zeros_like(dlg_ref)
            dgn_ref[...] = jnp.zeros_like(dgn_ref)
            if nr:
                _scatter_start(ride_in, ride_out, *ride_sems)

        lb_all = _lower_bound(lg_ref)
        tril_mask = _tri(CHUNK)
        tri = tril_mask.astype(f32)
        triu = _tri(CHUNK, upper=True).astype(f32)

        def chunk(j, carry):
            c = nc_blk - 1 - j
            r0 = pl.multiple_of(c * CHUNK, CHUNK)
            rows = pl.ds(r0, CHUNK)
            heads = range(HEADS)
            nsub = CHUNK // SUB
            cols = [slice(h * HEAD_W, (h + 1) * HEAD_W) for h in heads]
            hq = [u_ref[rows, h * HEAD_W:(h + 1) * HEAD_W] for h in heads]
            hf = [u_ref[rows, MIX_W + h * HEAD_W:MIX_W + (h + 1) * HEAD_W] for h in heads]
            hv = [u_ref[rows, 2 * MIX_W + h * HEAD_W:2 * MIX_W + (h + 1) * HEAD_W] for h in heads]
            lb = [lb_all[:, cols[h]] for h in heads]
            gts = [_hg_gates(hq[h], hf[h], lb[h], tri) for h in heads]
            q, sq, s, f, k, b = ([g[n] for g in gts] for n in range(6))
            scs = [_hg_scores(q[h], k[h], b[h], tril_mask) for h in heads]
            a, qts, kts, eqs, eks = ([sc[n] for sc in scs] for n in range(5))
            st = [sst_ref[c, h] for h in heads]
            dst = [dst_ref[h] for h in heads]
            bl = [b[h][CHUNK - 1:CHUNK] for h in heads]
            eb = [jnp.exp(b[h]) for h in heads]
            qh = [q[h] * eb[h] for h in heads]
            ekl = [jnp.exp(bl[h] - b[h]) for h in heads]
            kh = [k[h] * ekl[h] for h in heads]
            o = [_bdot(a[h], hv[h]) + _bdot_nt(qh[h], st[h]) for h in heads]
            do = []
            for h in heads:
                hgate = u_ref[rows, 3 * MIX_W + h * HEAD_W:3 * MIX_W + (h + 1) * HEAD_W]
                gnh = gn_ref[:, cols[h]]
                oh, rstd, y = _head_rms(o[h], gnh)
                sg = _sigmoid(hgate)
                dogh = dog_ref[rows, cols[h]]
                dy = dogh * (hgate * sg)
                du_ref[rows, 3 * MIX_W + h * HEAD_W:3 * MIX_W + (h + 1) * HEAD_W] = (dogh * y * (sg * (1.0 + hgate * (1.0 - sg)))).astype(ACT)
                dgn_ref[:, cols[h]] += jnp.sum(dy * oh, axis=0, keepdims=True)
                doh = dy * gnh
                do.append(rstd * (doh - oh * jnp.mean(doh * oh, axis=-1, keepdims=True)))
            da = [jnp.where(tril_mask, _bdot_nt(do[h], hv[h]), 0.0) for h in heads]
            dv = [_bdot_tn(a[h], do[h]) + _bdot_nt(kh[h], dst[h]) for h in heads]
            dq = [_bdot(do[h], st[h]) * eb[h] for h in heads]
            dk = [_bdot(hv[h], dst[h]) * ekl[h] for h in heads]
            d_last = [jnp.sum(k[h] * dk[h], axis=0, keepdims=True) + jnp.exp(bl[h]) * jnp.sum(dst[h] * st[h], axis=0, keepdims=True)
                      for h in heads]
            dqs = [[] for _ in heads]
            for i in range(nsub):
                for h in heads:
                    da_i = da[h][i * SUB:(i + 1) * SUB]
                    dqs[h].append(_dot3(da_i, kts[h][i], ((1,), (0,))) * eqs[h][i])
                    dk[h] = dk[h] + _dot3(da_i, qts[h][i], ((0,), (0,))) * eks[h][i]
            for h in heads:
                dq[h] = dq[h] + jnp.concatenate(dqs[h], axis=0)
                dst_ref[h] = dst[h] * jnp.exp(bl[h]) + _bdot_tn(do[h], qh[h])
            dg = [_sel_dot(triu, q[h] * dq[h] - k[h] * dk[h]) + d_last[h] for h in heads]
            for h in heads:
                dfk = dg[h] / f[h] - dk[h]
                du_ref[rows, h * HEAD_W:(h + 1) * HEAD_W] = (dq[h] * (sq[h] * (1.0 + hq[h] * (1.0 - sq[h])))).astype(ACT)
                du_ref[rows, MIX_W + h * HEAD_W:MIX_W + (h + 1) * HEAD_W] = ((1.0 - lb[h]) * dfk * s[h] * (1.0 - s[h])).astype(ACT)
                du_ref[rows, 2 * MIX_W + h * HEAD_W:2 * MIX_W + (h + 1) * HEAD_W] = dv[h].astype(ACT)
                dlb = jnp.sum((1.0 - s[h]) * dfk, axis=0, keepdims=True) * (lb[h] * (1.0 - lb[h]))
                dlg_ref[0:1, cols[h]] += dlb
                dlg_ref[1:2, cols[h]] -= dlb
            return carry

        lax.fori_loop(0, nc_blk, chunk, 0)

        if nr:
            @pl.when(pl.program_id(0) == nb - 1)
            def _():
                _scatter_wait(ride_in, ride_out, *ride_sems)

    hbm = pl.BlockSpec(memory_space=pltpu.HBM)
    ride_scratch = [pltpu.SemaphoreType.DMA((3 * nr,)), pltpu.SemaphoreType.DMA((3 * nr,)), pltpu.SemaphoreType.DMA((nr,))] if nr else []
    return pl.pallas_call(
        body, name="hgrn2_bwd", grid=(nb,),
        in_specs=[_rows_rev(tb, U_HG, nb), _const((2, MIX_W)), _const((1, MIX_W)),
                  pl.BlockSpec((nc_blk, HEADS, HEAD_W, HEAD_W), lambda i: (nb - 1 - i, 0, 0, 0)), _rows_rev(tb, MIX_W, nb)] + [hbm] * nr,
        out_specs=[_rows_rev(tb, U_HG, nb), _const((2, MIX_W)), _const((1, MIX_W))] + [hbm] * nr,
        out_shape=[jax.ShapeDtypeStruct((t, U_HG), ACT), jax.ShapeDtypeStruct((2, MIX_W), f32), jax.ShapeDtypeStruct((1, MIX_W), f32)]
        + [jax.ShapeDtypeStruct(r.shape, r.dtype) for r in riders],
        scratch_shapes=[pltpu.VMEM((HEADS, HEAD_W, HEAD_W), f32)] + ride_scratch,
        compiler_params=_cparams(1),
    )(u_hg, logits, gn, sst, dog, *riders)


def _conv_fwd(u_ml, w, b):
    t = u_ml.shape[0]
    tm = _tile(t, 512)

    def body(x_ref, w_ref, b_ref, pre_ref, act_ref, xbuf):
        @pl.when(pl.program_id(0) == 0)
        def _():
            xbuf[...] = jnp.zeros_like(xbuf)

        xbuf[0:HALO, :] = xbuf[tm:tm + HALO, :]
        xbuf[HALO:HALO + tm, :] = x_ref[...]
        pre = b_ref[...] + jnp.zeros((tm, MIX_W), f32)
        for kk in range(CONV_K):
            off = HALO - (CONV_K - 1) + kk
            pre = pre + w_ref[kk:kk + 1, :] * xbuf[off:off + tm, :]
        pre_ref[...] = pre
        act_ref[...] = pre * _sigmoid(pre)

    return pl.pallas_call(
        body, name="conv_fwd", grid=(t // tm,),
        in_specs=[_rows(tm, MIX_W), _const((CONV_K, MIX_W)), _const((1, MIX_W))],
        out_specs=[_rows(tm, MIX_W), _rows(tm, MIX_W)],
        out_shape=[jax.ShapeDtypeStruct((t, MIX_W), f32)] * 2,
        scratch_shapes=[pltpu.VMEM((tm + HALO, MIX_W), f32)],
        compiler_params=_cparams(1),
    )(u_ml, w, b)


def _conv_bwd(u_ml, w, pre, dact):
    t = u_ml.shape[0]
    tm = _tile(t, 512)
    nb = t // tm
    hb = tm // HALO

    def body(x_ref, halo_ref, w_ref, pre_ref, dact_ref, dx_ref, dw_ref, db_ref, dbuf, xbuf):
        i = pl.program_id(0)

        @pl.when(i == 0)
        def _():
            dbuf[...] = jnp.zeros_like(dbuf)
            dw_ref[...] = jnp.zeros_like(dw_ref)
            db_ref[...] = jnp.zeros_like(db_ref)

        p = pre_ref[...]
        sg = _sigmoid(p)
        dpre = dact_ref[...] * (sg * (1.0 + p * (1.0 - sg)))
        dbuf[tm:tm + HALO, :] = dbuf[0:HALO, :]
        dbuf[0:tm, :] = dpre
        has_prev = (i < nb - 1).astype(f32)
        xbuf[0:HALO, :] = halo_ref[...] * has_prev
        xbuf[HALO:HALO + tm, :] = x_ref[...]
        dx = jnp.zeros((tm, MIX_W), f32)
        for kk in range(CONV_K):
            back = CONV_K - 1 - kk
            dx = dx + w_ref[kk:kk + 1, :] * dbuf[back:back + tm, :]
            off = HALO - (CONV_K - 1) + kk
            dw_ref[kk:kk + 1, :] += jnp.sum(dpre * xbuf[off:off + tm, :], axis=0, keepdims=True)
        dx_ref[...] = dx.astype(ACT)
        db_ref[...] += jnp.sum(dpre, axis=0, keepdims=True)

    return pl.pallas_call(
        body, name="conv_bwd", grid=(nb,),
        in_specs=[_rows_rev(tm, MIX_W, nb),
                  pl.BlockSpec((HALO, MIX_W), lambda i: (jnp.maximum((nb - 1 - i) * hb - 1, 0), 0)),
                  _const((CONV_K, MIX_W)), _rows_rev(tm, MIX_W, nb), _rows_rev(tm, MIX_W, nb)],
        out_specs=[_rows_rev(tm, MIX_W, nb), _const((CONV_K, MIX_W)), _const((1, MIX_W))],
        out_shape=[jax.ShapeDtypeStruct((t, MIX_W), ACT), jax.ShapeDtypeStruct((CONV_K, MIX_W), f32), jax.ShapeDtypeStruct((1, MIX_W), f32)],
        scratch_shapes=[pltpu.VMEM((tm + HALO, MIX_W), f32), pltpu.VMEM((tm + HALO, MIX_W), f32)],
        compiler_params=_cparams(1),
    )(u_ml, u_ml, w, pre, dact)


def _lane_pick(x, lane):
    idx = lax.broadcasted_iota(jnp.int32, x.shape, 1)
    return jnp.sum(jnp.where(idx == lane, x, 0.0), axis=-1, keepdims=True)


def _ml_gate_forms(gates, tri):
    lf = _log_sigmoid(gates)
    gc = _sel_dot(tri, lf)
    lane = lax.broadcasted_iota(jnp.int32, gates.shape, 1)
    mixed = jnp.where(lane < HEADS, gates, gc)
    sel = (lax.broadcasted_iota(jnp.int32, (8, 128), 0) == lax.broadcasted_iota(jnp.int32, (8, 128), 1)).astype(f32)
    rowsf = _sel_dot_nt(sel, mixed)
    return gc, rowsf


def _ml_chunk(q, k, v, gates, gc, rowsf, c_st, n_st, m_st, tril_mask):
    hs = range(HEADS)
    g_col = [_lane_pick(gc, HEADS + h) for h in hs]
    ig_col = [_lane_pick(gates, h) for h in hs]
    dmat = [jnp.where(tril_mask, g_col[h] - rowsf[HEADS + h:HEADS + h + 1, :] + rowsf[h:h + 1, :], NEG) for h in hs]
    m_inter = [g_col[h] + m_st[h] for h in hs]
    m_t = [jnp.maximum(m_inter[h], jnp.max(dmat[h], axis=-1, keepdims=True)) for h in hs]
    wi = [jnp.exp(dmat[h] - m_t[h]) for h in hs]
    wo = [jnp.exp(m_inter[h] - m_t[h]) for h in hs]
    qk = [_bdot_nt(q[h], k[h]) * wi[h] for h in hs]
    num = [_bdot(qk[h], v[h]) + wo[h] * _bdot(q[h], c_st[h]) for h in hs]
    den = [jnp.sum(qk[h], axis=-1, keepdims=True) + wo[h] * jnp.sum(q[h] * n_st[h], axis=-1, keepdims=True) for h in hs]
    floor = [jnp.exp(-m_t[h]) for h in hs]
    z = [jnp.maximum(jnp.abs(den[h]), floor[h]) for h in hs]
    g_last = [g_col[h][CHUNK - 1:CHUNK] for h in hs]
    a_col = [g_last[h] - g_col[h] + ig_col[h] for h in hs]
    m_new = [jnp.maximum(g_last[h] + m_st[h], jnp.max(a_col[h], axis=0, keepdims=True)) for h in hs]
    ws = [jnp.exp(a_col[h] - m_new[h]) for h in hs]
    w_old = [jnp.exp(g_last[h] + m_st[h] - m_new[h]) for h in hs]
    return dict(wi=wi, wo=wo, qk=qk, num=num, den=den, z=z, floor=floor, ws=ws, w_old=w_old, m_new=m_new)


def _mlstm_fwd(qkc, u_ml, gn, riders=()):
    t = qkc.shape[0]
    tb = _tile(t, 256)
    nc_blk = tb // CHUNK

    def body(qk_ref, v_ref, mo_ref, gt_ref, gn_ref, og_ref, cst_ref, nst_ref, mst_ref, c_sc, n_sc, m_sc):
        @pl.when(pl.program_id(0) == 0)
        def _():
            c_sc[...] = jnp.zeros_like(c_sc)
            n_sc[...] = jnp.zeros_like(n_sc)
            m_sc[...] = jnp.zeros_like(m_sc)

        tril_mask = _tri(CHUNK)
        tri = tril_mask.astype(f32)

        def chunk(c, carry):
            r0 = pl.multiple_of(c * CHUNK, CHUNK)
            rows = pl.ds(r0, CHUNK)
            gates = gt_ref[rows, :]
            gc, rowsf = _ml_gate_forms(gates, tri)
            hs = range(HEADS)
            q = [qk_ref[rows, h * ML_DQK:(h + 1) * ML_DQK] * (ML_DQK ** -0.5) for h in hs]
            k = [qk_ref[rows, HEADS * ML_DQK + h * ML_DQK:HEADS * ML_DQK + (h + 1) * ML_DQK] for h in hs]
            v = [v_ref[rows, h * HEAD_W:(h + 1) * HEAD_W] for h in hs]
            c_st = [c_sc[h] for h in hs]
            n_st = [n_sc[h] for h in hs]
            m_full = [m_sc[h] for h in hs]
            r = _ml_chunk(q, k, v, gates, gc, rowsf, c_st, n_st, [m[:, 0:1] for m in m_full], tril_mask)
            ksc = [k[h] * r["ws"][h] for h in hs]
            new_c = [r["w_old"][h] * c_st[h] + _bdot_tn(ksc[h], v[h]) for h in hs]
            for h in hs:
                cs = slice(h * HEAD_W, (h + 1) * HEAD_W)
                cst_ref[c, h] = c_st[h]
                nst_ref[c, h] = n_st[h]
                mst_ref[c, h] = m_full[h]
                c_sc[h] = new_c[h]
                n_sc[h] = r["w_old"][h] * n_st[h] + jnp.sum(ksc[h], axis=0, keepdims=True)
                m_sc[h] = r["m_new"][h] + jnp.zeros((1, 128), f32)
                _, _, y = _head_rms(r["num"][h] / r["z"][h], gn_ref[:, cs])
                og_ref[rows, cs] = (y * _sigmoid(mo_ref[rows, h * HEAD_W:(h + 1) * HEAD_W])).astype(ACT)
            return carry

        lax.fori_loop(0, nc_blk, chunk, 0)

    nchunks = t // CHUNK
    return _riding_call(
        body, "mlstm_fwd", t // tb,
        in_specs=[_rows(tb, MIX_W), _rows(tb, MIX_W, 1), _rows(tb, MIX_W, 2), _rows(tb, 128, 12), _const((1, MIX_W))],
        out_specs=[_rows(tb, MIX_W),
                   pl.BlockSpec((nc_blk, HEADS, ML_DQK, HEAD_W), lambda i: (i, 0, 0, 0)),
                   pl.BlockSpec((nc_blk, HEADS, 1, ML_DQK), lambda i: (i, 0, 0, 0)),
                   pl.BlockSpec((nc_blk, HEADS, 1, 128), lambda i: (i, 0, 0, 0))],
        out_shape=[jax.ShapeDtypeStruct((t, MIX_W), ACT),
                   jax.ShapeDtypeStruct((nchunks, HEADS, ML_DQK, HEAD_W), f32),
                   jax.ShapeDtypeStruct((nchunks, HEADS, 1, ML_DQK), f32),
                   jax.ShapeDtypeStruct((nchunks, HEADS, 1, 128), f32)],
        scratch_shapes=[pltpu.VMEM((HEADS, ML_DQK, HEAD_W), f32), pltpu.VMEM((HEADS, 1, ML_DQK), f32), pltpu.VMEM((HEADS, 1, 128), f32)],
        operands=(qkc, u_ml, u_ml, u_ml, gn), riders=riders, copies=_gather_copies, ride_shapes=_gather_shapes(riders))


def _mlstm_bwd(qkc, u_ml, gn, cst, nst, mst, dog):
    t = qkc.shape[0]
    tb = _tile(t, 256)
    nb = t // tb
    nc_blk = tb // CHUNK

    def body(qk_ref, v_ref, mo_ref, gt_ref, gn_ref, cst_ref, nst_ref, mst_ref, dog_ref,
             dqk_ref, dv_ref, dmo_ref, dgt_ref, dgn_ref, dc_sc, dn_sc):
        @pl.when(pl.program_id(0) == 0)
        def _():
            dc_sc[...] = jnp.zeros_like(dc_sc)
            dn_sc[...] = jnp.zeros_like(dn_sc)
            dgn_ref[...] = jnp.zeros_like(dgn_ref)

        tril_mask = _tri(CHUNK)
        tri = tril_mask.astype(f32)
        triu = _tri(CHUNK, upper=True).astype(f32)
        lane = lax.broadcasted_iota(jnp.int32, (CHUNK, 128), 1)

        def chunk(j, carry):
            c = nc_blk - 1 - j
            r0 = pl.multiple_of(c * CHUNK, CHUNK)
            rows = pl.ds(r0, CHUNK)
            gates = gt_ref[rows, :]
            gc, rowsf = _ml_gate_forms(gates, tri)
            dg_mat = jnp.zeros((CHUNK, 128), f32)
            dig_mat = jnp.zeros((CHUNK, 128), f32)
            dlast_row = jnp.zeros((1, 128), f32)
            hs = range(HEADS)
            cols = [slice(h * HEAD_W, (h + 1) * HEAD_W) for h in hs]
            q = [qk_ref[rows, h * ML_DQK:(h + 1) * ML_DQK] * (ML_DQK ** -0.5) for h in hs]
            k = [qk_ref[rows, HEADS * ML_DQK + h * ML_DQK:HEADS * ML_DQK + (h + 1) * ML_DQK] for h in hs]
            v = [v_ref[rows, h * HEAD_W:(h + 1) * HEAD_W] for h in hs]
            c_st = [cst_ref[c, h] for h in hs]
            n_st = [nst_ref[c, h] for h in hs]
            m_st = [mst_ref[c, h][:, 0:1] for h in hs]
            dc = [dc_sc[h] for h in hs]
            dn = [dn_sc[h] for h in hs]
            r = _ml_chunk(q, k, v, gates, gc, rowsf, c_st, n_st, m_st, tril_mask)
            z, wi, wo, ws, w_old, den = r["z"], r["wi"], r["wo"], r["ws"], r["w_old"], r["den"]
            hh = [r["num"][h] / z[h] for h in hs]
            dh = []
            for h in hs:
                gnh = gn_ref[:, cols[h]]
                oh, rstd, y = _head_rms(hh[h], gnh)
                sg = _sigmoid(mo_ref[rows, h * HEAD_W:(h + 1) * HEAD_W])
                dogh = dog_ref[rows, cols[h]]
                dy = dogh * sg
                dmo_ref[rows, cols[h]] = (dogh * y * (sg * (1.0 - sg))).astype(ACT)
                dgn_ref[:, cols[h]] += jnp.sum(dy * oh, axis=0, keepdims=True)
                doh = dy * gnh
                dh.append(rstd * (doh - oh * jnp.mean(doh * oh, axis=-1, keepdims=True)))
            dnum = [dh[h] / z[h] for h in hs]
            dz = [-jnp.sum(dh[h] * hh[h], axis=-1, keepdims=True) / z[h] for h in hs]
            dden = [jnp.where(jnp.abs(den[h]) > r["floor"][h], dz[h] * jnp.sign(den[h]), 0.0) for h in hs]
            dsw = [(_bdot_nt(dnum[h], v[h]) + dden[h]) * wi[h] for h in hs]
            dq = [_bdot(dsw[h], k[h]) + wo[h] * (_bdot_nt(dnum[h], c_st[h]) + dden[h] * n_st[h]) for h in hs]
            dk_state = [ws[h] * (_bdot_nt(v[h], dc[h]) + dn[h]) for h in hs]
            dk = [_bdot_tn(dsw[h], q[h]) + dk_state[h] for h in hs]
            dv = [_bdot_tn(r["qk"][h], dnum[h]) + ws[h] * _bdot(k[h], dc[h]) for h in hs]
            woq = [wo[h] * q[h] for h in hs]
            new_dc = [w_old[h] * dc[h] + _bdot_tn(woq[h], dnum[h]) for h in hs]
            for h in hs:
                dv_ref[rows, cols[h]] = dv[h].astype(ACT)
                dc_sc[h] = new_dc[h]
                dn_sc[h] = w_old[h] * dn[h] + jnp.sum(woq[h] * dden[h], axis=0, keepdims=True)
                d_last = (jnp.sum(jnp.sum(k[h] * dk_state[h], axis=-1, keepdims=True), axis=0, keepdims=True)
                          + w_old[h] * (jnp.sum(jnp.sum(dc[h] * c_st[h], axis=-1, keepdims=True), axis=0, keepdims=True)
                                        + jnp.sum(dn[h] * n_st[h], axis=-1, keepdims=True)))
                kdk = jnp.sum(k[h] * dk[h], axis=-1, keepdims=True)
                qdq = jnp.sum(q[h] * dq[h], axis=-1, keepdims=True)
                dg_mat = dg_mat + jnp.where(lane == HEADS + h, qdq - kdk, 0.0)
                dlast_row = dlast_row + jnp.where(lane[0:1] == HEADS + h, d_last, 0.0)
                dig_mat = dig_mat + jnp.where(lane == h, kdk, 0.0)
                dqk_ref[rows, h * ML_DQK:(h + 1) * ML_DQK] = dq[h] * (ML_DQK ** -0.5)
                dqk_ref[rows, HEADS * ML_DQK + h * ML_DQK:HEADS * ML_DQK + (h + 1) * ML_DQK] = dk[h]
            dlf = _sel_dot(triu, dg_mat) + dlast_row
            dgt_ref[rows, :] = (dig_mat + dlf * _sigmoid(-gates)).astype(ACT)
            return carry

        lax.fori_loop(0, nc_blk, chunk, 0)

    st4 = lambda a, b: pl.BlockSpec((nc_blk, HEADS, a, b), lambda i: (nb - 1 - i, 0, 0, 0))
    return pl.pallas_call(
        body, name="mlstm_bwd", grid=(nb,),
        in_specs=[_rows_rev(tb, MIX_W, nb), _rows_rev(tb, MIX_W, nb, 1), _rows_rev(tb, MIX_W, nb, 2), _rows_rev(tb, 128, nb, 12),
                  _const((1, MIX_W)), st4(ML_DQK, HEAD_W), st4(1, ML_DQK), st4(1, 128), _rows_rev(tb, MIX_W, nb)],
        out_specs=[_rows_rev(tb, MIX_W, nb), _rows_rev(tb, MIX_W, nb), _rows_rev(tb, MIX_W, nb), _rows_rev(tb, 128, nb), _const((1, MIX_W))],
        out_shape=[jax.ShapeDtypeStruct((t, MIX_W), f32), jax.ShapeDtypeStruct((t, MIX_W), ACT), jax.ShapeDtypeStruct((t, MIX_W), ACT),
                   jax.ShapeDtypeStruct((t, 128), ACT), jax.ShapeDtypeStruct((1, MIX_W), f32)],
        scratch_shapes=[pltpu.VMEM((HEADS, ML_DQK, HEAD_W), f32), pltpu.VMEM((HEADS, 1, ML_DQK), f32)],
        compiler_params=_cparams(1),
    )(qkc, u_ml, u_ml, u_ml, gn, cst, nst, mst, dog)


def _ln_fwd(r, g, b):
    mu = jnp.mean(r, axis=-1, keepdims=True)
    xc = r - mu
    rstd = lax.rsqrt(jnp.mean(xc * xc, axis=-1, keepdims=True) + LN_EPS)
    xh = xc * rstd
    return xh * g + b, xh, rstd


def _ln_bwd(dy, xh, rstd, g):
    dxh = dy * g
    return rstd * (dxh - jnp.mean(dxh, axis=-1, keepdims=True) - xh * jnp.mean(dxh * xh, axis=-1, keepdims=True))


def _outproj_ln1(og_hg, og_ml, x, w_out, g, b):
    t = x.shape[0]
    tm = _tile(t, DENSE_ROWS)

    def body(a_ref, b_ref, x_ref, w_ref, g_ref, bb_ref, x1_ref, xh_ref, rs_ref, x1b_ref):
        mix = _bdot(a_ref[...], w_ref[0:MIX_W, :]) + _bdot(b_ref[...], w_ref[MIX_W:2 * MIX_W, :])
        y, xh, rstd = _ln_fwd(ALPHA * x_ref[...] + mix, g_ref[...], bb_ref[...])
        x1_ref[...] = y
        x1b_ref[...] = y.astype(ACT)
        xh_ref[...] = xh.astype(ACT)
        rs_ref[...] = rstd

    return pl.pallas_call(
        body, name="outproj_ln1", grid=(t // tm,),
        in_specs=[_rows(tm, MIX_W), _rows(tm, MIX_W), _rows(tm, D_MODEL), _resident((D_MODEL, D_MODEL)), _const((1, D_MODEL)), _const((1, D_MODEL))],
        out_specs=[_rows(tm, D_MODEL), _rows(tm, D_MODEL), _rows(tm, 1), _rows(tm, D_MODEL)],
        out_shape=[jax.ShapeDtypeStruct((t, D_MODEL), f32), jax.ShapeDtypeStruct((t, D_MODEL), ACT), jax.ShapeDtypeStruct((t, 1), f32),
                   jax.ShapeDtypeStruct((t, D_MODEL), ACT)],
        compiler_params=_cparams(1, arbitrary=False),
    )(og_hg, og_ml, x, w_out, g, b)


def _ffn_up(x1, wg, wu):
    t = x1.shape[0]
    tm = _tile(t, DENSE_ROWS)

    def body(x_ref, wg_ref, wu_ref, hg_ref, up_ref, a_ref):
        xv = x_ref[...]
        hg = _bdot_nt(xv, wg_ref[...])
        up = _bdot_nt(xv, wu_ref[...])
        hg_ref[...] = hg.astype(ACT)
        up_ref[...] = up.astype(ACT)
        a_ref[...] = (hg * _sigmoid(hg) * up).astype(ACT)

    return pl.pallas_call(
        body, name="ffn_up", grid=(t // tm,),
        in_specs=[_rows(tm, D_MODEL), _resident((D_FF, D_MODEL)), _resident((D_FF, D_MODEL))],
        out_specs=[_rows(tm, D_FF), _rows(tm, D_FF), _rows(tm, D_FF)],
        out_shape=[jax.ShapeDtypeStruct((t, D_FF), ACT), jax.ShapeDtypeStruct((t, D_FF), ACT), jax.ShapeDtypeStruct((t, D_FF), ACT)],
        compiler_params=_cparams(1, arbitrary=False),
    )(x1, wg, wu)


def _ffn_down_ln2(a, x1, wd, g, b):
    t = x1.shape[0]
    tm = _tile(t, DENSE_ROWS)

    def body(a_ref, x_ref, w_ref, g_ref, bb_ref, x2_ref, xh_ref, rs_ref, x2b_ref):
        ffn = _bdot(a_ref[...], w_ref[...])
        y, xh, rstd = _ln_fwd(ALPHA * x_ref[...] + ffn, g_ref[...], bb_ref[...])
        x2_ref[...] = y
        x2b_ref[...] = y.astype(ACT)
        xh_ref[...] = xh.astype(ACT)
        rs_ref[...] = rstd

    return pl.pallas_call(
        body, name="ffn_down_ln2", grid=(t // tm,),
        in_specs=[_rows(tm, D_FF), _rows(tm, D_MODEL), _resident((D_FF, D_MODEL)), _const((1, D_MODEL)), _const((1, D_MODEL))],
        out_specs=[_rows(tm, D_MODEL), _rows(tm, D_MODEL), _rows(tm, 1), _rows(tm, D_MODEL)],
        out_shape=[jax.ShapeDtypeStruct((t, D_MODEL), f32), jax.ShapeDtypeStruct((t, D_MODEL), ACT), jax.ShapeDtypeStruct((t, 1), f32),
                   jax.ShapeDtypeStruct((t, D_MODEL), ACT)],
        compiler_params=_cparams(1, arbitrary=False),
    )(a, x1, wd, g, b)


def _head_loss_bwd(x2, xh2, rs2, p, tgt, w_pg, b_pg, w_pp, g2):
    t = x2.shape[0]
    tm = _tile(t, DENSE_ROWS)

    def body(x_ref, xh_ref, rs_ref, p_ref, t_ref, wg_ref, bg_ref, wp_ref, g_ref,
             dr_ref, de_ref, dz_ref, loss_ref, dbg_ref, dg2_ref, db2_ref):
        @pl.when(pl.program_id(0) == 0)
        def _():
            loss_ref[...] = jnp.zeros_like(loss_ref)
            dbg_ref[...] = jnp.zeros_like(dbg_ref)
            dg2_ref[...] = jnp.zeros_like(dg2_ref)
            db2_ref[...] = jnp.zeros_like(db2_ref)

        x2v = x_ref[...]
        z = _bdot(x2v, wg_ref[...]) + bg_ref[...]
        e = _bdot(p_ref[...], wp_ref[...])
        sg = _sigmoid(z)
        diff = x2v + sg * e - t_ref[...]
        loss_ref[...] += 0.5 * jnp.sum(jnp.mean(diff * diff, axis=-1, keepdims=True), axis=0, keepdims=True)
        dy = diff * (1.0 / D_MODEL)
        de_ref[...] = (dy * sg).astype(ACT)
        dz = dy * e * (sg * (1.0 - sg))
        dz_ref[...] = dz.astype(ACT)
        dbg_ref[...] += jnp.sum(dz, axis=0, keepdims=True)
        dx2 = dy + _bdot_nt(dz, wg_ref[...])
        xh = xh_ref[...].astype(f32)
        dg2_ref[...] += jnp.sum(dx2 * xh, axis=0, keepdims=True)
        db2_ref[...] += jnp.sum(dx2, axis=0, keepdims=True)
        dr_ref[...] = _ln_bwd(dx2, xh, rs_ref[...], g_ref[...])

    row = jax.ShapeDtypeStruct((1, D_MODEL), f32)
    return pl.pallas_call(
        body, name="head_loss_bwd", grid=(t // tm,),
        in_specs=[_rows(tm, D_MODEL), _rows(tm, D_MODEL), _rows(tm, 1), _rows(tm, PLE), _rows(tm, D_MODEL),
                  _resident((D_MODEL, D_MODEL)), _const((1, D_MODEL)), _resident((PLE, D_MODEL)), _const((1, D_MODEL))],
        out_specs=[_rows(tm, D_MODEL), _rows(tm, D_MODEL), _rows(tm, D_MODEL), _const((1, 1)), _const((1, D_MODEL)), _const((1, D_MODEL)), _const((1, D_MODEL))],
        out_shape=[jax.ShapeDtypeStruct((t, D_MODEL), f32), jax.ShapeDtypeStruct((t, D_MODEL), ACT), jax.ShapeDtypeStruct((t, D_MODEL), ACT),
                   jax.ShapeDtypeStruct((1, 1), f32), row, row, row],
        compiler_params=_cparams(1),
    )(x2, xh2, rs2, p, tgt, w_pg, b_pg, w_pp, g2)


def _ffn_bwd(dr2, hg, up, xh1, rs1, wd, wg, wu, g1):
    t = dr2.shape[0]
    tm = _tile(t, DENSE_ROWS // 2)

    def body(dr_ref, hg_ref, up_ref, xh_ref, rs_ref, wd_ref, wg_ref, wu_ref, g_ref,
             dr1_ref, dhg_ref, dup_ref, dg1_ref, db1_ref):
        @pl.when(pl.program_id(0) == 0)
        def _():
            dg1_ref[...] = jnp.zeros_like(dg1_ref)
            db1_ref[...] = jnp.zeros_like(db1_ref)

        dr2v = dr_ref[...]
        da = _bdot_nt(dr2v, wd_ref[...])
        hgv = hg_ref[...].astype(f32)
        sg = _sigmoid(hgv)
        dhg = da * up_ref[...].astype(f32) * (sg * (1.0 + hgv * (1.0 - sg)))
        dup = da * (hgv * sg)
        dhg_ref[...] = dhg.astype(ACT)
        dup_ref[...] = dup.astype(ACT)
        dx1 = ALPHA * dr2v + _bdot(dhg, wg_ref[...]) + _bdot(dup, wu_ref[...])
        xh = xh_ref[...].astype(f32)
        dg1_ref[...] += jnp.sum(dx1 * xh, axis=0, keepdims=True)
        db1_ref[...] += jnp.sum(dx1, axis=0, keepdims=True)
        dr1_ref[...] = _ln_bwd(dx1, xh, rs_ref[...], g_ref[...])

    row = jax.ShapeDtypeStruct((1, D_MODEL), f32)
    return pl.pallas_call(
        body, name="ffn_bwd", grid=(t // tm,),
        in_specs=[_rows(tm, D_MODEL), _rows(tm, D_FF), _rows(tm, D_FF), _rows(tm, D_MODEL), _rows(tm, 1),
                  _resident((D_FF, D_MODEL)), _resident((D_FF, D_MODEL)), _resident((D_FF, D_MODEL)), _const((1, D_MODEL))],
        out_specs=[_rows(tm, D_MODEL), _rows(tm, D_FF), _rows(tm, D_FF), _const((1, D_MODEL)), _const((1, D_MODEL))],
        out_shape=[jax.ShapeDtypeStruct((t, D_MODEL), f32), jax.ShapeDtypeStruct((t, D_FF), ACT), jax.ShapeDtypeStruct((t, D_FF), ACT), row, row],
        compiler_params=_cparams(1),
    )(dr2, hg, up, xh1, rs1, wd, wg, wu, g1)


def _outproj_bwd(dr1, w_out):
    t = dr1.shape[0]
    tm = _tile(t, DENSE_ROWS)

    def body(dr_ref, w_ref, dhg_ref, dml_ref):
        d = _bdot_nt(dr_ref[...], w_ref[...])
        dhg_ref[...] = d[:, 0:MIX_W]
        dml_ref[...] = d[:, MIX_W:2 * MIX_W]

    return pl.pallas_call(
        body, name="outproj_bwd", grid=(t // tm,),
        in_specs=[_rows(tm, D_MODEL), _resident((D_MODEL, D_MODEL))],
        out_specs=[_rows(tm, MIX_W), _rows(tm, MIX_W)],
        out_shape=[jax.ShapeDtypeStruct((t, MIX_W), f32)] * 2,
        compiler_params=_cparams(1, arbitrary=False),
    )(dr1, w_out)


def _inproj_bwd(dr1, du_hg, dqk, dmv, dmo, dgt, w_hg, w_ml):
    t = dr1.shape[0]
    tm = _tile(t, DENSE_ROWS)

    def body(dr_ref, dhg_ref, dqk_ref, dmv_ref, dmo_ref, dgt_ref, whg_ref, wml_ref, gx_ref, dml_ref):
        dml = jnp.concatenate([dqk_ref[...], dmv_ref[...], dmo_ref[...], dgt_ref[...]], axis=-1).astype(ACT)
        dml_ref[...] = dml
        gx_ref[...] = ALPHA * dr_ref[...] + _bdot(dhg_ref[...], whg_ref[...]) + _bdot(dml, wml_ref[...])

    return pl.pallas_call(
        body, name="inproj_bwd", grid=(t // tm,),
        in_specs=[_rows(tm, D_MODEL), _rows(tm, U_HG), _rows(tm, MIX_W), _rows(tm, MIX_W), _rows(tm, MIX_W), _rows(tm, 128),
                  _resident((U_HG, D_MODEL)), _resident((U_ML, D_MODEL))],
        out_specs=[_rows(tm, D_MODEL), _rows(tm, U_ML)],
        out_shape=[jax.ShapeDtypeStruct((t, D_MODEL), f32), jax.ShapeDtypeStruct((t, U_ML), ACT)],
        compiler_params=_cparams(1, arbitrary=False),
    )(dr1, du_hg, dqk, dmv, dmo, dgt, w_hg, w_ml)


def _wgrad(a, b, name, tk=None, tn=None, colsum=False):
    t, kdim = a.shape
    n = b.shape[1]
    tk = tk or kdim
    tn = tn or n
    tt = _tile(t, WGRAD_ROWS)
    assert not colsum or tn == n

    def body(a_ref, b_ref, o_ref, *s_ref):
        @pl.when(pl.program_id(2) == 0)
        def _():
            o_ref[...] = jnp.zeros_like(o_ref)
            if colsum:
                s_ref[0][...] = jnp.zeros_like(s_ref[0])

        av = a_ref[...]
        o_ref[...] += _bdot_tn(av, b_ref[...])
        if colsum:
            s_ref[0][...] += jnp.sum(av.astype(f32), axis=0, keepdims=True)

    out_specs = [pl.BlockSpec((tk, tn), lambda i, j, s: (i, j))]
    out_shape = [jax.ShapeDtypeStruct((kdim, n), f32)]
    if colsum:
        out_specs.append(pl.BlockSpec((1, tk), lambda i, j, s: (0, i)))
        out_shape.append(jax.ShapeDtypeStruct((1, kdim), f32))
    res = pl.pallas_call(
        body, name=name, grid=(kdim // tk, n // tn, t // tt),
        in_specs=[pl.BlockSpec((tt, tk), lambda i, j, s: (s, i)), pl.BlockSpec((tt, tn), lambda i, j, s: (s, j))],
        out_specs=out_specs, out_shape=out_shape,
        compiler_params=_cparams(3),
    )(a, b)
    return res if colsum else res[0]


def _colsum(parts, name):
    t = parts[0].shape[0]
    tt = _tile(t, 512)
    widths = [a.shape[1] for a in parts]

    def body(*refs):
        o_ref = refs[-1]

        @pl.when(pl.program_id(0) == 0)
        def _():
            o_ref[...] = jnp.zeros_like(o_ref)

        off = 0
        for r, w in zip(refs[:-1], widths):
            o_ref[:, off:off + w] += jnp.sum(r[...].astype(f32), axis=0, keepdims=True)
            off += w

    return pl.pallas_call(
        body, name=name, grid=(t // tt,),
        in_specs=[_rows(tt, w) for w in widths],
        out_specs=_const((1, sum(widths))),
        out_shape=jax.ShapeDtypeStruct((1, sum(widths)), f32),
        compiler_params=_cparams(1),
    )(*parts)


_TRANSPOSED = {"w_in", "w_ffn_gate", "w_ffn_up"}
_COL_SPLIT = {"ple_w_proj"}
_RIDE_PLAN = (("w_out", "ple_w_gate", "ple_w_proj"), ("w_ffn_down",), ("w_ffn_gate", "w_ffn_up"))


def _from_chip_major(a, col_split):
    if col_split:
        return a.transpose(1, 0, 2).reshape(a.shape[1], 4 * a.shape[2])
    return a.reshape(4 * a.shape[1], a.shape[2])


def _local_step(x, p, tgt, w_in_b, b_in, logits, conv_w, conv_b, hg_gn, ml_gn, w_out_b, ln1_g, ln1_b,
                wg_b, wu_b, wd_b, ln2_g, ln2_b, w_pp_b, w_pg_b, b_pg, early_hook=None, late_shards=None):
    pad_w = U_HG + U_ML - PROJ_W
    w_hg = w_in_b[:U_HG]
    w_ml = jnp.pad(w_in_b[U_HG:], ((0, pad_w), (0, 0)))
    bb_hg = b_in[:, :U_HG]
    bb_ml = jnp.pad(b_in[:, U_HG:], ((0, 0), (0, pad_w)))

    ride = [[late_shards[k] for k in names] for names in _RIDE_PLAN] if late_shards is not None else [(), (), ()]
    (u_hg, u_ml, xb), got0 = _inproj(x, w_hg, w_ml, bb_hg, bb_ml, ride[0])
    (og_hg, sst), got1 = _hgrn2_fwd(u_hg, logits, hg_gn, ride[1])
    pre, qkc = _conv_fwd(u_ml, conv_w, conv_b)
    (og_ml, cst, nst, mst), got2 = _mlstm_fwd(qkc, u_ml, ml_gn, ride[2])
    if late_shards is not None:
        late = {k: _from_chip_major(g, k in _COL_SPLIT) for names, got in zip(_RIDE_PLAN, (got0, got1, got2)) for k, g in zip(names, got)}
        w_out_b, wg_b, wu_b, wd_b = late["w_out"], late["w_ffn_gate"], late["w_ffn_up"], late["w_ffn_down"]
        w_pp_b, w_pg_b = late["ple_w_proj"], late["ple_w_gate"]
    x1, xh1, rs1, x1b = _outproj_ln1(og_hg, og_ml, x, w_out_b, ln1_g, ln1_b)
    hgp, up, act = _ffn_up(x1b, wg_b, wu_b)
    x2, xh2, rs2, x2b = _ffn_down_ln2(act, x1, wd_b, ln2_g, ln2_b)
    dr2, de, dz, loss, d_bpg, d_ln2g, d_ln2b = _head_loss_bwd(x2, xh2, rs2, p, tgt, w_pg_b, b_pg, w_pp_b, ln2_g)
    dr1, dhg, dup, d_ln1g, d_ln1b = _ffn_bwd(dr2, hgp, up, xh1, rs1, wd_b, wg_b, wu_b, ln1_g)

    d_wo_a = _wgrad(og_hg, dr1, "wgrad_out_hg")
    d_wo_b = _wgrad(og_ml, dr1, "wgrad_out_ml")
    d_w_out = jnp.concatenate([d_wo_a, d_wo_b], axis=0)
    d_wg = _wgrad(dhg, x1b, "wgrad_ffn_gate", tk=D_FF // 2)
    d_wu = _wgrad(dup, x1b, "wgrad_ffn_up", tk=D_FF // 2)
    d_wd = _wgrad(act, dr2, "wgrad_ffn_down", tk=D_FF // 2)
    d_wpp = _wgrad(p, de, "wgrad_ple_proj")
    d_wpg = _wgrad(x2b, dz, "wgrad_ple_gate")
    early = dict(w_out=d_w_out, w_ffn_gate=d_wg, w_ffn_up=d_wu, w_ffn_down=d_wd, ple_w_proj=d_wpp, ple_w_gate=d_wpg)
    riders = early_hook(early) if early_hook is not None else ()

    dog_hg, dog_ml = _outproj_bwd(dr1, w_out_b)
    res = _hgrn2_bwd(u_hg, logits, hg_gn, sst, dog_hg, riders)
    du_hg, d_logits, d_hg_gn = res[:3]
    dqkc, dmv, dmo, dgt, d_ml_gn = _mlstm_bwd(qkc, u_ml, ml_gn, cst, nst, mst, dog_ml)
    dqk, d_conv_w, d_conv_b = _conv_bwd(u_ml, conv_w, pre, dqkc)
    grad_x, du_ml = _inproj_bwd(dr1, du_hg, dqk, dmv, dmo, dgt, w_hg, w_ml)

    dw_hg, db_hg = _wgrad(du_hg, xb, "wgrad_in_hg", tk=U_HG // 2, colsum=True)
    dw_ml, db_ml = _wgrad(du_ml, xb, "wgrad_in_ml", colsum=True)
    d_w_in = jnp.concatenate([dw_hg, dw_ml[:PROJ_W - U_HG]], axis=0)
    d_b_in = jnp.concatenate([db_hg, db_ml[:, :PROJ_W - U_HG]], axis=1)

    grads = dict(w_in=d_w_in, b_in=d_b_in, hg_lb_logits=d_logits, ml_conv_w=d_conv_w, ml_conv_b=d_conv_b,
                 hg_norm_g=d_hg_gn, ml_norm_g=d_ml_gn, ln1_g=d_ln1g, ln1_b=d_ln1b, ln2_g=d_ln2g, ln2_b=d_ln2b,
                 ple_b_gate=d_bpg, **early)
    return loss, grad_x, grads, list(res[3:])


_ANY = pl.BlockSpec(memory_space=pltpu.HBM)
_MESH = pl.DeviceIdType.MESH


def _my_place():
    return lax.axis_index("x"), lax.axis_index("y"), lax.axis_index("c")


def _other_chips(x, y):
    return [(1 - x, y), (x, 1 - y), (1 - x, 1 - y)]


def _allgather_weights(shards, taps, name):
    n = len(shards)
    halves = [s.shape[0] // 2 for s in shards]

    def body(*refs):
        ins, tap_in = refs[:n], refs[n]
        outs, tap_out = refs[n + 1:2 * n + 1], refs[2 * n + 1]
        send_sems, recv_sems, local_sems = refs[2 * n + 2:]
        x, y, c = _my_place()
        me = 2 * x + y
        sibling = (x, y, 1 - c)
        chips = _other_chips(x, y)

        def ici(a, j, block_chip):
            px, py = chips[j]
            src = ins[a].at[pl.ds(pl.multiple_of(c * halves[a], 16), halves[a])] if block_chip is None else outs[a].at[block_chip, c]
            dst = outs[a].at[me if block_chip is None else block_chip, c]
            return pltpu.make_async_remote_copy(src_ref=src, dst_ref=dst, send_sem=send_sems.at[6 * a + j], recv_sem=recv_sems.at[6 * a + j],
                                                device_id=(px, py, c), device_id_type=_MESH)

        def d2d(a, j, half):
            px, py = chips[j]
            blk = outs[a].at[2 * px + py, half]
            return pltpu.make_async_remote_copy(src_ref=blk, dst_ref=blk, send_sem=send_sems.at[6 * a + 3 + j], recv_sem=recv_sems.at[6 * a + 3 + j],
                                                device_id=sibling, device_id_type=_MESH)

        local = []
        for a in range(n):
            for h in range(2):
                cp = pltpu.make_async_copy(ins[a].at[pl.ds(h * halves[a], halves[a])], outs[a].at[me, h], local_sems.at[2 * a + h])
                cp.start()
                local.append(cp)
            for j in range(3):
                ici(a, j, None).start()
        tap_local = pltpu.make_async_copy(tap_in, tap_out.at[me], local_sems.at[2 * n])
        tap_local.start()
        tap_copies = []
        for j, (px, py) in enumerate(chips):
            cp = pltpu.make_async_remote_copy(src_ref=tap_in, dst_ref=tap_out.at[me], send_sem=send_sems.at[6 * n + j], recv_sem=recv_sems.at[6 * n + j],
                                              device_id=(px, py, c), device_id_type=_MESH)
            cp.start()
            tap_copies.append(cp)
        for a in range(n):
            for j, (px, py) in enumerate(chips):
                ici(a, j, 2 * px + py).wait_recv()
                d2d(a, j, c).start()
        for a in range(n):
            for j in range(3):
                d2d(a, j, 1 - c).wait_recv()
        for a in range(n):
            for j in range(3):
                ici(a, j, None).wait_send()
                d2d(a, j, c).wait_send()
        for j, (px, py) in enumerate(chips):
            pltpu.make_async_remote_copy(src_ref=tap_in, dst_ref=tap_out.at[2 * px + py], send_sem=send_sems.at[6 * n + j], recv_sem=recv_sems.at[6 * n + j],
                                         device_id=(px, py, c), device_id_type=_MESH).wait()
        for cp in local:
            cp.wait()
        tap_local.wait()

    res = pl.pallas_call(
        body, name=name,
        in_specs=[_ANY] * (n + 1), out_specs=[_ANY] * (n + 1),
        out_shape=[jax.ShapeDtypeStruct((4, 2, s.shape[0] // 2, s.shape[1]), s.dtype) for s in shards]
        + [jax.ShapeDtypeStruct((4,) + taps.shape, taps.dtype)],
        scratch_shapes=[pltpu.SemaphoreType.DMA((6 * n + 3,)), pltpu.SemaphoreType.DMA((6 * n + 3,)), pltpu.SemaphoreType.DMA((2 * n + 1,))],
    )(*shards, taps)
    return [w.reshape((4,) + s.shape) for w, s in zip(res[:n], shards)], res[n]


def _swap_halves(pieces, name):
    n = len(pieces)
    halves = [p.shape[1] // 2 for p in pieces]

    def body(*refs):
        ins, own, other = refs[:n], refs[n:2 * n], refs[2 * n:3 * n]
        send_sems, recv_sems, local_sems = refs[3 * n:]
        x, y, c = _my_place()

        def half_of(a, which):
            return ins[a].at[pl.ds(0, 4), pl.ds(pl.multiple_of(which * halves[a], 16), halves[a])]

        def to_sibling(a):
            return pltpu.make_async_remote_copy(src_ref=half_of(a, 1 - c), dst_ref=other[a], send_sem=send_sems.at[a], recv_sem=recv_sems.at[a],
                                                device_id=(x, y, 1 - c), device_id_type=_MESH)

        local = []
        for a in range(n):
            cp = pltpu.make_async_copy(half_of(a, c), own[a], local_sems.at[a])
            cp.start()
            local.append(cp)
            to_sibling(a).start()
        for a in range(n):
            to_sibling(a).wait()
            local[a].wait()

    shapes = [jax.ShapeDtypeStruct((4, p.shape[1] // 2, p.shape[2]), p.dtype) for p in pieces]
    res = pl.pallas_call(
        body, name=name,
        in_specs=[_ANY] * n, out_specs=[_ANY] * (2 * n), out_shape=shapes + shapes,
        scratch_shapes=[pltpu.SemaphoreType.DMA((n,)), pltpu.SemaphoreType.DMA((n,)), pltpu.SemaphoreType.DMA((n,))],
    )(*pieces)
    return res[:n], res[n:]


_VMEM = pl.BlockSpec(memory_space=pltpu.VMEM)
_EX_ROWS = 32


def _pair_reduce(p, name):
    s, r, c = p.shape
    half = r // 2

    def body(p_ref, o_ref, other, send_sem, recv_sem):
        x, y, cc = _my_place()
        theirs = pl.multiple_of((1 - cc) * half, 16)
        mine = pl.multiple_of(cc * half, 16)
        cp = pltpu.make_async_remote_copy(src_ref=p_ref.at[pl.ds(0, s), pl.ds(theirs, half)], dst_ref=other, send_sem=send_sem, recv_sem=recv_sem,
                                          device_id=(x, y, 1 - cc), device_id_type=_MESH)
        cp.start()
        cp.wait()

        def step(i, carry):
            r0 = pl.multiple_of(i * _EX_ROWS, _EX_ROWS)
            for slot in range(s):
                own_rows = pl.ds(pl.multiple_of(mine + r0, 16), _EX_ROWS)
                o_ref[slot, pl.ds(r0, _EX_ROWS), :] = (p_ref[slot, own_rows, :] + other[slot, pl.ds(r0, _EX_ROWS), :]).astype(bf16)
            return carry

        lax.fori_loop(0, half // _EX_ROWS, step, 0)

    return pl.pallas_call(
        body, name=name, in_specs=[_VMEM], out_specs=_VMEM,
        out_shape=jax.ShapeDtypeStruct((s, half, c), bf16),
        scratch_shapes=[pltpu.VMEM((s, half, c), f32), pltpu.SemaphoreType.DMA, pltpu.SemaphoreType.DMA],
        compiler_params=pltpu.CompilerParams(vmem_limit_bytes=VMEM_LIMIT),
    )(p)


def _chip_reduce_swap(rcv, name):
    s, h, c = rcv.shape

    def body(r_ref, g_ref, send_sem, recv_sem):
        x, y, cc = _my_place()

        def step(i, carry):
            r0 = pl.multiple_of(i * _EX_ROWS, _EX_ROWS)
            acc = r_ref[0, pl.ds(r0, _EX_ROWS), :].astype(f32)
            for slot in range(1, s):
                acc = acc + r_ref[slot, pl.ds(r0, _EX_ROWS), :].astype(f32)
            g_ref[cc, pl.ds(r0, _EX_ROWS), :] = acc
            return carry

        lax.fori_loop(0, h // _EX_ROWS, step, 0)
        cp = pltpu.make_async_remote_copy(src_ref=g_ref.at[cc], dst_ref=g_ref.at[cc], send_sem=send_sem, recv_sem=recv_sem,
                                          device_id=(x, y, 1 - cc), device_id_type=_MESH)
        cp.start()
        cp.wait()

    return pl.pallas_call(
        body, name=name, in_specs=[_VMEM], out_specs=_VMEM,
        out_shape=jax.ShapeDtypeStruct((2, h, c), f32),
        scratch_shapes=[pltpu.SemaphoreType.DMA, pltpu.SemaphoreType.DMA],
        compiler_params=pltpu.CompilerParams(vmem_limit_bytes=VMEM_LIMIT),
    )(rcv)


def _pair_reduce_cols(p, name):
    s, r, c = p.shape
    hc = c // 2

    def body(p_ref, o_ref, other, send_sem, recv_sem):
        x, y, cc = _my_place()

        def run(mine_lo, theirs_lo):
            cp = pltpu.make_async_remote_copy(src_ref=p_ref.at[pl.ds(0, s), pl.ds(0, r), pl.ds(theirs_lo, hc)], dst_ref=other,
                                              send_sem=send_sem, recv_sem=recv_sem, device_id=(x, y, 1 - cc), device_id_type=_MESH)
            cp.start()
            cp.wait()
            for slot in range(s):
                o_ref[slot] = (p_ref[slot, :, mine_lo:mine_lo + hc] + other[slot]).astype(bf16)

        @pl.when(cc == 0)
        def _():
            run(0, hc)

        @pl.when(cc == 1)
        def _():
            run(hc, 0)

    return pl.pallas_call(
        body, name=name, in_specs=[_VMEM], out_specs=_VMEM,
        out_shape=jax.ShapeDtypeStruct((s, r, hc), bf16),
        scratch_shapes=[pltpu.VMEM((s, r, hc), f32), pltpu.SemaphoreType.DMA, pltpu.SemaphoreType.DMA],
        compiler_params=pltpu.CompilerParams(vmem_limit_bytes=VMEM_LIMIT),
    )(p)


def _chip_reduce_swap_cols(rcv, name):
    s, r, hc = rcv.shape

    def body(r_ref, g_ref, send_sem, recv_sem):
        x, y, cc = _my_place()
        acc = r_ref[0].astype(f32)
        for slot in range(1, s):
            acc = acc + r_ref[slot].astype(f32)
        g_ref[cc] = acc
        cp = pltpu.make_async_remote_copy(src_ref=g_ref.at[cc], dst_ref=g_ref.at[cc], send_sem=send_sem, recv_sem=recv_sem,
                                          device_id=(x, y, 1 - cc), device_id_type=_MESH)
        cp.start()
        cp.wait()

    both = pl.pallas_call(
        body, name=name, in_specs=[_VMEM], out_specs=_VMEM,
        out_shape=jax.ShapeDtypeStruct((2, r, hc), f32),
        scratch_shapes=[pltpu.SemaphoreType.DMA, pltpu.SemaphoreType.DMA],
        compiler_params=pltpu.CompilerParams(vmem_limit_bytes=VMEM_LIMIT),
    )(rcv)
    return both.transpose(1, 0, 2).reshape(r, 2 * hc)


def _add_cast(a, b, name):
    s, r, c = a.shape
    tr = _row_tile(r, c)

    def body(a_ref, b_ref, o_ref):
        o_ref[...] = (a_ref[...] + b_ref[...]).astype(bf16)

    blk = pl.BlockSpec((1, tr, c), lambda i, j: (i, j, 0))
    return pl.pallas_call(
        body, name=name, grid=(s, r // tr), in_specs=[blk, blk], out_specs=blk,
        out_shape=jax.ShapeDtypeStruct(a.shape, bf16),
        compiler_params=_cparams(2, arbitrary=False),
    )(a, b)


def _gather_copies(ins, outs, send_sems, recv_sems, local_sems):
    x, y, c = _my_place()
    me = 2 * x + y
    local, outgoing, incoming = [], [], []
    for a in range(len(ins)):
        local.append(pltpu.make_async_copy(ins[a], outs[a].at[me], local_sems.at[a]))
        for j, (px, py) in enumerate(_other_chips(x, y)):
            sems = dict(send_sem=send_sems.at[3 * a + j], recv_sem=recv_sems.at[3 * a + j], device_id=(px, py, c), device_id_type=_MESH)
            outgoing.append(pltpu.make_async_remote_copy(src_ref=ins[a], dst_ref=outs[a].at[me], **sems))
            incoming.append(pltpu.make_async_remote_copy(src_ref=ins[a], dst_ref=outs[a].at[2 * px + py], **sems))
    return local, outgoing, incoming


def _gather_chips(blocks, name):
    n = len(blocks)

    def body(*refs):
        local, outgoing, incoming = _gather_copies(refs[:n], refs[n:2 * n], *refs[2 * n:])
        for cp in local + outgoing:
            cp.start()
        for cp in incoming:
            cp.wait_recv()
        for cp in outgoing:
            cp.wait_send()
        for cp in local:
            cp.wait()

    return pl.pallas_call(
        body, name=name, in_specs=[_ANY] * n, out_specs=[_ANY] * n, out_shape=_gather_shapes(blocks),
        scratch_shapes=[pltpu.SemaphoreType.DMA((3 * n,)), pltpu.SemaphoreType.DMA((3 * n,)), pltpu.SemaphoreType.DMA((n,))],
    )(*blocks)


def _riding_call(body, name, nsteps, in_specs, out_specs, out_shape, scratch_shapes, operands, riders, copies, ride_shapes):
    nr, n_in, n_out, n_scr = len(riders), len(in_specs), len(out_specs), len(scratch_shapes)

    def wrapped(*refs):
        ins, ride_in = refs[:n_in], refs[n_in:n_in + nr]
        outs, ride_out = refs[n_in + nr:n_in + nr + n_out], refs[n_in + nr + n_out:n_in + 2 * nr + n_out]
        scratch, sems = refs[n_in + 2 * nr + n_out:n_in + 2 * nr + n_out + n_scr], refs[n_in + 2 * nr + n_out + n_scr:]
        if nr:
            @pl.when(pl.program_id(0) == 0)
            def _():
                local, outgoing, _ = copies(ride_in, ride_out, *sems)
                for cp in local + outgoing:
                    cp.start()

        body(*ins, *outs, *scratch)
        if nr:
            @pl.when(pl.program_id(0) == nsteps - 1)
            def _():
                local, outgoing, incoming = copies(ride_in, ride_out, *sems)
                for cp in incoming:
                    cp.wait_recv()
                for cp in outgoing:
                    cp.wait_send()
                for cp in local:
                    cp.wait()

    hbm = pl.BlockSpec(memory_space=pltpu.HBM)
    sems = [pltpu.SemaphoreType.DMA((3 * nr,)), pltpu.SemaphoreType.DMA((3 * nr,)), pltpu.SemaphoreType.DMA((nr,))] if nr else []
    res = pl.pallas_call(
        wrapped, name=name, grid=(nsteps,),
        in_specs=list(in_specs) + [hbm] * nr, out_specs=list(out_specs) + [hbm] * nr,
        out_shape=list(out_shape) + list(ride_shapes),
        scratch_shapes=list(scratch_shapes) + sems,
        compiler_params=_cparams(1),
    )(*operands, *riders)
    return list(res[:n_out]), list(res[n_out:])


def _gather_shapes(riders):
    return [jax.ShapeDtypeStruct((4,) + r.shape, r.dtype) for r in riders]


def _scatter_copies(ins, outs, send_sems, recv_sems, local_sems):
    x, y, c = _my_place()
    me = 2 * x + y
    local, outgoing, incoming = [], [], []
    for a in range(len(ins)):
        local.append(pltpu.make_async_copy(ins[a].at[me], outs[a].at[me], local_sems.at[a]))
        for j, (px, py) in enumerate(_other_chips(x, y)):
            sems = dict(send_sem=send_sems.at[3 * a + j], recv_sem=recv_sems.at[3 * a + j], device_id=(px, py, c), device_id_type=_MESH)
            outgoing.append(pltpu.make_async_remote_copy(src_ref=ins[a].at[2 * px + py], dst_ref=outs[a].at[me], **sems))
            incoming.append(pltpu.make_async_remote_copy(src_ref=ins[a].at[2 * px + py], dst_ref=outs[a].at[2 * px + py], **sems))
    return local, outgoing, incoming


def _scatter_start(ins, outs, send_sems, recv_sems, local_sems):
    local, outgoing, _ = _scatter_copies(ins, outs, send_sems, recv_sems, local_sems)
    for cp in local + outgoing:
        cp.start()


def _scatter_wait(ins, outs, send_sems, recv_sems, local_sems):
    local, outgoing, incoming = _scatter_copies(ins, outs, send_sems, recv_sems, local_sems)
    for cp in incoming:
        cp.wait_recv()
    for cp in outgoing:
        cp.wait_send()
    for cp in local:
        cp.wait()


def _scatter_chips(pieces, name):
    n = len(pieces)

    def body(*refs):
        ins, outs = refs[:n], refs[n:2 * n]
        _scatter_start(ins, outs, *refs[2 * n:])
        _scatter_wait(ins, outs, *refs[2 * n:])

    return pl.pallas_call(
        body, name=name,
        in_specs=[_ANY] * n, out_specs=[_ANY] * n,
        out_shape=[jax.ShapeDtypeStruct(s.shape, s.dtype) for s in pieces],
        scratch_shapes=[pltpu.SemaphoreType.DMA((3 * n,)), pltpu.SemaphoreType.DMA((3 * n,)), pltpu.SemaphoreType.DMA((n,))],
    )(*pieces)


def _swap_cores(blocks, name):
    n = len(blocks)
    parts = 4
    rows = [b.shape[0] // parts for b in blocks]

    def body(*refs):
        ins, outs = refs[:n], refs[n:2 * n]
        send_sems, recv_sems, local_sems = refs[2 * n:]
        x, y, c = _my_place()

        def remote(a, k, slot):
            rs = pl.ds(k * rows[a], rows[a])
            return pltpu.make_async_remote_copy(src_ref=ins[a].at[rs], dst_ref=outs[a].at[slot, rs], send_sem=send_sems.at[parts * a + k],
                                                recv_sem=recv_sems.at[parts * a + k], device_id=(x, y, 1 - c), device_id_type=_MESH)

        local = []
        for a in range(n):
            cp = pltpu.make_async_copy(ins[a], outs[a].at[c], local_sems.at[a])
            cp.start()
            local.append(cp)
            for k in range(parts):
                remote(a, k, c).start()
        for a in range(n):
            for k in range(parts):
                remote(a, k, 1 - c).wait()
            local[a].wait()

    return pl.pallas_call(
        body, name=name,
        in_specs=[_ANY] * n, out_specs=[_ANY] * n,
        out_shape=[jax.ShapeDtypeStruct((2,) + s.shape, s.dtype) for s in blocks],
        scratch_shapes=[pltpu.SemaphoreType.DMA((parts * n,)), pltpu.SemaphoreType.DMA((parts * n,)), pltpu.SemaphoreType.DMA((n,))],
    )(*blocks)


def _gather_all(block, name):
    def body(in_ref, out_ref, send_sems, recv_sems, local_sem):
        x, y, c = _my_place()
        me = 4 * x + 2 * y + c
        cp = pltpu.make_async_copy(in_ref, out_ref.at[me], local_sem)
        cp.start()
        peers = []
        for dx in range(2):
            for dy in range(2):
                for dc in range(2):
                    if dx or dy or dc:
                        peers.append((1 - x if dx else x, 1 - y if dy else y, 1 - c if dc else c))
        for j, pr in enumerate(peers):
            pltpu.make_async_remote_copy(src_ref=in_ref, dst_ref=out_ref.at[me], send_sem=send_sems.at[j], recv_sem=recv_sems.at[j],
                                         device_id=pr, device_id_type=_MESH).start()
        for j, (px, py, pc) in enumerate(peers):
            pltpu.make_async_remote_copy(src_ref=in_ref, dst_ref=out_ref.at[4 * px + 2 * py + pc], send_sem=send_sems.at[j], recv_sem=recv_sems.at[j],
                                         device_id=(px, py, pc), device_id_type=_MESH).wait()
        cp.wait()

    return pl.pallas_call(
        body, name=name,
        in_specs=[_ANY], out_specs=_ANY,
        out_shape=jax.ShapeDtypeStruct((8,) + block.shape, block.dtype),
        scratch_shapes=[pltpu.SemaphoreType.DMA((7,)), pltpu.SemaphoreType.DMA((7,)), pltpu.SemaphoreType.DMA],
    )(block)


def _row_tile(r, c):
    best = r
    for cand in range(16, r + 1, 16):
        if r % cand == 0 and cand * c * 4 <= (1 << 20):
            best = cand
    return best if best * c * 4 <= (4 << 20) else r


def _sum_slots(parts, name):
    n, r, c = parts.shape
    tr = _row_tile(r, c)

    def body(p_ref, o_ref):
        acc = p_ref[0].astype(f32)
        for s in range(1, n):
            acc = acc + p_ref[s].astype(f32)
        o_ref[...] = acc

    return pl.pallas_call(
        body, name=name, grid=(r // tr,),
        in_specs=[pl.BlockSpec((n, tr, c), lambda i: (0, i, 0))],
        out_specs=pl.BlockSpec((tr, c), lambda i: (i, 0)),
        out_shape=jax.ShapeDtypeStruct((r, c), f32),
        compiler_params=_cparams(1, arbitrary=False),
    )(parts)


def _adamw(parts, w, m, v, name):
    n, r, c = parts.shape
    tr = _row_tile(r, c)
    tc = c
    if tr == r and r * c * 4 > (1 << 20) and c % 256 == 0:
        tc = 256

    def body(p_ref, w_ref, m_ref, v_ref, g_ref, d_ref, nm_ref, nv_ref):
        g = p_ref[0]
        for s in range(1, n):
            g = g + p_ref[s]
        nm = B1 * m_ref[...] + (1.0 - B1) * g
        nv = B2 * v_ref[...] + (1.0 - B2) * (g * g)
        m_hat = nm / (1.0 - B1 ** STEP)
        v_hat = nv / (1.0 - B2 ** STEP)
        g_ref[...] = g
        nm_ref[...] = nm
        nv_ref[...] = nv
        d_ref[...] = -LR * (m_hat / (jnp.sqrt(v_hat) + EPS_ADAM) + WD * w_ref[...])

    blk = pl.BlockSpec((tr, tc), lambda i, j: (i, j))
    return pl.pallas_call(
        body, name=name, grid=(r // tr, c // tc),
        in_specs=[pl.BlockSpec((n, tr, tc), lambda i, j: (0, i, j)), blk, blk, blk],
        out_specs=[blk] * 4,
        out_shape=[jax.ShapeDtypeStruct((r, c), f32)] * 4,
        compiler_params=_cparams(2, arbitrary=False),
    )(parts, w, m, v)


_BIG = ["w_in", "w_out", "w_ffn_gate", "w_ffn_up", "w_ffn_down", "ple_w_proj", "ple_w_gate"]
_SMALL = ["b_in", "hg_lb_logits", "ml_conv_w", "ml_conv_b", "hg_norm_g", "ml_norm_g", "ln1_g", "ln1_b", "ln2_g", "ln2_b", "ple_b_gate"]
_ORDER = ["w_in", "b_in", "hg_lb_logits", "ml_conv_w", "ml_conv_b", "hg_norm_g", "ml_norm_g", "w_out", "ln1_g", "ln1_b",
          "w_ffn_gate", "w_ffn_up", "w_ffn_down", "ln2_g", "ln2_b", "ple_w_proj", "ple_w_gate", "ple_b_gate"]
_PACK_ROWS, _PACK_COLS = 16, 1024


def _pack(arrays):
    flat = jnp.concatenate([a.reshape(-1) for a in arrays])
    return jnp.pad(flat, (0, _PACK_ROWS * _PACK_COLS - flat.shape[0])).reshape(_PACK_ROWS, _PACK_COLS)


def _unpack(pack, shapes):
    flat = pack.reshape(-1)
    out, off = [], 0
    for s in shapes:
        size = 1
        for d in s:
            size *= d
        out.append(flat[off:off + size].reshape(s))
        off += size
    return out


def _to_chip_major(g, col_split):
    if col_split:
        k, n = g.shape
        return g.reshape(k, 4, n // 4).transpose(1, 0, 2)
    k, n = g.shape
    return g.reshape(4, k // 4, n)


def kernel(x, p, w_in, b_in, hg_lb_logits, ml_conv_w, ml_conv_b, hg_norm_g, ml_norm_g, w_out, ln1_g, ln1_b, w_ffn_gate, w_ffn_up, w_ffn_down, ln2_g, ln2_b, ple_w_proj, ple_w_gate, ple_b_gate, loss_target, m_w_in, m_b_in, m_hg_lb_logits, m_ml_conv_w, m_ml_conv_b, m_hg_norm_g, m_ml_norm_g, m_w_out, m_ln1_g, m_ln1_b, m_w_ffn_gate, m_w_ffn_up, m_w_ffn_down, m_ln2_g, m_ln2_b, m_ple_w_proj, m_ple_w_gate, m_ple_b_gate, v_w_in, v_b_in, v_hg_lb_logits, v_ml_conv_w, v_ml_conv_b, v_hg_norm_g, v_ml_norm_g, v_w_out, v_ln1_g, v_ln1_b, v_w_ffn_gate, v_w_ffn_up, v_w_ffn_down, v_ln2_g, v_ln2_b, v_ple_w_proj, v_ple_w_gate, v_ple_b_gate):
    args = dict(locals())
    wts = {k: args[k] for k in _ORDER}
    mom = {k: args["m_" + k] for k in _ORDER}
    var = {k: args["v_" + k] for k in _ORDER}
    two_d = lambda a: a.reshape(a.shape[-2], a.shape[-1])
    block = lambda k, a: jnp.swapaxes(two_d(a), 0, 1) if k in _TRANSPOSED else two_d(a)
    unblock = lambda k, a: (jnp.swapaxes(a, 0, 1) if k in _TRANSPOSED else a).reshape(wts[k].shape)

    shards = {k: block(k, wts[k]).astype(bf16) for k in _BIG}
    w_in_blocks, taps = _gather_chips([shards["w_in"], two_d(ml_conv_w)], "gather_w_in")
    w_in_full = _from_chip_major(w_in_blocks, False)
    conv_w_full = _from_chip_major(taps, True)

    def core_sum(k, g):
        pieces = _to_chip_major(g, k in _COL_SPLIT)
        if pieces.shape[1] % (2 * _EX_ROWS):
            return _pair_reduce_cols(pieces, "pair_reduce_" + k)
        return _pair_reduce(pieces, "pair_reduce_" + k)

    early_keys = _BIG[1:]
    loss, grad_x, grads, early_received = _local_step(
        x[0], p[0, 0], loss_target[0], w_in_full, b_in, hg_lb_logits, conv_w_full, ml_conv_b, hg_norm_g, ml_norm_g,
        None, ln1_g, ln1_b, None, None, None, ln2_g, ln2_b, None, None, ple_b_gate,
        early_hook=lambda early: [core_sum(k, early[k]) for k in early_keys],
        late_shards={k: shards[k] for k in early_keys})

    received = list(_scatter_chips([core_sum("w_in", grads["w_in"])], "scatter_grad_w_in")) + list(early_received)
    out_g, out_d, out_m, out_v = {}, {}, {}, {}
    for k, rcv in zip(_BIG, received):
        own = block(k, wts[k])
        if rcv.shape[1] == own.shape[0]:
            whole = _chip_reduce_swap_cols(rcv, "chip_reduce_" + k)
        else:
            parts = _chip_reduce_swap(rcv, "chip_reduce_" + k)
            whole = parts.reshape(2 * parts.shape[1], parts.shape[2])
        g, d, nm, nv = _adamw(whole[None], own, block(k, mom[k]), block(k, var[k]), "adamw_" + k)
        out_g[k], out_d[k], out_m[k], out_v[k] = unblock(k, g), unblock(k, d), unblock(k, nm), unblock(k, nv)

    small_shapes = [(1, PROJ_W), (2, MIX_W), (CONV_K, MIX_W)] + [(1, MIX_W)] * 3 + [(1, D_MODEL)] * 5 + [(1, 1)]
    contrib = _pack([grads[k] for k in _SMALL] + [loss])
    summed = _sum_slots(_gather_all(contrib, "gather_small"), "sum_small")
    small = _unpack(summed, small_shapes)
    loss_total = small[-1].reshape(())
    gsm = dict(zip(_SMALL, small[:-1]))
    place = 2 * lax.axis_index("x") + lax.axis_index("y")
    conv_cols = ml_conv_w.shape[-1]
    gsm["ml_conv_w"] = lax.dynamic_slice(gsm["ml_conv_w"], (0, place * conv_cols), (CONV_K, conv_cols))
    own_shapes = [wts[k].shape for k in _SMALL]
    g_pack = _pack([gsm[k] for k in _SMALL])
    res = _adamw(g_pack[None], _pack([wts[k] for k in _SMALL]), _pack([mom[k] for k in _SMALL]), _pack([var[k] for k in _SMALL]), "adamw_small")
    for dst, pack in zip((out_g, out_d, out_m, out_v), res):
        for k, a in zip(_SMALL, _unpack(pack, own_shapes)):
            dst[k] = a

    outs = [loss_total, grad_x[None]]
    for group in (out_g, out_d, out_m, out_v):
        outs += [group[k] for k in _ORDER]
    return tuple(outs)
```

```python
import functools

import jax
import jax.numpy as jnp
from jax import lax
from jax.experimental import pallas as pl
from jax.experimental.pallas import tpu as pltpu

f32 = jnp.float32
bf16 = jnp.bfloat16
HI = lax.Precision.HIGHEST

D_MODEL = 1024
HEADS = 4
HEAD_W = 128
MIX_W = HEADS * HEAD_W
ML_DQK = 64
PROJ_W = 3592
U_HG = 4 * MIX_W
U_ML = 3 * MIX_W + 128
D_FF = 2816
PLE = 256
CHUNK = 128
SUB = 16
EXP_CAP = 80.0
CONV_K = 4
HALO = 8
ALPHA = float(2.0 ** 0.25)
LN_EPS = 1e-5
RMS_EPS = 1e-6
NEG = -1e30
LR, B1, B2, EPS_ADAM, WD, STEP = 0.001, 0.9, 0.999, 1e-08, 0.01, 10
VMEM_LIMIT = 56 * 1024 * 1024
DENSE_ROWS = 512
WGRAD_ROWS = 2048


def _cparams(n_axes, arbitrary=True):
    sem = ("arbitrary",) * n_axes if arbitrary else ("parallel",) * n_axes
    return pltpu.CompilerParams(dimension_semantics=sem, vmem_limit_bytes=VMEM_LIMIT)


ACT = bf16


def _mx(a):
    return a.astype(ACT)


def _bdot(a, b):
    return jnp.dot(_mx(a), _mx(b), preferred_element_type=f32)


def _bdot_nt(a, b):
    return lax.dot_general(_mx(a), _mx(b), (((1,), (1,)), ((), ())), preferred_element_type=f32)


def _bdot_tn(a, b):
    return lax.dot_general(_mx(a), _mx(b), (((0,), (0,)), ((), ())), preferred_element_type=f32)


def _split3(x):
    hi = x.astype(bf16)
    r1 = x - hi.astype(f32)
    mid = r1.astype(bf16)
    lo = (r1 - mid.astype(f32)).astype(bf16)
    return hi, mid, lo


def _dot3(a, b, dims):
    a_hi = a.astype(bf16)
    a_lo = (a - a_hi.astype(f32)).astype(bf16)
    b_hi = b.astype(bf16)
    b_lo = (b - b_hi.astype(f32)).astype(bf16)
    dn = (dims, ((), ()))
    return (lax.dot_general(a_hi, b_hi, dn, preferred_element_type=f32) + lax.dot_general(a_hi, b_lo, dn, preferred_element_type=f32)
            + lax.dot_general(a_lo, b_hi, dn, preferred_element_type=f32))


def _sel_dot(sel, x):
    sb = sel.astype(bf16)
    return sum(jnp.dot(sb, part, preferred_element_type=f32) for part in _split3(x))


def _sel_dot_nt(sel, x):
    sb = sel.astype(bf16)
    return sum(lax.dot_general(sb, part, (((1,), (1,)), ((), ())), preferred_element_type=f32) for part in _split3(x))


def _sigmoid(x):
    return 1.0 / (1.0 + jnp.exp(-x))


def _log_sigmoid(x):
    return jnp.minimum(x, 0.0) - jnp.log(1.0 + jnp.exp(-jnp.abs(x)))


def _tri(n, upper=False):
    r = lax.broadcasted_iota(jnp.int32, (n, n), 0)
    c = lax.broadcasted_iota(jnp.int32, (n, n), 1)
    return (c >= r) if upper else (c <= r)


def _rows(tm, n, col=0):
    return pl.BlockSpec((tm, n), lambda i, _c=col: (i, _c))


def _rows_rev(tm, n, nb, col=0):
    return pl.BlockSpec((tm, n), lambda i, _c=col, _nb=nb: (_nb - 1 - i, _c))


def _const(shape):
    return pl.BlockSpec(shape, lambda i, _n=len(shape): (0,) * _n)


def _resident(shape):
    return pl.BlockSpec(shape, lambda i, _n=len(shape): (0,) * _n, pipeline_mode=pl.Buffered(1))


def _tile(t, want):
    return want if t % want == 0 else t


def _inproj(x, w_hg, w_ml, b_hg, b_ml, riders=()):
    t = x.shape[0]
    tm = _tile(t, DENSE_ROWS)

    def body(x_ref, whg_ref, wml_ref, bhg_ref, bml_ref, uhg_ref, uml_ref, xb_ref):
        xb = _mx(x_ref[...])
        xb_ref[...] = xb
        uhg_ref[...] = _bdot_nt(xb, whg_ref[...]) + bhg_ref[...]
        uml_ref[...] = _bdot_nt(xb, wml_ref[...]) + bml_ref[...]

    return _riding_call(
        body, "inproj", t // tm,
        in_specs=[_rows(tm, D_MODEL), _resident((U_HG, D_MODEL)), _resident((U_ML, D_MODEL)), _const((1, U_HG)), _const((1, U_ML))],
        out_specs=[_rows(tm, U_HG), _rows(tm, U_ML), _rows(tm, D_MODEL)],
        out_shape=[jax.ShapeDtypeStruct((t, U_HG), f32), jax.ShapeDtypeStruct((t, U_ML), f32), jax.ShapeDtypeStruct((t, D_MODEL), ACT)],
        scratch_shapes=[], operands=(x, w_hg, w_ml, b_hg, b_ml), riders=riders, copies=_gather_copies, ride_shapes=_gather_shapes(riders))


def _hg_gates(hq, hf, lb, tri):
    s = _sigmoid(hf)
    om = 1.0 - lb
    f = lb + om * s
    g = jnp.log(f)
    k = om * (1.0 - s)
    sq = _sigmoid(hq)
    q = hq * sq
    b = _sel_dot(tri, g)
    return q, sq, s, f, k, b


def _hg_scores(q, k, b, tril_mask):
    qts, kts, eqs, eks, rows = [], [], [], [], []
    for i in range(CHUNK // SUB):
        lo = i * SUB
        ref = jnp.zeros_like(b[0:1]) if i == 0 else b[lo - 1:lo]
        eq = jnp.exp(b[lo:lo + SUB] - ref)
        ek = jnp.exp(jnp.minimum(ref - b, EXP_CAP))
        qt = q[lo:lo + SUB] * eq
        kt = k * ek
        rows.append(_bdot_nt(qt, kt))
        qts.append(qt); kts.append(kt); eqs.append(eq); eks.append(ek)
    a = jnp.where(tril_mask, jnp.concatenate(rows, axis=0), 0.0)
    return a, qts, kts, eqs, eks


def _head_rms(o, gn):
    rstd = lax.rsqrt(jnp.mean(o * o, axis=-1, keepdims=True) + RMS_EPS)
    oh = o * rstd
    return oh, rstd, oh * gn


def _lower_bound(logit_ref):
    lg = logit_ref[...]
    return _sigmoid(lg[0:1] - lg[1:2])


def _hgrn2_fwd(u_hg, logits, gn, riders=()):
    t = u_hg.shape[0]
    tb = _tile(t, 256)
    nc_blk = tb // CHUNK

    def body(u_ref, lg_ref, gn_ref, og_ref, sst_ref, st_ref):
        @pl.when(pl.program_id(0) == 0)
        def _():
            st_ref[...] = jnp.zeros_like(st_ref)

        lb_all = _lower_bound(lg_ref)
        tril_mask = _tri(CHUNK)
        tri = tril_mask.astype(f32)

        def chunk(c, carry):
            r0 = pl.multiple_of(c * CHUNK, CHUNK)
            rows = pl.ds(r0, CHUNK)
            heads = range(HEADS)
            cols = [slice(h * HEAD_W, (h + 1) * HEAD_W) for h in heads]
            hv = [u_ref[rows, 2 * MIX_W + h * HEAD_W:2 * MIX_W + (h + 1) * HEAD_W] for h in heads]
            gts = [_hg_gates(u_ref[rows, h * HEAD_W:(h + 1) * HEAD_W], u_ref[rows, MIX_W + h * HEAD_W:MIX_W + (h + 1) * HEAD_W],
                             lb_all[:, cols[h]], tri) for h in heads]
            q = [g[0] for g in gts]
            k = [g[4] for g in gts]
            b = [g[5] for g in gts]
            a = [_hg_scores(q[h], k[h], b[h], tril_mask)[0] for h in heads]
            st = [st_ref[h] for h in heads]
            bl = [b[h][CHUNK - 1:CHUNK] for h in heads]
            o = [_bdot(a[h], hv[h]) + _bdot_nt(q[h] * jnp.exp(b[h]), st[h]) for h in heads]
            new_st = [st[h] * jnp.exp(bl[h]) + _bdot_tn(hv[h], k[h] * jnp.exp(bl[h] - b[h])) for h in heads]
            for h in heads:
                sst_ref[c, h] = st[h]
                st_ref[h] = new_st[h]
                hgate = u_ref[rows, 3 * MIX_W + h * HEAD_W:3 * MIX_W + (h + 1) * HEAD_W]
                _, _, y = _head_rms(o[h], gn_ref[:, cols[h]])
                og_ref[rows, cols[h]] = (y * (hgate * _sigmoid(hgate))).astype(ACT)
            return carry

        lax.fori_loop(0, nc_blk, chunk, 0, unroll=True)

    return _riding_call(
        body, "hgrn2_fwd", t // tb,
        in_specs=[_rows(tb, U_HG), _const((2, MIX_W)), _const((1, MIX_W))],
        out_specs=[_rows(tb, MIX_W), pl.BlockSpec((nc_blk, HEADS, HEAD_W, HEAD_W), lambda i: (i, 0, 0, 0))],
        out_shape=[jax.ShapeDtypeStruct((t, MIX_W), ACT), jax.ShapeDtypeStruct((t // CHUNK, HEADS, HEAD_W, HEAD_W), f32)],
        scratch_shapes=[pltpu.VMEM((HEADS, HEAD_W, HEAD_W), f32)],
        operands=(u_hg, logits, gn), riders=riders, copies=_gather_copies, ride_shapes=_gather_shapes(riders))


def _hgrn2_bwd(u_hg, logits, gn, sst, dog, riders=()):
    t = u_hg.shape[0]
    tb = _tile(t, 256)
    nb = t // tb
    nc_blk = tb // CHUNK
    nr = len(riders)

    def body(*refs):
        u_ref, lg_ref, gn_ref, sst_ref, dog_ref = refs[:5]
        ride_in = refs[5:5 + nr]
        du_ref, dlg_ref, dgn_ref = refs[5 + nr:8 + nr]
        ride_out = refs[8 + nr:8 + 2 * nr]
        dst_ref = refs[8 + 2 * nr]
        ride_sems = refs[9 + 2 * nr:]

        @pl.when(pl.program_id(0) == 0)
        def _():
            dst_ref[...] = jnp.zeros_like(dst_ref)
            dlg_ref[...] = jnp.zeros_like(dlg_ref)
            dgn_ref[...] = jnp.zeros_like(dgn_ref)
            if nr:
                _scatter_start(ride_in, ride_out, *ride_sems)

        lb_all = _lower_bound(lg_ref)
        tril_mask = _tri(CHUNK)
        tri = tril_mask.astype(f32)
        triu = _tri(CHUNK, upper=True).astype(f32)

        def chunk(j, carry):
            c = nc_blk - 1 - j
            r0 = pl.multiple_of(c * CHUNK, CHUNK)
            rows = pl.ds(r0, CHUNK)
            heads = range(HEADS)
            nsub = CHUNK // SUB
            cols = [slice(h * HEAD_W, (h + 1) * HEAD_W) for h in heads]
            hq = [u_ref[rows, h * HEAD_W:(h + 1) * HEAD_W] for h in heads]
            hf = [u_ref[rows, MIX_W + h * HEAD_W:MIX_W + (h + 1) * HEAD_W] for h in heads]
            hv = [u_ref[rows, 2 * MIX_W + h * HEAD_W:2 * MIX_W + (h + 1) * HEAD_W] for h in heads]
            lb = [lb_all[:, cols[h]] for h in heads]
            gts = [_hg_gates(hq[h], hf[h], lb[h], tri) for h in heads]
            q, sq, s, f, k, b = ([g[n] for g in gts] for n in range(6))
            scs = [_hg_scores(q[h], k[h], b[h], tril_mask) for h in heads]
            a, qts, kts, eqs, eks = ([sc[n] for sc in scs] for n in range(5))
            st = [sst_ref[c, h] for h in heads]
            dst = [dst_ref[h] for h in heads]
            bl = [b[h][CHUNK - 1:CHUNK] for h in heads]
            eb = [jnp.exp(b[h]) for h in heads]
            qh = [q[h] * eb[h] for h in heads]
            ekl = [jnp.exp(bl[h] - b[h]) for h in heads]
            kh = [k[h] * ekl[h] for h in heads]
            o = [_bdot(a[h], hv[h]) + _bdot_nt(qh[h], st[h]) for h in heads]
            do = []
            for h in heads:
                hgate = u_ref[rows, 3 * MIX_W + h * HEAD_W:3 * MIX_W + (h + 1) * HEAD_W]
                gnh = gn_ref[:, cols[h]]
                oh, rstd, y = _head_rms(o[h], gnh)
                sg = _sigmoid(hgate)
                dogh = dog_ref[rows, cols[h]]
                dy = dogh * (hgate * sg)
                du_ref[rows, 3 * MIX_W + h * HEAD_W:3 * MIX_W + (h + 1) * HEAD_W] = (dogh * y * (sg * (1.0 + hgate * (1.0 - sg)))).astype(ACT)
                dgn_ref[:, cols[h]] += jnp.sum(dy * oh, axis=0, keepdims=True)
                doh = dy * gnh
                do.append(rstd * (doh - oh * jnp.mean(doh * oh, axis=-1, keepdims=True)))
            da = [jnp.where(tril_mask, _bdot_nt(do[h], hv[h]), 0.0) for h in heads]
            dv = [_bdot_tn(a[h], do[h]) + _bdot_nt(kh[h], dst[h]) for h in heads]
            dq = [_bdot(do[h], st[h]) * eb[h] for h in heads]
            dk = [_bdot(hv[h], dst[h]) * ekl[h] for h in heads]
            d_last = [jnp.sum(k[h] * dk[h], axis=0, keepdims=True) + jnp.exp(bl[h]) * jnp.sum(dst[h] * st[h], axis=0, keepdims=True)
                      for h in heads]
            dqs = [[] for _ in heads]
            for i in range(nsub):
                for h in heads:
                    da_i = da[h][i * SUB:(i + 1) * SUB]
                    dqs[h].append(_dot3(da_i, kts[h][i], ((1,), (0,))) * eqs[h][i])
                    dk[h] = dk[h] + _dot3(da_i, qts[h][i], ((0,), (0,))) * eks[h][i]
            for h in heads:
                dq[h] = dq[h] + jnp.concatenate(dqs[h], axis=0)
                dst_ref[h] = dst[h] * jnp.exp(bl[h]) + _bdot_tn(do[h], qh[h])
            dg = [_sel_dot(triu, q[h] * dq[h] - k[h] * dk[h]) + d_last[h] for h in heads]
            for h in heads:
                dfk = dg[h] / f[h] - dk[h]
                du_ref[rows, h * HEAD_W:(h + 1) * HEAD_W] = (dq[h] * (sq[h] * (1.0 + hq[h] * (1.0 - sq[h])))).astype(ACT)
                du_ref[rows, MIX_W + h * HEAD_W:MIX_W + (h + 1) * HEAD_W] = ((1.0 - lb[h]) * dfk * s[h] * (1.0 - s[h])).astype(ACT)
                du_ref[rows, 2 * MIX_W + h * HEAD_W:2 * MIX_W + (h + 1) * HEAD_W] = dv[h].astype(ACT)
                dlb = jnp.sum((1.0 - s[h]) * dfk, axis=0, keepdims=True) * (lb[h] * (1.0 - lb[h]))
                dlg_ref[0:1, cols[h]] += dlb
                dlg_ref[1:2, cols[h]] -= dlb
            return carry

        lax.fori_loop(0, nc_blk, chunk, 0, unroll=True)

        if nr:
            @pl.when(pl.program_id(0) == nb - 1)
            def _():
                _scatter_wait(ride_in, ride_out, *ride_sems)

    hbm = pl.BlockSpec(memory_space=pltpu.HBM)
    ride_scratch = [pltpu.SemaphoreType.DMA((3 * nr,)), pltpu.SemaphoreType.DMA((3 * nr,)), pltpu.SemaphoreType.DMA((nr,))] if nr else []
    return pl.pallas_call(
        body, name="hgrn2_bwd", grid=(nb,),
        in_specs=[_rows_rev(tb, U_HG, nb), _const((2, MIX_W)), _const((1, MIX_W)),
                  pl.BlockSpec((nc_blk, HEADS, HEAD_W, HEAD_W), lambda i: (nb - 1 - i, 0, 0, 0)), _rows_rev(tb, MIX_W, nb)] + [hbm] * nr,
        out_specs=[_rows_rev(tb, U_HG, nb), _const((2, MIX_W)), _const((1, MIX_W))] + [hbm] * nr,
        out_shape=[jax.ShapeDtypeStruct((t, U_HG), ACT), jax.ShapeDtypeStruct((2, MIX_W), f32), jax.ShapeDtypeStruct((1, MIX_W), f32)]
        + [jax.ShapeDtypeStruct(r.shape, r.dtype) for r in riders],
        scratch_shapes=[pltpu.VMEM((HEADS, HEAD_W, HEAD_W), f32)] + ride_scratch,
        compiler_params=_cparams(1),
    )(u_hg, logits, gn, sst, dog, *riders)


def _conv_fwd(u_ml, w, b):
    t = u_ml.shape[0]
    tm = _tile(t, 512)

    def body(x_ref, w_ref, b_ref, pre_ref, act_ref, xbuf):
        @pl.when(pl.program_id(0) == 0)
        def _():
            xbuf[...] = jnp.zeros_like(xbuf)

        xbuf[0:HALO, :] = xbuf[tm:tm + HALO, :]
        xbuf[HALO:HALO + tm, :] = x_ref[...]
        pre = b_ref[...] + jnp.zeros((tm, MIX_W), f32)
        for kk in range(CONV_K):
            off = HALO - (CONV_K - 1) + kk
            pre = pre + w_ref[kk:kk + 1, :] * xbuf[off:off + tm, :]
        pre_ref[...] = pre
        act_ref[...] = pre * _sigmoid(pre)

    return pl.pallas_call(
        body, name="conv_fwd", grid=(t // tm,),
        in_specs=[_rows(tm, MIX_W), _const((CONV_K, MIX_W)), _const((1, MIX_W))],
        out_specs=[_rows(tm, MIX_W), _rows(tm, MIX_W)],
        out_shape=[jax.ShapeDtypeStruct((t, MIX_W), f32)] * 2,
        scratch_shapes=[pltpu.VMEM((tm + HALO, MIX_W), f32)],
        compiler_params=_cparams(1),
    )(u_ml, w, b)


def _conv_bwd(u_ml, w, pre, dact):
    t = u_ml.shape[0]
    tm = _tile(t, 512)
    nb = t // tm
    hb = tm // HALO

    def body(x_ref, halo_ref, w_ref, pre_ref, dact_ref, dx_ref, dw_ref, db_ref, dbuf, xbuf):
        i = pl.program_id(0)

        @pl.when(i == 0)
        def _():
            dbuf[...] = jnp.zeros_like(dbuf)
            dw_ref[...] = jnp.zeros_like(dw_ref)
            db_ref[...] = jnp.zeros_like(db_ref)

        p = pre_ref[...]
        sg = _sigmoid(p)
        dpre = dact_ref[...] * (sg * (1.0 + p * (1.0 - sg)))
        dbuf[tm:tm + HALO, :] = dbuf[0:HALO, :]
        dbuf[0:tm, :] = dpre
        has_prev = (i < nb - 1).astype(f32)
        xbuf[0:HALO, :] = halo_ref[...] * has_prev
        xbuf[HALO:HALO + tm, :] = x_ref[...]
        dx = jnp.zeros((tm, MIX_W), f32)
        for kk in range(CONV_K):
            back = CONV_K - 1 - kk
            dx = dx + w_ref[kk:kk + 1, :] * dbuf[back:back + tm, :]
            off = HALO - (CONV_K - 1) + kk
            dw_ref[kk:kk + 1, :] += jnp.sum(dpre * xbuf[off:off + tm, :], axis=0, keepdims=True)
        dx_ref[...] = dx.astype(ACT)
        db_ref[...] += jnp.sum(dpre, axis=0, keepdims=True)

    return pl.pallas_call(
        body, name="conv_bwd", grid=(nb,),
        in_specs=[_rows_rev(tm, MIX_W, nb),
                  pl.BlockSpec((HALO, MIX_W), lambda i: (jnp.maximum((nb - 1 - i) * hb - 1, 0), 0)),
                  _const((CONV_K, MIX_W)), _rows_rev(tm, MIX_W, nb), _rows_rev(tm, MIX_W, nb)],
        out_specs=[_rows_rev(tm, MIX_W, nb), _const((CONV_K, MIX_W)), _const((1, MIX_W))],
        out_shape=[jax.ShapeDtypeStruct((t, MIX_W), ACT), jax.ShapeDtypeStruct((CONV_K, MIX_W), f32), jax.ShapeDtypeStruct((1, MIX_W), f32)],
        scratch_shapes=[pltpu.VMEM((tm + HALO, MIX_W), f32), pltpu.VMEM((tm + HALO, MIX_W), f32)],
        compiler_params=_cparams(1),
    )(u_ml, u_ml, w, pre, dact)


def _lane_pick(x, lane):
    idx = lax.broadcasted_iota(jnp.int32, x.shape, 1)
    return jnp.sum(jnp.where(idx == lane, x, 0.0), axis=-1, keepdims=True)


def _ml_gate_forms(gates, tri):
    lf = _log_sigmoid(gates)
    gc = _sel_dot(tri, lf)
    lane = lax.broadcasted_iota(jnp.int32, gates.shape, 1)
    mixed = jnp.where(lane < HEADS, gates, gc)
    sel = (lax.broadcasted_iota(jnp.int32, (8, 128), 0) == lax.broadcasted_iota(jnp.int32, (8, 128), 1)).astype(f32)
    rowsf = _sel_dot_nt(sel, mixed)
    return gc, rowsf


def _ml_chunk(q, k, v, gates, gc, rowsf, c_st, n_st, m_st, tril_mask):
    hs = range(HEADS)
    g_col = [_lane_pick(gc, HEADS + h) for h in hs]
    ig_col = [_lane_pick(gates, h) for h in hs]
    dmat = [jnp.where(tril_mask, g_col[h] - rowsf[HEADS + h:HEADS + h + 1, :] + rowsf[h:h + 1, :], NEG) for h in hs]
    m_inter = [g_col[h] + m_st[h] for h in hs]
    m_t = [jnp.maximum(m_inter[h], jnp.max(dmat[h], axis=-1, keepdims=True)) for h in hs]
    wi = [jnp.exp(dmat[h] - m_t[h]) for h in hs]
    wo = [jnp.exp(m_inter[h] - m_t[h]) for h in hs]
    qk = [_bdot_nt(q[h], k[h]) * wi[h] for h in hs]
    num = [_bdot(qk[h], v[h]) + wo[h] * _bdot(q[h], c_st[h]) for h in hs]
    den = [jnp.sum(qk[h], axis=-1, keepdims=True) + wo[h] * jnp.sum(q[h] * n_st[h], axis=-1, keepdims=True) for h in hs]
    floor = [jnp.exp(-m_t[h]) for h in hs]
    z = [jnp.maximum(jnp.abs(den[h]), floor[h]) for h in hs]
    g_last = [g_col[h][CHUNK - 1:CHUNK] for h in hs]
    a_col = [g_last[h] - g_col[h] + ig_col[h] for h in hs]
    m_new = [jnp.maximum(g_last[h] + m_st[h], jnp.max(a_col[h], axis=0, keepdims=True)) for h in hs]
    ws = [jnp.exp(a_col[h] - m_new[h]) for h in hs]
    w_old = [jnp.exp(g_last[h] + m_st[h] - m_new[h]) for h in hs]
    return dict(wi=wi, wo=wo, qk=qk, num=num, den=den, z=z, floor=floor, ws=ws, w_old=w_old, m_new=m_new)


def _mlstm_fwd(qkc, u_ml, gn, riders=()):
    t = qkc.shape[0]
    tb = _tile(t, 256)
    nc_blk = tb // CHUNK

    def body(qk_ref, v_ref, mo_ref, gt_ref, gn_ref, og_ref, cst_ref, nst_ref, mst_ref, c_sc, n_sc, m_sc):
        @pl.when(pl.program_id(0) == 0)
        def _():
            c_sc[...] = jnp.zeros_like(c_sc)
            n_sc[...] = jnp.zeros_like(n_sc)
            m_sc[...] = jnp.zeros_like(m_sc)

        tril_mask = _tri(CHUNK)
        tri = tril_mask.astype(f32)

        def chunk(c, carry):
            r0 = pl.multiple_of(c * CHUNK, CHUNK)
            rows = pl.ds(r0, CHUNK)
            gates = gt_ref[rows, :]
            gc, rowsf = _ml_gate_forms(gates, tri)
            hs = range(HEADS)
            q = [qk_ref[rows, h * ML_DQK:(h + 1) * ML_DQK] * (ML_DQK ** -0.5) for h in hs]
            k = [qk_ref[rows, HEADS * ML_DQK + h * ML_DQK:HEADS * ML_DQK + (h + 1) * ML_DQK] for h in hs]
            v = [v_ref[rows, h * HEAD_W:(h + 1) * HEAD_W] for h in hs]
            c_st = [c_sc[h] for h in hs]
            n_st = [n_sc[h] for h in hs]
            m_full = [m_sc[h] for h in hs]
            r = _ml_chunk(q, k, v, gates, gc, rowsf, c_st, n_st, [m[:, 0:1] for m in m_full], tril_mask)
            ksc = [k[h] * r["ws"][h] for h in hs]
            new_c = [r["w_old"][h] * c_st[h] + _bdot_tn(ksc[h], v[h]) for h in hs]
            for h in hs:
                cs = slice(h * HEAD_W, (h + 1) * HEAD_W)
                cst_ref[c, h] = c_st[h]
                nst_ref[c, h] = n_st[h]
                mst_ref[c, h] = m_full[h]
                c_sc[h] = new_c[h]
                n_sc[h] = r["w_old"][h] * n_st[h] + jnp.sum(ksc[h], axis=0, keepdims=True)
                m_sc[h] = r["m_new"][h] + jnp.zeros((1, 128), f32)
                _, _, y = _head_rms(r["num"][h] / r["z"][h], gn_ref[:, cs])
                og_ref[rows, cs] = (y * _sigmoid(mo_ref[rows, h * HEAD_W:(h + 1) * HEAD_W])).astype(ACT)
            return carry

        lax.fori_loop(0, nc_blk, chunk, 0, unroll=True)

    nchunks = t // CHUNK
    return _riding_call(
        body, "mlstm_fwd", t // tb,
        in_specs=[_rows(tb, MIX_W), _rows(tb, MIX_W, 1), _rows(tb, MIX_W, 2), _rows(tb, 128, 12), _const((1, MIX_W))],
        out_specs=[_rows(tb, MIX_W),
                   pl.BlockSpec((nc_blk, HEADS, ML_DQK, HEAD_W), lambda i: (i, 0, 0, 0)),
                   pl.BlockSpec((nc_blk, HEADS, 1, ML_DQK), lambda i: (i, 0, 0, 0)),
                   pl.BlockSpec((nc_blk, HEADS, 1, 128), lambda i: (i, 0, 0, 0))],
        out_shape=[jax.ShapeDtypeStruct((t, MIX_W), ACT),
                   jax.ShapeDtypeStruct((nchunks, HEADS, ML_DQK, HEAD_W), f32),
                   jax.ShapeDtypeStruct((nchunks, HEADS, 1, ML_DQK), f32),
                   jax.ShapeDtypeStruct((nchunks, HEADS, 1, 128), f32)],
        scratch_shapes=[pltpu.VMEM((HEADS, ML_DQK, HEAD_W), f32), pltpu.VMEM((HEADS, 1, ML_DQK), f32), pltpu.VMEM((HEADS, 1, 128), f32)],
        operands=(qkc, u_ml, u_ml, u_ml, gn), riders=riders, copies=_gather_copies, ride_shapes=_gather_shapes(riders))


def _mlstm_bwd(qkc, u_ml, gn, cst, nst, mst, dog):
    t = qkc.shape[0]
    tb = _tile(t, 256)
    nb = t // tb
    nc_blk = tb // CHUNK

    def body(qk_ref, v_ref, mo_ref, gt_ref, gn_ref, cst_ref, nst_ref, mst_ref, dog_ref,
             dqk_ref, dv_ref, dmo_ref, dgt_ref, dgn_ref, dc_sc, dn_sc):
        @pl.when(pl.program_id(0) == 0)
        def _():
            dc_sc[...] = jnp.zeros_like(dc_sc)
            dn_sc[...] = jnp.zeros_like(dn_sc)
            dgn_ref[...] = jnp.zeros_like(dgn_ref)

        tril_mask = _tri(CHUNK)
        tri = tril_mask.astype(f32)
        triu = _tri(CHUNK, upper=True).astype(f32)
        lane = lax.broadcasted_iota(jnp.int32, (CHUNK, 128), 1)

        def chunk(j, carry):
            c = nc_blk - 1 - j
            r0 = pl.multiple_of(c * CHUNK, CHUNK)
            rows = pl.ds(r0, CHUNK)
            gates = gt_ref[rows, :]
            gc, rowsf = _ml_gate_forms(gates, tri)
            dg_mat = jnp.zeros((CHUNK, 128), f32)
            dig_mat = jnp.zeros((CHUNK, 128), f32)
            dlast_row = jnp.zeros((1, 128), f32)
            hs = range(HEADS)
            cols = [slice(h * HEAD_W, (h + 1) * HEAD_W) for h in hs]
            q = [qk_ref[rows, h * ML_DQK:(h + 1) * ML_DQK] * (ML_DQK ** -0.5) for h in hs]
            k = [qk_ref[rows, HEADS * ML_DQK + h * ML_DQK:HEADS * ML_DQK + (h + 1) * ML_DQK] for h in hs]
            v = [v_ref[rows, h * HEAD_W:(h + 1) * HEAD_W] for h in hs]
            c_st = [cst_ref[c, h] for h in hs]
            n_st = [nst_ref[c, h] for h in hs]
            m_st = [mst_ref[c, h][:, 0:1] for h in hs]
            dc = [dc_sc[h] for h in hs]
            dn = [dn_sc[h] for h in hs]
            r = _ml_chunk(q, k, v, gates, gc, rowsf, c_st, n_st, m_st, tril_mask)
            z, wi, wo, ws, w_old, den = r["z"], r["wi"], r["wo"], r["ws"], r["w_old"], r["den"]
            hh = [r["num"][h] / z[h] for h in hs]
            dh = []
            for h in hs:
                gnh = gn_ref[:, cols[h]]
                oh, rstd, y = _head_rms(hh[h], gnh)
                sg = _sigmoid(mo_ref[rows, h * HEAD_W:(h + 1) * HEAD_W])
                dogh = dog_ref[rows, cols[h]]
                dy = dogh * sg
                dmo_ref[rows, cols[h]] = (dogh * y * (sg * (1.0 - sg))).astype(ACT)
                dgn_ref[:, cols[h]] += jnp.sum(dy * oh, axis=0, keepdims=True)
                doh = dy * gnh
                dh.append(rstd * (doh - oh * jnp.mean(doh * oh, axis=-1, keepdims=True)))
            dnum = [dh[h] / z[h] for h in hs]
            dz = [-jnp.sum(dh[h] * hh[h], axis=-1, keepdims=True) / z[h] for h in hs]
            dden = [jnp.where(jnp.abs(den[h]) > r["floor"][h], dz[h] * jnp.sign(den[h]), 0.0) for h in hs]
            dsw = [(_bdot_nt(dnum[h], v[h]) + dden[h]) * wi[h] for h in hs]
            dq = [_bdot(dsw[h], k[h]) + wo[h] * (_bdot_nt(dnum[h], c_st[h]) + dden[h] * n_st[h]) for h in hs]
            dk_state = [ws[h] * (_bdot_nt(v[h], dc[h]) + dn[h]) for h in hs]
            dk = [_bdot_tn(dsw[h], q[h]) + dk_state[h] for h in hs]
            dv = [_bdot_tn(r["qk"][h], dnum[h]) + ws[h] * _bdot(k[h], dc[h]) for h in hs]
            woq = [wo[h] * q[h] for h in hs]
            new_dc = [w_old[h] * dc[h] + _bdot_tn(woq[h], dnum[h]) for h in hs]
            for h in hs:
                dv_ref[rows, cols[h]] = dv[h].astype(ACT)
                dc_sc[h] = new_dc[h]
                dn_sc[h] = w_old[h] * dn[h] + jnp.sum(woq[h] * dden[h], axis=0, keepdims=True)
                d_last = (jnp.sum(jnp.sum(k[h] * dk_state[h], axis=-1, keepdims=True), axis=0, keepdims=True)
                          + w_old[h] * (jnp.sum(jnp.sum(dc[h] * c_st[h], axis=-1, keepdims=True), axis=0, keepdims=True)
                                        + jnp.sum(dn[h] * n_st[h], axis=-1, keepdims=True)))
                kdk = jnp.sum(k[h] * dk[h], axis=-1, keepdims=True)
                qdq = jnp.sum(q[h] * dq[h], axis=-1, keepdims=True)
                dg_mat = dg_mat + jnp.where(lane == HEADS + h, qdq - kdk, 0.0)
                dlast_row = dlast_row + jnp.where(lane[0:1] == HEADS + h, d_last, 0.0)
                dig_mat = dig_mat + jnp.where(lane == h, kdk, 0.0)
                dqk_ref[rows, h * ML_DQK:(h + 1) * ML_DQK] = dq[h] * (ML_DQK ** -0.5)
                dqk_ref[rows, HEADS * ML_DQK + h * ML_DQK:HEADS * ML_DQK + (h + 1) * ML_DQK] = dk[h]
            dlf = _sel_dot(triu, dg_mat) + dlast_row
            dgt_ref[rows, :] = (dig_mat + dlf * _sigmoid(-gates)).astype(ACT)
            return carry

        lax.fori_loop(0, nc_blk, chunk, 0)

    st4 = lambda a, b: pl.BlockSpec((nc_blk, HEADS, a, b), lambda i: (nb - 1 - i, 0, 0, 0))
    return pl.pallas_call(
        body, name="mlstm_bwd", grid=(nb,),
        in_specs=[_rows_rev(tb, MIX_W, nb), _rows_rev(tb, MIX_W, nb, 1), _rows_rev(tb, MIX_W, nb, 2), _rows_rev(tb, 128, nb, 12),
                  _const((1, MIX_W)), st4(ML_DQK, HEAD_W), st4(1, ML_DQK), st4(1, 128), _rows_rev(tb, MIX_W, nb)],
        out_specs=[_rows_rev(tb, MIX_W, nb), _rows_rev(tb, MIX_W, nb), _rows_rev(tb, MIX_W, nb), _rows_rev(tb, 128, nb), _const((1, MIX_W))],
        out_shape=[jax.ShapeDtypeStruct((t, MIX_W), f32), jax.ShapeDtypeStruct((t, MIX_W), ACT), jax.ShapeDtypeStruct((t, MIX_W), ACT),
                   jax.ShapeDtypeStruct((t, 128), ACT), jax.ShapeDtypeStruct((1, MIX_W), f32)],
        scratch_shapes=[pltpu.VMEM((HEADS, ML_DQK, HEAD_W), f32), pltpu.VMEM((HEADS, 1, ML_DQK), f32)],
        compiler_params=_cparams(1),
    )(qkc, u_ml, u_ml, u_ml, gn, cst, nst, mst, dog)


def _ln_fwd(r, g, b):
    mu = jnp.mean(r, axis=-1, keepdims=True)
    xc = r - mu
    rstd = lax.rsqrt(jnp.mean(xc * xc, axis=-1, keepdims=True) + LN_EPS)
    xh = xc * rstd
    return xh * g + b, xh, rstd


def _ln_bwd(dy, xh, rstd, g):
    dxh = dy * g
    return rstd * (dxh - jnp.mean(dxh, axis=-1, keepdims=True) - xh * jnp.mean(dxh * xh, axis=-1, keepdims=True))


def _outproj_ln1(og_hg, og_ml, x, w_out, g, b):
    t = x.shape[0]
    tm = _tile(t, DENSE_ROWS)

    def body(a_ref, b_ref, x_ref, w_ref, g_ref, bb_ref, x1_ref, xh_ref, rs_ref, x1b_ref):
        mix = _bdot(a_ref[...], w_ref[0:MIX_W, :]) + _bdot(b_ref[...], w_ref[MIX_W:2 * MIX_W, :])
        y, xh, rstd = _ln_fwd(ALPHA * x_ref[...] + mix, g_ref[...], bb_ref[...])
        x1_ref[...] = y
        x1b_ref[...] = y.astype(ACT)
        xh_ref[...] = xh.astype(ACT)
        rs_ref[...] = rstd

    return pl.pallas_call(
        body, name="outproj_ln1", grid=(t // tm,),
        in_specs=[_rows(tm, MIX_W), _rows(tm, MIX_W), _rows(tm, D_MODEL), _resident((D_MODEL, D_MODEL)), _const((1, D_MODEL)), _const((1, D_MODEL))],
        out_specs=[_rows(tm, D_MODEL), _rows(tm, D_MODEL), _rows(tm, 1), _rows(tm, D_MODEL)],
        out_shape=[jax.ShapeDtypeStruct((t, D_MODEL), f32), jax.ShapeDtypeStruct((t, D_MODEL), ACT), jax.ShapeDtypeStruct((t, 1), f32),
                   jax.ShapeDtypeStruct((t, D_MODEL), ACT)],
        compiler_params=_cparams(1, arbitrary=False),
    )(og_hg, og_ml, x, w_out, g, b)


def _ffn_up(x1, wg, wu):
    t = x1.shape[0]
    tm = _tile(t, DENSE_ROWS)

    def body(x_ref, wg_ref, wu_ref, hg_ref, up_ref, a_ref):
        xv = x_ref[...]
        hg = _bdot_nt(xv, wg_ref[...])
        up = _bdot_nt(xv, wu_ref[...])
        hg_ref[...] = hg.astype(ACT)
        up_ref[...] = up.astype(ACT)
        a_ref[...] = (hg * _sigmoid(hg) * up).astype(ACT)

    return pl.pallas_call(
        body, name="ffn_up", grid=(t // tm,),
        in_specs=[_rows(tm, D_MODEL), _resident((D_FF, D_MODEL)), _resident((D_FF, D_MODEL))],
        out_specs=[_rows(tm, D_FF), _rows(tm, D_FF), _rows(tm, D_FF)],
        out_shape=[jax.ShapeDtypeStruct((t, D_FF), ACT), jax.ShapeDtypeStruct((t, D_FF), ACT), jax.ShapeDtypeStruct((t, D_FF), ACT)],
        compiler_params=_cparams(1, arbitrary=False),
    )(x1, wg, wu)


def _ffn_down_ln2(a, x1, wd, g, b):
    t = x1.shape[0]
    tm = _tile(t, DENSE_ROWS)

    def body(a_ref, x_ref, w_ref, g_ref, bb_ref, x2_ref, xh_ref, rs_ref, x2b_ref):
        ffn = _bdot(a_ref[...], w_ref[...])
        y, xh, rstd = _ln_fwd(ALPHA * x_ref[...] + ffn, g_ref[...], bb_ref[...])
        x2_ref[...] = y
        x2b_ref[...] = y.astype(ACT)
        xh_ref[...] = xh.astype(ACT)
        rs_ref[...] = rstd

    return pl.pallas_call(
        body, name="ffn_down_ln2", grid=(t // tm,),
        in_specs=[_rows(tm, D_FF), _rows(tm, D_MODEL), _resident((D_FF, D_MODEL)), _const((1, D_MODEL)), _const((1, D_MODEL))],
        out_specs=[_rows(tm, D_MODEL), _rows(tm, D_MODEL), _rows(tm, 1), _rows(tm, D_MODEL)],
        out_shape=[jax.ShapeDtypeStruct((t, D_MODEL), f32), jax.ShapeDtypeStruct((t, D_MODEL), ACT), jax.ShapeDtypeStruct((t, 1), f32),
                   jax.ShapeDtypeStruct((t, D_MODEL), ACT)],
        compiler_params=_cparams(1, arbitrary=False),
    )(a, x1, wd, g, b)


def _head_loss_bwd(x2, xh2, rs2, p, tgt, w_pg, b_pg, w_pp, g2):
    t = x2.shape[0]
    tm = _tile(t, DENSE_ROWS)

    def body(x_ref, xh_ref, rs_ref, p_ref, t_ref, wg_ref, bg_ref, wp_ref, g_ref,
             dr_ref, de_ref, dz_ref, loss_ref, dbg_ref, dg2_ref, db2_ref):
        @pl.when(pl.program_id(0) == 0)
        def _():
            loss_ref[...] = jnp.zeros_like(loss_ref)
            dbg_ref[...] = jnp.zeros_like(dbg_ref)
            dg2_ref[...] = jnp.zeros_like(dg2_ref)
            db2_ref[...] = jnp.zeros_like(db2_ref)

        x2v = x_ref[...]
        z = _bdot(x2v, wg_ref[...]) + bg_ref[...]
        e = _bdot(p_ref[...], wp_ref[...])
        sg = _sigmoid(z)
        diff = x2v + sg * e - t_ref[...]
        loss_ref[...] += 0.5 * jnp.sum(jnp.mean(diff * diff, axis=-1, keepdims=True), axis=0, keepdims=True)
        dy = diff * (1.0 / D_MODEL)
        de_ref[...] = (dy * sg).astype(ACT)
        dz = dy * e * (sg * (1.0 - sg))
        dz_ref[...] = dz.astype(ACT)
        dbg_ref[...] += jnp.sum(dz, axis=0, keepdims=True)
        dx2 = dy + _bdot_nt(dz, wg_ref[...])
        xh = xh_ref[...].astype(f32)
        dg2_ref[...] += jnp.sum(dx2 * xh, axis=0, keepdims=True)
        db2_ref[...] += jnp.sum(dx2, axis=0, keepdims=True)
        dr_ref[...] = _ln_bwd(dx2, xh, rs_ref[...], g_ref[...])

    row = jax.ShapeDtypeStruct((1, D_MODEL), f32)
    return pl.pallas_call(
        body, name="head_loss_bwd", grid=(t // tm,),
        in_specs=[_rows(tm, D_MODEL), _rows(tm, D_MODEL), _rows(tm, 1), _rows(tm, PLE), _rows(tm, D_MODEL),
                  _resident((D_MODEL, D_MODEL)), _const((1, D_MODEL)), _resident((PLE, D_MODEL)), _const((1, D_MODEL))],
        out_specs=[_rows(tm, D_MODEL), _rows(tm, D_MODEL), _rows(tm, D_MODEL), _const((1, 1)), _const((1, D_MODEL)), _const((1, D_MODEL)), _const((1, D_MODEL))],
        out_shape=[jax.ShapeDtypeStruct((t, D_MODEL), f32), jax.ShapeDtypeStruct((t, D_MODEL), ACT), jax.ShapeDtypeStruct((t, D_MODEL), ACT),
                   jax.ShapeDtypeStruct((1, 1), f32), row, row, row],
        compiler_params=_cparams(1),
    )(x2, xh2, rs2, p, tgt, w_pg, b_pg, w_pp, g2)


def _ffn_bwd(dr2, hg, up, xh1, rs1, wd, wg, wu, g1, w_out):
    t = dr2.shape[0]
    tm = _tile(t, DENSE_ROWS // 2)

    def body(dr_ref, hg_ref, up_ref, xh_ref, rs_ref, wd_ref, wg_ref, wu_ref, g_ref, wo_ref,
             dr1_ref, dhg_ref, dup_ref, dg1_ref, db1_ref, doghg_ref, dogml_ref):
        @pl.when(pl.program_id(0) == 0)
        def _():
            dg1_ref[...] = jnp.zeros_like(dg1_ref)
            db1_ref[...] = jnp.zeros_like(db1_ref)

        dr2v = dr_ref[...]
        da = _bdot_nt(dr2v, wd_ref[...])
        hgv = hg_ref[...].astype(f32)
        sg = _sigmoid(hgv)
        dhg = da * up_ref[...].astype(f32) * (sg * (1.0 + hgv * (1.0 - sg)))
        dup = da * (hgv * sg)
        dhg_ref[...] = dhg.astype(ACT)
        dup_ref[...] = dup.astype(ACT)
        dx1 = ALPHA * dr2v + _bdot(dhg, wg_ref[...]) + _bdot(dup, wu_ref[...])
        xh = xh_ref[...].astype(f32)
        dg1_ref[...] += jnp.sum(dx1 * xh, axis=0, keepdims=True)
        db1_ref[...] += jnp.sum(dx1, axis=0, keepdims=True)
        dr1 = _ln_bwd(dx1, xh, rs_ref[...], g_ref[...])
        dr1_ref[...] = dr1
        dog = _bdot_nt(dr1, wo_ref[...])
        doghg_ref[...] = dog[:, 0:MIX_W]
        dogml_ref[...] = dog[:, MIX_W:2 * MIX_W]

    row = jax.ShapeDtypeStruct((1, D_MODEL), f32)
    return pl.pallas_call(
        body, name="ffn_bwd", grid=(t // tm,),
        in_specs=[_rows(tm, D_MODEL), _rows(tm, D_FF), _rows(tm, D_FF), _rows(tm, D_MODEL), _rows(tm, 1),
                  _resident((D_FF, D_MODEL)), _resident((D_FF, D_MODEL)), _resident((D_FF, D_MODEL)), _const((1, D_MODEL)),
                  _resident((D_MODEL, D_MODEL))],
        out_specs=[_rows(tm, D_MODEL), _rows(tm, D_FF), _rows(tm, D_FF), _const((1, D_MODEL)), _const((1, D_MODEL)),
                   _rows(tm, MIX_W), _rows(tm, MIX_W)],
        out_shape=[jax.ShapeDtypeStruct((t, D_MODEL), f32), jax.ShapeDtypeStruct((t, D_FF), ACT), jax.ShapeDtypeStruct((t, D_FF), ACT), row, row,
                   jax.ShapeDtypeStruct((t, MIX_W), f32), jax.ShapeDtypeStruct((t, MIX_W), f32)],
        compiler_params=_cparams(1),
    )(dr2, hg, up, xh1, rs1, wd, wg, wu, g1, w_out)


def _inproj_bwd(dr1, du_hg, dqk, dmv, dmo, dgt, w_hg, w_ml):
    t = dr1.shape[0]
    tm = _tile(t, DENSE_ROWS)

    def body(dr_ref, dhg_ref, dqk_ref, dmv_ref, dmo_ref, dgt_ref, whg_ref, wml_ref, gx_ref, dml_ref):
        dml = jnp.concatenate([dqk_ref[...], dmv_ref[...], dmo_ref[...], dgt_ref[...]], axis=-1).astype(ACT)
        dml_ref[...] = dml
        gx_ref[...] = ALPHA * dr_ref[...] + _bdot(dhg_ref[...], whg_ref[...]) + _bdot(dml, wml_ref[...])

    return pl.pallas_call(
        body, name="inproj_bwd", grid=(t // tm,),
        in_specs=[_rows(tm, D_MODEL), _rows(tm, U_HG), _rows(tm, MIX_W), _rows(tm, MIX_W), _rows(tm, MIX_W), _rows(tm, 128),
                  _resident((U_HG, D_MODEL)), _resident((U_ML, D_MODEL))],
        out_specs=[_rows(tm, D_MODEL), _rows(tm, U_ML)],
        out_shape=[jax.ShapeDtypeStruct((t, D_MODEL), f32), jax.ShapeDtypeStruct((t, U_ML), ACT)],
        compiler_params=_cparams(1, arbitrary=False),
    )(dr1, du_hg, dqk, dmv, dmo, dgt, w_hg, w_ml)


def _wgrad(a, b, name, tk=None, tn=None, colsum=False):
    t, kdim = a.shape
    n = b.shape[1]
    tk = tk or kdim
    tn = tn or n
    tt = _tile(t, WGRAD_ROWS)
    assert not colsum or tn == n

    def body(a_ref, b_ref, o_ref, *s_ref):
        @pl.when(pl.program_id(2) == 0)
        def _():
            o_ref[...] = jnp.zeros_like(o_ref)
            if colsum:
                s_ref[0][...] = jnp.zeros_like(s_ref[0])

        av = a_ref[...]
        o_ref[...] += _bdot_tn(av, b_ref[...])
        if colsum:
            s_ref[0][...] += jnp.sum(av.astype(f32), axis=0, keepdims=True)

    out_specs = [pl.BlockSpec((tk, tn), lambda i, j, s: (i, j))]
    out_shape = [jax.ShapeDtypeStruct((kdim, n), f32)]
    if colsum:
        out_specs.append(pl.BlockSpec((1, tk), lambda i, j, s: (0, i)))
        out_shape.append(jax.ShapeDtypeStruct((1, kdim), f32))
    res = pl.pallas_call(
        body, name=name, grid=(kdim // tk, n // tn, t // tt),
        in_specs=[pl.BlockSpec((tt, tk), lambda i, j, s: (s, i)), pl.BlockSpec((tt, tn), lambda i, j, s: (s, j))],
        out_specs=out_specs, out_shape=out_shape,
        compiler_params=_cparams(3),
    )(a, b)
    return res if colsum else res[0]


def _colsum(parts, name):
    t = parts[0].shape[0]
    tt = _tile(t, 512)
    widths = [a.shape[1] for a in parts]

    def body(*refs):
        o_ref = refs[-1]

        @pl.when(pl.program_id(0) == 0)
        def _():
            o_ref[...] = jnp.zeros_like(o_ref)

        off = 0
        for r, w in zip(refs[:-1], widths):
            o_ref[:, off:off + w] += jnp.sum(r[...].astype(f32), axis=0, keepdims=True)
            off += w

    return pl.pallas_call(
        body, name=name, grid=(t // tt,),
        in_specs=[_rows(tt, w) for w in widths],
        out_specs=_const((1, sum(widths))),
        out_shape=jax.ShapeDtypeStruct((1, sum(widths)), f32),
        compiler_params=_cparams(1),
    )(*parts)


_TRANSPOSED = {"w_in", "w_ffn_gate", "w_ffn_up"}
_COL_SPLIT = {"ple_w_proj"}
_RIDE_PLAN = (("w_out", "ple_w_gate", "ple_w_proj"), ("w_ffn_down",), ("w_ffn_gate", "w_ffn_up"))


def _from_chip_major(a, col_split):
    if col_split:
        return a.transpose(1, 0, 2).reshape(a.shape[1], 4 * a.shape[2])
    return a.reshape(4 * a.shape[1], a.shape[2])


def _local_step(x, p, tgt, w_in_b, b_in, logits, conv_w, conv_b, hg_gn, ml_gn, w_out_b, ln1_g, ln1_b,
                wg_b, wu_b, wd_b, ln2_g, ln2_b, w_pp_b, w_pg_b, b_pg, early_hook=None, late_shards=None):
    pad_w = U_HG + U_ML - PROJ_W
    w_hg = w_in_b[:U_HG]
    w_ml = jnp.pad(w_in_b[U_HG:], ((0, pad_w), (0, 0)))
    bb_hg = b_in[:, :U_HG]
    bb_ml = jnp.pad(b_in[:, U_HG:], ((0, 0), (0, pad_w)))

    ride = [[late_shards[k] for k in names] for names in _RIDE_PLAN] if late_shards is not None else [(), (), ()]
    (u_hg, u_ml, xb), got0 = _inproj(x, w_hg, w_ml, bb_hg, bb_ml, ride[0])
    (og_hg, sst), got1 = _hgrn2_fwd(u_hg, logits, hg_gn, ride[1])
    pre, qkc = _conv_fwd(u_ml, conv_w, conv_b)
    (og_ml, cst, nst, mst), got2 = _mlstm_fwd(qkc, u_ml, ml_gn, ride[2])
    if late_shards is not None:
        late = {k: _from_chip_major(g, k in _COL_SPLIT) for names, got in zip(_RIDE_PLAN, (got0, got1, got2)) for k, g in zip(names, got)}
        w_out_b, wg_b, wu_b, wd_b = late["w_out"], late["w_ffn_gate"], late["w_ffn_up"], late["w_ffn_down"]
        w_pp_b, w_pg_b = late["ple_w_proj"], late["ple_w_gate"]
    x1, xh1, rs1, x1b = _outproj_ln1(og_hg, og_ml, x, w_out_b, ln1_g, ln1_b)
    hgp, up, act = _ffn_up(x1b, wg_b, wu_b)
    x2, xh2, rs2, x2b = _ffn_down_ln2(act, x1, wd_b, ln2_g, ln2_b)
    dr2, de, dz, loss, d_bpg, d_ln2g, d_ln2b = _head_loss_bwd(x2, xh2, rs2, p, tgt, w_pg_b, b_pg, w_pp_b, ln2_g)
    dr1, dhg, dup, d_ln1g, d_ln1b, dog_hg, dog_ml = _ffn_bwd(dr2, hgp, up, xh1, rs1, wd_b, wg_b, wu_b, ln1_g, w_out_b)

    d_wo_a = _wgrad(og_hg, dr1, "wgrad_out_hg")
    d_wo_b = _wgrad(og_ml, dr1, "wgrad_out_ml")
    d_w_out = jnp.concatenate([d_wo_a, d_wo_b], axis=0)
    d_wg = _wgrad(dhg, x1b, "wgrad_ffn_gate", tk=D_FF // 2)
    d_wu = _wgrad(dup, x1b, "wgrad_ffn_up", tk=D_FF // 2)
    d_wd = _wgrad(act, dr2, "wgrad_ffn_down", tk=D_FF // 2)
    d_wpp = _wgrad(p, de, "wgrad_ple_proj")
    d_wpg = _wgrad(x2b, dz, "wgrad_ple_gate")
    early = dict(w_out=d_w_out, w_ffn_gate=d_wg, w_ffn_up=d_wu, w_ffn_down=d_wd, ple_w_proj=d_wpp, ple_w_gate=d_wpg)
    riders = early_hook(early) if early_hook is not None else ()

    res = _hgrn2_bwd(u_hg, logits, hg_gn, sst, dog_hg, riders)
    du_hg, d_logits, d_hg_gn = res[:3]
    dqkc, dmv, dmo, dgt, d_ml_gn = _mlstm_bwd(qkc, u_ml, ml_gn, cst, nst, mst, dog_ml)
    dqk, d_conv_w, d_conv_b = _conv_bwd(u_ml, conv_w, pre, dqkc)
    grad_x, du_ml = _inproj_bwd(dr1, du_hg, dqk, dmv, dmo, dgt, w_hg, w_ml)

    dw_hg, db_hg = _wgrad(du_hg, xb, "wgrad_in_hg", tk=U_HG // 2, colsum=True)
    dw_ml, db_ml = _wgrad(du_ml, xb, "wgrad_in_ml", colsum=True)
    d_w_in = jnp.concatenate([dw_hg, dw_ml[:PROJ_W - U_HG]], axis=0)
    d_b_in = jnp.concatenate([db_hg, db_ml[:, :PROJ_W - U_HG]], axis=1)

    grads = dict(w_in=d_w_in, b_in=d_b_in, hg_lb_logits=d_logits, ml_conv_w=d_conv_w, ml_conv_b=d_conv_b,
                 hg_norm_g=d_hg_gn, ml_norm_g=d_ml_gn, ln1_g=d_ln1g, ln1_b=d_ln1b, ln2_g=d_ln2g, ln2_b=d_ln2b,
                 ple_b_gate=d_bpg, **early)
    return loss, grad_x, grads, list(res[3:])


_ANY = pl.BlockSpec(memory_space=pltpu.HBM)
_MESH = pl.DeviceIdType.MESH


def _my_place():
    return lax.axis_index("x"), lax.axis_index("y"), lax.axis_index("c")


def _other_chips(x, y):
    return [(1 - x, y), (x, 1 - y), (1 - x, 1 - y)]


def _allgather_weights(shards, taps, name):
    n = len(shards)
    halves = [s.shape[0] // 2 for s in shards]

    def body(*refs):
        ins, tap_in = refs[:n], refs[n]
        outs, tap_out = refs[n + 1:2 * n + 1], refs[2 * n + 1]
        send_sems, recv_sems, local_sems = refs[2 * n + 2:]
        x, y, c = _my_place()
        me = 2 * x + y
        sibling = (x, y, 1 - c)
        chips = _other_chips(x, y)

        def ici(a, j, block_chip):
            px, py = chips[j]
            src = ins[a].at[pl.ds(pl.multiple_of(c * halves[a], 16), halves[a])] if block_chip is None else outs[a].at[block_chip, c]
            dst = outs[a].at[me if block_chip is None else block_chip, c]
            return pltpu.make_async_remote_copy(src_ref=src, dst_ref=dst, send_sem=send_sems.at[6 * a + j], recv_sem=recv_sems.at[6 * a + j],
                                                device_id=(px, py, c), device_id_type=_MESH)

        def d2d(a, j, half):
            px, py = chips[j]
            blk = outs[a].at[2 * px + py, half]
            return pltpu.make_async_remote_copy(src_ref=blk, dst_ref=blk, send_sem=send_sems.at[6 * a + 3 + j], recv_sem=recv_sems.at[6 * a + 3 + j],
                                                device_id=sibling, device_id_type=_MESH)

        local = []
        for a in range(n):
            for h in range(2):
                cp = pltpu.make_async_copy(ins[a].at[pl.ds(h * halves[a], halves[a])], outs[a].at[me, h], local_sems.at[2 * a + h])
                cp.start()
                local.append(cp)
            for j in range(3):
                ici(a, j, None).start()
        tap_local = pltpu.make_async_copy(tap_in, tap_out.at[me], local_sems.at[2 * n])
        tap_local.start()
        tap_copies = []
        for j, (px, py) in enumerate(chips):
            cp = pltpu.make_async_remote_copy(src_ref=tap_in, dst_ref=tap_out.at[me], send_sem=send_sems.at[6 * n + j], recv_sem=recv_sems.at[6 * n + j],
                                              device_id=(px, py, c), device_id_type=_MESH)
            cp.start()
            tap_copies.append(cp)
        for a in range(n):
            for j, (px, py) in enumerate(chips):
                ici(a, j, 2 * px + py).wait_recv()
                d2d(a, j, c).start()
        for a in range(n):
            for j in range(3):
                d2d(a, j, 1 - c).wait_recv()
        for a in range(n):
            for j in range(3):
                ici(a, j, None).wait_send()
                d2d(a, j, c).wait_send()
        for j, (px, py) in enumerate(chips):
            pltpu.make_async_remote_copy(src_ref=tap_in, dst_ref=tap_out.at[2 * px + py], send_sem=send_sems.at[6 * n + j], recv_sem=recv_sems.at[6 * n + j],
                                         device_id=(px, py, c), device_id_type=_MESH).wait()
        for cp in local:
            cp.wait()
        tap_local.wait()

    res = pl.pallas_call(
        body, name=name,
        in_specs=[_ANY] * (n + 1), out_specs=[_ANY] * (n + 1),
        out_shape=[jax.ShapeDtypeStruct((4, 2, s.shape[0] // 2, s.shape[1]), s.dtype) for s in shards]
        + [jax.ShapeDtypeStruct((4,) + taps.shape, taps.dtype)],
        scratch_shapes=[pltpu.SemaphoreType.DMA((6 * n + 3,)), pltpu.SemaphoreType.DMA((6 * n + 3,)), pltpu.SemaphoreType.DMA((2 * n + 1,))],
    )(*shards, taps)
    return [w.reshape((4,) + s.shape) for w, s in zip(res[:n], shards)], res[n]


def _swap_halves(pieces, name):
    n = len(pieces)
    halves = [p.shape[1] // 2 for p in pieces]

    def body(*refs):
        ins, own, other = refs[:n], refs[n:2 * n], refs[2 * n:3 * n]
        send_sems, recv_sems, local_sems = refs[3 * n:]
        x, y, c = _my_place()

        def half_of(a, which):
            return ins[a].at[pl.ds(0, 4), pl.ds(pl.multiple_of(which * halves[a], 16), halves[a])]

        def to_sibling(a):
            return pltpu.make_async_remote_copy(src_ref=half_of(a, 1 - c), dst_ref=other[a], send_sem=send_sems.at[a], recv_sem=recv_sems.at[a],
                                                device_id=(x, y, 1 - c), device_id_type=_MESH)

        local = []
        for a in range(n):
            cp = pltpu.make_async_copy(half_of(a, c), own[a], local_sems.at[a])
            cp.start()
            local.append(cp)
            to_sibling(a).start()
        for a in range(n):
            to_sibling(a).wait()
            local[a].wait()

    shapes = [jax.ShapeDtypeStruct((4, p.shape[1] // 2, p.shape[2]), p.dtype) for p in pieces]
    res = pl.pallas_call(
        body, name=name,
        in_specs=[_ANY] * n, out_specs=[_ANY] * (2 * n), out_shape=shapes + shapes,
        scratch_shapes=[pltpu.SemaphoreType.DMA((n,)), pltpu.SemaphoreType.DMA((n,)), pltpu.SemaphoreType.DMA((n,))],
    )(*pieces)
    return res[:n], res[n:]


_VMEM = pl.BlockSpec(memory_space=pltpu.VMEM)
_EX_ROWS = 32


def _pair_reduce(p, name):
    s, r, c = p.shape
    half = r // 2

    def body(p_ref, o_ref, other, send_sem, recv_sem):
        x, y, cc = _my_place()
        theirs = pl.multiple_of((1 - cc) * half, 16)
        mine = pl.multiple_of(cc * half, 16)
        cp = pltpu.make_async_remote_copy(src_ref=p_ref.at[pl.ds(0, s), pl.ds(theirs, half)], dst_ref=other, send_sem=send_sem, recv_sem=recv_sem,
                                          device_id=(x, y, 1 - cc), device_id_type=_MESH)
        cp.start()
        cp.wait()

        def step(i, carry):
            r0 = pl.multiple_of(i * _EX_ROWS, _EX_ROWS)
            for slot in range(s):
                own_rows = pl.ds(pl.multiple_of(mine + r0, 16), _EX_ROWS)
                o_ref[slot, pl.ds(r0, _EX_ROWS), :] = (p_ref[slot, own_rows, :] + other[slot, pl.ds(r0, _EX_ROWS), :]).astype(bf16)
            return carry

        lax.fori_loop(0, half // _EX_ROWS, step, 0)

    return pl.pallas_call(
        body, name=name, in_specs=[_VMEM], out_specs=_VMEM,
        out_shape=jax.ShapeDtypeStruct((s, half, c), bf16),
        scratch_shapes=[pltpu.VMEM((s, half, c), f32), pltpu.SemaphoreType.DMA, pltpu.SemaphoreType.DMA],
        compiler_params=pltpu.CompilerParams(vmem_limit_bytes=VMEM_LIMIT),
    )(p)


def _chip_reduce_swap(rcv, name):
    s, h, c = rcv.shape

    def body(r_ref, g_ref, send_sem, recv_sem):
        x, y, cc = _my_place()

        def step(i, carry):
            r0 = pl.multiple_of(i * _EX_ROWS, _EX_ROWS)
            acc = r_ref[0, pl.ds(r0, _EX_ROWS), :].astype(f32)
            for slot in range(1, s):
                acc = acc + r_ref[slot, pl.ds(r0, _EX_ROWS), :].astype(f32)
            g_ref[cc, pl.ds(r0, _EX_ROWS), :] = acc
            return carry

        lax.fori_loop(0, h // _EX_ROWS, step, 0)
        cp = pltpu.make_async_remote_copy(src_ref=g_ref.at[cc], dst_ref=g_ref.at[cc], send_sem=send_sem, recv_sem=recv_sem,
                                          device_id=(x, y, 1 - cc), device_id_type=_MESH)
        cp.start()
        cp.wait()

    return pl.pallas_call(
        body, name=name, in_specs=[_VMEM], out_specs=_VMEM,
        out_shape=jax.ShapeDtypeStruct((2, h, c), f32),
        scratch_shapes=[pltpu.SemaphoreType.DMA, pltpu.SemaphoreType.DMA],
        compiler_params=pltpu.CompilerParams(vmem_limit_bytes=VMEM_LIMIT),
    )(rcv)


def _pair_reduce_cols(p, name):
    s, r, c = p.shape
    hc = c // 2

    def body(p_ref, o_ref, other, send_sem, recv_sem):
        x, y, cc = _my_place()

        def run(mine_lo, theirs_lo):
            cp = pltpu.make_async_remote_copy(src_ref=p_ref.at[pl.ds(0, s), pl.ds(0, r), pl.ds(theirs_lo, hc)], dst_ref=other,
                                              send_sem=send_sem, recv_sem=recv_sem, device_id=(x, y, 1 - cc), device_id_type=_MESH)
            cp.start()
            cp.wait()
            for slot in range(s):
                o_ref[slot] = (p_ref[slot, :, mine_lo:mine_lo + hc] + other[slot]).astype(bf16)

        @pl.when(cc == 0)
        def _():
            run(0, hc)

        @pl.when(cc == 1)
        def _():
            run(hc, 0)

    return pl.pallas_call(
        body, name=name, in_specs=[_VMEM], out_specs=_VMEM,
        out_shape=jax.ShapeDtypeStruct((s, r, hc), bf16),
        scratch_shapes=[pltpu.VMEM((s, r, hc), f32), pltpu.SemaphoreType.DMA, pltpu.SemaphoreType.DMA],
        compiler_params=pltpu.CompilerParams(vmem_limit_bytes=VMEM_LIMIT),
    )(p)


def _chip_reduce_swap_cols(rcv, name):
    s, r, hc = rcv.shape

    def body(r_ref, g_ref, send_sem, recv_sem):
        x, y, cc = _my_place()
        acc = r_ref[0].astype(f32)
        for slot in range(1, s):
            acc = acc + r_ref[slot].astype(f32)
        g_ref[cc] = acc
        cp = pltpu.make_async_remote_copy(src_ref=g_ref.at[cc], dst_ref=g_ref.at[cc], send_sem=send_sem, recv_sem=recv_sem,
                                          device_id=(x, y, 1 - cc), device_id_type=_MESH)
        cp.start()
        cp.wait()

    both = pl.pallas_call(
        body, name=name, in_specs=[_VMEM], out_specs=_VMEM,
        out_shape=jax.ShapeDtypeStruct((2, r, hc), f32),
        scratch_shapes=[pltpu.SemaphoreType.DMA, pltpu.SemaphoreType.DMA],
        compiler_params=pltpu.CompilerParams(vmem_limit_bytes=VMEM_LIMIT),
    )(rcv)
    return both.transpose(1, 0, 2).reshape(r, 2 * hc)


def _add_cast(a, b, name):
    s, r, c = a.shape
    tr = _row_tile(r, c)

    def body(a_ref, b_ref, o_ref):
        o_ref[...] = (a_ref[...] + b_ref[...]).astype(bf16)

    blk = pl.BlockSpec((1, tr, c), lambda i, j: (i, j, 0))
    return pl.pallas_call(
        body, name=name, grid=(s, r // tr), in_specs=[blk, blk], out_specs=blk,
        out_shape=jax.ShapeDtypeStruct(a.shape, bf16),
        compiler_params=_cparams(2, arbitrary=False),
    )(a, b)


def _gather_copies(ins, outs, send_sems, recv_sems, local_sems):
    x, y, c = _my_place()
    me = 2 * x + y
    local, outgoing, incoming = [], [], []
    for a in range(len(ins)):
        local.append(pltpu.make_async_copy(ins[a], outs[a].at[me], local_sems.at[a]))
        for j, (px, py) in enumerate(_other_chips(x, y)):
            sems = dict(send_sem=send_sems.at[3 * a + j], recv_sem=recv_sems.at[3 * a + j], device_id=(px, py, c), device_id_type=_MESH)
            outgoing.append(pltpu.make_async_remote_copy(src_ref=ins[a], dst_ref=outs[a].at[me], **sems))
            incoming.append(pltpu.make_async_remote_copy(src_ref=ins[a], dst_ref=outs[a].at[2 * px + py], **sems))
    return local, outgoing, incoming


def _gather_chips(blocks, name):
    n = len(blocks)

    def body(*refs):
        local, outgoing, incoming = _gather_copies(refs[:n], refs[n:2 * n], *refs[2 * n:])
        for cp in local + outgoing:
            cp.start()
        for cp in incoming:
            cp.wait_recv()
        for cp in outgoing:
            cp.wait_send()
        for cp in local:
            cp.wait()

    return pl.pallas_call(
        body, name=name, in_specs=[_ANY] * n, out_specs=[_ANY] * n, out_shape=_gather_shapes(blocks),
        scratch_shapes=[pltpu.SemaphoreType.DMA((3 * n,)), pltpu.SemaphoreType.DMA((3 * n,)), pltpu.SemaphoreType.DMA((n,))],
    )(*blocks)


def _riding_call(body, name, nsteps, in_specs, out_specs, out_shape, scratch_shapes, operands, riders, copies, ride_shapes):
    nr, n_in, n_out, n_scr = len(riders), len(in_specs), len(out_specs), len(scratch_shapes)

    def wrapped(*refs):
        ins, ride_in = refs[:n_in], refs[n_in:n_in + nr]
        outs, ride_out = refs[n_in + nr:n_in + nr + n_out], refs[n_in + nr + n_out:n_in + 2 * nr + n_out]
        scratch, sems = refs[n_in + 2 * nr + n_out:n_in + 2 * nr + n_out + n_scr], refs[n_in + 2 * nr + n_out + n_scr:]
        if nr:
            @pl.when(pl.program_id(0) == 0)
            def _():
                local, outgoing, _ = copies(ride_in, ride_out, *sems)
                for cp in local + outgoing:
                    cp.start()

        body(*ins, *outs, *scratch)
        if nr:
            @pl.when(pl.program_id(0) == nsteps - 1)
            def _():
                local, outgoing, incoming = copies(ride_in, ride_out, *sems)
                for cp in incoming:
                    cp.wait_recv()
                for cp in outgoing:
                    cp.wait_send()
                for cp in local:
                    cp.wait()

    hbm = pl.BlockSpec(memory_space=pltpu.HBM)
    sems = [pltpu.SemaphoreType.DMA((3 * nr,)), pltpu.SemaphoreType.DMA((3 * nr,)), pltpu.SemaphoreType.DMA((nr,))] if nr else []
    res = pl.pallas_call(
        wrapped, name=name, grid=(nsteps,),
        in_specs=list(in_specs) + [hbm] * nr, out_specs=list(out_specs) + [hbm] * nr,
        out_shape=list(out_shape) + list(ride_shapes),
        scratch_shapes=list(scratch_shapes) + sems,
        compiler_params=_cparams(1),
    )(*operands, *riders)
    return list(res[:n_out]), list(res[n_out:])


def _gather_shapes(riders):
    return [jax.ShapeDtypeStruct((4,) + r.shape, r.dtype) for r in riders]


def _scatter_copies(ins, outs, send_sems, recv_sems, local_sems):
    x, y, c = _my_place()
    me = 2 * x + y
    local, outgoing, incoming = [], [], []
    for a in range(len(ins)):
        local.append(pltpu.make_async_copy(ins[a].at[me], outs[a].at[me], local_sems.at[a]))
        for j, (px, py) in enumerate(_other_chips(x, y)):
            sems = dict(send_sem=send_sems.at[3 * a + j], recv_sem=recv_sems.at[3 * a + j], device_id=(px, py, c), device_id_type=_MESH)
            outgoing.append(pltpu.make_async_remote_copy(src_ref=ins[a].at[2 * px + py], dst_ref=outs[a].at[me], **sems))
            incoming.append(pltpu.make_async_remote_copy(src_ref=ins[a].at[2 * px + py], dst_ref=outs[a].at[2 * px + py], **sems))
    return local, outgoing, incoming


def _scatter_start(ins, outs, send_sems, recv_sems, local_sems):
    local, outgoing, _ = _scatter_copies(ins, outs, send_sems, recv_sems, local_sems)
    for cp in local + outgoing:
        cp.start()


def _scatter_wait(ins, outs, send_sems, recv_sems, local_sems):
    local, outgoing, incoming = _scatter_copies(ins, outs, send_sems, recv_sems, local_sems)
    for cp in incoming:
        cp.wait_recv()
    for cp in outgoing:
        cp.wait_send()
    for cp in local:
        cp.wait()


def _scatter_chips(pieces, name):
    n = len(pieces)

    def body(*refs):
        ins, outs = refs[:n], refs[n:2 * n]
        _scatter_start(ins, outs, *refs[2 * n:])
        _scatter_wait(ins, outs, *refs[2 * n:])

    return pl.pallas_call(
        body, name=name,
        in_specs=[_ANY] * n, out_specs=[_ANY] * n,
        out_shape=[jax.ShapeDtypeStruct(s.shape, s.dtype) for s in pieces],
        scratch_shapes=[pltpu.SemaphoreType.DMA((3 * n,)), pltpu.SemaphoreType.DMA((3 * n,)), pltpu.SemaphoreType.DMA((n,))],
    )(*pieces)


def _swap_cores(blocks, name):
    n = len(blocks)
    parts = 4
    rows = [b.shape[0] // parts for b in blocks]

    def body(*refs):
        ins, outs = refs[:n], refs[n:2 * n]
        send_sems, recv_sems, local_sems = refs[2 * n:]
        x, y, c = _my_place()

        def remote(a, k, slot):
            rs = pl.ds(k * rows[a], rows[a])
            return pltpu.make_async_remote_copy(src_ref=ins[a].at[rs], dst_ref=outs[a].at[slot, rs], send_sem=send_sems.at[parts * a + k],
                                                recv_sem=recv_sems.at[parts * a + k], device_id=(x, y, 1 - c), device_id_type=_MESH)

        local = []
        for a in range(n):
            cp = pltpu.make_async_copy(ins[a], outs[a].at[c], local_sems.at[a])
            cp.start()
            local.append(cp)
            for k in range(parts):
                remote(a, k, c).start()
        for a in range(n):
            for k in range(parts):
                remote(a, k, 1 - c).wait()
            local[a].wait()

    return pl.pallas_call(
        body, name=name,
        in_specs=[_ANY] * n, out_specs=[_ANY] * n,
        out_shape=[jax.ShapeDtypeStruct((2,) + s.shape, s.dtype) for s in blocks],
        scratch_shapes=[pltpu.SemaphoreType.DMA((parts * n,)), pltpu.SemaphoreType.DMA((parts * n,)), pltpu.SemaphoreType.DMA((n,))],
    )(*blocks)


def _gather_all(block, name):
    def body(in_ref, out_ref, send_sems, recv_sems, local_sem):
        x, y, c = _my_place()
        me = 4 * x + 2 * y + c
        cp = pltpu.make_async_copy(in_ref, out_ref.at[me], local_sem)
        cp.start()
        peers = []
        for dx in range(2):
            for dy in range(2):
                for dc in range(2):
                    if dx or dy or dc:
                        peers.append((1 - x if dx else x, 1 - y if dy else y, 1 - c if dc else c))
        for j, pr in enumerate(peers):
            pltpu.make_async_remote_copy(src_ref=in_ref, dst_ref=out_ref.at[me], send_sem=send_sems.at[j], recv_sem=recv_sems.at[j],
                                         device_id=pr, device_id_type=_MESH).start()
        for j, (px, py, pc) in enumerate(peers):
            pltpu.make_async_remote_copy(src_ref=in_ref, dst_ref=out_ref.at[4 * px + 2 * py + pc], send_sem=send_sems.at[j], recv_sem=recv_sems.at[j],
                                         device_id=(px, py, pc), device_id_type=_MESH).wait()
        cp.wait()

    return pl.pallas_call(
        body, name=name,
        in_specs=[_ANY], out_specs=_ANY,
        out_shape=jax.ShapeDtypeStruct((8,) + block.shape, block.dtype),
        scratch_shapes=[pltpu.SemaphoreType.DMA((7,)), pltpu.SemaphoreType.DMA((7,)), pltpu.SemaphoreType.DMA],
    )(block)


def _row_tile(r, c):
    best = r
    for cand in range(16, r + 1, 16):
        if r % cand == 0 and cand * c * 4 <= (1 << 20):
            best = cand
    return best if best * c * 4 <= (4 << 20) else r


def _sum_slots(parts, name):
    n, r, c = parts.shape
    tr = _row_tile(r, c)

    def body(p_ref, o_ref):
        acc = p_ref[0].astype(f32)
        for s in range(1, n):
            acc = acc + p_ref[s].astype(f32)
        o_ref[...] = acc

    return pl.pallas_call(
        body, name=name, grid=(r // tr,),
        in_specs=[pl.BlockSpec((n, tr, c), lambda i: (0, i, 0))],
        out_specs=pl.BlockSpec((tr, c), lambda i: (i, 0)),
        out_shape=jax.ShapeDtypeStruct((r, c), f32),
        compiler_params=_cparams(1, arbitrary=False),
    )(parts)


def _adamw(parts, w, m, v, name):
    n, r, c = parts.shape
    tr = _row_tile(r, c)
    tc = c
    if tr == r and r * c * 4 > (1 << 20) and c % 256 == 0:
        tc = 256

    def body(p_ref, w_ref, m_ref, v_ref, g_ref, d_ref, nm_ref, nv_ref):
        g = p_ref[0]
        for s in range(1, n):
            g = g + p_ref[s]
        nm = B1 * m_ref[...] + (1.0 - B1) * g
        nv = B2 * v_ref[...] + (1.0 - B2) * (g * g)
        m_hat = nm / (1.0 - B1 ** STEP)
        v_hat = nv / (1.0 - B2 ** STEP)
        g_ref[...] = g
        nm_ref[...] = nm
        nv_ref[...] = nv
        d_ref[...] = -LR * (m_hat / (jnp.sqrt(v_hat) + EPS_ADAM) + WD * w_ref[...])

    blk = pl.BlockSpec((tr, tc), lambda i, j: (i, j))
    return pl.pallas_call(
        body, name=name, grid=(r // tr, c // tc),
        in_specs=[pl.BlockSpec((n, tr, tc), lambda i, j: (0, i, j)), blk, blk, blk],
        out_specs=[blk] * 4,
        out_shape=[jax.ShapeDtypeStruct((r, c), f32)] * 4,
        compiler_params=_cparams(2, arbitrary=False),
    )(parts, w, m, v)


_BIG = ["w_in", "w_out", "w_ffn_gate", "w_ffn_up", "w_ffn_down", "ple_w_proj", "ple_w_gate"]
_SMALL = ["b_in", "hg_lb_logits", "ml_conv_w", "ml_conv_b", "hg_norm_g", "ml_norm_g", "ln1_g", "ln1_b", "ln2_g", "ln2_b", "ple_b_gate"]
_ORDER = ["w_in", "b_in", "hg_lb_logits", "ml_conv_w", "ml_conv_b", "hg_norm_g", "ml_norm_g", "w_out", "ln1_g", "ln1_b",
          "w_ffn_gate", "w_ffn_up", "w_ffn_down", "ln2_g", "ln2_b", "ple_w_proj", "ple_w_gate", "ple_b_gate"]
_PACK_ROWS, _PACK_COLS = 16, 1024


def _pack(arrays):
    flat = jnp.concatenate([a.reshape(-1) for a in arrays])
    return jnp.pad(flat, (0, _PACK_ROWS * _PACK_COLS - flat.shape[0])).reshape(_PACK_ROWS, _PACK_COLS)


def _unpack(pack, shapes):
    flat = pack.reshape(-1)
    out, off = [], 0
    for s in shapes:
        size = 1
        for d in s:
            size *= d
        out.append(flat[off:off + size].reshape(s))
        off += size
    return out


def _to_chip_major(g, col_split):
    if col_split:
        k, n = g.shape
        return g.reshape(k, 4, n // 4).transpose(1, 0, 2)
    k, n = g.shape
    return g.reshape(4, k // 4, n)


def kernel(x, p, w_in, b_in, hg_lb_logits, ml_conv_w, ml_conv_b, hg_norm_g, ml_norm_g, w_out, ln1_g, ln1_b, w_ffn_gate, w_ffn_up, w_ffn_down, ln2_g, ln2_b, ple_w_proj, ple_w_gate, ple_b_gate, loss_target, m_w_in, m_b_in, m_hg_lb_logits, m_ml_conv_w, m_ml_conv_b, m_hg_norm_g, m_ml_norm_g, m_w_out, m_ln1_g, m_ln1_b, m_w_ffn_gate, m_w_ffn_up, m_w_ffn_down, m_ln2_g, m_ln2_b, m_ple_w_proj, m_ple_w_gate, m_ple_b_gate, v_w_in, v_b_in, v_hg_lb_logits, v_ml_conv_w, v_ml_conv_b, v_hg_norm_g, v_ml_norm_g, v_w_out, v_ln1_g, v_ln1_b, v_w_ffn_gate, v_w_ffn_up, v_w_ffn_down, v_ln2_g, v_ln2_b, v_ple_w_proj, v_ple_w_gate, v_ple_b_gate):
    args = dict(locals())
    wts = {k: args[k] for k in _ORDER}
    mom = {k: args["m_" + k] for k in _ORDER}
    var = {k: args["v_" + k] for k in _ORDER}
    two_d = lambda a: a.reshape(a.shape[-2], a.shape[-1])
    block = lambda k, a: jnp.swapaxes(two_d(a), 0, 1) if k in _TRANSPOSED else two_d(a)
    unblock = lambda k, a: (jnp.swapaxes(a, 0, 1) if k in _TRANSPOSED else a).reshape(wts[k].shape)

    shards = {k: block(k, wts[k]).astype(bf16) for k in _BIG}
    w_in_blocks, taps = _gather_chips([shards["w_in"], two_d(ml_conv_w)], "gather_w_in")
    w_in_full = _from_chip_major(w_in_blocks, False)
    conv_w_full = _from_chip_major(taps, True)

    def core_sum(k, g):
        pieces = _to_chip_major(g, k in _COL_SPLIT)
        if pieces.shape[1] % (2 * _EX_ROWS):
            return _pair_reduce_cols(pieces, "pair_reduce_" + k)
        return _pair_reduce(pieces, "pair_reduce_" + k)

    early_keys = _BIG[1:]
    loss, grad_x, grads, early_received = _local_step(
        x[0], p[0, 0], loss_target[0], w_in_full, b_in, hg_lb_logits, conv_w_full, ml_conv_b, hg_norm_g, ml_norm_g,
        None, ln1_g, ln1_b, None, None, None, ln2_g, ln2_b, None, None, ple_b_gate,
        early_hook=lambda early: [core_sum(k, early[k]) for k in early_keys],
        late_shards={k: shards[k] for k in early_keys})

    received = list(_scatter_chips([core_sum("w_in", grads["w_in"])], "scatter_grad_w_in")) + list(early_received)
    out_g, out_d, out_m, out_v = {}, {}, {}, {}
    for k, rcv in zip(_BIG, received):
        own = block(k, wts[k])
        if rcv.shape[1] == own.shape[0]:
            whole = _chip_reduce_swap_cols(rcv, "chip_reduce_" + k)
        else:
            parts = _chip_reduce_swap(rcv, "chip_reduce_" + k)
            whole = parts.reshape(2 * parts.shape[1], parts.shape[2])
        g, d, nm, nv = _adamw(whole[None], own, block(k, mom[k]), block(k, var[k]), "adamw_" + k)
        out_g[k], out_d[k], out_m[k], out_v[k] = unblock(k, g), unblock(k, d), unblock(k, nm), unblock(k, nv)

    small_shapes = [(1, PROJ_W), (2, MIX_W), (CONV_K, MIX_W)] + [(1, MIX_W)] * 3 + [(1, D_MODEL)] * 5 + [(1, 1)]
    contrib = _pack([grads[k] for k in _SMALL] + [loss])
    summed = _sum_slots(_gather_all(contrib, "gather_small"), "sum_small")
    small = _unpack(summed, small_shapes)
    loss_total = small[-1].reshape(())
    gsm = dict(zip(_SMALL, small[:-1]))
    place = 2 * lax.axis_index("x") + lax.axis_index("y")
    conv_cols = ml_conv_w.shape[-1]
    gsm["ml_conv_w"] = lax.dynamic_slice(gsm["ml_conv_w"], (0, place * conv_cols), (CONV_K, conv_cols))
    own_shapes = [wts[k].shape for k in _SMALL]
    g_pack = _pack([gsm[k] for k in _SMALL])
    res = _adamw(g_pack[None], _pack([wts[k] for k in _SMALL]), _pack([mom[k] for k in _SMALL]), _pack([var[k] for k in _SMALL]), "adamw_small")
    for dst, pack in zip((out_g, out_d, out_m, out_v), res):
        for k, a in zip(_SMALL, _unpack(pack, own_shapes)):
            dst[k] = a

    outs = [loss_total, grad_x[None]]
    for group in (out_g, out_d, out_m, out_v):
        outs += [group[k] for k in _ORDER]
    return tuple(outs)
```

```python
import functools

import jax
import jax.numpy as jnp
from jax import lax
from jax.experimental import pallas as pl
from jax.experimental.pallas import tpu as pltpu

f32 = jnp.float32
bf16 = jnp.bfloat16
HI = lax.Precision.HIGHEST

D_MODEL = 1024
HEADS = 4
HEAD_W = 128
MIX_W = HEADS * HEAD_W
ML_DQK = 64
PROJ_W = 3592
U_HG = 4 * MIX_W
U_ML = 3 * MIX_W + 128
D_FF = 2816
PLE = 256
CHUNK = 128
SUB = 16
EXP_CAP = 80.0
CONV_K = 4
HALO = 8
ALPHA = float(2.0 ** 0.25)
LN_EPS = 1e-5
RMS_EPS = 1e-6
NEG = -1e30
LR, B1, B2, EPS_ADAM, WD, STEP = 0.001, 0.9, 0.999, 1e-08, 0.01, 10
VMEM_LIMIT = 56 * 1024 * 1024
DENSE_ROWS = 512
WGRAD_ROWS = 2048


def _cparams(n_axes, arbitrary=True):
    sem = ("arbitrary",) * n_axes if arbitrary else ("parallel",) * n_axes
    return pltpu.CompilerParams(dimension_semantics=sem, vmem_limit_bytes=VMEM_LIMIT)


ACT = bf16


def _mx(a):
    return a.astype(ACT)


def _bdot(a, b):
    return jnp.dot(_mx(a), _mx(b), preferred_element_type=f32)


def _bdot_nt(a, b):
    return lax.dot_general(_mx(a), _mx(b), (((1,), (1,)), ((), ())), preferred_element_type=f32)


def _bdot_tn(a, b):
    return lax.dot_general(_mx(a), _mx(b), (((0,), (0,)), ((), ())), preferred_element_type=f32)


def _split3(x):
    hi = x.astype(bf16)
    r1 = x - hi.astype(f32)
    mid = r1.astype(bf16)
    lo = (r1 - mid.astype(f32)).astype(bf16)
    return hi, mid, lo


def _dot3(a, b, dims):
    a_hi = a.astype(bf16)
    a_lo = (a - a_hi.astype(f32)).astype(bf16)
    b_hi = b.astype(bf16)
    b_lo = (b - b_hi.astype(f32)).astype(bf16)
    dn = (dims, ((), ()))
    return (lax.dot_general(a_hi, b_hi, dn, preferred_element_type=f32) + lax.dot_general(a_hi, b_lo, dn, preferred_element_type=f32)
            + lax.dot_general(a_lo, b_hi, dn, preferred_element_type=f32))


def _sel_dot(sel, x):
    sb = sel.astype(bf16)
    return sum(jnp.dot(sb, part, preferred_element_type=f32) for part in _split3(x))


def _sel_dot_nt(sel, x):
    sb = sel.astype(bf16)
    return sum(lax.dot_general(sb, part, (((1,), (1,)), ((), ())), preferred_element_type=f32) for part in _split3(x))


def _sigmoid(x):
    return 1.0 / (1.0 + jnp.exp(-x))


def _log_sigmoid(x):
    return jnp.minimum(x, 0.0) - jnp.log(1.0 + jnp.exp(-jnp.abs(x)))


def _tri(n, upper=False):
    r = lax.broadcasted_iota(jnp.int32, (n, n), 0)
    c = lax.broadcasted_iota(jnp.int32, (n, n), 1)
    return (c >= r) if upper else (c <= r)


def _rows(tm, n, col=0):
    return pl.BlockSpec((tm, n), lambda i, _c=col: (i, _c))


def _rows_rev(tm, n, nb, col=0):
    return pl.BlockSpec((tm, n), lambda i, _c=col, _nb=nb: (_nb - 1 - i, _c))


def _const(shape):
    return pl.BlockSpec(shape, lambda i, _n=len(shape): (0,) * _n)


def _resident(shape):
    return pl.BlockSpec(shape, lambda i, _n=len(shape): (0,) * _n, pipeline_mode=pl.Buffered(1))


def _tile(t, want):
    return want if t % want == 0 else t


def _inproj(x, w_hg, w_ml, b_hg, b_ml, riders=()):
    t = x.shape[0]
    tm = _tile(t, DENSE_ROWS)

    def body(x_ref, whg_ref, wml_ref, bhg_ref, bml_ref, uhg_ref, uml_ref, xb_ref):
        xb = _mx(x_ref[...])
        xb_ref[...] = xb
        uhg_ref[...] = _bdot_nt(xb, whg_ref[...]) + bhg_ref[...]
        uml_ref[...] = _bdot_nt(xb, wml_ref[...]) + bml_ref[...]

    return _riding_call(
        body, "inproj", t // tm,
        in_specs=[_rows(tm, D_MODEL), _resident((U_HG, D_MODEL)), _resident((U_ML, D_MODEL)), _const((1, U_HG)), _const((1, U_ML))],
        out_specs=[_rows(tm, U_HG), _rows(tm, U_ML), _rows(tm, D_MODEL)],
        out_shape=[jax.ShapeDtypeStruct((t, U_HG), f32), jax.ShapeDtypeStruct((t, U_ML), f32), jax.ShapeDtypeStruct((t, D_MODEL), ACT)],
        scratch_shapes=[], operands=(x, w_hg, w_ml, b_hg, b_ml), riders=riders, copies=_gather_copies, ride_shapes=_gather_shapes(riders))


def _hg_gates(hq, hf, lb, tri):
    s = _sigmoid(hf)
    om = 1.0 - lb
    f = lb + om * s
    g = jnp.log(f)
    k = om * (1.0 - s)
    sq = _sigmoid(hq)
    q = hq * sq
    b = _sel_dot(tri, g)
    return q, sq, s, f, k, b


def _hg_scores(q, k, b, tril_mask):
    qts, kts, eqs, eks, rows = [], [], [], [], []
    for i in range(CHUNK // SUB):
        lo = i * SUB
        ref = jnp.zeros_like(b[0:1]) if i == 0 else b[lo - 1:lo]
        eq = jnp.exp(b[lo:lo + SUB] - ref)
        ek = jnp.exp(jnp.minimum(ref - b, EXP_CAP))
        qt = q[lo:lo + SUB] * eq
        kt = k * ek
        rows.append(_bdot_nt(qt, kt))
        qts.append(qt); kts.append(kt); eqs.append(eq); eks.append(ek)
    a = jnp.where(tril_mask, jnp.concatenate(rows, axis=0), 0.0)
    return a, qts, kts, eqs, eks


def _head_rms(o, gn):
    rstd = lax.rsqrt(jnp.mean(o * o, axis=-1, keepdims=True) + RMS_EPS)
    oh = o * rstd
    return oh, rstd, oh * gn


def _lower_bound(logit_ref):
    lg = logit_ref[...]
    return _sigmoid(lg[0:1] - lg[1:2])


def _hgrn2_fwd(u_hg, logits, gn, riders=()):
    t = u_hg.shape[0]
    tb = _tile(t, 256)
    nc_blk = tb // CHUNK

    def body(u_ref, lg_ref, gn_ref, og_ref, sst_ref, st_ref):
        @pl.when(pl.program_id(0) == 0)
        def _():
            st_ref[...] = jnp.zeros_like(st_ref)

        lb_all = _lower_bound(lg_ref)
        tril_mask = _tri(CHUNK)
        tri = tril_mask.astype(f32)

        def chunk(c, carry):
            r0 = pl.multiple_of(c * CHUNK, CHUNK)
            rows = pl.ds(r0, CHUNK)
            heads = range(HEADS)
            cols = [slice(h * HEAD_W, (h + 1) * HEAD_W) for h in heads]
            hv = [u_ref[rows, 2 * MIX_W + h * HEAD_W:2 * MIX_W + (h + 1) * HEAD_W] for h in heads]
            gts = [_hg_gates(u_ref[rows, h * HEAD_W:(h + 1) * HEAD_W], u_ref[rows, MIX_W + h * HEAD_W:MIX_W + (h + 1) * HEAD_W],
                             lb_all[:, cols[h]], tri) for h in heads]
            q = [g[0] for g in gts]
            k = [g[4] for g in gts]
            b = [g[5] for g in gts]
            a = [_hg_scores(q[h], k[h], b[h], tril_mask)[0] for h in heads]
            st = [st_ref[h] for h in heads]
            bl = [b[h][CHUNK - 1:CHUNK] for h in heads]
            o = [_bdot(a[h], hv[h]) + _bdot_nt(q[h] * jnp.exp(b[h]), st[h]) for h in heads]
            new_st = [st[h] * jnp.exp(bl[h]) + _bdot_tn(hv[h], k[h] * jnp.exp(bl[h] - b[h])) for h in heads]
            for h in heads:
                sst_ref[c, h] = st[h]
                st_ref[h] = new_st[h]
                hgate = u_ref[rows, 3 * MIX_W + h * HEAD_W:3 * MIX_W + (h + 1) * HEAD_W]
                _, _, y = _head_rms(o[h], gn_ref[:, cols[h]])
                og_ref[rows, cols[h]] = (y * (hgate * _sigmoid(hgate))).astype(ACT)
            return carry

        lax.fori_loop(0, nc_blk, chunk, 0, unroll=True)

    return _riding_call(
        body, "hgrn2_fwd", t // tb,
        in_specs=[_rows(tb, U_HG), _const((2, MIX_W)), _const((1, MIX_W))],
        out_specs=[_rows(tb, MIX_W), pl.BlockSpec((nc_blk, HEADS, HEAD_W, HEAD_W), lambda i: (i, 0, 0, 0))],
        out_shape=[jax.ShapeDtypeStruct((t, MIX_W), ACT), jax.ShapeDtypeStruct((t // CHUNK, HEADS, HEAD_W, HEAD_W), f32)],
        scratch_shapes=[pltpu.VMEM((HEADS, HEAD_W, HEAD_W), f32)],
        operands=(u_hg, logits, gn), riders=riders, copies=_gather_copies, ride_shapes=_gather_shapes(riders))


def _hgrn2_bwd(u_hg, logits, gn, sst, dog, riders=()):
    t = u_hg.shape[0]
    tb = _tile(t, 256)
    nb = t // tb
    nc_blk = tb // CHUNK
    nr = len(riders)

    def body(*refs):
        u_ref, lg_ref, gn_ref, sst_ref, dog_ref = refs[:5]
        ride_in = refs[5:5 + nr]
        du_ref, dlg_ref, dgn_ref = refs[5 + nr:8 + nr]
        ride_out = refs[8 + nr:8 + 2 * nr]
        dst_ref = refs[8 + 2 * nr]
        ride_sems = refs[9 + 2 * nr:]

        @pl.when(pl.program_id(0) == 0)
        def _():
            dst_ref[...] = jnp.zeros_like(dst_ref)
            dlg_ref[...] = jnp.zeros_like(dlg_ref)
            dgn_ref[...] = jnp.zeros_like(dgn_ref)
            if nr:
                _scatter_start(ride_in, ride_out, *ride_sems)

        lb_all = _lower_bound(lg_ref)
        tril_mask = _tri(CHUNK)
        tri = tril_mask.astype(f32)
        triu = _tri(CHUNK, upper=True).astype(f32)

        def chunk(j, carry):
            c = nc_blk - 1 - j
            r0 = pl.multiple_of(c * CHUNK, CHUNK)
            rows = pl.ds(r0, CHUNK)
            heads = range(HEADS)
            nsub = CHUNK // SUB
            cols = [slice(h * HEAD_W, (h + 1) * HEAD_W) for h in heads]
            hq = [u_ref[rows, h * HEAD_W:(h + 1) * HEAD_W] for h in heads]
            hf = [u_ref[rows, MIX_W + h * HEAD_W:MIX_W + (h + 1) * HEAD_W] for h in heads]
            hv = [u_ref[rows, 2 * MIX_W + h * HEAD_W:2 * MIX_W + (h + 1) * HEAD_W] for h in heads]
            lb = [lb_all[:, cols[h]] for h in heads]
            gts = [_hg_gates(hq[h], hf[h], lb[h], tri) for h in heads]
            q, sq, s, f, k, b = ([g[n] for g in gts] for n in range(6))
            scs = [_hg_scores(q[h], k[h], b[h], tril_mask) for h in heads]
            a, qts, kts, eqs, eks = ([sc[n] for sc in scs] for n in range(5))
            st = [sst_ref[c, h] for h in heads]
            dst = [dst_ref[h] for h in heads]
            bl = [b[h][CHUNK - 1:CHUNK] for h in heads]
            eb = [jnp.exp(b[h]) for h in heads]
            qh = [q[h] * eb[h] for h in heads]
            ekl = [jnp.exp(bl[h] - b[h]) for h in heads]
            kh = [k[h] * ekl[h] for h in heads]
            o = [_bdot(a[h], hv[h]) + _bdot_nt(qh[h], st[h]) for h in heads]
            do = []
            for h in heads:
                hgate = u_ref[rows, 3 * MIX_W + h * HEAD_W:3 * MIX_W + (h + 1) * HEAD_W]
                gnh = gn_ref[:, cols[h]]
                oh, rstd, y = _head_rms(o[h], gnh)
                sg = _sigmoid(hgate)
                dogh = dog_ref[rows, cols[h]]
                dy = dogh * (hgate * sg)
                du_ref[rows, 3 * MIX_W + h * HEAD_W:3 * MIX_W + (h + 1) * HEAD_W] = (dogh * y * (sg * (1.0 + hgate * (1.0 - sg)))).astype(ACT)
                dgn_ref[:, cols[h]] += jnp.sum(dy * oh, axis=0, keepdims=True)
                doh = dy * gnh
                do.append(rstd * (doh - oh * jnp.mean(doh * oh, axis=-1, keepdims=True)))
            da = [jnp.where(tril_mask, _bdot_nt(do[h], hv[h]), 0.0) for h in heads]
            dv = [_bdot_tn(a[h], do[h]) + _bdot_nt(kh[h], dst[h]) for h in heads]
            dq = [_bdot(do[h], st[h]) * eb[h] for h in heads]
            dk = [_bdot(hv[h], dst[h]) * ekl[h] for h in heads]
            d_last = [jnp.sum(k[h] * dk[h], axis=0, keepdims=True) + jnp.exp(bl[h]) * jnp.sum(dst[h] * st[h], axis=0, keepdims=True)
                      for h in heads]
            d_b = [q[h] * dq[h] - k[h] * dk[h] for h in heads]
            dqs = [[] for _ in heads]
            q_dq = [[] for _ in heads]
            for i in range(nsub):
                for h in heads:
                    da_i = _mx(da[h][i * SUB:(i + 1) * SUB])
                    q_r, k_r = _mx(qts[h][i]), _mx(kts[h][i])
                    g_q = jnp.dot(da_i, k_r, preferred_element_type=f32)
                    g_k = lax.dot_general(da_i, q_r, (((0,), (0,)), ((), ())), preferred_element_type=f32)
                    dqs[h].append(g_q * eqs[h][i])
                    q_dq[h].append(q_r.astype(f32) * g_q)
                    dk[h] = dk[h] + g_k * eks[h][i]
                    d_b[h] = d_b[h] - k_r.astype(f32) * g_k
            for h in heads:
                dq[h] = dq[h] + jnp.concatenate(dqs[h], axis=0)
                d_b[h] = d_b[h] + jnp.concatenate(q_dq[h], axis=0)
                dst_ref[h] = dst[h] * jnp.exp(bl[h]) + _bdot_tn(do[h], qh[h])
            dg = [_sel_dot(triu, d_b[h]) + d_last[h] for h in heads]
            for h in heads:
                dfk = dg[h] / f[h] - dk[h]
                du_ref[rows, h * HEAD_W:(h + 1) * HEAD_W] = (dq[h] * (sq[h] * (1.0 + hq[h] * (1.0 - sq[h])))).astype(ACT)
                du_ref[rows, MIX_W + h * HEAD_W:MIX_W + (h + 1) * HEAD_W] = ((1.0 - lb[h]) * dfk * s[h] * (1.0 - s[h])).astype(ACT)
                du_ref[rows, 2 * MIX_W + h * HEAD_W:2 * MIX_W + (h + 1) * HEAD_W] = dv[h].astype(ACT)
                dlb = jnp.sum((1.0 - s[h]) * dfk, axis=0, keepdims=True) * (lb[h] * (1.0 - lb[h]))
                dlg_ref[0:1, cols[h]] += dlb
                dlg_ref[1:2, cols[h]] -= dlb
            return carry

        lax.fori_loop(0, nc_blk, chunk, 0, unroll=True)

        if nr:
            @pl.when(pl.program_id(0) == nb - 1)
            def _():
                _scatter_wait(ride_in, ride_out, *ride_sems)

    hbm = pl.BlockSpec(memory_space=pltpu.HBM)
    ride_scratch = [pltpu.SemaphoreType.DMA((3 * nr,)), pltpu.SemaphoreType.DMA((3 * nr,)), pltpu.SemaphoreType.DMA((nr,))] if nr else []
    return pl.pallas_call(
        body, name="hgrn2_bwd", grid=(nb,),
        in_specs=[_rows_rev(tb, U_HG, nb), _const((2, MIX_W)), _const((1, MIX_W)),
                  pl.BlockSpec((nc_blk, HEADS, HEAD_W, HEAD_W), lambda i: (nb - 1 - i, 0, 0, 0)), _rows_rev(tb, MIX_W, nb)] + [hbm] * nr,
        out_specs=[_rows_rev(tb, U_HG, nb), _const((2, MIX_W)), _const((1, MIX_W))] + [hbm] * nr,
        out_shape=[jax.ShapeDtypeStruct((t, U_HG), ACT), jax.ShapeDtypeStruct((2, MIX_W), f32), jax.ShapeDtypeStruct((1, MIX_W), f32)]
        + [jax.ShapeDtypeStruct(r.shape, r.dtype) for r in riders],
        scratch_shapes=[pltpu.VMEM((HEADS, HEAD_W, HEAD_W), f32)] + ride_scratch,
        compiler_params=_cparams(1),
    )(u_hg, logits, gn, sst, dog, *riders)


def _conv_fwd(u_ml, w, b):
    t = u_ml.shape[0]
    tm = _tile(t, 512)

    def body(x_ref, w_ref, b_ref, pre_ref, act_ref, xbuf):
        @pl.when(pl.program_id(0) == 0)
        def _():
            xbuf[...] = jnp.zeros_like(xbuf)

        xbuf[0:HALO, :] = xbuf[tm:tm + HALO, :]
        xbuf[HALO:HALO + tm, :] = x_ref[...]
        pre = b_ref[...] + jnp.zeros((tm, MIX_W), f32)
        for kk in range(CONV_K):
            off = HALO - (CONV_K - 1) + kk
            pre = pre + w_ref[kk:kk + 1, :] * xbuf[off:off + tm, :]
        pre_ref[...] = pre
        act_ref[...] = pre * _sigmoid(pre)

    return pl.pallas_call(
        body, name="conv_fwd", grid=(t // tm,),
        in_specs=[_rows(tm, MIX_W), _const((CONV_K, MIX_W)), _const((1, MIX_W))],
        out_specs=[_rows(tm, MIX_W), _rows(tm, MIX_W)],
        out_shape=[jax.ShapeDtypeStruct((t, MIX_W), f32)] * 2,
        scratch_shapes=[pltpu.VMEM((tm + HALO, MIX_W), f32)],
        compiler_params=_cparams(1),
    )(u_ml, w, b)


def _conv_bwd(u_ml, w, pre, dact):
    t = u_ml.shape[0]
    tm = _tile(t, 512)
    nb = t // tm
    hb = tm // HALO

    def body(x_ref, halo_ref, w_ref, pre_ref, dact_ref, dx_ref, dw_ref, db_ref, dbuf, xbuf):
        i = pl.program_id(0)

        @pl.when(i == 0)
        def _():
            dbuf[...] = jnp.zeros_like(dbuf)
            dw_ref[...] = jnp.zeros_like(dw_ref)
            db_ref[...] = jnp.zeros_like(db_ref)

        p = pre_ref[...]
        sg = _sigmoid(p)
        dpre = dact_ref[...] * (sg * (1.0 + p * (1.0 - sg)))
        dbuf[tm:tm + HALO, :] = dbuf[0:HALO, :]
        dbuf[0:tm, :] = dpre
        has_prev = (i < nb - 1).astype(f32)
        xbuf[0:HALO, :] = halo_ref[...] * has_prev
        xbuf[HALO:HALO + tm, :] = x_ref[...]
        dx = jnp.zeros((tm, MIX_W), f32)
        for kk in range(CONV_K):
            back = CONV_K - 1 - kk
            dx = dx + w_ref[kk:kk + 1, :] * dbuf[back:back + tm, :]
            off = HALO - (CONV_K - 1) + kk
            dw_ref[kk:kk + 1, :] += jnp.sum(dpre * xbuf[off:off + tm, :], axis=0, keepdims=True)
        dx_ref[...] = dx.astype(ACT)
        db_ref[...] += jnp.sum(dpre, axis=0, keepdims=True)

    return pl.pallas_call(
        body, name="conv_bwd", grid=(nb,),
        in_specs=[_rows_rev(tm, MIX_W, nb),
                  pl.BlockSpec((HALO, MIX_W), lambda i: (jnp.maximum((nb - 1 - i) * hb - 1, 0), 0)),
                  _const((CONV_K, MIX_W)), _rows_rev(tm, MIX_W, nb), _rows_rev(tm, MIX_W, nb)],
        out_specs=[_rows_rev(tm, MIX_W, nb), _const((CONV_K, MIX_W)), _const((1, MIX_W))],
        out_shape=[jax.ShapeDtypeStruct((t, MIX_W), ACT), jax.ShapeDtypeStruct((CONV_K, MIX_W), f32), jax.ShapeDtypeStruct((1, MIX_W), f32)],
        scratch_shapes=[pltpu.VMEM((tm + HALO, MIX_W), f32), pltpu.VMEM((tm + HALO, MIX_W), f32)],
        compiler_params=_cparams(1),
    )(u_ml, u_ml, w, pre, dact)


def _lane_pick(x, lane):
    idx = lax.broadcasted_iota(jnp.int32, x.shape, 1)
    return jnp.sum(jnp.where(idx == lane, x, 0.0), axis=-1, keepdims=True)


def _ml_gate_forms(gates, tri):
    lf = _log_sigmoid(gates)
    gc = _sel_dot(tri, lf)
    lane = lax.broadcasted_iota(jnp.int32, gates.shape, 1)
    mixed = jnp.where(lane < HEADS, gates, gc)
    sel = (lax.broadcasted_iota(jnp.int32, (8, 128), 0) == lax.broadcasted_iota(jnp.int32, (8, 128), 1)).astype(f32)
    rowsf = _sel_dot_nt(sel, mixed)
    return gc, rowsf


def _ml_chunk(q, k, v, gates, gc, rowsf, c_st, n_st, m_st, tril_mask):
    hs = range(HEADS)
    g_col = [_lane_pick(gc, HEADS + h) for h in hs]
    ig_col = [_lane_pick(gates, h) for h in hs]
    dmat = [jnp.where(tril_mask, g_col[h] - rowsf[HEADS + h:HEADS + h + 1, :] + rowsf[h:h + 1, :], NEG) for h in hs]
    m_inter = [g_col[h] + m_st[h] for h in hs]
    m_t = [jnp.maximum(m_inter[h], jnp.max(dmat[h], axis=-1, keepdims=True)) for h in hs]
    wi = [jnp.exp(dmat[h] - m_t[h]) for h in hs]
    wo = [jnp.exp(m_inter[h] - m_t[h]) for h in hs]
    qk = [_bdot_nt(q[h], k[h]) * wi[h] for h in hs]
    num = [_bdot(qk[h], v[h]) + wo[h] * _bdot(q[h], c_st[h]) for h in hs]
    den = [jnp.sum(qk[h], axis=-1, keepdims=True) + wo[h] * jnp.sum(q[h] * n_st[h], axis=-1, keepdims=True) for h in hs]
    floor = [jnp.exp(-m_t[h]) for h in hs]
    z = [jnp.maximum(jnp.abs(den[h]), floor[h]) for h in hs]
    g_last = [g_col[h][CHUNK - 1:CHUNK] for h in hs]
    a_col = [g_last[h] - g_col[h] + ig_col[h] for h in hs]
    m_new = [jnp.maximum(g_last[h] + m_st[h], jnp.max(a_col[h], axis=0, keepdims=True)) for h in hs]
    ws = [jnp.exp(a_col[h] - m_new[h]) for h in hs]
    w_old = [jnp.exp(g_last[h] + m_st[h] - m_new[h]) for h in hs]
    return dict(wi=wi, wo=wo, qk=qk, num=num, den=den, z=z, floor=floor, ws=ws, w_old=w_old, m_new=m_new)


def _mlstm_fwd(qkc, u_ml, gn, riders=()):
    t = qkc.shape[0]
    tb = _tile(t, 256)
    nc_blk = tb // CHUNK

    def body(qk_ref, v_ref, mo_ref, gt_ref, gn_ref, og_ref, cst_ref, nst_ref, mst_ref, c_sc, n_sc, m_sc):
        @pl.when(pl.program_id(0) == 0)
        def _():
            c_sc[...] = jnp.zeros_like(c_sc)
            n_sc[...] = jnp.zeros_like(n_sc)
            m_sc[...] = jnp.zeros_like(m_sc)

        tril_mask = _tri(CHUNK)
        tri = tril_mask.astype(f32)

        def chunk(c, carry):
            r0 = pl.multiple_of(c * CHUNK, CHUNK)
            rows = pl.ds(r0, CHUNK)
            gates = gt_ref[rows, :]
            gc, rowsf = _ml_gate_forms(gates, tri)
            hs = range(HEADS)
            q = [qk_ref[rows, h * ML_DQK:(h + 1) * ML_DQK] * (ML_DQK ** -0.5) for h in hs]
            k = [qk_ref[rows, HEADS * ML_DQK + h * ML_DQK:HEADS * ML_DQK + (h + 1) * ML_DQK] for h in hs]
            v = [v_ref[rows, h * HEAD_W:(h + 1) * HEAD_W] for h in hs]
            c_st = [c_sc[h] for h in hs]
            n_st = [n_sc[h] for h in hs]
            m_full = [m_sc[h] for h in hs]
            r = _ml_chunk(q, k, v, gates, gc, rowsf, c_st, n_st, [m[:, 0:1] for m in m_full], tril_mask)
            ksc = [k[h] * r["ws"][h] for h in hs]
            new_c = [r["w_old"][h] * c_st[h] + _bdot_tn(ksc[h], v[h]) for h in hs]
            for h in hs:
                cs = slice(h * HEAD_W, (h + 1) * HEAD_W)
                cst_ref[c, h] = c_st[h]
                nst_ref[c, h] = n_st[h]
                mst_ref[c, h] = m_full[h]
                c_sc[h] = new_c[h]
                n_sc[h] = r["w_old"][h] * n_st[h] + jnp.sum(ksc[h], axis=0, keepdims=True)
                m_sc[h] = r["m_new"][h] + jnp.zeros((1, 128), f32)
                _, _, y = _head_rms(r["num"][h] / r["z"][h], gn_ref[:, cs])
                og_ref[rows, cs] = (y * _sigmoid(mo_ref[rows, h * HEAD_W:(h + 1) * HEAD_W])).astype(ACT)
            return carry

        lax.fori_loop(0, nc_blk, chunk, 0)

    nchunks = t // CHUNK
    return _riding_call(
        body, "mlstm_fwd", t // tb,
        in_specs=[_rows(tb, MIX_W), _rows(tb, MIX_W, 1), _rows(tb, MIX_W, 2), _rows(tb, 128, 12), _const((1, MIX_W))],
        out_specs=[_rows(tb, MIX_W),
                   pl.BlockSpec((nc_blk, HEADS, ML_DQK, HEAD_W), lambda i: (i, 0, 0, 0)),
                   pl.BlockSpec((nc_blk, HEADS, 1, ML_DQK), lambda i: (i, 0, 0, 0)),
                   pl.BlockSpec((nc_blk, HEADS, 1, 128), lambda i: (i, 0, 0, 0))],
        out_shape=[jax.ShapeDtypeStruct((t, MIX_W), ACT),
                   jax.ShapeDtypeStruct((nchunks, HEADS, ML_DQK, HEAD_W), f32),
                   jax.ShapeDtypeStruct((nchunks, HEADS, 1, ML_DQK), f32),
                   jax.ShapeDtypeStruct((nchunks, HEADS, 1, 128), f32)],
        scratch_shapes=[pltpu.VMEM((HEADS, ML_DQK, HEAD_W), f32), pltpu.VMEM((HEADS, 1, ML_DQK), f32), pltpu.VMEM((HEADS, 1, 128), f32)],
        operands=(qkc, u_ml, u_ml, u_ml, gn), riders=riders, copies=_gather_copies, ride_shapes=_gather_shapes(riders))


def _mlstm_bwd(qkc, u_ml, gn, cst, nst, mst, dog):
    t = qkc.shape[0]
    tb = _tile(t, 256)
    nb = t // tb
    nc_blk = tb // CHUNK

    def body(qk_ref, v_ref, mo_ref, gt_ref, gn_ref, cst_ref, nst_ref, mst_ref, dog_ref,
             dqk_ref, dv_ref, dmo_ref, dgt_ref, dgn_ref, dc_sc, dn_sc):
        @pl.when(pl.program_id(0) == 0)
        def _():
            dc_sc[...] = jnp.zeros_like(dc_sc)
            dn_sc[...] = jnp.zeros_like(dn_sc)
            dgn_ref[...] = jnp.zeros_like(dgn_ref)

        tril_mask = _tri(CHUNK)
        tri = tril_mask.astype(f32)
        triu = _tri(CHUNK, upper=True).astype(f32)
        lane = lax.broadcasted_iota(jnp.int32, (CHUNK, 128), 1)

        def chunk(j, carry):
            c = nc_blk - 1 - j
            r0 = pl.multiple_of(c * CHUNK, CHUNK)
            rows = pl.ds(r0, CHUNK)
            gates = gt_ref[rows, :]
            gc, rowsf = _ml_gate_forms(gates, tri)
            dg_mat = jnp.zeros((CHUNK, 128), f32)
            dig_mat = jnp.zeros((CHUNK, 128), f32)
            dlast_row = jnp.zeros((1, 128), f32)
            hs = range(HEADS)
            cols = [slice(h * HEAD_W, (h + 1) * HEAD_W) for h in hs]
            q = [qk_ref[rows, h * ML_DQK:(h + 1) * ML_DQK] * (ML_DQK ** -0.5) for h in hs]
            k = [qk_ref[rows, HEADS * ML_DQK + h * ML_DQK:HEADS * ML_DQK + (h + 1) * ML_DQK] for h in hs]
            v = [v_ref[rows, h * HEAD_W:(h + 1) * HEAD_W] for h in hs]
            c_st = [cst_ref[c, h] for h in hs]
            n_st = [nst_ref[c, h] for h in hs]
            m_st = [mst_ref[c, h][:, 0:1] for h in hs]
            dc = [dc_sc[h] for h in hs]
            dn = [dn_sc[h] for h in hs]
            r = _ml_chunk(q, k, v, gates, gc, rowsf, c_st, n_st, m_st, tril_mask)
            z, wi, wo, ws, w_old, den = r["z"], r["wi"], r["wo"], r["ws"], r["w_old"], r["den"]
            hh = [r["num"][h] / z[h] for h in hs]
            dh = []
            for h in hs:
                gnh = gn_ref[:, cols[h]]
                oh, rstd, y = _head_rms(hh[h], gnh)
                sg = _sigmoid(mo_ref[rows, h * HEAD_W:(h + 1) * HEAD_W])
                dogh = dog_ref[rows, cols[h]]
                dy = dogh * sg
                dmo_ref[rows, cols[h]] = (dogh * y * (sg * (1.0 - sg))).astype(ACT)
                dgn_ref[:, cols[h]] += jnp.sum(dy * oh, axis=0, keepdims=True)
                doh = dy * gnh
                dh.append(rstd * (doh - oh * jnp.mean(doh * oh, axis=-1, keepdims=True)))
            dnum = [dh[h] / z[h] for h in hs]
            dz = [-jnp.sum(dh[h] * hh[h], axis=-1, keepdims=True) / z[h] for h in hs]
            dden = [jnp.where(jnp.abs(den[h]) > r["floor"][h], dz[h] * jnp.sign(den[h]), 0.0) for h in hs]
            dsw = [(_bdot_nt(dnum[h], v[h]) + dden[h]) * wi[h] for h in hs]
            dq = [_bdot(dsw[h], k[h]) + wo[h] * (_bdot_nt(dnum[h], c_st[h]) + dden[h] * n_st[h]) for h in hs]
            dk_state = [ws[h] * (_bdot_nt(v[h], dc[h]) + dn[h]) for h in hs]
            dk = [_bdot_tn(dsw[h], q[h]) + dk_state[h] for h in hs]
            dv = [_bdot_tn(r["qk"][h], dnum[h]) + ws[h] * _bdot(k[h], dc[h]) for h in hs]
            woq = [wo[h] * q[h] for h in hs]
            new_dc = [w_old[h] * dc[h] + _bdot_tn(woq[h], dnum[h]) for h in hs]
            for h in hs:
                dv_ref[rows, cols[h]] = dv[h].astype(ACT)
                dc_sc[h] = new_dc[h]
                dn_sc[h] = w_old[h] * dn[h] + jnp.sum(woq[h] * dden[h], axis=0, keepdims=True)
                d_last = (jnp.sum(jnp.sum(k[h] * dk_state[h], axis=-1, keepdims=True), axis=0, keepdims=True)
                          + w_old[h] * (jnp.sum(jnp.sum(dc[h] * c_st[h], axis=-1, keepdims=True), axis=0, keepdims=True)
                                        + jnp.sum(dn[h] * n_st[h], axis=-1, keepdims=True)))
                kdk = jnp.sum(k[h] * dk[h], axis=-1, keepdims=True)
                qdq = jnp.sum(q[h] * dq[h], axis=-1, keepdims=True)
                dg_mat = dg_mat + jnp.where(lane == HEADS + h, qdq - kdk, 0.0)
                dlast_row = dlast_row + jnp.where(lane[0:1] == HEADS + h, d_last, 0.0)
                dig_mat = dig_mat + jnp.where(lane == h, kdk, 0.0)
                dqk_ref[rows, h * ML_DQK:(h + 1) * ML_DQK] = dq[h] * (ML_DQK ** -0.5)
                dqk_ref[rows, HEADS * ML_DQK + h * ML_DQK:HEADS * ML_DQK + (h + 1) * ML_DQK] = dk[h]
            dlf = _sel_dot(triu, dg_mat) + dlast_row
            dgt_ref[rows, :] = (dig_mat + dlf * _sigmoid(-gates)).astype(ACT)
            return carry

        lax.fori_loop(0, nc_blk, chunk, 0)

    st4 = lambda a, b: pl.BlockSpec((nc_blk, HEADS, a, b), lambda i: (nb - 1 - i, 0, 0, 0))
    return pl.pallas_call(
        body, name="mlstm_bwd", grid=(nb,),
        in_specs=[_rows_rev(tb, MIX_W, nb), _rows_rev(tb, MIX_W, nb, 1), _rows_rev(tb, MIX_W, nb, 2), _rows_rev(tb, 128, nb, 12),
                  _const((1, MIX_W)), st4(ML_DQK, HEAD_W), st4(1, ML_DQK), st4(1, 128), _rows_rev(tb, MIX_W, nb)],
        out_specs=[_rows_rev(tb, MIX_W, nb), _rows_rev(tb, MIX_W, nb), _rows_rev(tb, MIX_W, nb), _rows_rev(tb, 128, nb), _const((1, MIX_W))],
        out_shape=[jax.ShapeDtypeStruct((t, MIX_W), f32), jax.ShapeDtypeStruct((t, MIX_W), ACT), jax.ShapeDtypeStruct((t, MIX_W), ACT),
                   jax.ShapeDtypeStruct((t, 128), ACT), jax.ShapeDtypeStruct((1, MIX_W), f32)],
        scratch_shapes=[pltpu.VMEM((HEADS, ML_DQK, HEAD_W), f32), pltpu.VMEM((HEADS, 1, ML_DQK), f32)],
        compiler_params=_cparams(1),
    )(qkc, u_ml, u_ml, u_ml, gn, cst, nst, mst, dog)


def _ln_fwd(r, g, b):
    mu = jnp.mean(r, axis=-1, keepdims=True)
    xc = r - mu
    rstd = lax.rsqrt(jnp.mean(xc * xc, axis=-1, keepdims=True) + LN_EPS)
    xh = xc * rstd
    return xh * g + b, xh, rstd


def _ln_bwd(dy, xh, rstd, g):
    dxh = dy * g
    return rstd * (dxh - jnp.mean(dxh, axis=-1, keepdims=True) - xh * jnp.mean(dxh * xh, axis=-1, keepdims=True))


def _outproj_ln1(og_hg, og_ml, x, w_out, g, b):
    t = x.shape[0]
    tm = _tile(t, DENSE_ROWS)

    def body(a_ref, b_ref, x_ref, w_ref, g_ref, bb_ref, x1_ref, xh_ref, rs_ref, x1b_ref):
        mix = _bdot(a_ref[...], w_ref[0:MIX_W, :]) + _bdot(b_ref[...], w_ref[MIX_W:2 * MIX_W, :])
        y, xh, rstd = _ln_fwd(ALPHA * x_ref[...] + mix, g_ref[...], bb_ref[...])
        x1_ref[...] = y
        x1b_ref[...] = y.astype(ACT)
        xh_ref[...] = xh.astype(ACT)
        rs_ref[...] = rstd

    return pl.pallas_call(
        body, name="outproj_ln1", grid=(t // tm,),
        in_specs=[_rows(tm, MIX_W), _rows(tm, MIX_W), _rows(tm, D_MODEL), _resident((D_MODEL, D_MODEL)), _const((1, D_MODEL)), _const((1, D_MODEL))],
        out_specs=[_rows(tm, D_MODEL), _rows(tm, D_MODEL), _rows(tm, 1), _rows(tm, D_MODEL)],
        out_shape=[jax.ShapeDtypeStruct((t, D_MODEL), f32), jax.ShapeDtypeStruct((t, D_MODEL), ACT), jax.ShapeDtypeStruct((t, 1), f32),
                   jax.ShapeDtypeStruct((t, D_MODEL), ACT)],
        compiler_params=_cparams(1, arbitrary=False),
    )(og_hg, og_ml, x, w_out, g, b)


def _ffn_up(x1, wg, wu):
    t = x1.shape[0]
    tm = _tile(t, DENSE_ROWS)

    def body(x_ref, wg_ref, wu_ref, hg_ref, up_ref, a_ref):
        xv = x_ref[...]
        hg = _bdot_nt(xv, wg_ref[...])
        up = _bdot_nt(xv, wu_ref[...])
        hg_ref[...] = hg.astype(ACT)
        up_ref[...] = up.astype(ACT)
        a_ref[...] = (hg * _sigmoid(hg) * up).astype(ACT)

    return pl.pallas_call(
        body, name="ffn_up", grid=(t // tm,),
        in_specs=[_rows(tm, D_MODEL), _resident((D_FF, D_MODEL)), _resident((D_FF, D_MODEL))],
        out_specs=[_rows(tm, D_FF), _rows(tm, D_FF), _rows(tm, D_FF)],
        out_shape=[jax.ShapeDtypeStruct((t, D_FF), ACT), jax.ShapeDtypeStruct((t, D_FF), ACT), jax.ShapeDtypeStruct((t, D_FF), ACT)],
        compiler_params=_cparams(1, arbitrary=False),
    )(x1, wg, wu)


def _ffn_down_ln2(a, x1, wd, g, b):
    t = x1.shape[0]
    tm = _tile(t, DENSE_ROWS)

    def body(a_ref, x_ref, w_ref, g_ref, bb_ref, x2_ref, xh_ref, rs_ref, x2b_ref):
        ffn = _bdot(a_ref[...], w_ref[...])
        y, xh, rstd = _ln_fwd(ALPHA * x_ref[...] + ffn, g_ref[...], bb_ref[...])
        x2_ref[...] = y
        x2b_ref[...] = y.astype(ACT)
        xh_ref[...] = xh.astype(ACT)
        rs_ref[...] = rstd

    return pl.pallas_call(
        body, name="ffn_down_ln2", grid=(t // tm,),
        in_specs=[_rows(tm, D_FF), _rows(tm, D_MODEL), _resident((D_FF, D_MODEL)), _const((1, D_MODEL)), _const((1, D_MODEL))],
        out_specs=[_rows(tm, D_MODEL), _rows(tm, D_MODEL), _rows(tm, 1), _rows(tm, D_MODEL)],
        out_shape=[jax.ShapeDtypeStruct((t, D_MODEL), f32), jax.ShapeDtypeStruct((t, D_MODEL), ACT), jax.ShapeDtypeStruct((t, 1), f32),
                   jax.ShapeDtypeStruct((t, D_MODEL), ACT)],
        compiler_params=_cparams(1, arbitrary=False),
    )(a, x1, wd, g, b)


def _head_loss_bwd(x2, xh2, rs2, p, tgt, w_pg, b_pg, w_pp, g2):
    t = x2.shape[0]
    tm = _tile(t, DENSE_ROWS)

    def body(x_ref, xh_ref, rs_ref, p_ref, t_ref, wg_ref, bg_ref, wp_ref, g_ref,
             dr_ref, de_ref, dz_ref, loss_ref, dbg_ref, dg2_ref, db2_ref):
        @pl.when(pl.program_id(0) == 0)
        def _():
            loss_ref[...] = jnp.zeros_like(loss_ref)
            dbg_ref[...] = jnp.zeros_like(dbg_ref)
            dg2_ref[...] = jnp.zeros_like(dg2_ref)
            db2_ref[...] = jnp.zeros_like(db2_ref)

        x2v = x_ref[...]
        z = _bdot(x2v, wg_ref[...]) + bg_ref[...]
        e = _bdot(p_ref[...], wp_ref[...])
        sg = _sigmoid(z)
        diff = x2v + sg * e - t_ref[...]
        loss_ref[...] += 0.5 * jnp.sum(jnp.mean(diff * diff, axis=-1, keepdims=True), axis=0, keepdims=True)
        dy = diff * (1.0 / D_MODEL)
        de_ref[...] = (dy * sg).astype(ACT)
        dz = dy * e * (sg * (1.0 - sg))
        dz_ref[...] = dz.astype(ACT)
        dbg_ref[...] += jnp.sum(dz, axis=0, keepdims=True)
        dx2 = dy + _bdot_nt(dz, wg_ref[...])
        xh = xh_ref[...].astype(f32)
        dg2_ref[...] += jnp.sum(dx2 * xh, axis=0, keepdims=True)
        db2_ref[...] += jnp.sum(dx2, axis=0, keepdims=True)
        dr_ref[...] = _ln_bwd(dx2, xh, rs_ref[...], g_ref[...])

    row = jax.ShapeDtypeStruct((1, D_MODEL), f32)
    return pl.pallas_call(
        body, name="head_loss_bwd", grid=(t // tm,),
        in_specs=[_rows(tm, D_MODEL), _rows(tm, D_MODEL), _rows(tm, 1), _rows(tm, PLE), _rows(tm, D_MODEL),
                  _resident((D_MODEL, D_MODEL)), _const((1, D_MODEL)), _resident((PLE, D_MODEL)), _const((1, D_MODEL))],
        out_specs=[_rows(tm, D_MODEL), _rows(tm, D_MODEL), _rows(tm, D_MODEL), _const((1, 1)), _const((1, D_MODEL)), _const((1, D_MODEL)), _const((1, D_MODEL))],
        out_shape=[jax.ShapeDtypeStruct((t, D_MODEL), f32), jax.ShapeDtypeStruct((t, D_MODEL), ACT), jax.ShapeDtypeStruct((t, D_MODEL), ACT),
                   jax.ShapeDtypeStruct((1, 1), f32), row, row, row],
        compiler_params=_cparams(1),
    )(x2, xh2, rs2, p, tgt, w_pg, b_pg, w_pp, g2)


def _ffn_bwd(dr2, hg, up, xh1, rs1, wd, wg, wu, g1, w_out):
    t = dr2.shape[0]
    tm = _tile(t, DENSE_ROWS // 2)

    def body(dr_ref, hg_ref, up_ref, xh_ref, rs_ref, wd_ref, wg_ref, wu_ref, g_ref, wo_ref,
             dr1_ref, dhg_ref, dup_ref, dg1_ref, db1_ref, doghg_ref, dogml_ref):
        @pl.when(pl.program_id(0) == 0)
        def _():
            dg1_ref[...] = jnp.zeros_like(dg1_ref)
            db1_ref[...] = jnp.zeros_like(db1_ref)

        dr2v = dr_ref[...]
        da = _bdot_nt(dr2v, wd_ref[...])
        hgv = hg_ref[...].astype(f32)
        sg = _sigmoid(hgv)
        dhg = da * up_ref[...].astype(f32) * (sg * (1.0 + hgv * (1.0 - sg)))
        dup = da * (hgv * sg)
        dhg_ref[...] = dhg.astype(ACT)
        dup_ref[...] = dup.astype(ACT)
        dx1 = ALPHA * dr2v + _bdot(dhg, wg_ref[...]) + _bdot(dup, wu_ref[...])
        xh = xh_ref[...].astype(f32)
        dg1_ref[...] += jnp.sum(dx1 * xh, axis=0, keepdims=True)
        db1_ref[...] += jnp.sum(dx1, axis=0, keepdims=True)
        dr1 = _ln_bwd(dx1, xh, rs_ref[...], g_ref[...])
        dr1_ref[...] = dr1
        dog = _bdot_nt(dr1, wo_ref[...])
        doghg_ref[...] = dog[:, 0:MIX_W]
        dogml_ref[...] = dog[:, MIX_W:2 * MIX_W]

    row = jax.ShapeDtypeStruct((1, D_MODEL), f32)
    return pl.pallas_call(
        body, name="ffn_bwd", grid=(t // tm,),
        in_specs=[_rows(tm, D_MODEL), _rows(tm, D_FF), _rows(tm, D_FF), _rows(tm, D_MODEL), _rows(tm, 1),
                  _resident((D_FF, D_MODEL)), _resident((D_FF, D_MODEL)), _resident((D_FF, D_MODEL)), _const((1, D_MODEL)),
                  _resident((D_MODEL, D_MODEL))],
        out_specs=[_rows(tm, D_MODEL), _rows(tm, D_FF), _rows(tm, D_FF), _const((1, D_MODEL)), _const((1, D_MODEL)),
                   _rows(tm, MIX_W), _rows(tm, MIX_W)],
        out_shape=[jax.ShapeDtypeStruct((t, D_MODEL), f32), jax.ShapeDtypeStruct((t, D_FF), ACT), jax.ShapeDtypeStruct((t, D_FF), ACT), row, row,
                   jax.ShapeDtypeStruct((t, MIX_W), f32), jax.ShapeDtypeStruct((t, MIX_W), f32)],
        compiler_params=_cparams(1),
    )(dr2, hg, up, xh1, rs1, wd, wg, wu, g1, w_out)


def _inproj_bwd(dr1, du_hg, dqk, dmv, dmo, dgt, w_hg, w_ml):
    t = dr1.shape[0]
    tm = _tile(t, DENSE_ROWS)

    def body(dr_ref, dhg_ref, dqk_ref, dmv_ref, dmo_ref, dgt_ref, whg_ref, wml_ref, gx_ref, dml_ref):
        dml = jnp.concatenate([dqk_ref[...], dmv_ref[...], dmo_ref[...], dgt_ref[...]], axis=-1).astype(ACT)
        dml_ref[...] = dml
        gx_ref[...] = ALPHA * dr_ref[...] + _bdot(dhg_ref[...], whg_ref[...]) + _bdot(dml, wml_ref[...])

    return pl.pallas_call(
        body, name="inproj_bwd", grid=(t // tm,),
        in_specs=[_rows(tm, D_MODEL), _rows(tm, U_HG), _rows(tm, MIX_W), _rows(tm, MIX_W), _rows(tm, MIX_W), _rows(tm, 128),
                  _resident((U_HG, D_MODEL)), _resident((U_ML, D_MODEL))],
        out_specs=[_rows(tm, D_MODEL), _rows(tm, U_ML)],
        out_shape=[jax.ShapeDtypeStruct((t, D_MODEL), f32), jax.ShapeDtypeStruct((t, U_ML), ACT)],
        compiler_params=_cparams(1, arbitrary=False),
    )(dr1, du_hg, dqk, dmv, dmo, dgt, w_hg, w_ml)


def _wgrad(a, b, name, tk=None, tn=None, colsum=False):
    t, kdim = a.shape
    n = b.shape[1]
    tk = tk or kdim
    tn = tn or n
    tt = _tile(t, WGRAD_ROWS)
    assert not colsum or tn == n

    def body(a_ref, b_ref, o_ref, *s_ref):
        @pl.when(pl.program_id(2) == 0)
        def _():
            o_ref[...] = jnp.zeros_like(o_ref)
            if colsum:
                s_ref[0][...] = jnp.zeros_like(s_ref[0])

        av = a_ref[...]
        o_ref[...] += _bdot_tn(av, b_ref[...])
        if colsum:
            s_ref[0][...] += jnp.sum(av.astype(f32), axis=0, keepdims=True)

    out_specs = [pl.BlockSpec((tk, tn), lambda i, j, s: (i, j))]
    out_shape = [jax.ShapeDtypeStruct((kdim, n), f32)]
    if colsum:
        out_specs.append(pl.BlockSpec((1, tk), lambda i, j, s: (0, i)))
        out_shape.append(jax.ShapeDtypeStruct((1, kdim), f32))
    res = pl.pallas_call(
        body, name=name, grid=(kdim // tk, n // tn, t // tt),
        in_specs=[pl.BlockSpec((tt, tk), lambda i, j, s: (s, i)), pl.BlockSpec((tt, tn), lambda i, j, s: (s, j))],
        out_specs=out_specs, out_shape=out_shape,
        compiler_params=_cparams(3),
    )(a, b)
    return res if colsum else res[0]


def _colsum(parts, name):
    t = parts[0].shape[0]
    tt = _tile(t, 512)
    widths = [a.shape[1] for a in parts]

    def body(*refs):
        o_ref = refs[-1]

        @pl.when(pl.program_id(0) == 0)
        def _():
            o_ref[...] = jnp.zeros_like(o_ref)

        off = 0
        for r, w in zip(refs[:-1], widths):
            o_ref[:, off:off + w] += jnp.sum(r[...].astype(f32), axis=0, keepdims=True)
            off += w

    return pl.pallas_call(
        body, name=name, grid=(t // tt,),
        in_specs=[_rows(tt, w) for w in widths],
        out_specs=_const((1, sum(widths))),
        out_shape=jax.ShapeDtypeStruct((1, sum(widths)), f32),
        compiler_params=_cparams(1),
    )(*parts)


_TRANSPOSED = {"w_in", "w_ffn_gate", "w_ffn_up"}
_COL_SPLIT = {"ple_w_proj"}
_RIDE_PLAN = (("w_out", "ple_w_gate", "ple_w_proj"), ("w_ffn_down",), ("w_ffn_gate", "w_ffn_up"))


def _from_chip_major(a, col_split):
    if col_split:
        return a.transpose(1, 0, 2).reshape(a.shape[1], 4 * a.shape[2])
    return a.reshape(4 * a.shape[1], a.shape[2])


def _local_step(x, p, tgt, w_in_b, b_in, logits, conv_w, conv_b, hg_gn, ml_gn, w_out_b, ln1_g, ln1_b,
                wg_b, wu_b, wd_b, ln2_g, ln2_b, w_pp_b, w_pg_b, b_pg, early_hook=None, late_shards=None):
    pad_w = U_HG + U_ML - PROJ_W
    w_hg = w_in_b[:U_HG]
    w_ml = jnp.pad(w_in_b[U_HG:], ((0, pad_w), (0, 0)))
    bb_hg = b_in[:, :U_HG]
    bb_ml = jnp.pad(b_in[:, U_HG:], ((0, 0), (0, pad_w)))

    ride = [[late_shards[k] for k in names] for names in _RIDE_PLAN] if late_shards is not None else [(), (), ()]
    (u_hg, u_ml, xb), got0 = _inproj(x, w_hg, w_ml, bb_hg, bb_ml, ride[0])
    (og_hg, sst), got1 = _hgrn2_fwd(u_hg, logits, hg_gn, ride[1])
    pre, qkc = _conv_fwd(u_ml, conv_w, conv_b)
    (og_ml, cst, nst, mst), got2 = _mlstm_fwd(qkc, u_ml, ml_gn, ride[2])
    if late_shards is not None:
        late = {k: _from_chip_major(g, k in _COL_SPLIT) for names, got in zip(_RIDE_PLAN, (got0, got1, got2)) for k, g in zip(names, got)}
        w_out_b, wg_b, wu_b, wd_b = late["w_out"], late["w_ffn_gate"], late["w_ffn_up"], late["w_ffn_down"]
        w_pp_b, w_pg_b = late["ple_w_proj"], late["ple_w_gate"]
    x1, xh1, rs1, x1b = _outproj_ln1(og_hg, og_ml, x, w_out_b, ln1_g, ln1_b)
    hgp, up, act = _ffn_up(x1b, wg_b, wu_b)
    x2, xh2, rs2, x2b = _ffn_down_ln2(act, x1, wd_b, ln2_g, ln2_b)
    dr2, de, dz, loss, d_bpg, d_ln2g, d_ln2b = _head_loss_bwd(x2, xh2, rs2, p, tgt, w_pg_b, b_pg, w_pp_b, ln2_g)
    dr1, dhg, dup, d_ln1g, d_ln1b, dog_hg, dog_ml = _ffn_bwd(dr2, hgp, up, xh1, rs1, wd_b, wg_b, wu_b, ln1_g, w_out_b)

    d_wo_a = _wgrad(og_hg, dr1, "wgrad_out_hg")
    d_wo_b = _wgrad(og_ml, dr1, "wgrad_out_ml")
    d_w_out = jnp.concatenate([d_wo_a, d_wo_b], axis=0)
    d_wg = _wgrad(dhg, x1b, "wgrad_ffn_gate", tk=D_FF // 2)
    d_wu = _wgrad(dup, x1b, "wgrad_ffn_up", tk=D_FF // 2)
    d_wd = _wgrad(act, dr2, "wgrad_ffn_down", tk=D_FF // 2)
    d_wpp = _wgrad(p, de, "wgrad_ple_proj")
    d_wpg = _wgrad(x2b, dz, "wgrad_ple_gate")
    early = dict(w_out=d_w_out, w_ffn_gate=d_wg, w_ffn_up=d_wu, w_ffn_down=d_wd, ple_w_proj=d_wpp, ple_w_gate=d_wpg)
    riders = early_hook(early) if early_hook is not None else ()

    res = _hgrn2_bwd(u_hg, logits, hg_gn, sst, dog_hg, riders)
    du_hg, d_logits, d_hg_gn = res[:3]
    dqkc, dmv, dmo, dgt, d_ml_gn = _mlstm_bwd(qkc, u_ml, ml_gn, cst, nst, mst, dog_ml)
    dqk, d_conv_w, d_conv_b = _conv_bwd(u_ml, conv_w, pre, dqkc)
    grad_x, du_ml = _inproj_bwd(dr1, du_hg, dqk, dmv, dmo, dgt, w_hg, w_ml)

    dw_hg, db_hg = _wgrad(du_hg, xb, "wgrad_in_hg", tk=U_HG // 2, colsum=True)
    dw_ml, db_ml = _wgrad(du_ml, xb, "wgrad_in_ml", colsum=True)
    d_w_in = jnp.concatenate([dw_hg, dw_ml[:PROJ_W - U_HG]], axis=0)
    d_b_in = jnp.concatenate([db_hg, db_ml[:, :PROJ_W - U_HG]], axis=1)

    grads = dict(w_in=d_w_in, b_in=d_b_in, hg_lb_logits=d_logits, ml_conv_w=d_conv_w, ml_conv_b=d_conv_b,
                 hg_norm_g=d_hg_gn, ml_norm_g=d_ml_gn, ln1_g=d_ln1g, ln1_b=d_ln1b, ln2_g=d_ln2g, ln2_b=d_ln2b,
                 ple_b_gate=d_bpg, **early)
    return loss, grad_x, grads, list(res[3:])


_ANY = pl.BlockSpec(memory_space=pltpu.HBM)
_MESH = pl.DeviceIdType.MESH


def _my_place():
    return lax.axis_index("x"), lax.axis_index("y"), lax.axis_index("c")


def _other_chips(x, y):
    return [(1 - x, y), (x, 1 - y), (1 - x, 1 - y)]


def _allgather_weights(shards, taps, name):
    n = len(shards)
    halves = [s.shape[0] // 2 for s in shards]

    def body(*refs):
        ins, tap_in = refs[:n], refs[n]
        outs, tap_out = refs[n + 1:2 * n + 1], refs[2 * n + 1]
        send_sems, recv_sems, local_sems = refs[2 * n + 2:]
        x, y, c = _my_place()
        me = 2 * x + y
        sibling = (x, y, 1 - c)
        chips = _other_chips(x, y)

        def ici(a, j, block_chip):
            px, py = chips[j]
            src = ins[a].at[pl.ds(pl.multiple_of(c * halves[a], 16), halves[a])] if block_chip is None else outs[a].at[block_chip, c]
            dst = outs[a].at[me if block_chip is None else block_chip, c]
            return pltpu.make_async_remote_copy(src_ref=src, dst_ref=dst, send_sem=send_sems.at[6 * a + j], recv_sem=recv_sems.at[6 * a + j],
                                                device_id=(px, py, c), device_id_type=_MESH)

        def d2d(a, j, half):
            px, py = chips[j]
            blk = outs[a].at[2 * px + py, half]
            return pltpu.make_async_remote_copy(src_ref=blk, dst_ref=blk, send_sem=send_sems.at[6 * a + 3 + j], recv_sem=recv_sems.at[6 * a + 3 + j],
                                                device_id=sibling, device_id_type=_MESH)

        local = []
        for a in range(n):
            for h in range(2):
                cp = pltpu.make_async_copy(ins[a].at[pl.ds(h * halves[a], halves[a])], outs[a].at[me, h], local_sems.at[2 * a + h])
                cp.start()
                local.append(cp)
            for j in range(3):
                ici(a, j, None).start()
        tap_local = pltpu.make_async_copy(tap_in, tap_out.at[me], local_sems.at[2 * n])
        tap_local.start()
        tap_copies = []
        for j, (px, py) in enumerate(chips):
            cp = pltpu.make_async_remote_copy(src_ref=tap_in, dst_ref=tap_out.at[me], send_sem=send_sems.at[6 * n + j], recv_sem=recv_sems.at[6 * n + j],
                                              device_id=(px, py, c), device_id_type=_MESH)
            cp.start()
            tap_copies.append(cp)
        for a in range(n):
            for j, (px, py) in enumerate(chips):
                ici(a, j, 2 * px + py).wait_recv()
                d2d(a, j, c).start()
        for a in range(n):
            for j in range(3):
                d2d(a, j, 1 - c).wait_recv()
        for a in range(n):
            for j in range(3):
                ici(a, j, None).wait_send()
                d2d(a, j, c).wait_send()
        for j, (px, py) in enumerate(chips):
            pltpu.make_async_remote_copy(src_ref=tap_in, dst_ref=tap_out.at[2 * px + py], send_sem=send_sems.at[6 * n + j], recv_sem=recv_sems.at[6 * n + j],
                                         device_id=(px, py, c), device_id_type=_MESH).wait()
        for cp in local:
            cp.wait()
        tap_local.wait()

    res = pl.pallas_call(
        body, name=name,
        in_specs=[_ANY] * (n + 1), out_specs=[_ANY] * (n + 1),
        out_shape=[jax.ShapeDtypeStruct((4, 2, s.shape[0] // 2, s.shape[1]), s.dtype) for s in shards]
        + [jax.ShapeDtypeStruct((4,) + taps.shape, taps.dtype)],
        scratch_shapes=[pltpu.SemaphoreType.DMA((6 * n + 3,)), pltpu.SemaphoreType.DMA((6 * n + 3,)), pltpu.SemaphoreType.DMA((2 * n + 1,))],
    )(*shards, taps)
    return [w.reshape((4,) + s.shape) for w, s in zip(res[:n], shards)], res[n]


def _swap_halves(pieces, name):
    n = len(pieces)
    halves = [p.shape[1] // 2 for p in pieces]

    def body(*refs):
        ins, own, other = refs[:n], refs[n:2 * n], refs[2 * n:3 * n]
        send_sems, recv_sems, local_sems = refs[3 * n:]
        x, y, c = _my_place()

        def half_of(a, which):
            return ins[a].at[pl.ds(0, 4), pl.ds(pl.multiple_of(which * halves[a], 16), halves[a])]

        def to_sibling(a):
            return pltpu.make_async_remote_copy(src_ref=half_of(a, 1 - c), dst_ref=other[a], send_sem=send_sems.at[a], recv_sem=recv_sems.at[a],
                                                device_id=(x, y, 1 - c), device_id_type=_MESH)

        local = []
        for a in range(n):
            cp = pltpu.make_async_copy(half_of(a, c), own[a], local_sems.at[a])
            cp.start()
            local.append(cp)
            to_sibling(a).start()
        for a in range(n):
            to_sibling(a).wait()
            local[a].wait()

    shapes = [jax.ShapeDtypeStruct((4, p.shape[1] // 2, p.shape[2]), p.dtype) for p in pieces]
    res = pl.pallas_call(
        body, name=name,
        in_specs=[_ANY] * n, out_specs=[_ANY] * (2 * n), out_shape=shapes + shapes,
        scratch_shapes=[pltpu.SemaphoreType.DMA((n,)), pltpu.SemaphoreType.DMA((n,)), pltpu.SemaphoreType.DMA((n,))],
    )(*pieces)
    return res[:n], res[n:]


_VMEM = pl.BlockSpec(memory_space=pltpu.VMEM)
_EX_ROWS = 32


def _pair_reduce(p, name):
    s, r, c = p.shape
    half = r // 2

    def body(p_ref, o_ref, other, send_sem, recv_sem):
        x, y, cc = _my_place()
        theirs = pl.multiple_of((1 - cc) * half, 16)
        mine = pl.multiple_of(cc * half, 16)
        cp = pltpu.make_async_remote_copy(src_ref=p_ref.at[pl.ds(0, s), pl.ds(theirs, half)], dst_ref=other, send_sem=send_sem, recv_sem=recv_sem,
                                          device_id=(x, y, 1 - cc), device_id_type=_MESH)
        cp.start()
        cp.wait()

        def step(i, carry):
            r0 = pl.multiple_of(i * _EX_ROWS, _EX_ROWS)
            for slot in range(s):
                own_rows = pl.ds(pl.multiple_of(mine + r0, 16), _EX_ROWS)
                o_ref[slot, pl.ds(r0, _EX_ROWS), :] = (p_ref[slot, own_rows, :] + other[slot, pl.ds(r0, _EX_ROWS), :]).astype(bf16)
            return carry

        lax.fori_loop(0, half // _EX_ROWS, step, 0)

    return pl.pallas_call(
        body, name=name, in_specs=[_VMEM], out_specs=_VMEM,
        out_shape=jax.ShapeDtypeStruct((s, half, c), bf16),
        scratch_shapes=[pltpu.VMEM((s, half, c), f32), pltpu.SemaphoreType.DMA, pltpu.SemaphoreType.DMA],
        compiler_params=pltpu.CompilerParams(vmem_limit_bytes=VMEM_LIMIT),
    )(p)


def _chip_reduce_swap(rcv, name):
    s, h, c = rcv.shape

    def body(r_ref, g_ref, send_sem, recv_sem):
        x, y, cc = _my_place()

        def step(i, carry):
            r0 = pl.multiple_of(i * _EX_ROWS, _EX_ROWS)
            acc = r_ref[0, pl.ds(r0, _EX_ROWS), :].astype(f32)
            for slot in range(1, s):
                acc = acc + r_ref[slot, pl.ds(r0, _EX_ROWS), :].astype(f32)
            g_ref[cc, pl.ds(r0, _EX_ROWS), :] = acc
            return carry

        lax.fori_loop(0, h // _EX_ROWS, step, 0)
        cp = pltpu.make_async_remote_copy(src_ref=g_ref.at[cc], dst_ref=g_ref.at[cc], send_sem=send_sem, recv_sem=recv_sem,
                                          device_id=(x, y, 1 - cc), device_id_type=_MESH)
        cp.start()
        cp.wait()

    return pl.pallas_call(
        body, name=name, in_specs=[_VMEM], out_specs=_VMEM,
        out_shape=jax.ShapeDtypeStruct((2, h, c), f32),
        scratch_shapes=[pltpu.SemaphoreType.DMA, pltpu.SemaphoreType.DMA],
        compiler_params=pltpu.CompilerParams(vmem_limit_bytes=VMEM_LIMIT),
    )(rcv)


def _pair_reduce_cols(p, name):
    s, r, c = p.shape
    hc = c // 2

    def body(p_ref, o_ref, other, send_sem, recv_sem):
        x, y, cc = _my_place()

        def run(mine_lo, theirs_lo):
            cp = pltpu.make_async_remote_copy(src_ref=p_ref.at[pl.ds(0, s), pl.ds(0, r), pl.ds(theirs_lo, hc)], dst_ref=other,
                                              send_sem=send_sem, recv_sem=recv_sem, device_id=(x, y, 1 - cc), device_id_type=_MESH)
            cp.start()
            cp.wait()
            for slot in range(s):
                o_ref[slot] = (p_ref[slot, :, mine_lo:mine_lo + hc] + other[slot]).astype(bf16)

        @pl.when(cc == 0)
        def _():
            run(0, hc)

        @pl.when(cc == 1)
        def _():
            run(hc, 0)

    return pl.pallas_call(
        body, name=name, in_specs=[_VMEM], out_specs=_VMEM,
        out_shape=jax.ShapeDtypeStruct((s, r, hc), bf16),
        scratch_shapes=[pltpu.VMEM((s, r, hc), f32), pltpu.SemaphoreType.DMA, pltpu.SemaphoreType.DMA],
        compiler_params=pltpu.CompilerParams(vmem_limit_bytes=VMEM_LIMIT),
    )(p)


def _chip_reduce_swap_cols(rcv, name):
    s, r, hc = rcv.shape

    def body(r_ref, g_ref, send_sem, recv_sem):
        x, y, cc = _my_place()
        acc = r_ref[0].astype(f32)
        for slot in range(1, s):
            acc = acc + r_ref[slot].astype(f32)
        g_ref[cc] = acc
        cp = pltpu.make_async_remote_copy(src_ref=g_ref.at[cc], dst_ref=g_ref.at[cc], send_sem=send_sem, recv_sem=recv_sem,
                                          device_id=(x, y, 1 - cc), device_id_type=_MESH)
        cp.start()
        cp.wait()

    both = pl.pallas_call(
        body, name=name, in_specs=[_VMEM], out_specs=_VMEM,
        out_shape=jax.ShapeDtypeStruct((2, r, hc), f32),
        scratch_shapes=[pltpu.SemaphoreType.DMA, pltpu.SemaphoreType.DMA],
        compiler_params=pltpu.CompilerParams(vmem_limit_bytes=VMEM_LIMIT),
    )(rcv)
    return both.transpose(1, 0, 2).reshape(r, 2 * hc)


def _add_cast(a, b, name):
    s, r, c = a.shape
    tr = _row_tile(r, c)

    def body(a_ref, b_ref, o_ref):
        o_ref[...] = (a_ref[...] + b_ref[...]).astype(bf16)

    blk = pl.BlockSpec((1, tr, c), lambda i, j: (i, j, 0))
    return pl.pallas_call(
        body, name=name, grid=(s, r // tr), in_specs=[blk, blk], out_specs=blk,
        out_shape=jax.ShapeDtypeStruct(a.shape, bf16),
        compiler_params=_cparams(2, arbitrary=False),
    )(a, b)


def _gather_copies(ins, outs, send_sems, recv_sems, local_sems):
    x, y, c = _my_place()
    me = 2 * x + y
    local, outgoing, incoming = [], [], []
    for a in range(len(ins)):
        local.append(pltpu.make_async_copy(ins[a], outs[a].at[me], local_sems.at[a]))
        for j, (px, py) in enumerate(_other_chips(x, y)):
            sems = dict(send_sem=send_sems.at[3 * a + j], recv_sem=recv_sems.at[3 * a + j], device_id=(px, py, c), device_id_type=_MESH)
            outgoing.append(pltpu.make_async_remote_copy(src_ref=ins[a], dst_ref=outs[a].at[me], **sems))
            incoming.append(pltpu.make_async_remote_copy(src_ref=ins[a], dst_ref=outs[a].at[2 * px + py], **sems))
    return local, outgoing, incoming


def _gather_chips(blocks, name):
    n = len(blocks)

    def body(*refs):
        local, outgoing, incoming = _gather_copies(refs[:n], refs[n:2 * n], *refs[2 * n:])
        for cp in local + outgoing:
            cp.start()
        for cp in incoming:
            cp.wait_recv()
        for cp in outgoing:
            cp.wait_send()
        for cp in local:
            cp.wait()

    return pl.pallas_call(
        body, name=name, in_specs=[_ANY] * n, out_specs=[_ANY] * n, out_shape=_gather_shapes(blocks),
        scratch_shapes=[pltpu.SemaphoreType.DMA((3 * n,)), pltpu.SemaphoreType.DMA((3 * n,)), pltpu.SemaphoreType.DMA((n,))],
    )(*blocks)


def _riding_call(body, name, nsteps, in_specs, out_specs, out_shape, scratch_shapes, operands, riders, copies, ride_shapes):
    nr, n_in, n_out, n_scr = len(riders), len(in_specs), len(out_specs), len(scratch_shapes)

    def wrapped(*refs):
        ins, ride_in = refs[:n_in], refs[n_in:n_in + nr]
        outs, ride_out = refs[n_in + nr:n_in + nr + n_out], refs[n_in + nr + n_out:n_in + 2 * nr + n_out]
        scratch, sems = refs[n_in + 2 * nr + n_out:n_in + 2 * nr + n_out + n_scr], refs[n_in + 2 * nr + n_out + n_scr:]
        if nr:
            @pl.when(pl.program_id(0) == 0)
            def _():
                local, outgoing, _ = copies(ride_in, ride_out, *sems)
                for cp in local + outgoing:
                    cp.start()

        body(*ins, *outs, *scratch)
        if nr:
            @pl.when(pl.program_id(0) == nsteps - 1)
            def _():
                local, outgoing, incoming = copies(ride_in, ride_out, *sems)
                for cp in incoming:
                    cp.wait_recv()
                for cp in outgoing:
                    cp.wait_send()
                for cp in local:
                    cp.wait()

    hbm = pl.BlockSpec(memory_space=pltpu.HBM)
    sems = [pltpu.SemaphoreType.DMA((3 * nr,)), pltpu.SemaphoreType.DMA((3 * nr,)), pltpu.SemaphoreType.DMA((nr,))] if nr else []
    res = pl.pallas_call(
        wrapped, name=name, grid=(nsteps,),
        in_specs=list(in_specs) + [hbm] * nr, out_specs=list(out_specs) + [hbm] * nr,
        out_shape=list(out_shape) + list(ride_shapes),
        scratch_shapes=list(scratch_shapes) + sems,
        compiler_params=_cparams(1),
    )(*operands, *riders)
    return list(res[:n_out]), list(res[n_out:])


def _gather_shapes(riders):
    return [jax.ShapeDtypeStruct((4,) + r.shape, r.dtype) for r in riders]


def _scatter_copies(ins, outs, send_sems, recv_sems, local_sems):
    x, y, c = _my_place()
    me = 2 * x + y
    local, outgoing, incoming = [], [], []
    for a in range(len(ins)):
        local.append(pltpu.make_async_copy(ins[a].at[me], outs[a].at[me], local_sems.at[a]))
        for j, (px, py) in enumerate(_other_chips(x, y)):
            sems = dict(send_sem=send_sems.at[3 * a + j], recv_sem=recv_sems.at[3 * a + j], device_id=(px, py, c), device_id_type=_MESH)
            outgoing.append(pltpu.make_async_remote_copy(src_ref=ins[a].at[2 * px + py], dst_ref=outs[a].at[me], **sems))
            incoming.append(pltpu.make_async_remote_copy(src_ref=ins[a].at[2 * px + py], dst_ref=outs[a].at[2 * px + py], **sems))
    return local, outgoing, incoming


def _scatter_start(ins, outs, send_sems, recv_sems, local_sems):
    local, outgoing, _ = _scatter_copies(ins, outs, send_sems, recv_sems, local_sems)
    for cp in local + outgoing:
        cp.start()


def _scatter_wait(ins, outs, send_sems, recv_sems, local_sems):
    local, outgoing, incoming = _scatter_copies(ins, outs, send_sems, recv_sems, local_sems)
    for cp in incoming:
        cp.wait_recv()
    for cp in outgoing:
        cp.wait_send()
    for cp in local:
        cp.wait()


def _scatter_chips(pieces, name):
    n = len(pieces)

    def body(*refs):
        ins, outs = refs[:n], refs[n:2 * n]
        _scatter_start(ins, outs, *refs[2 * n:])
        _scatter_wait(ins, outs, *refs[2 * n:])

    return pl.pallas_call(
        body, name=name,
        in_specs=[_ANY] * n, out_specs=[_ANY] * n,
        out_shape=[jax.ShapeDtypeStruct(s.shape, s.dtype) for s in pieces],
        scratch_shapes=[pltpu.SemaphoreType.DMA((3 * n,)), pltpu.SemaphoreType.DMA((3 * n,)), pltpu.SemaphoreType.DMA((n,))],
    )(*pieces)


def _swap_cores(blocks, name):
    n = len(blocks)
    parts = 4
    rows = [b.shape[0] // parts for b in blocks]

    def body(*refs):
        ins, outs = refs[:n], refs[n:2 * n]
        send_sems, recv_sems, local_sems = refs[2 * n:]
        x, y, c = _my_place()

        def remote(a, k, slot):
            rs = pl.ds(k * rows[a], rows[a])
            return pltpu.make_async_remote_copy(src_ref=ins[a].at[rs], dst_ref=outs[a].at[slot, rs], send_sem=send_sems.at[parts * a + k],
                                                recv_sem=recv_sems.at[parts * a + k], device_id=(x, y, 1 - c), device_id_type=_MESH)

        local = []
        for a in range(n):
            cp = pltpu.make_async_copy(ins[a], outs[a].at[c], local_sems.at[a])
            cp.start()
            local.append(cp)
            for k in range(parts):
                remote(a, k, c).start()
        for a in range(n):
            for k in range(parts):
                remote(a, k, 1 - c).wait()
            local[a].wait()

    return pl.pallas_call(
        body, name=name,
        in_specs=[_ANY] * n, out_specs=[_ANY] * n,
        out_shape=[jax.ShapeDtypeStruct((2,) + s.shape, s.dtype) for s in blocks],
        scratch_shapes=[pltpu.SemaphoreType.DMA((parts * n,)), pltpu.SemaphoreType.DMA((parts * n,)), pltpu.SemaphoreType.DMA((n,))],
    )(*blocks)


def _gather_all(block, name):
    def body(in_ref, out_ref, send_sems, recv_sems, local_sem):
        x, y, c = _my_place()
        me = 4 * x + 2 * y + c
        cp = pltpu.make_async_copy(in_ref, out_ref.at[me], local_sem)
        cp.start()
        peers = []
        for dx in range(2):
            for dy in range(2):
                for dc in range(2):
                    if dx or dy or dc:
                        peers.append((1 - x if dx else x, 1 - y if dy else y, 1 - c if dc else c))
        for j, pr in enumerate(peers):
            pltpu.make_async_remote_copy(src_ref=in_ref, dst_ref=out_ref.at[me], send_sem=send_sems.at[j], recv_sem=recv_sems.at[j],
                                         device_id=pr, device_id_type=_MESH).start()
        for j, (px, py, pc) in enumerate(peers):
            pltpu.make_async_remote_copy(src_ref=in_ref, dst_ref=out_ref.at[4 * px + 2 * py + pc], send_sem=send_sems.at[j], recv_sem=recv_sems.at[j],
                                         device_id=(px, py, pc), device_id_type=_MESH).wait()
        cp.wait()

    return pl.pallas_call(
        body, name=name,
        in_specs=[_ANY], out_specs=_ANY,
        out_shape=jax.ShapeDtypeStruct((8,) + block.shape, block.dtype),
        scratch_shapes=[pltpu.SemaphoreType.DMA((7,)), pltpu.SemaphoreType.DMA((7,)), pltpu.SemaphoreType.DMA],
    )(block)


def _row_tile(r, c):
    best = r
    for cand in range(16, r + 1, 16):
        if r % cand == 0 and cand * c * 4 <= (1 << 20):
            best = cand
    return best if best * c * 4 <= (4 << 20) else r


def _sum_slots(parts, name):
    n, r, c = parts.shape
    tr = _row_tile(r, c)

    def body(p_ref, o_ref):
        acc = p_ref[0].astype(f32)
        for s in range(1, n):
            acc = acc + p_ref[s].astype(f32)
        o_ref[...] = acc

    return pl.pallas_call(
        body, name=name, grid=(r // tr,),
        in_specs=[pl.BlockSpec((n, tr, c), lambda i: (0, i, 0))],
        out_specs=pl.BlockSpec((tr, c), lambda i: (i, 0)),
        out_shape=jax.ShapeDtypeStruct((r, c), f32),
        compiler_params=_cparams(1, arbitrary=False),
    )(parts)


def _adamw(parts, w, m, v, name):
    n, r, c = parts.shape
    tr = _row_tile(r, c)
    tc = c
    if tr == r and r * c * 4 > (1 << 20) and c % 256 == 0:
        tc = 256

    def body(p_ref, w_ref, m_ref, v_ref, g_ref, d_ref, nm_ref, nv_ref):
        g = p_ref[0]
        for s in range(1, n):
            g = g + p_ref[s]
        nm = B1 * m_ref[...] + (1.0 - B1) * g
        nv = B2 * v_ref[...] + (1.0 - B2) * (g * g)
        m_hat = nm / (1.0 - B1 ** STEP)
        v_hat = nv / (1.0 - B2 ** STEP)
        g_ref[...] = g
        nm_ref[...] = nm
        nv_ref[...] = nv
        d_ref[...] = -LR * (m_hat / (jnp.sqrt(v_hat) + EPS_ADAM) + WD * w_ref[...])

    blk = pl.BlockSpec((tr, tc), lambda i, j: (i, j))
    return pl.pallas_call(
        body, name=name, grid=(r // tr, c // tc),
        in_specs=[pl.BlockSpec((n, tr, tc), lambda i, j: (0, i, j)), blk, blk, blk],
        out_specs=[blk] * 4,
        out_shape=[jax.ShapeDtypeStruct((r, c), f32)] * 4,
        compiler_params=_cparams(2, arbitrary=False),
    )(parts, w, m, v)


_BIG = ["w_in", "w_out", "w_ffn_gate", "w_ffn_up", "w_ffn_down", "ple_w_proj", "ple_w_gate"]
_SMALL = ["b_in", "hg_lb_logits", "ml_conv_w", "ml_conv_b", "hg_norm_g", "ml_norm_g", "ln1_g", "ln1_b", "ln2_g", "ln2_b", "ple_b_gate"]
_ORDER = ["w_in", "b_in", "hg_lb_logits", "ml_conv_w", "ml_conv_b", "hg_norm_g", "ml_norm_g", "w_out", "ln1_g", "ln1_b",
          "w_ffn_gate", "w_ffn_up", "w_ffn_down", "ln2_g", "ln2_b", "ple_w_proj", "ple_w_gate", "ple_b_gate"]
_PACK_ROWS, _PACK_COLS = 16, 1024


def _pack(arrays):
    flat = jnp.concatenate([a.reshape(-1) for a in arrays])
    return jnp.pad(flat, (0, _PACK_ROWS * _PACK_COLS - flat.shape[0])).reshape(_PACK_ROWS, _PACK_COLS)


def _unpack(pack, shapes):
    flat = pack.reshape(-1)
    out, off = [], 0
    for s in shapes:
        size = 1
        for d in s:
            size *= d
        out.append(flat[off:off + size].reshape(s))
        off += size
    return out


def _to_chip_major(g, col_split):
    if col_split:
        k, n = g.shape
        return g.reshape(k, 4, n // 4).transpose(1, 0, 2)
    k, n = g.shape
    return g.reshape(4, k // 4, n)


def kernel(x, p, w_in, b_in, hg_lb_logits, ml_conv_w, ml_conv_b, hg_norm_g, ml_norm_g, w_out, ln1_g, ln1_b, w_ffn_gate, w_ffn_up, w_ffn_down, ln2_g, ln2_b, ple_w_proj, ple_w_gate, ple_b_gate, loss_target, m_w_in, m_b_in, m_hg_lb_logits, m_ml_conv_w, m_ml_conv_b, m_hg_norm_g, m_ml_norm_g, m_w_out, m_ln1_g, m_ln1_b, m_w_ffn_gate, m_w_ffn_up, m_w_ffn_down, m_ln2_g, m_ln2_b, m_ple_w_proj, m_ple_w_gate, m_ple_b_gate, v_w_in, v_b_in, v_hg_lb_logits, v_ml_conv_w, v_ml_conv_b, v_hg_norm_g, v_ml_norm_g, v_w_out, v_ln1_g, v_ln1_b, v_w_ffn_gate, v_w_ffn_up, v_w_ffn_down, v_ln2_g, v_ln2_b, v_ple_w_proj, v_ple_w_gate, v_ple_b_gate):
    args = dict(locals())
    wts = {k: args[k] for k in _ORDER}
    mom = {k: args["m_" + k] for k in _ORDER}
    var = {k: args["v_" + k] for k in _ORDER}
    two_d = lambda a: a.reshape(a.shape[-2], a.shape[-1])
    block = lambda k, a: jnp.swapaxes(two_d(a), 0, 1) if k in _TRANSPOSED else two_d(a)
    unblock = lambda k, a: (jnp.swapaxes(a, 0, 1) if k in _TRANSPOSED else a).reshape(wts[k].shape)

    shards = {k: block(k, wts[k]).astype(bf16) for k in _BIG}
    w_in_blocks, taps = _gather_chips([shards["w_in"], two_d(ml_conv_w)], "gather_w_in")
    w_in_full = _from_chip_major(w_in_blocks, False)
    conv_w_full = _from_chip_major(taps, True)

    def core_sum(k, g):
        pieces = _to_chip_major(g, k in _COL_SPLIT)
        if pieces.shape[1] % (2 * _EX_ROWS):
            return _pair_reduce_cols(pieces, "pair_reduce_" + k)
        return _pair_reduce(pieces, "pair_reduce_" + k)

    early_keys = _BIG[1:]
    loss, grad_x, grads, early_received = _local_step(
        x[0], p[0, 0], loss_target[0], w_in_full, b_in, hg_lb_logits, conv_w_full, ml_conv_b, hg_norm_g, ml_norm_g,
        None, ln1_g, ln1_b, None, None, None, ln2_g, ln2_b, None, None, ple_b_gate,
        early_hook=lambda early: [core_sum(k, early[k]) for k in early_keys],
        late_shards={k: shards[k] for k in early_keys})

    received = list(_scatter_chips([core_sum("w_in", grads["w_in"])], "scatter_grad_w_in")) + list(early_received)
    out_g, out_d, out_m, out_v = {}, {}, {}, {}
    for k, rcv in zip(_BIG, received):
        own = block(k, wts[k])
        if rcv.shape[1] == own.shape[0]:
            whole = _chip_reduce_swap_cols(rcv, "chip_reduce_" + k)
        else:
            parts = _chip_reduce_swap(rcv, "chip_reduce_" + k)
            whole = parts.reshape(2 * parts.shape[1], parts.shape[2])
        g, d, nm, nv = _adamw(whole[None], own, block(k, mom[k]), block(k, var[k]), "adamw_" + k)
        out_g[k], out_d[k], out_m[k], out_v[k] = unblock(k, g), unblock(k, d), unblock(k, nm), unblock(k, nv)

    small_shapes = [(1, PROJ_W), (2, MIX_W), (CONV_K, MIX_W)] + [(1, MIX_W)] * 3 + [(1, D_MODEL)] * 5 + [(1, 1)]
    contrib = _pack([grads[k] for k in _SMALL] + [loss])
    summed = _sum_slots(_gather_all(contrib, "gather_small"), "sum_small")
    small = _unpack(summed, small_shapes)
    loss_total = small[-1].reshape(())
    gsm = dict(zip(_SMALL, small[:-1]))
    place = 2 * lax.axis_index("x") + lax.axis_index("y")
    conv_cols = ml_conv_w.shape[-1]
    gsm["ml_conv_w"] = lax.dynamic_slice(gsm["ml_conv_w"], (0, place * conv_cols), (CONV_K, conv_cols))
    own_shapes = [wts[k].shape for k in _SMALL]
    g_pack = _pack([gsm[k] for k in _SMALL])
    res = _adamw(g_pack[None], _pack([wts[k] for k in _SMALL]), _pack([mom[k] for k in _SMALL]), _pack([var[k] for k in _SMALL]), "adamw_small")
    for dst, pack in zip((out_g, out_d, out_m, out_v), res):
        for k, a in zip(_SMALL, _unpack(pack, own_shapes)):
            dst[k] = a

    outs = [loss_total, grad_x[None]]
    for group in (out_g, out_d, out_m, out_v):
        outs += [group[k] for k in _ORDER]
    return tuple(outs)
```

```python
import functools

import jax
import jax.numpy as jnp
from jax import lax
from jax.experimental import pallas as pl
from jax.experimental.pallas import tpu as pltpu

f32 = jnp.float32
bf16 = jnp.bfloat16
HI = lax.Precision.HIGHEST

D_MODEL = 1024
HEADS = 4
HEAD_W = 128
MIX_W = HEADS * HEAD_W
ML_DQK = 64
PROJ_W = 3592
U_HG = 4 * MIX_W
U_ML = 3 * MIX_W + 128
D_FF = 2816
PLE = 256
CHUNK = 128
SUB = 16
EXP_CAP = 80.0
CONV_K = 4
HALO = 8
ALPHA = float(2.0 ** 0.25)
LN_EPS = 1e-5
RMS_EPS = 1e-6
NEG = -1e30
LR, B1, B2, EPS_ADAM, WD, STEP = 0.001, 0.9, 0.999, 1e-08, 0.01, 10
VMEM_LIMIT = 56 * 1024 * 1024
DENSE_ROWS = 512
WGRAD_ROWS = 2048


def _cparams(n_axes, arbitrary=True):
    sem = ("arbitrary",) * n_axes if arbitrary else ("parallel",) * n_axes
    return pltpu.CompilerParams(dimension_semantics=sem, vmem_limit_bytes=VMEM_LIMIT)


ACT = bf16


def _mx(a):
    return a.astype(ACT)


def _bdot(a, b):
    return jnp.dot(_mx(a), _mx(b), preferred_element_type=f32)


def _bdot_nt(a, b):
    return lax.dot_general(_mx(a), _mx(b), (((1,), (1,)), ((), ())), preferred_element_type=f32)


def _bdot_tn(a, b):
    return lax.dot_general(_mx(a), _mx(b), (((0,), (0,)), ((), ())), preferred_element_type=f32)


def _split3(x):
    hi = x.astype(bf16)
    r1 = x - hi.astype(f32)
    mid = r1.astype(bf16)
    lo = (r1 - mid.astype(f32)).astype(bf16)
    return hi, mid, lo


def _dot3(a, b, dims):
    a_hi = a.astype(bf16)
    a_lo = (a - a_hi.astype(f32)).astype(bf16)
    b_hi = b.astype(bf16)
    b_lo = (b - b_hi.astype(f32)).astype(bf16)
    dn = (dims, ((), ()))
    return (lax.dot_general(a_hi, b_hi, dn, preferred_element_type=f32) + lax.dot_general(a_hi, b_lo, dn, preferred_element_type=f32)
            + lax.dot_general(a_lo, b_hi, dn, preferred_element_type=f32))


def _lane_sum(x):
    hi = x.astype(bf16)
    lo = (x - hi.astype(f32)).astype(bf16)
    ones = jnp.ones((x.shape[1], 128), bf16)
    return jnp.dot(hi, ones, preferred_element_type=f32) + jnp.dot(lo, ones, preferred_element_type=f32)


def _lane_dot(x, row):
    return _dot3(x, jnp.broadcast_to(row, (128, row.shape[1])), ((1,), (1,)))


def _sel_dot(sel, x):
    sb = sel.astype(bf16)
    return sum(jnp.dot(sb, part, preferred_element_type=f32) for part in _split3(x))


def _sel_dot_nt(sel, x):
    sb = sel.astype(bf16)
    return sum(lax.dot_general(sb, part, (((1,), (1,)), ((), ())), preferred_element_type=f32) for part in _split3(x))


def _sigmoid(x):
    return 1.0 / (1.0 + jnp.exp(-x))


def _log_sigmoid(x):
    return jnp.minimum(x, 0.0) - jnp.log(1.0 + jnp.exp(-jnp.abs(x)))


def _tri(n, upper=False):
    r = lax.broadcasted_iota(jnp.int32, (n, n), 0)
    c = lax.broadcasted_iota(jnp.int32, (n, n), 1)
    return (c >= r) if upper else (c <= r)


def _rows(tm, n, col=0):
    return pl.BlockSpec((tm, n), lambda i, _c=col: (i, _c))


def _rows_rev(tm, n, nb, col=0):
    return pl.BlockSpec((tm, n), lambda i, _c=col, _nb=nb: (_nb - 1 - i, _c))


def _const(shape):
    return pl.BlockSpec(shape, lambda i, _n=len(shape): (0,) * _n)


def _resident(shape):
    return pl.BlockSpec(shape, lambda i, _n=len(shape): (0,) * _n, pipeline_mode=pl.Buffered(1))


def _tile(t, want):
    return want if t % want == 0 else t


def _inproj(x, w_hg, w_ml, b_hg, b_ml, riders=()):
    t = x.shape[0]
    tm = _tile(t, DENSE_ROWS)

    def body(x_ref, whg_ref, wml_ref, bhg_ref, bml_ref, uhg_ref, uml_ref, xb_ref):
        xb = _mx(x_ref[...])
        xb_ref[...] = xb
        uhg_ref[...] = _bdot_nt(xb, whg_ref[...]) + bhg_ref[...]
        uml_ref[...] = _bdot_nt(xb, wml_ref[...]) + bml_ref[...]

    return _riding_call(
        body, "inproj", t // tm,
        in_specs=[_rows(tm, D_MODEL), _resident((U_HG, D_MODEL)), _resident((U_ML, D_MODEL)), _const((1, U_HG)), _const((1, U_ML))],
        out_specs=[_rows(tm, U_HG), _rows(tm, U_ML), _rows(tm, D_MODEL)],
        out_shape=[jax.ShapeDtypeStruct((t, U_HG), f32), jax.ShapeDtypeStruct((t, U_ML), f32), jax.ShapeDtypeStruct((t, D_MODEL), ACT)],
        scratch_shapes=[], operands=(x, w_hg, w_ml, b_hg, b_ml), riders=riders, copies=_gather_copies, ride_shapes=_gather_shapes(riders))


def _hg_gates(hq, hf, lb, tri):
    s = _sigmoid(hf)
    om = 1.0 - lb
    f = lb + om * s
    g = jnp.log(f)
    k = om * (1.0 - s)
    sq = _sigmoid(hq)
    q = hq * sq
    b = _sel_dot(tri, g)
    return q, sq, s, f, k, b


def _hg_scores(q, k, b, tril_mask):
    qts, kts, eqs, eks, rows = [], [], [], [], []
    for i in range(CHUNK // SUB):
        lo = i * SUB
        ref = jnp.zeros_like(b[0:1]) if i == 0 else b[lo - 1:lo]
        eq = jnp.exp(b[lo:lo + SUB] - ref)
        ek = jnp.exp(jnp.minimum(ref - b, EXP_CAP))
        qt = q[lo:lo + SUB] * eq
        kt = k * ek
        rows.append(_bdot_nt(qt, kt))
        qts.append(qt); kts.append(kt); eqs.append(eq); eks.append(ek)
    a = jnp.where(tril_mask, jnp.concatenate(rows, axis=0), 0.0)
    return a, qts, kts, eqs, eks


def _head_rms(o, gn, on_mxu=False):
    ms = _lane_sum(o * o) * (1.0 / o.shape[1]) if on_mxu else jnp.mean(o * o, axis=-1, keepdims=True)
    rstd = lax.rsqrt(ms + RMS_EPS)
    oh = o * rstd
    return oh, rstd, oh * gn


def _lower_bound(logit_ref):
    lg = logit_ref[...]
    return _sigmoid(lg[0:1] - lg[1:2])


def _hgrn2_fwd(u_hg, logits, gn, riders=()):
    t = u_hg.shape[0]
    tb = _tile(t, 256)
    nc_blk = tb // CHUNK

    def body(u_ref, lg_ref, gn_ref, og_ref, sst_ref, st_ref):
        @pl.when(pl.program_id(0) == 0)
        def _():
            st_ref[...] = jnp.zeros_like(st_ref)

        lb_all = _lower_bound(lg_ref)
        tril_mask = _tri(CHUNK)
        tri = tril_mask.astype(f32)

        def chunk(c, carry):
            r0 = pl.multiple_of(c * CHUNK, CHUNK)
            rows = pl.ds(r0, CHUNK)
            heads = range(HEADS)
            cols = [slice(h * HEAD_W, (h + 1) * HEAD_W) for h in heads]
            hv = [u_ref[rows, 2 * MIX_W + h * HEAD_W:2 * MIX_W + (h + 1) * HEAD_W] for h in heads]
            gts = [_hg_gates(u_ref[rows, h * HEAD_W:(h + 1) * HEAD_W], u_ref[rows, MIX_W + h * HEAD_W:MIX_W + (h + 1) * HEAD_W],
                             lb_all[:, cols[h]], tri) for h in heads]
            q = [g[0] for g in gts]
            k = [g[4] for g in gts]
            b = [g[5] for g in gts]
            a = [_hg_scores(q[h], k[h], b[h], tril_mask)[0] for h in heads]
            st = [st_ref[h] for h in heads]
            bl = [b[h][CHUNK - 1:CHUNK] for h in heads]
            o = [_bdot(a[h], hv[h]) + _bdot_nt(q[h] * jnp.exp(b[h]), st[h]) for h in heads]
            new_st = [st[h] * jnp.exp(bl[h]) + _bdot_tn(hv[h], k[h] * jnp.exp(bl[h] - b[h])) for h in heads]
            for h in heads:
                sst_ref[c, h] = st[h]
                st_ref[h] = new_st[h]
                hgate = u_ref[rows, 3 * MIX_W + h * HEAD_W:3 * MIX_W + (h + 1) * HEAD_W]
                _, _, y = _head_rms(o[h], gn_ref[:, cols[h]])
                og_ref[rows, cols[h]] = (y * (hgate * _sigmoid(hgate))).astype(ACT)
            return carry

        lax.fori_loop(0, nc_blk, chunk, 0, unroll=True)

    return _riding_call(
        body, "hgrn2_fwd", t // tb,
        in_specs=[_rows(tb, U_HG), _const((2, MIX_W)), _const((1, MIX_W))],
        out_specs=[_rows(tb, MIX_W), pl.BlockSpec((nc_blk, HEADS, HEAD_W, HEAD_W), lambda i: (i, 0, 0, 0))],
        out_shape=[jax.ShapeDtypeStruct((t, MIX_W), ACT), jax.ShapeDtypeStruct((t // CHUNK, HEADS, HEAD_W, HEAD_W), f32)],
        scratch_shapes=[pltpu.VMEM((HEADS, HEAD_W, HEAD_W), f32)],
        operands=(u_hg, logits, gn), riders=riders, copies=_gather_copies, ride_shapes=_gather_shapes(riders))


def _hgrn2_bwd(u_hg, logits, gn, sst, dog, riders=()):
    t = u_hg.shape[0]
    tb = _tile(t, 256)
    nb = t // tb
    nc_blk = tb // CHUNK
    nr = len(riders)

    def body(*refs):
        u_ref, lg_ref, gn_ref, sst_ref, dog_ref = refs[:5]
        ride_in = refs[5:5 + nr]
        du_ref, dlg_ref, dgn_ref = refs[5 + nr:8 + nr]
        ride_out = refs[8 + nr:8 + 2 * nr]
        dst_ref = refs[8 + 2 * nr]
        ride_sems = refs[9 + 2 * nr:]

        @pl.when(pl.program_id(0) == 0)
        def _():
            dst_ref[...] = jnp.zeros_like(dst_ref)
            dlg_ref[...] = jnp.zeros_like(dlg_ref)
            dgn_ref[...] = jnp.zeros_like(dgn_ref)
            if nr:
                _scatter_start(ride_in, ride_out, *ride_sems)

        lb_all = _lower_bound(lg_ref)
        tril_mask = _tri(CHUNK)
        tri = tril_mask.astype(f32)
        triu = _tri(CHUNK, upper=True).astype(f32)

        def chunk(j, carry):
            c = nc_blk - 1 - j
            r0 = pl.multiple_of(c * CHUNK, CHUNK)
            rows = pl.ds(r0, CHUNK)
            heads = range(HEADS)
            nsub = CHUNK // SUB
            cols = [slice(h * HEAD_W, (h + 1) * HEAD_W) for h in heads]
            hq = [u_ref[rows, h * HEAD_W:(h + 1) * HEAD_W] for h in heads]
            hf = [u_ref[rows, MIX_W + h * HEAD_W:MIX_W + (h + 1) * HEAD_W] for h in heads]
            hv = [u_ref[rows, 2 * MIX_W + h * HEAD_W:2 * MIX_W + (h + 1) * HEAD_W] for h in heads]
            lb = [lb_all[:, cols[h]] for h in heads]
            gts = [_hg_gates(hq[h], hf[h], lb[h], tri) for h in heads]
            q, sq, s, f, k, b = ([g[n] for g in gts] for n in range(6))
            scs = [_hg_scores(q[h], k[h], b[h], tril_mask) for h in heads]
            a, qts, kts, eqs, eks = ([sc[n] for sc in scs] for n in range(5))
            st = [sst_ref[c, h] for h in heads]
            dst = [dst_ref[h] for h in heads]
            bl = [b[h][CHUNK - 1:CHUNK] for h in heads]
            eb = [jnp.exp(b[h]) for h in heads]
            qh = [q[h] * eb[h] for h in heads]
            ekl = [jnp.exp(bl[h] - b[h]) for h in heads]
            kh = [k[h] * ekl[h] for h in heads]
            o = [_bdot(a[h], hv[h]) + _bdot_nt(qh[h], st[h]) for h in heads]
            do = []
            for h in heads:
                hgate = u_ref[rows, 3 * MIX_W + h * HEAD_W:3 * MIX_W + (h + 1) * HEAD_W]
                gnh = gn_ref[:, cols[h]]
                oh, rstd, y = _head_rms(o[h], gnh)
                sg = _sigmoid(hgate)
                dogh = dog_ref[rows, cols[h]]
                dy = dogh * (hgate * sg)
                du_ref[rows, 3 * MIX_W + h * HEAD_W:3 * MIX_W + (h + 1) * HEAD_W] = (dogh * y * (sg * (1.0 + hgate * (1.0 - sg)))).astype(ACT)
                dgn_ref[:, cols[h]] += jnp.sum(dy * oh, axis=0, keepdims=True)
                doh = dy * gnh
                do.append(rstd * (doh - oh * jnp.mean(doh * oh, axis=-1, keepdims=True)))
            da = [jnp.where(tril_mask, _bdot_nt(do[h], hv[h]), 0.0) for h in heads]
            dv = [_bdot_tn(a[h], do[h]) + _bdot_nt(kh[h], dst[h]) for h in heads]
            dq = [_bdot(do[h], st[h]) * eb[h] for h in heads]
            dk = [_bdot(hv[h], dst[h]) * ekl[h] for h in heads]
            d_last = [jnp.sum(k[h] * dk[h], axis=0, keepdims=True) + jnp.exp(bl[h]) * jnp.sum(dst[h] * st[h], axis=0, keepdims=True)
                      for h in heads]
            d_b = [q[h] * dq[h] - k[h] * dk[h] for h in heads]
            dqs = [[] for _ in heads]
            q_dq = [[] for _ in heads]
            for i in range(nsub):
                for h in heads:
                    da_i = _mx(da[h][i * SUB:(i + 1) * SUB])
                    q_r, k_r = _mx(qts[h][i]), _mx(kts[h][i])
                    g_q = jnp.dot(da_i, k_r, preferred_element_type=f32)
                    g_k = lax.dot_general(da_i, q_r, (((0,), (0,)), ((), ())), preferred_element_type=f32)
                    dqs[h].append(g_q * eqs[h][i])
                    q_dq[h].append(q_r.astype(f32) * g_q)
                    dk[h] = dk[h] + g_k * eks[h][i]
                    d_b[h] = d_b[h] - k_r.astype(f32) * g_k
            for h in heads:
                dq[h] = dq[h] + jnp.concatenate(dqs[h], axis=0)
                d_b[h] = d_b[h] + jnp.concatenate(q_dq[h], axis=0)
                dst_ref[h] = dst[h] * jnp.exp(bl[h]) + _bdot_tn(do[h], qh[h])
            dg = [_sel_dot(triu, d_b[h]) + d_last[h] for h in heads]
            for h in heads:
                dfk = dg[h] / f[h] - dk[h]
                du_ref[rows, h * HEAD_W:(h + 1) * HEAD_W] = (dq[h] * (sq[h] * (1.0 + hq[h] * (1.0 - sq[h])))).astype(ACT)
                du_ref[rows, MIX_W + h * HEAD_W:MIX_W + (h + 1) * HEAD_W] = ((1.0 - lb[h]) * dfk * s[h] * (1.0 - s[h])).astype(ACT)
                du_ref[rows, 2 * MIX_W + h * HEAD_W:2 * MIX_W + (h + 1) * HEAD_W] = dv[h].astype(ACT)
                dlb = jnp.sum((1.0 - s[h]) * dfk, axis=0, keepdims=True) * (lb[h] * (1.0 - lb[h]))
                dlg_ref[0:1, cols[h]] += dlb
                dlg_ref[1:2, cols[h]] -= dlb
            return carry

        lax.fori_loop(0, nc_blk, chunk, 0, unroll=True)

        if nr:
            @pl.when(pl.program_id(0) == nb - 1)
            def _():
                _scatter_wait(ride_in, ride_out, *ride_sems)

    hbm = pl.BlockSpec(memory_space=pltpu.HBM)
    ride_scratch = [pltpu.SemaphoreType.DMA((3 * nr,)), pltpu.SemaphoreType.DMA((3 * nr,)), pltpu.SemaphoreType.DMA((nr,))] if nr else []
    return pl.pallas_call(
        body, name="hgrn2_bwd", grid=(nb,),
        in_specs=[_rows_rev(tb, U_HG, nb), _const((2, MIX_W)), _const((1, MIX_W)),
                  pl.BlockSpec((nc_blk, HEADS, HEAD_W, HEAD_W), lambda i: (nb - 1 - i, 0, 0, 0)), _rows_rev(tb, MIX_W, nb)] + [hbm] * nr,
        out_specs=[_rows_rev(tb, U_HG, nb), _const((2, MIX_W)), _const((1, MIX_W))] + [hbm] * nr,
        out_shape=[jax.ShapeDtypeStruct((t, U_HG), ACT), jax.ShapeDtypeStruct((2, MIX_W), f32), jax.ShapeDtypeStruct((1, MIX_W), f32)]
        + [jax.ShapeDtypeStruct(r.shape, r.dtype) for r in riders],
        scratch_shapes=[pltpu.VMEM((HEADS, HEAD_W, HEAD_W), f32)] + ride_scratch,
        compiler_params=_cparams(1),
    )(u_hg, logits, gn, sst, dog, *riders)


def _conv_fwd(u_ml, w, b):
    t = u_ml.shape[0]
    tm = _tile(t, 512)

    def body(x_ref, w_ref, b_ref, pre_ref, act_ref, xbuf):
        @pl.when(pl.program_id(0) == 0)
        def _():
            xbuf[...] = jnp.zeros_like(xbuf)

        xbuf[0:HALO, :] = xbuf[tm:tm + HALO, :]
        xbuf[HALO:HALO + tm, :] = x_ref[...]
        pre = b_ref[...] + jnp.zeros((tm, MIX_W), f32)
        for kk in range(CONV_K):
            off = HALO - (CONV_K - 1) + kk
            pre = pre + w_ref[kk:kk + 1, :] * xbuf[off:off + tm, :]
        pre_ref[...] = pre
        act_ref[...] = pre * _sigmoid(pre)

    return pl.pallas_call(
        body, name="conv_fwd", grid=(t // tm,),
        in_specs=[_rows(tm, MIX_W), _const((CONV_K, MIX_W)), _const((1, MIX_W))],
        out_specs=[_rows(tm, MIX_W), _rows(tm, MIX_W)],
        out_shape=[jax.ShapeDtypeStruct((t, MIX_W), f32)] * 2,
        scratch_shapes=[pltpu.VMEM((tm + HALO, MIX_W), f32)],
        compiler_params=_cparams(1),
    )(u_ml, w, b)


def _conv_bwd(u_ml, w, pre, dact):
    t = u_ml.shape[0]
    tm = _tile(t, 512)
    nb = t // tm
    hb = tm // HALO

    def body(x_ref, halo_ref, w_ref, pre_ref, dact_ref, dx_ref, dw_ref, db_ref, dbuf, xbuf):
        i = pl.program_id(0)

        @pl.when(i == 0)
        def _():
            dbuf[...] = jnp.zeros_like(dbuf)
            dw_ref[...] = jnp.zeros_like(dw_ref)
            db_ref[...] = jnp.zeros_like(db_ref)

        p = pre_ref[...]
        sg = _sigmoid(p)
        dpre = dact_ref[...] * (sg * (1.0 + p * (1.0 - sg)))
        dbuf[tm:tm + HALO, :] = dbuf[0:HALO, :]
        dbuf[0:tm, :] = dpre
        has_prev = (i < nb - 1).astype(f32)
        xbuf[0:HALO, :] = halo_ref[...] * has_prev
        xbuf[HALO:HALO + tm, :] = x_ref[...]
        dx = jnp.zeros((tm, MIX_W), f32)
        for kk in range(CONV_K):
            back = CONV_K - 1 - kk
            dx = dx + w_ref[kk:kk + 1, :] * dbuf[back:back + tm, :]
            off = HALO - (CONV_K - 1) + kk
            dw_ref[kk:kk + 1, :] += jnp.sum(dpre * xbuf[off:off + tm, :], axis=0, keepdims=True)
        dx_ref[...] = dx.astype(ACT)
        db_ref[...] += jnp.sum(dpre, axis=0, keepdims=True)

    return pl.pallas_call(
        body, name="conv_bwd", grid=(nb,),
        in_specs=[_rows_rev(tm, MIX_W, nb),
                  pl.BlockSpec((HALO, MIX_W), lambda i: (jnp.maximum((nb - 1 - i) * hb - 1, 0), 0)),
                  _const((CONV_K, MIX_W)), _rows_rev(tm, MIX_W, nb), _rows_rev(tm, MIX_W, nb)],
        out_specs=[_rows_rev(tm, MIX_W, nb), _const((CONV_K, MIX_W)), _const((1, MIX_W))],
        out_shape=[jax.ShapeDtypeStruct((t, MIX_W), ACT), jax.ShapeDtypeStruct((CONV_K, MIX_W), f32), jax.ShapeDtypeStruct((1, MIX_W), f32)],
        scratch_shapes=[pltpu.VMEM((tm + HALO, MIX_W), f32), pltpu.VMEM((tm + HALO, MIX_W), f32)],
        compiler_params=_cparams(1),
    )(u_ml, u_ml, w, pre, dact)


def _lane_pick(x, lane):
    idx = lax.broadcasted_iota(jnp.int32, x.shape, 1)
    return jnp.sum(jnp.where(idx == lane, x, 0.0), axis=-1, keepdims=True)


def _ml_gate_forms(gates, tri):
    lf = _log_sigmoid(gates)
    gc = _sel_dot(tri, lf)
    lane = lax.broadcasted_iota(jnp.int32, gates.shape, 1)
    mixed = jnp.where(lane < HEADS, gates, gc)
    sel = (lax.broadcasted_iota(jnp.int32, (8, 128), 0) == lax.broadcasted_iota(jnp.int32, (8, 128), 1)).astype(f32)
    rowsf = _sel_dot_nt(sel, mixed)
    return gc, rowsf


def _ml_chunk(q, k, v, gates, gc, rowsf, c_st, n_st, m_st, tril_mask):
    hs = range(HEADS)
    g_col = [_lane_pick(gc, HEADS + h) for h in hs]
    ig_col = [_lane_pick(gates, h) for h in hs]
    dmat = [jnp.where(tril_mask, g_col[h] - rowsf[HEADS + h:HEADS + h + 1, :] + rowsf[h:h + 1, :], NEG) for h in hs]
    m_inter = [g_col[h] + m_st[h] for h in hs]
    m_t = [jnp.maximum(m_inter[h], jnp.max(dmat[h], axis=-1, keepdims=True)) for h in hs]
    wi = [jnp.exp(dmat[h] - m_t[h]) for h in hs]
    wo = [jnp.exp(m_inter[h] - m_t[h]) for h in hs]
    qk = [_bdot_nt(q[h], k[h]) * wi[h] for h in hs]
    num = [_bdot(qk[h], v[h]) + wo[h] * _bdot(q[h], c_st[h]) for h in hs]
    den = [_lane_sum(qk[h]) + wo[h] * _lane_dot(q[h], n_st[h]) for h in hs]
    floor = [jnp.exp(-m_t[h]) for h in hs]
    z = [jnp.maximum(jnp.abs(den[h]), floor[h]) for h in hs]
    g_last = [g_col[h][CHUNK - 1:CHUNK] for h in hs]
    a_col = [g_last[h] - g_col[h] + ig_col[h] for h in hs]
    m_new = [jnp.maximum(g_last[h] + m_st[h], jnp.max(a_col[h], axis=0, keepdims=True)) for h in hs]
    ws = [jnp.exp(a_col[h] - m_new[h]) for h in hs]
    w_old = [jnp.exp(g_last[h] + m_st[h] - m_new[h]) for h in hs]
    return dict(wi=wi, wo=wo, qk=qk, num=num, den=den, z=z, floor=floor, ws=ws, w_old=w_old, m_new=m_new)


def _mlstm_fwd(qkc, u_ml, gn, riders=()):
    t = qkc.shape[0]
    tb = _tile(t, 256)
    nc_blk = tb // CHUNK

    def body(qk_ref, v_ref, mo_ref, gt_ref, gn_ref, og_ref, cst_ref, nst_ref, mst_ref, c_sc, n_sc, m_sc):
        @pl.when(pl.program_id(0) == 0)
        def _():
            c_sc[...] = jnp.zeros_like(c_sc)
            n_sc[...] = jnp.zeros_like(n_sc)
            m_sc[...] = jnp.zeros_like(m_sc)

        tril_mask = _tri(CHUNK)
        tri = tril_mask.astype(f32)

        def chunk(c, carry):
            r0 = pl.multiple_of(c * CHUNK, CHUNK)
            rows = pl.ds(r0, CHUNK)
            gates = gt_ref[rows, :]
            gc, rowsf = _ml_gate_forms(gates, tri)
            hs = range(HEADS)
            q = [qk_ref[rows, h * ML_DQK:(h + 1) * ML_DQK] * (ML_DQK ** -0.5) for h in hs]
            k = [qk_ref[rows, HEADS * ML_DQK + h * ML_DQK:HEADS * ML_DQK + (h + 1) * ML_DQK] for h in hs]
            v = [v_ref[rows, h * HEAD_W:(h + 1) * HEAD_W] for h in hs]
            c_st = [c_sc[h] for h in hs]
            n_st = [n_sc[h] for h in hs]
            m_full = [m_sc[h] for h in hs]
            r = _ml_chunk(q, k, v, gates, gc, rowsf, c_st, n_st, [m[:, 0:1] for m in m_full], tril_mask)
            ksc = [k[h] * r["ws"][h] for h in hs]
            new_c = [r["w_old"][h] * c_st[h] + _bdot_tn(ksc[h], v[h]) for h in hs]
            for h in hs:
                cs = slice(h * HEAD_W, (h + 1) * HEAD_W)
                cst_ref[c, h] = c_st[h]
                nst_ref[c, h] = n_st[h]
                mst_ref[c, h] = m_full[h]
                c_sc[h] = new_c[h]
                n_sc[h] = r["w_old"][h] * n_st[h] + jnp.sum(ksc[h], axis=0, keepdims=True)
                m_sc[h] = r["m_new"][h] + jnp.zeros((1, 128), f32)
                _, _, y = _head_rms(r["num"][h] / r["z"][h], gn_ref[:, cs], on_mxu=True)
                og_ref[rows, cs] = (y * _sigmoid(mo_ref[rows, h * HEAD_W:(h + 1) * HEAD_W])).astype(ACT)
            return carry

        lax.fori_loop(0, nc_blk, chunk, 0)

    nchunks = t // CHUNK
    return _riding_call(
        body, "mlstm_fwd", t // tb,
        in_specs=[_rows(tb, MIX_W), _rows(tb, MIX_W, 1), _rows(tb, MIX_W, 2), _rows(tb, 128, 12), _const((1, MIX_W))],
        out_specs=[_rows(tb, MIX_W),
                   pl.BlockSpec((nc_blk, HEADS, ML_DQK, HEAD_W), lambda i: (i, 0, 0, 0)),
                   pl.BlockSpec((nc_blk, HEADS, 1, ML_DQK), lambda i: (i, 0, 0, 0)),
                   pl.BlockSpec((nc_blk, HEADS, 1, 128), lambda i: (i, 0, 0, 0))],
        out_shape=[jax.ShapeDtypeStruct((t, MIX_W), ACT),
                   jax.ShapeDtypeStruct((nchunks, HEADS, ML_DQK, HEAD_W), f32),
                   jax.ShapeDtypeStruct((nchunks, HEADS, 1, ML_DQK), f32),
                   jax.ShapeDtypeStruct((nchunks, HEADS, 1, 128), f32)],
        scratch_shapes=[pltpu.VMEM((HEADS, ML_DQK, HEAD_W), f32), pltpu.VMEM((HEADS, 1, ML_DQK), f32), pltpu.VMEM((HEADS, 1, 128), f32)],
        operands=(qkc, u_ml, u_ml, u_ml, gn), riders=riders, copies=_gather_copies, ride_shapes=_gather_shapes(riders))


def _mlstm_bwd(qkc, u_ml, gn, cst, nst, mst, dog):
    t = qkc.shape[0]
    tb = _tile(t, 256)
    nb = t // tb
    nc_blk = tb // CHUNK

    def body(qk_ref, v_ref, mo_ref, gt_ref, gn_ref, cst_ref, nst_ref, mst_ref, dog_ref,
             dqk_ref, dv_ref, dmo_ref, dgt_ref, dgn_ref, dc_sc, dn_sc):
        @pl.when(pl.program_id(0) == 0)
        def _():
            dc_sc[...] = jnp.zeros_like(dc_sc)
            dn_sc[...] = jnp.zeros_like(dn_sc)
            dgn_ref[...] = jnp.zeros_like(dgn_ref)

        tril_mask = _tri(CHUNK)
        tri = tril_mask.astype(f32)
        triu = _tri(CHUNK, upper=True).astype(f32)
        lane = lax.broadcasted_iota(jnp.int32, (CHUNK, 128), 1)

        def chunk(j, carry):
            c = nc_blk - 1 - j
            r0 = pl.multiple_of(c * CHUNK, CHUNK)
            rows = pl.ds(r0, CHUNK)
            gates = gt_ref[rows, :]
            gc, rowsf = _ml_gate_forms(gates, tri)
            dg_mat = jnp.zeros((CHUNK, 128), f32)
            dig_mat = jnp.zeros((CHUNK, 128), f32)
            dlast_row = jnp.zeros((1, 128), f32)
            hs = range(HEADS)
            cols = [slice(h * HEAD_W, (h + 1) * HEAD_W) for h in hs]
            q = [qk_ref[rows, h * ML_DQK:(h + 1) * ML_DQK] * (ML_DQK ** -0.5) for h in hs]
            k = [qk_ref[rows, HEADS * ML_DQK + h * ML_DQK:HEADS * ML_DQK + (h + 1) * ML_DQK] for h in hs]
            v = [v_ref[rows, h * HEAD_W:(h + 1) * HEAD_W] for h in hs]
            c_st = [cst_ref[c, h] for h in hs]
            n_st = [nst_ref[c, h] for h in hs]
            m_st = [mst_ref[c, h][:, 0:1] for h in hs]
            dc = [dc_sc[h] for h in hs]
            dn = [dn_sc[h] for h in hs]
            r = _ml_chunk(q, k, v, gates, gc, rowsf, c_st, n_st, m_st, tril_mask)
            z, wi, wo, ws, w_old, den = r["z"], r["wi"], r["wo"], r["ws"], r["w_old"], r["den"]
            hh = [r["num"][h] / z[h] for h in hs]
            dh = []
            for h in hs:
                gnh = gn_ref[:, cols[h]]
                oh, rstd, y = _head_rms(hh[h], gnh, on_mxu=True)
                sg = _sigmoid(mo_ref[rows, h * HEAD_W:(h + 1) * HEAD_W])
                dogh = dog_ref[rows, cols[h]]
                dy = dogh * sg
                dmo_ref[rows, cols[h]] = (dogh * y * (sg * (1.0 - sg))).astype(ACT)
                dgn_ref[:, cols[h]] += jnp.sum(dy * oh, axis=0, keepdims=True)
                doh = dy * gnh
                dh.append(rstd * (doh - oh * (_lane_sum(doh * oh) * (1.0 / HEAD_W))))
            dnum = [dh[h] / z[h] for h in hs]
            dz = [-_lane_sum(dh[h] * hh[h]) / z[h] for h in hs]
            dden = [jnp.where(jnp.abs(den[h]) > r["floor"][h], dz[h] * jnp.sign(den[h]), 0.0) for h in hs]
            dsw = [(_bdot_nt(dnum[h], v[h]) + dden[h]) * wi[h] for h in hs]
            dq = [_bdot(dsw[h], k[h]) + wo[h] * (_bdot_nt(dnum[h], c_st[h]) + dden[h][:, :ML_DQK] * n_st[h]) for h in hs]
            dk_state = [ws[h] * (_bdot_nt(v[h], dc[h]) + dn[h]) for h in hs]
            dk = [_bdot_tn(dsw[h], q[h]) + dk_state[h] for h in hs]
            dv = [_bdot_tn(r["qk"][h], dnum[h]) + ws[h] * _bdot(k[h], dc[h]) for h in hs]
            woq = [wo[h] * q[h] for h in hs]
            new_dc = [w_old[h] * dc[h] + _bdot_tn(woq[h], dnum[h]) for h in hs]
            for h in hs:
                dv_ref[rows, cols[h]] = dv[h].astype(ACT)
                dc_sc[h] = new_dc[h]
                dn_sc[h] = w_old[h] * dn[h] + jnp.sum(woq[h] * dden[h][:, :ML_DQK], axis=0, keepdims=True)
                d_last = (jnp.sum(jnp.sum(k[h] * dk_state[h], axis=0, keepdims=True), axis=-1, keepdims=True)
                          + w_old[h] * (jnp.sum(jnp.sum(dc[h] * c_st[h], axis=0, keepdims=True), axis=-1, keepdims=True)
                                        + jnp.sum(dn[h] * n_st[h], axis=-1, keepdims=True)))
                kdk = _lane_sum(k[h] * dk[h])
                qdq = _lane_sum(q[h] * dq[h])
                dg_mat = dg_mat + jnp.where(lane == HEADS + h, qdq - kdk, 0.0)
                dlast_row = dlast_row + jnp.where(lane[0:1] == HEADS + h, d_last, 0.0)
                dig_mat = dig_mat + jnp.where(lane == h, kdk, 0.0)
                dqk_ref[rows, h * ML_DQK:(h + 1) * ML_DQK] = dq[h] * (ML_DQK ** -0.5)
                dqk_ref[rows, HEADS * ML_DQK + h * ML_DQK:HEADS * ML_DQK + (h + 1) * ML_DQK] = dk[h]
            dlf = _sel_dot(triu, dg_mat) + dlast_row
            dgt_ref[rows, :] = (dig_mat + dlf * _sigmoid(-gates)).astype(ACT)
            return carry

        lax.fori_loop(0, nc_blk, chunk, 0)

    st4 = lambda a, b: pl.BlockSpec((nc_blk, HEADS, a, b), lambda i: (nb - 1 - i, 0, 0, 0))
    return pl.pallas_call(
        body, name="mlstm_bwd", grid=(nb,),
        in_specs=[_rows_rev(tb, MIX_W, nb), _rows_rev(tb, MIX_W, nb, 1), _rows_rev(tb, MIX_W, nb, 2), _rows_rev(tb, 128, nb, 12),
                  _const((1, MIX_W)), st4(ML_DQK, HEAD_W), st4(1, ML_DQK), st4(1, 128), _rows_rev(tb, MIX_W, nb)],
        out_specs=[_rows_rev(tb, MIX_W, nb), _rows_rev(tb, MIX_W, nb), _rows_rev(tb, MIX_W, nb), _rows_rev(tb, 128, nb), _const((1, MIX_W))],
        out_shape=[jax.ShapeDtypeStruct((t, MIX_W), f32), jax.ShapeDtypeStruct((t, MIX_W), ACT), jax.ShapeDtypeStruct((t, MIX_W), ACT),
                   jax.ShapeDtypeStruct((t, 128), ACT), jax.ShapeDtypeStruct((1, MIX_W), f32)],
        scratch_shapes=[pltpu.VMEM((HEADS, ML_DQK, HEAD_W), f32), pltpu.VMEM((HEADS, 1, ML_DQK), f32)],
        compiler_params=_cparams(1),
    )(qkc, u_ml, u_ml, u_ml, gn, cst, nst, mst, dog)


def _ln_fwd(r, g, b):
    mu = jnp.mean(r, axis=-1, keepdims=True)
    xc = r - mu
    rstd = lax.rsqrt(jnp.mean(xc * xc, axis=-1, keepdims=True) + LN_EPS)
    xh = xc * rstd
    return xh * g + b, xh, rstd


def _ln_bwd(dy, xh, rstd, g):
    dxh = dy * g
    return rstd * (dxh - jnp.mean(dxh, axis=-1, keepdims=True) - xh * jnp.mean(dxh * xh, axis=-1, keepdims=True))


def _outproj_ln1(og_hg, og_ml, x, w_out, g, b):
    t = x.shape[0]
    tm = _tile(t, DENSE_ROWS)

    def body(a_ref, b_ref, x_ref, w_ref, g_ref, bb_ref, x1_ref, xh_ref, rs_ref, x1b_ref):
        mix = _bdot(a_ref[...], w_ref[0:MIX_W, :]) + _bdot(b_ref[...], w_ref[MIX_W:2 * MIX_W, :])
        y, xh, rstd = _ln_fwd(ALPHA * x_ref[...] + mix, g_ref[...], bb_ref[...])
        x1_ref[...] = y
        x1b_ref[...] = y.astype(ACT)
        xh_ref[...] = xh.astype(ACT)
        rs_ref[...] = rstd

    return pl.pallas_call(
        body, name="outproj_ln1", grid=(t // tm,),
        in_specs=[_rows(tm, MIX_W), _rows(tm, MIX_W), _rows(tm, D_MODEL), _resident((D_MODEL, D_MODEL)), _const((1, D_MODEL)), _const((1, D_MODEL))],
        out_specs=[_rows(tm, D_MODEL), _rows(tm, D_MODEL), _rows(tm, 1), _rows(tm, D_MODEL)],
        out_shape=[jax.ShapeDtypeStruct((t, D_MODEL), f32), jax.ShapeDtypeStruct((t, D_MODEL), ACT), jax.ShapeDtypeStruct((t, 1), f32),
                   jax.ShapeDtypeStruct((t, D_MODEL), ACT)],
        compiler_params=_cparams(1, arbitrary=False),
    )(og_hg, og_ml, x, w_out, g, b)


def _ffn_up(x1, wg, wu):
    t = x1.shape[0]
    tm = _tile(t, DENSE_ROWS)

    def body(x_ref, wg_ref, wu_ref, hg_ref, up_ref, a_ref):
        xv = x_ref[...]
        hg = _bdot_nt(xv, wg_ref[...])
        up = _bdot_nt(xv, wu_ref[...])
        hg_ref[...] = hg.astype(ACT)
        up_ref[...] = up.astype(ACT)
        a_ref[...] = (hg * _sigmoid(hg) * up).astype(ACT)

    return pl.pallas_call(
        body, name="ffn_up", grid=(t // tm,),
        in_specs=[_rows(tm, D_MODEL), _resident((D_FF, D_MODEL)), _resident((D_FF, D_MODEL))],
        out_specs=[_rows(tm, D_FF), _rows(tm, D_FF), _rows(tm, D_FF)],
        out_shape=[jax.ShapeDtypeStruct((t, D_FF), ACT), jax.ShapeDtypeStruct((t, D_FF), ACT), jax.ShapeDtypeStruct((t, D_FF), ACT)],
        compiler_params=_cparams(1, arbitrary=False),
    )(x1, wg, wu)


def _ffn_down_ln2(a, x1, wd, g, b):
    t = x1.shape[0]
    tm = _tile(t, DENSE_ROWS)

    def body(a_ref, x_ref, w_ref, g_ref, bb_ref, x2_ref, xh_ref, rs_ref, x2b_ref):
        ffn = _bdot(a_ref[...], w_ref[...])
        y, xh, rstd = _ln_fwd(ALPHA * x_ref[...] + ffn, g_ref[...], bb_ref[...])
        x2_ref[...] = y
        x2b_ref[...] = y.astype(ACT)
        xh_ref[...] = xh.astype(ACT)
        rs_ref[...] = rstd

    return pl.pallas_call(
        body, name="ffn_down_ln2", grid=(t // tm,),
        in_specs=[_rows(tm, D_FF), _rows(tm, D_MODEL), _resident((D_FF, D_MODEL)), _const((1, D_MODEL)), _const((1, D_MODEL))],
        out_specs=[_rows(tm, D_MODEL), _rows(tm, D_MODEL), _rows(tm, 1), _rows(tm, D_MODEL)],
        out_shape=[jax.ShapeDtypeStruct((t, D_MODEL), f32), jax.ShapeDtypeStruct((t, D_MODEL), ACT), jax.ShapeDtypeStruct((t, 1), f32),
                   jax.ShapeDtypeStruct((t, D_MODEL), ACT)],
        compiler_params=_cparams(1, arbitrary=False),
    )(a, x1, wd, g, b)


def _head_loss_bwd(x2, xh2, rs2, p, tgt, w_pg, b_pg, w_pp, g2):
    t = x2.shape[0]
    tm = _tile(t, DENSE_ROWS)

    def body(x_ref, xh_ref, rs_ref, p_ref, t_ref, wg_ref, bg_ref, wp_ref, g_ref,
             dr_ref, de_ref, dz_ref, loss_ref, dbg_ref, dg2_ref, db2_ref):
        @pl.when(pl.program_id(0) == 0)
        def _():
            loss_ref[...] = jnp.zeros_like(loss_ref)
            dbg_ref[...] = jnp.zeros_like(dbg_ref)
            dg2_ref[...] = jnp.zeros_like(dg2_ref)
            db2_ref[...] = jnp.zeros_like(db2_ref)

        x2v = x_ref[...]
        z = _bdot(x2v, wg_ref[...]) + bg_ref[...]
        e = _bdot(p_ref[...], wp_ref[...])
        sg = _sigmoid(z)
        diff = x2v + sg * e - t_ref[...]
        loss_ref[...] += 0.5 * jnp.sum(jnp.mean(diff * diff, axis=-1, keepdims=True), axis=0, keepdims=True)
        dy = diff * (1.0 / D_MODEL)
        de_ref[...] = (dy * sg).astype(ACT)
        dz = dy * e * (sg * (1.0 - sg))
        dz_ref[...] = dz.astype(ACT)
        dbg_ref[...] += jnp.sum(dz, axis=0, keepdims=True)
        dx2 = dy + _bdot_nt(dz, wg_ref[...])
        xh = xh_ref[...].astype(f32)
        dg2_ref[...] += jnp.sum(dx2 * xh, axis=0, keepdims=True)
        db2_ref[...] += jnp.sum(dx2, axis=0, keepdims=True)
        dr_ref[...] = _ln_bwd(dx2, xh, rs_ref[...], g_ref[...])

    row = jax.ShapeDtypeStruct((1, D_MODEL), f32)
    return pl.pallas_call(
        body, name="head_loss_bwd", grid=(t // tm,),
        in_specs=[_rows(tm, D_MODEL), _rows(tm, D_MODEL), _rows(tm, 1), _rows(tm, PLE), _rows(tm, D_MODEL),
                  _resident((D_MODEL, D_MODEL)), _const((1, D_MODEL)), _resident((PLE, D_MODEL)), _const((1, D_MODEL))],
        out_specs=[_rows(tm, D_MODEL), _rows(tm, D_MODEL), _rows(tm, D_MODEL), _const((1, 1)), _const((1, D_MODEL)), _const((1, D_MODEL)), _const((1, D_MODEL))],
        out_shape=[jax.ShapeDtypeStruct((t, D_MODEL), f32), jax.ShapeDtypeStruct((t, D_MODEL), ACT), jax.ShapeDtypeStruct((t, D_MODEL), ACT),
                   jax.ShapeDtypeStruct((1, 1), f32), row, row, row],
        compiler_params=_cparams(1),
    )(x2, xh2, rs2, p, tgt, w_pg, b_pg, w_pp, g2)


def _ffn_bwd(dr2, hg, up, xh1, rs1, wd, wg, wu, g1, w_out):
    t = dr2.shape[0]
    tm = _tile(t, DENSE_ROWS // 2)

    def body(dr_ref, hg_ref, up_ref, xh_ref, rs_ref, wd_ref, wg_ref, wu_ref, g_ref, wo_ref,
             dr1_ref, dhg_ref, dup_ref, dg1_ref, db1_ref, doghg_ref, dogml_ref):
        @pl.when(pl.program_id(0) == 0)
        def _():
            dg1_ref[...] = jnp.zeros_like(dg1_ref)
            db1_ref[...] = jnp.zeros_like(db1_ref)

        dr2v = dr_ref[...]
        da = _bdot_nt(dr2v, wd_ref[...])
        hgv = hg_ref[...].astype(f32)
        sg = _sigmoid(hgv)
        dhg = da * up_ref[...].astype(f32) * (sg * (1.0 + hgv * (1.0 - sg)))
        dup = da * (hgv * sg)
        dhg_ref[...] = dhg.astype(ACT)
        dup_ref[...] = dup.astype(ACT)
        dx1 = ALPHA * dr2v + _bdot(dhg, wg_ref[...]) + _bdot(dup, wu_ref[...])
        xh = xh_ref[...].astype(f32)
        dg1_ref[...] += jnp.sum(dx1 * xh, axis=0, keepdims=True)
        db1_ref[...] += jnp.sum(dx1, axis=0, keepdims=True)
        dr1 = _ln_bwd(dx1, xh, rs_ref[...], g_ref[...])
        dr1_ref[...] = dr1
        dog = _bdot_nt(dr1, wo_ref[...])
        doghg_ref[...] = dog[:, 0:MIX_W]
        dogml_ref[...] = dog[:, MIX_W:2 * MIX_W]

    row = jax.ShapeDtypeStruct((1, D_MODEL), f32)
    return pl.pallas_call(
        body, name="ffn_bwd", grid=(t // tm,),
        in_specs=[_rows(tm, D_MODEL), _rows(tm, D_FF), _rows(tm, D_FF), _rows(tm, D_MODEL), _rows(tm, 1),
                  _resident((D_FF, D_MODEL)), _resident((D_FF, D_MODEL)), _resident((D_FF, D_MODEL)), _const((1, D_MODEL)),
                  _resident((D_MODEL, D_MODEL))],
        out_specs=[_rows(tm, D_MODEL), _rows(tm, D_FF), _rows(tm, D_FF), _const((1, D_MODEL)), _const((1, D_MODEL)),
                   _rows(tm, MIX_W), _rows(tm, MIX_W)],
        out_shape=[jax.ShapeDtypeStruct((t, D_MODEL), f32), jax.ShapeDtypeStruct((t, D_FF), ACT), jax.ShapeDtypeStruct((t, D_FF), ACT), row, row,
                   jax.ShapeDtypeStruct((t, MIX_W), f32), jax.ShapeDtypeStruct((t, MIX_W), f32)],
        compiler_params=_cparams(1),
    )(dr2, hg, up, xh1, rs1, wd, wg, wu, g1, w_out)


def _inproj_bwd(dr1, du_hg, dqk, dmv, dmo, dgt, w_hg, w_ml):
    t = dr1.shape[0]
    tm = _tile(t, DENSE_ROWS)

    def body(dr_ref, dhg_ref, dqk_ref, dmv_ref, dmo_ref, dgt_ref, whg_ref, wml_ref, gx_ref, dml_ref):
        dml = jnp.concatenate([dqk_ref[...], dmv_ref[...], dmo_ref[...], dgt_ref[...]], axis=-1).astype(ACT)
        dml_ref[...] = dml
        gx_ref[...] = ALPHA * dr_ref[...] + _bdot(dhg_ref[...], whg_ref[...]) + _bdot(dml, wml_ref[...])

    return pl.pallas_call(
        body, name="inproj_bwd", grid=(t // tm,),
        in_specs=[_rows(tm, D_MODEL), _rows(tm, U_HG), _rows(tm, MIX_W), _rows(tm, MIX_W), _rows(tm, MIX_W), _rows(tm, 128),
                  _resident((U_HG, D_MODEL)), _resident((U_ML, D_MODEL))],
        out_specs=[_rows(tm, D_MODEL), _rows(tm, U_ML)],
        out_shape=[jax.ShapeDtypeStruct((t, D_MODEL), f32), jax.ShapeDtypeStruct((t, U_ML), ACT)],
        compiler_params=_cparams(1, arbitrary=False),
    )(dr1, du_hg, dqk, dmv, dmo, dgt, w_hg, w_ml)


def _wgrad(a, b, name, tk=None, tn=None, colsum=False):
    t, kdim = a.shape
    n = b.shape[1]
    tk = tk or kdim
    tn = tn or n
    tt = _tile(t, WGRAD_ROWS)
    assert not colsum or tn == n

    def body(a_ref, b_ref, o_ref, *s_ref):
        @pl.when(pl.program_id(2) == 0)
        def _():
            o_ref[...] = jnp.zeros_like(o_ref)
            if colsum:
                s_ref[0][...] = jnp.zeros_like(s_ref[0])

        av = a_ref[...]
        o_ref[...] += _bdot_tn(av, b_ref[...])
        if colsum:
            s_ref[0][...] += jnp.sum(av.astype(f32), axis=0, keepdims=True)

    out_specs = [pl.BlockSpec((tk, tn), lambda i, j, s: (i, j))]
    out_shape = [jax.ShapeDtypeStruct((kdim, n), f32)]
    if colsum:
        out_specs.append(pl.BlockSpec((1, tk), lambda i, j, s: (0, i)))
        out_shape.append(jax.ShapeDtypeStruct((1, kdim), f32))
    res = pl.pallas_call(
        body, name=name, grid=(kdim // tk, n // tn, t // tt),
        in_specs=[pl.BlockSpec((tt, tk), lambda i, j, s: (s, i)), pl.BlockSpec((tt, tn), lambda i, j, s: (s, j))],
        out_specs=out_specs, out_shape=out_shape,
        compiler_params=_cparams(3),
    )(a, b)
    return res if colsum else res[0]


def _colsum(parts, name):
    t = parts[0].shape[0]
    tt = _tile(t, 512)
    widths = [a.shape[1] for a in parts]

    def body(*refs):
        o_ref = refs[-1]

        @pl.when(pl.program_id(0) == 0)
        def _():
            o_ref[...] = jnp.zeros_like(o_ref)

        off = 0
        for r, w in zip(refs[:-1], widths):
            o_ref[:, off:off + w] += jnp.sum(r[...].astype(f32), axis=0, keepdims=True)
            off += w

    return pl.pallas_call(
        body, name=name, grid=(t // tt,),
        in_specs=[_rows(tt, w) for w in widths],
        out_specs=_const((1, sum(widths))),
        out_shape=jax.ShapeDtypeStruct((1, sum(widths)), f32),
        compiler_params=_cparams(1),
    )(*parts)


_TRANSPOSED = {"w_in", "w_ffn_gate", "w_ffn_up"}
_COL_SPLIT = {"ple_w_proj"}
_RIDE_PLAN = (("w_out", "ple_w_gate", "ple_w_proj"), ("w_ffn_down",), ("w_ffn_gate", "w_ffn_up"))


def _from_chip_major(a, col_split):
    if col_split:
        return a.transpose(1, 0, 2).reshape(a.shape[1], 4 * a.shape[2])
    return a.reshape(4 * a.shape[1], a.shape[2])


def _local_step(x, p, tgt, w_in_b, b_in, logits, conv_w, conv_b, hg_gn, ml_gn, w_out_b, ln1_g, ln1_b,
                wg_b, wu_b, wd_b, ln2_g, ln2_b, w_pp_b, w_pg_b, b_pg, early_hook=None, late_shards=None):
    pad_w = U_HG + U_ML - PROJ_W
    w_hg = w_in_b[:U_HG]
    w_ml = jnp.pad(w_in_b[U_HG:], ((0, pad_w), (0, 0)))
    bb_hg = b_in[:, :U_HG]
    bb_ml = jnp.pad(b_in[:, U_HG:], ((0, 0), (0, pad_w)))

    ride = [[late_shards[k] for k in names] for names in _RIDE_PLAN] if late_shards is not None else [(), (), ()]
    (u_hg, u_ml, xb), got0 = _inproj(x, w_hg, w_ml, bb_hg, bb_ml, ride[0])
    (og_hg, sst), got1 = _hgrn2_fwd(u_hg, logits, hg_gn, ride[1])
    pre, qkc = _conv_fwd(u_ml, conv_w, conv_b)
    (og_ml, cst, nst, mst), got2 = _mlstm_fwd(qkc, u_ml, ml_gn, ride[2])
    if late_shards is not None:
        late = {k: _from_chip_major(g, k in _COL_SPLIT) for names, got in zip(_RIDE_PLAN, (got0, got1, got2)) for k, g in zip(names, got)}
        w_out_b, wg_b, wu_b, wd_b = late["w_out"], late["w_ffn_gate"], late["w_ffn_up"], late["w_ffn_down"]
        w_pp_b, w_pg_b = late["ple_w_proj"], late["ple_w_gate"]
    x1, xh1, rs1, x1b = _outproj_ln1(og_hg, og_ml, x, w_out_b, ln1_g, ln1_b)
    hgp, up, act = _ffn_up(x1b, wg_b, wu_b)
    x2, xh2, rs2, x2b = _ffn_down_ln2(act, x1, wd_b, ln2_g, ln2_b)
    dr2, de, dz, loss, d_bpg, d_ln2g, d_ln2b = _head_loss_bwd(x2, xh2, rs2, p, tgt, w_pg_b, b_pg, w_pp_b, ln2_g)
    dr1, dhg, dup, d_ln1g, d_ln1b, dog_hg, dog_ml = _ffn_bwd(dr2, hgp, up, xh1, rs1, wd_b, wg_b, wu_b, ln1_g, w_out_b)

    d_wo_a = _wgrad(og_hg, dr1, "wgrad_out_hg")
    d_wo_b = _wgrad(og_ml, dr1, "wgrad_out_ml")
    d_w_out = jnp.concatenate([d_wo_a, d_wo_b], axis=0)
    d_wg = _wgrad(dhg, x1b, "wgrad_ffn_gate", tk=D_FF // 2)
    d_wu = _wgrad(dup, x1b, "wgrad_ffn_up", tk=D_FF // 2)
    d_wd = _wgrad(act, dr2, "wgrad_ffn_down", tk=D_FF // 2)
    d_wpp = _wgrad(p, de, "wgrad_ple_proj")
    d_wpg = _wgrad(x2b, dz, "wgrad_ple_gate")
    early = dict(w_out=d_w_out, w_ffn_gate=d_wg, w_ffn_up=d_wu, w_ffn_down=d_wd, ple_w_proj=d_wpp, ple_w_gate=d_wpg)
    riders = early_hook(early) if early_hook is not None else ()

    res = _hgrn2_bwd(u_hg, logits, hg_gn, sst, dog_hg, riders)
    du_hg, d_logits, d_hg_gn = res[:3]
    dqkc, dmv, dmo, dgt, d_ml_gn = _mlstm_bwd(qkc, u_ml, ml_gn, cst, nst, mst, dog_ml)
    dqk, d_conv_w, d_conv_b = _conv_bwd(u_ml, conv_w, pre, dqkc)
    grad_x, du_ml = _inproj_bwd(dr1, du_hg, dqk, dmv, dmo, dgt, w_hg, w_ml)

    dw_hg, db_hg = _wgrad(du_hg, xb, "wgrad_in_hg", tk=U_HG // 2, colsum=True)
    dw_ml, db_ml = _wgrad(du_ml, xb, "wgrad_in_ml", colsum=True)
    d_w_in = jnp.concatenate([dw_hg, dw_ml[:PROJ_W - U_HG]], axis=0)
    d_b_in = jnp.concatenate([db_hg, db_ml[:, :PROJ_W - U_HG]], axis=1)

    grads = dict(w_in=d_w_in, b_in=d_b_in, hg_lb_logits=d_logits, ml_conv_w=d_conv_w, ml_conv_b=d_conv_b,
                 hg_norm_g=d_hg_gn, ml_norm_g=d_ml_gn, ln1_g=d_ln1g, ln1_b=d_ln1b, ln2_g=d_ln2g, ln2_b=d_ln2b,
                 ple_b_gate=d_bpg, **early)
    return loss, grad_x, grads, list(res[3:])


_ANY = pl.BlockSpec(memory_space=pltpu.HBM)
_MESH = pl.DeviceIdType.MESH


def _my_place():
    return lax.axis_index("x"), lax.axis_index("y"), lax.axis_index("c")


def _other_chips(x, y):
    return [(1 - x, y), (x, 1 - y), (1 - x, 1 - y)]


def _allgather_weights(shards, taps, name):
    n = len(shards)
    halves = [s.shape[0] // 2 for s in shards]

    def body(*refs):
        ins, tap_in = refs[:n], refs[n]
        outs, tap_out = refs[n + 1:2 * n + 1], refs[2 * n + 1]
        send_sems, recv_sems, local_sems = refs[2 * n + 2:]
        x, y, c = _my_place()
        me = 2 * x + y
        sibling = (x, y, 1 - c)
        chips = _other_chips(x, y)

        def ici(a, j, block_chip):
            px, py = chips[j]
            src = ins[a].at[pl.ds(pl.multiple_of(c * halves[a], 16), halves[a])] if block_chip is None else outs[a].at[block_chip, c]
            dst = outs[a].at[me if block_chip is None else block_chip, c]
            return pltpu.make_async_remote_copy(src_ref=src, dst_ref=dst, send_sem=send_sems.at[6 * a + j], recv_sem=recv_sems.at[6 * a + j],
                                                device_id=(px, py, c), device_id_type=_MESH)

        def d2d(a, j, half):
            px, py = chips[j]
            blk = outs[a].at[2 * px + py, half]
            return pltpu.make_async_remote_copy(src_ref=blk, dst_ref=blk, send_sem=send_sems.at[6 * a + 3 + j], recv_sem=recv_sems.at[6 * a + 3 + j],
                                                device_id=sibling, device_id_type=_MESH)

        local = []
        for a in range(n):
            for h in range(2):
                cp = pltpu.make_async_copy(ins[a].at[pl.ds(h * halves[a], halves[a])], outs[a].at[me, h], local_sems.at[2 * a + h])
                cp.start()
                local.append(cp)
            for j in range(3):
                ici(a, j, None).start()
        tap_local = pltpu.make_async_copy(tap_in, tap_out.at[me], local_sems.at[2 * n])
        tap_local.start()
        tap_copies = []
        for j, (px, py) in enumerate(chips):
            cp = pltpu.make_async_remote_copy(src_ref=tap_in, dst_ref=tap_out.at[me], send_sem=send_sems.at[6 * n + j], recv_sem=recv_sems.at[6 * n + j],
                                              device_id=(px, py, c), device_id_type=_MESH)
            cp.start()
            tap_copies.append(cp)
        for a in range(n):
            for j, (px, py) in enumerate(chips):
                ici(a, j, 2 * px + py).wait_recv()
                d2d(a, j, c).start()
        for a in range(n):
            for j in range(3):
                d2d(a, j, 1 - c).wait_recv()
        for a in range(n):
            for j in range(3):
                ici(a, j, None).wait_send()
                d2d(a, j, c).wait_send()
        for j, (px, py) in enumerate(chips):
            pltpu.make_async_remote_copy(src_ref=tap_in, dst_ref=tap_out.at[2 * px + py], send_sem=send_sems.at[6 * n + j], recv_sem=recv_sems.at[6 * n + j],
                                         device_id=(px, py, c), device_id_type=_MESH).wait()
        for cp in local:
            cp.wait()
        tap_local.wait()

    res = pl.pallas_call(
        body, name=name,
        in_specs=[_ANY] * (n + 1), out_specs=[_ANY] * (n + 1),
        out_shape=[jax.ShapeDtypeStruct((4, 2, s.shape[0] // 2, s.shape[1]), s.dtype) for s in shards]
        + [jax.ShapeDtypeStruct((4,) + taps.shape, taps.dtype)],
        scratch_shapes=[pltpu.SemaphoreType.DMA((6 * n + 3,)), pltpu.SemaphoreType.DMA((6 * n + 3,)), pltpu.SemaphoreType.DMA((2 * n + 1,))],
    )(*shards, taps)
    return [w.reshape((4,) + s.shape) for w, s in zip(res[:n], shards)], res[n]


def _swap_halves(pieces, name):
    n = len(pieces)
    halves = [p.shape[1] // 2 for p in pieces]

    def body(*refs):
        ins, own, other = refs[:n], refs[n:2 * n], refs[2 * n:3 * n]
        send_sems, recv_sems, local_sems = refs[3 * n:]
        x, y, c = _my_place()

        def half_of(a, which):
            return ins[a].at[pl.ds(0, 4), pl.ds(pl.multiple_of(which * halves[a], 16), halves[a])]

        def to_sibling(a):
            return pltpu.make_async_remote_copy(src_ref=half_of(a, 1 - c), dst_ref=other[a], send_sem=send_sems.at[a], recv_sem=recv_sems.at[a],
                                                device_id=(x, y, 1 - c), device_id_type=_MESH)

        local = []
        for a in range(n):
            cp = pltpu.make_async_copy(half_of(a, c), own[a], local_sems.at[a])
            cp.start()
            local.append(cp)
            to_sibling(a).start()
        for a in range(n):
            to_sibling(a).wait()
            local[a].wait()

    shapes = [jax.ShapeDtypeStruct((4, p.shape[1] // 2, p.shape[2]), p.dtype) for p in pieces]
    res = pl.pallas_call(
        body, name=name,
        in_specs=[_ANY] * n, out_specs=[_ANY] * (2 * n), out_shape=shapes + shapes,
        scratch_shapes=[pltpu.SemaphoreType.DMA((n,)), pltpu.SemaphoreType.DMA((n,)), pltpu.SemaphoreType.DMA((n,))],
    )(*pieces)
    return res[:n], res[n:]


_VMEM = pl.BlockSpec(memory_space=pltpu.VMEM)
_EX_ROWS = 32


def _pair_reduce(p, name):
    s, r, c = p.shape
    half = r // 2

    def body(p_ref, o_ref, other, send_sem, recv_sem):
        x, y, cc = _my_place()
        theirs = pl.multiple_of((1 - cc) * half, 16)
        mine = pl.multiple_of(cc * half, 16)
        cp = pltpu.make_async_remote_copy(src_ref=p_ref.at[pl.ds(0, s), pl.ds(theirs, half)], dst_ref=other, send_sem=send_sem, recv_sem=recv_sem,
                                          device_id=(x, y, 1 - cc), device_id_type=_MESH)
        cp.start()
        cp.wait()

        def step(i, carry):
            r0 = pl.multiple_of(i * _EX_ROWS, _EX_ROWS)
            for slot in range(s):
                own_rows = pl.ds(pl.multiple_of(mine + r0, 16), _EX_ROWS)
                o_ref[slot, pl.ds(r0, _EX_ROWS), :] = (p_ref[slot, own_rows, :] + other[slot, pl.ds(r0, _EX_ROWS), :]).astype(bf16)
            return carry

        lax.fori_loop(0, half // _EX_ROWS, step, 0)

    return pl.pallas_call(
        body, name=name, in_specs=[_VMEM], out_specs=_VMEM,
        out_shape=jax.ShapeDtypeStruct((s, half, c), bf16),
        scratch_shapes=[pltpu.VMEM((s, half, c), f32), pltpu.SemaphoreType.DMA, pltpu.SemaphoreType.DMA],
        compiler_params=pltpu.CompilerParams(vmem_limit_bytes=VMEM_LIMIT),
    )(p)


def _chip_reduce_swap(rcv, name):
    s, h, c = rcv.shape

    def body(r_ref, g_ref, send_sem, recv_sem):
        x, y, cc = _my_place()

        def step(i, carry):
            r0 = pl.multiple_of(i * _EX_ROWS, _EX_ROWS)
            acc = r_ref[0, pl.ds(r0, _EX_ROWS), :].astype(f32)
            for slot in range(1, s):
                acc = acc + r_ref[slot, pl.ds(r0, _EX_ROWS), :].astype(f32)
            g_ref[cc, pl.ds(r0, _EX_ROWS), :] = acc
            return carry

        lax.fori_loop(0, h // _EX_ROWS, step, 0)
        cp = pltpu.make_async_remote_copy(src_ref=g_ref.at[cc], dst_ref=g_ref.at[cc], send_sem=send_sem, recv_sem=recv_sem,
                                          device_id=(x, y, 1 - cc), device_id_type=_MESH)
        cp.start()
        cp.wait()

    return pl.pallas_call(
        body, name=name, in_specs=[_VMEM], out_specs=_VMEM,
        out_shape=jax.ShapeDtypeStruct((2, h, c), f32),
        scratch_shapes=[pltpu.SemaphoreType.DMA, pltpu.SemaphoreType.DMA],
        compiler_params=pltpu.CompilerParams(vmem_limit_bytes=VMEM_LIMIT),
    )(rcv)


def _pair_reduce_cols(p, name):
    s, r, c = p.shape
    hc = c // 2

    def body(p_ref, o_ref, other, send_sem, recv_sem):
        x, y, cc = _my_place()

        def run(mine_lo, theirs_lo):
            cp = pltpu.make_async_remote_copy(src_ref=p_ref.at[pl.ds(0, s), pl.ds(0, r), pl.ds(theirs_lo, hc)], dst_ref=other,
                                              send_sem=send_sem, recv_sem=recv_sem, device_id=(x, y, 1 - cc), device_id_type=_MESH)
            cp.start()
            cp.wait()
            for slot in range(s):
                o_ref[slot] = (p_ref[slot, :, mine_lo:mine_lo + hc] + other[slot]).astype(bf16)

        @pl.when(cc == 0)
        def _():
            run(0, hc)

        @pl.when(cc == 1)
        def _():
            run(hc, 0)

    return pl.pallas_call(
        body, name=name, in_specs=[_VMEM], out_specs=_VMEM,
        out_shape=jax.ShapeDtypeStruct((s, r, hc), bf16),
        scratch_shapes=[pltpu.VMEM((s, r, hc), f32), pltpu.SemaphoreType.DMA, pltpu.SemaphoreType.DMA],
        compiler_params=pltpu.CompilerParams(vmem_limit_bytes=VMEM_LIMIT),
    )(p)


def _chip_reduce_swap_cols(rcv, name):
    s, r, hc = rcv.shape

    def body(r_ref, g_ref, send_sem, recv_sem):
        x, y, cc = _my_place()
        acc = r_ref[0].astype(f32)
        for slot in range(1, s):
            acc = acc + r_ref[slot].astype(f32)
        g_ref[cc] = acc
        cp = pltpu.make_async_remote_copy(src_ref=g_ref.at[cc], dst_ref=g_ref.at[cc], send_sem=send_sem, recv_sem=recv_sem,
                                          device_id=(x, y, 1 - cc), device_id_type=_MESH)
        cp.start()
        cp.wait()

    both = pl.pallas_call(
        body, name=name, in_specs=[_VMEM], out_specs=_VMEM,
        out_shape=jax.ShapeDtypeStruct((2, r, hc), f32),
        scratch_shapes=[pltpu.SemaphoreType.DMA, pltpu.SemaphoreType.DMA],
        compiler_params=pltpu.CompilerParams(vmem_limit_bytes=VMEM_LIMIT),
    )(rcv)
    return both.transpose(1, 0, 2).reshape(r, 2 * hc)


def _add_cast(a, b, name):
    s, r, c = a.shape
    tr = _row_tile(r, c)

    def body(a_ref, b_ref, o_ref):
        o_ref[...] = (a_ref[...] + b_ref[...]).astype(bf16)

    blk = pl.BlockSpec((1, tr, c), lambda i, j: (i, j, 0))
    return pl.pallas_call(
        body, name=name, grid=(s, r // tr), in_specs=[blk, blk], out_specs=blk,
        out_shape=jax.ShapeDtypeStruct(a.shape, bf16),
        compiler_params=_cparams(2, arbitrary=False),
    )(a, b)


def _gather_copies(ins, outs, send_sems, recv_sems, local_sems):
    x, y, c = _my_place()
    me = 2 * x + y
    local, outgoing, incoming = [], [], []
    for a in range(len(ins)):
        local.append(pltpu.make_async_copy(ins[a], outs[a].at[me], local_sems.at[a]))
        for j, (px, py) in enumerate(_other_chips(x, y)):
            sems = dict(send_sem=send_sems.at[3 * a + j], recv_sem=recv_sems.at[3 * a + j], device_id=(px, py, c), device_id_type=_MESH)
            outgoing.append(pltpu.make_async_remote_copy(src_ref=ins[a], dst_ref=outs[a].at[me], **sems))
            incoming.append(pltpu.make_async_remote_copy(src_ref=ins[a], dst_ref=outs[a].at[2 * px + py], **sems))
    return local, outgoing, incoming


def _gather_chips(blocks, name):
    n = len(blocks)

    def body(*refs):
        local, outgoing, incoming = _gather_copies(refs[:n], refs[n:2 * n], *refs[2 * n:])
        for cp in local + outgoing:
            cp.start()
        for cp in incoming:
            cp.wait_recv()
        for cp in outgoing:
            cp.wait_send()
        for cp in local:
            cp.wait()

    return pl.pallas_call(
        body, name=name, in_specs=[_ANY] * n, out_specs=[_ANY] * n, out_shape=_gather_shapes(blocks),
        scratch_shapes=[pltpu.SemaphoreType.DMA((3 * n,)), pltpu.SemaphoreType.DMA((3 * n,)), pltpu.SemaphoreType.DMA((n,))],
    )(*blocks)


def _riding_call(body, name, nsteps, in_specs, out_specs, out_shape, scratch_shapes, operands, riders, copies, ride_shapes):
    nr, n_in, n_out, n_scr = len(riders), len(in_specs), len(out_specs), len(scratch_shapes)

    def wrapped(*refs):
        ins, ride_in = refs[:n_in], refs[n_in:n_in + nr]
        outs, ride_out = refs[n_in + nr:n_in + nr + n_out], refs[n_in + nr + n_out:n_in + 2 * nr + n_out]
        scratch, sems = refs[n_in + 2 * nr + n_out:n_in + 2 * nr + n_out + n_scr], refs[n_in + 2 * nr + n_out + n_scr:]
        if nr:
            @pl.when(pl.program_id(0) == 0)
            def _():
                local, outgoing, _ = copies(ride_in, ride_out, *sems)
                for cp in local + outgoing:
                    cp.start()

        body(*ins, *outs, *scratch)
        if nr:
            @pl.when(pl.program_id(0) == nsteps - 1)
            def _():
                local, outgoing, incoming = copies(ride_in, ride_out, *sems)
                for cp in incoming:
                    cp.wait_recv()
                for cp in outgoing:
                    cp.wait_send()
                for cp in local:
                    cp.wait()

    hbm = pl.BlockSpec(memory_space=pltpu.HBM)
    sems = [pltpu.SemaphoreType.DMA((3 * nr,)), pltpu.SemaphoreType.DMA((3 * nr,)), pltpu.SemaphoreType.DMA((nr,))] if nr else []
    res = pl.pallas_call(
        wrapped, name=name, grid=(nsteps,),
        in_specs=list(in_specs) + [hbm] * nr, out_specs=list(out_specs) + [hbm] * nr,
        out_shape=list(out_shape) + list(ride_shapes),
        scratch_shapes=list(scratch_shapes) + sems,
        compiler_params=_cparams(1),
    )(*operands, *riders)
    return list(res[:n_out]), list(res[n_out:])


def _gather_shapes(riders):
    return [jax.ShapeDtypeStruct((4,) + r.shape, r.dtype) for r in riders]


def _scatter_copies(ins, outs, send_sems, recv_sems, local_sems):
    x, y, c = _my_place()
    me = 2 * x + y
    local, outgoing, incoming = [], [], []
    for a in range(len(ins)):
        local.append(pltpu.make_async_copy(ins[a].at[me], outs[a].at[me], local_sems.at[a]))
        for j, (px, py) in enumerate(_other_chips(x, y)):
            sems = dict(send_sem=send_sems.at[3 * a + j], recv_sem=recv_sems.at[3 * a + j], device_id=(px, py, c), device_id_type=_MESH)
            outgoing.append(pltpu.make_async_remote_copy(src_ref=ins[a].at[2 * px + py], dst_ref=outs[a].at[me], **sems))
            incoming.append(pltpu.make_async_remote_copy(src_ref=ins[a].at[2 * px + py], dst_ref=outs[a].at[2 * px + py], **sems))
    return local, outgoing, incoming


def _scatter_start(ins, outs, send_sems, recv_sems, local_sems):
    local, outgoing, _ = _scatter_copies(ins, outs, send_sems, recv_sems, local_sems)
    for cp in local + outgoing:
        cp.start()


def _scatter_wait(ins, outs, send_sems, recv_sems, local_sems):
    local, outgoing, incoming = _scatter_copies(ins, outs, send_sems, recv_sems, local_sems)
    for cp in incoming:
        cp.wait_recv()
    for cp in outgoing:
        cp.wait_send()
    for cp in local:
        cp.wait()


def _scatter_chips(pieces, name):
    n = len(pieces)

    def body(*refs):
        ins, outs = refs[:n], refs[n:2 * n]
        _scatter_start(ins, outs, *refs[2 * n:])
        _scatter_wait(ins, outs, *refs[2 * n:])

    return pl.pallas_call(
        body, name=name,
        in_specs=[_ANY] * n, out_specs=[_ANY] * n,
        out_shape=[jax.ShapeDtypeStruct(s.shape, s.dtype) for s in pieces],
        scratch_shapes=[pltpu.SemaphoreType.DMA((3 * n,)), pltpu.SemaphoreType.DMA((3 * n,)), pltpu.SemaphoreType.DMA((n,))],
    )(*pieces)


def _swap_cores(blocks, name):
    n = len(blocks)
    parts = 4
    rows = [b.shape[0] // parts for b in blocks]

    def body(*refs):
        ins, outs = refs[:n], refs[n:2 * n]
        send_sems, recv_sems, local_sems = refs[2 * n:]
        x, y, c = _my_place()

        def remote(a, k, slot):
            rs = pl.ds(k * rows[a], rows[a])
            return pltpu.make_async_remote_copy(src_ref=ins[a].at[rs], dst_ref=outs[a].at[slot, rs], send_sem=send_sems.at[parts * a + k],
                                                recv_sem=recv_sems.at[parts * a + k], device_id=(x, y, 1 - c), device_id_type=_MESH)

        local = []
        for a in range(n):
            cp = pltpu.make_async_copy(ins[a], outs[a].at[c], local_sems.at[a])
            cp.start()
            local.append(cp)
            for k in range(parts):
                remote(a, k, c).start()
        for a in range(n):
            for k in range(parts):
                remote(a, k, 1 - c).wait()
            local[a].wait()

    return pl.pallas_call(
        body, name=name,
        in_specs=[_ANY] * n, out_specs=[_ANY] * n,
        out_shape=[jax.ShapeDtypeStruct((2,) + s.shape, s.dtype) for s in blocks],
        scratch_shapes=[pltpu.SemaphoreType.DMA((parts * n,)), pltpu.SemaphoreType.DMA((parts * n,)), pltpu.SemaphoreType.DMA((n,))],
    )(*blocks)


def _gather_all(block, name):
    def body(in_ref, out_ref, send_sems, recv_sems, local_sem):
        x, y, c = _my_place()
        me = 4 * x + 2 * y + c
        cp = pltpu.make_async_copy(in_ref, out_ref.at[me], local_sem)
        cp.start()
        peers = []
        for dx in range(2):
            for dy in range(2):
                for dc in range(2):
                    if dx or dy or dc:
                        peers.append((1 - x if dx else x, 1 - y if dy else y, 1 - c if dc else c))
        for j, pr in enumerate(peers):
            pltpu.make_async_remote_copy(src_ref=in_ref, dst_ref=out_ref.at[me], send_sem=send_sems.at[j], recv_sem=recv_sems.at[j],
                                         device_id=pr, device_id_type=_MESH).start()
        for j, (px, py, pc) in enumerate(peers):
            pltpu.make_async_remote_copy(src_ref=in_ref, dst_ref=out_ref.at[4 * px + 2 * py + pc], send_sem=send_sems.at[j], recv_sem=recv_sems.at[j],
                                         device_id=(px, py, pc), device_id_type=_MESH).wait()
        cp.wait()

    return pl.pallas_call(
        body, name=name,
        in_specs=[_ANY], out_specs=_ANY,
        out_shape=jax.ShapeDtypeStruct((8,) + block.shape, block.dtype),
        scratch_shapes=[pltpu.SemaphoreType.DMA((7,)), pltpu.SemaphoreType.DMA((7,)), pltpu.SemaphoreType.DMA],
    )(block)


def _row_tile(r, c):
    best = r
    for cand in range(16, r + 1, 16):
        if r % cand == 0 and cand * c * 4 <= (1 << 20):
            best = cand
    return best if best * c * 4 <= (4 << 20) else r


def _sum_slots(parts, name):
    n, r, c = parts.shape
    tr = _row_tile(r, c)

    def body(p_ref, o_ref):
        acc = p_ref[0].astype(f32)
        for s in range(1, n):
            acc = acc + p_ref[s].astype(f32)
        o_ref[...] = acc

    return pl.pallas_call(
        body, name=name, grid=(r // tr,),
        in_specs=[pl.BlockSpec((n, tr, c), lambda i: (0, i, 0))],
        out_specs=pl.BlockSpec((tr, c), lambda i: (i, 0)),
        out_shape=jax.ShapeDtypeStruct((r, c), f32),
        compiler_params=_cparams(1, arbitrary=False),
    )(parts)


def _adamw(parts, w, m, v, name):
    n, r, c = parts.shape
    tr = _row_tile(r, c)
    tc = c
    if tr == r and r * c * 4 > (1 << 20) and c % 256 == 0:
        tc = 256

    def body(p_ref, w_ref, m_ref, v_ref, g_ref, d_ref, nm_ref, nv_ref):
        g = p_ref[0]
        for s in range(1, n):
            g = g + p_ref[s]
        nm = B1 * m_ref[...] + (1.0 - B1) * g
        nv = B2 * v_ref[...] + (1.0 - B2) * (g * g)
        m_hat = nm / (1.0 - B1 ** STEP)
        v_hat = nv / (1.0 - B2 ** STEP)
        g_ref[...] = g
        nm_ref[...] = nm
        nv_ref[...] = nv
        d_ref[...] = -LR * (m_hat / (jnp.sqrt(v_hat) + EPS_ADAM) + WD * w_ref[...])

    blk = pl.BlockSpec((tr, tc), lambda i, j: (i, j))
    return pl.pallas_call(
        body, name=name, grid=(r // tr, c // tc),
        in_specs=[pl.BlockSpec((n, tr, tc), lambda i, j: (0, i, j)), blk, blk, blk],
        out_specs=[blk] * 4,
        out_shape=[jax.ShapeDtypeStruct((r, c), f32)] * 4,
        compiler_params=_cparams(2, arbitrary=False),
    )(parts, w, m, v)


_BIG = ["w_in", "w_out", "w_ffn_gate", "w_ffn_up", "w_ffn_down", "ple_w_proj", "ple_w_gate"]
_SMALL = ["b_in", "hg_lb_logits", "ml_conv_w", "ml_conv_b", "hg_norm_g", "ml_norm_g", "ln1_g", "ln1_b", "ln2_g", "ln2_b", "ple_b_gate"]
_ORDER = ["w_in", "b_in", "hg_lb_logits", "ml_conv_w", "ml_conv_b", "hg_norm_g", "ml_norm_g", "w_out", "ln1_g", "ln1_b",
          "w_ffn_gate", "w_ffn_up", "w_ffn_down", "ln2_g", "ln2_b", "ple_w_proj", "ple_w_gate", "ple_b_gate"]
_PACK_ROWS, _PACK_COLS = 16, 1024


def _pack(arrays):
    flat = jnp.concatenate([a.reshape(-1) for a in arrays])
    return jnp.pad(flat, (0, _PACK_ROWS * _PACK_COLS - flat.shape[0])).reshape(_PACK_ROWS, _PACK_COLS)


def _unpack(pack, shapes):
    flat = pack.reshape(-1)
    out, off = [], 0
    for s in shapes:
        size = 1
        for d in s:
            size *= d
        out.append(flat[off:off + size].reshape(s))
        off += size
    return out


def _to_chip_major(g, col_split):
    if col_split:
        k, n = g.shape
        return g.reshape(k, 4, n // 4).transpose(1, 0, 2)
    k, n = g.shape
    return g.reshape(4, k // 4, n)


def kernel(x, p, w_in, b_in, hg_lb_logits, ml_conv_w, ml_conv_b, hg_norm_g, ml_norm_g, w_out, ln1_g, ln1_b, w_ffn_gate, w_ffn_up, w_ffn_down, ln2_g, ln2_b, ple_w_proj, ple_w_gate, ple_b_gate, loss_target, m_w_in, m_b_in, m_hg_lb_logits, m_ml_conv_w, m_ml_conv_b, m_hg_norm_g, m_ml_norm_g, m_w_out, m_ln1_g, m_ln1_b, m_w_ffn_gate, m_w_ffn_up, m_w_ffn_down, m_ln2_g, m_ln2_b, m_ple_w_proj, m_ple_w_gate, m_ple_b_gate, v_w_in, v_b_in, v_hg_lb_logits, v_ml_conv_w, v_ml_conv_b, v_hg_norm_g, v_ml_norm_g, v_w_out, v_ln1_g, v_ln1_b, v_w_ffn_gate, v_w_ffn_up, v_w_ffn_down, v_ln2_g, v_ln2_b, v_ple_w_proj, v_ple_w_gate, v_ple_b_gate):
    args = dict(locals())
    wts = {k: args[k] for k in _ORDER}
    mom = {k: args["m_" + k] for k in _ORDER}
    var = {k: args["v_" + k] for k in _ORDER}
    two_d = lambda a: a.reshape(a.shape[-2], a.shape[-1])
    block = lambda k, a: jnp.swapaxes(two_d(a), 0, 1) if k in _TRANSPOSED else two_d(a)
    unblock = lambda k, a: (jnp.swapaxes(a, 0, 1) if k in _TRANSPOSED else a).reshape(wts[k].shape)

    shards = {k: block(k, wts[k]).astype(bf16) for k in _BIG}
    w_in_blocks, taps = _gather_chips([shards["w_in"], two_d(ml_conv_w)], "gather_w_in")
    w_in_full = _from_chip_major(w_in_blocks, False)
    conv_w_full = _from_chip_major(taps, True)

    def core_sum(k, g):
        pieces = _to_chip_major(g, k in _COL_SPLIT)
        if pieces.shape[1] % (2 * _EX_ROWS):
            return _pair_reduce_cols(pieces, "pair_reduce_" + k)
        return _pair_reduce(pieces, "pair_reduce_" + k)

    early_keys = _BIG[1:]
    loss, grad_x, grads, early_received = _local_step(
        x[0], p[0, 0], loss_target[0], w_in_full, b_in, hg_lb_logits, conv_w_full, ml_conv_b, hg_norm_g, ml_norm_g,
        None, ln1_g, ln1_b, None, None, None, ln2_g, ln2_b, None, None, ple_b_gate,
        early_hook=lambda early: [core_sum(k, early[k]) for k in early_keys],
        late_shards={k: shards[k] for k in early_keys})

    received = list(_scatter_chips([core_sum("w_in", grads["w_in"])], "scatter_grad_w_in")) + list(early_received)
    out_g, out_d, out_m, out_v = {}, {}, {}, {}
    for k, rcv in zip(_BIG, received):
        own = block(k, wts[k])
        if rcv.shape[1] == own.shape[0]:
            whole = _chip_reduce_swap_cols(rcv, "chip_reduce_" + k)
        else:
            parts = _chip_reduce_swap(rcv, "chip_reduce_" + k)
            whole = parts.reshape(2 * parts.shape[1], parts.shape[2])
        g, d, nm, nv = _adamw(whole[None], own, block(k, mom[k]), block(k, var[k]), "adamw_" + k)
        out_g[k], out_d[k], out_m[k], out_v[k] = unblock(k, g), unblock(k, d), unblock(k, nm), unblock(k, nv)

    small_shapes = [(1, PROJ_W), (2, MIX_W), (CONV_K, MIX_W)] + [(1, MIX_W)] * 3 + [(1, D_MODEL)] * 5 + [(1, 1)]
    contrib = _pack([grads[k] for k in _SMALL] + [loss])
    summed = _sum_slots(_gather_all(contrib, "gather_small"), "sum_small")
    small = _unpack(summed, small_shapes)
    loss_total = small[-1].reshape(())
    gsm = dict(zip(_SMALL, small[:-1]))
    place = 2 * lax.axis_index("x") + lax.axis_index("y")
    conv_cols = ml_conv_w.shape[-1]
    gsm["ml_conv_w"] = lax.dynamic_slice(gsm["ml_conv_w"], (0, place * conv_cols), (CONV_K, conv_cols))
    own_shapes = [wts[k].shape for k in _SMALL]
    g_pack = _pack([gsm[k] for k in _SMALL])
    res = _adamw(g_pack[None], _pack([wts[k] for k in _SMALL]), _pack([mom[k] for k in _SMALL]), _pack([var[k] for k in _SMALL]), "adamw_small")
    for dst, pack in zip((out_g, out_d, out_m, out_v), res):
        for k, a in zip(_SMALL, _unpack(pack, own_shapes)):
            dst[k] = a

    outs = [loss_total, grad_x[None]]
    for group in (out_g, out_d, out_m, out_v):
        outs += [group[k] for k in _ORDER]
    return tuple(outs)
```

```python
import functools

import jax
import jax.numpy as jnp
from jax import lax
from jax.experimental import pallas as pl
from jax.experimental.pallas import tpu as pltpu

f32 = jnp.float32
bf16 = jnp.bfloat16
HI = lax.Precision.HIGHEST

D_MODEL = 1024
HEADS = 4
HEAD_W = 128
MIX_W = HEADS * HEAD_W
ML_DQK = 64
PROJ_W = 3592
U_HG = 4 * MIX_W
U_ML = 3 * MIX_W + 128
D_FF = 2816
PLE = 256
CHUNK = 128
SUB = 16
EXP_CAP = 80.0
CONV_K = 4
HALO = 8
ALPHA = float(2.0 ** 0.25)
LN_EPS = 1e-5
RMS_EPS = 1e-6
NEG = -1e30
LR, B1, B2, EPS_ADAM, WD, STEP = 0.001, 0.9, 0.999, 1e-08, 0.01, 10
VMEM_LIMIT = 56 * 1024 * 1024
DENSE_ROWS = 512
WGRAD_ROWS = 2048


def _cparams(n_axes, arbitrary=True):
    sem = ("arbitrary",) * n_axes if arbitrary else ("parallel",) * n_axes
    return pltpu.CompilerParams(dimension_semantics=sem, vmem_limit_bytes=VMEM_LIMIT)


ACT = bf16


def _mx(a):
    return a.astype(ACT)


def _bdot(a, b):
    return jnp.dot(_mx(a), _mx(b), preferred_element_type=f32)


def _bdot_nt(a, b):
    return lax.dot_general(_mx(a), _mx(b), (((1,), (1,)), ((), ())), preferred_element_type=f32)


def _bdot_tn(a, b):
    return lax.dot_general(_mx(a), _mx(b), (((0,), (0,)), ((), ())), preferred_element_type=f32)


def _split3(x):
    hi = x.astype(bf16)
    r1 = x - hi.astype(f32)
    mid = r1.astype(bf16)
    lo = (r1 - mid.astype(f32)).astype(bf16)
    return hi, mid, lo


def _dot3(a, b, dims):
    a_hi = a.astype(bf16)
    a_lo = (a - a_hi.astype(f32)).astype(bf16)
    b_hi = b.astype(bf16)
    b_lo = (b - b_hi.astype(f32)).astype(bf16)
    dn = (dims, ((), ()))
    return (lax.dot_general(a_hi, b_hi, dn, preferred_element_type=f32) + lax.dot_general(a_hi, b_lo, dn, preferred_element_type=f32)
            + lax.dot_general(a_lo, b_hi, dn, preferred_element_type=f32))


def _lane_sum(x):
    hi = x.astype(bf16)
    lo = (x - hi.astype(f32)).astype(bf16)
    ones = jnp.ones((x.shape[1], 128), bf16)
    return jnp.dot(hi, ones, preferred_element_type=f32) + jnp.dot(lo, ones, preferred_element_type=f32)


def _lane_dot(x, row):
    return _dot3(x, jnp.broadcast_to(row, (128, row.shape[1])), ((1,), (1,)))


def _sel_dot(sel, x):
    sb = sel.astype(bf16)
    return sum(jnp.dot(sb, part, preferred_element_type=f32) for part in _split3(x))


def _sel_dot_nt(sel, x):
    sb = sel.astype(bf16)
    return sum(lax.dot_general(sb, part, (((1,), (1,)), ((), ())), preferred_element_type=f32) for part in _split3(x))


def _sigmoid(x):
    return 1.0 / (1.0 + jnp.exp(-x))


def _log_sigmoid(x):
    return jnp.minimum(x, 0.0) - jnp.log(1.0 + jnp.exp(-jnp.abs(x)))


def _tri(n, upper=False):
    r = lax.broadcasted_iota(jnp.int32, (n, n), 0)
    c = lax.broadcasted_iota(jnp.int32, (n, n), 1)
    return (c >= r) if upper else (c <= r)


def _rows(tm, n, col=0):
    return pl.BlockSpec((tm, n), lambda i, _c=col: (i, _c))


def _rows_rev(tm, n, nb, col=0):
    return pl.BlockSpec((tm, n), lambda i, _c=col, _nb=nb: (_nb - 1 - i, _c))


def _const(shape):
    return pl.BlockSpec(shape, lambda i, _n=len(shape): (0,) * _n)


def _resident(shape):
    return pl.BlockSpec(shape, lambda i, _n=len(shape): (0,) * _n, pipeline_mode=pl.Buffered(1))


def _tile(t, want):
    return want if t % want == 0 else t


def _inproj(x, w_hg, w_ml, b_hg, b_ml, riders=()):
    t = x.shape[0]
    tm = _tile(t, DENSE_ROWS)

    def body(x_ref, whg_ref, wml_ref, bhg_ref, bml_ref, uhg_ref, uml_ref, xb_ref):
        xb = _mx(x_ref[...])
        xb_ref[...] = xb
        uhg_ref[...] = _bdot_nt(xb, whg_ref[...]) + bhg_ref[...]
        uml_ref[...] = _bdot_nt(xb, wml_ref[...]) + bml_ref[...]

    return _riding_call(
        body, "inproj", t // tm,
        in_specs=[_rows(tm, D_MODEL), _resident((U_HG, D_MODEL)), _resident((U_ML, D_MODEL)), _const((1, U_HG)), _const((1, U_ML))],
        out_specs=[_rows(tm, U_HG), _rows(tm, U_ML), _rows(tm, D_MODEL)],
        out_shape=[jax.ShapeDtypeStruct((t, U_HG), f32), jax.ShapeDtypeStruct((t, U_ML), f32), jax.ShapeDtypeStruct((t, D_MODEL), ACT)],
        scratch_shapes=[], operands=(x, w_hg, w_ml, b_hg, b_ml), riders=riders, copies=_gather_copies, ride_shapes=_gather_shapes(riders))


def _hg_gates(hq, hf, lb, tri):
    s = _sigmoid(hf)
    om = 1.0 - lb
    f = lb + om * s
    g = jnp.log(f)
    k = om * (1.0 - s)
    sq = _sigmoid(hq)
    q = hq * sq
    b = _sel_dot(tri, g)
    return q, sq, s, f, k, b


def _hg_scores(q, k, b, tril_mask):
    qts, kts, eqs, eks, rows = [], [], [], [], []
    for i in range(CHUNK // SUB):
        lo = i * SUB
        ref = jnp.zeros_like(b[0:1]) if i == 0 else b[lo - 1:lo]
        eq = jnp.exp(b[lo:lo + SUB] - ref)
        ek = jnp.exp(jnp.minimum(ref - b, EXP_CAP))
        qt = q[lo:lo + SUB] * eq
        kt = k * ek
        rows.append(_bdot_nt(qt, kt))
        qts.append(qt); kts.append(kt); eqs.append(eq); eks.append(ek)
    a = jnp.where(tril_mask, jnp.concatenate(rows, axis=0), 0.0)
    return a, qts, kts, eqs, eks


def _head_rms(o, gn, on_mxu=False):
    ms = _lane_sum(o * o) * (1.0 / o.shape[1]) if on_mxu else jnp.mean(o * o, axis=-1, keepdims=True)
    rstd = lax.rsqrt(ms + RMS_EPS)
    oh = o * rstd
    return oh, rstd, oh * gn


def _lower_bound(logit_ref):
    lg = logit_ref[...]
    return _sigmoid(lg[0:1] - lg[1:2])


def _hgrn2_fwd(u_hg, logits, gn, riders=()):
    t = u_hg.shape[0]
    tb = _tile(t, 256)
    nc_blk = tb // CHUNK

    def body(u_ref, lg_ref, gn_ref, og_ref, sst_ref, st_ref):
        @pl.when(pl.program_id(0) == 0)
        def _():
            st_ref[...] = jnp.zeros_like(st_ref)

        lb_all = _lower_bound(lg_ref)
        tril_mask = _tri(CHUNK)
        tri = tril_mask.astype(f32)

        def chunk(c, carry):
            r0 = pl.multiple_of(c * CHUNK, CHUNK)
            rows = pl.ds(r0, CHUNK)
            heads = range(HEADS)
            cols = [slice(h * HEAD_W, (h + 1) * HEAD_W) for h in heads]
            hv = [u_ref[rows, 2 * MIX_W + h * HEAD_W:2 * MIX_W + (h + 1) * HEAD_W] for h in heads]
            gts = [_hg_gates(u_ref[rows, h * HEAD_W:(h + 1) * HEAD_W], u_ref[rows, MIX_W + h * HEAD_W:MIX_W + (h + 1) * HEAD_W],
                             lb_all[:, cols[h]], tri) for h in heads]
            q = [g[0] for g in gts]
            k = [g[4] for g in gts]
            b = [g[5] for g in gts]
            a = [_hg_scores(q[h], k[h], b[h], tril_mask)[0] for h in heads]
            st = [st_ref[h] for h in heads]
            bl = [b[h][CHUNK - 1:CHUNK] for h in heads]
            o = [_bdot(a[h], hv[h]) + _bdot_nt(q[h] * jnp.exp(b[h]), st[h]) for h in heads]
            new_st = [st[h] * jnp.exp(bl[h]) + _bdot_tn(hv[h], k[h] * jnp.exp(bl[h] - b[h])) for h in heads]
            for h in heads:
                sst_ref[c, h] = st[h]
                st_ref[h] = new_st[h]
                hgate = u_ref[rows, 3 * MIX_W + h * HEAD_W:3 * MIX_W + (h + 1) * HEAD_W]
                _, _, y = _head_rms(o[h], gn_ref[:, cols[h]])
                og_ref[rows, cols[h]] = (y * (hgate * _sigmoid(hgate))).astype(ACT)
            return carry

        lax.fori_loop(0, nc_blk, chunk, 0, unroll=True)

    return _riding_call(
        body, "hgrn2_fwd", t // tb,
        in_specs=[_rows(tb, U_HG), _const((2, MIX_W)), _const((1, MIX_W))],
        out_specs=[_rows(tb, MIX_W), pl.BlockSpec((nc_blk, HEADS, HEAD_W, HEAD_W), lambda i: (i, 0, 0, 0))],
        out_shape=[jax.ShapeDtypeStruct((t, MIX_W), ACT), jax.ShapeDtypeStruct((t // CHUNK, HEADS, HEAD_W, HEAD_W), f32)],
        scratch_shapes=[pltpu.VMEM((HEADS, HEAD_W, HEAD_W), f32)],
        operands=(u_hg, logits, gn), riders=riders, copies=_gather_copies, ride_shapes=_gather_shapes(riders))


def _hgrn2_bwd(u_hg, logits, gn, sst, dog, riders=()):
    t = u_hg.shape[0]
    tb = _tile(t, 256)
    nb = t // tb
    nc_blk = tb // CHUNK
    nr = len(riders)

    def body(*refs):
        u_ref, lg_ref, gn_ref, sst_ref, dog_ref = refs[:5]
        ride_in = refs[5:5 + nr]
        du_ref, dlg_ref, dgn_ref = refs[5 + nr:8 + nr]
        ride_out = refs[8 + nr:8 + 2 * nr]
        dst_ref = refs[8 + 2 * nr]
        ride_sems = refs[9 + 2 * nr:]

        @pl.when(pl.program_id(0) == 0)
        def _():
            dst_ref[...] = jnp.zeros_like(dst_ref)
            dlg_ref[...] = jnp.zeros_like(dlg_ref)
            dgn_ref[...] = jnp.zeros_like(dgn_ref)
            if nr:
                _scatter_start(ride_in, ride_out, *ride_sems)

        lb_all = _lower_bound(lg_ref)
        tril_mask = _tri(CHUNK)
        tri = tril_mask.astype(f32)
        triu = _tri(CHUNK, upper=True).astype(f32)

        def chunk(j, carry):
            c = nc_blk - 1 - j
            r0 = pl.multiple_of(c * CHUNK, CHUNK)
            rows = pl.ds(r0, CHUNK)
            heads = range(HEADS)
            nsub = CHUNK // SUB
            cols = [slice(h * HEAD_W, (h + 1) * HEAD_W) for h in heads]
            hq = [u_ref[rows, h * HEAD_W:(h + 1) * HEAD_W] for h in heads]
            hf = [u_ref[rows, MIX_W + h * HEAD_W:MIX_W + (h + 1) * HEAD_W] for h in heads]
            hv = [u_ref[rows, 2 * MIX_W + h * HEAD_W:2 * MIX_W + (h + 1) * HEAD_W] for h in heads]
            lb = [lb_all[:, cols[h]] for h in heads]
            gts = [_hg_gates(hq[h], hf[h], lb[h], tri) for h in heads]
            q, sq, s, f, k, b = ([g[n] for g in gts] for n in range(6))
            scs = [_hg_scores(q[h], k[h], b[h], tril_mask) for h in heads]
            a, qts, kts, eqs, eks = ([sc[n] for sc in scs] for n in range(5))
            st = [sst_ref[c, h] for h in heads]
            dst = [dst_ref[h] for h in heads]
            bl = [b[h][CHUNK - 1:CHUNK] for h in heads]
            eb = [jnp.exp(b[h]) for h in heads]
            qh = [q[h] * eb[h] for h in heads]
            ekl = [jnp.exp(bl[h] - b[h]) for h in heads]
            kh = [k[h] * ekl[h] for h in heads]
            o = [_bdot(a[h], hv[h]) + _bdot_nt(qh[h], st[h]) for h in heads]
            do = []
            for h in heads:
                hgate = u_ref[rows, 3 * MIX_W + h * HEAD_W:3 * MIX_W + (h + 1) * HEAD_W]
                gnh = gn_ref[:, cols[h]]
                oh, rstd, y = _head_rms(o[h], gnh)
                sg = _sigmoid(hgate)
                dogh = dog_ref[rows, cols[h]]
                dy = dogh * (hgate * sg)
                du_ref[rows, 3 * MIX_W + h * HEAD_W:3 * MIX_W + (h + 1) * HEAD_W] = (dogh * y * (sg * (1.0 + hgate * (1.0 - sg)))).astype(ACT)
                dgn_ref[:, cols[h]] += jnp.sum(dy * oh, axis=0, keepdims=True)
                doh = dy * gnh
                do.append(rstd * (doh - oh * jnp.mean(doh * oh, axis=-1, keepdims=True)))
            da = [jnp.where(tril_mask, _bdot_nt(do[h], hv[h]), 0.0) for h in heads]
            dv = [_bdot_tn(a[h], do[h]) + _bdot_nt(kh[h], dst[h]) for h in heads]
            dq = [_bdot(do[h], st[h]) * eb[h] for h in heads]
            dk = [_bdot(hv[h], dst[h]) * ekl[h] for h in heads]
            d_last = [jnp.sum(k[h] * dk[h], axis=0, keepdims=True) + jnp.exp(bl[h]) * jnp.sum(dst[h] * st[h], axis=0, keepdims=True)
                      for h in heads]
            d_b = [q[h] * dq[h] - k[h] * dk[h] for h in heads]
            dqs = [[] for _ in heads]
            q_dq = [[] for _ in heads]
            for i in range(nsub):
                for h in heads:
                    da_i = _mx(da[h][i * SUB:(i + 1) * SUB])
                    q_r, k_r = _mx(qts[h][i]), _mx(kts[h][i])
                    g_q = jnp.dot(da_i, k_r, preferred_element_type=f32)
                    g_k = lax.dot_general(da_i, q_r, (((0,), (0,)), ((), ())), preferred_element_type=f32)
                    dqs[h].append(g_q * eqs[h][i])
                    q_dq[h].append(q_r.astype(f32) * g_q)
                    dk[h] = dk[h] + g_k * eks[h][i]
                    d_b[h] = d_b[h] - k_r.astype(f32) * g_k
            for h in heads:
                dq[h] = dq[h] + jnp.concatenate(dqs[h], axis=0)
                d_b[h] = d_b[h] + jnp.concatenate(q_dq[h], axis=0)
                dst_ref[h] = dst[h] * jnp.exp(bl[h]) + _bdot_tn(do[h], qh[h])
            dg = [_sel_dot(triu, d_b[h]) + d_last[h] for h in heads]
            for h in heads:
                dfk = dg[h] / f[h] - dk[h]
                du_ref[rows, h * HEAD_W:(h + 1) * HEAD_W] = (dq[h] * (sq[h] * (1.0 + hq[h] * (1.0 - sq[h])))).astype(ACT)
                du_ref[rows, MIX_W + h * HEAD_W:MIX_W + (h + 1) * HEAD_W] = ((1.0 - lb[h]) * dfk * s[h] * (1.0 - s[h])).astype(ACT)
                du_ref[rows, 2 * MIX_W + h * HEAD_W:2 * MIX_W + (h + 1) * HEAD_W] = dv[h].astype(ACT)
                dlb = jnp.sum((1.0 - s[h]) * dfk, axis=0, keepdims=True) * (lb[h] * (1.0 - lb[h]))
                dlg_ref[0:1, cols[h]] += dlb
                dlg_ref[1:2, cols[h]] -= dlb
            return carry

        lax.fori_loop(0, nc_blk, chunk, 0, unroll=True)

        if nr:
            @pl.when(pl.program_id(0) == nb - 1)
            def _():
                _scatter_wait(ride_in, ride_out, *ride_sems)

    hbm = pl.BlockSpec(memory_space=pltpu.HBM)
    ride_scratch = [pltpu.SemaphoreType.DMA((3 * nr,)), pltpu.SemaphoreType.DMA((3 * nr,)), pltpu.SemaphoreType.DMA((nr,))] if nr else []
    return pl.pallas_call(
        body, name="hgrn2_bwd", grid=(nb,),
        in_specs=[_rows_rev(tb, U_HG, nb), _const((2, MIX_W)), _const((1, MIX_W)),
                  pl.BlockSpec((nc_blk, HEADS, HEAD_W, HEAD_W), lambda i: (nb - 1 - i, 0, 0, 0)), _rows_rev(tb, MIX_W, nb)] + [hbm] * nr,
        out_specs=[_rows_rev(tb, U_HG, nb), _const((2, MIX_W)), _const((1, MIX_W))] + [hbm] * nr,
        out_shape=[jax.ShapeDtypeStruct((t, U_HG), ACT), jax.ShapeDtypeStruct((2, MIX_W), f32), jax.ShapeDtypeStruct((1, MIX_W), f32)]
        + [jax.ShapeDtypeStruct(r.shape, r.dtype) for r in riders],
        scratch_shapes=[pltpu.VMEM((HEADS, HEAD_W, HEAD_W), f32)] + ride_scratch,
        compiler_params=_cparams(1),
    )(u_hg, logits, gn, sst, dog, *riders)


def _conv_fwd(u_ml, w, b):
    t = u_ml.shape[0]
    tm = _tile(t, 512)

    def body(x_ref, w_ref, b_ref, pre_ref, act_ref, xbuf):
        @pl.when(pl.program_id(0) == 0)
        def _():
            xbuf[...] = jnp.zeros_like(xbuf)

        xbuf[0:HALO, :] = xbuf[tm:tm + HALO, :]
        xbuf[HALO:HALO + tm, :] = x_ref[...]
        pre = b_ref[...] + jnp.zeros((tm, MIX_W), f32)
        for kk in range(CONV_K):
            off = HALO - (CONV_K - 1) + kk
            pre = pre + w_ref[kk:kk + 1, :] * xbuf[off:off + tm, :]
        pre_ref[...] = pre
        act_ref[...] = pre * _sigmoid(pre)

    return pl.pallas_call(
        body, name="conv_fwd", grid=(t // tm,),
        in_specs=[_rows(tm, MIX_W), _const((CONV_K, MIX_W)), _const((1, MIX_W))],
        out_specs=[_rows(tm, MIX_W), _rows(tm, MIX_W)],
        out_shape=[jax.ShapeDtypeStruct((t, MIX_W), f32)] * 2,
        scratch_shapes=[pltpu.VMEM((tm + HALO, MIX_W), f32)],
        compiler_params=_cparams(1),
    )(u_ml, w, b)


def _conv_bwd(u_ml, w, pre, dact):
    t = u_ml.shape[0]
    tm = _tile(t, 512)
    nb = t // tm
    hb = tm // HALO

    def body(x_ref, halo_ref, w_ref, pre_ref, dact_ref, dx_ref, dw_ref, db_ref, dbuf, xbuf):
        i = pl.program_id(0)

        @pl.when(i == 0)
        def _():
            dbuf[...] = jnp.zeros_like(dbuf)
            dw_ref[...] = jnp.zeros_like(dw_ref)
            db_ref[...] = jnp.zeros_like(db_ref)

        p = pre_ref[...]
        sg = _sigmoid(p)
        dpre = dact_ref[...] * (sg * (1.0 + p * (1.0 - sg)))
        dbuf[tm:tm + HALO, :] = dbuf[0:HALO, :]
        dbuf[0:tm, :] = dpre
        has_prev = (i < nb - 1).astype(f32)
        xbuf[0:HALO, :] = halo_ref[...] * has_prev
        xbuf[HALO:HALO + tm, :] = x_ref[...]
        dx = jnp.zeros((tm, MIX_W), f32)
        for kk in range(CONV_K):
            back = CONV_K - 1 - kk
            dx = dx + w_ref[kk:kk + 1, :] * dbuf[back:back + tm, :]
            off = HALO - (CONV_K - 1) + kk
            dw_ref[kk:kk + 1, :] += jnp.sum(dpre * xbuf[off:off + tm, :], axis=0, keepdims=True)
        dx_ref[...] = dx.astype(ACT)
        db_ref[...] += jnp.sum(dpre, axis=0, keepdims=True)

    return pl.pallas_call(
        body, name="conv_bwd", grid=(nb,),
        in_specs=[_rows_rev(tm, MIX_W, nb),
                  pl.BlockSpec((HALO, MIX_W), lambda i: (jnp.maximum((nb - 1 - i) * hb - 1, 0), 0)),
                  _const((CONV_K, MIX_W)), _rows_rev(tm, MIX_W, nb), _rows_rev(tm, MIX_W, nb)],
        out_specs=[_rows_rev(tm, MIX_W, nb), _const((CONV_K, MIX_W)), _const((1, MIX_W))],
        out_shape=[jax.ShapeDtypeStruct((t, MIX_W), ACT), jax.ShapeDtypeStruct((CONV_K, MIX_W), f32), jax.ShapeDtypeStruct((1, MIX_W), f32)],
        scratch_shapes=[pltpu.VMEM((tm + HALO, MIX_W), f32), pltpu.VMEM((tm + HALO, MIX_W), f32)],
        compiler_params=_cparams(1),
    )(u_ml, u_ml, w, pre, dact)


def _lane_pick(x, lane):
    idx = lax.broadcasted_iota(jnp.int32, x.shape, 1)
    return jnp.sum(jnp.where(idx == lane, x, 0.0), axis=-1, keepdims=True)


def _ml_gate_forms(gates, tri):
    lf = _log_sigmoid(gates)
    gc = _sel_dot(tri, lf)
    lane = lax.broadcasted_iota(jnp.int32, gates.shape, 1)
    mixed = jnp.where(lane < HEADS, gates, gc)
    sel = (lax.broadcasted_iota(jnp.int32, (8, 128), 0) == lax.broadcasted_iota(jnp.int32, (8, 128), 1)).astype(f32)
    rowsf = _sel_dot_nt(sel, mixed)
    return gc, rowsf


def _ml_chunk(q, k, v, gates, gc, rowsf, c_st, n_st, m_st, tril_mask):
    hs = range(HEADS)
    g_col = [_lane_pick(gc, HEADS + h) for h in hs]
    ig_col = [_lane_pick(gates, h) for h in hs]
    dmat = [jnp.where(tril_mask, g_col[h] - rowsf[HEADS + h:HEADS + h + 1, :] + rowsf[h:h + 1, :], NEG) for h in hs]
    m_inter = [g_col[h] + m_st[h] for h in hs]
    m_t = [jnp.maximum(m_inter[h], jnp.max(dmat[h], axis=-1, keepdims=True)) for h in hs]
    wi = [jnp.exp(dmat[h] - m_t[h]) for h in hs]
    wo = [jnp.exp(m_inter[h] - m_t[h]) for h in hs]
    qk = [_bdot_nt(q[h], k[h]) * wi[h] for h in hs]
    num = [_bdot(qk[h], v[h]) + wo[h] * _bdot(q[h], c_st[h]) for h in hs]
    den = [_lane_sum(qk[h]) + wo[h] * _lane_dot(q[h], n_st[h]) for h in hs]
    floor = [jnp.exp(-m_t[h]) for h in hs]
    z = [jnp.maximum(jnp.abs(den[h]), floor[h]) for h in hs]
    g_last = [g_col[h][CHUNK - 1:CHUNK] for h in hs]
    a_col = [g_last[h] - g_col[h] + ig_col[h] for h in hs]
    m_new = [jnp.maximum(g_last[h] + m_st[h], jnp.max(a_col[h], axis=0, keepdims=True)) for h in hs]
    ws = [jnp.exp(a_col[h] - m_new[h]) for h in hs]
    w_old = [jnp.exp(g_last[h] + m_st[h] - m_new[h]) for h in hs]
    return dict(wi=wi, wo=wo, qk=qk, num=num, den=den, z=z, floor=floor, ws=ws, w_old=w_old, m_new=m_new)


def _mlstm_fwd(qkc, u_ml, gn, riders=()):
    t = qkc.shape[0]
    tb = _tile(t, 256)
    nc_blk = tb // CHUNK

    def body(qk_ref, v_ref, mo_ref, gt_ref, gn_ref, og_ref, cst_ref, nst_ref, mst_ref, c_sc, n_sc, m_sc):
        @pl.when(pl.program_id(0) == 0)
        def _():
            c_sc[...] = jnp.zeros_like(c_sc)
            n_sc[...] = jnp.zeros_like(n_sc)
            m_sc[...] = jnp.zeros_like(m_sc)

        tril_mask = _tri(CHUNK)
        tri = tril_mask.astype(f32)

        def chunk(c, carry):
            r0 = pl.multiple_of(c * CHUNK, CHUNK)
            rows = pl.ds(r0, CHUNK)
            gates = gt_ref[rows, :]
            gc, rowsf = _ml_gate_forms(gates, tri)
            hs = range(HEADS)
            q = [qk_ref[rows, h * ML_DQK:(h + 1) * ML_DQK] * (ML_DQK ** -0.5) for h in hs]
            k = [qk_ref[rows, HEADS * ML_DQK + h * ML_DQK:HEADS * ML_DQK + (h + 1) * ML_DQK] for h in hs]
            v = [v_ref[rows, h * HEAD_W:(h + 1) * HEAD_W] for h in hs]
            c_st = [c_sc[h] for h in hs]
            n_st = [n_sc[h] for h in hs]
            m_full = [m_sc[h] for h in hs]
            r = _ml_chunk(q, k, v, gates, gc, rowsf, c_st, n_st, [m[:, 0:1] for m in m_full], tril_mask)
            ksc = [k[h] * r["ws"][h] for h in hs]
            new_c = [r["w_old"][h] * c_st[h] + _bdot_tn(ksc[h], v[h]) for h in hs]
            for h in hs:
                cs = slice(h * HEAD_W, (h + 1) * HEAD_W)
                cst_ref[c, h] = c_st[h]
                nst_ref[c, h] = n_st[h]
                mst_ref[c, h] = m_full[h]
                c_sc[h] = new_c[h]
                n_sc[h] = r["w_old"][h] * n_st[h] + jnp.sum(ksc[h], axis=0, keepdims=True)
                m_sc[h] = r["m_new"][h] + jnp.zeros((1, 128), f32)
                _, _, y = _head_rms(r["num"][h] / r["z"][h], gn_ref[:, cs], on_mxu=True)
                og_ref[rows, cs] = (y * _sigmoid(mo_ref[rows, h * HEAD_W:(h + 1) * HEAD_W])).astype(ACT)
            return carry

        lax.fori_loop(0, nc_blk, chunk, 0)

    nchunks = t // CHUNK
    return _riding_call(
        body, "mlstm_fwd", t // tb,
        in_specs=[_rows(tb, MIX_W), _rows(tb, MIX_W, 1), _rows(tb, MIX_W, 2), _rows(tb, 128, 12), _const((1, MIX_W))],
        out_specs=[_rows(tb, MIX_W),
                   pl.BlockSpec((nc_blk, HEADS, ML_DQK, HEAD_W), lambda i: (i, 0, 0, 0)),
                   pl.BlockSpec((nc_blk, HEADS, 1, ML_DQK), lambda i: (i, 0, 0, 0)),
                   pl.BlockSpec((nc_blk, HEADS, 1, 128), lambda i: (i, 0, 0, 0))],
        out_shape=[jax.ShapeDtypeStruct((t, MIX_W), ACT),
                   jax.ShapeDtypeStruct((nchunks, HEADS, ML_DQK, HEAD_W), f32),
                   jax.ShapeDtypeStruct((nchunks, HEADS, 1, ML_DQK), f32),
                   jax.ShapeDtypeStruct((nchunks, HEADS, 1, 128), f32)],
        scratch_shapes=[pltpu.VMEM((HEADS, ML_DQK, HEAD_W), f32), pltpu.VMEM((HEADS, 1, ML_DQK), f32), pltpu.VMEM((HEADS, 1, 128), f32)],
        operands=(qkc, u_ml, u_ml, u_ml, gn), riders=riders, copies=_gather_copies, ride_shapes=_gather_shapes(riders))


def _mlstm_bwd(qkc, u_ml, gn, cst, nst, mst, dog):
    t = qkc.shape[0]
    tb = _tile(t, 256)
    nb = t // tb
    nc_blk = tb // CHUNK

    def body(qk_ref, v_ref, mo_ref, gt_ref, gn_ref, cst_ref, nst_ref, mst_ref, dog_ref,
             dqk_ref, dv_ref, dmo_ref, dgt_ref, dgn_ref, dc_sc, dn_sc):
        @pl.when(pl.program_id(0) == 0)
        def _():
            dc_sc[...] = jnp.zeros_like(dc_sc)
            dn_sc[...] = jnp.zeros_like(dn_sc)
            dgn_ref[...] = jnp.zeros_like(dgn_ref)

        tril_mask = _tri(CHUNK)
        tri = tril_mask.astype(f32)
        triu = _tri(CHUNK, upper=True).astype(f32)
        lane = lax.broadcasted_iota(jnp.int32, (CHUNK, 128), 1)

        def chunk(j, carry):
            c = nc_blk - 1 - j
            r0 = pl.multiple_of(c * CHUNK, CHUNK)
            rows = pl.ds(r0, CHUNK)
            gates = gt_ref[rows, :]
            gc, rowsf = _ml_gate_forms(gates, tri)
            dg_mat = jnp.zeros((CHUNK, 128), f32)
            dig_mat = jnp.zeros((CHUNK, 128), f32)
            dlast_row = jnp.zeros((1, 128), f32)
            hs = range(HEADS)
            cols = [slice(h * HEAD_W, (h + 1) * HEAD_W) for h in hs]
            q = [qk_ref[rows, h * ML_DQK:(h + 1) * ML_DQK] * (ML_DQK ** -0.5) for h in hs]
            k = [qk_ref[rows, HEADS * ML_DQK + h * ML_DQK:HEADS * ML_DQK + (h + 1) * ML_DQK] for h in hs]
            v = [v_ref[rows, h * HEAD_W:(h + 1) * HEAD_W] for h in hs]
            c_st = [cst_ref[c, h] for h in hs]
            n_st = [nst_ref[c, h] for h in hs]
            m_st = [mst_ref[c, h][:, 0:1] for h in hs]
            dc = [dc_sc[h] for h in hs]
            dn = [dn_sc[h] for h in hs]
            r = _ml_chunk(q, k, v, gates, gc, rowsf, c_st, n_st, m_st, tril_mask)
            z, wi, wo, ws, w_old, den = r["z"], r["wi"], r["wo"], r["ws"], r["w_old"], r["den"]
            hh = [r["num"][h] / z[h] for h in hs]
            dh = []
            for h in hs:
                gnh = gn_ref[:, cols[h]]
                oh, rstd, y = _head_rms(hh[h], gnh, on_mxu=True)
                sg = _sigmoid(mo_ref[rows, h * HEAD_W:(h + 1) * HEAD_W])
                dogh = dog_ref[rows, cols[h]]
                dy = dogh * sg
                dmo_ref[rows, cols[h]] = (dogh * y * (sg * (1.0 - sg))).astype(ACT)
                dgn_ref[:, cols[h]] += jnp.sum(dy * oh, axis=0, keepdims=True)
                doh = dy * gnh
                dh.append(rstd * (doh - oh * (_lane_sum(doh * oh) * (1.0 / HEAD_W))))
            dnum = [dh[h] / z[h] for h in hs]
            dz = [-_lane_sum(dh[h] * hh[h]) / z[h] for h in hs]
            dden = [jnp.where(jnp.abs(den[h]) > r["floor"][h], dz[h] * jnp.sign(den[h]), 0.0) for h in hs]
            dsw = [(_bdot_nt(dnum[h], v[h]) + dden[h]) * wi[h] for h in hs]
            dq = [_bdot(dsw[h], k[h]) + wo[h] * (_bdot_nt(dnum[h], c_st[h]) + dden[h][:, :ML_DQK] * n_st[h]) for h in hs]
            dk_state = [ws[h] * (_bdot_nt(v[h], dc[h]) + dn[h]) for h in hs]
            dk = [_bdot_tn(dsw[h], q[h]) + dk_state[h] for h in hs]
            dv = [_bdot_tn(r["qk"][h], dnum[h]) + ws[h] * _bdot(k[h], dc[h]) for h in hs]
            woq = [wo[h] * q[h] for h in hs]
            new_dc = [w_old[h] * dc[h] + _bdot_tn(woq[h], dnum[h]) for h in hs]
            for h in hs:
                dv_ref[rows, cols[h]] = dv[h].astype(ACT)
                dc_sc[h] = new_dc[h]
                dn_sc[h] = w_old[h] * dn[h] + jnp.sum(woq[h] * dden[h][:, :ML_DQK], axis=0, keepdims=True)
                d_last = (jnp.sum(jnp.sum(k[h] * dk_state[h], axis=0, keepdims=True), axis=-1, keepdims=True)
                          + w_old[h] * (jnp.sum(jnp.sum(dc[h] * c_st[h], axis=0, keepdims=True), axis=-1, keepdims=True)
                                        + jnp.sum(dn[h] * n_st[h], axis=-1, keepdims=True)))
                kdk = _lane_sum(k[h] * dk[h])
                qdq = _lane_sum(q[h] * dq[h])
                dg_mat = dg_mat + jnp.where(lane == HEADS + h, qdq - kdk, 0.0)
                dlast_row = dlast_row + jnp.where(lane[0:1] == HEADS + h, d_last, 0.0)
                dig_mat = dig_mat + jnp.where(lane == h, kdk, 0.0)
                dqk_ref[rows, h * ML_DQK:(h + 1) * ML_DQK] = dq[h] * (ML_DQK ** -0.5)
                dqk_ref[rows, HEADS * ML_DQK + h * ML_DQK:HEADS * ML_DQK + (h + 1) * ML_DQK] = dk[h]
            dlf = _sel_dot(triu, dg_mat) + dlast_row
            dgt_ref[rows, :] = (dig_mat + dlf * _sigmoid(-gates)).astype(ACT)
            return carry

        lax.fori_loop(0, nc_blk, chunk, 0)

    st4 = lambda a, b: pl.BlockSpec((nc_blk, HEADS, a, b), lambda i: (nb - 1 - i, 0, 0, 0))
    return pl.pallas_call(
        body, name="mlstm_bwd", grid=(nb,),
        in_specs=[_rows_rev(tb, MIX_W, nb), _rows_rev(tb, MIX_W, nb, 1), _rows_rev(tb, MIX_W, nb, 2), _rows_rev(tb, 128, nb, 12),
                  _const((1, MIX_W)), st4(ML_DQK, HEAD_W), st4(1, ML_DQK), st4(1, 128), _rows_rev(tb, MIX_W, nb)],
        out_specs=[_rows_rev(tb, MIX_W, nb), _rows_rev(tb, MIX_W, nb), _rows_rev(tb, MIX_W, nb), _rows_rev(tb, 128, nb), _const((1, MIX_W))],
        out_shape=[jax.ShapeDtypeStruct((t, MIX_W), f32), jax.ShapeDtypeStruct((t, MIX_W), ACT), jax.ShapeDtypeStruct((t, MIX_W), ACT),
                   jax.ShapeDtypeStruct((t, 128), ACT), jax.ShapeDtypeStruct((1, MIX_W), f32)],
        scratch_shapes=[pltpu.VMEM((HEADS, ML_DQK, HEAD_W), f32), pltpu.VMEM((HEADS, 1, ML_DQK), f32)],
        compiler_params=_cparams(1),
    )(qkc, u_ml, u_ml, u_ml, gn, cst, nst, mst, dog)


def _ln_fwd(r, g, b):
    mu = jnp.mean(r, axis=-1, keepdims=True)
    xc = r - mu
    rstd = lax.rsqrt(jnp.mean(xc * xc, axis=-1, keepdims=True) + LN_EPS)
    xh = xc * rstd
    return xh * g + b, xh, rstd


def _ln_bwd(dy, xh, rstd, g):
    dxh = dy * g
    return rstd * (dxh - jnp.mean(dxh, axis=-1, keepdims=True) - xh * jnp.mean(dxh * xh, axis=-1, keepdims=True))


def _outproj_ln1(og_hg, og_ml, x, w_out, g, b, riders=()):
    t = x.shape[0]
    tm = _tile(t, DENSE_ROWS)

    def body(a_ref, b_ref, x_ref, w_ref, g_ref, bb_ref, x1_ref, xh_ref, rs_ref, x1b_ref):
        mix = _bdot(a_ref[...], w_ref[0:MIX_W, :]) + _bdot(b_ref[...], w_ref[MIX_W:2 * MIX_W, :])
        y, xh, rstd = _ln_fwd(ALPHA * x_ref[...] + mix, g_ref[...], bb_ref[...])
        x1_ref[...] = y
        x1b_ref[...] = y.astype(ACT)
        xh_ref[...] = xh.astype(ACT)
        rs_ref[...] = rstd

    return _riding_call(
        body, "outproj_ln1", t // tm,
        in_specs=[_rows(tm, MIX_W), _rows(tm, MIX_W), _rows(tm, D_MODEL), _resident((D_MODEL, D_MODEL)), _const((1, D_MODEL)), _const((1, D_MODEL))],
        out_specs=[_rows(tm, D_MODEL), _rows(tm, D_MODEL), _rows(tm, 1), _rows(tm, D_MODEL)],
        out_shape=[jax.ShapeDtypeStruct((t, D_MODEL), f32), jax.ShapeDtypeStruct((t, D_MODEL), ACT), jax.ShapeDtypeStruct((t, 1), f32),
                   jax.ShapeDtypeStruct((t, D_MODEL), ACT)],
        scratch_shapes=[], operands=(og_hg, og_ml, x, w_out, g, b), riders=riders, copies=_gather_copies, ride_shapes=_gather_shapes(riders))


def _ffn_up(x1, wg, wu, riders=()):
    t = x1.shape[0]
    tm = _tile(t, DENSE_ROWS)

    def body(x_ref, wg_ref, wu_ref, hg_ref, up_ref, a_ref):
        xv = x_ref[...]
        hg = _bdot_nt(xv, wg_ref[...])
        up = _bdot_nt(xv, wu_ref[...])
        hg_ref[...] = hg.astype(ACT)
        up_ref[...] = up.astype(ACT)
        a_ref[...] = (hg * _sigmoid(hg) * up).astype(ACT)

    return _riding_call(
        body, "ffn_up", t // tm,
        in_specs=[_rows(tm, D_MODEL), _resident((D_FF, D_MODEL)), _resident((D_FF, D_MODEL))],
        out_specs=[_rows(tm, D_FF), _rows(tm, D_FF), _rows(tm, D_FF)],
        out_shape=[jax.ShapeDtypeStruct((t, D_FF), ACT), jax.ShapeDtypeStruct((t, D_FF), ACT), jax.ShapeDtypeStruct((t, D_FF), ACT)],
        scratch_shapes=[], operands=(x1, wg, wu), riders=riders, copies=_gather_copies, ride_shapes=_gather_shapes(riders))


def _ffn_down_ln2(a, x1, wd, g, b):
    t = x1.shape[0]
    tm = _tile(t, DENSE_ROWS)

    def body(a_ref, x_ref, w_ref, g_ref, bb_ref, x2_ref, xh_ref, rs_ref, x2b_ref):
        ffn = _bdot(a_ref[...], w_ref[...])
        y, xh, rstd = _ln_fwd(ALPHA * x_ref[...] + ffn, g_ref[...], bb_ref[...])
        x2_ref[...] = y
        x2b_ref[...] = y.astype(ACT)
        xh_ref[...] = xh.astype(ACT)
        rs_ref[...] = rstd

    return pl.pallas_call(
        body, name="ffn_down_ln2", grid=(t // tm,),
        in_specs=[_rows(tm, D_FF), _rows(tm, D_MODEL), _resident((D_FF, D_MODEL)), _const((1, D_MODEL)), _const((1, D_MODEL))],
        out_specs=[_rows(tm, D_MODEL), _rows(tm, D_MODEL), _rows(tm, 1), _rows(tm, D_MODEL)],
        out_shape=[jax.ShapeDtypeStruct((t, D_MODEL), f32), jax.ShapeDtypeStruct((t, D_MODEL), ACT), jax.ShapeDtypeStruct((t, 1), f32),
                   jax.ShapeDtypeStruct((t, D_MODEL), ACT)],
        compiler_params=_cparams(1, arbitrary=False),
    )(a, x1, wd, g, b)


def _head_loss_bwd(x2, xh2, rs2, p, tgt, w_pg, b_pg, w_pp, g2):
    t = x2.shape[0]
    tm = _tile(t, DENSE_ROWS)

    def body(x_ref, xh_ref, rs_ref, p_ref, t_ref, wg_ref, bg_ref, wp_ref, g_ref,
             dr_ref, de_ref, dz_ref, loss_ref, dbg_ref, dg2_ref, db2_ref):
        @pl.when(pl.program_id(0) == 0)
        def _():
            loss_ref[...] = jnp.zeros_like(loss_ref)
            dbg_ref[...] = jnp.zeros_like(dbg_ref)
            dg2_ref[...] = jnp.zeros_like(dg2_ref)
            db2_ref[...] = jnp.zeros_like(db2_ref)

        x2v = x_ref[...]
        z = _bdot(x2v, wg_ref[...]) + bg_ref[...]
        e = _bdot(p_ref[...], wp_ref[...])
        sg = _sigmoid(z)
        diff = x2v + sg * e - t_ref[...]
        loss_ref[...] += 0.5 * jnp.sum(jnp.mean(diff * diff, axis=-1, keepdims=True), axis=0, keepdims=True)
        dy = diff * (1.0 / D_MODEL)
        de_ref[...] = (dy * sg).astype(ACT)
        dz = dy * e * (sg * (1.0 - sg))
        dz_ref[...] = dz.astype(ACT)
        dbg_ref[...] += jnp.sum(dz, axis=0, keepdims=True)
        dx2 = dy + _bdot_nt(dz, wg_ref[...])
        xh = xh_ref[...].astype(f32)
        dg2_ref[...] += jnp.sum(dx2 * xh, axis=0, keepdims=True)
        db2_ref[...] += jnp.sum(dx2, axis=0, keepdims=True)
        dr_ref[...] = _ln_bwd(dx2, xh, rs_ref[...], g_ref[...])

    row = jax.ShapeDtypeStruct((1, D_MODEL), f32)
    return pl.pallas_call(
        body, name="head_loss_bwd", grid=(t // tm,),
        in_specs=[_rows(tm, D_MODEL), _rows(tm, D_MODEL), _rows(tm, 1), _rows(tm, PLE), _rows(tm, D_MODEL),
                  _resident((D_MODEL, D_MODEL)), _const((1, D_MODEL)), _resident((PLE, D_MODEL)), _const((1, D_MODEL))],
        out_specs=[_rows(tm, D_MODEL), _rows(tm, D_MODEL), _rows(tm, D_MODEL), _const((1, 1)), _const((1, D_MODEL)), _const((1, D_MODEL)), _const((1, D_MODEL))],
        out_shape=[jax.ShapeDtypeStruct((t, D_MODEL), f32), jax.ShapeDtypeStruct((t, D_MODEL), ACT), jax.ShapeDtypeStruct((t, D_MODEL), ACT),
                   jax.ShapeDtypeStruct((1, 1), f32), row, row, row],
        compiler_params=_cparams(1),
    )(x2, xh2, rs2, p, tgt, w_pg, b_pg, w_pp, g2)


def _ffn_bwd(dr2, hg, up, xh1, rs1, wd, wg, wu, g1, w_out):
    t = dr2.shape[0]
    tm = _tile(t, DENSE_ROWS // 2)

    def body(dr_ref, hg_ref, up_ref, xh_ref, rs_ref, wd_ref, wg_ref, wu_ref, g_ref, wo_ref,
             dr1_ref, dhg_ref, dup_ref, dg1_ref, db1_ref, doghg_ref, dogml_ref):
        @pl.when(pl.program_id(0) == 0)
        def _():
            dg1_ref[...] = jnp.zeros_like(dg1_ref)
            db1_ref[...] = jnp.zeros_like(db1_ref)

        dr2v = dr_ref[...]
        da = _bdot_nt(dr2v, wd_ref[...])
        hgv = hg_ref[...].astype(f32)
        sg = _sigmoid(hgv)
        dhg = da * up_ref[...].astype(f32) * (sg * (1.0 + hgv * (1.0 - sg)))
        dup = da * (hgv * sg)
        dhg_ref[...] = dhg.astype(ACT)
        dup_ref[...] = dup.astype(ACT)
        dx1 = ALPHA * dr2v + _bdot(dhg, wg_ref[...]) + _bdot(dup, wu_ref[...])
        xh = xh_ref[...].astype(f32)
        dg1_ref[...] += jnp.sum(dx1 * xh, axis=0, keepdims=True)
        db1_ref[...] += jnp.sum(dx1, axis=0, keepdims=True)
        dr1 = _ln_bwd(dx1, xh, rs_ref[...], g_ref[...])
        dr1_ref[...] = dr1
        dog = _bdot_nt(dr1, wo_ref[...])
        doghg_ref[...] = dog[:, 0:MIX_W]
        dogml_ref[...] = dog[:, MIX_W:2 * MIX_W]

    row = jax.ShapeDtypeStruct((1, D_MODEL), f32)
    return pl.pallas_call(
        body, name="ffn_bwd", grid=(t // tm,),
        in_specs=[_rows(tm, D_MODEL), _rows(tm, D_FF), _rows(tm, D_FF), _rows(tm, D_MODEL), _rows(tm, 1),
                  _resident((D_FF, D_MODEL)), _resident((D_FF, D_MODEL)), _resident((D_FF, D_MODEL)), _const((1, D_MODEL)),
                  _resident((D_MODEL, D_MODEL))],
        out_specs=[_rows(tm, D_MODEL), _rows(tm, D_FF), _rows(tm, D_FF), _const((1, D_MODEL)), _const((1, D_MODEL)),
                   _rows(tm, MIX_W), _rows(tm, MIX_W)],
        out_shape=[jax.ShapeDtypeStruct((t, D_MODEL), f32), jax.ShapeDtypeStruct((t, D_FF), ACT), jax.ShapeDtypeStruct((t, D_FF), ACT), row, row,
                   jax.ShapeDtypeStruct((t, MIX_W), f32), jax.ShapeDtypeStruct((t, MIX_W), f32)],
        compiler_params=_cparams(1),
    )(dr2, hg, up, xh1, rs1, wd, wg, wu, g1, w_out)


def _inproj_bwd(dr1, du_hg, dqk, dmv, dmo, dgt, w_hg, w_ml):
    t = dr1.shape[0]
    tm = _tile(t, DENSE_ROWS)

    def body(dr_ref, dhg_ref, dqk_ref, dmv_ref, dmo_ref, dgt_ref, whg_ref, wml_ref, gx_ref, dml_ref):
        dml = jnp.concatenate([dqk_ref[...], dmv_ref[...], dmo_ref[...], dgt_ref[...]], axis=-1).astype(ACT)
        dml_ref[...] = dml
        gx_ref[...] = ALPHA * dr_ref[...] + _bdot(dhg_ref[...], whg_ref[...]) + _bdot(dml, wml_ref[...])

    return pl.pallas_call(
        body, name="inproj_bwd", grid=(t // tm,),
        in_specs=[_rows(tm, D_MODEL), _rows(tm, U_HG), _rows(tm, MIX_W), _rows(tm, MIX_W), _rows(tm, MIX_W), _rows(tm, 128),
                  _resident((U_HG, D_MODEL)), _resident((U_ML, D_MODEL))],
        out_specs=[_rows(tm, D_MODEL), _rows(tm, U_ML)],
        out_shape=[jax.ShapeDtypeStruct((t, D_MODEL), f32), jax.ShapeDtypeStruct((t, U_ML), ACT)],
        compiler_params=_cparams(1, arbitrary=False),
    )(dr1, du_hg, dqk, dmv, dmo, dgt, w_hg, w_ml)


def _wgrad(a, b, name, tk=None, tn=None, colsum=False):
    t, kdim = a.shape
    n = b.shape[1]
    tk = tk or kdim
    tn = tn or n
    tt = _tile(t, WGRAD_ROWS)
    assert not colsum or tn == n

    def body(a_ref, b_ref, o_ref, *s_ref):
        @pl.when(pl.program_id(2) == 0)
        def _():
            o_ref[...] = jnp.zeros_like(o_ref)
            if colsum:
                s_ref[0][...] = jnp.zeros_like(s_ref[0])

        av = a_ref[...]
        o_ref[...] += _bdot_tn(av, b_ref[...])
        if colsum:
            s_ref[0][...] += jnp.sum(av.astype(f32), axis=0, keepdims=True)

    out_specs = [pl.BlockSpec((tk, tn), lambda i, j, s: (i, j))]
    out_shape = [jax.ShapeDtypeStruct((kdim, n), f32)]
    if colsum:
        out_specs.append(pl.BlockSpec((1, tk), lambda i, j, s: (0, i)))
        out_shape.append(jax.ShapeDtypeStruct((1, kdim), f32))
    res = pl.pallas_call(
        body, name=name, grid=(kdim // tk, n // tn, t // tt),
        in_specs=[pl.BlockSpec((tt, tk), lambda i, j, s: (s, i)), pl.BlockSpec((tt, tn), lambda i, j, s: (s, j))],
        out_specs=out_specs, out_shape=out_shape,
        compiler_params=_cparams(3),
    )(a, b)
    return res if colsum else res[0]


def _colsum(parts, name):
    t = parts[0].shape[0]
    tt = _tile(t, 512)
    widths = [a.shape[1] for a in parts]

    def body(*refs):
        o_ref = refs[-1]

        @pl.when(pl.program_id(0) == 0)
        def _():
            o_ref[...] = jnp.zeros_like(o_ref)

        off = 0
        for r, w in zip(refs[:-1], widths):
            o_ref[:, off:off + w] += jnp.sum(r[...].astype(f32), axis=0, keepdims=True)
            off += w

    return pl.pallas_call(
        body, name=name, grid=(t // tt,),
        in_specs=[_rows(tt, w) for w in widths],
        out_specs=_const((1, sum(widths))),
        out_shape=jax.ShapeDtypeStruct((1, sum(widths)), f32),
        compiler_params=_cparams(1),
    )(*parts)


_TRANSPOSED = {"w_in", "w_ffn_gate", "w_ffn_up"}
_COL_SPLIT = {"ple_w_proj"}
_RIDE_PLAN = {"inproj": ("w_ffn_gate",), "hgrn2_fwd": ("w_ffn_up",), "mlstm_fwd": ("w_out",),
              "outproj_ln1": ("ple_w_gate", "ple_w_proj"), "ffn_up": ("w_ffn_down",)}


def _from_chip_major(a, col_split):
    if col_split:
        return a.transpose(1, 0, 2).reshape(a.shape[1], 4 * a.shape[2])
    return a.reshape(4 * a.shape[1], a.shape[2])


def _local_step(x, p, tgt, w_in_b, b_in, logits, conv_w, conv_b, hg_gn, ml_gn, w_out_b, ln1_g, ln1_b,
                wg_b, wu_b, wd_b, ln2_g, ln2_b, w_pp_b, w_pg_b, b_pg, early_hook=None, late_shards=None):
    pad_w = U_HG + U_ML - PROJ_W
    w_hg = w_in_b[:U_HG]
    w_ml = jnp.pad(w_in_b[U_HG:], ((0, pad_w), (0, 0)))
    bb_hg = b_in[:, :U_HG]
    bb_ml = jnp.pad(b_in[:, U_HG:], ((0, 0), (0, pad_w)))

    late = dict(w_out=w_out_b, w_ffn_gate=wg_b, w_ffn_up=wu_b, w_ffn_down=wd_b, ple_w_proj=w_pp_b, ple_w_gate=w_pg_b)

    def riders_of(call):
        return [late_shards[k] for k in _RIDE_PLAN[call]] if late_shards is not None else ()

    def arrived(call, got):
        for k, g in zip(_RIDE_PLAN[call], got):
            late[k] = _from_chip_major(g, k in _COL_SPLIT)

    (u_hg, u_ml, xb), got = _inproj(x, w_hg, w_ml, bb_hg, bb_ml, riders_of("inproj"))
    arrived("inproj", got)
    (og_hg, sst), got = _hgrn2_fwd(u_hg, logits, hg_gn, riders_of("hgrn2_fwd"))
    arrived("hgrn2_fwd", got)
    pre, qkc = _conv_fwd(u_ml, conv_w, conv_b)
    (og_ml, cst, nst, mst), got = _mlstm_fwd(qkc, u_ml, ml_gn, riders_of("mlstm_fwd"))
    arrived("mlstm_fwd", got)
    (x1, xh1, rs1, x1b), got = _outproj_ln1(og_hg, og_ml, x, late["w_out"], ln1_g, ln1_b, riders_of("outproj_ln1"))
    arrived("outproj_ln1", got)
    (hgp, up, act), got = _ffn_up(x1b, late["w_ffn_gate"], late["w_ffn_up"], riders_of("ffn_up"))
    arrived("ffn_up", got)
    w_out_b, wg_b, wu_b, wd_b = late["w_out"], late["w_ffn_gate"], late["w_ffn_up"], late["w_ffn_down"]
    w_pp_b, w_pg_b = late["ple_w_proj"], late["ple_w_gate"]
    x2, xh2, rs2, x2b = _ffn_down_ln2(act, x1, wd_b, ln2_g, ln2_b)
    dr2, de, dz, loss, d_bpg, d_ln2g, d_ln2b = _head_loss_bwd(x2, xh2, rs2, p, tgt, w_pg_b, b_pg, w_pp_b, ln2_g)
    dr1, dhg, dup, d_ln1g, d_ln1b, dog_hg, dog_ml = _ffn_bwd(dr2, hgp, up, xh1, rs1, wd_b, wg_b, wu_b, ln1_g, w_out_b)

    d_wo_a = _wgrad(og_hg, dr1, "wgrad_out_hg")
    d_wo_b = _wgrad(og_ml, dr1, "wgrad_out_ml")
    d_w_out = jnp.concatenate([d_wo_a, d_wo_b], axis=0)
    d_wg = _wgrad(dhg, x1b, "wgrad_ffn_gate", tk=D_FF // 2)
    d_wu = _wgrad(dup, x1b, "wgrad_ffn_up", tk=D_FF // 2)
    d_wd = _wgrad(act, dr2, "wgrad_ffn_down", tk=D_FF // 2)
    d_wpp = _wgrad(p, de, "wgrad_ple_proj")
    d_wpg = _wgrad(x2b, dz, "wgrad_ple_gate")
    early = dict(w_out=d_w_out, w_ffn_gate=d_wg, w_ffn_up=d_wu, w_ffn_down=d_wd, ple_w_proj=d_wpp, ple_w_gate=d_wpg)
    riders = early_hook(early) if early_hook is not None else ()

    res = _hgrn2_bwd(u_hg, logits, hg_gn, sst, dog_hg, riders)
    du_hg, d_logits, d_hg_gn = res[:3]
    dqkc, dmv, dmo, dgt, d_ml_gn = _mlstm_bwd(qkc, u_ml, ml_gn, cst, nst, mst, dog_ml)
    dqk, d_conv_w, d_conv_b = _conv_bwd(u_ml, conv_w, pre, dqkc)
    grad_x, du_ml = _inproj_bwd(dr1, du_hg, dqk, dmv, dmo, dgt, w_hg, w_ml)

    dw_hg, db_hg = _wgrad(du_hg, xb, "wgrad_in_hg", tk=U_HG // 2, colsum=True)
    dw_ml, db_ml = _wgrad(du_ml, xb, "wgrad_in_ml", colsum=True)
    d_w_in = jnp.concatenate([dw_hg, dw_ml[:PROJ_W - U_HG]], axis=0)
    d_b_in = jnp.concatenate([db_hg, db_ml[:, :PROJ_W - U_HG]], axis=1)

    grads = dict(w_in=d_w_in, b_in=d_b_in, hg_lb_logits=d_logits, ml_conv_w=d_conv_w, ml_conv_b=d_conv_b,
                 hg_norm_g=d_hg_gn, ml_norm_g=d_ml_gn, ln1_g=d_ln1g, ln1_b=d_ln1b, ln2_g=d_ln2g, ln2_b=d_ln2b,
                 ple_b_gate=d_bpg, **early)
    return loss, grad_x, grads, list(res[3:])


_ANY = pl.BlockSpec(memory_space=pltpu.HBM)
_MESH = pl.DeviceIdType.MESH


def _my_place():
    return lax.axis_index("x"), lax.axis_index("y"), lax.axis_index("c")


def _other_chips(x, y):
    return [(1 - x, y), (x, 1 - y), (1 - x, 1 - y)]


def _allgather_weights(shards, taps, name):
    n = len(shards)
    halves = [s.shape[0] // 2 for s in shards]

    def body(*refs):
        ins, tap_in = refs[:n], refs[n]
        outs, tap_out = refs[n + 1:2 * n + 1], refs[2 * n + 1]
        send_sems, recv_sems, local_sems = refs[2 * n + 2:]
        x, y, c = _my_place()
        me = 2 * x + y
        sibling = (x, y, 1 - c)
        chips = _other_chips(x, y)

        def ici(a, j, block_chip):
            px, py = chips[j]
            src = ins[a].at[pl.ds(pl.multiple_of(c * halves[a], 16), halves[a])] if block_chip is None else outs[a].at[block_chip, c]
            dst = outs[a].at[me if block_chip is None else block_chip, c]
            return pltpu.make_async_remote_copy(src_ref=src, dst_ref=dst, send_sem=send_sems.at[6 * a + j], recv_sem=recv_sems.at[6 * a + j],
                                                device_id=(px, py, c), device_id_type=_MESH)

        def d2d(a, j, half):
            px, py = chips[j]
            blk = outs[a].at[2 * px + py, half]
            return pltpu.make_async_remote_copy(src_ref=blk, dst_ref=blk, send_sem=send_sems.at[6 * a + 3 + j], recv_sem=recv_sems.at[6 * a + 3 + j],
                                                device_id=sibling, device_id_type=_MESH)

        local = []
        for a in range(n):
            for h in range(2):
                cp = pltpu.make_async_copy(ins[a].at[pl.ds(h * halves[a], halves[a])], outs[a].at[me, h], local_sems.at[2 * a + h])
                cp.start()
                local.append(cp)
            for j in range(3):
                ici(a, j, None).start()
        tap_local = pltpu.make_async_copy(tap_in, tap_out.at[me], local_sems.at[2 * n])
        tap_local.start()
        tap_copies = []
        for j, (px, py) in enumerate(chips):
            cp = pltpu.make_async_remote_copy(src_ref=tap_in, dst_ref=tap_out.at[me], send_sem=send_sems.at[6 * n + j], recv_sem=recv_sems.at[6 * n + j],
                                              device_id=(px, py, c), device_id_type=_MESH)
            cp.start()
            tap_copies.append(cp)
        for a in range(n):
            for j, (px, py) in enumerate(chips):
                ici(a, j, 2 * px + py).wait_recv()
                d2d(a, j, c).start()
        for a in range(n):
            for j in range(3):
                d2d(a, j, 1 - c).wait_recv()
        for a in range(n):
            for j in range(3):
                ici(a, j, None).wait_send()
                d2d(a, j, c).wait_send()
        for j, (px, py) in enumerate(chips):
            pltpu.make_async_remote_copy(src_ref=tap_in, dst_ref=tap_out.at[2 * px + py], send_sem=send_sems.at[6 * n + j], recv_sem=recv_sems.at[6 * n + j],
                                         device_id=(px, py, c), device_id_type=_MESH).wait()
        for cp in local:
            cp.wait()
        tap_local.wait()

    res = pl.pallas_call(
        body, name=name,
        in_specs=[_ANY] * (n + 1), out_specs=[_ANY] * (n + 1),
        out_shape=[jax.ShapeDtypeStruct((4, 2, s.shape[0] // 2, s.shape[1]), s.dtype) for s in shards]
        + [jax.ShapeDtypeStruct((4,) + taps.shape, taps.dtype)],
        scratch_shapes=[pltpu.SemaphoreType.DMA((6 * n + 3,)), pltpu.SemaphoreType.DMA((6 * n + 3,)), pltpu.SemaphoreType.DMA((2 * n + 1,))],
    )(*shards, taps)
    return [w.reshape((4,) + s.shape) for w, s in zip(res[:n], shards)], res[n]


def _swap_halves(pieces, name):
    n = len(pieces)
    halves = [p.shape[1] // 2 for p in pieces]

    def body(*refs):
        ins, own, other = refs[:n], refs[n:2 * n], refs[2 * n:3 * n]
        send_sems, recv_sems, local_sems = refs[3 * n:]
        x, y, c = _my_place()

        def half_of(a, which):
            return ins[a].at[pl.ds(0, 4), pl.ds(pl.multiple_of(which * halves[a], 16), halves[a])]

        def to_sibling(a):
            return pltpu.make_async_remote_copy(src_ref=half_of(a, 1 - c), dst_ref=other[a], send_sem=send_sems.at[a], recv_sem=recv_sems.at[a],
                                                device_id=(x, y, 1 - c), device_id_type=_MESH)

        local = []
        for a in range(n):
            cp = pltpu.make_async_copy(half_of(a, c), own[a], local_sems.at[a])
            cp.start()
            local.append(cp)
            to_sibling(a).start()
        for a in range(n):
            to_sibling(a).wait()
            local[a].wait()

    shapes = [jax.ShapeDtypeStruct((4, p.shape[1] // 2, p.shape[2]), p.dtype) for p in pieces]
    res = pl.pallas_call(
        body, name=name,
        in_specs=[_ANY] * n, out_specs=[_ANY] * (2 * n), out_shape=shapes + shapes,
        scratch_shapes=[pltpu.SemaphoreType.DMA((n,)), pltpu.SemaphoreType.DMA((n,)), pltpu.SemaphoreType.DMA((n,))],
    )(*pieces)
    return res[:n], res[n:]


_VMEM = pl.BlockSpec(memory_space=pltpu.VMEM)
_EX_ROWS = 32


def _pair_reduce(p, name):
    s, r, c = p.shape
    half = r // 2

    def body(p_ref, o_ref, other, send_sem, recv_sem):
        x, y, cc = _my_place()
        theirs = pl.multiple_of((1 - cc) * half, 16)
        mine = pl.multiple_of(cc * half, 16)
        cp = pltpu.make_async_remote_copy(src_ref=p_ref.at[pl.ds(0, s), pl.ds(theirs, half)], dst_ref=other, send_sem=send_sem, recv_sem=recv_sem,
                                          device_id=(x, y, 1 - cc), device_id_type=_MESH)
        cp.start()
        cp.wait()

        def step(i, carry):
            r0 = pl.multiple_of(i * _EX_ROWS, _EX_ROWS)
            for slot in range(s):
                own_rows = pl.ds(pl.multiple_of(mine + r0, 16), _EX_ROWS)
                o_ref[slot, pl.ds(r0, _EX_ROWS), :] = (p_ref[slot, own_rows, :] + other[slot, pl.ds(r0, _EX_ROWS), :]).astype(bf16)
            return carry

        lax.fori_loop(0, half // _EX_ROWS, step, 0)

    return pl.pallas_call(
        body, name=name, in_specs=[_VMEM], out_specs=_VMEM,
        out_shape=jax.ShapeDtypeStruct((s, half, c), bf16),
        scratch_shapes=[pltpu.VMEM((s, half, c), f32), pltpu.SemaphoreType.DMA, pltpu.SemaphoreType.DMA],
        compiler_params=pltpu.CompilerParams(vmem_limit_bytes=VMEM_LIMIT),
    )(p)


def _chip_reduce_swap(rcv, name):
    s, h, c = rcv.shape

    def body(r_ref, g_ref, send_sem, recv_sem):
        x, y, cc = _my_place()

        def step(i, carry):
            r0 = pl.multiple_of(i * _EX_ROWS, _EX_ROWS)
            acc = r_ref[0, pl.ds(r0, _EX_ROWS), :].astype(f32)
            for slot in range(1, s):
                acc = acc + r_ref[slot, pl.ds(r0, _EX_ROWS), :].astype(f32)
            g_ref[cc, pl.ds(r0, _EX_ROWS), :] = acc
            return carry

        lax.fori_loop(0, h // _EX_ROWS, step, 0)
        cp = pltpu.make_async_remote_copy(src_ref=g_ref.at[cc], dst_ref=g_ref.at[cc], send_sem=send_sem, recv_sem=recv_sem,
                                          device_id=(x, y, 1 - cc), device_id_type=_MESH)
        cp.start()
        cp.wait()

    return pl.pallas_call(
        body, name=name, in_specs=[_VMEM], out_specs=_VMEM,
        out_shape=jax.ShapeDtypeStruct((2, h, c), f32),
        scratch_shapes=[pltpu.SemaphoreType.DMA, pltpu.SemaphoreType.DMA],
        compiler_params=pltpu.CompilerParams(vmem_limit_bytes=VMEM_LIMIT),
    )(rcv)


def _pair_reduce_cols(p, name):
    s, r, c = p.shape
    hc = c // 2

    def body(p_ref, o_ref, other, send_sem, recv_sem):
        x, y, cc = _my_place()

        def run(mine_lo, theirs_lo):
            cp = pltpu.make_async_remote_copy(src_ref=p_ref.at[pl.ds(0, s), pl.ds(0, r), pl.ds(theirs_lo, hc)], dst_ref=other,
                                              send_sem=send_sem, recv_sem=recv_sem, device_id=(x, y, 1 - cc), device_id_type=_MESH)
            cp.start()
            cp.wait()
            for slot in range(s):
                o_ref[slot] = (p_ref[slot, :, mine_lo:mine_lo + hc] + other[slot]).astype(bf16)

        @pl.when(cc == 0)
        def _():
            run(0, hc)

        @pl.when(cc == 1)
        def _():
            run(hc, 0)

    return pl.pallas_call(
        body, name=name, in_specs=[_VMEM], out_specs=_VMEM,
        out_shape=jax.ShapeDtypeStruct((s, r, hc), bf16),
        scratch_shapes=[pltpu.VMEM((s, r, hc), f32), pltpu.SemaphoreType.DMA, pltpu.SemaphoreType.DMA],
        compiler_params=pltpu.CompilerParams(vmem_limit_bytes=VMEM_LIMIT),
    )(p)


def _chip_reduce_swap_cols(rcv, name):
    s, r, hc = rcv.shape

    def body(r_ref, g_ref, send_sem, recv_sem):
        x, y, cc = _my_place()
        acc = r_ref[0].astype(f32)
        for slot in range(1, s):
            acc = acc + r_ref[slot].astype(f32)
        g_ref[cc] = acc
        cp = pltpu.make_async_remote_copy(src_ref=g_ref.at[cc], dst_ref=g_ref.at[cc], send_sem=send_sem, recv_sem=recv_sem,
                                          device_id=(x, y, 1 - cc), device_id_type=_MESH)
        cp.start()
        cp.wait()

    both = pl.pallas_call(
        body, name=name, in_specs=[_VMEM], out_specs=_VMEM,
        out_shape=jax.ShapeDtypeStruct((2, r, hc), f32),
        scratch_shapes=[pltpu.SemaphoreType.DMA, pltpu.SemaphoreType.DMA],
        compiler_params=pltpu.CompilerParams(vmem_limit_bytes=VMEM_LIMIT),
    )(rcv)
    return both.transpose(1, 0, 2).reshape(r, 2 * hc)


def _add_cast(a, b, name):
    s, r, c = a.shape
    tr = _row_tile(r, c)

    def body(a_ref, b_ref, o_ref):
        o_ref[...] = (a_ref[...] + b_ref[...]).astype(bf16)

    blk = pl.BlockSpec((1, tr, c), lambda i, j: (i, j, 0))
    return pl.pallas_call(
        body, name=name, grid=(s, r // tr), in_specs=[blk, blk], out_specs=blk,
        out_shape=jax.ShapeDtypeStruct(a.shape, bf16),
        compiler_params=_cparams(2, arbitrary=False),
    )(a, b)


def _gather_copies(ins, outs, send_sems, recv_sems, local_sems):
    x, y, c = _my_place()
    me = 2 * x + y
    local, outgoing, incoming = [], [], []
    for a in range(len(ins)):
        local.append(pltpu.make_async_copy(ins[a], outs[a].at[me], local_sems.at[a]))
        for j, (px, py) in enumerate(_other_chips(x, y)):
            sems = dict(send_sem=send_sems.at[3 * a + j], recv_sem=recv_sems.at[3 * a + j], device_id=(px, py, c), device_id_type=_MESH)
            outgoing.append(pltpu.make_async_remote_copy(src_ref=ins[a], dst_ref=outs[a].at[me], **sems))
            incoming.append(pltpu.make_async_remote_copy(src_ref=ins[a], dst_ref=outs[a].at[2 * px + py], **sems))
    return local, outgoing, incoming


def _gather_chips(blocks, name):
    n = len(blocks)

    def body(*refs):
        local, outgoing, incoming = _gather_copies(refs[:n], refs[n:2 * n], *refs[2 * n:])
        for cp in local + outgoing:
            cp.start()
        for cp in incoming:
            cp.wait_recv()
        for cp in outgoing:
            cp.wait_send()
        for cp in local:
            cp.wait()

    return pl.pallas_call(
        body, name=name, in_specs=[_ANY] * n, out_specs=[_ANY] * n, out_shape=_gather_shapes(blocks),
        scratch_shapes=[pltpu.SemaphoreType.DMA((3 * n,)), pltpu.SemaphoreType.DMA((3 * n,)), pltpu.SemaphoreType.DMA((n,))],
    )(*blocks)


def _riding_call(body, name, nsteps, in_specs, out_specs, out_shape, scratch_shapes, operands, riders, copies, ride_shapes):
    nr, n_in, n_out, n_scr = len(riders), len(in_specs), len(out_specs), len(scratch_shapes)

    def wrapped(*refs):
        ins, ride_in = refs[:n_in], refs[n_in:n_in + nr]
        outs, ride_out = refs[n_in + nr:n_in + nr + n_out], refs[n_in + nr + n_out:n_in + 2 * nr + n_out]
        scratch, sems = refs[n_in + 2 * nr + n_out:n_in + 2 * nr + n_out + n_scr], refs[n_in + 2 * nr + n_out + n_scr:]
        if nr:
            @pl.when(pl.program_id(0) == 0)
            def _():
                local, outgoing, _ = copies(ride_in, ride_out, *sems)
                for cp in local + outgoing:
                    cp.start()

        body(*ins, *outs, *scratch)
        if nr:
            @pl.when(pl.program_id(0) == nsteps - 1)
            def _():
                local, outgoing, incoming = copies(ride_in, ride_out, *sems)
                for cp in incoming:
                    cp.wait_recv()
                for cp in outgoing:
                    cp.wait_send()
                for cp in local:
                    cp.wait()

    hbm = pl.BlockSpec(memory_space=pltpu.HBM)
    sems = [pltpu.SemaphoreType.DMA((3 * nr,)), pltpu.SemaphoreType.DMA((3 * nr,)), pltpu.SemaphoreType.DMA((nr,))] if nr else []
    res = pl.pallas_call(
        wrapped, name=name, grid=(nsteps,),
        in_specs=list(in_specs) + [hbm] * nr, out_specs=list(out_specs) + [hbm] * nr,
        out_shape=list(out_shape) + list(ride_shapes),
        scratch_shapes=list(scratch_shapes) + sems,
        compiler_params=_cparams(1),
    )(*operands, *riders)
    return list(res[:n_out]), list(res[n_out:])


def _gather_shapes(riders):
    return [jax.ShapeDtypeStruct((4,) + r.shape, r.dtype) for r in riders]


def _scatter_copies(ins, outs, send_sems, recv_sems, local_sems):
    x, y, c = _my_place()
    me = 2 * x + y
    local, outgoing, incoming = [], [], []
    for a in range(len(ins)):
        local.append(pltpu.make_async_copy(ins[a].at[me], outs[a].at[me], local_sems.at[a]))
        for j, (px, py) in enumerate(_other_chips(x, y)):
            sems = dict(send_sem=send_sems.at[3 * a + j], recv_sem=recv_sems.at[3 * a + j], device_id=(px, py, c), device_id_type=_MESH)
            outgoing.append(pltpu.make_async_remote_copy(src_ref=ins[a].at[2 * px + py], dst_ref=outs[a].at[me], **sems))
            incoming.append(pltpu.make_async_remote_copy(src_ref=ins[a].at[2 * px + py], dst_ref=outs[a].at[2 * px + py], **sems))
    return local, outgoing, incoming


def _scatter_start(ins, outs, send_sems, recv_sems, local_sems):
    local, outgoing, _ = _scatter_copies(ins, outs, send_sems, recv_sems, local_sems)
    for cp in local + outgoing:
        cp.start()


def _scatter_wait(ins, outs, send_sems, recv_sems, local_sems):
    local, outgoing, incoming = _scatter_copies(ins, outs, send_sems, recv_sems, local_sems)
    for cp in incoming:
        cp.wait_recv()
    for cp in outgoing:
        cp.wait_send()
    for cp in local:
        cp.wait()


def _scatter_chips(pieces, name):
    n = len(pieces)

    def body(*refs):
        ins, outs = refs[:n], refs[n:2 * n]
        _scatter_start(ins, outs, *refs[2 * n:])
        _scatter_wait(ins, outs, *refs[2 * n:])

    return pl.pallas_call(
        body, name=name,
        in_specs=[_ANY] * n, out_specs=[_ANY] * n,
        out_shape=[jax.ShapeDtypeStruct(s.shape, s.dtype) for s in pieces],
        scratch_shapes=[pltpu.SemaphoreType.DMA((3 * n,)), pltpu.SemaphoreType.DMA((3 * n,)), pltpu.SemaphoreType.DMA((n,))],
    )(*pieces)


def _swap_cores(blocks, name):
    n = len(blocks)
    parts = 4
    rows = [b.shape[0] // parts for b in blocks]

    def body(*refs):
        ins, outs = refs[:n], refs[n:2 * n]
        send_sems, recv_sems, local_sems = refs[2 * n:]
        x, y, c = _my_place()

        def remote(a, k, slot):
            rs = pl.ds(k * rows[a], rows[a])
            return pltpu.make_async_remote_copy(src_ref=ins[a].at[rs], dst_ref=outs[a].at[slot, rs], send_sem=send_sems.at[parts * a + k],
                                                recv_sem=recv_sems.at[parts * a + k], device_id=(x, y, 1 - c), device_id_type=_MESH)

        local = []
        for a in range(n):
            cp = pltpu.make_async_copy(ins[a], outs[a].at[c], local_sems.at[a])
            cp.start()
            local.append(cp)
            for k in range(parts):
                remote(a, k, c).start()
        for a in range(n):
            for k in range(parts):
                remote(a, k, 1 - c).wait()
            local[a].wait()

    return pl.pallas_call(
        body, name=name,
        in_specs=[_ANY] * n, out_specs=[_ANY] * n,
        out_shape=[jax.ShapeDtypeStruct((2,) + s.shape, s.dtype) for s in blocks],
        scratch_shapes=[pltpu.SemaphoreType.DMA((parts * n,)), pltpu.SemaphoreType.DMA((parts * n,)), pltpu.SemaphoreType.DMA((n,))],
    )(*blocks)


def _gather_all(block, name):
    def body(in_ref, out_ref, send_sems, recv_sems, local_sem):
        x, y, c = _my_place()
        me = 4 * x + 2 * y + c
        cp = pltpu.make_async_copy(in_ref, out_ref.at[me], local_sem)
        cp.start()
        peers = []
        for dx in range(2):
            for dy in range(2):
                for dc in range(2):
                    if dx or dy or dc:
                        peers.append((1 - x if dx else x, 1 - y if dy else y, 1 - c if dc else c))
        for j, pr in enumerate(peers):
            pltpu.make_async_remote_copy(src_ref=in_ref, dst_ref=out_ref.at[me], send_sem=send_sems.at[j], recv_sem=recv_sems.at[j],
                                         device_id=pr, device_id_type=_MESH).start()
        for j, (px, py, pc) in enumerate(peers):
            pltpu.make_async_remote_copy(src_ref=in_ref, dst_ref=out_ref.at[4 * px + 2 * py + pc], send_sem=send_sems.at[j], recv_sem=recv_sems.at[j],
                                         device_id=(px, py, pc), device_id_type=_MESH).wait()
        cp.wait()

    return pl.pallas_call(
        body, name=name,
        in_specs=[_ANY], out_specs=_ANY,
        out_shape=jax.ShapeDtypeStruct((8,) + block.shape, block.dtype),
        scratch_shapes=[pltpu.SemaphoreType.DMA((7,)), pltpu.SemaphoreType.DMA((7,)), pltpu.SemaphoreType.DMA],
    )(block)


def _row_tile(r, c):
    best = r
    for cand in range(16, r + 1, 16):
        if r % cand == 0 and cand * c * 4 <= (1 << 20):
            best = cand
    return best if best * c * 4 <= (4 << 20) else r


def _sum_slots(parts, name):
    n, r, c = parts.shape
    tr = _row_tile(r, c)

    def body(p_ref, o_ref):
        acc = p_ref[0].astype(f32)
        for s in range(1, n):
            acc = acc + p_ref[s].astype(f32)
        o_ref[...] = acc

    return pl.pallas_call(
        body, name=name, grid=(r // tr,),
        in_specs=[pl.BlockSpec((n, tr, c), lambda i: (0, i, 0))],
        out_specs=pl.BlockSpec((tr, c), lambda i: (i, 0)),
        out_shape=jax.ShapeDtypeStruct((r, c), f32),
        compiler_params=_cparams(1, arbitrary=False),
    )(parts)


def _adamw(parts, w, m, v, name):
    n, r, c = parts.shape
    tr = _row_tile(r, c)
    tc = c
    if tr == r and r * c * 4 > (1 << 20) and c % 256 == 0:
        tc = 256

    def body(p_ref, w_ref, m_ref, v_ref, g_ref, d_ref, nm_ref, nv_ref):
        g = p_ref[0]
        for s in range(1, n):
            g = g + p_ref[s]
        nm = B1 * m_ref[...] + (1.0 - B1) * g
        nv = B2 * v_ref[...] + (1.0 - B2) * (g * g)
        m_hat = nm / (1.0 - B1 ** STEP)
        v_hat = nv / (1.0 - B2 ** STEP)
        g_ref[...] = g
        nm_ref[...] = nm
        nv_ref[...] = nv
        d_ref[...] = -LR * (m_hat / (jnp.sqrt(v_hat) + EPS_ADAM) + WD * w_ref[...])

    blk = pl.BlockSpec((tr, tc), lambda i, j: (i, j))
    return pl.pallas_call(
        body, name=name, grid=(r // tr, c // tc),
        in_specs=[pl.BlockSpec((n, tr, tc), lambda i, j: (0, i, j)), blk, blk, blk],
        out_specs=[blk] * 4,
        out_shape=[jax.ShapeDtypeStruct((r, c), f32)] * 4,
        compiler_params=_cparams(2, arbitrary=False),
    )(parts, w, m, v)


_BIG = ["w_in", "w_out", "w_ffn_gate", "w_ffn_up", "w_ffn_down", "ple_w_proj", "ple_w_gate"]
_SMALL = ["b_in", "hg_lb_logits", "ml_conv_w", "ml_conv_b", "hg_norm_g", "ml_norm_g", "ln1_g", "ln1_b", "ln2_g", "ln2_b", "ple_b_gate"]
_ORDER = ["w_in", "b_in", "hg_lb_logits", "ml_conv_w", "ml_conv_b", "hg_norm_g", "ml_norm_g", "w_out", "ln1_g", "ln1_b",
          "w_ffn_gate", "w_ffn_up", "w_ffn_down", "ln2_g", "ln2_b", "ple_w_proj", "ple_w_gate", "ple_b_gate"]
_PACK_ROWS, _PACK_COLS = 16, 1024


def _pack(arrays):
    flat = jnp.concatenate([a.reshape(-1) for a in arrays])
    return jnp.pad(flat, (0, _PACK_ROWS * _PACK_COLS - flat.shape[0])).reshape(_PACK_ROWS, _PACK_COLS)


def _unpack(pack, shapes):
    flat = pack.reshape(-1)
    out, off = [], 0
    for s in shapes:
        size = 1
        for d in s:
            size *= d
        out.append(flat[off:off + size].reshape(s))
        off += size
    return out


def _to_chip_major(g, col_split):
    if col_split:
        k, n = g.shape
        return g.reshape(k, 4, n // 4).transpose(1, 0, 2)
    k, n = g.shape
    return g.reshape(4, k // 4, n)


def kernel(x, p, w_in, b_in, hg_lb_logits, ml_conv_w, ml_conv_b, hg_norm_g, ml_norm_g, w_out, ln1_g, ln1_b, w_ffn_gate, w_ffn_up, w_ffn_down, ln2_g, ln2_b, ple_w_proj, ple_w_gate, ple_b_gate, loss_target, m_w_in, m_b_in, m_hg_lb_logits, m_ml_conv_w, m_ml_conv_b, m_hg_norm_g, m_ml_norm_g, m_w_out, m_ln1_g, m_ln1_b, m_w_ffn_gate, m_w_ffn_up, m_w_ffn_down, m_ln2_g, m_ln2_b, m_ple_w_proj, m_ple_w_gate, m_ple_b_gate, v_w_in, v_b_in, v_hg_lb_logits, v_ml_conv_w, v_ml_conv_b, v_hg_norm_g, v_ml_norm_g, v_w_out, v_ln1_g, v_ln1_b, v_w_ffn_gate, v_w_ffn_up, v_w_ffn_down, v_ln2_g, v_ln2_b, v_ple_w_proj, v_ple_w_gate, v_ple_b_gate):
    args = dict(locals())
    wts = {k: args[k] for k in _ORDER}
    mom = {k: args["m_" + k] for k in _ORDER}
    var = {k: args["v_" + k] for k in _ORDER}
    two_d = lambda a: a.reshape(a.shape[-2], a.shape[-1])
    block = lambda k, a: jnp.swapaxes(two_d(a), 0, 1) if k in _TRANSPOSED else two_d(a)
    unblock = lambda k, a: (jnp.swapaxes(a, 0, 1) if k in _TRANSPOSED else a).reshape(wts[k].shape)

    shards = {k: block(k, wts[k]).astype(bf16) for k in _BIG}
    w_in_blocks, taps = _gather_chips([shards["w_in"], two_d(ml_conv_w)], "gather_w_in")
    w_in_full = _from_chip_major(w_in_blocks, False)
    conv_w_full = _from_chip_major(taps, True)

    def core_sum(k, g):
        pieces = _to_chip_major(g, k in _COL_SPLIT)
        if pieces.shape[1] % (2 * _EX_ROWS):
            return _pair_reduce_cols(pieces, "pair_reduce_" + k)
        return _pair_reduce(pieces, "pair_reduce_" + k)

    early_keys = _BIG[1:]
    loss, grad_x, grads, early_received = _local_step(
        x[0], p[0, 0], loss_target[0], w_in_full, b_in, hg_lb_logits, conv_w_full, ml_conv_b, hg_norm_g, ml_norm_g,
        None, ln1_g, ln1_b, None, None, None, ln2_g, ln2_b, None, None, ple_b_gate,
        early_hook=lambda early: [core_sum(k, early[k]) for k in early_keys],
        late_shards={k: shards[k] for k in early_keys})

    received = list(_scatter_chips([core_sum("w_in", grads["w_in"])], "scatter_grad_w_in")) + list(early_received)
    out_g, out_d, out_m, out_v = {}, {}, {}, {}
    for k, rcv in zip(_BIG, received):
        own = block(k, wts[k])
        if rcv.shape[1] == own.shape[0]:
            whole = _chip_reduce_swap_cols(rcv, "chip_reduce_" + k)
        else:
            parts = _chip_reduce_swap(rcv, "chip_reduce_" + k)
            whole = parts.reshape(2 * parts.shape[1], parts.shape[2])
        g, d, nm, nv = _adamw(whole[None], own, block(k, mom[k]), block(k, var[k]), "adamw_" + k)
        out_g[k], out_d[k], out_m[k], out_v[k] = unblock(k, g), unblock(k, d), unblock(k, nm), unblock(k, nv)

    small_shapes = [(1, PROJ_W), (2, MIX_W), (CONV_K, MIX_W)] + [(1, MIX_W)] * 3 + [(1, D_MODEL)] * 5 + [(1, 1)]
    contrib = _pack([grads[k] for k in _SMALL] + [loss])
    summed = _sum_slots(_gather_all(contrib, "gather_small"), "sum_small")
    small = _unpack(summed, small_shapes)
    loss_total = small[-1].reshape(())
    gsm = dict(zip(_SMALL, small[:-1]))
    place = 2 * lax.axis_index("x") + lax.axis_index("y")
    conv_cols = ml_conv_w.shape[-1]
    gsm["ml_conv_w"] = lax.dynamic_slice(gsm["ml_conv_w"], (0, place * conv_cols), (CONV_K, conv_cols))
    own_shapes = [wts[k].shape for k in _SMALL]
    g_pack = _pack([gsm[k] for k in _SMALL])
    res = _adamw(g_pack[None], _pack([wts[k] for k in _SMALL]), _pack([mom[k] for k in _SMALL]), _pack([var[k] for k in _SMALL]), "adamw_small")
    for dst, pack in zip((out_g, out_d, out_m, out_v), res):
        for k, a in zip(_SMALL, _unpack(pack, own_shapes)):
            dst[k] = a

    outs = [loss_total, grad_x[None]]
    for group in (out_g, out_d, out_m, out_v):
        outs += [group[k] for k in _ORDER]
    return tuple(outs)
```

```python
import functools

import jax
import jax.numpy as jnp
from jax import lax
from jax.experimental import pallas as pl
from jax.experimental.pallas import tpu as pltpu

f32 = jnp.float32
bf16 = jnp.bfloat16
HI = lax.Precision.HIGHEST

D_MODEL = 1024
HEADS = 4
HEAD_W = 128
MIX_W = HEADS * HEAD_W
ML_DQK = 64
PROJ_W = 3592
U_HG = 4 * MIX_W
U_ML = 3 * MIX_W + 128
D_FF = 2816
PLE = 256
CHUNK = 128
SUB = 16
EXP_CAP = 80.0
CONV_K = 4
HALO = 8
ALPHA = float(2.0 ** 0.25)
LN_EPS = 1e-5
RMS_EPS = 1e-6
NEG = -1e30
LR, B1, B2, EPS_ADAM, WD, STEP = 0.001, 0.9, 0.999, 1e-08, 0.01, 10
VMEM_LIMIT = 56 * 1024 * 1024
DENSE_ROWS = 512
WGRAD_ROWS = 2048


def _cparams(n_axes, arbitrary=True):
    sem = ("arbitrary",) * n_axes if arbitrary else ("parallel",) * n_axes
    return pltpu.CompilerParams(dimension_semantics=sem, vmem_limit_bytes=VMEM_LIMIT)


ACT = bf16


def _mx(a):
    return a.astype(ACT)


def _bdot(a, b):
    return jnp.dot(_mx(a), _mx(b), preferred_element_type=f32)


def _bdot_nt(a, b):
    return lax.dot_general(_mx(a), _mx(b), (((1,), (1,)), ((), ())), preferred_element_type=f32)


def _bdot_tn(a, b):
    return lax.dot_general(_mx(a), _mx(b), (((0,), (0,)), ((), ())), preferred_element_type=f32)


def _split3(x):
    hi = x.astype(bf16)
    r1 = x - hi.astype(f32)
    mid = r1.astype(bf16)
    lo = (r1 - mid.astype(f32)).astype(bf16)
    return hi, mid, lo


def _dot3(a, b, dims):
    a_hi = a.astype(bf16)
    a_lo = (a - a_hi.astype(f32)).astype(bf16)
    b_hi = b.astype(bf16)
    b_lo = (b - b_hi.astype(f32)).astype(bf16)
    dn = (dims, ((), ()))
    return (lax.dot_general(a_hi, b_hi, dn, preferred_element_type=f32) + lax.dot_general(a_hi, b_lo, dn, preferred_element_type=f32)
            + lax.dot_general(a_lo, b_hi, dn, preferred_element_type=f32))


def _lane_sum(x):
    hi = x.astype(bf16)
    lo = (x - hi.astype(f32)).astype(bf16)
    ones = jnp.ones((x.shape[1], 128), bf16)
    return jnp.dot(hi, ones, preferred_element_type=f32) + jnp.dot(lo, ones, preferred_element_type=f32)


def _lane_dot(x, row):
    return _dot3(x, jnp.broadcast_to(row, (128, row.shape[1])), ((1,), (1,)))


def _sel_dot(sel, x):
    sb = sel.astype(bf16)
    return sum(jnp.dot(sb, part, preferred_element_type=f32) for part in _split3(x))


def _sel_dot_nt(sel, x):
    sb = sel.astype(bf16)
    return sum(lax.dot_general(sb, part, (((1,), (1,)), ((), ())), preferred_element_type=f32) for part in _split3(x))


def _sigmoid(x):
    return 1.0 / (1.0 + jnp.exp(-x))


def _log_sigmoid(x):
    return jnp.minimum(x, 0.0) - jnp.log(1.0 + jnp.exp(-jnp.abs(x)))


def _tri(n, upper=False):
    r = lax.broadcasted_iota(jnp.int32, (n, n), 0)
    c = lax.broadcasted_iota(jnp.int32, (n, n), 1)
    return (c >= r) if upper else (c <= r)


def _rows(tm, n, col=0):
    return pl.BlockSpec((tm, n), lambda i, _c=col: (i, _c))


def _rows_rev(tm, n, nb, col=0):
    return pl.BlockSpec((tm, n), lambda i, _c=col, _nb=nb: (_nb - 1 - i, _c))


def _const(shape):
    return pl.BlockSpec(shape, lambda i, _n=len(shape): (0,) * _n)


def _resident(shape):
    return pl.BlockSpec(shape, lambda i, _n=len(shape): (0,) * _n, pipeline_mode=pl.Buffered(1))


def _tile(t, want):
    return want if t % want == 0 else t


def _inproj(x, w_hg, w_ml, b_hg, b_ml, riders=()):
    t = x.shape[0]
    tm = _tile(t, DENSE_ROWS)

    def body(x_ref, whg_ref, wml_ref, bhg_ref, bml_ref, uhg_ref, uml_ref, xb_ref):
        xb = _mx(x_ref[...])
        xb_ref[...] = xb
        uhg_ref[...] = _bdot_nt(xb, whg_ref[...]) + bhg_ref[...]
        uml_ref[...] = _bdot_nt(xb, wml_ref[...]) + bml_ref[...]

    return _riding_call(
        body, "inproj", t // tm,
        in_specs=[_rows(tm, D_MODEL), _resident((U_HG, D_MODEL)), _resident((U_ML, D_MODEL)), _const((1, U_HG)), _const((1, U_ML))],
        out_specs=[_rows(tm, U_HG), _rows(tm, U_ML), _rows(tm, D_MODEL)],
        out_shape=[jax.ShapeDtypeStruct((t, U_HG), f32), jax.ShapeDtypeStruct((t, U_ML), f32), jax.ShapeDtypeStruct((t, D_MODEL), ACT)],
        scratch_shapes=[], operands=(x, w_hg, w_ml, b_hg, b_ml), riders=riders, copies=_gather_copies, ride_shapes=_gather_shapes(riders))


def _hg_gates(hq, hf, lb, tri):
    s = _sigmoid(hf)
    om = 1.0 - lb
    f = lb + om * s
    g = jnp.log(f)
    k = om * (1.0 - s)
    sq = _sigmoid(hq)
    q = hq * sq
    b = _sel_dot(tri, g)
    return q, sq, s, f, k, b


def _hg_scores(q, k, b, tril_mask):
    qts, kts, eqs, eks, rows = [], [], [], [], []
    for i in range(CHUNK // SUB):
        lo = i * SUB
        ref = jnp.zeros_like(b[0:1]) if i == 0 else b[lo - 1:lo]
        eq = jnp.exp(b[lo:lo + SUB] - ref)
        ek = jnp.exp(jnp.minimum(ref - b, EXP_CAP))
        qt = q[lo:lo + SUB] * eq
        kt = k * ek
        rows.append(_bdot_nt(qt, kt))
        qts.append(qt); kts.append(kt); eqs.append(eq); eks.append(ek)
    a = jnp.where(tril_mask, jnp.concatenate(rows, axis=0), 0.0)
    return a, qts, kts, eqs, eks


def _head_rms(o, gn, on_mxu=False):
    ms = _lane_sum(o * o) * (1.0 / o.shape[1]) if on_mxu else jnp.mean(o * o, axis=-1, keepdims=True)
    rstd = lax.rsqrt(ms + RMS_EPS)
    oh = o * rstd
    return oh, rstd, oh * gn


def _lower_bound(logit_ref):
    lg = logit_ref[...]
    return _sigmoid(lg[0:1] - lg[1:2])


def _hgrn2_fwd(u_hg, logits, gn, riders=()):
    t = u_hg.shape[0]
    tb = _tile(t, 256)
    nc_blk = tb // CHUNK

    def body(u_ref, lg_ref, gn_ref, og_ref, sst_ref, st_ref):
        @pl.when(pl.program_id(0) == 0)
        def _():
            st_ref[...] = jnp.zeros_like(st_ref)

        lb_all = _lower_bound(lg_ref)
        tril_mask = _tri(CHUNK)
        tri = tril_mask.astype(f32)

        def chunk(c, carry):
            r0 = pl.multiple_of(c * CHUNK, CHUNK)
            rows = pl.ds(r0, CHUNK)
            heads = range(HEADS)
            cols = [slice(h * HEAD_W, (h + 1) * HEAD_W) for h in heads]
            hv = [u_ref[rows, 2 * MIX_W + h * HEAD_W:2 * MIX_W + (h + 1) * HEAD_W] for h in heads]
            gts = [_hg_gates(u_ref[rows, h * HEAD_W:(h + 1) * HEAD_W], u_ref[rows, MIX_W + h * HEAD_W:MIX_W + (h + 1) * HEAD_W],
                             lb_all[:, cols[h]], tri) for h in heads]
            q = [g[0] for g in gts]
            k = [g[4] for g in gts]
            b = [g[5] for g in gts]
            a = [_hg_scores(q[h], k[h], b[h], tril_mask)[0] for h in heads]
            st = [st_ref[h] for h in heads]
            bl = [b[h][CHUNK - 1:CHUNK] for h in heads]
            o = [_bdot(a[h], hv[h]) + _bdot_nt(q[h] * jnp.exp(b[h]), st[h]) for h in heads]
            new_st = [st[h] * jnp.exp(bl[h]) + _bdot_tn(hv[h], k[h] * jnp.exp(bl[h] - b[h])) for h in heads]
            for h in heads:
                sst_ref[c, h] = st[h]
                st_ref[h] = new_st[h]
                hgate = u_ref[rows, 3 * MIX_W + h * HEAD_W:3 * MIX_W + (h + 1) * HEAD_W]
                _, _, y = _head_rms(o[h], gn_ref[:, cols[h]])
                og_ref[rows, cols[h]] = (y * (hgate * _sigmoid(hgate))).astype(ACT)
            return carry

        lax.fori_loop(0, nc_blk, chunk, 0, unroll=True)

    return _riding_call(
        body, "hgrn2_fwd", t // tb,
        in_specs=[_rows(tb, U_HG), _const((2, MIX_W)), _const((1, MIX_W))],
        out_specs=[_rows(tb, MIX_W), pl.BlockSpec((nc_blk, HEADS, HEAD_W, HEAD_W), lambda i: (i, 0, 0, 0))],
        out_shape=[jax.ShapeDtypeStruct((t, MIX_W), ACT), jax.ShapeDtypeStruct((t // CHUNK, HEADS, HEAD_W, HEAD_W), f32)],
        scratch_shapes=[pltpu.VMEM((HEADS, HEAD_W, HEAD_W), f32)],
        operands=(u_hg, logits, gn), riders=riders, copies=_gather_copies, ride_shapes=_gather_shapes(riders))


def _hgrn2_bwd(u_hg, logits, gn, sst, dog, riders=()):
    t = u_hg.shape[0]
    tb = _tile(t, 256)
    nb = t // tb
    nc_blk = tb // CHUNK
    nr = len(riders)

    def body(*refs):
        u_ref, lg_ref, gn_ref, sst_ref, dog_ref = refs[:5]
        ride_in = refs[5:5 + nr]
        du_ref, dlg_ref, dgn_ref = refs[5 + nr:8 + nr]
        ride_out = refs[8 + nr:8 + 2 * nr]
        dst_ref = refs[8 + 2 * nr]
        ride_sems = refs[9 + 2 * nr:]

        @pl.when(pl.program_id(0) == 0)
        def _():
            dst_ref[...] = jnp.zeros_like(dst_ref)
            dlg_ref[...] = jnp.zeros_like(dlg_ref)
            dgn_ref[...] = jnp.zeros_like(dgn_ref)
            if nr:
                _scatter_start(ride_in, ride_out, *ride_sems)

        lb_all = _lower_bound(lg_ref)
        tril_mask = _tri(CHUNK)
        tri = tril_mask.astype(f32)
        triu = _tri(CHUNK, upper=True).astype(f32)

        def chunk(j, carry):
            c = nc_blk - 1 - j
            r0 = pl.multiple_of(c * CHUNK, CHUNK)
            rows = pl.ds(r0, CHUNK)
            heads = range(HEADS)
            nsub = CHUNK // SUB
            cols = [slice(h * HEAD_W, (h + 1) * HEAD_W) for h in heads]
            hq = [u_ref[rows, h * HEAD_W:(h + 1) * HEAD_W] for h in heads]
            hf = [u_ref[rows, MIX_W + h * HEAD_W:MIX_W + (h + 1) * HEAD_W] for h in heads]
            hv = [u_ref[rows, 2 * MIX_W + h * HEAD_W:2 * MIX_W + (h + 1) * HEAD_W] for h in heads]
            lb = [lb_all[:, cols[h]] for h in heads]
            gts = [_hg_gates(hq[h], hf[h], lb[h], tri) for h in heads]
            q, sq, s, f, k, b = ([g[n] for g in gts] for n in range(6))
            scs = [_hg_scores(q[h], k[h], b[h], tril_mask) for h in heads]
            a, qts, kts, eqs, eks = ([sc[n] for sc in scs] for n in range(5))
            st = [sst_ref[c, h] for h in heads]
            dst = [dst_ref[h] for h in heads]
            bl = [b[h][CHUNK - 1:CHUNK] for h in heads]
            eb = [jnp.exp(b[h]) for h in heads]
            qh = [q[h] * eb[h] for h in heads]
            ekl = [jnp.exp(bl[h] - b[h]) for h in heads]
            kh = [k[h] * ekl[h] for h in heads]
            o = [_bdot(a[h], hv[h]) + _bdot_nt(qh[h], st[h]) for h in heads]
            do = []
            for h in heads:
                hgate = u_ref[rows, 3 * MIX_W + h * HEAD_W:3 * MIX_W + (h + 1) * HEAD_W]
                gnh = gn_ref[:, cols[h]]
                oh, rstd, y = _head_rms(o[h], gnh)
                sg = _sigmoid(hgate)
                dogh = dog_ref[rows, cols[h]]
                dy = dogh * (hgate * sg)
                du_ref[rows, 3 * MIX_W + h * HEAD_W:3 * MIX_W + (h + 1) * HEAD_W] = (dogh * y * (sg * (1.0 + hgate * (1.0 - sg)))).astype(ACT)
                dgn_ref[:, cols[h]] += jnp.sum(dy * oh, axis=0, keepdims=True)
                doh = dy * gnh
                do.append(rstd * (doh - oh * jnp.mean(doh * oh, axis=-1, keepdims=True)))
            da = [jnp.where(tril_mask, _bdot_nt(do[h], hv[h]), 0.0) for h in heads]
            dv = [_bdot_tn(a[h], do[h]) + _bdot_nt(kh[h], dst[h]) for h in heads]
            dq = [_bdot(do[h], st[h]) * eb[h] for h in heads]
            dk = [_bdot(hv[h], dst[h]) * ekl[h] for h in heads]
            d_last = [jnp.sum(k[h] * dk[h], axis=0, keepdims=True) + jnp.exp(bl[h]) * jnp.sum(dst[h] * st[h], axis=0, keepdims=True)
                      for h in heads]
            d_b = [q[h] * dq[h] - k[h] * dk[h] for h in heads]
            dqs = [[] for _ in heads]
            q_dq = [[] for _ in heads]
            for i in range(nsub):
                for h in heads:
                    da_i = _mx(da[h][i * SUB:(i + 1) * SUB])
                    q_r, k_r = _mx(qts[h][i]), _mx(kts[h][i])
                    g_q = jnp.dot(da_i, k_r, preferred_element_type=f32)
                    g_k = lax.dot_general(da_i, q_r, (((0,), (0,)), ((), ())), preferred_element_type=f32)
                    dqs[h].append(g_q * eqs[h][i])
                    q_dq[h].append(q_r.astype(f32) * g_q)
                    dk[h] = dk[h] + g_k * eks[h][i]
                    d_b[h] = d_b[h] - k_r.astype(f32) * g_k
            for h in heads:
                dq[h] = dq[h] + jnp.concatenate(dqs[h], axis=0)
                d_b[h] = d_b[h] + jnp.concatenate(q_dq[h], axis=0)
                dst_ref[h] = dst[h] * jnp.exp(bl[h]) + _bdot_tn(do[h], qh[h])
            dg = [_sel_dot(triu, d_b[h]) + d_last[h] for h in heads]
            for h in heads:
                dfk = dg[h] / f[h] - dk[h]
                du_ref[rows, h * HEAD_W:(h + 1) * HEAD_W] = (dq[h] * (sq[h] * (1.0 + hq[h] * (1.0 - sq[h])))).astype(ACT)
                du_ref[rows, MIX_W + h * HEAD_W:MIX_W + (h + 1) * HEAD_W] = ((1.0 - lb[h]) * dfk * s[h] * (1.0 - s[h])).astype(ACT)
                du_ref[rows, 2 * MIX_W + h * HEAD_W:2 * MIX_W + (h + 1) * HEAD_W] = dv[h].astype(ACT)
                dlb = jnp.sum((1.0 - s[h]) * dfk, axis=0, keepdims=True) * (lb[h] * (1.0 - lb[h]))
                dlg_ref[0:1, cols[h]] += dlb
                dlg_ref[1:2, cols[h]] -= dlb
            return carry

        lax.fori_loop(0, nc_blk, chunk, 0, unroll=True)

        if nr:
            @pl.when(pl.program_id(0) == nb - 1)
            def _():
                _scatter_wait(ride_in, ride_out, *ride_sems)

    hbm = pl.BlockSpec(memory_space=pltpu.HBM)
    ride_scratch = [pltpu.SemaphoreType.DMA((3 * nr,)), pltpu.SemaphoreType.DMA((3 * nr,)), pltpu.SemaphoreType.DMA((nr,))] if nr else []
    return pl.pallas_call(
        body, name="hgrn2_bwd", grid=(nb,),
        in_specs=[_rows_rev(tb, U_HG, nb), _const((2, MIX_W)), _const((1, MIX_W)),
                  pl.BlockSpec((nc_blk, HEADS, HEAD_W, HEAD_W), lambda i: (nb - 1 - i, 0, 0, 0)), _rows_rev(tb, MIX_W, nb)] + [hbm] * nr,
        out_specs=[_rows_rev(tb, U_HG, nb), _const((2, MIX_W)), _const((1, MIX_W))] + [hbm] * nr,
        out_shape=[jax.ShapeDtypeStruct((t, U_HG), ACT), jax.ShapeDtypeStruct((2, MIX_W), f32), jax.ShapeDtypeStruct((1, MIX_W), f32)]
        + [jax.ShapeDtypeStruct(r.shape, r.dtype) for r in riders],
        scratch_shapes=[pltpu.VMEM((HEADS, HEAD_W, HEAD_W), f32)] + ride_scratch,
        compiler_params=_cparams(1),
    )(u_hg, logits, gn, sst, dog, *riders)


def _conv_fwd(u_ml, w, b):
    t = u_ml.shape[0]
    tm = _tile(t, 512)

    def body(x_ref, w_ref, b_ref, pre_ref, act_ref, xbuf):
        @pl.when(pl.program_id(0) == 0)
        def _():
            xbuf[...] = jnp.zeros_like(xbuf)

        xbuf[0:HALO, :] = xbuf[tm:tm + HALO, :]
        xbuf[HALO:HALO + tm, :] = x_ref[...]
        pre = b_ref[...] + jnp.zeros((tm, MIX_W), f32)
        for kk in range(CONV_K):
            off = HALO - (CONV_K - 1) + kk
            pre = pre + w_ref[kk:kk + 1, :] * xbuf[off:off + tm, :]
        pre_ref[...] = pre
        act_ref[...] = pre * _sigmoid(pre)

    return pl.pallas_call(
        body, name="conv_fwd", grid=(t // tm,),
        in_specs=[_rows(tm, MIX_W), _const((CONV_K, MIX_W)), _const((1, MIX_W))],
        out_specs=[_rows(tm, MIX_W), _rows(tm, MIX_W)],
        out_shape=[jax.ShapeDtypeStruct((t, MIX_W), f32)] * 2,
        scratch_shapes=[pltpu.VMEM((tm + HALO, MIX_W), f32)],
        compiler_params=_cparams(1),
    )(u_ml, w, b)


def _conv_bwd(u_ml, w, pre, dact):
    t = u_ml.shape[0]
    tm = _tile(t, 512)
    nb = t // tm
    hb = tm // HALO

    def body(x_ref, halo_ref, w_ref, pre_ref, dact_ref, dx_ref, dw_ref, db_ref, dbuf, xbuf):
        i = pl.program_id(0)

        @pl.when(i == 0)
        def _():
            dbuf[...] = jnp.zeros_like(dbuf)
            dw_ref[...] = jnp.zeros_like(dw_ref)
            db_ref[...] = jnp.zeros_like(db_ref)

        p = pre_ref[...]
        sg = _sigmoid(p)
        dpre = dact_ref[...] * (sg * (1.0 + p * (1.0 - sg)))
        dbuf[tm:tm + HALO, :] = dbuf[0:HALO, :]
        dbuf[0:tm, :] = dpre
        has_prev = (i < nb - 1).astype(f32)
        xbuf[0:HALO, :] = halo_ref[...] * has_prev
        xbuf[HALO:HALO + tm, :] = x_ref[...]
        dx = jnp.zeros((tm, MIX_W), f32)
        for kk in range(CONV_K):
            back = CONV_K - 1 - kk
            dx = dx + w_ref[kk:kk + 1, :] * dbuf[back:back + tm, :]
            off = HALO - (CONV_K - 1) + kk
            dw_ref[kk:kk + 1, :] += jnp.sum(dpre * xbuf[off:off + tm, :], axis=0, keepdims=True)
        dx_ref[...] = dx.astype(ACT)
        db_ref[...] += jnp.sum(dpre, axis=0, keepdims=True)

    return pl.pallas_call(
        body, name="conv_bwd", grid=(nb,),
        in_specs=[_rows_rev(tm, MIX_W, nb),
                  pl.BlockSpec((HALO, MIX_W), lambda i: (jnp.maximum((nb - 1 - i) * hb - 1, 0), 0)),
                  _const((CONV_K, MIX_W)), _rows_rev(tm, MIX_W, nb), _rows_rev(tm, MIX_W, nb)],
        out_specs=[_rows_rev(tm, MIX_W, nb), _const((CONV_K, MIX_W)), _const((1, MIX_W))],
        out_shape=[jax.ShapeDtypeStruct((t, MIX_W), ACT), jax.ShapeDtypeStruct((CONV_K, MIX_W), f32), jax.ShapeDtypeStruct((1, MIX_W), f32)],
        scratch_shapes=[pltpu.VMEM((tm + HALO, MIX_W), f32), pltpu.VMEM((tm + HALO, MIX_W), f32)],
        compiler_params=_cparams(1),
    )(u_ml, u_ml, w, pre, dact)


def _lane_pick(x, lane):
    idx = lax.broadcasted_iota(jnp.int32, x.shape, 1)
    return jnp.sum(jnp.where(idx == lane, x, 0.0), axis=-1, keepdims=True)


def _ml_gate_forms(gates, tri):
    lf = _log_sigmoid(gates)
    gc = _sel_dot(tri, lf)
    lane = lax.broadcasted_iota(jnp.int32, gates.shape, 1)
    mixed = jnp.where(lane < HEADS, gates, gc)
    sel = (lax.broadcasted_iota(jnp.int32, (8, 128), 0) == lax.broadcasted_iota(jnp.int32, (8, 128), 1)).astype(f32)
    rowsf = _sel_dot_nt(sel, mixed)
    return gc, rowsf


def _ml_chunk(q, k, v, gates, gc, rowsf, c_st, n_st, m_st, tril_mask):
    hs = range(HEADS)
    g_col = [_lane_pick(gc, HEADS + h) for h in hs]
    ig_col = [_lane_pick(gates, h) for h in hs]
    dmat = [jnp.where(tril_mask, g_col[h] - rowsf[HEADS + h:HEADS + h + 1, :] + rowsf[h:h + 1, :], NEG) for h in hs]
    m_inter = [g_col[h] + m_st[h] for h in hs]
    m_t = [jnp.maximum(m_inter[h], jnp.max(dmat[h], axis=-1, keepdims=True)) for h in hs]
    wi = [jnp.exp(dmat[h] - m_t[h]) for h in hs]
    wo = [jnp.exp(m_inter[h] - m_t[h]) for h in hs]
    qk = [_bdot_nt(q[h], k[h]) * wi[h] for h in hs]
    num = [_bdot(qk[h], v[h]) + wo[h] * _bdot(q[h], c_st[h]) for h in hs]
    den = [_lane_sum(qk[h]) + wo[h] * _lane_dot(q[h], n_st[h]) for h in hs]
    floor = [jnp.exp(-m_t[h]) for h in hs]
    z = [jnp.maximum(jnp.abs(den[h]), floor[h]) for h in hs]
    g_last = [g_col[h][CHUNK - 1:CHUNK] for h in hs]
    a_col = [g_last[h] - g_col[h] + ig_col[h] for h in hs]
    m_new = [jnp.maximum(g_last[h] + m_st[h], jnp.max(a_col[h], axis=0, keepdims=True)) for h in hs]
    ws = [jnp.exp(a_col[h] - m_new[h]) for h in hs]
    w_old = [jnp.exp(g_last[h] + m_st[h] - m_new[h]) for h in hs]
    return dict(wi=wi, wo=wo, qk=qk, num=num, den=den, z=z, floor=floor, ws=ws, w_old=w_old, m_new=m_new)


def _mlstm_fwd(qkc, u_ml, gn, riders=()):
    t = qkc.shape[0]
    tb = _tile(t, 256)
    nc_blk = tb // CHUNK

    def body(qk_ref, v_ref, mo_ref, gt_ref, gn_ref, og_ref, cst_ref, nst_ref, mst_ref, c_sc, n_sc, m_sc):
        @pl.when(pl.program_id(0) == 0)
        def _():
            c_sc[...] = jnp.zeros_like(c_sc)
            n_sc[...] = jnp.zeros_like(n_sc)
            m_sc[...] = jnp.zeros_like(m_sc)

        tril_mask = _tri(CHUNK)
        tri = tril_mask.astype(f32)

        def chunk(c, carry):
            r0 = pl.multiple_of(c * CHUNK, CHUNK)
            rows = pl.ds(r0, CHUNK)
            gates = gt_ref[rows, :]
            gc, rowsf = _ml_gate_forms(gates, tri)
            hs = range(HEADS)
            q = [qk_ref[rows, h * ML_DQK:(h + 1) * ML_DQK] * (ML_DQK ** -0.5) for h in hs]
            k = [qk_ref[rows, HEADS * ML_DQK + h * ML_DQK:HEADS * ML_DQK + (h + 1) * ML_DQK] for h in hs]
            v = [v_ref[rows, h * HEAD_W:(h + 1) * HEAD_W] for h in hs]
            c_st = [c_sc[h] for h in hs]
            n_st = [n_sc[h] for h in hs]
            m_full = [m_sc[h] for h in hs]
            r = _ml_chunk(q, k, v, gates, gc, rowsf, c_st, n_st, [m[:, 0:1] for m in m_full], tril_mask)
            ksc = [k[h] * r["ws"][h] for h in hs]
            new_c = [r["w_old"][h] * c_st[h] + _bdot_tn(ksc[h], v[h]) for h in hs]
            for h in hs:
                cs = slice(h * HEAD_W, (h + 1) * HEAD_W)
                cst_ref[c, h] = c_st[h]
                nst_ref[c, h] = n_st[h]
                mst_ref[c, h] = m_full[h]
                c_sc[h] = new_c[h]
                n_sc[h] = r["w_old"][h] * n_st[h] + jnp.sum(ksc[h], axis=0, keepdims=True)
                m_sc[h] = r["m_new"][h] + jnp.zeros((1, 128), f32)
                _, _, y = _head_rms(r["num"][h] / r["z"][h], gn_ref[:, cs], on_mxu=True)
                og_ref[rows, cs] = (y * _sigmoid(mo_ref[rows, h * HEAD_W:(h + 1) * HEAD_W])).astype(ACT)
            return carry

        lax.fori_loop(0, nc_blk, chunk, 0)

    nchunks = t // CHUNK
    return _riding_call(
        body, "mlstm_fwd", t // tb,
        in_specs=[_rows(tb, MIX_W), _rows(tb, MIX_W, 1), _rows(tb, MIX_W, 2), _rows(tb, 128, 12), _const((1, MIX_W))],
        out_specs=[_rows(tb, MIX_W),
                   pl.BlockSpec((nc_blk, HEADS, ML_DQK, HEAD_W), lambda i: (i, 0, 0, 0)),
                   pl.BlockSpec((nc_blk, HEADS, 1, ML_DQK), lambda i: (i, 0, 0, 0)),
                   pl.BlockSpec((nc_blk, HEADS, 1, 128), lambda i: (i, 0, 0, 0))],
        out_shape=[jax.ShapeDtypeStruct((t, MIX_W), ACT),
                   jax.ShapeDtypeStruct((nchunks, HEADS, ML_DQK, HEAD_W), f32),
                   jax.ShapeDtypeStruct((nchunks, HEADS, 1, ML_DQK), f32),
                   jax.ShapeDtypeStruct((nchunks, HEADS, 1, 128), f32)],
        scratch_shapes=[pltpu.VMEM((HEADS, ML_DQK, HEAD_W), f32), pltpu.VMEM((HEADS, 1, ML_DQK), f32), pltpu.VMEM((HEADS, 1, 128), f32)],
        operands=(qkc, u_ml, u_ml, u_ml, gn), riders=riders, copies=_gather_copies, ride_shapes=_gather_shapes(riders))


def _mlstm_bwd(qkc, u_ml, gn, cst, nst, mst, dog, riders=()):
    t = qkc.shape[0]
    tb = _tile(t, 256)
    nb = t // tb
    nc_blk = tb // CHUNK

    def body(qk_ref, v_ref, mo_ref, gt_ref, gn_ref, cst_ref, nst_ref, mst_ref, dog_ref,
             dqk_ref, dv_ref, dmo_ref, dgt_ref, dgn_ref, dc_sc, dn_sc):
        @pl.when(pl.program_id(0) == 0)
        def _():
            dc_sc[...] = jnp.zeros_like(dc_sc)
            dn_sc[...] = jnp.zeros_like(dn_sc)
            dgn_ref[...] = jnp.zeros_like(dgn_ref)

        tril_mask = _tri(CHUNK)
        tri = tril_mask.astype(f32)
        triu = _tri(CHUNK, upper=True).astype(f32)
        lane = lax.broadcasted_iota(jnp.int32, (CHUNK, 128), 1)

        def chunk(j, carry):
            c = nc_blk - 1 - j
            r0 = pl.multiple_of(c * CHUNK, CHUNK)
            rows = pl.ds(r0, CHUNK)
            gates = gt_ref[rows, :]
            gc, rowsf = _ml_gate_forms(gates, tri)
            dg_mat = jnp.zeros((CHUNK, 128), f32)
            dig_mat = jnp.zeros((CHUNK, 128), f32)
            dlast_row = jnp.zeros((1, 128), f32)
            hs = range(HEADS)
            cols = [slice(h * HEAD_W, (h + 1) * HEAD_W) for h in hs]
            q = [qk_ref[rows, h * ML_DQK:(h + 1) * ML_DQK] * (ML_DQK ** -0.5) for h in hs]
            k = [qk_ref[rows, HEADS * ML_DQK + h * ML_DQK:HEADS * ML_DQK + (h + 1) * ML_DQK] for h in hs]
            v = [v_ref[rows, h * HEAD_W:(h + 1) * HEAD_W] for h in hs]
            c_st = [cst_ref[c, h] for h in hs]
            n_st = [nst_ref[c, h] for h in hs]
            m_st = [mst_ref[c, h][:, 0:1] for h in hs]
            dc = [dc_sc[h] for h in hs]
            dn = [dn_sc[h] for h in hs]
            r = _ml_chunk(q, k, v, gates, gc, rowsf, c_st, n_st, m_st, tril_mask)
            z, wi, wo, ws, w_old, den = r["z"], r["wi"], r["wo"], r["ws"], r["w_old"], r["den"]
            hh = [r["num"][h] / z[h] for h in hs]
            dh = []
            for h in hs:
                gnh = gn_ref[:, cols[h]]
                oh, rstd, y = _head_rms(hh[h], gnh, on_mxu=True)
                sg = _sigmoid(mo_ref[rows, h * HEAD_W:(h + 1) * HEAD_W])
                dogh = dog_ref[rows, cols[h]]
                dy = dogh * sg
                dmo_ref[rows, cols[h]] = (dogh * y * (sg * (1.0 - sg))).astype(ACT)
                dgn_ref[:, cols[h]] += jnp.sum(dy * oh, axis=0, keepdims=True)
                doh = dy * gnh
                dh.append(rstd * (doh - oh * (_lane_sum(doh * oh) * (1.0 / HEAD_W))))
            dnum = [dh[h] / z[h] for h in hs]
            dz = [-_lane_sum(dh[h] * hh[h]) / z[h] for h in hs]
            dden = [jnp.where(jnp.abs(den[h]) > r["floor"][h], dz[h] * jnp.sign(den[h]), 0.0) for h in hs]
            dsw = [(_bdot_nt(dnum[h], v[h]) + dden[h]) * wi[h] for h in hs]
            dq = [_bdot(dsw[h], k[h]) + wo[h] * (_bdot_nt(dnum[h], c_st[h]) + dden[h][:, :ML_DQK] * n_st[h]) for h in hs]
            dk_state = [ws[h] * (_bdot_nt(v[h], dc[h]) + dn[h]) for h in hs]
            dk = [_bdot_tn(dsw[h], q[h]) + dk_state[h] for h in hs]
            dv = [_bdot_tn(r["qk"][h], dnum[h]) + ws[h] * _bdot(k[h], dc[h]) for h in hs]
            woq = [wo[h] * q[h] for h in hs]
            new_dc = [w_old[h] * dc[h] + _bdot_tn(woq[h], dnum[h]) for h in hs]
            for h in hs:
                dv_ref[rows, cols[h]] = dv[h].astype(ACT)
                dc_sc[h] = new_dc[h]
                dn_sc[h] = w_old[h] * dn[h] + jnp.sum(woq[h] * dden[h][:, :ML_DQK], axis=0, keepdims=True)
                d_last = (jnp.sum(jnp.sum(k[h] * dk_state[h], axis=0, keepdims=True), axis=-1, keepdims=True)
                          + w_old[h] * (jnp.sum(jnp.sum(dc[h] * c_st[h], axis=0, keepdims=True), axis=-1, keepdims=True)
                                        + jnp.sum(dn[h] * n_st[h], axis=-1, keepdims=True)))
                kdk = _lane_sum(k[h] * dk[h])
                qdq = _lane_sum(q[h] * dq[h])
                dg_mat = dg_mat + jnp.where(lane == HEADS + h, qdq - kdk, 0.0)
                dlast_row = dlast_row + jnp.where(lane[0:1] == HEADS + h, d_last, 0.0)
                dig_mat = dig_mat + jnp.where(lane == h, kdk, 0.0)
                dqk_ref[rows, h * ML_DQK:(h + 1) * ML_DQK] = dq[h] * (ML_DQK ** -0.5)
                dqk_ref[rows, HEADS * ML_DQK + h * ML_DQK:HEADS * ML_DQK + (h + 1) * ML_DQK] = dk[h]
            dlf = _sel_dot(triu, dg_mat) + dlast_row
            dgt_ref[rows, :] = (dig_mat + dlf * _sigmoid(-gates)).astype(ACT)
            return carry

        lax.fori_loop(0, nc_blk, chunk, 0)

    st4 = lambda a, b: pl.BlockSpec((nc_blk, HEADS, a, b), lambda i: (nb - 1 - i, 0, 0, 0))
    return _riding_call(
        body, "mlstm_bwd", nb,
        in_specs=[_rows_rev(tb, MIX_W, nb), _rows_rev(tb, MIX_W, nb, 1), _rows_rev(tb, MIX_W, nb, 2), _rows_rev(tb, 128, nb, 12),
                  _const((1, MIX_W)), st4(ML_DQK, HEAD_W), st4(1, ML_DQK), st4(1, 128), _rows_rev(tb, MIX_W, nb)],
        out_specs=[_rows_rev(tb, MIX_W, nb), _rows_rev(tb, MIX_W, nb), _rows_rev(tb, MIX_W, nb), _rows_rev(tb, 128, nb), _const((1, MIX_W))],
        out_shape=[jax.ShapeDtypeStruct((t, MIX_W), f32), jax.ShapeDtypeStruct((t, MIX_W), ACT), jax.ShapeDtypeStruct((t, MIX_W), ACT),
                   jax.ShapeDtypeStruct((t, 128), ACT), jax.ShapeDtypeStruct((1, MIX_W), f32)],
        scratch_shapes=[pltpu.VMEM((HEADS, ML_DQK, HEAD_W), f32), pltpu.VMEM((HEADS, 1, ML_DQK), f32)],
        operands=(qkc, u_ml, u_ml, u_ml, gn, cst, nst, mst, dog), riders=riders, copies=_scatter_copies,
        ride_shapes=[jax.ShapeDtypeStruct(r.shape, r.dtype) for r in riders])


def _ln_fwd(r, g, b):
    mu = jnp.mean(r, axis=-1, keepdims=True)
    xc = r - mu
    rstd = lax.rsqrt(jnp.mean(xc * xc, axis=-1, keepdims=True) + LN_EPS)
    xh = xc * rstd
    return xh * g + b, xh, rstd


def _ln_bwd(dy, xh, rstd, g):
    dxh = dy * g
    return rstd * (dxh - jnp.mean(dxh, axis=-1, keepdims=True) - xh * jnp.mean(dxh * xh, axis=-1, keepdims=True))


def _outproj_ln1(og_hg, og_ml, x, w_out, g, b, riders=()):
    t = x.shape[0]
    tm = _tile(t, DENSE_ROWS)

    def body(a_ref, b_ref, x_ref, w_ref, g_ref, bb_ref, x1_ref, xh_ref, rs_ref, x1b_ref):
        mix = _bdot(a_ref[...], w_ref[0:MIX_W, :]) + _bdot(b_ref[...], w_ref[MIX_W:2 * MIX_W, :])
        y, xh, rstd = _ln_fwd(ALPHA * x_ref[...] + mix, g_ref[...], bb_ref[...])
        x1_ref[...] = y
        x1b_ref[...] = y.astype(ACT)
        xh_ref[...] = xh.astype(ACT)
        rs_ref[...] = rstd

    return _riding_call(
        body, "outproj_ln1", t // tm,
        in_specs=[_rows(tm, MIX_W), _rows(tm, MIX_W), _rows(tm, D_MODEL), _resident((D_MODEL, D_MODEL)), _const((1, D_MODEL)), _const((1, D_MODEL))],
        out_specs=[_rows(tm, D_MODEL), _rows(tm, D_MODEL), _rows(tm, 1), _rows(tm, D_MODEL)],
        out_shape=[jax.ShapeDtypeStruct((t, D_MODEL), f32), jax.ShapeDtypeStruct((t, D_MODEL), ACT), jax.ShapeDtypeStruct((t, 1), f32),
                   jax.ShapeDtypeStruct((t, D_MODEL), ACT)],
        scratch_shapes=[], operands=(og_hg, og_ml, x, w_out, g, b), riders=riders, copies=_gather_copies, ride_shapes=_gather_shapes(riders))


def _ffn_up(x1, wg, wu, riders=()):
    t = x1.shape[0]
    tm = _tile(t, DENSE_ROWS)

    def body(x_ref, wg_ref, wu_ref, hg_ref, up_ref, a_ref):
        xv = x_ref[...]
        hg = _bdot_nt(xv, wg_ref[...])
        up = _bdot_nt(xv, wu_ref[...])
        hg_ref[...] = hg.astype(ACT)
        up_ref[...] = up.astype(ACT)
        a_ref[...] = (hg * _sigmoid(hg) * up).astype(ACT)

    return _riding_call(
        body, "ffn_up", t // tm,
        in_specs=[_rows(tm, D_MODEL), _resident((D_FF, D_MODEL)), _resident((D_FF, D_MODEL))],
        out_specs=[_rows(tm, D_FF), _rows(tm, D_FF), _rows(tm, D_FF)],
        out_shape=[jax.ShapeDtypeStruct((t, D_FF), ACT), jax.ShapeDtypeStruct((t, D_FF), ACT), jax.ShapeDtypeStruct((t, D_FF), ACT)],
        scratch_shapes=[], operands=(x1, wg, wu), riders=riders, copies=_gather_copies, ride_shapes=_gather_shapes(riders))


def _ffn_down_ln2(a, x1, wd, g, b):
    t = x1.shape[0]
    tm = _tile(t, DENSE_ROWS)

    def body(a_ref, x_ref, w_ref, g_ref, bb_ref, x2_ref, xh_ref, rs_ref, x2b_ref):
        ffn = _bdot(a_ref[...], w_ref[...])
        y, xh, rstd = _ln_fwd(ALPHA * x_ref[...] + ffn, g_ref[...], bb_ref[...])
        x2_ref[...] = y
        x2b_ref[...] = y.astype(ACT)
        xh_ref[...] = xh.astype(ACT)
        rs_ref[...] = rstd

    return pl.pallas_call(
        body, name="ffn_down_ln2", grid=(t // tm,),
        in_specs=[_rows(tm, D_FF), _rows(tm, D_MODEL), _resident((D_FF, D_MODEL)), _const((1, D_MODEL)), _const((1, D_MODEL))],
        out_specs=[_rows(tm, D_MODEL), _rows(tm, D_MODEL), _rows(tm, 1), _rows(tm, D_MODEL)],
        out_shape=[jax.ShapeDtypeStruct((t, D_MODEL), f32), jax.ShapeDtypeStruct((t, D_MODEL), ACT), jax.ShapeDtypeStruct((t, 1), f32),
                   jax.ShapeDtypeStruct((t, D_MODEL), ACT)],
        compiler_params=_cparams(1, arbitrary=False),
    )(a, x1, wd, g, b)


def _head_loss_bwd(x2, xh2, rs2, p, tgt, w_pg, b_pg, w_pp, g2):
    t = x2.shape[0]
    tm = _tile(t, DENSE_ROWS)

    def body(x_ref, xh_ref, rs_ref, p_ref, t_ref, wg_ref, bg_ref, wp_ref, g_ref,
             dr_ref, de_ref, dz_ref, loss_ref, dbg_ref, dg2_ref, db2_ref):
        @pl.when(pl.program_id(0) == 0)
        def _():
            loss_ref[...] = jnp.zeros_like(loss_ref)
            dbg_ref[...] = jnp.zeros_like(dbg_ref)
            dg2_ref[...] = jnp.zeros_like(dg2_ref)
            db2_ref[...] = jnp.zeros_like(db2_ref)

        x2v = x_ref[...]
        z = _bdot(x2v, wg_ref[...]) + bg_ref[...]
        e = _bdot(p_ref[...], wp_ref[...])
        sg = _sigmoid(z)
        diff = x2v + sg * e - t_ref[...]
        loss_ref[...] += 0.5 * jnp.sum(jnp.mean(diff * diff, axis=-1, keepdims=True), axis=0, keepdims=True)
        dy = diff * (1.0 / D_MODEL)
        de_ref[...] = (dy * sg).astype(ACT)
        dz = dy * e * (sg * (1.0 - sg))
        dz_ref[...] = dz.astype(ACT)
        dbg_ref[...] += jnp.sum(dz, axis=0, keepdims=True)
        dx2 = dy + _bdot_nt(dz, wg_ref[...])
        xh = xh_ref[...].astype(f32)
        dg2_ref[...] += jnp.sum(dx2 * xh, axis=0, keepdims=True)
        db2_ref[...] += jnp.sum(dx2, axis=0, keepdims=True)
        dr_ref[...] = _ln_bwd(dx2, xh, rs_ref[...], g_ref[...])

    row = jax.ShapeDtypeStruct((1, D_MODEL), f32)
    return pl.pallas_call(
        body, name="head_loss_bwd", grid=(t // tm,),
        in_specs=[_rows(tm, D_MODEL), _rows(tm, D_MODEL), _rows(tm, 1), _rows(tm, PLE), _rows(tm, D_MODEL),
                  _resident((D_MODEL, D_MODEL)), _const((1, D_MODEL)), _resident((PLE, D_MODEL)), _const((1, D_MODEL))],
        out_specs=[_rows(tm, D_MODEL), _rows(tm, D_MODEL), _rows(tm, D_MODEL), _const((1, 1)), _const((1, D_MODEL)), _const((1, D_MODEL)), _const((1, D_MODEL))],
        out_shape=[jax.ShapeDtypeStruct((t, D_MODEL), f32), jax.ShapeDtypeStruct((t, D_MODEL), ACT), jax.ShapeDtypeStruct((t, D_MODEL), ACT),
                   jax.ShapeDtypeStruct((1, 1), f32), row, row, row],
        compiler_params=_cparams(1),
    )(x2, xh2, rs2, p, tgt, w_pg, b_pg, w_pp, g2)


def _ffn_bwd(dr2, hg, up, xh1, rs1, wd, wg, wu, g1, w_out):
    t = dr2.shape[0]
    tm = _tile(t, DENSE_ROWS // 2)

    def body(dr_ref, hg_ref, up_ref, xh_ref, rs_ref, wd_ref, wg_ref, wu_ref, g_ref, wo_ref,
             dr1_ref, dhg_ref, dup_ref, dg1_ref, db1_ref, doghg_ref, dogml_ref):
        @pl.when(pl.program_id(0) == 0)
        def _():
            dg1_ref[...] = jnp.zeros_like(dg1_ref)
            db1_ref[...] = jnp.zeros_like(db1_ref)

        dr2v = dr_ref[...]
        da = _bdot_nt(dr2v, wd_ref[...])
        hgv = hg_ref[...].astype(f32)
        sg = _sigmoid(hgv)
        dhg = da * up_ref[...].astype(f32) * (sg * (1.0 + hgv * (1.0 - sg)))
        dup = da * (hgv * sg)
        dhg_ref[...] = dhg.astype(ACT)
        dup_ref[...] = dup.astype(ACT)
        dx1 = ALPHA * dr2v + _bdot(dhg, wg_ref[...]) + _bdot(dup, wu_ref[...])
        xh = xh_ref[...].astype(f32)
        dg1_ref[...] += jnp.sum(dx1 * xh, axis=0, keepdims=True)
        db1_ref[...] += jnp.sum(dx1, axis=0, keepdims=True)
        dr1 = _ln_bwd(dx1, xh, rs_ref[...], g_ref[...])
        dr1_ref[...] = dr1
        dog = _bdot_nt(dr1, wo_ref[...])
        doghg_ref[...] = dog[:, 0:MIX_W]
        dogml_ref[...] = dog[:, MIX_W:2 * MIX_W]

    row = jax.ShapeDtypeStruct((1, D_MODEL), f32)
    return pl.pallas_call(
        body, name="ffn_bwd", grid=(t // tm,),
        in_specs=[_rows(tm, D_MODEL), _rows(tm, D_FF), _rows(tm, D_FF), _rows(tm, D_MODEL), _rows(tm, 1),
                  _resident((D_FF, D_MODEL)), _resident((D_FF, D_MODEL)), _resident((D_FF, D_MODEL)), _const((1, D_MODEL)),
                  _resident((D_MODEL, D_MODEL))],
        out_specs=[_rows(tm, D_MODEL), _rows(tm, D_FF), _rows(tm, D_FF), _const((1, D_MODEL)), _const((1, D_MODEL)),
                   _rows(tm, MIX_W), _rows(tm, MIX_W)],
        out_shape=[jax.ShapeDtypeStruct((t, D_MODEL), f32), jax.ShapeDtypeStruct((t, D_FF), ACT), jax.ShapeDtypeStruct((t, D_FF), ACT), row, row,
                   jax.ShapeDtypeStruct((t, MIX_W), f32), jax.ShapeDtypeStruct((t, MIX_W), f32)],
        compiler_params=_cparams(1),
    )(dr2, hg, up, xh1, rs1, wd, wg, wu, g1, w_out)


def _inproj_bwd(dr1, du_hg, dqk, dmv, dmo, dgt, w_hg, w_ml):
    t = dr1.shape[0]
    tm = _tile(t, DENSE_ROWS)

    def body(dr_ref, dhg_ref, dqk_ref, dmv_ref, dmo_ref, dgt_ref, whg_ref, wml_ref, gx_ref, dml_ref):
        dml = jnp.concatenate([dqk_ref[...], dmv_ref[...], dmo_ref[...], dgt_ref[...]], axis=-1).astype(ACT)
        dml_ref[...] = dml
        gx_ref[...] = ALPHA * dr_ref[...] + _bdot(dhg_ref[...], whg_ref[...]) + _bdot(dml, wml_ref[...])

    return pl.pallas_call(
        body, name="inproj_bwd", grid=(t // tm,),
        in_specs=[_rows(tm, D_MODEL), _rows(tm, U_HG), _rows(tm, MIX_W), _rows(tm, MIX_W), _rows(tm, MIX_W), _rows(tm, 128),
                  _resident((U_HG, D_MODEL)), _resident((U_ML, D_MODEL))],
        out_specs=[_rows(tm, D_MODEL), _rows(tm, U_ML)],
        out_shape=[jax.ShapeDtypeStruct((t, D_MODEL), f32), jax.ShapeDtypeStruct((t, U_ML), ACT)],
        compiler_params=_cparams(1, arbitrary=False),
    )(dr1, du_hg, dqk, dmv, dmo, dgt, w_hg, w_ml)


def _wgrad(a, b, name, tk=None, tn=None, colsum=False, low=False):
    t, kdim = a.shape
    n = b.shape[1]
    tk = tk or kdim
    tn = tn or n
    tt = _tile(t, WGRAD_ROWS)
    nt = t // tt
    assert not (colsum and low) and (not colsum or tn == n)

    def body(a_ref, b_ref, o_ref, *s_ref):
        @pl.when(pl.program_id(2) == 0)
        def _():
            o_ref[...] = jnp.zeros_like(o_ref)
            if colsum:
                s_ref[0][...] = jnp.zeros_like(s_ref[0])

        av = a_ref[...]
        o_ref[...] += _bdot_tn(av, b_ref[...])
        if colsum:
            s_ref[0][...] += jnp.sum(av.astype(f32), axis=0, keepdims=True)
        if low:
            @pl.when(pl.program_id(2) == nt - 1)
            def _():
                s_ref[0][...] = o_ref[...].astype(bf16)

    out_specs = [pl.BlockSpec((tk, tn), lambda i, j, s: (i, j))]
    out_shape = [jax.ShapeDtypeStruct((kdim, n), f32)]
    if colsum:
        out_specs.append(pl.BlockSpec((1, tk), lambda i, j, s: (0, i)))
        out_shape.append(jax.ShapeDtypeStruct((1, kdim), f32))
    if low:
        out_specs.append(pl.BlockSpec((tk, tn), lambda i, j, s: (i, j)))
        out_shape.append(jax.ShapeDtypeStruct((kdim, n), bf16))
    res = pl.pallas_call(
        body, name=name, grid=(kdim // tk, n // tn, t // tt),
        in_specs=[pl.BlockSpec((tt, tk), lambda i, j, s: (s, i)), pl.BlockSpec((tt, tn), lambda i, j, s: (s, j))],
        out_specs=out_specs, out_shape=out_shape,
        compiler_params=_cparams(3),
    )(a, b)
    return res if (colsum or low) else res[0]


def _colsum(parts, name):
    t = parts[0].shape[0]
    tt = _tile(t, 512)
    widths = [a.shape[1] for a in parts]

    def body(*refs):
        o_ref = refs[-1]

        @pl.when(pl.program_id(0) == 0)
        def _():
            o_ref[...] = jnp.zeros_like(o_ref)

        off = 0
        for r, w in zip(refs[:-1], widths):
            o_ref[:, off:off + w] += jnp.sum(r[...].astype(f32), axis=0, keepdims=True)
            off += w

    return pl.pallas_call(
        body, name=name, grid=(t // tt,),
        in_specs=[_rows(tt, w) for w in widths],
        out_specs=_const((1, sum(widths))),
        out_shape=jax.ShapeDtypeStruct((1, sum(widths)), f32),
        compiler_params=_cparams(1),
    )(*parts)


_TRANSPOSED = {"w_in", "w_ffn_gate", "w_ffn_up"}
_COL_SPLIT = {"ple_w_proj"}
_SCATTER_PLAN = (("w_ffn_gate", "w_ffn_up", "w_out"), ("w_ffn_down", "ple_w_gate", "ple_w_proj"))
_RIDE_PLAN = {"inproj": ("w_ffn_gate",), "hgrn2_fwd": ("w_ffn_up",), "mlstm_fwd": ("w_out",),
              "outproj_ln1": ("ple_w_gate", "ple_w_proj"), "ffn_up": ("w_ffn_down",)}


def _from_chip_major(a, col_split):
    if col_split:
        return a.transpose(1, 0, 2).reshape(a.shape[1], 4 * a.shape[2])
    return a.reshape(4 * a.shape[1], a.shape[2])


def _local_step(x, p, tgt, w_in_b, b_in, logits, conv_w, conv_b, hg_gn, ml_gn, w_out_b, ln1_g, ln1_b,
                wg_b, wu_b, wd_b, ln2_g, ln2_b, w_pp_b, w_pg_b, b_pg, early_hook=None, late_shards=None):
    pad_w = U_HG + U_ML - PROJ_W
    w_hg = w_in_b[:U_HG]
    w_ml = jnp.pad(w_in_b[U_HG:], ((0, pad_w), (0, 0)))
    bb_hg = b_in[:, :U_HG]
    bb_ml = jnp.pad(b_in[:, U_HG:], ((0, 0), (0, pad_w)))

    late = dict(w_out=w_out_b, w_ffn_gate=wg_b, w_ffn_up=wu_b, w_ffn_down=wd_b, ple_w_proj=w_pp_b, ple_w_gate=w_pg_b)

    def riders_of(call):
        return [late_shards[k] for k in _RIDE_PLAN[call]] if late_shards is not None else ()

    def arrived(call, got):
        for k, g in zip(_RIDE_PLAN[call], got):
            late[k] = _from_chip_major(g, k in _COL_SPLIT)

    (u_hg, u_ml, xb), got = _inproj(x, w_hg, w_ml, bb_hg, bb_ml, riders_of("inproj"))
    arrived("inproj", got)
    (og_hg, sst), got = _hgrn2_fwd(u_hg, logits, hg_gn, riders_of("hgrn2_fwd"))
    arrived("hgrn2_fwd", got)
    pre, qkc = _conv_fwd(u_ml, conv_w, conv_b)
    (og_ml, cst, nst, mst), got = _mlstm_fwd(qkc, u_ml, ml_gn, riders_of("mlstm_fwd"))
    arrived("mlstm_fwd", got)
    (x1, xh1, rs1, x1b), got = _outproj_ln1(og_hg, og_ml, x, late["w_out"], ln1_g, ln1_b, riders_of("outproj_ln1"))
    arrived("outproj_ln1", got)
    (hgp, up, act), got = _ffn_up(x1b, late["w_ffn_gate"], late["w_ffn_up"], riders_of("ffn_up"))
    arrived("ffn_up", got)
    w_out_b, wg_b, wu_b, wd_b = late["w_out"], late["w_ffn_gate"], late["w_ffn_up"], late["w_ffn_down"]
    w_pp_b, w_pg_b = late["ple_w_proj"], late["ple_w_gate"]
    x2, xh2, rs2, x2b = _ffn_down_ln2(act, x1, wd_b, ln2_g, ln2_b)
    dr2, de, dz, loss, d_bpg, d_ln2g, d_ln2b = _head_loss_bwd(x2, xh2, rs2, p, tgt, w_pg_b, b_pg, w_pp_b, ln2_g)
    dr1, dhg, dup, d_ln1g, d_ln1b, dog_hg, dog_ml = _ffn_bwd(dr2, hgp, up, xh1, rs1, wd_b, wg_b, wu_b, ln1_g, w_out_b)

    d_wo_a, lo_wo_a = _wgrad(og_hg, dr1, "wgrad_out_hg", low=True)
    d_wo_b, lo_wo_b = _wgrad(og_ml, dr1, "wgrad_out_ml", low=True)
    d_wg, lo_wg = _wgrad(dhg, x1b, "wgrad_ffn_gate", tk=D_FF // 2, low=True)
    d_wu, lo_wu = _wgrad(dup, x1b, "wgrad_ffn_up", tk=D_FF // 2, low=True)
    d_wd, lo_wd = _wgrad(act, dr2, "wgrad_ffn_down", tk=D_FF // 2, low=True)
    d_wpp, lo_wpp = _wgrad(p, de, "wgrad_ple_proj", low=True)
    d_wpg, lo_wpg = _wgrad(x2b, dz, "wgrad_ple_gate", low=True)
    early = dict(w_out=jnp.concatenate([d_wo_a, d_wo_b], axis=0), w_ffn_gate=d_wg, w_ffn_up=d_wu, w_ffn_down=d_wd,
                 ple_w_proj=d_wpp, ple_w_gate=d_wpg)
    early_low = dict(w_out=jnp.concatenate([lo_wo_a, lo_wo_b], axis=0), w_ffn_gate=lo_wg, w_ffn_up=lo_wu, w_ffn_down=lo_wd,
                     ple_w_proj=lo_wpp, ple_w_gate=lo_wpg)
    ride_hg, ride_ml = early_hook(early_low) if early_hook is not None else ((), ())

    res = _hgrn2_bwd(u_hg, logits, hg_gn, sst, dog_hg, ride_hg)
    du_hg, d_logits, d_hg_gn = res[:3]
    (dqkc, dmv, dmo, dgt, d_ml_gn), got_ml = _mlstm_bwd(qkc, u_ml, ml_gn, cst, nst, mst, dog_ml, ride_ml)
    dqk, d_conv_w, d_conv_b = _conv_bwd(u_ml, conv_w, pre, dqkc)
    grad_x, du_ml = _inproj_bwd(dr1, du_hg, dqk, dmv, dmo, dgt, w_hg, w_ml)

    dw_hg, db_hg = _wgrad(du_hg, xb, "wgrad_in_hg", tk=U_HG // 2, colsum=True)
    dw_ml, db_ml = _wgrad(du_ml, xb, "wgrad_in_ml", colsum=True)
    d_w_in = jnp.concatenate([dw_hg, dw_ml[:PROJ_W - U_HG]], axis=0)
    d_b_in = jnp.concatenate([db_hg, db_ml[:, :PROJ_W - U_HG]], axis=1)

    grads = dict(w_in=d_w_in, b_in=d_b_in, hg_lb_logits=d_logits, ml_conv_w=d_conv_w, ml_conv_b=d_conv_b,
                 hg_norm_g=d_hg_gn, ml_norm_g=d_ml_gn, ln1_g=d_ln1g, ln1_b=d_ln1b, ln2_g=d_ln2g, ln2_b=d_ln2b,
                 ple_b_gate=d_bpg, **early)
    return loss, grad_x, grads, (list(res[3:]), list(got_ml))


_ANY = pl.BlockSpec(memory_space=pltpu.HBM)
_MESH = pl.DeviceIdType.MESH


def _my_place():
    return lax.axis_index("x"), lax.axis_index("y"), lax.axis_index("c")


def _other_chips(x, y):
    return [(1 - x, y), (x, 1 - y), (1 - x, 1 - y)]


def _allgather_weights(shards, taps, name):
    n = len(shards)
    halves = [s.shape[0] // 2 for s in shards]

    def body(*refs):
        ins, tap_in = refs[:n], refs[n]
        outs, tap_out = refs[n + 1:2 * n + 1], refs[2 * n + 1]
        send_sems, recv_sems, local_sems = refs[2 * n + 2:]
        x, y, c = _my_place()
        me = 2 * x + y
        sibling = (x, y, 1 - c)
        chips = _other_chips(x, y)

        def ici(a, j, block_chip):
            px, py = chips[j]
            src = ins[a].at[pl.ds(pl.multiple_of(c * halves[a], 16), halves[a])] if block_chip is None else outs[a].at[block_chip, c]
            dst = outs[a].at[me if block_chip is None else block_chip, c]
            return pltpu.make_async_remote_copy(src_ref=src, dst_ref=dst, send_sem=send_sems.at[6 * a + j], recv_sem=recv_sems.at[6 * a + j],
                                                device_id=(px, py, c), device_id_type=_MESH)

        def d2d(a, j, half):
            px, py = chips[j]
            blk = outs[a].at[2 * px + py, half]
            return pltpu.make_async_remote_copy(src_ref=blk, dst_ref=blk, send_sem=send_sems.at[6 * a + 3 + j], recv_sem=recv_sems.at[6 * a + 3 + j],
                                                device_id=sibling, device_id_type=_MESH)

        local = []
        for a in range(n):
            for h in range(2):
                cp = pltpu.make_async_copy(ins[a].at[pl.ds(h * halves[a], halves[a])], outs[a].at[me, h], local_sems.at[2 * a + h])
                cp.start()
                local.append(cp)
            for j in range(3):
                ici(a, j, None).start()
        tap_local = pltpu.make_async_copy(tap_in, tap_out.at[me], local_sems.at[2 * n])
        tap_local.start()
        tap_copies = []
        for j, (px, py) in enumerate(chips):
            cp = pltpu.make_async_remote_copy(src_ref=tap_in, dst_ref=tap_out.at[me], send_sem=send_sems.at[6 * n + j], recv_sem=recv_sems.at[6 * n + j],
                                              device_id=(px, py, c), device_id_type=_MESH)
            cp.start()
            tap_copies.append(cp)
        for a in range(n):
            for j, (px, py) in enumerate(chips):
                ici(a, j, 2 * px + py).wait_recv()
                d2d(a, j, c).start()
        for a in range(n):
            for j in range(3):
                d2d(a, j, 1 - c).wait_recv()
        for a in range(n):
            for j in range(3):
                ici(a, j, None).wait_send()
                d2d(a, j, c).wait_send()
        for j, (px, py) in enumerate(chips):
            pltpu.make_async_remote_copy(src_ref=tap_in, dst_ref=tap_out.at[2 * px + py], send_sem=send_sems.at[6 * n + j], recv_sem=recv_sems.at[6 * n + j],
                                         device_id=(px, py, c), device_id_type=_MESH).wait()
        for cp in local:
            cp.wait()
        tap_local.wait()

    res = pl.pallas_call(
        body, name=name,
        in_specs=[_ANY] * (n + 1), out_specs=[_ANY] * (n + 1),
        out_shape=[jax.ShapeDtypeStruct((4, 2, s.shape[0] // 2, s.shape[1]), s.dtype) for s in shards]
        + [jax.ShapeDtypeStruct((4,) + taps.shape, taps.dtype)],
        scratch_shapes=[pltpu.SemaphoreType.DMA((6 * n + 3,)), pltpu.SemaphoreType.DMA((6 * n + 3,)), pltpu.SemaphoreType.DMA((2 * n + 1,))],
    )(*shards, taps)
    return [w.reshape((4,) + s.shape) for w, s in zip(res[:n], shards)], res[n]


def _swap_halves(pieces, name):
    n = len(pieces)
    halves = [p.shape[1] // 2 for p in pieces]

    def body(*refs):
        ins, own, other = refs[:n], refs[n:2 * n], refs[2 * n:3 * n]
        send_sems, recv_sems, local_sems = refs[3 * n:]
        x, y, c = _my_place()

        def half_of(a, which):
            return ins[a].at[pl.ds(0, 4), pl.ds(pl.multiple_of(which * halves[a], 16), halves[a])]

        def to_sibling(a):
            return pltpu.make_async_remote_copy(src_ref=half_of(a, 1 - c), dst_ref=other[a], send_sem=send_sems.at[a], recv_sem=recv_sems.at[a],
                                                device_id=(x, y, 1 - c), device_id_type=_MESH)

        local = []
        for a in range(n):
            cp = pltpu.make_async_copy(half_of(a, c), own[a], local_sems.at[a])
            cp.start()
            local.append(cp)
            to_sibling(a).start()
        for a in range(n):
            to_sibling(a).wait()
            local[a].wait()

    shapes = [jax.ShapeDtypeStruct((4, p.shape[1] // 2, p.shape[2]), p.dtype) for p in pieces]
    res = pl.pallas_call(
        body, name=name,
        in_specs=[_ANY] * n, out_specs=[_ANY] * (2 * n), out_shape=shapes + shapes,
        scratch_shapes=[pltpu.SemaphoreType.DMA((n,)), pltpu.SemaphoreType.DMA((n,)), pltpu.SemaphoreType.DMA((n,))],
    )(*pieces)
    return res[:n], res[n:]


_VMEM = pl.BlockSpec(memory_space=pltpu.VMEM)
_EX_ROWS = 32


def _pair_reduce(p, name):
    s, r, c = p.shape
    half = r // 2

    def body(p_ref, o_ref, other, send_sem, recv_sem):
        x, y, cc = _my_place()
        theirs = pl.multiple_of((1 - cc) * half, 16)
        mine = pl.multiple_of(cc * half, 16)
        cp = pltpu.make_async_remote_copy(src_ref=p_ref.at[pl.ds(0, s), pl.ds(theirs, half)], dst_ref=other, send_sem=send_sem, recv_sem=recv_sem,
                                          device_id=(x, y, 1 - cc), device_id_type=_MESH)
        cp.start()
        cp.wait()

        def step(i, carry):
            r0 = pl.multiple_of(i * _EX_ROWS, _EX_ROWS)
            for slot in range(s):
                own_rows = pl.ds(pl.multiple_of(mine + r0, 16), _EX_ROWS)
                o_ref[slot, pl.ds(r0, _EX_ROWS), :] = (p_ref[slot, own_rows, :] + other[slot, pl.ds(r0, _EX_ROWS), :]).astype(bf16)
            return carry

        lax.fori_loop(0, half // _EX_ROWS, step, 0)

    return pl.pallas_call(
        body, name=name, in_specs=[_VMEM], out_specs=_VMEM,
        out_shape=jax.ShapeDtypeStruct((s, half, c), bf16),
        scratch_shapes=[pltpu.VMEM((s, half, c), f32), pltpu.SemaphoreType.DMA, pltpu.SemaphoreType.DMA],
        compiler_params=pltpu.CompilerParams(vmem_limit_bytes=VMEM_LIMIT),
    )(p)


def _chip_reduce_swap(rcv, name):
    s, h, c = rcv.shape

    def body(r_ref, g_ref, send_sem, recv_sem):
        x, y, cc = _my_place()

        def step(i, carry):
            r0 = pl.multiple_of(i * _EX_ROWS, _EX_ROWS)
            acc = r_ref[0, pl.ds(r0, _EX_ROWS), :].astype(f32)
            for slot in range(1, s):
                acc = acc + r_ref[slot, pl.ds(r0, _EX_ROWS), :].astype(f32)
            g_ref[cc, pl.ds(r0, _EX_ROWS), :] = acc
            return carry

        lax.fori_loop(0, h // _EX_ROWS, step, 0)
        cp = pltpu.make_async_remote_copy(src_ref=g_ref.at[cc], dst_ref=g_ref.at[cc], send_sem=send_sem, recv_sem=recv_sem,
                                          device_id=(x, y, 1 - cc), device_id_type=_MESH)
        cp.start()
        cp.wait()

    return pl.pallas_call(
        body, name=name, in_specs=[_VMEM], out_specs=_VMEM,
        out_shape=jax.ShapeDtypeStruct((2, h, c), f32),
        scratch_shapes=[pltpu.SemaphoreType.DMA, pltpu.SemaphoreType.DMA],
        compiler_params=pltpu.CompilerParams(vmem_limit_bytes=VMEM_LIMIT),
    )(rcv)


def _pair_reduce_cols(p, name):
    s, r, c = p.shape
    hc = c // 2

    def body(p_ref, o_ref, other, send_sem, recv_sem):
        x, y, cc = _my_place()

        def run(mine_lo, theirs_lo):
            cp = pltpu.make_async_remote_copy(src_ref=p_ref.at[pl.ds(0, s), pl.ds(0, r), pl.ds(theirs_lo, hc)], dst_ref=other,
                                              send_sem=send_sem, recv_sem=recv_sem, device_id=(x, y, 1 - cc), device_id_type=_MESH)
            cp.start()
            cp.wait()
            for slot in range(s):
                o_ref[slot] = (p_ref[slot, :, mine_lo:mine_lo + hc] + other[slot]).astype(bf16)

        @pl.when(cc == 0)
        def _():
            run(0, hc)

        @pl.when(cc == 1)
        def _():
            run(hc, 0)

    return pl.pallas_call(
        body, name=name, in_specs=[_VMEM], out_specs=_VMEM,
        out_shape=jax.ShapeDtypeStruct((s, r, hc), bf16),
        scratch_shapes=[pltpu.VMEM((s, r, hc), f32), pltpu.SemaphoreType.DMA, pltpu.SemaphoreType.DMA],
        compiler_params=pltpu.CompilerParams(vmem_limit_bytes=VMEM_LIMIT),
    )(p)


def _chip_reduce_swap_cols(rcv, name):
    s, r, hc = rcv.shape

    def body(r_ref, g_ref, send_sem, recv_sem):
        x, y, cc = _my_place()
        acc = r_ref[0].astype(f32)
        for slot in range(1, s):
            acc = acc + r_ref[slot].astype(f32)
        g_ref[cc] = acc
        cp = pltpu.make_async_remote_copy(src_ref=g_ref.at[cc], dst_ref=g_ref.at[cc], send_sem=send_sem, recv_sem=recv_sem,
                                          device_id=(x, y, 1 - cc), device_id_type=_MESH)
        cp.start()
        cp.wait()

    both = pl.pallas_call(
        body, name=name, in_specs=[_VMEM], out_specs=_VMEM,
        out_shape=jax.ShapeDtypeStruct((2, r, hc), f32),
        scratch_shapes=[pltpu.SemaphoreType.DMA, pltpu.SemaphoreType.DMA],
        compiler_params=pltpu.CompilerParams(vmem_limit_bytes=VMEM_LIMIT),
    )(rcv)
    return both.transpose(1, 0, 2).reshape(r, 2 * hc)


def _reduce_adamw(rcv, w, m, v, name):
    s, r, c = rcv.shape
    rows_per = _EX_ROWS

    def body(r_ref, w_ref, m_ref, v_ref, g_ref, d_ref, nm_ref, nv_ref, mine, theirs, send_sem, recv_sem):
        x, y, cc = _my_place()

        def chip_sum(i, carry):
            rs = pl.ds(pl.multiple_of(i * rows_per, rows_per), rows_per)
            acc = r_ref[0, rs, :].astype(f32)
            for slot in range(1, s):
                acc = acc + r_ref[slot, rs, :].astype(f32)
            mine[rs, :] = acc
            return carry

        lax.fori_loop(0, r // rows_per, chip_sum, 0)
        cp = pltpu.make_async_remote_copy(src_ref=mine, dst_ref=theirs, send_sem=send_sem, recv_sem=recv_sem,
                                          device_id=(x, y, 1 - cc), device_id_type=_MESH)
        cp.start()
        cp.wait()

        def update(i, carry):
            rs = pl.ds(pl.multiple_of(i * rows_per, rows_per), rows_per)
            g = mine[rs, :] + theirs[rs, :]
            nm = B1 * m_ref[rs, :] + (1.0 - B1) * g
            nv = B2 * v_ref[rs, :] + (1.0 - B2) * (g * g)
            g_ref[rs, :] = g
            nm_ref[rs, :] = nm
            nv_ref[rs, :] = nv
            d_ref[rs, :] = -LR * ((nm / (1.0 - B1 ** STEP)) / (jnp.sqrt(nv / (1.0 - B2 ** STEP)) + EPS_ADAM) + WD * w_ref[rs, :])
            return carry

        lax.fori_loop(0, r // rows_per, update, 0)

    return pl.pallas_call(
        body, name=name, in_specs=[_VMEM] * 4, out_specs=[_VMEM] * 4,
        out_shape=[jax.ShapeDtypeStruct((r, c), f32)] * 4,
        scratch_shapes=[pltpu.VMEM((r, c), f32), pltpu.VMEM((r, c), f32), pltpu.SemaphoreType.DMA, pltpu.SemaphoreType.DMA],
        compiler_params=pltpu.CompilerParams(vmem_limit_bytes=VMEM_LIMIT),
    )(rcv, w, m, v)


def _add_cast(a, b, name):
    s, r, c = a.shape
    tr = _row_tile(r, c)

    def body(a_ref, b_ref, o_ref):
        o_ref[...] = (a_ref[...] + b_ref[...]).astype(bf16)

    blk = pl.BlockSpec((1, tr, c), lambda i, j: (i, j, 0))
    return pl.pallas_call(
        body, name=name, grid=(s, r // tr), in_specs=[blk, blk], out_specs=blk,
        out_shape=jax.ShapeDtypeStruct(a.shape, bf16),
        compiler_params=_cparams(2, arbitrary=False),
    )(a, b)


def _gather_copies(ins, outs, send_sems, recv_sems, local_sems):
    x, y, c = _my_place()
    me = 2 * x + y
    local, outgoing, incoming = [], [], []
    for a in range(len(ins)):
        local.append(pltpu.make_async_copy(ins[a], outs[a].at[me], local_sems.at[a]))
        for j, (px, py) in enumerate(_other_chips(x, y)):
            sems = dict(send_sem=send_sems.at[3 * a + j], recv_sem=recv_sems.at[3 * a + j], device_id=(px, py, c), device_id_type=_MESH)
            outgoing.append(pltpu.make_async_remote_copy(src_ref=ins[a], dst_ref=outs[a].at[me], **sems))
            incoming.append(pltpu.make_async_remote_copy(src_ref=ins[a], dst_ref=outs[a].at[2 * px + py], **sems))
    return local, outgoing, incoming


def _gather_chips(blocks, name):
    n = len(blocks)

    def body(*refs):
        local, outgoing, incoming = _gather_copies(refs[:n], refs[n:2 * n], *refs[2 * n:])
        for cp in local + outgoing:
            cp.start()
        for cp in incoming:
            cp.wait_recv()
        for cp in outgoing:
            cp.wait_send()
        for cp in local:
            cp.wait()

    return pl.pallas_call(
        body, name=name, in_specs=[_ANY] * n, out_specs=[_ANY] * n, out_shape=_gather_shapes(blocks),
        scratch_shapes=[pltpu.SemaphoreType.DMA((3 * n,)), pltpu.SemaphoreType.DMA((3 * n,)), pltpu.SemaphoreType.DMA((n,))],
    )(*blocks)


def _gather_first(block, taps, name):
    r, c = block.shape
    hc = c // 2

    def body(in_ref, tap_in, out_ref, tap_out, send_sems, recv_sems):
        x, y, cc = _my_place()
        me = 2 * x + y
        sibling = (x, y, 1 - cc)
        chips = _other_chips(x, y)
        out_ref[me] = in_ref[...]
        tap_out[me] = tap_in[...]

        def run(mine, theirs):
            def ici(j, chip):
                px, py = chips[j]
                src = in_ref.at[pl.ds(0, r), pl.ds(mine, hc)] if chip is None else out_ref.at[chip, pl.ds(0, r), pl.ds(mine, hc)]
                dst = out_ref.at[me if chip is None else chip, pl.ds(0, r), pl.ds(mine, hc)]
                return pltpu.make_async_remote_copy(src_ref=src, dst_ref=dst, send_sem=send_sems.at[j], recv_sem=recv_sems.at[j],
                                                    device_id=(px, py, cc), device_id_type=_MESH)

            def d2d(j, lo):
                px, py = chips[j]
                blk = out_ref.at[2 * px + py, pl.ds(0, r), pl.ds(lo, hc)]
                return pltpu.make_async_remote_copy(src_ref=blk, dst_ref=blk, send_sem=send_sems.at[3 + j], recv_sem=recv_sems.at[3 + j],
                                                    device_id=sibling, device_id_type=_MESH)

            def tap(j, chip):
                px, py = chips[j]
                return pltpu.make_async_remote_copy(src_ref=tap_in, dst_ref=tap_out.at[me if chip is None else chip],
                                                    send_sem=send_sems.at[6 + j], recv_sem=recv_sems.at[6 + j],
                                                    device_id=(px, py, cc), device_id_type=_MESH)

            for j in range(3):
                ici(j, None).start()
                tap(j, None).start()
            for j, (px, py) in enumerate(chips):
                ici(j, 2 * px + py).wait_recv()
                d2d(j, mine).start()
            for j, (px, py) in enumerate(chips):
                d2d(j, theirs).wait_recv()
                tap(j, 2 * px + py).wait_recv()
            for j in range(3):
                ici(j, None).wait_send()
                d2d(j, mine).wait_send()
                tap(j, None).wait_send()

        @pl.when(cc == 0)
        def _():
            run(0, hc)

        @pl.when(cc == 1)
        def _():
            run(hc, 0)

    return pl.pallas_call(
        body, name=name, in_specs=[_VMEM, _VMEM], out_specs=[_VMEM, _VMEM],
        out_shape=[jax.ShapeDtypeStruct((4, r, c), block.dtype), jax.ShapeDtypeStruct((4,) + taps.shape, taps.dtype)],
        scratch_shapes=[pltpu.SemaphoreType.DMA((9,)), pltpu.SemaphoreType.DMA((9,))],
        compiler_params=pltpu.CompilerParams(vmem_limit_bytes=VMEM_LIMIT),
    )(block, taps)


def _riding_call(body, name, nsteps, in_specs, out_specs, out_shape, scratch_shapes, operands, riders, copies, ride_shapes):
    nr, n_in, n_out, n_scr = len(riders), len(in_specs), len(out_specs), len(scratch_shapes)

    def wrapped(*refs):
        ins, ride_in = refs[:n_in], refs[n_in:n_in + nr]
        outs, ride_out = refs[n_in + nr:n_in + nr + n_out], refs[n_in + nr + n_out:n_in + 2 * nr + n_out]
        scratch, sems = refs[n_in + 2 * nr + n_out:n_in + 2 * nr + n_out + n_scr], refs[n_in + 2 * nr + n_out + n_scr:]
        if nr:
            @pl.when(pl.program_id(0) == 0)
            def _():
                local, outgoing, _ = copies(ride_in, ride_out, *sems)
                for cp in local + outgoing:
                    cp.start()

        body(*ins, *outs, *scratch)
        if nr:
            @pl.when(pl.program_id(0) == nsteps - 1)
            def _():
                local, outgoing, incoming = copies(ride_in, ride_out, *sems)
                for cp in incoming:
                    cp.wait_recv()
                for cp in outgoing:
                    cp.wait_send()
                for cp in local:
                    cp.wait()

    hbm = pl.BlockSpec(memory_space=pltpu.HBM)
    sems = [pltpu.SemaphoreType.DMA((3 * nr,)), pltpu.SemaphoreType.DMA((3 * nr,)), pltpu.SemaphoreType.DMA((nr,))] if nr else []
    res = pl.pallas_call(
        wrapped, name=name, grid=(nsteps,),
        in_specs=list(in_specs) + [hbm] * nr, out_specs=list(out_specs) + [hbm] * nr,
        out_shape=list(out_shape) + list(ride_shapes),
        scratch_shapes=list(scratch_shapes) + sems,
        compiler_params=_cparams(1),
    )(*operands, *riders)
    return list(res[:n_out]), list(res[n_out:])


def _gather_shapes(riders):
    return [jax.ShapeDtypeStruct((4,) + r.shape, r.dtype) for r in riders]


def _scatter_copies(ins, outs, send_sems, recv_sems, local_sems):
    x, y, c = _my_place()
    me = 2 * x + y
    local, outgoing, incoming = [], [], []
    for a in range(len(ins)):
        local.append(pltpu.make_async_copy(ins[a].at[me], outs[a].at[me], local_sems.at[a]))
        for j, (px, py) in enumerate(_other_chips(x, y)):
            sems = dict(send_sem=send_sems.at[3 * a + j], recv_sem=recv_sems.at[3 * a + j], device_id=(px, py, c), device_id_type=_MESH)
            outgoing.append(pltpu.make_async_remote_copy(src_ref=ins[a].at[2 * px + py], dst_ref=outs[a].at[me], **sems))
            incoming.append(pltpu.make_async_remote_copy(src_ref=ins[a].at[2 * px + py], dst_ref=outs[a].at[2 * px + py], **sems))
    return local, outgoing, incoming


def _scatter_start(ins, outs, send_sems, recv_sems, local_sems):
    local, outgoing, _ = _scatter_copies(ins, outs, send_sems, recv_sems, local_sems)
    for cp in local + outgoing:
        cp.start()


def _scatter_wait(ins, outs, send_sems, recv_sems, local_sems):
    local, outgoing, incoming = _scatter_copies(ins, outs, send_sems, recv_sems, local_sems)
    for cp in incoming:
        cp.wait_recv()
    for cp in outgoing:
        cp.wait_send()
    for cp in local:
        cp.wait()


def _scatter_chips(pieces, name):
    n = len(pieces)

    def body(*refs):
        ins, outs = refs[:n], refs[n:2 * n]
        _scatter_start(ins, outs, *refs[2 * n:])
        _scatter_wait(ins, outs, *refs[2 * n:])

    return pl.pallas_call(
        body, name=name,
        in_specs=[_ANY] * n, out_specs=[_ANY] * n,
        out_shape=[jax.ShapeDtypeStruct(s.shape, s.dtype) for s in pieces],
        scratch_shapes=[pltpu.SemaphoreType.DMA((3 * n,)), pltpu.SemaphoreType.DMA((3 * n,)), pltpu.SemaphoreType.DMA((n,))],
    )(*pieces)


def _swap_cores(blocks, name):
    n = len(blocks)
    parts = 4
    rows = [b.shape[0] // parts for b in blocks]

    def body(*refs):
        ins, outs = refs[:n], refs[n:2 * n]
        send_sems, recv_sems, local_sems = refs[2 * n:]
        x, y, c = _my_place()

        def remote(a, k, slot):
            rs = pl.ds(k * rows[a], rows[a])
            return pltpu.make_async_remote_copy(src_ref=ins[a].at[rs], dst_ref=outs[a].at[slot, rs], send_sem=send_sems.at[parts * a + k],
                                                recv_sem=recv_sems.at[parts * a + k], device_id=(x, y, 1 - c), device_id_type=_MESH)

        local = []
        for a in range(n):
            cp = pltpu.make_async_copy(ins[a], outs[a].at[c], local_sems.at[a])
            cp.start()
            local.append(cp)
            for k in range(parts):
                remote(a, k, c).start()
        for a in range(n):
            for k in range(parts):
                remote(a, k, 1 - c).wait()
            local[a].wait()

    return pl.pallas_call(
        body, name=name,
        in_specs=[_ANY] * n, out_specs=[_ANY] * n,
        out_shape=[jax.ShapeDtypeStruct((2,) + s.shape, s.dtype) for s in blocks],
        scratch_shapes=[pltpu.SemaphoreType.DMA((parts * n,)), pltpu.SemaphoreType.DMA((parts * n,)), pltpu.SemaphoreType.DMA((n,))],
    )(*blocks)


def _gather_all(block, name):
    def body(in_ref, out_ref, send_sems, recv_sems, local_sem):
        x, y, c = _my_place()
        me = 4 * x + 2 * y + c
        cp = pltpu.make_async_copy(in_ref, out_ref.at[me], local_sem)
        cp.start()
        peers = []
        for dx in range(2):
            for dy in range(2):
                for dc in range(2):
                    if dx or dy or dc:
                        peers.append((1 - x if dx else x, 1 - y if dy else y, 1 - c if dc else c))
        for j, pr in enumerate(peers):
            pltpu.make_async_remote_copy(src_ref=in_ref, dst_ref=out_ref.at[me], send_sem=send_sems.at[j], recv_sem=recv_sems.at[j],
                                         device_id=pr, device_id_type=_MESH).start()
        for j, (px, py, pc) in enumerate(peers):
            pltpu.make_async_remote_copy(src_ref=in_ref, dst_ref=out_ref.at[4 * px + 2 * py + pc], send_sem=send_sems.at[j], recv_sem=recv_sems.at[j],
                                         device_id=(px, py, pc), device_id_type=_MESH).wait()
        cp.wait()

    return pl.pallas_call(
        body, name=name,
        in_specs=[_ANY], out_specs=_ANY,
        out_shape=jax.ShapeDtypeStruct((8,) + block.shape, block.dtype),
        scratch_shapes=[pltpu.SemaphoreType.DMA((7,)), pltpu.SemaphoreType.DMA((7,)), pltpu.SemaphoreType.DMA],
    )(block)


def _row_tile(r, c):
    best = r
    for cand in range(16, r + 1, 16):
        if r % cand == 0 and cand * c * 4 <= (1 << 20):
            best = cand
    return best if best * c * 4 <= (4 << 20) else r


def _sum_slots(parts, name):
    n, r, c = parts.shape
    tr = _row_tile(r, c)

    def body(p_ref, o_ref):
        acc = p_ref[0].astype(f32)
        for s in range(1, n):
            acc = acc + p_ref[s].astype(f32)
        o_ref[...] = acc

    return pl.pallas_call(
        body, name=name, grid=(r // tr,),
        in_specs=[pl.BlockSpec((n, tr, c), lambda i: (0, i, 0))],
        out_specs=pl.BlockSpec((tr, c), lambda i: (i, 0)),
        out_shape=jax.ShapeDtypeStruct((r, c), f32),
        compiler_params=_cparams(1, arbitrary=False),
    )(parts)


def _adamw(parts, w, m, v, name):
    n, r, c = parts.shape
    tr = _row_tile(r, c)
    tc = c
    if tr == r and r * c * 4 > (1 << 20) and c % 256 == 0:
        tc = 256

    def body(p_ref, w_ref, m_ref, v_ref, g_ref, d_ref, nm_ref, nv_ref):
        g = p_ref[0]
        for s in range(1, n):
            g = g + p_ref[s]
        nm = B1 * m_ref[...] + (1.0 - B1) * g
        nv = B2 * v_ref[...] + (1.0 - B2) * (g * g)
        m_hat = nm / (1.0 - B1 ** STEP)
        v_hat = nv / (1.0 - B2 ** STEP)
        g_ref[...] = g
        nm_ref[...] = nm
        nv_ref[...] = nv
        d_ref[...] = -LR * (m_hat / (jnp.sqrt(v_hat) + EPS_ADAM) + WD * w_ref[...])

    blk = pl.BlockSpec((tr, tc), lambda i, j: (i, j))
    return pl.pallas_call(
        body, name=name, grid=(r // tr, c // tc),
        in_specs=[pl.BlockSpec((n, tr, tc), lambda i, j: (0, i, j)), blk, blk, blk],
        out_specs=[blk] * 4,
        out_shape=[jax.ShapeDtypeStruct((r, c), f32)] * 4,
        compiler_params=_cparams(2, arbitrary=False),
    )(parts, w, m, v)


_BIG = ["w_in", "w_out", "w_ffn_gate", "w_ffn_up", "w_ffn_down", "ple_w_proj", "ple_w_gate"]
_SMALL = ["b_in", "hg_lb_logits", "ml_conv_w", "ml_conv_b", "hg_norm_g", "ml_norm_g", "ln1_g", "ln1_b", "ln2_g", "ln2_b", "ple_b_gate"]
_ORDER = ["w_in", "b_in", "hg_lb_logits", "ml_conv_w", "ml_conv_b", "hg_norm_g", "ml_norm_g", "w_out", "ln1_g", "ln1_b",
          "w_ffn_gate", "w_ffn_up", "w_ffn_down", "ln2_g", "ln2_b", "ple_w_proj", "ple_w_gate", "ple_b_gate"]
_PACK_ROWS, _PACK_COLS = 16, 1024


def _pack(arrays):
    flat = jnp.concatenate([a.reshape(-1) for a in arrays])
    return jnp.pad(flat, (0, _PACK_ROWS * _PACK_COLS - flat.shape[0])).reshape(_PACK_ROWS, _PACK_COLS)


def _unpack(pack, shapes):
    flat = pack.reshape(-1)
    out, off = [], 0
    for s in shapes:
        size = 1
        for d in s:
            size *= d
        out.append(flat[off:off + size].reshape(s))
        off += size
    return out


def _to_chip_major(g, col_split):
    if col_split:
        k, n = g.shape
        return g.reshape(k, 4, n // 4).transpose(1, 0, 2)
    k, n = g.shape
    return g.reshape(4, k // 4, n)


def kernel(x, p, w_in, b_in, hg_lb_logits, ml_conv_w, ml_conv_b, hg_norm_g, ml_norm_g, w_out, ln1_g, ln1_b, w_ffn_gate, w_ffn_up, w_ffn_down, ln2_g, ln2_b, ple_w_proj, ple_w_gate, ple_b_gate, loss_target, m_w_in, m_b_in, m_hg_lb_logits, m_ml_conv_w, m_ml_conv_b, m_hg_norm_g, m_ml_norm_g, m_w_out, m_ln1_g, m_ln1_b, m_w_ffn_gate, m_w_ffn_up, m_w_ffn_down, m_ln2_g, m_ln2_b, m_ple_w_proj, m_ple_w_gate, m_ple_b_gate, v_w_in, v_b_in, v_hg_lb_logits, v_ml_conv_w, v_ml_conv_b, v_hg_norm_g, v_ml_norm_g, v_w_out, v_ln1_g, v_ln1_b, v_w_ffn_gate, v_w_ffn_up, v_w_ffn_down, v_ln2_g, v_ln2_b, v_ple_w_proj, v_ple_w_gate, v_ple_b_gate):
    args = dict(locals())
    wts = {k: args[k] for k in _ORDER}
    mom = {k: args["m_" + k] for k in _ORDER}
    var = {k: args["v_" + k] for k in _ORDER}
    two_d = lambda a: a.reshape(a.shape[-2], a.shape[-1])
    block = lambda k, a: jnp.swapaxes(two_d(a), 0, 1) if k in _TRANSPOSED else two_d(a)
    unblock = lambda k, a: (jnp.swapaxes(a, 0, 1) if k in _TRANSPOSED else a).reshape(wts[k].shape)

    shards = {k: block(k, wts[k]).astype(bf16) for k in _BIG}
    w_in_blocks, taps = _gather_first(shards["w_in"], two_d(ml_conv_w), "gather_w_in")
    w_in_full = _from_chip_major(w_in_blocks, False)
    conv_w_full = _from_chip_major(taps, True)

    def core_sum(k, g):
        pieces = _to_chip_major(g, k in _COL_SPLIT)
        if pieces.shape[1] % (2 * _EX_ROWS):
            return _pair_reduce_cols(pieces, "pair_reduce_" + k)
        return _pair_reduce(pieces, "pair_reduce_" + k)

    early_keys = _BIG[1:]
    loss, grad_x, grads, (got_hg, got_ml) = _local_step(
        x[0], p[0, 0], loss_target[0], w_in_full, b_in, hg_lb_logits, conv_w_full, ml_conv_b, hg_norm_g, ml_norm_g,
        None, ln1_g, ln1_b, None, None, None, ln2_g, ln2_b, None, None, ple_b_gate,
        early_hook=lambda low: tuple([_to_chip_major(low[k], k in _COL_SPLIT) for k in names] for names in _SCATTER_PLAN),
        late_shards={k: shards[k] for k in early_keys})

    out_g, out_d, out_m, out_v = {}, {}, {}, {}

    def finish(k, g, d, nm, nv):
        out_g[k], out_d[k], out_m[k], out_v[k] = unblock(k, g), unblock(k, d), unblock(k, nm), unblock(k, nv)

    for names, got in zip(_SCATTER_PLAN, (got_hg, got_ml)):
        for k, rcv in zip(names, got):
            finish(k, *_reduce_adamw(rcv, block(k, wts[k]), block(k, mom[k]), block(k, var[k]), "reduce_adamw_" + k))

    rcv = _scatter_chips([core_sum("w_in", grads["w_in"])], "scatter_grad_w_in")[0]
    own = block("w_in", wts["w_in"])
    if rcv.shape[1] == own.shape[0]:
        whole = _chip_reduce_swap_cols(rcv, "chip_reduce_w_in")
    else:
        parts = _chip_reduce_swap(rcv, "chip_reduce_w_in")
        whole = parts.reshape(2 * parts.shape[1], parts.shape[2])
    finish("w_in", *_adamw(whole[None], own, block("w_in", mom["w_in"]), block("w_in", var["w_in"]), "adamw_w_in"))

    small_shapes = [(1, PROJ_W), (2, MIX_W), (CONV_K, MIX_W)] + [(1, MIX_W)] * 3 + [(1, D_MODEL)] * 5 + [(1, 1)]
    contrib = _pack([grads[k] for k in _SMALL] + [loss])
    summed = _sum_slots(_gather_all(contrib, "gather_small"), "sum_small")
    small = _unpack(summed, small_shapes)
    loss_total = small[-1].reshape(())
    gsm = dict(zip(_SMALL, small[:-1]))
    place = 2 * lax.axis_index("x") + lax.axis_index("y")
    conv_cols = ml_conv_w.shape[-1]
    gsm["ml_conv_w"] = lax.dynamic_slice(gsm["ml_conv_w"], (0, place * conv_cols), (CONV_K, conv_cols))
    own_shapes = [wts[k].shape for k in _SMALL]
    g_pack = _pack([gsm[k] for k in _SMALL])
    res = _adamw(g_pack[None], _pack([wts[k] for k in _SMALL]), _pack([mom[k] for k in _SMALL]), _pack([var[k] for k in _SMALL]), "adamw_small")
    for dst, pack in zip((out_g, out_d, out_m, out_v), res):
        for k, a in zip(_SMALL, _unpack(pack, own_shapes)):
            dst[k] = a

    outs = [loss_total, grad_x[None]]
    for group in (out_g, out_d, out_m, out_v):
        outs += [group[k] for k in _ORDER]
    return tuple(outs)
```

```python
import functools

import jax
import jax.numpy as jnp
from jax import lax
from jax.experimental import pallas as pl
from jax.experimental.pallas import tpu as pltpu

f32 = jnp.float32
bf16 = jnp.bfloat16
HI = lax.Precision.HIGHEST

D_MODEL = 1024
HEADS = 4
HEAD_W = 128
MIX_W = HEADS * HEAD_W
ML_DQK = 64
PROJ_W = 3592
U_HG = 4 * MIX_W
U_ML = 3 * MIX_W + 128
D_FF = 2816
PLE = 256
CHUNK = 128
SUB = 16
EXP_CAP = 80.0
CONV_K = 4
HALO = 8
ALPHA = float(2.0 ** 0.25)
LN_EPS = 1e-5
RMS_EPS = 1e-6
NEG = -1e30
LR, B1, B2, EPS_ADAM, WD, STEP = 0.001, 0.9, 0.999, 1e-08, 0.01, 10
VMEM_LIMIT = 56 * 1024 * 1024
DENSE_ROWS = 512
WGRAD_ROWS = 2048


def _cparams(n_axes, arbitrary=True):
    sem = ("arbitrary",) * n_axes if arbitrary else ("parallel",) * n_axes
    return pltpu.CompilerParams(dimension_semantics=sem, vmem_limit_bytes=VMEM_LIMIT)


ACT = bf16


def _mx(a):
    return a.astype(ACT)


def _bdot(a, b):
    return jnp.dot(_mx(a), _mx(b), preferred_element_type=f32)


def _bdot_nt(a, b):
    return lax.dot_general(_mx(a), _mx(b), (((1,), (1,)), ((), ())), preferred_element_type=f32)


def _bdot_tn(a, b):
    return lax.dot_general(_mx(a), _mx(b), (((0,), (0,)), ((), ())), preferred_element_type=f32)


def _split3(x):
    hi = x.astype(bf16)
    r1 = x - hi.astype(f32)
    mid = r1.astype(bf16)
    lo = (r1 - mid.astype(f32)).astype(bf16)
    return hi, mid, lo


def _dot3(a, b, dims):
    a_hi = a.astype(bf16)
    a_lo = (a - a_hi.astype(f32)).astype(bf16)
    b_hi = b.astype(bf16)
    b_lo = (b - b_hi.astype(f32)).astype(bf16)
    dn = (dims, ((), ()))
    return (lax.dot_general(a_hi, b_hi, dn, preferred_element_type=f32) + lax.dot_general(a_hi, b_lo, dn, preferred_element_type=f32)
            + lax.dot_general(a_lo, b_hi, dn, preferred_element_type=f32))


def _lane_sum(x):
    hi = x.astype(bf16)
    lo = (x - hi.astype(f32)).astype(bf16)
    ones = jnp.ones((x.shape[1], 128), bf16)
    return jnp.dot(hi, ones, preferred_element_type=f32) + jnp.dot(lo, ones, preferred_element_type=f32)


def _lane_dot(x, row):
    return _dot3(x, jnp.broadcast_to(row, (128, row.shape[1])), ((1,), (1,)))


def _sel_dot(sel, x):
    sb = sel.astype(bf16)
    return sum(jnp.dot(sb, part, preferred_element_type=f32) for part in _split3(x))


def _sel_dot_nt(sel, x):
    sb = sel.astype(bf16)
    return sum(lax.dot_general(sb, part, (((1,), (1,)), ((), ())), preferred_element_type=f32) for part in _split3(x))


def _sigmoid(x):
    return 1.0 / (1.0 + jnp.exp(-x))


def _log_sigmoid(x):
    return jnp.minimum(x, 0.0) - jnp.log(1.0 + jnp.exp(-jnp.abs(x)))


def _tri(n, upper=False):
    r = lax.broadcasted_iota(jnp.int32, (n, n), 0)
    c = lax.broadcasted_iota(jnp.int32, (n, n), 1)
    return (c >= r) if upper else (c <= r)


def _rows(tm, n, col=0):
    return pl.BlockSpec((tm, n), lambda i, _c=col: (i, _c))


def _rows_rev(tm, n, nb, col=0):
    return pl.BlockSpec((tm, n), lambda i, _c=col, _nb=nb: (_nb - 1 - i, _c))


def _const(shape):
    return pl.BlockSpec(shape, lambda i, _n=len(shape): (0,) * _n)


def _resident(shape):
    return pl.BlockSpec(shape, lambda i, _n=len(shape): (0,) * _n, pipeline_mode=pl.Buffered(1))


def _tile(t, want):
    return want if t % want == 0 else t


def _inproj(x, w_hg, w_ml, b_hg, b_ml, riders=()):
    t = x.shape[0]
    tm = _tile(t, DENSE_ROWS)

    def body(x_ref, whg_ref, wml_ref, bhg_ref, bml_ref, uhg_ref, uml_ref, xb_ref):
        xb = _mx(x_ref[...])
        xb_ref[...] = xb
        uhg_ref[...] = _bdot_nt(xb, whg_ref[...]) + bhg_ref[...]
        uml_ref[...] = _bdot_nt(xb, wml_ref[...]) + bml_ref[...]

    return _riding_call(
        body, "inproj", t // tm,
        in_specs=[_rows(tm, D_MODEL), _resident((U_HG, D_MODEL)), _resident((U_ML, D_MODEL)), _const((1, U_HG)), _const((1, U_ML))],
        out_specs=[_rows(tm, U_HG), _rows(tm, U_ML), _rows(tm, D_MODEL)],
        out_shape=[jax.ShapeDtypeStruct((t, U_HG), f32), jax.ShapeDtypeStruct((t, U_ML), f32), jax.ShapeDtypeStruct((t, D_MODEL), ACT)],
        scratch_shapes=[], operands=(x, w_hg, w_ml, b_hg, b_ml), riders=riders, copies=_gather_copies, ride_shapes=_gather_shapes(riders))


def _hg_gates(hq, hf, lb, tri, b=None):
    s = _sigmoid(hf)
    om = 1.0 - lb
    f = lb + om * s
    k = om * (1.0 - s)
    sq = _sigmoid(hq)
    q = hq * sq
    if b is None:
        b = _sel_dot(tri, jnp.log(f))
    return q, sq, s, f, k, b


def _hg_scores(q, k, b, tril_mask, a=None):
    qts, kts, eqs, eks, rows = [], [], [], [], []
    for i in range(CHUNK // SUB):
        lo = i * SUB
        ref = jnp.zeros_like(b[0:1]) if i == 0 else b[lo - 1:lo]
        eq = jnp.exp(b[lo:lo + SUB] - ref)
        ek = jnp.exp(jnp.minimum(ref - b, EXP_CAP))
        qt = q[lo:lo + SUB] * eq
        kt = k * ek
        if a is None:
            rows.append(_bdot_nt(qt, kt))
        qts.append(qt); kts.append(kt); eqs.append(eq); eks.append(ek)
    if a is None:
        a = jnp.where(tril_mask, jnp.concatenate(rows, axis=0), 0.0)
    return a, qts, kts, eqs, eks


def _head_rms(o, gn, on_mxu=False):
    ms = _lane_sum(o * o) * (1.0 / o.shape[1]) if on_mxu else jnp.mean(o * o, axis=-1, keepdims=True)
    rstd = lax.rsqrt(ms + RMS_EPS)
    oh = o * rstd
    return oh, rstd, oh * gn


def _lower_bound(logit_ref):
    lg = logit_ref[...]
    return _sigmoid(lg[0:1] - lg[1:2])


def _hgrn2_fwd(u_hg, logits, gn, riders=()):
    t = u_hg.shape[0]
    tb = _tile(t, 256)
    nc_blk = tb // CHUNK

    def body(u_ref, lg_ref, gn_ref, og_ref, sst_ref, b_ref, a_ref, o_ref, st_ref):
        @pl.when(pl.program_id(0) == 0)
        def _():
            st_ref[...] = jnp.zeros_like(st_ref)

        lb_all = _lower_bound(lg_ref)
        tril_mask = _tri(CHUNK)
        tri = tril_mask.astype(f32)

        def chunk(c, carry):
            r0 = pl.multiple_of(c * CHUNK, CHUNK)
            rows = pl.ds(r0, CHUNK)
            heads = range(HEADS)
            cols = [slice(h * HEAD_W, (h + 1) * HEAD_W) for h in heads]
            hv = [u_ref[rows, 2 * MIX_W + h * HEAD_W:2 * MIX_W + (h + 1) * HEAD_W] for h in heads]
            gts = [_hg_gates(u_ref[rows, h * HEAD_W:(h + 1) * HEAD_W], u_ref[rows, MIX_W + h * HEAD_W:MIX_W + (h + 1) * HEAD_W],
                             lb_all[:, cols[h]], tri) for h in heads]
            q = [g[0] for g in gts]
            k = [g[4] for g in gts]
            b = [g[5] for g in gts]
            a = [_hg_scores(q[h], k[h], b[h], tril_mask)[0] for h in heads]
            st = [st_ref[h] for h in heads]
            bl = [b[h][CHUNK - 1:CHUNK] for h in heads]
            o = [_bdot(a[h], hv[h]) + _bdot_nt(q[h] * jnp.exp(b[h]), st[h]) for h in heads]
            new_st = [st[h] * jnp.exp(bl[h]) + _bdot_tn(hv[h], k[h] * jnp.exp(bl[h] - b[h])) for h in heads]
            for h in heads:
                sst_ref[c, h] = st[h]
                st_ref[h] = new_st[h]
                b_ref[rows, cols[h]] = b[h]
                a_ref[rows, cols[h]] = a[h].astype(ACT)
                o_ref[rows, cols[h]] = o[h]
                hgate = u_ref[rows, 3 * MIX_W + h * HEAD_W:3 * MIX_W + (h + 1) * HEAD_W]
                _, _, y = _head_rms(o[h], gn_ref[:, cols[h]])
                og_ref[rows, cols[h]] = (y * (hgate * _sigmoid(hgate))).astype(ACT)
            return carry

        lax.fori_loop(0, nc_blk, chunk, 0, unroll=True)

    assert CHUNK == HEAD_W
    return _riding_call(
        body, "hgrn2_fwd", t // tb,
        in_specs=[_rows(tb, U_HG), _const((2, MIX_W)), _const((1, MIX_W))],
        out_specs=[_rows(tb, MIX_W), pl.BlockSpec((nc_blk, HEADS, HEAD_W, HEAD_W), lambda i: (i, 0, 0, 0)),
                   _rows(tb, MIX_W), _rows(tb, MIX_W), _rows(tb, MIX_W)],
        out_shape=[jax.ShapeDtypeStruct((t, MIX_W), ACT), jax.ShapeDtypeStruct((t // CHUNK, HEADS, HEAD_W, HEAD_W), f32),
                   jax.ShapeDtypeStruct((t, MIX_W), f32), jax.ShapeDtypeStruct((t, MIX_W), ACT), jax.ShapeDtypeStruct((t, MIX_W), f32)],
        scratch_shapes=[pltpu.VMEM((HEADS, HEAD_W, HEAD_W), f32)],
        operands=(u_hg, logits, gn), riders=riders, copies=_gather_copies, ride_shapes=_gather_shapes(riders))


def _hgrn2_bwd(u_hg, logits, gn, sst, bcum, scores, o_raw, dog, riders=()):
    t = u_hg.shape[0]
    tb = _tile(t, 256)
    nb = t // tb
    nc_blk = tb // CHUNK

    def body(u_ref, lg_ref, gn_ref, sst_ref, b_ref, a_ref, o_ref, dog_ref, du_ref, dlg_ref, dgn_ref, dst_ref):
        @pl.when(pl.program_id(0) == 0)
        def _():
            dst_ref[...] = jnp.zeros_like(dst_ref)
            dlg_ref[...] = jnp.zeros_like(dlg_ref)
            dgn_ref[...] = jnp.zeros_like(dgn_ref)

        lb_all = _lower_bound(lg_ref)
        tril_mask = _tri(CHUNK)
        tri = tril_mask.astype(f32)
        triu = _tri(CHUNK, upper=True).astype(f32)

        def chunk(j, carry):
            c = nc_blk - 1 - j
            r0 = pl.multiple_of(c * CHUNK, CHUNK)
            rows = pl.ds(r0, CHUNK)
            heads = range(HEADS)
            nsub = CHUNK // SUB
            cols = [slice(h * HEAD_W, (h + 1) * HEAD_W) for h in heads]
            hq = [u_ref[rows, h * HEAD_W:(h + 1) * HEAD_W] for h in heads]
            hf = [u_ref[rows, MIX_W + h * HEAD_W:MIX_W + (h + 1) * HEAD_W] for h in heads]
            hv = [u_ref[rows, 2 * MIX_W + h * HEAD_W:2 * MIX_W + (h + 1) * HEAD_W] for h in heads]
            lb = [lb_all[:, cols[h]] for h in heads]
            gts = [_hg_gates(hq[h], hf[h], lb[h], tri, b=b_ref[rows, cols[h]]) for h in heads]
            q, sq, s, f, k, b = ([g[n] for g in gts] for n in range(6))
            scs = [_hg_scores(q[h], k[h], b[h], tril_mask, a=a_ref[rows, cols[h]]) for h in heads]
            a, qts, kts, eqs, eks = ([sc[n] for sc in scs] for n in range(5))
            st = [sst_ref[c, h] for h in heads]
            dst = [dst_ref[h] for h in heads]
            bl = [b[h][CHUNK - 1:CHUNK] for h in heads]
            eb = [jnp.exp(b[h]) for h in heads]
            qh = [q[h] * eb[h] for h in heads]
            ekl = [jnp.exp(bl[h] - b[h]) for h in heads]
            kh = [k[h] * ekl[h] for h in heads]
            o = [o_ref[rows, cols[h]] for h in heads]
            do = []
            for h in heads:
                hgate = u_ref[rows, 3 * MIX_W + h * HEAD_W:3 * MIX_W + (h + 1) * HEAD_W]
                gnh = gn_ref[:, cols[h]]
                oh, rstd, y = _head_rms(o[h], gnh)
                sg = _sigmoid(hgate)
                dogh = dog_ref[rows, cols[h]]
                dy = dogh * (hgate * sg)
                du_ref[rows, 3 * MIX_W + h * HEAD_W:3 * MIX_W + (h + 1) * HEAD_W] = (dogh * y * (sg * (1.0 + hgate * (1.0 - sg)))).astype(ACT)
                dgn_ref[:, cols[h]] += jnp.sum(dy * oh, axis=0, keepdims=True)
                doh = dy * gnh
                do.append(rstd * (doh - oh * jnp.mean(doh * oh, axis=-1, keepdims=True)))
            da = [jnp.where(tril_mask, _bdot_nt(do[h], hv[h]), 0.0) for h in heads]
            dv = [_bdot_tn(a[h], do[h]) + _bdot_nt(kh[h], dst[h]) for h in heads]
            dq = [_bdot(do[h], st[h]) * eb[h] for h in heads]
            dk = [_bdot(hv[h], dst[h]) * ekl[h] for h in heads]
            d_last = [jnp.sum(k[h] * dk[h], axis=0, keepdims=True) + jnp.exp(bl[h]) * jnp.sum(dst[h] * st[h], axis=0, keepdims=True)
                      for h in heads]
            d_b = [q[h] * dq[h] - k[h] * dk[h] for h in heads]
            dqs = [[] for _ in heads]
            q_dq = [[] for _ in heads]
            for i in range(nsub):
                for h in heads:
                    da_i = _mx(da[h][i * SUB:(i + 1) * SUB])
                    q_r, k_r = _mx(qts[h][i]), _mx(kts[h][i])
                    g_q = jnp.dot(da_i, k_r, preferred_element_type=f32)
                    g_k = lax.dot_general(da_i, q_r, (((0,), (0,)), ((), ())), preferred_element_type=f32)
                    dqs[h].append(g_q * eqs[h][i])
                    q_dq[h].append(q_r.astype(f32) * g_q)
                    dk[h] = dk[h] + g_k * eks[h][i]
                    d_b[h] = d_b[h] - k_r.astype(f32) * g_k
            for h in heads:
                dq[h] = dq[h] + jnp.concatenate(dqs[h], axis=0)
                d_b[h] = d_b[h] + jnp.concatenate(q_dq[h], axis=0)
                dst_ref[h] = dst[h] * jnp.exp(bl[h]) + _bdot_tn(do[h], qh[h])
            dg = [_sel_dot(triu, d_b[h]) + d_last[h] for h in heads]
            for h in heads:
                dfk = dg[h] / f[h] - dk[h]
                du_ref[rows, h * HEAD_W:(h + 1) * HEAD_W] = (dq[h] * (sq[h] * (1.0 + hq[h] * (1.0 - sq[h])))).astype(ACT)
                du_ref[rows, MIX_W + h * HEAD_W:MIX_W + (h + 1) * HEAD_W] = ((1.0 - lb[h]) * dfk * s[h] * (1.0 - s[h])).astype(ACT)
                du_ref[rows, 2 * MIX_W + h * HEAD_W:2 * MIX_W + (h + 1) * HEAD_W] = dv[h].astype(ACT)
                dlb = jnp.sum((1.0 - s[h]) * dfk, axis=0, keepdims=True) * (lb[h] * (1.0 - lb[h]))
                dlg_ref[0:1, cols[h]] += dlb
                dlg_ref[1:2, cols[h]] -= dlb
            return carry

        lax.fori_loop(0, nc_blk, chunk, 0, unroll=True)

    rev = _rows_rev(tb, MIX_W, nb)
    return _riding_call(
        body, "hgrn2_bwd", nb,
        in_specs=[_rows_rev(tb, U_HG, nb), _const((2, MIX_W)), _const((1, MIX_W)),
                  pl.BlockSpec((nc_blk, HEADS, HEAD_W, HEAD_W), lambda i: (nb - 1 - i, 0, 0, 0)), rev, rev, rev, rev],
        out_specs=[_rows_rev(tb, U_HG, nb), _const((2, MIX_W)), _const((1, MIX_W))],
        out_shape=[jax.ShapeDtypeStruct((t, U_HG), ACT), jax.ShapeDtypeStruct((2, MIX_W), f32), jax.ShapeDtypeStruct((1, MIX_W), f32)],
        scratch_shapes=[pltpu.VMEM((HEADS, HEAD_W, HEAD_W), f32)],
        operands=(u_hg, logits, gn, sst, bcum, scores, o_raw, dog), riders=riders, copies=_scatter_copies,
        ride_shapes=[jax.ShapeDtypeStruct(r.shape, r.dtype) for r in riders])


def _conv_fwd(u_ml, w, b):
    t = u_ml.shape[0]
    tm = _tile(t, 512)

    def body(x_ref, w_ref, b_ref, pre_ref, act_ref, xbuf):
        @pl.when(pl.program_id(0) == 0)
        def _():
            xbuf[...] = jnp.zeros_like(xbuf)

        xbuf[0:HALO, :] = xbuf[tm:tm + HALO, :]
        xbuf[HALO:HALO + tm, :] = x_ref[...]
        pre = b_ref[...] + jnp.zeros((tm, MIX_W), f32)
        for kk in range(CONV_K):
            off = HALO - (CONV_K - 1) + kk
            pre = pre + w_ref[kk:kk + 1, :] * xbuf[off:off + tm, :]
        pre_ref[...] = pre
        act_ref[...] = pre * _sigmoid(pre)

    return pl.pallas_call(
        body, name="conv_fwd", grid=(t // tm,),
        in_specs=[_rows(tm, MIX_W), _const((CONV_K, MIX_W)), _const((1, MIX_W))],
        out_specs=[_rows(tm, MIX_W), _rows(tm, MIX_W)],
        out_shape=[jax.ShapeDtypeStruct((t, MIX_W), f32)] * 2,
        scratch_shapes=[pltpu.VMEM((tm + HALO, MIX_W), f32)],
        compiler_params=_cparams(1),
    )(u_ml, w, b)


def _conv_bwd(u_ml, w, pre, dact):
    t = u_ml.shape[0]
    tm = _tile(t, 512)
    nb = t // tm
    hb = tm // HALO

    def body(x_ref, halo_ref, w_ref, pre_ref, dact_ref, dx_ref, dw_ref, db_ref, dbuf, xbuf):
        i = pl.program_id(0)

        @pl.when(i == 0)
        def _():
            dbuf[...] = jnp.zeros_like(dbuf)
            dw_ref[...] = jnp.zeros_like(dw_ref)
            db_ref[...] = jnp.zeros_like(db_ref)

        p = pre_ref[...]
        sg = _sigmoid(p)
        dpre = dact_ref[...] * (sg * (1.0 + p * (1.0 - sg)))
        dbuf[tm:tm + HALO, :] = dbuf[0:HALO, :]
        dbuf[0:tm, :] = dpre
        has_prev = (i < nb - 1).astype(f32)
        xbuf[0:HALO, :] = halo_ref[...] * has_prev
        xbuf[HALO:HALO + tm, :] = x_ref[...]
        dx = jnp.zeros((tm, MIX_W), f32)
        for kk in range(CONV_K):
            back = CONV_K - 1 - kk
            dx = dx + w_ref[kk:kk + 1, :] * dbuf[back:back + tm, :]
            off = HALO - (CONV_K - 1) + kk
            dw_ref[kk:kk + 1, :] += jnp.sum(dpre * xbuf[off:off + tm, :], axis=0, keepdims=True)
        dx_ref[...] = dx.astype(ACT)
        db_ref[...] += jnp.sum(dpre, axis=0, keepdims=True)

    return pl.pallas_call(
        body, name="conv_bwd", grid=(nb,),
        in_specs=[_rows_rev(tm, MIX_W, nb),
                  pl.BlockSpec((HALO, MIX_W), lambda i: (jnp.maximum((nb - 1 - i) * hb - 1, 0), 0)),
                  _const((CONV_K, MIX_W)), _rows_rev(tm, MIX_W, nb), _rows_rev(tm, MIX_W, nb)],
        out_specs=[_rows_rev(tm, MIX_W, nb), _const((CONV_K, MIX_W)), _const((1, MIX_W))],
        out_shape=[jax.ShapeDtypeStruct((t, MIX_W), ACT), jax.ShapeDtypeStruct((CONV_K, MIX_W), f32), jax.ShapeDtypeStruct((1, MIX_W), f32)],
        scratch_shapes=[pltpu.VMEM((tm + HALO, MIX_W), f32), pltpu.VMEM((tm + HALO, MIX_W), f32)],
        compiler_params=_cparams(1),
    )(u_ml, u_ml, w, pre, dact)


def _lane_pick(x, lane):
    idx = lax.broadcasted_iota(jnp.int32, x.shape, 1)
    return jnp.sum(jnp.where(idx == lane, x, 0.0), axis=-1, keepdims=True)


def _ml_gate_forms(gates, tri):
    lf = _log_sigmoid(gates)
    gc = _sel_dot(tri, lf)
    lane = lax.broadcasted_iota(jnp.int32, gates.shape, 1)
    mixed = jnp.where(lane < HEADS, gates, gc)
    sel = (lax.broadcasted_iota(jnp.int32, (8, 128), 0) == lax.broadcasted_iota(jnp.int32, (8, 128), 1)).astype(f32)
    rowsf = _sel_dot_nt(sel, mixed)
    return gc, rowsf


def _ml_chunk(q, k, v, gates, gc, rowsf, c_st, n_st, m_st, tril_mask):
    hs = range(HEADS)
    g_col = [_lane_pick(gc, HEADS + h) for h in hs]
    ig_col = [_lane_pick(gates, h) for h in hs]
    dmat = [jnp.where(tril_mask, g_col[h] - rowsf[HEADS + h:HEADS + h + 1, :] + rowsf[h:h + 1, :], NEG) for h in hs]
    m_inter = [g_col[h] + m_st[h] for h in hs]
    m_t = [jnp.maximum(m_inter[h], jnp.max(dmat[h], axis=-1, keepdims=True)) for h in hs]
    wi = [jnp.exp(dmat[h] - m_t[h]) for h in hs]
    wo = [jnp.exp(m_inter[h] - m_t[h]) for h in hs]
    qk = [_bdot_nt(q[h], k[h]) * wi[h] for h in hs]
    num = [_bdot(qk[h], v[h]) + wo[h] * _bdot(q[h], c_st[h]) for h in hs]
    den = [_lane_sum(qk[h]) + wo[h] * _lane_dot(q[h], n_st[h]) for h in hs]
    floor = [jnp.exp(-m_t[h]) for h in hs]
    z = [jnp.maximum(jnp.abs(den[h]), floor[h]) for h in hs]
    g_last = [g_col[h][CHUNK - 1:CHUNK] for h in hs]
    a_col = [g_last[h] - g_col[h] + ig_col[h] for h in hs]
    m_new = [jnp.maximum(g_last[h] + m_st[h], jnp.max(a_col[h], axis=0, keepdims=True)) for h in hs]
    ws = [jnp.exp(a_col[h] - m_new[h]) for h in hs]
    w_old = [jnp.exp(g_last[h] + m_st[h] - m_new[h]) for h in hs]
    return dict(wi=wi, wo=wo, qk=qk, num=num, den=den, z=z, floor=floor, ws=ws, w_old=w_old, m_new=m_new)


def _mlstm_fwd(qkc, u_ml, gn, riders=()):
    t = qkc.shape[0]
    tb = _tile(t, 256)
    nc_blk = tb // CHUNK

    def body(qk_ref, v_ref, mo_ref, gt_ref, gn_ref, og_ref, cst_ref, nst_ref, mst_ref, c_sc, n_sc, m_sc):
        @pl.when(pl.program_id(0) == 0)
        def _():
            c_sc[...] = jnp.zeros_like(c_sc)
            n_sc[...] = jnp.zeros_like(n_sc)
            m_sc[...] = jnp.zeros_like(m_sc)

        tril_mask = _tri(CHUNK)
        tri = tril_mask.astype(f32)

        def chunk(c, carry):
            r0 = pl.multiple_of(c * CHUNK, CHUNK)
            rows = pl.ds(r0, CHUNK)
            gates = gt_ref[rows, :]
            gc, rowsf = _ml_gate_forms(gates, tri)
            hs = range(HEADS)
            q = [qk_ref[rows, h * ML_DQK:(h + 1) * ML_DQK] * (ML_DQK ** -0.5) for h in hs]
            k = [qk_ref[rows, HEADS * ML_DQK + h * ML_DQK:HEADS * ML_DQK + (h + 1) * ML_DQK] for h in hs]
            v = [v_ref[rows, h * HEAD_W:(h + 1) * HEAD_W] for h in hs]
            c_st = [c_sc[h] for h in hs]
            n_st = [n_sc[h] for h in hs]
            m_full = [m_sc[h] for h in hs]
            r = _ml_chunk(q, k, v, gates, gc, rowsf, c_st, n_st, [m[:, 0:1] for m in m_full], tril_mask)
            ksc = [k[h] * r["ws"][h] for h in hs]
            new_c = [r["w_old"][h] * c_st[h] + _bdot_tn(ksc[h], v[h]) for h in hs]
            for h in hs:
                cs = slice(h * HEAD_W, (h + 1) * HEAD_W)
                cst_ref[c, h] = c_st[h]
                nst_ref[c, h] = n_st[h]
                mst_ref[c, h] = m_full[h]
                c_sc[h] = new_c[h]
                n_sc[h] = r["w_old"][h] * n_st[h] + jnp.sum(ksc[h], axis=0, keepdims=True)
                m_sc[h] = r["m_new"][h] + jnp.zeros((1, 128), f32)
                _, _, y = _head_rms(r["num"][h] / r["z"][h], gn_ref[:, cs], on_mxu=True)
                og_ref[rows, cs] = (y * _sigmoid(mo_ref[rows, h * HEAD_W:(h + 1) * HEAD_W])).astype(ACT)
            return carry

        lax.fori_loop(0, nc_blk, chunk, 0)

    nchunks = t // CHUNK
    return _riding_call(
        body, "mlstm_fwd", t // tb,
        in_specs=[_rows(tb, MIX_W), _rows(tb, MIX_W, 1), _rows(tb, MIX_W, 2), _rows(tb, 128, 12), _const((1, MIX_W))],
        out_specs=[_rows(tb, MIX_W),
                   pl.BlockSpec((nc_blk, HEADS, ML_DQK, HEAD_W), lambda i: (i, 0, 0, 0)),
                   pl.BlockSpec((nc_blk, HEADS, 1, ML_DQK), lambda i: (i, 0, 0, 0)),
                   pl.BlockSpec((nc_blk, HEADS, 1, 128), lambda i: (i, 0, 0, 0))],
        out_shape=[jax.ShapeDtypeStruct((t, MIX_W), ACT),
                   jax.ShapeDtypeStruct((nchunks, HEADS, ML_DQK, HEAD_W), f32),
                   jax.ShapeDtypeStruct((nchunks, HEADS, 1, ML_DQK), f32),
                   jax.ShapeDtypeStruct((nchunks, HEADS, 1, 128), f32)],
        scratch_shapes=[pltpu.VMEM((HEADS, ML_DQK, HEAD_W), f32), pltpu.VMEM((HEADS, 1, ML_DQK), f32), pltpu.VMEM((HEADS, 1, 128), f32)],
        operands=(qkc, u_ml, u_ml, u_ml, gn), riders=riders, copies=_gather_copies, ride_shapes=_gather_shapes(riders))


def _mlstm_bwd(qkc, u_ml, gn, cst, nst, mst, dog, riders=()):
    t = qkc.shape[0]
    tb = _tile(t, 256)
    nb = t // tb
    nc_blk = tb // CHUNK

    def body(qk_ref, v_ref, mo_ref, gt_ref, gn_ref, cst_ref, nst_ref, mst_ref, dog_ref,
             dqk_ref, dv_ref, dmo_ref, dgt_ref, dgn_ref, dc_sc, dn_sc):
        @pl.when(pl.program_id(0) == 0)
        def _():
            dc_sc[...] = jnp.zeros_like(dc_sc)
            dn_sc[...] = jnp.zeros_like(dn_sc)
            dgn_ref[...] = jnp.zeros_like(dgn_ref)

        tril_mask = _tri(CHUNK)
        tri = tril_mask.astype(f32)
        triu = _tri(CHUNK, upper=True).astype(f32)
        lane = lax.broadcasted_iota(jnp.int32, (CHUNK, 128), 1)

        def chunk(j, carry):
            c = nc_blk - 1 - j
            r0 = pl.multiple_of(c * CHUNK, CHUNK)
            rows = pl.ds(r0, CHUNK)
            gates = gt_ref[rows, :]
            gc, rowsf = _ml_gate_forms(gates, tri)
            dg_mat = jnp.zeros((CHUNK, 128), f32)
            dig_mat = jnp.zeros((CHUNK, 128), f32)
            dlast_row = jnp.zeros((1, 128), f32)
            hs = range(HEADS)
            cols = [slice(h * HEAD_W, (h + 1) * HEAD_W) for h in hs]
            q = [qk_ref[rows, h * ML_DQK:(h + 1) * ML_DQK] * (ML_DQK ** -0.5) for h in hs]
            k = [qk_ref[rows, HEADS * ML_DQK + h * ML_DQK:HEADS * ML_DQK + (h + 1) * ML_DQK] for h in hs]
            v = [v_ref[rows, h * HEAD_W:(h + 1) * HEAD_W] for h in hs]
            c_st = [cst_ref[c, h] for h in hs]
            n_st = [nst_ref[c, h] for h in hs]
            m_st = [mst_ref[c, h][:, 0:1] for h in hs]
            dc = [dc_sc[h] for h in hs]
            dn = [dn_sc[h] for h in hs]
            r = _ml_chunk(q, k, v, gates, gc, rowsf, c_st, n_st, m_st, tril_mask)
            z, wi, wo, ws, w_old, den = r["z"], r["wi"], r["wo"], r["ws"], r["w_old"], r["den"]
            hh = [r["num"][h] / z[h] for h in hs]
            dh = []
            for h in hs:
                gnh = gn_ref[:, cols[h]]
                oh, rstd, y = _head_rms(hh[h], gnh, on_mxu=True)
                sg = _sigmoid(mo_ref[rows, h * HEAD_W:(h + 1) * HEAD_W])
                dogh = dog_ref[rows, cols[h]]
                dy = dogh * sg
                dmo_ref[rows, cols[h]] = (dogh * y * (sg * (1.0 - sg))).astype(ACT)
                dgn_ref[:, cols[h]] += jnp.sum(dy * oh, axis=0, keepdims=True)
                doh = dy * gnh
                dh.append(rstd * (doh - oh * (_lane_sum(doh * oh) * (1.0 / HEAD_W))))
            dnum = [dh[h] / z[h] for h in hs]
            dz = [-_lane_sum(dh[h] * hh[h]) / z[h] for h in hs]
            dden = [jnp.where(jnp.abs(den[h]) > r["floor"][h], dz[h] * jnp.sign(den[h]), 0.0) for h in hs]
            dsw = [(_bdot_nt(dnum[h], v[h]) + dden[h]) * wi[h] for h in hs]
            dq = [_bdot(dsw[h], k[h]) + wo[h] * (_bdot_nt(dnum[h], c_st[h]) + dden[h][:, :ML_DQK] * n_st[h]) for h in hs]
            dk_state = [ws[h] * (_bdot_nt(v[h], dc[h]) + dn[h]) for h in hs]
            dk = [_bdot_tn(dsw[h], q[h]) + dk_state[h] for h in hs]
            dv = [_bdot_tn(r["qk"][h], dnum[h]) + ws[h] * _bdot(k[h], dc[h]) for h in hs]
            woq = [wo[h] * q[h] for h in hs]
            new_dc = [w_old[h] * dc[h] + _bdot_tn(woq[h], dnum[h]) for h in hs]
            for h in hs:
                dv_ref[rows, cols[h]] = dv[h].astype(ACT)
                dc_sc[h] = new_dc[h]
                dn_sc[h] = w_old[h] * dn[h] + jnp.sum(woq[h] * dden[h][:, :ML_DQK], axis=0, keepdims=True)
                d_last = (jnp.sum(jnp.sum(k[h] * dk_state[h], axis=0, keepdims=True), axis=-1, keepdims=True)
                          + w_old[h] * (jnp.sum(jnp.sum(dc[h] * c_st[h], axis=0, keepdims=True), axis=-1, keepdims=True)
                                        + jnp.sum(dn[h] * n_st[h], axis=-1, keepdims=True)))
                kdk = _lane_sum(k[h] * dk[h])
                qdq = _lane_sum(q[h] * dq[h])
                dg_mat = dg_mat + jnp.where(lane == HEADS + h, qdq - kdk, 0.0)
                dlast_row = dlast_row + jnp.where(lane[0:1] == HEADS + h, d_last, 0.0)
                dig_mat = dig_mat + jnp.where(lane == h, kdk, 0.0)
                dqk_ref[rows, h * ML_DQK:(h + 1) * ML_DQK] = dq[h] * (ML_DQK ** -0.5)
                dqk_ref[rows, HEADS * ML_DQK + h * ML_DQK:HEADS * ML_DQK + (h + 1) * ML_DQK] = dk[h]
            dlf = _sel_dot(triu, dg_mat) + dlast_row
            dgt_ref[rows, :] = (dig_mat + dlf * _sigmoid(-gates)).astype(ACT)
            return carry

        lax.fori_loop(0, nc_blk, chunk, 0)

    st4 = lambda a, b: pl.BlockSpec((nc_blk, HEADS, a, b), lambda i: (nb - 1 - i, 0, 0, 0))
    return _riding_call(
        body, "mlstm_bwd", nb,
        in_specs=[_rows_rev(tb, MIX_W, nb), _rows_rev(tb, MIX_W, nb, 1), _rows_rev(tb, MIX_W, nb, 2), _rows_rev(tb, 128, nb, 12),
                  _const((1, MIX_W)), st4(ML_DQK, HEAD_W), st4(1, ML_DQK), st4(1, 128), _rows_rev(tb, MIX_W, nb)],
        out_specs=[_rows_rev(tb, MIX_W, nb), _rows_rev(tb, MIX_W, nb), _rows_rev(tb, MIX_W, nb), _rows_rev(tb, 128, nb), _const((1, MIX_W))],
        out_shape=[jax.ShapeDtypeStruct((t, MIX_W), f32), jax.ShapeDtypeStruct((t, MIX_W), ACT), jax.ShapeDtypeStruct((t, MIX_W), ACT),
                   jax.ShapeDtypeStruct((t, 128), ACT), jax.ShapeDtypeStruct((1, MIX_W), f32)],
        scratch_shapes=[pltpu.VMEM((HEADS, ML_DQK, HEAD_W), f32), pltpu.VMEM((HEADS, 1, ML_DQK), f32)],
        operands=(qkc, u_ml, u_ml, u_ml, gn, cst, nst, mst, dog), riders=riders, copies=_scatter_copies,
        ride_shapes=[jax.ShapeDtypeStruct(r.shape, r.dtype) for r in riders])


def _ln_fwd(r, g, b):
    mu = jnp.mean(r, axis=-1, keepdims=True)
    xc = r - mu
    rstd = lax.rsqrt(jnp.mean(xc * xc, axis=-1, keepdims=True) + LN_EPS)
    xh = xc * rstd
    return xh * g + b, xh, rstd


def _ln_bwd(dy, xh, rstd, g):
    dxh = dy * g
    return rstd * (dxh - jnp.mean(dxh, axis=-1, keepdims=True) - xh * jnp.mean(dxh * xh, axis=-1, keepdims=True))


def _outproj_ln1(og_hg, og_ml, x, w_out, g, b, riders=()):
    t = x.shape[0]
    tm = _tile(t, DENSE_ROWS)

    def body(a_ref, b_ref, x_ref, w_ref, g_ref, bb_ref, x1_ref, xh_ref, rs_ref, x1b_ref):
        mix = _bdot(a_ref[...], w_ref[0:MIX_W, :]) + _bdot(b_ref[...], w_ref[MIX_W:2 * MIX_W, :])
        y, xh, rstd = _ln_fwd(ALPHA * x_ref[...] + mix, g_ref[...], bb_ref[...])
        x1_ref[...] = y
        x1b_ref[...] = y.astype(ACT)
        xh_ref[...] = xh.astype(ACT)
        rs_ref[...] = rstd

    return _riding_call(
        body, "outproj_ln1", t // tm,
        in_specs=[_rows(tm, MIX_W), _rows(tm, MIX_W), _rows(tm, D_MODEL), _resident((D_MODEL, D_MODEL)), _const((1, D_MODEL)), _const((1, D_MODEL))],
        out_specs=[_rows(tm, D_MODEL), _rows(tm, D_MODEL), _rows(tm, 1), _rows(tm, D_MODEL)],
        out_shape=[jax.ShapeDtypeStruct((t, D_MODEL), f32), jax.ShapeDtypeStruct((t, D_MODEL), ACT), jax.ShapeDtypeStruct((t, 1), f32),
                   jax.ShapeDtypeStruct((t, D_MODEL), ACT)],
        scratch_shapes=[], operands=(og_hg, og_ml, x, w_out, g, b), riders=riders, copies=_gather_copies, ride_shapes=_gather_shapes(riders))


def _ffn_up(x1, wg, wu, riders=()):
    t = x1.shape[0]
    tm = _tile(t, DENSE_ROWS)

    def body(x_ref, wg_ref, wu_ref, hg_ref, up_ref, a_ref):
        xv = x_ref[...]
        hg = _bdot_nt(xv, wg_ref[...])
        up = _bdot_nt(xv, wu_ref[...])
        hg_ref[...] = hg.astype(ACT)
        up_ref[...] = up.astype(ACT)
        a_ref[...] = (hg * _sigmoid(hg) * up).astype(ACT)

    return _riding_call(
        body, "ffn_up", t // tm,
        in_specs=[_rows(tm, D_MODEL), _resident((D_FF, D_MODEL)), _resident((D_FF, D_MODEL))],
        out_specs=[_rows(tm, D_FF), _rows(tm, D_FF), _rows(tm, D_FF)],
        out_shape=[jax.ShapeDtypeStruct((t, D_FF), ACT), jax.ShapeDtypeStruct((t, D_FF), ACT), jax.ShapeDtypeStruct((t, D_FF), ACT)],
        scratch_shapes=[], operands=(x1, wg, wu), riders=riders, copies=_gather_copies, ride_shapes=_gather_shapes(riders))


def _ffn_down_ln2(a, x1, wd, g, b):
    t = x1.shape[0]
    tm = _tile(t, DENSE_ROWS)

    def body(a_ref, x_ref, w_ref, g_ref, bb_ref, x2_ref, xh_ref, rs_ref, x2b_ref):
        ffn = _bdot(a_ref[...], w_ref[...])
        y, xh, rstd = _ln_fwd(ALPHA * x_ref[...] + ffn, g_ref[...], bb_ref[...])
        x2_ref[...] = y
        x2b_ref[...] = y.astype(ACT)
        xh_ref[...] = xh.astype(ACT)
        rs_ref[...] = rstd

    return pl.pallas_call(
        body, name="ffn_down_ln2", grid=(t // tm,),
        in_specs=[_rows(tm, D_FF), _rows(tm, D_MODEL), _resident((D_FF, D_MODEL)), _const((1, D_MODEL)), _const((1, D_MODEL))],
        out_specs=[_rows(tm, D_MODEL), _rows(tm, D_MODEL), _rows(tm, 1), _rows(tm, D_MODEL)],
        out_shape=[jax.ShapeDtypeStruct((t, D_MODEL), f32), jax.ShapeDtypeStruct((t, D_MODEL), ACT), jax.ShapeDtypeStruct((t, 1), f32),
                   jax.ShapeDtypeStruct((t, D_MODEL), ACT)],
        compiler_params=_cparams(1, arbitrary=False),
    )(a, x1, wd, g, b)


def _head_loss_bwd(x2, xh2, rs2, p, tgt, w_pg, b_pg, w_pp, g2):
    t = x2.shape[0]
    tm = _tile(t, DENSE_ROWS)

    def body(x_ref, xh_ref, rs_ref, p_ref, t_ref, wg_ref, bg_ref, wp_ref, g_ref,
             dr_ref, de_ref, dz_ref, loss_ref, dbg_ref, dg2_ref, db2_ref):
        @pl.when(pl.program_id(0) == 0)
        def _():
            loss_ref[...] = jnp.zeros_like(loss_ref)
            dbg_ref[...] = jnp.zeros_like(dbg_ref)
            dg2_ref[...] = jnp.zeros_like(dg2_ref)
            db2_ref[...] = jnp.zeros_like(db2_ref)

        x2v = x_ref[...]
        z = _bdot(x2v, wg_ref[...]) + bg_ref[...]
        e = _bdot(p_ref[...], wp_ref[...])
        sg = _sigmoid(z)
        diff = x2v + sg * e - t_ref[...]
        loss_ref[...] += 0.5 * jnp.sum(jnp.mean(diff * diff, axis=-1, keepdims=True), axis=0, keepdims=True)
        dy = diff * (1.0 / D_MODEL)
        de_ref[...] = (dy * sg).astype(ACT)
        dz = dy * e * (sg * (1.0 - sg))
        dz_ref[...] = dz.astype(ACT)
        dbg_ref[...] += jnp.sum(dz, axis=0, keepdims=True)
        dx2 = dy + _bdot_nt(dz, wg_ref[...])
        xh = xh_ref[...].astype(f32)
        dg2_ref[...] += jnp.sum(dx2 * xh, axis=0, keepdims=True)
        db2_ref[...] += jnp.sum(dx2, axis=0, keepdims=True)
        dr_ref[...] = _ln_bwd(dx2, xh, rs_ref[...], g_ref[...])

    row = jax.ShapeDtypeStruct((1, D_MODEL), f32)
    return pl.pallas_call(
        body, name="head_loss_bwd", grid=(t // tm,),
        in_specs=[_rows(tm, D_MODEL), _rows(tm, D_MODEL), _rows(tm, 1), _rows(tm, PLE), _rows(tm, D_MODEL),
                  _resident((D_MODEL, D_MODEL)), _const((1, D_MODEL)), _resident((PLE, D_MODEL)), _const((1, D_MODEL))],
        out_specs=[_rows(tm, D_MODEL), _rows(tm, D_MODEL), _rows(tm, D_MODEL), _const((1, 1)), _const((1, D_MODEL)), _const((1, D_MODEL)), _const((1, D_MODEL))],
        out_shape=[jax.ShapeDtypeStruct((t, D_MODEL), f32), jax.ShapeDtypeStruct((t, D_MODEL), ACT), jax.ShapeDtypeStruct((t, D_MODEL), ACT),
                   jax.ShapeDtypeStruct((1, 1), f32), row, row, row],
        compiler_params=_cparams(1),
    )(x2, xh2, rs2, p, tgt, w_pg, b_pg, w_pp, g2)


def _ffn_bwd(dr2, hg, up, xh1, rs1, wd, wg, wu, g1, w_out):
    t = dr2.shape[0]
    tm = _tile(t, DENSE_ROWS // 2)

    def body(dr_ref, hg_ref, up_ref, xh_ref, rs_ref, wd_ref, wg_ref, wu_ref, g_ref, wo_ref,
             dr1_ref, dhg_ref, dup_ref, dg1_ref, db1_ref, doghg_ref, dogml_ref):
        @pl.when(pl.program_id(0) == 0)
        def _():
            dg1_ref[...] = jnp.zeros_like(dg1_ref)
            db1_ref[...] = jnp.zeros_like(db1_ref)

        dr2v = dr_ref[...]
        da = _bdot_nt(dr2v, wd_ref[...])
        hgv = hg_ref[...].astype(f32)
        sg = _sigmoid(hgv)
        dhg = da * up_ref[...].astype(f32) * (sg * (1.0 + hgv * (1.0 - sg)))
        dup = da * (hgv * sg)
        dhg_ref[...] = dhg.astype(ACT)
        dup_ref[...] = dup.astype(ACT)
        dx1 = ALPHA * dr2v + _bdot(dhg, wg_ref[...]) + _bdot(dup, wu_ref[...])
        xh = xh_ref[...].astype(f32)
        dg1_ref[...] += jnp.sum(dx1 * xh, axis=0, keepdims=True)
        db1_ref[...] += jnp.sum(dx1, axis=0, keepdims=True)
        dr1 = _ln_bwd(dx1, xh, rs_ref[...], g_ref[...])
        dr1_ref[...] = dr1
        dog = _bdot_nt(dr1, wo_ref[...])
        doghg_ref[...] = dog[:, 0:MIX_W]
        dogml_ref[...] = dog[:, MIX_W:2 * MIX_W]

    row = jax.ShapeDtypeStruct((1, D_MODEL), f32)
    return pl.pallas_call(
        body, name="ffn_bwd", grid=(t // tm,),
        in_specs=[_rows(tm, D_MODEL), _rows(tm, D_FF), _rows(tm, D_FF), _rows(tm, D_MODEL), _rows(tm, 1),
                  _resident((D_FF, D_MODEL)), _resident((D_FF, D_MODEL)), _resident((D_FF, D_MODEL)), _const((1, D_MODEL)),
                  _resident((D_MODEL, D_MODEL))],
        out_specs=[_rows(tm, D_MODEL), _rows(tm, D_FF), _rows(tm, D_FF), _const((1, D_MODEL)), _const((1, D_MODEL)),
                   _rows(tm, MIX_W), _rows(tm, MIX_W)],
        out_shape=[jax.ShapeDtypeStruct((t, D_MODEL), f32), jax.ShapeDtypeStruct((t, D_FF), ACT), jax.ShapeDtypeStruct((t, D_FF), ACT), row, row,
                   jax.ShapeDtypeStruct((t, MIX_W), f32), jax.ShapeDtypeStruct((t, MIX_W), f32)],
        compiler_params=_cparams(1),
    )(dr2, hg, up, xh1, rs1, wd, wg, wu, g1, w_out)


def _inproj_bwd(dr1, du_hg, dqk, dmv, dmo, dgt, w_hg, w_ml):
    t = dr1.shape[0]
    tm = _tile(t, DENSE_ROWS)

    def body(dr_ref, dhg_ref, dqk_ref, dmv_ref, dmo_ref, dgt_ref, whg_ref, wml_ref, gx_ref, dml_ref):
        dml = jnp.concatenate([dqk_ref[...], dmv_ref[...], dmo_ref[...], dgt_ref[...]], axis=-1).astype(ACT)
        dml_ref[...] = dml
        gx_ref[...] = ALPHA * dr_ref[...] + _bdot(dhg_ref[...], whg_ref[...]) + _bdot(dml, wml_ref[...])

    return pl.pallas_call(
        body, name="inproj_bwd", grid=(t // tm,),
        in_specs=[_rows(tm, D_MODEL), _rows(tm, U_HG), _rows(tm, MIX_W), _rows(tm, MIX_W), _rows(tm, MIX_W), _rows(tm, 128),
                  _resident((U_HG, D_MODEL)), _resident((U_ML, D_MODEL))],
        out_specs=[_rows(tm, D_MODEL), _rows(tm, U_ML)],
        out_shape=[jax.ShapeDtypeStruct((t, D_MODEL), f32), jax.ShapeDtypeStruct((t, U_ML), ACT)],
        compiler_params=_cparams(1, arbitrary=False),
    )(dr1, du_hg, dqk, dmv, dmo, dgt, w_hg, w_ml)


def _wgrad(a, b, name, tk=None, tn=None, colsum=False, low=False):
    t, kdim = a.shape
    n = b.shape[1]
    tk = tk or kdim
    tn = tn or n
    tt = _tile(t, WGRAD_ROWS)
    nt = t // tt
    assert not (colsum and low) and (not colsum or tn == n)

    def body(a_ref, b_ref, o_ref, *s_ref):
        @pl.when(pl.program_id(2) == 0)
        def _():
            o_ref[...] = jnp.zeros_like(o_ref)
            if colsum:
                s_ref[0][...] = jnp.zeros_like(s_ref[0])

        av = a_ref[...]
        o_ref[...] += _bdot_tn(av, b_ref[...])
        if colsum:
            s_ref[0][...] += jnp.sum(av.astype(f32), axis=0, keepdims=True)
        if low:
            @pl.when(pl.program_id(2) == nt - 1)
            def _():
                s_ref[0][...] = o_ref[...].astype(bf16)

    out_specs = [pl.BlockSpec((tk, tn), lambda i, j, s: (i, j))]
    out_shape = [jax.ShapeDtypeStruct((kdim, n), f32)]
    if colsum:
        out_specs.append(pl.BlockSpec((1, tk), lambda i, j, s: (0, i)))
        out_shape.append(jax.ShapeDtypeStruct((1, kdim), f32))
    if low:
        out_specs.append(pl.BlockSpec((tk, tn), lambda i, j, s: (i, j)))
        out_shape.append(jax.ShapeDtypeStruct((kdim, n), bf16))
    res = pl.pallas_call(
        body, name=name, grid=(kdim // tk, n // tn, t // tt),
        in_specs=[pl.BlockSpec((tt, tk), lambda i, j, s: (s, i)), pl.BlockSpec((tt, tn), lambda i, j, s: (s, j))],
        out_specs=out_specs, out_shape=out_shape,
        compiler_params=_cparams(3),
    )(a, b)
    return res if (colsum or low) else res[0]


def _colsum(parts, name):
    t = parts[0].shape[0]
    tt = _tile(t, 512)
    widths = [a.shape[1] for a in parts]

    def body(*refs):
        o_ref = refs[-1]

        @pl.when(pl.program_id(0) == 0)
        def _():
            o_ref[...] = jnp.zeros_like(o_ref)

        off = 0
        for r, w in zip(refs[:-1], widths):
            o_ref[:, off:off + w] += jnp.sum(r[...].astype(f32), axis=0, keepdims=True)
            off += w

    return pl.pallas_call(
        body, name=name, grid=(t // tt,),
        in_specs=[_rows(tt, w) for w in widths],
        out_specs=_const((1, sum(widths))),
        out_shape=jax.ShapeDtypeStruct((1, sum(widths)), f32),
        compiler_params=_cparams(1),
    )(*parts)


_TRANSPOSED = {"w_in", "w_ffn_gate", "w_ffn_up"}
_COL_SPLIT = {"ple_w_proj"}
_SCATTER_PLAN = (("w_ffn_gate", "w_ffn_up", "w_out"), ("w_ffn_down", "ple_w_gate", "ple_w_proj"))
_RIDE_PLAN = {"inproj": ("w_ffn_gate",), "hgrn2_fwd": ("w_ffn_up",), "mlstm_fwd": ("w_out",),
              "outproj_ln1": ("ple_w_gate", "ple_w_proj"), "ffn_up": ("w_ffn_down",)}


def _from_chip_major(a, col_split):
    if col_split:
        return a.transpose(1, 0, 2).reshape(a.shape[1], 4 * a.shape[2])
    return a.reshape(4 * a.shape[1], a.shape[2])


def _local_step(x, p, tgt, w_in_b, b_in, logits, conv_w, conv_b, hg_gn, ml_gn, w_out_b, ln1_g, ln1_b,
                wg_b, wu_b, wd_b, ln2_g, ln2_b, w_pp_b, w_pg_b, b_pg, early_hook=None, late_shards=None):
    pad_w = U_HG + U_ML - PROJ_W
    w_hg = w_in_b[:U_HG]
    w_ml = jnp.pad(w_in_b[U_HG:], ((0, pad_w), (0, 0)))
    bb_hg = b_in[:, :U_HG]
    bb_ml = jnp.pad(b_in[:, U_HG:], ((0, 0), (0, pad_w)))

    late = dict(w_out=w_out_b, w_ffn_gate=wg_b, w_ffn_up=wu_b, w_ffn_down=wd_b, ple_w_proj=w_pp_b, ple_w_gate=w_pg_b)

    def riders_of(call):
        return [late_shards[k] for k in _RIDE_PLAN[call]] if late_shards is not None else ()

    def arrived(call, got):
        for k, g in zip(_RIDE_PLAN[call], got):
            late[k] = _from_chip_major(g, k in _COL_SPLIT)

    (u_hg, u_ml, xb), got = _inproj(x, w_hg, w_ml, bb_hg, bb_ml, riders_of("inproj"))
    arrived("inproj", got)
    (og_hg, sst, hg_b, hg_a, hg_o), got = _hgrn2_fwd(u_hg, logits, hg_gn, riders_of("hgrn2_fwd"))
    arrived("hgrn2_fwd", got)
    pre, qkc = _conv_fwd(u_ml, conv_w, conv_b)
    (og_ml, cst, nst, mst), got = _mlstm_fwd(qkc, u_ml, ml_gn, riders_of("mlstm_fwd"))
    arrived("mlstm_fwd", got)
    (x1, xh1, rs1, x1b), got = _outproj_ln1(og_hg, og_ml, x, late["w_out"], ln1_g, ln1_b, riders_of("outproj_ln1"))
    arrived("outproj_ln1", got)
    (hgp, up, act), got = _ffn_up(x1b, late["w_ffn_gate"], late["w_ffn_up"], riders_of("ffn_up"))
    arrived("ffn_up", got)
    w_out_b, wg_b, wu_b, wd_b = late["w_out"], late["w_ffn_gate"], late["w_ffn_up"], late["w_ffn_down"]
    w_pp_b, w_pg_b = late["ple_w_proj"], late["ple_w_gate"]
    x2, xh2, rs2, x2b = _ffn_down_ln2(act, x1, wd_b, ln2_g, ln2_b)
    dr2, de, dz, loss, d_bpg, d_ln2g, d_ln2b = _head_loss_bwd(x2, xh2, rs2, p, tgt, w_pg_b, b_pg, w_pp_b, ln2_g)
    dr1, dhg, dup, d_ln1g, d_ln1b, dog_hg, dog_ml = _ffn_bwd(dr2, hgp, up, xh1, rs1, wd_b, wg_b, wu_b, ln1_g, w_out_b)

    d_wo_a, lo_wo_a = _wgrad(og_hg, dr1, "wgrad_out_hg", low=True)
    d_wo_b, lo_wo_b = _wgrad(og_ml, dr1, "wgrad_out_ml", low=True)
    d_wg, lo_wg = _wgrad(dhg, x1b, "wgrad_ffn_gate", tk=D_FF // 2, low=True)
    d_wu, lo_wu = _wgrad(dup, x1b, "wgrad_ffn_up", tk=D_FF // 2, low=True)
    d_wd, lo_wd = _wgrad(act, dr2, "wgrad_ffn_down", tk=D_FF // 2, low=True)
    d_wpp, lo_wpp = _wgrad(p, de, "wgrad_ple_proj", low=True)
    d_wpg, lo_wpg = _wgrad(x2b, dz, "wgrad_ple_gate", low=True)
    early = dict(w_out=jnp.concatenate([d_wo_a, d_wo_b], axis=0), w_ffn_gate=d_wg, w_ffn_up=d_wu, w_ffn_down=d_wd,
                 ple_w_proj=d_wpp, ple_w_gate=d_wpg)
    early_low = dict(w_out=jnp.concatenate([lo_wo_a, lo_wo_b], axis=0), w_ffn_gate=lo_wg, w_ffn_up=lo_wu, w_ffn_down=lo_wd,
                     ple_w_proj=lo_wpp, ple_w_gate=lo_wpg)
    ride_hg, ride_ml = early_hook(early_low) if early_hook is not None else ((), ())

    (du_hg, d_logits, d_hg_gn), got_hg = _hgrn2_bwd(u_hg, logits, hg_gn, sst, hg_b, hg_a, hg_o, dog_hg, ride_hg)
    (dqkc, dmv, dmo, dgt, d_ml_gn), got_ml = _mlstm_bwd(qkc, u_ml, ml_gn, cst, nst, mst, dog_ml, ride_ml)
    dqk, d_conv_w, d_conv_b = _conv_bwd(u_ml, conv_w, pre, dqkc)
    grad_x, du_ml = _inproj_bwd(dr1, du_hg, dqk, dmv, dmo, dgt, w_hg, w_ml)

    dw_hg, db_hg = _wgrad(du_hg, xb, "wgrad_in_hg", tk=U_HG // 2, colsum=True)
    dw_ml, db_ml = _wgrad(du_ml, xb, "wgrad_in_ml", colsum=True)
    d_w_in = jnp.concatenate([dw_hg, dw_ml[:PROJ_W - U_HG]], axis=0)
    d_b_in = jnp.concatenate([db_hg, db_ml[:, :PROJ_W - U_HG]], axis=1)

    grads = dict(w_in=d_w_in, b_in=d_b_in, hg_lb_logits=d_logits, ml_conv_w=d_conv_w, ml_conv_b=d_conv_b,
                 hg_norm_g=d_hg_gn, ml_norm_g=d_ml_gn, ln1_g=d_ln1g, ln1_b=d_ln1b, ln2_g=d_ln2g, ln2_b=d_ln2b,
                 ple_b_gate=d_bpg, **early)
    return loss, grad_x, grads, (list(got_hg), list(got_ml))


_ANY = pl.BlockSpec(memory_space=pltpu.HBM)
_MESH = pl.DeviceIdType.MESH


def _my_place():
    return lax.axis_index("x"), lax.axis_index("y"), lax.axis_index("c")


def _other_chips(x, y):
    return [(1 - x, y), (x, 1 - y), (1 - x, 1 - y)]


def _allgather_weights(shards, taps, name):
    n = len(shards)
    halves = [s.shape[0] // 2 for s in shards]

    def body(*refs):
        ins, tap_in = refs[:n], refs[n]
        outs, tap_out = refs[n + 1:2 * n + 1], refs[2 * n + 1]
        send_sems, recv_sems, local_sems = refs[2 * n + 2:]
        x, y, c = _my_place()
        me = 2 * x + y
        sibling = (x, y, 1 - c)
        chips = _other_chips(x, y)

        def ici(a, j, block_chip):
            px, py = chips[j]
            src = ins[a].at[pl.ds(pl.multiple_of(c * halves[a], 16), halves[a])] if block_chip is None else outs[a].at[block_chip, c]
            dst = outs[a].at[me if block_chip is None else block_chip, c]
            return pltpu.make_async_remote_copy(src_ref=src, dst_ref=dst, send_sem=send_sems.at[6 * a + j], recv_sem=recv_sems.at[6 * a + j],
                                                device_id=(px, py, c), device_id_type=_MESH)

        def d2d(a, j, half):
            px, py = chips[j]
            blk = outs[a].at[2 * px + py, half]
            return pltpu.make_async_remote_copy(src_ref=blk, dst_ref=blk, send_sem=send_sems.at[6 * a + 3 + j], recv_sem=recv_sems.at[6 * a + 3 + j],
                                                device_id=sibling, device_id_type=_MESH)

        local = []
        for a in range(n):
            for h in range(2):
                cp = pltpu.make_async_copy(ins[a].at[pl.ds(h * halves[a], halves[a])], outs[a].at[me, h], local_sems.at[2 * a + h])
                cp.start()
                local.append(cp)
            for j in range(3):
                ici(a, j, None).start()
        tap_local = pltpu.make_async_copy(tap_in, tap_out.at[me], local_sems.at[2 * n])
        tap_local.start()
        tap_copies = []
        for j, (px, py) in enumerate(chips):
            cp = pltpu.make_async_remote_copy(src_ref=tap_in, dst_ref=tap_out.at[me], send_sem=send_sems.at[6 * n + j], recv_sem=recv_sems.at[6 * n + j],
                                              device_id=(px, py, c), device_id_type=_MESH)
            cp.start()
            tap_copies.append(cp)
        for a in range(n):
            for j, (px, py) in enumerate(chips):
                ici(a, j, 2 * px + py).wait_recv()
                d2d(a, j, c).start()
        for a in range(n):
            for j in range(3):
                d2d(a, j, 1 - c).wait_recv()
        for a in range(n):
            for j in range(3):
                ici(a, j, None).wait_send()
                d2d(a, j, c).wait_send()
        for j, (px, py) in enumerate(chips):
            pltpu.make_async_remote_copy(src_ref=tap_in, dst_ref=tap_out.at[2 * px + py], send_sem=send_sems.at[6 * n + j], recv_sem=recv_sems.at[6 * n + j],
                                         device_id=(px, py, c), device_id_type=_MESH).wait()
        for cp in local:
            cp.wait()
        tap_local.wait()

    res = pl.pallas_call(
        body, name=name,
        in_specs=[_ANY] * (n + 1), out_specs=[_ANY] * (n + 1),
        out_shape=[jax.ShapeDtypeStruct((4, 2, s.shape[0] // 2, s.shape[1]), s.dtype) for s in shards]
        + [jax.ShapeDtypeStruct((4,) + taps.shape, taps.dtype)],
        scratch_shapes=[pltpu.SemaphoreType.DMA((6 * n + 3,)), pltpu.SemaphoreType.DMA((6 * n + 3,)), pltpu.SemaphoreType.DMA((2 * n + 1,))],
    )(*shards, taps)
    return [w.reshape((4,) + s.shape) for w, s in zip(res[:n], shards)], res[n]


def _swap_halves(pieces, name):
    n = len(pieces)
    halves = [p.shape[1] // 2 for p in pieces]

    def body(*refs):
        ins, own, other = refs[:n], refs[n:2 * n], refs[2 * n:3 * n]
        send_sems, recv_sems, local_sems = refs[3 * n:]
        x, y, c = _my_place()

        def half_of(a, which):
            return ins[a].at[pl.ds(0, 4), pl.ds(pl.multiple_of(which * halves[a], 16), halves[a])]

        def to_sibling(a):
            return pltpu.make_async_remote_copy(src_ref=half_of(a, 1 - c), dst_ref=other[a], send_sem=send_sems.at[a], recv_sem=recv_sems.at[a],
                                                device_id=(x, y, 1 - c), device_id_type=_MESH)

        local = []
        for a in range(n):
            cp = pltpu.make_async_copy(half_of(a, c), own[a], local_sems.at[a])
            cp.start()
            local.append(cp)
            to_sibling(a).start()
        for a in range(n):
            to_sibling(a).wait()
            local[a].wait()

    shapes = [jax.ShapeDtypeStruct((4, p.shape[1] // 2, p.shape[2]), p.dtype) for p in pieces]
    res = pl.pallas_call(
        body, name=name,
        in_specs=[_ANY] * n, out_specs=[_ANY] * (2 * n), out_shape=shapes + shapes,
        scratch_shapes=[pltpu.SemaphoreType.DMA((n,)), pltpu.SemaphoreType.DMA((n,)), pltpu.SemaphoreType.DMA((n,))],
    )(*pieces)
    return res[:n], res[n:]


_VMEM = pl.BlockSpec(memory_space=pltpu.VMEM)
_EX_ROWS = 32


def _pair_reduce(p, name):
    s, r, c = p.shape
    half = r // 2

    def body(p_ref, o_ref, other, send_sem, recv_sem):
        x, y, cc = _my_place()
        theirs = pl.multiple_of((1 - cc) * half, 16)
        mine = pl.multiple_of(cc * half, 16)
        cp = pltpu.make_async_remote_copy(src_ref=p_ref.at[pl.ds(0, s), pl.ds(theirs, half)], dst_ref=other, send_sem=send_sem, recv_sem=recv_sem,
                                          device_id=(x, y, 1 - cc), device_id_type=_MESH)
        cp.start()
        cp.wait()

        def step(i, carry):
            r0 = pl.multiple_of(i * _EX_ROWS, _EX_ROWS)
            for slot in range(s):
                own_rows = pl.ds(pl.multiple_of(mine + r0, 16), _EX_ROWS)
                o_ref[slot, pl.ds(r0, _EX_ROWS), :] = (p_ref[slot, own_rows, :] + other[slot, pl.ds(r0, _EX_ROWS), :]).astype(bf16)
            return carry

        lax.fori_loop(0, half // _EX_ROWS, step, 0)

    return pl.pallas_call(
        body, name=name, in_specs=[_VMEM], out_specs=_VMEM,
        out_shape=jax.ShapeDtypeStruct((s, half, c), bf16),
        scratch_shapes=[pltpu.VMEM((s, half, c), f32), pltpu.SemaphoreType.DMA, pltpu.SemaphoreType.DMA],
        compiler_params=pltpu.CompilerParams(vmem_limit_bytes=VMEM_LIMIT),
    )(p)


def _chip_reduce_swap(rcv, name):
    s, h, c = rcv.shape

    def body(r_ref, g_ref, send_sem, recv_sem):
        x, y, cc = _my_place()

        def step(i, carry):
            r0 = pl.multiple_of(i * _EX_ROWS, _EX_ROWS)
            acc = r_ref[0, pl.ds(r0, _EX_ROWS), :].astype(f32)
            for slot in range(1, s):
                acc = acc + r_ref[slot, pl.ds(r0, _EX_ROWS), :].astype(f32)
            g_ref[cc, pl.ds(r0, _EX_ROWS), :] = acc
            return carry

        lax.fori_loop(0, h // _EX_ROWS, step, 0)
        cp = pltpu.make_async_remote_copy(src_ref=g_ref.at[cc], dst_ref=g_ref.at[cc], send_sem=send_sem, recv_sem=recv_sem,
                                          device_id=(x, y, 1 - cc), device_id_type=_MESH)
        cp.start()
        cp.wait()

    return pl.pallas_call(
        body, name=name, in_specs=[_VMEM], out_specs=_VMEM,
        out_shape=jax.ShapeDtypeStruct((2, h, c), f32),
        scratch_shapes=[pltpu.SemaphoreType.DMA, pltpu.SemaphoreType.DMA],
        compiler_params=pltpu.CompilerParams(vmem_limit_bytes=VMEM_LIMIT),
    )(rcv)


def _pair_reduce_cols(p, name):
    s, r, c = p.shape
    hc = c // 2

    def body(p_ref, o_ref, other, send_sem, recv_sem):
        x, y, cc = _my_place()

        def run(mine_lo, theirs_lo):
            cp = pltpu.make_async_remote_copy(src_ref=p_ref.at[pl.ds(0, s), pl.ds(0, r), pl.ds(theirs_lo, hc)], dst_ref=other,
                                              send_sem=send_sem, recv_sem=recv_sem, device_id=(x, y, 1 - cc), device_id_type=_MESH)
            cp.start()
            cp.wait()
            for slot in range(s):
                o_ref[slot] = (p_ref[slot, :, mine_lo:mine_lo + hc] + other[slot]).astype(bf16)

        @pl.when(cc == 0)
        def _():
            run(0, hc)

        @pl.when(cc == 1)
        def _():
            run(hc, 0)

    return pl.pallas_call(
        body, name=name, in_specs=[_VMEM], out_specs=_VMEM,
        out_shape=jax.ShapeDtypeStruct((s, r, hc), bf16),
        scratch_shapes=[pltpu.VMEM((s, r, hc), f32), pltpu.SemaphoreType.DMA, pltpu.SemaphoreType.DMA],
        compiler_params=pltpu.CompilerParams(vmem_limit_bytes=VMEM_LIMIT),
    )(p)


def _chip_reduce_swap_cols(rcv, name):
    s, r, hc = rcv.shape

    def body(r_ref, g_ref, send_sem, recv_sem):
        x, y, cc = _my_place()
        acc = r_ref[0].astype(f32)
        for slot in range(1, s):
            acc = acc + r_ref[slot].astype(f32)
        g_ref[cc] = acc
        cp = pltpu.make_async_remote_copy(src_ref=g_ref.at[cc], dst_ref=g_ref.at[cc], send_sem=send_sem, recv_sem=recv_sem,
                                          device_id=(x, y, 1 - cc), device_id_type=_MESH)
        cp.start()
        cp.wait()

    both = pl.pallas_call(
        body, name=name, in_specs=[_VMEM], out_specs=_VMEM,
        out_shape=jax.ShapeDtypeStruct((2, r, hc), f32),
        scratch_shapes=[pltpu.SemaphoreType.DMA, pltpu.SemaphoreType.DMA],
        compiler_params=pltpu.CompilerParams(vmem_limit_bytes=VMEM_LIMIT),
    )(rcv)
    return both.transpose(1, 0, 2).reshape(r, 2 * hc)


def _reduce_adamw(rcv, w, m, v, name):
    s, r, c = rcv.shape
    rows_per = _EX_ROWS

    def body(r_ref, w_ref, m_ref, v_ref, g_ref, d_ref, nm_ref, nv_ref, mine, theirs, send_sem, recv_sem):
        x, y, cc = _my_place()

        def chip_sum(i, carry):
            rs = pl.ds(pl.multiple_of(i * rows_per, rows_per), rows_per)
            acc = r_ref[0, rs, :].astype(f32)
            for slot in range(1, s):
                acc = acc + r_ref[slot, rs, :].astype(f32)
            mine[rs, :] = acc
            return carry

        lax.fori_loop(0, r // rows_per, chip_sum, 0)
        cp = pltpu.make_async_remote_copy(src_ref=mine, dst_ref=theirs, send_sem=send_sem, recv_sem=recv_sem,
                                          device_id=(x, y, 1 - cc), device_id_type=_MESH)
        cp.start()
        cp.wait()

        def update(i, carry):
            rs = pl.ds(pl.multiple_of(i * rows_per, rows_per), rows_per)
            g = mine[rs, :] + theirs[rs, :]
            nm = B1 * m_ref[rs, :] + (1.0 - B1) * g
            nv = B2 * v_ref[rs, :] + (1.0 - B2) * (g * g)
            g_ref[rs, :] = g
            nm_ref[rs, :] = nm
            nv_ref[rs, :] = nv
            d_ref[rs, :] = -LR * ((nm / (1.0 - B1 ** STEP)) / (jnp.sqrt(nv / (1.0 - B2 ** STEP)) + EPS_ADAM) + WD * w_ref[rs, :])
            return carry

        lax.fori_loop(0, r // rows_per, update, 0)

    return pl.pallas_call(
        body, name=name, in_specs=[_VMEM] * 4, out_specs=[_VMEM] * 4,
        out_shape=[jax.ShapeDtypeStruct((r, c), f32)] * 4,
        scratch_shapes=[pltpu.VMEM((r, c), f32), pltpu.VMEM((r, c), f32), pltpu.SemaphoreType.DMA, pltpu.SemaphoreType.DMA],
        compiler_params=pltpu.CompilerParams(vmem_limit_bytes=VMEM_LIMIT),
    )(rcv, w, m, v)


def _add_cast(a, b, name):
    s, r, c = a.shape
    tr = _row_tile(r, c)

    def body(a_ref, b_ref, o_ref):
        o_ref[...] = (a_ref[...] + b_ref[...]).astype(bf16)

    blk = pl.BlockSpec((1, tr, c), lambda i, j: (i, j, 0))
    return pl.pallas_call(
        body, name=name, grid=(s, r // tr), in_specs=[blk, blk], out_specs=blk,
        out_shape=jax.ShapeDtypeStruct(a.shape, bf16),
        compiler_params=_cparams(2, arbitrary=False),
    )(a, b)


def _gather_copies(ins, outs, send_sems, recv_sems, local_sems):
    x, y, c = _my_place()
    me = 2 * x + y
    local, outgoing, incoming = [], [], []
    for a in range(len(ins)):
        local.append(pltpu.make_async_copy(ins[a], outs[a].at[me], local_sems.at[a]))
        for j, (px, py) in enumerate(_other_chips(x, y)):
            sems = dict(send_sem=send_sems.at[3 * a + j], recv_sem=recv_sems.at[3 * a + j], device_id=(px, py, c), device_id_type=_MESH)
            outgoing.append(pltpu.make_async_remote_copy(src_ref=ins[a], dst_ref=outs[a].at[me], **sems))
            incoming.append(pltpu.make_async_remote_copy(src_ref=ins[a], dst_ref=outs[a].at[2 * px + py], **sems))
    return local, outgoing, incoming


def _gather_chips(blocks, name):
    n = len(blocks)

    def body(*refs):
        local, outgoing, incoming = _gather_copies(refs[:n], refs[n:2 * n], *refs[2 * n:])
        for cp in local + outgoing:
            cp.start()
        for cp in incoming:
            cp.wait_recv()
        for cp in outgoing:
            cp.wait_send()
        for cp in local:
            cp.wait()

    return pl.pallas_call(
        body, name=name, in_specs=[_ANY] * n, out_specs=[_ANY] * n, out_shape=_gather_shapes(blocks),
        scratch_shapes=[pltpu.SemaphoreType.DMA((3 * n,)), pltpu.SemaphoreType.DMA((3 * n,)), pltpu.SemaphoreType.DMA((n,))],
    )(*blocks)


def _gather_first(block, taps, name):
    r, c = block.shape
    hc = c // 2

    def body(in_ref, tap_in, out_ref, tap_out, send_sems, recv_sems):
        x, y, cc = _my_place()
        me = 2 * x + y
        sibling = (x, y, 1 - cc)
        chips = _other_chips(x, y)
        out_ref[me] = in_ref[...]
        tap_out[me] = tap_in[...]

        def run(mine, theirs):
            def ici(j, chip):
                px, py = chips[j]
                src = in_ref.at[pl.ds(0, r), pl.ds(mine, hc)] if chip is None else out_ref.at[chip, pl.ds(0, r), pl.ds(mine, hc)]
                dst = out_ref.at[me if chip is None else chip, pl.ds(0, r), pl.ds(mine, hc)]
                return pltpu.make_async_remote_copy(src_ref=src, dst_ref=dst, send_sem=send_sems.at[j], recv_sem=recv_sems.at[j],
                                                    device_id=(px, py, cc), device_id_type=_MESH)

            def d2d(j, lo):
                px, py = chips[j]
                blk = out_ref.at[2 * px + py, pl.ds(0, r), pl.ds(lo, hc)]
                return pltpu.make_async_remote_copy(src_ref=blk, dst_ref=blk, send_sem=send_sems.at[3 + j], recv_sem=recv_sems.at[3 + j],
                                                    device_id=sibling, device_id_type=_MESH)

            def tap(j, chip):
                px, py = chips[j]
                return pltpu.make_async_remote_copy(src_ref=tap_in, dst_ref=tap_out.at[me if chip is None else chip],
                                                    send_sem=send_sems.at[6 + j], recv_sem=recv_sems.at[6 + j],
                                                    device_id=(px, py, cc), device_id_type=_MESH)

            for j in range(3):
                ici(j, None).start()
                tap(j, None).start()
            for j, (px, py) in enumerate(chips):
                ici(j, 2 * px + py).wait_recv()
                d2d(j, mine).start()
            for j, (px, py) in enumerate(chips):
                d2d(j, theirs).wait_recv()
                tap(j, 2 * px + py).wait_recv()
            for j in range(3):
                ici(j, None).wait_send()
                d2d(j, mine).wait_send()
                tap(j, None).wait_send()

        @pl.when(cc == 0)
        def _():
            run(0, hc)

        @pl.when(cc == 1)
        def _():
            run(hc, 0)

    return pl.pallas_call(
        body, name=name, in_specs=[_VMEM, _VMEM], out_specs=[_VMEM, _VMEM],
        out_shape=[jax.ShapeDtypeStruct((4, r, c), block.dtype), jax.ShapeDtypeStruct((4,) + taps.shape, taps.dtype)],
        scratch_shapes=[pltpu.SemaphoreType.DMA((9,)), pltpu.SemaphoreType.DMA((9,))],
        compiler_params=pltpu.CompilerParams(vmem_limit_bytes=VMEM_LIMIT),
    )(block, taps)


def _riding_call(body, name, nsteps, in_specs, out_specs, out_shape, scratch_shapes, operands, riders, copies, ride_shapes):
    nr, n_in, n_out, n_scr = len(riders), len(in_specs), len(out_specs), len(scratch_shapes)

    def wrapped(*refs):
        ins, ride_in = refs[:n_in], refs[n_in:n_in + nr]
        outs, ride_out = refs[n_in + nr:n_in + nr + n_out], refs[n_in + nr + n_out:n_in + 2 * nr + n_out]
        scratch, sems = refs[n_in + 2 * nr + n_out:n_in + 2 * nr + n_out + n_scr], refs[n_in + 2 * nr + n_out + n_scr:]
        if nr:
            @pl.when(pl.program_id(0) == 0)
            def _():
                local, outgoing, _ = copies(ride_in, ride_out, *sems)
                for cp in local + outgoing:
                    cp.start()

        body(*ins, *outs, *scratch)
        if nr:
            @pl.when(pl.program_id(0) == nsteps - 1)
            def _():
                local, outgoing, incoming = copies(ride_in, ride_out, *sems)
                for cp in incoming:
                    cp.wait_recv()
                for cp in outgoing:
                    cp.wait_send()
                for cp in local:
                    cp.wait()

    hbm = pl.BlockSpec(memory_space=pltpu.HBM)
    sems = [pltpu.SemaphoreType.DMA((3 * nr,)), pltpu.SemaphoreType.DMA((3 * nr,)), pltpu.SemaphoreType.DMA((nr,))] if nr else []
    res = pl.pallas_call(
        wrapped, name=name, grid=(nsteps,),
        in_specs=list(in_specs) + [hbm] * nr, out_specs=list(out_specs) + [hbm] * nr,
        out_shape=list(out_shape) + list(ride_shapes),
        scratch_shapes=list(scratch_shapes) + sems,
        compiler_params=_cparams(1),
    )(*operands, *riders)
    return list(res[:n_out]), list(res[n_out:])


def _gather_shapes(riders):
    return [jax.ShapeDtypeStruct((4,) + r.shape, r.dtype) for r in riders]


def _scatter_copies(ins, outs, send_sems, recv_sems, local_sems):
    x, y, c = _my_place()
    me = 2 * x + y
    local, outgoing, incoming = [], [], []
    for a in range(len(ins)):
        local.append(pltpu.make_async_copy(ins[a].at[me], outs[a].at[me], local_sems.at[a]))
        for j, (px, py) in enumerate(_other_chips(x, y)):
            sems = dict(send_sem=send_sems.at[3 * a + j], recv_sem=recv_sems.at[3 * a + j], device_id=(px, py, c), device_id_type=_MESH)
            outgoing.append(pltpu.make_async_remote_copy(src_ref=ins[a].at[2 * px + py], dst_ref=outs[a].at[me], **sems))
            incoming.append(pltpu.make_async_remote_copy(src_ref=ins[a].at[2 * px + py], dst_ref=outs[a].at[2 * px + py], **sems))
    return local, outgoing, incoming


def _scatter_start(ins, outs, send_sems, recv_sems, local_sems):
    local, outgoing, _ = _scatter_copies(ins, outs, send_sems, recv_sems, local_sems)
    for cp in local + outgoing:
        cp.start()


def _scatter_wait(ins, outs, send_sems, recv_sems, local_sems):
    local, outgoing, incoming = _scatter_copies(ins, outs, send_sems, recv_sems, local_sems)
    for cp in incoming:
        cp.wait_recv()
    for cp in outgoing:
        cp.wait_send()
    for cp in local:
        cp.wait()


def _scatter_chips(pieces, name):
    n = len(pieces)

    def body(*refs):
        ins, outs = refs[:n], refs[n:2 * n]
        _scatter_start(ins, outs, *refs[2 * n:])
        _scatter_wait(ins, outs, *refs[2 * n:])

    return pl.pallas_call(
        body, name=name,
        in_specs=[_ANY] * n, out_specs=[_ANY] * n,
        out_shape=[jax.ShapeDtypeStruct(s.shape, s.dtype) for s in pieces],
        scratch_shapes=[pltpu.SemaphoreType.DMA((3 * n,)), pltpu.SemaphoreType.DMA((3 * n,)), pltpu.SemaphoreType.DMA((n,))],
    )(*pieces)


def _swap_cores(blocks, name):
    n = len(blocks)
    parts = 4
    rows = [b.shape[0] // parts for b in blocks]

    def body(*refs):
        ins, outs = refs[:n], refs[n:2 * n]
        send_sems, recv_sems, local_sems = refs[2 * n:]
        x, y, c = _my_place()

        def remote(a, k, slot):
            rs = pl.ds(k * rows[a], rows[a])
            return pltpu.make_async_remote_copy(src_ref=ins[a].at[rs], dst_ref=outs[a].at[slot, rs], send_sem=send_sems.at[parts * a + k],
                                                recv_sem=recv_sems.at[parts * a + k], device_id=(x, y, 1 - c), device_id_type=_MESH)

        local = []
        for a in range(n):
            cp = pltpu.make_async_copy(ins[a], outs[a].at[c], local_sems.at[a])
            cp.start()
            local.append(cp)
            for k in range(parts):
                remote(a, k, c).start()
        for a in range(n):
            for k in range(parts):
                remote(a, k, 1 - c).wait()
            local[a].wait()

    return pl.pallas_call(
        body, name=name,
        in_specs=[_ANY] * n, out_specs=[_ANY] * n,
        out_shape=[jax.ShapeDtypeStruct((2,) + s.shape, s.dtype) for s in blocks],
        scratch_shapes=[pltpu.SemaphoreType.DMA((parts * n,)), pltpu.SemaphoreType.DMA((parts * n,)), pltpu.SemaphoreType.DMA((n,))],
    )(*blocks)


def _gather_all(block, name):
    def body(in_ref, out_ref, send_sems, recv_sems, local_sem):
        x, y, c = _my_place()
        me = 4 * x + 2 * y + c
        cp = pltpu.make_async_copy(in_ref, out_ref.at[me], local_sem)
        cp.start()
        peers = []
        for dx in range(2):
            for dy in range(2):
                for dc in range(2):
                    if dx or dy or dc:
                        peers.append((1 - x if dx else x, 1 - y if dy else y, 1 - c if dc else c))
        for j, pr in enumerate(peers):
            pltpu.make_async_remote_copy(src_ref=in_ref, dst_ref=out_ref.at[me], send_sem=send_sems.at[j], recv_sem=recv_sems.at[j],
                                         device_id=pr, device_id_type=_MESH).start()
        for j, (px, py, pc) in enumerate(peers):
            pltpu.make_async_remote_copy(src_ref=in_ref, dst_ref=out_ref.at[4 * px + 2 * py + pc], send_sem=send_sems.at[j], recv_sem=recv_sems.at[j],
                                         device_id=(px, py, pc), device_id_type=_MESH).wait()
        cp.wait()

    return pl.pallas_call(
        body, name=name,
        in_specs=[_ANY], out_specs=_ANY,
        out_shape=jax.ShapeDtypeStruct((8,) + block.shape, block.dtype),
        scratch_shapes=[pltpu.SemaphoreType.DMA((7,)), pltpu.SemaphoreType.DMA((7,)), pltpu.SemaphoreType.DMA],
    )(block)


def _row_tile(r, c):
    best = r
    for cand in range(16, r + 1, 16):
        if r % cand == 0 and cand * c * 4 <= (1 << 20):
            best = cand
    return best if best * c * 4 <= (4 << 20) else r


def _sum_slots(parts, name):
    n, r, c = parts.shape
    tr = _row_tile(r, c)

    def body(p_ref, o_ref):
        acc = p_ref[0].astype(f32)
        for s in range(1, n):
            acc = acc + p_ref[s].astype(f32)
        o_ref[...] = acc

    return pl.pallas_call(
        body, name=name, grid=(r // tr,),
        in_specs=[pl.BlockSpec((n, tr, c), lambda i: (0, i, 0))],
        out_specs=pl.BlockSpec((tr, c), lambda i: (i, 0)),
        out_shape=jax.ShapeDtypeStruct((r, c), f32),
        compiler_params=_cparams(1, arbitrary=False),
    )(parts)


def _adamw(parts, w, m, v, name):
    n, r, c = parts.shape
    tr = _row_tile(r, c)
    tc = c
    if tr == r and r * c * 4 > (1 << 20) and c % 256 == 0:
        tc = 256

    def body(p_ref, w_ref, m_ref, v_ref, g_ref, d_ref, nm_ref, nv_ref):
        g = p_ref[0]
        for s in range(1, n):
            g = g + p_ref[s]
        nm = B1 * m_ref[...] + (1.0 - B1) * g
        nv = B2 * v_ref[...] + (1.0 - B2) * (g * g)
        m_hat = nm / (1.0 - B1 ** STEP)
        v_hat = nv / (1.0 - B2 ** STEP)
        g_ref[...] = g
        nm_ref[...] = nm
        nv_ref[...] = nv
        d_ref[...] = -LR * (m_hat / (jnp.sqrt(v_hat) + EPS_ADAM) + WD * w_ref[...])

    blk = pl.BlockSpec((tr, tc), lambda i, j: (i, j))
    return pl.pallas_call(
        body, name=name, grid=(r // tr, c // tc),
        in_specs=[pl.BlockSpec((n, tr, tc), lambda i, j: (0, i, j)), blk, blk, blk],
        out_specs=[blk] * 4,
        out_shape=[jax.ShapeDtypeStruct((r, c), f32)] * 4,
        compiler_params=_cparams(2, arbitrary=False),
    )(parts, w, m, v)


_BIG = ["w_in", "w_out", "w_ffn_gate", "w_ffn_up", "w_ffn_down", "ple_w_proj", "ple_w_gate"]
_SMALL = ["b_in", "hg_lb_logits", "ml_conv_w", "ml_conv_b", "hg_norm_g", "ml_norm_g", "ln1_g", "ln1_b", "ln2_g", "ln2_b", "ple_b_gate"]
_ORDER = ["w_in", "b_in", "hg_lb_logits", "ml_conv_w", "ml_conv_b", "hg_norm_g", "ml_norm_g", "w_out", "ln1_g", "ln1_b",
          "w_ffn_gate", "w_ffn_up", "w_ffn_down", "ln2_g", "ln2_b", "ple_w_proj", "ple_w_gate", "ple_b_gate"]
_PACK_ROWS, _PACK_COLS = 16, 1024


def _pack(arrays):
    flat = jnp.concatenate([a.reshape(-1) for a in arrays])
    return jnp.pad(flat, (0, _PACK_ROWS * _PACK_COLS - flat.shape[0])).reshape(_PACK_ROWS, _PACK_COLS)


def _unpack(pack, shapes):
    flat = pack.reshape(-1)
    out, off = [], 0
    for s in shapes:
        size = 1
        for d in s:
            size *= d
        out.append(flat[off:off + size].reshape(s))
        off += size
    return out


def _to_chip_major(g, col_split):
    if col_split:
        k, n = g.shape
        return g.reshape(k, 4, n // 4).transpose(1, 0, 2)
    k, n = g.shape
    return g.reshape(4, k // 4, n)


def kernel(x, p, w_in, b_in, hg_lb_logits, ml_conv_w, ml_conv_b, hg_norm_g, ml_norm_g, w_out, ln1_g, ln1_b, w_ffn_gate, w_ffn_up, w_ffn_down, ln2_g, ln2_b, ple_w_proj, ple_w_gate, ple_b_gate, loss_target, m_w_in, m_b_in, m_hg_lb_logits, m_ml_conv_w, m_ml_conv_b, m_hg_norm_g, m_ml_norm_g, m_w_out, m_ln1_g, m_ln1_b, m_w_ffn_gate, m_w_ffn_up, m_w_ffn_down, m_ln2_g, m_ln2_b, m_ple_w_proj, m_ple_w_gate, m_ple_b_gate, v_w_in, v_b_in, v_hg_lb_logits, v_ml_conv_w, v_ml_conv_b, v_hg_norm_g, v_ml_norm_g, v_w_out, v_ln1_g, v_ln1_b, v_w_ffn_gate, v_w_ffn_up, v_w_ffn_down, v_ln2_g, v_ln2_b, v_ple_w_proj, v_ple_w_gate, v_ple_b_gate):
    args = dict(locals())
    wts = {k: args[k] for k in _ORDER}
    mom = {k: args["m_" + k] for k in _ORDER}
    var = {k: args["v_" + k] for k in _ORDER}
    two_d = lambda a: a.reshape(a.shape[-2], a.shape[-1])
    block = lambda k, a: jnp.swapaxes(two_d(a), 0, 1) if k in _TRANSPOSED else two_d(a)
    unblock = lambda k, a: (jnp.swapaxes(a, 0, 1) if k in _TRANSPOSED else a).reshape(wts[k].shape)

    shards = {k: block(k, wts[k]).astype(bf16) for k in _BIG}
    w_in_blocks, taps = _gather_first(shards["w_in"], two_d(ml_conv_w), "gather_w_in")
    w_in_full = _from_chip_major(w_in_blocks, False)
    conv_w_full = _from_chip_major(taps, True)

    def core_sum(k, g):
        pieces = _to_chip_major(g, k in _COL_SPLIT)
        if pieces.shape[1] % (2 * _EX_ROWS):
            return _pair_reduce_cols(pieces, "pair_reduce_" + k)
        return _pair_reduce(pieces, "pair_reduce_" + k)

    early_keys = _BIG[1:]
    loss, grad_x, grads, (got_hg, got_ml) = _local_step(
        x[0], p[0, 0], loss_target[0], w_in_full, b_in, hg_lb_logits, conv_w_full, ml_conv_b, hg_norm_g, ml_norm_g,
        None, ln1_g, ln1_b, None, None, None, ln2_g, ln2_b, None, None, ple_b_gate,
        early_hook=lambda low: tuple([_to_chip_major(low[k], k in _COL_SPLIT) for k in names] for names in _SCATTER_PLAN),
        late_shards={k: shards[k] for k in early_keys})

    out_g, out_d, out_m, out_v = {}, {}, {}, {}

    def finish(k, g, d, nm, nv):
        out_g[k], out_d[k], out_m[k], out_v[k] = unblock(k, g), unblock(k, d), unblock(k, nm), unblock(k, nv)

    for names, got in zip(_SCATTER_PLAN, (got_hg, got_ml)):
        for k, rcv in zip(names, got):
            finish(k, *_reduce_adamw(rcv, block(k, wts[k]), block(k, mom[k]), block(k, var[k]), "reduce_adamw_" + k))

    rcv = _scatter_chips([core_sum("w_in", grads["w_in"])], "scatter_grad_w_in")[0]
    own = block("w_in", wts["w_in"])
    if rcv.shape[1] == own.shape[0]:
        whole = _chip_reduce_swap_cols(rcv, "chip_reduce_w_in")
    else:
        parts = _chip_reduce_swap(rcv, "chip_reduce_w_in")
        whole = parts.reshape(2 * parts.shape[1], parts.shape[2])
    finish("w_in", *_adamw(whole[None], own, block("w_in", mom["w_in"]), block("w_in", var["w_in"]), "adamw_w_in"))

    small_shapes = [(1, PROJ_W), (2, MIX_W), (CONV_K, MIX_W)] + [(1, MIX_W)] * 3 + [(1, D_MODEL)] * 5 + [(1, 1)]
    contrib = _pack([grads[k] for k in _SMALL] + [loss])
    summed = _sum_slots(_gather_all(contrib, "gather_small"), "sum_small")
    small = _unpack(summed, small_shapes)
    loss_total = small[-1].reshape(())
    gsm = dict(zip(_SMALL, small[:-1]))
    place = 2 * lax.axis_index("x") + lax.axis_index("y")
    conv_cols = ml_conv_w.shape[-1]
    gsm["ml_conv_w"] = lax.dynamic_slice(gsm["ml_conv_w"], (0, place * conv_cols), (CONV_K, conv_cols))
    own_shapes = [wts[k].shape for k in _SMALL]
    g_pack = _pack([gsm[k] for k in _SMALL])
    res = _adamw(g_pack[None], _pack([wts[k] for k in _SMALL]), _pack([mom[k] for k in _SMALL]), _pack([var[k] for k in _SMALL]), "adamw_small")
    for dst, pack in zip((out_g, out_d, out_m, out_v), res):
        for k, a in zip(_SMALL, _unpack(pack, own_shapes)):
            dst[k] = a

    outs = [loss_total, grad_x[None]]
    for group in (out_g, out_d, out_m, out_v):
        outs += [group[k] for k in _ORDER]
    return tuple(outs)
```

```python
import functools

import jax
import jax.numpy as jnp
from jax import lax
from jax.experimental import pallas as pl
from jax.experimental.pallas import tpu as pltpu

f32 = jnp.float32
bf16 = jnp.bfloat16
HI = lax.Precision.HIGHEST

D_MODEL = 1024
HEADS = 4
HEAD_W = 128
MIX_W = HEADS * HEAD_W
ML_DQK = 64
PROJ_W = 3592
U_HG = 4 * MIX_W
U_ML = 3 * MIX_W + 128
D_FF = 2816
PLE = 256
CHUNK = 128
SUB = 16
EXP_CAP = 80.0
CONV_K = 4
HALO = 8
ALPHA = float(2.0 ** 0.25)
LN_EPS = 1e-5
RMS_EPS = 1e-6
NEG = -1e30
LR, B1, B2, EPS_ADAM, WD, STEP = 0.001, 0.9, 0.999, 1e-08, 0.01, 10
VMEM_LIMIT = 56 * 1024 * 1024
DENSE_ROWS = 512
WGRAD_ROWS = 2048


def _cparams(n_axes, arbitrary=True):
    sem = ("arbitrary",) * n_axes if arbitrary else ("parallel",) * n_axes
    return pltpu.CompilerParams(dimension_semantics=sem, vmem_limit_bytes=VMEM_LIMIT)


ACT = bf16


def _mx(a):
    return a.astype(ACT)


def _bdot(a, b):
    return jnp.dot(_mx(a), _mx(b), preferred_element_type=f32)


def _bdot_nt(a, b):
    return lax.dot_general(_mx(a), _mx(b), (((1,), (1,)), ((), ())), preferred_element_type=f32)


def _bdot_tn(a, b):
    return lax.dot_general(_mx(a), _mx(b), (((0,), (0,)), ((), ())), preferred_element_type=f32)


def _split3(x):
    hi = x.astype(bf16)
    r1 = x - hi.astype(f32)
    mid = r1.astype(bf16)
    lo = (r1 - mid.astype(f32)).astype(bf16)
    return hi, mid, lo


def _dot3(a, b, dims):
    a_hi = a.astype(bf16)
    a_lo = (a - a_hi.astype(f32)).astype(bf16)
    b_hi = b.astype(bf16)
    b_lo = (b - b_hi.astype(f32)).astype(bf16)
    dn = (dims, ((), ()))
    return (lax.dot_general(a_hi, b_hi, dn, preferred_element_type=f32) + lax.dot_general(a_hi, b_lo, dn, preferred_element_type=f32)
            + lax.dot_general(a_lo, b_hi, dn, preferred_element_type=f32))


def _lane_sum(x):
    hi = x.astype(bf16)
    lo = (x - hi.astype(f32)).astype(bf16)
    ones = jnp.ones((x.shape[1], 128), bf16)
    return jnp.dot(hi, ones, preferred_element_type=f32) + jnp.dot(lo, ones, preferred_element_type=f32)


def _lane_dot(x, row):
    return _dot3(x, jnp.broadcast_to(row, (128, row.shape[1])), ((1,), (1,)))


def _sel_dot(sel, x):
    sb = sel.astype(bf16)
    return sum(jnp.dot(sb, part, preferred_element_type=f32) for part in _split3(x))


def _sel_dot_nt(sel, x):
    sb = sel.astype(bf16)
    return sum(lax.dot_general(sb, part, (((1,), (1,)), ((), ())), preferred_element_type=f32) for part in _split3(x))


def _sigmoid(x):
    return 1.0 / (1.0 + jnp.exp(-x))


def _log_sigmoid(x):
    return jnp.minimum(x, 0.0) - jnp.log(1.0 + jnp.exp(-jnp.abs(x)))


def _tri(n, upper=False):
    r = lax.broadcasted_iota(jnp.int32, (n, n), 0)
    c = lax.broadcasted_iota(jnp.int32, (n, n), 1)
    return (c >= r) if upper else (c <= r)


def _rows(tm, n, col=0):
    return pl.BlockSpec((tm, n), lambda i, _c=col: (i, _c))


def _rows_rev(tm, n, nb, col=0):
    return pl.BlockSpec((tm, n), lambda i, _c=col, _nb=nb: (_nb - 1 - i, _c))


def _const(shape):
    return pl.BlockSpec(shape, lambda i, _n=len(shape): (0,) * _n)


def _resident(shape):
    return pl.BlockSpec(shape, lambda i, _n=len(shape): (0,) * _n, pipeline_mode=pl.Buffered(1))


def _tile(t, want):
    return want if t % want == 0 else t


def _inproj(x, w_hg, w_ml, b_hg, b_ml, riders=()):
    t = x.shape[0]
    tm = _tile(t, DENSE_ROWS)

    def body(x_ref, whg_ref, wml_ref, bhg_ref, bml_ref, uhg_ref, uml_ref, xb_ref):
        xb = _mx(x_ref[...])
        xb_ref[...] = xb
        uhg_ref[...] = _bdot_nt(xb, whg_ref[...]) + bhg_ref[...]
        uml_ref[...] = _bdot_nt(xb, wml_ref[...]) + bml_ref[...]

    return _riding_call(
        body, "inproj", t // tm,
        in_specs=[_rows(tm, D_MODEL), _resident((U_HG, D_MODEL)), _resident((U_ML, D_MODEL)), _const((1, U_HG)), _const((1, U_ML))],
        out_specs=[_rows(tm, U_HG), _rows(tm, U_ML), _rows(tm, D_MODEL)],
        out_shape=[jax.ShapeDtypeStruct((t, U_HG), f32), jax.ShapeDtypeStruct((t, U_ML), f32), jax.ShapeDtypeStruct((t, D_MODEL), ACT)],
        scratch_shapes=[], operands=(x, w_hg, w_ml, b_hg, b_ml), riders=riders, copies=_gather_copies, ride_shapes=_gather_shapes(riders))


def _hg_gates(hq, hf, lb, tri, b=None):
    s = _sigmoid(hf)
    om = 1.0 - lb
    f = lb + om * s
    k = om * (1.0 - s)
    sq = _sigmoid(hq)
    q = hq * sq
    if b is None:
        b = _sel_dot(tri, jnp.log(f))
    return q, sq, s, f, k, b


def _hg_scores(q, k, b, tril_mask, a=None):
    qts, kts, eqs, eks, rows = [], [], [], [], []
    for i in range(CHUNK // SUB):
        lo = i * SUB
        ref = jnp.zeros_like(b[0:1]) if i == 0 else b[lo - 1:lo]
        eq = jnp.exp(b[lo:lo + SUB] - ref)
        ek = jnp.exp(jnp.minimum(ref - b, EXP_CAP))
        qt = q[lo:lo + SUB] * eq
        kt = k * ek
        if a is None:
            rows.append(_bdot_nt(qt, kt))
        qts.append(qt); kts.append(kt); eqs.append(eq); eks.append(ek)
    if a is None:
        a = jnp.where(tril_mask, jnp.concatenate(rows, axis=0), 0.0)
    return a, qts, kts, eqs, eks


def _head_rms(o, gn, on_mxu=False):
    ms = _lane_sum(o * o) * (1.0 / o.shape[1]) if on_mxu else jnp.mean(o * o, axis=-1, keepdims=True)
    rstd = lax.rsqrt(ms + RMS_EPS)
    oh = o * rstd
    return oh, rstd, oh * gn


def _lower_bound(logit_ref):
    lg = logit_ref[...]
    return _sigmoid(lg[0:1] - lg[1:2])


def _hgrn2_fwd(u_hg, logits, gn, riders=()):
    t = u_hg.shape[0]
    tb = _tile(t, 256)
    nc_blk = tb // CHUNK

    def body(u_ref, lg_ref, gn_ref, og_ref, sst_ref, b_ref, a_ref, o_ref, st_ref):
        @pl.when(pl.program_id(0) == 0)
        def _():
            st_ref[...] = jnp.zeros_like(st_ref)

        lb_all = _lower_bound(lg_ref)
        tril_mask = _tri(CHUNK)
        tri = tril_mask.astype(f32)

        def chunk(c, carry):
            r0 = pl.multiple_of(c * CHUNK, CHUNK)
            rows = pl.ds(r0, CHUNK)
            heads = range(HEADS)
            cols = [slice(h * HEAD_W, (h + 1) * HEAD_W) for h in heads]
            hv = [u_ref[rows, 2 * MIX_W + h * HEAD_W:2 * MIX_W + (h + 1) * HEAD_W] for h in heads]
            gts = [_hg_gates(u_ref[rows, h * HEAD_W:(h + 1) * HEAD_W], u_ref[rows, MIX_W + h * HEAD_W:MIX_W + (h + 1) * HEAD_W],
                             lb_all[:, cols[h]], tri) for h in heads]
            q = [g[0] for g in gts]
            k = [g[4] for g in gts]
            b = [g[5] for g in gts]
            a = [_hg_scores(q[h], k[h], b[h], tril_mask)[0] for h in heads]
            st = [st_ref[h] for h in heads]
            bl = [b[h][CHUNK - 1:CHUNK] for h in heads]
            o = [_bdot(a[h], hv[h]) + _bdot_nt(q[h] * jnp.exp(b[h]), st[h]) for h in heads]
            new_st = [st[h] * jnp.exp(bl[h]) + _bdot_tn(hv[h], k[h] * jnp.exp(bl[h] - b[h])) for h in heads]
            for h in heads:
                sst_ref[c, h] = st[h]
                st_ref[h] = new_st[h]
                b_ref[rows, cols[h]] = b[h]
                a_ref[rows, cols[h]] = a[h].astype(ACT)
                o_ref[rows, cols[h]] = o[h]
                hgate = u_ref[rows, 3 * MIX_W + h * HEAD_W:3 * MIX_W + (h + 1) * HEAD_W]
                _, _, y = _head_rms(o[h], gn_ref[:, cols[h]])
                og_ref[rows, cols[h]] = (y * (hgate * _sigmoid(hgate))).astype(ACT)
            return carry

        lax.fori_loop(0, nc_blk, chunk, 0, unroll=True)

    assert CHUNK == HEAD_W
    return _riding_call(
        body, "hgrn2_fwd", t // tb,
        in_specs=[_rows(tb, U_HG), _const((2, MIX_W)), _const((1, MIX_W))],
        out_specs=[_rows(tb, MIX_W), pl.BlockSpec((nc_blk, HEADS, HEAD_W, HEAD_W), lambda i: (i, 0, 0, 0)),
                   _rows(tb, MIX_W), _rows(tb, MIX_W), _rows(tb, MIX_W)],
        out_shape=[jax.ShapeDtypeStruct((t, MIX_W), ACT), jax.ShapeDtypeStruct((t // CHUNK, HEADS, HEAD_W, HEAD_W), f32),
                   jax.ShapeDtypeStruct((t, MIX_W), f32), jax.ShapeDtypeStruct((t, MIX_W), ACT), jax.ShapeDtypeStruct((t, MIX_W), f32)],
        scratch_shapes=[pltpu.VMEM((HEADS, HEAD_W, HEAD_W), f32)],
        operands=(u_hg, logits, gn), riders=riders, copies=_gather_copies, ride_shapes=_gather_shapes(riders))


def _hgrn2_bwd(u_hg, logits, gn, sst, bcum, scores, o_raw, dog, riders=()):
    t = u_hg.shape[0]
    tb = _tile(t, 256)
    nb = t // tb
    nc_blk = tb // CHUNK

    def body(u_ref, lg_ref, gn_ref, sst_ref, b_ref, a_ref, o_ref, dog_ref, du_ref, dlg_ref, dgn_ref, dst_ref):
        @pl.when(pl.program_id(0) == 0)
        def _():
            dst_ref[...] = jnp.zeros_like(dst_ref)
            dlg_ref[...] = jnp.zeros_like(dlg_ref)
            dgn_ref[...] = jnp.zeros_like(dgn_ref)

        lb_all = _lower_bound(lg_ref)
        tril_mask = _tri(CHUNK)
        tri = tril_mask.astype(f32)
        triu = _tri(CHUNK, upper=True).astype(f32)

        def chunk(j, carry):
            c = nc_blk - 1 - j
            r0 = pl.multiple_of(c * CHUNK, CHUNK)
            rows = pl.ds(r0, CHUNK)
            heads = range(HEADS)
            nsub = CHUNK // SUB
            cols = [slice(h * HEAD_W, (h + 1) * HEAD_W) for h in heads]
            hq = [u_ref[rows, h * HEAD_W:(h + 1) * HEAD_W] for h in heads]
            hf = [u_ref[rows, MIX_W + h * HEAD_W:MIX_W + (h + 1) * HEAD_W] for h in heads]
            hv = [u_ref[rows, 2 * MIX_W + h * HEAD_W:2 * MIX_W + (h + 1) * HEAD_W] for h in heads]
            lb = [lb_all[:, cols[h]] for h in heads]
            gts = [_hg_gates(hq[h], hf[h], lb[h], tri, b=b_ref[rows, cols[h]]) for h in heads]
            q, sq, s, f, k, b = ([g[n] for g in gts] for n in range(6))
            scs = [_hg_scores(q[h], k[h], b[h], tril_mask, a=a_ref[rows, cols[h]]) for h in heads]
            a, qts, kts, eqs, eks = ([sc[n] for sc in scs] for n in range(5))
            st = [sst_ref[c, h] for h in heads]
            dst = [dst_ref[h] for h in heads]
            bl = [b[h][CHUNK - 1:CHUNK] for h in heads]
            eb = [jnp.exp(b[h]) for h in heads]
            qh = [q[h] * eb[h] for h in heads]
            ekl = [jnp.exp(bl[h] - b[h]) for h in heads]
            kh = [k[h] * ekl[h] for h in heads]
            o = [o_ref[rows, cols[h]] for h in heads]
            do = []
            for h in heads:
                hgate = u_ref[rows, 3 * MIX_W + h * HEAD_W:3 * MIX_W + (h + 1) * HEAD_W]
                gnh = gn_ref[:, cols[h]]
                oh, rstd, y = _head_rms(o[h], gnh)
                sg = _sigmoid(hgate)
                dogh = dog_ref[rows, cols[h]]
                dy = dogh * (hgate * sg)
                du_ref[rows, 3 * MIX_W + h * HEAD_W:3 * MIX_W + (h + 1) * HEAD_W] = (dogh * y * (sg * (1.0 + hgate * (1.0 - sg)))).astype(ACT)
                dgn_ref[:, cols[h]] += jnp.sum(dy * oh, axis=0, keepdims=True)
                doh = dy * gnh
                do.append(rstd * (doh - oh * jnp.mean(doh * oh, axis=-1, keepdims=True)))
            da = [jnp.where(tril_mask, _bdot_nt(do[h], hv[h]), 0.0) for h in heads]
            dv = [_bdot_tn(a[h], do[h]) + _bdot_nt(kh[h], dst[h]) for h in heads]
            dq = [_bdot(do[h], st[h]) * eb[h] for h in heads]
            dk = [_bdot(hv[h], dst[h]) * ekl[h] for h in heads]
            d_last = [jnp.sum(k[h] * dk[h], axis=0, keepdims=True) + jnp.exp(bl[h]) * jnp.sum(dst[h] * st[h], axis=0, keepdims=True)
                      for h in heads]
            d_b = [q[h] * dq[h] - k[h] * dk[h] for h in heads]
            dqs = [[] for _ in heads]
            q_dq = [[] for _ in heads]
            for i in range(nsub):
                for h in heads:
                    da_i = _mx(da[h][i * SUB:(i + 1) * SUB])
                    q_r, k_r = _mx(qts[h][i]), _mx(kts[h][i])
                    g_q = jnp.dot(da_i, k_r, preferred_element_type=f32)
                    g_k = lax.dot_general(da_i, q_r, (((0,), (0,)), ((), ())), preferred_element_type=f32)
                    dqs[h].append(g_q * eqs[h][i])
                    q_dq[h].append(q_r.astype(f32) * g_q)
                    dk[h] = dk[h] + g_k * eks[h][i]
                    d_b[h] = d_b[h] - k_r.astype(f32) * g_k
            for h in heads:
                dq[h] = dq[h] + jnp.concatenate(dqs[h], axis=0)
                d_b[h] = d_b[h] + jnp.concatenate(q_dq[h], axis=0)
                dst_ref[h] = dst[h] * jnp.exp(bl[h]) + _bdot_tn(do[h], qh[h])
            dg = [_sel_dot(triu, d_b[h]) + d_last[h] for h in heads]
            for h in heads:
                dfk = dg[h] / f[h] - dk[h]
                du_ref[rows, h * HEAD_W:(h + 1) * HEAD_W] = (dq[h] * (sq[h] * (1.0 + hq[h] * (1.0 - sq[h])))).astype(ACT)
                du_ref[rows, MIX_W + h * HEAD_W:MIX_W + (h + 1) * HEAD_W] = ((1.0 - lb[h]) * dfk * s[h] * (1.0 - s[h])).astype(ACT)
                du_ref[rows, 2 * MIX_W + h * HEAD_W:2 * MIX_W + (h + 1) * HEAD_W] = dv[h].astype(ACT)
                dlb = jnp.sum((1.0 - s[h]) * dfk, axis=0, keepdims=True) * (lb[h] * (1.0 - lb[h]))
                dlg_ref[0:1, cols[h]] += dlb
                dlg_ref[1:2, cols[h]] -= dlb
            return carry

        lax.fori_loop(0, nc_blk, chunk, 0, unroll=True)

    rev = _rows_rev(tb, MIX_W, nb)
    return _riding_call(
        body, "hgrn2_bwd", nb,
        in_specs=[_rows_rev(tb, U_HG, nb), _const((2, MIX_W)), _const((1, MIX_W)),
                  pl.BlockSpec((nc_blk, HEADS, HEAD_W, HEAD_W), lambda i: (nb - 1 - i, 0, 0, 0)), rev, rev, rev, rev],
        out_specs=[_rows_rev(tb, U_HG, nb), _const((2, MIX_W)), _const((1, MIX_W))],
        out_shape=[jax.ShapeDtypeStruct((t, U_HG), ACT), jax.ShapeDtypeStruct((2, MIX_W), f32), jax.ShapeDtypeStruct((1, MIX_W), f32)],
        scratch_shapes=[pltpu.VMEM((HEADS, HEAD_W, HEAD_W), f32)],
        operands=(u_hg, logits, gn, sst, bcum, scores, o_raw, dog), riders=riders, copies=_scatter_copies,
        ride_shapes=[jax.ShapeDtypeStruct(r.shape, r.dtype) for r in riders])


def _conv_fwd(u_ml, w, b):
    t = u_ml.shape[0]
    tm = _tile(t, 512)

    def body(x_ref, w_ref, b_ref, pre_ref, act_ref, xbuf):
        @pl.when(pl.program_id(0) == 0)
        def _():
            xbuf[...] = jnp.zeros_like(xbuf)

        xbuf[0:HALO, :] = xbuf[tm:tm + HALO, :]
        xbuf[HALO:HALO + tm, :] = x_ref[...]
        pre = b_ref[...] + jnp.zeros((tm, MIX_W), f32)
        for kk in range(CONV_K):
            off = HALO - (CONV_K - 1) + kk
            pre = pre + w_ref[kk:kk + 1, :] * xbuf[off:off + tm, :]
        pre_ref[...] = pre
        act_ref[...] = pre * _sigmoid(pre)

    return pl.pallas_call(
        body, name="conv_fwd", grid=(t // tm,),
        in_specs=[_rows(tm, MIX_W), _const((CONV_K, MIX_W)), _const((1, MIX_W))],
        out_specs=[_rows(tm, MIX_W), _rows(tm, MIX_W)],
        out_shape=[jax.ShapeDtypeStruct((t, MIX_W), f32)] * 2,
        scratch_shapes=[pltpu.VMEM((tm + HALO, MIX_W), f32)],
        compiler_params=_cparams(1),
    )(u_ml, w, b)


def _conv_bwd(u_ml, w, pre, dact):
    t = u_ml.shape[0]
    tm = _tile(t, 512)
    nb = t // tm
    hb = tm // HALO

    def body(x_ref, halo_ref, w_ref, pre_ref, dact_ref, dx_ref, dw_ref, db_ref, dbuf, xbuf):
        i = pl.program_id(0)

        @pl.when(i == 0)
        def _():
            dbuf[...] = jnp.zeros_like(dbuf)
            dw_ref[...] = jnp.zeros_like(dw_ref)
            db_ref[...] = jnp.zeros_like(db_ref)

        p = pre_ref[...]
        sg = _sigmoid(p)
        dpre = dact_ref[...] * (sg * (1.0 + p * (1.0 - sg)))
        dbuf[tm:tm + HALO, :] = dbuf[0:HALO, :]
        dbuf[0:tm, :] = dpre
        has_prev = (i < nb - 1).astype(f32)
        xbuf[0:HALO, :] = halo_ref[...] * has_prev
        xbuf[HALO:HALO + tm, :] = x_ref[...]
        dx = jnp.zeros((tm, MIX_W), f32)
        for kk in range(CONV_K):
            back = CONV_K - 1 - kk
            dx = dx + w_ref[kk:kk + 1, :] * dbuf[back:back + tm, :]
            off = HALO - (CONV_K - 1) + kk
            dw_ref[kk:kk + 1, :] += jnp.sum(dpre * xbuf[off:off + tm, :], axis=0, keepdims=True)
        dx_ref[...] = dx.astype(ACT)
        db_ref[...] += jnp.sum(dpre, axis=0, keepdims=True)

    return pl.pallas_call(
        body, name="conv_bwd", grid=(nb,),
        in_specs=[_rows_rev(tm, MIX_W, nb),
                  pl.BlockSpec((HALO, MIX_W), lambda i: (jnp.maximum((nb - 1 - i) * hb - 1, 0), 0)),
                  _const((CONV_K, MIX_W)), _rows_rev(tm, MIX_W, nb), _rows_rev(tm, MIX_W, nb)],
        out_specs=[_rows_rev(tm, MIX_W, nb), _const((CONV_K, MIX_W)), _const((1, MIX_W))],
        out_shape=[jax.ShapeDtypeStruct((t, MIX_W), ACT), jax.ShapeDtypeStruct((CONV_K, MIX_W), f32), jax.ShapeDtypeStruct((1, MIX_W), f32)],
        scratch_shapes=[pltpu.VMEM((tm + HALO, MIX_W), f32), pltpu.VMEM((tm + HALO, MIX_W), f32)],
        compiler_params=_cparams(1),
    )(u_ml, u_ml, w, pre, dact)


def _lane_pick(x, lane):
    idx = lax.broadcasted_iota(jnp.int32, x.shape, 1)
    return jnp.sum(jnp.where(idx == lane, x, 0.0), axis=-1, keepdims=True)


def _ml_gate_forms(gates, tri):
    lf = _log_sigmoid(gates)
    gc = _sel_dot(tri, lf)
    lane = lax.broadcasted_iota(jnp.int32, gates.shape, 1)
    mixed = jnp.where(lane < HEADS, gates, gc)
    sel = (lax.broadcasted_iota(jnp.int32, (8, 128), 0) == lax.broadcasted_iota(jnp.int32, (8, 128), 1)).astype(f32)
    rowsf = _sel_dot_nt(sel, mixed)
    return gc, rowsf


def _ml_chunk(q, k, v, gates, gc, rowsf, c_st, n_st, m_st, tril_mask):
    hs = range(HEADS)
    g_col = [_lane_pick(gc, HEADS + h) for h in hs]
    ig_col = [_lane_pick(gates, h) for h in hs]
    dmat = [jnp.where(tril_mask, g_col[h] - rowsf[HEADS + h:HEADS + h + 1, :] + rowsf[h:h + 1, :], NEG) for h in hs]
    m_inter = [g_col[h] + m_st[h] for h in hs]
    m_t = [jnp.maximum(m_inter[h], jnp.max(dmat[h], axis=-1, keepdims=True)) for h in hs]
    wi = [jnp.exp(dmat[h] - m_t[h]) for h in hs]
    wo = [jnp.exp(m_inter[h] - m_t[h]) for h in hs]
    qk = [_bdot_nt(q[h], k[h]) * wi[h] for h in hs]
    num = [_bdot(qk[h], v[h]) + wo[h] * _bdot(q[h], c_st[h]) for h in hs]
    den = [_lane_sum(qk[h]) + wo[h] * _lane_dot(q[h], n_st[h]) for h in hs]
    floor = [jnp.exp(-m_t[h]) for h in hs]
    z = [jnp.maximum(jnp.abs(den[h]), floor[h]) for h in hs]
    g_last = [g_col[h][CHUNK - 1:CHUNK] for h in hs]
    a_col = [g_last[h] - g_col[h] + ig_col[h] for h in hs]
    m_new = [jnp.maximum(g_last[h] + m_st[h], jnp.max(a_col[h], axis=0, keepdims=True)) for h in hs]
    ws = [jnp.exp(a_col[h] - m_new[h]) for h in hs]
    w_old = [jnp.exp(g_last[h] + m_st[h] - m_new[h]) for h in hs]
    return dict(wi=wi, wo=wo, qk=qk, num=num, den=den, z=z, floor=floor, ws=ws, w_old=w_old, m_new=m_new)


def _mlstm_fwd(qkc, u_ml, gn, riders=()):
    t = qkc.shape[0]
    tb = _tile(t, 256)
    nc_blk = tb // CHUNK

    def body(qk_ref, v_ref, mo_ref, gt_ref, gn_ref, og_ref, cst_ref, nst_ref, mst_ref, c_sc, n_sc, m_sc):
        @pl.when(pl.program_id(0) == 0)
        def _():
            c_sc[...] = jnp.zeros_like(c_sc)
            n_sc[...] = jnp.zeros_like(n_sc)
            m_sc[...] = jnp.zeros_like(m_sc)

        tril_mask = _tri(CHUNK)
        tri = tril_mask.astype(f32)

        def chunk(c, carry):
            r0 = pl.multiple_of(c * CHUNK, CHUNK)
            rows = pl.ds(r0, CHUNK)
            gates = gt_ref[rows, :]
            gc, rowsf = _ml_gate_forms(gates, tri)
            hs = range(HEADS)
            q = [qk_ref[rows, h * ML_DQK:(h + 1) * ML_DQK] * (ML_DQK ** -0.5) for h in hs]
            k = [qk_ref[rows, HEADS * ML_DQK + h * ML_DQK:HEADS * ML_DQK + (h + 1) * ML_DQK] for h in hs]
            v = [v_ref[rows, h * HEAD_W:(h + 1) * HEAD_W] for h in hs]
            c_st = [c_sc[h] for h in hs]
            n_st = [n_sc[h] for h in hs]
            m_full = [m_sc[h] for h in hs]
            r = _ml_chunk(q, k, v, gates, gc, rowsf, c_st, n_st, [m[:, 0:1] for m in m_full], tril_mask)
            ksc = [k[h] * r["ws"][h] for h in hs]
            new_c = [r["w_old"][h] * c_st[h] + _bdot_tn(ksc[h], v[h]) for h in hs]
            for h in hs:
                cs = slice(h * HEAD_W, (h + 1) * HEAD_W)
                cst_ref[c, h] = c_st[h]
                nst_ref[c, h] = n_st[h]
                mst_ref[c, h] = m_full[h]
                c_sc[h] = new_c[h]
                n_sc[h] = r["w_old"][h] * n_st[h] + jnp.sum(ksc[h], axis=0, keepdims=True)
                m_sc[h] = r["m_new"][h] + jnp.zeros((1, 128), f32)
                _, _, y = _head_rms(r["num"][h] / r["z"][h], gn_ref[:, cs], on_mxu=True)
                og_ref[rows, cs] = (y * _sigmoid(mo_ref[rows, h * HEAD_W:(h + 1) * HEAD_W])).astype(ACT)
            return carry

        lax.fori_loop(0, nc_blk, chunk, 0)

    nchunks = t // CHUNK
    return _riding_call(
        body, "mlstm_fwd", t // tb,
        in_specs=[_rows(tb, MIX_W), _rows(tb, MIX_W, 1), _rows(tb, MIX_W, 2), _rows(tb, 128, 12), _const((1, MIX_W))],
        out_specs=[_rows(tb, MIX_W),
                   pl.BlockSpec((nc_blk, HEADS, ML_DQK, HEAD_W), lambda i: (i, 0, 0, 0)),
                   pl.BlockSpec((nc_blk, HEADS, 1, ML_DQK), lambda i: (i, 0, 0, 0)),
                   pl.BlockSpec((nc_blk, HEADS, 1, 128), lambda i: (i, 0, 0, 0))],
        out_shape=[jax.ShapeDtypeStruct((t, MIX_W), ACT),
                   jax.ShapeDtypeStruct((nchunks, HEADS, ML_DQK, HEAD_W), f32),
                   jax.ShapeDtypeStruct((nchunks, HEADS, 1, ML_DQK), f32),
                   jax.ShapeDtypeStruct((nchunks, HEADS, 1, 128), f32)],
        scratch_shapes=[pltpu.VMEM((HEADS, ML_DQK, HEAD_W), f32), pltpu.VMEM((HEADS, 1, ML_DQK), f32), pltpu.VMEM((HEADS, 1, 128), f32)],
        operands=(qkc, u_ml, u_ml, u_ml, gn), riders=riders, copies=_gather_copies, ride_shapes=_gather_shapes(riders))


def _mlstm_bwd(qkc, u_ml, gn, cst, nst, mst, dog, riders=()):
    t = qkc.shape[0]
    tb = _tile(t, 256)
    nb = t // tb
    nc_blk = tb // CHUNK

    def body(qk_ref, v_ref, mo_ref, gt_ref, gn_ref, cst_ref, nst_ref, mst_ref, dog_ref,
             dqk_ref, dv_ref, dmo_ref, dgt_ref, dgn_ref, dc_sc, dn_sc):
        @pl.when(pl.program_id(0) == 0)
        def _():
            dc_sc[...] = jnp.zeros_like(dc_sc)
            dn_sc[...] = jnp.zeros_like(dn_sc)
            dgn_ref[...] = jnp.zeros_like(dgn_ref)

        tril_mask = _tri(CHUNK)
        tri = tril_mask.astype(f32)
        triu = _tri(CHUNK, upper=True).astype(f32)
        lane = lax.broadcasted_iota(jnp.int32, (CHUNK, 128), 1)

        def chunk(j, carry):
            c = nc_blk - 1 - j
            r0 = pl.multiple_of(c * CHUNK, CHUNK)
            rows = pl.ds(r0, CHUNK)
            gates = gt_ref[rows, :]
            gc, rowsf = _ml_gate_forms(gates, tri)
            dg_mat = jnp.zeros((CHUNK, 128), f32)
            dig_mat = jnp.zeros((CHUNK, 128), f32)
            dlast_row = jnp.zeros((1, 128), f32)
            hs = range(HEADS)
            cols = [slice(h * HEAD_W, (h + 1) * HEAD_W) for h in hs]
            q = [qk_ref[rows, h * ML_DQK:(h + 1) * ML_DQK] * (ML_DQK ** -0.5) for h in hs]
            k = [qk_ref[rows, HEADS * ML_DQK + h * ML_DQK:HEADS * ML_DQK + (h + 1) * ML_DQK] for h in hs]
            v = [v_ref[rows, h * HEAD_W:(h + 1) * HEAD_W] for h in hs]
            c_st = [cst_ref[c, h] for h in hs]
            n_st = [nst_ref[c, h] for h in hs]
            m_st = [mst_ref[c, h][:, 0:1] for h in hs]
            dc = [dc_sc[h] for h in hs]
            dn = [dn_sc[h] for h in hs]
            r = _ml_chunk(q, k, v, gates, gc, rowsf, c_st, n_st, m_st, tril_mask)
            z, wi, wo, ws, w_old, den = r["z"], r["wi"], r["wo"], r["ws"], r["w_old"], r["den"]
            hh = [r["num"][h] / z[h] for h in hs]
            dh = []
            for h in hs:
                gnh = gn_ref[:, cols[h]]
                oh, rstd, y = _head_rms(hh[h], gnh, on_mxu=True)
                sg = _sigmoid(mo_ref[rows, h * HEAD_W:(h + 1) * HEAD_W])
                dogh = dog_ref[rows, cols[h]]
                dy = dogh * sg
                dmo_ref[rows, cols[h]] = (dogh * y * (sg * (1.0 - sg))).astype(ACT)
                dgn_ref[:, cols[h]] += jnp.sum(dy * oh, axis=0, keepdims=True)
                doh = dy * gnh
                dh.append(rstd * (doh - oh * (_lane_sum(doh * oh) * (1.0 / HEAD_W))))
            dnum = [dh[h] / z[h] for h in hs]
            dz = [-_lane_sum(dh[h] * hh[h]) / z[h] for h in hs]
            dden = [jnp.where(jnp.abs(den[h]) > r["floor"][h], dz[h] * jnp.sign(den[h]), 0.0) for h in hs]
            dsw = [(_bdot_nt(dnum[h], v[h]) + dden[h]) * wi[h] for h in hs]
            dq = [_bdot(dsw[h], k[h]) + wo[h] * (_bdot_nt(dnum[h], c_st[h]) + dden[h][:, :ML_DQK] * n_st[h]) for h in hs]
            dk_state = [ws[h] * (_bdot_nt(v[h], dc[h]) + dn[h]) for h in hs]
            dk = [_bdot_tn(dsw[h], q[h]) + dk_state[h] for h in hs]
            dv = [_bdot_tn(r["qk"][h], dnum[h]) + ws[h] * _bdot(k[h], dc[h]) for h in hs]
            woq = [wo[h] * q[h] for h in hs]
            new_dc = [w_old[h] * dc[h] + _bdot_tn(woq[h], dnum[h]) for h in hs]
            for h in hs:
                dv_ref[rows, cols[h]] = dv[h].astype(ACT)
                dc_sc[h] = new_dc[h]
                dn_sc[h] = w_old[h] * dn[h] + jnp.sum(woq[h] * dden[h][:, :ML_DQK], axis=0, keepdims=True)
                d_last = (jnp.sum(jnp.sum(k[h] * dk_state[h], axis=0, keepdims=True), axis=-1, keepdims=True)
                          + w_old[h] * (jnp.sum(jnp.sum(dc[h] * c_st[h], axis=0, keepdims=True), axis=-1, keepdims=True)
                                        + jnp.sum(dn[h] * n_st[h], axis=-1, keepdims=True)))
                kdk = _lane_sum(k[h] * dk[h])
                qdq = _lane_sum(q[h] * dq[h])
                dg_mat = dg_mat + jnp.where(lane == HEADS + h, qdq - kdk, 0.0)
                dlast_row = dlast_row + jnp.where(lane[0:1] == HEADS + h, d_last, 0.0)
                dig_mat = dig_mat + jnp.where(lane == h, kdk, 0.0)
                dqk_ref[rows, h * ML_DQK:(h + 1) * ML_DQK] = dq[h] * (ML_DQK ** -0.5)
                dqk_ref[rows, HEADS * ML_DQK + h * ML_DQK:HEADS * ML_DQK + (h + 1) * ML_DQK] = dk[h]
            dlf = _sel_dot(triu, dg_mat) + dlast_row
            dgt_ref[rows, :] = (dig_mat + dlf * _sigmoid(-gates)).astype(ACT)
            return carry

        lax.fori_loop(0, nc_blk, chunk, 0)

    st4 = lambda a, b: pl.BlockSpec((nc_blk, HEADS, a, b), lambda i: (nb - 1 - i, 0, 0, 0))
    return _riding_call(
        body, "mlstm_bwd", nb,
        in_specs=[_rows_rev(tb, MIX_W, nb), _rows_rev(tb, MIX_W, nb, 1), _rows_rev(tb, MIX_W, nb, 2), _rows_rev(tb, 128, nb, 12),
                  _const((1, MIX_W)), st4(ML_DQK, HEAD_W), st4(1, ML_DQK), st4(1, 128), _rows_rev(tb, MIX_W, nb)],
        out_specs=[_rows_rev(tb, MIX_W, nb), _rows_rev(tb, MIX_W, nb), _rows_rev(tb, MIX_W, nb), _rows_rev(tb, 128, nb), _const((1, MIX_W))],
        out_shape=[jax.ShapeDtypeStruct((t, MIX_W), f32), jax.ShapeDtypeStruct((t, MIX_W), ACT), jax.ShapeDtypeStruct((t, MIX_W), ACT),
                   jax.ShapeDtypeStruct((t, 128), ACT), jax.ShapeDtypeStruct((1, MIX_W), f32)],
        scratch_shapes=[pltpu.VMEM((HEADS, ML_DQK, HEAD_W), f32), pltpu.VMEM((HEADS, 1, ML_DQK), f32)],
        operands=(qkc, u_ml, u_ml, u_ml, gn, cst, nst, mst, dog), riders=riders, copies=_scatter_copies,
        ride_shapes=[jax.ShapeDtypeStruct(r.shape, r.dtype) for r in riders])


def _ln_fwd(r, g, b):
    mu = jnp.mean(r, axis=-1, keepdims=True)
    xc = r - mu
    rstd = lax.rsqrt(jnp.mean(xc * xc, axis=-1, keepdims=True) + LN_EPS)
    xh = xc * rstd
    return xh * g + b, xh, rstd


def _ln_bwd(dy, xh, rstd, g):
    dxh = dy * g
    return rstd * (dxh - jnp.mean(dxh, axis=-1, keepdims=True) - xh * jnp.mean(dxh * xh, axis=-1, keepdims=True))


def _outproj_ln1(og_hg, og_ml, x, w_out, g, b, riders=()):
    t = x.shape[0]
    tm = _tile(t, DENSE_ROWS)

    def body(a_ref, b_ref, x_ref, w_ref, g_ref, bb_ref, x1_ref, xh_ref, rs_ref, x1b_ref):
        mix = _bdot(a_ref[...], w_ref[0:MIX_W, :]) + _bdot(b_ref[...], w_ref[MIX_W:2 * MIX_W, :])
        y, xh, rstd = _ln_fwd(ALPHA * x_ref[...] + mix, g_ref[...], bb_ref[...])
        x1_ref[...] = y
        x1b_ref[...] = y.astype(ACT)
        xh_ref[...] = xh.astype(ACT)
        rs_ref[...] = rstd

    return _riding_call(
        body, "outproj_ln1", t // tm,
        in_specs=[_rows(tm, MIX_W), _rows(tm, MIX_W), _rows(tm, D_MODEL), _resident((D_MODEL, D_MODEL)), _const((1, D_MODEL)), _const((1, D_MODEL))],
        out_specs=[_rows(tm, D_MODEL), _rows(tm, D_MODEL), _rows(tm, 1), _rows(tm, D_MODEL)],
        out_shape=[jax.ShapeDtypeStruct((t, D_MODEL), f32), jax.ShapeDtypeStruct((t, D_MODEL), ACT), jax.ShapeDtypeStruct((t, 1), f32),
                   jax.ShapeDtypeStruct((t, D_MODEL), ACT)],
        scratch_shapes=[], operands=(og_hg, og_ml, x, w_out, g, b), riders=riders, copies=_gather_copies, ride_shapes=_gather_shapes(riders))


def _ffn_up(x1, wg, wu, riders=()):
    t = x1.shape[0]
    tm = _tile(t, DENSE_ROWS)

    def body(x_ref, wg_ref, wu_ref, hg_ref, up_ref, a_ref):
        xv = x_ref[...]
        hg = _bdot_nt(xv, wg_ref[...])
        up = _bdot_nt(xv, wu_ref[...])
        hg_ref[...] = hg.astype(ACT)
        up_ref[...] = up.astype(ACT)
        a_ref[...] = (hg * _sigmoid(hg) * up).astype(ACT)

    return _riding_call(
        body, "ffn_up", t // tm,
        in_specs=[_rows(tm, D_MODEL), _resident((D_FF, D_MODEL)), _resident((D_FF, D_MODEL))],
        out_specs=[_rows(tm, D_FF), _rows(tm, D_FF), _rows(tm, D_FF)],
        out_shape=[jax.ShapeDtypeStruct((t, D_FF), ACT), jax.ShapeDtypeStruct((t, D_FF), ACT), jax.ShapeDtypeStruct((t, D_FF), ACT)],
        scratch_shapes=[], operands=(x1, wg, wu), riders=riders, copies=_gather_copies, ride_shapes=_gather_shapes(riders))


def _ffn_down_ln2(a, x1, wd, g, b):
    t = x1.shape[0]
    tm = _tile(t, DENSE_ROWS)

    def body(a_ref, x_ref, w_ref, g_ref, bb_ref, x2_ref, xh_ref, rs_ref, x2b_ref):
        ffn = _bdot(a_ref[...], w_ref[...])
        y, xh, rstd = _ln_fwd(ALPHA * x_ref[...] + ffn, g_ref[...], bb_ref[...])
        x2_ref[...] = y
        x2b_ref[...] = y.astype(ACT)
        xh_ref[...] = xh.astype(ACT)
        rs_ref[...] = rstd

    return pl.pallas_call(
        body, name="ffn_down_ln2", grid=(t // tm,),
        in_specs=[_rows(tm, D_FF), _rows(tm, D_MODEL), _resident((D_FF, D_MODEL)), _const((1, D_MODEL)), _const((1, D_MODEL))],
        out_specs=[_rows(tm, D_MODEL), _rows(tm, D_MODEL), _rows(tm, 1), _rows(tm, D_MODEL)],
        out_shape=[jax.ShapeDtypeStruct((t, D_MODEL), f32), jax.ShapeDtypeStruct((t, D_MODEL), ACT), jax.ShapeDtypeStruct((t, 1), f32),
                   jax.ShapeDtypeStruct((t, D_MODEL), ACT)],
        compiler_params=_cparams(1, arbitrary=False),
    )(a, x1, wd, g, b)


def _head_loss_bwd(x2, xh2, rs2, p, tgt, w_pg, b_pg, w_pp, g2):
    t = x2.shape[0]
    tm = _tile(t, DENSE_ROWS)

    def body(x_ref, xh_ref, rs_ref, p_ref, t_ref, wg_ref, bg_ref, wp_ref, g_ref,
             dr_ref, de_ref, dz_ref, loss_ref, dbg_ref, dg2_ref, db2_ref):
        @pl.when(pl.program_id(0) == 0)
        def _():
            loss_ref[...] = jnp.zeros_like(loss_ref)
            dbg_ref[...] = jnp.zeros_like(dbg_ref)
            dg2_ref[...] = jnp.zeros_like(dg2_ref)
            db2_ref[...] = jnp.zeros_like(db2_ref)

        x2v = x_ref[...]
        z = _bdot(x2v, wg_ref[...]) + bg_ref[...]
        e = _bdot(p_ref[...], wp_ref[...])
        sg = _sigmoid(z)
        diff = x2v + sg * e - t_ref[...]
        loss_ref[...] += 0.5 * jnp.sum(jnp.mean(diff * diff, axis=-1, keepdims=True), axis=0, keepdims=True)
        dy = diff * (1.0 / D_MODEL)
        de_ref[...] = (dy * sg).astype(ACT)
        dz = dy * e * (sg * (1.0 - sg))
        dz_ref[...] = dz.astype(ACT)
        dbg_ref[...] += jnp.sum(dz, axis=0, keepdims=True)
        dx2 = dy + _bdot_nt(dz, wg_ref[...])
        xh = xh_ref[...].astype(f32)
        dg2_ref[...] += jnp.sum(dx2 * xh, axis=0, keepdims=True)
        db2_ref[...] += jnp.sum(dx2, axis=0, keepdims=True)
        dr_ref[...] = _ln_bwd(dx2, xh, rs_ref[...], g_ref[...])

    row = jax.ShapeDtypeStruct((1, D_MODEL), f32)
    return pl.pallas_call(
        body, name="head_loss_bwd", grid=(t // tm,),
        in_specs=[_rows(tm, D_MODEL), _rows(tm, D_MODEL), _rows(tm, 1), _rows(tm, PLE), _rows(tm, D_MODEL),
                  _resident((D_MODEL, D_MODEL)), _const((1, D_MODEL)), _resident((PLE, D_MODEL)), _const((1, D_MODEL))],
        out_specs=[_rows(tm, D_MODEL), _rows(tm, D_MODEL), _rows(tm, D_MODEL), _const((1, 1)), _const((1, D_MODEL)), _const((1, D_MODEL)), _const((1, D_MODEL))],
        out_shape=[jax.ShapeDtypeStruct((t, D_MODEL), f32), jax.ShapeDtypeStruct((t, D_MODEL), ACT), jax.ShapeDtypeStruct((t, D_MODEL), ACT),
                   jax.ShapeDtypeStruct((1, 1), f32), row, row, row],
        compiler_params=_cparams(1),
    )(x2, xh2, rs2, p, tgt, w_pg, b_pg, w_pp, g2)


def _ffn_bwd(dr2, hg, up, xh1, rs1, wd, wg, wu, g1, w_out):
    t = dr2.shape[0]
    tm = _tile(t, DENSE_ROWS // 2)

    def body(dr_ref, hg_ref, up_ref, xh_ref, rs_ref, wd_ref, wg_ref, wu_ref, g_ref, wo_ref,
             dr1_ref, dhg_ref, dup_ref, dg1_ref, db1_ref, doghg_ref, dogml_ref):
        @pl.when(pl.program_id(0) == 0)
        def _():
            dg1_ref[...] = jnp.zeros_like(dg1_ref)
            db1_ref[...] = jnp.zeros_like(db1_ref)

        dr2v = dr_ref[...]
        da = _bdot_nt(dr2v, wd_ref[...])
        hgv = hg_ref[...].astype(f32)
        sg = _sigmoid(hgv)
        dhg = da * up_ref[...].astype(f32) * (sg * (1.0 + hgv * (1.0 - sg)))
        dup = da * (hgv * sg)
        dhg_ref[...] = dhg.astype(ACT)
        dup_ref[...] = dup.astype(ACT)
        dx1 = ALPHA * dr2v + _bdot(dhg, wg_ref[...]) + _bdot(dup, wu_ref[...])
        xh = xh_ref[...].astype(f32)
        dg1_ref[...] += jnp.sum(dx1 * xh, axis=0, keepdims=True)
        db1_ref[...] += jnp.sum(dx1, axis=0, keepdims=True)
        dr1 = _ln_bwd(dx1, xh, rs_ref[...], g_ref[...])
        dr1_ref[...] = dr1
        dog = _bdot_nt(dr1, wo_ref[...])
        doghg_ref[...] = dog[:, 0:MIX_W]
        dogml_ref[...] = dog[:, MIX_W:2 * MIX_W]

    row = jax.ShapeDtypeStruct((1, D_MODEL), f32)
    return pl.pallas_call(
        body, name="ffn_bwd", grid=(t // tm,),
        in_specs=[_rows(tm, D_MODEL), _rows(tm, D_FF), _rows(tm, D_FF), _rows(tm, D_MODEL), _rows(tm, 1),
                  _resident((D_FF, D_MODEL)), _resident((D_FF, D_MODEL)), _resident((D_FF, D_MODEL)), _const((1, D_MODEL)),
                  _resident((D_MODEL, D_MODEL))],
        out_specs=[_rows(tm, D_MODEL), _rows(tm, D_FF), _rows(tm, D_FF), _const((1, D_MODEL)), _const((1, D_MODEL)),
                   _rows(tm, MIX_W), _rows(tm, MIX_W)],
        out_shape=[jax.ShapeDtypeStruct((t, D_MODEL), f32), jax.ShapeDtypeStruct((t, D_FF), ACT), jax.ShapeDtypeStruct((t, D_FF), ACT), row, row,
                   jax.ShapeDtypeStruct((t, MIX_W), f32), jax.ShapeDtypeStruct((t, MIX_W), f32)],
        compiler_params=_cparams(1),
    )(dr2, hg, up, xh1, rs1, wd, wg, wu, g1, w_out)


def _inproj_bwd(dr1, du_hg, dqk, dmv, dmo, dgt, w_hg, w_ml):
    t = dr1.shape[0]
    tm = _tile(t, DENSE_ROWS)

    def body(dr_ref, dhg_ref, dqk_ref, dmv_ref, dmo_ref, dgt_ref, whg_ref, wml_ref, gx_ref, dml_ref):
        dml = jnp.concatenate([dqk_ref[...], dmv_ref[...], dmo_ref[...], dgt_ref[...]], axis=-1).astype(ACT)
        dml_ref[...] = dml
        gx_ref[...] = ALPHA * dr_ref[...] + _bdot(dhg_ref[...], whg_ref[...]) + _bdot(dml, wml_ref[...])

    return pl.pallas_call(
        body, name="inproj_bwd", grid=(t // tm,),
        in_specs=[_rows(tm, D_MODEL), _rows(tm, U_HG), _rows(tm, MIX_W), _rows(tm, MIX_W), _rows(tm, MIX_W), _rows(tm, 128),
                  _resident((U_HG, D_MODEL)), _resident((U_ML, D_MODEL))],
        out_specs=[_rows(tm, D_MODEL), _rows(tm, U_ML)],
        out_shape=[jax.ShapeDtypeStruct((t, D_MODEL), f32), jax.ShapeDtypeStruct((t, U_ML), ACT)],
        compiler_params=_cparams(1, arbitrary=False),
    )(dr1, du_hg, dqk, dmv, dmo, dgt, w_hg, w_ml)


def _wgrad(a, b, name, tk=None, tn=None, colsum=False, low=False):
    t, kdim = a.shape
    n = b.shape[1]
    tk = tk or kdim
    tn = tn or n
    tt = _tile(t, WGRAD_ROWS)
    nt = t // tt
    assert not (colsum and low) and (not colsum or tn == n)

    def body(a_ref, b_ref, o_ref, *s_ref):
        @pl.when(pl.program_id(2) == 0)
        def _():
            o_ref[...] = jnp.zeros_like(o_ref)
            if colsum:
                s_ref[0][...] = jnp.zeros_like(s_ref[0])

        av = a_ref[...]
        o_ref[...] += _bdot_tn(av, b_ref[...])
        if colsum:
            s_ref[0][...] += jnp.sum(av.astype(f32), axis=0, keepdims=True)
        if low:
            @pl.when(pl.program_id(2) == nt - 1)
            def _():
                s_ref[0][...] = o_ref[...].astype(bf16)

    out_specs = [pl.BlockSpec((tk, tn), lambda i, j, s: (i, j))]
    out_shape = [jax.ShapeDtypeStruct((kdim, n), f32)]
    if colsum:
        out_specs.append(pl.BlockSpec((1, tk), lambda i, j, s: (0, i)))
        out_shape.append(jax.ShapeDtypeStruct((1, kdim), f32))
    if low:
        out_specs.append(pl.BlockSpec((tk, tn), lambda i, j, s: (i, j)))
        out_shape.append(jax.ShapeDtypeStruct((kdim, n), bf16))
    res = pl.pallas_call(
        body, name=name, grid=(kdim // tk, n // tn, t // tt),
        in_specs=[pl.BlockSpec((tt, tk), lambda i, j, s: (s, i)), pl.BlockSpec((tt, tn), lambda i, j, s: (s, j))],
        out_specs=out_specs, out_shape=out_shape,
        compiler_params=_cparams(3),
    )(a, b)
    return res if (colsum or low) else res[0]


def _colsum(parts, name):
    t = parts[0].shape[0]
    tt = _tile(t, 512)
    widths = [a.shape[1] for a in parts]

    def body(*refs):
        o_ref = refs[-1]

        @pl.when(pl.program_id(0) == 0)
        def _():
            o_ref[...] = jnp.zeros_like(o_ref)

        off = 0
        for r, w in zip(refs[:-1], widths):
            o_ref[:, off:off + w] += jnp.sum(r[...].astype(f32), axis=0, keepdims=True)
            off += w

    return pl.pallas_call(
        body, name=name, grid=(t // tt,),
        in_specs=[_rows(tt, w) for w in widths],
        out_specs=_const((1, sum(widths))),
        out_shape=jax.ShapeDtypeStruct((1, sum(widths)), f32),
        compiler_params=_cparams(1),
    )(*parts)


_TRANSPOSED = {"w_in", "w_ffn_gate", "w_ffn_up"}
_COL_SPLIT = {"ple_w_proj"}
_SCATTER_PLAN = (("w_ffn_gate", "w_ffn_up"), ("w_ffn_down", "w_out", "ple_w_gate", "ple_w_proj"))
_RIDE_PLAN = {"inproj": ("w_ffn_gate",), "hgrn2_fwd": ("w_ffn_up",), "mlstm_fwd": ("w_out",),
              "outproj_ln1": ("ple_w_gate", "ple_w_proj"), "ffn_up": ("w_ffn_down",)}


def _from_chip_major(a, col_split):
    if col_split:
        return a.transpose(1, 0, 2).reshape(a.shape[1], 4 * a.shape[2])
    return a.reshape(4 * a.shape[1], a.shape[2])


def _local_step(x, p, tgt, w_in_b, b_in, logits, conv_w, conv_b, hg_gn, ml_gn, w_out_b, ln1_g, ln1_b,
                wg_b, wu_b, wd_b, ln2_g, ln2_b, w_pp_b, w_pg_b, b_pg, early_hook=None, late_shards=None):
    pad_w = U_HG + U_ML - PROJ_W
    w_hg = w_in_b[:U_HG]
    w_ml = jnp.pad(w_in_b[U_HG:], ((0, pad_w), (0, 0)))
    bb_hg = b_in[:, :U_HG]
    bb_ml = jnp.pad(b_in[:, U_HG:], ((0, 0), (0, pad_w)))

    late = dict(w_out=w_out_b, w_ffn_gate=wg_b, w_ffn_up=wu_b, w_ffn_down=wd_b, ple_w_proj=w_pp_b, ple_w_gate=w_pg_b)

    def riders_of(call):
        return [late_shards[k] for k in _RIDE_PLAN[call]] if late_shards is not None else ()

    def arrived(call, got):
        for k, g in zip(_RIDE_PLAN[call], got):
            late[k] = _from_chip_major(g, k in _COL_SPLIT)

    (u_hg, u_ml, xb), got = _inproj(x, w_hg, w_ml, bb_hg, bb_ml, riders_of("inproj"))
    arrived("inproj", got)
    (og_hg, sst, hg_b, hg_a, hg_o), got = _hgrn2_fwd(u_hg, logits, hg_gn, riders_of("hgrn2_fwd"))
    arrived("hgrn2_fwd", got)
    pre, qkc = _conv_fwd(u_ml, conv_w, conv_b)
    (og_ml, cst, nst, mst), got = _mlstm_fwd(qkc, u_ml, ml_gn, riders_of("mlstm_fwd"))
    arrived("mlstm_fwd", got)
    (x1, xh1, rs1, x1b), got = _outproj_ln1(og_hg, og_ml, x, late["w_out"], ln1_g, ln1_b, riders_of("outproj_ln1"))
    arrived("outproj_ln1", got)
    (hgp, up, act), got = _ffn_up(x1b, late["w_ffn_gate"], late["w_ffn_up"], riders_of("ffn_up"))
    arrived("ffn_up", got)
    w_out_b, wg_b, wu_b, wd_b = late["w_out"], late["w_ffn_gate"], late["w_ffn_up"], late["w_ffn_down"]
    w_pp_b, w_pg_b = late["ple_w_proj"], late["ple_w_gate"]
    x2, xh2, rs2, x2b = _ffn_down_ln2(act, x1, wd_b, ln2_g, ln2_b)
    dr2, de, dz, loss, d_bpg, d_ln2g, d_ln2b = _head_loss_bwd(x2, xh2, rs2, p, tgt, w_pg_b, b_pg, w_pp_b, ln2_g)
    dr1, dhg, dup, d_ln1g, d_ln1b, dog_hg, dog_ml = _ffn_bwd(dr2, hgp, up, xh1, rs1, wd_b, wg_b, wu_b, ln1_g, w_out_b)

    d_wo_a, lo_wo_a = _wgrad(og_hg, dr1, "wgrad_out_hg", low=True)
    d_wo_b, lo_wo_b = _wgrad(og_ml, dr1, "wgrad_out_ml", low=True)
    d_wg, lo_wg = _wgrad(dhg, x1b, "wgrad_ffn_gate", tk=D_FF // 2, low=True)
    d_wu, lo_wu = _wgrad(dup, x1b, "wgrad_ffn_up", tk=D_FF // 2, low=True)
    d_wd, lo_wd = _wgrad(act, dr2, "wgrad_ffn_down", tk=D_FF // 2, low=True)
    d_wpp, lo_wpp = _wgrad(p, de, "wgrad_ple_proj", low=True)
    d_wpg, lo_wpg = _wgrad(x2b, dz, "wgrad_ple_gate", low=True)
    early = dict(w_out=jnp.concatenate([d_wo_a, d_wo_b], axis=0), w_ffn_gate=d_wg, w_ffn_up=d_wu, w_ffn_down=d_wd,
                 ple_w_proj=d_wpp, ple_w_gate=d_wpg)
    early_low = dict(w_out=jnp.concatenate([lo_wo_a, lo_wo_b], axis=0), w_ffn_gate=lo_wg, w_ffn_up=lo_wu, w_ffn_down=lo_wd,
                     ple_w_proj=lo_wpp, ple_w_gate=lo_wpg)
    ride_hg, ride_ml = early_hook(early_low) if early_hook is not None else ((), ())

    (du_hg, d_logits, d_hg_gn), got_hg = _hgrn2_bwd(u_hg, logits, hg_gn, sst, hg_b, hg_a, hg_o, dog_hg, ride_hg)
    (dqkc, dmv, dmo, dgt, d_ml_gn), got_ml = _mlstm_bwd(qkc, u_ml, ml_gn, cst, nst, mst, dog_ml, ride_ml)
    dqk, d_conv_w, d_conv_b = _conv_bwd(u_ml, conv_w, pre, dqkc)
    grad_x, du_ml = _inproj_bwd(dr1, du_hg, dqk, dmv, dmo, dgt, w_hg, w_ml)

    dw_hg, db_hg = _wgrad(du_hg, xb, "wgrad_in_hg", tk=U_HG // 2, colsum=True)
    dw_ml, db_ml = _wgrad(du_ml, xb, "wgrad_in_ml", colsum=True)
    d_w_in = jnp.concatenate([dw_hg, dw_ml[:PROJ_W - U_HG]], axis=0)
    d_b_in = jnp.concatenate([db_hg, db_ml[:, :PROJ_W - U_HG]], axis=1)

    grads = dict(w_in=d_w_in, b_in=d_b_in, hg_lb_logits=d_logits, ml_conv_w=d_conv_w, ml_conv_b=d_conv_b,
                 hg_norm_g=d_hg_gn, ml_norm_g=d_ml_gn, ln1_g=d_ln1g, ln1_b=d_ln1b, ln2_g=d_ln2g, ln2_b=d_ln2b,
                 ple_b_gate=d_bpg, **early)
    return loss, grad_x, grads, (list(got_hg), list(got_ml))


_ANY = pl.BlockSpec(memory_space=pltpu.HBM)
_MESH = pl.DeviceIdType.MESH


def _my_place():
    return lax.axis_index("x"), lax.axis_index("y"), lax.axis_index("c")


def _other_chips(x, y):
    return [(1 - x, y), (x, 1 - y), (1 - x, 1 - y)]


def _allgather_weights(shards, taps, name):
    n = len(shards)
    halves = [s.shape[0] // 2 for s in shards]

    def body(*refs):
        ins, tap_in = refs[:n], refs[n]
        outs, tap_out = refs[n + 1:2 * n + 1], refs[2 * n + 1]
        send_sems, recv_sems, local_sems = refs[2 * n + 2:]
        x, y, c = _my_place()
        me = 2 * x + y
        sibling = (x, y, 1 - c)
        chips = _other_chips(x, y)

        def ici(a, j, block_chip):
            px, py = chips[j]
            src = ins[a].at[pl.ds(pl.multiple_of(c * halves[a], 16), halves[a])] if block_chip is None else outs[a].at[block_chip, c]
            dst = outs[a].at[me if block_chip is None else block_chip, c]
            return pltpu.make_async_remote_copy(src_ref=src, dst_ref=dst, send_sem=send_sems.at[6 * a + j], recv_sem=recv_sems.at[6 * a + j],
                                                device_id=(px, py, c), device_id_type=_MESH)

        def d2d(a, j, half):
            px, py = chips[j]
            blk = outs[a].at[2 * px + py, half]
            return pltpu.make_async_remote_copy(src_ref=blk, dst_ref=blk, send_sem=send_sems.at[6 * a + 3 + j], recv_sem=recv_sems.at[6 * a + 3 + j],
                                                device_id=sibling, device_id_type=_MESH)

        local = []
        for a in range(n):
            for h in range(2):
                cp = pltpu.make_async_copy(ins[a].at[pl.ds(h * halves[a], halves[a])], outs[a].at[me, h], local_sems.at[2 * a + h])
                cp.start()
                local.append(cp)
            for j in range(3):
                ici(a, j, None).start()
        tap_local = pltpu.make_async_copy(tap_in, tap_out.at[me], local_sems.at[2 * n])
        tap_local.start()
        tap_copies = []
        for j, (px, py) in enumerate(chips):
            cp = pltpu.make_async_remote_copy(src_ref=tap_in, dst_ref=tap_out.at[me], send_sem=send_sems.at[6 * n + j], recv_sem=recv_sems.at[6 * n + j],
                                              device_id=(px, py, c), device_id_type=_MESH)
            cp.start()
            tap_copies.append(cp)
        for a in range(n):
            for j, (px, py) in enumerate(chips):
                ici(a, j, 2 * px + py).wait_recv()
                d2d(a, j, c).start()
        for a in range(n):
            for j in range(3):
                d2d(a, j, 1 - c).wait_recv()
        for a in range(n):
            for j in range(3):
                ici(a, j, None).wait_send()
                d2d(a, j, c).wait_send()
        for j, (px, py) in enumerate(chips):
            pltpu.make_async_remote_copy(src_ref=tap_in, dst_ref=tap_out.at[2 * px + py], send_sem=send_sems.at[6 * n + j], recv_sem=recv_sems.at[6 * n + j],
                                         device_id=(px, py, c), device_id_type=_MESH).wait()
        for cp in local:
            cp.wait()
        tap_local.wait()

    res = pl.pallas_call(
        body, name=name,
        in_specs=[_ANY] * (n + 1), out_specs=[_ANY] * (n + 1),
        out_shape=[jax.ShapeDtypeStruct((4, 2, s.shape[0] // 2, s.shape[1]), s.dtype) for s in shards]
        + [jax.ShapeDtypeStruct((4,) + taps.shape, taps.dtype)],
        scratch_shapes=[pltpu.SemaphoreType.DMA((6 * n + 3,)), pltpu.SemaphoreType.DMA((6 * n + 3,)), pltpu.SemaphoreType.DMA((2 * n + 1,))],
    )(*shards, taps)
    return [w.reshape((4,) + s.shape) for w, s in zip(res[:n], shards)], res[n]


def _swap_halves(pieces, name):
    n = len(pieces)
    halves = [p.shape[1] // 2 for p in pieces]

    def body(*refs):
        ins, own, other = refs[:n], refs[n:2 * n], refs[2 * n:3 * n]
        send_sems, recv_sems, local_sems = refs[3 * n:]
        x, y, c = _my_place()

        def half_of(a, which):
            return ins[a].at[pl.ds(0, 4), pl.ds(pl.multiple_of(which * halves[a], 16), halves[a])]

        def to_sibling(a):
            return pltpu.make_async_remote_copy(src_ref=half_of(a, 1 - c), dst_ref=other[a], send_sem=send_sems.at[a], recv_sem=recv_sems.at[a],
                                                device_id=(x, y, 1 - c), device_id_type=_MESH)

        local = []
        for a in range(n):
            cp = pltpu.make_async_copy(half_of(a, c), own[a], local_sems.at[a])
            cp.start()
            local.append(cp)
            to_sibling(a).start()
        for a in range(n):
            to_sibling(a).wait()
            local[a].wait()

    shapes = [jax.ShapeDtypeStruct((4, p.shape[1] // 2, p.shape[2]), p.dtype) for p in pieces]
    res = pl.pallas_call(
        body, name=name,
        in_specs=[_ANY] * n, out_specs=[_ANY] * (2 * n), out_shape=shapes + shapes,
        scratch_shapes=[pltpu.SemaphoreType.DMA((n,)), pltpu.SemaphoreType.DMA((n,)), pltpu.SemaphoreType.DMA((n,))],
    )(*pieces)
    return res[:n], res[n:]


_VMEM = pl.BlockSpec(memory_space=pltpu.VMEM)
_EX_ROWS = 32


def _pair_reduce(p, name):
    s, r, c = p.shape
    half = r // 2

    def body(p_ref, o_ref, other, send_sem, recv_sem):
        x, y, cc = _my_place()
        theirs = pl.multiple_of((1 - cc) * half, 16)
        mine = pl.multiple_of(cc * half, 16)
        cp = pltpu.make_async_remote_copy(src_ref=p_ref.at[pl.ds(0, s), pl.ds(theirs, half)], dst_ref=other, send_sem=send_sem, recv_sem=recv_sem,
                                          device_id=(x, y, 1 - cc), device_id_type=_MESH)
        cp.start()
        cp.wait()

        def step(i, carry):
            r0 = pl.multiple_of(i * _EX_ROWS, _EX_ROWS)
            for slot in range(s):
                own_rows = pl.ds(pl.multiple_of(mine + r0, 16), _EX_ROWS)
                o_ref[slot, pl.ds(r0, _EX_ROWS), :] = (p_ref[slot, own_rows, :] + other[slot, pl.ds(r0, _EX_ROWS), :]).astype(bf16)
            return carry

        lax.fori_loop(0, half // _EX_ROWS, step, 0)

    return pl.pallas_call(
        body, name=name, in_specs=[_VMEM], out_specs=_VMEM,
        out_shape=jax.ShapeDtypeStruct((s, half, c), bf16),
        scratch_shapes=[pltpu.VMEM((s, half, c), f32), pltpu.SemaphoreType.DMA, pltpu.SemaphoreType.DMA],
        compiler_params=pltpu.CompilerParams(vmem_limit_bytes=VMEM_LIMIT),
    )(p)


def _chip_reduce_swap(rcv, name):
    s, h, c = rcv.shape

    def body(r_ref, g_ref, send_sem, recv_sem):
        x, y, cc = _my_place()

        def step(i, carry):
            r0 = pl.multiple_of(i * _EX_ROWS, _EX_ROWS)
            acc = r_ref[0, pl.ds(r0, _EX_ROWS), :].astype(f32)
            for slot in range(1, s):
                acc = acc + r_ref[slot, pl.ds(r0, _EX_ROWS), :].astype(f32)
            g_ref[cc, pl.ds(r0, _EX_ROWS), :] = acc
            return carry

        lax.fori_loop(0, h // _EX_ROWS, step, 0)
        cp = pltpu.make_async_remote_copy(src_ref=g_ref.at[cc], dst_ref=g_ref.at[cc], send_sem=send_sem, recv_sem=recv_sem,
                                          device_id=(x, y, 1 - cc), device_id_type=_MESH)
        cp.start()
        cp.wait()

    return pl.pallas_call(
        body, name=name, in_specs=[_VMEM], out_specs=_VMEM,
        out_shape=jax.ShapeDtypeStruct((2, h, c), f32),
        scratch_shapes=[pltpu.SemaphoreType.DMA, pltpu.SemaphoreType.DMA],
        compiler_params=pltpu.CompilerParams(vmem_limit_bytes=VMEM_LIMIT),
    )(rcv)


def _pair_reduce_cols(p, name):
    s, r, c = p.shape
    hc = c // 2

    def body(p_ref, o_ref, other, send_sem, recv_sem):
        x, y, cc = _my_place()

        def run(mine_lo, theirs_lo):
            cp = pltpu.make_async_remote_copy(src_ref=p_ref.at[pl.ds(0, s), pl.ds(0, r), pl.ds(theirs_lo, hc)], dst_ref=other,
                                              send_sem=send_sem, recv_sem=recv_sem, device_id=(x, y, 1 - cc), device_id_type=_MESH)
            cp.start()
            cp.wait()
            for slot in range(s):
                o_ref[slot] = (p_ref[slot, :, mine_lo:mine_lo + hc] + other[slot]).astype(bf16)

        @pl.when(cc == 0)
        def _():
            run(0, hc)

        @pl.when(cc == 1)
        def _():
            run(hc, 0)

    return pl.pallas_call(
        body, name=name, in_specs=[_VMEM], out_specs=_VMEM,
        out_shape=jax.ShapeDtypeStruct((s, r, hc), bf16),
        scratch_shapes=[pltpu.VMEM((s, r, hc), f32), pltpu.SemaphoreType.DMA, pltpu.SemaphoreType.DMA],
        compiler_params=pltpu.CompilerParams(vmem_limit_bytes=VMEM_LIMIT),
    )(p)


def _chip_reduce_swap_cols(rcv, name):
    s, r, hc = rcv.shape

    def body(r_ref, g_ref, send_sem, recv_sem):
        x, y, cc = _my_place()
        acc = r_ref[0].astype(f32)
        for slot in range(1, s):
            acc = acc + r_ref[slot].astype(f32)
        g_ref[cc] = acc
        cp = pltpu.make_async_remote_copy(src_ref=g_ref.at[cc], dst_ref=g_ref.at[cc], send_sem=send_sem, recv_sem=recv_sem,
                                          device_id=(x, y, 1 - cc), device_id_type=_MESH)
        cp.start()
        cp.wait()

    both = pl.pallas_call(
        body, name=name, in_specs=[_VMEM], out_specs=_VMEM,
        out_shape=jax.ShapeDtypeStruct((2, r, hc), f32),
        scratch_shapes=[pltpu.SemaphoreType.DMA, pltpu.SemaphoreType.DMA],
        compiler_params=pltpu.CompilerParams(vmem_limit_bytes=VMEM_LIMIT),
    )(rcv)
    return both.transpose(1, 0, 2).reshape(r, 2 * hc)


def _reduce_adamw(rcv, w, m, v, name):
    s, r, c = rcv.shape
    rows_per = _EX_ROWS

    def body(r_ref, w_ref, m_ref, v_ref, g_ref, d_ref, nm_ref, nv_ref, mine, theirs, send_sem, recv_sem):
        x, y, cc = _my_place()

        def chip_sum(i, carry):
            rs = pl.ds(pl.multiple_of(i * rows_per, rows_per), rows_per)
            acc = r_ref[0, rs, :].astype(f32)
            for slot in range(1, s):
                acc = acc + r_ref[slot, rs, :].astype(f32)
            mine[rs, :] = acc
            return carry

        lax.fori_loop(0, r // rows_per, chip_sum, 0)
        cp = pltpu.make_async_remote_copy(src_ref=mine, dst_ref=theirs, send_sem=send_sem, recv_sem=recv_sem,
                                          device_id=(x, y, 1 - cc), device_id_type=_MESH)
        cp.start()
        cp.wait()

        def update(i, carry):
            rs = pl.ds(pl.multiple_of(i * rows_per, rows_per), rows_per)
            g = mine[rs, :] + theirs[rs, :]
            nm = B1 * m_ref[rs, :] + (1.0 - B1) * g
            nv = B2 * v_ref[rs, :] + (1.0 - B2) * (g * g)
            g_ref[rs, :] = g
            nm_ref[rs, :] = nm
            nv_ref[rs, :] = nv
            d_ref[rs, :] = -LR * ((nm / (1.0 - B1 ** STEP)) / (jnp.sqrt(nv / (1.0 - B2 ** STEP)) + EPS_ADAM) + WD * w_ref[rs, :])
            return carry

        lax.fori_loop(0, r // rows_per, update, 0)

    return pl.pallas_call(
        body, name=name, in_specs=[_VMEM] * 4, out_specs=[_VMEM] * 4,
        out_shape=[jax.ShapeDtypeStruct((r, c), f32)] * 4,
        scratch_shapes=[pltpu.VMEM((r, c), f32), pltpu.VMEM((r, c), f32), pltpu.SemaphoreType.DMA, pltpu.SemaphoreType.DMA],
        compiler_params=pltpu.CompilerParams(vmem_limit_bytes=VMEM_LIMIT),
    )(rcv, w, m, v)


def _add_cast(a, b, name):
    s, r, c = a.shape
    tr = _row_tile(r, c)

    def body(a_ref, b_ref, o_ref):
        o_ref[...] = (a_ref[...] + b_ref[...]).astype(bf16)

    blk = pl.BlockSpec((1, tr, c), lambda i, j: (i, j, 0))
    return pl.pallas_call(
        body, name=name, grid=(s, r // tr), in_specs=[blk, blk], out_specs=blk,
        out_shape=jax.ShapeDtypeStruct(a.shape, bf16),
        compiler_params=_cparams(2, arbitrary=False),
    )(a, b)


def _gather_copies(ins, outs, send_sems, recv_sems, local_sems):
    x, y, c = _my_place()
    me = 2 * x + y
    local, outgoing, incoming = [], [], []
    for a in range(len(ins)):
        local.append(pltpu.make_async_copy(ins[a], outs[a].at[me], local_sems.at[a]))
        for j, (px, py) in enumerate(_other_chips(x, y)):
            sems = dict(send_sem=send_sems.at[3 * a + j], recv_sem=recv_sems.at[3 * a + j], device_id=(px, py, c), device_id_type=_MESH)
            outgoing.append(pltpu.make_async_remote_copy(src_ref=ins[a], dst_ref=outs[a].at[me], **sems))
            incoming.append(pltpu.make_async_remote_copy(src_ref=ins[a], dst_ref=outs[a].at[2 * px + py], **sems))
    return local, outgoing, incoming


def _gather_chips(blocks, name):
    n = len(blocks)

    def body(*refs):
        local, outgoing, incoming = _gather_copies(refs[:n], refs[n:2 * n], *refs[2 * n:])
        for cp in local + outgoing:
            cp.start()
        for cp in incoming:
            cp.wait_recv()
        for cp in outgoing:
            cp.wait_send()
        for cp in local:
            cp.wait()

    return pl.pallas_call(
        body, name=name, in_specs=[_ANY] * n, out_specs=[_ANY] * n, out_shape=_gather_shapes(blocks),
        scratch_shapes=[pltpu.SemaphoreType.DMA((3 * n,)), pltpu.SemaphoreType.DMA((3 * n,)), pltpu.SemaphoreType.DMA((n,))],
    )(*blocks)


def _gather_first(block, taps, name):
    r, c = block.shape
    hc = c // 2

    def body(in_ref, tap_in, out_ref, tap_out, send_sems, recv_sems):
        x, y, cc = _my_place()
        me = 2 * x + y
        sibling = (x, y, 1 - cc)
        chips = _other_chips(x, y)
        out_ref[me] = in_ref[...]
        tap_out[me] = tap_in[...]

        def run(mine, theirs):
            def ici(j, chip):
                px, py = chips[j]
                src = in_ref.at[pl.ds(0, r), pl.ds(mine, hc)] if chip is None else out_ref.at[chip, pl.ds(0, r), pl.ds(mine, hc)]
                dst = out_ref.at[me if chip is None else chip, pl.ds(0, r), pl.ds(mine, hc)]
                return pltpu.make_async_remote_copy(src_ref=src, dst_ref=dst, send_sem=send_sems.at[j], recv_sem=recv_sems.at[j],
                                                    device_id=(px, py, cc), device_id_type=_MESH)

            def d2d(j, lo):
                px, py = chips[j]
                blk = out_ref.at[2 * px + py, pl.ds(0, r), pl.ds(lo, hc)]
                return pltpu.make_async_remote_copy(src_ref=blk, dst_ref=blk, send_sem=send_sems.at[3 + j], recv_sem=recv_sems.at[3 + j],
                                                    device_id=sibling, device_id_type=_MESH)

            def tap(j, chip):
                px, py = chips[j]
                return pltpu.make_async_remote_copy(src_ref=tap_in, dst_ref=tap_out.at[me if chip is None else chip],
                                                    send_sem=send_sems.at[6 + j], recv_sem=recv_sems.at[6 + j],
                                                    device_id=(px, py, cc), device_id_type=_MESH)

            for j in range(3):
                ici(j, None).start()
                tap(j, None).start()
            for j, (px, py) in enumerate(chips):
                ici(j, 2 * px + py).wait_recv()
                d2d(j, mine).start()
            for j, (px, py) in enumerate(chips):
                d2d(j, theirs).wait_recv()
                tap(j, 2 * px + py).wait_recv()
            for j in range(3):
                ici(j, None).wait_send()
                d2d(j, mine).wait_send()
                tap(j, None).wait_send()

        @pl.when(cc == 0)
        def _():
            run(0, hc)

        @pl.when(cc == 1)
        def _():
            run(hc, 0)

    return pl.pallas_call(
        body, name=name, in_specs=[_VMEM, _VMEM], out_specs=[_VMEM, _VMEM],
        out_shape=[jax.ShapeDtypeStruct((4, r, c), block.dtype), jax.ShapeDtypeStruct((4,) + taps.shape, taps.dtype)],
        scratch_shapes=[pltpu.SemaphoreType.DMA((9,)), pltpu.SemaphoreType.DMA((9,))],
        compiler_params=pltpu.CompilerParams(vmem_limit_bytes=VMEM_LIMIT),
    )(block, taps)


def _riding_call(body, name, nsteps, in_specs, out_specs, out_shape, scratch_shapes, operands, riders, copies, ride_shapes):
    nr, n_in, n_out, n_scr = len(riders), len(in_specs), len(out_specs), len(scratch_shapes)

    def wrapped(*refs):
        ins, ride_in = refs[:n_in], refs[n_in:n_in + nr]
        outs, ride_out = refs[n_in + nr:n_in + nr + n_out], refs[n_in + nr + n_out:n_in + 2 * nr + n_out]
        scratch, sems = refs[n_in + 2 * nr + n_out:n_in + 2 * nr + n_out + n_scr], refs[n_in + 2 * nr + n_out + n_scr:]
        if nr:
            @pl.when(pl.program_id(0) == 0)
            def _():
                local, outgoing, _ = copies(ride_in, ride_out, *sems)
                for cp in local + outgoing:
                    cp.start()

        body(*ins, *outs, *scratch)
        if nr:
            @pl.when(pl.program_id(0) == nsteps - 1)
            def _():
                local, outgoing, incoming = copies(ride_in, ride_out, *sems)
                for cp in incoming:
                    cp.wait_recv()
                for cp in outgoing:
                    cp.wait_send()
                for cp in local:
                    cp.wait()

    hbm = pl.BlockSpec(memory_space=pltpu.HBM)
    sems = [pltpu.SemaphoreType.DMA((3 * nr,)), pltpu.SemaphoreType.DMA((3 * nr,)), pltpu.SemaphoreType.DMA((nr,))] if nr else []
    res = pl.pallas_call(
        wrapped, name=name, grid=(nsteps,),
        in_specs=list(in_specs) + [hbm] * nr, out_specs=list(out_specs) + [hbm] * nr,
        out_shape=list(out_shape) + list(ride_shapes),
        scratch_shapes=list(scratch_shapes) + sems,
        compiler_params=_cparams(1),
    )(*operands, *riders)
    return list(res[:n_out]), list(res[n_out:])


def _gather_shapes(riders):
    return [jax.ShapeDtypeStruct((4,) + r.shape, r.dtype) for r in riders]


def _scatter_copies(ins, outs, send_sems, recv_sems, local_sems):
    x, y, c = _my_place()
    me = 2 * x + y
    local, outgoing, incoming = [], [], []
    for a in range(len(ins)):
        local.append(pltpu.make_async_copy(ins[a].at[me], outs[a].at[me], local_sems.at[a]))
        for j, (px, py) in enumerate(_other_chips(x, y)):
            sems = dict(send_sem=send_sems.at[3 * a + j], recv_sem=recv_sems.at[3 * a + j], device_id=(px, py, c), device_id_type=_MESH)
            outgoing.append(pltpu.make_async_remote_copy(src_ref=ins[a].at[2 * px + py], dst_ref=outs[a].at[me], **sems))
            incoming.append(pltpu.make_async_remote_copy(src_ref=ins[a].at[2 * px + py], dst_ref=outs[a].at[2 * px + py], **sems))
    return local, outgoing, incoming


def _scatter_start(ins, outs, send_sems, recv_sems, local_sems):
    local, outgoing, _ = _scatter_copies(ins, outs, send_sems, recv_sems, local_sems)
    for cp in local + outgoing:
        cp.start()


def _scatter_wait(ins, outs, send_sems, recv_sems, local_sems):
    local, outgoing, incoming = _scatter_copies(ins, outs, send_sems, recv_sems, local_sems)
    for cp in incoming:
        cp.wait_recv()
    for cp in outgoing:
        cp.wait_send()
    for cp in local:
        cp.wait()


def _scatter_chips(pieces, name):
    n = len(pieces)

    def body(*refs):
        ins, outs = refs[:n], refs[n:2 * n]
        _scatter_start(ins, outs, *refs[2 * n:])
        _scatter_wait(ins, outs, *refs[2 * n:])

    return pl.pallas_call(
        body, name=name,
        in_specs=[_ANY] * n, out_specs=[_ANY] * n,
        out_shape=[jax.ShapeDtypeStruct(s.shape, s.dtype) for s in pieces],
        scratch_shapes=[pltpu.SemaphoreType.DMA((3 * n,)), pltpu.SemaphoreType.DMA((3 * n,)), pltpu.SemaphoreType.DMA((n,))],
    )(*pieces)


def _swap_cores(blocks, name):
    n = len(blocks)
    parts = 4
    rows = [b.shape[0] // parts for b in blocks]

    def body(*refs):
        ins, outs = refs[:n], refs[n:2 * n]
        send_sems, recv_sems, local_sems = refs[2 * n:]
        x, y, c = _my_place()

        def remote(a, k, slot):
            rs = pl.ds(k * rows[a], rows[a])
            return pltpu.make_async_remote_copy(src_ref=ins[a].at[rs], dst_ref=outs[a].at[slot, rs], send_sem=send_sems.at[parts * a + k],
                                                recv_sem=recv_sems.at[parts * a + k], device_id=(x, y, 1 - c), device_id_type=_MESH)

        local = []
        for a in range(n):
            cp = pltpu.make_async_copy(ins[a], outs[a].at[c], local_sems.at[a])
            cp.start()
            local.append(cp)
            for k in range(parts):
                remote(a, k, c).start()
        for a in range(n):
            for k in range(parts):
                remote(a, k, 1 - c).wait()
            local[a].wait()

    return pl.pallas_call(
        body, name=name,
        in_specs=[_ANY] * n, out_specs=[_ANY] * n,
        out_shape=[jax.ShapeDtypeStruct((2,) + s.shape, s.dtype) for s in blocks],
        scratch_shapes=[pltpu.SemaphoreType.DMA((parts * n,)), pltpu.SemaphoreType.DMA((parts * n,)), pltpu.SemaphoreType.DMA((n,))],
    )(*blocks)


def _gather_all(block, name):
    def body(in_ref, out_ref, send_sems, recv_sems, local_sem):
        x, y, c = _my_place()
        me = 4 * x + 2 * y + c
        cp = pltpu.make_async_copy(in_ref, out_ref.at[me], local_sem)
        cp.start()
        peers = []
        for dx in range(2):
            for dy in range(2):
                for dc in range(2):
                    if dx or dy or dc:
                        peers.append((1 - x if dx else x, 1 - y if dy else y, 1 - c if dc else c))
        for j, pr in enumerate(peers):
            pltpu.make_async_remote_copy(src_ref=in_ref, dst_ref=out_ref.at[me], send_sem=send_sems.at[j], recv_sem=recv_sems.at[j],
                                         device_id=pr, device_id_type=_MESH).start()
        for j, (px, py, pc) in enumerate(peers):
            pltpu.make_async_remote_copy(src_ref=in_ref, dst_ref=out_ref.at[4 * px + 2 * py + pc], send_sem=send_sems.at[j], recv_sem=recv_sems.at[j],
                                         device_id=(px, py, pc), device_id_type=_MESH).wait()
        cp.wait()

    return pl.pallas_call(
        body, name=name,
        in_specs=[_ANY], out_specs=_ANY,
        out_shape=jax.ShapeDtypeStruct((8,) + block.shape, block.dtype),
        scratch_shapes=[pltpu.SemaphoreType.DMA((7,)), pltpu.SemaphoreType.DMA((7,)), pltpu.SemaphoreType.DMA],
    )(block)


def _row_tile(r, c):
    best = r
    for cand in range(16, r + 1, 16):
        if r % cand == 0 and cand * c * 4 <= (1 << 20):
            best = cand
    return best if best * c * 4 <= (4 << 20) else r


def _sum_slots(parts, name):
    n, r, c = parts.shape
    tr = _row_tile(r, c)

    def body(p_ref, o_ref):
        acc = p_ref[0].astype(f32)
        for s in range(1, n):
            acc = acc + p_ref[s].astype(f32)
        o_ref[...] = acc

    return pl.pallas_call(
        body, name=name, grid=(r // tr,),
        in_specs=[pl.BlockSpec((n, tr, c), lambda i: (0, i, 0))],
        out_specs=pl.BlockSpec((tr, c), lambda i: (i, 0)),
        out_shape=jax.ShapeDtypeStruct((r, c), f32),
        compiler_params=_cparams(1, arbitrary=False),
    )(parts)


def _adamw(parts, w, m, v, name):
    n, r, c = parts.shape
    tr = _row_tile(r, c)
    tc = c
    if tr == r and r * c * 4 > (1 << 20) and c % 256 == 0:
        tc = 256

    def body(p_ref, w_ref, m_ref, v_ref, g_ref, d_ref, nm_ref, nv_ref):
        g = p_ref[0]
        for s in range(1, n):
            g = g + p_ref[s]
        nm = B1 * m_ref[...] + (1.0 - B1) * g
        nv = B2 * v_ref[...] + (1.0 - B2) * (g * g)
        m_hat = nm / (1.0 - B1 ** STEP)
        v_hat = nv / (1.0 - B2 ** STEP)
        g_ref[...] = g
        nm_ref[...] = nm
        nv_ref[...] = nv
        d_ref[...] = -LR * (m_hat / (jnp.sqrt(v_hat) + EPS_ADAM) + WD * w_ref[...])

    blk = pl.BlockSpec((tr, tc), lambda i, j: (i, j))
    return pl.pallas_call(
        body, name=name, grid=(r // tr, c // tc),
        in_specs=[pl.BlockSpec((n, tr, tc), lambda i, j: (0, i, j)), blk, blk, blk],
        out_specs=[blk] * 4,
        out_shape=[jax.ShapeDtypeStruct((r, c), f32)] * 4,
        compiler_params=_cparams(2, arbitrary=False),
    )(parts, w, m, v)


_BIG = ["w_in", "w_out", "w_ffn_gate", "w_ffn_up", "w_ffn_down", "ple_w_proj", "ple_w_gate"]
_SMALL = ["b_in", "hg_lb_logits", "ml_conv_w", "ml_conv_b", "hg_norm_g", "ml_norm_g", "ln1_g", "ln1_b", "ln2_g", "ln2_b", "ple_b_gate"]
_ORDER = ["w_in", "b_in", "hg_lb_logits", "ml_conv_w", "ml_conv_b", "hg_norm_g", "ml_norm_g", "w_out", "ln1_g", "ln1_b",
          "w_ffn_gate", "w_ffn_up", "w_ffn_down", "ln2_g", "ln2_b", "ple_w_proj", "ple_w_gate", "ple_b_gate"]
_PACK_ROWS, _PACK_COLS = 16, 1024


def _pack(arrays):
    flat = jnp.concatenate([a.reshape(-1) for a in arrays])
    return jnp.pad(flat, (0, _PACK_ROWS * _PACK_COLS - flat.shape[0])).reshape(_PACK_ROWS, _PACK_COLS)


def _unpack(pack, shapes):
    flat = pack.reshape(-1)
    out, off = [], 0
    for s in shapes:
        size = 1
        for d in s:
            size *= d
        out.append(flat[off:off + size].reshape(s))
        off += size
    return out


def _to_chip_major(g, col_split):
    if col_split:
        k, n = g.shape
        return g.reshape(k, 4, n // 4).transpose(1, 0, 2)
    k, n = g.shape
    return g.reshape(4, k // 4, n)


def kernel(x, p, w_in, b_in, hg_lb_logits, ml_conv_w, ml_conv_b, hg_norm_g, ml_norm_g, w_out, ln1_g, ln1_b, w_ffn_gate, w_ffn_up, w_ffn_down, ln2_g, ln2_b, ple_w_proj, ple_w_gate, ple_b_gate, loss_target, m_w_in, m_b_in, m_hg_lb_logits, m_ml_conv_w, m_ml_conv_b, m_hg_norm_g, m_ml_norm_g, m_w_out, m_ln1_g, m_ln1_b, m_w_ffn_gate, m_w_ffn_up, m_w_ffn_down, m_ln2_g, m_ln2_b, m_ple_w_proj, m_ple_w_gate, m_ple_b_gate, v_w_in, v_b_in, v_hg_lb_logits, v_ml_conv_w, v_ml_conv_b, v_hg_norm_g, v_ml_norm_g, v_w_out, v_ln1_g, v_ln1_b, v_w_ffn_gate, v_w_ffn_up, v_w_ffn_down, v_ln2_g, v_ln2_b, v_ple_w_proj, v_ple_w_gate, v_ple_b_gate):
    args = dict(locals())
    wts = {k: args[k] for k in _ORDER}
    mom = {k: args["m_" + k] for k in _ORDER}
    var = {k: args["v_" + k] for k in _ORDER}
    two_d = lambda a: a.reshape(a.shape[-2], a.shape[-1])
    block = lambda k, a: jnp.swapaxes(two_d(a), 0, 1) if k in _TRANSPOSED else two_d(a)
    unblock = lambda k, a: (jnp.swapaxes(a, 0, 1) if k in _TRANSPOSED else a).reshape(wts[k].shape)

    shards = {k: block(k, wts[k]).astype(bf16) for k in _BIG}
    w_in_blocks, taps = _gather_first(shards["w_in"], two_d(ml_conv_w), "gather_w_in")
    w_in_full = _from_chip_major(w_in_blocks, False)
    conv_w_full = _from_chip_major(taps, True)

    def core_sum(k, g):
        pieces = _to_chip_major(g, k in _COL_SPLIT)
        if pieces.shape[1] % (2 * _EX_ROWS):
            return _pair_reduce_cols(pieces, "pair_reduce_" + k)
        return _pair_reduce(pieces, "pair_reduce_" + k)

    early_keys = _BIG[1:]
    loss, grad_x, grads, (got_hg, got_ml) = _local_step(
        x[0], p[0, 0], loss_target[0], w_in_full, b_in, hg_lb_logits, conv_w_full, ml_conv_b, hg_norm_g, ml_norm_g,
        None, ln1_g, ln1_b, None, None, None, ln2_g, ln2_b, None, None, ple_b_gate,
        early_hook=lambda low: tuple([_to_chip_major(low[k], k in _COL_SPLIT) for k in names] for names in _SCATTER_PLAN),
        late_shards={k: shards[k] for k in early_keys})

    out_g, out_d, out_m, out_v = {}, {}, {}, {}

    def finish(k, g, d, nm, nv):
        out_g[k], out_d[k], out_m[k], out_v[k] = unblock(k, g), unblock(k, d), unblock(k, nm), unblock(k, nv)

    for names, got in zip(_SCATTER_PLAN, (got_hg, got_ml)):
        for k, rcv in zip(names, got):
            finish(k, *_reduce_adamw(rcv, block(k, wts[k]), block(k, mom[k]), block(k, var[k]), "reduce_adamw_" + k))

    rcv = _scatter_chips([core_sum("w_in", grads["w_in"])], "scatter_grad_w_in")[0]
    own = block("w_in", wts["w_in"])
    if rcv.shape[1] == own.shape[0]:
        whole = _chip_reduce_swap_cols(rcv, "chip_reduce_w_in")
    else:
        parts = _chip_reduce_swap(rcv, "chip_reduce_w_in")
        whole = parts.reshape(2 * parts.shape[1], parts.shape[2])
    finish("w_in", *_adamw(whole[None], own, block("w_in", mom["w_in"]), block("w_in", var["w_in"]), "adamw_w_in"))

    small_shapes = [(1, PROJ_W), (2, MIX_W), (CONV_K, MIX_W)] + [(1, MIX_W)] * 3 + [(1, D_MODEL)] * 5 + [(1, 1)]
    contrib = _pack([grads[k] for k in _SMALL] + [loss])
    summed = _sum_slots(_gather_all(contrib, "gather_small"), "sum_small")
    small = _unpack(summed, small_shapes)
    loss_total = small[-1].reshape(())
    gsm = dict(zip(_SMALL, small[:-1]))
    place = 2 * lax.axis_index("x") + lax.axis_index("y")
    conv_cols = ml_conv_w.shape[-1]
    gsm["ml_conv_w"] = lax.dynamic_slice(gsm["ml_conv_w"], (0, place * conv_cols), (CONV_K, conv_cols))
    own_shapes = [wts[k].shape for k in _SMALL]
    g_pack = _pack([gsm[k] for k in _SMALL])
    res = _adamw(g_pack[None], _pack([wts[k] for k in _SMALL]), _pack([mom[k] for k in _SMALL]), _pack([var[k] for k in _SMALL]), "adamw_small")
    for dst, pack in zip((out_g, out_d, out_m, out_v), res):
        for k, a in zip(_SMALL, _unpack(pack, own_shapes)):
            dst[k] = a

    outs = [loss_total, grad_x[None]]
    for group in (out_g, out_d, out_m, out_v):
        outs += [group[k] for k in _ORDER]
    return tuple(outs)
```

```python
import jax
import jax.numpy as jnp
from jax import lax
from jax.experimental import pallas as pl
from jax.experimental.pallas import tpu as pltpu

f32 = jnp.float32
bf16 = jnp.bfloat16

D_MODEL = 1024
HEADS = 4
HEAD_W = 128
MIX_W = HEADS * HEAD_W
ML_DQK = 64
PROJ_W = 3592
U_HG = 4 * MIX_W
U_ML = 3 * MIX_W + 128
D_FF = 2816
PLE = 256
CHUNK = 128
SUB = 16
EXP_CAP = 80.0
CONV_K = 4
HALO = 8
ALPHA = float(2.0 ** 0.25)
LN_EPS = 1e-5
RMS_EPS = 1e-6
NEG = -1e30
LR, B1, B2, EPS_ADAM, WD, STEP = 0.001, 0.9, 0.999, 1e-08, 0.01, 10
VMEM_LIMIT = 56 * 1024 * 1024
DENSE_ROWS = 512
WGRAD_ROWS = 2048


def _cparams(n_axes, arbitrary=True):
    sem = ("arbitrary",) * n_axes if arbitrary else ("parallel",) * n_axes
    return pltpu.CompilerParams(dimension_semantics=sem, vmem_limit_bytes=VMEM_LIMIT)


ACT = bf16


def _mx(a):
    return a.astype(ACT)


def _bdot(a, b):
    return jnp.dot(_mx(a), _mx(b), preferred_element_type=f32)


def _bdot_nt(a, b):
    return lax.dot_general(_mx(a), _mx(b), (((1,), (1,)), ((), ())), preferred_element_type=f32)


def _bdot_tn(a, b):
    return lax.dot_general(_mx(a), _mx(b), (((0,), (0,)), ((), ())), preferred_element_type=f32)


def _split3(x):
    hi = x.astype(bf16)
    r1 = x - hi.astype(f32)
    mid = r1.astype(bf16)
    lo = (r1 - mid.astype(f32)).astype(bf16)
    return hi, mid, lo


def _dot3(a, b, dims):
    a_hi = a.astype(bf16)
    a_lo = (a - a_hi.astype(f32)).astype(bf16)
    b_hi = b.astype(bf16)
    b_lo = (b - b_hi.astype(f32)).astype(bf16)
    dn = (dims, ((), ()))
    return (lax.dot_general(a_hi, b_hi, dn, preferred_element_type=f32) + lax.dot_general(a_hi, b_lo, dn, preferred_element_type=f32)
            + lax.dot_general(a_lo, b_hi, dn, preferred_element_type=f32))


def _lane_sum(x):
    hi = x.astype(bf16)
    lo = (x - hi.astype(f32)).astype(bf16)
    ones = jnp.ones((x.shape[1], 128), bf16)
    return jnp.dot(hi, ones, preferred_element_type=f32) + jnp.dot(lo, ones, preferred_element_type=f32)


def _lane_dot(x, row):
    return _dot3(x, jnp.broadcast_to(row, (128, row.shape[1])), ((1,), (1,)))


def _sel_dot(sel, x):
    sb = sel.astype(bf16)
    return sum(jnp.dot(sb, part, preferred_element_type=f32) for part in _split3(x))


def _sel_dot_nt(sel, x):
    sb = sel.astype(bf16)
    return sum(lax.dot_general(sb, part, (((1,), (1,)), ((), ())), preferred_element_type=f32) for part in _split3(x))


def _sigmoid(x):
    return 1.0 / (1.0 + jnp.exp(-x))


def _log_sigmoid(x):
    return jnp.minimum(x, 0.0) - jnp.log(1.0 + jnp.exp(-jnp.abs(x)))


def _tri(n, upper=False):
    r = lax.broadcasted_iota(jnp.int32, (n, n), 0)
    c = lax.broadcasted_iota(jnp.int32, (n, n), 1)
    return (c >= r) if upper else (c <= r)


def _rows(tm, n, col=0):
    return pl.BlockSpec((tm, n), lambda i, _c=col: (i, _c))


def _rows_rev(tm, n, nb, col=0):
    return pl.BlockSpec((tm, n), lambda i, _c=col, _nb=nb: (_nb - 1 - i, _c))


def _const(shape):
    return pl.BlockSpec(shape, lambda i, _n=len(shape): (0,) * _n)


def _resident(shape):
    return pl.BlockSpec(shape, lambda i, _n=len(shape): (0,) * _n, pipeline_mode=pl.Buffered(1))


def _tile(t, want):
    return want if t % want == 0 else t


def _inproj(x, w_hg, w_ml, b_hg, b_ml, riders=()):
    t = x.shape[0]
    tm = _tile(t, DENSE_ROWS)

    def body(x_ref, whg_ref, wml_ref, bhg_ref, bml_ref, uhg_ref, uml_ref, xb_ref):
        xb = _mx(x_ref[...])
        xb_ref[...] = xb
        uhg_ref[...] = _bdot_nt(xb, whg_ref[...]) + bhg_ref[...]
        uml_ref[...] = _bdot_nt(xb, wml_ref[...]) + bml_ref[...]

    return _riding_call(
        body, "inproj", t // tm,
        in_specs=[_rows(tm, D_MODEL), _resident((U_HG, D_MODEL)), _resident((U_ML, D_MODEL)), _const((1, U_HG)), _const((1, U_ML))],
        out_specs=[_rows(tm, U_HG), _rows(tm, U_ML), _rows(tm, D_MODEL)],
        out_shape=[jax.ShapeDtypeStruct((t, U_HG), f32), jax.ShapeDtypeStruct((t, U_ML), f32), jax.ShapeDtypeStruct((t, D_MODEL), ACT)],
        scratch_shapes=[], operands=(x, w_hg, w_ml, b_hg, b_ml), riders=riders, copies=_gather_copies, ride_shapes=_gather_shapes(riders))


def _hg_gates(hq, hf, lb, tri, b=None):
    s = _sigmoid(hf)
    om = 1.0 - lb
    f = lb + om * s
    k = om * (1.0 - s)
    sq = _sigmoid(hq)
    q = hq * sq
    if b is None:
        b = _sel_dot(tri, jnp.log(f))
    return q, sq, s, f, k, b


def _hg_scores(q, k, b, tril_mask, a=None):
    qts, kts, eqs, eks, rows = [], [], [], [], []
    for i in range(CHUNK // SUB):
        lo = i * SUB
        ref = jnp.zeros_like(b[0:1]) if i == 0 else b[lo - 1:lo]
        eq = jnp.exp(b[lo:lo + SUB] - ref)
        ek = jnp.exp(jnp.minimum(ref - b, EXP_CAP))
        qt = q[lo:lo + SUB] * eq
        kt = k * ek
        if a is None:
            rows.append(_bdot_nt(qt, kt))
        qts.append(qt); kts.append(kt); eqs.append(eq); eks.append(ek)
    if a is None:
        a = jnp.where(tril_mask, jnp.concatenate(rows, axis=0), 0.0)
    return a, qts, kts, eqs, eks


def _head_rms(o, gn, on_mxu=False):
    ms = _lane_sum(o * o) * (1.0 / o.shape[1]) if on_mxu else jnp.mean(o * o, axis=-1, keepdims=True)
    rstd = lax.rsqrt(ms + RMS_EPS)
    oh = o * rstd
    return oh, rstd, oh * gn


def _lower_bound(logit_ref):
    lg = logit_ref[...]
    return _sigmoid(lg[0:1] - lg[1:2])


def _hgrn2_fwd(u_hg, logits, gn, riders=()):
    t = u_hg.shape[0]
    tb = _tile(t, 256)
    nc_blk = tb // CHUNK

    def body(u_ref, lg_ref, gn_ref, og_ref, sst_ref, b_ref, a_ref, o_ref, st_ref):
        @pl.when(pl.program_id(0) == 0)
        def _():
            st_ref[...] = jnp.zeros_like(st_ref)

        lb_all = _lower_bound(lg_ref)
        tril_mask = _tri(CHUNK)
        tri = tril_mask.astype(f32)

        def chunk(c, carry):
            r0 = pl.multiple_of(c * CHUNK, CHUNK)
            rows = pl.ds(r0, CHUNK)
            heads = range(HEADS)
            cols = [slice(h * HEAD_W, (h + 1) * HEAD_W) for h in heads]
            hv = [u_ref[rows, 2 * MIX_W + h * HEAD_W:2 * MIX_W + (h + 1) * HEAD_W] for h in heads]
            gts = [_hg_gates(u_ref[rows, h * HEAD_W:(h + 1) * HEAD_W], u_ref[rows, MIX_W + h * HEAD_W:MIX_W + (h + 1) * HEAD_W],
                             lb_all[:, cols[h]], tri) for h in heads]
            q = [g[0] for g in gts]
            k = [g[4] for g in gts]
            b = [g[5] for g in gts]
            a = [_hg_scores(q[h], k[h], b[h], tril_mask)[0] for h in heads]
            st = [st_ref[h] for h in heads]
            bl = [b[h][CHUNK - 1:CHUNK] for h in heads]
            o = [_bdot(a[h], hv[h]) + _bdot_nt(q[h] * jnp.exp(b[h]), st[h]) for h in heads]
            new_st = [st[h] * jnp.exp(bl[h]) + _bdot_tn(hv[h], k[h] * jnp.exp(bl[h] - b[h])) for h in heads]
            for h in heads:
                sst_ref[c, h] = st[h]
                st_ref[h] = new_st[h]
                b_ref[rows, cols[h]] = b[h]
                a_ref[rows, cols[h]] = a[h].astype(ACT)
                o_ref[rows, cols[h]] = o[h]
                hgate = u_ref[rows, 3 * MIX_W + h * HEAD_W:3 * MIX_W + (h + 1) * HEAD_W]
                _, _, y = _head_rms(o[h], gn_ref[:, cols[h]])
                og_ref[rows, cols[h]] = (y * (hgate * _sigmoid(hgate))).astype(ACT)
            return carry

        lax.fori_loop(0, nc_blk, chunk, 0, unroll=True)

    assert CHUNK == HEAD_W
    return _riding_call(
        body, "hgrn2_fwd", t // tb,
        in_specs=[_rows(tb, U_HG), _const((2, MIX_W)), _const((1, MIX_W))],
        out_specs=[_rows(tb, MIX_W), pl.BlockSpec((nc_blk, HEADS, HEAD_W, HEAD_W), lambda i: (i, 0, 0, 0)),
                   _rows(tb, MIX_W), _rows(tb, MIX_W), _rows(tb, MIX_W)],
        out_shape=[jax.ShapeDtypeStruct((t, MIX_W), ACT), jax.ShapeDtypeStruct((t // CHUNK, HEADS, HEAD_W, HEAD_W), f32),
                   jax.ShapeDtypeStruct((t, MIX_W), f32), jax.ShapeDtypeStruct((t, MIX_W), ACT), jax.ShapeDtypeStruct((t, MIX_W), f32)],
        scratch_shapes=[pltpu.VMEM((HEADS, HEAD_W, HEAD_W), f32)],
        operands=(u_hg, logits, gn), riders=riders, copies=_gather_copies, ride_shapes=_gather_shapes(riders))


def _hgrn2_bwd(u_hg, logits, gn, sst, bcum, scores, o_raw, dog, riders=()):
    t = u_hg.shape[0]
    tb = _tile(t, 256)
    nb = t // tb
    nc_blk = tb // CHUNK

    def body(u_ref, lg_ref, gn_ref, sst_ref, b_ref, a_ref, o_ref, dog_ref, du_ref, dlg_ref, dgn_ref, dst_ref):
        @pl.when(pl.program_id(0) == 0)
        def _():
            dst_ref[...] = jnp.zeros_like(dst_ref)
            dlg_ref[...] = jnp.zeros_like(dlg_ref)
            dgn_ref[...] = jnp.zeros_like(dgn_ref)

        lb_all = _lower_bound(lg_ref)
        tril_mask = _tri(CHUNK)
        tri = tril_mask.astype(f32)
        triu = _tri(CHUNK, upper=True).astype(f32)

        def chunk(j, carry):
            c = nc_blk - 1 - j
            r0 = pl.multiple_of(c * CHUNK, CHUNK)
            rows = pl.ds(r0, CHUNK)
            heads = range(HEADS)
            nsub = CHUNK // SUB
            cols = [slice(h * HEAD_W, (h + 1) * HEAD_W) for h in heads]
            hq = [u_ref[rows, h * HEAD_W:(h + 1) * HEAD_W] for h in heads]
            hf = [u_ref[rows, MIX_W + h * HEAD_W:MIX_W + (h + 1) * HEAD_W] for h in heads]
            hv = [u_ref[rows, 2 * MIX_W + h * HEAD_W:2 * MIX_W + (h + 1) * HEAD_W] for h in heads]
            lb = [lb_all[:, cols[h]] for h in heads]
            gts = [_hg_gates(hq[h], hf[h], lb[h], tri, b=b_ref[rows, cols[h]]) for h in heads]
            q, sq, s, f, k, b = ([g[n] for g in gts] for n in range(6))
            scs = [_hg_scores(q[h], k[h], b[h], tril_mask, a=a_ref[rows, cols[h]]) for h in heads]
            a, qts, kts, eqs, eks = ([sc[n] for sc in scs] for n in range(5))
            st = [sst_ref[c, h] for h in heads]
            dst = [dst_ref[h] for h in heads]
            bl = [b[h][CHUNK - 1:CHUNK] for h in heads]
            eb = [jnp.exp(b[h]) for h in heads]
            qh = [q[h] * eb[h] for h in heads]
            ekl = [jnp.exp(bl[h] - b[h]) for h in heads]
            kh = [k[h] * ekl[h] for h in heads]
            o = [o_ref[rows, cols[h]] for h in heads]
            do = []
            for h in heads:
                hgate = u_ref[rows, 3 * MIX_W + h * HEAD_W:3 * MIX_W + (h + 1) * HEAD_W]
                gnh = gn_ref[:, cols[h]]
                oh, rstd, y = _head_rms(o[h], gnh)
                sg = _sigmoid(hgate)
                dogh = dog_ref[rows, cols[h]]
                dy = dogh * (hgate * sg)
                du_ref[rows, 3 * MIX_W + h * HEAD_W:3 * MIX_W + (h + 1) * HEAD_W] = (dogh * y * (sg * (1.0 + hgate * (1.0 - sg)))).astype(ACT)
                dgn_ref[:, cols[h]] += jnp.sum(dy * oh, axis=0, keepdims=True)
                doh = dy * gnh
                do.append(rstd * (doh - oh * jnp.mean(doh * oh, axis=-1, keepdims=True)))
            da = [jnp.where(tril_mask, _bdot_nt(do[h], hv[h]), 0.0) for h in heads]
            dv = [_bdot_tn(a[h], do[h]) + _bdot_nt(kh[h], dst[h]) for h in heads]
            dq = [_bdot(do[h], st[h]) * eb[h] for h in heads]
            dk = [_bdot(hv[h], dst[h]) * ekl[h] for h in heads]
            d_last = [jnp.sum(k[h] * dk[h], axis=0, keepdims=True) + jnp.exp(bl[h]) * jnp.sum(dst[h] * st[h], axis=0, keepdims=True)
                      for h in heads]
            d_b = [q[h] * dq[h] - k[h] * dk[h] for h in heads]
            dqs = [[] for _ in heads]
            q_dq = [[] for _ in heads]
            for i in range(nsub):
                for h in heads:
                    da_i = _mx(da[h][i * SUB:(i + 1) * SUB])
                    q_r, k_r = _mx(qts[h][i]), _mx(kts[h][i])
                    g_q = jnp.dot(da_i, k_r, preferred_element_type=f32)
                    g_k = lax.dot_general(da_i, q_r, (((0,), (0,)), ((), ())), preferred_element_type=f32)
                    dqs[h].append(g_q * eqs[h][i])
                    q_dq[h].append(q_r.astype(f32) * g_q)
                    dk[h] = dk[h] + g_k * eks[h][i]
                    d_b[h] = d_b[h] - k_r.astype(f32) * g_k
            for h in heads:
                dq[h] = dq[h] + jnp.concatenate(dqs[h], axis=0)
                d_b[h] = d_b[h] + jnp.concatenate(q_dq[h], axis=0)
                dst_ref[h] = dst[h] * jnp.exp(bl[h]) + _bdot_tn(do[h], qh[h])
            dg = [_sel_dot(triu, d_b[h]) + d_last[h] for h in heads]
            for h in heads:
                dfk = dg[h] / f[h] - dk[h]
                du_ref[rows, h * HEAD_W:(h + 1) * HEAD_W] = (dq[h] * (sq[h] * (1.0 + hq[h] * (1.0 - sq[h])))).astype(ACT)
                du_ref[rows, MIX_W + h * HEAD_W:MIX_W + (h + 1) * HEAD_W] = ((1.0 - lb[h]) * dfk * s[h] * (1.0 - s[h])).astype(ACT)
                du_ref[rows, 2 * MIX_W + h * HEAD_W:2 * MIX_W + (h + 1) * HEAD_W] = dv[h].astype(ACT)
                dlb = jnp.sum((1.0 - s[h]) * dfk, axis=0, keepdims=True) * (lb[h] * (1.0 - lb[h]))
                dlg_ref[0:1, cols[h]] += dlb
                dlg_ref[1:2, cols[h]] -= dlb
            return carry

        lax.fori_loop(0, nc_blk, chunk, 0, unroll=True)

    rev = _rows_rev(tb, MIX_W, nb)
    return _riding_call(
        body, "hgrn2_bwd", nb,
        in_specs=[_rows_rev(tb, U_HG, nb), _const((2, MIX_W)), _const((1, MIX_W)),
                  pl.BlockSpec((nc_blk, HEADS, HEAD_W, HEAD_W), lambda i: (nb - 1 - i, 0, 0, 0)), rev, rev, rev, rev],
        out_specs=[_rows_rev(tb, U_HG, nb), _const((2, MIX_W)), _const((1, MIX_W))],
        out_shape=[jax.ShapeDtypeStruct((t, U_HG), ACT), jax.ShapeDtypeStruct((2, MIX_W), f32), jax.ShapeDtypeStruct((1, MIX_W), f32)],
        scratch_shapes=[pltpu.VMEM((HEADS, HEAD_W, HEAD_W), f32)],
        operands=(u_hg, logits, gn, sst, bcum, scores, o_raw, dog), riders=riders, copies=_scatter_copies,
        ride_shapes=[jax.ShapeDtypeStruct(r.shape, r.dtype) for r in riders])


def _conv_fwd(u_ml, w, b):
    t = u_ml.shape[0]
    tm = _tile(t, 512)

    def body(x_ref, w_ref, b_ref, pre_ref, act_ref, xbuf):
        @pl.when(pl.program_id(0) == 0)
        def _():
            xbuf[...] = jnp.zeros_like(xbuf)

        xbuf[0:HALO, :] = xbuf[tm:tm + HALO, :]
        xbuf[HALO:HALO + tm, :] = x_ref[...]
        pre = b_ref[...] + jnp.zeros((tm, MIX_W), f32)
        for kk in range(CONV_K):
            off = HALO - (CONV_K - 1) + kk
            pre = pre + w_ref[kk:kk + 1, :] * xbuf[off:off + tm, :]
        pre_ref[...] = pre
        act_ref[...] = pre * _sigmoid(pre)

    return pl.pallas_call(
        body, name="conv_fwd", grid=(t // tm,),
        in_specs=[_rows(tm, MIX_W), _const((CONV_K, MIX_W)), _const((1, MIX_W))],
        out_specs=[_rows(tm, MIX_W), _rows(tm, MIX_W)],
        out_shape=[jax.ShapeDtypeStruct((t, MIX_W), f32)] * 2,
        scratch_shapes=[pltpu.VMEM((tm + HALO, MIX_W), f32)],
        compiler_params=_cparams(1),
    )(u_ml, w, b)


def _conv_bwd(u_ml, w, pre, dact):
    t = u_ml.shape[0]
    tm = _tile(t, 512)
    nb = t // tm
    hb = tm // HALO

    def body(x_ref, halo_ref, w_ref, pre_ref, dact_ref, dx_ref, dw_ref, db_ref, dbuf, xbuf):
        i = pl.program_id(0)

        @pl.when(i == 0)
        def _():
            dbuf[...] = jnp.zeros_like(dbuf)
            dw_ref[...] = jnp.zeros_like(dw_ref)
            db_ref[...] = jnp.zeros_like(db_ref)

        p = pre_ref[...]
        sg = _sigmoid(p)
        dpre = dact_ref[...] * (sg * (1.0 + p * (1.0 - sg)))
        dbuf[tm:tm + HALO, :] = dbuf[0:HALO, :]
        dbuf[0:tm, :] = dpre
        has_prev = (i < nb - 1).astype(f32)
        xbuf[0:HALO, :] = halo_ref[...] * has_prev
        xbuf[HALO:HALO + tm, :] = x_ref[...]
        dx = jnp.zeros((tm, MIX_W), f32)
        for kk in range(CONV_K):
            back = CONV_K - 1 - kk
            dx = dx + w_ref[kk:kk + 1, :] * dbuf[back:back + tm, :]
            off = HALO - (CONV_K - 1) + kk
            dw_ref[kk:kk + 1, :] += jnp.sum(dpre * xbuf[off:off + tm, :], axis=0, keepdims=True)
        dx_ref[...] = dx.astype(ACT)
        db_ref[...] += jnp.sum(dpre, axis=0, keepdims=True)

    return pl.pallas_call(
        body, name="conv_bwd", grid=(nb,),
        in_specs=[_rows_rev(tm, MIX_W, nb),
                  pl.BlockSpec((HALO, MIX_W), lambda i: (jnp.maximum((nb - 1 - i) * hb - 1, 0), 0)),
                  _const((CONV_K, MIX_W)), _rows_rev(tm, MIX_W, nb), _rows_rev(tm, MIX_W, nb)],
        out_specs=[_rows_rev(tm, MIX_W, nb), _const((CONV_K, MIX_W)), _const((1, MIX_W))],
        out_shape=[jax.ShapeDtypeStruct((t, MIX_W), ACT), jax.ShapeDtypeStruct((CONV_K, MIX_W), f32), jax.ShapeDtypeStruct((1, MIX_W), f32)],
        scratch_shapes=[pltpu.VMEM((tm + HALO, MIX_W), f32), pltpu.VMEM((tm + HALO, MIX_W), f32)],
        compiler_params=_cparams(1),
    )(u_ml, u_ml, w, pre, dact)


def _lane_pick(x, lane):
    idx = lax.broadcasted_iota(jnp.int32, x.shape, 1)
    return jnp.sum(jnp.where(idx == lane, x, 0.0), axis=-1, keepdims=True)


def _ml_gate_forms(gates, tri):
    lf = _log_sigmoid(gates)
    gc = _sel_dot(tri, lf)
    lane = lax.broadcasted_iota(jnp.int32, gates.shape, 1)
    mixed = jnp.where(lane < HEADS, gates, gc)
    sel = (lax.broadcasted_iota(jnp.int32, (8, 128), 0) == lax.broadcasted_iota(jnp.int32, (8, 128), 1)).astype(f32)
    rowsf = _sel_dot_nt(sel, mixed)
    return gc, rowsf


def _ml_chunk(q, k, v, gates, gc, rowsf, c_st, n_st, m_st, tril_mask):
    hs = range(HEADS)
    g_col = [_lane_pick(gc, HEADS + h) for h in hs]
    ig_col = [_lane_pick(gates, h) for h in hs]
    dmat = [jnp.where(tril_mask, g_col[h] - rowsf[HEADS + h:HEADS + h + 1, :] + rowsf[h:h + 1, :], NEG) for h in hs]
    m_inter = [g_col[h] + m_st[h] for h in hs]
    m_t = [jnp.maximum(m_inter[h], jnp.max(dmat[h], axis=-1, keepdims=True)) for h in hs]
    wi = [jnp.exp(dmat[h] - m_t[h]) for h in hs]
    wo = [jnp.exp(m_inter[h] - m_t[h]) for h in hs]
    qk = [_bdot_nt(q[h], k[h]) * wi[h] for h in hs]
    num = [_bdot(qk[h], v[h]) + wo[h] * _bdot(q[h], c_st[h]) for h in hs]
    den = [_lane_sum(qk[h]) + wo[h] * _lane_dot(q[h], n_st[h]) for h in hs]
    floor = [jnp.exp(-m_t[h]) for h in hs]
    z = [jnp.maximum(jnp.abs(den[h]), floor[h]) for h in hs]
    g_last = [g_col[h][CHUNK - 1:CHUNK] for h in hs]
    a_col = [g_last[h] - g_col[h] + ig_col[h] for h in hs]
    m_new = [jnp.maximum(g_last[h] + m_st[h], jnp.max(a_col[h], axis=0, keepdims=True)) for h in hs]
    ws = [jnp.exp(a_col[h] - m_new[h]) for h in hs]
    w_old = [jnp.exp(g_last[h] + m_st[h] - m_new[h]) for h in hs]
    return dict(wi=wi, wo=wo, qk=qk, num=num, den=den, z=z, floor=floor, ws=ws, w_old=w_old, m_new=m_new)


def _mlstm_fwd(qkc, u_ml, gn, riders=()):
    t = qkc.shape[0]
    tb = _tile(t, 256)
    nc_blk = tb // CHUNK

    def body(qk_ref, v_ref, mo_ref, gt_ref, gn_ref, og_ref, cst_ref, nst_ref, mst_ref, c_sc, n_sc, m_sc):
        @pl.when(pl.program_id(0) == 0)
        def _():
            c_sc[...] = jnp.zeros_like(c_sc)
            n_sc[...] = jnp.zeros_like(n_sc)
            m_sc[...] = jnp.zeros_like(m_sc)

        tril_mask = _tri(CHUNK)
        tri = tril_mask.astype(f32)

        def chunk(c, carry):
            r0 = pl.multiple_of(c * CHUNK, CHUNK)
            rows = pl.ds(r0, CHUNK)
            gates = gt_ref[rows, :]
            gc, rowsf = _ml_gate_forms(gates, tri)
            hs = range(HEADS)
            q = [qk_ref[rows, h * ML_DQK:(h + 1) * ML_DQK] * (ML_DQK ** -0.5) for h in hs]
            k = [qk_ref[rows, HEADS * ML_DQK + h * ML_DQK:HEADS * ML_DQK + (h + 1) * ML_DQK] for h in hs]
            v = [v_ref[rows, h * HEAD_W:(h + 1) * HEAD_W] for h in hs]
            c_st = [c_sc[h] for h in hs]
            n_st = [n_sc[h] for h in hs]
            m_full = [m_sc[h] for h in hs]
            r = _ml_chunk(q, k, v, gates, gc, rowsf, c_st, n_st, [m[:, 0:1] for m in m_full], tril_mask)
            ksc = [k[h] * r["ws"][h] for h in hs]
            new_c = [r["w_old"][h] * c_st[h] + _bdot_tn(ksc[h], v[h]) for h in hs]
            for h in hs:
                cs = slice(h * HEAD_W, (h + 1) * HEAD_W)
                cst_ref[c, h] = c_st[h]
                nst_ref[c, h] = n_st[h]
                mst_ref[c, h] = m_full[h]
                c_sc[h] = new_c[h]
                n_sc[h] = r["w_old"][h] * n_st[h] + jnp.sum(ksc[h], axis=0, keepdims=True)
                m_sc[h] = r["m_new"][h] + jnp.zeros((1, 128), f32)
                _, _, y = _head_rms(r["num"][h] / r["z"][h], gn_ref[:, cs], on_mxu=True)
                og_ref[rows, cs] = (y * _sigmoid(mo_ref[rows, h * HEAD_W:(h + 1) * HEAD_W])).astype(ACT)
            return carry

        lax.fori_loop(0, nc_blk, chunk, 0)

    nchunks = t // CHUNK
    return _riding_call(
        body, "mlstm_fwd", t // tb,
        in_specs=[_rows(tb, MIX_W), _rows(tb, MIX_W, 1), _rows(tb, MIX_W, 2), _rows(tb, 128, 12), _const((1, MIX_W))],
        out_specs=[_rows(tb, MIX_W),
                   pl.BlockSpec((nc_blk, HEADS, ML_DQK, HEAD_W), lambda i: (i, 0, 0, 0)),
                   pl.BlockSpec((nc_blk, HEADS, 1, ML_DQK), lambda i: (i, 0, 0, 0)),
                   pl.BlockSpec((nc_blk, HEADS, 1, 128), lambda i: (i, 0, 0, 0))],
        out_shape=[jax.ShapeDtypeStruct((t, MIX_W), ACT),
                   jax.ShapeDtypeStruct((nchunks, HEADS, ML_DQK, HEAD_W), f32),
                   jax.ShapeDtypeStruct((nchunks, HEADS, 1, ML_DQK), f32),
                   jax.ShapeDtypeStruct((nchunks, HEADS, 1, 128), f32)],
        scratch_shapes=[pltpu.VMEM((HEADS, ML_DQK, HEAD_W), f32), pltpu.VMEM((HEADS, 1, ML_DQK), f32), pltpu.VMEM((HEADS, 1, 128), f32)],
        operands=(qkc, u_ml, u_ml, u_ml, gn), riders=riders, copies=_gather_copies, ride_shapes=_gather_shapes(riders))


def _mlstm_bwd(qkc, u_ml, gn, cst, nst, mst, dog, riders=()):
    t = qkc.shape[0]
    tb = _tile(t, 256)
    nb = t // tb
    nc_blk = tb // CHUNK

    def body(qk_ref, v_ref, mo_ref, gt_ref, gn_ref, cst_ref, nst_ref, mst_ref, dog_ref,
             dqk_ref, dv_ref, dmo_ref, dgt_ref, dgn_ref, dc_sc, dn_sc):
        @pl.when(pl.program_id(0) == 0)
        def _():
            dc_sc[...] = jnp.zeros_like(dc_sc)
            dn_sc[...] = jnp.zeros_like(dn_sc)
            dgn_ref[...] = jnp.zeros_like(dgn_ref)

        tril_mask = _tri(CHUNK)
        tri = tril_mask.astype(f32)
        triu = _tri(CHUNK, upper=True).astype(f32)
        lane = lax.broadcasted_iota(jnp.int32, (CHUNK, 128), 1)

        def chunk(j, carry):
            c = nc_blk - 1 - j
            r0 = pl.multiple_of(c * CHUNK, CHUNK)
            rows = pl.ds(r0, CHUNK)
            gates = gt_ref[rows, :]
            gc, rowsf = _ml_gate_forms(gates, tri)
            dg_mat = jnp.zeros((CHUNK, 128), f32)
            dig_mat = jnp.zeros((CHUNK, 128), f32)
            dlast_row = jnp.zeros((1, 128), f32)
            hs = range(HEADS)
            cols = [slice(h * HEAD_W, (h + 1) * HEAD_W) for h in hs]
            q = [qk_ref[rows, h * ML_DQK:(h + 1) * ML_DQK] * (ML_DQK ** -0.5) for h in hs]
            k = [qk_ref[rows, HEADS * ML_DQK + h * ML_DQK:HEADS * ML_DQK + (h + 1) * ML_DQK] for h in hs]
            v = [v_ref[rows, h * HEAD_W:(h + 1) * HEAD_W] for h in hs]
            c_st = [cst_ref[c, h] for h in hs]
            n_st = [nst_ref[c, h] for h in hs]
            m_st = [mst_ref[c, h][:, 0:1] for h in hs]
            dc = [dc_sc[h] for h in hs]
            dn = [dn_sc[h] for h in hs]
            r = _ml_chunk(q, k, v, gates, gc, rowsf, c_st, n_st, m_st, tril_mask)
            z, wi, wo, ws, w_old, den = r["z"], r["wi"], r["wo"], r["ws"], r["w_old"], r["den"]
            hh = [r["num"][h] / z[h] for h in hs]
            dh = []
            for h in hs:
                gnh = gn_ref[:, cols[h]]
                oh, rstd, y = _head_rms(hh[h], gnh, on_mxu=True)
                sg = _sigmoid(mo_ref[rows, h * HEAD_W:(h + 1) * HEAD_W])
                dogh = dog_ref[rows, cols[h]]
                dy = dogh * sg
                dmo_ref[rows, cols[h]] = (dogh * y * (sg * (1.0 - sg))).astype(ACT)
                dgn_ref[:, cols[h]] += jnp.sum(dy * oh, axis=0, keepdims=True)
                doh = dy * gnh
                dh.append(rstd * (doh - oh * (_lane_sum(doh * oh) * (1.0 / HEAD_W))))
            dnum = [dh[h] / z[h] for h in hs]
            dz = [-_lane_sum(dh[h] * hh[h]) / z[h] for h in hs]
            dden = [jnp.where(jnp.abs(den[h]) > r["floor"][h], dz[h] * jnp.sign(den[h]), 0.0) for h in hs]
            dsw = [(_bdot_nt(dnum[h], v[h]) + dden[h]) * wi[h] for h in hs]
            dq_intra = [_bdot(dsw[h], k[h]) for h in hs]
            dk_intra = [_bdot_tn(dsw[h], q[h]) for h in hs]
            dq_state = [wo[h] * (_bdot_nt(dnum[h], c_st[h]) + dden[h][:, :ML_DQK] * n_st[h]) for h in hs]
            dk_state = [ws[h] * (_bdot_nt(v[h], dc[h]) + dn[h]) for h in hs]
            dq = [dq_intra[h] + dq_state[h] for h in hs]
            dk = [dk_intra[h] + dk_state[h] for h in hs]
            dv = [_bdot_tn(r["qk"][h], dnum[h]) + ws[h] * _bdot(k[h], dc[h]) for h in hs]
            woq = [wo[h] * q[h] for h in hs]
            new_dc = [w_old[h] * dc[h] + _bdot_tn(woq[h], dnum[h]) for h in hs]
            for h in hs:
                dv_ref[rows, cols[h]] = dv[h].astype(ACT)
                dc_sc[h] = new_dc[h]
                dn_sc[h] = w_old[h] * dn[h] + jnp.sum(woq[h] * dden[h][:, :ML_DQK], axis=0, keepdims=True)
                d_last = (jnp.sum(jnp.sum(k[h] * dk_state[h], axis=0, keepdims=True), axis=-1, keepdims=True)
                          + w_old[h] * (jnp.sum(jnp.sum(dc[h] * c_st[h], axis=0, keepdims=True), axis=-1, keepdims=True)
                                        + jnp.sum(dn[h] * n_st[h], axis=-1, keepdims=True)))
                kdk = _lane_sum(_mx(k[h]).astype(f32) * dk_intra[h] + k[h] * dk_state[h])
                qdq = _lane_sum(_mx(q[h]).astype(f32) * dq_intra[h] + q[h] * dq_state[h])
                dg_mat = dg_mat + jnp.where(lane == HEADS + h, qdq - kdk, 0.0)
                dlast_row = dlast_row + jnp.where(lane[0:1] == HEADS + h, d_last, 0.0)
                dig_mat = dig_mat + jnp.where(lane == h, kdk, 0.0)
                dqk_ref[rows, h * ML_DQK:(h + 1) * ML_DQK] = dq[h] * (ML_DQK ** -0.5)
                dqk_ref[rows, HEADS * ML_DQK + h * ML_DQK:HEADS * ML_DQK + (h + 1) * ML_DQK] = dk[h]
            dlf = _sel_dot(triu, dg_mat) + dlast_row
            dgt_ref[rows, :] = (dig_mat + dlf * _sigmoid(-gates)).astype(ACT)
            return carry

        lax.fori_loop(0, nc_blk, chunk, 0)

    st4 = lambda a, b: pl.BlockSpec((nc_blk, HEADS, a, b), lambda i: (nb - 1 - i, 0, 0, 0))
    return _riding_call(
        body, "mlstm_bwd", nb,
        in_specs=[_rows_rev(tb, MIX_W, nb), _rows_rev(tb, MIX_W, nb, 1), _rows_rev(tb, MIX_W, nb, 2), _rows_rev(tb, 128, nb, 12),
                  _const((1, MIX_W)), st4(ML_DQK, HEAD_W), st4(1, ML_DQK), st4(1, 128), _rows_rev(tb, MIX_W, nb)],
        out_specs=[_rows_rev(tb, MIX_W, nb), _rows_rev(tb, MIX_W, nb), _rows_rev(tb, MIX_W, nb), _rows_rev(tb, 128, nb), _const((1, MIX_W))],
        out_shape=[jax.ShapeDtypeStruct((t, MIX_W), f32), jax.ShapeDtypeStruct((t, MIX_W), ACT), jax.ShapeDtypeStruct((t, MIX_W), ACT),
                   jax.ShapeDtypeStruct((t, 128), ACT), jax.ShapeDtypeStruct((1, MIX_W), f32)],
        scratch_shapes=[pltpu.VMEM((HEADS, ML_DQK, HEAD_W), f32), pltpu.VMEM((HEADS, 1, ML_DQK), f32)],
        operands=(qkc, u_ml, u_ml, u_ml, gn, cst, nst, mst, dog), riders=riders, copies=_scatter_copies,
        ride_shapes=[jax.ShapeDtypeStruct(r.shape, r.dtype) for r in riders])


def _ln_fwd(r, g, b):
    mu = jnp.mean(r, axis=-1, keepdims=True)
    xc = r - mu
    rstd = lax.rsqrt(jnp.mean(xc * xc, axis=-1, keepdims=True) + LN_EPS)
    xh = xc * rstd
    return xh * g + b, xh, rstd


def _ln_bwd(dy, xh, rstd, g):
    dxh = dy * g
    return rstd * (dxh - jnp.mean(dxh, axis=-1, keepdims=True) - xh * jnp.mean(dxh * xh, axis=-1, keepdims=True))


def _outproj_ln1(og_hg, og_ml, x, w_out, g, b, riders=()):
    t = x.shape[0]
    tm = _tile(t, DENSE_ROWS)

    def body(a_ref, b_ref, x_ref, w_ref, g_ref, bb_ref, x1_ref, xh_ref, rs_ref, x1b_ref):
        mix = _bdot(a_ref[...], w_ref[0:MIX_W, :]) + _bdot(b_ref[...], w_ref[MIX_W:2 * MIX_W, :])
        y, xh, rstd = _ln_fwd(ALPHA * x_ref[...] + mix, g_ref[...], bb_ref[...])
        x1_ref[...] = y
        x1b_ref[...] = y.astype(ACT)
        xh_ref[...] = xh.astype(ACT)
        rs_ref[...] = rstd

    return _riding_call(
        body, "outproj_ln1", t // tm,
        in_specs=[_rows(tm, MIX_W), _rows(tm, MIX_W), _rows(tm, D_MODEL), _resident((D_MODEL, D_MODEL)), _const((1, D_MODEL)), _const((1, D_MODEL))],
        out_specs=[_rows(tm, D_MODEL), _rows(tm, D_MODEL), _rows(tm, 1), _rows(tm, D_MODEL)],
        out_shape=[jax.ShapeDtypeStruct((t, D_MODEL), f32), jax.ShapeDtypeStruct((t, D_MODEL), ACT), jax.ShapeDtypeStruct((t, 1), f32),
                   jax.ShapeDtypeStruct((t, D_MODEL), ACT)],
        scratch_shapes=[], operands=(og_hg, og_ml, x, w_out, g, b), riders=riders, copies=_gather_copies, ride_shapes=_gather_shapes(riders))


def _ffn_up(x1, wg, wu, riders=()):
    t = x1.shape[0]
    tm = _tile(t, DENSE_ROWS)

    def body(x_ref, wg_ref, wu_ref, hg_ref, up_ref, a_ref):
        xv = x_ref[...]
        hg = _bdot_nt(xv, wg_ref[...])
        up = _bdot_nt(xv, wu_ref[...])
        hg_ref[...] = hg.astype(ACT)
        up_ref[...] = up.astype(ACT)
        a_ref[...] = (hg * _sigmoid(hg) * up).astype(ACT)

    return _riding_call(
        body, "ffn_up", t // tm,
        in_specs=[_rows(tm, D_MODEL), _resident((D_FF, D_MODEL)), _resident((D_FF, D_MODEL))],
        out_specs=[_rows(tm, D_FF), _rows(tm, D_FF), _rows(tm, D_FF)],
        out_shape=[jax.ShapeDtypeStruct((t, D_FF), ACT), jax.ShapeDtypeStruct((t, D_FF), ACT), jax.ShapeDtypeStruct((t, D_FF), ACT)],
        scratch_shapes=[], operands=(x1, wg, wu), riders=riders, copies=_gather_copies, ride_shapes=_gather_shapes(riders))


def _ffn_down_ln2(a, x1, wd, g, b):
    t = x1.shape[0]
    tm = _tile(t, DENSE_ROWS)

    def body(a_ref, x_ref, w_ref, g_ref, bb_ref, x2_ref, xh_ref, rs_ref, x2b_ref):
        ffn = _bdot(a_ref[...], w_ref[...])
        y, xh, rstd = _ln_fwd(ALPHA * x_ref[...] + ffn, g_ref[...], bb_ref[...])
        x2_ref[...] = y
        x2b_ref[...] = y.astype(ACT)
        xh_ref[...] = xh.astype(ACT)
        rs_ref[...] = rstd

    return pl.pallas_call(
        body, name="ffn_down_ln2", grid=(t // tm,),
        in_specs=[_rows(tm, D_FF), _rows(tm, D_MODEL), _resident((D_FF, D_MODEL)), _const((1, D_MODEL)), _const((1, D_MODEL))],
        out_specs=[_rows(tm, D_MODEL), _rows(tm, D_MODEL), _rows(tm, 1), _rows(tm, D_MODEL)],
        out_shape=[jax.ShapeDtypeStruct((t, D_MODEL), f32), jax.ShapeDtypeStruct((t, D_MODEL), ACT), jax.ShapeDtypeStruct((t, 1), f32),
                   jax.ShapeDtypeStruct((t, D_MODEL), ACT)],
        compiler_params=_cparams(1, arbitrary=False),
    )(a, x1, wd, g, b)


def _head_loss_bwd(x2, xh2, rs2, p, tgt, w_pg, b_pg, w_pp, g2):
    t = x2.shape[0]
    tm = _tile(t, DENSE_ROWS)

    def body(x_ref, xh_ref, rs_ref, p_ref, t_ref, wg_ref, bg_ref, wp_ref, g_ref,
             dr_ref, de_ref, dz_ref, loss_ref, dbg_ref, dg2_ref, db2_ref):
        @pl.when(pl.program_id(0) == 0)
        def _():
            loss_ref[...] = jnp.zeros_like(loss_ref)
            dbg_ref[...] = jnp.zeros_like(dbg_ref)
            dg2_ref[...] = jnp.zeros_like(dg2_ref)
            db2_ref[...] = jnp.zeros_like(db2_ref)

        x2v = x_ref[...]
        z = _bdot(x2v, wg_ref[...]) + bg_ref[...]
        e = _bdot(p_ref[...], wp_ref[...])
        sg = _sigmoid(z)
        diff = x2v + sg * e - t_ref[...]
        loss_ref[...] += 0.5 * jnp.sum(jnp.mean(diff * diff, axis=-1, keepdims=True), axis=0, keepdims=True)
        dy = diff * (1.0 / D_MODEL)
        de_ref[...] = (dy * sg).astype(ACT)
        dz = dy * e * (sg * (1.0 - sg))
        dz_ref[...] = dz.astype(ACT)
        dbg_ref[...] += jnp.sum(dz, axis=0, keepdims=True)
        dx2 = dy + _bdot_nt(dz, wg_ref[...])
        xh = xh_ref[...].astype(f32)
        dg2_ref[...] += jnp.sum(dx2 * xh, axis=0, keepdims=True)
        db2_ref[...] += jnp.sum(dx2, axis=0, keepdims=True)
        dr_ref[...] = _ln_bwd(dx2, xh, rs_ref[...], g_ref[...])

    row = jax.ShapeDtypeStruct((1, D_MODEL), f32)
    return pl.pallas_call(
        body, name="head_loss_bwd", grid=(t // tm,),
        in_specs=[_rows(tm, D_MODEL), _rows(tm, D_MODEL), _rows(tm, 1), _rows(tm, PLE), _rows(tm, D_MODEL),
                  _resident((D_MODEL, D_MODEL)), _const((1, D_MODEL)), _resident((PLE, D_MODEL)), _const((1, D_MODEL))],
        out_specs=[_rows(tm, D_MODEL), _rows(tm, D_MODEL), _rows(tm, D_MODEL), _const((1, 1)), _const((1, D_MODEL)), _const((1, D_MODEL)), _const((1, D_MODEL))],
        out_shape=[jax.ShapeDtypeStruct((t, D_MODEL), f32), jax.ShapeDtypeStruct((t, D_MODEL), ACT), jax.ShapeDtypeStruct((t, D_MODEL), ACT),
                   jax.ShapeDtypeStruct((1, 1), f32), row, row, row],
        compiler_params=_cparams(1),
    )(x2, xh2, rs2, p, tgt, w_pg, b_pg, w_pp, g2)


def _ffn_bwd(dr2, hg, up, xh1, rs1, wd, wg, wu, g1, w_out):
    t = dr2.shape[0]
    tm = _tile(t, DENSE_ROWS // 2)

    def body(dr_ref, hg_ref, up_ref, xh_ref, rs_ref, wd_ref, wg_ref, wu_ref, g_ref, wo_ref,
             dr1_ref, dhg_ref, dup_ref, dg1_ref, db1_ref, doghg_ref, dogml_ref):
        @pl.when(pl.program_id(0) == 0)
        def _():
            dg1_ref[...] = jnp.zeros_like(dg1_ref)
            db1_ref[...] = jnp.zeros_like(db1_ref)

        dr2v = dr_ref[...]
        da = _bdot_nt(dr2v, wd_ref[...])
        hgv = hg_ref[...].astype(f32)
        sg = _sigmoid(hgv)
        dhg = da * up_ref[...].astype(f32) * (sg * (1.0 + hgv * (1.0 - sg)))
        dup = da * (hgv * sg)
        dhg_ref[...] = dhg.astype(ACT)
        dup_ref[...] = dup.astype(ACT)
        dx1 = ALPHA * dr2v + _bdot(dhg, wg_ref[...]) + _bdot(dup, wu_ref[...])
        xh = xh_ref[...].astype(f32)
        dg1_ref[...] += jnp.sum(dx1 * xh, axis=0, keepdims=True)
        db1_ref[...] += jnp.sum(dx1, axis=0, keepdims=True)
        dr1 = _ln_bwd(dx1, xh, rs_ref[...], g_ref[...])
        dr1_ref[...] = dr1
        dog = _bdot_nt(dr1, wo_ref[...])
        doghg_ref[...] = dog[:, 0:MIX_W]
        dogml_ref[...] = dog[:, MIX_W:2 * MIX_W]

    row = jax.ShapeDtypeStruct((1, D_MODEL), f32)
    return pl.pallas_call(
        body, name="ffn_bwd", grid=(t // tm,),
        in_specs=[_rows(tm, D_MODEL), _rows(tm, D_FF), _rows(tm, D_FF), _rows(tm, D_MODEL), _rows(tm, 1),
                  _resident((D_FF, D_MODEL)), _resident((D_FF, D_MODEL)), _resident((D_FF, D_MODEL)), _const((1, D_MODEL)),
                  _resident((D_MODEL, D_MODEL))],
        out_specs=[_rows(tm, D_MODEL), _rows(tm, D_FF), _rows(tm, D_FF), _const((1, D_MODEL)), _const((1, D_MODEL)),
                   _rows(tm, MIX_W), _rows(tm, MIX_W)],
        out_shape=[jax.ShapeDtypeStruct((t, D_MODEL), f32), jax.ShapeDtypeStruct((t, D_FF), ACT), jax.ShapeDtypeStruct((t, D_FF), ACT), row, row,
                   jax.ShapeDtypeStruct((t, MIX_W), f32), jax.ShapeDtypeStruct((t, MIX_W), f32)],
        compiler_params=_cparams(1),
    )(dr2, hg, up, xh1, rs1, wd, wg, wu, g1, w_out)


def _inproj_bwd(dr1, du_hg, dqk, dmv, dmo, dgt, w_hg, w_ml):
    t = dr1.shape[0]
    tm = _tile(t, DENSE_ROWS)

    def body(dr_ref, dhg_ref, dqk_ref, dmv_ref, dmo_ref, dgt_ref, whg_ref, wml_ref, gx_ref, dml_ref):
        dml = jnp.concatenate([dqk_ref[...], dmv_ref[...], dmo_ref[...], dgt_ref[...]], axis=-1).astype(ACT)
        dml_ref[...] = dml
        gx_ref[...] = ALPHA * dr_ref[...] + _bdot(dhg_ref[...], whg_ref[...]) + _bdot(dml, wml_ref[...])

    return pl.pallas_call(
        body, name="inproj_bwd", grid=(t // tm,),
        in_specs=[_rows(tm, D_MODEL), _rows(tm, U_HG), _rows(tm, MIX_W), _rows(tm, MIX_W), _rows(tm, MIX_W), _rows(tm, 128),
                  _resident((U_HG, D_MODEL)), _resident((U_ML, D_MODEL))],
        out_specs=[_rows(tm, D_MODEL), _rows(tm, U_ML)],
        out_shape=[jax.ShapeDtypeStruct((t, D_MODEL), f32), jax.ShapeDtypeStruct((t, U_ML), ACT)],
        compiler_params=_cparams(1, arbitrary=False),
    )(dr1, du_hg, dqk, dmv, dmo, dgt, w_hg, w_ml)


def _wgrad(a, b, name, tk=None, tn=None, colsum=False, low=False):
    t, kdim = a.shape
    n = b.shape[1]
    tk = tk or kdim
    tn = tn or n
    tt = _tile(t, WGRAD_ROWS)
    nt = t // tt
    assert not (colsum and low) and (not colsum or tn == n)

    def body(a_ref, b_ref, o_ref, *s_ref):
        @pl.when(pl.program_id(2) == 0)
        def _():
            o_ref[...] = jnp.zeros_like(o_ref)
            if colsum:
                s_ref[0][...] = jnp.zeros_like(s_ref[0])

        av = a_ref[...]
        o_ref[...] += _bdot_tn(av, b_ref[...])
        if colsum:
            s_ref[0][...] += jnp.sum(av.astype(f32), axis=0, keepdims=True)
        if low:
            @pl.when(pl.program_id(2) == nt - 1)
            def _():
                s_ref[0][...] = o_ref[...].astype(bf16)

    out_specs = [pl.BlockSpec((tk, tn), lambda i, j, s: (i, j))]
    out_shape = [jax.ShapeDtypeStruct((kdim, n), f32)]
    if colsum:
        out_specs.append(pl.BlockSpec((1, tk), lambda i, j, s: (0, i)))
        out_shape.append(jax.ShapeDtypeStruct((1, kdim), f32))
    if low:
        out_specs.append(pl.BlockSpec((tk, tn), lambda i, j, s: (i, j)))
        out_shape.append(jax.ShapeDtypeStruct((kdim, n), bf16))
    res = pl.pallas_call(
        body, name=name, grid=(kdim // tk, n // tn, t // tt),
        in_specs=[pl.BlockSpec((tt, tk), lambda i, j, s: (s, i)), pl.BlockSpec((tt, tn), lambda i, j, s: (s, j))],
        out_specs=out_specs, out_shape=out_shape,
        compiler_params=_cparams(3),
    )(a, b)
    return res if (colsum or low) else res[0]


_TRANSPOSED = {"w_in", "w_ffn_gate", "w_ffn_up"}
_COL_SPLIT = {"ple_w_proj"}
_SCATTER_PLAN = (("w_ffn_gate", "w_ffn_up"), ("w_ffn_down", "w_out", "ple_w_gate", "ple_w_proj"))
_RIDE_PLAN = {"inproj": ("w_ffn_gate",), "hgrn2_fwd": ("w_ffn_up",), "mlstm_fwd": ("w_out",),
              "outproj_ln1": ("ple_w_gate", "ple_w_proj"), "ffn_up": ("w_ffn_down",)}


def _from_chip_major(a, col_split):
    if col_split:
        return a.transpose(1, 0, 2).reshape(a.shape[1], 4 * a.shape[2])
    return a.reshape(4 * a.shape[1], a.shape[2])


def _local_step(x, p, tgt, w_in_b, b_in, logits, conv_w, conv_b, hg_gn, ml_gn, w_out_b, ln1_g, ln1_b,
                wg_b, wu_b, wd_b, ln2_g, ln2_b, w_pp_b, w_pg_b, b_pg, early_hook=None, late_shards=None):
    pad_w = U_HG + U_ML - PROJ_W
    w_hg = w_in_b[:U_HG]
    w_ml = jnp.pad(w_in_b[U_HG:], ((0, pad_w), (0, 0)))
    bb_hg = b_in[:, :U_HG]
    bb_ml = jnp.pad(b_in[:, U_HG:], ((0, 0), (0, pad_w)))

    late = dict(w_out=w_out_b, w_ffn_gate=wg_b, w_ffn_up=wu_b, w_ffn_down=wd_b, ple_w_proj=w_pp_b, ple_w_gate=w_pg_b)

    def riders_of(call):
        return [late_shards[k] for k in _RIDE_PLAN[call]] if late_shards is not None else ()

    def arrived(call, got):
        for k, g in zip(_RIDE_PLAN[call], got):
            late[k] = _from_chip_major(g, k in _COL_SPLIT)

    (u_hg, u_ml, xb), got = _inproj(x, w_hg, w_ml, bb_hg, bb_ml, riders_of("inproj"))
    arrived("inproj", got)
    (og_hg, sst, hg_b, hg_a, hg_o), got = _hgrn2_fwd(u_hg, logits, hg_gn, riders_of("hgrn2_fwd"))
    arrived("hgrn2_fwd", got)
    pre, qkc = _conv_fwd(u_ml, conv_w, conv_b)
    (og_ml, cst, nst, mst), got = _mlstm_fwd(qkc, u_ml, ml_gn, riders_of("mlstm_fwd"))
    arrived("mlstm_fwd", got)
    (x1, xh1, rs1, x1b), got = _outproj_ln1(og_hg, og_ml, x, late["w_out"], ln1_g, ln1_b, riders_of("outproj_ln1"))
    arrived("outproj_ln1", got)
    (hgp, up, act), got = _ffn_up(x1b, late["w_ffn_gate"], late["w_ffn_up"], riders_of("ffn_up"))
    arrived("ffn_up", got)
    w_out_b, wg_b, wu_b, wd_b = late["w_out"], late["w_ffn_gate"], late["w_ffn_up"], late["w_ffn_down"]
    w_pp_b, w_pg_b = late["ple_w_proj"], late["ple_w_gate"]
    x2, xh2, rs2, x2b = _ffn_down_ln2(act, x1, wd_b, ln2_g, ln2_b)
    dr2, de, dz, loss, d_bpg, d_ln2g, d_ln2b = _head_loss_bwd(x2, xh2, rs2, p, tgt, w_pg_b, b_pg, w_pp_b, ln2_g)
    dr1, dhg, dup, d_ln1g, d_ln1b, dog_hg, dog_ml = _ffn_bwd(dr2, hgp, up, xh1, rs1, wd_b, wg_b, wu_b, ln1_g, w_out_b)

    d_wo_a, lo_wo_a = _wgrad(og_hg, dr1, "wgrad_out_hg", low=True)
    d_wo_b, lo_wo_b = _wgrad(og_ml, dr1, "wgrad_out_ml", low=True)
    d_wg, lo_wg = _wgrad(dhg, x1b, "wgrad_ffn_gate", tk=D_FF // 2, low=True)
    d_wu, lo_wu = _wgrad(dup, x1b, "wgrad_ffn_up", tk=D_FF // 2, low=True)
    d_wd, lo_wd = _wgrad(act, dr2, "wgrad_ffn_down", tk=D_FF // 2, low=True)
    d_wpp, lo_wpp = _wgrad(p, de, "wgrad_ple_proj", low=True)
    d_wpg, lo_wpg = _wgrad(x2b, dz, "wgrad_ple_gate", low=True)
    early = dict(w_out=jnp.concatenate([d_wo_a, d_wo_b], axis=0), w_ffn_gate=d_wg, w_ffn_up=d_wu, w_ffn_down=d_wd,
                 ple_w_proj=d_wpp, ple_w_gate=d_wpg)
    early_low = dict(w_out=jnp.concatenate([lo_wo_a, lo_wo_b], axis=0), w_ffn_gate=lo_wg, w_ffn_up=lo_wu, w_ffn_down=lo_wd,
                     ple_w_proj=lo_wpp, ple_w_gate=lo_wpg)
    ride_hg, ride_ml = early_hook(early_low) if early_hook is not None else ((), ())

    (du_hg, d_logits, d_hg_gn), got_hg = _hgrn2_bwd(u_hg, logits, hg_gn, sst, hg_b, hg_a, hg_o, dog_hg, ride_hg)
    (dqkc, dmv, dmo, dgt, d_ml_gn), got_ml = _mlstm_bwd(qkc, u_ml, ml_gn, cst, nst, mst, dog_ml, ride_ml)
    dqk, d_conv_w, d_conv_b = _conv_bwd(u_ml, conv_w, pre, dqkc)
    grad_x, du_ml = _inproj_bwd(dr1, du_hg, dqk, dmv, dmo, dgt, w_hg, w_ml)

    dw_hg, db_hg = _wgrad(du_hg, xb, "wgrad_in_hg", tk=U_HG // 2, colsum=True)
    dw_ml, db_ml = _wgrad(du_ml, xb, "wgrad_in_ml", colsum=True)
    d_w_in = jnp.concatenate([dw_hg, dw_ml[:PROJ_W - U_HG]], axis=0)
    d_b_in = jnp.concatenate([db_hg, db_ml[:, :PROJ_W - U_HG]], axis=1)

    grads = dict(w_in=d_w_in, b_in=d_b_in, hg_lb_logits=d_logits, ml_conv_w=d_conv_w, ml_conv_b=d_conv_b,
                 hg_norm_g=d_hg_gn, ml_norm_g=d_ml_gn, ln1_g=d_ln1g, ln1_b=d_ln1b, ln2_g=d_ln2g, ln2_b=d_ln2b,
                 ple_b_gate=d_bpg, **early)
    return loss, grad_x, grads, (list(got_hg), list(got_ml))


_ANY = pl.BlockSpec(memory_space=pltpu.HBM)
_MESH = pl.DeviceIdType.MESH


def _my_place():
    return lax.axis_index("x"), lax.axis_index("y"), lax.axis_index("c")


def _other_chips(x, y):
    return [(1 - x, y), (x, 1 - y), (1 - x, 1 - y)]


_VMEM = pl.BlockSpec(memory_space=pltpu.VMEM)
_EX_ROWS = 32


def _pair_reduce_cols(p, name):
    s, r, c = p.shape
    hc = c // 2

    def body(p_ref, o_ref, other, send_sem, recv_sem):
        x, y, cc = _my_place()

        def run(mine_lo, theirs_lo):
            cp = pltpu.make_async_remote_copy(src_ref=p_ref.at[pl.ds(0, s), pl.ds(0, r), pl.ds(theirs_lo, hc)], dst_ref=other,
                                              send_sem=send_sem, recv_sem=recv_sem, device_id=(x, y, 1 - cc), device_id_type=_MESH)
            cp.start()
            cp.wait()
            for slot in range(s):
                o_ref[slot] = (p_ref[slot, :, mine_lo:mine_lo + hc] + other[slot]).astype(bf16)

        @pl.when(cc == 0)
        def _():
            run(0, hc)

        @pl.when(cc == 1)
        def _():
            run(hc, 0)

    return pl.pallas_call(
        body, name=name, in_specs=[_VMEM], out_specs=_VMEM,
        out_shape=jax.ShapeDtypeStruct((s, r, hc), bf16),
        scratch_shapes=[pltpu.VMEM((s, r, hc), f32), pltpu.SemaphoreType.DMA, pltpu.SemaphoreType.DMA],
        compiler_params=pltpu.CompilerParams(vmem_limit_bytes=VMEM_LIMIT),
    )(p)


def _chip_reduce_swap_cols(rcv, name):
    s, r, hc = rcv.shape

    def body(r_ref, g_ref, send_sem, recv_sem):
        x, y, cc = _my_place()
        acc = r_ref[0].astype(f32)
        for slot in range(1, s):
            acc = acc + r_ref[slot].astype(f32)
        g_ref[cc] = acc
        cp = pltpu.make_async_remote_copy(src_ref=g_ref.at[cc], dst_ref=g_ref.at[cc], send_sem=send_sem, recv_sem=recv_sem,
                                          device_id=(x, y, 1 - cc), device_id_type=_MESH)
        cp.start()
        cp.wait()

    both = pl.pallas_call(
        body, name=name, in_specs=[_VMEM], out_specs=_VMEM,
        out_shape=jax.ShapeDtypeStruct((2, r, hc), f32),
        scratch_shapes=[pltpu.SemaphoreType.DMA, pltpu.SemaphoreType.DMA],
        compiler_params=pltpu.CompilerParams(vmem_limit_bytes=VMEM_LIMIT),
    )(rcv)
    return both.transpose(1, 0, 2).reshape(r, 2 * hc)


def _reduce_adamw(rcv, w, m, v, name):
    s, r, c = rcv.shape
    rows_per = _EX_ROWS

    def body(r_ref, w_ref, m_ref, v_ref, g_ref, d_ref, nm_ref, nv_ref, mine, theirs, send_sem, recv_sem):
        x, y, cc = _my_place()

        def chip_sum(i, carry):
            rs = pl.ds(pl.multiple_of(i * rows_per, rows_per), rows_per)
            acc = r_ref[0, rs, :].astype(f32)
            for slot in range(1, s):
                acc = acc + r_ref[slot, rs, :].astype(f32)
            mine[rs, :] = acc
            return carry

        lax.fori_loop(0, r // rows_per, chip_sum, 0)
        cp = pltpu.make_async_remote_copy(src_ref=mine, dst_ref=theirs, send_sem=send_sem, recv_sem=recv_sem,
                                          device_id=(x, y, 1 - cc), device_id_type=_MESH)
        cp.start()
        cp.wait()

        def update(i, carry):
            rs = pl.ds(pl.multiple_of(i * rows_per, rows_per), rows_per)
            g = mine[rs, :] + theirs[rs, :]
            nm = B1 * m_ref[rs, :] + (1.0 - B1) * g
            nv = B2 * v_ref[rs, :] + (1.0 - B2) * (g * g)
            g_ref[rs, :] = g
            nm_ref[rs, :] = nm
            nv_ref[rs, :] = nv
            d_ref[rs, :] = -LR * ((nm / (1.0 - B1 ** STEP)) / (jnp.sqrt(nv / (1.0 - B2 ** STEP)) + EPS_ADAM) + WD * w_ref[rs, :])
            return carry

        lax.fori_loop(0, r // rows_per, update, 0)

    return pl.pallas_call(
        body, name=name, in_specs=[_VMEM] * 4, out_specs=[_VMEM] * 4,
        out_shape=[jax.ShapeDtypeStruct((r, c), f32)] * 4,
        scratch_shapes=[pltpu.VMEM((r, c), f32), pltpu.VMEM((r, c), f32), pltpu.SemaphoreType.DMA, pltpu.SemaphoreType.DMA],
        compiler_params=pltpu.CompilerParams(vmem_limit_bytes=VMEM_LIMIT),
    )(rcv, w, m, v)


def _gather_copies(ins, outs, send_sems, recv_sems, local_sems):
    x, y, c = _my_place()
    me = 2 * x + y
    local, outgoing, incoming = [], [], []
    for a in range(len(ins)):
        local.append(pltpu.make_async_copy(ins[a], outs[a].at[me], local_sems.at[a]))
        for j, (px, py) in enumerate(_other_chips(x, y)):
            sems = dict(send_sem=send_sems.at[3 * a + j], recv_sem=recv_sems.at[3 * a + j], device_id=(px, py, c), device_id_type=_MESH)
            outgoing.append(pltpu.make_async_remote_copy(src_ref=ins[a], dst_ref=outs[a].at[me], **sems))
            incoming.append(pltpu.make_async_remote_copy(src_ref=ins[a], dst_ref=outs[a].at[2 * px + py], **sems))
    return local, outgoing, incoming


def _gather_first(block, taps, name):
    r, c = block.shape
    hc = c // 2

    def body(in_ref, tap_in, out_ref, tap_out, send_sems, recv_sems):
        x, y, cc = _my_place()
        me = 2 * x + y
        sibling = (x, y, 1 - cc)
        chips = _other_chips(x, y)
        out_ref[me] = in_ref[...]
        tap_out[me] = tap_in[...]

        def run(mine, theirs):
            def ici(j, chip):
                px, py = chips[j]
                src = in_ref.at[pl.ds(0, r), pl.ds(mine, hc)] if chip is None else out_ref.at[chip, pl.ds(0, r), pl.ds(mine, hc)]
                dst = out_ref.at[me if chip is None else chip, pl.ds(0, r), pl.ds(mine, hc)]
                return pltpu.make_async_remote_copy(src_ref=src, dst_ref=dst, send_sem=send_sems.at[j], recv_sem=recv_sems.at[j],
                                                    device_id=(px, py, cc), device_id_type=_MESH)

            def d2d(j, lo):
                px, py = chips[j]
                blk = out_ref.at[2 * px + py, pl.ds(0, r), pl.ds(lo, hc)]
                return pltpu.make_async_remote_copy(src_ref=blk, dst_ref=blk, send_sem=send_sems.at[3 + j], recv_sem=recv_sems.at[3 + j],
                                                    device_id=sibling, device_id_type=_MESH)

            def tap(j, chip):
                px, py = chips[j]
                return pltpu.make_async_remote_copy(src_ref=tap_in, dst_ref=tap_out.at[me if chip is None else chip],
                                                    send_sem=send_sems.at[6 + j], recv_sem=recv_sems.at[6 + j],
                                                    device_id=(px, py, cc), device_id_type=_MESH)

            for j in range(3):
                ici(j, None).start()
                tap(j, None).start()
            for j, (px, py) in enumerate(chips):
                ici(j, 2 * px + py).wait_recv()
                d2d(j, mine).start()
            for j, (px, py) in enumerate(chips):
                d2d(j, theirs).wait_recv()
                tap(j, 2 * px + py).wait_recv()
            for j in range(3):
                ici(j, None).wait_send()
                d2d(j, mine).wait_send()
                tap(j, None).wait_send()

        @pl.when(cc == 0)
        def _():
            run(0, hc)

        @pl.when(cc == 1)
        def _():
            run(hc, 0)

    return pl.pallas_call(
        body, name=name, in_specs=[_VMEM, _VMEM], out_specs=[_VMEM, _VMEM],
        out_shape=[jax.ShapeDtypeStruct((4, r, c), block.dtype), jax.ShapeDtypeStruct((4,) + taps.shape, taps.dtype)],
        scratch_shapes=[pltpu.SemaphoreType.DMA((9,)), pltpu.SemaphoreType.DMA((9,))],
        compiler_params=pltpu.CompilerParams(vmem_limit_bytes=VMEM_LIMIT),
    )(block, taps)


def _riding_call(body, name, nsteps, in_specs, out_specs, out_shape, scratch_shapes, operands, riders, copies, ride_shapes):
    nr, n_in, n_out, n_scr = len(riders), len(in_specs), len(out_specs), len(scratch_shapes)

    def wrapped(*refs):
        ins, ride_in = refs[:n_in], refs[n_in:n_in + nr]
        outs, ride_out = refs[n_in + nr:n_in + nr + n_out], refs[n_in + nr + n_out:n_in + 2 * nr + n_out]
        scratch, sems = refs[n_in + 2 * nr + n_out:n_in + 2 * nr + n_out + n_scr], refs[n_in + 2 * nr + n_out + n_scr:]
        if nr:
            @pl.when(pl.program_id(0) == 0)
            def _():
                local, outgoing, _ = copies(ride_in, ride_out, *sems)
                for cp in local + outgoing:
                    cp.start()

        body(*ins, *outs, *scratch)
        if nr:
            @pl.when(pl.program_id(0) == nsteps - 1)
            def _():
                local, outgoing, incoming = copies(ride_in, ride_out, *sems)
                for cp in incoming:
                    cp.wait_recv()
                for cp in outgoing:
                    cp.wait_send()
                for cp in local:
                    cp.wait()

    hbm = pl.BlockSpec(memory_space=pltpu.HBM)
    sems = [pltpu.SemaphoreType.DMA((3 * nr,)), pltpu.SemaphoreType.DMA((3 * nr,)), pltpu.SemaphoreType.DMA((nr,))] if nr else []
    res = pl.pallas_call(
        wrapped, name=name, grid=(nsteps,),
        in_specs=list(in_specs) + [hbm] * nr, out_specs=list(out_specs) + [hbm] * nr,
        out_shape=list(out_shape) + list(ride_shapes),
        scratch_shapes=list(scratch_shapes) + sems,
        compiler_params=_cparams(1),
    )(*operands, *riders)
    return list(res[:n_out]), list(res[n_out:])


def _gather_shapes(riders):
    return [jax.ShapeDtypeStruct((4,) + r.shape, r.dtype) for r in riders]


def _scatter_copies(ins, outs, send_sems, recv_sems, local_sems):
    x, y, c = _my_place()
    me = 2 * x + y
    local, outgoing, incoming = [], [], []
    for a in range(len(ins)):
        local.append(pltpu.make_async_copy(ins[a].at[me], outs[a].at[me], local_sems.at[a]))
        for j, (px, py) in enumerate(_other_chips(x, y)):
            sems = dict(send_sem=send_sems.at[3 * a + j], recv_sem=recv_sems.at[3 * a + j], device_id=(px, py, c), device_id_type=_MESH)
            outgoing.append(pltpu.make_async_remote_copy(src_ref=ins[a].at[2 * px + py], dst_ref=outs[a].at[me], **sems))
            incoming.append(pltpu.make_async_remote_copy(src_ref=ins[a].at[2 * px + py], dst_ref=outs[a].at[2 * px + py], **sems))
    return local, outgoing, incoming


def _scatter_chips(pieces, name):
    n = len(pieces)

    def body(*refs):
        local, outgoing, incoming = _scatter_copies(refs[:n], refs[n:2 * n], *refs[2 * n:])
        for cp in local + outgoing:
            cp.start()
        for cp in incoming:
            cp.wait_recv()
        for cp in outgoing:
            cp.wait_send()
        for cp in local:
            cp.wait()

    return pl.pallas_call(
        body, name=name,
        in_specs=[_ANY] * n, out_specs=[_ANY] * n,
        out_shape=[jax.ShapeDtypeStruct(s.shape, s.dtype) for s in pieces],
        scratch_shapes=[pltpu.SemaphoreType.DMA((3 * n,)), pltpu.SemaphoreType.DMA((3 * n,)), pltpu.SemaphoreType.DMA((n,))],
    )(*pieces)


def _gather_all(block, name):
    def body(in_ref, out_ref, send_sems, recv_sems, local_sem):
        x, y, c = _my_place()
        me = 4 * x + 2 * y + c
        cp = pltpu.make_async_copy(in_ref, out_ref.at[me], local_sem)
        cp.start()
        peers = []
        for dx in range(2):
            for dy in range(2):
                for dc in range(2):
                    if dx or dy or dc:
                        peers.append((1 - x if dx else x, 1 - y if dy else y, 1 - c if dc else c))
        for j, pr in enumerate(peers):
            pltpu.make_async_remote_copy(src_ref=in_ref, dst_ref=out_ref.at[me], send_sem=send_sems.at[j], recv_sem=recv_sems.at[j],
                                         device_id=pr, device_id_type=_MESH).start()
        for j, (px, py, pc) in enumerate(peers):
            pltpu.make_async_remote_copy(src_ref=in_ref, dst_ref=out_ref.at[4 * px + 2 * py + pc], send_sem=send_sems.at[j], recv_sem=recv_sems.at[j],
                                         device_id=(px, py, pc), device_id_type=_MESH).wait()
        cp.wait()

    return pl.pallas_call(
        body, name=name,
        in_specs=[_ANY], out_specs=_ANY,
        out_shape=jax.ShapeDtypeStruct((8,) + block.shape, block.dtype),
        scratch_shapes=[pltpu.SemaphoreType.DMA((7,)), pltpu.SemaphoreType.DMA((7,)), pltpu.SemaphoreType.DMA],
    )(block)


def _row_tile(r, c):
    best = r
    for cand in range(16, r + 1, 16):
        if r % cand == 0 and cand * c * 4 <= (1 << 20):
            best = cand
    return best if best * c * 4 <= (4 << 20) else r


def _sum_slots(parts, name):
    n, r, c = parts.shape
    tr = _row_tile(r, c)

    def body(p_ref, o_ref):
        acc = p_ref[0].astype(f32)
        for s in range(1, n):
            acc = acc + p_ref[s].astype(f32)
        o_ref[...] = acc

    return pl.pallas_call(
        body, name=name, grid=(r // tr,),
        in_specs=[pl.BlockSpec((n, tr, c), lambda i: (0, i, 0))],
        out_specs=pl.BlockSpec((tr, c), lambda i: (i, 0)),
        out_shape=jax.ShapeDtypeStruct((r, c), f32),
        compiler_params=_cparams(1, arbitrary=False),
    )(parts)


def _adamw(parts, w, m, v, name):
    n, r, c = parts.shape
    tr = _row_tile(r, c)
    tc = c
    if tr == r and r * c * 4 > (1 << 20) and c % 256 == 0:
        tc = 256

    def body(p_ref, w_ref, m_ref, v_ref, g_ref, d_ref, nm_ref, nv_ref):
        g = p_ref[0]
        for s in range(1, n):
            g = g + p_ref[s]
        nm = B1 * m_ref[...] + (1.0 - B1) * g
        nv = B2 * v_ref[...] + (1.0 - B2) * (g * g)
        m_hat = nm / (1.0 - B1 ** STEP)
        v_hat = nv / (1.0 - B2 ** STEP)
        g_ref[...] = g
        nm_ref[...] = nm
        nv_ref[...] = nv
        d_ref[...] = -LR * (m_hat / (jnp.sqrt(v_hat) + EPS_ADAM) + WD * w_ref[...])

    blk = pl.BlockSpec((tr, tc), lambda i, j: (i, j))
    return pl.pallas_call(
        body, name=name, grid=(r // tr, c // tc),
        in_specs=[pl.BlockSpec((n, tr, tc), lambda i, j: (0, i, j)), blk, blk, blk],
        out_specs=[blk] * 4,
        out_shape=[jax.ShapeDtypeStruct((r, c), f32)] * 4,
        compiler_params=_cparams(2, arbitrary=False),
    )(parts, w, m, v)


_BIG = ["w_in", "w_out", "w_ffn_gate", "w_ffn_up", "w_ffn_down", "ple_w_proj", "ple_w_gate"]
_SMALL = ["b_in", "hg_lb_logits", "ml_conv_w", "ml_conv_b", "hg_norm_g", "ml_norm_g", "ln1_g", "ln1_b", "ln2_g", "ln2_b", "ple_b_gate"]
_ORDER = ["w_in", "b_in", "hg_lb_logits", "ml_conv_w", "ml_conv_b", "hg_norm_g", "ml_norm_g", "w_out", "ln1_g", "ln1_b",
          "w_ffn_gate", "w_ffn_up", "w_ffn_down", "ln2_g", "ln2_b", "ple_w_proj", "ple_w_gate", "ple_b_gate"]
_PACK_ROWS, _PACK_COLS = 16, 1024


def _pack(arrays):
    flat = jnp.concatenate([a.reshape(-1) for a in arrays])
    return jnp.pad(flat, (0, _PACK_ROWS * _PACK_COLS - flat.shape[0])).reshape(_PACK_ROWS, _PACK_COLS)


def _unpack(pack, shapes):
    flat = pack.reshape(-1)
    out, off = [], 0
    for s in shapes:
        size = 1
        for d in s:
            size *= d
        out.append(flat[off:off + size].reshape(s))
        off += size
    return out


def _to_chip_major(g, col_split):
    if col_split:
        k, n = g.shape
        return g.reshape(k, 4, n // 4).transpose(1, 0, 2)
    k, n = g.shape
    return g.reshape(4, k // 4, n)


def kernel(x, p, w_in, b_in, hg_lb_logits, ml_conv_w, ml_conv_b, hg_norm_g, ml_norm_g, w_out, ln1_g, ln1_b, w_ffn_gate, w_ffn_up, w_ffn_down, ln2_g, ln2_b, ple_w_proj, ple_w_gate, ple_b_gate, loss_target, m_w_in, m_b_in, m_hg_lb_logits, m_ml_conv_w, m_ml_conv_b, m_hg_norm_g, m_ml_norm_g, m_w_out, m_ln1_g, m_ln1_b, m_w_ffn_gate, m_w_ffn_up, m_w_ffn_down, m_ln2_g, m_ln2_b, m_ple_w_proj, m_ple_w_gate, m_ple_b_gate, v_w_in, v_b_in, v_hg_lb_logits, v_ml_conv_w, v_ml_conv_b, v_hg_norm_g, v_ml_norm_g, v_w_out, v_ln1_g, v_ln1_b, v_w_ffn_gate, v_w_ffn_up, v_w_ffn_down, v_ln2_g, v_ln2_b, v_ple_w_proj, v_ple_w_gate, v_ple_b_gate):
    args = dict(locals())
    wts = {k: args[k] for k in _ORDER}
    mom = {k: args["m_" + k] for k in _ORDER}
    var = {k: args["v_" + k] for k in _ORDER}
    two_d = lambda a: a.reshape(a.shape[-2], a.shape[-1])
    block = lambda k, a: jnp.swapaxes(two_d(a), 0, 1) if k in _TRANSPOSED else two_d(a)
    unblock = lambda k, a: (jnp.swapaxes(a, 0, 1) if k in _TRANSPOSED else a).reshape(wts[k].shape)

    shards = {k: block(k, wts[k]).astype(bf16) for k in _BIG}
    w_in_blocks, taps = _gather_first(shards["w_in"], two_d(ml_conv_w), "gather_w_in")
    w_in_full = _from_chip_major(w_in_blocks, False)
    conv_w_full = _from_chip_major(taps, True)

    early_keys = _BIG[1:]
    loss, grad_x, grads, (got_hg, got_ml) = _local_step(
        x[0], p[0, 0], loss_target[0], w_in_full, b_in, hg_lb_logits, conv_w_full, ml_conv_b, hg_norm_g, ml_norm_g,
        None, ln1_g, ln1_b, None, None, None, ln2_g, ln2_b, None, None, ple_b_gate,
        early_hook=lambda low: tuple([_to_chip_major(low[k], k in _COL_SPLIT) for k in names] for names in _SCATTER_PLAN),
        late_shards={k: shards[k] for k in early_keys})

    out_g, out_d, out_m, out_v = {}, {}, {}, {}

    def finish(k, g, d, nm, nv):
        out_g[k], out_d[k], out_m[k], out_v[k] = unblock(k, g), unblock(k, d), unblock(k, nm), unblock(k, nv)

    for names, got in zip(_SCATTER_PLAN, (got_hg, got_ml)):
        for k, rcv in zip(names, got):
            finish(k, *_reduce_adamw(rcv, block(k, wts[k]), block(k, mom[k]), block(k, var[k]), "reduce_adamw_" + k))

    core_sums = _pair_reduce_cols(_to_chip_major(grads["w_in"], False), "pair_reduce_w_in")
    whole = _chip_reduce_swap_cols(_scatter_chips([core_sums], "scatter_grad_w_in")[0], "chip_reduce_w_in")
    finish("w_in", *_adamw(whole[None], block("w_in", wts["w_in"]), block("w_in", mom["w_in"]), block("w_in", var["w_in"]), "adamw_w_in"))

    small_shapes = [(1, PROJ_W), (2, MIX_W), (CONV_K, MIX_W)] + [(1, MIX_W)] * 3 + [(1, D_MODEL)] * 5 + [(1, 1)]
    contrib = _pack([grads[k] for k in _SMALL] + [loss])
    summed = _sum_slots(_gather_all(contrib, "gather_small"), "sum_small")
    small = _unpack(summed, small_shapes)
    loss_total = small[-1].reshape(())
    gsm = dict(zip(_SMALL, small[:-1]))
    place = 2 * lax.axis_index("x") + lax.axis_index("y")
    conv_cols = ml_conv_w.shape[-1]
    gsm["ml_conv_w"] = lax.dynamic_slice(gsm["ml_conv_w"], (0, place * conv_cols), (CONV_K, conv_cols))
    own_shapes = [wts[k].shape for k in _SMALL]
    g_pack = _pack([gsm[k] for k in _SMALL])
    res = _adamw(g_pack[None], _pack([wts[k] for k in _SMALL]), _pack([mom[k] for k in _SMALL]), _pack([var[k] for k in _SMALL]), "adamw_small")
    for dst, pack in zip((out_g, out_d, out_m, out_v), res):
        for k, a in zip(_SMALL, _unpack(pack, own_shapes)):
            dst[k] = a

    outs = [loss_total, grad_x[None]]
    for group in (out_g, out_d, out_m, out_v):
        outs += [group[k] for k in _ORDER]
    return tuple(outs)
```

```python
import jax
import jax.numpy as jnp
from jax import lax
from jax.experimental import pallas as pl
from jax.experimental.pallas import tpu as pltpu

f32 = jnp.float32
bf16 = jnp.bfloat16

D_MODEL = 1024
HEADS = 4
HEAD_W = 128
MIX_W = HEADS * HEAD_W
ML_DQK = 64
PROJ_W = 3592
U_HG = 4 * MIX_W
U_ML = 3 * MIX_W + 128
D_FF = 2816
PLE = 256
CHUNK = 128
SUB = 16
EXP_CAP = 80.0
CONV_K = 4
HALO = 8
ALPHA = float(2.0 ** 0.25)
LN_EPS = 1e-5
RMS_EPS = 1e-6
NEG = -1e30
LR, B1, B2, EPS_ADAM, WD, STEP = 0.001, 0.9, 0.999, 1e-08, 0.01, 10
VMEM_LIMIT = 56 * 1024 * 1024
MIXER_ROWS = 512
DENSE_ROWS = 512
WGRAD_ROWS = 2048


def _cparams(n_axes, arbitrary=True):
    sem = ("arbitrary",) * n_axes if arbitrary else ("parallel",) * n_axes
    return pltpu.CompilerParams(dimension_semantics=sem, vmem_limit_bytes=VMEM_LIMIT)


ACT = bf16


def _mx(a):
    return a.astype(ACT)


def _bdot(a, b):
    return jnp.dot(_mx(a), _mx(b), preferred_element_type=f32)


def _bdot_nt(a, b):
    return lax.dot_general(_mx(a), _mx(b), (((1,), (1,)), ((), ())), preferred_element_type=f32)


def _bdot_tn(a, b):
    return lax.dot_general(_mx(a), _mx(b), (((0,), (0,)), ((), ())), preferred_element_type=f32)


def _split3(x):
    hi = x.astype(bf16)
    r1 = x - hi.astype(f32)
    mid = r1.astype(bf16)
    lo = (r1 - mid.astype(f32)).astype(bf16)
    return hi, mid, lo


def _dot3(a, b, dims):
    a_hi = a.astype(bf16)
    a_lo = (a - a_hi.astype(f32)).astype(bf16)
    b_hi = b.astype(bf16)
    b_lo = (b - b_hi.astype(f32)).astype(bf16)
    dn = (dims, ((), ()))
    return (lax.dot_general(a_hi, b_hi, dn, preferred_element_type=f32) + lax.dot_general(a_hi, b_lo, dn, preferred_element_type=f32)
            + lax.dot_general(a_lo, b_hi, dn, preferred_element_type=f32))


def _lane_sum(x):
    hi = x.astype(bf16)
    lo = (x - hi.astype(f32)).astype(bf16)
    ones = jnp.ones((x.shape[1], 128), bf16)
    return jnp.dot(hi, ones, preferred_element_type=f32) + jnp.dot(lo, ones, preferred_element_type=f32)


def _lane_dot(x, row):
    return _dot3(x, jnp.broadcast_to(row, (128, row.shape[1])), ((1,), (1,)))


def _sel_dot(sel, x):
    sb = sel.astype(bf16)
    return sum(jnp.dot(sb, part, preferred_element_type=f32) for part in _split3(x))


def _sel_dot_nt(sel, x):
    sb = sel.astype(bf16)
    return sum(lax.dot_general(sb, part, (((1,), (1,)), ((), ())), preferred_element_type=f32) for part in _split3(x))


def _sigmoid(x):
    return 1.0 / (1.0 + jnp.exp(-x))


def _log_sigmoid(x):
    return jnp.minimum(x, 0.0) - jnp.log(1.0 + jnp.exp(-jnp.abs(x)))


def _tri(n, upper=False):
    r = lax.broadcasted_iota(jnp.int32, (n, n), 0)
    c = lax.broadcasted_iota(jnp.int32, (n, n), 1)
    return (c >= r) if upper else (c <= r)


def _rows(tm, n, col=0):
    return pl.BlockSpec((tm, n), lambda i, _c=col: (i, _c))


def _rows_rev(tm, n, nb, col=0):
    return pl.BlockSpec((tm, n), lambda i, _c=col, _nb=nb: (_nb - 1 - i, _c))


def _const(shape):
    return pl.BlockSpec(shape, lambda i, _n=len(shape): (0,) * _n)


def _resident(shape):
    return pl.BlockSpec(shape, lambda i, _n=len(shape): (0,) * _n, pipeline_mode=pl.Buffered(1))


def _tile(t, want):
    return want if t % want == 0 else t


def _inproj(x, w_hg, w_ml, b_hg, b_ml, riders=()):
    t = x.shape[0]
    tm = _tile(t, DENSE_ROWS)

    def body(x_ref, whg_ref, wml_ref, bhg_ref, bml_ref, uhg_ref, uml_ref, xb_ref):
        xb = _mx(x_ref[...])
        xb_ref[...] = xb
        uhg_ref[...] = _bdot_nt(xb, whg_ref[...]) + bhg_ref[...]
        uml_ref[...] = _bdot_nt(xb, wml_ref[...]) + bml_ref[...]

    return _riding_call(
        body, "inproj", t // tm,
        in_specs=[_rows(tm, D_MODEL), _resident((U_HG, D_MODEL)), _resident((U_ML, D_MODEL)), _const((1, U_HG)), _const((1, U_ML))],
        out_specs=[_rows(tm, U_HG), _rows(tm, U_ML), _rows(tm, D_MODEL)],
        out_shape=[jax.ShapeDtypeStruct((t, U_HG), f32), jax.ShapeDtypeStruct((t, U_ML), f32), jax.ShapeDtypeStruct((t, D_MODEL), ACT)],
        scratch_shapes=[], operands=(x, w_hg, w_ml, b_hg, b_ml), riders=riders, copies=_gather_copies, ride_shapes=_gather_shapes(riders))


def _hg_gates(hq, hf, lb, tri, b=None):
    s = _sigmoid(hf)
    om = 1.0 - lb
    f = lb + om * s
    k = om * (1.0 - s)
    sq = _sigmoid(hq)
    q = hq * sq
    if b is None:
        b = _sel_dot(tri, jnp.log(f))
    return q, sq, s, f, k, b


def _hg_scores(q, k, b, tril_mask, a=None):
    qts, kts, eqs, eks, rows = [], [], [], [], []
    for i in range(CHUNK // SUB):
        lo = i * SUB
        ref = jnp.zeros_like(b[0:1]) if i == 0 else b[lo - 1:lo]
        eq = jnp.exp(b[lo:lo + SUB] - ref)
        ek = jnp.exp(jnp.minimum(ref - b, EXP_CAP))
        qt = q[lo:lo + SUB] * eq
        kt = k * ek
        if a is None:
            rows.append(_bdot_nt(qt, kt))
        qts.append(qt); kts.append(kt); eqs.append(eq); eks.append(ek)
    if a is None:
        a = jnp.where(tril_mask, jnp.concatenate(rows, axis=0), 0.0)
    return a, qts, kts, eqs, eks


def _head_rms(o, gn, on_mxu=False):
    ms = _lane_sum(o * o) * (1.0 / o.shape[1]) if on_mxu else jnp.mean(o * o, axis=-1, keepdims=True)
    rstd = lax.rsqrt(ms + RMS_EPS)
    oh = o * rstd
    return oh, rstd, oh * gn


def _lower_bound(logit_ref):
    lg = logit_ref[...]
    return _sigmoid(lg[0:1] - lg[1:2])


def _hgrn2_fwd(u_hg, logits, gn, riders=()):
    t = u_hg.shape[0]
    tb = _tile(t, MIXER_ROWS)
    nc_blk = tb // CHUNK

    def body(u_ref, lg_ref, gn_ref, og_ref, sst_ref, b_ref, a_ref, o_ref, st_ref):
        @pl.when(pl.program_id(0) == 0)
        def _():
            st_ref[...] = jnp.zeros_like(st_ref)

        lb_all = _lower_bound(lg_ref)
        tril_mask = _tri(CHUNK)
        tri = tril_mask.astype(f32)

        def chunk(c, carry):
            r0 = pl.multiple_of(c * CHUNK, CHUNK)
            rows = pl.ds(r0, CHUNK)
            heads = range(HEADS)
            cols = [slice(h * HEAD_W, (h + 1) * HEAD_W) for h in heads]
            hv = [u_ref[rows, 2 * MIX_W + h * HEAD_W:2 * MIX_W + (h + 1) * HEAD_W] for h in heads]
            gts = [_hg_gates(u_ref[rows, h * HEAD_W:(h + 1) * HEAD_W], u_ref[rows, MIX_W + h * HEAD_W:MIX_W + (h + 1) * HEAD_W],
                             lb_all[:, cols[h]], tri) for h in heads]
            q = [g[0] for g in gts]
            k = [g[4] for g in gts]
            b = [g[5] for g in gts]
            a = [_hg_scores(q[h], k[h], b[h], tril_mask)[0] for h in heads]
            st = [st_ref[h] for h in heads]
            bl = [b[h][CHUNK - 1:CHUNK] for h in heads]
            o = [_bdot(a[h], hv[h]) + _bdot_nt(q[h] * jnp.exp(b[h]), st[h]) for h in heads]
            new_st = [st[h] * jnp.exp(bl[h]) + _bdot_tn(hv[h], k[h] * jnp.exp(bl[h] - b[h])) for h in heads]
            for h in heads:
                sst_ref[c, h] = st[h]
                st_ref[h] = new_st[h]
                b_ref[rows, cols[h]] = b[h]
                a_ref[rows, cols[h]] = a[h].astype(ACT)
                o_ref[rows, cols[h]] = o[h]
                hgate = u_ref[rows, 3 * MIX_W + h * HEAD_W:3 * MIX_W + (h + 1) * HEAD_W]
                _, _, y = _head_rms(o[h], gn_ref[:, cols[h]])
                og_ref[rows, cols[h]] = (y * (hgate * _sigmoid(hgate))).astype(ACT)
            return carry

        lax.fori_loop(0, nc_blk, chunk, 0, unroll=True)

    assert CHUNK == HEAD_W
    return _riding_call(
        body, "hgrn2_fwd", t // tb,
        in_specs=[_rows(tb, U_HG), _const((2, MIX_W)), _const((1, MIX_W))],
        out_specs=[_rows(tb, MIX_W), pl.BlockSpec((nc_blk, HEADS, HEAD_W, HEAD_W), lambda i: (i, 0, 0, 0)),
                   _rows(tb, MIX_W), _rows(tb, MIX_W), _rows(tb, MIX_W)],
        out_shape=[jax.ShapeDtypeStruct((t, MIX_W), ACT), jax.ShapeDtypeStruct((t // CHUNK, HEADS, HEAD_W, HEAD_W), f32),
                   jax.ShapeDtypeStruct((t, MIX_W), f32), jax.ShapeDtypeStruct((t, MIX_W), ACT), jax.ShapeDtypeStruct((t, MIX_W), f32)],
        scratch_shapes=[pltpu.VMEM((HEADS, HEAD_W, HEAD_W), f32)],
        operands=(u_hg, logits, gn), riders=riders, copies=_gather_copies, ride_shapes=_gather_shapes(riders))


def _hgrn2_bwd(u_hg, logits, gn, sst, bcum, scores, o_raw, dog, riders=()):
    t = u_hg.shape[0]
    tb = _tile(t, MIXER_ROWS)
    nb = t // tb
    nc_blk = tb // CHUNK

    def body(u_ref, lg_ref, gn_ref, sst_ref, b_ref, a_ref, o_ref, dog_ref, du_ref, dlg_ref, dgn_ref, dst_ref):
        @pl.when(pl.program_id(0) == 0)
        def _():
            dst_ref[...] = jnp.zeros_like(dst_ref)
            dlg_ref[...] = jnp.zeros_like(dlg_ref)
            dgn_ref[...] = jnp.zeros_like(dgn_ref)

        lb_all = _lower_bound(lg_ref)
        tril_mask = _tri(CHUNK)
        tri = tril_mask.astype(f32)
        triu = _tri(CHUNK, upper=True).astype(f32)

        def chunk(j, carry):
            c = nc_blk - 1 - j
            r0 = pl.multiple_of(c * CHUNK, CHUNK)
            rows = pl.ds(r0, CHUNK)
            heads = range(HEADS)
            nsub = CHUNK // SUB
            cols = [slice(h * HEAD_W, (h + 1) * HEAD_W) for h in heads]
            hq = [u_ref[rows, h * HEAD_W:(h + 1) * HEAD_W] for h in heads]
            hf = [u_ref[rows, MIX_W + h * HEAD_W:MIX_W + (h + 1) * HEAD_W] for h in heads]
            hv = [u_ref[rows, 2 * MIX_W + h * HEAD_W:2 * MIX_W + (h + 1) * HEAD_W] for h in heads]
            lb = [lb_all[:, cols[h]] for h in heads]
            gts = [_hg_gates(hq[h], hf[h], lb[h], tri, b=b_ref[rows, cols[h]]) for h in heads]
            q, sq, s, f, k, b = ([g[n] for g in gts] for n in range(6))
            scs = [_hg_scores(q[h], k[h], b[h], tril_mask, a=a_ref[rows, cols[h]]) for h in heads]
            a, qts, kts, eqs, eks = ([sc[n] for sc in scs] for n in range(5))
            st = [sst_ref[c, h] for h in heads]
            dst = [dst_ref[h] for h in heads]
            bl = [b[h][CHUNK - 1:CHUNK] for h in heads]
            eb = [jnp.exp(b[h]) for h in heads]
            qh = [q[h] * eb[h] for h in heads]
            ekl = [jnp.exp(bl[h] - b[h]) for h in heads]
            kh = [k[h] * ekl[h] for h in heads]
            o = [o_ref[rows, cols[h]] for h in heads]
            do = []
            for h in heads:
                hgate = u_ref[rows, 3 * MIX_W + h * HEAD_W:3 * MIX_W + (h + 1) * HEAD_W]
                gnh = gn_ref[:, cols[h]]
                oh, rstd, y = _head_rms(o[h], gnh)
                sg = _sigmoid(hgate)
                dogh = dog_ref[rows, cols[h]]
                dy = dogh * (hgate * sg)
                du_ref[rows, 3 * MIX_W + h * HEAD_W:3 * MIX_W + (h + 1) * HEAD_W] = (dogh * y * (sg * (1.0 + hgate * (1.0 - sg)))).astype(ACT)
                dgn_ref[:, cols[h]] += jnp.sum(dy * oh, axis=0, keepdims=True)
                doh = dy * gnh
                do.append(rstd * (doh - oh * jnp.mean(doh * oh, axis=-1, keepdims=True)))
            da = [jnp.where(tril_mask, _bdot_nt(do[h], hv[h]), 0.0) for h in heads]
            dv = [_bdot_tn(a[h], do[h]) + _bdot_nt(kh[h], dst[h]) for h in heads]
            dq = [_bdot(do[h], st[h]) * eb[h] for h in heads]
            dk = [_bdot(hv[h], dst[h]) * ekl[h] for h in heads]
            d_last = [jnp.sum(k[h] * dk[h], axis=0, keepdims=True) + jnp.exp(bl[h]) * jnp.sum(dst[h] * st[h], axis=0, keepdims=True)
                      for h in heads]
            d_b = [q[h] * dq[h] - k[h] * dk[h] for h in heads]
            dqs = [[] for _ in heads]
            q_dq = [[] for _ in heads]
            for i in range(nsub):
                for h in heads:
                    da_i = _mx(da[h][i * SUB:(i + 1) * SUB])
                    q_r, k_r = _mx(qts[h][i]), _mx(kts[h][i])
                    g_q = jnp.dot(da_i, k_r, preferred_element_type=f32)
                    g_k = lax.dot_general(da_i, q_r, (((0,), (0,)), ((), ())), preferred_element_type=f32)
                    dqs[h].append(g_q * eqs[h][i])
                    q_dq[h].append(q_r.astype(f32) * g_q)
                    dk[h] = dk[h] + g_k * eks[h][i]
                    d_b[h] = d_b[h] - k_r.astype(f32) * g_k
            for h in heads:
                dq[h] = dq[h] + jnp.concatenate(dqs[h], axis=0)
                d_b[h] = d_b[h] + jnp.concatenate(q_dq[h], axis=0)
                dst_ref[h] = dst[h] * jnp.exp(bl[h]) + _bdot_tn(do[h], qh[h])
            dg = [_sel_dot(triu, d_b[h]) + d_last[h] for h in heads]
            for h in heads:
                dfk = dg[h] / f[h] - dk[h]
                du_ref[rows, h * HEAD_W:(h + 1) * HEAD_W] = (dq[h] * (sq[h] * (1.0 + hq[h] * (1.0 - sq[h])))).astype(ACT)
                du_ref[rows, MIX_W + h * HEAD_W:MIX_W + (h + 1) * HEAD_W] = ((1.0 - lb[h]) * dfk * s[h] * (1.0 - s[h])).astype(ACT)
                du_ref[rows, 2 * MIX_W + h * HEAD_W:2 * MIX_W + (h + 1) * HEAD_W] = dv[h].astype(ACT)
                dlb = jnp.sum((1.0 - s[h]) * dfk, axis=0, keepdims=True) * (lb[h] * (1.0 - lb[h]))
                dlg_ref[0:1, cols[h]] += dlb
                dlg_ref[1:2, cols[h]] -= dlb
            return carry

        lax.fori_loop(0, nc_blk, chunk, 0, unroll=True)

    rev = _rows_rev(tb, MIX_W, nb)
    return _riding_call(
        body, "hgrn2_bwd", nb,
        in_specs=[_rows_rev(tb, U_HG, nb), _const((2, MIX_W)), _const((1, MIX_W)),
                  pl.BlockSpec((nc_blk, HEADS, HEAD_W, HEAD_W), lambda i: (nb - 1 - i, 0, 0, 0)), rev, rev, rev, rev],
        out_specs=[_rows_rev(tb, U_HG, nb), _const((2, MIX_W)), _const((1, MIX_W))],
        out_shape=[jax.ShapeDtypeStruct((t, U_HG), ACT), jax.ShapeDtypeStruct((2, MIX_W), f32), jax.ShapeDtypeStruct((1, MIX_W), f32)],
        scratch_shapes=[pltpu.VMEM((HEADS, HEAD_W, HEAD_W), f32)],
        operands=(u_hg, logits, gn, sst, bcum, scores, o_raw, dog), riders=riders, copies=_scatter_copies,
        ride_shapes=[jax.ShapeDtypeStruct(r.shape, r.dtype) for r in riders])


def _conv_fwd(u_ml, w, b):
    t = u_ml.shape[0]
    tm = _tile(t, 512)

    def body(x_ref, w_ref, b_ref, pre_ref, act_ref, xbuf):
        @pl.when(pl.program_id(0) == 0)
        def _():
            xbuf[...] = jnp.zeros_like(xbuf)

        xbuf[0:HALO, :] = xbuf[tm:tm + HALO, :]
        xbuf[HALO:HALO + tm, :] = x_ref[...]
        pre = b_ref[...] + jnp.zeros((tm, MIX_W), f32)
        for kk in range(CONV_K):
            off = HALO - (CONV_K - 1) + kk
            pre = pre + w_ref[kk:kk + 1, :] * xbuf[off:off + tm, :]
        pre_ref[...] = pre
        act_ref[...] = pre * _sigmoid(pre)

    return pl.pallas_call(
        body, name="conv_fwd", grid=(t // tm,),
        in_specs=[_rows(tm, MIX_W), _const((CONV_K, MIX_W)), _const((1, MIX_W))],
        out_specs=[_rows(tm, MIX_W), _rows(tm, MIX_W)],
        out_shape=[jax.ShapeDtypeStruct((t, MIX_W), f32)] * 2,
        scratch_shapes=[pltpu.VMEM((tm + HALO, MIX_W), f32)],
        compiler_params=_cparams(1),
    )(u_ml, w, b)


def _conv_bwd(u_ml, w, pre, dact):
    t = u_ml.shape[0]
    tm = _tile(t, 512)
    nb = t // tm
    hb = tm // HALO

    def body(x_ref, halo_ref, w_ref, pre_ref, dact_ref, dx_ref, dw_ref, db_ref, dbuf, xbuf):
        i = pl.program_id(0)

        @pl.when(i == 0)
        def _():
            dbuf[...] = jnp.zeros_like(dbuf)
            dw_ref[...] = jnp.zeros_like(dw_ref)
            db_ref[...] = jnp.zeros_like(db_ref)

        p = pre_ref[...]
        sg = _sigmoid(p)
        dpre = dact_ref[...] * (sg * (1.0 + p * (1.0 - sg)))
        dbuf[tm:tm + HALO, :] = dbuf[0:HALO, :]
        dbuf[0:tm, :] = dpre
        has_prev = (i < nb - 1).astype(f32)
        xbuf[0:HALO, :] = halo_ref[...] * has_prev
        xbuf[HALO:HALO + tm, :] = x_ref[...]
        dx = jnp.zeros((tm, MIX_W), f32)
        for kk in range(CONV_K):
            back = CONV_K - 1 - kk
            dx = dx + w_ref[kk:kk + 1, :] * dbuf[back:back + tm, :]
            off = HALO - (CONV_K - 1) + kk
            dw_ref[kk:kk + 1, :] += jnp.sum(dpre * xbuf[off:off + tm, :], axis=0, keepdims=True)
        dx_ref[...] = dx.astype(ACT)
        db_ref[...] += jnp.sum(dpre, axis=0, keepdims=True)

    return pl.pallas_call(
        body, name="conv_bwd", grid=(nb,),
        in_specs=[_rows_rev(tm, MIX_W, nb),
                  pl.BlockSpec((HALO, MIX_W), lambda i: (jnp.maximum((nb - 1 - i) * hb - 1, 0), 0)),
                  _const((CONV_K, MIX_W)), _rows_rev(tm, MIX_W, nb), _rows_rev(tm, MIX_W, nb)],
        out_specs=[_rows_rev(tm, MIX_W, nb), _const((CONV_K, MIX_W)), _const((1, MIX_W))],
        out_shape=[jax.ShapeDtypeStruct((t, MIX_W), ACT), jax.ShapeDtypeStruct((CONV_K, MIX_W), f32), jax.ShapeDtypeStruct((1, MIX_W), f32)],
        scratch_shapes=[pltpu.VMEM((tm + HALO, MIX_W), f32), pltpu.VMEM((tm + HALO, MIX_W), f32)],
        compiler_params=_cparams(1),
    )(u_ml, u_ml, w, pre, dact)


def _lane_pick(x, lane):
    idx = lax.broadcasted_iota(jnp.int32, x.shape, 1)
    return jnp.sum(jnp.where(idx == lane, x, 0.0), axis=-1, keepdims=True)


def _ml_gate_forms(gates, tri):
    lf = _log_sigmoid(gates)
    gc = _sel_dot(tri, lf)
    lane = lax.broadcasted_iota(jnp.int32, gates.shape, 1)
    mixed = jnp.where(lane < HEADS, gates, gc)
    sel = (lax.broadcasted_iota(jnp.int32, (8, 128), 0) == lax.broadcasted_iota(jnp.int32, (8, 128), 1)).astype(f32)
    rowsf = _sel_dot_nt(sel, mixed)
    return gc, rowsf


def _ml_chunk(q, k, v, gates, gc, rowsf, c_st, n_st, m_st, tril_mask):
    hs = range(HEADS)
    g_col = [_lane_pick(gc, HEADS + h) for h in hs]
    ig_col = [_lane_pick(gates, h) for h in hs]
    dmat = [jnp.where(tril_mask, g_col[h] - rowsf[HEADS + h:HEADS + h + 1, :] + rowsf[h:h + 1, :], NEG) for h in hs]
    m_inter = [g_col[h] + m_st[h] for h in hs]
    m_t = [jnp.maximum(m_inter[h], jnp.max(dmat[h], axis=-1, keepdims=True)) for h in hs]
    wi = [jnp.exp(dmat[h] - m_t[h]) for h in hs]
    wo = [jnp.exp(m_inter[h] - m_t[h]) for h in hs]
    qk = [_bdot_nt(q[h], k[h]) * wi[h] for h in hs]
    num = [_bdot(qk[h], v[h]) + wo[h] * _bdot(q[h], c_st[h]) for h in hs]
    den = [_lane_sum(qk[h]) + wo[h] * _lane_dot(q[h], n_st[h]) for h in hs]
    floor = [jnp.exp(-m_t[h]) for h in hs]
    z = [jnp.maximum(jnp.abs(den[h]), floor[h]) for h in hs]
    g_last = [g_col[h][CHUNK - 1:CHUNK] for h in hs]
    a_col = [g_last[h] - g_col[h] + ig_col[h] for h in hs]
    m_new = [jnp.maximum(g_last[h] + m_st[h], jnp.max(a_col[h], axis=0, keepdims=True)) for h in hs]
    ws = [jnp.exp(a_col[h] - m_new[h]) for h in hs]
    w_old = [jnp.exp(g_last[h] + m_st[h] - m_new[h]) for h in hs]
    return dict(wi=wi, wo=wo, qk=qk, num=num, den=den, z=z, floor=floor, ws=ws, w_old=w_old, m_new=m_new)


def _mlstm_fwd(qkc, u_ml, gn, riders=()):
    t = qkc.shape[0]
    tb = _tile(t, MIXER_ROWS)
    nc_blk = tb // CHUNK

    def body(qk_ref, v_ref, mo_ref, gt_ref, gn_ref, og_ref, cst_ref, nst_ref, mst_ref, c_sc, n_sc, m_sc):
        @pl.when(pl.program_id(0) == 0)
        def _():
            c_sc[...] = jnp.zeros_like(c_sc)
            n_sc[...] = jnp.zeros_like(n_sc)
            m_sc[...] = jnp.zeros_like(m_sc)

        tril_mask = _tri(CHUNK)
        tri = tril_mask.astype(f32)

        def chunk(c, carry):
            r0 = pl.multiple_of(c * CHUNK, CHUNK)
            rows = pl.ds(r0, CHUNK)
            gates = gt_ref[rows, :]
            gc, rowsf = _ml_gate_forms(gates, tri)
            hs = range(HEADS)
            q = [qk_ref[rows, h * ML_DQK:(h + 1) * ML_DQK] * (ML_DQK ** -0.5) for h in hs]
            k = [qk_ref[rows, HEADS * ML_DQK + h * ML_DQK:HEADS * ML_DQK + (h + 1) * ML_DQK] for h in hs]
            v = [v_ref[rows, h * HEAD_W:(h + 1) * HEAD_W] for h in hs]
            c_st = [c_sc[h] for h in hs]
            n_st = [n_sc[h] for h in hs]
            m_full = [m_sc[h] for h in hs]
            r = _ml_chunk(q, k, v, gates, gc, rowsf, c_st, n_st, [m[:, 0:1] for m in m_full], tril_mask)
            ksc = [k[h] * r["ws"][h] for h in hs]
            new_c = [r["w_old"][h] * c_st[h] + _bdot_tn(ksc[h], v[h]) for h in hs]
            for h in hs:
                cs = slice(h * HEAD_W, (h + 1) * HEAD_W)
                cst_ref[c, h] = c_st[h]
                nst_ref[c, h] = n_st[h]
                mst_ref[c, h] = m_full[h]
                c_sc[h] = new_c[h]
                n_sc[h] = r["w_old"][h] * n_st[h] + jnp.sum(ksc[h], axis=0, keepdims=True)
                m_sc[h] = r["m_new"][h] + jnp.zeros((1, 128), f32)
                _, _, y = _head_rms(r["num"][h] / r["z"][h], gn_ref[:, cs], on_mxu=True)
                og_ref[rows, cs] = (y * _sigmoid(mo_ref[rows, h * HEAD_W:(h + 1) * HEAD_W])).astype(ACT)
            return carry

        lax.fori_loop(0, nc_blk, chunk, 0, unroll=True)

    nchunks = t // CHUNK
    return _riding_call(
        body, "mlstm_fwd", t // tb,
        in_specs=[_rows(tb, MIX_W), _rows(tb, MIX_W, 1), _rows(tb, MIX_W, 2), _rows(tb, 128, 12), _const((1, MIX_W))],
        out_specs=[_rows(tb, MIX_W),
                   pl.BlockSpec((nc_blk, HEADS, ML_DQK, HEAD_W), lambda i: (i, 0, 0, 0)),
                   pl.BlockSpec((nc_blk, HEADS, 1, ML_DQK), lambda i: (i, 0, 0, 0)),
                   pl.BlockSpec((nc_blk, HEADS, 1, 128), lambda i: (i, 0, 0, 0))],
        out_shape=[jax.ShapeDtypeStruct((t, MIX_W), ACT),
                   jax.ShapeDtypeStruct((nchunks, HEADS, ML_DQK, HEAD_W), f32),
                   jax.ShapeDtypeStruct((nchunks, HEADS, 1, ML_DQK), f32),
                   jax.ShapeDtypeStruct((nchunks, HEADS, 1, 128), f32)],
        scratch_shapes=[pltpu.VMEM((HEADS, ML_DQK, HEAD_W), f32), pltpu.VMEM((HEADS, 1, ML_DQK), f32), pltpu.VMEM((HEADS, 1, 128), f32)],
        operands=(qkc, u_ml, u_ml, u_ml, gn), riders=riders, copies=_gather_copies, ride_shapes=_gather_shapes(riders))


def _mlstm_bwd(qkc, u_ml, gn, cst, nst, mst, dog, riders=()):
    t = qkc.shape[0]
    tb = _tile(t, MIXER_ROWS)
    nb = t // tb
    nc_blk = tb // CHUNK

    def body(qk_ref, v_ref, mo_ref, gt_ref, gn_ref, cst_ref, nst_ref, mst_ref, dog_ref,
             dqk_ref, dv_ref, dmo_ref, dgt_ref, dgn_ref, dc_sc, dn_sc):
        @pl.when(pl.program_id(0) == 0)
        def _():
            dc_sc[...] = jnp.zeros_like(dc_sc)
            dn_sc[...] = jnp.zeros_like(dn_sc)
            dgn_ref[...] = jnp.zeros_like(dgn_ref)

        tril_mask = _tri(CHUNK)
        tri = tril_mask.astype(f32)
        triu = _tri(CHUNK, upper=True).astype(f32)
        lane = lax.broadcasted_iota(jnp.int32, (CHUNK, 128), 1)

        def chunk(j, carry):
            c = nc_blk - 1 - j
            r0 = pl.multiple_of(c * CHUNK, CHUNK)
            rows = pl.ds(r0, CHUNK)
            gates = gt_ref[rows, :]
            gc, rowsf = _ml_gate_forms(gates, tri)
            dg_mat = jnp.zeros((CHUNK, 128), f32)
            dig_mat = jnp.zeros((CHUNK, 128), f32)
            dlast_row = jnp.zeros((1, 128), f32)
            hs = range(HEADS)
            cols = [slice(h * HEAD_W, (h + 1) * HEAD_W) for h in hs]
            q = [qk_ref[rows, h * ML_DQK:(h + 1) * ML_DQK] * (ML_DQK ** -0.5) for h in hs]
            k = [qk_ref[rows, HEADS * ML_DQK + h * ML_DQK:HEADS * ML_DQK + (h + 1) * ML_DQK] for h in hs]
            v = [v_ref[rows, h * HEAD_W:(h + 1) * HEAD_W] for h in hs]
            c_st = [cst_ref[c, h] for h in hs]
            n_st = [nst_ref[c, h] for h in hs]
            m_st = [mst_ref[c, h][:, 0:1] for h in hs]
            dc = [dc_sc[h] for h in hs]
            dn = [dn_sc[h] for h in hs]
            r = _ml_chunk(q, k, v, gates, gc, rowsf, c_st, n_st, m_st, tril_mask)
            z, wi, wo, ws, w_old, den = r["z"], r["wi"], r["wo"], r["ws"], r["w_old"], r["den"]
            hh = [r["num"][h] / z[h] for h in hs]
            dh = []
            for h in hs:
                gnh = gn_ref[:, cols[h]]
                oh, rstd, y = _head_rms(hh[h], gnh, on_mxu=True)
                sg = _sigmoid(mo_ref[rows, h * HEAD_W:(h + 1) * HEAD_W])
                dogh = dog_ref[rows, cols[h]]
                dy = dogh * sg
                dmo_ref[rows, cols[h]] = (dogh * y * (sg * (1.0 - sg))).astype(ACT)
                dgn_ref[:, cols[h]] += jnp.sum(dy * oh, axis=0, keepdims=True)
                doh = dy * gnh
                dh.append(rstd * (doh - oh * (_lane_sum(doh * oh) * (1.0 / HEAD_W))))
            dnum = [dh[h] / z[h] for h in hs]
            dz = [-_lane_sum(dh[h] * hh[h]) / z[h] for h in hs]
            dden = [jnp.where(jnp.abs(den[h]) > r["floor"][h], dz[h] * jnp.sign(den[h]), 0.0) for h in hs]
            dsw = [(_bdot_nt(dnum[h], v[h]) + dden[h]) * wi[h] for h in hs]
            dq = [_bdot(dsw[h], k[h]) + wo[h] * (_bdot_nt(dnum[h], c_st[h]) + dden[h][:, :ML_DQK] * n_st[h]) for h in hs]
            dk_state = [ws[h] * (_bdot_nt(v[h], dc[h]) + dn[h]) for h in hs]
            dk = [_bdot_tn(dsw[h], q[h]) + dk_state[h] for h in hs]
            dv = [_bdot_tn(r["qk"][h], dnum[h]) + ws[h] * _bdot(k[h], dc[h]) for h in hs]
            woq = [wo[h] * q[h] for h in hs]
            new_dc = [w_old[h] * dc[h] + _bdot_tn(woq[h], dnum[h]) for h in hs]
            for h in hs:
                dv_ref[rows, cols[h]] = dv[h].astype(ACT)
                dc_sc[h] = new_dc[h]
                dn_sc[h] = w_old[h] * dn[h] + jnp.sum(woq[h] * dden[h][:, :ML_DQK], axis=0, keepdims=True)
                d_last = (jnp.sum(jnp.sum(k[h] * dk_state[h], axis=0, keepdims=True), axis=-1, keepdims=True)
                          + w_old[h] * (jnp.sum(jnp.sum(dc[h] * c_st[h], axis=0, keepdims=True), axis=-1, keepdims=True)
                                        + jnp.sum(dn[h] * n_st[h], axis=-1, keepdims=True)))
                kdk = _lane_sum(k[h] * dk[h])
                qdq = _lane_sum(q[h] * dq[h])
                dg_mat = dg_mat + jnp.where(lane == HEADS + h, qdq - kdk, 0.0)
                dlast_row = dlast_row + jnp.where(lane[0:1] == HEADS + h, d_last, 0.0)
                dig_mat = dig_mat + jnp.where(lane == h, kdk, 0.0)
                dqk_ref[rows, h * ML_DQK:(h + 1) * ML_DQK] = dq[h] * (ML_DQK ** -0.5)
                dqk_ref[rows, HEADS * ML_DQK + h * ML_DQK:HEADS * ML_DQK + (h + 1) * ML_DQK] = dk[h]
            dlf = _sel_dot(triu, dg_mat) + dlast_row
            dgt_ref[rows, :] = (dig_mat + dlf * _sigmoid(-gates)).astype(ACT)
            return carry

        lax.fori_loop(0, nc_blk, chunk, 0, unroll=True)

    st4 = lambda a, b: pl.BlockSpec((nc_blk, HEADS, a, b), lambda i: (nb - 1 - i, 0, 0, 0))
    return _riding_call(
        body, "mlstm_bwd", nb,
        in_specs=[_rows_rev(tb, MIX_W, nb), _rows_rev(tb, MIX_W, nb, 1), _rows_rev(tb, MIX_W, nb, 2), _rows_rev(tb, 128, nb, 12),
                  _const((1, MIX_W)), st4(ML_DQK, HEAD_W), st4(1, ML_DQK), st4(1, 128), _rows_rev(tb, MIX_W, nb)],
        out_specs=[_rows_rev(tb, MIX_W, nb), _rows_rev(tb, MIX_W, nb), _rows_rev(tb, MIX_W, nb), _rows_rev(tb, 128, nb), _const((1, MIX_W))],
        out_shape=[jax.ShapeDtypeStruct((t, MIX_W), f32), jax.ShapeDtypeStruct((t, MIX_W), ACT), jax.ShapeDtypeStruct((t, MIX_W), ACT),
                   jax.ShapeDtypeStruct((t, 128), ACT), jax.ShapeDtypeStruct((1, MIX_W), f32)],
        scratch_shapes=[pltpu.VMEM((HEADS, ML_DQK, HEAD_W), f32), pltpu.VMEM((HEADS, 1, ML_DQK), f32)],
        operands=(qkc, u_ml, u_ml, u_ml, gn, cst, nst, mst, dog), riders=riders, copies=_scatter_copies,
        ride_shapes=[jax.ShapeDtypeStruct(r.shape, r.dtype) for r in riders])


def _ln_fwd(r, g, b):
    mu = jnp.mean(r, axis=-1, keepdims=True)
    xc = r - mu
    rstd = lax.rsqrt(jnp.mean(xc * xc, axis=-1, keepdims=True) + LN_EPS)
    xh = xc * rstd
    return xh * g + b, xh, rstd


def _ln_bwd(dy, xh, rstd, g):
    dxh = dy * g
    return rstd * (dxh - jnp.mean(dxh, axis=-1, keepdims=True) - xh * jnp.mean(dxh * xh, axis=-1, keepdims=True))


def _outproj_ln1(og_hg, og_ml, x, w_out, g, b, riders=()):
    t = x.shape[0]
    tm = _tile(t, DENSE_ROWS)

    def body(a_ref, b_ref, x_ref, w_ref, g_ref, bb_ref, x1_ref, xh_ref, rs_ref, x1b_ref):
        mix = _bdot(a_ref[...], w_ref[0:MIX_W, :]) + _bdot(b_ref[...], w_ref[MIX_W:2 * MIX_W, :])
        y, xh, rstd = _ln_fwd(ALPHA * x_ref[...] + mix, g_ref[...], bb_ref[...])
        x1_ref[...] = y
        x1b_ref[...] = y.astype(ACT)
        xh_ref[...] = xh.astype(ACT)
        rs_ref[...] = rstd

    return _riding_call(
        body, "outproj_ln1", t // tm,
        in_specs=[_rows(tm, MIX_W), _rows(tm, MIX_W), _rows(tm, D_MODEL), _resident((D_MODEL, D_MODEL)), _const((1, D_MODEL)), _const((1, D_MODEL))],
        out_specs=[_rows(tm, D_MODEL), _rows(tm, D_MODEL), _rows(tm, 1), _rows(tm, D_MODEL)],
        out_shape=[jax.ShapeDtypeStruct((t, D_MODEL), f32), jax.ShapeDtypeStruct((t, D_MODEL), ACT), jax.ShapeDtypeStruct((t, 1), f32),
                   jax.ShapeDtypeStruct((t, D_MODEL), ACT)],
        scratch_shapes=[], operands=(og_hg, og_ml, x, w_out, g, b), riders=riders, copies=_gather_copies, ride_shapes=_gather_shapes(riders))


def _ffn_up(x1, wg, wu, riders=()):
    t = x1.shape[0]
    tm = _tile(t, DENSE_ROWS)

    def body(x_ref, wg_ref, wu_ref, hg_ref, up_ref, a_ref):
        xv = x_ref[...]
        hg = _bdot_nt(xv, wg_ref[...])
        up = _bdot_nt(xv, wu_ref[...])
        hg_ref[...] = hg.astype(ACT)
        up_ref[...] = up.astype(ACT)
        a_ref[...] = (hg * _sigmoid(hg) * up).astype(ACT)

    return _riding_call(
        body, "ffn_up", t // tm,
        in_specs=[_rows(tm, D_MODEL), _resident((D_FF, D_MODEL)), _resident((D_FF, D_MODEL))],
        out_specs=[_rows(tm, D_FF), _rows(tm, D_FF), _rows(tm, D_FF)],
        out_shape=[jax.ShapeDtypeStruct((t, D_FF), ACT), jax.ShapeDtypeStruct((t, D_FF), ACT), jax.ShapeDtypeStruct((t, D_FF), ACT)],
        scratch_shapes=[], operands=(x1, wg, wu), riders=riders, copies=_gather_copies, ride_shapes=_gather_shapes(riders))


def _ffn_down_ln2(a, x1, wd, g, b):
    t = x1.shape[0]
    tm = _tile(t, DENSE_ROWS)

    def body(a_ref, x_ref, w_ref, g_ref, bb_ref, x2_ref, xh_ref, rs_ref, x2b_ref):
        ffn = _bdot(a_ref[...], w_ref[...])
        y, xh, rstd = _ln_fwd(ALPHA * x_ref[...] + ffn, g_ref[...], bb_ref[...])
        x2_ref[...] = y
        x2b_ref[...] = y.astype(ACT)
        xh_ref[...] = xh.astype(ACT)
        rs_ref[...] = rstd

    return pl.pallas_call(
        body, name="ffn_down_ln2", grid=(t // tm,),
        in_specs=[_rows(tm, D_FF), _rows(tm, D_MODEL), _resident((D_FF, D_MODEL)), _const((1, D_MODEL)), _const((1, D_MODEL))],
        out_specs=[_rows(tm, D_MODEL), _rows(tm, D_MODEL), _rows(tm, 1), _rows(tm, D_MODEL)],
        out_shape=[jax.ShapeDtypeStruct((t, D_MODEL), f32), jax.ShapeDtypeStruct((t, D_MODEL), ACT), jax.ShapeDtypeStruct((t, 1), f32),
                   jax.ShapeDtypeStruct((t, D_MODEL), ACT)],
        compiler_params=_cparams(1, arbitrary=False),
    )(a, x1, wd, g, b)


def _head_loss_bwd(x2, xh2, rs2, p, tgt, w_pg, b_pg, w_pp, g2):
    t = x2.shape[0]
    tm = _tile(t, DENSE_ROWS)

    def body(x_ref, xh_ref, rs_ref, p_ref, t_ref, wg_ref, bg_ref, wp_ref, g_ref,
             dr_ref, de_ref, dz_ref, loss_ref, dbg_ref, dg2_ref, db2_ref):
        @pl.when(pl.program_id(0) == 0)
        def _():
            loss_ref[...] = jnp.zeros_like(loss_ref)
            dbg_ref[...] = jnp.zeros_like(dbg_ref)
            dg2_ref[...] = jnp.zeros_like(dg2_ref)
            db2_ref[...] = jnp.zeros_like(db2_ref)

        x2v = x_ref[...]
        z = _bdot(x2v, wg_ref[...]) + bg_ref[...]
        e = _bdot(p_ref[...], wp_ref[...])
        sg = _sigmoid(z)
        diff = x2v + sg * e - t_ref[...]
        loss_ref[...] += 0.5 * jnp.sum(jnp.mean(diff * diff, axis=-1, keepdims=True), axis=0, keepdims=True)
        dy = diff * (1.0 / D_MODEL)
        de_ref[...] = (dy * sg).astype(ACT)
        dz = dy * e * (sg * (1.0 - sg))
        dz_ref[...] = dz.astype(ACT)
        dbg_ref[...] += jnp.sum(dz, axis=0, keepdims=True)
        dx2 = dy + _bdot_nt(dz, wg_ref[...])
        xh = xh_ref[...].astype(f32)
        dg2_ref[...] += jnp.sum(dx2 * xh, axis=0, keepdims=True)
        db2_ref[...] += jnp.sum(dx2, axis=0, keepdims=True)
        dr_ref[...] = _ln_bwd(dx2, xh, rs_ref[...], g_ref[...])

    row = jax.ShapeDtypeStruct((1, D_MODEL), f32)
    return pl.pallas_call(
        body, name="head_loss_bwd", grid=(t // tm,),
        in_specs=[_rows(tm, D_MODEL), _rows(tm, D_MODEL), _rows(tm, 1), _rows(tm, PLE), _rows(tm, D_MODEL),
                  _resident((D_MODEL, D_MODEL)), _const((1, D_MODEL)), _resident((PLE, D_MODEL)), _const((1, D_MODEL))],
        out_specs=[_rows(tm, D_MODEL), _rows(tm, D_MODEL), _rows(tm, D_MODEL), _const((1, 1)), _const((1, D_MODEL)), _const((1, D_MODEL)), _const((1, D_MODEL))],
        out_shape=[jax.ShapeDtypeStruct((t, D_MODEL), f32), jax.ShapeDtypeStruct((t, D_MODEL), ACT), jax.ShapeDtypeStruct((t, D_MODEL), ACT),
                   jax.ShapeDtypeStruct((1, 1), f32), row, row, row],
        compiler_params=_cparams(1),
    )(x2, xh2, rs2, p, tgt, w_pg, b_pg, w_pp, g2)


def _ffn_bwd(dr2, hg, up, xh1, rs1, wd, wg, wu, g1, w_out):
    t = dr2.shape[0]
    tm = _tile(t, DENSE_ROWS // 2)

    def body(dr_ref, hg_ref, up_ref, xh_ref, rs_ref, wd_ref, wg_ref, wu_ref, g_ref, wo_ref,
             dr1_ref, dhg_ref, dup_ref, dg1_ref, db1_ref, doghg_ref, dogml_ref):
        @pl.when(pl.program_id(0) == 0)
        def _():
            dg1_ref[...] = jnp.zeros_like(dg1_ref)
            db1_ref[...] = jnp.zeros_like(db1_ref)

        dr2v = dr_ref[...]
        da = _bdot_nt(dr2v, wd_ref[...])
        hgv = hg_ref[...].astype(f32)
        sg = _sigmoid(hgv)
        dhg = da * up_ref[...].astype(f32) * (sg * (1.0 + hgv * (1.0 - sg)))
        dup = da * (hgv * sg)
        dhg_ref[...] = dhg.astype(ACT)
        dup_ref[...] = dup.astype(ACT)
        dx1 = ALPHA * dr2v + _bdot(dhg, wg_ref[...]) + _bdot(dup, wu_ref[...])
        xh = xh_ref[...].astype(f32)
        dg1_ref[...] += jnp.sum(dx1 * xh, axis=0, keepdims=True)
        db1_ref[...] += jnp.sum(dx1, axis=0, keepdims=True)
        dr1 = _ln_bwd(dx1, xh, rs_ref[...], g_ref[...])
        dr1_ref[...] = dr1
        dog = _bdot_nt(dr1, wo_ref[...])
        doghg_ref[...] = dog[:, 0:MIX_W]
        dogml_ref[...] = dog[:, MIX_W:2 * MIX_W]

    row = jax.ShapeDtypeStruct((1, D_MODEL), f32)
    return pl.pallas_call(
        body, name="ffn_bwd", grid=(t // tm,),
        in_specs=[_rows(tm, D_MODEL), _rows(tm, D_FF), _rows(tm, D_FF), _rows(tm, D_MODEL), _rows(tm, 1),
                  _resident((D_FF, D_MODEL)), _resident((D_FF, D_MODEL)), _resident((D_FF, D_MODEL)), _const((1, D_MODEL)),
                  _resident((D_MODEL, D_MODEL))],
        out_specs=[_rows(tm, D_MODEL), _rows(tm, D_FF), _rows(tm, D_FF), _const((1, D_MODEL)), _const((1, D_MODEL)),
                   _rows(tm, MIX_W), _rows(tm, MIX_W)],
        out_shape=[jax.ShapeDtypeStruct((t, D_MODEL), f32), jax.ShapeDtypeStruct((t, D_FF), ACT), jax.ShapeDtypeStruct((t, D_FF), ACT), row, row,
                   jax.ShapeDtypeStruct((t, MIX_W), f32), jax.ShapeDtypeStruct((t, MIX_W), f32)],
        compiler_params=_cparams(1),
    )(dr2, hg, up, xh1, rs1, wd, wg, wu, g1, w_out)


def _inproj_bwd(dr1, du_hg, dqk, dmv, dmo, dgt, w_hg, w_ml):
    t = dr1.shape[0]
    tm = _tile(t, DENSE_ROWS)

    def body(dr_ref, dhg_ref, dqk_ref, dmv_ref, dmo_ref, dgt_ref, whg_ref, wml_ref, gx_ref, dml_ref):
        dml = jnp.concatenate([dqk_ref[...], dmv_ref[...], dmo_ref[...], dgt_ref[...]], axis=-1).astype(ACT)
        dml_ref[...] = dml
        gx_ref[...] = ALPHA * dr_ref[...] + _bdot(dhg_ref[...], whg_ref[...]) + _bdot(dml, wml_ref[...])

    return pl.pallas_call(
        body, name="inproj_bwd", grid=(t // tm,),
        in_specs=[_rows(tm, D_MODEL), _rows(tm, U_HG), _rows(tm, MIX_W), _rows(tm, MIX_W), _rows(tm, MIX_W), _rows(tm, 128),
                  _resident((U_HG, D_MODEL)), _resident((U_ML, D_MODEL))],
        out_specs=[_rows(tm, D_MODEL), _rows(tm, U_ML)],
        out_shape=[jax.ShapeDtypeStruct((t, D_MODEL), f32), jax.ShapeDtypeStruct((t, U_ML), ACT)],
        compiler_params=_cparams(1, arbitrary=False),
    )(dr1, du_hg, dqk, dmv, dmo, dgt, w_hg, w_ml)


def _wgrad(a, b, name, tk=None, tn=None, colsum=False, low=False):
    t, kdim = a.shape
    n = b.shape[1]
    tk = tk or kdim
    tn = tn or n
    tt = _tile(t, WGRAD_ROWS)
    nt = t // tt
    assert not (colsum and low) and (not colsum or tn == n)

    def body(a_ref, b_ref, o_ref, *s_ref):
        @pl.when(pl.program_id(2) == 0)
        def _():
            o_ref[...] = jnp.zeros_like(o_ref)
            if colsum:
                s_ref[0][...] = jnp.zeros_like(s_ref[0])

        av = a_ref[...]
        o_ref[...] += _bdot_tn(av, b_ref[...])
        if colsum:
            s_ref[0][...] += jnp.sum(av.astype(f32), axis=0, keepdims=True)
        if low:
            @pl.when(pl.program_id(2) == nt - 1)
            def _():
                s_ref[0][...] = o_ref[...].astype(bf16)

    out_specs = [pl.BlockSpec((tk, tn), lambda i, j, s: (i, j))]
    out_shape = [jax.ShapeDtypeStruct((kdim, n), f32)]
    if colsum:
        out_specs.append(pl.BlockSpec((1, tk), lambda i, j, s: (0, i)))
        out_shape.append(jax.ShapeDtypeStruct((1, kdim), f32))
    if low:
        out_specs.append(pl.BlockSpec((tk, tn), lambda i, j, s: (i, j)))
        out_shape.append(jax.ShapeDtypeStruct((kdim, n), bf16))
    res = pl.pallas_call(
        body, name=name, grid=(kdim // tk, n // tn, t // tt),
        in_specs=[pl.BlockSpec((tt, tk), lambda i, j, s: (s, i)), pl.BlockSpec((tt, tn), lambda i, j, s: (s, j))],
        out_specs=out_specs, out_shape=out_shape,
        compiler_params=_cparams(3),
    )(a, b)
    return res if (colsum or low) else res[0]


_TRANSPOSED = {"w_in", "w_ffn_gate", "w_ffn_up"}
_COL_SPLIT = {"ple_w_proj"}
_SCATTER_PLAN = (("w_ffn_gate", "w_ffn_up"), ("w_ffn_down", "w_out", "ple_w_gate", "ple_w_proj"))
_RIDE_PLAN = {"inproj": ("w_ffn_gate",), "hgrn2_fwd": ("w_ffn_up",), "mlstm_fwd": ("w_out",),
              "outproj_ln1": ("ple_w_gate", "ple_w_proj"), "ffn_up": ("w_ffn_down",)}


def _from_chip_major(a, col_split):
    if col_split:
        return a.transpose(1, 0, 2).reshape(a.shape[1], 4 * a.shape[2])
    return a.reshape(4 * a.shape[1], a.shape[2])


def _local_step(x, p, tgt, w_in_b, b_in, logits, conv_w, conv_b, hg_gn, ml_gn, w_out_b, ln1_g, ln1_b,
                wg_b, wu_b, wd_b, ln2_g, ln2_b, w_pp_b, w_pg_b, b_pg, early_hook=None, late_shards=None):
    pad_w = U_HG + U_ML - PROJ_W
    w_hg = w_in_b[:U_HG]
    w_ml = jnp.pad(w_in_b[U_HG:], ((0, pad_w), (0, 0)))
    bb_hg = b_in[:, :U_HG]
    bb_ml = jnp.pad(b_in[:, U_HG:], ((0, 0), (0, pad_w)))

    late = dict(w_out=w_out_b, w_ffn_gate=wg_b, w_ffn_up=wu_b, w_ffn_down=wd_b, ple_w_proj=w_pp_b, ple_w_gate=w_pg_b)

    def riders_of(call):
        return [late_shards[k] for k in _RIDE_PLAN[call]] if late_shards is not None else ()

    def arrived(call, got):
        for k, g in zip(_RIDE_PLAN[call], got):
            late[k] = _from_chip_major(g, k in _COL_SPLIT)

    (u_hg, u_ml, xb), got = _inproj(x, w_hg, w_ml, bb_hg, bb_ml, riders_of("inproj"))
    arrived("inproj", got)
    (og_hg, sst, hg_b, hg_a, hg_o), got = _hgrn2_fwd(u_hg, logits, hg_gn, riders_of("hgrn2_fwd"))
    arrived("hgrn2_fwd", got)
    pre, qkc = _conv_fwd(u_ml, conv_w, conv_b)
    (og_ml, cst, nst, mst), got = _mlstm_fwd(qkc, u_ml, ml_gn, riders_of("mlstm_fwd"))
    arrived("mlstm_fwd", got)
    (x1, xh1, rs1, x1b), got = _outproj_ln1(og_hg, og_ml, x, late["w_out"], ln1_g, ln1_b, riders_of("outproj_ln1"))
    arrived("outproj_ln1", got)
    (hgp, up, act), got = _ffn_up(x1b, late["w_ffn_gate"], late["w_ffn_up"], riders_of("ffn_up"))
    arrived("ffn_up", got)
    w_out_b, wg_b, wu_b, wd_b = late["w_out"], late["w_ffn_gate"], late["w_ffn_up"], late["w_ffn_down"]
    w_pp_b, w_pg_b = late["ple_w_proj"], late["ple_w_gate"]
    x2, xh2, rs2, x2b = _ffn_down_ln2(act, x1, wd_b, ln2_g, ln2_b)
    dr2, de, dz, loss, d_bpg, d_ln2g, d_ln2b = _head_loss_bwd(x2, xh2, rs2, p, tgt, w_pg_b, b_pg, w_pp_b, ln2_g)
    dr1, dhg, dup, d_ln1g, d_ln1b, dog_hg, dog_ml = _ffn_bwd(dr2, hgp, up, xh1, rs1, wd_b, wg_b, wu_b, ln1_g, w_out_b)

    d_wo_a, lo_wo_a = _wgrad(og_hg, dr1, "wgrad_out_hg", low=True)
    d_wo_b, lo_wo_b = _wgrad(og_ml, dr1, "wgrad_out_ml", low=True)
    d_wg, lo_wg = _wgrad(dhg, x1b, "wgrad_ffn_gate", tk=D_FF // 2, low=True)
    d_wu, lo_wu = _wgrad(dup, x1b, "wgrad_ffn_up", tk=D_FF // 2, low=True)
    d_wd, lo_wd = _wgrad(act, dr2, "wgrad_ffn_down", tk=D_FF // 2, low=True)
    d_wpp, lo_wpp = _wgrad(p, de, "wgrad_ple_proj", low=True)
    d_wpg, lo_wpg = _wgrad(x2b, dz, "wgrad_ple_gate", low=True)
    early = dict(w_out=jnp.concatenate([d_wo_a, d_wo_b], axis=0), w_ffn_gate=d_wg, w_ffn_up=d_wu, w_ffn_down=d_wd,
                 ple_w_proj=d_wpp, ple_w_gate=d_wpg)
    early_low = dict(w_out=jnp.concatenate([lo_wo_a, lo_wo_b], axis=0), w_ffn_gate=lo_wg, w_ffn_up=lo_wu, w_ffn_down=lo_wd,
                     ple_w_proj=lo_wpp, ple_w_gate=lo_wpg)
    ride_hg, ride_ml = early_hook(early_low) if early_hook is not None else ((), ())

    (du_hg, d_logits, d_hg_gn), got_hg = _hgrn2_bwd(u_hg, logits, hg_gn, sst, hg_b, hg_a, hg_o, dog_hg, ride_hg)
    (dqkc, dmv, dmo, dgt, d_ml_gn), got_ml = _mlstm_bwd(qkc, u_ml, ml_gn, cst, nst, mst, dog_ml, ride_ml)
    dqk, d_conv_w, d_conv_b = _conv_bwd(u_ml, conv_w, pre, dqkc)
    grad_x, du_ml = _inproj_bwd(dr1, du_hg, dqk, dmv, dmo, dgt, w_hg, w_ml)

    dw_hg, db_hg = _wgrad(du_hg, xb, "wgrad_in_hg", tk=U_HG // 2, colsum=True)
    dw_ml, db_ml = _wgrad(du_ml, xb, "wgrad_in_ml", colsum=True)
    d_w_in = jnp.concatenate([dw_hg, dw_ml[:PROJ_W - U_HG]], axis=0)
    d_b_in = jnp.concatenate([db_hg, db_ml[:, :PROJ_W - U_HG]], axis=1)

    grads = dict(w_in=d_w_in, b_in=d_b_in, hg_lb_logits=d_logits, ml_conv_w=d_conv_w, ml_conv_b=d_conv_b,
                 hg_norm_g=d_hg_gn, ml_norm_g=d_ml_gn, ln1_g=d_ln1g, ln1_b=d_ln1b, ln2_g=d_ln2g, ln2_b=d_ln2b,
                 ple_b_gate=d_bpg, **early)
    return loss, grad_x, grads, (list(got_hg), list(got_ml))


_ANY = pl.BlockSpec(memory_space=pltpu.HBM)
_MESH = pl.DeviceIdType.MESH


def _my_place():
    return lax.axis_index("x"), lax.axis_index("y"), lax.axis_index("c")


def _other_chips(x, y):
    return [(1 - x, y), (x, 1 - y), (1 - x, 1 - y)]


_VMEM = pl.BlockSpec(memory_space=pltpu.VMEM)
_EX_ROWS = 32


def _pair_reduce_cols(p, name):
    s, r, c = p.shape
    hc = c // 2

    def body(p_ref, o_ref, other, send_sem, recv_sem):
        x, y, cc = _my_place()

        def run(mine_lo, theirs_lo):
            cp = pltpu.make_async_remote_copy(src_ref=p_ref.at[pl.ds(0, s), pl.ds(0, r), pl.ds(theirs_lo, hc)], dst_ref=other,
                                              send_sem=send_sem, recv_sem=recv_sem, device_id=(x, y, 1 - cc), device_id_type=_MESH)
            cp.start()
            cp.wait()
            for slot in range(s):
                o_ref[slot] = (p_ref[slot, :, mine_lo:mine_lo + hc] + other[slot]).astype(bf16)

        @pl.when(cc == 0)
        def _():
            run(0, hc)

        @pl.when(cc == 1)
        def _():
            run(hc, 0)

    return pl.pallas_call(
        body, name=name, in_specs=[_VMEM], out_specs=_VMEM,
        out_shape=jax.ShapeDtypeStruct((s, r, hc), bf16),
        scratch_shapes=[pltpu.VMEM((s, r, hc), f32), pltpu.SemaphoreType.DMA, pltpu.SemaphoreType.DMA],
        compiler_params=pltpu.CompilerParams(vmem_limit_bytes=VMEM_LIMIT),
    )(p)


def _chip_reduce_swap_cols(rcv, name):
    s, r, hc = rcv.shape

    def body(r_ref, g_ref, send_sem, recv_sem):
        x, y, cc = _my_place()
        acc = r_ref[0].astype(f32)
        for slot in range(1, s):
            acc = acc + r_ref[slot].astype(f32)
        g_ref[cc] = acc
        cp = pltpu.make_async_remote_copy(src_ref=g_ref.at[cc], dst_ref=g_ref.at[cc], send_sem=send_sem, recv_sem=recv_sem,
                                          device_id=(x, y, 1 - cc), device_id_type=_MESH)
        cp.start()
        cp.wait()

    both = pl.pallas_call(
        body, name=name, in_specs=[_VMEM], out_specs=_VMEM,
        out_shape=jax.ShapeDtypeStruct((2, r, hc), f32),
        scratch_shapes=[pltpu.SemaphoreType.DMA, pltpu.SemaphoreType.DMA],
        compiler_params=pltpu.CompilerParams(vmem_limit_bytes=VMEM_LIMIT),
    )(rcv)
    return both.transpose(1, 0, 2).reshape(r, 2 * hc)


def _reduce_adamw(rcv, w, m, v, name):
    s, r, c = rcv.shape
    rows_per = _EX_ROWS

    def body(r_ref, w_ref, m_ref, v_ref, g_ref, d_ref, nm_ref, nv_ref, mine, theirs, send_sem, recv_sem):
        x, y, cc = _my_place()

        def chip_sum(i, carry):
            rs = pl.ds(pl.multiple_of(i * rows_per, rows_per), rows_per)
            acc = r_ref[0, rs, :].astype(f32)
            for slot in range(1, s):
                acc = acc + r_ref[slot, rs, :].astype(f32)
            mine[rs, :] = acc
            return carry

        lax.fori_loop(0, r // rows_per, chip_sum, 0)
        cp = pltpu.make_async_remote_copy(src_ref=mine, dst_ref=theirs, send_sem=send_sem, recv_sem=recv_sem,
                                          device_id=(x, y, 1 - cc), device_id_type=_MESH)
        cp.start()
        cp.wait()

        def update(i, carry):
            rs = pl.ds(pl.multiple_of(i * rows_per, rows_per), rows_per)
            g = mine[rs, :] + theirs[rs, :]
            nm = B1 * m_ref[rs, :] + (1.0 - B1) * g
            nv = B2 * v_ref[rs, :] + (1.0 - B2) * (g * g)
            g_ref[rs, :] = g
            nm_ref[rs, :] = nm
            nv_ref[rs, :] = nv
            d_ref[rs, :] = -LR * ((nm / (1.0 - B1 ** STEP)) / (jnp.sqrt(nv / (1.0 - B2 ** STEP)) + EPS_ADAM) + WD * w_ref[rs, :])
            return carry

        lax.fori_loop(0, r // rows_per, update, 0)

    return pl.pallas_call(
        body, name=name, in_specs=[_VMEM] * 4, out_specs=[_VMEM] * 4,
        out_shape=[jax.ShapeDtypeStruct((r, c), f32)] * 4,
        scratch_shapes=[pltpu.VMEM((r, c), f32), pltpu.VMEM((r, c), f32), pltpu.SemaphoreType.DMA, pltpu.SemaphoreType.DMA],
        compiler_params=pltpu.CompilerParams(vmem_limit_bytes=VMEM_LIMIT),
    )(rcv, w, m, v)


def _gather_copies(ins, outs, send_sems, recv_sems, local_sems):
    x, y, c = _my_place()
    me = 2 * x + y
    local, outgoing, incoming = [], [], []
    for a in range(len(ins)):
        local.append(pltpu.make_async_copy(ins[a], outs[a].at[me], local_sems.at[a]))
        for j, (px, py) in enumerate(_other_chips(x, y)):
            sems = dict(send_sem=send_sems.at[3 * a + j], recv_sem=recv_sems.at[3 * a + j], device_id=(px, py, c), device_id_type=_MESH)
            outgoing.append(pltpu.make_async_remote_copy(src_ref=ins[a], dst_ref=outs[a].at[me], **sems))
            incoming.append(pltpu.make_async_remote_copy(src_ref=ins[a], dst_ref=outs[a].at[2 * px + py], **sems))
    return local, outgoing, incoming


def _gather_first(block, taps, name):
    r, c = block.shape
    hc = c // 2

    def body(in_ref, tap_in, out_ref, tap_out, send_sems, recv_sems):
        x, y, cc = _my_place()
        me = 2 * x + y
        sibling = (x, y, 1 - cc)
        chips = _other_chips(x, y)
        out_ref[me] = in_ref[...]
        tap_out[me] = tap_in[...]

        def run(mine, theirs):
            def ici(j, chip):
                px, py = chips[j]
                src = in_ref.at[pl.ds(0, r), pl.ds(mine, hc)] if chip is None else out_ref.at[chip, pl.ds(0, r), pl.ds(mine, hc)]
                dst = out_ref.at[me if chip is None else chip, pl.ds(0, r), pl.ds(mine, hc)]
                return pltpu.make_async_remote_copy(src_ref=src, dst_ref=dst, send_sem=send_sems.at[j], recv_sem=recv_sems.at[j],
                                                    device_id=(px, py, cc), device_id_type=_MESH)

            def d2d(j, lo):
                px, py = chips[j]
                blk = out_ref.at[2 * px + py, pl.ds(0, r), pl.ds(lo, hc)]
                return pltpu.make_async_remote_copy(src_ref=blk, dst_ref=blk, send_sem=send_sems.at[3 + j], recv_sem=recv_sems.at[3 + j],
                                                    device_id=sibling, device_id_type=_MESH)

            def tap(j, chip):
                px, py = chips[j]
                return pltpu.make_async_remote_copy(src_ref=tap_in, dst_ref=tap_out.at[me if chip is None else chip],
                                                    send_sem=send_sems.at[6 + j], recv_sem=recv_sems.at[6 + j],
                                                    device_id=(px, py, cc), device_id_type=_MESH)

            for j in range(3):
                ici(j, None).start()
                tap(j, None).start()
            for j, (px, py) in enumerate(chips):
                ici(j, 2 * px + py).wait_recv()
                d2d(j, mine).start()
            for j, (px, py) in enumerate(chips):
                d2d(j, theirs).wait_recv()
                tap(j, 2 * px + py).wait_recv()
            for j in range(3):
                ici(j, None).wait_send()
                d2d(j, mine).wait_send()
                tap(j, None).wait_send()

        @pl.when(cc == 0)
        def _():
            run(0, hc)

        @pl.when(cc == 1)
        def _():
            run(hc, 0)

    return pl.pallas_call(
        body, name=name, in_specs=[_VMEM, _VMEM], out_specs=[_VMEM, _VMEM],
        out_shape=[jax.ShapeDtypeStruct((4, r, c), block.dtype), jax.ShapeDtypeStruct((4,) + taps.shape, taps.dtype)],
        scratch_shapes=[pltpu.SemaphoreType.DMA((9,)), pltpu.SemaphoreType.DMA((9,))],
        compiler_params=pltpu.CompilerParams(vmem_limit_bytes=VMEM_LIMIT),
    )(block, taps)


def _riding_call(body, name, nsteps, in_specs, out_specs, out_shape, scratch_shapes, operands, riders, copies, ride_shapes):
    nr, n_in, n_out, n_scr = len(riders), len(in_specs), len(out_specs), len(scratch_shapes)

    def wrapped(*refs):
        ins, ride_in = refs[:n_in], refs[n_in:n_in + nr]
        outs, ride_out = refs[n_in + nr:n_in + nr + n_out], refs[n_in + nr + n_out:n_in + 2 * nr + n_out]
        scratch, sems = refs[n_in + 2 * nr + n_out:n_in + 2 * nr + n_out + n_scr], refs[n_in + 2 * nr + n_out + n_scr:]
        if nr:
            @pl.when(pl.program_id(0) == 0)
            def _():
                local, outgoing, _ = copies(ride_in, ride_out, *sems)
                for cp in local + outgoing:
                    cp.start()

        body(*ins, *outs, *scratch)
        if nr:
            @pl.when(pl.program_id(0) == nsteps - 1)
            def _():
                local, outgoing, incoming = copies(ride_in, ride_out, *sems)
                for cp in incoming:
                    cp.wait_recv()
                for cp in outgoing:
                    cp.wait_send()
                for cp in local:
                    cp.wait()

    hbm = pl.BlockSpec(memory_space=pltpu.HBM)
    sems = [pltpu.SemaphoreType.DMA((3 * nr,)), pltpu.SemaphoreType.DMA((3 * nr,)), pltpu.SemaphoreType.DMA((nr,))] if nr else []
    res = pl.pallas_call(
        wrapped, name=name, grid=(nsteps,),
        in_specs=list(in_specs) + [hbm] * nr, out_specs=list(out_specs) + [hbm] * nr,
        out_shape=list(out_shape) + list(ride_shapes),
        scratch_shapes=list(scratch_shapes) + sems,
        compiler_params=_cparams(1),
    )(*operands, *riders)
    return list(res[:n_out]), list(res[n_out:])


def _gather_shapes(riders):
    return [jax.ShapeDtypeStruct((4,) + r.shape, r.dtype) for r in riders]


def _scatter_copies(ins, outs, send_sems, recv_sems, local_sems):
    x, y, c = _my_place()
    me = 2 * x + y
    local, outgoing, incoming = [], [], []
    for a in range(len(ins)):
        local.append(pltpu.make_async_copy(ins[a].at[me], outs[a].at[me], local_sems.at[a]))
        for j, (px, py) in enumerate(_other_chips(x, y)):
            sems = dict(send_sem=send_sems.at[3 * a + j], recv_sem=recv_sems.at[3 * a + j], device_id=(px, py, c), device_id_type=_MESH)
            outgoing.append(pltpu.make_async_remote_copy(src_ref=ins[a].at[2 * px + py], dst_ref=outs[a].at[me], **sems))
            incoming.append(pltpu.make_async_remote_copy(src_ref=ins[a].at[2 * px + py], dst_ref=outs[a].at[2 * px + py], **sems))
    return local, outgoing, incoming


def _scatter_chips(pieces, name):
    n = len(pieces)

    def body(*refs):
        local, outgoing, incoming = _scatter_copies(refs[:n], refs[n:2 * n], *refs[2 * n:])
        for cp in local + outgoing:
            cp.start()
        for cp in incoming:
            cp.wait_recv()
        for cp in outgoing:
            cp.wait_send()
        for cp in local:
            cp.wait()

    return pl.pallas_call(
        body, name=name,
        in_specs=[_ANY] * n, out_specs=[_ANY] * n,
        out_shape=[jax.ShapeDtypeStruct(s.shape, s.dtype) for s in pieces],
        scratch_shapes=[pltpu.SemaphoreType.DMA((3 * n,)), pltpu.SemaphoreType.DMA((3 * n,)), pltpu.SemaphoreType.DMA((n,))],
    )(*pieces)


def _gather_all(block, name):
    def body(in_ref, out_ref, send_sems, recv_sems, local_sem):
        x, y, c = _my_place()
        me = 4 * x + 2 * y + c
        cp = pltpu.make_async_copy(in_ref, out_ref.at[me], local_sem)
        cp.start()
        peers = []
        for dx in range(2):
            for dy in range(2):
                for dc in range(2):
                    if dx or dy or dc:
                        peers.append((1 - x if dx else x, 1 - y if dy else y, 1 - c if dc else c))
        for j, pr in enumerate(peers):
            pltpu.make_async_remote_copy(src_ref=in_ref, dst_ref=out_ref.at[me], send_sem=send_sems.at[j], recv_sem=recv_sems.at[j],
                                         device_id=pr, device_id_type=_MESH).start()
        for j, (px, py, pc) in enumerate(peers):
            pltpu.make_async_remote_copy(src_ref=in_ref, dst_ref=out_ref.at[4 * px + 2 * py + pc], send_sem=send_sems.at[j], recv_sem=recv_sems.at[j],
                                         device_id=(px, py, pc), device_id_type=_MESH).wait()
        cp.wait()

    return pl.pallas_call(
        body, name=name,
        in_specs=[_ANY], out_specs=_ANY,
        out_shape=jax.ShapeDtypeStruct((8,) + block.shape, block.dtype),
        scratch_shapes=[pltpu.SemaphoreType.DMA((7,)), pltpu.SemaphoreType.DMA((7,)), pltpu.SemaphoreType.DMA],
    )(block)


def _row_tile(r, c):
    best = r
    for cand in range(16, r + 1, 16):
        if r % cand == 0 and cand * c * 4 <= (1 << 20):
            best = cand
    return best if best * c * 4 <= (4 << 20) else r


def _sum_slots(parts, name):
    n, r, c = parts.shape
    tr = _row_tile(r, c)

    def body(p_ref, o_ref):
        acc = p_ref[0].astype(f32)
        for s in range(1, n):
            acc = acc + p_ref[s].astype(f32)
        o_ref[...] = acc

    return pl.pallas_call(
        body, name=name, grid=(r // tr,),
        in_specs=[pl.BlockSpec((n, tr, c), lambda i: (0, i, 0))],
        out_specs=pl.BlockSpec((tr, c), lambda i: (i, 0)),
        out_shape=jax.ShapeDtypeStruct((r, c), f32),
        compiler_params=_cparams(1, arbitrary=False),
    )(parts)


def _adamw(parts, w, m, v, name):
    n, r, c = parts.shape
    tr = _row_tile(r, c)
    tc = c
    if tr == r and r * c * 4 > (1 << 20) and c % 256 == 0:
        tc = 256

    def body(p_ref, w_ref, m_ref, v_ref, g_ref, d_ref, nm_ref, nv_ref):
        g = p_ref[0]
        for s in range(1, n):
            g = g + p_ref[s]
        nm = B1 * m_ref[...] + (1.0 - B1) * g
        nv = B2 * v_ref[...] + (1.0 - B2) * (g * g)
        m_hat = nm / (1.0 - B1 ** STEP)
        v_hat = nv / (1.0 - B2 ** STEP)
        g_ref[...] = g
        nm_ref[...] = nm
        nv_ref[...] = nv
        d_ref[...] = -LR * (m_hat / (jnp.sqrt(v_hat) + EPS_ADAM) + WD * w_ref[...])

    blk = pl.BlockSpec((tr, tc), lambda i, j: (i, j))
    return pl.pallas_call(
        body, name=name, grid=(r // tr, c // tc),
        in_specs=[pl.BlockSpec((n, tr, tc), lambda i, j: (0, i, j)), blk, blk, blk],
        out_specs=[blk] * 4,
        out_shape=[jax.ShapeDtypeStruct((r, c), f32)] * 4,
        compiler_params=_cparams(2, arbitrary=False),
    )(parts, w, m, v)


_BIG = ["w_in", "w_out", "w_ffn_gate", "w_ffn_up", "w_ffn_down", "ple_w_proj", "ple_w_gate"]
_SMALL = ["b_in", "hg_lb_logits", "ml_conv_w", "ml_conv_b", "hg_norm_g", "ml_norm_g", "ln1_g", "ln1_b", "ln2_g", "ln2_b", "ple_b_gate"]
_ORDER = ["w_in", "b_in", "hg_lb_logits", "ml_conv_w", "ml_conv_b", "hg_norm_g", "ml_norm_g", "w_out", "ln1_g", "ln1_b",
          "w_ffn_gate", "w_ffn_up", "w_ffn_down", "ln2_g", "ln2_b", "ple_w_proj", "ple_w_gate", "ple_b_gate"]
_PACK_ROWS, _PACK_COLS = 16, 1024


def _pack(arrays):
    flat = jnp.concatenate([a.reshape(-1) for a in arrays])
    return jnp.pad(flat, (0, _PACK_ROWS * _PACK_COLS - flat.shape[0])).reshape(_PACK_ROWS, _PACK_COLS)


def _unpack(pack, shapes):
    flat = pack.reshape(-1)
    out, off = [], 0
    for s in shapes:
        size = 1
        for d in s:
            size *= d
        out.append(flat[off:off + size].reshape(s))
        off += size
    return out


def _to_chip_major(g, col_split):
    if col_split:
        k, n = g.shape
        return g.reshape(k, 4, n // 4).transpose(1, 0, 2)
    k, n = g.shape
    return g.reshape(4, k // 4, n)


def kernel(x, p, w_in, b_in, hg_lb_logits, ml_conv_w, ml_conv_b, hg_norm_g, ml_norm_g, w_out, ln1_g, ln1_b, w_ffn_gate, w_ffn_up, w_ffn_down, ln2_g, ln2_b, ple_w_proj, ple_w_gate, ple_b_gate, loss_target, m_w_in, m_b_in, m_hg_lb_logits, m_ml_conv_w, m_ml_conv_b, m_hg_norm_g, m_ml_norm_g, m_w_out, m_ln1_g, m_ln1_b, m_w_ffn_gate, m_w_ffn_up, m_w_ffn_down, m_ln2_g, m_ln2_b, m_ple_w_proj, m_ple_w_gate, m_ple_b_gate, v_w_in, v_b_in, v_hg_lb_logits, v_ml_conv_w, v_ml_conv_b, v_hg_norm_g, v_ml_norm_g, v_w_out, v_ln1_g, v_ln1_b, v_w_ffn_gate, v_w_ffn_up, v_w_ffn_down, v_ln2_g, v_ln2_b, v_ple_w_proj, v_ple_w_gate, v_ple_b_gate):
    args = dict(locals())
    wts = {k: args[k] for k in _ORDER}
    mom = {k: args["m_" + k] for k in _ORDER}
    var = {k: args["v_" + k] for k in _ORDER}
    two_d = lambda a: a.reshape(a.shape[-2], a.shape[-1])
    block = lambda k, a: jnp.swapaxes(two_d(a), 0, 1) if k in _TRANSPOSED else two_d(a)
    unblock = lambda k, a: (jnp.swapaxes(a, 0, 1) if k in _TRANSPOSED else a).reshape(wts[k].shape)

    shards = {k: block(k, wts[k]).astype(bf16) for k in _BIG}
    w_in_blocks, taps = _gather_first(shards["w_in"], two_d(ml_conv_w), "gather_w_in")
    w_in_full = _from_chip_major(w_in_blocks, False)
    conv_w_full = _from_chip_major(taps, True)

    early_keys = _BIG[1:]
    loss, grad_x, grads, (got_hg, got_ml) = _local_step(
        x[0], p[0, 0], loss_target[0], w_in_full, b_in, hg_lb_logits, conv_w_full, ml_conv_b, hg_norm_g, ml_norm_g,
        None, ln1_g, ln1_b, None, None, None, ln2_g, ln2_b, None, None, ple_b_gate,
        early_hook=lambda low: tuple([_to_chip_major(low[k], k in _COL_SPLIT) for k in names] for names in _SCATTER_PLAN),
        late_shards={k: shards[k] for k in early_keys})

    out_g, out_d, out_m, out_v = {}, {}, {}, {}

    def finish(k, g, d, nm, nv):
        out_g[k], out_d[k], out_m[k], out_v[k] = unblock(k, g), unblock(k, d), unblock(k, nm), unblock(k, nv)

    for names, got in zip(_SCATTER_PLAN, (got_hg, got_ml)):
        for k, rcv in zip(names, got):
            finish(k, *_reduce_adamw(rcv, block(k, wts[k]), block(k, mom[k]), block(k, var[k]), "reduce_adamw_" + k))

    core_sums = _pair_reduce_cols(_to_chip_major(grads["w_in"], False), "pair_reduce_w_in")
    whole = _chip_reduce_swap_cols(_scatter_chips([core_sums], "scatter_grad_w_in")[0], "chip_reduce_w_in")
    finish("w_in", *_adamw(whole[None], block("w_in", wts["w_in"]), block("w_in", mom["w_in"]), block("w_in", var["w_in"]), "adamw_w_in"))

    small_shapes = [(1, PROJ_W), (2, MIX_W), (CONV_K, MIX_W)] + [(1, MIX_W)] * 3 + [(1, D_MODEL)] * 5 + [(1, 1)]
    contrib = _pack([grads[k] for k in _SMALL] + [loss])
    summed = _sum_slots(_gather_all(contrib, "gather_small"), "sum_small")
    small = _unpack(summed, small_shapes)
    loss_total = small[-1].reshape(())
    gsm = dict(zip(_SMALL, small[:-1]))
    place = 2 * lax.axis_index("x") + lax.axis_index("y")
    conv_cols = ml_conv_w.shape[-1]
    gsm["ml_conv_w"] = lax.dynamic_slice(gsm["ml_conv_w"], (0, place * conv_cols), (CONV_K, conv_cols))
    own_shapes = [wts[k].shape for k in _SMALL]
    g_pack = _pack([gsm[k] for k in _SMALL])
    res = _adamw(g_pack[None], _pack([wts[k] for k in _SMALL]), _pack([mom[k] for k in _SMALL]), _pack([var[k] for k in _SMALL]), "adamw_small")
    for dst, pack in zip((out_g, out_d, out_m, out_v), res):
        for k, a in zip(_SMALL, _unpack(pack, own_shapes)):
            dst[k] = a

    outs = [loss_total, grad_x[None]]
    for group in (out_g, out_d, out_m, out_v):
        outs += [group[k] for k in _ORDER]
    return tuple(outs)
```

```python
import jax
import jax.numpy as jnp
from jax import lax
from jax.experimental import pallas as pl
from jax.experimental.pallas import tpu as pltpu

f32 = jnp.float32
bf16 = jnp.bfloat16

D_MODEL = 1024
HEADS = 4
HEAD_W = 128
MIX_W = HEADS * HEAD_W
ML_DQK = 64
PROJ_W = 3592
U_HG = 4 * MIX_W
U_ML = 3 * MIX_W + 128
D_FF = 2816
PLE = 256
CHUNK = 128
SUB = 16
EXP_CAP = 80.0
CONV_K = 4
HALO = 8
ALPHA = float(2.0 ** 0.25)
LN_EPS = 1e-5
RMS_EPS = 1e-6
NEG = -1e30
LR, B1, B2, EPS_ADAM, WD, STEP = 0.001, 0.9, 0.999, 1e-08, 0.01, 10
VMEM_LIMIT = 56 * 1024 * 1024
MIXER_ROWS = 512
DENSE_ROWS = 512
WGRAD_ROWS = 2048


def _cparams(n_axes, arbitrary=True):
    sem = ("arbitrary",) * n_axes if arbitrary else ("parallel",) * n_axes
    return pltpu.CompilerParams(dimension_semantics=sem, vmem_limit_bytes=VMEM_LIMIT)


ACT = bf16


def _mx(a):
    return a.astype(ACT)


def _bdot(a, b):
    return jnp.dot(_mx(a), _mx(b), preferred_element_type=f32)


def _bdot_nt(a, b):
    return lax.dot_general(_mx(a), _mx(b), (((1,), (1,)), ((), ())), preferred_element_type=f32)


def _bdot_tn(a, b):
    return lax.dot_general(_mx(a), _mx(b), (((0,), (0,)), ((), ())), preferred_element_type=f32)


def _split3(x):
    hi = x.astype(bf16)
    r1 = x - hi.astype(f32)
    mid = r1.astype(bf16)
    lo = (r1 - mid.astype(f32)).astype(bf16)
    return hi, mid, lo


def _dot3(a, b, dims):
    a_hi = a.astype(bf16)
    a_lo = (a - a_hi.astype(f32)).astype(bf16)
    b_hi = b.astype(bf16)
    b_lo = (b - b_hi.astype(f32)).astype(bf16)
    dn = (dims, ((), ()))
    return (lax.dot_general(a_hi, b_hi, dn, preferred_element_type=f32) + lax.dot_general(a_hi, b_lo, dn, preferred_element_type=f32)
            + lax.dot_general(a_lo, b_hi, dn, preferred_element_type=f32))


def _lane_sum(x):
    hi = x.astype(bf16)
    lo = (x - hi.astype(f32)).astype(bf16)
    ones = jnp.ones((x.shape[1], 128), bf16)
    return jnp.dot(hi, ones, preferred_element_type=f32) + jnp.dot(lo, ones, preferred_element_type=f32)


def _lane_dot(x, row):
    return _dot3(x, jnp.broadcast_to(row, (128, row.shape[1])), ((1,), (1,)))


def _sel_dot(sel, x):
    sb = sel.astype(bf16)
    return sum(jnp.dot(sb, part, preferred_element_type=f32) for part in _split3(x))


def _sel_dot_nt(sel, x):
    sb = sel.astype(bf16)
    return sum(lax.dot_general(sb, part, (((1,), (1,)), ((), ())), preferred_element_type=f32) for part in _split3(x))


def _sigmoid(x):
    return 1.0 / (1.0 + jnp.exp(-x))


def _log_sigmoid(x):
    return jnp.minimum(x, 0.0) - jnp.log(1.0 + jnp.exp(-jnp.abs(x)))


def _tri(n, upper=False):
    r = lax.broadcasted_iota(jnp.int32, (n, n), 0)
    c = lax.broadcasted_iota(jnp.int32, (n, n), 1)
    return (c >= r) if upper else (c <= r)


def _rows(tm, n, col=0):
    return pl.BlockSpec((tm, n), lambda i, _c=col: (i, _c))


def _rows_rev(tm, n, nb, col=0):
    return pl.BlockSpec((tm, n), lambda i, _c=col, _nb=nb: (_nb - 1 - i, _c))


def _const(shape):
    return pl.BlockSpec(shape, lambda i, _n=len(shape): (0,) * _n)


def _resident(shape):
    return pl.BlockSpec(shape, lambda i, _n=len(shape): (0,) * _n, pipeline_mode=pl.Buffered(1))


def _tile(t, want):
    return want if t % want == 0 else t


def _inproj(x, w_hg, w_ml, b_hg, b_ml, riders=()):
    t = x.shape[0]
    tm = _tile(t, DENSE_ROWS)

    def body(x_ref, whg_ref, wml_ref, bhg_ref, bml_ref, uhg_ref, uml_ref, xb_ref):
        xb = _mx(x_ref[...])
        xb_ref[...] = xb
        uhg_ref[...] = _bdot_nt(xb, whg_ref[...]) + bhg_ref[...]
        uml_ref[...] = _bdot_nt(xb, wml_ref[...]) + bml_ref[...]

    return _riding_call(
        body, "inproj", t // tm,
        in_specs=[_rows(tm, D_MODEL), _resident((U_HG, D_MODEL)), _resident((U_ML, D_MODEL)), _const((1, U_HG)), _const((1, U_ML))],
        out_specs=[_rows(tm, U_HG), _rows(tm, U_ML), _rows(tm, D_MODEL)],
        out_shape=[jax.ShapeDtypeStruct((t, U_HG), f32), jax.ShapeDtypeStruct((t, U_ML), f32), jax.ShapeDtypeStruct((t, D_MODEL), ACT)],
        scratch_shapes=[], operands=(x, w_hg, w_ml, b_hg, b_ml), riders=riders, copies=_gather_copies, ride_shapes=_gather_shapes(riders))


def _hg_gates(hq, hf, lb, tri, b=None):
    s = _sigmoid(hf)
    om = 1.0 - lb
    f = lb + om * s
    k = om * (1.0 - s)
    sq = _sigmoid(hq)
    q = hq * sq
    if b is None:
        b = _sel_dot(tri, jnp.log(f))
    return q, sq, s, f, k, b


def _hg_scores(q, k, b, tril_mask, a=None):
    qts, kts, eqs, eks, rows = [], [], [], [], []
    for i in range(CHUNK // SUB):
        lo = i * SUB
        ref = jnp.zeros_like(b[0:1]) if i == 0 else b[lo - 1:lo]
        eq = jnp.exp(b[lo:lo + SUB] - ref)
        ek = jnp.exp(jnp.minimum(ref - b, EXP_CAP))
        qt = q[lo:lo + SUB] * eq
        kt = k * ek
        if a is None:
            rows.append(_bdot_nt(qt, kt))
        qts.append(qt); kts.append(kt); eqs.append(eq); eks.append(ek)
    if a is None:
        a = jnp.where(tril_mask, jnp.concatenate(rows, axis=0), 0.0)
    return a, qts, kts, eqs, eks


def _head_rms(o, gn, on_mxu=False):
    ms = _lane_sum(o * o) * (1.0 / o.shape[1]) if on_mxu else jnp.mean(o * o, axis=-1, keepdims=True)
    rstd = lax.rsqrt(ms + RMS_EPS)
    oh = o * rstd
    return oh, rstd, oh * gn


def _lower_bound(logit_ref):
    lg = logit_ref[...]
    return _sigmoid(lg[0:1] - lg[1:2])


def _hgrn2_fwd(u_hg, logits, gn, riders=()):
    t = u_hg.shape[0]
    tb = _tile(t, MIXER_ROWS)
    nc_blk = tb // CHUNK

    def body(u_ref, lg_ref, gn_ref, og_ref, sst_ref, b_ref, a_ref, o_ref, st_ref):
        @pl.when(pl.program_id(0) == 0)
        def _():
            st_ref[...] = jnp.zeros_like(st_ref)

        lb_all = _lower_bound(lg_ref)
        tril_mask = _tri(CHUNK)
        tri = tril_mask.astype(f32)

        def chunk(c, carry):
            r0 = pl.multiple_of(c * CHUNK, CHUNK)
            rows = pl.ds(r0, CHUNK)
            heads = range(HEADS)
            cols = [slice(h * HEAD_W, (h + 1) * HEAD_W) for h in heads]
            hv = [u_ref[rows, 2 * MIX_W + h * HEAD_W:2 * MIX_W + (h + 1) * HEAD_W] for h in heads]
            gts = [_hg_gates(u_ref[rows, h * HEAD_W:(h + 1) * HEAD_W], u_ref[rows, MIX_W + h * HEAD_W:MIX_W + (h + 1) * HEAD_W],
                             lb_all[:, cols[h]], tri) for h in heads]
            q = [g[0] for g in gts]
            k = [g[4] for g in gts]
            b = [g[5] for g in gts]
            a = [_hg_scores(q[h], k[h], b[h], tril_mask)[0] for h in heads]
            st = [st_ref[h] for h in heads]
            bl = [b[h][CHUNK - 1:CHUNK] for h in heads]
            o = [_bdot(a[h], hv[h]) + _bdot_nt(q[h] * jnp.exp(b[h]), st[h]) for h in heads]
            new_st = [st[h] * jnp.exp(bl[h]) + _bdot_tn(hv[h], k[h] * jnp.exp(bl[h] - b[h])) for h in heads]
            for h in heads:
                sst_ref[c, h] = st[h]
                st_ref[h] = new_st[h]
                b_ref[rows, cols[h]] = b[h]
                a_ref[rows, cols[h]] = a[h].astype(ACT)
                o_ref[rows, cols[h]] = o[h]
                hgate = u_ref[rows, 3 * MIX_W + h * HEAD_W:3 * MIX_W + (h + 1) * HEAD_W]
                _, _, y = _head_rms(o[h], gn_ref[:, cols[h]])
                og_ref[rows, cols[h]] = (y * (hgate * _sigmoid(hgate))).astype(ACT)
            return carry

        lax.fori_loop(0, nc_blk, chunk, 0, unroll=True)

    assert CHUNK == HEAD_W
    return _riding_call(
        body, "hgrn2_fwd", t // tb,
        in_specs=[_rows(tb, U_HG), _const((2, MIX_W)), _const((1, MIX_W))],
        out_specs=[_rows(tb, MIX_W), pl.BlockSpec((nc_blk, HEADS, HEAD_W, HEAD_W), lambda i: (i, 0, 0, 0)),
                   _rows(tb, MIX_W), _rows(tb, MIX_W), _rows(tb, MIX_W)],
        out_shape=[jax.ShapeDtypeStruct((t, MIX_W), ACT), jax.ShapeDtypeStruct((t // CHUNK, HEADS, HEAD_W, HEAD_W), f32),
                   jax.ShapeDtypeStruct((t, MIX_W), f32), jax.ShapeDtypeStruct((t, MIX_W), ACT), jax.ShapeDtypeStruct((t, MIX_W), f32)],
        scratch_shapes=[pltpu.VMEM((HEADS, HEAD_W, HEAD_W), f32)],
        operands=(u_hg, logits, gn), riders=riders, copies=_gather_copies, ride_shapes=_gather_shapes(riders))


def _hgrn2_bwd(u_hg, logits, gn, sst, bcum, scores, o_raw, dog, riders=()):
    t = u_hg.shape[0]
    tb = _tile(t, MIXER_ROWS)
    nb = t // tb
    nc_blk = tb // CHUNK

    def body(u_ref, lg_ref, gn_ref, sst_ref, b_ref, a_ref, o_ref, dog_ref, du_ref, dlg_ref, dgn_ref, dst_ref):
        @pl.when(pl.program_id(0) == 0)
        def _():
            dst_ref[...] = jnp.zeros_like(dst_ref)
            dlg_ref[...] = jnp.zeros_like(dlg_ref)
            dgn_ref[...] = jnp.zeros_like(dgn_ref)

        lb_all = _lower_bound(lg_ref)
        tril_mask = _tri(CHUNK)
        tri = tril_mask.astype(f32)
        triu = _tri(CHUNK, upper=True).astype(f32)

        def chunk(j, carry):
            c = nc_blk - 1 - j
            r0 = pl.multiple_of(c * CHUNK, CHUNK)
            rows = pl.ds(r0, CHUNK)
            heads = range(HEADS)
            nsub = CHUNK // SUB
            cols = [slice(h * HEAD_W, (h + 1) * HEAD_W) for h in heads]
            hq = [u_ref[rows, h * HEAD_W:(h + 1) * HEAD_W] for h in heads]
            hf = [u_ref[rows, MIX_W + h * HEAD_W:MIX_W + (h + 1) * HEAD_W] for h in heads]
            hv = [u_ref[rows, 2 * MIX_W + h * HEAD_W:2 * MIX_W + (h + 1) * HEAD_W] for h in heads]
            lb = [lb_all[:, cols[h]] for h in heads]
            gts = [_hg_gates(hq[h], hf[h], lb[h], tri, b=b_ref[rows, cols[h]]) for h in heads]
            q, sq, s, f, k, b = ([g[n] for g in gts] for n in range(6))
            scs = [_hg_scores(q[h], k[h], b[h], tril_mask, a=a_ref[rows, cols[h]]) for h in heads]
            a, qts, kts, eqs, eks = ([sc[n] for sc in scs] for n in range(5))
            st = [sst_ref[c, h] for h in heads]
            dst = [dst_ref[h] for h in heads]
            bl = [b[h][CHUNK - 1:CHUNK] for h in heads]
            eb = [jnp.exp(b[h]) for h in heads]
            qh = [q[h] * eb[h] for h in heads]
            ekl = [jnp.exp(bl[h] - b[h]) for h in heads]
            kh = [k[h] * ekl[h] for h in heads]
            o = [o_ref[rows, cols[h]] for h in heads]
            do = []
            for h in heads:
                hgate = u_ref[rows, 3 * MIX_W + h * HEAD_W:3 * MIX_W + (h + 1) * HEAD_W]
                gnh = gn_ref[:, cols[h]]
                oh, rstd, y = _head_rms(o[h], gnh)
                sg = _sigmoid(hgate)
                dogh = dog_ref[rows, cols[h]]
                dy = dogh * (hgate * sg)
                du_ref[rows, 3 * MIX_W + h * HEAD_W:3 * MIX_W + (h + 1) * HEAD_W] = (dogh * y * (sg * (1.0 + hgate * (1.0 - sg)))).astype(ACT)
                dgn_ref[:, cols[h]] += jnp.sum(dy * oh, axis=0, keepdims=True)
                doh = dy * gnh
                do.append(rstd * (doh - oh * jnp.mean(doh * oh, axis=-1, keepdims=True)))
            da = [jnp.where(tril_mask, _bdot_nt(do[h], hv[h]), 0.0) for h in heads]
            dv = [_bdot_tn(a[h], do[h]) + _bdot_nt(kh[h], dst[h]) for h in heads]
            dq = [_bdot(do[h], st[h]) * eb[h] for h in heads]
            dk = [_bdot(hv[h], dst[h]) * ekl[h] for h in heads]
            d_last = [jnp.sum(k[h] * dk[h], axis=0, keepdims=True) + jnp.exp(bl[h]) * jnp.sum(dst[h] * st[h], axis=0, keepdims=True)
                      for h in heads]
            d_b = [q[h] * dq[h] - k[h] * dk[h] for h in heads]
            dqs = [[] for _ in heads]
            q_dq = [[] for _ in heads]
            for i in range(nsub):
                for h in heads:
                    da_i = _mx(da[h][i * SUB:(i + 1) * SUB])
                    q_r, k_r = _mx(qts[h][i]), _mx(kts[h][i])
                    g_q = jnp.dot(da_i, k_r, preferred_element_type=f32)
                    g_k = lax.dot_general(da_i, q_r, (((0,), (0,)), ((), ())), preferred_element_type=f32)
                    dqs[h].append(g_q * eqs[h][i])
                    q_dq[h].append(q_r.astype(f32) * g_q)
                    dk[h] = dk[h] + g_k * eks[h][i]
                    d_b[h] = d_b[h] - k_r.astype(f32) * g_k
            for h in heads:
                dq[h] = dq[h] + jnp.concatenate(dqs[h], axis=0)
                d_b[h] = d_b[h] + jnp.concatenate(q_dq[h], axis=0)
                dst_ref[h] = dst[h] * jnp.exp(bl[h]) + _bdot_tn(do[h], qh[h])
            dg = [_sel_dot(triu, d_b[h]) + d_last[h] for h in heads]
            for h in heads:
                dfk = dg[h] / f[h] - dk[h]
                du_ref[rows, h * HEAD_W:(h + 1) * HEAD_W] = (dq[h] * (sq[h] * (1.0 + hq[h] * (1.0 - sq[h])))).astype(ACT)
                du_ref[rows, MIX_W + h * HEAD_W:MIX_W + (h + 1) * HEAD_W] = ((1.0 - lb[h]) * dfk * s[h] * (1.0 - s[h])).astype(ACT)
                du_ref[rows, 2 * MIX_W + h * HEAD_W:2 * MIX_W + (h + 1) * HEAD_W] = dv[h].astype(ACT)
                dlb = jnp.sum((1.0 - s[h]) * dfk, axis=0, keepdims=True) * (lb[h] * (1.0 - lb[h]))
                dlg_ref[0:1, cols[h]] += dlb
                dlg_ref[1:2, cols[h]] -= dlb
            return carry

        lax.fori_loop(0, nc_blk, chunk, 0, unroll=True)

    rev = _rows_rev(tb, MIX_W, nb)
    return _riding_call(
        body, "hgrn2_bwd", nb,
        in_specs=[_rows_rev(tb, U_HG, nb), _const((2, MIX_W)), _const((1, MIX_W)),
                  pl.BlockSpec((nc_blk, HEADS, HEAD_W, HEAD_W), lambda i: (nb - 1 - i, 0, 0, 0)), rev, rev, rev, rev],
        out_specs=[_rows_rev(tb, U_HG, nb), _const((2, MIX_W)), _const((1, MIX_W))],
        out_shape=[jax.ShapeDtypeStruct((t, U_HG), ACT), jax.ShapeDtypeStruct((2, MIX_W), f32), jax.ShapeDtypeStruct((1, MIX_W), f32)],
        scratch_shapes=[pltpu.VMEM((HEADS, HEAD_W, HEAD_W), f32)],
        operands=(u_hg, logits, gn, sst, bcum, scores, o_raw, dog), riders=riders, copies=_scatter_copies,
        ride_shapes=[jax.ShapeDtypeStruct(r.shape, r.dtype) for r in riders])


def _conv_fwd(u_ml, w, b):
    t = u_ml.shape[0]
    tm = _tile(t, 512)

    def body(x_ref, w_ref, b_ref, pre_ref, act_ref, xbuf):
        @pl.when(pl.program_id(0) == 0)
        def _():
            xbuf[...] = jnp.zeros_like(xbuf)

        xbuf[0:HALO, :] = xbuf[tm:tm + HALO, :]
        xbuf[HALO:HALO + tm, :] = x_ref[...]
        pre = b_ref[...] + jnp.zeros((tm, MIX_W), f32)
        for kk in range(CONV_K):
            off = HALO - (CONV_K - 1) + kk
            pre = pre + w_ref[kk:kk + 1, :] * xbuf[off:off + tm, :]
        pre_ref[...] = pre
        act_ref[...] = pre * _sigmoid(pre)

    return pl.pallas_call(
        body, name="conv_fwd", grid=(t // tm,),
        in_specs=[_rows(tm, MIX_W), _const((CONV_K, MIX_W)), _const((1, MIX_W))],
        out_specs=[_rows(tm, MIX_W), _rows(tm, MIX_W)],
        out_shape=[jax.ShapeDtypeStruct((t, MIX_W), f32)] * 2,
        scratch_shapes=[pltpu.VMEM((tm + HALO, MIX_W), f32)],
        compiler_params=_cparams(1),
    )(u_ml, w, b)


def _conv_bwd(u_ml, w, pre, dact):
    t = u_ml.shape[0]
    tm = _tile(t, 512)
    nb = t // tm
    hb = tm // HALO

    def body(x_ref, halo_ref, w_ref, pre_ref, dact_ref, dx_ref, dw_ref, db_ref, dbuf, xbuf):
        i = pl.program_id(0)

        @pl.when(i == 0)
        def _():
            dbuf[...] = jnp.zeros_like(dbuf)
            dw_ref[...] = jnp.zeros_like(dw_ref)
            db_ref[...] = jnp.zeros_like(db_ref)

        p = pre_ref[...]
        sg = _sigmoid(p)
        dpre = dact_ref[...] * (sg * (1.0 + p * (1.0 - sg)))
        dbuf[tm:tm + HALO, :] = dbuf[0:HALO, :]
        dbuf[0:tm, :] = dpre
        has_prev = (i < nb - 1).astype(f32)
        xbuf[0:HALO, :] = halo_ref[...] * has_prev
        xbuf[HALO:HALO + tm, :] = x_ref[...]
        dx = jnp.zeros((tm, MIX_W), f32)
        for kk in range(CONV_K):
            back = CONV_K - 1 - kk
            dx = dx + w_ref[kk:kk + 1, :] * dbuf[back:back + tm, :]
            off = HALO - (CONV_K - 1) + kk
            dw_ref[kk:kk + 1, :] += jnp.sum(dpre * xbuf[off:off + tm, :], axis=0, keepdims=True)
        dx_ref[...] = dx.astype(ACT)
        db_ref[...] += jnp.sum(dpre, axis=0, keepdims=True)

    return pl.pallas_call(
        body, name="conv_bwd", grid=(nb,),
        in_specs=[_rows_rev(tm, MIX_W, nb),
                  pl.BlockSpec((HALO, MIX_W), lambda i: (jnp.maximum((nb - 1 - i) * hb - 1, 0), 0)),
                  _const((CONV_K, MIX_W)), _rows_rev(tm, MIX_W, nb), _rows_rev(tm, MIX_W, nb)],
        out_specs=[_rows_rev(tm, MIX_W, nb), _const((CONV_K, MIX_W)), _const((1, MIX_W))],
        out_shape=[jax.ShapeDtypeStruct((t, MIX_W), ACT), jax.ShapeDtypeStruct((CONV_K, MIX_W), f32), jax.ShapeDtypeStruct((1, MIX_W), f32)],
        scratch_shapes=[pltpu.VMEM((tm + HALO, MIX_W), f32), pltpu.VMEM((tm + HALO, MIX_W), f32)],
        compiler_params=_cparams(1),
    )(u_ml, u_ml, w, pre, dact)


def _lane_pick(x, lane):
    idx = lax.broadcasted_iota(jnp.int32, x.shape, 1)
    return jnp.sum(jnp.where(idx == lane, x, 0.0), axis=-1, keepdims=True)


def _ml_gate_forms(gates, tri):
    lf = _log_sigmoid(gates)
    gc = _sel_dot(tri, lf)
    lane = lax.broadcasted_iota(jnp.int32, gates.shape, 1)
    mixed = jnp.where(lane < HEADS, gates, gc)
    sel = (lax.broadcasted_iota(jnp.int32, (8, 128), 0) == lax.broadcasted_iota(jnp.int32, (8, 128), 1)).astype(f32)
    rowsf = _sel_dot_nt(sel, mixed)
    return gc, rowsf


def _ml_chunk(q, k, v, gates, gc, rowsf, c_st, n_st, m_st, tril_mask):
    hs = range(HEADS)
    g_col = [_lane_pick(gc, HEADS + h) for h in hs]
    ig_col = [_lane_pick(gates, h) for h in hs]
    dmat = [jnp.where(tril_mask, g_col[h] - rowsf[HEADS + h:HEADS + h + 1, :] + rowsf[h:h + 1, :], NEG) for h in hs]
    m_inter = [g_col[h] + m_st[h] for h in hs]
    m_t = [jnp.maximum(m_inter[h], jnp.max(dmat[h], axis=-1, keepdims=True)) for h in hs]
    wi = [jnp.exp(dmat[h] - m_t[h]) for h in hs]
    wo = [jnp.exp(m_inter[h] - m_t[h]) for h in hs]
    qk = [_bdot_nt(q[h], k[h]) * wi[h] for h in hs]
    num = [_bdot(qk[h], v[h]) + wo[h] * _bdot(q[h], c_st[h]) for h in hs]
    den = [_lane_sum(qk[h]) + wo[h] * _lane_dot(q[h], n_st[h]) for h in hs]
    floor = [jnp.exp(-m_t[h]) for h in hs]
    z = [jnp.maximum(jnp.abs(den[h]), floor[h]) for h in hs]
    g_last = [g_col[h][CHUNK - 1:CHUNK] for h in hs]
    a_col = [g_last[h] - g_col[h] + ig_col[h] for h in hs]
    m_new = [jnp.maximum(g_last[h] + m_st[h], jnp.max(a_col[h], axis=0, keepdims=True)) for h in hs]
    ws = [jnp.exp(a_col[h] - m_new[h]) for h in hs]
    w_old = [jnp.exp(g_last[h] + m_st[h] - m_new[h]) for h in hs]
    return dict(wi=wi, wo=wo, qk=qk, num=num, den=den, z=z, floor=floor, ws=ws, w_old=w_old, m_new=m_new)


def _mlstm_fwd(qkc, u_ml, gn, riders=()):
    t = qkc.shape[0]
    tb = _tile(t, MIXER_ROWS)
    nc_blk = tb // CHUNK

    def body(qk_ref, v_ref, mo_ref, gt_ref, gn_ref, og_ref, cst_ref, nst_ref, mst_ref, c_sc, n_sc, m_sc):
        @pl.when(pl.program_id(0) == 0)
        def _():
            c_sc[...] = jnp.zeros_like(c_sc)
            n_sc[...] = jnp.zeros_like(n_sc)
            m_sc[...] = jnp.zeros_like(m_sc)

        tril_mask = _tri(CHUNK)
        tri = tril_mask.astype(f32)

        def chunk(c, carry):
            r0 = pl.multiple_of(c * CHUNK, CHUNK)
            rows = pl.ds(r0, CHUNK)
            gates = gt_ref[rows, :]
            gc, rowsf = _ml_gate_forms(gates, tri)
            hs = range(HEADS)
            q = [qk_ref[rows, h * ML_DQK:(h + 1) * ML_DQK] * (ML_DQK ** -0.5) for h in hs]
            k = [qk_ref[rows, HEADS * ML_DQK + h * ML_DQK:HEADS * ML_DQK + (h + 1) * ML_DQK] for h in hs]
            v = [v_ref[rows, h * HEAD_W:(h + 1) * HEAD_W] for h in hs]
            c_st = [c_sc[h] for h in hs]
            n_st = [n_sc[h] for h in hs]
            m_full = [m_sc[h] for h in hs]
            r = _ml_chunk(q, k, v, gates, gc, rowsf, c_st, n_st, [m[:, 0:1] for m in m_full], tril_mask)
            ksc = [k[h] * r["ws"][h] for h in hs]
            new_c = [r["w_old"][h] * c_st[h] + _bdot_tn(ksc[h], v[h]) for h in hs]
            for h in hs:
                cs = slice(h * HEAD_W, (h + 1) * HEAD_W)
                cst_ref[c, h] = c_st[h]
                nst_ref[c, h] = n_st[h]
                mst_ref[c, h] = m_full[h]
                c_sc[h] = new_c[h]
                n_sc[h] = r["w_old"][h] * n_st[h] + jnp.sum(ksc[h], axis=0, keepdims=True)
                m_sc[h] = r["m_new"][h] + jnp.zeros((1, 128), f32)
                _, _, y = _head_rms(r["num"][h] / r["z"][h], gn_ref[:, cs], on_mxu=True)
                og_ref[rows, cs] = (y * _sigmoid(mo_ref[rows, h * HEAD_W:(h + 1) * HEAD_W])).astype(ACT)
            return carry

        lax.fori_loop(0, nc_blk, chunk, 0, unroll=True)

    nchunks = t // CHUNK
    return _riding_call(
        body, "mlstm_fwd", t // tb,
        in_specs=[_rows(tb, MIX_W), _rows(tb, MIX_W, 1), _rows(tb, MIX_W, 2), _rows(tb, 128, 12), _const((1, MIX_W))],
        out_specs=[_rows(tb, MIX_W),
                   pl.BlockSpec((nc_blk, HEADS, ML_DQK, HEAD_W), lambda i: (i, 0, 0, 0)),
                   pl.BlockSpec((nc_blk, HEADS, 1, ML_DQK), lambda i: (i, 0, 0, 0)),
                   pl.BlockSpec((nc_blk, HEADS, 1, 128), lambda i: (i, 0, 0, 0))],
        out_shape=[jax.ShapeDtypeStruct((t, MIX_W), ACT),
                   jax.ShapeDtypeStruct((nchunks, HEADS, ML_DQK, HEAD_W), f32),
                   jax.ShapeDtypeStruct((nchunks, HEADS, 1, ML_DQK), f32),
                   jax.ShapeDtypeStruct((nchunks, HEADS, 1, 128), f32)],
        scratch_shapes=[pltpu.VMEM((HEADS, ML_DQK, HEAD_W), f32), pltpu.VMEM((HEADS, 1, ML_DQK), f32), pltpu.VMEM((HEADS, 1, 128), f32)],
        operands=(qkc, u_ml, u_ml, u_ml, gn), riders=riders, copies=_gather_copies, ride_shapes=_gather_shapes(riders))


def _mlstm_bwd(qkc, u_ml, gn, cst, nst, mst, dog, riders=()):
    t = qkc.shape[0]
    tb = _tile(t, MIXER_ROWS)
    nb = t // tb
    nc_blk = tb // CHUNK

    def body(qk_ref, v_ref, mo_ref, gt_ref, gn_ref, cst_ref, nst_ref, mst_ref, dog_ref,
             dqk_ref, dv_ref, dmo_ref, dgt_ref, dgn_ref, dc_sc, dn_sc):
        @pl.when(pl.program_id(0) == 0)
        def _():
            dc_sc[...] = jnp.zeros_like(dc_sc)
            dn_sc[...] = jnp.zeros_like(dn_sc)
            dgn_ref[...] = jnp.zeros_like(dgn_ref)

        tril_mask = _tri(CHUNK)
        tri = tril_mask.astype(f32)
        triu = _tri(CHUNK, upper=True).astype(f32)
        lane = lax.broadcasted_iota(jnp.int32, (CHUNK, 128), 1)

        def chunk(j, carry):
            c = nc_blk - 1 - j
            r0 = pl.multiple_of(c * CHUNK, CHUNK)
            rows = pl.ds(r0, CHUNK)
            gates = gt_ref[rows, :]
            gc, rowsf = _ml_gate_forms(gates, tri)
            dg_mat = jnp.zeros((CHUNK, 128), f32)
            dig_mat = jnp.zeros((CHUNK, 128), f32)
            dlast_row = jnp.zeros((1, 128), f32)
            hs = range(HEADS)
            cols = [slice(h * HEAD_W, (h + 1) * HEAD_W) for h in hs]
            q = [qk_ref[rows, h * ML_DQK:(h + 1) * ML_DQK] * (ML_DQK ** -0.5) for h in hs]
            k = [qk_ref[rows, HEADS * ML_DQK + h * ML_DQK:HEADS * ML_DQK + (h + 1) * ML_DQK] for h in hs]
            v = [v_ref[rows, h * HEAD_W:(h + 1) * HEAD_W] for h in hs]
            c_st = [cst_ref[c, h] for h in hs]
            n_st = [nst_ref[c, h] for h in hs]
            m_st = [mst_ref[c, h][:, 0:1] for h in hs]
            dc = [dc_sc[h] for h in hs]
            dn = [dn_sc[h] for h in hs]
            r = _ml_chunk(q, k, v, gates, gc, rowsf, c_st, n_st, m_st, tril_mask)
            z, wi, wo, ws, w_old, den = r["z"], r["wi"], r["wo"], r["ws"], r["w_old"], r["den"]
            hh = [r["num"][h] / z[h] for h in hs]
            dh = []
            for h in hs:
                gnh = gn_ref[:, cols[h]]
                oh, rstd, y = _head_rms(hh[h], gnh, on_mxu=True)
                sg = _sigmoid(mo_ref[rows, h * HEAD_W:(h + 1) * HEAD_W])
                dogh = dog_ref[rows, cols[h]]
                dy = dogh * sg
                dmo_ref[rows, cols[h]] = (dogh * y * (sg * (1.0 - sg))).astype(ACT)
                dgn_ref[:, cols[h]] += jnp.sum(dy * oh, axis=0, keepdims=True)
                doh = dy * gnh
                dh.append(rstd * (doh - oh * (_lane_sum(doh * oh) * (1.0 / HEAD_W))))
            dnum = [dh[h] / z[h] for h in hs]
            dz = [-_lane_sum(dh[h] * hh[h]) / z[h] for h in hs]
            dden = [jnp.where(jnp.abs(den[h]) > r["floor"][h], dz[h] * jnp.sign(den[h]), 0.0) for h in hs]
            dsw = [(_bdot_nt(dnum[h], v[h]) + dden[h]) * wi[h] for h in hs]
            dq = [_bdot(dsw[h], k[h]) + wo[h] * (_bdot_nt(dnum[h], c_st[h]) + dden[h][:, :ML_DQK] * n_st[h]) for h in hs]
            dk_state = [ws[h] * (_bdot_nt(v[h], dc[h]) + dn[h]) for h in hs]
            dk = [_bdot_tn(dsw[h], q[h]) + dk_state[h] for h in hs]
            dv = [_bdot_tn(r["qk"][h], dnum[h]) + ws[h] * _bdot(k[h], dc[h]) for h in hs]
            woq = [wo[h] * q[h] for h in hs]
            new_dc = [w_old[h] * dc[h] + _bdot_tn(woq[h], dnum[h]) for h in hs]
            for h in hs:
                dv_ref[rows, cols[h]] = dv[h].astype(ACT)
                dc_sc[h] = new_dc[h]
                dn_sc[h] = w_old[h] * dn[h] + jnp.sum(woq[h] * dden[h][:, :ML_DQK], axis=0, keepdims=True)
                d_last = (jnp.sum(jnp.sum(k[h] * dk_state[h], axis=0, keepdims=True), axis=-1, keepdims=True)
                          + w_old[h] * (jnp.sum(jnp.sum(dc[h] * c_st[h], axis=0, keepdims=True), axis=-1, keepdims=True)
                                        + jnp.sum(dn[h] * n_st[h], axis=-1, keepdims=True)))
                kdk = _lane_sum(k[h] * dk[h])
                qdq = _lane_sum(q[h] * dq[h])
                dg_mat = dg_mat + jnp.where(lane == HEADS + h, qdq - kdk, 0.0)
                dlast_row = dlast_row + jnp.where(lane[0:1] == HEADS + h, d_last, 0.0)
                dig_mat = dig_mat + jnp.where(lane == h, kdk, 0.0)
                dqk_ref[rows, h * ML_DQK:(h + 1) * ML_DQK] = dq[h] * (ML_DQK ** -0.5)
                dqk_ref[rows, HEADS * ML_DQK + h * ML_DQK:HEADS * ML_DQK + (h + 1) * ML_DQK] = dk[h]
            dlf = _sel_dot(triu, dg_mat) + dlast_row
            dgt_ref[rows, :] = (dig_mat + dlf * _sigmoid(-gates)).astype(ACT)
            return carry

        lax.fori_loop(0, nc_blk, chunk, 0, unroll=True)

    st4 = lambda a, b: pl.BlockSpec((nc_blk, HEADS, a, b), lambda i: (nb - 1 - i, 0, 0, 0))
    return _riding_call(
        body, "mlstm_bwd", nb,
        in_specs=[_rows_rev(tb, MIX_W, nb), _rows_rev(tb, MIX_W, nb, 1), _rows_rev(tb, MIX_W, nb, 2), _rows_rev(tb, 128, nb, 12),
                  _const((1, MIX_W)), st4(ML_DQK, HEAD_W), st4(1, ML_DQK), st4(1, 128), _rows_rev(tb, MIX_W, nb)],
        out_specs=[_rows_rev(tb, MIX_W, nb), _rows_rev(tb, MIX_W, nb), _rows_rev(tb, MIX_W, nb), _rows_rev(tb, 128, nb), _const((1, MIX_W))],
        out_shape=[jax.ShapeDtypeStruct((t, MIX_W), f32), jax.ShapeDtypeStruct((t, MIX_W), ACT), jax.ShapeDtypeStruct((t, MIX_W), ACT),
                   jax.ShapeDtypeStruct((t, 128), ACT), jax.ShapeDtypeStruct((1, MIX_W), f32)],
        scratch_shapes=[pltpu.VMEM((HEADS, ML_DQK, HEAD_W), f32), pltpu.VMEM((HEADS, 1, ML_DQK), f32)],
        operands=(qkc, u_ml, u_ml, u_ml, gn, cst, nst, mst, dog), riders=riders, copies=_scatter_copies,
        ride_shapes=[jax.ShapeDtypeStruct(r.shape, r.dtype) for r in riders])


def _ln_fwd(r, g, b):
    mu = jnp.mean(r, axis=-1, keepdims=True)
    xc = r - mu
    rstd = lax.rsqrt(jnp.mean(xc * xc, axis=-1, keepdims=True) + LN_EPS)
    xh = xc * rstd
    return xh * g + b, xh, rstd


def _ln_bwd(dy, xh, rstd, g):
    dxh = dy * g
    return rstd * (dxh - jnp.mean(dxh, axis=-1, keepdims=True) - xh * jnp.mean(dxh * xh, axis=-1, keepdims=True))


def _outproj_ln1(og_hg, og_ml, x, w_out, g, b, riders=()):
    t = x.shape[0]
    tm = _tile(t, DENSE_ROWS)

    def body(a_ref, b_ref, x_ref, w_ref, g_ref, bb_ref, x1_ref, xh_ref, rs_ref, x1b_ref):
        mix = _bdot(a_ref[...], w_ref[0:MIX_W, :]) + _bdot(b_ref[...], w_ref[MIX_W:2 * MIX_W, :])
        y, xh, rstd = _ln_fwd(ALPHA * x_ref[...] + mix, g_ref[...], bb_ref[...])
        x1_ref[...] = y
        x1b_ref[...] = y.astype(ACT)
        xh_ref[...] = xh.astype(ACT)
        rs_ref[...] = rstd

    return _riding_call(
        body, "outproj_ln1", t // tm,
        in_specs=[_rows(tm, MIX_W), _rows(tm, MIX_W), _rows(tm, D_MODEL), _resident((D_MODEL, D_MODEL)), _const((1, D_MODEL)), _const((1, D_MODEL))],
        out_specs=[_rows(tm, D_MODEL), _rows(tm, D_MODEL), _rows(tm, 1), _rows(tm, D_MODEL)],
        out_shape=[jax.ShapeDtypeStruct((t, D_MODEL), f32), jax.ShapeDtypeStruct((t, D_MODEL), ACT), jax.ShapeDtypeStruct((t, 1), f32),
                   jax.ShapeDtypeStruct((t, D_MODEL), ACT)],
        scratch_shapes=[], operands=(og_hg, og_ml, x, w_out, g, b), riders=riders, copies=_gather_copies, ride_shapes=_gather_shapes(riders))


def _ffn_up(x1, wg, wu, riders=()):
    t = x1.shape[0]
    tm = _tile(t, DENSE_ROWS)

    def body(x_ref, wg_ref, wu_ref, hg_ref, up_ref, a_ref):
        xv = x_ref[...]
        hg = _bdot_nt(xv, wg_ref[...])
        up = _bdot_nt(xv, wu_ref[...])
        hg_ref[...] = hg.astype(ACT)
        up_ref[...] = up.astype(ACT)
        a_ref[...] = (hg * _sigmoid(hg) * up).astype(ACT)

    return _riding_call(
        body, "ffn_up", t // tm,
        in_specs=[_rows(tm, D_MODEL), _resident((D_FF, D_MODEL)), _resident((D_FF, D_MODEL))],
        out_specs=[_rows(tm, D_FF), _rows(tm, D_FF), _rows(tm, D_FF)],
        out_shape=[jax.ShapeDtypeStruct((t, D_FF), ACT), jax.ShapeDtypeStruct((t, D_FF), ACT), jax.ShapeDtypeStruct((t, D_FF), ACT)],
        scratch_shapes=[], operands=(x1, wg, wu), riders=riders, copies=_gather_copies, ride_shapes=_gather_shapes(riders))


def _ffn_down_ln2(a, x1, wd, g, b):
    t = x1.shape[0]
    tm = _tile(t, DENSE_ROWS)

    def body(a_ref, x_ref, w_ref, g_ref, bb_ref, x2_ref, xh_ref, rs_ref, x2b_ref):
        ffn = _bdot(a_ref[...], w_ref[...])
        y, xh, rstd = _ln_fwd(ALPHA * x_ref[...] + ffn, g_ref[...], bb_ref[...])
        x2_ref[...] = y
        x2b_ref[...] = y.astype(ACT)
        xh_ref[...] = xh.astype(ACT)
        rs_ref[...] = rstd

    return pl.pallas_call(
        body, name="ffn_down_ln2", grid=(t // tm,),
        in_specs=[_rows(tm, D_FF), _rows(tm, D_MODEL), _resident((D_FF, D_MODEL)), _const((1, D_MODEL)), _const((1, D_MODEL))],
        out_specs=[_rows(tm, D_MODEL), _rows(tm, D_MODEL), _rows(tm, 1), _rows(tm, D_MODEL)],
        out_shape=[jax.ShapeDtypeStruct((t, D_MODEL), f32), jax.ShapeDtypeStruct((t, D_MODEL), ACT), jax.ShapeDtypeStruct((t, 1), f32),
                   jax.ShapeDtypeStruct((t, D_MODEL), ACT)],
        compiler_params=_cparams(1, arbitrary=False),
    )(a, x1, wd, g, b)


def _head_loss_bwd(x2, xh2, rs2, p, tgt, w_pg, b_pg, w_pp, g2):
    t = x2.shape[0]
    tm = _tile(t, DENSE_ROWS)

    def body(x_ref, xh_ref, rs_ref, p_ref, t_ref, wg_ref, bg_ref, wp_ref, g_ref,
             dr_ref, de_ref, dz_ref, loss_ref, dbg_ref, dg2_ref, db2_ref):
        @pl.when(pl.program_id(0) == 0)
        def _():
            loss_ref[...] = jnp.zeros_like(loss_ref)
            dbg_ref[...] = jnp.zeros_like(dbg_ref)
            dg2_ref[...] = jnp.zeros_like(dg2_ref)
            db2_ref[...] = jnp.zeros_like(db2_ref)

        x2v = x_ref[...]
        z = _bdot(x2v, wg_ref[...]) + bg_ref[...]
        e = _bdot(p_ref[...], wp_ref[...])
        sg = _sigmoid(z)
        diff = x2v + sg * e - t_ref[...]
        loss_ref[...] += 0.5 * jnp.sum(jnp.mean(diff * diff, axis=-1, keepdims=True), axis=0, keepdims=True)
        dy = diff * (1.0 / D_MODEL)
        de_ref[...] = (dy * sg).astype(ACT)
        dz = dy * e * (sg * (1.0 - sg))
        dz_ref[...] = dz.astype(ACT)
        dbg_ref[...] += jnp.sum(dz, axis=0, keepdims=True)
        dx2 = dy + _bdot_nt(dz, wg_ref[...])
        xh = xh_ref[...].astype(f32)
        dg2_ref[...] += jnp.sum(dx2 * xh, axis=0, keepdims=True)
        db2_ref[...] += jnp.sum(dx2, axis=0, keepdims=True)
        dr_ref[...] = _ln_bwd(dx2, xh, rs_ref[...], g_ref[...])

    row = jax.ShapeDtypeStruct((1, D_MODEL), f32)
    return pl.pallas_call(
        body, name="head_loss_bwd", grid=(t // tm,),
        in_specs=[_rows(tm, D_MODEL), _rows(tm, D_MODEL), _rows(tm, 1), _rows(tm, PLE), _rows(tm, D_MODEL),
                  _resident((D_MODEL, D_MODEL)), _const((1, D_MODEL)), _resident((PLE, D_MODEL)), _const((1, D_MODEL))],
        out_specs=[_rows(tm, D_MODEL), _rows(tm, D_MODEL), _rows(tm, D_MODEL), _const((1, 1)), _const((1, D_MODEL)), _const((1, D_MODEL)), _const((1, D_MODEL))],
        out_shape=[jax.ShapeDtypeStruct((t, D_MODEL), f32), jax.ShapeDtypeStruct((t, D_MODEL), ACT), jax.ShapeDtypeStruct((t, D_MODEL), ACT),
                   jax.ShapeDtypeStruct((1, 1), f32), row, row, row],
        compiler_params=_cparams(1),
    )(x2, xh2, rs2, p, tgt, w_pg, b_pg, w_pp, g2)


def _ffn_bwd(dr2, hg, up, xh1, rs1, wd, wg, wu, g1, w_out):
    t = dr2.shape[0]
    tm = _tile(t, DENSE_ROWS // 2)

    def body(dr_ref, hg_ref, up_ref, xh_ref, rs_ref, wd_ref, wg_ref, wu_ref, g_ref, wo_ref,
             dr1_ref, dhg_ref, dup_ref, dg1_ref, db1_ref, doghg_ref, dogml_ref):
        @pl.when(pl.program_id(0) == 0)
        def _():
            dg1_ref[...] = jnp.zeros_like(dg1_ref)
            db1_ref[...] = jnp.zeros_like(db1_ref)

        dr2v = dr_ref[...]
        da = _bdot_nt(dr2v, wd_ref[...])
        hgv = hg_ref[...].astype(f32)
        sg = _sigmoid(hgv)
        dhg = da * up_ref[...].astype(f32) * (sg * (1.0 + hgv * (1.0 - sg)))
        dup = da * (hgv * sg)
        dhg_ref[...] = dhg.astype(ACT)
        dup_ref[...] = dup.astype(ACT)
        dx1 = ALPHA * dr2v + _bdot(dhg, wg_ref[...]) + _bdot(dup, wu_ref[...])
        xh = xh_ref[...].astype(f32)
        dg1_ref[...] += jnp.sum(dx1 * xh, axis=0, keepdims=True)
        db1_ref[...] += jnp.sum(dx1, axis=0, keepdims=True)
        dr1 = _ln_bwd(dx1, xh, rs_ref[...], g_ref[...])
        dr1_ref[...] = dr1
        dog = _bdot_nt(dr1, wo_ref[...])
        doghg_ref[...] = dog[:, 0:MIX_W]
        dogml_ref[...] = dog[:, MIX_W:2 * MIX_W]

    row = jax.ShapeDtypeStruct((1, D_MODEL), f32)
    return pl.pallas_call(
        body, name="ffn_bwd", grid=(t // tm,),
        in_specs=[_rows(tm, D_MODEL), _rows(tm, D_FF), _rows(tm, D_FF), _rows(tm, D_MODEL), _rows(tm, 1),
                  _resident((D_FF, D_MODEL)), _resident((D_FF, D_MODEL)), _resident((D_FF, D_MODEL)), _const((1, D_MODEL)),
                  _resident((D_MODEL, D_MODEL))],
        out_specs=[_rows(tm, D_MODEL), _rows(tm, D_FF), _rows(tm, D_FF), _const((1, D_MODEL)), _const((1, D_MODEL)),
                   _rows(tm, MIX_W), _rows(tm, MIX_W)],
        out_shape=[jax.ShapeDtypeStruct((t, D_MODEL), f32), jax.ShapeDtypeStruct((t, D_FF), ACT), jax.ShapeDtypeStruct((t, D_FF), ACT), row, row,
                   jax.ShapeDtypeStruct((t, MIX_W), f32), jax.ShapeDtypeStruct((t, MIX_W), f32)],
        compiler_params=_cparams(1),
    )(dr2, hg, up, xh1, rs1, wd, wg, wu, g1, w_out)


def _inproj_bwd(dr1, du_hg, dqk, dmv, dmo, dgt, w_hg, w_ml):
    t = dr1.shape[0]
    tm = _tile(t, DENSE_ROWS)

    def body(dr_ref, dhg_ref, dqk_ref, dmv_ref, dmo_ref, dgt_ref, whg_ref, wml_ref, gx_ref, dml_ref):
        dml = jnp.concatenate([dqk_ref[...], dmv_ref[...], dmo_ref[...], dgt_ref[...]], axis=-1).astype(ACT)
        dml_ref[...] = dml
        gx_ref[...] = ALPHA * dr_ref[...] + _bdot(dhg_ref[...], whg_ref[...]) + _bdot(dml, wml_ref[...])

    return pl.pallas_call(
        body, name="inproj_bwd", grid=(t // tm,),
        in_specs=[_rows(tm, D_MODEL), _rows(tm, U_HG), _rows(tm, MIX_W), _rows(tm, MIX_W), _rows(tm, MIX_W), _rows(tm, 128),
                  _resident((U_HG, D_MODEL)), _resident((U_ML, D_MODEL))],
        out_specs=[_rows(tm, D_MODEL), _rows(tm, U_ML)],
        out_shape=[jax.ShapeDtypeStruct((t, D_MODEL), f32), jax.ShapeDtypeStruct((t, U_ML), ACT)],
        compiler_params=_cparams(1, arbitrary=False),
    )(dr1, du_hg, dqk, dmv, dmo, dgt, w_hg, w_ml)


def _wgrad(a, b, name, tk=None, tn=None, colsum=False, low=False):
    t, kdim = a.shape
    n = b.shape[1]
    tk = tk or kdim
    tn = tn or n
    tt = _tile(t, WGRAD_ROWS)
    nt = t // tt
    assert not (colsum and low) and (not colsum or tn == n)

    def body(a_ref, b_ref, o_ref, *s_ref):
        @pl.when(pl.program_id(2) == 0)
        def _():
            o_ref[...] = jnp.zeros_like(o_ref)
            if colsum:
                s_ref[0][...] = jnp.zeros_like(s_ref[0])

        av = a_ref[...]
        o_ref[...] += _bdot_tn(av, b_ref[...])
        if colsum:
            s_ref[0][...] += jnp.sum(av.astype(f32), axis=0, keepdims=True)
        if low:
            @pl.when(pl.program_id(2) == nt - 1)
            def _():
                s_ref[0][...] = o_ref[...].astype(bf16)

    out_specs = [pl.BlockSpec((tk, tn), lambda i, j, s: (i, j))]
    out_shape = [jax.ShapeDtypeStruct((kdim, n), f32)]
    if colsum:
        out_specs.append(pl.BlockSpec((1, tk), lambda i, j, s: (0, i)))
        out_shape.append(jax.ShapeDtypeStruct((1, kdim), f32))
    if low:
        out_specs.append(pl.BlockSpec((tk, tn), lambda i, j, s: (i, j)))
        out_shape.append(jax.ShapeDtypeStruct((kdim, n), bf16))
    res = pl.pallas_call(
        body, name=name, grid=(kdim // tk, n // tn, t // tt),
        in_specs=[pl.BlockSpec((tt, tk), lambda i, j, s: (s, i)), pl.BlockSpec((tt, tn), lambda i, j, s: (s, j))],
        out_specs=out_specs, out_shape=out_shape,
        compiler_params=_cparams(3),
    )(a, b)
    return res if (colsum or low) else res[0]


_TRANSPOSED = {"w_in", "w_ffn_gate", "w_ffn_up"}
_COL_SPLIT = {"ple_w_proj"}
_SCATTER_PLAN = (("w_ffn_gate", "w_out", "ple_w_gate", "ple_w_proj"), ("w_ffn_up", "w_ffn_down"))
_RIDE_PLAN = {"inproj": ("w_ffn_gate",), "hgrn2_fwd": ("w_out",), "mlstm_fwd": ("w_ffn_up",),
              "outproj_ln1": ("ple_w_gate", "ple_w_proj"), "ffn_up": ("w_ffn_down",)}


def _from_chip_major(a, col_split):
    if col_split:
        return a.transpose(1, 0, 2).reshape(a.shape[1], 4 * a.shape[2])
    return a.reshape(4 * a.shape[1], a.shape[2])


def _local_step(x, p, tgt, w_in_b, b_in, logits, conv_w, conv_b, hg_gn, ml_gn, w_out_b, ln1_g, ln1_b,
                wg_b, wu_b, wd_b, ln2_g, ln2_b, w_pp_b, w_pg_b, b_pg, early_hook=None, late_shards=None):
    pad_w = U_HG + U_ML - PROJ_W
    w_hg = w_in_b[:U_HG]
    w_ml = jnp.pad(w_in_b[U_HG:], ((0, pad_w), (0, 0)))
    bb_hg = b_in[:, :U_HG]
    bb_ml = jnp.pad(b_in[:, U_HG:], ((0, 0), (0, pad_w)))

    late = dict(w_out=w_out_b, w_ffn_gate=wg_b, w_ffn_up=wu_b, w_ffn_down=wd_b, ple_w_proj=w_pp_b, ple_w_gate=w_pg_b)

    def riders_of(call):
        return [late_shards[k] for k in _RIDE_PLAN[call]] if late_shards is not None else ()

    def arrived(call, got):
        for k, g in zip(_RIDE_PLAN[call], got):
            late[k] = _from_chip_major(g, k in _COL_SPLIT)

    (u_hg, u_ml, xb), got = _inproj(x, w_hg, w_ml, bb_hg, bb_ml, riders_of("inproj"))
    arrived("inproj", got)
    (og_hg, sst, hg_b, hg_a, hg_o), got = _hgrn2_fwd(u_hg, logits, hg_gn, riders_of("hgrn2_fwd"))
    arrived("hgrn2_fwd", got)
    pre, qkc = _conv_fwd(u_ml, conv_w, conv_b)
    (og_ml, cst, nst, mst), got = _mlstm_fwd(qkc, u_ml, ml_gn, riders_of("mlstm_fwd"))
    arrived("mlstm_fwd", got)
    (x1, xh1, rs1, x1b), got = _outproj_ln1(og_hg, og_ml, x, late["w_out"], ln1_g, ln1_b, riders_of("outproj_ln1"))
    arrived("outproj_ln1", got)
    (hgp, up, act), got = _ffn_up(x1b, late["w_ffn_gate"], late["w_ffn_up"], riders_of("ffn_up"))
    arrived("ffn_up", got)
    w_out_b, wg_b, wu_b, wd_b = late["w_out"], late["w_ffn_gate"], late["w_ffn_up"], late["w_ffn_down"]
    w_pp_b, w_pg_b = late["ple_w_proj"], late["ple_w_gate"]
    x2, xh2, rs2, x2b = _ffn_down_ln2(act, x1, wd_b, ln2_g, ln2_b)
    dr2, de, dz, loss, d_bpg, d_ln2g, d_ln2b = _head_loss_bwd(x2, xh2, rs2, p, tgt, w_pg_b, b_pg, w_pp_b, ln2_g)
    dr1, dhg, dup, d_ln1g, d_ln1b, dog_hg, dog_ml = _ffn_bwd(dr2, hgp, up, xh1, rs1, wd_b, wg_b, wu_b, ln1_g, w_out_b)

    d_wo_a, lo_wo_a = _wgrad(og_hg, dr1, "wgrad_out_hg", low=True)
    d_wo_b, lo_wo_b = _wgrad(og_ml, dr1, "wgrad_out_ml", low=True)
    d_wg, lo_wg = _wgrad(dhg, x1b, "wgrad_ffn_gate", tk=D_FF // 2, low=True)
    d_wu, lo_wu = _wgrad(dup, x1b, "wgrad_ffn_up", tk=D_FF // 2, low=True)
    d_wd, lo_wd = _wgrad(act, dr2, "wgrad_ffn_down", tk=D_FF // 2, low=True)
    d_wpp, lo_wpp = _wgrad(p, de, "wgrad_ple_proj", low=True)
    d_wpg, lo_wpg = _wgrad(x2b, dz, "wgrad_ple_gate", low=True)
    early = dict(w_out=jnp.concatenate([d_wo_a, d_wo_b], axis=0), w_ffn_gate=d_wg, w_ffn_up=d_wu, w_ffn_down=d_wd,
                 ple_w_proj=d_wpp, ple_w_gate=d_wpg)
    early_low = dict(w_out=jnp.concatenate([lo_wo_a, lo_wo_b], axis=0), w_ffn_gate=lo_wg, w_ffn_up=lo_wu, w_ffn_down=lo_wd,
                     ple_w_proj=lo_wpp, ple_w_gate=lo_wpg)
    ride_hg, ride_ml = early_hook(early_low) if early_hook is not None else ((), ())

    (du_hg, d_logits, d_hg_gn), got_hg = _hgrn2_bwd(u_hg, logits, hg_gn, sst, hg_b, hg_a, hg_o, dog_hg, ride_hg)
    (dqkc, dmv, dmo, dgt, d_ml_gn), got_ml = _mlstm_bwd(qkc, u_ml, ml_gn, cst, nst, mst, dog_ml, ride_ml)
    dqk, d_conv_w, d_conv_b = _conv_bwd(u_ml, conv_w, pre, dqkc)
    grad_x, du_ml = _inproj_bwd(dr1, du_hg, dqk, dmv, dmo, dgt, w_hg, w_ml)

    dw_hg, db_hg = _wgrad(du_hg, xb, "wgrad_in_hg", tk=U_HG // 2, colsum=True)
    dw_ml, db_ml = _wgrad(du_ml, xb, "wgrad_in_ml", colsum=True)
    d_w_in = jnp.concatenate([dw_hg, dw_ml[:PROJ_W - U_HG]], axis=0)
    d_b_in = jnp.concatenate([db_hg, db_ml[:, :PROJ_W - U_HG]], axis=1)

    grads = dict(w_in=d_w_in, b_in=d_b_in, hg_lb_logits=d_logits, ml_conv_w=d_conv_w, ml_conv_b=d_conv_b,
                 hg_norm_g=d_hg_gn, ml_norm_g=d_ml_gn, ln1_g=d_ln1g, ln1_b=d_ln1b, ln2_g=d_ln2g, ln2_b=d_ln2b,
                 ple_b_gate=d_bpg, **early)
    return loss, grad_x, grads, (list(got_hg), list(got_ml))


_ANY = pl.BlockSpec(memory_space=pltpu.HBM)
_MESH = pl.DeviceIdType.MESH


def _my_place():
    return lax.axis_index("x"), lax.axis_index("y"), lax.axis_index("c")


def _other_chips(x, y):
    return [(1 - x, y), (x, 1 - y), (1 - x, 1 - y)]


_VMEM = pl.BlockSpec(memory_space=pltpu.VMEM)
_EX_ROWS = 32


def _pair_reduce_cols(p, name):
    s, r, c = p.shape
    hc = c // 2

    def body(p_ref, o_ref, other, send_sem, recv_sem):
        x, y, cc = _my_place()

        def run(mine_lo, theirs_lo):
            cp = pltpu.make_async_remote_copy(src_ref=p_ref.at[pl.ds(0, s), pl.ds(0, r), pl.ds(theirs_lo, hc)], dst_ref=other,
                                              send_sem=send_sem, recv_sem=recv_sem, device_id=(x, y, 1 - cc), device_id_type=_MESH)
            cp.start()
            cp.wait()
            for slot in range(s):
                o_ref[slot] = (p_ref[slot, :, mine_lo:mine_lo + hc] + other[slot]).astype(bf16)

        @pl.when(cc == 0)
        def _():
            run(0, hc)

        @pl.when(cc == 1)
        def _():
            run(hc, 0)

    return pl.pallas_call(
        body, name=name, in_specs=[_VMEM], out_specs=_VMEM,
        out_shape=jax.ShapeDtypeStruct((s, r, hc), bf16),
        scratch_shapes=[pltpu.VMEM((s, r, hc), f32), pltpu.SemaphoreType.DMA, pltpu.SemaphoreType.DMA],
        compiler_params=pltpu.CompilerParams(vmem_limit_bytes=VMEM_LIMIT),
    )(p)


def _chip_reduce_swap_cols(rcv, name):
    s, r, hc = rcv.shape

    def body(r_ref, g_ref, send_sem, recv_sem):
        x, y, cc = _my_place()
        acc = r_ref[0].astype(f32)
        for slot in range(1, s):
            acc = acc + r_ref[slot].astype(f32)
        g_ref[cc] = acc
        cp = pltpu.make_async_remote_copy(src_ref=g_ref.at[cc], dst_ref=g_ref.at[cc], send_sem=send_sem, recv_sem=recv_sem,
                                          device_id=(x, y, 1 - cc), device_id_type=_MESH)
        cp.start()
        cp.wait()

    both = pl.pallas_call(
        body, name=name, in_specs=[_VMEM], out_specs=_VMEM,
        out_shape=jax.ShapeDtypeStruct((2, r, hc), f32),
        scratch_shapes=[pltpu.SemaphoreType.DMA, pltpu.SemaphoreType.DMA],
        compiler_params=pltpu.CompilerParams(vmem_limit_bytes=VMEM_LIMIT),
    )(rcv)
    return both.transpose(1, 0, 2).reshape(r, 2 * hc)


def _reduce_adamw(rcv, w, m, v, name):
    s, r, c = rcv.shape
    rows_per = _EX_ROWS

    def body(r_ref, w_ref, m_ref, v_ref, g_ref, d_ref, nm_ref, nv_ref, mine, theirs, send_sem, recv_sem):
        x, y, cc = _my_place()

        def chip_sum(i, carry):
            rs = pl.ds(pl.multiple_of(i * rows_per, rows_per), rows_per)
            acc = r_ref[0, rs, :].astype(f32)
            for slot in range(1, s):
                acc = acc + r_ref[slot, rs, :].astype(f32)
            mine[rs, :] = acc
            return carry

        lax.fori_loop(0, r // rows_per, chip_sum, 0)
        cp = pltpu.make_async_remote_copy(src_ref=mine, dst_ref=theirs, send_sem=send_sem, recv_sem=recv_sem,
                                          device_id=(x, y, 1 - cc), device_id_type=_MESH)
        cp.start()
        cp.wait()

        def update(i, carry):
            rs = pl.ds(pl.multiple_of(i * rows_per, rows_per), rows_per)
            g = mine[rs, :] + theirs[rs, :]
            nm = B1 * m_ref[rs, :] + (1.0 - B1) * g
            nv = B2 * v_ref[rs, :] + (1.0 - B2) * (g * g)
            g_ref[rs, :] = g
            nm_ref[rs, :] = nm
            nv_ref[rs, :] = nv
            d_ref[rs, :] = -LR * ((nm / (1.0 - B1 ** STEP)) / (jnp.sqrt(nv / (1.0 - B2 ** STEP)) + EPS_ADAM) + WD * w_ref[rs, :])
            return carry

        lax.fori_loop(0, r // rows_per, update, 0)

    return pl.pallas_call(
        body, name=name, in_specs=[_VMEM] * 4, out_specs=[_VMEM] * 4,
        out_shape=[jax.ShapeDtypeStruct((r, c), f32)] * 4,
        scratch_shapes=[pltpu.VMEM((r, c), f32), pltpu.VMEM((r, c), f32), pltpu.SemaphoreType.DMA, pltpu.SemaphoreType.DMA],
        compiler_params=pltpu.CompilerParams(vmem_limit_bytes=VMEM_LIMIT),
    )(rcv, w, m, v)


def _gather_copies(ins, outs, send_sems, recv_sems, local_sems):
    x, y, c = _my_place()
    me = 2 * x + y
    local, outgoing, incoming = [], [], []
    for a in range(len(ins)):
        local.append(pltpu.make_async_copy(ins[a], outs[a].at[me], local_sems.at[a]))
        for j, (px, py) in enumerate(_other_chips(x, y)):
            sems = dict(send_sem=send_sems.at[3 * a + j], recv_sem=recv_sems.at[3 * a + j], device_id=(px, py, c), device_id_type=_MESH)
            outgoing.append(pltpu.make_async_remote_copy(src_ref=ins[a], dst_ref=outs[a].at[me], **sems))
            incoming.append(pltpu.make_async_remote_copy(src_ref=ins[a], dst_ref=outs[a].at[2 * px + py], **sems))
    return local, outgoing, incoming


def _gather_first(block, taps, name):
    r, c = block.shape
    hc = c // 2

    def body(in_ref, tap_in, out_ref, tap_out, send_sems, recv_sems):
        x, y, cc = _my_place()
        me = 2 * x + y
        sibling = (x, y, 1 - cc)
        chips = _other_chips(x, y)
        out_ref[me] = in_ref[...]
        tap_out[me] = tap_in[...]

        def run(mine, theirs):
            def ici(j, chip):
                px, py = chips[j]
                src = in_ref.at[pl.ds(0, r), pl.ds(mine, hc)] if chip is None else out_ref.at[chip, pl.ds(0, r), pl.ds(mine, hc)]
                dst = out_ref.at[me if chip is None else chip, pl.ds(0, r), pl.ds(mine, hc)]
                return pltpu.make_async_remote_copy(src_ref=src, dst_ref=dst, send_sem=send_sems.at[j], recv_sem=recv_sems.at[j],
                                                    device_id=(px, py, cc), device_id_type=_MESH)

            def d2d(j, lo):
                px, py = chips[j]
                blk = out_ref.at[2 * px + py, pl.ds(0, r), pl.ds(lo, hc)]
                return pltpu.make_async_remote_copy(src_ref=blk, dst_ref=blk, send_sem=send_sems.at[3 + j], recv_sem=recv_sems.at[3 + j],
                                                    device_id=sibling, device_id_type=_MESH)

            def tap(j, chip):
                px, py = chips[j]
                return pltpu.make_async_remote_copy(src_ref=tap_in, dst_ref=tap_out.at[me if chip is None else chip],
                                                    send_sem=send_sems.at[6 + j], recv_sem=recv_sems.at[6 + j],
                                                    device_id=(px, py, cc), device_id_type=_MESH)

            for j in range(3):
                ici(j, None).start()
                tap(j, None).start()
            for j, (px, py) in enumerate(chips):
                ici(j, 2 * px + py).wait_recv()
                d2d(j, mine).start()
            for j, (px, py) in enumerate(chips):
                d2d(j, theirs).wait_recv()
                tap(j, 2 * px + py).wait_recv()
            for j in range(3):
                ici(j, None).wait_send()
                d2d(j, mine).wait_send()
                tap(j, None).wait_send()

        @pl.when(cc == 0)
        def _():
            run(0, hc)

        @pl.when(cc == 1)
        def _():
            run(hc, 0)

    return pl.pallas_call(
        body, name=name, in_specs=[_VMEM, _VMEM], out_specs=[_VMEM, _VMEM],
        out_shape=[jax.ShapeDtypeStruct((4, r, c), block.dtype), jax.ShapeDtypeStruct((4,) + taps.shape, taps.dtype)],
        scratch_shapes=[pltpu.SemaphoreType.DMA((9,)), pltpu.SemaphoreType.DMA((9,))],
        compiler_params=pltpu.CompilerParams(vmem_limit_bytes=VMEM_LIMIT),
    )(block, taps)


def _riding_call(body, name, nsteps, in_specs, out_specs, out_shape, scratch_shapes, operands, riders, copies, ride_shapes):
    nr, n_in, n_out, n_scr = len(riders), len(in_specs), len(out_specs), len(scratch_shapes)

    def wrapped(*refs):
        ins, ride_in = refs[:n_in], refs[n_in:n_in + nr]
        outs, ride_out = refs[n_in + nr:n_in + nr + n_out], refs[n_in + nr + n_out:n_in + 2 * nr + n_out]
        scratch, sems = refs[n_in + 2 * nr + n_out:n_in + 2 * nr + n_out + n_scr], refs[n_in + 2 * nr + n_out + n_scr:]
        if nr:
            @pl.when(pl.program_id(0) == 0)
            def _():
                local, outgoing, _ = copies(ride_in, ride_out, *sems)
                for cp in local + outgoing:
                    cp.start()

        body(*ins, *outs, *scratch)
        if nr:
            @pl.when(pl.program_id(0) == nsteps - 1)
            def _():
                local, outgoing, incoming = copies(ride_in, ride_out, *sems)
                for cp in incoming:
                    cp.wait_recv()
                for cp in outgoing:
                    cp.wait_send()
                for cp in local:
                    cp.wait()

    hbm = pl.BlockSpec(memory_space=pltpu.HBM)
    sems = [pltpu.SemaphoreType.DMA((3 * nr,)), pltpu.SemaphoreType.DMA((3 * nr,)), pltpu.SemaphoreType.DMA((nr,))] if nr else []
    res = pl.pallas_call(
        wrapped, name=name, grid=(nsteps,),
        in_specs=list(in_specs) + [hbm] * nr, out_specs=list(out_specs) + [hbm] * nr,
        out_shape=list(out_shape) + list(ride_shapes),
        scratch_shapes=list(scratch_shapes) + sems,
        compiler_params=_cparams(1),
    )(*operands, *riders)
    return list(res[:n_out]), list(res[n_out:])


def _gather_shapes(riders):
    return [jax.ShapeDtypeStruct((4,) + r.shape, r.dtype) for r in riders]


def _scatter_copies(ins, outs, send_sems, recv_sems, local_sems):
    x, y, c = _my_place()
    me = 2 * x + y
    local, outgoing, incoming = [], [], []
    for a in range(len(ins)):
        local.append(pltpu.make_async_copy(ins[a].at[me], outs[a].at[me], local_sems.at[a]))
        for j, (px, py) in enumerate(_other_chips(x, y)):
            sems = dict(send_sem=send_sems.at[3 * a + j], recv_sem=recv_sems.at[3 * a + j], device_id=(px, py, c), device_id_type=_MESH)
            outgoing.append(pltpu.make_async_remote_copy(src_ref=ins[a].at[2 * px + py], dst_ref=outs[a].at[me], **sems))
            incoming.append(pltpu.make_async_remote_copy(src_ref=ins[a].at[2 * px + py], dst_ref=outs[a].at[2 * px + py], **sems))
    return local, outgoing, incoming


def _scatter_chips(pieces, name):
    n = len(pieces)

    def body(*refs):
        local, outgoing, incoming = _scatter_copies(refs[:n], refs[n:2 * n], *refs[2 * n:])
        for cp in local + outgoing:
            cp.start()
        for cp in incoming:
            cp.wait_recv()
        for cp in outgoing:
            cp.wait_send()
        for cp in local:
            cp.wait()

    return pl.pallas_call(
        body, name=name,
        in_specs=[_ANY] * n, out_specs=[_ANY] * n,
        out_shape=[jax.ShapeDtypeStruct(s.shape, s.dtype) for s in pieces],
        scratch_shapes=[pltpu.SemaphoreType.DMA((3 * n,)), pltpu.SemaphoreType.DMA((3 * n,)), pltpu.SemaphoreType.DMA((n,))],
    )(*pieces)


def _gather_all(block, name):
    def body(in_ref, out_ref, send_sems, recv_sems, local_sem):
        x, y, c = _my_place()
        me = 4 * x + 2 * y + c
        cp = pltpu.make_async_copy(in_ref, out_ref.at[me], local_sem)
        cp.start()
        peers = []
        for dx in range(2):
            for dy in range(2):
                for dc in range(2):
                    if dx or dy or dc:
                        peers.append((1 - x if dx else x, 1 - y if dy else y, 1 - c if dc else c))
        for j, pr in enumerate(peers):
            pltpu.make_async_remote_copy(src_ref=in_ref, dst_ref=out_ref.at[me], send_sem=send_sems.at[j], recv_sem=recv_sems.at[j],
                                         device_id=pr, device_id_type=_MESH).start()
        for j, (px, py, pc) in enumerate(peers):
            pltpu.make_async_remote_copy(src_ref=in_ref, dst_ref=out_ref.at[4 * px + 2 * py + pc], send_sem=send_sems.at[j], recv_sem=recv_sems.at[j],
                                         device_id=(px, py, pc), device_id_type=_MESH).wait()
        cp.wait()

    return pl.pallas_call(
        body, name=name,
        in_specs=[_ANY], out_specs=_ANY,
        out_shape=jax.ShapeDtypeStruct((8,) + block.shape, block.dtype),
        scratch_shapes=[pltpu.SemaphoreType.DMA((7,)), pltpu.SemaphoreType.DMA((7,)), pltpu.SemaphoreType.DMA],
    )(block)


def _row_tile(r, c):
    best = r
    for cand in range(16, r + 1, 16):
        if r % cand == 0 and cand * c * 4 <= (1 << 20):
            best = cand
    return best if best * c * 4 <= (4 << 20) else r


def _sum_slots(parts, name):
    n, r, c = parts.shape
    tr = _row_tile(r, c)

    def body(p_ref, o_ref):
        acc = p_ref[0].astype(f32)
        for s in range(1, n):
            acc = acc + p_ref[s].astype(f32)
        o_ref[...] = acc

    return pl.pallas_call(
        body, name=name, grid=(r // tr,),
        in_specs=[pl.BlockSpec((n, tr, c), lambda i: (0, i, 0))],
        out_specs=pl.BlockSpec((tr, c), lambda i: (i, 0)),
        out_shape=jax.ShapeDtypeStruct((r, c), f32),
        compiler_params=_cparams(1, arbitrary=False),
    )(parts)


def _adamw(parts, w, m, v, name):
    n, r, c = parts.shape
    tr = _row_tile(r, c)
    tc = c
    if tr == r and r * c * 4 > (1 << 20) and c % 256 == 0:
        tc = 256

    def body(p_ref, w_ref, m_ref, v_ref, g_ref, d_ref, nm_ref, nv_ref):
        g = p_ref[0]
        for s in range(1, n):
            g = g + p_ref[s]
        nm = B1 * m_ref[...] + (1.0 - B1) * g
        nv = B2 * v_ref[...] + (1.0 - B2) * (g * g)
        m_hat = nm / (1.0 - B1 ** STEP)
        v_hat = nv / (1.0 - B2 ** STEP)
        g_ref[...] = g
        nm_ref[...] = nm
        nv_ref[...] = nv
        d_ref[...] = -LR * (m_hat / (jnp.sqrt(v_hat) + EPS_ADAM) + WD * w_ref[...])

    blk = pl.BlockSpec((tr, tc), lambda i, j: (i, j))
    return pl.pallas_call(
        body, name=name, grid=(r // tr, c // tc),
        in_specs=[pl.BlockSpec((n, tr, tc), lambda i, j: (0, i, j)), blk, blk, blk],
        out_specs=[blk] * 4,
        out_shape=[jax.ShapeDtypeStruct((r, c), f32)] * 4,
        compiler_params=_cparams(2, arbitrary=False),
    )(parts, w, m, v)


_BIG = ["w_in", "w_out", "w_ffn_gate", "w_ffn_up", "w_ffn_down", "ple_w_proj", "ple_w_gate"]
_SMALL = ["b_in", "hg_lb_logits", "ml_conv_w", "ml_conv_b", "hg_norm_g", "ml_norm_g", "ln1_g", "ln1_b", "ln2_g", "ln2_b", "ple_b_gate"]
_ORDER = ["w_in", "b_in", "hg_lb_logits", "ml_conv_w", "ml_conv_b", "hg_norm_g", "ml_norm_g", "w_out", "ln1_g", "ln1_b",
          "w_ffn_gate", "w_ffn_up", "w_ffn_down", "ln2_g", "ln2_b", "ple_w_proj", "ple_w_gate", "ple_b_gate"]
_PACK_ROWS, _PACK_COLS = 16, 1024


def _pack(arrays):
    flat = jnp.concatenate([a.reshape(-1) for a in arrays])
    return jnp.pad(flat, (0, _PACK_ROWS * _PACK_COLS - flat.shape[0])).reshape(_PACK_ROWS, _PACK_COLS)


def _unpack(pack, shapes):
    flat = pack.reshape(-1)
    out, off = [], 0
    for s in shapes:
        size = 1
        for d in s:
            size *= d
        out.append(flat[off:off + size].reshape(s))
        off += size
    return out


def _to_chip_major(g, col_split):
    if col_split:
        k, n = g.shape
        return g.reshape(k, 4, n // 4).transpose(1, 0, 2)
    k, n = g.shape
    return g.reshape(4, k // 4, n)


def kernel(x, p, w_in, b_in, hg_lb_logits, ml_conv_w, ml_conv_b, hg_norm_g, ml_norm_g, w_out, ln1_g, ln1_b, w_ffn_gate, w_ffn_up, w_ffn_down, ln2_g, ln2_b, ple_w_proj, ple_w_gate, ple_b_gate, loss_target, m_w_in, m_b_in, m_hg_lb_logits, m_ml_conv_w, m_ml_conv_b, m_hg_norm_g, m_ml_norm_g, m_w_out, m_ln1_g, m_ln1_b, m_w_ffn_gate, m_w_ffn_up, m_w_ffn_down, m_ln2_g, m_ln2_b, m_ple_w_proj, m_ple_w_gate, m_ple_b_gate, v_w_in, v_b_in, v_hg_lb_logits, v_ml_conv_w, v_ml_conv_b, v_hg_norm_g, v_ml_norm_g, v_w_out, v_ln1_g, v_ln1_b, v_w_ffn_gate, v_w_ffn_up, v_w_ffn_down, v_ln2_g, v_ln2_b, v_ple_w_proj, v_ple_w_gate, v_ple_b_gate):
    args = dict(locals())
    wts = {k: args[k] for k in _ORDER}
    mom = {k: args["m_" + k] for k in _ORDER}
    var = {k: args["v_" + k] for k in _ORDER}
    two_d = lambda a: a.reshape(a.shape[-2], a.shape[-1])
    block = lambda k, a: jnp.swapaxes(two_d(a), 0, 1) if k in _TRANSPOSED else two_d(a)
    unblock = lambda k, a: (jnp.swapaxes(a, 0, 1) if k in _TRANSPOSED else a).reshape(wts[k].shape)

    shards = {k: block(k, wts[k]).astype(bf16) for k in _BIG}
    w_in_blocks, taps = _gather_first(shards["w_in"], two_d(ml_conv_w), "gather_w_in")
    w_in_full = _from_chip_major(w_in_blocks, False)
    conv_w_full = _from_chip_major(taps, True)

    early_keys = _BIG[1:]
    loss, grad_x, grads, (got_hg, got_ml) = _local_step(
        x[0], p[0, 0], loss_target[0], w_in_full, b_in, hg_lb_logits, conv_w_full, ml_conv_b, hg_norm_g, ml_norm_g,
        None, ln1_g, ln1_b, None, None, None, ln2_g, ln2_b, None, None, ple_b_gate,
        early_hook=lambda low: tuple([_to_chip_major(low[k], k in _COL_SPLIT) for k in names] for names in _SCATTER_PLAN),
        late_shards={k: shards[k] for k in early_keys})

    out_g, out_d, out_m, out_v = {}, {}, {}, {}

    def finish(k, g, d, nm, nv):
        out_g[k], out_d[k], out_m[k], out_v[k] = unblock(k, g), unblock(k, d), unblock(k, nm), unblock(k, nv)

    for names, got in zip(_SCATTER_PLAN, (got_hg, got_ml)):
        for k, rcv in zip(names, got):
            finish(k, *_reduce_adamw(rcv, block(k, wts[k]), block(k, mom[k]), block(k, var[k]), "reduce_adamw_" + k))

    core_sums = _pair_reduce_cols(_to_chip_major(grads["w_in"], False), "pair_reduce_w_in")
    whole = _chip_reduce_swap_cols(_scatter_chips([core_sums], "scatter_grad_w_in")[0], "chip_reduce_w_in")
    finish("w_in", *_adamw(whole[None], block("w_in", wts["w_in"]), block("w_in", mom["w_in"]), block("w_in", var["w_in"]), "adamw_w_in"))

    small_shapes = [(1, PROJ_W), (2, MIX_W), (CONV_K, MIX_W)] + [(1, MIX_W)] * 3 + [(1, D_MODEL)] * 5 + [(1, 1)]
    contrib = _pack([grads[k] for k in _SMALL] + [loss])
    summed = _sum_slots(_gather_all(contrib, "gather_small"), "sum_small")
    small = _unpack(summed, small_shapes)
    loss_total = small[-1].reshape(())
    gsm = dict(zip(_SMALL, small[:-1]))
    place = 2 * lax.axis_index("x") + lax.axis_index("y")
    conv_cols = ml_conv_w.shape[-1]
    gsm["ml_conv_w"] = lax.dynamic_slice(gsm["ml_conv_w"], (0, place * conv_cols), (CONV_K, conv_cols))
    own_shapes = [wts[k].shape for k in _SMALL]
    g_pack = _pack([gsm[k] for k in _SMALL])
    res = _adamw(g_pack[None], _pack([wts[k] for k in _SMALL]), _pack([mom[k] for k in _SMALL]), _pack([var[k] for k in _SMALL]), "adamw_small")
    for dst, pack in zip((out_g, out_d, out_m, out_v), res):
        for k, a in zip(_SMALL, _unpack(pack, own_shapes)):
            dst[k] = a

    outs = [loss_total, grad_x[None]]
    for group in (out_g, out_d, out_m, out_v):
        outs += [group[k] for k in _ORDER]
    return tuple(outs)
```

```python
import jax
import jax.numpy as jnp
from jax import lax
from jax.experimental import pallas as pl
from jax.experimental.pallas import tpu as pltpu

f32 = jnp.float32
bf16 = jnp.bfloat16

D_MODEL = 1024
HEADS = 4
HEAD_W = 128
MIX_W = HEADS * HEAD_W
ML_DQK = 64
PROJ_W = 3592
U_HG = 4 * MIX_W
U_ML = 3 * MIX_W + 128
D_FF = 2816
PLE = 256
CHUNK = 128
SUB = 16
EXP_CAP = 80.0
CONV_K = 4
HALO = 8
ALPHA = float(2.0 ** 0.25)
LN_EPS = 1e-5
RMS_EPS = 1e-6
NEG = -1e30
LR, B1, B2, EPS_ADAM, WD, STEP = 0.001, 0.9, 0.999, 1e-08, 0.01, 10
VMEM_LIMIT = 56 * 1024 * 1024
MIXER_ROWS = 512
DENSE_ROWS = 512
WGRAD_ROWS = 2048


def _cparams(n_axes, arbitrary=True):
    sem = ("arbitrary",) * n_axes if arbitrary else ("parallel",) * n_axes
    return pltpu.CompilerParams(dimension_semantics=sem, vmem_limit_bytes=VMEM_LIMIT)


ACT = bf16


def _mx(a):
    return a.astype(ACT)


def _bdot(a, b):
    return jnp.dot(_mx(a), _mx(b), preferred_element_type=f32)


def _bdot_nt(a, b):
    return lax.dot_general(_mx(a), _mx(b), (((1,), (1,)), ((), ())), preferred_element_type=f32)


def _bdot_tn(a, b):
    return lax.dot_general(_mx(a), _mx(b), (((0,), (0,)), ((), ())), preferred_element_type=f32)


def _split3(x):
    hi = x.astype(bf16)
    r1 = x - hi.astype(f32)
    mid = r1.astype(bf16)
    lo = (r1 - mid.astype(f32)).astype(bf16)
    return hi, mid, lo


def _dot3(a, b, dims):
    a_hi = a.astype(bf16)
    a_lo = (a - a_hi.astype(f32)).astype(bf16)
    b_hi = b.astype(bf16)
    b_lo = (b - b_hi.astype(f32)).astype(bf16)
    dn = (dims, ((), ()))
    return (lax.dot_general(a_hi, b_hi, dn, preferred_element_type=f32) + lax.dot_general(a_hi, b_lo, dn, preferred_element_type=f32)
            + lax.dot_general(a_lo, b_hi, dn, preferred_element_type=f32))


def _lane_sum(x):
    hi = x.astype(bf16)
    lo = (x - hi.astype(f32)).astype(bf16)
    ones = jnp.ones((x.shape[1], 128), bf16)
    return jnp.dot(hi, ones, preferred_element_type=f32) + jnp.dot(lo, ones, preferred_element_type=f32)


def _lane_dot(x, row):
    return _dot3(x, jnp.broadcast_to(row, (128, row.shape[1])), ((1,), (1,)))


def _sel_dot(sel, x):
    sb = sel.astype(bf16)
    return sum(jnp.dot(sb, part, preferred_element_type=f32) for part in _split3(x))


def _sel_dot_nt(sel, x):
    sb = sel.astype(bf16)
    return sum(lax.dot_general(sb, part, (((1,), (1,)), ((), ())), preferred_element_type=f32) for part in _split3(x))


def _sigmoid(x):
    return 1.0 / (1.0 + jnp.exp(-x))


def _log_sigmoid(x):
    return jnp.minimum(x, 0.0) - jnp.log(1.0 + jnp.exp(-jnp.abs(x)))


def _tri(n, upper=False):
    r = lax.broadcasted_iota(jnp.int32, (n, n), 0)
    c = lax.broadcasted_iota(jnp.int32, (n, n), 1)
    return (c >= r) if upper else (c <= r)


def _rows(tm, n, col=0):
    return pl.BlockSpec((tm, n), lambda i, _c=col: (i, _c))


def _rows_rev(tm, n, nb, col=0):
    return pl.BlockSpec((tm, n), lambda i, _c=col, _nb=nb: (_nb - 1 - i, _c))


def _const(shape):
    return pl.BlockSpec(shape, lambda i, _n=len(shape): (0,) * _n)


def _resident(shape):
    return pl.BlockSpec(shape, lambda i, _n=len(shape): (0,) * _n, pipeline_mode=pl.Buffered(1))


def _tile(t, want):
    return want if t % want == 0 else t


def _inproj(x, w_hg, w_ml, b_hg, b_ml, riders=()):
    t = x.shape[0]
    tm = _tile(t, DENSE_ROWS)

    def body(x_ref, whg_ref, wml_ref, bhg_ref, bml_ref, uhg_ref, uml_ref, xb_ref):
        xb = _mx(x_ref[...])
        xb_ref[...] = xb
        uhg_ref[...] = _bdot_nt(xb, whg_ref[...]) + bhg_ref[...]
        uml_ref[...] = _bdot_nt(xb, wml_ref[...]) + bml_ref[...]

    return _riding_call(
        body, "inproj", t // tm,
        in_specs=[_rows(tm, D_MODEL), _resident((U_HG, D_MODEL)), _resident((U_ML, D_MODEL)), _const((1, U_HG)), _const((1, U_ML))],
        out_specs=[_rows(tm, U_HG), _rows(tm, U_ML), _rows(tm, D_MODEL)],
        out_shape=[jax.ShapeDtypeStruct((t, U_HG), f32), jax.ShapeDtypeStruct((t, U_ML), f32), jax.ShapeDtypeStruct((t, D_MODEL), ACT)],
        scratch_shapes=[], operands=(x, w_hg, w_ml, b_hg, b_ml), riders=riders, copies=_gather_copies, ride_shapes=_gather_shapes(riders))


def _hg_gates(hq, hf, lb, tri, b=None):
    s = _sigmoid(hf)
    om = 1.0 - lb
    f = lb + om * s
    k = om * (1.0 - s)
    sq = _sigmoid(hq)
    q = hq * sq
    if b is None:
        b = _sel_dot(tri, jnp.log(f))
    return q, sq, s, f, k, b


def _hg_scores(q, k, b, tril_mask, a=None):
    qts, kts, eqs, eks, rows = [], [], [], [], []
    for i in range(CHUNK // SUB):
        lo = i * SUB
        ref = jnp.zeros_like(b[0:1]) if i == 0 else b[lo - 1:lo]
        eq = jnp.exp(b[lo:lo + SUB] - ref)
        ek = jnp.exp(jnp.minimum(ref - b, EXP_CAP))
        qt = q[lo:lo + SUB] * eq
        kt = k * ek
        if a is None:
            rows.append(_bdot_nt(qt, kt))
        qts.append(qt); kts.append(kt); eqs.append(eq); eks.append(ek)
    if a is None:
        a = jnp.where(tril_mask, jnp.concatenate(rows, axis=0), 0.0)
    return a, qts, kts, eqs, eks


def _head_rms(o, gn, on_mxu=False):
    ms = _lane_sum(o * o) * (1.0 / o.shape[1]) if on_mxu else jnp.mean(o * o, axis=-1, keepdims=True)
    rstd = lax.rsqrt(ms + RMS_EPS)
    oh = o * rstd
    return oh, rstd, oh * gn


def _lower_bound(logit_ref):
    lg = logit_ref[...]
    return _sigmoid(lg[0:1] - lg[1:2])


def _hgrn2_fwd(u_hg, logits, gn, riders=()):
    t = u_hg.shape[0]
    tb = _tile(t, 2 * MIXER_ROWS)
    nc_blk = tb // CHUNK

    def body(u_ref, lg_ref, gn_ref, og_ref, sst_ref, b_ref, a_ref, o_ref, st_ref):
        @pl.when(pl.program_id(0) == 0)
        def _():
            st_ref[...] = jnp.zeros_like(st_ref)

        lb_all = _lower_bound(lg_ref)
        tril_mask = _tri(CHUNK)
        tri = tril_mask.astype(f32)

        def chunk(c, carry):
            r0 = pl.multiple_of(c * CHUNK, CHUNK)
            rows = pl.ds(r0, CHUNK)
            heads = range(HEADS)
            cols = [slice(h * HEAD_W, (h + 1) * HEAD_W) for h in heads]
            hv = [u_ref[rows, 2 * MIX_W + h * HEAD_W:2 * MIX_W + (h + 1) * HEAD_W] for h in heads]
            gts = [_hg_gates(u_ref[rows, h * HEAD_W:(h + 1) * HEAD_W], u_ref[rows, MIX_W + h * HEAD_W:MIX_W + (h + 1) * HEAD_W],
                             lb_all[:, cols[h]], tri) for h in heads]
            q = [g[0] for g in gts]
            k = [g[4] for g in gts]
            b = [g[5] for g in gts]
            a = [_hg_scores(q[h], k[h], b[h], tril_mask)[0] for h in heads]
            st = [st_ref[h] for h in heads]
            bl = [b[h][CHUNK - 1:CHUNK] for h in heads]
            o = [_bdot(a[h], hv[h]) + _bdot_nt(q[h] * jnp.exp(b[h]), st[h]) for h in heads]
            new_st = [st[h] * jnp.exp(bl[h]) + _bdot_tn(hv[h], k[h] * jnp.exp(bl[h] - b[h])) for h in heads]
            for h in heads:
                sst_ref[c, h] = st[h]
                st_ref[h] = new_st[h]
                b_ref[rows, cols[h]] = b[h]
                a_ref[rows, cols[h]] = a[h].astype(ACT)
                o_ref[rows, cols[h]] = o[h]
                hgate = u_ref[rows, 3 * MIX_W + h * HEAD_W:3 * MIX_W + (h + 1) * HEAD_W]
                _, _, y = _head_rms(o[h], gn_ref[:, cols[h]])
                og_ref[rows, cols[h]] = (y * (hgate * _sigmoid(hgate))).astype(ACT)
            return carry

        lax.fori_loop(0, nc_blk, chunk, 0, unroll=True)

    assert CHUNK == HEAD_W
    return _riding_call(
        body, "hgrn2_fwd", t // tb,
        in_specs=[_rows(tb, U_HG), _const((2, MIX_W)), _const((1, MIX_W))],
        out_specs=[_rows(tb, MIX_W), pl.BlockSpec((nc_blk, HEADS, HEAD_W, HEAD_W), lambda i: (i, 0, 0, 0)),
                   _rows(tb, MIX_W), _rows(tb, MIX_W), _rows(tb, MIX_W)],
        out_shape=[jax.ShapeDtypeStruct((t, MIX_W), ACT), jax.ShapeDtypeStruct((t // CHUNK, HEADS, HEAD_W, HEAD_W), f32),
                   jax.ShapeDtypeStruct((t, MIX_W), f32), jax.ShapeDtypeStruct((t, MIX_W), ACT), jax.ShapeDtypeStruct((t, MIX_W), f32)],
        scratch_shapes=[pltpu.VMEM((HEADS, HEAD_W, HEAD_W), f32)],
        operands=(u_hg, logits, gn), riders=riders, copies=_gather_copies, ride_shapes=_gather_shapes(riders))


def _hgrn2_bwd(u_hg, logits, gn, sst, bcum, scores, o_raw, dog, riders=()):
    t = u_hg.shape[0]
    tb = _tile(t, MIXER_ROWS)
    nb = t // tb
    nc_blk = tb // CHUNK

    def body(u_ref, lg_ref, gn_ref, sst_ref, b_ref, a_ref, o_ref, dog_ref, du_ref, dlg_ref, dgn_ref, dst_ref):
        @pl.when(pl.program_id(0) == 0)
        def _():
            dst_ref[...] = jnp.zeros_like(dst_ref)
            dlg_ref[...] = jnp.zeros_like(dlg_ref)
            dgn_ref[...] = jnp.zeros_like(dgn_ref)

        lb_all = _lower_bound(lg_ref)
        tril_mask = _tri(CHUNK)
        tri = tril_mask.astype(f32)
        triu = _tri(CHUNK, upper=True).astype(f32)

        def chunk(j, carry):
            c = nc_blk - 1 - j
            r0 = pl.multiple_of(c * CHUNK, CHUNK)
            rows = pl.ds(r0, CHUNK)
            heads = range(HEADS)
            nsub = CHUNK // SUB
            cols = [slice(h * HEAD_W, (h + 1) * HEAD_W) for h in heads]
            hq = [u_ref[rows, h * HEAD_W:(h + 1) * HEAD_W] for h in heads]
            hf = [u_ref[rows, MIX_W + h * HEAD_W:MIX_W + (h + 1) * HEAD_W] for h in heads]
            hv = [u_ref[rows, 2 * MIX_W + h * HEAD_W:2 * MIX_W + (h + 1) * HEAD_W] for h in heads]
            lb = [lb_all[:, cols[h]] for h in heads]
            gts = [_hg_gates(hq[h], hf[h], lb[h], tri, b=b_ref[rows, cols[h]]) for h in heads]
            q, sq, s, f, k, b = ([g[n] for g in gts] for n in range(6))
            scs = [_hg_scores(q[h], k[h], b[h], tril_mask, a=a_ref[rows, cols[h]]) for h in heads]
            a, qts, kts, eqs, eks = ([sc[n] for sc in scs] for n in range(5))
            st = [sst_ref[c, h] for h in heads]
            dst = [dst_ref[h] for h in heads]
            bl = [b[h][CHUNK - 1:CHUNK] for h in heads]
            eb = [jnp.exp(b[h]) for h in heads]
            qh = [q[h] * eb[h] for h in heads]
            ekl = [jnp.exp(bl[h] - b[h]) for h in heads]
            kh = [k[h] * ekl[h] for h in heads]
            o = [o_ref[rows, cols[h]] for h in heads]
            do = []
            for h in heads:
                hgate = u_ref[rows, 3 * MIX_W + h * HEAD_W:3 * MIX_W + (h + 1) * HEAD_W]
                gnh = gn_ref[:, cols[h]]
                oh, rstd, y = _head_rms(o[h], gnh)
                sg = _sigmoid(hgate)
                dogh = dog_ref[rows, cols[h]]
                dy = dogh * (hgate * sg)
                du_ref[rows, 3 * MIX_W + h * HEAD_W:3 * MIX_W + (h + 1) * HEAD_W] = (dogh * y * (sg * (1.0 + hgate * (1.0 - sg)))).astype(ACT)
                dgn_ref[:, cols[h]] += jnp.sum(dy * oh, axis=0, keepdims=True)
                doh = dy * gnh
                do.append(rstd * (doh - oh * jnp.mean(doh * oh, axis=-1, keepdims=True)))
            da = [jnp.where(tril_mask, _bdot_nt(do[h], hv[h]), 0.0) for h in heads]
            dv = [_bdot_tn(a[h], do[h]) + _bdot_nt(kh[h], dst[h]) for h in heads]
            dq = [_bdot(do[h], st[h]) * eb[h] for h in heads]
            dk = [_bdot(hv[h], dst[h]) * ekl[h] for h in heads]
            d_last = [jnp.sum(k[h] * dk[h], axis=0, keepdims=True) + jnp.exp(bl[h]) * jnp.sum(dst[h] * st[h], axis=0, keepdims=True)
                      for h in heads]
            d_b = [q[h] * dq[h] - k[h] * dk[h] for h in heads]
            dqs = [[] for _ in heads]
            q_dq = [[] for _ in heads]
            for i in range(nsub):
                for h in heads:
                    da_i = _mx(da[h][i * SUB:(i + 1) * SUB])
                    q_r, k_r = _mx(qts[h][i]), _mx(kts[h][i])
                    g_q = jnp.dot(da_i, k_r, preferred_element_type=f32)
                    g_k = lax.dot_general(da_i, q_r, (((0,), (0,)), ((), ())), preferred_element_type=f32)
                    dqs[h].append(g_q * eqs[h][i])
                    q_dq[h].append(q_r.astype(f32) * g_q)
                    dk[h] = dk[h] + g_k * eks[h][i]
                    d_b[h] = d_b[h] - k_r.astype(f32) * g_k
            for h in heads:
                dq[h] = dq[h] + jnp.concatenate(dqs[h], axis=0)
                d_b[h] = d_b[h] + jnp.concatenate(q_dq[h], axis=0)
                dst_ref[h] = dst[h] * jnp.exp(bl[h]) + _bdot_tn(do[h], qh[h])
            dg = [_sel_dot(triu, d_b[h]) + d_last[h] for h in heads]
            for h in heads:
                dfk = dg[h] / f[h] - dk[h]
                du_ref[rows, h * HEAD_W:(h + 1) * HEAD_W] = (dq[h] * (sq[h] * (1.0 + hq[h] * (1.0 - sq[h])))).astype(ACT)
                du_ref[rows, MIX_W + h * HEAD_W:MIX_W + (h + 1) * HEAD_W] = ((1.0 - lb[h]) * dfk * s[h] * (1.0 - s[h])).astype(ACT)
                du_ref[rows, 2 * MIX_W + h * HEAD_W:2 * MIX_W + (h + 1) * HEAD_W] = dv[h].astype(ACT)
                dlb = jnp.sum((1.0 - s[h]) * dfk, axis=0, keepdims=True) * (lb[h] * (1.0 - lb[h]))
                dlg_ref[0:1, cols[h]] += dlb
                dlg_ref[1:2, cols[h]] -= dlb
            return carry

        lax.fori_loop(0, nc_blk, chunk, 0, unroll=True)

    rev = _rows_rev(tb, MIX_W, nb)
    return _riding_call(
        body, "hgrn2_bwd", nb,
        in_specs=[_rows_rev(tb, U_HG, nb), _const((2, MIX_W)), _const((1, MIX_W)),
                  pl.BlockSpec((nc_blk, HEADS, HEAD_W, HEAD_W), lambda i: (nb - 1 - i, 0, 0, 0)), rev, rev, rev, rev],
        out_specs=[_rows_rev(tb, U_HG, nb), _const((2, MIX_W)), _const((1, MIX_W))],
        out_shape=[jax.ShapeDtypeStruct((t, U_HG), ACT), jax.ShapeDtypeStruct((2, MIX_W), f32), jax.ShapeDtypeStruct((1, MIX_W), f32)],
        scratch_shapes=[pltpu.VMEM((HEADS, HEAD_W, HEAD_W), f32)],
        operands=(u_hg, logits, gn, sst, bcum, scores, o_raw, dog), riders=riders, copies=_scatter_copies,
        ride_shapes=[jax.ShapeDtypeStruct(r.shape, r.dtype) for r in riders])


def _conv_fwd(u_ml, w, b):
    t = u_ml.shape[0]
    tm = _tile(t, 512)

    def body(x_ref, w_ref, b_ref, pre_ref, act_ref, xbuf):
        @pl.when(pl.program_id(0) == 0)
        def _():
            xbuf[...] = jnp.zeros_like(xbuf)

        xbuf[0:HALO, :] = xbuf[tm:tm + HALO, :]
        xbuf[HALO:HALO + tm, :] = x_ref[...]
        pre = b_ref[...] + jnp.zeros((tm, MIX_W), f32)
        for kk in range(CONV_K):
            off = HALO - (CONV_K - 1) + kk
            pre = pre + w_ref[kk:kk + 1, :] * xbuf[off:off + tm, :]
        pre_ref[...] = pre
        act_ref[...] = pre * _sigmoid(pre)

    return pl.pallas_call(
        body, name="conv_fwd", grid=(t // tm,),
        in_specs=[_rows(tm, MIX_W), _const((CONV_K, MIX_W)), _const((1, MIX_W))],
        out_specs=[_rows(tm, MIX_W), _rows(tm, MIX_W)],
        out_shape=[jax.ShapeDtypeStruct((t, MIX_W), f32)] * 2,
        scratch_shapes=[pltpu.VMEM((tm + HALO, MIX_W), f32)],
        compiler_params=_cparams(1),
    )(u_ml, w, b)


def _conv_bwd(u_ml, w, pre, dact):
    t = u_ml.shape[0]
    tm = _tile(t, 512)
    nb = t // tm
    hb = tm // HALO

    def body(x_ref, halo_ref, w_ref, pre_ref, dact_ref, dx_ref, dw_ref, db_ref, dbuf, xbuf):
        i = pl.program_id(0)

        @pl.when(i == 0)
        def _():
            dbuf[...] = jnp.zeros_like(dbuf)
            dw_ref[...] = jnp.zeros_like(dw_ref)
            db_ref[...] = jnp.zeros_like(db_ref)

        p = pre_ref[...]
        sg = _sigmoid(p)
        dpre = dact_ref[...] * (sg * (1.0 + p * (1.0 - sg)))
        dbuf[tm:tm + HALO, :] = dbuf[0:HALO, :]
        dbuf[0:tm, :] = dpre
        has_prev = (i < nb - 1).astype(f32)
        xbuf[0:HALO, :] = halo_ref[...] * has_prev
        xbuf[HALO:HALO + tm, :] = x_ref[...]
        dx = jnp.zeros((tm, MIX_W), f32)
        for kk in range(CONV_K):
            back = CONV_K - 1 - kk
            dx = dx + w_ref[kk:kk + 1, :] * dbuf[back:back + tm, :]
            off = HALO - (CONV_K - 1) + kk
            dw_ref[kk:kk + 1, :] += jnp.sum(dpre * xbuf[off:off + tm, :], axis=0, keepdims=True)
        dx_ref[...] = dx.astype(ACT)
        db_ref[...] += jnp.sum(dpre, axis=0, keepdims=True)

    return pl.pallas_call(
        body, name="conv_bwd", grid=(nb,),
        in_specs=[_rows_rev(tm, MIX_W, nb),
                  pl.BlockSpec((HALO, MIX_W), lambda i: (jnp.maximum((nb - 1 - i) * hb - 1, 0), 0)),
                  _const((CONV_K, MIX_W)), _rows_rev(tm, MIX_W, nb), _rows_rev(tm, MIX_W, nb)],
        out_specs=[_rows_rev(tm, MIX_W, nb), _const((CONV_K, MIX_W)), _const((1, MIX_W))],
        out_shape=[jax.ShapeDtypeStruct((t, MIX_W), ACT), jax.ShapeDtypeStruct((CONV_K, MIX_W), f32), jax.ShapeDtypeStruct((1, MIX_W), f32)],
        scratch_shapes=[pltpu.VMEM((tm + HALO, MIX_W), f32), pltpu.VMEM((tm + HALO, MIX_W), f32)],
        compiler_params=_cparams(1),
    )(u_ml, u_ml, w, pre, dact)


def _lane_pick(x, lane):
    idx = lax.broadcasted_iota(jnp.int32, x.shape, 1)
    return jnp.sum(jnp.where(idx == lane, x, 0.0), axis=-1, keepdims=True)


def _ml_gate_forms(gates, tri):
    lf = _log_sigmoid(gates)
    gc = _sel_dot(tri, lf)
    lane = lax.broadcasted_iota(jnp.int32, gates.shape, 1)
    mixed = jnp.where(lane < HEADS, gates, gc)
    sel = (lax.broadcasted_iota(jnp.int32, (8, 128), 0) == lax.broadcasted_iota(jnp.int32, (8, 128), 1)).astype(f32)
    rowsf = _sel_dot_nt(sel, mixed)
    return gc, rowsf


def _ml_chunk(q, k, v, gates, gc, rowsf, c_st, n_st, m_st, tril_mask):
    hs = range(HEADS)
    g_col = [_lane_pick(gc, HEADS + h) for h in hs]
    ig_col = [_lane_pick(gates, h) for h in hs]
    dmat = [jnp.where(tril_mask, g_col[h] - rowsf[HEADS + h:HEADS + h + 1, :] + rowsf[h:h + 1, :], NEG) for h in hs]
    m_inter = [g_col[h] + m_st[h] for h in hs]
    m_t = [jnp.maximum(m_inter[h], jnp.max(dmat[h], axis=-1, keepdims=True)) for h in hs]
    wi = [jnp.exp(dmat[h] - m_t[h]) for h in hs]
    wo = [jnp.exp(m_inter[h] - m_t[h]) for h in hs]
    qk = [_bdot_nt(q[h], k[h]) * wi[h] for h in hs]
    num = [_bdot(qk[h], v[h]) + wo[h] * _bdot(q[h], c_st[h]) for h in hs]
    den = [_lane_sum(qk[h]) + wo[h] * _lane_dot(q[h], n_st[h]) for h in hs]
    floor = [jnp.exp(-m_t[h]) for h in hs]
    z = [jnp.maximum(jnp.abs(den[h]), floor[h]) for h in hs]
    g_last = [g_col[h][CHUNK - 1:CHUNK] for h in hs]
    a_col = [g_last[h] - g_col[h] + ig_col[h] for h in hs]
    m_new = [jnp.maximum(g_last[h] + m_st[h], jnp.max(a_col[h], axis=0, keepdims=True)) for h in hs]
    ws = [jnp.exp(a_col[h] - m_new[h]) for h in hs]
    w_old = [jnp.exp(g_last[h] + m_st[h] - m_new[h]) for h in hs]
    return dict(wi=wi, wo=wo, qk=qk, num=num, den=den, z=z, floor=floor, ws=ws, w_old=w_old, m_new=m_new)


def _mlstm_fwd(qkc, u_ml, gn, riders=()):
    t = qkc.shape[0]
    tb = _tile(t, MIXER_ROWS)
    nc_blk = tb // CHUNK

    def body(qk_ref, v_ref, mo_ref, gt_ref, gn_ref, og_ref, cst_ref, nst_ref, mst_ref, c_sc, n_sc, m_sc):
        @pl.when(pl.program_id(0) == 0)
        def _():
            c_sc[...] = jnp.zeros_like(c_sc)
            n_sc[...] = jnp.zeros_like(n_sc)
            m_sc[...] = jnp.zeros_like(m_sc)

        tril_mask = _tri(CHUNK)
        tri = tril_mask.astype(f32)

        def chunk(c, carry):
            r0 = pl.multiple_of(c * CHUNK, CHUNK)
            rows = pl.ds(r0, CHUNK)
            gates = gt_ref[rows, :]
            gc, rowsf = _ml_gate_forms(gates, tri)
            hs = range(HEADS)
            q = [qk_ref[rows, h * ML_DQK:(h + 1) * ML_DQK] * (ML_DQK ** -0.5) for h in hs]
            k = [qk_ref[rows, HEADS * ML_DQK + h * ML_DQK:HEADS * ML_DQK + (h + 1) * ML_DQK] for h in hs]
            v = [v_ref[rows, h * HEAD_W:(h + 1) * HEAD_W] for h in hs]
            c_st = [c_sc[h] for h in hs]
            n_st = [n_sc[h] for h in hs]
            m_full = [m_sc[h] for h in hs]
            r = _ml_chunk(q, k, v, gates, gc, rowsf, c_st, n_st, [m[:, 0:1] for m in m_full], tril_mask)
            ksc = [k[h] * r["ws"][h] for h in hs]
            new_c = [r["w_old"][h] * c_st[h] + _bdot_tn(ksc[h], v[h]) for h in hs]
            for h in hs:
                cs = slice(h * HEAD_W, (h + 1) * HEAD_W)
                cst_ref[c, h] = c_st[h]
                nst_ref[c, h] = n_st[h]
                mst_ref[c, h] = m_full[h]
                c_sc[h] = new_c[h]
                n_sc[h] = r["w_old"][h] * n_st[h] + jnp.sum(ksc[h], axis=0, keepdims=True)
                m_sc[h] = r["m_new"][h] + jnp.zeros((1, 128), f32)
                _, _, y = _head_rms(r["num"][h] / r["z"][h], gn_ref[:, cs], on_mxu=True)
                og_ref[rows, cs] = (y * _sigmoid(mo_ref[rows, h * HEAD_W:(h + 1) * HEAD_W])).astype(ACT)
            return carry

        lax.fori_loop(0, nc_blk, chunk, 0, unroll=True)

    nchunks = t // CHUNK
    return _riding_call(
        body, "mlstm_fwd", t // tb,
        in_specs=[_rows(tb, MIX_W), _rows(tb, MIX_W, 1), _rows(tb, MIX_W, 2), _rows(tb, 128, 12), _const((1, MIX_W))],
        out_specs=[_rows(tb, MIX_W),
                   pl.BlockSpec((nc_blk, HEADS, ML_DQK, HEAD_W), lambda i: (i, 0, 0, 0)),
                   pl.BlockSpec((nc_blk, HEADS, 1, ML_DQK), lambda i: (i, 0, 0, 0)),
                   pl.BlockSpec((nc_blk, HEADS, 1, 128), lambda i: (i, 0, 0, 0))],
        out_shape=[jax.ShapeDtypeStruct((t, MIX_W), ACT),
                   jax.ShapeDtypeStruct((nchunks, HEADS, ML_DQK, HEAD_W), f32),
                   jax.ShapeDtypeStruct((nchunks, HEADS, 1, ML_DQK), f32),
                   jax.ShapeDtypeStruct((nchunks, HEADS, 1, 128), f32)],
        scratch_shapes=[pltpu.VMEM((HEADS, ML_DQK, HEAD_W), f32), pltpu.VMEM((HEADS, 1, ML_DQK), f32), pltpu.VMEM((HEADS, 1, 128), f32)],
        operands=(qkc, u_ml, u_ml, u_ml, gn), riders=riders, copies=_gather_copies, ride_shapes=_gather_shapes(riders))


def _mlstm_bwd(qkc, u_ml, gn, cst, nst, mst, dog, riders=()):
    t = qkc.shape[0]
    tb = _tile(t, 2 * MIXER_ROWS)
    nb = t // tb
    nc_blk = tb // CHUNK

    def body(qk_ref, v_ref, mo_ref, gt_ref, gn_ref, cst_ref, nst_ref, mst_ref, dog_ref,
             dqk_ref, dv_ref, dmo_ref, dgt_ref, dgn_ref, dc_sc, dn_sc):
        @pl.when(pl.program_id(0) == 0)
        def _():
            dc_sc[...] = jnp.zeros_like(dc_sc)
            dn_sc[...] = jnp.zeros_like(dn_sc)
            dgn_ref[...] = jnp.zeros_like(dgn_ref)

        tril_mask = _tri(CHUNK)
        tri = tril_mask.astype(f32)
        triu = _tri(CHUNK, upper=True).astype(f32)
        lane = lax.broadcasted_iota(jnp.int32, (CHUNK, 128), 1)

        def chunk(j, carry):
            c = nc_blk - 1 - j
            r0 = pl.multiple_of(c * CHUNK, CHUNK)
            rows = pl.ds(r0, CHUNK)
            gates = gt_ref[rows, :]
            gc, rowsf = _ml_gate_forms(gates, tri)
            dg_mat = jnp.zeros((CHUNK, 128), f32)
            dig_mat = jnp.zeros((CHUNK, 128), f32)
            dlast_row = jnp.zeros((1, 128), f32)
            hs = range(HEADS)
            cols = [slice(h * HEAD_W, (h + 1) * HEAD_W) for h in hs]
            q = [qk_ref[rows, h * ML_DQK:(h + 1) * ML_DQK] * (ML_DQK ** -0.5) for h in hs]
            k = [qk_ref[rows, HEADS * ML_DQK + h * ML_DQK:HEADS * ML_DQK + (h + 1) * ML_DQK] for h in hs]
            v = [v_ref[rows, h * HEAD_W:(h + 1) * HEAD_W] for h in hs]
            c_st = [cst_ref[c, h] for h in hs]
            n_st = [nst_ref[c, h] for h in hs]
            m_st = [mst_ref[c, h][:, 0:1] for h in hs]
            dc = [dc_sc[h] for h in hs]
            dn = [dn_sc[h] for h in hs]
            r = _ml_chunk(q, k, v, gates, gc, rowsf, c_st, n_st, m_st, tril_mask)
            z, wi, wo, ws, w_old, den = r["z"], r["wi"], r["wo"], r["ws"], r["w_old"], r["den"]
            hh = [r["num"][h] / z[h] for h in hs]
            dh = []
            for h in hs:
                gnh = gn_ref[:, cols[h]]
                oh, rstd, y = _head_rms(hh[h], gnh, on_mxu=True)
                sg = _sigmoid(mo_ref[rows, h * HEAD_W:(h + 1) * HEAD_W])
                dogh = dog_ref[rows, cols[h]]
                dy = dogh * sg
                dmo_ref[rows, cols[h]] = (dogh * y * (sg * (1.0 - sg))).astype(ACT)
                dgn_ref[:, cols[h]] += jnp.sum(dy * oh, axis=0, keepdims=True)
                doh = dy * gnh
                dh.append(rstd * (doh - oh * (_lane_sum(doh * oh) * (1.0 / HEAD_W))))
            dnum = [dh[h] / z[h] for h in hs]
            dz = [-_lane_sum(dh[h] * hh[h]) / z[h] for h in hs]
            dden = [jnp.where(jnp.abs(den[h]) > r["floor"][h], dz[h] * jnp.sign(den[h]), 0.0) for h in hs]
            dsw = [(_bdot_nt(dnum[h], v[h]) + dden[h]) * wi[h] for h in hs]
            dq = [_bdot(dsw[h], k[h]) + wo[h] * (_bdot_nt(dnum[h], c_st[h]) + dden[h][:, :ML_DQK] * n_st[h]) for h in hs]
            dk_state = [ws[h] * (_bdot_nt(v[h], dc[h]) + dn[h]) for h in hs]
            dk = [_bdot_tn(dsw[h], q[h]) + dk_state[h] for h in hs]
            dv = [_bdot_tn(r["qk"][h], dnum[h]) + ws[h] * _bdot(k[h], dc[h]) for h in hs]
            woq = [wo[h] * q[h] for h in hs]
            new_dc = [w_old[h] * dc[h] + _bdot_tn(woq[h], dnum[h]) for h in hs]
            for h in hs:
                dv_ref[rows, cols[h]] = dv[h].astype(ACT)
                dc_sc[h] = new_dc[h]
                dn_sc[h] = w_old[h] * dn[h] + jnp.sum(woq[h] * dden[h][:, :ML_DQK], axis=0, keepdims=True)
                d_last = (jnp.sum(jnp.sum(k[h] * dk_state[h], axis=0, keepdims=True), axis=-1, keepdims=True)
                          + w_old[h] * (jnp.sum(jnp.sum(dc[h] * c_st[h], axis=0, keepdims=True), axis=-1, keepdims=True)
                                        + jnp.sum(dn[h] * n_st[h], axis=-1, keepdims=True)))
                kdk = _lane_sum(k[h] * dk[h])
                qdq = _lane_sum(q[h] * dq[h])
                dg_mat = dg_mat + jnp.where(lane == HEADS + h, qdq - kdk, 0.0)
                dlast_row = dlast_row + jnp.where(lane[0:1] == HEADS + h, d_last, 0.0)
                dig_mat = dig_mat + jnp.where(lane == h, kdk, 0.0)
                dqk_ref[rows, h * ML_DQK:(h + 1) * ML_DQK] = dq[h] * (ML_DQK ** -0.5)
                dqk_ref[rows, HEADS * ML_DQK + h * ML_DQK:HEADS * ML_DQK + (h + 1) * ML_DQK] = dk[h]
            dlf = _sel_dot(triu, dg_mat) + dlast_row
            dgt_ref[rows, :] = (dig_mat + dlf * _sigmoid(-gates)).astype(ACT)
            return carry

        lax.fori_loop(0, nc_blk, chunk, 0, unroll=True)

    st4 = lambda a, b: pl.BlockSpec((nc_blk, HEADS, a, b), lambda i: (nb - 1 - i, 0, 0, 0))
    return _riding_call(
        body, "mlstm_bwd", nb,
        in_specs=[_rows_rev(tb, MIX_W, nb), _rows_rev(tb, MIX_W, nb, 1), _rows_rev(tb, MIX_W, nb, 2), _rows_rev(tb, 128, nb, 12),
                  _const((1, MIX_W)), st4(ML_DQK, HEAD_W), st4(1, ML_DQK), st4(1, 128), _rows_rev(tb, MIX_W, nb)],
        out_specs=[_rows_rev(tb, MIX_W, nb), _rows_rev(tb, MIX_W, nb), _rows_rev(tb, MIX_W, nb), _rows_rev(tb, 128, nb), _const((1, MIX_W))],
        out_shape=[jax.ShapeDtypeStruct((t, MIX_W), f32), jax.ShapeDtypeStruct((t, MIX_W), ACT), jax.ShapeDtypeStruct((t, MIX_W), ACT),
                   jax.ShapeDtypeStruct((t, 128), ACT), jax.ShapeDtypeStruct((1, MIX_W), f32)],
        scratch_shapes=[pltpu.VMEM((HEADS, ML_DQK, HEAD_W), f32), pltpu.VMEM((HEADS, 1, ML_DQK), f32)],
        operands=(qkc, u_ml, u_ml, u_ml, gn, cst, nst, mst, dog), riders=riders, copies=_scatter_copies,
        ride_shapes=[jax.ShapeDtypeStruct(r.shape, r.dtype) for r in riders])


def _ln_fwd(r, g, b):
    mu = jnp.mean(r, axis=-1, keepdims=True)
    xc = r - mu
    rstd = lax.rsqrt(jnp.mean(xc * xc, axis=-1, keepdims=True) + LN_EPS)
    xh = xc * rstd
    return xh * g + b, xh, rstd


def _ln_bwd(dy, xh, rstd, g):
    dxh = dy * g
    return rstd * (dxh - jnp.mean(dxh, axis=-1, keepdims=True) - xh * jnp.mean(dxh * xh, axis=-1, keepdims=True))


def _outproj_ln1(og_hg, og_ml, x, w_out, g, b, riders=()):
    t = x.shape[0]
    tm = _tile(t, DENSE_ROWS)

    def body(a_ref, b_ref, x_ref, w_ref, g_ref, bb_ref, x1_ref, xh_ref, rs_ref, x1b_ref):
        mix = _bdot(a_ref[...], w_ref[0:MIX_W, :]) + _bdot(b_ref[...], w_ref[MIX_W:2 * MIX_W, :])
        y, xh, rstd = _ln_fwd(ALPHA * x_ref[...] + mix, g_ref[...], bb_ref[...])
        x1_ref[...] = y
        x1b_ref[...] = y.astype(ACT)
        xh_ref[...] = xh.astype(ACT)
        rs_ref[...] = rstd

    return _riding_call(
        body, "outproj_ln1", t // tm,
        in_specs=[_rows(tm, MIX_W), _rows(tm, MIX_W), _rows(tm, D_MODEL), _resident((D_MODEL, D_MODEL)), _const((1, D_MODEL)), _const((1, D_MODEL))],
        out_specs=[_rows(tm, D_MODEL), _rows(tm, D_MODEL), _rows(tm, 1), _rows(tm, D_MODEL)],
        out_shape=[jax.ShapeDtypeStruct((t, D_MODEL), f32), jax.ShapeDtypeStruct((t, D_MODEL), ACT), jax.ShapeDtypeStruct((t, 1), f32),
                   jax.ShapeDtypeStruct((t, D_MODEL), ACT)],
        scratch_shapes=[], operands=(og_hg, og_ml, x, w_out, g, b), riders=riders, copies=_gather_copies, ride_shapes=_gather_shapes(riders))


def _ffn_up(x1, wg, wu, riders=()):
    t = x1.shape[0]
    tm = _tile(t, DENSE_ROWS)

    def body(x_ref, wg_ref, wu_ref, hg_ref, up_ref, a_ref):
        xv = x_ref[...]
        hg = _bdot_nt(xv, wg_ref[...])
        up = _bdot_nt(xv, wu_ref[...])
        hg_ref[...] = hg.astype(ACT)
        up_ref[...] = up.astype(ACT)
        a_ref[...] = (hg * _sigmoid(hg) * up).astype(ACT)

    return _riding_call(
        body, "ffn_up", t // tm,
        in_specs=[_rows(tm, D_MODEL), _resident((D_FF, D_MODEL)), _resident((D_FF, D_MODEL))],
        out_specs=[_rows(tm, D_FF), _rows(tm, D_FF), _rows(tm, D_FF)],
        out_shape=[jax.ShapeDtypeStruct((t, D_FF), ACT), jax.ShapeDtypeStruct((t, D_FF), ACT), jax.ShapeDtypeStruct((t, D_FF), ACT)],
        scratch_shapes=[], operands=(x1, wg, wu), riders=riders, copies=_gather_copies, ride_shapes=_gather_shapes(riders))


def _ffn_down_ln2(a, x1, wd, g, b):
    t = x1.shape[0]
    tm = _tile(t, DENSE_ROWS)

    def body(a_ref, x_ref, w_ref, g_ref, bb_ref, x2_ref, xh_ref, rs_ref, x2b_ref):
        ffn = _bdot(a_ref[...], w_ref[...])
        y, xh, rstd = _ln_fwd(ALPHA * x_ref[...] + ffn, g_ref[...], bb_ref[...])
        x2_ref[...] = y
        x2b_ref[...] = y.astype(ACT)
        xh_ref[...] = xh.astype(ACT)
        rs_ref[...] = rstd

    return pl.pallas_call(
        body, name="ffn_down_ln2", grid=(t // tm,),
        in_specs=[_rows(tm, D_FF), _rows(tm, D_MODEL), _resident((D_FF, D_MODEL)), _const((1, D_MODEL)), _const((1, D_MODEL))],
        out_specs=[_rows(tm, D_MODEL), _rows(tm, D_MODEL), _rows(tm, 1), _rows(tm, D_MODEL)],
        out_shape=[jax.ShapeDtypeStruct((t, D_MODEL), f32), jax.ShapeDtypeStruct((t, D_MODEL), ACT), jax.ShapeDtypeStruct((t, 1), f32),
                   jax.ShapeDtypeStruct((t, D_MODEL), ACT)],
        compiler_params=_cparams(1, arbitrary=False),
    )(a, x1, wd, g, b)


def _head_loss_bwd(x2, xh2, rs2, p, tgt, w_pg, b_pg, w_pp, g2):
    t = x2.shape[0]
    tm = _tile(t, DENSE_ROWS)

    def body(x_ref, xh_ref, rs_ref, p_ref, t_ref, wg_ref, bg_ref, wp_ref, g_ref,
             dr_ref, de_ref, dz_ref, loss_ref, dbg_ref, dg2_ref, db2_ref):
        @pl.when(pl.program_id(0) == 0)
        def _():
            loss_ref[...] = jnp.zeros_like(loss_ref)
            dbg_ref[...] = jnp.zeros_like(dbg_ref)
            dg2_ref[...] = jnp.zeros_like(dg2_ref)
            db2_ref[...] = jnp.zeros_like(db2_ref)

        x2v = x_ref[...]
        z = _bdot(x2v, wg_ref[...]) + bg_ref[...]
        e = _bdot(p_ref[...], wp_ref[...])
        sg = _sigmoid(z)
        diff = x2v + sg * e - t_ref[...]
        loss_ref[...] += 0.5 * jnp.sum(jnp.mean(diff * diff, axis=-1, keepdims=True), axis=0, keepdims=True)
        dy = diff * (1.0 / D_MODEL)
        de_ref[...] = (dy * sg).astype(ACT)
        dz = dy * e * (sg * (1.0 - sg))
        dz_ref[...] = dz.astype(ACT)
        dbg_ref[...] += jnp.sum(dz, axis=0, keepdims=True)
        dx2 = dy + _bdot_nt(dz, wg_ref[...])
        xh = xh_ref[...].astype(f32)
        dg2_ref[...] += jnp.sum(dx2 * xh, axis=0, keepdims=True)
        db2_ref[...] += jnp.sum(dx2, axis=0, keepdims=True)
        dr_ref[...] = _ln_bwd(dx2, xh, rs_ref[...], g_ref[...])

    row = jax.ShapeDtypeStruct((1, D_MODEL), f32)
    return pl.pallas_call(
        body, name="head_loss_bwd", grid=(t // tm,),
        in_specs=[_rows(tm, D_MODEL), _rows(tm, D_MODEL), _rows(tm, 1), _rows(tm, PLE), _rows(tm, D_MODEL),
                  _resident((D_MODEL, D_MODEL)), _const((1, D_MODEL)), _resident((PLE, D_MODEL)), _const((1, D_MODEL))],
        out_specs=[_rows(tm, D_MODEL), _rows(tm, D_MODEL), _rows(tm, D_MODEL), _const((1, 1)), _const((1, D_MODEL)), _const((1, D_MODEL)), _const((1, D_MODEL))],
        out_shape=[jax.ShapeDtypeStruct((t, D_MODEL), f32), jax.ShapeDtypeStruct((t, D_MODEL), ACT), jax.ShapeDtypeStruct((t, D_MODEL), ACT),
                   jax.ShapeDtypeStruct((1, 1), f32), row, row, row],
        compiler_params=_cparams(1),
    )(x2, xh2, rs2, p, tgt, w_pg, b_pg, w_pp, g2)


def _ffn_bwd(dr2, hg, up, xh1, rs1, wd, wg, wu, g1, w_out):
    t = dr2.shape[0]
    tm = _tile(t, DENSE_ROWS // 2)

    def body(dr_ref, hg_ref, up_ref, xh_ref, rs_ref, wd_ref, wg_ref, wu_ref, g_ref, wo_ref,
             dr1_ref, dhg_ref, dup_ref, dg1_ref, db1_ref, doghg_ref, dogml_ref):
        @pl.when(pl.program_id(0) == 0)
        def _():
            dg1_ref[...] = jnp.zeros_like(dg1_ref)
            db1_ref[...] = jnp.zeros_like(db1_ref)

        dr2v = dr_ref[...]
        da = _bdot_nt(dr2v, wd_ref[...])
        hgv = hg_ref[...].astype(f32)
        sg = _sigmoid(hgv)
        dhg = da * up_ref[...].astype(f32) * (sg * (1.0 + hgv * (1.0 - sg)))
        dup = da * (hgv * sg)
        dhg_ref[...] = dhg.astype(ACT)
        dup_ref[...] = dup.astype(ACT)
        dx1 = ALPHA * dr2v + _bdot(dhg, wg_ref[...]) + _bdot(dup, wu_ref[...])
        xh = xh_ref[...].astype(f32)
        dg1_ref[...] += jnp.sum(dx1 * xh, axis=0, keepdims=True)
        db1_ref[...] += jnp.sum(dx1, axis=0, keepdims=True)
        dr1 = _ln_bwd(dx1, xh, rs_ref[...], g_ref[...])
        dr1_ref[...] = dr1
        dog = _bdot_nt(dr1, wo_ref[...])
        doghg_ref[...] = dog[:, 0:MIX_W]
        dogml_ref[...] = dog[:, MIX_W:2 * MIX_W]

    row = jax.ShapeDtypeStruct((1, D_MODEL), f32)
    return pl.pallas_call(
        body, name="ffn_bwd", grid=(t // tm,),
        in_specs=[_rows(tm, D_MODEL), _rows(tm, D_FF), _rows(tm, D_FF), _rows(tm, D_MODEL), _rows(tm, 1),
                  _resident((D_FF, D_MODEL)), _resident((D_FF, D_MODEL)), _resident((D_FF, D_MODEL)), _const((1, D_MODEL)),
                  _resident((D_MODEL, D_MODEL))],
        out_specs=[_rows(tm, D_MODEL), _rows(tm, D_FF), _rows(tm, D_FF), _const((1, D_MODEL)), _const((1, D_MODEL)),
                   _rows(tm, MIX_W), _rows(tm, MIX_W)],
        out_shape=[jax.ShapeDtypeStruct((t, D_MODEL), f32), jax.ShapeDtypeStruct((t, D_FF), ACT), jax.ShapeDtypeStruct((t, D_FF), ACT), row, row,
                   jax.ShapeDtypeStruct((t, MIX_W), f32), jax.ShapeDtypeStruct((t, MIX_W), f32)],
        compiler_params=_cparams(1),
    )(dr2, hg, up, xh1, rs1, wd, wg, wu, g1, w_out)


def _inproj_bwd(dr1, du_hg, dqk, dmv, dmo, dgt, w_hg, w_ml):
    t = dr1.shape[0]
    tm = _tile(t, DENSE_ROWS)

    def body(dr_ref, dhg_ref, dqk_ref, dmv_ref, dmo_ref, dgt_ref, whg_ref, wml_ref, gx_ref, dml_ref):
        dml = jnp.concatenate([dqk_ref[...], dmv_ref[...], dmo_ref[...], dgt_ref[...]], axis=-1).astype(ACT)
        dml_ref[...] = dml
        gx_ref[...] = ALPHA * dr_ref[...] + _bdot(dhg_ref[...], whg_ref[...]) + _bdot(dml, wml_ref[...])

    return pl.pallas_call(
        body, name="inproj_bwd", grid=(t // tm,),
        in_specs=[_rows(tm, D_MODEL), _rows(tm, U_HG), _rows(tm, MIX_W), _rows(tm, MIX_W), _rows(tm, MIX_W), _rows(tm, 128),
                  _resident((U_HG, D_MODEL)), _resident((U_ML, D_MODEL))],
        out_specs=[_rows(tm, D_MODEL), _rows(tm, U_ML)],
        out_shape=[jax.ShapeDtypeStruct((t, D_MODEL), f32), jax.ShapeDtypeStruct((t, U_ML), ACT)],
        compiler_params=_cparams(1, arbitrary=False),
    )(dr1, du_hg, dqk, dmv, dmo, dgt, w_hg, w_ml)


def _wgrad(a, b, name, tk=None, tn=None, colsum=False, low=False):
    t, kdim = a.shape
    n = b.shape[1]
    tk = tk or kdim
    tn = tn or n
    tt = _tile(t, WGRAD_ROWS)
    nt = t // tt
    assert not (colsum and low) and (not colsum or tn == n)

    def body(a_ref, b_ref, o_ref, *s_ref):
        @pl.when(pl.program_id(2) == 0)
        def _():
            o_ref[...] = jnp.zeros_like(o_ref)
            if colsum:
                s_ref[0][...] = jnp.zeros_like(s_ref[0])

        av = a_ref[...]
        o_ref[...] += _bdot_tn(av, b_ref[...])
        if colsum:
            s_ref[0][...] += jnp.sum(av.astype(f32), axis=0, keepdims=True)
        if low:
            @pl.when(pl.program_id(2) == nt - 1)
            def _():
                s_ref[0][...] = o_ref[...].astype(bf16)

    out_specs = [pl.BlockSpec((tk, tn), lambda i, j, s: (i, j))]
    out_shape = [jax.ShapeDtypeStruct((kdim, n), f32)]
    if colsum:
        out_specs.append(pl.BlockSpec((1, tk), lambda i, j, s: (0, i)))
        out_shape.append(jax.ShapeDtypeStruct((1, kdim), f32))
    if low:
        out_specs.append(pl.BlockSpec((tk, tn), lambda i, j, s: (i, j)))
        out_shape.append(jax.ShapeDtypeStruct((kdim, n), bf16))
    res = pl.pallas_call(
        body, name=name, grid=(kdim // tk, n // tn, t // tt),
        in_specs=[pl.BlockSpec((tt, tk), lambda i, j, s: (s, i)), pl.BlockSpec((tt, tn), lambda i, j, s: (s, j))],
        out_specs=out_specs, out_shape=out_shape,
        compiler_params=_cparams(3),
    )(a, b)
    return res if (colsum or low) else res[0]


_TRANSPOSED = {"w_in", "w_ffn_gate", "w_ffn_up"}
_COL_SPLIT = {"ple_w_proj"}
_SCATTER_PLAN = (("w_ffn_gate", "w_out", "ple_w_gate", "ple_w_proj"), ("w_ffn_up", "w_ffn_down"))
_RIDE_PLAN = {"inproj": ("w_ffn_gate",), "hgrn2_fwd": ("w_out",), "mlstm_fwd": ("w_ffn_up",),
              "outproj_ln1": ("ple_w_gate", "ple_w_proj"), "ffn_up": ("w_ffn_down",)}


def _from_chip_major(a, col_split):
    if col_split:
        return a.transpose(1, 0, 2).reshape(a.shape[1], 4 * a.shape[2])
    return a.reshape(4 * a.shape[1], a.shape[2])


def _local_step(x, p, tgt, w_in_b, b_in, logits, conv_w, conv_b, hg_gn, ml_gn, w_out_b, ln1_g, ln1_b,
                wg_b, wu_b, wd_b, ln2_g, ln2_b, w_pp_b, w_pg_b, b_pg, early_hook=None, late_shards=None):
    pad_w = U_HG + U_ML - PROJ_W
    w_hg = w_in_b[:U_HG]
    w_ml = jnp.pad(w_in_b[U_HG:], ((0, pad_w), (0, 0)))
    bb_hg = b_in[:, :U_HG]
    bb_ml = jnp.pad(b_in[:, U_HG:], ((0, 0), (0, pad_w)))

    late = dict(w_out=w_out_b, w_ffn_gate=wg_b, w_ffn_up=wu_b, w_ffn_down=wd_b, ple_w_proj=w_pp_b, ple_w_gate=w_pg_b)

    def riders_of(call):
        return [late_shards[k] for k in _RIDE_PLAN[call]] if late_shards is not None else ()

    def arrived(call, got):
        for k, g in zip(_RIDE_PLAN[call], got):
            late[k] = _from_chip_major(g, k in _COL_SPLIT)

    (u_hg, u_ml, xb), got = _inproj(x, w_hg, w_ml, bb_hg, bb_ml, riders_of("inproj"))
    arrived("inproj", got)
    (og_hg, sst, hg_b, hg_a, hg_o), got = _hgrn2_fwd(u_hg, logits, hg_gn, riders_of("hgrn2_fwd"))
    arrived("hgrn2_fwd", got)
    pre, qkc = _conv_fwd(u_ml, conv_w, conv_b)
    (og_ml, cst, nst, mst), got = _mlstm_fwd(qkc, u_ml, ml_gn, riders_of("mlstm_fwd"))
    arrived("mlstm_fwd", got)
    (x1, xh1, rs1, x1b), got = _outproj_ln1(og_hg, og_ml, x, late["w_out"], ln1_g, ln1_b, riders_of("outproj_ln1"))
    arrived("outproj_ln1", got)
    (hgp, up, act), got = _ffn_up(x1b, late["w_ffn_gate"], late["w_ffn_up"], riders_of("ffn_up"))
    arrived("ffn_up", got)
    w_out_b, wg_b, wu_b, wd_b = late["w_out"], late["w_ffn_gate"], late["w_ffn_up"], late["w_ffn_down"]
    w_pp_b, w_pg_b = late["ple_w_proj"], late["ple_w_gate"]
    x2, xh2, rs2, x2b = _ffn_down_ln2(act, x1, wd_b, ln2_g, ln2_b)
    dr2, de, dz, loss, d_bpg, d_ln2g, d_ln2b = _head_loss_bwd(x2, xh2, rs2, p, tgt, w_pg_b, b_pg, w_pp_b, ln2_g)
    dr1, dhg, dup, d_ln1g, d_ln1b, dog_hg, dog_ml = _ffn_bwd(dr2, hgp, up, xh1, rs1, wd_b, wg_b, wu_b, ln1_g, w_out_b)

    d_wo_a, lo_wo_a = _wgrad(og_hg, dr1, "wgrad_out_hg", low=True)
    d_wo_b, lo_wo_b = _wgrad(og_ml, dr1, "wgrad_out_ml", low=True)
    d_wg, lo_wg = _wgrad(dhg, x1b, "wgrad_ffn_gate", tk=D_FF // 2, low=True)
    d_wu, lo_wu = _wgrad(dup, x1b, "wgrad_ffn_up", tk=D_FF // 2, low=True)
    d_wd, lo_wd = _wgrad(act, dr2, "wgrad_ffn_down", tk=D_FF // 2, low=True)
    d_wpp, lo_wpp = _wgrad(p, de, "wgrad_ple_proj", low=True)
    d_wpg, lo_wpg = _wgrad(x2b, dz, "wgrad_ple_gate", low=True)
    early = dict(w_out=jnp.concatenate([d_wo_a, d_wo_b], axis=0), w_ffn_gate=d_wg, w_ffn_up=d_wu, w_ffn_down=d_wd,
                 ple_w_proj=d_wpp, ple_w_gate=d_wpg)
    early_low = dict(w_out=jnp.concatenate([lo_wo_a, lo_wo_b], axis=0), w_ffn_gate=lo_wg, w_ffn_up=lo_wu, w_ffn_down=lo_wd,
                     ple_w_proj=lo_wpp, ple_w_gate=lo_wpg)
    ride_hg, ride_ml = early_hook(early_low) if early_hook is not None else ((), ())

    (du_hg, d_logits, d_hg_gn), got_hg = _hgrn2_bwd(u_hg, logits, hg_gn, sst, hg_b, hg_a, hg_o, dog_hg, ride_hg)
    (dqkc, dmv, dmo, dgt, d_ml_gn), got_ml = _mlstm_bwd(qkc, u_ml, ml_gn, cst, nst, mst, dog_ml, ride_ml)
    dqk, d_conv_w, d_conv_b = _conv_bwd(u_ml, conv_w, pre, dqkc)
    grad_x, du_ml = _inproj_bwd(dr1, du_hg, dqk, dmv, dmo, dgt, w_hg, w_ml)

    dw_hg, db_hg = _wgrad(du_hg, xb, "wgrad_in_hg", tk=U_HG // 2, colsum=True)
    dw_ml, db_ml = _wgrad(du_ml, xb, "wgrad_in_ml", colsum=True)
    d_w_in = jnp.concatenate([dw_hg, dw_ml[:PROJ_W - U_HG]], axis=0)
    d_b_in = jnp.concatenate([db_hg, db_ml[:, :PROJ_W - U_HG]], axis=1)

    grads = dict(w_in=d_w_in, b_in=d_b_in, hg_lb_logits=d_logits, ml_conv_w=d_conv_w, ml_conv_b=d_conv_b,
                 hg_norm_g=d_hg_gn, ml_norm_g=d_ml_gn, ln1_g=d_ln1g, ln1_b=d_ln1b, ln2_g=d_ln2g, ln2_b=d_ln2b,
                 ple_b_gate=d_bpg, **early)
    return loss, grad_x, grads, (list(got_hg), list(got_ml))


_ANY = pl.BlockSpec(memory_space=pltpu.HBM)
_MESH = pl.DeviceIdType.MESH


def _my_place():
    return lax.axis_index("x"), lax.axis_index("y"), lax.axis_index("c")


def _other_chips(x, y):
    return [(1 - x, y), (x, 1 - y), (1 - x, 1 - y)]


_VMEM = pl.BlockSpec(memory_space=pltpu.VMEM)
_EX_ROWS = 32


def _pair_reduce_cols(p, name):
    s, r, c = p.shape
    hc = c // 2

    def body(p_ref, o_ref, other, send_sem, recv_sem):
        x, y, cc = _my_place()

        def run(mine_lo, theirs_lo):
            cp = pltpu.make_async_remote_copy(src_ref=p_ref.at[pl.ds(0, s), pl.ds(0, r), pl.ds(theirs_lo, hc)], dst_ref=other,
                                              send_sem=send_sem, recv_sem=recv_sem, device_id=(x, y, 1 - cc), device_id_type=_MESH)
            cp.start()
            cp.wait()
            for slot in range(s):
                o_ref[slot] = (p_ref[slot, :, mine_lo:mine_lo + hc] + other[slot]).astype(bf16)

        @pl.when(cc == 0)
        def _():
            run(0, hc)

        @pl.when(cc == 1)
        def _():
            run(hc, 0)

    return pl.pallas_call(
        body, name=name, in_specs=[_VMEM], out_specs=_VMEM,
        out_shape=jax.ShapeDtypeStruct((s, r, hc), bf16),
        scratch_shapes=[pltpu.VMEM((s, r, hc), f32), pltpu.SemaphoreType.DMA, pltpu.SemaphoreType.DMA],
        compiler_params=pltpu.CompilerParams(vmem_limit_bytes=VMEM_LIMIT),
    )(p)


def _chip_reduce_swap_cols(rcv, name):
    s, r, hc = rcv.shape

    def body(r_ref, g_ref, send_sem, recv_sem):
        x, y, cc = _my_place()
        acc = r_ref[0].astype(f32)
        for slot in range(1, s):
            acc = acc + r_ref[slot].astype(f32)
        g_ref[cc] = acc
        cp = pltpu.make_async_remote_copy(src_ref=g_ref.at[cc], dst_ref=g_ref.at[cc], send_sem=send_sem, recv_sem=recv_sem,
                                          device_id=(x, y, 1 - cc), device_id_type=_MESH)
        cp.start()
        cp.wait()

    both = pl.pallas_call(
        body, name=name, in_specs=[_VMEM], out_specs=_VMEM,
        out_shape=jax.ShapeDtypeStruct((2, r, hc), f32),
        scratch_shapes=[pltpu.SemaphoreType.DMA, pltpu.SemaphoreType.DMA],
        compiler_params=pltpu.CompilerParams(vmem_limit_bytes=VMEM_LIMIT),
    )(rcv)
    return both.transpose(1, 0, 2).reshape(r, 2 * hc)


def _reduce_adamw(rcv, w, m, v, name):
    s, r, c = rcv.shape
    rows_per = _EX_ROWS

    def body(r_ref, w_ref, m_ref, v_ref, g_ref, d_ref, nm_ref, nv_ref, mine, theirs, send_sem, recv_sem):
        x, y, cc = _my_place()

        def chip_sum(i, carry):
            rs = pl.ds(pl.multiple_of(i * rows_per, rows_per), rows_per)
            acc = r_ref[0, rs, :].astype(f32)
            for slot in range(1, s):
                acc = acc + r_ref[slot, rs, :].astype(f32)
            mine[rs, :] = acc
            return carry

        lax.fori_loop(0, r // rows_per, chip_sum, 0)
        cp = pltpu.make_async_remote_copy(src_ref=mine, dst_ref=theirs, send_sem=send_sem, recv_sem=recv_sem,
                                          device_id=(x, y, 1 - cc), device_id_type=_MESH)
        cp.start()
        cp.wait()

        def update(i, carry):
            rs = pl.ds(pl.multiple_of(i * rows_per, rows_per), rows_per)
            g = mine[rs, :] + theirs[rs, :]
            nm = B1 * m_ref[rs, :] + (1.0 - B1) * g
            nv = B2 * v_ref[rs, :] + (1.0 - B2) * (g * g)
            g_ref[rs, :] = g
            nm_ref[rs, :] = nm
            nv_ref[rs, :] = nv
            d_ref[rs, :] = -LR * ((nm / (1.0 - B1 ** STEP)) / (jnp.sqrt(nv / (1.0 - B2 ** STEP)) + EPS_ADAM) + WD * w_ref[rs, :])
            return carry

        lax.fori_loop(0, r // rows_per, update, 0)

    return pl.pallas_call(
        body, name=name, in_specs=[_VMEM] * 4, out_specs=[_VMEM] * 4,
        out_shape=[jax.ShapeDtypeStruct((r, c), f32)] * 4,
        scratch_shapes=[pltpu.VMEM((r, c), f32), pltpu.VMEM((r, c), f32), pltpu.SemaphoreType.DMA, pltpu.SemaphoreType.DMA],
        compiler_params=pltpu.CompilerParams(vmem_limit_bytes=VMEM_LIMIT),
    )(rcv, w, m, v)


def _gather_copies(ins, outs, send_sems, recv_sems, local_sems):
    x, y, c = _my_place()
    me = 2 * x + y
    local, outgoing, incoming = [], [], []
    for a in range(len(ins)):
        local.append(pltpu.make_async_copy(ins[a], outs[a].at[me], local_sems.at[a]))
        for j, (px, py) in enumerate(_other_chips(x, y)):
            sems = dict(send_sem=send_sems.at[3 * a + j], recv_sem=recv_sems.at[3 * a + j], device_id=(px, py, c), device_id_type=_MESH)
            outgoing.append(pltpu.make_async_remote_copy(src_ref=ins[a], dst_ref=outs[a].at[me], **sems))
            incoming.append(pltpu.make_async_remote_copy(src_ref=ins[a], dst_ref=outs[a].at[2 * px + py], **sems))
    return local, outgoing, incoming


def _gather_first(block, taps, name):
    r, c = block.shape
    hc = c // 2

    def body(in_ref, tap_in, out_ref, tap_out, send_sems, recv_sems):
        x, y, cc = _my_place()
        me = 2 * x + y
        sibling = (x, y, 1 - cc)
        chips = _other_chips(x, y)
        out_ref[me] = in_ref[...]
        tap_out[me] = tap_in[...]

        def run(mine, theirs):
            def ici(j, chip):
                px, py = chips[j]
                src = in_ref.at[pl.ds(0, r), pl.ds(mine, hc)] if chip is None else out_ref.at[chip, pl.ds(0, r), pl.ds(mine, hc)]
                dst = out_ref.at[me if chip is None else chip, pl.ds(0, r), pl.ds(mine, hc)]
                return pltpu.make_async_remote_copy(src_ref=src, dst_ref=dst, send_sem=send_sems.at[j], recv_sem=recv_sems.at[j],
                                                    device_id=(px, py, cc), device_id_type=_MESH)

            def d2d(j, lo):
                px, py = chips[j]
                blk = out_ref.at[2 * px + py, pl.ds(0, r), pl.ds(lo, hc)]
                return pltpu.make_async_remote_copy(src_ref=blk, dst_ref=blk, send_sem=send_sems.at[3 + j], recv_sem=recv_sems.at[3 + j],
                                                    device_id=sibling, device_id_type=_MESH)

            def tap(j, chip):
                px, py = chips[j]
                return pltpu.make_async_remote_copy(src_ref=tap_in, dst_ref=tap_out.at[me if chip is None else chip],
                                                    send_sem=send_sems.at[6 + j], recv_sem=recv_sems.at[6 + j],
                                                    device_id=(px, py, cc), device_id_type=_MESH)

            for j in range(3):
                ici(j, None).start()
                tap(j, None).start()
            for j, (px, py) in enumerate(chips):
                ici(j, 2 * px + py).wait_recv()
                d2d(j, mine).start()
            for j, (px, py) in enumerate(chips):
                d2d(j, theirs).wait_recv()
                tap(j, 2 * px + py).wait_recv()
            for j in range(3):
                ici(j, None).wait_send()
                d2d(j, mine).wait_send()
                tap(j, None).wait_send()

        @pl.when(cc == 0)
        def _():
            run(0, hc)

        @pl.when(cc == 1)
        def _():
            run(hc, 0)

    return pl.pallas_call(
        body, name=name, in_specs=[_VMEM, _VMEM], out_specs=[_VMEM, _VMEM],
        out_shape=[jax.ShapeDtypeStruct((4, r, c), block.dtype), jax.ShapeDtypeStruct((4,) + taps.shape, taps.dtype)],
        scratch_shapes=[pltpu.SemaphoreType.DMA((9,)), pltpu.SemaphoreType.DMA((9,))],
        compiler_params=pltpu.CompilerParams(vmem_limit_bytes=VMEM_LIMIT),
    )(block, taps)


def _riding_call(body, name, nsteps, in_specs, out_specs, out_shape, scratch_shapes, operands, riders, copies, ride_shapes):
    nr, n_in, n_out, n_scr = len(riders), len(in_specs), len(out_specs), len(scratch_shapes)

    def wrapped(*refs):
        ins, ride_in = refs[:n_in], refs[n_in:n_in + nr]
        outs, ride_out = refs[n_in + nr:n_in + nr + n_out], refs[n_in + nr + n_out:n_in + 2 * nr + n_out]
        scratch, sems = refs[n_in + 2 * nr + n_out:n_in + 2 * nr + n_out + n_scr], refs[n_in + 2 * nr + n_out + n_scr:]
        if nr:
            @pl.when(pl.program_id(0) == 0)
            def _():
                local, outgoing, _ = copies(ride_in, ride_out, *sems)
                for cp in local + outgoing:
                    cp.start()

        body(*ins, *outs, *scratch)
        if nr:
            @pl.when(pl.program_id(0) == nsteps - 1)
            def _():
                local, outgoing, incoming = copies(ride_in, ride_out, *sems)
                for cp in incoming:
                    cp.wait_recv()
                for cp in outgoing:
                    cp.wait_send()
                for cp in local:
                    cp.wait()

    hbm = pl.BlockSpec(memory_space=pltpu.HBM)
    sems = [pltpu.SemaphoreType.DMA((3 * nr,)), pltpu.SemaphoreType.DMA((3 * nr,)), pltpu.SemaphoreType.DMA((nr,))] if nr else []
    res = pl.pallas_call(
        wrapped, name=name, grid=(nsteps,),
        in_specs=list(in_specs) + [hbm] * nr, out_specs=list(out_specs) + [hbm] * nr,
        out_shape=list(out_shape) + list(ride_shapes),
        scratch_shapes=list(scratch_shapes) + sems,
        compiler_params=_cparams(1),
    )(*operands, *riders)
    return list(res[:n_out]), list(res[n_out:])


def _gather_shapes(riders):
    return [jax.ShapeDtypeStruct((4,) + r.shape, r.dtype) for r in riders]


def _scatter_copies(ins, outs, send_sems, recv_sems, local_sems):
    x, y, c = _my_place()
    me = 2 * x + y
    local, outgoing, incoming = [], [], []
    for a in range(len(ins)):
        local.append(pltpu.make_async_copy(ins[a].at[me], outs[a].at[me], local_sems.at[a]))
        for j, (px, py) in enumerate(_other_chips(x, y)):
            sems = dict(send_sem=send_sems.at[3 * a + j], recv_sem=recv_sems.at[3 * a + j], device_id=(px, py, c), device_id_type=_MESH)
            outgoing.append(pltpu.make_async_remote_copy(src_ref=ins[a].at[2 * px + py], dst_ref=outs[a].at[me], **sems))
            incoming.append(pltpu.make_async_remote_copy(src_ref=ins[a].at[2 * px + py], dst_ref=outs[a].at[2 * px + py], **sems))
    return local, outgoing, incoming


def _scatter_chips(pieces, name):
    n = len(pieces)

    def body(*refs):
        local, outgoing, incoming = _scatter_copies(refs[:n], refs[n:2 * n], *refs[2 * n:])
        for cp in local + outgoing:
            cp.start()
        for cp in incoming:
            cp.wait_recv()
        for cp in outgoing:
            cp.wait_send()
        for cp in local:
            cp.wait()

    return pl.pallas_call(
        body, name=name,
        in_specs=[_ANY] * n, out_specs=[_ANY] * n,
        out_shape=[jax.ShapeDtypeStruct(s.shape, s.dtype) for s in pieces],
        scratch_shapes=[pltpu.SemaphoreType.DMA((3 * n,)), pltpu.SemaphoreType.DMA((3 * n,)), pltpu.SemaphoreType.DMA((n,))],
    )(*pieces)


def _gather_all(block, name):
    def body(in_ref, out_ref, send_sems, recv_sems, local_sem):
        x, y, c = _my_place()
        me = 4 * x + 2 * y + c
        cp = pltpu.make_async_copy(in_ref, out_ref.at[me], local_sem)
        cp.start()
        peers = []
        for dx in range(2):
            for dy in range(2):
                for dc in range(2):
                    if dx or dy or dc:
                        peers.append((1 - x if dx else x, 1 - y if dy else y, 1 - c if dc else c))
        for j, pr in enumerate(peers):
            pltpu.make_async_remote_copy(src_ref=in_ref, dst_ref=out_ref.at[me], send_sem=send_sems.at[j], recv_sem=recv_sems.at[j],
                                         device_id=pr, device_id_type=_MESH).start()
        for j, (px, py, pc) in enumerate(peers):
            pltpu.make_async_remote_copy(src_ref=in_ref, dst_ref=out_ref.at[4 * px + 2 * py + pc], send_sem=send_sems.at[j], recv_sem=recv_sems.at[j],
                                         device_id=(px, py, pc), device_id_type=_MESH).wait()
        cp.wait()

    return pl.pallas_call(
        body, name=name,
        in_specs=[_ANY], out_specs=_ANY,
        out_shape=jax.ShapeDtypeStruct((8,) + block.shape, block.dtype),
        scratch_shapes=[pltpu.SemaphoreType.DMA((7,)), pltpu.SemaphoreType.DMA((7,)), pltpu.SemaphoreType.DMA],
    )(block)


def _row_tile(r, c):
    best = r
    for cand in range(16, r + 1, 16):
        if r % cand == 0 and cand * c * 4 <= (1 << 20):
            best = cand
    return best if best * c * 4 <= (4 << 20) else r


def _sum_slots(parts, name):
    n, r, c = parts.shape
    tr = _row_tile(r, c)

    def body(p_ref, o_ref):
        acc = p_ref[0].astype(f32)
        for s in range(1, n):
            acc = acc + p_ref[s].astype(f32)
        o_ref[...] = acc

    return pl.pallas_call(
        body, name=name, grid=(r // tr,),
        in_specs=[pl.BlockSpec((n, tr, c), lambda i: (0, i, 0))],
        out_specs=pl.BlockSpec((tr, c), lambda i: (i, 0)),
        out_shape=jax.ShapeDtypeStruct((r, c), f32),
        compiler_params=_cparams(1, arbitrary=False),
    )(parts)


def _adamw(parts, w, m, v, name):
    n, r, c = parts.shape
    tr = _row_tile(r, c)
    tc = c
    if tr == r and r * c * 4 > (1 << 20) and c % 256 == 0:
        tc = 256

    def body(p_ref, w_ref, m_ref, v_ref, g_ref, d_ref, nm_ref, nv_ref):
        g = p_ref[0]
        for s in range(1, n):
            g = g + p_ref[s]
        nm = B1 * m_ref[...] + (1.0 - B1) * g
        nv = B2 * v_ref[...] + (1.0 - B2) * (g * g)
        m_hat = nm / (1.0 - B1 ** STEP)
        v_hat = nv / (1.0 - B2 ** STEP)
        g_ref[...] = g
        nm_ref[...] = nm
        nv_ref[...] = nv
        d_ref[...] = -LR * (m_hat / (jnp.sqrt(v_hat) + EPS_ADAM) + WD * w_ref[...])

    blk = pl.BlockSpec((tr, tc), lambda i, j: (i, j))
    return pl.pallas_call(
        body, name=name, grid=(r // tr, c // tc),
        in_specs=[pl.BlockSpec((n, tr, tc), lambda i, j: (0, i, j)), blk, blk, blk],
        out_specs=[blk] * 4,
        out_shape=[jax.ShapeDtypeStruct((r, c), f32)] * 4,
        compiler_params=_cparams(2, arbitrary=False),
    )(parts, w, m, v)


_BIG = ["w_in", "w_out", "w_ffn_gate", "w_ffn_up", "w_ffn_down", "ple_w_proj", "ple_w_gate"]
_SMALL = ["b_in", "hg_lb_logits", "ml_conv_w", "ml_conv_b", "hg_norm_g", "ml_norm_g", "ln1_g", "ln1_b", "ln2_g", "ln2_b", "ple_b_gate"]
_ORDER = ["w_in", "b_in", "hg_lb_logits", "ml_conv_w", "ml_conv_b", "hg_norm_g", "ml_norm_g", "w_out", "ln1_g", "ln1_b",
          "w_ffn_gate", "w_ffn_up", "w_ffn_down", "ln2_g", "ln2_b", "ple_w_proj", "ple_w_gate", "ple_b_gate"]
_PACK_ROWS, _PACK_COLS = 16, 1024


def _pack(arrays):
    flat = jnp.concatenate([a.reshape(-1) for a in arrays])
    return jnp.pad(flat, (0, _PACK_ROWS * _PACK_COLS - flat.shape[0])).reshape(_PACK_ROWS, _PACK_COLS)


def _unpack(pack, shapes):
    flat = pack.reshape(-1)
    out, off = [], 0
    for s in shapes:
        size = 1
        for d in s:
            size *= d
        out.append(flat[off:off + size].reshape(s))
        off += size
    return out


def _to_chip_major(g, col_split):
    if col_split:
        k, n = g.shape
        return g.reshape(k, 4, n // 4).transpose(1, 0, 2)
    k, n = g.shape
    return g.reshape(4, k // 4, n)


def kernel(x, p, w_in, b_in, hg_lb_logits, ml_conv_w, ml_conv_b, hg_norm_g, ml_norm_g, w_out, ln1_g, ln1_b, w_ffn_gate, w_ffn_up, w_ffn_down, ln2_g, ln2_b, ple_w_proj, ple_w_gate, ple_b_gate, loss_target, m_w_in, m_b_in, m_hg_lb_logits, m_ml_conv_w, m_ml_conv_b, m_hg_norm_g, m_ml_norm_g, m_w_out, m_ln1_g, m_ln1_b, m_w_ffn_gate, m_w_ffn_up, m_w_ffn_down, m_ln2_g, m_ln2_b, m_ple_w_proj, m_ple_w_gate, m_ple_b_gate, v_w_in, v_b_in, v_hg_lb_logits, v_ml_conv_w, v_ml_conv_b, v_hg_norm_g, v_ml_norm_g, v_w_out, v_ln1_g, v_ln1_b, v_w_ffn_gate, v_w_ffn_up, v_w_ffn_down, v_ln2_g, v_ln2_b, v_ple_w_proj, v_ple_w_gate, v_ple_b_gate):
    args = dict(locals())
    wts = {k: args[k] for k in _ORDER}
    mom = {k: args["m_" + k] for k in _ORDER}
    var = {k: args["v_" + k] for k in _ORDER}
    two_d = lambda a: a.reshape(a.shape[-2], a.shape[-1])
    block = lambda k, a: jnp.swapaxes(two_d(a), 0, 1) if k in _TRANSPOSED else two_d(a)
    unblock = lambda k, a: (jnp.swapaxes(a, 0, 1) if k in _TRANSPOSED else a).reshape(wts[k].shape)

    shards = {k: block(k, wts[k]).astype(bf16) for k in _BIG}
    w_in_blocks, taps = _gather_first(shards["w_in"], two_d(ml_conv_w), "gather_w_in")
    w_in_full = _from_chip_major(w_in_blocks, False)
    conv_w_full = _from_chip_major(taps, True)

    early_keys = _BIG[1:]
    loss, grad_x, grads, (got_hg, got_ml) = _local_step(
        x[0], p[0, 0], loss_target[0], w_in_full, b_in, hg_lb_logits, conv_w_full, ml_conv_b, hg_norm_g, ml_norm_g,
        None, ln1_g, ln1_b, None, None, None, ln2_g, ln2_b, None, None, ple_b_gate,
        early_hook=lambda low: tuple([_to_chip_major(low[k], k in _COL_SPLIT) for k in names] for names in _SCATTER_PLAN),
        late_shards={k: shards[k] for k in early_keys})

    out_g, out_d, out_m, out_v = {}, {}, {}, {}

    def finish(k, g, d, nm, nv):
        out_g[k], out_d[k], out_m[k], out_v[k] = unblock(k, g), unblock(k, d), unblock(k, nm), unblock(k, nv)

    for names, got in zip(_SCATTER_PLAN, (got_hg, got_ml)):
        for k, rcv in zip(names, got):
            finish(k, *_reduce_adamw(rcv, block(k, wts[k]), block(k, mom[k]), block(k, var[k]), "reduce_adamw_" + k))

    core_sums = _pair_reduce_cols(_to_chip_major(grads["w_in"], False), "pair_reduce_w_in")
    whole = _chip_reduce_swap_cols(_scatter_chips([core_sums], "scatter_grad_w_in")[0], "chip_reduce_w_in")
    finish("w_in", *_adamw(whole[None], block("w_in", wts["w_in"]), block("w_in", mom["w_in"]), block("w_in", var["w_in"]), "adamw_w_in"))

    small_shapes = [(1, PROJ_W), (2, MIX_W), (CONV_K, MIX_W)] + [(1, MIX_W)] * 3 + [(1, D_MODEL)] * 5 + [(1, 1)]
    contrib = _pack([grads[k] for k in _SMALL] + [loss])
    summed = _sum_slots(_gather_all(contrib, "gather_small"), "sum_small")
    small = _unpack(summed, small_shapes)
    loss_total = small[-1].reshape(())
    gsm = dict(zip(_SMALL, small[:-1]))
    place = 2 * lax.axis_index("x") + lax.axis_index("y")
    conv_cols = ml_conv_w.shape[-1]
    gsm["ml_conv_w"] = lax.dynamic_slice(gsm["ml_conv_w"], (0, place * conv_cols), (CONV_K, conv_cols))
    own_shapes = [wts[k].shape for k in _SMALL]
    g_pack = _pack([gsm[k] for k in _SMALL])
    res = _adamw(g_pack[None], _pack([wts[k] for k in _SMALL]), _pack([mom[k] for k in _SMALL]), _pack([var[k] for k in _SMALL]), "adamw_small")
    for dst, pack in zip((out_g, out_d, out_m, out_v), res):
        for k, a in zip(_SMALL, _unpack(pack, own_shapes)):
            dst[k] = a

    outs = [loss_total, grad_x[None]]
    for group in (out_g, out_d, out_m, out_v):
        outs += [group[k] for k in _ORDER]
    return tuple(outs)
```

```python
import jax
import jax.numpy as jnp
from jax import lax
from jax.experimental import pallas as pl
from jax.experimental.pallas import tpu as pltpu

f32 = jnp.float32
bf16 = jnp.bfloat16

D_MODEL = 1024
HEADS = 4
HEAD_W = 128
MIX_W = HEADS * HEAD_W
ML_DQK = 64
PROJ_W = 3592
U_HG = 4 * MIX_W
U_ML = 3 * MIX_W + 128
D_FF = 2816
PLE = 256
CHUNK = 128
SUB = 16
EXP_CAP = 80.0
CONV_K = 4
HALO = 8
ALPHA = float(2.0 ** 0.25)
LN_EPS = 1e-5
RMS_EPS = 1e-6
NEG = -1e30
LR, B1, B2, EPS_ADAM, WD, STEP = 0.001, 0.9, 0.999, 1e-08, 0.01, 10
VMEM_LIMIT = 56 * 1024 * 1024
MIXER_ROWS = 512
DENSE_ROWS = 512
WGRAD_ROWS = 2048


def _cparams(n_axes, arbitrary=True):
    sem = ("arbitrary",) * n_axes if arbitrary else ("parallel",) * n_axes
    return pltpu.CompilerParams(dimension_semantics=sem, vmem_limit_bytes=VMEM_LIMIT)


ACT = bf16


def _mx(a):
    return a.astype(ACT)


def _bdot(a, b):
    return jnp.dot(_mx(a), _mx(b), preferred_element_type=f32)


def _bdot_nt(a, b):
    return lax.dot_general(_mx(a), _mx(b), (((1,), (1,)), ((), ())), preferred_element_type=f32)


def _bdot_tn(a, b):
    return lax.dot_general(_mx(a), _mx(b), (((0,), (0,)), ((), ())), preferred_element_type=f32)


def _split3(x):
    hi = x.astype(bf16)
    r1 = x - hi.astype(f32)
    mid = r1.astype(bf16)
    lo = (r1 - mid.astype(f32)).astype(bf16)
    return hi, mid, lo


def _dot3(a, b, dims):
    a_hi = a.astype(bf16)
    a_lo = (a - a_hi.astype(f32)).astype(bf16)
    b_hi = b.astype(bf16)
    b_lo = (b - b_hi.astype(f32)).astype(bf16)
    dn = (dims, ((), ()))
    return (lax.dot_general(a_hi, b_hi, dn, preferred_element_type=f32) + lax.dot_general(a_hi, b_lo, dn, preferred_element_type=f32)
            + lax.dot_general(a_lo, b_hi, dn, preferred_element_type=f32))


def _lane_sum(x):
    hi = x.astype(bf16)
    lo = (x - hi.astype(f32)).astype(bf16)
    ones = jnp.ones((x.shape[1], 128), bf16)
    return jnp.dot(hi, ones, preferred_element_type=f32) + jnp.dot(lo, ones, preferred_element_type=f32)


def _lane_dot(x, row):
    return _dot3(x, jnp.broadcast_to(row, (128, row.shape[1])), ((1,), (1,)))


def _sel_dot(sel, x):
    sb = sel.astype(bf16)
    return sum(jnp.dot(sb, part, preferred_element_type=f32) for part in _split3(x))


def _sel_dot_nt(sel, x):
    sb = sel.astype(bf16)
    return sum(lax.dot_general(sb, part, (((1,), (1,)), ((), ())), preferred_element_type=f32) for part in _split3(x))


def _sigmoid(x):
    return 1.0 / (1.0 + jnp.exp(-x))


def _log_sigmoid(x):
    return jnp.minimum(x, 0.0) - jnp.log(1.0 + jnp.exp(-jnp.abs(x)))


def _tri(n, upper=False):
    r = lax.broadcasted_iota(jnp.int32, (n, n), 0)
    c = lax.broadcasted_iota(jnp.int32, (n, n), 1)
    return (c >= r) if upper else (c <= r)


def _rows(tm, n, col=0):
    return pl.BlockSpec((tm, n), lambda i, _c=col: (i, _c))


def _rows_rev(tm, n, nb, col=0):
    return pl.BlockSpec((tm, n), lambda i, _c=col, _nb=nb: (_nb - 1 - i, _c))


def _const(shape):
    return pl.BlockSpec(shape, lambda i, _n=len(shape): (0,) * _n)


def _resident(shape):
    return pl.BlockSpec(shape, lambda i, _n=len(shape): (0,) * _n, pipeline_mode=pl.Buffered(1))


def _tile(t, want):
    return want if t % want == 0 else t


def _inproj(x, w_hg, w_ml, b_hg, b_ml, riders=()):
    t = x.shape[0]
    tm = _tile(t, DENSE_ROWS)

    def body(x_ref, whg_ref, wml_ref, bhg_ref, bml_ref, uhg_ref, uml_ref, xb_ref):
        xb = _mx(x_ref[...])
        xb_ref[...] = xb
        uhg_ref[...] = _bdot_nt(xb, whg_ref[...]) + bhg_ref[...]
        uml_ref[...] = _bdot_nt(xb, wml_ref[...]) + bml_ref[...]

    return _riding_call(
        body, "inproj", t // tm,
        in_specs=[_rows(tm, D_MODEL), _resident((U_HG, D_MODEL)), _resident((U_ML, D_MODEL)), _const((1, U_HG)), _const((1, U_ML))],
        out_specs=[_rows(tm, U_HG), _rows(tm, U_ML), _rows(tm, D_MODEL)],
        out_shape=[jax.ShapeDtypeStruct((t, U_HG), f32), jax.ShapeDtypeStruct((t, U_ML), f32), jax.ShapeDtypeStruct((t, D_MODEL), ACT)],
        scratch_shapes=[], operands=(x, w_hg, w_ml, b_hg, b_ml), riders=riders, copies=_gather_copies, ride_shapes=_gather_shapes(riders))


def _hg_gates(hq, hf, lb, tri, b=None):
    s = _sigmoid(hf)
    om = 1.0 - lb
    f = lb + om * s
    k = om * (1.0 - s)
    sq = _sigmoid(hq)
    q = hq * sq
    if b is None:
        b = _sel_dot(tri, jnp.log(f))
    return q, sq, s, f, k, b


def _hg_scores(q, k, b, tril_mask, a=None):
    qts, kts, eqs, eks, rows = [], [], [], [], []
    for i in range(CHUNK // SUB):
        lo = i * SUB
        ref = jnp.zeros_like(b[0:1]) if i == 0 else b[lo - 1:lo]
        eq = jnp.exp(b[lo:lo + SUB] - ref)
        ek = jnp.exp(jnp.minimum(ref - b, EXP_CAP))
        qt = q[lo:lo + SUB] * eq
        kt = k * ek
        if a is None:
            rows.append(_bdot_nt(qt, kt))
        qts.append(qt); kts.append(kt); eqs.append(eq); eks.append(ek)
    if a is None:
        a = jnp.where(tril_mask, jnp.concatenate(rows, axis=0), 0.0)
    return a, qts, kts, eqs, eks


def _head_rms(o, gn, on_mxu=False):
    ms = _lane_sum(o * o) * (1.0 / o.shape[1]) if on_mxu else jnp.mean(o * o, axis=-1, keepdims=True)
    rstd = lax.rsqrt(ms + RMS_EPS)
    oh = o * rstd
    return oh, rstd, oh * gn


def _lower_bound(logit_ref):
    lg = logit_ref[...]
    return _sigmoid(lg[0:1] - lg[1:2])


def _hgrn2_fwd(u_hg, logits, gn, riders=()):
    t = u_hg.shape[0]
    tb = _tile(t, MIXER_ROWS)
    nc_blk = tb // CHUNK

    def body(u_ref, lg_ref, gn_ref, og_ref, sst_ref, b_ref, a_ref, o_ref, st_ref):
        @pl.when(pl.program_id(0) == 0)
        def _():
            st_ref[...] = jnp.zeros_like(st_ref)

        lb_all = _lower_bound(lg_ref)
        tril_mask = _tri(CHUNK)
        tri = tril_mask.astype(f32)

        def chunk(c, carry):
            r0 = pl.multiple_of(c * CHUNK, CHUNK)
            rows = pl.ds(r0, CHUNK)
            heads = range(HEADS)
            cols = [slice(h * HEAD_W, (h + 1) * HEAD_W) for h in heads]
            hv = [u_ref[rows, 2 * MIX_W + h * HEAD_W:2 * MIX_W + (h + 1) * HEAD_W] for h in heads]
            gts = [_hg_gates(u_ref[rows, h * HEAD_W:(h + 1) * HEAD_W], u_ref[rows, MIX_W + h * HEAD_W:MIX_W + (h + 1) * HEAD_W],
                             lb_all[:, cols[h]], tri) for h in heads]
            q = [g[0] for g in gts]
            k = [g[4] for g in gts]
            b = [g[5] for g in gts]
            a = [_hg_scores(q[h], k[h], b[h], tril_mask)[0] for h in heads]
            st = [st_ref[h] for h in heads]
            bl = [b[h][CHUNK - 1:CHUNK] for h in heads]
            o = [_bdot(a[h], hv[h]) + _bdot_nt(q[h] * jnp.exp(b[h]), st[h]) for h in heads]
            new_st = [st[h] * jnp.exp(bl[h]) + _bdot_tn(hv[h], k[h] * jnp.exp(bl[h] - b[h])) for h in heads]
            for h in heads:
                sst_ref[c, h] = st[h]
                st_ref[h] = new_st[h]
                b_ref[rows, cols[h]] = b[h]
                a_ref[rows, cols[h]] = a[h].astype(ACT)
                o_ref[rows, cols[h]] = o[h]
                hgate = u_ref[rows, 3 * MIX_W + h * HEAD_W:3 * MIX_W + (h + 1) * HEAD_W]
                _, _, y = _head_rms(o[h], gn_ref[:, cols[h]])
                og_ref[rows, cols[h]] = (y * (hgate * _sigmoid(hgate))).astype(ACT)
            return carry

        lax.fori_loop(0, nc_blk, chunk, 0, unroll=True)

    assert CHUNK == HEAD_W
    return _riding_call(
        body, "hgrn2_fwd", t // tb,
        in_specs=[_rows(tb, U_HG), _const((2, MIX_W)), _const((1, MIX_W))],
        out_specs=[_rows(tb, MIX_W), pl.BlockSpec((nc_blk, HEADS, HEAD_W, HEAD_W), lambda i: (i, 0, 0, 0)),
                   _rows(tb, MIX_W), _rows(tb, MIX_W), _rows(tb, MIX_W)],
        out_shape=[jax.ShapeDtypeStruct((t, MIX_W), ACT), jax.ShapeDtypeStruct((t // CHUNK, HEADS, HEAD_W, HEAD_W), f32),
                   jax.ShapeDtypeStruct((t, MIX_W), f32), jax.ShapeDtypeStruct((t, MIX_W), ACT), jax.ShapeDtypeStruct((t, MIX_W), f32)],
        scratch_shapes=[pltpu.VMEM((HEADS, HEAD_W, HEAD_W), f32)],
        operands=(u_hg, logits, gn), riders=riders, copies=_gather_copies, ride_shapes=_gather_shapes(riders))


def _hgrn2_bwd(u_hg, logits, gn, sst, bcum, scores, o_raw, dog, riders=()):
    t = u_hg.shape[0]
    tb = _tile(t, MIXER_ROWS)
    nb = t // tb
    nc_blk = tb // CHUNK

    def body(u_ref, lg_ref, gn_ref, sst_ref, b_ref, a_ref, o_ref, dog_ref, du_ref, dlg_ref, dgn_ref, dst_ref):
        @pl.when(pl.program_id(0) == 0)
        def _():
            dst_ref[...] = jnp.zeros_like(dst_ref)
            dlg_ref[...] = jnp.zeros_like(dlg_ref)
            dgn_ref[...] = jnp.zeros_like(dgn_ref)

        lb_all = _lower_bound(lg_ref)
        tril_mask = _tri(CHUNK)
        tri = tril_mask.astype(f32)
        triu = _tri(CHUNK, upper=True).astype(f32)

        def chunk(j, carry):
            c = nc_blk - 1 - j
            r0 = pl.multiple_of(c * CHUNK, CHUNK)
            rows = pl.ds(r0, CHUNK)
            heads = range(HEADS)
            nsub = CHUNK // SUB
            cols = [slice(h * HEAD_W, (h + 1) * HEAD_W) for h in heads]
            hq = [u_ref[rows, h * HEAD_W:(h + 1) * HEAD_W] for h in heads]
            hf = [u_ref[rows, MIX_W + h * HEAD_W:MIX_W + (h + 1) * HEAD_W] for h in heads]
            hv = [u_ref[rows, 2 * MIX_W + h * HEAD_W:2 * MIX_W + (h + 1) * HEAD_W] for h in heads]
            lb = [lb_all[:, cols[h]] for h in heads]
            gts = [_hg_gates(hq[h], hf[h], lb[h], tri, b=b_ref[rows, cols[h]]) for h in heads]
            q, sq, s, f, k, b = ([g[n] for g in gts] for n in range(6))
            scs = [_hg_scores(q[h], k[h], b[h], tril_mask, a=a_ref[rows, cols[h]]) for h in heads]
            a, qts, kts, eqs, eks = ([sc[n] for sc in scs] for n in range(5))
            st = [sst_ref[c, h] for h in heads]
            dst = [dst_ref[h] for h in heads]
            bl = [b[h][CHUNK - 1:CHUNK] for h in heads]
            eb = [jnp.exp(b[h]) for h in heads]
            qh = [q[h] * eb[h] for h in heads]
            ekl = [jnp.exp(bl[h] - b[h]) for h in heads]
            kh = [k[h] * ekl[h] for h in heads]
            o = [o_ref[rows, cols[h]] for h in heads]
            do = []
            for h in heads:
                hgate = u_ref[rows, 3 * MIX_W + h * HEAD_W:3 * MIX_W + (h + 1) * HEAD_W]
                gnh = gn_ref[:, cols[h]]
                oh, rstd, y = _head_rms(o[h], gnh)
                sg = _sigmoid(hgate)
                dogh = dog_ref[rows, cols[h]]
                dy = dogh * (hgate * sg)
                du_ref[rows, 3 * MIX_W + h * HEAD_W:3 * MIX_W + (h + 1) * HEAD_W] = (dogh * y * (sg * (1.0 + hgate * (1.0 - sg)))).astype(ACT)
                dgn_ref[:, cols[h]] += jnp.sum(dy * oh, axis=0, keepdims=True)
                doh = dy * gnh
                do.append(rstd * (doh - oh * jnp.mean(doh * oh, axis=-1, keepdims=True)))
            da = [jnp.where(tril_mask, _bdot_nt(do[h], hv[h]), 0.0) for h in heads]
            dv = [_bdot_tn(a[h], do[h]) + _bdot_nt(kh[h], dst[h]) for h in heads]
            dq = [_bdot(do[h], st[h]) * eb[h] for h in heads]
            dk = [_bdot(hv[h], dst[h]) * ekl[h] for h in heads]
            d_last = [jnp.sum(k[h] * dk[h], axis=0, keepdims=True) + jnp.exp(bl[h]) * jnp.sum(dst[h] * st[h], axis=0, keepdims=True)
                      for h in heads]
            d_b = [q[h] * dq[h] - k[h] * dk[h] for h in heads]
            dqs = [[] for _ in heads]
            q_dq = [[] for _ in heads]
            for i in range(nsub):
                for h in heads:
                    da_i = _mx(da[h][i * SUB:(i + 1) * SUB])
                    q_r, k_r = _mx(qts[h][i]), _mx(kts[h][i])
                    g_q = jnp.dot(da_i, k_r, preferred_element_type=f32)
                    g_k = lax.dot_general(da_i, q_r, (((0,), (0,)), ((), ())), preferred_element_type=f32)
                    dqs[h].append(g_q * eqs[h][i])
                    q_dq[h].append(q_r.astype(f32) * g_q)
                    dk[h] = dk[h] + g_k * eks[h][i]
                    d_b[h] = d_b[h] - k_r.astype(f32) * g_k
            for h in heads:
                dq[h] = dq[h] + jnp.concatenate(dqs[h], axis=0)
                d_b[h] = d_b[h] + jnp.concatenate(q_dq[h], axis=0)
                dst_ref[h] = dst[h] * jnp.exp(bl[h]) + _bdot_tn(do[h], qh[h])
            dg = [_sel_dot(triu, d_b[h]) + d_last[h] for h in heads]
            for h in heads:
                dfk = dg[h] / f[h] - dk[h]
                du_ref[rows, h * HEAD_W:(h + 1) * HEAD_W] = (dq[h] * (sq[h] * (1.0 + hq[h] * (1.0 - sq[h])))).astype(ACT)
                du_ref[rows, MIX_W + h * HEAD_W:MIX_W + (h + 1) * HEAD_W] = ((1.0 - lb[h]) * dfk * s[h] * (1.0 - s[h])).astype(ACT)
                du_ref[rows, 2 * MIX_W + h * HEAD_W:2 * MIX_W + (h + 1) * HEAD_W] = dv[h].astype(ACT)
                dlb = jnp.sum((1.0 - s[h]) * dfk, axis=0, keepdims=True) * (lb[h] * (1.0 - lb[h]))
                dlg_ref[0:1, cols[h]] += dlb
                dlg_ref[1:2, cols[h]] -= dlb
            return carry

        lax.fori_loop(0, nc_blk, chunk, 0, unroll=True)

    rev = _rows_rev(tb, MIX_W, nb)
    return _riding_call(
        body, "hgrn2_bwd", nb,
        in_specs=[_rows_rev(tb, U_HG, nb), _const((2, MIX_W)), _const((1, MIX_W)),
                  pl.BlockSpec((nc_blk, HEADS, HEAD_W, HEAD_W), lambda i: (nb - 1 - i, 0, 0, 0)), rev, rev, rev, rev],
        out_specs=[_rows_rev(tb, U_HG, nb), _const((2, MIX_W)), _const((1, MIX_W))],
        out_shape=[jax.ShapeDtypeStruct((t, U_HG), ACT), jax.ShapeDtypeStruct((2, MIX_W), f32), jax.ShapeDtypeStruct((1, MIX_W), f32)],
        scratch_shapes=[pltpu.VMEM((HEADS, HEAD_W, HEAD_W), f32)],
        operands=(u_hg, logits, gn, sst, bcum, scores, o_raw, dog), riders=riders, copies=_scatter_copies,
        ride_shapes=[jax.ShapeDtypeStruct(r.shape, r.dtype) for r in riders])


def _conv_fwd(u_ml, w, b):
    t = u_ml.shape[0]
    tm = _tile(t, 512)

    def body(x_ref, w_ref, b_ref, pre_ref, act_ref, xbuf):
        @pl.when(pl.program_id(0) == 0)
        def _():
            xbuf[...] = jnp.zeros_like(xbuf)

        xbuf[0:HALO, :] = xbuf[tm:tm + HALO, :]
        xbuf[HALO:HALO + tm, :] = x_ref[...]
        pre = b_ref[...] + jnp.zeros((tm, MIX_W), f32)
        for kk in range(CONV_K):
            off = HALO - (CONV_K - 1) + kk
            pre = pre + w_ref[kk:kk + 1, :] * xbuf[off:off + tm, :]
        pre_ref[...] = pre
        act_ref[...] = pre * _sigmoid(pre)

    return pl.pallas_call(
        body, name="conv_fwd", grid=(t // tm,),
        in_specs=[_rows(tm, MIX_W), _const((CONV_K, MIX_W)), _const((1, MIX_W))],
        out_specs=[_rows(tm, MIX_W), _rows(tm, MIX_W)],
        out_shape=[jax.ShapeDtypeStruct((t, MIX_W), f32)] * 2,
        scratch_shapes=[pltpu.VMEM((tm + HALO, MIX_W), f32)],
        compiler_params=_cparams(1),
    )(u_ml, w, b)


def _conv_bwd(u_ml, w, pre, dact):
    t = u_ml.shape[0]
    tm = _tile(t, 512)
    nb = t // tm
    hb = tm // HALO

    def body(x_ref, halo_ref, w_ref, pre_ref, dact_ref, dx_ref, dw_ref, db_ref, dbuf, xbuf):
        i = pl.program_id(0)

        @pl.when(i == 0)
        def _():
            dbuf[...] = jnp.zeros_like(dbuf)
            dw_ref[...] = jnp.zeros_like(dw_ref)
            db_ref[...] = jnp.zeros_like(db_ref)

        p = pre_ref[...]
        sg = _sigmoid(p)
        dpre = dact_ref[...] * (sg * (1.0 + p * (1.0 - sg)))
        dbuf[tm:tm + HALO, :] = dbuf[0:HALO, :]
        dbuf[0:tm, :] = dpre
        has_prev = (i < nb - 1).astype(f32)
        xbuf[0:HALO, :] = halo_ref[...] * has_prev
        xbuf[HALO:HALO + tm, :] = x_ref[...]
        dx = jnp.zeros((tm, MIX_W), f32)
        for kk in range(CONV_K):
            back = CONV_K - 1 - kk
            dx = dx + w_ref[kk:kk + 1, :] * dbuf[back:back + tm, :]
            off = HALO - (CONV_K - 1) + kk
            dw_ref[kk:kk + 1, :] += jnp.sum(dpre * xbuf[off:off + tm, :], axis=0, keepdims=True)
        dx_ref[...] = dx.astype(ACT)
        db_ref[...] += jnp.sum(dpre, axis=0, keepdims=True)

    return pl.pallas_call(
        body, name="conv_bwd", grid=(nb,),
        in_specs=[_rows_rev(tm, MIX_W, nb),
                  pl.BlockSpec((HALO, MIX_W), lambda i: (jnp.maximum((nb - 1 - i) * hb - 1, 0), 0)),
                  _const((CONV_K, MIX_W)), _rows_rev(tm, MIX_W, nb), _rows_rev(tm, MIX_W, nb)],
        out_specs=[_rows_rev(tm, MIX_W, nb), _const((CONV_K, MIX_W)), _const((1, MIX_W))],
        out_shape=[jax.ShapeDtypeStruct((t, MIX_W), ACT), jax.ShapeDtypeStruct((CONV_K, MIX_W), f32), jax.ShapeDtypeStruct((1, MIX_W), f32)],
        scratch_shapes=[pltpu.VMEM((tm + HALO, MIX_W), f32), pltpu.VMEM((tm + HALO, MIX_W), f32)],
        compiler_params=_cparams(1),
    )(u_ml, u_ml, w, pre, dact)


def _lane_pick(x, lane):
    idx = lax.broadcasted_iota(jnp.int32, x.shape, 1)
    return jnp.sum(jnp.where(idx == lane, x, 0.0), axis=-1, keepdims=True)


def _ml_gate_forms(gates, tri):
    lf = _log_sigmoid(gates)
    gc = _sel_dot(tri, lf)
    lane = lax.broadcasted_iota(jnp.int32, gates.shape, 1)
    mixed = jnp.where(lane < HEADS, gates, gc)
    sel = (lax.broadcasted_iota(jnp.int32, (8, 128), 0) == lax.broadcasted_iota(jnp.int32, (8, 128), 1)).astype(f32)
    rowsf = _sel_dot_nt(sel, mixed)
    return gc, rowsf


def _ml_chunk(q, k, v, gates, gc, rowsf, c_st, n_st, m_st, tril_mask):
    hs = range(HEADS)
    g_col = [_lane_pick(gc, HEADS + h) for h in hs]
    ig_col = [_lane_pick(gates, h) for h in hs]
    dmat = [jnp.where(tril_mask, g_col[h] - rowsf[HEADS + h:HEADS + h + 1, :] + rowsf[h:h + 1, :], NEG) for h in hs]
    m_inter = [g_col[h] + m_st[h] for h in hs]
    m_t = [jnp.maximum(m_inter[h], jnp.max(dmat[h], axis=-1, keepdims=True)) for h in hs]
    wi = [jnp.exp(dmat[h] - m_t[h]) for h in hs]
    wo = [jnp.exp(m_inter[h] - m_t[h]) for h in hs]
    qk = [_bdot_nt(q[h], k[h]) * wi[h] for h in hs]
    num = [_bdot(qk[h], v[h]) + wo[h] * _bdot(q[h], c_st[h]) for h in hs]
    den = [_lane_sum(qk[h]) + wo[h] * _lane_dot(q[h], n_st[h]) for h in hs]
    floor = [jnp.exp(-m_t[h]) for h in hs]
    z = [jnp.maximum(jnp.abs(den[h]), floor[h]) for h in hs]
    g_last = [g_col[h][CHUNK - 1:CHUNK] for h in hs]
    a_col = [g_last[h] - g_col[h] + ig_col[h] for h in hs]
    m_new = [jnp.maximum(g_last[h] + m_st[h], jnp.max(a_col[h], axis=0, keepdims=True)) for h in hs]
    ws = [jnp.exp(a_col[h] - m_new[h]) for h in hs]
    w_old = [jnp.exp(g_last[h] + m_st[h] - m_new[h]) for h in hs]
    return dict(wi=wi, wo=wo, qk=qk, num=num, den=den, z=z, floor=floor, ws=ws, w_old=w_old, m_new=m_new)


def _mlstm_fwd(qkc, u_ml, gn, riders=()):
    t = qkc.shape[0]
    tb = _tile(t, MIXER_ROWS)
    nc_blk = tb // CHUNK

    def body(qk_ref, v_ref, mo_ref, gt_ref, gn_ref, og_ref, cst_ref, nst_ref, mst_ref, c_sc, n_sc, m_sc):
        @pl.when(pl.program_id(0) == 0)
        def _():
            c_sc[...] = jnp.zeros_like(c_sc)
            n_sc[...] = jnp.zeros_like(n_sc)
            m_sc[...] = jnp.zeros_like(m_sc)

        tril_mask = _tri(CHUNK)
        tri = tril_mask.astype(f32)

        def chunk(c, carry):
            r0 = pl.multiple_of(c * CHUNK, CHUNK)
            rows = pl.ds(r0, CHUNK)
            gates = gt_ref[rows, :]
            gc, rowsf = _ml_gate_forms(gates, tri)
            hs = range(HEADS)
            q = [qk_ref[rows, h * ML_DQK:(h + 1) * ML_DQK] * (ML_DQK ** -0.5) for h in hs]
            k = [qk_ref[rows, HEADS * ML_DQK + h * ML_DQK:HEADS * ML_DQK + (h + 1) * ML_DQK] for h in hs]
            v = [v_ref[rows, h * HEAD_W:(h + 1) * HEAD_W] for h in hs]
            c_st = [c_sc[h] for h in hs]
            n_st = [n_sc[h] for h in hs]
            m_full = [m_sc[h] for h in hs]
            r = _ml_chunk(q, k, v, gates, gc, rowsf, c_st, n_st, [m[:, 0:1] for m in m_full], tril_mask)
            ksc = [k[h] * r["ws"][h] for h in hs]
            new_c = [r["w_old"][h] * c_st[h] + _bdot_tn(ksc[h], v[h]) for h in hs]
            for h in hs:
                cs = slice(h * HEAD_W, (h + 1) * HEAD_W)
                cst_ref[c, h] = c_st[h]
                nst_ref[c, h] = n_st[h]
                mst_ref[c, h] = m_full[h]
                c_sc[h] = new_c[h]
                n_sc[h] = r["w_old"][h] * n_st[h] + jnp.sum(ksc[h], axis=0, keepdims=True)
                m_sc[h] = r["m_new"][h] + jnp.zeros((1, 128), f32)
                _, _, y = _head_rms(r["num"][h] / r["z"][h], gn_ref[:, cs], on_mxu=True)
                og_ref[rows, cs] = (y * _sigmoid(mo_ref[rows, h * HEAD_W:(h + 1) * HEAD_W])).astype(ACT)
            return carry

        lax.fori_loop(0, nc_blk, chunk, 0, unroll=True)

    nchunks = t // CHUNK
    return _riding_call(
        body, "mlstm_fwd", t // tb,
        in_specs=[_rows(tb, MIX_W), _rows(tb, MIX_W, 1), _rows(tb, MIX_W, 2), _rows(tb, 128, 12), _const((1, MIX_W))],
        out_specs=[_rows(tb, MIX_W),
                   pl.BlockSpec((nc_blk, HEADS, ML_DQK, HEAD_W), lambda i: (i, 0, 0, 0)),
                   pl.BlockSpec((nc_blk, HEADS, 1, ML_DQK), lambda i: (i, 0, 0, 0)),
                   pl.BlockSpec((nc_blk, HEADS, 1, 128), lambda i: (i, 0, 0, 0))],
        out_shape=[jax.ShapeDtypeStruct((t, MIX_W), ACT),
                   jax.ShapeDtypeStruct((nchunks, HEADS, ML_DQK, HEAD_W), f32),
                   jax.ShapeDtypeStruct((nchunks, HEADS, 1, ML_DQK), f32),
                   jax.ShapeDtypeStruct((nchunks, HEADS, 1, 128), f32)],
        scratch_shapes=[pltpu.VMEM((HEADS, ML_DQK, HEAD_W), f32), pltpu.VMEM((HEADS, 1, ML_DQK), f32), pltpu.VMEM((HEADS, 1, 128), f32)],
        operands=(qkc, u_ml, u_ml, u_ml, gn), riders=riders, copies=_gather_copies, ride_shapes=_gather_shapes(riders))


def _mlstm_bwd(qkc, u_ml, gn, cst, nst, mst, dog, riders=()):
    t = qkc.shape[0]
    tb = _tile(t, MIXER_ROWS)
    nb = t // tb
    nc_blk = tb // CHUNK

    def body(qk_ref, v_ref, mo_ref, gt_ref, gn_ref, cst_ref, nst_ref, mst_ref, dog_ref,
             dqk_ref, dv_ref, dmo_ref, dgt_ref, dgn_ref, dc_sc, dn_sc):
        @pl.when(pl.program_id(0) == 0)
        def _():
            dc_sc[...] = jnp.zeros_like(dc_sc)
            dn_sc[...] = jnp.zeros_like(dn_sc)
            dgn_ref[...] = jnp.zeros_like(dgn_ref)

        tril_mask = _tri(CHUNK)
        tri = tril_mask.astype(f32)
        triu = _tri(CHUNK, upper=True).astype(f32)
        lane = lax.broadcasted_iota(jnp.int32, (CHUNK, 128), 1)

        def chunk(j, carry):
            c = nc_blk - 1 - j
            r0 = pl.multiple_of(c * CHUNK, CHUNK)
            rows = pl.ds(r0, CHUNK)
            gates = gt_ref[rows, :]
            gc, rowsf = _ml_gate_forms(gates, tri)
            dg_mat = jnp.zeros((CHUNK, 128), f32)
            dig_mat = jnp.zeros((CHUNK, 128), f32)
            dlast_row = jnp.zeros((1, 128), f32)
            hs = range(HEADS)
            cols = [slice(h * HEAD_W, (h + 1) * HEAD_W) for h in hs]
            q = [qk_ref[rows, h * ML_DQK:(h + 1) * ML_DQK] * (ML_DQK ** -0.5) for h in hs]
            k = [qk_ref[rows, HEADS * ML_DQK + h * ML_DQK:HEADS * ML_DQK + (h + 1) * ML_DQK] for h in hs]
            v = [v_ref[rows, h * HEAD_W:(h + 1) * HEAD_W] for h in hs]
            c_st = [cst_ref[c, h] for h in hs]
            n_st = [nst_ref[c, h] for h in hs]
            m_st = [mst_ref[c, h][:, 0:1] for h in hs]
            dc = [dc_sc[h] for h in hs]
            dn = [dn_sc[h] for h in hs]
            r = _ml_chunk(q, k, v, gates, gc, rowsf, c_st, n_st, m_st, tril_mask)
            z, wi, wo, ws, w_old, den = r["z"], r["wi"], r["wo"], r["ws"], r["w_old"], r["den"]
            hh = [r["num"][h] / z[h] for h in hs]
            dh = []
            for h in hs:
                gnh = gn_ref[:, cols[h]]
                oh, rstd, y = _head_rms(hh[h], gnh, on_mxu=True)
                sg = _sigmoid(mo_ref[rows, h * HEAD_W:(h + 1) * HEAD_W])
                dogh = dog_ref[rows, cols[h]]
                dy = dogh * sg
                dmo_ref[rows, cols[h]] = (dogh * y * (sg * (1.0 - sg))).astype(ACT)
                dgn_ref[:, cols[h]] += jnp.sum(dy * oh, axis=0, keepdims=True)
                doh = dy * gnh
                dh.append(rstd * (doh - oh * (_lane_sum(doh * oh) * (1.0 / HEAD_W))))
            dnum = [dh[h] / z[h] for h in hs]
            dz = [-_lane_sum(dh[h] * hh[h]) / z[h] for h in hs]
            dden = [jnp.where(jnp.abs(den[h]) > r["floor"][h], dz[h] * jnp.sign(den[h]), 0.0) for h in hs]
            dsw = [(_bdot_nt(dnum[h], v[h]) + dden[h]) * wi[h] for h in hs]
            dq = [_bdot(dsw[h], k[h]) + wo[h] * (_bdot_nt(dnum[h], c_st[h]) + dden[h][:, :ML_DQK] * n_st[h]) for h in hs]
            dk_state = [ws[h] * (_bdot_nt(v[h], dc[h]) + dn[h]) for h in hs]
            dk = [_bdot_tn(dsw[h], q[h]) + dk_state[h] for h in hs]
            dv = [_bdot_tn(r["qk"][h], dnum[h]) + ws[h] * _bdot(k[h], dc[h]) for h in hs]
            woq = [wo[h] * q[h] for h in hs]
            new_dc = [w_old[h] * dc[h] + _bdot_tn(woq[h], dnum[h]) for h in hs]
            for h in hs:
                dv_ref[rows, cols[h]] = dv[h].astype(ACT)
                dc_sc[h] = new_dc[h]
                dn_sc[h] = w_old[h] * dn[h] + jnp.sum(woq[h] * dden[h][:, :ML_DQK], axis=0, keepdims=True)
                d_last = (jnp.sum(jnp.sum(k[h] * dk_state[h], axis=0, keepdims=True), axis=-1, keepdims=True)
                          + w_old[h] * (jnp.sum(jnp.sum(dc[h] * c_st[h], axis=0, keepdims=True), axis=-1, keepdims=True)
                                        + jnp.sum(dn[h] * n_st[h], axis=-1, keepdims=True)))
                kdk = _lane_sum(k[h] * dk[h])
                qdq = _lane_sum(q[h] * dq[h])
                dg_mat = dg_mat + jnp.where(lane == HEADS + h, qdq - kdk, 0.0)
                dlast_row = dlast_row + jnp.where(lane[0:1] == HEADS + h, d_last, 0.0)
                dig_mat = dig_mat + jnp.where(lane == h, kdk, 0.0)
                dqk_ref[rows, h * ML_DQK:(h + 1) * ML_DQK] = dq[h] * (ML_DQK ** -0.5)
                dqk_ref[rows, HEADS * ML_DQK + h * ML_DQK:HEADS * ML_DQK + (h + 1) * ML_DQK] = dk[h]
            dlf = _sel_dot(triu, dg_mat) + dlast_row
            dgt_ref[rows, :] = (dig_mat + dlf * _sigmoid(-gates)).astype(ACT)
            return carry

        lax.fori_loop(0, nc_blk, chunk, 0, unroll=True)

    st4 = lambda a, b: pl.BlockSpec((nc_blk, HEADS, a, b), lambda i: (nb - 1 - i, 0, 0, 0))
    return _riding_call(
        body, "mlstm_bwd", nb,
        in_specs=[_rows_rev(tb, MIX_W, nb), _rows_rev(tb, MIX_W, nb, 1), _rows_rev(tb, MIX_W, nb, 2), _rows_rev(tb, 128, nb, 12),
                  _const((1, MIX_W)), st4(ML_DQK, HEAD_W), st4(1, ML_DQK), st4(1, 128), _rows_rev(tb, MIX_W, nb)],
        out_specs=[_rows_rev(tb, MIX_W, nb), _rows_rev(tb, MIX_W, nb), _rows_rev(tb, MIX_W, nb), _rows_rev(tb, 128, nb), _const((1, MIX_W))],
        out_shape=[jax.ShapeDtypeStruct((t, MIX_W), f32), jax.ShapeDtypeStruct((t, MIX_W), ACT), jax.ShapeDtypeStruct((t, MIX_W), ACT),
                   jax.ShapeDtypeStruct((t, 128), ACT), jax.ShapeDtypeStruct((1, MIX_W), f32)],
        scratch_shapes=[pltpu.VMEM((HEADS, ML_DQK, HEAD_W), f32), pltpu.VMEM((HEADS, 1, ML_DQK), f32)],
        operands=(qkc, u_ml, u_ml, u_ml, gn, cst, nst, mst, dog), riders=riders, copies=_scatter_copies,
        ride_shapes=[jax.ShapeDtypeStruct(r.shape, r.dtype) for r in riders])


def _ln_fwd(r, g, b):
    mu = jnp.mean(r, axis=-1, keepdims=True)
    xc = r - mu
    rstd = lax.rsqrt(jnp.mean(xc * xc, axis=-1, keepdims=True) + LN_EPS)
    xh = xc * rstd
    return xh * g + b, xh, rstd


def _ln_bwd(dy, xh, rstd, g):
    dxh = dy * g
    return rstd * (dxh - jnp.mean(dxh, axis=-1, keepdims=True) - xh * jnp.mean(dxh * xh, axis=-1, keepdims=True))


def _outproj_ln1(og_hg, og_ml, x, w_out, g, b, riders=()):
    t = x.shape[0]
    tm = _tile(t, DENSE_ROWS)

    def body(a_ref, b_ref, x_ref, w_ref, g_ref, bb_ref, x1_ref, xh_ref, rs_ref, x1b_ref):
        mix = _bdot(a_ref[...], w_ref[0:MIX_W, :]) + _bdot(b_ref[...], w_ref[MIX_W:2 * MIX_W, :])
        y, xh, rstd = _ln_fwd(ALPHA * x_ref[...] + mix, g_ref[...], bb_ref[...])
        x1_ref[...] = y
        x1b_ref[...] = y.astype(ACT)
        xh_ref[...] = xh.astype(ACT)
        rs_ref[...] = rstd

    return _riding_call(
        body, "outproj_ln1", t // tm,
        in_specs=[_rows(tm, MIX_W), _rows(tm, MIX_W), _rows(tm, D_MODEL), _resident((D_MODEL, D_MODEL)), _const((1, D_MODEL)), _const((1, D_MODEL))],
        out_specs=[_rows(tm, D_MODEL), _rows(tm, D_MODEL), _rows(tm, 1), _rows(tm, D_MODEL)],
        out_shape=[jax.ShapeDtypeStruct((t, D_MODEL), f32), jax.ShapeDtypeStruct((t, D_MODEL), ACT), jax.ShapeDtypeStruct((t, 1), f32),
                   jax.ShapeDtypeStruct((t, D_MODEL), ACT)],
        scratch_shapes=[], operands=(og_hg, og_ml, x, w_out, g, b), riders=riders, copies=_gather_copies, ride_shapes=_gather_shapes(riders))


def _ffn_up(x1, wg, wu, riders=()):
    t = x1.shape[0]
    tm = _tile(t, DENSE_ROWS)

    def body(x_ref, wg_ref, wu_ref, hg_ref, up_ref, a_ref):
        xv = x_ref[...]
        hg = _bdot_nt(xv, wg_ref[...])
        up = _bdot_nt(xv, wu_ref[...])
        hg_ref[...] = hg.astype(ACT)
        up_ref[...] = up.astype(ACT)
        a_ref[...] = (hg * _sigmoid(hg) * up).astype(ACT)

    return _riding_call(
        body, "ffn_up", t // tm,
        in_specs=[_rows(tm, D_MODEL), _resident((D_FF, D_MODEL)), _resident((D_FF, D_MODEL))],
        out_specs=[_rows(tm, D_FF), _rows(tm, D_FF), _rows(tm, D_FF)],
        out_shape=[jax.ShapeDtypeStruct((t, D_FF), ACT), jax.ShapeDtypeStruct((t, D_FF), ACT), jax.ShapeDtypeStruct((t, D_FF), ACT)],
        scratch_shapes=[], operands=(x1, wg, wu), riders=riders, copies=_gather_copies, ride_shapes=_gather_shapes(riders))


def _ffn_down_ln2(a, x1, wd, g, b):
    t = x1.shape[0]
    tm = _tile(t, DENSE_ROWS)

    def body(a_ref, x_ref, w_ref, g_ref, bb_ref, x2_ref, xh_ref, rs_ref, x2b_ref):
        ffn = _bdot(a_ref[...], w_ref[...])
        y, xh, rstd = _ln_fwd(ALPHA * x_ref[...] + ffn, g_ref[...], bb_ref[...])
        x2_ref[...] = y
        x2b_ref[...] = y.astype(ACT)
        xh_ref[...] = xh.astype(ACT)
        rs_ref[...] = rstd

    return pl.pallas_call(
        body, name="ffn_down_ln2", grid=(t // tm,),
        in_specs=[_rows(tm, D_FF), _rows(tm, D_MODEL), _resident((D_FF, D_MODEL)), _const((1, D_MODEL)), _const((1, D_MODEL))],
        out_specs=[_rows(tm, D_MODEL), _rows(tm, D_MODEL), _rows(tm, 1), _rows(tm, D_MODEL)],
        out_shape=[jax.ShapeDtypeStruct((t, D_MODEL), f32), jax.ShapeDtypeStruct((t, D_MODEL), ACT), jax.ShapeDtypeStruct((t, 1), f32),
                   jax.ShapeDtypeStruct((t, D_MODEL), ACT)],
        compiler_params=_cparams(1, arbitrary=False),
    )(a, x1, wd, g, b)


def _head_loss_bwd(x2, xh2, rs2, p, tgt, w_pg, b_pg, w_pp, g2):
    t = x2.shape[0]
    tm = _tile(t, DENSE_ROWS)

    def body(x_ref, xh_ref, rs_ref, p_ref, t_ref, wg_ref, bg_ref, wp_ref, g_ref,
             dr_ref, de_ref, dz_ref, loss_ref, dbg_ref, dg2_ref, db2_ref):
        @pl.when(pl.program_id(0) == 0)
        def _():
            loss_ref[...] = jnp.zeros_like(loss_ref)
            dbg_ref[...] = jnp.zeros_like(dbg_ref)
            dg2_ref[...] = jnp.zeros_like(dg2_ref)
            db2_ref[...] = jnp.zeros_like(db2_ref)

        x2v = x_ref[...]
        z = _bdot(x2v, wg_ref[...]) + bg_ref[...]
        e = _bdot(p_ref[...], wp_ref[...])
        sg = _sigmoid(z)
        diff = x2v + sg * e - t_ref[...]
        loss_ref[...] += 0.5 * jnp.sum(jnp.mean(diff * diff, axis=-1, keepdims=True), axis=0, keepdims=True)
        dy = diff * (1.0 / D_MODEL)
        de_ref[...] = (dy * sg).astype(ACT)
        dz = dy * e * (sg * (1.0 - sg))
        dz_ref[...] = dz.astype(ACT)
        dbg_ref[...] += jnp.sum(dz, axis=0, keepdims=True)
        dx2 = dy + _bdot_nt(dz, wg_ref[...])
        xh = xh_ref[...].astype(f32)
        dg2_ref[...] += jnp.sum(dx2 * xh, axis=0, keepdims=True)
        db2_ref[...] += jnp.sum(dx2, axis=0, keepdims=True)
        dr_ref[...] = _ln_bwd(dx2, xh, rs_ref[...], g_ref[...])

    row = jax.ShapeDtypeStruct((1, D_MODEL), f32)
    return pl.pallas_call(
        body, name="head_loss_bwd", grid=(t // tm,),
        in_specs=[_rows(tm, D_MODEL), _rows(tm, D_MODEL), _rows(tm, 1), _rows(tm, PLE), _rows(tm, D_MODEL),
                  _resident((D_MODEL, D_MODEL)), _const((1, D_MODEL)), _resident((PLE, D_MODEL)), _const((1, D_MODEL))],
        out_specs=[_rows(tm, D_MODEL), _rows(tm, D_MODEL), _rows(tm, D_MODEL), _const((1, 1)), _const((1, D_MODEL)), _const((1, D_MODEL)), _const((1, D_MODEL))],
        out_shape=[jax.ShapeDtypeStruct((t, D_MODEL), f32), jax.ShapeDtypeStruct((t, D_MODEL), ACT), jax.ShapeDtypeStruct((t, D_MODEL), ACT),
                   jax.ShapeDtypeStruct((1, 1), f32), row, row, row],
        compiler_params=_cparams(1),
    )(x2, xh2, rs2, p, tgt, w_pg, b_pg, w_pp, g2)


def _ffn_bwd(dr2, hg, up, xh1, rs1, wd, wg, wu, g1, w_out):
    t = dr2.shape[0]
    tm = _tile(t, DENSE_ROWS // 2)

    def body(dr_ref, hg_ref, up_ref, xh_ref, rs_ref, wd_ref, wg_ref, wu_ref, g_ref, wo_ref,
             dr1_ref, dhg_ref, dup_ref, dg1_ref, db1_ref, doghg_ref, dogml_ref):
        @pl.when(pl.program_id(0) == 0)
        def _():
            dg1_ref[...] = jnp.zeros_like(dg1_ref)
            db1_ref[...] = jnp.zeros_like(db1_ref)

        dr2v = dr_ref[...]
        da = _bdot_nt(dr2v, wd_ref[...])
        hgv = hg_ref[...].astype(f32)
        sg = _sigmoid(hgv)
        dhg = da * up_ref[...].astype(f32) * (sg * (1.0 + hgv * (1.0 - sg)))
        dup = da * (hgv * sg)
        dhg_ref[...] = dhg.astype(ACT)
        dup_ref[...] = dup.astype(ACT)
        dx1 = ALPHA * dr2v + _bdot(dhg, wg_ref[...]) + _bdot(dup, wu_ref[...])
        xh = xh_ref[...].astype(f32)
        dg1_ref[...] += jnp.sum(dx1 * xh, axis=0, keepdims=True)
        db1_ref[...] += jnp.sum(dx1, axis=0, keepdims=True)
        dr1 = _ln_bwd(dx1, xh, rs_ref[...], g_ref[...])
        dr1_ref[...] = dr1
        dog = _bdot_nt(dr1, wo_ref[...])
        doghg_ref[...] = dog[:, 0:MIX_W]
        dogml_ref[...] = dog[:, MIX_W:2 * MIX_W]

    row = jax.ShapeDtypeStruct((1, D_MODEL), f32)
    return pl.pallas_call(
        body, name="ffn_bwd", grid=(t // tm,),
        in_specs=[_rows(tm, D_MODEL), _rows(tm, D_FF), _rows(tm, D_FF), _rows(tm, D_MODEL), _rows(tm, 1),
                  _resident((D_FF, D_MODEL)), _resident((D_FF, D_MODEL)), _resident((D_FF, D_MODEL)), _const((1, D_MODEL)),
                  _resident((D_MODEL, D_MODEL))],
        out_specs=[_rows(tm, D_MODEL), _rows(tm, D_FF), _rows(tm, D_FF), _const((1, D_MODEL)), _const((1, D_MODEL)),
                   _rows(tm, MIX_W), _rows(tm, MIX_W)],
        out_shape=[jax.ShapeDtypeStruct((t, D_MODEL), f32), jax.ShapeDtypeStruct((t, D_FF), ACT), jax.ShapeDtypeStruct((t, D_FF), ACT), row, row,
                   jax.ShapeDtypeStruct((t, MIX_W), f32), jax.ShapeDtypeStruct((t, MIX_W), f32)],
        compiler_params=_cparams(1),
    )(dr2, hg, up, xh1, rs1, wd, wg, wu, g1, w_out)


def _inproj_bwd(dr1, du_hg, dqk, dmv, dmo, dgt, w_hg, w_ml):
    t = dr1.shape[0]
    tm = _tile(t, DENSE_ROWS)

    def body(dr_ref, dhg_ref, dqk_ref, dmv_ref, dmo_ref, dgt_ref, whg_ref, wml_ref, gx_ref, dml_ref):
        dml = jnp.concatenate([dqk_ref[...], dmv_ref[...], dmo_ref[...], dgt_ref[...]], axis=-1).astype(ACT)
        dml_ref[...] = dml
        gx_ref[...] = ALPHA * dr_ref[...] + _bdot(dhg_ref[...], whg_ref[...]) + _bdot(dml, wml_ref[...])

    return pl.pallas_call(
        body, name="inproj_bwd", grid=(t // tm,),
        in_specs=[_rows(tm, D_MODEL), _rows(tm, U_HG), _rows(tm, MIX_W), _rows(tm, MIX_W), _rows(tm, MIX_W), _rows(tm, 128),
                  _resident((U_HG, D_MODEL)), _resident((U_ML, D_MODEL))],
        out_specs=[_rows(tm, D_MODEL), _rows(tm, U_ML)],
        out_shape=[jax.ShapeDtypeStruct((t, D_MODEL), f32), jax.ShapeDtypeStruct((t, U_ML), ACT)],
        compiler_params=_cparams(1, arbitrary=False),
    )(dr1, du_hg, dqk, dmv, dmo, dgt, w_hg, w_ml)


def _wgrad(a, b, name, tk=None, tn=None, colsum=False, low=False):
    t, kdim = a.shape
    n = b.shape[1]
    tk = tk or kdim
    tn = tn or n
    tt = _tile(t, WGRAD_ROWS)
    nt = t // tt
    assert not (colsum and low) and (not colsum or tn == n)

    def body(a_ref, b_ref, o_ref, *s_ref):
        @pl.when(pl.program_id(2) == 0)
        def _():
            o_ref[...] = jnp.zeros_like(o_ref)
            if colsum:
                s_ref[0][...] = jnp.zeros_like(s_ref[0])

        av = a_ref[...]
        o_ref[...] += _bdot_tn(av, b_ref[...])
        if colsum:
            s_ref[0][...] += jnp.sum(av.astype(f32), axis=0, keepdims=True)
        if low:
            @pl.when(pl.program_id(2) == nt - 1)
            def _():
                s_ref[0][...] = o_ref[...].astype(bf16)

    out_specs = [pl.BlockSpec((tk, tn), lambda i, j, s: (i, j))]
    out_shape = [jax.ShapeDtypeStruct((kdim, n), f32)]
    if colsum:
        out_specs.append(pl.BlockSpec((1, tk), lambda i, j, s: (0, i)))
        out_shape.append(jax.ShapeDtypeStruct((1, kdim), f32))
    if low:
        out_specs.append(pl.BlockSpec((tk, tn), lambda i, j, s: (i, j)))
        out_shape.append(jax.ShapeDtypeStruct((kdim, n), bf16))
    res = pl.pallas_call(
        body, name=name, grid=(kdim // tk, n // tn, t // tt),
        in_specs=[pl.BlockSpec((tt, tk), lambda i, j, s: (s, i)), pl.BlockSpec((tt, tn), lambda i, j, s: (s, j))],
        out_specs=out_specs, out_shape=out_shape,
        compiler_params=_cparams(3),
    )(a, b)
    return res if (colsum or low) else res[0]


_TRANSPOSED = {"w_in", "w_ffn_gate", "w_ffn_up"}
_COL_SPLIT = {"ple_w_proj"}
_SCATTER_PLAN = (("w_ffn_gate", "w_ffn_up"), ("w_ffn_down", "w_out", "ple_w_gate", "ple_w_proj"))
_RIDE_PLAN = {"inproj": ("w_ffn_gate",), "hgrn2_fwd": ("w_ffn_up",), "mlstm_fwd": ("w_out",),
              "outproj_ln1": ("ple_w_gate", "ple_w_proj"), "ffn_up": ("w_ffn_down",)}


def _from_chip_major(a, col_split):
    if col_split:
        return a.transpose(1, 0, 2).reshape(a.shape[1], 4 * a.shape[2])
    return a.reshape(4 * a.shape[1], a.shape[2])


def _local_step(x, p, tgt, w_in_b, b_in, logits, conv_w, conv_b, hg_gn, ml_gn, w_out_b, ln1_g, ln1_b,
                wg_b, wu_b, wd_b, ln2_g, ln2_b, w_pp_b, w_pg_b, b_pg, early_hook=None, late_shards=None):
    pad_w = U_HG + U_ML - PROJ_W
    w_hg = w_in_b[:U_HG]
    w_ml = jnp.pad(w_in_b[U_HG:], ((0, pad_w), (0, 0)))
    bb_hg = b_in[:, :U_HG]
    bb_ml = jnp.pad(b_in[:, U_HG:], ((0, 0), (0, pad_w)))

    late = dict(w_out=w_out_b, w_ffn_gate=wg_b, w_ffn_up=wu_b, w_ffn_down=wd_b, ple_w_proj=w_pp_b, ple_w_gate=w_pg_b)

    def riders_of(call):
        return [late_shards[k] for k in _RIDE_PLAN[call]] if late_shards is not None else ()

    def arrived(call, got):
        for k, g in zip(_RIDE_PLAN[call], got):
            late[k] = _from_chip_major(g, k in _COL_SPLIT)

    (u_hg, u_ml, xb), got = _inproj(x, w_hg, w_ml, bb_hg, bb_ml, riders_of("inproj"))
    arrived("inproj", got)
    (og_hg, sst, hg_b, hg_a, hg_o), got = _hgrn2_fwd(u_hg, logits, hg_gn, riders_of("hgrn2_fwd"))
    arrived("hgrn2_fwd", got)
    pre, qkc = _conv_fwd(u_ml, conv_w, conv_b)
    (og_ml, cst, nst, mst), got = _mlstm_fwd(qkc, u_ml, ml_gn, riders_of("mlstm_fwd"))
    arrived("mlstm_fwd", got)
    (x1, xh1, rs1, x1b), got = _outproj_ln1(og_hg, og_ml, x, late["w_out"], ln1_g, ln1_b, riders_of("outproj_ln1"))
    arrived("outproj_ln1", got)
    (hgp, up, act), got = _ffn_up(x1b, late["w_ffn_gate"], late["w_ffn_up"], riders_of("ffn_up"))
    arrived("ffn_up", got)
    w_out_b, wg_b, wu_b, wd_b = late["w_out"], late["w_ffn_gate"], late["w_ffn_up"], late["w_ffn_down"]
    w_pp_b, w_pg_b = late["ple_w_proj"], late["ple_w_gate"]
    x2, xh2, rs2, x2b = _ffn_down_ln2(act, x1, wd_b, ln2_g, ln2_b)
    dr2, de, dz, loss, d_bpg, d_ln2g, d_ln2b = _head_loss_bwd(x2, xh2, rs2, p, tgt, w_pg_b, b_pg, w_pp_b, ln2_g)
    dr1, dhg, dup, d_ln1g, d_ln1b, dog_hg, dog_ml = _ffn_bwd(dr2, hgp, up, xh1, rs1, wd_b, wg_b, wu_b, ln1_g, w_out_b)

    d_wo_a, lo_wo_a = _wgrad(og_hg, dr1, "wgrad_out_hg", low=True)
    d_wo_b, lo_wo_b = _wgrad(og_ml, dr1, "wgrad_out_ml", low=True)
    d_wg, lo_wg = _wgrad(dhg, x1b, "wgrad_ffn_gate", tk=D_FF // 2, low=True)
    d_wu, lo_wu = _wgrad(dup, x1b, "wgrad_ffn_up", tk=D_FF // 2, low=True)
    d_wd, lo_wd = _wgrad(act, dr2, "wgrad_ffn_down", tk=D_FF // 2, low=True)
    d_wpp, lo_wpp = _wgrad(p, de, "wgrad_ple_proj", low=True)
    d_wpg, lo_wpg = _wgrad(x2b, dz, "wgrad_ple_gate", low=True)
    early = dict(w_out=jnp.concatenate([d_wo_a, d_wo_b], axis=0), w_ffn_gate=d_wg, w_ffn_up=d_wu, w_ffn_down=d_wd,
                 ple_w_proj=d_wpp, ple_w_gate=d_wpg)
    early_low = dict(w_out=jnp.concatenate([lo_wo_a, lo_wo_b], axis=0), w_ffn_gate=lo_wg, w_ffn_up=lo_wu, w_ffn_down=lo_wd,
                     ple_w_proj=lo_wpp, ple_w_gate=lo_wpg)
    ride_hg, ride_ml = early_hook(early_low) if early_hook is not None else ((), ())

    (du_hg, d_logits, d_hg_gn), got_hg = _hgrn2_bwd(u_hg, logits, hg_gn, sst, hg_b, hg_a, hg_o, dog_hg, ride_hg)
    (dqkc, dmv, dmo, dgt, d_ml_gn), got_ml = _mlstm_bwd(qkc, u_ml, ml_gn, cst, nst, mst, dog_ml, ride_ml)
    dqk, d_conv_w, d_conv_b = _conv_bwd(u_ml, conv_w, pre, dqkc)
    grad_x, du_ml = _inproj_bwd(dr1, du_hg, dqk, dmv, dmo, dgt, w_hg, w_ml)

    dw_hg, db_hg = _wgrad(du_hg, xb, "wgrad_in_hg", tk=U_HG // 2, colsum=True)
    dw_ml, db_ml = _wgrad(du_ml, xb, "wgrad_in_ml", colsum=True)
    d_w_in = jnp.concatenate([dw_hg, dw_ml[:PROJ_W - U_HG]], axis=0)
    d_b_in = jnp.concatenate([db_hg, db_ml[:, :PROJ_W - U_HG]], axis=1)

    grads = dict(w_in=d_w_in, b_in=d_b_in, hg_lb_logits=d_logits, ml_conv_w=d_conv_w, ml_conv_b=d_conv_b,
                 hg_norm_g=d_hg_gn, ml_norm_g=d_ml_gn, ln1_g=d_ln1g, ln1_b=d_ln1b, ln2_g=d_ln2g, ln2_b=d_ln2b,
                 ple_b_gate=d_bpg, **early)
    return loss, grad_x, grads, (list(got_hg), list(got_ml))


_ANY = pl.BlockSpec(memory_space=pltpu.HBM)
_MESH = pl.DeviceIdType.MESH


def _my_place():
    return lax.axis_index("x"), lax.axis_index("y"), lax.axis_index("c")


def _other_chips(x, y):
    return [(1 - x, y), (x, 1 - y), (1 - x, 1 - y)]


_VMEM = pl.BlockSpec(memory_space=pltpu.VMEM)
_EX_ROWS = 32


def _pair_reduce_cols(p, name):
    s, r, c = p.shape
    hc = c // 2

    def body(p_ref, o_ref, other, send_sem, recv_sem):
        x, y, cc = _my_place()

        def run(mine_lo, theirs_lo):
            cp = pltpu.make_async_remote_copy(src_ref=p_ref.at[pl.ds(0, s), pl.ds(0, r), pl.ds(theirs_lo, hc)], dst_ref=other,
                                              send_sem=send_sem, recv_sem=recv_sem, device_id=(x, y, 1 - cc), device_id_type=_MESH)
            cp.start()
            cp.wait()
            for slot in range(s):
                o_ref[slot] = (p_ref[slot, :, mine_lo:mine_lo + hc] + other[slot]).astype(bf16)

        @pl.when(cc == 0)
        def _():
            run(0, hc)

        @pl.when(cc == 1)
        def _():
            run(hc, 0)

    return pl.pallas_call(
        body, name=name, in_specs=[_VMEM], out_specs=_VMEM,
        out_shape=jax.ShapeDtypeStruct((s, r, hc), bf16),
        scratch_shapes=[pltpu.VMEM((s, r, hc), f32), pltpu.SemaphoreType.DMA, pltpu.SemaphoreType.DMA],
        compiler_params=pltpu.CompilerParams(vmem_limit_bytes=VMEM_LIMIT),
    )(p)


def _chip_reduce_swap_cols(rcv, name):
    s, r, hc = rcv.shape

    def body(r_ref, g_ref, send_sem, recv_sem):
        x, y, cc = _my_place()
        acc = r_ref[0].astype(f32)
        for slot in range(1, s):
            acc = acc + r_ref[slot].astype(f32)
        g_ref[cc] = acc
        cp = pltpu.make_async_remote_copy(src_ref=g_ref.at[cc], dst_ref=g_ref.at[cc], send_sem=send_sem, recv_sem=recv_sem,
                                          device_id=(x, y, 1 - cc), device_id_type=_MESH)
        cp.start()
        cp.wait()

    both = pl.pallas_call(
        body, name=name, in_specs=[_VMEM], out_specs=_VMEM,
        out_shape=jax.ShapeDtypeStruct((2, r, hc), f32),
        scratch_shapes=[pltpu.SemaphoreType.DMA, pltpu.SemaphoreType.DMA],
        compiler_params=pltpu.CompilerParams(vmem_limit_bytes=VMEM_LIMIT),
    )(rcv)
    return both.transpose(1, 0, 2).reshape(r, 2 * hc)


def _reduce_adamw(rcv, w, m, v, name):
    s, r, c = rcv.shape
    rows_per = _EX_ROWS
    half = r // 2
    steps = half // rows_per

    def body(r_ref, w_ref, m_ref, v_ref, g_ref, d_ref, nm_ref, nv_ref, mine, theirs, send_sems, recv_sems):
        x, y, cc = _my_place()

        def swap(k):
            rs = pl.ds(k * half, half)
            return pltpu.make_async_remote_copy(src_ref=mine.at[rs], dst_ref=theirs.at[rs], send_sem=send_sems.at[k], recv_sem=recv_sems.at[k],
                                                device_id=(x, y, 1 - cc), device_id_type=_MESH)

        def chip_sum(i, carry):
            rs = pl.ds(pl.multiple_of(i * rows_per, rows_per), rows_per)
            acc = r_ref[0, rs, :].astype(f32)
            for slot in range(1, s):
                acc = acc + r_ref[slot, rs, :].astype(f32)
            mine[rs, :] = acc
            return carry

        def update(i, carry):
            rs = pl.ds(pl.multiple_of(i * rows_per, rows_per), rows_per)
            g = mine[rs, :] + theirs[rs, :]
            nm = B1 * m_ref[rs, :] + (1.0 - B1) * g
            nv = B2 * v_ref[rs, :] + (1.0 - B2) * (g * g)
            g_ref[rs, :] = g
            nm_ref[rs, :] = nm
            nv_ref[rs, :] = nv
            d_ref[rs, :] = -LR * ((nm / (1.0 - B1 ** STEP)) / (jnp.sqrt(nv / (1.0 - B2 ** STEP)) + EPS_ADAM) + WD * w_ref[rs, :])
            return carry

        lax.fori_loop(0, steps, chip_sum, 0)
        swap(0).start()
        lax.fori_loop(steps, 2 * steps, chip_sum, 0)
        swap(1).start()
        swap(0).wait()
        lax.fori_loop(0, steps, update, 0)
        swap(1).wait()
        lax.fori_loop(steps, 2 * steps, update, 0)

    return pl.pallas_call(
        body, name=name, in_specs=[_VMEM] * 4, out_specs=[_VMEM] * 4,
        out_shape=[jax.ShapeDtypeStruct((r, c), f32)] * 4,
        scratch_shapes=[pltpu.VMEM((r, c), f32), pltpu.VMEM((r, c), f32), pltpu.SemaphoreType.DMA((2,)), pltpu.SemaphoreType.DMA((2,))],
        compiler_params=pltpu.CompilerParams(vmem_limit_bytes=VMEM_LIMIT),
    )(rcv, w, m, v)


def _gather_copies(ins, outs, send_sems, recv_sems, local_sems):
    x, y, c = _my_place()
    me = 2 * x + y
    local, outgoing, incoming = [], [], []
    for a in range(len(ins)):
        local.append(pltpu.make_async_copy(ins[a], outs[a].at[me], local_sems.at[a]))
        for j, (px, py) in enumerate(_other_chips(x, y)):
            sems = dict(send_sem=send_sems.at[3 * a + j], recv_sem=recv_sems.at[3 * a + j], device_id=(px, py, c), device_id_type=_MESH)
            outgoing.append(pltpu.make_async_remote_copy(src_ref=ins[a], dst_ref=outs[a].at[me], **sems))
            incoming.append(pltpu.make_async_remote_copy(src_ref=ins[a], dst_ref=outs[a].at[2 * px + py], **sems))
    return local, outgoing, incoming


def _gather_first(block, taps, name):
    r, c = block.shape
    hc = c // 2

    def body(in_ref, tap_in, out_ref, tap_out, send_sems, recv_sems):
        x, y, cc = _my_place()
        me = 2 * x + y
        sibling = (x, y, 1 - cc)
        chips = _other_chips(x, y)
        out_ref[me] = in_ref[...]
        tap_out[me] = tap_in[...]

        def run(mine, theirs):
            def ici(j, chip):
                px, py = chips[j]
                src = in_ref.at[pl.ds(0, r), pl.ds(mine, hc)] if chip is None else out_ref.at[chip, pl.ds(0, r), pl.ds(mine, hc)]
                dst = out_ref.at[me if chip is None else chip, pl.ds(0, r), pl.ds(mine, hc)]
                return pltpu.make_async_remote_copy(src_ref=src, dst_ref=dst, send_sem=send_sems.at[j], recv_sem=recv_sems.at[j],
                                                    device_id=(px, py, cc), device_id_type=_MESH)

            def d2d(j, lo):
                px, py = chips[j]
                blk = out_ref.at[2 * px + py, pl.ds(0, r), pl.ds(lo, hc)]
                return pltpu.make_async_remote_copy(src_ref=blk, dst_ref=blk, send_sem=send_sems.at[3 + j], recv_sem=recv_sems.at[3 + j],
                                                    device_id=sibling, device_id_type=_MESH)

            def tap(j, chip):
                px, py = chips[j]
                return pltpu.make_async_remote_copy(src_ref=tap_in, dst_ref=tap_out.at[me if chip is None else chip],
                                                    send_sem=send_sems.at[6 + j], recv_sem=recv_sems.at[6 + j],
                                                    device_id=(px, py, cc), device_id_type=_MESH)

            for j in range(3):
                ici(j, None).start()
                tap(j, None).start()
            for j, (px, py) in enumerate(chips):
                ici(j, 2 * px + py).wait_recv()
                d2d(j, mine).start()
            for j, (px, py) in enumerate(chips):
                d2d(j, theirs).wait_recv()
                tap(j, 2 * px + py).wait_recv()
            for j in range(3):
                ici(j, None).wait_send()
                d2d(j, mine).wait_send()
                tap(j, None).wait_send()

        @pl.when(cc == 0)
        def _():
            run(0, hc)

        @pl.when(cc == 1)
        def _():
            run(hc, 0)

    return pl.pallas_call(
        body, name=name, in_specs=[_VMEM, _VMEM], out_specs=[_VMEM, _VMEM],
        out_shape=[jax.ShapeDtypeStruct((4, r, c), block.dtype), jax.ShapeDtypeStruct((4,) + taps.shape, taps.dtype)],
        scratch_shapes=[pltpu.SemaphoreType.DMA((9,)), pltpu.SemaphoreType.DMA((9,))],
        compiler_params=pltpu.CompilerParams(vmem_limit_bytes=VMEM_LIMIT),
    )(block, taps)


def _riding_call(body, name, nsteps, in_specs, out_specs, out_shape, scratch_shapes, operands, riders, copies, ride_shapes):
    nr, n_in, n_out, n_scr = len(riders), len(in_specs), len(out_specs), len(scratch_shapes)

    def wrapped(*refs):
        ins, ride_in = refs[:n_in], refs[n_in:n_in + nr]
        outs, ride_out = refs[n_in + nr:n_in + nr + n_out], refs[n_in + nr + n_out:n_in + 2 * nr + n_out]
        scratch, sems = refs[n_in + 2 * nr + n_out:n_in + 2 * nr + n_out + n_scr], refs[n_in + 2 * nr + n_out + n_scr:]
        if nr:
            @pl.when(pl.program_id(0) == 0)
            def _():
                local, outgoing, _ = copies(ride_in, ride_out, *sems)
                for cp in local + outgoing:
                    cp.start()

        body(*ins, *outs, *scratch)
        if nr:
            @pl.when(pl.program_id(0) == nsteps - 1)
            def _():
                local, outgoing, incoming = copies(ride_in, ride_out, *sems)
                for cp in incoming:
                    cp.wait_recv()
                for cp in outgoing:
                    cp.wait_send()
                for cp in local:
                    cp.wait()

    hbm = pl.BlockSpec(memory_space=pltpu.HBM)
    sems = [pltpu.SemaphoreType.DMA((3 * nr,)), pltpu.SemaphoreType.DMA((3 * nr,)), pltpu.SemaphoreType.DMA((nr,))] if nr else []
    res = pl.pallas_call(
        wrapped, name=name, grid=(nsteps,),
        in_specs=list(in_specs) + [hbm] * nr, out_specs=list(out_specs) + [hbm] * nr,
        out_shape=list(out_shape) + list(ride_shapes),
        scratch_shapes=list(scratch_shapes) + sems,
        compiler_params=_cparams(1),
    )(*operands, *riders)
    return list(res[:n_out]), list(res[n_out:])


def _gather_shapes(riders):
    return [jax.ShapeDtypeStruct((4,) + r.shape, r.dtype) for r in riders]


def _scatter_copies(ins, outs, send_sems, recv_sems, local_sems):
    x, y, c = _my_place()
    me = 2 * x + y
    local, outgoing, incoming = [], [], []
    for a in range(len(ins)):
        local.append(pltpu.make_async_copy(ins[a].at[me], outs[a].at[me], local_sems.at[a]))
        for j, (px, py) in enumerate(_other_chips(x, y)):
            sems = dict(send_sem=send_sems.at[3 * a + j], recv_sem=recv_sems.at[3 * a + j], device_id=(px, py, c), device_id_type=_MESH)
            outgoing.append(pltpu.make_async_remote_copy(src_ref=ins[a].at[2 * px + py], dst_ref=outs[a].at[me], **sems))
            incoming.append(pltpu.make_async_remote_copy(src_ref=ins[a].at[2 * px + py], dst_ref=outs[a].at[2 * px + py], **sems))
    return local, outgoing, incoming


def _scatter_chips(pieces, name):
    n = len(pieces)

    def body(*refs):
        local, outgoing, incoming = _scatter_copies(refs[:n], refs[n:2 * n], *refs[2 * n:])
        for cp in local + outgoing:
            cp.start()
        for cp in incoming:
            cp.wait_recv()
        for cp in outgoing:
            cp.wait_send()
        for cp in local:
            cp.wait()

    return pl.pallas_call(
        body, name=name,
        in_specs=[_ANY] * n, out_specs=[_ANY] * n,
        out_shape=[jax.ShapeDtypeStruct(s.shape, s.dtype) for s in pieces],
        scratch_shapes=[pltpu.SemaphoreType.DMA((3 * n,)), pltpu.SemaphoreType.DMA((3 * n,)), pltpu.SemaphoreType.DMA((n,))],
    )(*pieces)


def _gather_all(block, name):
    def body(in_ref, out_ref, send_sems, recv_sems, local_sem):
        x, y, c = _my_place()
        me = 4 * x + 2 * y + c
        cp = pltpu.make_async_copy(in_ref, out_ref.at[me], local_sem)
        cp.start()
        peers = []
        for dx in range(2):
            for dy in range(2):
                for dc in range(2):
                    if dx or dy or dc:
                        peers.append((1 - x if dx else x, 1 - y if dy else y, 1 - c if dc else c))
        for j, pr in enumerate(peers):
            pltpu.make_async_remote_copy(src_ref=in_ref, dst_ref=out_ref.at[me], send_sem=send_sems.at[j], recv_sem=recv_sems.at[j],
                                         device_id=pr, device_id_type=_MESH).start()
        for j, (px, py, pc) in enumerate(peers):
            pltpu.make_async_remote_copy(src_ref=in_ref, dst_ref=out_ref.at[4 * px + 2 * py + pc], send_sem=send_sems.at[j], recv_sem=recv_sems.at[j],
                                         device_id=(px, py, pc), device_id_type=_MESH).wait()
        cp.wait()

    return pl.pallas_call(
        body, name=name,
        in_specs=[_ANY], out_specs=_ANY,
        out_shape=jax.ShapeDtypeStruct((8,) + block.shape, block.dtype),
        scratch_shapes=[pltpu.SemaphoreType.DMA((7,)), pltpu.SemaphoreType.DMA((7,)), pltpu.SemaphoreType.DMA],
    )(block)


def _row_tile(r, c):
    best = r
    for cand in range(16, r + 1, 16):
        if r % cand == 0 and cand * c * 4 <= (1 << 20):
            best = cand
    return best if best * c * 4 <= (4 << 20) else r


def _sum_slots(parts, name):
    n, r, c = parts.shape
    tr = _row_tile(r, c)

    def body(p_ref, o_ref):
        acc = p_ref[0].astype(f32)
        for s in range(1, n):
            acc = acc + p_ref[s].astype(f32)
        o_ref[...] = acc

    return pl.pallas_call(
        body, name=name, grid=(r // tr,),
        in_specs=[pl.BlockSpec((n, tr, c), lambda i: (0, i, 0))],
        out_specs=pl.BlockSpec((tr, c), lambda i: (i, 0)),
        out_shape=jax.ShapeDtypeStruct((r, c), f32),
        compiler_params=_cparams(1, arbitrary=False),
    )(parts)


def _adamw(parts, w, m, v, name):
    n, r, c = parts.shape
    tr = _row_tile(r, c)
    tc = c
    if tr == r and r * c * 4 > (1 << 20) and c % 256 == 0:
        tc = 256

    def body(p_ref, w_ref, m_ref, v_ref, g_ref, d_ref, nm_ref, nv_ref):
        g = p_ref[0]
        for s in range(1, n):
            g = g + p_ref[s]
        nm = B1 * m_ref[...] + (1.0 - B1) * g
        nv = B2 * v_ref[...] + (1.0 - B2) * (g * g)
        m_hat = nm / (1.0 - B1 ** STEP)
        v_hat = nv / (1.0 - B2 ** STEP)
        g_ref[...] = g
        nm_ref[...] = nm
        nv_ref[...] = nv
        d_ref[...] = -LR * (m_hat / (jnp.sqrt(v_hat) + EPS_ADAM) + WD * w_ref[...])

    blk = pl.BlockSpec((tr, tc), lambda i, j: (i, j))
    return pl.pallas_call(
        body, name=name, grid=(r // tr, c // tc),
        in_specs=[pl.BlockSpec((n, tr, tc), lambda i, j: (0, i, j)), blk, blk, blk],
        out_specs=[blk] * 4,
        out_shape=[jax.ShapeDtypeStruct((r, c), f32)] * 4,
        compiler_params=_cparams(2, arbitrary=False),
    )(parts, w, m, v)


_BIG = ["w_in", "w_out", "w_ffn_gate", "w_ffn_up", "w_ffn_down", "ple_w_proj", "ple_w_gate"]
_SMALL = ["b_in", "hg_lb_logits", "ml_conv_w", "ml_conv_b", "hg_norm_g", "ml_norm_g", "ln1_g", "ln1_b", "ln2_g", "ln2_b", "ple_b_gate"]
_ORDER = ["w_in", "b_in", "hg_lb_logits", "ml_conv_w", "ml_conv_b", "hg_norm_g", "ml_norm_g", "w_out", "ln1_g", "ln1_b",
          "w_ffn_gate", "w_ffn_up", "w_ffn_down", "ln2_g", "ln2_b", "ple_w_proj", "ple_w_gate", "ple_b_gate"]
_PACK_ROWS, _PACK_COLS = 16, 1024


def _pack(arrays):
    flat = jnp.concatenate([a.reshape(-1) for a in arrays])
    return jnp.pad(flat, (0, _PACK_ROWS * _PACK_COLS - flat.shape[0])).reshape(_PACK_ROWS, _PACK_COLS)


def _unpack(pack, shapes):
    flat = pack.reshape(-1)
    out, off = [], 0
    for s in shapes:
        size = 1
        for d in s:
            size *= d
        out.append(flat[off:off + size].reshape(s))
        off += size
    return out


def _to_chip_major(g, col_split):
    if col_split:
        k, n = g.shape
        return g.reshape(k, 4, n // 4).transpose(1, 0, 2)
    k, n = g.shape
    return g.reshape(4, k // 4, n)


def kernel(x, p, w_in, b_in, hg_lb_logits, ml_conv_w, ml_conv_b, hg_norm_g, ml_norm_g, w_out, ln1_g, ln1_b, w_ffn_gate, w_ffn_up, w_ffn_down, ln2_g, ln2_b, ple_w_proj, ple_w_gate, ple_b_gate, loss_target, m_w_in, m_b_in, m_hg_lb_logits, m_ml_conv_w, m_ml_conv_b, m_hg_norm_g, m_ml_norm_g, m_w_out, m_ln1_g, m_ln1_b, m_w_ffn_gate, m_w_ffn_up, m_w_ffn_down, m_ln2_g, m_ln2_b, m_ple_w_proj, m_ple_w_gate, m_ple_b_gate, v_w_in, v_b_in, v_hg_lb_logits, v_ml_conv_w, v_ml_conv_b, v_hg_norm_g, v_ml_norm_g, v_w_out, v_ln1_g, v_ln1_b, v_w_ffn_gate, v_w_ffn_up, v_w_ffn_down, v_ln2_g, v_ln2_b, v_ple_w_proj, v_ple_w_gate, v_ple_b_gate):
    args = dict(locals())
    wts = {k: args[k] for k in _ORDER}
    mom = {k: args["m_" + k] for k in _ORDER}
    var = {k: args["v_" + k] for k in _ORDER}
    two_d = lambda a: a.reshape(a.shape[-2], a.shape[-1])
    block = lambda k, a: jnp.swapaxes(two_d(a), 0, 1) if k in _TRANSPOSED else two_d(a)
    unblock = lambda k, a: (jnp.swapaxes(a, 0, 1) if k in _TRANSPOSED else a).reshape(wts[k].shape)

    shards = {k: block(k, wts[k]).astype(bf16) for k in _BIG}
    w_in_blocks, taps = _gather_first(shards["w_in"], two_d(ml_conv_w), "gather_w_in")
    w_in_full = _from_chip_major(w_in_blocks, False)
    conv_w_full = _from_chip_major(taps, True)

    early_keys = _BIG[1:]
    loss, grad_x, grads, (got_hg, got_ml) = _local_step(
        x[0], p[0, 0], loss_target[0], w_in_full, b_in, hg_lb_logits, conv_w_full, ml_conv_b, hg_norm_g, ml_norm_g,
        None, ln1_g, ln1_b, None, None, None, ln2_g, ln2_b, None, None, ple_b_gate,
        early_hook=lambda low: tuple([_to_chip_major(low[k], k in _COL_SPLIT) for k in names] for names in _SCATTER_PLAN),
        late_shards={k: shards[k] for k in early_keys})

    out_g, out_d, out_m, out_v = {}, {}, {}, {}

    def finish(k, g, d, nm, nv):
        out_g[k], out_d[k], out_m[k], out_v[k] = unblock(k, g), unblock(k, d), unblock(k, nm), unblock(k, nv)

    for names, got in zip(_SCATTER_PLAN, (got_hg, got_ml)):
        for k, rcv in zip(names, got):
            finish(k, *_reduce_adamw(rcv, block(k, wts[k]), block(k, mom[k]), block(k, var[k]), "reduce_adamw_" + k))

    core_sums = _pair_reduce_cols(_to_chip_major(grads["w_in"], False), "pair_reduce_w_in")
    whole = _chip_reduce_swap_cols(_scatter_chips([core_sums], "scatter_grad_w_in")[0], "chip_reduce_w_in")
    finish("w_in", *_adamw(whole[None], block("w_in", wts["w_in"]), block("w_in", mom["w_in"]), block("w_in", var["w_in"]), "adamw_w_in"))

    small_shapes = [(1, PROJ_W), (2, MIX_W), (CONV_K, MIX_W)] + [(1, MIX_W)] * 3 + [(1, D_MODEL)] * 5 + [(1, 1)]
    contrib = _pack([grads[k] for k in _SMALL] + [loss])
    summed = _sum_slots(_gather_all(contrib, "gather_small"), "sum_small")
    small = _unpack(summed, small_shapes)
    loss_total = small[-1].reshape(())
    gsm = dict(zip(_SMALL, small[:-1]))
    place = 2 * lax.axis_index("x") + lax.axis_index("y")
    conv_cols = ml_conv_w.shape[-1]
    gsm["ml_conv_w"] = lax.dynamic_slice(gsm["ml_conv_w"], (0, place * conv_cols), (CONV_K, conv_cols))
    own_shapes = [wts[k].shape for k in _SMALL]
    g_pack = _pack([gsm[k] for k in _SMALL])
    res = _adamw(g_pack[None], _pack([wts[k] for k in _SMALL]), _pack([mom[k] for k in _SMALL]), _pack([var[k] for k in _SMALL]), "adamw_small")
    for dst, pack in zip((out_g, out_d, out_m, out_v), res):
        for k, a in zip(_SMALL, _unpack(pack, own_shapes)):
            dst[k] = a

    outs = [loss_total, grad_x[None]]
    for group in (out_g, out_d, out_m, out_v):
        outs += [group[k] for k in _ORDER]
    return tuple(outs)
```

```python
import jax
import jax.numpy as jnp
from jax import lax
from jax.experimental import pallas as pl
from jax.experimental.pallas import tpu as pltpu

f32 = jnp.float32
bf16 = jnp.bfloat16

D_MODEL = 1024
HEADS = 4
HEAD_W = 128
MIX_W = HEADS * HEAD_W
ML_DQK = 64
PROJ_W = 3592
U_HG = 4 * MIX_W
U_ML = 3 * MIX_W + 128
D_FF = 2816
PLE = 256
CHUNK = 128
SUB = 16
EXP_CAP = 80.0
CONV_K = 4
HALO = 8
ALPHA = float(2.0 ** 0.25)
LN_EPS = 1e-5
RMS_EPS = 1e-6
NEG = -1e30
LR, B1, B2, EPS_ADAM, WD, STEP = 0.001, 0.9, 0.999, 1e-08, 0.01, 10
VMEM_LIMIT = 56 * 1024 * 1024
MIXER_ROWS = 512
DENSE_ROWS = 512
LIGHT_ROWS = 1024
WGRAD_ROWS = 2048


def _cparams(n_axes, arbitrary=True):
    sem = ("arbitrary",) * n_axes if arbitrary else ("parallel",) * n_axes
    return pltpu.CompilerParams(dimension_semantics=sem, vmem_limit_bytes=VMEM_LIMIT)


ACT = bf16


def _mx(a):
    return a.astype(ACT)


def _bdot(a, b):
    return jnp.dot(_mx(a), _mx(b), preferred_element_type=f32)


def _bdot_nt(a, b):
    return lax.dot_general(_mx(a), _mx(b), (((1,), (1,)), ((), ())), preferred_element_type=f32)


def _bdot_tn(a, b):
    return lax.dot_general(_mx(a), _mx(b), (((0,), (0,)), ((), ())), preferred_element_type=f32)


def _split3(x):
    hi = x.astype(bf16)
    r1 = x - hi.astype(f32)
    mid = r1.astype(bf16)
    lo = (r1 - mid.astype(f32)).astype(bf16)
    return hi, mid, lo


def _dot3(a, b, dims):
    a_hi = a.astype(bf16)
    a_lo = (a - a_hi.astype(f32)).astype(bf16)
    b_hi = b.astype(bf16)
    b_lo = (b - b_hi.astype(f32)).astype(bf16)
    dn = (dims, ((), ()))
    return (lax.dot_general(a_hi, b_hi, dn, preferred_element_type=f32) + lax.dot_general(a_hi, b_lo, dn, preferred_element_type=f32)
            + lax.dot_general(a_lo, b_hi, dn, preferred_element_type=f32))


def _lane_sum(x):
    hi = x.astype(bf16)
    lo = (x - hi.astype(f32)).astype(bf16)
    ones = jnp.ones((x.shape[1], 128), bf16)
    return jnp.dot(hi, ones, preferred_element_type=f32) + jnp.dot(lo, ones, preferred_element_type=f32)


def _lane_dot(x, row):
    return _dot3(x, jnp.broadcast_to(row, (128, row.shape[1])), ((1,), (1,)))


def _sel_dot(sel, x):
    sb = sel.astype(bf16)
    return sum(jnp.dot(sb, part, preferred_element_type=f32) for part in _split3(x))


def _sel_dot_nt(sel, x):
    sb = sel.astype(bf16)
    return sum(lax.dot_general(sb, part, (((1,), (1,)), ((), ())), preferred_element_type=f32) for part in _split3(x))


def _sigmoid(x):
    return 1.0 / (1.0 + jnp.exp(-x))


def _log_sigmoid(x):
    return jnp.minimum(x, 0.0) - jnp.log(1.0 + jnp.exp(-jnp.abs(x)))


def _tri(n, upper=False):
    r = lax.broadcasted_iota(jnp.int32, (n, n), 0)
    c = lax.broadcasted_iota(jnp.int32, (n, n), 1)
    return (c >= r) if upper else (c <= r)


def _rows(tm, n, col=0):
    return pl.BlockSpec((tm, n), lambda i, _c=col: (i, _c))


def _rows_rev(tm, n, nb, col=0):
    return pl.BlockSpec((tm, n), lambda i, _c=col, _nb=nb: (_nb - 1 - i, _c))


def _const(shape):
    return pl.BlockSpec(shape, lambda i, _n=len(shape): (0,) * _n)


def _resident(shape):
    return pl.BlockSpec(shape, lambda i, _n=len(shape): (0,) * _n, pipeline_mode=pl.Buffered(1))


def _tile(t, want):
    return want if t % want == 0 else t


def _inproj(x, w_hg, w_ml, b_hg, b_ml, riders=()):
    t = x.shape[0]
    tm = _tile(t, DENSE_ROWS)

    def body(x_ref, whg_ref, wml_ref, bhg_ref, bml_ref, uhg_ref, uml_ref, xb_ref):
        xb = _mx(x_ref[...])
        xb_ref[...] = xb
        uhg_ref[...] = _bdot_nt(xb, whg_ref[...]) + bhg_ref[...]
        uml_ref[...] = _bdot_nt(xb, wml_ref[...]) + bml_ref[...]

    return _riding_call(
        body, "inproj", t // tm,
        in_specs=[_rows(tm, D_MODEL), _resident((U_HG, D_MODEL)), _resident((U_ML, D_MODEL)), _const((1, U_HG)), _const((1, U_ML))],
        out_specs=[_rows(tm, U_HG), _rows(tm, U_ML), _rows(tm, D_MODEL)],
        out_shape=[jax.ShapeDtypeStruct((t, U_HG), f32), jax.ShapeDtypeStruct((t, U_ML), f32), jax.ShapeDtypeStruct((t, D_MODEL), ACT)],
        scratch_shapes=[], operands=(x, w_hg, w_ml, b_hg, b_ml), riders=riders, copies=_gather_copies, ride_shapes=_gather_shapes(riders))


def _hg_gates(hq, hf, lb, tri, b=None):
    s = _sigmoid(hf)
    om = 1.0 - lb
    f = lb + om * s
    k = om * (1.0 - s)
    sq = _sigmoid(hq)
    q = hq * sq
    if b is None:
        b = _sel_dot(tri, jnp.log(f))
    return q, sq, s, f, k, b


def _hg_scores(q, k, b, tril_mask, a=None):
    qts, kts, eqs, eks, rows = [], [], [], [], []
    for i in range(CHUNK // SUB):
        lo = i * SUB
        ref = jnp.zeros_like(b[0:1]) if i == 0 else b[lo - 1:lo]
        eq = jnp.exp(b[lo:lo + SUB] - ref)
        ek = jnp.exp(jnp.minimum(ref - b, EXP_CAP))
        qt = q[lo:lo + SUB] * eq
        kt = k * ek
        if a is None:
            rows.append(_bdot_nt(qt, kt))
        qts.append(qt); kts.append(kt); eqs.append(eq); eks.append(ek)
    if a is None:
        a = jnp.where(tril_mask, jnp.concatenate(rows, axis=0), 0.0)
    return a, qts, kts, eqs, eks


def _head_rms(o, gn, on_mxu=False):
    ms = _lane_sum(o * o) * (1.0 / o.shape[1]) if on_mxu else jnp.mean(o * o, axis=-1, keepdims=True)
    rstd = lax.rsqrt(ms + RMS_EPS)
    oh = o * rstd
    return oh, rstd, oh * gn


def _lower_bound(logit_ref):
    lg = logit_ref[...]
    return _sigmoid(lg[0:1] - lg[1:2])


def _hgrn2_fwd(u_hg, logits, gn, riders=()):
    t = u_hg.shape[0]
    tb = _tile(t, MIXER_ROWS)
    nc_blk = tb // CHUNK

    def body(u_ref, lg_ref, gn_ref, og_ref, sst_ref, b_ref, a_ref, o_ref, st_ref):
        @pl.when(pl.program_id(0) == 0)
        def _():
            st_ref[...] = jnp.zeros_like(st_ref)

        lb_all = _lower_bound(lg_ref)
        tril_mask = _tri(CHUNK)
        tri = tril_mask.astype(f32)

        def chunk(c, carry):
            r0 = pl.multiple_of(c * CHUNK, CHUNK)
            rows = pl.ds(r0, CHUNK)
            heads = range(HEADS)
            cols = [slice(h * HEAD_W, (h + 1) * HEAD_W) for h in heads]
            hv = [u_ref[rows, 2 * MIX_W + h * HEAD_W:2 * MIX_W + (h + 1) * HEAD_W] for h in heads]
            gts = [_hg_gates(u_ref[rows, h * HEAD_W:(h + 1) * HEAD_W], u_ref[rows, MIX_W + h * HEAD_W:MIX_W + (h + 1) * HEAD_W],
                             lb_all[:, cols[h]], tri) for h in heads]
            q = [g[0] for g in gts]
            k = [g[4] for g in gts]
            b = [g[5] for g in gts]
            a = [_hg_scores(q[h], k[h], b[h], tril_mask)[0] for h in heads]
            st = [st_ref[h] for h in heads]
            bl = [b[h][CHUNK - 1:CHUNK] for h in heads]
            o = [_bdot(a[h], hv[h]) + _bdot_nt(q[h] * jnp.exp(b[h]), st[h]) for h in heads]
            new_st = [st[h] * jnp.exp(bl[h]) + _bdot_tn(hv[h], k[h] * jnp.exp(bl[h] - b[h])) for h in heads]
            for h in heads:
                sst_ref[c, h] = st[h]
                st_ref[h] = new_st[h]
                b_ref[rows, cols[h]] = b[h]
                a_ref[rows, cols[h]] = a[h].astype(ACT)
                o_ref[rows, cols[h]] = o[h]
                hgate = u_ref[rows, 3 * MIX_W + h * HEAD_W:3 * MIX_W + (h + 1) * HEAD_W]
                _, _, y = _head_rms(o[h], gn_ref[:, cols[h]])
                og_ref[rows, cols[h]] = (y * (hgate * _sigmoid(hgate))).astype(ACT)
            return carry

        lax.fori_loop(0, nc_blk, chunk, 0, unroll=True)

    assert CHUNK == HEAD_W
    return _riding_call(
        body, "hgrn2_fwd", t // tb,
        in_specs=[_rows(tb, U_HG), _const((2, MIX_W)), _const((1, MIX_W))],
        out_specs=[_rows(tb, MIX_W), pl.BlockSpec((nc_blk, HEADS, HEAD_W, HEAD_W), lambda i: (i, 0, 0, 0)),
                   _rows(tb, MIX_W), _rows(tb, MIX_W), _rows(tb, MIX_W)],
        out_shape=[jax.ShapeDtypeStruct((t, MIX_W), ACT), jax.ShapeDtypeStruct((t // CHUNK, HEADS, HEAD_W, HEAD_W), f32),
                   jax.ShapeDtypeStruct((t, MIX_W), f32), jax.ShapeDtypeStruct((t, MIX_W), ACT), jax.ShapeDtypeStruct((t, MIX_W), f32)],
        scratch_shapes=[pltpu.VMEM((HEADS, HEAD_W, HEAD_W), f32)],
        operands=(u_hg, logits, gn), riders=riders, copies=_gather_copies, ride_shapes=_gather_shapes(riders))


def _hgrn2_bwd(u_hg, logits, gn, sst, bcum, scores, o_raw, dog, riders=()):
    t = u_hg.shape[0]
    tb = _tile(t, MIXER_ROWS)
    nb = t // tb
    nc_blk = tb // CHUNK

    def body(u_ref, lg_ref, gn_ref, sst_ref, b_ref, a_ref, o_ref, dog_ref, du_ref, dlg_ref, dgn_ref, dst_ref):
        @pl.when(pl.program_id(0) == 0)
        def _():
            dst_ref[...] = jnp.zeros_like(dst_ref)
            dlg_ref[...] = jnp.zeros_like(dlg_ref)
            dgn_ref[...] = jnp.zeros_like(dgn_ref)

        lb_all = _lower_bound(lg_ref)
        tril_mask = _tri(CHUNK)
        tri = tril_mask.astype(f32)
        triu = _tri(CHUNK, upper=True).astype(f32)

        def chunk(j, carry):
            c = nc_blk - 1 - j
            r0 = pl.multiple_of(c * CHUNK, CHUNK)
            rows = pl.ds(r0, CHUNK)
            heads = range(HEADS)
            nsub = CHUNK // SUB
            cols = [slice(h * HEAD_W, (h + 1) * HEAD_W) for h in heads]
            hq = [u_ref[rows, h * HEAD_W:(h + 1) * HEAD_W] for h in heads]
            hf = [u_ref[rows, MIX_W + h * HEAD_W:MIX_W + (h + 1) * HEAD_W] for h in heads]
            hv = [u_ref[rows, 2 * MIX_W + h * HEAD_W:2 * MIX_W + (h + 1) * HEAD_W] for h in heads]
            lb = [lb_all[:, cols[h]] for h in heads]
            gts = [_hg_gates(hq[h], hf[h], lb[h], tri, b=b_ref[rows, cols[h]]) for h in heads]
            q, sq, s, f, k, b = ([g[n] for g in gts] for n in range(6))
            scs = [_hg_scores(q[h], k[h], b[h], tril_mask, a=a_ref[rows, cols[h]]) for h in heads]
            a, qts, kts, eqs, eks = ([sc[n] for sc in scs] for n in range(5))
            st = [sst_ref[c, h] for h in heads]
            dst = [dst_ref[h] for h in heads]
            bl = [b[h][CHUNK - 1:CHUNK] for h in heads]
            eb = [jnp.exp(b[h]) for h in heads]
            qh = [q[h] * eb[h] for h in heads]
            ekl = [jnp.exp(bl[h] - b[h]) for h in heads]
            kh = [k[h] * ekl[h] for h in heads]
            o = [o_ref[rows, cols[h]] for h in heads]
            do = []
            for h in heads:
                hgate = u_ref[rows, 3 * MIX_W + h * HEAD_W:3 * MIX_W + (h + 1) * HEAD_W]
                gnh = gn_ref[:, cols[h]]
                oh, rstd, y = _head_rms(o[h], gnh)
                sg = _sigmoid(hgate)
                dogh = dog_ref[rows, cols[h]]
                dy = dogh * (hgate * sg)
                du_ref[rows, 3 * MIX_W + h * HEAD_W:3 * MIX_W + (h + 1) * HEAD_W] = (dogh * y * (sg * (1.0 + hgate * (1.0 - sg)))).astype(ACT)
                dgn_ref[:, cols[h]] += jnp.sum(dy * oh, axis=0, keepdims=True)
                doh = dy * gnh
                do.append(rstd * (doh - oh * jnp.mean(doh * oh, axis=-1, keepdims=True)))
            da = [jnp.where(tril_mask, _bdot_nt(do[h], hv[h]), 0.0) for h in heads]
            dv = [_bdot_tn(a[h], do[h]) + _bdot_nt(kh[h], dst[h]) for h in heads]
            dq = [_bdot(do[h], st[h]) * eb[h] for h in heads]
            dk = [_bdot(hv[h], dst[h]) * ekl[h] for h in heads]
            d_last = [jnp.sum(k[h] * dk[h], axis=0, keepdims=True) + jnp.exp(bl[h]) * jnp.sum(dst[h] * st[h], axis=0, keepdims=True)
                      for h in heads]
            d_b = [q[h] * dq[h] - k[h] * dk[h] for h in heads]
            dqs = [[] for _ in heads]
            q_dq = [[] for _ in heads]
            for i in range(nsub):
                for h in heads:
                    da_i = _mx(da[h][i * SUB:(i + 1) * SUB])
                    q_r, k_r = _mx(qts[h][i]), _mx(kts[h][i])
                    g_q = jnp.dot(da_i, k_r, preferred_element_type=f32)
                    g_k = lax.dot_general(da_i, q_r, (((0,), (0,)), ((), ())), preferred_element_type=f32)
                    dqs[h].append(g_q * eqs[h][i])
                    q_dq[h].append(q_r.astype(f32) * g_q)
                    dk[h] = dk[h] + g_k * eks[h][i]
                    d_b[h] = d_b[h] - k_r.astype(f32) * g_k
            for h in heads:
                dq[h] = dq[h] + jnp.concatenate(dqs[h], axis=0)
                d_b[h] = d_b[h] + jnp.concatenate(q_dq[h], axis=0)
                dst_ref[h] = dst[h] * jnp.exp(bl[h]) + _bdot_tn(do[h], qh[h])
            dg = [_sel_dot(triu, d_b[h]) + d_last[h] for h in heads]
            for h in heads:
                dfk = dg[h] / f[h] - dk[h]
                du_ref[rows, h * HEAD_W:(h + 1) * HEAD_W] = (dq[h] * (sq[h] * (1.0 + hq[h] * (1.0 - sq[h])))).astype(ACT)
                du_ref[rows, MIX_W + h * HEAD_W:MIX_W + (h + 1) * HEAD_W] = ((1.0 - lb[h]) * dfk * s[h] * (1.0 - s[h])).astype(ACT)
                du_ref[rows, 2 * MIX_W + h * HEAD_W:2 * MIX_W + (h + 1) * HEAD_W] = dv[h].astype(ACT)
                dlb = jnp.sum((1.0 - s[h]) * dfk, axis=0, keepdims=True) * (lb[h] * (1.0 - lb[h]))
                dlg_ref[0:1, cols[h]] += dlb
                dlg_ref[1:2, cols[h]] -= dlb
            return carry

        lax.fori_loop(0, nc_blk, chunk, 0, unroll=True)

    rev = _rows_rev(tb, MIX_W, nb)
    return _riding_call(
        body, "hgrn2_bwd", nb,
        in_specs=[_rows_rev(tb, U_HG, nb), _const((2, MIX_W)), _const((1, MIX_W)),
                  pl.BlockSpec((nc_blk, HEADS, HEAD_W, HEAD_W), lambda i: (nb - 1 - i, 0, 0, 0)), rev, rev, rev, rev],
        out_specs=[_rows_rev(tb, U_HG, nb), _const((2, MIX_W)), _const((1, MIX_W))],
        out_shape=[jax.ShapeDtypeStruct((t, U_HG), ACT), jax.ShapeDtypeStruct((2, MIX_W), f32), jax.ShapeDtypeStruct((1, MIX_W), f32)],
        scratch_shapes=[pltpu.VMEM((HEADS, HEAD_W, HEAD_W), f32)],
        operands=(u_hg, logits, gn, sst, bcum, scores, o_raw, dog), riders=riders, copies=_scatter_copies,
        ride_shapes=[jax.ShapeDtypeStruct(r.shape, r.dtype) for r in riders])


def _conv_fwd(u_ml, w, b):
    t = u_ml.shape[0]
    tm = _tile(t, LIGHT_ROWS)

    def body(x_ref, w_ref, b_ref, pre_ref, act_ref, xbuf):
        @pl.when(pl.program_id(0) == 0)
        def _():
            xbuf[...] = jnp.zeros_like(xbuf)

        xbuf[0:HALO, :] = xbuf[tm:tm + HALO, :]
        xbuf[HALO:HALO + tm, :] = x_ref[...]
        pre = b_ref[...] + jnp.zeros((tm, MIX_W), f32)
        for kk in range(CONV_K):
            off = HALO - (CONV_K - 1) + kk
            pre = pre + w_ref[kk:kk + 1, :] * xbuf[off:off + tm, :]
        pre_ref[...] = pre
        act_ref[...] = pre * _sigmoid(pre)

    return pl.pallas_call(
        body, name="conv_fwd", grid=(t // tm,),
        in_specs=[_rows(tm, MIX_W), _const((CONV_K, MIX_W)), _const((1, MIX_W))],
        out_specs=[_rows(tm, MIX_W), _rows(tm, MIX_W)],
        out_shape=[jax.ShapeDtypeStruct((t, MIX_W), f32)] * 2,
        scratch_shapes=[pltpu.VMEM((tm + HALO, MIX_W), f32)],
        compiler_params=_cparams(1),
    )(u_ml, w, b)


def _conv_bwd(u_ml, w, pre, dact):
    t = u_ml.shape[0]
    tm = _tile(t, LIGHT_ROWS)
    nb = t // tm
    hb = tm // HALO

    def body(x_ref, halo_ref, w_ref, pre_ref, dact_ref, dx_ref, dw_ref, db_ref, dbuf, xbuf):
        i = pl.program_id(0)

        @pl.when(i == 0)
        def _():
            dbuf[...] = jnp.zeros_like(dbuf)
            dw_ref[...] = jnp.zeros_like(dw_ref)
            db_ref[...] = jnp.zeros_like(db_ref)

        p = pre_ref[...]
        sg = _sigmoid(p)
        dpre = dact_ref[...] * (sg * (1.0 + p * (1.0 - sg)))
        dbuf[tm:tm + HALO, :] = dbuf[0:HALO, :]
        dbuf[0:tm, :] = dpre
        has_prev = (i < nb - 1).astype(f32)
        xbuf[0:HALO, :] = halo_ref[...] * has_prev
        xbuf[HALO:HALO + tm, :] = x_ref[...]
        dx = jnp.zeros((tm, MIX_W), f32)
        for kk in range(CONV_K):
            back = CONV_K - 1 - kk
            dx = dx + w_ref[kk:kk + 1, :] * dbuf[back:back + tm, :]
            off = HALO - (CONV_K - 1) + kk
            dw_ref[kk:kk + 1, :] += jnp.sum(dpre * xbuf[off:off + tm, :], axis=0, keepdims=True)
        dx_ref[...] = dx.astype(ACT)
        db_ref[...] += jnp.sum(dpre, axis=0, keepdims=True)

    return pl.pallas_call(
        body, name="conv_bwd", grid=(nb,),
        in_specs=[_rows_rev(tm, MIX_W, nb),
                  pl.BlockSpec((HALO, MIX_W), lambda i: (jnp.maximum((nb - 1 - i) * hb - 1, 0), 0)),
                  _const((CONV_K, MIX_W)), _rows_rev(tm, MIX_W, nb), _rows_rev(tm, MIX_W, nb)],
        out_specs=[_rows_rev(tm, MIX_W, nb), _const((CONV_K, MIX_W)), _const((1, MIX_W))],
        out_shape=[jax.ShapeDtypeStruct((t, MIX_W), ACT), jax.ShapeDtypeStruct((CONV_K, MIX_W), f32), jax.ShapeDtypeStruct((1, MIX_W), f32)],
        scratch_shapes=[pltpu.VMEM((tm + HALO, MIX_W), f32), pltpu.VMEM((tm + HALO, MIX_W), f32)],
        compiler_params=_cparams(1),
    )(u_ml, u_ml, w, pre, dact)


def _lane_pick(x, lane):
    idx = lax.broadcasted_iota(jnp.int32, x.shape, 1)
    return jnp.sum(jnp.where(idx == lane, x, 0.0), axis=-1, keepdims=True)


def _ml_gate_forms(gates, tri):
    lf = _log_sigmoid(gates)
    gc = _sel_dot(tri, lf)
    lane = lax.broadcasted_iota(jnp.int32, gates.shape, 1)
    mixed = jnp.where(lane < HEADS, gates, gc)
    sel = (lax.broadcasted_iota(jnp.int32, (8, 128), 0) == lax.broadcasted_iota(jnp.int32, (8, 128), 1)).astype(f32)
    rowsf = _sel_dot_nt(sel, mixed)
    return gc, rowsf


def _ml_chunk(q, k, v, gates, gc, rowsf, c_st, n_st, m_st, tril_mask):
    hs = range(HEADS)
    g_col = [_lane_pick(gc, HEADS + h) for h in hs]
    ig_col = [_lane_pick(gates, h) for h in hs]
    dmat = [jnp.where(tril_mask, g_col[h] - rowsf[HEADS + h:HEADS + h + 1, :] + rowsf[h:h + 1, :], NEG) for h in hs]
    m_inter = [g_col[h] + m_st[h] for h in hs]
    m_t = [jnp.maximum(m_inter[h], jnp.max(dmat[h], axis=-1, keepdims=True)) for h in hs]
    wi = [jnp.exp(dmat[h] - m_t[h]) for h in hs]
    wo = [jnp.exp(m_inter[h] - m_t[h]) for h in hs]
    qk = [_bdot_nt(q[h], k[h]) * wi[h] for h in hs]
    num = [_bdot(qk[h], v[h]) + wo[h] * _bdot(q[h], c_st[h]) for h in hs]
    den = [_lane_sum(qk[h]) + wo[h] * _lane_dot(q[h], n_st[h]) for h in hs]
    floor = [jnp.exp(-m_t[h]) for h in hs]
    z = [jnp.maximum(jnp.abs(den[h]), floor[h]) for h in hs]
    g_last = [g_col[h][CHUNK - 1:CHUNK] for h in hs]
    a_col = [g_last[h] - g_col[h] + ig_col[h] for h in hs]
    m_new = [jnp.maximum(g_last[h] + m_st[h], jnp.max(a_col[h], axis=0, keepdims=True)) for h in hs]
    ws = [jnp.exp(a_col[h] - m_new[h]) for h in hs]
    w_old = [jnp.exp(g_last[h] + m_st[h] - m_new[h]) for h in hs]
    return dict(wi=wi, wo=wo, qk=qk, num=num, den=den, z=z, floor=floor, ws=ws, w_old=w_old, m_new=m_new)


def _mlstm_fwd(qkc, u_ml, gn, riders=()):
    t = qkc.shape[0]
    tb = _tile(t, MIXER_ROWS)
    nc_blk = tb // CHUNK

    def body(qk_ref, v_ref, mo_ref, gt_ref, gn_ref, og_ref, cst_ref, nst_ref, mst_ref, c_sc, n_sc, m_sc):
        @pl.when(pl.program_id(0) == 0)
        def _():
            c_sc[...] = jnp.zeros_like(c_sc)
            n_sc[...] = jnp.zeros_like(n_sc)
            m_sc[...] = jnp.zeros_like(m_sc)

        tril_mask = _tri(CHUNK)
        tri = tril_mask.astype(f32)

        def chunk(c, carry):
            r0 = pl.multiple_of(c * CHUNK, CHUNK)
            rows = pl.ds(r0, CHUNK)
            gates = gt_ref[rows, :]
            gc, rowsf = _ml_gate_forms(gates, tri)
            hs = range(HEADS)
            q = [qk_ref[rows, h * ML_DQK:(h + 1) * ML_DQK] * (ML_DQK ** -0.5) for h in hs]
            k = [qk_ref[rows, HEADS * ML_DQK + h * ML_DQK:HEADS * ML_DQK + (h + 1) * ML_DQK] for h in hs]
            v = [v_ref[rows, h * HEAD_W:(h + 1) * HEAD_W] for h in hs]
            c_st = [c_sc[h] for h in hs]
            n_st = [n_sc[h] for h in hs]
            m_full = [m_sc[h] for h in hs]
            r = _ml_chunk(q, k, v, gates, gc, rowsf, c_st, n_st, [m[:, 0:1] for m in m_full], tril_mask)
            ksc = [k[h] * r["ws"][h] for h in hs]
            new_c = [r["w_old"][h] * c_st[h] + _bdot_tn(ksc[h], v[h]) for h in hs]
            for h in hs:
                cs = slice(h * HEAD_W, (h + 1) * HEAD_W)
                cst_ref[c, h] = c_st[h]
                nst_ref[c, h] = n_st[h]
                mst_ref[c, h] = m_full[h]
                c_sc[h] = new_c[h]
                n_sc[h] = r["w_old"][h] * n_st[h] + jnp.sum(ksc[h], axis=0, keepdims=True)
                m_sc[h] = r["m_new"][h] + jnp.zeros((1, 128), f32)
                _, _, y = _head_rms(r["num"][h] / r["z"][h], gn_ref[:, cs], on_mxu=True)
                og_ref[rows, cs] = (y * _sigmoid(mo_ref[rows, h * HEAD_W:(h + 1) * HEAD_W])).astype(ACT)
            return carry

        lax.fori_loop(0, nc_blk, chunk, 0, unroll=True)

    nchunks = t // CHUNK
    return _riding_call(
        body, "mlstm_fwd", t // tb,
        in_specs=[_rows(tb, MIX_W), _rows(tb, MIX_W, 1), _rows(tb, MIX_W, 2), _rows(tb, 128, 12), _const((1, MIX_W))],
        out_specs=[_rows(tb, MIX_W),
                   pl.BlockSpec((nc_blk, HEADS, ML_DQK, HEAD_W), lambda i: (i, 0, 0, 0)),
                   pl.BlockSpec((nc_blk, HEADS, 1, ML_DQK), lambda i: (i, 0, 0, 0)),
                   pl.BlockSpec((nc_blk, HEADS, 1, 128), lambda i: (i, 0, 0, 0))],
        out_shape=[jax.ShapeDtypeStruct((t, MIX_W), ACT),
                   jax.ShapeDtypeStruct((nchunks, HEADS, ML_DQK, HEAD_W), f32),
                   jax.ShapeDtypeStruct((nchunks, HEADS, 1, ML_DQK), f32),
                   jax.ShapeDtypeStruct((nchunks, HEADS, 1, 128), f32)],
        scratch_shapes=[pltpu.VMEM((HEADS, ML_DQK, HEAD_W), f32), pltpu.VMEM((HEADS, 1, ML_DQK), f32), pltpu.VMEM((HEADS, 1, 128), f32)],
        operands=(qkc, u_ml, u_ml, u_ml, gn), riders=riders, copies=_gather_copies, ride_shapes=_gather_shapes(riders))


def _mlstm_bwd(qkc, u_ml, gn, cst, nst, mst, dog, riders=()):
    t = qkc.shape[0]
    tb = _tile(t, MIXER_ROWS)
    nb = t // tb
    nc_blk = tb // CHUNK

    def body(qk_ref, v_ref, mo_ref, gt_ref, gn_ref, cst_ref, nst_ref, mst_ref, dog_ref,
             dqk_ref, dv_ref, dmo_ref, dgt_ref, dgn_ref, dc_sc, dn_sc):
        @pl.when(pl.program_id(0) == 0)
        def _():
            dc_sc[...] = jnp.zeros_like(dc_sc)
            dn_sc[...] = jnp.zeros_like(dn_sc)
            dgn_ref[...] = jnp.zeros_like(dgn_ref)

        tril_mask = _tri(CHUNK)
        tri = tril_mask.astype(f32)
        triu = _tri(CHUNK, upper=True).astype(f32)
        lane = lax.broadcasted_iota(jnp.int32, (CHUNK, 128), 1)

        def chunk(j, carry):
            c = nc_blk - 1 - j
            r0 = pl.multiple_of(c * CHUNK, CHUNK)
            rows = pl.ds(r0, CHUNK)
            gates = gt_ref[rows, :]
            gc, rowsf = _ml_gate_forms(gates, tri)
            dg_mat = jnp.zeros((CHUNK, 128), f32)
            dig_mat = jnp.zeros((CHUNK, 128), f32)
            dlast_row = jnp.zeros((1, 128), f32)
            hs = range(HEADS)
            cols = [slice(h * HEAD_W, (h + 1) * HEAD_W) for h in hs]
            q = [qk_ref[rows, h * ML_DQK:(h + 1) * ML_DQK] * (ML_DQK ** -0.5) for h in hs]
            k = [qk_ref[rows, HEADS * ML_DQK + h * ML_DQK:HEADS * ML_DQK + (h + 1) * ML_DQK] for h in hs]
            v = [v_ref[rows, h * HEAD_W:(h + 1) * HEAD_W] for h in hs]
            c_st = [cst_ref[c, h] for h in hs]
            n_st = [nst_ref[c, h] for h in hs]
            m_st = [mst_ref[c, h][:, 0:1] for h in hs]
            dc = [dc_sc[h] for h in hs]
            dn = [dn_sc[h] for h in hs]
            r = _ml_chunk(q, k, v, gates, gc, rowsf, c_st, n_st, m_st, tril_mask)
            z, wi, wo, ws, w_old, den = r["z"], r["wi"], r["wo"], r["ws"], r["w_old"], r["den"]
            hh = [r["num"][h] / z[h] for h in hs]
            dh = []
            for h in hs:
                gnh = gn_ref[:, cols[h]]
                oh, rstd, y = _head_rms(hh[h], gnh, on_mxu=True)
                sg = _sigmoid(mo_ref[rows, h * HEAD_W:(h + 1) * HEAD_W])
                dogh = dog_ref[rows, cols[h]]
                dy = dogh * sg
                dmo_ref[rows, cols[h]] = (dogh * y * (sg * (1.0 - sg))).astype(ACT)
                dgn_ref[:, cols[h]] += jnp.sum(dy * oh, axis=0, keepdims=True)
                doh = dy * gnh
                dh.append(rstd * (doh - oh * (_lane_sum(doh * oh) * (1.0 / HEAD_W))))
            dnum = [dh[h] / z[h] for h in hs]
            dz = [-_lane_sum(dh[h] * hh[h]) / z[h] for h in hs]
            dden = [jnp.where(jnp.abs(den[h]) > r["floor"][h], dz[h] * jnp.sign(den[h]), 0.0) for h in hs]
            dsw = [(_bdot_nt(dnum[h], v[h]) + dden[h]) * wi[h] for h in hs]
            dq = [_bdot(dsw[h], k[h]) + wo[h] * (_bdot_nt(dnum[h], c_st[h]) + dden[h][:, :ML_DQK] * n_st[h]) for h in hs]
            dk_state = [ws[h] * (_bdot_nt(v[h], dc[h]) + dn[h]) for h in hs]
            dk = [_bdot_tn(dsw[h], q[h]) + dk_state[h] for h in hs]
            dv = [_bdot_tn(r["qk"][h], dnum[h]) + ws[h] * _bdot(k[h], dc[h]) for h in hs]
            woq = [wo[h] * q[h] for h in hs]
            new_dc = [w_old[h] * dc[h] + _bdot_tn(woq[h], dnum[h]) for h in hs]
            for h in hs:
                dv_ref[rows, cols[h]] = dv[h].astype(ACT)
                dc_sc[h] = new_dc[h]
                dn_sc[h] = w_old[h] * dn[h] + jnp.sum(woq[h] * dden[h][:, :ML_DQK], axis=0, keepdims=True)
                d_last = (jnp.sum(jnp.sum(k[h] * dk_state[h], axis=0, keepdims=True), axis=-1, keepdims=True)
                          + w_old[h] * (jnp.sum(jnp.sum(dc[h] * c_st[h], axis=0, keepdims=True), axis=-1, keepdims=True)
                                        + jnp.sum(dn[h] * n_st[h], axis=-1, keepdims=True)))
                kdk = _lane_sum(k[h] * dk[h])
                qdq = _lane_sum(q[h] * dq[h])
                dg_mat = dg_mat + jnp.where(lane == HEADS + h, qdq - kdk, 0.0)
                dlast_row = dlast_row + jnp.where(lane[0:1] == HEADS + h, d_last, 0.0)
                dig_mat = dig_mat + jnp.where(lane == h, kdk, 0.0)
                dqk_ref[rows, h * ML_DQK:(h + 1) * ML_DQK] = dq[h] * (ML_DQK ** -0.5)
                dqk_ref[rows, HEADS * ML_DQK + h * ML_DQK:HEADS * ML_DQK + (h + 1) * ML_DQK] = dk[h]
            dlf = _sel_dot(triu, dg_mat) + dlast_row
            dgt_ref[rows, :] = (dig_mat + dlf * _sigmoid(-gates)).astype(ACT)
            return carry

        lax.fori_loop(0, nc_blk, chunk, 0, unroll=True)

    st4 = lambda a, b: pl.BlockSpec((nc_blk, HEADS, a, b), lambda i: (nb - 1 - i, 0, 0, 0))
    return _riding_call(
        body, "mlstm_bwd", nb,
        in_specs=[_rows_rev(tb, MIX_W, nb), _rows_rev(tb, MIX_W, nb, 1), _rows_rev(tb, MIX_W, nb, 2), _rows_rev(tb, 128, nb, 12),
                  _const((1, MIX_W)), st4(ML_DQK, HEAD_W), st4(1, ML_DQK), st4(1, 128), _rows_rev(tb, MIX_W, nb)],
        out_specs=[_rows_rev(tb, MIX_W, nb), _rows_rev(tb, MIX_W, nb), _rows_rev(tb, MIX_W, nb), _rows_rev(tb, 128, nb), _const((1, MIX_W))],
        out_shape=[jax.ShapeDtypeStruct((t, MIX_W), f32), jax.ShapeDtypeStruct((t, MIX_W), ACT), jax.ShapeDtypeStruct((t, MIX_W), ACT),
                   jax.ShapeDtypeStruct((t, 128), ACT), jax.ShapeDtypeStruct((1, MIX_W), f32)],
        scratch_shapes=[pltpu.VMEM((HEADS, ML_DQK, HEAD_W), f32), pltpu.VMEM((HEADS, 1, ML_DQK), f32)],
        operands=(qkc, u_ml, u_ml, u_ml, gn, cst, nst, mst, dog), riders=riders, copies=_scatter_copies,
        ride_shapes=[jax.ShapeDtypeStruct(r.shape, r.dtype) for r in riders])


def _ln_fwd(r, g, b):
    mu = jnp.mean(r, axis=-1, keepdims=True)
    xc = r - mu
    rstd = lax.rsqrt(jnp.mean(xc * xc, axis=-1, keepdims=True) + LN_EPS)
    xh = xc * rstd
    return xh * g + b, xh, rstd


def _ln_bwd(dy, xh, rstd, g):
    dxh = dy * g
    return rstd * (dxh - jnp.mean(dxh, axis=-1, keepdims=True) - xh * jnp.mean(dxh * xh, axis=-1, keepdims=True))


def _outproj_ln1(og_hg, og_ml, x, w_out, g, b, riders=()):
    t = x.shape[0]
    tm = _tile(t, LIGHT_ROWS)

    def body(a_ref, b_ref, x_ref, w_ref, g_ref, bb_ref, x1_ref, xh_ref, rs_ref, x1b_ref):
        mix = _bdot(a_ref[...], w_ref[0:MIX_W, :]) + _bdot(b_ref[...], w_ref[MIX_W:2 * MIX_W, :])
        y, xh, rstd = _ln_fwd(ALPHA * x_ref[...] + mix, g_ref[...], bb_ref[...])
        x1_ref[...] = y
        x1b_ref[...] = y.astype(ACT)
        xh_ref[...] = xh.astype(ACT)
        rs_ref[...] = rstd

    return _riding_call(
        body, "outproj_ln1", t // tm,
        in_specs=[_rows(tm, MIX_W), _rows(tm, MIX_W), _rows(tm, D_MODEL), _resident((D_MODEL, D_MODEL)), _const((1, D_MODEL)), _const((1, D_MODEL))],
        out_specs=[_rows(tm, D_MODEL), _rows(tm, D_MODEL), _rows(tm, 1), _rows(tm, D_MODEL)],
        out_shape=[jax.ShapeDtypeStruct((t, D_MODEL), f32), jax.ShapeDtypeStruct((t, D_MODEL), ACT), jax.ShapeDtypeStruct((t, 1), f32),
                   jax.ShapeDtypeStruct((t, D_MODEL), ACT)],
        scratch_shapes=[], operands=(og_hg, og_ml, x, w_out, g, b), riders=riders, copies=_gather_copies, ride_shapes=_gather_shapes(riders))


def _ffn_up(x1, wg, wu, riders=()):
    t = x1.shape[0]
    tm = _tile(t, DENSE_ROWS)

    def body(x_ref, wg_ref, wu_ref, hg_ref, up_ref, a_ref):
        xv = x_ref[...]
        hg = _bdot_nt(xv, wg_ref[...])
        up = _bdot_nt(xv, wu_ref[...])
        hg_ref[...] = hg.astype(ACT)
        up_ref[...] = up.astype(ACT)
        a_ref[...] = (hg * _sigmoid(hg) * up).astype(ACT)

    return _riding_call(
        body, "ffn_up", t // tm,
        in_specs=[_rows(tm, D_MODEL), _resident((D_FF, D_MODEL)), _resident((D_FF, D_MODEL))],
        out_specs=[_rows(tm, D_FF), _rows(tm, D_FF), _rows(tm, D_FF)],
        out_shape=[jax.ShapeDtypeStruct((t, D_FF), ACT), jax.ShapeDtypeStruct((t, D_FF), ACT), jax.ShapeDtypeStruct((t, D_FF), ACT)],
        scratch_shapes=[], operands=(x1, wg, wu), riders=riders, copies=_gather_copies, ride_shapes=_gather_shapes(riders))


def _ffn_down_ln2(a, x1, wd, g, b):
    t = x1.shape[0]
    tm = _tile(t, LIGHT_ROWS)

    def body(a_ref, x_ref, w_ref, g_ref, bb_ref, x2_ref, xh_ref, rs_ref, x2b_ref):
        ffn = _bdot(a_ref[...], w_ref[...])
        y, xh, rstd = _ln_fwd(ALPHA * x_ref[...] + ffn, g_ref[...], bb_ref[...])
        x2_ref[...] = y
        x2b_ref[...] = y.astype(ACT)
        xh_ref[...] = xh.astype(ACT)
        rs_ref[...] = rstd

    return pl.pallas_call(
        body, name="ffn_down_ln2", grid=(t // tm,),
        in_specs=[_rows(tm, D_FF), _rows(tm, D_MODEL), _resident((D_FF, D_MODEL)), _const((1, D_MODEL)), _const((1, D_MODEL))],
        out_specs=[_rows(tm, D_MODEL), _rows(tm, D_MODEL), _rows(tm, 1), _rows(tm, D_MODEL)],
        out_shape=[jax.ShapeDtypeStruct((t, D_MODEL), f32), jax.ShapeDtypeStruct((t, D_MODEL), ACT), jax.ShapeDtypeStruct((t, 1), f32),
                   jax.ShapeDtypeStruct((t, D_MODEL), ACT)],
        compiler_params=_cparams(1, arbitrary=False),
    )(a, x1, wd, g, b)


def _head_loss_bwd(x2, xh2, rs2, p, tgt, w_pg, b_pg, w_pp, g2):
    t = x2.shape[0]
    tm = _tile(t, LIGHT_ROWS)

    def body(x_ref, xh_ref, rs_ref, p_ref, t_ref, wg_ref, bg_ref, wp_ref, g_ref,
             dr_ref, de_ref, dz_ref, loss_ref, dbg_ref, dg2_ref, db2_ref):
        @pl.when(pl.program_id(0) == 0)
        def _():
            loss_ref[...] = jnp.zeros_like(loss_ref)
            dbg_ref[...] = jnp.zeros_like(dbg_ref)
            dg2_ref[...] = jnp.zeros_like(dg2_ref)
            db2_ref[...] = jnp.zeros_like(db2_ref)

        x2v = x_ref[...]
        z = _bdot(x2v, wg_ref[...]) + bg_ref[...]
        e = _bdot(p_ref[...], wp_ref[...])
        sg = _sigmoid(z)
        diff = x2v + sg * e - t_ref[...]
        loss_ref[...] += 0.5 * jnp.sum(jnp.mean(diff * diff, axis=-1, keepdims=True), axis=0, keepdims=True)
        dy = diff * (1.0 / D_MODEL)
        de_ref[...] = (dy * sg).astype(ACT)
        dz = dy * e * (sg * (1.0 - sg))
        dz_ref[...] = dz.astype(ACT)
        dbg_ref[...] += jnp.sum(dz, axis=0, keepdims=True)
        dx2 = dy + _bdot_nt(dz, wg_ref[...])
        xh = xh_ref[...].astype(f32)
        dg2_ref[...] += jnp.sum(dx2 * xh, axis=0, keepdims=True)
        db2_ref[...] += jnp.sum(dx2, axis=0, keepdims=True)
        dr_ref[...] = _ln_bwd(dx2, xh, rs_ref[...], g_ref[...])

    row = jax.ShapeDtypeStruct((1, D_MODEL), f32)
    return pl.pallas_call(
        body, name="head_loss_bwd", grid=(t // tm,),
        in_specs=[_rows(tm, D_MODEL), _rows(tm, D_MODEL), _rows(tm, 1), _rows(tm, PLE), _rows(tm, D_MODEL),
                  _resident((D_MODEL, D_MODEL)), _const((1, D_MODEL)), _resident((PLE, D_MODEL)), _const((1, D_MODEL))],
        out_specs=[_rows(tm, D_MODEL), _rows(tm, D_MODEL), _rows(tm, D_MODEL), _const((1, 1)), _const((1, D_MODEL)), _const((1, D_MODEL)), _const((1, D_MODEL))],
        out_shape=[jax.ShapeDtypeStruct((t, D_MODEL), f32), jax.ShapeDtypeStruct((t, D_MODEL), ACT), jax.ShapeDtypeStruct((t, D_MODEL), ACT),
                   jax.ShapeDtypeStruct((1, 1), f32), row, row, row],
        compiler_params=_cparams(1),
    )(x2, xh2, rs2, p, tgt, w_pg, b_pg, w_pp, g2)


def _ffn_bwd(dr2, hg, up, xh1, rs1, wd, wg, wu, g1, w_out):
    t = dr2.shape[0]
    tm = _tile(t, DENSE_ROWS // 2)

    def body(dr_ref, hg_ref, up_ref, xh_ref, rs_ref, wd_ref, wg_ref, wu_ref, g_ref, wo_ref,
             dr1_ref, dhg_ref, dup_ref, dg1_ref, db1_ref, doghg_ref, dogml_ref):
        @pl.when(pl.program_id(0) == 0)
        def _():
            dg1_ref[...] = jnp.zeros_like(dg1_ref)
            db1_ref[...] = jnp.zeros_like(db1_ref)

        dr2v = dr_ref[...]
        da = _bdot_nt(dr2v, wd_ref[...])
        hgv = hg_ref[...].astype(f32)
        sg = _sigmoid(hgv)
        dhg = da * up_ref[...].astype(f32) * (sg * (1.0 + hgv * (1.0 - sg)))
        dup = da * (hgv * sg)
        dhg_ref[...] = dhg.astype(ACT)
        dup_ref[...] = dup.astype(ACT)
        dx1 = ALPHA * dr2v + _bdot(dhg, wg_ref[...]) + _bdot(dup, wu_ref[...])
        xh = xh_ref[...].astype(f32)
        dg1_ref[...] += jnp.sum(dx1 * xh, axis=0, keepdims=True)
        db1_ref[...] += jnp.sum(dx1, axis=0, keepdims=True)
        dr1 = _ln_bwd(dx1, xh, rs_ref[...], g_ref[...])
        dr1_ref[...] = dr1
        dog = _bdot_nt(dr1, wo_ref[...])
        doghg_ref[...] = dog[:, 0:MIX_W]
        dogml_ref[...] = dog[:, MIX_W:2 * MIX_W]

    row = jax.ShapeDtypeStruct((1, D_MODEL), f32)
    return pl.pallas_call(
        body, name="ffn_bwd", grid=(t // tm,),
        in_specs=[_rows(tm, D_MODEL), _rows(tm, D_FF), _rows(tm, D_FF), _rows(tm, D_MODEL), _rows(tm, 1),
                  _resident((D_FF, D_MODEL)), _resident((D_FF, D_MODEL)), _resident((D_FF, D_MODEL)), _const((1, D_MODEL)),
                  _resident((D_MODEL, D_MODEL))],
        out_specs=[_rows(tm, D_MODEL), _rows(tm, D_FF), _rows(tm, D_FF), _const((1, D_MODEL)), _const((1, D_MODEL)),
                   _rows(tm, MIX_W), _rows(tm, MIX_W)],
        out_shape=[jax.ShapeDtypeStruct((t, D_MODEL), f32), jax.ShapeDtypeStruct((t, D_FF), ACT), jax.ShapeDtypeStruct((t, D_FF), ACT), row, row,
                   jax.ShapeDtypeStruct((t, MIX_W), f32), jax.ShapeDtypeStruct((t, MIX_W), f32)],
        compiler_params=_cparams(1),
    )(dr2, hg, up, xh1, rs1, wd, wg, wu, g1, w_out)


def _inproj_bwd(dr1, du_hg, dqk, dmv, dmo, dgt, w_hg, w_ml):
    t = dr1.shape[0]
    tm = _tile(t, DENSE_ROWS)

    def body(dr_ref, dhg_ref, dqk_ref, dmv_ref, dmo_ref, dgt_ref, whg_ref, wml_ref, gx_ref, dml_ref):
        dml = jnp.concatenate([dqk_ref[...], dmv_ref[...], dmo_ref[...], dgt_ref[...]], axis=-1).astype(ACT)
        dml_ref[...] = dml
        gx_ref[...] = ALPHA * dr_ref[...] + _bdot(dhg_ref[...], whg_ref[...]) + _bdot(dml, wml_ref[...])

    return pl.pallas_call(
        body, name="inproj_bwd", grid=(t // tm,),
        in_specs=[_rows(tm, D_MODEL), _rows(tm, U_HG), _rows(tm, MIX_W), _rows(tm, MIX_W), _rows(tm, MIX_W), _rows(tm, 128),
                  _resident((U_HG, D_MODEL)), _resident((U_ML, D_MODEL))],
        out_specs=[_rows(tm, D_MODEL), _rows(tm, U_ML)],
        out_shape=[jax.ShapeDtypeStruct((t, D_MODEL), f32), jax.ShapeDtypeStruct((t, U_ML), ACT)],
        compiler_params=_cparams(1, arbitrary=False),
    )(dr1, du_hg, dqk, dmv, dmo, dgt, w_hg, w_ml)


def _wgrad(a, b, name, tk=None, tn=None, colsum=False, low=False):
    t, kdim = a.shape
    n = b.shape[1]
    tk = tk or kdim
    tn = tn or n
    tt = _tile(t, WGRAD_ROWS)
    nt = t // tt
    assert not (colsum and low) and (not colsum or tn == n)

    def body(a_ref, b_ref, o_ref, *s_ref):
        @pl.when(pl.program_id(2) == 0)
        def _():
            o_ref[...] = jnp.zeros_like(o_ref)
            if colsum:
                s_ref[0][...] = jnp.zeros_like(s_ref[0])

        av = a_ref[...]
        o_ref[...] += _bdot_tn(av, b_ref[...])
        if colsum:
            s_ref[0][...] += jnp.sum(av.astype(f32), axis=0, keepdims=True)
        if low:
            @pl.when(pl.program_id(2) == nt - 1)
            def _():
                s_ref[0][...] = o_ref[...].astype(bf16)

    out_specs = [pl.BlockSpec((tk, tn), lambda i, j, s: (i, j))]
    out_shape = [jax.ShapeDtypeStruct((kdim, n), f32)]
    if colsum:
        out_specs.append(pl.BlockSpec((1, tk), lambda i, j, s: (0, i)))
        out_shape.append(jax.ShapeDtypeStruct((1, kdim), f32))
    if low:
        out_specs.append(pl.BlockSpec((tk, tn), lambda i, j, s: (i, j)))
        out_shape.append(jax.ShapeDtypeStruct((kdim, n), bf16))
    res = pl.pallas_call(
        body, name=name, grid=(kdim // tk, n // tn, t // tt),
        in_specs=[pl.BlockSpec((tt, tk), lambda i, j, s: (s, i)), pl.BlockSpec((tt, tn), lambda i, j, s: (s, j))],
        out_specs=out_specs, out_shape=out_shape,
        compiler_params=_cparams(3),
    )(a, b)
    return res if (colsum or low) else res[0]


_TRANSPOSED = {"w_in", "w_ffn_gate", "w_ffn_up"}
_COL_SPLIT = {"ple_w_proj"}
_SCATTER_PLAN = (("w_ffn_gate", "w_ffn_up"), ("w_ffn_down", "w_out", "ple_w_gate", "ple_w_proj"))
_RIDE_PLAN = {"inproj": ("w_ffn_gate",), "hgrn2_fwd": ("w_ffn_up",), "mlstm_fwd": ("w_out",),
              "outproj_ln1": ("ple_w_gate", "ple_w_proj"), "ffn_up": ("w_ffn_down",)}


def _from_chip_major(a, col_split):
    if col_split:
        return a.transpose(1, 0, 2).reshape(a.shape[1], 4 * a.shape[2])
    return a.reshape(4 * a.shape[1], a.shape[2])


def _local_step(x, p, tgt, w_in_b, b_in, logits, conv_w, conv_b, hg_gn, ml_gn, w_out_b, ln1_g, ln1_b,
                wg_b, wu_b, wd_b, ln2_g, ln2_b, w_pp_b, w_pg_b, b_pg, early_hook=None, late_shards=None):
    pad_w = U_HG + U_ML - PROJ_W
    w_hg = w_in_b[:U_HG]
    w_ml = jnp.pad(w_in_b[U_HG:], ((0, pad_w), (0, 0)))
    bb_hg = b_in[:, :U_HG]
    bb_ml = jnp.pad(b_in[:, U_HG:], ((0, 0), (0, pad_w)))

    late = dict(w_out=w_out_b, w_ffn_gate=wg_b, w_ffn_up=wu_b, w_ffn_down=wd_b, ple_w_proj=w_pp_b, ple_w_gate=w_pg_b)

    def riders_of(call):
        return [late_shards[k] for k in _RIDE_PLAN[call]] if late_shards is not None else ()

    def arrived(call, got):
        for k, g in zip(_RIDE_PLAN[call], got):
            late[k] = _from_chip_major(g, k in _COL_SPLIT)

    (u_hg, u_ml, xb), got = _inproj(x, w_hg, w_ml, bb_hg, bb_ml, riders_of("inproj"))
    arrived("inproj", got)
    (og_hg, sst, hg_b, hg_a, hg_o), got = _hgrn2_fwd(u_hg, logits, hg_gn, riders_of("hgrn2_fwd"))
    arrived("hgrn2_fwd", got)
    pre, qkc = _conv_fwd(u_ml, conv_w, conv_b)
    (og_ml, cst, nst, mst), got = _mlstm_fwd(qkc, u_ml, ml_gn, riders_of("mlstm_fwd"))
    arrived("mlstm_fwd", got)
    (x1, xh1, rs1, x1b), got = _outproj_ln1(og_hg, og_ml, x, late["w_out"], ln1_g, ln1_b, riders_of("outproj_ln1"))
    arrived("outproj_ln1", got)
    (hgp, up, act), got = _ffn_up(x1b, late["w_ffn_gate"], late["w_ffn_up"], riders_of("ffn_up"))
    arrived("ffn_up", got)
    w_out_b, wg_b, wu_b, wd_b = late["w_out"], late["w_ffn_gate"], late["w_ffn_up"], late["w_ffn_down"]
    w_pp_b, w_pg_b = late["ple_w_proj"], late["ple_w_gate"]
    x2, xh2, rs2, x2b = _ffn_down_ln2(act, x1, wd_b, ln2_g, ln2_b)
    dr2, de, dz, loss, d_bpg, d_ln2g, d_ln2b = _head_loss_bwd(x2, xh2, rs2, p, tgt, w_pg_b, b_pg, w_pp_b, ln2_g)
    dr1, dhg, dup, d_ln1g, d_ln1b, dog_hg, dog_ml = _ffn_bwd(dr2, hgp, up, xh1, rs1, wd_b, wg_b, wu_b, ln1_g, w_out_b)

    d_wo_a, lo_wo_a = _wgrad(og_hg, dr1, "wgrad_out_hg", low=True)
    d_wo_b, lo_wo_b = _wgrad(og_ml, dr1, "wgrad_out_ml", low=True)
    d_wg, lo_wg = _wgrad(dhg, x1b, "wgrad_ffn_gate", tk=D_FF // 2, low=True)
    d_wu, lo_wu = _wgrad(dup, x1b, "wgrad_ffn_up", tk=D_FF // 2, low=True)
    d_wd, lo_wd = _wgrad(act, dr2, "wgrad_ffn_down", tk=D_FF // 2, low=True)
    d_wpp, lo_wpp = _wgrad(p, de, "wgrad_ple_proj", low=True)
    d_wpg, lo_wpg = _wgrad(x2b, dz, "wgrad_ple_gate", low=True)
    early = dict(w_out=jnp.concatenate([d_wo_a, d_wo_b], axis=0), w_ffn_gate=d_wg, w_ffn_up=d_wu, w_ffn_down=d_wd,
                 ple_w_proj=d_wpp, ple_w_gate=d_wpg)
    early_low = dict(w_out=jnp.concatenate([lo_wo_a, lo_wo_b], axis=0), w_ffn_gate=lo_wg, w_ffn_up=lo_wu, w_ffn_down=lo_wd,
                     ple_w_proj=lo_wpp, ple_w_gate=lo_wpg)
    ride_hg, ride_ml = early_hook(early_low) if early_hook is not None else ((), ())

    (du_hg, d_logits, d_hg_gn), got_hg = _hgrn2_bwd(u_hg, logits, hg_gn, sst, hg_b, hg_a, hg_o, dog_hg, ride_hg)
    (dqkc, dmv, dmo, dgt, d_ml_gn), got_ml = _mlstm_bwd(qkc, u_ml, ml_gn, cst, nst, mst, dog_ml, ride_ml)
    dqk, d_conv_w, d_conv_b = _conv_bwd(u_ml, conv_w, pre, dqkc)
    grad_x, du_ml = _inproj_bwd(dr1, du_hg, dqk, dmv, dmo, dgt, w_hg, w_ml)

    dw_hg, db_hg = _wgrad(du_hg, xb, "wgrad_in_hg", tk=U_HG // 2, colsum=True)
    dw_ml, db_ml = _wgrad(du_ml, xb, "wgrad_in_ml", colsum=True)
    d_w_in = jnp.concatenate([dw_hg, dw_ml[:PROJ_W - U_HG]], axis=0)
    d_b_in = jnp.concatenate([db_hg, db_ml[:, :PROJ_W - U_HG]], axis=1)

    grads = dict(w_in=d_w_in, b_in=d_b_in, hg_lb_logits=d_logits, ml_conv_w=d_conv_w, ml_conv_b=d_conv_b,
                 hg_norm_g=d_hg_gn, ml_norm_g=d_ml_gn, ln1_g=d_ln1g, ln1_b=d_ln1b, ln2_g=d_ln2g, ln2_b=d_ln2b,
                 ple_b_gate=d_bpg, **early)
    return loss, grad_x, grads, (list(got_hg), list(got_ml))


_ANY = pl.BlockSpec(memory_space=pltpu.HBM)
_MESH = pl.DeviceIdType.MESH


def _my_place():
    return lax.axis_index("x"), lax.axis_index("y"), lax.axis_index("c")


def _other_chips(x, y):
    return [(1 - x, y), (x, 1 - y), (1 - x, 1 - y)]


_VMEM = pl.BlockSpec(memory_space=pltpu.VMEM)
_EX_ROWS = 32


def _pair_reduce_cols(p, name):
    s, r, c = p.shape
    hc = c // 2

    def body(p_ref, o_ref, other, send_sem, recv_sem):
        x, y, cc = _my_place()

        def run(mine_lo, theirs_lo):
            cp = pltpu.make_async_remote_copy(src_ref=p_ref.at[pl.ds(0, s), pl.ds(0, r), pl.ds(theirs_lo, hc)], dst_ref=other,
                                              send_sem=send_sem, recv_sem=recv_sem, device_id=(x, y, 1 - cc), device_id_type=_MESH)
            cp.start()
            cp.wait()
            for slot in range(s):
                o_ref[slot] = (p_ref[slot, :, mine_lo:mine_lo + hc] + other[slot]).astype(bf16)

        @pl.when(cc == 0)
        def _():
            run(0, hc)

        @pl.when(cc == 1)
        def _():
            run(hc, 0)

    return pl.pallas_call(
        body, name=name, in_specs=[_VMEM], out_specs=_VMEM,
        out_shape=jax.ShapeDtypeStruct((s, r, hc), bf16),
        scratch_shapes=[pltpu.VMEM((s, r, hc), f32), pltpu.SemaphoreType.DMA, pltpu.SemaphoreType.DMA],
        compiler_params=pltpu.CompilerParams(vmem_limit_bytes=VMEM_LIMIT),
    )(p)


def _chip_reduce_swap_cols(rcv, name):
    s, r, hc = rcv.shape

    def body(r_ref, g_ref, send_sem, recv_sem):
        x, y, cc = _my_place()
        acc = r_ref[0].astype(f32)
        for slot in range(1, s):
            acc = acc + r_ref[slot].astype(f32)
        g_ref[cc] = acc
        cp = pltpu.make_async_remote_copy(src_ref=g_ref.at[cc], dst_ref=g_ref.at[cc], send_sem=send_sem, recv_sem=recv_sem,
                                          device_id=(x, y, 1 - cc), device_id_type=_MESH)
        cp.start()
        cp.wait()

    both = pl.pallas_call(
        body, name=name, in_specs=[_VMEM], out_specs=_VMEM,
        out_shape=jax.ShapeDtypeStruct((2, r, hc), f32),
        scratch_shapes=[pltpu.SemaphoreType.DMA, pltpu.SemaphoreType.DMA],
        compiler_params=pltpu.CompilerParams(vmem_limit_bytes=VMEM_LIMIT),
    )(rcv)
    return both.transpose(1, 0, 2).reshape(r, 2 * hc)


def _reduce_adamw(rcv, w, m, v, name):
    s, r, c = rcv.shape
    rows_per = _EX_ROWS
    half = r // 2
    steps = half // rows_per

    def body(r_ref, w_ref, m_ref, v_ref, g_ref, d_ref, nm_ref, nv_ref, mine, theirs, send_sems, recv_sems):
        x, y, cc = _my_place()

        def swap(k):
            rs = pl.ds(k * half, half)
            return pltpu.make_async_remote_copy(src_ref=mine.at[rs], dst_ref=theirs.at[rs], send_sem=send_sems.at[k], recv_sem=recv_sems.at[k],
                                                device_id=(x, y, 1 - cc), device_id_type=_MESH)

        def chip_sum(i, carry):
            rs = pl.ds(pl.multiple_of(i * rows_per, rows_per), rows_per)
            acc = r_ref[0, rs, :].astype(f32)
            for slot in range(1, s):
                acc = acc + r_ref[slot, rs, :].astype(f32)
            mine[rs, :] = acc
            return carry

        def update(i, carry):
            rs = pl.ds(pl.multiple_of(i * rows_per, rows_per), rows_per)
            g = mine[rs, :] + theirs[rs, :]
            nm = B1 * m_ref[rs, :] + (1.0 - B1) * g
            nv = B2 * v_ref[rs, :] + (1.0 - B2) * (g * g)
            g_ref[rs, :] = g
            nm_ref[rs, :] = nm
            nv_ref[rs, :] = nv
            d_ref[rs, :] = -LR * ((nm / (1.0 - B1 ** STEP)) / (jnp.sqrt(nv / (1.0 - B2 ** STEP)) + EPS_ADAM) + WD * w_ref[rs, :])
            return carry

        lax.fori_loop(0, steps, chip_sum, 0)
        swap(0).start()
        lax.fori_loop(steps, 2 * steps, chip_sum, 0)
        swap(1).start()
        swap(0).wait()
        lax.fori_loop(0, steps, update, 0)
        swap(1).wait()
        lax.fori_loop(steps, 2 * steps, update, 0)

    return pl.pallas_call(
        body, name=name, in_specs=[_VMEM] * 4, out_specs=[_VMEM] * 4,
        out_shape=[jax.ShapeDtypeStruct((r, c), f32)] * 4,
        scratch_shapes=[pltpu.VMEM((r, c), f32), pltpu.VMEM((r, c), f32), pltpu.SemaphoreType.DMA((2,)), pltpu.SemaphoreType.DMA((2,))],
        compiler_params=pltpu.CompilerParams(vmem_limit_bytes=VMEM_LIMIT),
    )(rcv, w, m, v)


def _gather_copies(ins, outs, send_sems, recv_sems, local_sems):
    x, y, c = _my_place()
    me = 2 * x + y
    local, outgoing, incoming = [], [], []
    for a in range(len(ins)):
        local.append(pltpu.make_async_copy(ins[a], outs[a].at[me], local_sems.at[a]))
        for j, (px, py) in enumerate(_other_chips(x, y)):
            sems = dict(send_sem=send_sems.at[3 * a + j], recv_sem=recv_sems.at[3 * a + j], device_id=(px, py, c), device_id_type=_MESH)
            outgoing.append(pltpu.make_async_remote_copy(src_ref=ins[a], dst_ref=outs[a].at[me], **sems))
            incoming.append(pltpu.make_async_remote_copy(src_ref=ins[a], dst_ref=outs[a].at[2 * px + py], **sems))
    return local, outgoing, incoming


def _gather_first(block, taps, name):
    r, c = block.shape
    hc = c // 2

    def body(in_ref, tap_in, out_ref, tap_out, send_sems, recv_sems):
        x, y, cc = _my_place()
        me = 2 * x + y
        sibling = (x, y, 1 - cc)
        chips = _other_chips(x, y)
        out_ref[me] = in_ref[...]
        tap_out[me] = tap_in[...]

        def run(mine, theirs):
            def ici(j, chip):
                px, py = chips[j]
                src = in_ref.at[pl.ds(0, r), pl.ds(mine, hc)] if chip is None else out_ref.at[chip, pl.ds(0, r), pl.ds(mine, hc)]
                dst = out_ref.at[me if chip is None else chip, pl.ds(0, r), pl.ds(mine, hc)]
                return pltpu.make_async_remote_copy(src_ref=src, dst_ref=dst, send_sem=send_sems.at[j], recv_sem=recv_sems.at[j],
                                                    device_id=(px, py, cc), device_id_type=_MESH)

            def d2d(j, lo):
                px, py = chips[j]
                blk = out_ref.at[2 * px + py, pl.ds(0, r), pl.ds(lo, hc)]
                return pltpu.make_async_remote_copy(src_ref=blk, dst_ref=blk, send_sem=send_sems.at[3 + j], recv_sem=recv_sems.at[3 + j],
                                                    device_id=sibling, device_id_type=_MESH)

            def tap(j, chip):
                px, py = chips[j]
                return pltpu.make_async_remote_copy(src_ref=tap_in, dst_ref=tap_out.at[me if chip is None else chip],
                                                    send_sem=send_sems.at[6 + j], recv_sem=recv_sems.at[6 + j],
                                                    device_id=(px, py, cc), device_id_type=_MESH)

            for j in range(3):
                ici(j, None).start()
                tap(j, None).start()
            for j, (px, py) in enumerate(chips):
                ici(j, 2 * px + py).wait_recv()
                d2d(j, mine).start()
            for j, (px, py) in enumerate(chips):
                d2d(j, theirs).wait_recv()
                tap(j, 2 * px + py).wait_recv()
            for j in range(3):
                ici(j, None).wait_send()
                d2d(j, mine).wait_send()
                tap(j, None).wait_send()

        @pl.when(cc == 0)
        def _():
            run(0, hc)

        @pl.when(cc == 1)
        def _():
            run(hc, 0)

    return pl.pallas_call(
        body, name=name, in_specs=[_VMEM, _VMEM], out_specs=[_VMEM, _VMEM],
        out_shape=[jax.ShapeDtypeStruct((4, r, c), block.dtype), jax.ShapeDtypeStruct((4,) + taps.shape, taps.dtype)],
        scratch_shapes=[pltpu.SemaphoreType.DMA((9,)), pltpu.SemaphoreType.DMA((9,))],
        compiler_params=pltpu.CompilerParams(vmem_limit_bytes=VMEM_LIMIT),
    )(block, taps)


def _riding_call(body, name, nsteps, in_specs, out_specs, out_shape, scratch_shapes, operands, riders, copies, ride_shapes):
    nr, n_in, n_out, n_scr = len(riders), len(in_specs), len(out_specs), len(scratch_shapes)

    def wrapped(*refs):
        ins, ride_in = refs[:n_in], refs[n_in:n_in + nr]
        outs, ride_out = refs[n_in + nr:n_in + nr + n_out], refs[n_in + nr + n_out:n_in + 2 * nr + n_out]
        scratch, sems = refs[n_in + 2 * nr + n_out:n_in + 2 * nr + n_out + n_scr], refs[n_in + 2 * nr + n_out + n_scr:]
        if nr:
            @pl.when(pl.program_id(0) == 0)
            def _():
                local, outgoing, _ = copies(ride_in, ride_out, *sems)
                for cp in local + outgoing:
                    cp.start()

        body(*ins, *outs, *scratch)
        if nr:
            @pl.when(pl.program_id(0) == nsteps - 1)
            def _():
                local, outgoing, incoming = copies(ride_in, ride_out, *sems)
                for cp in incoming:
                    cp.wait_recv()
                for cp in outgoing:
                    cp.wait_send()
                for cp in local:
                    cp.wait()

    hbm = pl.BlockSpec(memory_space=pltpu.HBM)
    sems = [pltpu.SemaphoreType.DMA((3 * nr,)), pltpu.SemaphoreType.DMA((3 * nr,)), pltpu.SemaphoreType.DMA((nr,))] if nr else []
    res = pl.pallas_call(
        wrapped, name=name, grid=(nsteps,),
        in_specs=list(in_specs) + [hbm] * nr, out_specs=list(out_specs) + [hbm] * nr,
        out_shape=list(out_shape) + list(ride_shapes),
        scratch_shapes=list(scratch_shapes) + sems,
        compiler_params=_cparams(1),
    )(*operands, *riders)
    return list(res[:n_out]), list(res[n_out:])


def _gather_shapes(riders):
    return [jax.ShapeDtypeStruct((4,) + r.shape, r.dtype) for r in riders]


def _scatter_copies(ins, outs, send_sems, recv_sems, local_sems):
    x, y, c = _my_place()
    me = 2 * x + y
    local, outgoing, incoming = [], [], []
    for a in range(len(ins)):
        local.append(pltpu.make_async_copy(ins[a].at[me], outs[a].at[me], local_sems.at[a]))
        for j, (px, py) in enumerate(_other_chips(x, y)):
            sems = dict(send_sem=send_sems.at[3 * a + j], recv_sem=recv_sems.at[3 * a + j], device_id=(px, py, c), device_id_type=_MESH)
            outgoing.append(pltpu.make_async_remote_copy(src_ref=ins[a].at[2 * px + py], dst_ref=outs[a].at[me], **sems))
            incoming.append(pltpu.make_async_remote_copy(src_ref=ins[a].at[2 * px + py], dst_ref=outs[a].at[2 * px + py], **sems))
    return local, outgoing, incoming


def _scatter_chips(pieces, name):
    n = len(pieces)

    def body(*refs):
        local, outgoing, incoming = _scatter_copies(refs[:n], refs[n:2 * n], *refs[2 * n:])
        for cp in local + outgoing:
            cp.start()
        for cp in incoming:
            cp.wait_recv()
        for cp in outgoing:
            cp.wait_send()
        for cp in local:
            cp.wait()

    return pl.pallas_call(
        body, name=name,
        in_specs=[_ANY] * n, out_specs=[_ANY] * n,
        out_shape=[jax.ShapeDtypeStruct(s.shape, s.dtype) for s in pieces],
        scratch_shapes=[pltpu.SemaphoreType.DMA((3 * n,)), pltpu.SemaphoreType.DMA((3 * n,)), pltpu.SemaphoreType.DMA((n,))],
    )(*pieces)


def _gather_all(block, name):
    def body(in_ref, out_ref, send_sems, recv_sems, local_sem):
        x, y, c = _my_place()
        me = 4 * x + 2 * y + c
        cp = pltpu.make_async_copy(in_ref, out_ref.at[me], local_sem)
        cp.start()
        peers = []
        for dx in range(2):
            for dy in range(2):
                for dc in range(2):
                    if dx or dy or dc:
                        peers.append((1 - x if dx else x, 1 - y if dy else y, 1 - c if dc else c))
        for j, pr in enumerate(peers):
            pltpu.make_async_remote_copy(src_ref=in_ref, dst_ref=out_ref.at[me], send_sem=send_sems.at[j], recv_sem=recv_sems.at[j],
                                         device_id=pr, device_id_type=_MESH).start()
        for j, (px, py, pc) in enumerate(peers):
            pltpu.make_async_remote_copy(src_ref=in_ref, dst_ref=out_ref.at[4 * px + 2 * py + pc], send_sem=send_sems.at[j], recv_sem=recv_sems.at[j],
                                         device_id=(px, py, pc), device_id_type=_MESH).wait()
        cp.wait()

    return pl.pallas_call(
        body, name=name,
        in_specs=[_ANY], out_specs=_ANY,
        out_shape=jax.ShapeDtypeStruct((8,) + block.shape, block.dtype),
        scratch_shapes=[pltpu.SemaphoreType.DMA((7,)), pltpu.SemaphoreType.DMA((7,)), pltpu.SemaphoreType.DMA],
    )(block)


def _row_tile(r, c):
    best = r
    for cand in range(16, r + 1, 16):
        if r % cand == 0 and cand * c * 4 <= (1 << 20):
            best = cand
    return best if best * c * 4 <= (4 << 20) else r


def _sum_slots(parts, name):
    n, r, c = parts.shape
    tr = _row_tile(r, c)

    def body(p_ref, o_ref):
        acc = p_ref[0].astype(f32)
        for s in range(1, n):
            acc = acc + p_ref[s].astype(f32)
        o_ref[...] = acc

    return pl.pallas_call(
        body, name=name, grid=(r // tr,),
        in_specs=[pl.BlockSpec((n, tr, c), lambda i: (0, i, 0))],
        out_specs=pl.BlockSpec((tr, c), lambda i: (i, 0)),
        out_shape=jax.ShapeDtypeStruct((r, c), f32),
        compiler_params=_cparams(1, arbitrary=False),
    )(parts)


def _adamw(parts, w, m, v, name):
    n, r, c = parts.shape
    tr = _row_tile(r, c)
    tc = c
    if tr == r and r * c * 4 > (1 << 20) and c % 256 == 0:
        tc = 256

    def body(p_ref, w_ref, m_ref, v_ref, g_ref, d_ref, nm_ref, nv_ref):
        g = p_ref[0]
        for s in range(1, n):
            g = g + p_ref[s]
        nm = B1 * m_ref[...] + (1.0 - B1) * g
        nv = B2 * v_ref[...] + (1.0 - B2) * (g * g)
        m_hat = nm / (1.0 - B1 ** STEP)
        v_hat = nv / (1.0 - B2 ** STEP)
        g_ref[...] = g
        nm_ref[...] = nm
        nv_ref[...] = nv
        d_ref[...] = -LR * (m_hat / (jnp.sqrt(v_hat) + EPS_ADAM) + WD * w_ref[...])

    blk = pl.BlockSpec((tr, tc), lambda i, j: (i, j))
    return pl.pallas_call(
        body, name=name, grid=(r // tr, c // tc),
        in_specs=[pl.BlockSpec((n, tr, tc), lambda i, j: (0, i, j)), blk, blk, blk],
        out_specs=[blk] * 4,
        out_shape=[jax.ShapeDtypeStruct((r, c), f32)] * 4,
        compiler_params=_cparams(2, arbitrary=False),
    )(parts, w, m, v)


_BIG = ["w_in", "w_out", "w_ffn_gate", "w_ffn_up", "w_ffn_down", "ple_w_proj", "ple_w_gate"]
_SMALL = ["b_in", "hg_lb_logits", "ml_conv_w", "ml_conv_b", "hg_norm_g", "ml_norm_g", "ln1_g", "ln1_b", "ln2_g", "ln2_b", "ple_b_gate"]
_ORDER = ["w_in", "b_in", "hg_lb_logits", "ml_conv_w", "ml_conv_b", "hg_norm_g", "ml_norm_g", "w_out", "ln1_g", "ln1_b",
          "w_ffn_gate", "w_ffn_up", "w_ffn_down", "ln2_g", "ln2_b", "ple_w_proj", "ple_w_gate", "ple_b_gate"]
_PACK_ROWS, _PACK_COLS = 16, 1024


def _pack(arrays):
    flat = jnp.concatenate([a.reshape(-1) for a in arrays])
    return jnp.pad(flat, (0, _PACK_ROWS * _PACK_COLS - flat.shape[0])).reshape(_PACK_ROWS, _PACK_COLS)


def _unpack(pack, shapes):
    flat = pack.reshape(-1)
    out, off = [], 0
    for s in shapes:
        size = 1
        for d in s:
            size *= d
        out.append(flat[off:off + size].reshape(s))
        off += size
    return out


def _to_chip_major(g, col_split):
    if col_split:
        k, n = g.shape
        return g.reshape(k, 4, n // 4).transpose(1, 0, 2)
    k, n = g.shape
    return g.reshape(4, k // 4, n)


def kernel(x, p, w_in, b_in, hg_lb_logits, ml_conv_w, ml_conv_b, hg_norm_g, ml_norm_g, w_out, ln1_g, ln1_b, w_ffn_gate, w_ffn_up, w_ffn_down, ln2_g, ln2_b, ple_w_proj, ple_w_gate, ple_b_gate, loss_target, m_w_in, m_b_in, m_hg_lb_logits, m_ml_conv_w, m_ml_conv_b, m_hg_norm_g, m_ml_norm_g, m_w_out, m_ln1_g, m_ln1_b, m_w_ffn_gate, m_w_ffn_up, m_w_ffn_down, m_ln2_g, m_ln2_b, m_ple_w_proj, m_ple_w_gate, m_ple_b_gate, v_w_in, v_b_in, v_hg_lb_logits, v_ml_conv_w, v_ml_conv_b, v_hg_norm_g, v_ml_norm_g, v_w_out, v_ln1_g, v_ln1_b, v_w_ffn_gate, v_w_ffn_up, v_w_ffn_down, v_ln2_g, v_ln2_b, v_ple_w_proj, v_ple_w_gate, v_ple_b_gate):
    args = dict(locals())
    wts = {k: args[k] for k in _ORDER}
    mom = {k: args["m_" + k] for k in _ORDER}
    var = {k: args["v_" + k] for k in _ORDER}
    two_d = lambda a: a.reshape(a.shape[-2], a.shape[-1])
    block = lambda k, a: jnp.swapaxes(two_d(a), 0, 1) if k in _TRANSPOSED else two_d(a)
    unblock = lambda k, a: (jnp.swapaxes(a, 0, 1) if k in _TRANSPOSED else a).reshape(wts[k].shape)

    shards = {k: block(k, wts[k]).astype(bf16) for k in _BIG}
    w_in_blocks, taps = _gather_first(shards["w_in"], two_d(ml_conv_w), "gather_w_in")
    w_in_full = _from_chip_major(w_in_blocks, False)
    conv_w_full = _from_chip_major(taps, True)

    early_keys = _BIG[1:]
    loss, grad_x, grads, (got_hg, got_ml) = _local_step(
        x[0], p[0, 0], loss_target[0], w_in_full, b_in, hg_lb_logits, conv_w_full, ml_conv_b, hg_norm_g, ml_norm_g,
        None, ln1_g, ln1_b, None, None, None, ln2_g, ln2_b, None, None, ple_b_gate,
        early_hook=lambda low: tuple([_to_chip_major(low[k], k in _COL_SPLIT) for k in names] for names in _SCATTER_PLAN),
        late_shards={k: shards[k] for k in early_keys})

    out_g, out_d, out_m, out_v = {}, {}, {}, {}

    def finish(k, g, d, nm, nv):
        out_g[k], out_d[k], out_m[k], out_v[k] = unblock(k, g), unblock(k, d), unblock(k, nm), unblock(k, nv)

    for names, got in zip(_SCATTER_PLAN, (got_hg, got_ml)):
        for k, rcv in zip(names, got):
            finish(k, *_reduce_adamw(rcv, block(k, wts[k]), block(k, mom[k]), block(k, var[k]), "reduce_adamw_" + k))

    core_sums = _pair_reduce_cols(_to_chip_major(grads["w_in"], False), "pair_reduce_w_in")
    whole = _chip_reduce_swap_cols(_scatter_chips([core_sums], "scatter_grad_w_in")[0], "chip_reduce_w_in")
    finish("w_in", *_adamw(whole[None], block("w_in", wts["w_in"]), block("w_in", mom["w_in"]), block("w_in", var["w_in"]), "adamw_w_in"))

    small_shapes = [(1, PROJ_W), (2, MIX_W), (CONV_K, MIX_W)] + [(1, MIX_W)] * 3 + [(1, D_MODEL)] * 5 + [(1, 1)]
    contrib = _pack([grads[k] for k in _SMALL] + [loss])
    summed = _sum_slots(_gather_all(contrib, "gather_small"), "sum_small")
    small = _unpack(summed, small_shapes)
    loss_total = small[-1].reshape(())
    gsm = dict(zip(_SMALL, small[:-1]))
    place = 2 * lax.axis_index("x") + lax.axis_index("y")
    conv_cols = ml_conv_w.shape[-1]
    gsm["ml_conv_w"] = lax.dynamic_slice(gsm["ml_conv_w"], (0, place * conv_cols), (CONV_K, conv_cols))
    own_shapes = [wts[k].shape for k in _SMALL]
    g_pack = _pack([gsm[k] for k in _SMALL])
    res = _adamw(g_pack[None], _pack([wts[k] for k in _SMALL]), _pack([mom[k] for k in _SMALL]), _pack([var[k] for k in _SMALL]), "adamw_small")
    for dst, pack in zip((out_g, out_d, out_m, out_v), res):
        for k, a in zip(_SMALL, _unpack(pack, own_shapes)):
            dst[k] = a

    outs = [loss_total, grad_x[None]]
    for group in (out_g, out_d, out_m, out_v):
        outs += [group[k] for k in _ORDER]
    return tuple(outs)
```

```python
import jax
import jax.numpy as jnp
from jax import lax
from jax.experimental import pallas as pl
from jax.experimental.pallas import tpu as pltpu

f32 = jnp.float32
bf16 = jnp.bfloat16

D_MODEL = 1024
HEADS = 4
HEAD_W = 128
MIX_W = HEADS * HEAD_W
ML_DQK = 64
PROJ_W = 3592
U_HG = 4 * MIX_W
U_ML = 3 * MIX_W + 128
D_FF = 2816
PLE = 256
CHUNK = 128
SUB = 16
EXP_CAP = 80.0
CONV_K = 4
HALO = 8
ALPHA = float(2.0 ** 0.25)
LN_EPS = 1e-5
RMS_EPS = 1e-6
NEG = -1e30
LR, B1, B2, EPS_ADAM, WD, STEP = 0.001, 0.9, 0.999, 1e-08, 0.01, 10
VMEM_LIMIT = 56 * 1024 * 1024
MIXER_ROWS = 512
DENSE_ROWS = 512
LIGHT_ROWS = 1024
WGRAD_ROWS = 2048


def _cparams(n_axes, arbitrary=True):
    sem = ("arbitrary",) * n_axes if arbitrary else ("parallel",) * n_axes
    return pltpu.CompilerParams(dimension_semantics=sem, vmem_limit_bytes=VMEM_LIMIT)


ACT = bf16


def _mx(a):
    return a.astype(ACT)


def _bdot(a, b):
    return jnp.dot(_mx(a), _mx(b), preferred_element_type=f32)


def _bdot_nt(a, b):
    return lax.dot_general(_mx(a), _mx(b), (((1,), (1,)), ((), ())), preferred_element_type=f32)


def _bdot_tn(a, b):
    return lax.dot_general(_mx(a), _mx(b), (((0,), (0,)), ((), ())), preferred_element_type=f32)


def _split3(x):
    hi = x.astype(bf16)
    r1 = x - hi.astype(f32)
    mid = r1.astype(bf16)
    lo = (r1 - mid.astype(f32)).astype(bf16)
    return hi, mid, lo


def _dot3(a, b, dims):
    a_hi = a.astype(bf16)
    a_lo = (a - a_hi.astype(f32)).astype(bf16)
    b_hi = b.astype(bf16)
    b_lo = (b - b_hi.astype(f32)).astype(bf16)
    dn = (dims, ((), ()))
    return (lax.dot_general(a_hi, b_hi, dn, preferred_element_type=f32) + lax.dot_general(a_hi, b_lo, dn, preferred_element_type=f32)
            + lax.dot_general(a_lo, b_hi, dn, preferred_element_type=f32))


def _lane_sum(x):
    hi = x.astype(bf16)
    lo = (x - hi.astype(f32)).astype(bf16)
    ones = jnp.ones((x.shape[1], 128), bf16)
    return jnp.dot(hi, ones, preferred_element_type=f32) + jnp.dot(lo, ones, preferred_element_type=f32)


def _lane_dot(x, row):
    return _dot3(x, jnp.broadcast_to(row, (128, row.shape[1])), ((1,), (1,)))


def _sel_dot(sel, x):
    sb = sel.astype(bf16)
    return sum(jnp.dot(sb, part, preferred_element_type=f32) for part in _split3(x))


def _sel_dot_nt(sel, x):
    sb = sel.astype(bf16)
    return sum(lax.dot_general(sb, part, (((1,), (1,)), ((), ())), preferred_element_type=f32) for part in _split3(x))


def _sigmoid(x):
    return 1.0 / (1.0 + jnp.exp(-x))


def _log_sigmoid(x):
    return jnp.minimum(x, 0.0) - jnp.log(1.0 + jnp.exp(-jnp.abs(x)))


def _tri(n, upper=False):
    r = lax.broadcasted_iota(jnp.int32, (n, n), 0)
    c = lax.broadcasted_iota(jnp.int32, (n, n), 1)
    return (c >= r) if upper else (c <= r)


def _rows(tm, n, col=0):
    return pl.BlockSpec((tm, n), lambda i, _c=col: (i, _c))


def _rows_rev(tm, n, nb, col=0):
    return pl.BlockSpec((tm, n), lambda i, _c=col, _nb=nb: (_nb - 1 - i, _c))


def _const(shape):
    return pl.BlockSpec(shape, lambda i, _n=len(shape): (0,) * _n)


def _resident(shape):
    return pl.BlockSpec(shape, lambda i, _n=len(shape): (0,) * _n, pipeline_mode=pl.Buffered(1))


def _tile(t, want):
    return want if t % want == 0 else t


def _inproj(x, w_hg, w_ml, b_hg, b_ml, riders=()):
    t = x.shape[0]
    tm = _tile(t, DENSE_ROWS)

    def body(x_ref, whg_ref, wml_ref, bhg_ref, bml_ref, uhg_ref, uml_ref, xb_ref):
        xb = _mx(x_ref[...])
        xb_ref[...] = xb
        uhg_ref[...] = _bdot_nt(xb, whg_ref[...]) + bhg_ref[...]
        uml_ref[...] = _bdot_nt(xb, wml_ref[...]) + bml_ref[...]

    return _riding_call(
        body, "inproj", t // tm,
        in_specs=[_rows(tm, D_MODEL), _resident((U_HG, D_MODEL)), _resident((U_ML, D_MODEL)), _const((1, U_HG)), _const((1, U_ML))],
        out_specs=[_rows(tm, U_HG), _rows(tm, U_ML), _rows(tm, D_MODEL)],
        out_shape=[jax.ShapeDtypeStruct((t, U_HG), f32), jax.ShapeDtypeStruct((t, U_ML), f32), jax.ShapeDtypeStruct((t, D_MODEL), ACT)],
        scratch_shapes=[], operands=(x, w_hg, w_ml, b_hg, b_ml), riders=riders, copies=_gather_copies, ride_shapes=_gather_shapes(riders))


def _hg_gates(hq, hf, lb, tri, b=None):
    s = _sigmoid(hf)
    om = 1.0 - lb
    f = lb + om * s
    k = om * (1.0 - s)
    sq = _sigmoid(hq)
    q = hq * sq
    if b is None:
        b = _sel_dot(tri, jnp.log(f))
    return q, sq, s, f, k, b


def _hg_scores(q, k, b, tril_mask, a=None):
    qts, kts, eqs, eks, rows = [], [], [], [], []
    for i in range(CHUNK // SUB):
        lo = i * SUB
        ref = jnp.zeros_like(b[0:1]) if i == 0 else b[lo - 1:lo]
        eq = jnp.exp(b[lo:lo + SUB] - ref)
        ek = jnp.exp(jnp.minimum(ref - b, EXP_CAP))
        qt = q[lo:lo + SUB] * eq
        kt = k * ek
        if a is None:
            rows.append(_bdot_nt(qt, kt))
        qts.append(qt); kts.append(kt); eqs.append(eq); eks.append(ek)
    if a is None:
        a = jnp.where(tril_mask, jnp.concatenate(rows, axis=0), 0.0)
    return a, qts, kts, eqs, eks


def _head_rms(o, gn, on_mxu=False):
    ms = _lane_sum(o * o) * (1.0 / o.shape[1]) if on_mxu else jnp.mean(o * o, axis=-1, keepdims=True)
    rstd = lax.rsqrt(ms + RMS_EPS)
    oh = o * rstd
    return oh, rstd, oh * gn


def _lower_bound(logit_ref):
    lg = logit_ref[...]
    return _sigmoid(lg[0:1] - lg[1:2])


def _hgrn2_fwd(u_hg, logits, gn, riders=()):
    t = u_hg.shape[0]
    tb = _tile(t, MIXER_ROWS)
    nc_blk = tb // CHUNK

    def body(u_ref, lg_ref, gn_ref, og_ref, sst_ref, b_ref, a_ref, o_ref, st_ref):
        @pl.when(pl.program_id(0) == 0)
        def _():
            st_ref[...] = jnp.zeros_like(st_ref)

        lb_all = _lower_bound(lg_ref)
        tril_mask = _tri(CHUNK)
        tri = tril_mask.astype(f32)

        def chunk(c, carry):
            r0 = pl.multiple_of(c * CHUNK, CHUNK)
            rows = pl.ds(r0, CHUNK)
            heads = range(HEADS)
            cols = [slice(h * HEAD_W, (h + 1) * HEAD_W) for h in heads]
            hv = [u_ref[rows, 2 * MIX_W + h * HEAD_W:2 * MIX_W + (h + 1) * HEAD_W] for h in heads]
            gts = [_hg_gates(u_ref[rows, h * HEAD_W:(h + 1) * HEAD_W], u_ref[rows, MIX_W + h * HEAD_W:MIX_W + (h + 1) * HEAD_W],
                             lb_all[:, cols[h]], tri) for h in heads]
            q = [g[0] for g in gts]
            k = [g[4] for g in gts]
            b = [g[5] for g in gts]
            a = [_hg_scores(q[h], k[h], b[h], tril_mask)[0] for h in heads]
            st = [st_ref[h] for h in heads]
            bl = [b[h][CHUNK - 1:CHUNK] for h in heads]
            o = [_bdot(a[h], hv[h]) + _bdot_nt(q[h] * jnp.exp(b[h]), st[h]) for h in heads]
            new_st = [st[h] * jnp.exp(bl[h]) + _bdot_tn(hv[h], k[h] * jnp.exp(bl[h] - b[h])) for h in heads]
            for h in heads:
                sst_ref[c, h] = st[h]
                st_ref[h] = new_st[h]
                b_ref[rows, cols[h]] = b[h]
                a_ref[rows, cols[h]] = a[h].astype(ACT)
                o_ref[rows, cols[h]] = o[h]
                hgate = u_ref[rows, 3 * MIX_W + h * HEAD_W:3 * MIX_W + (h + 1) * HEAD_W]
                _, _, y = _head_rms(o[h], gn_ref[:, cols[h]])
                og_ref[rows, cols[h]] = (y * (hgate * _sigmoid(hgate))).astype(ACT)
            return carry

        lax.fori_loop(0, nc_blk, chunk, 0, unroll=True)

    assert CHUNK == HEAD_W
    return _riding_call(
        body, "hgrn2_fwd", t // tb,
        in_specs=[_rows(tb, U_HG), _const((2, MIX_W)), _const((1, MIX_W))],
        out_specs=[_rows(tb, MIX_W), pl.BlockSpec((nc_blk, HEADS, HEAD_W, HEAD_W), lambda i: (i, 0, 0, 0)),
                   _rows(tb, MIX_W), _rows(tb, MIX_W), _rows(tb, MIX_W)],
        out_shape=[jax.ShapeDtypeStruct((t, MIX_W), ACT), jax.ShapeDtypeStruct((t // CHUNK, HEADS, HEAD_W, HEAD_W), f32),
                   jax.ShapeDtypeStruct((t, MIX_W), f32), jax.ShapeDtypeStruct((t, MIX_W), ACT), jax.ShapeDtypeStruct((t, MIX_W), f32)],
        scratch_shapes=[pltpu.VMEM((HEADS, HEAD_W, HEAD_W), f32)],
        operands=(u_hg, logits, gn), riders=riders, copies=_gather_copies, ride_shapes=_gather_shapes(riders))


def _hgrn2_bwd(u_hg, logits, gn, sst, bcum, scores, o_raw, dog, riders=()):
    t = u_hg.shape[0]
    tb = _tile(t, MIXER_ROWS)
    nb = t // tb
    nc_blk = tb // CHUNK

    def body(u_ref, lg_ref, gn_ref, sst_ref, b_ref, a_ref, o_ref, dog_ref, du_ref, dlg_ref, dgn_ref, dst_ref):
        @pl.when(pl.program_id(0) == 0)
        def _():
            dst_ref[...] = jnp.zeros_like(dst_ref)
            dlg_ref[...] = jnp.zeros_like(dlg_ref)
            dgn_ref[...] = jnp.zeros_like(dgn_ref)

        lb_all = _lower_bound(lg_ref)
        tril_mask = _tri(CHUNK)
        tri = tril_mask.astype(f32)
        triu = _tri(CHUNK, upper=True).astype(f32)

        def chunk(j, carry):
            c = nc_blk - 1 - j
            r0 = pl.multiple_of(c * CHUNK, CHUNK)
            rows = pl.ds(r0, CHUNK)
            heads = range(HEADS)
            nsub = CHUNK // SUB
            cols = [slice(h * HEAD_W, (h + 1) * HEAD_W) for h in heads]
            hq = [u_ref[rows, h * HEAD_W:(h + 1) * HEAD_W] for h in heads]
            hf = [u_ref[rows, MIX_W + h * HEAD_W:MIX_W + (h + 1) * HEAD_W] for h in heads]
            hv = [u_ref[rows, 2 * MIX_W + h * HEAD_W:2 * MIX_W + (h + 1) * HEAD_W] for h in heads]
            lb = [lb_all[:, cols[h]] for h in heads]
            gts = [_hg_gates(hq[h], hf[h], lb[h], tri, b=b_ref[rows, cols[h]]) for h in heads]
            q, sq, s, f, k, b = ([g[n] for g in gts] for n in range(6))
            scs = [_hg_scores(q[h], k[h], b[h], tril_mask, a=a_ref[rows, cols[h]]) for h in heads]
            a, qts, kts, eqs, eks = ([sc[n] for sc in scs] for n in range(5))
            st = [sst_ref[c, h] for h in heads]
            dst = [dst_ref[h] for h in heads]
            bl = [b[h][CHUNK - 1:CHUNK] for h in heads]
            eb = [jnp.exp(b[h]) for h in heads]
            qh = [q[h] * eb[h] for h in heads]
            ekl = [jnp.exp(bl[h] - b[h]) for h in heads]
            kh = [k[h] * ekl[h] for h in heads]
            o = [o_ref[rows, cols[h]] for h in heads]
            do = []
            for h in heads:
                hgate = u_ref[rows, 3 * MIX_W + h * HEAD_W:3 * MIX_W + (h + 1) * HEAD_W]
                gnh = gn_ref[:, cols[h]]
                oh, rstd, y = _head_rms(o[h], gnh)
                sg = _sigmoid(hgate)
                dogh = dog_ref[rows, cols[h]]
                dy = dogh * (hgate * sg)
                du_ref[rows, 3 * MIX_W + h * HEAD_W:3 * MIX_W + (h + 1) * HEAD_W] = (dogh * y * (sg * (1.0 + hgate * (1.0 - sg)))).astype(ACT)
                dgn_ref[:, cols[h]] += jnp.sum(dy * oh, axis=0, keepdims=True)
                doh = dy * gnh
                do.append(rstd * (doh - oh * jnp.mean(doh * oh, axis=-1, keepdims=True)))
            da = [jnp.where(tril_mask, _bdot_nt(do[h], hv[h]), 0.0) for h in heads]
            dv = [_bdot_tn(a[h], do[h]) + _bdot_nt(kh[h], dst[h]) for h in heads]
            dq = [_bdot(do[h], st[h]) * eb[h] for h in heads]
            dk = [_bdot(hv[h], dst[h]) * ekl[h] for h in heads]
            d_last = [jnp.sum(k[h] * dk[h], axis=0, keepdims=True) + jnp.exp(bl[h]) * jnp.sum(dst[h] * st[h], axis=0, keepdims=True)
                      for h in heads]
            d_b = [q[h] * dq[h] - k[h] * dk[h] for h in heads]
            dqs = [[] for _ in heads]
            q_dq = [[] for _ in heads]
            for i in range(nsub):
                for h in heads:
                    da_i = _mx(da[h][i * SUB:(i + 1) * SUB])
                    q_r, k_r = _mx(qts[h][i]), _mx(kts[h][i])
                    g_q = jnp.dot(da_i, k_r, preferred_element_type=f32)
                    g_k = lax.dot_general(da_i, q_r, (((0,), (0,)), ((), ())), preferred_element_type=f32)
                    dqs[h].append(g_q * eqs[h][i])
                    q_dq[h].append(q_r.astype(f32) * g_q)
                    dk[h] = dk[h] + g_k * eks[h][i]
                    d_b[h] = d_b[h] - k_r.astype(f32) * g_k
            for h in heads:
                dq[h] = dq[h] + jnp.concatenate(dqs[h], axis=0)
                d_b[h] = d_b[h] + jnp.concatenate(q_dq[h], axis=0)
                dst_ref[h] = dst[h] * jnp.exp(bl[h]) + _bdot_tn(do[h], qh[h])
            dg = [_sel_dot(triu, d_b[h]) + d_last[h] for h in heads]
            for h in heads:
                dfk = dg[h] / f[h] - dk[h]
                du_ref[rows, h * HEAD_W:(h + 1) * HEAD_W] = (dq[h] * (sq[h] * (1.0 + hq[h] * (1.0 - sq[h])))).astype(ACT)
                du_ref[rows, MIX_W + h * HEAD_W:MIX_W + (h + 1) * HEAD_W] = ((1.0 - lb[h]) * dfk * s[h] * (1.0 - s[h])).astype(ACT)
                du_ref[rows, 2 * MIX_W + h * HEAD_W:2 * MIX_W + (h + 1) * HEAD_W] = dv[h].astype(ACT)
                dlb = jnp.sum((1.0 - s[h]) * dfk, axis=0, keepdims=True) * (lb[h] * (1.0 - lb[h]))
                dlg_ref[0:1, cols[h]] += dlb
                dlg_ref[1:2, cols[h]] -= dlb
            return carry

        lax.fori_loop(0, nc_blk, chunk, 0, unroll=True)

    rev = _rows_rev(tb, MIX_W, nb)
    return _riding_call(
        body, "hgrn2_bwd", nb,
        in_specs=[_rows_rev(tb, U_HG, nb), _const((2, MIX_W)), _const((1, MIX_W)),
                  pl.BlockSpec((nc_blk, HEADS, HEAD_W, HEAD_W), lambda i: (nb - 1 - i, 0, 0, 0)), rev, rev, rev, rev],
        out_specs=[_rows_rev(tb, U_HG, nb), _const((2, MIX_W)), _const((1, MIX_W))],
        out_shape=[jax.ShapeDtypeStruct((t, U_HG), ACT), jax.ShapeDtypeStruct((2, MIX_W), f32), jax.ShapeDtypeStruct((1, MIX_W), f32)],
        scratch_shapes=[pltpu.VMEM((HEADS, HEAD_W, HEAD_W), f32)],
        operands=(u_hg, logits, gn, sst, bcum, scores, o_raw, dog), riders=riders, copies=_scatter_copies,
        ride_shapes=[jax.ShapeDtypeStruct(r.shape, r.dtype) for r in riders])


def _conv_fwd(u_ml, w, b):
    t = u_ml.shape[0]
    tm = _tile(t, LIGHT_ROWS)

    def body(x_ref, w_ref, b_ref, pre_ref, act_ref, xbuf):
        @pl.when(pl.program_id(0) == 0)
        def _():
            xbuf[...] = jnp.zeros_like(xbuf)

        xbuf[0:HALO, :] = xbuf[tm:tm + HALO, :]
        xbuf[HALO:HALO + tm, :] = x_ref[...]
        pre = b_ref[...] + jnp.zeros((tm, MIX_W), f32)
        for kk in range(CONV_K):
            off = HALO - (CONV_K - 1) + kk
            pre = pre + w_ref[kk:kk + 1, :] * xbuf[off:off + tm, :]
        pre_ref[...] = pre
        act_ref[...] = pre * _sigmoid(pre)

    return pl.pallas_call(
        body, name="conv_fwd", grid=(t // tm,),
        in_specs=[_rows(tm, MIX_W), _const((CONV_K, MIX_W)), _const((1, MIX_W))],
        out_specs=[_rows(tm, MIX_W), _rows(tm, MIX_W)],
        out_shape=[jax.ShapeDtypeStruct((t, MIX_W), f32)] * 2,
        scratch_shapes=[pltpu.VMEM((tm + HALO, MIX_W), f32)],
        compiler_params=_cparams(1),
    )(u_ml, w, b)


def _conv_bwd(u_ml, w, pre, dact):
    t = u_ml.shape[0]
    tm = _tile(t, LIGHT_ROWS)
    nb = t // tm
    hb = tm // HALO

    def body(x_ref, halo_ref, w_ref, pre_ref, dact_ref, dx_ref, dw_ref, db_ref, dbuf, xbuf):
        i = pl.program_id(0)

        @pl.when(i == 0)
        def _():
            dbuf[...] = jnp.zeros_like(dbuf)
            dw_ref[...] = jnp.zeros_like(dw_ref)
            db_ref[...] = jnp.zeros_like(db_ref)

        p = pre_ref[...]
        sg = _sigmoid(p)
        dpre = dact_ref[...] * (sg * (1.0 + p * (1.0 - sg)))
        dbuf[tm:tm + HALO, :] = dbuf[0:HALO, :]
        dbuf[0:tm, :] = dpre
        has_prev = (i < nb - 1).astype(f32)
        xbuf[0:HALO, :] = halo_ref[...] * has_prev
        xbuf[HALO:HALO + tm, :] = x_ref[...]
        dx = jnp.zeros((tm, MIX_W), f32)
        for kk in range(CONV_K):
            back = CONV_K - 1 - kk
            dx = dx + w_ref[kk:kk + 1, :] * dbuf[back:back + tm, :]
            off = HALO - (CONV_K - 1) + kk
            dw_ref[kk:kk + 1, :] += jnp.sum(dpre * xbuf[off:off + tm, :], axis=0, keepdims=True)
        dx_ref[...] = dx.astype(ACT)
        db_ref[...] += jnp.sum(dpre, axis=0, keepdims=True)

    return pl.pallas_call(
        body, name="conv_bwd", grid=(nb,),
        in_specs=[_rows_rev(tm, MIX_W, nb),
                  pl.BlockSpec((HALO, MIX_W), lambda i: (jnp.maximum((nb - 1 - i) * hb - 1, 0), 0)),
                  _const((CONV_K, MIX_W)), _rows_rev(tm, MIX_W, nb), _rows_rev(tm, MIX_W, nb)],
        out_specs=[_rows_rev(tm, MIX_W, nb), _const((CONV_K, MIX_W)), _const((1, MIX_W))],
        out_shape=[jax.ShapeDtypeStruct((t, MIX_W), ACT), jax.ShapeDtypeStruct((CONV_K, MIX_W), f32), jax.ShapeDtypeStruct((1, MIX_W), f32)],
        scratch_shapes=[pltpu.VMEM((tm + HALO, MIX_W), f32), pltpu.VMEM((tm + HALO, MIX_W), f32)],
        compiler_params=_cparams(1),
    )(u_ml, u_ml, w, pre, dact)


def _lane_pick(x, lane):
    idx = lax.broadcasted_iota(jnp.int32, x.shape, 1)
    return jnp.sum(jnp.where(idx == lane, x, 0.0), axis=-1, keepdims=True)


def _ml_gate_forms(gates, tri):
    lf = _log_sigmoid(gates)
    gc = _sel_dot(tri, lf)
    lane = lax.broadcasted_iota(jnp.int32, gates.shape, 1)
    mixed = jnp.where(lane < HEADS, gates, gc)
    sel = (lax.broadcasted_iota(jnp.int32, (8, 128), 0) == lax.broadcasted_iota(jnp.int32, (8, 128), 1)).astype(f32)
    rowsf = _sel_dot_nt(sel, mixed)
    return gc, rowsf


def _ml_chunk(q, k, v, gates, gc, rowsf, c_st, n_st, m_st, tril_mask):
    hs = range(HEADS)
    g_col = [_lane_pick(gc, HEADS + h) for h in hs]
    ig_col = [_lane_pick(gates, h) for h in hs]
    dmat = [jnp.where(tril_mask, g_col[h] - rowsf[HEADS + h:HEADS + h + 1, :] + rowsf[h:h + 1, :], NEG) for h in hs]
    m_inter = [g_col[h] + m_st[h] for h in hs]
    m_t = [jnp.maximum(m_inter[h], jnp.max(dmat[h], axis=-1, keepdims=True)) for h in hs]
    wi = [jnp.exp(dmat[h] - m_t[h]) for h in hs]
    wo = [jnp.exp(m_inter[h] - m_t[h]) for h in hs]
    qk = [_bdot_nt(q[h], k[h]) * wi[h] for h in hs]
    num = [_bdot(qk[h], v[h]) + wo[h] * _bdot(q[h], c_st[h]) for h in hs]
    den = [_lane_sum(qk[h]) + wo[h] * _lane_dot(q[h], n_st[h]) for h in hs]
    floor = [jnp.exp(-m_t[h]) for h in hs]
    z = [jnp.maximum(jnp.abs(den[h]), floor[h]) for h in hs]
    g_last = [g_col[h][CHUNK - 1:CHUNK] for h in hs]
    a_col = [g_last[h] - g_col[h] + ig_col[h] for h in hs]
    m_new = [jnp.maximum(g_last[h] + m_st[h], jnp.max(a_col[h], axis=0, keepdims=True)) for h in hs]
    ws = [jnp.exp(a_col[h] - m_new[h]) for h in hs]
    w_old = [jnp.exp(g_last[h] + m_st[h] - m_new[h]) for h in hs]
    return dict(wi=wi, wo=wo, qk=qk, num=num, den=den, z=z, floor=floor, ws=ws, w_old=w_old, m_new=m_new)


def _mlstm_fwd(qkc, u_ml, gn, riders=()):
    t = qkc.shape[0]
    tb = _tile(t, MIXER_ROWS)
    nc_blk = tb // CHUNK

    def body(qk_ref, v_ref, mo_ref, gt_ref, gn_ref, og_ref, cst_ref, nst_ref, mst_ref, c_sc, n_sc, m_sc):
        @pl.when(pl.program_id(0) == 0)
        def _():
            c_sc[...] = jnp.zeros_like(c_sc)
            n_sc[...] = jnp.zeros_like(n_sc)
            m_sc[...] = jnp.zeros_like(m_sc)

        tril_mask = _tri(CHUNK)
        tri = tril_mask.astype(f32)

        def chunk(c, carry):
            r0 = pl.multiple_of(c * CHUNK, CHUNK)
            rows = pl.ds(r0, CHUNK)
            gates = gt_ref[rows, :]
            gc, rowsf = _ml_gate_forms(gates, tri)
            hs = range(HEADS)
            q = [qk_ref[rows, h * ML_DQK:(h + 1) * ML_DQK] * (ML_DQK ** -0.5) for h in hs]
            k = [qk_ref[rows, HEADS * ML_DQK + h * ML_DQK:HEADS * ML_DQK + (h + 1) * ML_DQK] for h in hs]
            v = [v_ref[rows, h * HEAD_W:(h + 1) * HEAD_W] for h in hs]
            c_st = [c_sc[h] for h in hs]
            n_st = [n_sc[h] for h in hs]
            m_full = [m_sc[h] for h in hs]
            r = _ml_chunk(q, k, v, gates, gc, rowsf, c_st, n_st, [m[:, 0:1] for m in m_full], tril_mask)
            ksc = [k[h] * r["ws"][h] for h in hs]
            new_c = [r["w_old"][h] * c_st[h] + _bdot_tn(ksc[h], v[h]) for h in hs]
            for h in hs:
                cs = slice(h * HEAD_W, (h + 1) * HEAD_W)
                cst_ref[c, h] = c_st[h]
                nst_ref[c, h] = n_st[h]
                mst_ref[c, h] = m_full[h]
                c_sc[h] = new_c[h]
                n_sc[h] = r["w_old"][h] * n_st[h] + jnp.sum(ksc[h], axis=0, keepdims=True)
                m_sc[h] = r["m_new"][h] + jnp.zeros((1, 128), f32)
                _, _, y = _head_rms(r["num"][h] / r["z"][h], gn_ref[:, cs], on_mxu=True)
                og_ref[rows, cs] = (y * _sigmoid(mo_ref[rows, h * HEAD_W:(h + 1) * HEAD_W])).astype(ACT)
            return carry

        lax.fori_loop(0, nc_blk, chunk, 0, unroll=True)

    nchunks = t // CHUNK
    return _riding_call(
        body, "mlstm_fwd", t // tb,
        in_specs=[_rows(tb, MIX_W), _rows(tb, MIX_W, 1), _rows(tb, MIX_W, 2), _rows(tb, 128, 12), _const((1, MIX_W))],
        out_specs=[_rows(tb, MIX_W),
                   pl.BlockSpec((nc_blk, HEADS, ML_DQK, HEAD_W), lambda i: (i, 0, 0, 0)),
                   pl.BlockSpec((nc_blk, HEADS, 1, ML_DQK), lambda i: (i, 0, 0, 0)),
                   pl.BlockSpec((nc_blk, HEADS, 1, 128), lambda i: (i, 0, 0, 0))],
        out_shape=[jax.ShapeDtypeStruct((t, MIX_W), ACT),
                   jax.ShapeDtypeStruct((nchunks, HEADS, ML_DQK, HEAD_W), f32),
                   jax.ShapeDtypeStruct((nchunks, HEADS, 1, ML_DQK), f32),
                   jax.ShapeDtypeStruct((nchunks, HEADS, 1, 128), f32)],
        scratch_shapes=[pltpu.VMEM((HEADS, ML_DQK, HEAD_W), f32), pltpu.VMEM((HEADS, 1, ML_DQK), f32), pltpu.VMEM((HEADS, 1, 128), f32)],
        operands=(qkc, u_ml, u_ml, u_ml, gn), riders=riders, copies=_gather_copies, ride_shapes=_gather_shapes(riders))


def _mlstm_bwd(qkc, u_ml, gn, cst, nst, mst, dog, riders=()):
    t = qkc.shape[0]
    tb = _tile(t, MIXER_ROWS)
    nb = t // tb
    nc_blk = tb // CHUNK

    def body(qk_ref, v_ref, mo_ref, gt_ref, gn_ref, cst_ref, nst_ref, mst_ref, dog_ref,
             dqk_ref, dv_ref, dmo_ref, dgt_ref, dgn_ref, dc_sc, dn_sc):
        @pl.when(pl.program_id(0) == 0)
        def _():
            dc_sc[...] = jnp.zeros_like(dc_sc)
            dn_sc[...] = jnp.zeros_like(dn_sc)
            dgn_ref[...] = jnp.zeros_like(dgn_ref)

        tril_mask = _tri(CHUNK)
        tri = tril_mask.astype(f32)
        triu = _tri(CHUNK, upper=True).astype(f32)
        lane = lax.broadcasted_iota(jnp.int32, (CHUNK, 128), 1)

        def chunk(j, carry):
            c = nc_blk - 1 - j
            r0 = pl.multiple_of(c * CHUNK, CHUNK)
            rows = pl.ds(r0, CHUNK)
            gates = gt_ref[rows, :]
            gc, rowsf = _ml_gate_forms(gates, tri)
            dg_mat = jnp.zeros((CHUNK, 128), f32)
            dig_mat = jnp.zeros((CHUNK, 128), f32)
            dlast_row = jnp.zeros((1, 128), f32)
            hs = range(HEADS)
            cols = [slice(h * HEAD_W, (h + 1) * HEAD_W) for h in hs]
            q = [qk_ref[rows, h * ML_DQK:(h + 1) * ML_DQK] * (ML_DQK ** -0.5) for h in hs]
            k = [qk_ref[rows, HEADS * ML_DQK + h * ML_DQK:HEADS * ML_DQK + (h + 1) * ML_DQK] for h in hs]
            v = [v_ref[rows, h * HEAD_W:(h + 1) * HEAD_W] for h in hs]
            c_st = [cst_ref[c, h] for h in hs]
            n_st = [nst_ref[c, h] for h in hs]
            m_st = [mst_ref[c, h][:, 0:1] for h in hs]
            dc = [dc_sc[h] for h in hs]
            dn = [dn_sc[h] for h in hs]
            r = _ml_chunk(q, k, v, gates, gc, rowsf, c_st, n_st, m_st, tril_mask)
            z, wi, wo, ws, w_old, den = r["z"], r["wi"], r["wo"], r["ws"], r["w_old"], r["den"]
            hh = [r["num"][h] / z[h] for h in hs]
            dh = []
            for h in hs:
                gnh = gn_ref[:, cols[h]]
                oh, rstd, y = _head_rms(hh[h], gnh, on_mxu=True)
                sg = _sigmoid(mo_ref[rows, h * HEAD_W:(h + 1) * HEAD_W])
                dogh = dog_ref[rows, cols[h]]
                dy = dogh * sg
                dmo_ref[rows, cols[h]] = (dogh * y * (sg * (1.0 - sg))).astype(ACT)
                dgn_ref[:, cols[h]] += jnp.sum(dy * oh, axis=0, keepdims=True)
                doh = dy * gnh
                dh.append(rstd * (doh - oh * (_lane_sum(doh * oh) * (1.0 / HEAD_W))))
            dnum = [dh[h] / z[h] for h in hs]
            dz = [-_lane_sum(dh[h] * hh[h]) / z[h] for h in hs]
            dden = [jnp.where(jnp.abs(den[h]) > r["floor"][h], dz[h] * jnp.sign(den[h]), 0.0) for h in hs]
            dsw = [(_bdot_nt(dnum[h], v[h]) + dden[h]) * wi[h] for h in hs]
            dq = [_bdot(dsw[h], k[h]) + wo[h] * (_bdot_nt(dnum[h], c_st[h]) + dden[h][:, :ML_DQK] * n_st[h]) for h in hs]
            dk_state = [ws[h] * (_bdot_nt(v[h], dc[h]) + dn[h]) for h in hs]
            dk = [_bdot_tn(dsw[h], q[h]) + dk_state[h] for h in hs]
            dv = [_bdot_tn(r["qk"][h], dnum[h]) + ws[h] * _bdot(k[h], dc[h]) for h in hs]
            woq = [wo[h] * q[h] for h in hs]
            new_dc = [w_old[h] * dc[h] + _bdot_tn(woq[h], dnum[h]) for h in hs]
            for h in hs:
                dv_ref[rows, cols[h]] = dv[h].astype(ACT)
                dc_sc[h] = new_dc[h]
                dn_sc[h] = w_old[h] * dn[h] + jnp.sum(woq[h] * dden[h][:, :ML_DQK], axis=0, keepdims=True)
                d_last = (jnp.sum(jnp.sum(k[h] * dk_state[h], axis=0, keepdims=True), axis=-1, keepdims=True)
                          + w_old[h] * (jnp.sum(jnp.sum(dc[h] * c_st[h], axis=0, keepdims=True), axis=-1, keepdims=True)
                                        + jnp.sum(dn[h] * n_st[h], axis=-1, keepdims=True)))
                kdk = _lane_sum(k[h] * dk[h])
                qdq = _lane_sum(q[h] * dq[h])
                dg_mat = dg_mat + jnp.where(lane == HEADS + h, qdq - kdk, 0.0)
                dlast_row = dlast_row + jnp.where(lane[0:1] == HEADS + h, d_last, 0.0)
                dig_mat = dig_mat + jnp.where(lane == h, kdk, 0.0)
                dqk_ref[rows, h * ML_DQK:(h + 1) * ML_DQK] = dq[h] * (ML_DQK ** -0.5)
                dqk_ref[rows, HEADS * ML_DQK + h * ML_DQK:HEADS * ML_DQK + (h + 1) * ML_DQK] = dk[h]
            dlf = _sel_dot(triu, dg_mat) + dlast_row
            dgt_ref[rows, :] = (dig_mat + dlf * _sigmoid(-gates)).astype(ACT)
            return carry

        lax.fori_loop(0, nc_blk, chunk, 0, unroll=True)

    st4 = lambda a, b: pl.BlockSpec((nc_blk, HEADS, a, b), lambda i: (nb - 1 - i, 0, 0, 0))
    return _riding_call(
        body, "mlstm_bwd", nb,
        in_specs=[_rows_rev(tb, MIX_W, nb), _rows_rev(tb, MIX_W, nb, 1), _rows_rev(tb, MIX_W, nb, 2), _rows_rev(tb, 128, nb, 12),
                  _const((1, MIX_W)), st4(ML_DQK, HEAD_W), st4(1, ML_DQK), st4(1, 128), _rows_rev(tb, MIX_W, nb)],
        out_specs=[_rows_rev(tb, MIX_W, nb), _rows_rev(tb, MIX_W, nb), _rows_rev(tb, MIX_W, nb), _rows_rev(tb, 128, nb), _const((1, MIX_W))],
        out_shape=[jax.ShapeDtypeStruct((t, MIX_W), f32), jax.ShapeDtypeStruct((t, MIX_W), ACT), jax.ShapeDtypeStruct((t, MIX_W), ACT),
                   jax.ShapeDtypeStruct((t, 128), ACT), jax.ShapeDtypeStruct((1, MIX_W), f32)],
        scratch_shapes=[pltpu.VMEM((HEADS, ML_DQK, HEAD_W), f32), pltpu.VMEM((HEADS, 1, ML_DQK), f32)],
        operands=(qkc, u_ml, u_ml, u_ml, gn, cst, nst, mst, dog), riders=riders, copies=_scatter_copies,
        ride_shapes=[jax.ShapeDtypeStruct(r.shape, r.dtype) for r in riders])


def _ln_fwd(r, g, b):
    mu = jnp.mean(r, axis=-1, keepdims=True)
    xc = r - mu
    rstd = lax.rsqrt(jnp.mean(xc * xc, axis=-1, keepdims=True) + LN_EPS)
    xh = xc * rstd
    return xh * g + b, xh, rstd


def _ln_bwd(dy, xh, rstd, g):
    dxh = dy * g
    return rstd * (dxh - jnp.mean(dxh, axis=-1, keepdims=True) - xh * jnp.mean(dxh * xh, axis=-1, keepdims=True))


def _outproj_ln1(og_hg, og_ml, x, w_out, g, b, riders=()):
    t = x.shape[0]
    tm = _tile(t, LIGHT_ROWS)

    def body(a_ref, b_ref, x_ref, w_ref, g_ref, bb_ref, x1_ref, xh_ref, rs_ref, x1b_ref):
        mix = _bdot(a_ref[...], w_ref[0:MIX_W, :]) + _bdot(b_ref[...], w_ref[MIX_W:2 * MIX_W, :])
        y, xh, rstd = _ln_fwd(ALPHA * x_ref[...] + mix, g_ref[...], bb_ref[...])
        x1_ref[...] = y
        x1b_ref[...] = y.astype(ACT)
        xh_ref[...] = xh.astype(ACT)
        rs_ref[...] = rstd

    return _riding_call(
        body, "outproj_ln1", t // tm,
        in_specs=[_rows(tm, MIX_W), _rows(tm, MIX_W), _rows(tm, D_MODEL), _resident((D_MODEL, D_MODEL)), _const((1, D_MODEL)), _const((1, D_MODEL))],
        out_specs=[_rows(tm, D_MODEL), _rows(tm, D_MODEL), _rows(tm, 1), _rows(tm, D_MODEL)],
        out_shape=[jax.ShapeDtypeStruct((t, D_MODEL), f32), jax.ShapeDtypeStruct((t, D_MODEL), ACT), jax.ShapeDtypeStruct((t, 1), f32),
                   jax.ShapeDtypeStruct((t, D_MODEL), ACT)],
        scratch_shapes=[], operands=(og_hg, og_ml, x, w_out, g, b), riders=riders, copies=_gather_copies, ride_shapes=_gather_shapes(riders))


def _ffn_up(x1, wg, wu, riders=()):
    t = x1.shape[0]
    tm = _tile(t, DENSE_ROWS)

    def body(x_ref, wg_ref, wu_ref, hg_ref, up_ref, a_ref):
        xv = x_ref[...]
        hg = _bdot_nt(xv, wg_ref[...])
        up = _bdot_nt(xv, wu_ref[...])
        hg_ref[...] = hg.astype(ACT)
        up_ref[...] = up.astype(ACT)
        a_ref[...] = (hg * _sigmoid(hg) * up).astype(ACT)

    return _riding_call(
        body, "ffn_up", t // tm,
        in_specs=[_rows(tm, D_MODEL), _resident((D_FF, D_MODEL)), _resident((D_FF, D_MODEL))],
        out_specs=[_rows(tm, D_FF), _rows(tm, D_FF), _rows(tm, D_FF)],
        out_shape=[jax.ShapeDtypeStruct((t, D_FF), ACT), jax.ShapeDtypeStruct((t, D_FF), ACT), jax.ShapeDtypeStruct((t, D_FF), ACT)],
        scratch_shapes=[], operands=(x1, wg, wu), riders=riders, copies=_gather_copies, ride_shapes=_gather_shapes(riders))


def _ffn_down_ln2(a, x1, wd, g, b):
    t = x1.shape[0]
    tm = _tile(t, LIGHT_ROWS)

    def body(a_ref, x_ref, w_ref, g_ref, bb_ref, x2_ref, xh_ref, rs_ref, x2b_ref):
        ffn = _bdot(a_ref[...], w_ref[...])
        y, xh, rstd = _ln_fwd(ALPHA * x_ref[...] + ffn, g_ref[...], bb_ref[...])
        x2_ref[...] = y
        x2b_ref[...] = y.astype(ACT)
        xh_ref[...] = xh.astype(ACT)
        rs_ref[...] = rstd

    return pl.pallas_call(
        body, name="ffn_down_ln2", grid=(t // tm,),
        in_specs=[_rows(tm, D_FF), _rows(tm, D_MODEL), _resident((D_FF, D_MODEL)), _const((1, D_MODEL)), _const((1, D_MODEL))],
        out_specs=[_rows(tm, D_MODEL), _rows(tm, D_MODEL), _rows(tm, 1), _rows(tm, D_MODEL)],
        out_shape=[jax.ShapeDtypeStruct((t, D_MODEL), f32), jax.ShapeDtypeStruct((t, D_MODEL), ACT), jax.ShapeDtypeStruct((t, 1), f32),
                   jax.ShapeDtypeStruct((t, D_MODEL), ACT)],
        compiler_params=_cparams(1, arbitrary=False),
    )(a, x1, wd, g, b)


def _head_loss_bwd(x2, xh2, rs2, p, tgt, w_pg, b_pg, w_pp, g2):
    t = x2.shape[0]
    tm = _tile(t, LIGHT_ROWS)

    def body(x_ref, xh_ref, rs_ref, p_ref, t_ref, wg_ref, bg_ref, wp_ref, g_ref,
             dr_ref, de_ref, dz_ref, loss_ref, dbg_ref, dg2_ref, db2_ref):
        @pl.when(pl.program_id(0) == 0)
        def _():
            loss_ref[...] = jnp.zeros_like(loss_ref)
            dbg_ref[...] = jnp.zeros_like(dbg_ref)
            dg2_ref[...] = jnp.zeros_like(dg2_ref)
            db2_ref[...] = jnp.zeros_like(db2_ref)

        x2v = x_ref[...]
        z = _bdot(x2v, wg_ref[...]) + bg_ref[...]
        e = _bdot(p_ref[...], wp_ref[...])
        sg = _sigmoid(z)
        diff = x2v + sg * e - t_ref[...]
        loss_ref[...] += 0.5 * jnp.sum(jnp.mean(diff * diff, axis=-1, keepdims=True), axis=0, keepdims=True)
        dy = diff * (1.0 / D_MODEL)
        de_ref[...] = (dy * sg).astype(ACT)
        dz = dy * e * (sg * (1.0 - sg))
        dz_ref[...] = dz.astype(ACT)
        dbg_ref[...] += jnp.sum(dz, axis=0, keepdims=True)
        dx2 = dy + _bdot_nt(dz, wg_ref[...])
        xh = xh_ref[...].astype(f32)
        dg2_ref[...] += jnp.sum(dx2 * xh, axis=0, keepdims=True)
        db2_ref[...] += jnp.sum(dx2, axis=0, keepdims=True)
        dr_ref[...] = _ln_bwd(dx2, xh, rs_ref[...], g_ref[...])

    row = jax.ShapeDtypeStruct((1, D_MODEL), f32)
    return pl.pallas_call(
        body, name="head_loss_bwd", grid=(t // tm,),
        in_specs=[_rows(tm, D_MODEL), _rows(tm, D_MODEL), _rows(tm, 1), _rows(tm, PLE), _rows(tm, D_MODEL),
                  _resident((D_MODEL, D_MODEL)), _const((1, D_MODEL)), _resident((PLE, D_MODEL)), _const((1, D_MODEL))],
        out_specs=[_rows(tm, D_MODEL), _rows(tm, D_MODEL), _rows(tm, D_MODEL), _const((1, 1)), _const((1, D_MODEL)), _const((1, D_MODEL)), _const((1, D_MODEL))],
        out_shape=[jax.ShapeDtypeStruct((t, D_MODEL), f32), jax.ShapeDtypeStruct((t, D_MODEL), ACT), jax.ShapeDtypeStruct((t, D_MODEL), ACT),
                   jax.ShapeDtypeStruct((1, 1), f32), row, row, row],
        compiler_params=_cparams(1),
    )(x2, xh2, rs2, p, tgt, w_pg, b_pg, w_pp, g2)


def _ffn_bwd(dr2, hg, up, xh1, rs1, wd, wg, wu, g1, w_out):
    t = dr2.shape[0]
    tm = _tile(t, DENSE_ROWS // 2)

    def body(dr_ref, hg_ref, up_ref, xh_ref, rs_ref, wd_ref, wg_ref, wu_ref, g_ref, wo_ref,
             dr1_ref, dhg_ref, dup_ref, dg1_ref, db1_ref, doghg_ref, dogml_ref):
        @pl.when(pl.program_id(0) == 0)
        def _():
            dg1_ref[...] = jnp.zeros_like(dg1_ref)
            db1_ref[...] = jnp.zeros_like(db1_ref)

        dr2v = dr_ref[...]
        da = _bdot_nt(dr2v, wd_ref[...])
        hgv = hg_ref[...].astype(f32)
        sg = _sigmoid(hgv)
        dhg = da * up_ref[...].astype(f32) * (sg * (1.0 + hgv * (1.0 - sg)))
        dup = da * (hgv * sg)
        dhg_ref[...] = dhg.astype(ACT)
        dup_ref[...] = dup.astype(ACT)
        dx1 = ALPHA * dr2v + _bdot(dhg, wg_ref[...]) + _bdot(dup, wu_ref[...])
        xh = xh_ref[...].astype(f32)
        dg1_ref[...] += jnp.sum(dx1 * xh, axis=0, keepdims=True)
        db1_ref[...] += jnp.sum(dx1, axis=0, keepdims=True)
        dr1 = _ln_bwd(dx1, xh, rs_ref[...], g_ref[...])
        dr1_ref[...] = dr1
        dog = _bdot_nt(dr1, wo_ref[...])
        doghg_ref[...] = dog[:, 0:MIX_W]
        dogml_ref[...] = dog[:, MIX_W:2 * MIX_W]

    row = jax.ShapeDtypeStruct((1, D_MODEL), f32)
    return pl.pallas_call(
        body, name="ffn_bwd", grid=(t // tm,),
        in_specs=[_rows(tm, D_MODEL), _rows(tm, D_FF), _rows(tm, D_FF), _rows(tm, D_MODEL), _rows(tm, 1),
                  _resident((D_FF, D_MODEL)), _resident((D_FF, D_MODEL)), _resident((D_FF, D_MODEL)), _const((1, D_MODEL)),
                  _resident((D_MODEL, D_MODEL))],
        out_specs=[_rows(tm, D_MODEL), _rows(tm, D_FF), _rows(tm, D_FF), _const((1, D_MODEL)), _const((1, D_MODEL)),
                   _rows(tm, MIX_W), _rows(tm, MIX_W)],
        out_shape=[jax.ShapeDtypeStruct((t, D_MODEL), f32), jax.ShapeDtypeStruct((t, D_FF), ACT), jax.ShapeDtypeStruct((t, D_FF), ACT), row, row,
                   jax.ShapeDtypeStruct((t, MIX_W), f32), jax.ShapeDtypeStruct((t, MIX_W), f32)],
        compiler_params=_cparams(1),
    )(dr2, hg, up, xh1, rs1, wd, wg, wu, g1, w_out)


def _inproj_bwd(dr1, du_hg, dqk, dmv, dmo, dgt, w_hg, w_ml):
    t = dr1.shape[0]
    tm = _tile(t, LIGHT_ROWS)

    def body(dr_ref, dhg_ref, dqk_ref, dmv_ref, dmo_ref, dgt_ref, whg_ref, wml_ref, gx_ref, dml_ref):
        dml = jnp.concatenate([dqk_ref[...], dmv_ref[...], dmo_ref[...], dgt_ref[...]], axis=-1).astype(ACT)
        dml_ref[...] = dml
        gx_ref[...] = ALPHA * dr_ref[...] + _bdot(dhg_ref[...], whg_ref[...]) + _bdot(dml, wml_ref[...])

    return pl.pallas_call(
        body, name="inproj_bwd", grid=(t // tm,),
        in_specs=[_rows(tm, D_MODEL), _rows(tm, U_HG), _rows(tm, MIX_W), _rows(tm, MIX_W), _rows(tm, MIX_W), _rows(tm, 128),
                  _resident((U_HG, D_MODEL)), _resident((U_ML, D_MODEL))],
        out_specs=[_rows(tm, D_MODEL), _rows(tm, U_ML)],
        out_shape=[jax.ShapeDtypeStruct((t, D_MODEL), f32), jax.ShapeDtypeStruct((t, U_ML), ACT)],
        compiler_params=_cparams(1, arbitrary=False),
    )(dr1, du_hg, dqk, dmv, dmo, dgt, w_hg, w_ml)


def _wgrad(a, b, name, tk=None, tn=None, colsum=False, low=False):
    t, kdim = a.shape
    n = b.shape[1]
    tk = tk or kdim
    tn = tn or n
    tt = _tile(t, WGRAD_ROWS)
    nt = t // tt
    assert not (colsum and low) and (not colsum or tn == n)

    def body(a_ref, b_ref, o_ref, *s_ref):
        @pl.when(pl.program_id(2) == 0)
        def _():
            o_ref[...] = jnp.zeros_like(o_ref)
            if colsum:
                s_ref[0][...] = jnp.zeros_like(s_ref[0])

        av = a_ref[...]
        o_ref[...] += _bdot_tn(av, b_ref[...])
        if colsum:
            s_ref[0][...] += jnp.sum(av.astype(f32), axis=0, keepdims=True)
        if low:
            @pl.when(pl.program_id(2) == nt - 1)
            def _():
                s_ref[0][...] = o_ref[...].astype(bf16)

    out_specs = [pl.BlockSpec((tk, tn), lambda i, j, s: (i, j))]
    out_shape = [jax.ShapeDtypeStruct((kdim, n), f32)]
    if colsum:
        out_specs.append(pl.BlockSpec((1, tk), lambda i, j, s: (0, i)))
        out_shape.append(jax.ShapeDtypeStruct((1, kdim), f32))
    if low:
        out_specs.append(pl.BlockSpec((tk, tn), lambda i, j, s: (i, j)))
        out_shape.append(jax.ShapeDtypeStruct((kdim, n), bf16))
    res = pl.pallas_call(
        body, name=name, grid=(kdim // tk, n // tn, t // tt),
        in_specs=[pl.BlockSpec((tt, tk), lambda i, j, s: (s, i)), pl.BlockSpec((tt, tn), lambda i, j, s: (s, j))],
        out_specs=out_specs, out_shape=out_shape,
        compiler_params=_cparams(3),
    )(a, b)
    return res if (colsum or low) else res[0]


_TRANSPOSED = {"w_in", "w_ffn_gate", "w_ffn_up"}
_COL_SPLIT = {"ple_w_proj"}
_SCATTER_PLAN = (("w_ffn_gate", "w_ffn_up"), ("w_ffn_down", "w_out", "ple_w_gate", "ple_w_proj"))
_RIDE_PLAN = {"inproj": ("w_ffn_gate",), "hgrn2_fwd": ("w_ffn_up",), "mlstm_fwd": ("w_out",),
              "outproj_ln1": ("ple_w_gate", "ple_w_proj"), "ffn_up": ("w_ffn_down",)}


def _from_chip_major(a, col_split):
    if col_split:
        return a.transpose(1, 0, 2).reshape(a.shape[1], 4 * a.shape[2])
    return a.reshape(4 * a.shape[1], a.shape[2])


def _local_step(x, p, tgt, w_in_b, b_in, logits, conv_w, conv_b, hg_gn, ml_gn, w_out_b, ln1_g, ln1_b,
                wg_b, wu_b, wd_b, ln2_g, ln2_b, w_pp_b, w_pg_b, b_pg, early_hook=None, late_shards=None):
    pad_w = U_HG + U_ML - PROJ_W
    w_hg = w_in_b[:U_HG]
    w_ml = jnp.pad(w_in_b[U_HG:], ((0, pad_w), (0, 0)))
    bb_hg = b_in[:, :U_HG]
    bb_ml = jnp.pad(b_in[:, U_HG:], ((0, 0), (0, pad_w)))

    late = dict(w_out=w_out_b, w_ffn_gate=wg_b, w_ffn_up=wu_b, w_ffn_down=wd_b, ple_w_proj=w_pp_b, ple_w_gate=w_pg_b)

    def riders_of(call):
        return [late_shards[k] for k in _RIDE_PLAN[call]] if late_shards is not None else ()

    def arrived(call, got):
        for k, g in zip(_RIDE_PLAN[call], got):
            late[k] = _from_chip_major(g, k in _COL_SPLIT)

    (u_hg, u_ml, xb), got = _inproj(x, w_hg, w_ml, bb_hg, bb_ml, riders_of("inproj"))
    arrived("inproj", got)
    (og_hg, sst, hg_b, hg_a, hg_o), got = _hgrn2_fwd(u_hg, logits, hg_gn, riders_of("hgrn2_fwd"))
    arrived("hgrn2_fwd", got)
    pre, qkc = _conv_fwd(u_ml, conv_w, conv_b)
    (og_ml, cst, nst, mst), got = _mlstm_fwd(qkc, u_ml, ml_gn, riders_of("mlstm_fwd"))
    arrived("mlstm_fwd", got)
    (x1, xh1, rs1, x1b), got = _outproj_ln1(og_hg, og_ml, x, late["w_out"], ln1_g, ln1_b, riders_of("outproj_ln1"))
    arrived("outproj_ln1", got)
    (hgp, up, act), got = _ffn_up(x1b, late["w_ffn_gate"], late["w_ffn_up"], riders_of("ffn_up"))
    arrived("ffn_up", got)
    w_out_b, wg_b, wu_b, wd_b = late["w_out"], late["w_ffn_gate"], late["w_ffn_up"], late["w_ffn_down"]
    w_pp_b, w_pg_b = late["ple_w_proj"], late["ple_w_gate"]
    x2, xh2, rs2, x2b = _ffn_down_ln2(act, x1, wd_b, ln2_g, ln2_b)
    dr2, de, dz, loss, d_bpg, d_ln2g, d_ln2b = _head_loss_bwd(x2, xh2, rs2, p, tgt, w_pg_b, b_pg, w_pp_b, ln2_g)
    dr1, dhg, dup, d_ln1g, d_ln1b, dog_hg, dog_ml = _ffn_bwd(dr2, hgp, up, xh1, rs1, wd_b, wg_b, wu_b, ln1_g, w_out_b)

    d_wo_a, lo_wo_a = _wgrad(og_hg, dr1, "wgrad_out_hg", low=True)
    d_wo_b, lo_wo_b = _wgrad(og_ml, dr1, "wgrad_out_ml", low=True)
    d_wg, lo_wg = _wgrad(dhg, x1b, "wgrad_ffn_gate", tk=D_FF // 2, low=True)
    d_wu, lo_wu = _wgrad(dup, x1b, "wgrad_ffn_up", tk=D_FF // 2, low=True)
    d_wd, lo_wd = _wgrad(act, dr2, "wgrad_ffn_down", tk=D_FF // 2, low=True)
    d_wpp, lo_wpp = _wgrad(p, de, "wgrad_ple_proj", low=True)
    d_wpg, lo_wpg = _wgrad(x2b, dz, "wgrad_ple_gate", low=True)
    early = dict(w_out=jnp.concatenate([d_wo_a, d_wo_b], axis=0), w_ffn_gate=d_wg, w_ffn_up=d_wu, w_ffn_down=d_wd,
                 ple_w_proj=d_wpp, ple_w_gate=d_wpg)
    early_low = dict(w_out=jnp.concatenate([lo_wo_a, lo_wo_b], axis=0), w_ffn_gate=lo_wg, w_ffn_up=lo_wu, w_ffn_down=lo_wd,
                     ple_w_proj=lo_wpp, ple_w_gate=lo_wpg)
    ride_hg, ride_ml = early_hook(early_low) if early_hook is not None else ((), ())

    (du_hg, d_logits, d_hg_gn), got_hg = _hgrn2_bwd(u_hg, logits, hg_gn, sst, hg_b, hg_a, hg_o, dog_hg, ride_hg)
    (dqkc, dmv, dmo, dgt, d_ml_gn), got_ml = _mlstm_bwd(qkc, u_ml, ml_gn, cst, nst, mst, dog_ml, ride_ml)
    dqk, d_conv_w, d_conv_b = _conv_bwd(u_ml, conv_w, pre, dqkc)
    grad_x, du_ml = _inproj_bwd(dr1, du_hg, dqk, dmv, dmo, dgt, w_hg, w_ml)

    dw_hg, db_hg = _wgrad(du_hg, xb, "wgrad_in_hg", tk=U_HG // 2, colsum=True)
    dw_ml, db_ml = _wgrad(du_ml, xb, "wgrad_in_ml", colsum=True)
    d_w_in = jnp.concatenate([dw_hg, dw_ml[:PROJ_W - U_HG]], axis=0)
    d_b_in = jnp.concatenate([db_hg, db_ml[:, :PROJ_W - U_HG]], axis=1)

    grads = dict(w_in=d_w_in, b_in=d_b_in, hg_lb_logits=d_logits, ml_conv_w=d_conv_w, ml_conv_b=d_conv_b,
                 hg_norm_g=d_hg_gn, ml_norm_g=d_ml_gn, ln1_g=d_ln1g, ln1_b=d_ln1b, ln2_g=d_ln2g, ln2_b=d_ln2b,
                 ple_b_gate=d_bpg, **early)
    return loss, grad_x, grads, (list(got_hg), list(got_ml))


_ANY = pl.BlockSpec(memory_space=pltpu.HBM)
_MESH = pl.DeviceIdType.MESH


def _my_place():
    return lax.axis_index("x"), lax.axis_index("y"), lax.axis_index("c")


def _other_chips(x, y):
    return [(1 - x, y), (x, 1 - y), (1 - x, 1 - y)]


_VMEM = pl.BlockSpec(memory_space=pltpu.VMEM)
_EX_ROWS = 32


def _pair_reduce_cols(p, name):
    s, r, c = p.shape
    hc = c // 2

    def body(p_ref, o_ref, other, send_sem, recv_sem):
        x, y, cc = _my_place()

        def run(mine_lo, theirs_lo):
            cp = pltpu.make_async_remote_copy(src_ref=p_ref.at[pl.ds(0, s), pl.ds(0, r), pl.ds(theirs_lo, hc)], dst_ref=other,
                                              send_sem=send_sem, recv_sem=recv_sem, device_id=(x, y, 1 - cc), device_id_type=_MESH)
            cp.start()
            cp.wait()
            for slot in range(s):
                o_ref[slot] = (p_ref[slot, :, mine_lo:mine_lo + hc] + other[slot]).astype(bf16)

        @pl.when(cc == 0)
        def _():
            run(0, hc)

        @pl.when(cc == 1)
        def _():
            run(hc, 0)

    return pl.pallas_call(
        body, name=name, in_specs=[_VMEM], out_specs=_VMEM,
        out_shape=jax.ShapeDtypeStruct((s, r, hc), bf16),
        scratch_shapes=[pltpu.VMEM((s, r, hc), f32), pltpu.SemaphoreType.DMA, pltpu.SemaphoreType.DMA],
        compiler_params=pltpu.CompilerParams(vmem_limit_bytes=VMEM_LIMIT),
    )(p)


def _chip_reduce_swap_cols(rcv, name):
    s, r, hc = rcv.shape

    def body(r_ref, g_ref, send_sem, recv_sem):
        x, y, cc = _my_place()
        acc = r_ref[0].astype(f32)
        for slot in range(1, s):
            acc = acc + r_ref[slot].astype(f32)
        g_ref[cc] = acc
        cp = pltpu.make_async_remote_copy(src_ref=g_ref.at[cc], dst_ref=g_ref.at[cc], send_sem=send_sem, recv_sem=recv_sem,
                                          device_id=(x, y, 1 - cc), device_id_type=_MESH)
        cp.start()
        cp.wait()

    both = pl.pallas_call(
        body, name=name, in_specs=[_VMEM], out_specs=_VMEM,
        out_shape=jax.ShapeDtypeStruct((2, r, hc), f32),
        scratch_shapes=[pltpu.SemaphoreType.DMA, pltpu.SemaphoreType.DMA],
        compiler_params=pltpu.CompilerParams(vmem_limit_bytes=VMEM_LIMIT),
    )(rcv)
    return both.transpose(1, 0, 2).reshape(r, 2 * hc)


def _reduce_adamw(rcv, w, m, v, name):
    s, r, c = rcv.shape
    rows_per = _EX_ROWS
    half = r // 2
    steps = half // rows_per

    def body(r_ref, w_ref, m_ref, v_ref, g_ref, d_ref, nm_ref, nv_ref, mine, theirs, send_sems, recv_sems):
        x, y, cc = _my_place()

        def swap(k):
            rs = pl.ds(k * half, half)
            return pltpu.make_async_remote_copy(src_ref=mine.at[rs], dst_ref=theirs.at[rs], send_sem=send_sems.at[k], recv_sem=recv_sems.at[k],
                                                device_id=(x, y, 1 - cc), device_id_type=_MESH)

        def chip_sum(i, carry):
            rs = pl.ds(pl.multiple_of(i * rows_per, rows_per), rows_per)
            acc = r_ref[0, rs, :].astype(f32)
            for slot in range(1, s):
                acc = acc + r_ref[slot, rs, :].astype(f32)
            mine[rs, :] = acc
            return carry

        def update(i, carry):
            rs = pl.ds(pl.multiple_of(i * rows_per, rows_per), rows_per)
            g = mine[rs, :] + theirs[rs, :]
            nm = B1 * m_ref[rs, :] + (1.0 - B1) * g
            nv = B2 * v_ref[rs, :] + (1.0 - B2) * (g * g)
            g_ref[rs, :] = g
            nm_ref[rs, :] = nm
            nv_ref[rs, :] = nv
            d_ref[rs, :] = -LR * ((nm / (1.0 - B1 ** STEP)) / (jnp.sqrt(nv / (1.0 - B2 ** STEP)) + EPS_ADAM) + WD * w_ref[rs, :])
            return carry

        lax.fori_loop(0, steps, chip_sum, 0)
        swap(0).start()
        lax.fori_loop(steps, 2 * steps, chip_sum, 0)
        swap(1).start()
        swap(0).wait()
        lax.fori_loop(0, steps, update, 0)
        swap(1).wait()
        lax.fori_loop(steps, 2 * steps, update, 0)

    return pl.pallas_call(
        body, name=name, in_specs=[_VMEM] * 4, out_specs=[_VMEM] * 4,
        out_shape=[jax.ShapeDtypeStruct((r, c), f32)] * 4,
        scratch_shapes=[pltpu.VMEM((r, c), f32), pltpu.VMEM((r, c), f32), pltpu.SemaphoreType.DMA((2,)), pltpu.SemaphoreType.DMA((2,))],
        compiler_params=pltpu.CompilerParams(vmem_limit_bytes=VMEM_LIMIT),
    )(rcv, w, m, v)


def _gather_copies(ins, outs, send_sems, recv_sems, local_sems):
    x, y, c = _my_place()
    me = 2 * x + y
    local, outgoing, incoming = [], [], []
    for a in range(len(ins)):
        local.append(pltpu.make_async_copy(ins[a], outs[a].at[me], local_sems.at[a]))
        for j, (px, py) in enumerate(_other_chips(x, y)):
            sems = dict(send_sem=send_sems.at[3 * a + j], recv_sem=recv_sems.at[3 * a + j], device_id=(px, py, c), device_id_type=_MESH)
            outgoing.append(pltpu.make_async_remote_copy(src_ref=ins[a], dst_ref=outs[a].at[me], **sems))
            incoming.append(pltpu.make_async_remote_copy(src_ref=ins[a], dst_ref=outs[a].at[2 * px + py], **sems))
    return local, outgoing, incoming


def _gather_first(block, taps, name):
    r, c = block.shape
    hc = c // 2

    def body(in_ref, tap_in, out_ref, tap_out, send_sems, recv_sems):
        x, y, cc = _my_place()
        me = 2 * x + y
        sibling = (x, y, 1 - cc)
        chips = _other_chips(x, y)
        out_ref[me] = in_ref[...]
        tap_out[me] = tap_in[...]

        def run(mine, theirs):
            def ici(j, chip):
                px, py = chips[j]
                src = in_ref.at[pl.ds(0, r), pl.ds(mine, hc)] if chip is None else out_ref.at[chip, pl.ds(0, r), pl.ds(mine, hc)]
                dst = out_ref.at[me if chip is None else chip, pl.ds(0, r), pl.ds(mine, hc)]
                return pltpu.make_async_remote_copy(src_ref=src, dst_ref=dst, send_sem=send_sems.at[j], recv_sem=recv_sems.at[j],
                                                    device_id=(px, py, cc), device_id_type=_MESH)

            def d2d(j, lo):
                px, py = chips[j]
                blk = out_ref.at[2 * px + py, pl.ds(0, r), pl.ds(lo, hc)]
                return pltpu.make_async_remote_copy(src_ref=blk, dst_ref=blk, send_sem=send_sems.at[3 + j], recv_sem=recv_sems.at[3 + j],
                                                    device_id=sibling, device_id_type=_MESH)

            def tap(j, chip):
                px, py = chips[j]
                return pltpu.make_async_remote_copy(src_ref=tap_in, dst_ref=tap_out.at[me if chip is None else chip],
                                                    send_sem=send_sems.at[6 + j], recv_sem=recv_sems.at[6 + j],
                                                    device_id=(px, py, cc), device_id_type=_MESH)

            for j in range(3):
                ici(j, None).start()
                tap(j, None).start()
            for j, (px, py) in enumerate(chips):
                ici(j, 2 * px + py).wait_recv()
                d2d(j, mine).start()
            for j, (px, py) in enumerate(chips):
                d2d(j, theirs).wait_recv()
                tap(j, 2 * px + py).wait_recv()
            for j in range(3):
                ici(j, None).wait_send()
                d2d(j, mine).wait_send()
                tap(j, None).wait_send()

        @pl.when(cc == 0)
        def _():
            run(0, hc)

        @pl.when(cc == 1)
        def _():
            run(hc, 0)

    return pl.pallas_call(
        body, name=name, in_specs=[_VMEM, _VMEM], out_specs=[_VMEM, _VMEM],
        out_shape=[jax.ShapeDtypeStruct((4, r, c), block.dtype), jax.ShapeDtypeStruct((4,) + taps.shape, taps.dtype)],
        scratch_shapes=[pltpu.SemaphoreType.DMA((9,)), pltpu.SemaphoreType.DMA((9,))],
        compiler_params=pltpu.CompilerParams(vmem_limit_bytes=VMEM_LIMIT),
    )(block, taps)


def _riding_call(body, name, nsteps, in_specs, out_specs, out_shape, scratch_shapes, operands, riders, copies, ride_shapes):
    nr, n_in, n_out, n_scr = len(riders), len(in_specs), len(out_specs), len(scratch_shapes)

    def wrapped(*refs):
        ins, ride_in = refs[:n_in], refs[n_in:n_in + nr]
        outs, ride_out = refs[n_in + nr:n_in + nr + n_out], refs[n_in + nr + n_out:n_in + 2 * nr + n_out]
        scratch, sems = refs[n_in + 2 * nr + n_out:n_in + 2 * nr + n_out + n_scr], refs[n_in + 2 * nr + n_out + n_scr:]
        if nr:
            @pl.when(pl.program_id(0) == 0)
            def _():
                local, outgoing, _ = copies(ride_in, ride_out, *sems)
                for cp in local + outgoing:
                    cp.start()

        body(*ins, *outs, *scratch)
        if nr:
            @pl.when(pl.program_id(0) == nsteps - 1)
            def _():
                local, outgoing, incoming = copies(ride_in, ride_out, *sems)
                for cp in incoming:
                    cp.wait_recv()
                for cp in outgoing:
                    cp.wait_send()
                for cp in local:
                    cp.wait()

    hbm = pl.BlockSpec(memory_space=pltpu.HBM)
    sems = [pltpu.SemaphoreType.DMA((3 * nr,)), pltpu.SemaphoreType.DMA((3 * nr,)), pltpu.SemaphoreType.DMA((nr,))] if nr else []
    res = pl.pallas_call(
        wrapped, name=name, grid=(nsteps,),
        in_specs=list(in_specs) + [hbm] * nr, out_specs=list(out_specs) + [hbm] * nr,
        out_shape=list(out_shape) + list(ride_shapes),
        scratch_shapes=list(scratch_shapes) + sems,
        compiler_params=_cparams(1),
    )(*operands, *riders)
    return list(res[:n_out]), list(res[n_out:])


def _gather_shapes(riders):
    return [jax.ShapeDtypeStruct((4,) + r.shape, r.dtype) for r in riders]


def _scatter_copies(ins, outs, send_sems, recv_sems, local_sems):
    x, y, c = _my_place()
    me = 2 * x + y
    local, outgoing, incoming = [], [], []
    for a in range(len(ins)):
        local.append(pltpu.make_async_copy(ins[a].at[me], outs[a].at[me], local_sems.at[a]))
        for j, (px, py) in enumerate(_other_chips(x, y)):
            sems = dict(send_sem=send_sems.at[3 * a + j], recv_sem=recv_sems.at[3 * a + j], device_id=(px, py, c), device_id_type=_MESH)
            outgoing.append(pltpu.make_async_remote_copy(src_ref=ins[a].at[2 * px + py], dst_ref=outs[a].at[me], **sems))
            incoming.append(pltpu.make_async_remote_copy(src_ref=ins[a].at[2 * px + py], dst_ref=outs[a].at[2 * px + py], **sems))
    return local, outgoing, incoming


def _scatter_chips(pieces, name):
    n = len(pieces)

    def body(*refs):
        local, outgoing, incoming = _scatter_copies(refs[:n], refs[n:2 * n], *refs[2 * n:])
        for cp in local + outgoing:
            cp.start()
        for cp in incoming:
            cp.wait_recv()
        for cp in outgoing:
            cp.wait_send()
        for cp in local:
            cp.wait()

    return pl.pallas_call(
        body, name=name,
        in_specs=[_ANY] * n, out_specs=[_ANY] * n,
        out_shape=[jax.ShapeDtypeStruct(s.shape, s.dtype) for s in pieces],
        scratch_shapes=[pltpu.SemaphoreType.DMA((3 * n,)), pltpu.SemaphoreType.DMA((3 * n,)), pltpu.SemaphoreType.DMA((n,))],
    )(*pieces)


def _gather_all(block, name):
    def body(in_ref, out_ref, send_sems, recv_sems, local_sem):
        x, y, c = _my_place()
        me = 4 * x + 2 * y + c
        cp = pltpu.make_async_copy(in_ref, out_ref.at[me], local_sem)
        cp.start()
        peers = []
        for dx in range(2):
            for dy in range(2):
                for dc in range(2):
                    if dx or dy or dc:
                        peers.append((1 - x if dx else x, 1 - y if dy else y, 1 - c if dc else c))
        for j, pr in enumerate(peers):
            pltpu.make_async_remote_copy(src_ref=in_ref, dst_ref=out_ref.at[me], send_sem=send_sems.at[j], recv_sem=recv_sems.at[j],
                                         device_id=pr, device_id_type=_MESH).start()
        for j, (px, py, pc) in enumerate(peers):
            pltpu.make_async_remote_copy(src_ref=in_ref, dst_ref=out_ref.at[4 * px + 2 * py + pc], send_sem=send_sems.at[j], recv_sem=recv_sems.at[j],
                                         device_id=(px, py, pc), device_id_type=_MESH).wait()
        cp.wait()

    return pl.pallas_call(
        body, name=name,
        in_specs=[_ANY], out_specs=_ANY,
        out_shape=jax.ShapeDtypeStruct((8,) + block.shape, block.dtype),
        scratch_shapes=[pltpu.SemaphoreType.DMA((7,)), pltpu.SemaphoreType.DMA((7,)), pltpu.SemaphoreType.DMA],
    )(block)


def _row_tile(r, c):
    best = r
    for cand in range(16, r + 1, 16):
        if r % cand == 0 and cand * c * 4 <= (1 << 20):
            best = cand
    return best if best * c * 4 <= (4 << 20) else r


def _sum_slots(parts, name):
    n, r, c = parts.shape
    tr = _row_tile(r, c)

    def body(p_ref, o_ref):
        acc = p_ref[0].astype(f32)
        for s in range(1, n):
            acc = acc + p_ref[s].astype(f32)
        o_ref[...] = acc

    return pl.pallas_call(
        body, name=name, grid=(r // tr,),
        in_specs=[pl.BlockSpec((n, tr, c), lambda i: (0, i, 0))],
        out_specs=pl.BlockSpec((tr, c), lambda i: (i, 0)),
        out_shape=jax.ShapeDtypeStruct((r, c), f32),
        compiler_params=_cparams(1, arbitrary=False),
    )(parts)


def _adamw(parts, w, m, v, name):
    n, r, c = parts.shape
    tr = _row_tile(r, c)
    tc = c
    if tr == r and r * c * 4 > (1 << 20) and c % 256 == 0:
        tc = 256

    def body(p_ref, w_ref, m_ref, v_ref, g_ref, d_ref, nm_ref, nv_ref):
        g = p_ref[0]
        for s in range(1, n):
            g = g + p_ref[s]
        nm = B1 * m_ref[...] + (1.0 - B1) * g
        nv = B2 * v_ref[...] + (1.0 - B2) * (g * g)
        m_hat = nm / (1.0 - B1 ** STEP)
        v_hat = nv / (1.0 - B2 ** STEP)
        g_ref[...] = g
        nm_ref[...] = nm
        nv_ref[...] = nv
        d_ref[...] = -LR * (m_hat / (jnp.sqrt(v_hat) + EPS_ADAM) + WD * w_ref[...])

    blk = pl.BlockSpec((tr, tc), lambda i, j: (i, j))
    return pl.pallas_call(
        body, name=name, grid=(r // tr, c // tc),
        in_specs=[pl.BlockSpec((n, tr, tc), lambda i, j: (0, i, j)), blk, blk, blk],
        out_specs=[blk] * 4,
        out_shape=[jax.ShapeDtypeStruct((r, c), f32)] * 4,
        compiler_params=_cparams(2, arbitrary=False),
    )(parts, w, m, v)


_BIG = ["w_in", "w_out", "w_ffn_gate", "w_ffn_up", "w_ffn_down", "ple_w_proj", "ple_w_gate"]
_SMALL = ["b_in", "hg_lb_logits", "ml_conv_w", "ml_conv_b", "hg_norm_g", "ml_norm_g", "ln1_g", "ln1_b", "ln2_g", "ln2_b", "ple_b_gate"]
_ORDER = ["w_in", "b_in", "hg_lb_logits", "ml_conv_w", "ml_conv_b", "hg_norm_g", "ml_norm_g", "w_out", "ln1_g", "ln1_b",
          "w_ffn_gate", "w_ffn_up", "w_ffn_down", "ln2_g", "ln2_b", "ple_w_proj", "ple_w_gate", "ple_b_gate"]
_PACK_ROWS, _PACK_COLS = 16, 1024


def _pack(arrays):
    flat = jnp.concatenate([a.reshape(-1) for a in arrays])
    return jnp.pad(flat, (0, _PACK_ROWS * _PACK_COLS - flat.shape[0])).reshape(_PACK_ROWS, _PACK_COLS)


def _unpack(pack, shapes):
    flat = pack.reshape(-1)
    out, off = [], 0
    for s in shapes:
        size = 1
        for d in s:
            size *= d
        out.append(flat[off:off + size].reshape(s))
        off += size
    return out


def _to_chip_major(g, col_split):
    if col_split:
        k, n = g.shape
        return g.reshape(k, 4, n // 4).transpose(1, 0, 2)
    k, n = g.shape
    return g.reshape(4, k // 4, n)


def kernel(x, p, w_in, b_in, hg_lb_logits, ml_conv_w, ml_conv_b, hg_norm_g, ml_norm_g, w_out, ln1_g, ln1_b, w_ffn_gate, w_ffn_up, w_ffn_down, ln2_g, ln2_b, ple_w_proj, ple_w_gate, ple_b_gate, loss_target, m_w_in, m_b_in, m_hg_lb_logits, m_ml_conv_w, m_ml_conv_b, m_hg_norm_g, m_ml_norm_g, m_w_out, m_ln1_g, m_ln1_b, m_w_ffn_gate, m_w_ffn_up, m_w_ffn_down, m_ln2_g, m_ln2_b, m_ple_w_proj, m_ple_w_gate, m_ple_b_gate, v_w_in, v_b_in, v_hg_lb_logits, v_ml_conv_w, v_ml_conv_b, v_hg_norm_g, v_ml_norm_g, v_w_out, v_ln1_g, v_ln1_b, v_w_ffn_gate, v_w_ffn_up, v_w_ffn_down, v_ln2_g, v_ln2_b, v_ple_w_proj, v_ple_w_gate, v_ple_b_gate):
    args = dict(locals())
    wts = {k: args[k] for k in _ORDER}
    mom = {k: args["m_" + k] for k in _ORDER}
    var = {k: args["v_" + k] for k in _ORDER}
    two_d = lambda a: a.reshape(a.shape[-2], a.shape[-1])
    block = lambda k, a: jnp.swapaxes(two_d(a), 0, 1) if k in _TRANSPOSED else two_d(a)
    unblock = lambda k, a: (jnp.swapaxes(a, 0, 1) if k in _TRANSPOSED else a).reshape(wts[k].shape)

    shards = {k: block(k, wts[k]).astype(bf16) for k in _BIG}
    w_in_blocks, taps = _gather_first(shards["w_in"], two_d(ml_conv_w), "gather_w_in")
    w_in_full = _from_chip_major(w_in_blocks, False)
    conv_w_full = _from_chip_major(taps, True)

    early_keys = _BIG[1:]
    loss, grad_x, grads, (got_hg, got_ml) = _local_step(
        x[0], p[0, 0], loss_target[0], w_in_full, b_in, hg_lb_logits, conv_w_full, ml_conv_b, hg_norm_g, ml_norm_g,
        None, ln1_g, ln1_b, None, None, None, ln2_g, ln2_b, None, None, ple_b_gate,
        early_hook=lambda low: tuple([_to_chip_major(low[k], k in _COL_SPLIT) for k in names] for names in _SCATTER_PLAN),
        late_shards={k: shards[k] for k in early_keys})

    out_g, out_d, out_m, out_v = {}, {}, {}, {}

    def finish(k, g, d, nm, nv):
        out_g[k], out_d[k], out_m[k], out_v[k] = unblock(k, g), unblock(k, d), unblock(k, nm), unblock(k, nv)

    for names, got in zip(_SCATTER_PLAN, (got_hg, got_ml)):
        for k, rcv in zip(names, got):
            finish(k, *_reduce_adamw(rcv, block(k, wts[k]), block(k, mom[k]), block(k, var[k]), "reduce_adamw_" + k))

    core_sums = _pair_reduce_cols(_to_chip_major(grads["w_in"], False), "pair_reduce_w_in")
    whole = _chip_reduce_swap_cols(_scatter_chips([core_sums], "scatter_grad_w_in")[0], "chip_reduce_w_in")
    finish("w_in", *_adamw(whole[None], block("w_in", wts["w_in"]), block("w_in", mom["w_in"]), block("w_in", var["w_in"]), "adamw_w_in"))

    small_shapes = [(1, PROJ_W), (2, MIX_W), (CONV_K, MIX_W)] + [(1, MIX_W)] * 3 + [(1, D_MODEL)] * 5 + [(1, 1)]
    contrib = _pack([grads[k] for k in _SMALL] + [loss])
    summed = _sum_slots(_gather_all(contrib, "gather_small"), "sum_small")
    small = _unpack(summed, small_shapes)
    loss_total = small[-1].reshape(())
    gsm = dict(zip(_SMALL, small[:-1]))
    place = 2 * lax.axis_index("x") + lax.axis_index("y")
    conv_cols = ml_conv_w.shape[-1]
    gsm["ml_conv_w"] = lax.dynamic_slice(gsm["ml_conv_w"], (0, place * conv_cols), (CONV_K, conv_cols))
    own_shapes = [wts[k].shape for k in _SMALL]
    g_pack = _pack([gsm[k] for k in _SMALL])
    res = _adamw(g_pack[None], _pack([wts[k] for k in _SMALL]), _pack([mom[k] for k in _SMALL]), _pack([var[k] for k in _SMALL]), "adamw_small")
    for dst, pack in zip((out_g, out_d, out_m, out_v), res):
        for k, a in zip(_SMALL, _unpack(pack, own_shapes)):
            dst[k] = a

    outs = [loss_total, grad_x[None]]
    for group in (out_g, out_d, out_m, out_v):
        outs += [group[k] for k in _ORDER]
    return tuple(outs)
```

```python
import jax
import jax.numpy as jnp
from jax import lax
from jax.experimental import pallas as pl
from jax.experimental.pallas import tpu as pltpu

f32 = jnp.float32
bf16 = jnp.bfloat16

D_MODEL = 1024
HEADS = 4
HEAD_W = 128
MIX_W = HEADS * HEAD_W
ML_DQK = 64
PROJ_W = 3592
U_HG = 4 * MIX_W
U_ML = 3 * MIX_W + 128
D_FF = 2816
PLE = 256
CHUNK = 128
SUB = 16
EXP_CAP = 80.0
CONV_K = 4
HALO = 8
ALPHA = float(2.0 ** 0.25)
LN_EPS = 1e-5
RMS_EPS = 1e-6
NEG = -1e30
LR, B1, B2, EPS_ADAM, WD, STEP = 0.001, 0.9, 0.999, 1e-08, 0.01, 10
VMEM_LIMIT = 56 * 1024 * 1024
MIXER_ROWS = 512
DENSE_ROWS = 512
LIGHT_ROWS = 1024
WGRAD_ROWS = 2048


def _cparams(n_axes, arbitrary=True):
    sem = ("arbitrary",) * n_axes if arbitrary else ("parallel",) * n_axes
    return pltpu.CompilerParams(dimension_semantics=sem, vmem_limit_bytes=VMEM_LIMIT)


ACT = bf16


def _mx(a):
    return a.astype(ACT)


def _bdot(a, b):
    return jnp.dot(_mx(a), _mx(b), preferred_element_type=f32)


def _bdot_nt(a, b):
    return lax.dot_general(_mx(a), _mx(b), (((1,), (1,)), ((), ())), preferred_element_type=f32)


def _bdot_tn(a, b):
    return lax.dot_general(_mx(a), _mx(b), (((0,), (0,)), ((), ())), preferred_element_type=f32)


def _split3(x):
    hi = x.astype(bf16)
    r1 = x - hi.astype(f32)
    mid = r1.astype(bf16)
    lo = (r1 - mid.astype(f32)).astype(bf16)
    return hi, mid, lo


def _dot3(a, b, dims):
    a_hi = a.astype(bf16)
    a_lo = (a - a_hi.astype(f32)).astype(bf16)
    b_hi = b.astype(bf16)
    b_lo = (b - b_hi.astype(f32)).astype(bf16)
    dn = (dims, ((), ()))
    return (lax.dot_general(a_hi, b_hi, dn, preferred_element_type=f32) + lax.dot_general(a_hi, b_lo, dn, preferred_element_type=f32)
            + lax.dot_general(a_lo, b_hi, dn, preferred_element_type=f32))


def _lane_sum(x):
    hi = x.astype(bf16)
    lo = (x - hi.astype(f32)).astype(bf16)
    ones = jnp.ones((x.shape[1], 128), bf16)
    return jnp.dot(hi, ones, preferred_element_type=f32) + jnp.dot(lo, ones, preferred_element_type=f32)


def _lane_dot(x, row):
    return _dot3(x, jnp.broadcast_to(row, (128, row.shape[1])), ((1,), (1,)))


def _sel_dot(sel, x):
    sb = sel.astype(bf16)
    return sum(jnp.dot(sb, part, preferred_element_type=f32) for part in _split3(x))


def _sel_dot_nt(sel, x):
    sb = sel.astype(bf16)
    return sum(lax.dot_general(sb, part, (((1,), (1,)), ((), ())), preferred_element_type=f32) for part in _split3(x))


def _sigmoid(x):
    return 1.0 / (1.0 + jnp.exp(-x))


def _log_sigmoid(x):
    return jnp.minimum(x, 0.0) - jnp.log(1.0 + jnp.exp(-jnp.abs(x)))


def _tri(n, upper=False):
    r = lax.broadcasted_iota(jnp.int32, (n, n), 0)
    c = lax.broadcasted_iota(jnp.int32, (n, n), 1)
    return (c >= r) if upper else (c <= r)


def _rows(tm, n, col=0):
    return pl.BlockSpec((tm, n), lambda i, _c=col: (i, _c))


def _rows_rev(tm, n, nb, col=0):
    return pl.BlockSpec((tm, n), lambda i, _c=col, _nb=nb: (_nb - 1 - i, _c))


def _const(shape):
    return pl.BlockSpec(shape, lambda i, _n=len(shape): (0,) * _n)


def _resident(shape):
    return pl.BlockSpec(shape, lambda i, _n=len(shape): (0,) * _n, pipeline_mode=pl.Buffered(1))


def _tile(t, want):
    return want if t % want == 0 else t


def _inproj(x, w_hg, w_ml, b_hg, b_ml, riders=()):
    t = x.shape[0]
    tm = _tile(t, DENSE_ROWS)

    def body(x_ref, whg_ref, wml_ref, bhg_ref, bml_ref, uhg_ref, uml_ref, xb_ref):
        xb = _mx(x_ref[...])
        xb_ref[...] = xb
        uhg_ref[...] = _bdot_nt(xb, whg_ref[...]) + bhg_ref[...]
        uml_ref[...] = _bdot_nt(xb, wml_ref[...]) + bml_ref[...]

    return _riding_call(
        body, "inproj", t // tm,
        in_specs=[_rows(tm, D_MODEL), _resident((U_HG, D_MODEL)), _resident((U_ML, D_MODEL)), _const((1, U_HG)), _const((1, U_ML))],
        out_specs=[_rows(tm, U_HG), _rows(tm, U_ML), _rows(tm, D_MODEL)],
        out_shape=[jax.ShapeDtypeStruct((t, U_HG), f32), jax.ShapeDtypeStruct((t, U_ML), f32), jax.ShapeDtypeStruct((t, D_MODEL), ACT)],
        scratch_shapes=[], operands=(x, w_hg, w_ml, b_hg, b_ml), riders=riders, copies=_gather_copies, ride_shapes=_gather_shapes(riders))


def _hg_gates(hq, hf, lb, tri, b=None):
    s = _sigmoid(hf)
    om = 1.0 - lb
    f = lb + om * s
    k = om * (1.0 - s)
    sq = _sigmoid(hq)
    q = hq * sq
    if b is None:
        b = _sel_dot(tri, jnp.log(f))
    return q, sq, s, f, k, b


def _hg_scores(q, k, b, tril_mask, a=None):
    qts, kts, eqs, eks, rows = [], [], [], [], []
    for i in range(CHUNK // SUB):
        lo = i * SUB
        ref = jnp.zeros_like(b[0:1]) if i == 0 else b[lo - 1:lo]
        eq = jnp.exp(b[lo:lo + SUB] - ref)
        ek = jnp.exp(jnp.minimum(ref - b, EXP_CAP))
        qt = q[lo:lo + SUB] * eq
        kt = k * ek
        if a is None:
            rows.append(_bdot_nt(qt, kt))
        qts.append(qt); kts.append(kt); eqs.append(eq); eks.append(ek)
    if a is None:
        a = jnp.where(tril_mask, jnp.concatenate(rows, axis=0), 0.0)
    return a, qts, kts, eqs, eks


def _head_rms(o, gn, on_mxu=False):
    ms = _lane_sum(o * o) * (1.0 / o.shape[1]) if on_mxu else jnp.mean(o * o, axis=-1, keepdims=True)
    rstd = lax.rsqrt(ms + RMS_EPS)
    oh = o * rstd
    return oh, rstd, oh * gn


def _lower_bound(logit_ref):
    lg = logit_ref[...]
    return _sigmoid(lg[0:1] - lg[1:2])


def _hgrn2_fwd(u_hg, logits, gn, riders=()):
    t = u_hg.shape[0]
    tb = _tile(t, MIXER_ROWS)
    nc_blk = tb // CHUNK

    def body(u_ref, lg_ref, gn_ref, og_ref, sst_ref, b_ref, a_ref, o_ref, st_ref):
        @pl.when(pl.program_id(0) == 0)
        def _():
            st_ref[...] = jnp.zeros_like(st_ref)

        lb_all = _lower_bound(lg_ref)
        tril_mask = _tri(CHUNK)
        tri = tril_mask.astype(f32)

        def chunk(c, carry):
            r0 = pl.multiple_of(c * CHUNK, CHUNK)
            rows = pl.ds(r0, CHUNK)
            heads = range(HEADS)
            cols = [slice(h * HEAD_W, (h + 1) * HEAD_W) for h in heads]
            hv = [u_ref[rows, 2 * MIX_W + h * HEAD_W:2 * MIX_W + (h + 1) * HEAD_W] for h in heads]
            gts = [_hg_gates(u_ref[rows, h * HEAD_W:(h + 1) * HEAD_W], u_ref[rows, MIX_W + h * HEAD_W:MIX_W + (h + 1) * HEAD_W],
                             lb_all[:, cols[h]], tri) for h in heads]
            q = [g[0] for g in gts]
            k = [g[4] for g in gts]
            b = [g[5] for g in gts]
            a = [_hg_scores(q[h], k[h], b[h], tril_mask)[0] for h in heads]
            st = [st_ref[h] for h in heads]
            bl = [b[h][CHUNK - 1:CHUNK] for h in heads]
            o = [_bdot(a[h], hv[h]) + _bdot_nt(q[h] * jnp.exp(b[h]), st[h]) for h in heads]
            new_st = [st[h] * jnp.exp(bl[h]) + _bdot_tn(hv[h], k[h] * jnp.exp(bl[h] - b[h])) for h in heads]
            for h in heads:
                sst_ref[c, h] = st[h]
                st_ref[h] = new_st[h]
                b_ref[rows, cols[h]] = b[h]
                a_ref[rows, cols[h]] = a[h].astype(ACT)
                o_ref[rows, cols[h]] = o[h]
                hgate = u_ref[rows, 3 * MIX_W + h * HEAD_W:3 * MIX_W + (h + 1) * HEAD_W]
                _, _, y = _head_rms(o[h], gn_ref[:, cols[h]])
                og_ref[rows, cols[h]] = (y * (hgate * _sigmoid(hgate))).astype(ACT)
            return carry

        lax.fori_loop(0, nc_blk, chunk, 0, unroll=True)

    assert CHUNK == HEAD_W
    return _riding_call(
        body, "hgrn2_fwd", t // tb,
        in_specs=[_rows(tb, U_HG), _const((2, MIX_W)), _const((1, MIX_W))],
        out_specs=[_rows(tb, MIX_W), pl.BlockSpec((nc_blk, HEADS, HEAD_W, HEAD_W), lambda i: (i, 0, 0, 0)),
                   _rows(tb, MIX_W), _rows(tb, MIX_W), _rows(tb, MIX_W)],
        out_shape=[jax.ShapeDtypeStruct((t, MIX_W), ACT), jax.ShapeDtypeStruct((t // CHUNK, HEADS, HEAD_W, HEAD_W), f32),
                   jax.ShapeDtypeStruct((t, MIX_W), f32), jax.ShapeDtypeStruct((t, MIX_W), ACT), jax.ShapeDtypeStruct((t, MIX_W), f32)],
        scratch_shapes=[pltpu.VMEM((HEADS, HEAD_W, HEAD_W), f32)],
        operands=(u_hg, logits, gn), riders=riders, copies=_gather_copies, ride_shapes=_gather_shapes(riders))


def _hgrn2_bwd(u_hg, logits, gn, sst, bcum, scores, o_raw, dog, riders=()):
    t = u_hg.shape[0]
    tb = _tile(t, MIXER_ROWS)
    nb = t // tb
    nc_blk = tb // CHUNK

    def body(u_ref, lg_ref, gn_ref, sst_ref, b_ref, a_ref, o_ref, dog_ref, du_ref, dlg_ref, dgn_ref, dst_ref):
        @pl.when(pl.program_id(0) == 0)
        def _():
            dst_ref[...] = jnp.zeros_like(dst_ref)
            dlg_ref[...] = jnp.zeros_like(dlg_ref)
            dgn_ref[...] = jnp.zeros_like(dgn_ref)

        lb_all = _lower_bound(lg_ref)
        tril_mask = _tri(CHUNK)
        tri = tril_mask.astype(f32)
        triu = _tri(CHUNK, upper=True).astype(f32)

        def chunk(j, carry):
            c = nc_blk - 1 - j
            r0 = pl.multiple_of(c * CHUNK, CHUNK)
            rows = pl.ds(r0, CHUNK)
            heads = range(HEADS)
            nsub = CHUNK // SUB
            cols = [slice(h * HEAD_W, (h + 1) * HEAD_W) for h in heads]
            hq = [u_ref[rows, h * HEAD_W:(h + 1) * HEAD_W] for h in heads]
            hf = [u_ref[rows, MIX_W + h * HEAD_W:MIX_W + (h + 1) * HEAD_W] for h in heads]
            hv = [u_ref[rows, 2 * MIX_W + h * HEAD_W:2 * MIX_W + (h + 1) * HEAD_W] for h in heads]
            lb = [lb_all[:, cols[h]] for h in heads]
            gts = [_hg_gates(hq[h], hf[h], lb[h], tri, b=b_ref[rows, cols[h]]) for h in heads]
            q, sq, s, f, k, b = ([g[n] for g in gts] for n in range(6))
            scs = [_hg_scores(q[h], k[h], b[h], tril_mask, a=a_ref[rows, cols[h]]) for h in heads]
            a, qts, kts, eqs, eks = ([sc[n] for sc in scs] for n in range(5))
            st = [sst_ref[c, h] for h in heads]
            dst = [dst_ref[h] for h in heads]
            bl = [b[h][CHUNK - 1:CHUNK] for h in heads]
            eb = [jnp.exp(b[h]) for h in heads]
            qh = [q[h] * eb[h] for h in heads]
            ekl = [jnp.exp(bl[h] - b[h]) for h in heads]
            kh = [k[h] * ekl[h] for h in heads]
            o = [o_ref[rows, cols[h]] for h in heads]
            do = []
            for h in heads:
                hgate = u_ref[rows, 3 * MIX_W + h * HEAD_W:3 * MIX_W + (h + 1) * HEAD_W]
                gnh = gn_ref[:, cols[h]]
                oh, rstd, y = _head_rms(o[h], gnh)
                sg = _sigmoid(hgate)
                dogh = dog_ref[rows, cols[h]]
                dy = dogh * (hgate * sg)
                du_ref[rows, 3 * MIX_W + h * HEAD_W:3 * MIX_W + (h + 1) * HEAD_W] = (dogh * y * (sg * (1.0 + hgate * (1.0 - sg)))).astype(ACT)
                dgn_ref[:, cols[h]] += jnp.sum(dy * oh, axis=0, keepdims=True)
                doh = dy * gnh
                do.append(rstd * (doh - oh * jnp.mean(doh * oh, axis=-1, keepdims=True)))
            da = [jnp.where(tril_mask, _bdot_nt(do[h], hv[h]), 0.0) for h in heads]
            dv = [_bdot_tn(a[h], do[h]) + _bdot_nt(kh[h], dst[h]) for h in heads]
            dq = [_bdot(do[h], st[h]) * eb[h] for h in heads]
            dk = [_bdot(hv[h], dst[h]) * ekl[h] for h in heads]
            d_last = [jnp.sum(k[h] * dk[h], axis=0, keepdims=True) + jnp.exp(bl[h]) * jnp.sum(dst[h] * st[h], axis=0, keepdims=True)
                      for h in heads]
            d_b = [q[h] * dq[h] - k[h] * dk[h] for h in heads]
            dqs = [[] for _ in heads]
            q_dq = [[] for _ in heads]
            for i in range(nsub):
                for h in heads:
                    da_i = _mx(da[h][i * SUB:(i + 1) * SUB])
                    q_r, k_r = _mx(qts[h][i]), _mx(kts[h][i])
                    g_q = jnp.dot(da_i, k_r, preferred_element_type=f32)
                    g_k = lax.dot_general(da_i, q_r, (((0,), (0,)), ((), ())), preferred_element_type=f32)
                    dqs[h].append(g_q * eqs[h][i])
                    q_dq[h].append(q_r.astype(f32) * g_q)
                    dk[h] = dk[h] + g_k * eks[h][i]
                    d_b[h] = d_b[h] - k_r.astype(f32) * g_k
            for h in heads:
                dq[h] = dq[h] + jnp.concatenate(dqs[h], axis=0)
                d_b[h] = d_b[h] + jnp.concatenate(q_dq[h], axis=0)
                dst_ref[h] = dst[h] * jnp.exp(bl[h]) + _bdot_tn(do[h], qh[h])
            dg = [_sel_dot(triu, d_b[h]) + d_last[h] for h in heads]
            for h in heads:
                dfk = dg[h] / f[h] - dk[h]
                du_ref[rows, h * HEAD_W:(h + 1) * HEAD_W] = (dq[h] * (sq[h] * (1.0 + hq[h] * (1.0 - sq[h])))).astype(ACT)
                du_ref[rows, MIX_W + h * HEAD_W:MIX_W + (h + 1) * HEAD_W] = ((1.0 - lb[h]) * dfk * s[h] * (1.0 - s[h])).astype(ACT)
                du_ref[rows, 2 * MIX_W + h * HEAD_W:2 * MIX_W + (h + 1) * HEAD_W] = dv[h].astype(ACT)
                dlb = jnp.sum((1.0 - s[h]) * dfk, axis=0, keepdims=True) * (lb[h] * (1.0 - lb[h]))
                dlg_ref[0:1, cols[h]] += dlb
                dlg_ref[1:2, cols[h]] -= dlb
            return carry

        lax.fori_loop(0, nc_blk, chunk, 0, unroll=True)

    rev = _rows_rev(tb, MIX_W, nb)
    return _riding_call(
        body, "hgrn2_bwd", nb,
        in_specs=[_rows_rev(tb, U_HG, nb), _const((2, MIX_W)), _const((1, MIX_W)),
                  pl.BlockSpec((nc_blk, HEADS, HEAD_W, HEAD_W), lambda i: (nb - 1 - i, 0, 0, 0)), rev, rev, rev, rev],
        out_specs=[_rows_rev(tb, U_HG, nb), _const((2, MIX_W)), _const((1, MIX_W))],
        out_shape=[jax.ShapeDtypeStruct((t, U_HG), ACT), jax.ShapeDtypeStruct((2, MIX_W), f32), jax.ShapeDtypeStruct((1, MIX_W), f32)],
        scratch_shapes=[pltpu.VMEM((HEADS, HEAD_W, HEAD_W), f32)],
        operands=(u_hg, logits, gn, sst, bcum, scores, o_raw, dog), riders=riders, copies=_scatter_copies,
        ride_shapes=[jax.ShapeDtypeStruct(r.shape, r.dtype) for r in riders])


def _conv_fwd(u_ml, w, b):
    t = u_ml.shape[0]
    tm = _tile(t, LIGHT_ROWS)

    def body(x_ref, w_ref, b_ref, pre_ref, act_ref, xbuf):
        @pl.when(pl.program_id(0) == 0)
        def _():
            xbuf[...] = jnp.zeros_like(xbuf)

        xbuf[0:HALO, :] = xbuf[tm:tm + HALO, :]
        xbuf[HALO:HALO + tm, :] = x_ref[...]
        pre = b_ref[...] + jnp.zeros((tm, MIX_W), f32)
        for kk in range(CONV_K):
            off = HALO - (CONV_K - 1) + kk
            pre = pre + w_ref[kk:kk + 1, :] * xbuf[off:off + tm, :]
        pre_ref[...] = pre
        act_ref[...] = pre * _sigmoid(pre)

    return pl.pallas_call(
        body, name="conv_fwd", grid=(t // tm,),
        in_specs=[_rows(tm, MIX_W), _const((CONV_K, MIX_W)), _const((1, MIX_W))],
        out_specs=[_rows(tm, MIX_W), _rows(tm, MIX_W)],
        out_shape=[jax.ShapeDtypeStruct((t, MIX_W), f32)] * 2,
        scratch_shapes=[pltpu.VMEM((tm + HALO, MIX_W), f32)],
        compiler_params=_cparams(1),
    )(u_ml, w, b)


def _conv_bwd(u_ml, w, pre, dact):
    t = u_ml.shape[0]
    tm = _tile(t, LIGHT_ROWS)
    nb = t // tm
    hb = tm // HALO

    def body(x_ref, halo_ref, w_ref, pre_ref, dact_ref, dx_ref, dw_ref, db_ref, dbuf, xbuf):
        i = pl.program_id(0)

        @pl.when(i == 0)
        def _():
            dbuf[...] = jnp.zeros_like(dbuf)
            dw_ref[...] = jnp.zeros_like(dw_ref)
            db_ref[...] = jnp.zeros_like(db_ref)

        p = pre_ref[...]
        sg = _sigmoid(p)
        dpre = dact_ref[...] * (sg * (1.0 + p * (1.0 - sg)))
        dbuf[tm:tm + HALO, :] = dbuf[0:HALO, :]
        dbuf[0:tm, :] = dpre
        has_prev = (i < nb - 1).astype(f32)
        xbuf[0:HALO, :] = halo_ref[...] * has_prev
        xbuf[HALO:HALO + tm, :] = x_ref[...]
        dx = jnp.zeros((tm, MIX_W), f32)
        for kk in range(CONV_K):
            back = CONV_K - 1 - kk
            dx = dx + w_ref[kk:kk + 1, :] * dbuf[back:back + tm, :]
            off = HALO - (CONV_K - 1) + kk
            dw_ref[kk:kk + 1, :] += jnp.sum(dpre * xbuf[off:off + tm, :], axis=0, keepdims=True)
        dx_ref[...] = dx.astype(ACT)
        db_ref[...] += jnp.sum(dpre, axis=0, keepdims=True)

    return pl.pallas_call(
        body, name="conv_bwd", grid=(nb,),
        in_specs=[_rows_rev(tm, MIX_W, nb),
                  pl.BlockSpec((HALO, MIX_W), lambda i: (jnp.maximum((nb - 1 - i) * hb - 1, 0), 0)),
                  _const((CONV_K, MIX_W)), _rows_rev(tm, MIX_W, nb), _rows_rev(tm, MIX_W, nb)],
        out_specs=[_rows_rev(tm, MIX_W, nb), _const((CONV_K, MIX_W)), _const((1, MIX_W))],
        out_shape=[jax.ShapeDtypeStruct((t, MIX_W), ACT), jax.ShapeDtypeStruct((CONV_K, MIX_W), f32), jax.ShapeDtypeStruct((1, MIX_W), f32)],
        scratch_shapes=[pltpu.VMEM((tm + HALO, MIX_W), f32), pltpu.VMEM((tm + HALO, MIX_W), f32)],
        compiler_params=_cparams(1),
    )(u_ml, u_ml, w, pre, dact)


def _lane_pick(x, lane):
    idx = lax.broadcasted_iota(jnp.int32, x.shape, 1)
    return jnp.sum(jnp.where(idx == lane, x, 0.0), axis=-1, keepdims=True)


def _ml_gate_forms(gates, tri):
    lf = _log_sigmoid(gates)
    gc = _sel_dot(tri, lf)
    lane = lax.broadcasted_iota(jnp.int32, gates.shape, 1)
    mixed = jnp.where(lane < HEADS, gates, gc)
    sel = (lax.broadcasted_iota(jnp.int32, (8, 128), 0) == lax.broadcasted_iota(jnp.int32, (8, 128), 1)).astype(f32)
    rowsf = _sel_dot_nt(sel, mixed)
    return gc, rowsf


def _ml_chunk(q, k, v, gates, gc, rowsf, c_st, n_st, m_st, tril_mask):
    hs = range(HEADS)
    g_col = [_lane_pick(gc, HEADS + h) for h in hs]
    ig_col = [_lane_pick(gates, h) for h in hs]
    dmat = [jnp.where(tril_mask, g_col[h] - rowsf[HEADS + h:HEADS + h + 1, :] + rowsf[h:h + 1, :], NEG) for h in hs]
    m_inter = [g_col[h] + m_st[h] for h in hs]
    m_t = [jnp.maximum(m_inter[h], jnp.max(dmat[h], axis=-1, keepdims=True)) for h in hs]
    wi = [jnp.exp(dmat[h] - m_t[h]) for h in hs]
    wo = [jnp.exp(m_inter[h] - m_t[h]) for h in hs]
    qk = [_bdot_nt(q[h], k[h]) * wi[h] for h in hs]
    num = [_bdot(qk[h], v[h]) + wo[h] * _bdot(q[h], c_st[h]) for h in hs]
    den = [_lane_sum(qk[h]) + wo[h] * _lane_dot(q[h], n_st[h]) for h in hs]
    floor = [jnp.exp(-m_t[h]) for h in hs]
    z = [jnp.maximum(jnp.abs(den[h]), floor[h]) for h in hs]
    g_last = [g_col[h][CHUNK - 1:CHUNK] for h in hs]
    a_col = [g_last[h] - g_col[h] + ig_col[h] for h in hs]
    m_new = [jnp.maximum(g_last[h] + m_st[h], jnp.max(a_col[h], axis=0, keepdims=True)) for h in hs]
    ws = [jnp.exp(a_col[h] - m_new[h]) for h in hs]
    w_old = [jnp.exp(g_last[h] + m_st[h] - m_new[h]) for h in hs]
    return dict(wi=wi, wo=wo, qk=qk, num=num, den=den, z=z, floor=floor, ws=ws, w_old=w_old, m_new=m_new)


def _mlstm_fwd(qkc, u_ml, gn, riders=()):
    t = qkc.shape[0]
    tb = _tile(t, MIXER_ROWS)
    nc_blk = tb // CHUNK

    def body(qk_ref, v_ref, mo_ref, gt_ref, gn_ref, og_ref, cst_ref, nst_ref, mst_ref, c_sc, n_sc, m_sc):
        @pl.when(pl.program_id(0) == 0)
        def _():
            c_sc[...] = jnp.zeros_like(c_sc)
            n_sc[...] = jnp.zeros_like(n_sc)
            m_sc[...] = jnp.zeros_like(m_sc)

        tril_mask = _tri(CHUNK)
        tri = tril_mask.astype(f32)

        def chunk(c, carry):
            r0 = pl.multiple_of(c * CHUNK, CHUNK)
            rows = pl.ds(r0, CHUNK)
            gates = gt_ref[rows, :]
            gc, rowsf = _ml_gate_forms(gates, tri)
            hs = range(HEADS)
            q = [qk_ref[rows, h * ML_DQK:(h + 1) * ML_DQK] * (ML_DQK ** -0.5) for h in hs]
            k = [qk_ref[rows, HEADS * ML_DQK + h * ML_DQK:HEADS * ML_DQK + (h + 1) * ML_DQK] for h in hs]
            v = [v_ref[rows, h * HEAD_W:(h + 1) * HEAD_W] for h in hs]
            c_st = [c_sc[h] for h in hs]
            n_st = [n_sc[h] for h in hs]
            m_full = [m_sc[h] for h in hs]
            r = _ml_chunk(q, k, v, gates, gc, rowsf, c_st, n_st, [m[:, 0:1] for m in m_full], tril_mask)
            ksc = [k[h] * r["ws"][h] for h in hs]
            new_c = [r["w_old"][h] * c_st[h] + _bdot_tn(ksc[h], v[h]) for h in hs]
            for h in hs:
                cs = slice(h * HEAD_W, (h + 1) * HEAD_W)
                cst_ref[c, h] = c_st[h]
                nst_ref[c, h] = n_st[h]
                mst_ref[c, h] = m_full[h]
                c_sc[h] = new_c[h]
                n_sc[h] = r["w_old"][h] * n_st[h] + jnp.sum(ksc[h], axis=0, keepdims=True)
                m_sc[h] = r["m_new"][h] + jnp.zeros((1, 128), f32)
                _, _, y = _head_rms(r["num"][h] / r["z"][h], gn_ref[:, cs], on_mxu=True)
                og_ref[rows, cs] = (y * _sigmoid(mo_ref[rows, h * HEAD_W:(h + 1) * HEAD_W])).astype(ACT)
            return carry

        lax.fori_loop(0, nc_blk, chunk, 0, unroll=True)

    nchunks = t // CHUNK
    return _riding_call(
        body, "mlstm_fwd", t // tb,
        in_specs=[_rows(tb, MIX_W), _rows(tb, MIX_W, 1), _rows(tb, MIX_W, 2), _rows(tb, 128, 12), _const((1, MIX_W))],
        out_specs=[_rows(tb, MIX_W),
                   pl.BlockSpec((nc_blk, HEADS, ML_DQK, HEAD_W), lambda i: (i, 0, 0, 0)),
                   pl.BlockSpec((nc_blk, HEADS, 1, ML_DQK), lambda i: (i, 0, 0, 0)),
                   pl.BlockSpec((nc_blk, HEADS, 1, 128), lambda i: (i, 0, 0, 0))],
        out_shape=[jax.ShapeDtypeStruct((t, MIX_W), ACT),
                   jax.ShapeDtypeStruct((nchunks, HEADS, ML_DQK, HEAD_W), f32),
                   jax.ShapeDtypeStruct((nchunks, HEADS, 1, ML_DQK), f32),
                   jax.ShapeDtypeStruct((nchunks, HEADS, 1, 128), f32)],
        scratch_shapes=[pltpu.VMEM((HEADS, ML_DQK, HEAD_W), f32), pltpu.VMEM((HEADS, 1, ML_DQK), f32), pltpu.VMEM((HEADS, 1, 128), f32)],
        operands=(qkc, u_ml, u_ml, u_ml, gn), riders=riders, copies=_gather_copies, ride_shapes=_gather_shapes(riders))


def _mlstm_bwd(qkc, u_ml, gn, cst, nst, mst, dog, riders=()):
    t = qkc.shape[0]
    tb = _tile(t, MIXER_ROWS)
    nb = t // tb
    nc_blk = tb // CHUNK

    def body(qk_ref, v_ref, mo_ref, gt_ref, gn_ref, cst_ref, nst_ref, mst_ref, dog_ref,
             dqk_ref, dv_ref, dmo_ref, dgt_ref, dgn_ref, dc_sc, dn_sc):
        @pl.when(pl.program_id(0) == 0)
        def _():
            dc_sc[...] = jnp.zeros_like(dc_sc)
            dn_sc[...] = jnp.zeros_like(dn_sc)
            dgn_ref[...] = jnp.zeros_like(dgn_ref)

        tril_mask = _tri(CHUNK)
        tri = tril_mask.astype(f32)
        triu = _tri(CHUNK, upper=True).astype(f32)
        lane = lax.broadcasted_iota(jnp.int32, (CHUNK, 128), 1)

        def chunk(j, carry):
            c = nc_blk - 1 - j
            r0 = pl.multiple_of(c * CHUNK, CHUNK)
            rows = pl.ds(r0, CHUNK)
            gates = gt_ref[rows, :]
            gc, rowsf = _ml_gate_forms(gates, tri)
            dg_mat = jnp.zeros((CHUNK, 128), f32)
            dig_mat = jnp.zeros((CHUNK, 128), f32)
            dlast_row = jnp.zeros((1, 128), f32)
            hs = range(HEADS)
            cols = [slice(h * HEAD_W, (h + 1) * HEAD_W) for h in hs]
            q = [qk_ref[rows, h * ML_DQK:(h + 1) * ML_DQK] * (ML_DQK ** -0.5) for h in hs]
            k = [qk_ref[rows, HEADS * ML_DQK + h * ML_DQK:HEADS * ML_DQK + (h + 1) * ML_DQK] for h in hs]
            v = [v_ref[rows, h * HEAD_W:(h + 1) * HEAD_W] for h in hs]
            c_st = [cst_ref[c, h] for h in hs]
            n_st = [nst_ref[c, h] for h in hs]
            m_st = [mst_ref[c, h][:, 0:1] for h in hs]
            dc = [dc_sc[h] for h in hs]
            dn = [dn_sc[h] for h in hs]
            r = _ml_chunk(q, k, v, gates, gc, rowsf, c_st, n_st, m_st, tril_mask)
            z, wi, wo, ws, w_old, den = r["z"], r["wi"], r["wo"], r["ws"], r["w_old"], r["den"]
            hh = [r["num"][h] / z[h] for h in hs]
            dh = []
            for h in hs:
                gnh = gn_ref[:, cols[h]]
                oh, rstd, y = _head_rms(hh[h], gnh, on_mxu=True)
                sg = _sigmoid(mo_ref[rows, h * HEAD_W:(h + 1) * HEAD_W])
                dogh = dog_ref[rows, cols[h]]
                dy = dogh * sg
                dmo_ref[rows, cols[h]] = (dogh * y * (sg * (1.0 - sg))).astype(ACT)
                dgn_ref[:, cols[h]] += jnp.sum(dy * oh, axis=0, keepdims=True)
                doh = dy * gnh
                dh.append(rstd * (doh - oh * (_lane_sum(doh * oh) * (1.0 / HEAD_W))))
            dnum = [dh[h] / z[h] for h in hs]
            dz = [-_lane_sum(dh[h] * hh[h]) / z[h] for h in hs]
            dden = [jnp.where(jnp.abs(den[h]) > r["floor"][h], dz[h] * jnp.sign(den[h]), 0.0) for h in hs]
            dsw = [(_bdot_nt(dnum[h], v[h]) + dden[h]) * wi[h] for h in hs]
            dq = [_bdot(dsw[h], k[h]) + wo[h] * (_bdot_nt(dnum[h], c_st[h]) + dden[h][:, :ML_DQK] * n_st[h]) for h in hs]
            dk_state = [ws[h] * (_bdot_nt(v[h], dc[h]) + dn[h]) for h in hs]
            dk = [_bdot_tn(dsw[h], q[h]) + dk_state[h] for h in hs]
            dv = [_bdot_tn(r["qk"][h], dnum[h]) + ws[h] * _bdot(k[h], dc[h]) for h in hs]
            woq = [wo[h] * q[h] for h in hs]
            new_dc = [w_old[h] * dc[h] + _bdot_tn(woq[h], dnum[h]) for h in hs]
            for h in hs:
                dv_ref[rows, cols[h]] = dv[h].astype(ACT)
                dc_sc[h] = new_dc[h]
                dn_sc[h] = w_old[h] * dn[h] + jnp.sum(woq[h] * dden[h][:, :ML_DQK], axis=0, keepdims=True)
                d_last = (jnp.sum(jnp.sum(k[h] * dk_state[h], axis=0, keepdims=True), axis=-1, keepdims=True)
                          + w_old[h] * (jnp.sum(jnp.sum(dc[h] * c_st[h], axis=0, keepdims=True), axis=-1, keepdims=True)
                                        + jnp.sum(dn[h] * n_st[h], axis=-1, keepdims=True)))
                kdk = _lane_sum(k[h] * dk[h])
                qdq = _lane_sum(q[h] * dq[h])
                dg_mat = dg_mat + jnp.where(lane == HEADS + h, qdq - kdk, 0.0)
                dlast_row = dlast_row + jnp.where(lane[0:1] == HEADS + h, d_last, 0.0)
                dig_mat = dig_mat + jnp.where(lane == h, kdk, 0.0)
                dqk_ref[rows, h * ML_DQK:(h + 1) * ML_DQK] = dq[h] * (ML_DQK ** -0.5)
                dqk_ref[rows, HEADS * ML_DQK + h * ML_DQK:HEADS * ML_DQK + (h + 1) * ML_DQK] = dk[h]
            dlf = _sel_dot(triu, dg_mat) + dlast_row
            dgt_ref[rows, :] = (dig_mat + dlf * _sigmoid(-gates)).astype(ACT)
            return carry

        lax.fori_loop(0, nc_blk, chunk, 0, unroll=True)

    st4 = lambda a, b: pl.BlockSpec((nc_blk, HEADS, a, b), lambda i: (nb - 1 - i, 0, 0, 0))
    return _riding_call(
        body, "mlstm_bwd", nb,
        in_specs=[_rows_rev(tb, MIX_W, nb), _rows_rev(tb, MIX_W, nb, 1), _rows_rev(tb, MIX_W, nb, 2), _rows_rev(tb, 128, nb, 12),
                  _const((1, MIX_W)), st4(ML_DQK, HEAD_W), st4(1, ML_DQK), st4(1, 128), _rows_rev(tb, MIX_W, nb)],
        out_specs=[_rows_rev(tb, MIX_W, nb), _rows_rev(tb, MIX_W, nb), _rows_rev(tb, MIX_W, nb), _rows_rev(tb, 128, nb), _const((1, MIX_W))],
        out_shape=[jax.ShapeDtypeStruct((t, MIX_W), f32), jax.ShapeDtypeStruct((t, MIX_W), ACT), jax.ShapeDtypeStruct((t, MIX_W), ACT),
                   jax.ShapeDtypeStruct((t, 128), ACT), jax.ShapeDtypeStruct((1, MIX_W), f32)],
        scratch_shapes=[pltpu.VMEM((HEADS, ML_DQK, HEAD_W), f32), pltpu.VMEM((HEADS, 1, ML_DQK), f32)],
        operands=(qkc, u_ml, u_ml, u_ml, gn, cst, nst, mst, dog), riders=riders, copies=_scatter_copies,
        ride_shapes=[jax.ShapeDtypeStruct(r.shape, r.dtype) for r in riders])


def _ln_fwd(r, g, b):
    mu = jnp.mean(r, axis=-1, keepdims=True)
    xc = r - mu
    rstd = lax.rsqrt(jnp.mean(xc * xc, axis=-1, keepdims=True) + LN_EPS)
    xh = xc * rstd
    return xh * g + b, xh, rstd


def _ln_bwd(dy, xh, rstd, g):
    dxh = dy * g
    return rstd * (dxh - jnp.mean(dxh, axis=-1, keepdims=True) - xh * jnp.mean(dxh * xh, axis=-1, keepdims=True))


def _outproj_ln1(og_hg, og_ml, x, w_out, g, b, riders=()):
    t = x.shape[0]
    tm = _tile(t, LIGHT_ROWS)

    def body(a_ref, b_ref, x_ref, w_ref, g_ref, bb_ref, x1_ref, xh_ref, rs_ref, x1b_ref):
        mix = _bdot(a_ref[...], w_ref[0:MIX_W, :]) + _bdot(b_ref[...], w_ref[MIX_W:2 * MIX_W, :])
        y, xh, rstd = _ln_fwd(ALPHA * x_ref[...] + mix, g_ref[...], bb_ref[...])
        x1_ref[...] = y
        x1b_ref[...] = y.astype(ACT)
        xh_ref[...] = xh.astype(ACT)
        rs_ref[...] = rstd

    return _riding_call(
        body, "outproj_ln1", t // tm,
        in_specs=[_rows(tm, MIX_W), _rows(tm, MIX_W), _rows(tm, D_MODEL), _resident((D_MODEL, D_MODEL)), _const((1, D_MODEL)), _const((1, D_MODEL))],
        out_specs=[_rows(tm, D_MODEL), _rows(tm, D_MODEL), _rows(tm, 1), _rows(tm, D_MODEL)],
        out_shape=[jax.ShapeDtypeStruct((t, D_MODEL), f32), jax.ShapeDtypeStruct((t, D_MODEL), ACT), jax.ShapeDtypeStruct((t, 1), f32),
                   jax.ShapeDtypeStruct((t, D_MODEL), ACT)],
        scratch_shapes=[], operands=(og_hg, og_ml, x, w_out, g, b), riders=riders, copies=_gather_copies, ride_shapes=_gather_shapes(riders))


def _ffn_up(x1, wg, wu, riders=()):
    t = x1.shape[0]
    tm = _tile(t, DENSE_ROWS)

    def body(x_ref, wg_ref, wu_ref, hg_ref, up_ref, a_ref):
        xv = x_ref[...]
        hg = _bdot_nt(xv, wg_ref[...])
        up = _bdot_nt(xv, wu_ref[...])
        hg_ref[...] = hg.astype(ACT)
        up_ref[...] = up.astype(ACT)
        a_ref[...] = (hg * _sigmoid(hg) * up).astype(ACT)

    return _riding_call(
        body, "ffn_up", t // tm,
        in_specs=[_rows(tm, D_MODEL), _resident((D_FF, D_MODEL)), _resident((D_FF, D_MODEL))],
        out_specs=[_rows(tm, D_FF), _rows(tm, D_FF), _rows(tm, D_FF)],
        out_shape=[jax.ShapeDtypeStruct((t, D_FF), ACT), jax.ShapeDtypeStruct((t, D_FF), ACT), jax.ShapeDtypeStruct((t, D_FF), ACT)],
        scratch_shapes=[], operands=(x1, wg, wu), riders=riders, copies=_gather_copies, ride_shapes=_gather_shapes(riders))


def _ffn_down_ln2(a, x1, wd, g, b):
    t = x1.shape[0]
    tm = _tile(t, LIGHT_ROWS)

    def body(a_ref, x_ref, w_ref, g_ref, bb_ref, x2_ref, xh_ref, rs_ref, x2b_ref):
        ffn = _bdot(a_ref[...], w_ref[...])
        y, xh, rstd = _ln_fwd(ALPHA * x_ref[...] + ffn, g_ref[...], bb_ref[...])
        x2_ref[...] = y
        x2b_ref[...] = y.astype(ACT)
        xh_ref[...] = xh.astype(ACT)
        rs_ref[...] = rstd

    return pl.pallas_call(
        body, name="ffn_down_ln2", grid=(t // tm,),
        in_specs=[_rows(tm, D_FF), _rows(tm, D_MODEL), _resident((D_FF, D_MODEL)), _const((1, D_MODEL)), _const((1, D_MODEL))],
        out_specs=[_rows(tm, D_MODEL), _rows(tm, D_MODEL), _rows(tm, 1), _rows(tm, D_MODEL)],
        out_shape=[jax.ShapeDtypeStruct((t, D_MODEL), f32), jax.ShapeDtypeStruct((t, D_MODEL), ACT), jax.ShapeDtypeStruct((t, 1), f32),
                   jax.ShapeDtypeStruct((t, D_MODEL), ACT)],
        compiler_params=_cparams(1, arbitrary=False),
    )(a, x1, wd, g, b)


def _head_loss_bwd(x2, xh2, rs2, p, tgt, w_pg, b_pg, w_pp, g2):
    t = x2.shape[0]
    tm = _tile(t, LIGHT_ROWS)

    def body(x_ref, xh_ref, rs_ref, p_ref, t_ref, wg_ref, bg_ref, wp_ref, g_ref,
             dr_ref, de_ref, dz_ref, loss_ref, dbg_ref, dg2_ref, db2_ref):
        @pl.when(pl.program_id(0) == 0)
        def _():
            loss_ref[...] = jnp.zeros_like(loss_ref)
            dbg_ref[...] = jnp.zeros_like(dbg_ref)
            dg2_ref[...] = jnp.zeros_like(dg2_ref)
            db2_ref[...] = jnp.zeros_like(db2_ref)

        x2v = x_ref[...]
        z = _bdot(x2v, wg_ref[...]) + bg_ref[...]
        e = _bdot(p_ref[...], wp_ref[...])
        sg = _sigmoid(z)
        diff = x2v + sg * e - t_ref[...]
        loss_ref[...] += 0.5 * jnp.sum(jnp.mean(diff * diff, axis=-1, keepdims=True), axis=0, keepdims=True)
        dy = diff * (1.0 / D_MODEL)
        de_ref[...] = (dy * sg).astype(ACT)
        dz = dy * e * (sg * (1.0 - sg))
        dz_ref[...] = dz.astype(ACT)
        dbg_ref[...] += jnp.sum(dz, axis=0, keepdims=True)
        dx2 = dy + _bdot_nt(dz, wg_ref[...])
        xh = xh_ref[...].astype(f32)
        dg2_ref[...] += jnp.sum(dx2 * xh, axis=0, keepdims=True)
        db2_ref[...] += jnp.sum(dx2, axis=0, keepdims=True)
        dr_ref[...] = _ln_bwd(dx2, xh, rs_ref[...], g_ref[...])

    row = jax.ShapeDtypeStruct((1, D_MODEL), f32)
    return pl.pallas_call(
        body, name="head_loss_bwd", grid=(t // tm,),
        in_specs=[_rows(tm, D_MODEL), _rows(tm, D_MODEL), _rows(tm, 1), _rows(tm, PLE), _rows(tm, D_MODEL),
                  _resident((D_MODEL, D_MODEL)), _const((1, D_MODEL)), _resident((PLE, D_MODEL)), _const((1, D_MODEL))],
        out_specs=[_rows(tm, D_MODEL), _rows(tm, D_MODEL), _rows(tm, D_MODEL), _const((1, 1)), _const((1, D_MODEL)), _const((1, D_MODEL)), _const((1, D_MODEL))],
        out_shape=[jax.ShapeDtypeStruct((t, D_MODEL), f32), jax.ShapeDtypeStruct((t, D_MODEL), ACT), jax.ShapeDtypeStruct((t, D_MODEL), ACT),
                   jax.ShapeDtypeStruct((1, 1), f32), row, row, row],
        compiler_params=_cparams(1),
    )(x2, xh2, rs2, p, tgt, w_pg, b_pg, w_pp, g2)


def _ffn_bwd(dr2, hg, up, xh1, rs1, wd, wg, wu, g1, w_out):
    t = dr2.shape[0]
    tm = _tile(t, DENSE_ROWS // 2)

    def body(dr_ref, hg_ref, up_ref, xh_ref, rs_ref, wd_ref, wg_ref, wu_ref, g_ref, wo_ref,
             dr1_ref, dhg_ref, dup_ref, dg1_ref, db1_ref, doghg_ref, dogml_ref):
        @pl.when(pl.program_id(0) == 0)
        def _():
            dg1_ref[...] = jnp.zeros_like(dg1_ref)
            db1_ref[...] = jnp.zeros_like(db1_ref)

        dr2v = dr_ref[...]
        da = _bdot_nt(dr2v, wd_ref[...])
        hgv = hg_ref[...].astype(f32)
        sg = _sigmoid(hgv)
        dhg = da * up_ref[...].astype(f32) * (sg * (1.0 + hgv * (1.0 - sg)))
        dup = da * (hgv * sg)
        dhg_ref[...] = dhg.astype(ACT)
        dup_ref[...] = dup.astype(ACT)
        dx1 = ALPHA * dr2v + _bdot(dhg, wg_ref[...]) + _bdot(dup, wu_ref[...])
        xh = xh_ref[...].astype(f32)
        dg1_ref[...] += jnp.sum(dx1 * xh, axis=0, keepdims=True)
        db1_ref[...] += jnp.sum(dx1, axis=0, keepdims=True)
        dr1 = _ln_bwd(dx1, xh, rs_ref[...], g_ref[...])
        dr1_ref[...] = dr1
        dog = _bdot_nt(dr1, wo_ref[...])
        doghg_ref[...] = dog[:, 0:MIX_W]
        dogml_ref[...] = dog[:, MIX_W:2 * MIX_W]

    row = jax.ShapeDtypeStruct((1, D_MODEL), f32)
    return pl.pallas_call(
        body, name="ffn_bwd", grid=(t // tm,),
        in_specs=[_rows(tm, D_MODEL), _rows(tm, D_FF), _rows(tm, D_FF), _rows(tm, D_MODEL), _rows(tm, 1),
                  _resident((D_FF, D_MODEL)), _resident((D_FF, D_MODEL)), _resident((D_FF, D_MODEL)), _const((1, D_MODEL)),
                  _resident((D_MODEL, D_MODEL))],
        out_specs=[_rows(tm, D_MODEL), _rows(tm, D_FF), _rows(tm, D_FF), _const((1, D_MODEL)), _const((1, D_MODEL)),
                   _rows(tm, MIX_W), _rows(tm, MIX_W)],
        out_shape=[jax.ShapeDtypeStruct((t, D_MODEL), f32), jax.ShapeDtypeStruct((t, D_FF), ACT), jax.ShapeDtypeStruct((t, D_FF), ACT), row, row,
                   jax.ShapeDtypeStruct((t, MIX_W), f32), jax.ShapeDtypeStruct((t, MIX_W), f32)],
        compiler_params=_cparams(1),
    )(dr2, hg, up, xh1, rs1, wd, wg, wu, g1, w_out)


def _inproj_bwd(dr1, du_hg, dqk, dmv, dmo, dgt, w_hg, w_ml):
    t = dr1.shape[0]
    tm = _tile(t, DENSE_ROWS)

    def body(dr_ref, dhg_ref, dqk_ref, dmv_ref, dmo_ref, dgt_ref, whg_ref, wml_ref, gx_ref, dml_ref):
        dml = jnp.concatenate([dqk_ref[...], dmv_ref[...], dmo_ref[...], dgt_ref[...]], axis=-1).astype(ACT)
        dml_ref[...] = dml
        gx_ref[...] = ALPHA * dr_ref[...] + _bdot(dhg_ref[...], whg_ref[...]) + _bdot(dml, wml_ref[...])

    return pl.pallas_call(
        body, name="inproj_bwd", grid=(t // tm,),
        in_specs=[_rows(tm, D_MODEL), _rows(tm, U_HG), _rows(tm, MIX_W), _rows(tm, MIX_W), _rows(tm, MIX_W), _rows(tm, 128),
                  _resident((U_HG, D_MODEL)), _resident((U_ML, D_MODEL))],
        out_specs=[_rows(tm, D_MODEL), _rows(tm, U_ML)],
        out_shape=[jax.ShapeDtypeStruct((t, D_MODEL), f32), jax.ShapeDtypeStruct((t, U_ML), ACT)],
        compiler_params=_cparams(1, arbitrary=False),
    )(dr1, du_hg, dqk, dmv, dmo, dgt, w_hg, w_ml)


def _wgrad(a, b, name, tk=None, tn=None, colsum=False, low=False):
    t, kdim = a.shape
    n = b.shape[1]
    tk = tk or kdim
    tn = tn or n
    tt = _tile(t, WGRAD_ROWS)
    nt = t // tt
    assert not (colsum and low) and (not colsum or tn == n)

    def body(a_ref, b_ref, o_ref, *s_ref):
        @pl.when(pl.program_id(2) == 0)
        def _():
            o_ref[...] = jnp.zeros_like(o_ref)
            if colsum:
                s_ref[0][...] = jnp.zeros_like(s_ref[0])

        av = a_ref[...]
        o_ref[...] += _bdot_tn(av, b_ref[...])
        if colsum:
            s_ref[0][...] += jnp.sum(av.astype(f32), axis=0, keepdims=True)
        if low:
            @pl.when(pl.program_id(2) == nt - 1)
            def _():
                s_ref[0][...] = o_ref[...].astype(bf16)

    out_specs = [pl.BlockSpec((tk, tn), lambda i, j, s: (i, j))]
    out_shape = [jax.ShapeDtypeStruct((kdim, n), f32)]
    if colsum:
        out_specs.append(pl.BlockSpec((1, tk), lambda i, j, s: (0, i)))
        out_shape.append(jax.ShapeDtypeStruct((1, kdim), f32))
    if low:
        out_specs.append(pl.BlockSpec((tk, tn), lambda i, j, s: (i, j)))
        out_shape.append(jax.ShapeDtypeStruct((kdim, n), bf16))
    res = pl.pallas_call(
        body, name=name, grid=(kdim // tk, n // tn, t // tt),
        in_specs=[pl.BlockSpec((tt, tk), lambda i, j, s: (s, i)), pl.BlockSpec((tt, tn), lambda i, j, s: (s, j))],
        out_specs=out_specs, out_shape=out_shape,
        compiler_params=_cparams(3),
    )(a, b)
    return res if (colsum or low) else res[0]


_TRANSPOSED = {"w_in", "w_ffn_gate", "w_ffn_up"}
_COL_SPLIT = {"ple_w_proj"}
_SCATTER_PLAN = (("w_ffn_gate", "w_ffn_up"), ("w_ffn_down", "w_out", "ple_w_gate", "ple_w_proj"))
_RIDE_PLAN = {"inproj": ("w_ffn_gate",), "hgrn2_fwd": ("w_ffn_up",), "mlstm_fwd": ("w_out",),
              "outproj_ln1": ("ple_w_gate", "ple_w_proj"), "ffn_up": ("w_ffn_down",)}


def _from_chip_major(a, col_split):
    if col_split:
        return a.transpose(1, 0, 2).reshape(a.shape[1], 4 * a.shape[2])
    return a.reshape(4 * a.shape[1], a.shape[2])


def _local_step(x, p, tgt, w_in_b, b_in, logits, conv_w, conv_b, hg_gn, ml_gn, w_out_b, ln1_g, ln1_b,
                wg_b, wu_b, wd_b, ln2_g, ln2_b, w_pp_b, w_pg_b, b_pg, early_hook=None, late_shards=None):
    pad_w = U_HG + U_ML - PROJ_W
    w_hg = w_in_b[:U_HG]
    w_ml = jnp.pad(w_in_b[U_HG:], ((0, pad_w), (0, 0)))
    bb_hg = b_in[:, :U_HG]
    bb_ml = jnp.pad(b_in[:, U_HG:], ((0, 0), (0, pad_w)))

    late = dict(w_out=w_out_b, w_ffn_gate=wg_b, w_ffn_up=wu_b, w_ffn_down=wd_b, ple_w_proj=w_pp_b, ple_w_gate=w_pg_b)

    def riders_of(call):
        return [late_shards[k] for k in _RIDE_PLAN[call]] if late_shards is not None else ()

    def arrived(call, got):
        for k, g in zip(_RIDE_PLAN[call], got):
            late[k] = _from_chip_major(g, k in _COL_SPLIT)

    (u_hg, u_ml, xb), got = _inproj(x, w_hg, w_ml, bb_hg, bb_ml, riders_of("inproj"))
    arrived("inproj", got)
    (og_hg, sst, hg_b, hg_a, hg_o), got = _hgrn2_fwd(u_hg, logits, hg_gn, riders_of("hgrn2_fwd"))
    arrived("hgrn2_fwd", got)
    pre, qkc = _conv_fwd(u_ml, conv_w, conv_b)
    (og_ml, cst, nst, mst), got = _mlstm_fwd(qkc, u_ml, ml_gn, riders_of("mlstm_fwd"))
    arrived("mlstm_fwd", got)
    (x1, xh1, rs1, x1b), got = _outproj_ln1(og_hg, og_ml, x, late["w_out"], ln1_g, ln1_b, riders_of("outproj_ln1"))
    arrived("outproj_ln1", got)
    (hgp, up, act), got = _ffn_up(x1b, late["w_ffn_gate"], late["w_ffn_up"], riders_of("ffn_up"))
    arrived("ffn_up", got)
    w_out_b, wg_b, wu_b, wd_b = late["w_out"], late["w_ffn_gate"], late["w_ffn_up"], late["w_ffn_down"]
    w_pp_b, w_pg_b = late["ple_w_proj"], late["ple_w_gate"]
    x2, xh2, rs2, x2b = _ffn_down_ln2(act, x1, wd_b, ln2_g, ln2_b)
    dr2, de, dz, loss, d_bpg, d_ln2g, d_ln2b = _head_loss_bwd(x2, xh2, rs2, p, tgt, w_pg_b, b_pg, w_pp_b, ln2_g)
    dr1, dhg, dup, d_ln1g, d_ln1b, dog_hg, dog_ml = _ffn_bwd(dr2, hgp, up, xh1, rs1, wd_b, wg_b, wu_b, ln1_g, w_out_b)

    d_wo_a, lo_wo_a = _wgrad(og_hg, dr1, "wgrad_out_hg", low=True)
    d_wo_b, lo_wo_b = _wgrad(og_ml, dr1, "wgrad_out_ml", low=True)
    d_wg, lo_wg = _wgrad(dhg, x1b, "wgrad_ffn_gate", tk=D_FF // 2, low=True)
    d_wu, lo_wu = _wgrad(dup, x1b, "wgrad_ffn_up", tk=D_FF // 2, low=True)
    d_wd, lo_wd = _wgrad(act, dr2, "wgrad_ffn_down", tk=D_FF // 2, low=True)
    d_wpp, lo_wpp = _wgrad(p, de, "wgrad_ple_proj", low=True)
    d_wpg, lo_wpg = _wgrad(x2b, dz, "wgrad_ple_gate", low=True)
    early = dict(w_out=jnp.concatenate([d_wo_a, d_wo_b], axis=0), w_ffn_gate=d_wg, w_ffn_up=d_wu, w_ffn_down=d_wd,
                 ple_w_proj=d_wpp, ple_w_gate=d_wpg)
    early_low = dict(w_out=jnp.concatenate([lo_wo_a, lo_wo_b], axis=0), w_ffn_gate=lo_wg, w_ffn_up=lo_wu, w_ffn_down=lo_wd,
                     ple_w_proj=lo_wpp, ple_w_gate=lo_wpg)
    ride_hg, ride_ml = early_hook(early_low) if early_hook is not None else ((), ())

    (du_hg, d_logits, d_hg_gn), got_hg = _hgrn2_bwd(u_hg, logits, hg_gn, sst, hg_b, hg_a, hg_o, dog_hg, ride_hg)
    (dqkc, dmv, dmo, dgt, d_ml_gn), got_ml = _mlstm_bwd(qkc, u_ml, ml_gn, cst, nst, mst, dog_ml, ride_ml)
    dqk, d_conv_w, d_conv_b = _conv_bwd(u_ml, conv_w, pre, dqkc)
    grad_x, du_ml = _inproj_bwd(dr1, du_hg, dqk, dmv, dmo, dgt, w_hg, w_ml)

    dw_hg, db_hg = _wgrad(du_hg, xb, "wgrad_in_hg", tk=U_HG // 2, colsum=True)
    dw_ml, db_ml = _wgrad(du_ml, xb, "wgrad_in_ml", colsum=True)
    d_w_in = jnp.concatenate([dw_hg, dw_ml[:PROJ_W - U_HG]], axis=0)
    d_b_in = jnp.concatenate([db_hg, db_ml[:, :PROJ_W - U_HG]], axis=1)

    grads = dict(w_in=d_w_in, b_in=d_b_in, hg_lb_logits=d_logits, ml_conv_w=d_conv_w, ml_conv_b=d_conv_b,
                 hg_norm_g=d_hg_gn, ml_norm_g=d_ml_gn, ln1_g=d_ln1g, ln1_b=d_ln1b, ln2_g=d_ln2g, ln2_b=d_ln2b,
                 ple_b_gate=d_bpg, **early)
    return loss, grad_x, grads, (list(got_hg), list(got_ml))


_ANY = pl.BlockSpec(memory_space=pltpu.HBM)
_MESH = pl.DeviceIdType.MESH


def _my_place():
    return lax.axis_index("x"), lax.axis_index("y"), lax.axis_index("c")


def _other_chips(x, y):
    return [(1 - x, y), (x, 1 - y), (1 - x, 1 - y)]


_VMEM = pl.BlockSpec(memory_space=pltpu.VMEM)
_EX_ROWS = 32


def _pair_reduce_cols(p, name):
    s, r, c = p.shape
    hc = c // 2

    def body(p_ref, o_ref, other, send_sem, recv_sem):
        x, y, cc = _my_place()

        def run(mine_lo, theirs_lo):
            cp = pltpu.make_async_remote_copy(src_ref=p_ref.at[pl.ds(0, s), pl.ds(0, r), pl.ds(theirs_lo, hc)], dst_ref=other,
                                              send_sem=send_sem, recv_sem=recv_sem, device_id=(x, y, 1 - cc), device_id_type=_MESH)
            cp.start()
            cp.wait()
            for slot in range(s):
                o_ref[slot] = (p_ref[slot, :, mine_lo:mine_lo + hc] + other[slot]).astype(bf16)

        @pl.when(cc == 0)
        def _():
            run(0, hc)

        @pl.when(cc == 1)
        def _():
            run(hc, 0)

    return pl.pallas_call(
        body, name=name, in_specs=[_VMEM], out_specs=_VMEM,
        out_shape=jax.ShapeDtypeStruct((s, r, hc), bf16),
        scratch_shapes=[pltpu.VMEM((s, r, hc), f32), pltpu.SemaphoreType.DMA, pltpu.SemaphoreType.DMA],
        compiler_params=pltpu.CompilerParams(vmem_limit_bytes=VMEM_LIMIT),
    )(p)


def _reduce_adamw_cols(rcv, w, m, v, name):
    s, r, hc = rcv.shape

    def body(r_ref, w_ref, m_ref, v_ref, g_ref, d_ref, nm_ref, nv_ref, halves, send_sem, recv_sem):
        x, y, cc = _my_place()
        acc = r_ref[0].astype(f32)
        for slot in range(1, s):
            acc = acc + r_ref[slot].astype(f32)
        halves[cc] = acc
        cp = pltpu.make_async_remote_copy(src_ref=halves.at[cc], dst_ref=halves.at[cc], send_sem=send_sem, recv_sem=recv_sem,
                                          device_id=(x, y, 1 - cc), device_id_type=_MESH)
        cp.start()
        cp.wait()
        for k in range(2):
            cols = slice(k * hc, (k + 1) * hc)
            g = halves[k]
            nm = B1 * m_ref[:, cols] + (1.0 - B1) * g
            nv = B2 * v_ref[:, cols] + (1.0 - B2) * (g * g)
            g_ref[:, cols] = g
            nm_ref[:, cols] = nm
            nv_ref[:, cols] = nv
            d_ref[:, cols] = -LR * ((nm / (1.0 - B1 ** STEP)) / (jnp.sqrt(nv / (1.0 - B2 ** STEP)) + EPS_ADAM) + WD * w_ref[:, cols])

    return pl.pallas_call(
        body, name=name, in_specs=[_VMEM] * 4, out_specs=[_VMEM] * 4,
        out_shape=[jax.ShapeDtypeStruct((r, 2 * hc), f32)] * 4,
        scratch_shapes=[pltpu.VMEM((2, r, hc), f32), pltpu.SemaphoreType.DMA, pltpu.SemaphoreType.DMA],
        compiler_params=pltpu.CompilerParams(vmem_limit_bytes=VMEM_LIMIT),
    )(rcv, w, m, v)


def _reduce_adamw(rcv, w, m, v, name):
    s, r, c = rcv.shape
    rows_per = _EX_ROWS
    half = r // 2
    steps = half // rows_per

    def body(r_ref, w_ref, m_ref, v_ref, g_ref, d_ref, nm_ref, nv_ref, mine, theirs, send_sems, recv_sems):
        x, y, cc = _my_place()

        def swap(k):
            rs = pl.ds(k * half, half)
            return pltpu.make_async_remote_copy(src_ref=mine.at[rs], dst_ref=theirs.at[rs], send_sem=send_sems.at[k], recv_sem=recv_sems.at[k],
                                                device_id=(x, y, 1 - cc), device_id_type=_MESH)

        def chip_sum(i, carry):
            rs = pl.ds(pl.multiple_of(i * rows_per, rows_per), rows_per)
            acc = r_ref[0, rs, :].astype(f32)
            for slot in range(1, s):
                acc = acc + r_ref[slot, rs, :].astype(f32)
            mine[rs, :] = acc
            return carry

        def update(i, carry):
            rs = pl.ds(pl.multiple_of(i * rows_per, rows_per), rows_per)
            g = mine[rs, :] + theirs[rs, :]
            nm = B1 * m_ref[rs, :] + (1.0 - B1) * g
            nv = B2 * v_ref[rs, :] + (1.0 - B2) * (g * g)
            g_ref[rs, :] = g
            nm_ref[rs, :] = nm
            nv_ref[rs, :] = nv
            d_ref[rs, :] = -LR * ((nm / (1.0 - B1 ** STEP)) / (jnp.sqrt(nv / (1.0 - B2 ** STEP)) + EPS_ADAM) + WD * w_ref[rs, :])
            return carry

        lax.fori_loop(0, steps, chip_sum, 0)
        swap(0).start()
        lax.fori_loop(steps, 2 * steps, chip_sum, 0)
        swap(1).start()
        swap(0).wait()
        lax.fori_loop(0, steps, update, 0)
        swap(1).wait()
        lax.fori_loop(steps, 2 * steps, update, 0)

    return pl.pallas_call(
        body, name=name, in_specs=[_VMEM] * 4, out_specs=[_VMEM] * 4,
        out_shape=[jax.ShapeDtypeStruct((r, c), f32)] * 4,
        scratch_shapes=[pltpu.VMEM((r, c), f32), pltpu.VMEM((r, c), f32), pltpu.SemaphoreType.DMA((2,)), pltpu.SemaphoreType.DMA((2,))],
        compiler_params=pltpu.CompilerParams(vmem_limit_bytes=VMEM_LIMIT),
    )(rcv, w, m, v)


def _gather_copies(ins, outs, send_sems, recv_sems, local_sems):
    x, y, c = _my_place()
    me = 2 * x + y
    local, outgoing, incoming = [], [], []
    for a in range(len(ins)):
        local.append(pltpu.make_async_copy(ins[a], outs[a].at[me], local_sems.at[a]))
        for j, (px, py) in enumerate(_other_chips(x, y)):
            sems = dict(send_sem=send_sems.at[3 * a + j], recv_sem=recv_sems.at[3 * a + j], device_id=(px, py, c), device_id_type=_MESH)
            outgoing.append(pltpu.make_async_remote_copy(src_ref=ins[a], dst_ref=outs[a].at[me], **sems))
            incoming.append(pltpu.make_async_remote_copy(src_ref=ins[a], dst_ref=outs[a].at[2 * px + py], **sems))
    return local, outgoing, incoming


def _gather_first(block, taps, name):
    r, c = block.shape
    hc = c // 2

    def body(in_ref, tap_in, out_ref, tap_out, send_sems, recv_sems):
        x, y, cc = _my_place()
        me = 2 * x + y
        sibling = (x, y, 1 - cc)
        chips = _other_chips(x, y)
        out_ref[me] = in_ref[...]
        tap_out[me] = tap_in[...]

        def run(mine, theirs):
            def ici(j, chip):
                px, py = chips[j]
                src = in_ref.at[pl.ds(0, r), pl.ds(mine, hc)] if chip is None else out_ref.at[chip, pl.ds(0, r), pl.ds(mine, hc)]
                dst = out_ref.at[me if chip is None else chip, pl.ds(0, r), pl.ds(mine, hc)]
                return pltpu.make_async_remote_copy(src_ref=src, dst_ref=dst, send_sem=send_sems.at[j], recv_sem=recv_sems.at[j],
                                                    device_id=(px, py, cc), device_id_type=_MESH)

            def d2d(j, lo):
                px, py = chips[j]
                blk = out_ref.at[2 * px + py, pl.ds(0, r), pl.ds(lo, hc)]
                return pltpu.make_async_remote_copy(src_ref=blk, dst_ref=blk, send_sem=send_sems.at[3 + j], recv_sem=recv_sems.at[3 + j],
                                                    device_id=sibling, device_id_type=_MESH)

            def tap(j, chip):
                px, py = chips[j]
                return pltpu.make_async_remote_copy(src_ref=tap_in, dst_ref=tap_out.at[me if chip is None else chip],
                                                    send_sem=send_sems.at[6 + j], recv_sem=recv_sems.at[6 + j],
                                                    device_id=(px, py, cc), device_id_type=_MESH)

            for j in range(3):
                ici(j, None).start()
                tap(j, None).start()
            for j, (px, py) in enumerate(chips):
                ici(j, 2 * px + py).wait_recv()
                d2d(j, mine).start()
            for j, (px, py) in enumerate(chips):
                d2d(j, theirs).wait_recv()
                tap(j, 2 * px + py).wait_recv()
            for j in range(3):
                ici(j, None).wait_send()
                d2d(j, mine).wait_send()
                tap(j, None).wait_send()

        @pl.when(cc == 0)
        def _():
            run(0, hc)

        @pl.when(cc == 1)
        def _():
            run(hc, 0)

    return pl.pallas_call(
        body, name=name, in_specs=[_VMEM, _VMEM], out_specs=[_VMEM, _VMEM],
        out_shape=[jax.ShapeDtypeStruct((4, r, c), block.dtype), jax.ShapeDtypeStruct((4,) + taps.shape, taps.dtype)],
        scratch_shapes=[pltpu.SemaphoreType.DMA((9,)), pltpu.SemaphoreType.DMA((9,))],
        compiler_params=pltpu.CompilerParams(vmem_limit_bytes=VMEM_LIMIT),
    )(block, taps)


def _riding_call(body, name, nsteps, in_specs, out_specs, out_shape, scratch_shapes, operands, riders, copies, ride_shapes):
    nr, n_in, n_out, n_scr = len(riders), len(in_specs), len(out_specs), len(scratch_shapes)

    def wrapped(*refs):
        ins, ride_in = refs[:n_in], refs[n_in:n_in + nr]
        outs, ride_out = refs[n_in + nr:n_in + nr + n_out], refs[n_in + nr + n_out:n_in + 2 * nr + n_out]
        scratch, sems = refs[n_in + 2 * nr + n_out:n_in + 2 * nr + n_out + n_scr], refs[n_in + 2 * nr + n_out + n_scr:]
        if nr:
            @pl.when(pl.program_id(0) == 0)
            def _():
                local, outgoing, _ = copies(ride_in, ride_out, *sems)
                for cp in local + outgoing:
                    cp.start()

        body(*ins, *outs, *scratch)
        if nr:
            @pl.when(pl.program_id(0) == nsteps - 1)
            def _():
                local, outgoing, incoming = copies(ride_in, ride_out, *sems)
                for cp in incoming:
                    cp.wait_recv()
                for cp in outgoing:
                    cp.wait_send()
                for cp in local:
                    cp.wait()

    hbm = pl.BlockSpec(memory_space=pltpu.HBM)
    sems = [pltpu.SemaphoreType.DMA((3 * nr,)), pltpu.SemaphoreType.DMA((3 * nr,)), pltpu.SemaphoreType.DMA((nr,))] if nr else []
    res = pl.pallas_call(
        wrapped, name=name, grid=(nsteps,),
        in_specs=list(in_specs) + [hbm] * nr, out_specs=list(out_specs) + [hbm] * nr,
        out_shape=list(out_shape) + list(ride_shapes),
        scratch_shapes=list(scratch_shapes) + sems,
        compiler_params=_cparams(1),
    )(*operands, *riders)
    return list(res[:n_out]), list(res[n_out:])


def _gather_shapes(riders):
    return [jax.ShapeDtypeStruct((4,) + r.shape, r.dtype) for r in riders]


def _scatter_copies(ins, outs, send_sems, recv_sems, local_sems):
    x, y, c = _my_place()
    me = 2 * x + y
    local, outgoing, incoming = [], [], []
    for a in range(len(ins)):
        local.append(pltpu.make_async_copy(ins[a].at[me], outs[a].at[me], local_sems.at[a]))
        for j, (px, py) in enumerate(_other_chips(x, y)):
            sems = dict(send_sem=send_sems.at[3 * a + j], recv_sem=recv_sems.at[3 * a + j], device_id=(px, py, c), device_id_type=_MESH)
            outgoing.append(pltpu.make_async_remote_copy(src_ref=ins[a].at[2 * px + py], dst_ref=outs[a].at[me], **sems))
            incoming.append(pltpu.make_async_remote_copy(src_ref=ins[a].at[2 * px + py], dst_ref=outs[a].at[2 * px + py], **sems))
    return local, outgoing, incoming


def _scatter_chips(pieces, name):
    n = len(pieces)

    def body(*refs):
        local, outgoing, incoming = _scatter_copies(refs[:n], refs[n:2 * n], *refs[2 * n:])
        for cp in local + outgoing:
            cp.start()
        for cp in incoming:
            cp.wait_recv()
        for cp in outgoing:
            cp.wait_send()
        for cp in local:
            cp.wait()

    return pl.pallas_call(
        body, name=name,
        in_specs=[_ANY] * n, out_specs=[_ANY] * n,
        out_shape=[jax.ShapeDtypeStruct(s.shape, s.dtype) for s in pieces],
        scratch_shapes=[pltpu.SemaphoreType.DMA((3 * n,)), pltpu.SemaphoreType.DMA((3 * n,)), pltpu.SemaphoreType.DMA((n,))],
    )(*pieces)


def _gather_all(block, name):
    def body(in_ref, out_ref, send_sems, recv_sems, local_sem):
        x, y, c = _my_place()
        me = 4 * x + 2 * y + c
        cp = pltpu.make_async_copy(in_ref, out_ref.at[me], local_sem)
        cp.start()
        peers = []
        for dx in range(2):
            for dy in range(2):
                for dc in range(2):
                    if dx or dy or dc:
                        peers.append((1 - x if dx else x, 1 - y if dy else y, 1 - c if dc else c))
        for j, pr in enumerate(peers):
            pltpu.make_async_remote_copy(src_ref=in_ref, dst_ref=out_ref.at[me], send_sem=send_sems.at[j], recv_sem=recv_sems.at[j],
                                         device_id=pr, device_id_type=_MESH).start()
        for j, (px, py, pc) in enumerate(peers):
            pltpu.make_async_remote_copy(src_ref=in_ref, dst_ref=out_ref.at[4 * px + 2 * py + pc], send_sem=send_sems.at[j], recv_sem=recv_sems.at[j],
                                         device_id=(px, py, pc), device_id_type=_MESH).wait()
        cp.wait()

    return pl.pallas_call(
        body, name=name,
        in_specs=[_ANY], out_specs=_ANY,
        out_shape=jax.ShapeDtypeStruct((8,) + block.shape, block.dtype),
        scratch_shapes=[pltpu.SemaphoreType.DMA((7,)), pltpu.SemaphoreType.DMA((7,)), pltpu.SemaphoreType.DMA],
    )(block)


def _row_tile(r, c):
    best = r
    for cand in range(16, r + 1, 16):
        if r % cand == 0 and cand * c * 4 <= (1 << 20):
            best = cand
    return best if best * c * 4 <= (4 << 20) else r


def _sum_slots(parts, name):
    n, r, c = parts.shape
    tr = _row_tile(r, c)

    def body(p_ref, o_ref):
        acc = p_ref[0].astype(f32)
        for s in range(1, n):
            acc = acc + p_ref[s].astype(f32)
        o_ref[...] = acc

    return pl.pallas_call(
        body, name=name, grid=(r // tr,),
        in_specs=[pl.BlockSpec((n, tr, c), lambda i: (0, i, 0))],
        out_specs=pl.BlockSpec((tr, c), lambda i: (i, 0)),
        out_shape=jax.ShapeDtypeStruct((r, c), f32),
        compiler_params=_cparams(1, arbitrary=False),
    )(parts)


def _adamw(parts, w, m, v, name):
    n, r, c = parts.shape
    tr = _row_tile(r, c)
    tc = c
    if tr == r and r * c * 4 > (1 << 20) and c % 256 == 0:
        tc = 256

    def body(p_ref, w_ref, m_ref, v_ref, g_ref, d_ref, nm_ref, nv_ref):
        g = p_ref[0]
        for s in range(1, n):
            g = g + p_ref[s]
        nm = B1 * m_ref[...] + (1.0 - B1) * g
        nv = B2 * v_ref[...] + (1.0 - B2) * (g * g)
        m_hat = nm / (1.0 - B1 ** STEP)
        v_hat = nv / (1.0 - B2 ** STEP)
        g_ref[...] = g
        nm_ref[...] = nm
        nv_ref[...] = nv
        d_ref[...] = -LR * (m_hat / (jnp.sqrt(v_hat) + EPS_ADAM) + WD * w_ref[...])

    blk = pl.BlockSpec((tr, tc), lambda i, j: (i, j))
    return pl.pallas_call(
        body, name=name, grid=(r // tr, c // tc),
        in_specs=[pl.BlockSpec((n, tr, tc), lambda i, j: (0, i, j)), blk, blk, blk],
        out_specs=[blk] * 4,
        out_shape=[jax.ShapeDtypeStruct((r, c), f32)] * 4,
        compiler_params=_cparams(2, arbitrary=False),
    )(parts, w, m, v)


_BIG = ["w_in", "w_out", "w_ffn_gate", "w_ffn_up", "w_ffn_down", "ple_w_proj", "ple_w_gate"]
_SMALL = ["b_in", "hg_lb_logits", "ml_conv_w", "ml_conv_b", "hg_norm_g", "ml_norm_g", "ln1_g", "ln1_b", "ln2_g", "ln2_b", "ple_b_gate"]
_ORDER = ["w_in", "b_in", "hg_lb_logits", "ml_conv_w", "ml_conv_b", "hg_norm_g", "ml_norm_g", "w_out", "ln1_g", "ln1_b",
          "w_ffn_gate", "w_ffn_up", "w_ffn_down", "ln2_g", "ln2_b", "ple_w_proj", "ple_w_gate", "ple_b_gate"]
_PACK_ROWS, _PACK_COLS = 16, 1024


def _pack(arrays):
    flat = jnp.concatenate([a.reshape(-1) for a in arrays])
    return jnp.pad(flat, (0, _PACK_ROWS * _PACK_COLS - flat.shape[0])).reshape(_PACK_ROWS, _PACK_COLS)


def _unpack(pack, shapes):
    flat = pack.reshape(-1)
    out, off = [], 0
    for s in shapes:
        size = 1
        for d in s:
            size *= d
        out.append(flat[off:off + size].reshape(s))
        off += size
    return out


def _to_chip_major(g, col_split):
    if col_split:
        k, n = g.shape
        return g.reshape(k, 4, n // 4).transpose(1, 0, 2)
    k, n = g.shape
    return g.reshape(4, k // 4, n)


def kernel(x, p, w_in, b_in, hg_lb_logits, ml_conv_w, ml_conv_b, hg_norm_g, ml_norm_g, w_out, ln1_g, ln1_b, w_ffn_gate, w_ffn_up, w_ffn_down, ln2_g, ln2_b, ple_w_proj, ple_w_gate, ple_b_gate, loss_target, m_w_in, m_b_in, m_hg_lb_logits, m_ml_conv_w, m_ml_conv_b, m_hg_norm_g, m_ml_norm_g, m_w_out, m_ln1_g, m_ln1_b, m_w_ffn_gate, m_w_ffn_up, m_w_ffn_down, m_ln2_g, m_ln2_b, m_ple_w_proj, m_ple_w_gate, m_ple_b_gate, v_w_in, v_b_in, v_hg_lb_logits, v_ml_conv_w, v_ml_conv_b, v_hg_norm_g, v_ml_norm_g, v_w_out, v_ln1_g, v_ln1_b, v_w_ffn_gate, v_w_ffn_up, v_w_ffn_down, v_ln2_g, v_ln2_b, v_ple_w_proj, v_ple_w_gate, v_ple_b_gate):
    args = dict(locals())
    wts = {k: args[k] for k in _ORDER}
    mom = {k: args["m_" + k] for k in _ORDER}
    var = {k: args["v_" + k] for k in _ORDER}
    two_d = lambda a: a.reshape(a.shape[-2], a.shape[-1])
    block = lambda k, a: jnp.swapaxes(two_d(a), 0, 1) if k in _TRANSPOSED else two_d(a)
    unblock = lambda k, a: (jnp.swapaxes(a, 0, 1) if k in _TRANSPOSED else a).reshape(wts[k].shape)

    shards = {k: block(k, wts[k]).astype(bf16) for k in _BIG}
    w_in_blocks, taps = _gather_first(shards["w_in"], two_d(ml_conv_w), "gather_w_in")
    w_in_full = _from_chip_major(w_in_blocks, False)
    conv_w_full = _from_chip_major(taps, True)

    early_keys = _BIG[1:]
    loss, grad_x, grads, (got_hg, got_ml) = _local_step(
        x[0], p[0, 0], loss_target[0], w_in_full, b_in, hg_lb_logits, conv_w_full, ml_conv_b, hg_norm_g, ml_norm_g,
        None, ln1_g, ln1_b, None, None, None, ln2_g, ln2_b, None, None, ple_b_gate,
        early_hook=lambda low: tuple([_to_chip_major(low[k], k in _COL_SPLIT) for k in names] for names in _SCATTER_PLAN),
        late_shards={k: shards[k] for k in early_keys})

    out_g, out_d, out_m, out_v = {}, {}, {}, {}

    def finish(k, g, d, nm, nv):
        out_g[k], out_d[k], out_m[k], out_v[k] = unblock(k, g), unblock(k, d), unblock(k, nm), unblock(k, nv)

    for names, got in zip(_SCATTER_PLAN, (got_hg, got_ml)):
        for k, rcv in zip(names, got):
            finish(k, *_reduce_adamw(rcv, block(k, wts[k]), block(k, mom[k]), block(k, var[k]), "reduce_adamw_" + k))

    core_sums = _pair_reduce_cols(_to_chip_major(grads["w_in"], False), "pair_reduce_w_in")
    rcv = _scatter_chips([core_sums], "scatter_grad_w_in")[0]
    finish("w_in", *_reduce_adamw_cols(rcv, block("w_in", wts["w_in"]), block("w_in", mom["w_in"]), block("w_in", var["w_in"]),
                                       "reduce_adamw_w_in"))

    small_shapes = [(1, PROJ_W), (2, MIX_W), (CONV_K, MIX_W)] + [(1, MIX_W)] * 3 + [(1, D_MODEL)] * 5 + [(1, 1)]
    contrib = _pack([grads[k] for k in _SMALL] + [loss])
    summed = _sum_slots(_gather_all(contrib, "gather_small"), "sum_small")
    small = _unpack(summed, small_shapes)
    loss_total = small[-1].reshape(())
    gsm = dict(zip(_SMALL, small[:-1]))
    place = 2 * lax.axis_index("x") + lax.axis_index("y")
    conv_cols = ml_conv_w.shape[-1]
    gsm["ml_conv_w"] = lax.dynamic_slice(gsm["ml_conv_w"], (0, place * conv_cols), (CONV_K, conv_cols))
    own_shapes = [wts[k].shape for k in _SMALL]
    g_pack = _pack([gsm[k] for k in _SMALL])
    res = _adamw(g_pack[None], _pack([wts[k] for k in _SMALL]), _pack([mom[k] for k in _SMALL]), _pack([var[k] for k in _SMALL]), "adamw_small")
    for dst, pack in zip((out_g, out_d, out_m, out_v), res):
        for k, a in zip(_SMALL, _unpack(pack, own_shapes)):
            dst[k] = a

    outs = [loss_total, grad_x[None]]
    for group in (out_g, out_d, out_m, out_v):
        outs += [group[k] for k in _ORDER]
    return tuple(outs)
```

```python
import jax
import jax.numpy as jnp
from jax import lax
from jax.experimental import pallas as pl
from jax.experimental.pallas import tpu as pltpu

f32 = jnp.float32
bf16 = jnp.bfloat16

D_MODEL = 1024
HEADS = 4
HEAD_W = 128
MIX_W = HEADS * HEAD_W
ML_DQK = 64
PROJ_W = 3592
U_HG = 4 * MIX_W
U_ML = 3 * MIX_W + 128
D_FF = 2816
PLE = 256
CHUNK = 128
SUB = 16
EXP_CAP = 80.0
CONV_K = 4
HALO = 8
ALPHA = float(2.0 ** 0.25)
LN_EPS = 1e-5
RMS_EPS = 1e-6
NEG = -1e30
LR, B1, B2, EPS_ADAM, WD, STEP = 0.001, 0.9, 0.999, 1e-08, 0.01, 10
VMEM_LIMIT = 56 * 1024 * 1024
MIXER_ROWS = 512
DENSE_ROWS = 512
LIGHT_ROWS = 1024
WGRAD_ROWS = 2048


def _cparams(n_axes, arbitrary=True):
    sem = ("arbitrary",) * n_axes if arbitrary else ("parallel",) * n_axes
    return pltpu.CompilerParams(dimension_semantics=sem, vmem_limit_bytes=VMEM_LIMIT)


ACT = bf16


def _mx(a):
    return a.astype(ACT)


def _bdot(a, b):
    return jnp.dot(_mx(a), _mx(b), preferred_element_type=f32)


def _bdot_nt(a, b):
    return lax.dot_general(_mx(a), _mx(b), (((1,), (1,)), ((), ())), preferred_element_type=f32)


def _bdot_tn(a, b):
    return lax.dot_general(_mx(a), _mx(b), (((0,), (0,)), ((), ())), preferred_element_type=f32)


def _split3(x):
    hi = x.astype(bf16)
    r1 = x - hi.astype(f32)
    mid = r1.astype(bf16)
    lo = (r1 - mid.astype(f32)).astype(bf16)
    return hi, mid, lo


def _dot3(a, b, dims):
    a_hi = a.astype(bf16)
    a_lo = (a - a_hi.astype(f32)).astype(bf16)
    b_hi = b.astype(bf16)
    b_lo = (b - b_hi.astype(f32)).astype(bf16)
    dn = (dims, ((), ()))
    return (lax.dot_general(a_hi, b_hi, dn, preferred_element_type=f32) + lax.dot_general(a_hi, b_lo, dn, preferred_element_type=f32)
            + lax.dot_general(a_lo, b_hi, dn, preferred_element_type=f32))


def _lane_sum(x):
    hi = x.astype(bf16)
    lo = (x - hi.astype(f32)).astype(bf16)
    ones = jnp.ones((x.shape[1], 128), bf16)
    return jnp.dot(hi, ones, preferred_element_type=f32) + jnp.dot(lo, ones, preferred_element_type=f32)


def _lane_dot(x, row):
    return _dot3(x, jnp.broadcast_to(row, (128, row.shape[1])), ((1,), (1,)))


def _sel_dot(sel, x):
    sb = sel.astype(bf16)
    return sum(jnp.dot(sb, part, preferred_element_type=f32) for part in _split3(x))


def _sel_dot_nt(sel, x):
    sb = sel.astype(bf16)
    return sum(lax.dot_general(sb, part, (((1,), (1,)), ((), ())), preferred_element_type=f32) for part in _split3(x))


def _sigmoid(x):
    return 1.0 / (1.0 + jnp.exp(-x))


def _log_sigmoid(x):
    return jnp.minimum(x, 0.0) - jnp.log(1.0 + jnp.exp(-jnp.abs(x)))


def _tri(n, upper=False):
    r = lax.broadcasted_iota(jnp.int32, (n, n), 0)
    c = lax.broadcasted_iota(jnp.int32, (n, n), 1)
    return (c >= r) if upper else (c <= r)


def _rows(tm, n, col=0):
    return pl.BlockSpec((tm, n), lambda i, _c=col: (i, _c))


def _rows_rev(tm, n, nb, col=0):
    return pl.BlockSpec((tm, n), lambda i, _c=col, _nb=nb: (_nb - 1 - i, _c))


def _const(shape):
    return pl.BlockSpec(shape, lambda i, _n=len(shape): (0,) * _n)


def _resident(shape):
    return pl.BlockSpec(shape, lambda i, _n=len(shape): (0,) * _n, pipeline_mode=pl.Buffered(1))


def _tile(t, want):
    return want if t % want == 0 else t


def _inproj(x, w_hg, w_ml, b_hg, b_ml, riders=()):
    t = x.shape[0]
    tm = _tile(t, DENSE_ROWS)

    def body(x_ref, whg_ref, wml_ref, bhg_ref, bml_ref, uhg_ref, uml_ref, xb_ref):
        xb = _mx(x_ref[...])
        xb_ref[...] = xb
        uhg_ref[...] = _bdot_nt(xb, whg_ref[...]) + bhg_ref[...]
        uml_ref[...] = _bdot_nt(xb, wml_ref[...]) + bml_ref[...]

    return _riding_call(
        body, "inproj", t // tm,
        in_specs=[_rows(tm, D_MODEL), _resident((U_HG, D_MODEL)), _resident((U_ML, D_MODEL)), _const((1, U_HG)), _const((1, U_ML))],
        out_specs=[_rows(tm, U_HG), _rows(tm, U_ML), _rows(tm, D_MODEL)],
        out_shape=[jax.ShapeDtypeStruct((t, U_HG), f32), jax.ShapeDtypeStruct((t, U_ML), f32), jax.ShapeDtypeStruct((t, D_MODEL), ACT)],
        scratch_shapes=[], operands=(x, w_hg, w_ml, b_hg, b_ml), riders=riders, copies=_gather_copies, ride_shapes=_gather_shapes(riders))


def _hg_gates(hq, hf, lb, tri, b=None):
    s = _sigmoid(hf)
    om = 1.0 - lb
    f = lb + om * s
    k = om * (1.0 - s)
    sq = _sigmoid(hq)
    q = hq * sq
    if b is None:
        b = _sel_dot(tri, jnp.log(f))
    return q, sq, s, f, k, b


def _hg_scores(q, k, b, tril_mask, a=None):
    qts, kts, eqs, eks, rows = [], [], [], [], []
    for i in range(CHUNK // SUB):
        lo = i * SUB
        ref = jnp.zeros_like(b[0:1]) if i == 0 else b[lo - 1:lo]
        eq = jnp.exp(b[lo:lo + SUB] - ref)
        ek = jnp.exp(jnp.minimum(ref - b, EXP_CAP))
        qt = q[lo:lo + SUB] * eq
        kt = k * ek
        if a is None:
            rows.append(_bdot_nt(qt, kt))
        qts.append(qt); kts.append(kt); eqs.append(eq); eks.append(ek)
    if a is None:
        a = jnp.where(tril_mask, jnp.concatenate(rows, axis=0), 0.0)
    return a, qts, kts, eqs, eks


def _head_rms(o, gn, on_mxu=False):
    ms = _lane_sum(o * o) * (1.0 / o.shape[1]) if on_mxu else jnp.mean(o * o, axis=-1, keepdims=True)
    rstd = lax.rsqrt(ms + RMS_EPS)
    oh = o * rstd
    return oh, rstd, oh * gn


def _lower_bound(logit_ref):
    lg = logit_ref[...]
    return _sigmoid(lg[0:1] - lg[1:2])


def _hgrn2_fwd(u_hg, logits, gn, riders=()):
    t = u_hg.shape[0]
    tb = _tile(t, MIXER_ROWS)
    nc_blk = tb // CHUNK

    def body(u_ref, lg_ref, gn_ref, og_ref, sst_ref, b_ref, a_ref, o_ref, st_ref):
        @pl.when(pl.program_id(0) == 0)
        def _():
            st_ref[...] = jnp.zeros_like(st_ref)

        lb_all = _lower_bound(lg_ref)
        tril_mask = _tri(CHUNK)
        tri = tril_mask.astype(f32)

        def chunk(c, carry):
            r0 = pl.multiple_of(c * CHUNK, CHUNK)
            rows = pl.ds(r0, CHUNK)
            heads = range(HEADS)
            cols = [slice(h * HEAD_W, (h + 1) * HEAD_W) for h in heads]
            hv = [u_ref[rows, 2 * MIX_W + h * HEAD_W:2 * MIX_W + (h + 1) * HEAD_W] for h in heads]
            gts = [_hg_gates(u_ref[rows, h * HEAD_W:(h + 1) * HEAD_W], u_ref[rows, MIX_W + h * HEAD_W:MIX_W + (h + 1) * HEAD_W],
                             lb_all[:, cols[h]], tri) for h in heads]
            q = [g[0] for g in gts]
            k = [g[4] for g in gts]
            b = [g[5] for g in gts]
            a = [_hg_scores(q[h], k[h], b[h], tril_mask)[0] for h in heads]
            st = [st_ref[h] for h in heads]
            bl = [b[h][CHUNK - 1:CHUNK] for h in heads]
            o = [_bdot(a[h], hv[h]) + _bdot_nt(q[h] * jnp.exp(b[h]), st[h]) for h in heads]
            new_st = [st[h] * jnp.exp(bl[h]) + _bdot_tn(hv[h], k[h] * jnp.exp(bl[h] - b[h])) for h in heads]
            for h in heads:
                sst_ref[c, h] = st[h]
                st_ref[h] = new_st[h]
                b_ref[rows, cols[h]] = b[h]
                a_ref[rows, cols[h]] = a[h].astype(ACT)
                o_ref[rows, cols[h]] = o[h]
                hgate = u_ref[rows, 3 * MIX_W + h * HEAD_W:3 * MIX_W + (h + 1) * HEAD_W]
                _, _, y = _head_rms(o[h], gn_ref[:, cols[h]])
                og_ref[rows, cols[h]] = (y * (hgate * _sigmoid(hgate))).astype(ACT)
            return carry

        lax.fori_loop(0, nc_blk, chunk, 0, unroll=True)

    assert CHUNK == HEAD_W
    return _riding_call(
        body, "hgrn2_fwd", t // tb,
        in_specs=[_rows(tb, U_HG), _const((2, MIX_W)), _const((1, MIX_W))],
        out_specs=[_rows(tb, MIX_W), pl.BlockSpec((nc_blk, HEADS, HEAD_W, HEAD_W), lambda i: (i, 0, 0, 0)),
                   _rows(tb, MIX_W), _rows(tb, MIX_W), _rows(tb, MIX_W)],
        out_shape=[jax.ShapeDtypeStruct((t, MIX_W), ACT), jax.ShapeDtypeStruct((t // CHUNK, HEADS, HEAD_W, HEAD_W), f32),
                   jax.ShapeDtypeStruct((t, MIX_W), f32), jax.ShapeDtypeStruct((t, MIX_W), ACT), jax.ShapeDtypeStruct((t, MIX_W), f32)],
        scratch_shapes=[pltpu.VMEM((HEADS, HEAD_W, HEAD_W), f32)],
        operands=(u_hg, logits, gn), riders=riders, copies=_gather_copies, ride_shapes=_gather_shapes(riders))


def _hgrn2_bwd(u_hg, logits, gn, sst, bcum, scores, o_raw, dog, riders=()):
    t = u_hg.shape[0]
    tb = _tile(t, MIXER_ROWS)
    nb = t // tb
    nc_blk = tb // CHUNK

    def body(u_ref, lg_ref, gn_ref, sst_ref, b_ref, a_ref, o_ref, dog_ref, du_ref, dlg_ref, dgn_ref, dst_ref):
        @pl.when(pl.program_id(0) == 0)
        def _():
            dst_ref[...] = jnp.zeros_like(dst_ref)
            dlg_ref[...] = jnp.zeros_like(dlg_ref)
            dgn_ref[...] = jnp.zeros_like(dgn_ref)

        lb_all = _lower_bound(lg_ref)
        tril_mask = _tri(CHUNK)
        tri = tril_mask.astype(f32)
        triu = _tri(CHUNK, upper=True).astype(f32)

        def chunk(j, carry):
            c = nc_blk - 1 - j
            r0 = pl.multiple_of(c * CHUNK, CHUNK)
            rows = pl.ds(r0, CHUNK)
            heads = range(HEADS)
            nsub = CHUNK // SUB
            cols = [slice(h * HEAD_W, (h + 1) * HEAD_W) for h in heads]
            hq = [u_ref[rows, h * HEAD_W:(h + 1) * HEAD_W] for h in heads]
            hf = [u_ref[rows, MIX_W + h * HEAD_W:MIX_W + (h + 1) * HEAD_W] for h in heads]
            hv = [u_ref[rows, 2 * MIX_W + h * HEAD_W:2 * MIX_W + (h + 1) * HEAD_W] for h in heads]
            lb = [lb_all[:, cols[h]] for h in heads]
            gts = [_hg_gates(hq[h], hf[h], lb[h], tri, b=b_ref[rows, cols[h]]) for h in heads]
            q, sq, s, f, k, b = ([g[n] for g in gts] for n in range(6))
            scs = [_hg_scores(q[h], k[h], b[h], tril_mask, a=a_ref[rows, cols[h]]) for h in heads]
            a, qts, kts, eqs, eks = ([sc[n] for sc in scs] for n in range(5))
            st = [sst_ref[c, h] for h in heads]
            dst = [dst_ref[h] for h in heads]
            bl = [b[h][CHUNK - 1:CHUNK] for h in heads]
            eb = [jnp.exp(b[h]) for h in heads]
            qh = [q[h] * eb[h] for h in heads]
            ekl = [jnp.exp(bl[h] - b[h]) for h in heads]
            kh = [k[h] * ekl[h] for h in heads]
            o = [o_ref[rows, cols[h]] for h in heads]
            do = []
            for h in heads:
                hgate = u_ref[rows, 3 * MIX_W + h * HEAD_W:3 * MIX_W + (h + 1) * HEAD_W]
                gnh = gn_ref[:, cols[h]]
                oh, rstd, y = _head_rms(o[h], gnh)
                sg = _sigmoid(hgate)
                dogh = dog_ref[rows, cols[h]]
                dy = dogh * (hgate * sg)
                du_ref[rows, 3 * MIX_W + h * HEAD_W:3 * MIX_W + (h + 1) * HEAD_W] = (dogh * y * (sg * (1.0 + hgate * (1.0 - sg)))).astype(ACT)
                dgn_ref[:, cols[h]] += jnp.sum(dy * oh, axis=0, keepdims=True)
                doh = dy * gnh
                do.append(rstd * (doh - oh * jnp.mean(doh * oh, axis=-1, keepdims=True)))
            da = [jnp.where(tril_mask, _bdot_nt(do[h], hv[h]), 0.0) for h in heads]
            dv = [_bdot_tn(a[h], do[h]) + _bdot_nt(kh[h], dst[h]) for h in heads]
            dq = [_bdot(do[h], st[h]) * eb[h] for h in heads]
            dk = [_bdot(hv[h], dst[h]) * ekl[h] for h in heads]
            d_last = [jnp.sum(k[h] * dk[h], axis=0, keepdims=True) + jnp.exp(bl[h]) * jnp.sum(dst[h] * st[h], axis=0, keepdims=True)
                      for h in heads]
            d_b = [q[h] * dq[h] - k[h] * dk[h] for h in heads]
            dqs = [[] for _ in heads]
            q_dq = [[] for _ in heads]
            for i in range(nsub):
                for h in heads:
                    da_i = _mx(da[h][i * SUB:(i + 1) * SUB])
                    q_r, k_r = _mx(qts[h][i]), _mx(kts[h][i])
                    g_q = jnp.dot(da_i, k_r, preferred_element_type=f32)
                    g_k = lax.dot_general(da_i, q_r, (((0,), (0,)), ((), ())), preferred_element_type=f32)
                    dqs[h].append(g_q * eqs[h][i])
                    q_dq[h].append(q_r.astype(f32) * g_q)
                    dk[h] = dk[h] + g_k * eks[h][i]
                    d_b[h] = d_b[h] - k_r.astype(f32) * g_k
            for h in heads:
                dq[h] = dq[h] + jnp.concatenate(dqs[h], axis=0)
                d_b[h] = d_b[h] + jnp.concatenate(q_dq[h], axis=0)
                dst_ref[h] = dst[h] * jnp.exp(bl[h]) + _bdot_tn(do[h], qh[h])
            dg = [_sel_dot(triu, d_b[h]) + d_last[h] for h in heads]
            for h in heads:
                dfk = dg[h] / f[h] - dk[h]
                du_ref[rows, h * HEAD_W:(h + 1) * HEAD_W] = (dq[h] * (sq[h] * (1.0 + hq[h] * (1.0 - sq[h])))).astype(ACT)
                du_ref[rows, MIX_W + h * HEAD_W:MIX_W + (h + 1) * HEAD_W] = ((1.0 - lb[h]) * dfk * s[h] * (1.0 - s[h])).astype(ACT)
                du_ref[rows, 2 * MIX_W + h * HEAD_W:2 * MIX_W + (h + 1) * HEAD_W] = dv[h].astype(ACT)
                dlb = jnp.sum((1.0 - s[h]) * dfk, axis=0, keepdims=True) * (lb[h] * (1.0 - lb[h]))
                dlg_ref[0:1, cols[h]] += dlb
                dlg_ref[1:2, cols[h]] -= dlb
            return carry

        lax.fori_loop(0, nc_blk, chunk, 0, unroll=True)

    rev = _rows_rev(tb, MIX_W, nb)
    return _riding_call(
        body, "hgrn2_bwd", nb,
        in_specs=[_rows_rev(tb, U_HG, nb), _const((2, MIX_W)), _const((1, MIX_W)),
                  pl.BlockSpec((nc_blk, HEADS, HEAD_W, HEAD_W), lambda i: (nb - 1 - i, 0, 0, 0)), rev, rev, rev, rev],
        out_specs=[_rows_rev(tb, U_HG, nb), _const((2, MIX_W)), _const((1, MIX_W))],
        out_shape=[jax.ShapeDtypeStruct((t, U_HG), ACT), jax.ShapeDtypeStruct((2, MIX_W), f32), jax.ShapeDtypeStruct((1, MIX_W), f32)],
        scratch_shapes=[pltpu.VMEM((HEADS, HEAD_W, HEAD_W), f32)],
        operands=(u_hg, logits, gn, sst, bcum, scores, o_raw, dog), riders=riders, copies=_scatter_copies,
        ride_shapes=[jax.ShapeDtypeStruct(r.shape, r.dtype) for r in riders])


def _conv_fwd(u_ml, w, b):
    t = u_ml.shape[0]
    tm = _tile(t, LIGHT_ROWS)

    def body(x_ref, w_ref, b_ref, pre_ref, act_ref, xbuf):
        @pl.when(pl.program_id(0) == 0)
        def _():
            xbuf[...] = jnp.zeros_like(xbuf)

        xbuf[0:HALO, :] = xbuf[tm:tm + HALO, :]
        xbuf[HALO:HALO + tm, :] = x_ref[...]
        pre = b_ref[...] + jnp.zeros((tm, MIX_W), f32)
        for kk in range(CONV_K):
            off = HALO - (CONV_K - 1) + kk
            pre = pre + w_ref[kk:kk + 1, :] * xbuf[off:off + tm, :]
        pre_ref[...] = pre
        act_ref[...] = pre * _sigmoid(pre)

    return pl.pallas_call(
        body, name="conv_fwd", grid=(t // tm,),
        in_specs=[_rows(tm, MIX_W), _const((CONV_K, MIX_W)), _const((1, MIX_W))],
        out_specs=[_rows(tm, MIX_W), _rows(tm, MIX_W)],
        out_shape=[jax.ShapeDtypeStruct((t, MIX_W), f32)] * 2,
        scratch_shapes=[pltpu.VMEM((tm + HALO, MIX_W), f32)],
        compiler_params=_cparams(1),
    )(u_ml, w, b)


def _conv_bwd(u_ml, w, pre, dact):
    t = u_ml.shape[0]
    tm = _tile(t, LIGHT_ROWS)
    nb = t // tm
    hb = tm // HALO

    def body(x_ref, halo_ref, w_ref, pre_ref, dact_ref, dx_ref, dw_ref, db_ref, dbuf, xbuf):
        i = pl.program_id(0)

        @pl.when(i == 0)
        def _():
            dbuf[...] = jnp.zeros_like(dbuf)
            dw_ref[...] = jnp.zeros_like(dw_ref)
            db_ref[...] = jnp.zeros_like(db_ref)

        p = pre_ref[...]
        sg = _sigmoid(p)
        dpre = dact_ref[...] * (sg * (1.0 + p * (1.0 - sg)))
        dbuf[tm:tm + HALO, :] = dbuf[0:HALO, :]
        dbuf[0:tm, :] = dpre
        has_prev = (i < nb - 1).astype(f32)
        xbuf[0:HALO, :] = halo_ref[...] * has_prev
        xbuf[HALO:HALO + tm, :] = x_ref[...]
        dx = jnp.zeros((tm, MIX_W), f32)
        for kk in range(CONV_K):
            back = CONV_K - 1 - kk
            dx = dx + w_ref[kk:kk + 1, :] * dbuf[back:back + tm, :]
            off = HALO - (CONV_K - 1) + kk
            dw_ref[kk:kk + 1, :] += jnp.sum(dpre * xbuf[off:off + tm, :], axis=0, keepdims=True)
        dx_ref[...] = dx.astype(ACT)
        db_ref[...] += jnp.sum(dpre, axis=0, keepdims=True)

    return pl.pallas_call(
        body, name="conv_bwd", grid=(nb,),
        in_specs=[_rows_rev(tm, MIX_W, nb),
                  pl.BlockSpec((HALO, MIX_W), lambda i: (jnp.maximum((nb - 1 - i) * hb - 1, 0), 0)),
                  _const((CONV_K, MIX_W)), _rows_rev(tm, MIX_W, nb), _rows_rev(tm, MIX_W, nb)],
        out_specs=[_rows_rev(tm, MIX_W, nb), _const((CONV_K, MIX_W)), _const((1, MIX_W))],
        out_shape=[jax.ShapeDtypeStruct((t, MIX_W), ACT), jax.ShapeDtypeStruct((CONV_K, MIX_W), f32), jax.ShapeDtypeStruct((1, MIX_W), f32)],
        scratch_shapes=[pltpu.VMEM((tm + HALO, MIX_W), f32), pltpu.VMEM((tm + HALO, MIX_W), f32)],
        compiler_params=_cparams(1),
    )(u_ml, u_ml, w, pre, dact)


def _lane_pick(x, lane):
    idx = lax.broadcasted_iota(jnp.int32, x.shape, 1)
    return jnp.sum(jnp.where(idx == lane, x, 0.0), axis=-1, keepdims=True)


def _ml_gate_forms(gates, tri):
    lf = _log_sigmoid(gates)
    gc = _sel_dot(tri, lf)
    lane = lax.broadcasted_iota(jnp.int32, gates.shape, 1)
    mixed = jnp.where(lane < HEADS, gates, gc)
    sel = (lax.broadcasted_iota(jnp.int32, (8, 128), 0) == lax.broadcasted_iota(jnp.int32, (8, 128), 1)).astype(f32)
    rowsf = _sel_dot_nt(sel, mixed)
    return gc, rowsf


def _ml_chunk(q, k, v, gates, gc, rowsf, c_st, n_st, m_st, tril_mask):
    hs = range(HEADS)
    g_col = [_lane_pick(gc, HEADS + h) for h in hs]
    ig_col = [_lane_pick(gates, h) for h in hs]
    dmat = [jnp.where(tril_mask, g_col[h] - rowsf[HEADS + h:HEADS + h + 1, :] + rowsf[h:h + 1, :], NEG) for h in hs]
    m_inter = [g_col[h] + m_st[h] for h in hs]
    m_t = [jnp.maximum(m_inter[h], jnp.max(dmat[h], axis=-1, keepdims=True)) for h in hs]
    wi = [jnp.exp(dmat[h] - m_t[h]) for h in hs]
    wo = [jnp.exp(m_inter[h] - m_t[h]) for h in hs]
    qk = [_bdot_nt(q[h], k[h]) * wi[h] for h in hs]
    num = [_bdot(qk[h], v[h]) + wo[h] * _bdot(q[h], c_st[h]) for h in hs]
    den = [_lane_sum(qk[h]) + wo[h] * _lane_dot(q[h], n_st[h]) for h in hs]
    floor = [jnp.exp(-m_t[h]) for h in hs]
    z = [jnp.maximum(jnp.abs(den[h]), floor[h]) for h in hs]
    g_last = [g_col[h][CHUNK - 1:CHUNK] for h in hs]
    a_col = [g_last[h] - g_col[h] + ig_col[h] for h in hs]
    m_new = [jnp.maximum(g_last[h] + m_st[h], jnp.max(a_col[h], axis=0, keepdims=True)) for h in hs]
    ws = [jnp.exp(a_col[h] - m_new[h]) for h in hs]
    w_old = [jnp.exp(g_last[h] + m_st[h] - m_new[h]) for h in hs]
    return dict(wi=wi, wo=wo, qk=qk, num=num, den=den, z=z, floor=floor, ws=ws, w_old=w_old, m_new=m_new)


def _mlstm_fwd(qkc, u_ml, gn, riders=()):
    t = qkc.shape[0]
    tb = _tile(t, MIXER_ROWS)
    nc_blk = tb // CHUNK

    def body(qk_ref, v_ref, mo_ref, gt_ref, gn_ref, og_ref, cst_ref, nst_ref, mst_ref, c_sc, n_sc, m_sc):
        @pl.when(pl.program_id(0) == 0)
        def _():
            c_sc[...] = jnp.zeros_like(c_sc)
            n_sc[...] = jnp.zeros_like(n_sc)
            m_sc[...] = jnp.zeros_like(m_sc)

        tril_mask = _tri(CHUNK)
        tri = tril_mask.astype(f32)

        def chunk(c, carry):
            r0 = pl.multiple_of(c * CHUNK, CHUNK)
            rows = pl.ds(r0, CHUNK)
            gates = gt_ref[rows, :]
            gc, rowsf = _ml_gate_forms(gates, tri)
            hs = range(HEADS)
            q = [qk_ref[rows, h * ML_DQK:(h + 1) * ML_DQK] * (ML_DQK ** -0.5) for h in hs]
            k = [qk_ref[rows, HEADS * ML_DQK + h * ML_DQK:HEADS * ML_DQK + (h + 1) * ML_DQK] for h in hs]
            v = [v_ref[rows, h * HEAD_W:(h + 1) * HEAD_W] for h in hs]
            c_st = [c_sc[h] for h in hs]
            n_st = [n_sc[h] for h in hs]
            m_full = [m_sc[h] for h in hs]
            r = _ml_chunk(q, k, v, gates, gc, rowsf, c_st, n_st, [m[:, 0:1] for m in m_full], tril_mask)
            ksc = [k[h] * r["ws"][h] for h in hs]
            new_c = [r["w_old"][h] * c_st[h] + _bdot_tn(ksc[h], v[h]) for h in hs]
            for h in hs:
                cs = slice(h * HEAD_W, (h + 1) * HEAD_W)
                cst_ref[c, h] = c_st[h]
                nst_ref[c, h] = n_st[h]
                mst_ref[c, h] = m_full[h]
                c_sc[h] = new_c[h]
                n_sc[h] = r["w_old"][h] * n_st[h] + jnp.sum(ksc[h], axis=0, keepdims=True)
                m_sc[h] = r["m_new"][h] + jnp.zeros((1, 128), f32)
                _, _, y = _head_rms(r["num"][h] / r["z"][h], gn_ref[:, cs], on_mxu=True)
                og_ref[rows, cs] = (y * _sigmoid(mo_ref[rows, h * HEAD_W:(h + 1) * HEAD_W])).astype(ACT)
            return carry

        lax.fori_loop(0, nc_blk, chunk, 0, unroll=True)

    nchunks = t // CHUNK
    return _riding_call(
        body, "mlstm_fwd", t // tb,
        in_specs=[_rows(tb, MIX_W), _rows(tb, MIX_W, 1), _rows(tb, MIX_W, 2), _rows(tb, 128, 12), _const((1, MIX_W))],
        out_specs=[_rows(tb, MIX_W),
                   pl.BlockSpec((nc_blk, HEADS, ML_DQK, HEAD_W), lambda i: (i, 0, 0, 0)),
                   pl.BlockSpec((nc_blk, HEADS, 1, ML_DQK), lambda i: (i, 0, 0, 0)),
                   pl.BlockSpec((nc_blk, HEADS, 1, 128), lambda i: (i, 0, 0, 0))],
        out_shape=[jax.ShapeDtypeStruct((t, MIX_W), ACT),
                   jax.ShapeDtypeStruct((nchunks, HEADS, ML_DQK, HEAD_W), f32),
                   jax.ShapeDtypeStruct((nchunks, HEADS, 1, ML_DQK), f32),
                   jax.ShapeDtypeStruct((nchunks, HEADS, 1, 128), f32)],
        scratch_shapes=[pltpu.VMEM((HEADS, ML_DQK, HEAD_W), f32), pltpu.VMEM((HEADS, 1, ML_DQK), f32), pltpu.VMEM((HEADS, 1, 128), f32)],
        operands=(qkc, u_ml, u_ml, u_ml, gn), riders=riders, copies=_gather_copies, ride_shapes=_gather_shapes(riders))


def _mlstm_bwd(qkc, u_ml, gn, cst, nst, mst, dog, riders=()):
    t = qkc.shape[0]
    tb = _tile(t, MIXER_ROWS)
    nb = t // tb
    nc_blk = tb // CHUNK

    def body(qk_ref, v_ref, mo_ref, gt_ref, gn_ref, cst_ref, nst_ref, mst_ref, dog_ref,
             dqk_ref, dv_ref, dmo_ref, dgt_ref, dgn_ref, dc_sc, dn_sc):
        @pl.when(pl.program_id(0) == 0)
        def _():
            dc_sc[...] = jnp.zeros_like(dc_sc)
            dn_sc[...] = jnp.zeros_like(dn_sc)
            dgn_ref[...] = jnp.zeros_like(dgn_ref)

        tril_mask = _tri(CHUNK)
        tri = tril_mask.astype(f32)
        triu = _tri(CHUNK, upper=True).astype(f32)
        lane = lax.broadcasted_iota(jnp.int32, (CHUNK, 128), 1)

        def chunk(j, carry):
            c = nc_blk - 1 - j
            r0 = pl.multiple_of(c * CHUNK, CHUNK)
            rows = pl.ds(r0, CHUNK)
            gates = gt_ref[rows, :]
            gc, rowsf = _ml_gate_forms(gates, tri)
            dg_mat = jnp.zeros((CHUNK, 128), f32)
            dig_mat = jnp.zeros((CHUNK, 128), f32)
            dlast_row = jnp.zeros((1, 128), f32)
            hs = range(HEADS)
            cols = [slice(h * HEAD_W, (h + 1) * HEAD_W) for h in hs]
            q = [qk_ref[rows, h * ML_DQK:(h + 1) * ML_DQK] * (ML_DQK ** -0.5) for h in hs]
            k = [qk_ref[rows, HEADS * ML_DQK + h * ML_DQK:HEADS * ML_DQK + (h + 1) * ML_DQK] for h in hs]
            v = [v_ref[rows, h * HEAD_W:(h + 1) * HEAD_W] for h in hs]
            c_st = [cst_ref[c, h] for h in hs]
            n_st = [nst_ref[c, h] for h in hs]
            m_st = [mst_ref[c, h][:, 0:1] for h in hs]
            dc = [dc_sc[h] for h in hs]
            dn = [dn_sc[h] for h in hs]
            r = _ml_chunk(q, k, v, gates, gc, rowsf, c_st, n_st, m_st, tril_mask)
            z, wi, wo, ws, w_old, den = r["z"], r["wi"], r["wo"], r["ws"], r["w_old"], r["den"]
            hh = [r["num"][h] / z[h] for h in hs]
            dh = []
            for h in hs:
                gnh = gn_ref[:, cols[h]]
                oh, rstd, y = _head_rms(hh[h], gnh, on_mxu=True)
                sg = _sigmoid(mo_ref[rows, h * HEAD_W:(h + 1) * HEAD_W])
                dogh = dog_ref[rows, cols[h]]
                dy = dogh * sg
                dmo_ref[rows, cols[h]] = (dogh * y * (sg * (1.0 - sg))).astype(ACT)
                dgn_ref[:, cols[h]] += jnp.sum(dy * oh, axis=0, keepdims=True)
                doh = dy * gnh
                dh.append(rstd * (doh - oh * (_lane_sum(doh * oh) * (1.0 / HEAD_W))))
            dnum = [dh[h] / z[h] for h in hs]
            dz = [-_lane_sum(dh[h] * hh[h]) / z[h] for h in hs]
            dden = [jnp.where(jnp.abs(den[h]) > r["floor"][h], dz[h] * jnp.sign(den[h]), 0.0) for h in hs]
            dsw = [(_bdot_nt(dnum[h], v[h]) + dden[h]) * wi[h] for h in hs]
            dq = [_bdot(dsw[h], k[h]) + wo[h] * (_bdot_nt(dnum[h], c_st[h]) + dden[h][:, :ML_DQK] * n_st[h]) for h in hs]
            dk_state = [ws[h] * (_bdot_nt(v[h], dc[h]) + dn[h]) for h in hs]
            dk = [_bdot_tn(dsw[h], q[h]) + dk_state[h] for h in hs]
            dv = [_bdot_tn(r["qk"][h], dnum[h]) + ws[h] * _bdot(k[h], dc[h]) for h in hs]
            woq = [wo[h] * q[h] for h in hs]
            new_dc = [w_old[h] * dc[h] + _bdot_tn(woq[h], dnum[h]) for h in hs]
            for h in hs:
                dv_ref[rows, cols[h]] = dv[h].astype(ACT)
                dc_sc[h] = new_dc[h]
                dn_sc[h] = w_old[h] * dn[h] + jnp.sum(woq[h] * dden[h][:, :ML_DQK], axis=0, keepdims=True)
                d_last = (jnp.sum(jnp.sum(k[h] * dk_state[h], axis=0, keepdims=True), axis=-1, keepdims=True)
                          + w_old[h] * (jnp.sum(jnp.sum(dc[h] * c_st[h], axis=0, keepdims=True), axis=-1, keepdims=True)
                                        + jnp.sum(dn[h] * n_st[h], axis=-1, keepdims=True)))
                kdk = _lane_sum(k[h] * dk[h])
                qdq = _lane_sum(q[h] * dq[h])
                dg_mat = dg_mat + jnp.where(lane == HEADS + h, qdq - kdk, 0.0)
                dlast_row = dlast_row + jnp.where(lane[0:1] == HEADS + h, d_last, 0.0)
                dig_mat = dig_mat + jnp.where(lane == h, kdk, 0.0)
                dqk_ref[rows, h * ML_DQK:(h + 1) * ML_DQK] = dq[h] * (ML_DQK ** -0.5)
                dqk_ref[rows, HEADS * ML_DQK + h * ML_DQK:HEADS * ML_DQK + (h + 1) * ML_DQK] = dk[h]
            dlf = _sel_dot(triu, dg_mat) + dlast_row
            dgt_ref[rows, :] = (dig_mat + dlf * _sigmoid(-gates)).astype(ACT)
            return carry

        lax.fori_loop(0, nc_blk, chunk, 0, unroll=True)

    st4 = lambda a, b: pl.BlockSpec((nc_blk, HEADS, a, b), lambda i: (nb - 1 - i, 0, 0, 0))
    return _riding_call(
        body, "mlstm_bwd", nb,
        in_specs=[_rows_rev(tb, MIX_W, nb), _rows_rev(tb, MIX_W, nb, 1), _rows_rev(tb, MIX_W, nb, 2), _rows_rev(tb, 128, nb, 12),
                  _const((1, MIX_W)), st4(ML_DQK, HEAD_W), st4(1, ML_DQK), st4(1, 128), _rows_rev(tb, MIX_W, nb)],
        out_specs=[_rows_rev(tb, MIX_W, nb), _rows_rev(tb, MIX_W, nb), _rows_rev(tb, MIX_W, nb), _rows_rev(tb, 128, nb), _const((1, MIX_W))],
        out_shape=[jax.ShapeDtypeStruct((t, MIX_W), f32), jax.ShapeDtypeStruct((t, MIX_W), ACT), jax.ShapeDtypeStruct((t, MIX_W), ACT),
                   jax.ShapeDtypeStruct((t, 128), ACT), jax.ShapeDtypeStruct((1, MIX_W), f32)],
        scratch_shapes=[pltpu.VMEM((HEADS, ML_DQK, HEAD_W), f32), pltpu.VMEM((HEADS, 1, ML_DQK), f32)],
        operands=(qkc, u_ml, u_ml, u_ml, gn, cst, nst, mst, dog), riders=riders, copies=_scatter_copies,
        ride_shapes=[jax.ShapeDtypeStruct(r.shape, r.dtype) for r in riders])


def _ln_fwd(r, g, b):
    mu = jnp.mean(r, axis=-1, keepdims=True)
    xc = r - mu
    rstd = lax.rsqrt(jnp.mean(xc * xc, axis=-1, keepdims=True) + LN_EPS)
    xh = xc * rstd
    return xh * g + b, xh, rstd


def _ln_bwd(dy, xh, rstd, g):
    dxh = dy * g
    return rstd * (dxh - jnp.mean(dxh, axis=-1, keepdims=True) - xh * jnp.mean(dxh * xh, axis=-1, keepdims=True))


def _outproj_ln1(og_hg, og_ml, x, w_out, g, b, riders=()):
    t = x.shape[0]
    tm = _tile(t, LIGHT_ROWS)

    def body(a_ref, b_ref, x_ref, w_ref, g_ref, bb_ref, x1_ref, xh_ref, rs_ref, x1b_ref):
        mix = _bdot(a_ref[...], w_ref[0:MIX_W, :]) + _bdot(b_ref[...], w_ref[MIX_W:2 * MIX_W, :])
        y, xh, rstd = _ln_fwd(ALPHA * x_ref[...] + mix, g_ref[...], bb_ref[...])
        x1_ref[...] = y
        x1b_ref[...] = y.astype(ACT)
        xh_ref[...] = xh.astype(ACT)
        rs_ref[...] = rstd

    return _riding_call(
        body, "outproj_ln1", t // tm,
        in_specs=[_rows(tm, MIX_W), _rows(tm, MIX_W), _rows(tm, D_MODEL), _resident((D_MODEL, D_MODEL)), _const((1, D_MODEL)), _const((1, D_MODEL))],
        out_specs=[_rows(tm, D_MODEL), _rows(tm, D_MODEL), _rows(tm, 1), _rows(tm, D_MODEL)],
        out_shape=[jax.ShapeDtypeStruct((t, D_MODEL), f32), jax.ShapeDtypeStruct((t, D_MODEL), ACT), jax.ShapeDtypeStruct((t, 1), f32),
                   jax.ShapeDtypeStruct((t, D_MODEL), ACT)],
        scratch_shapes=[], operands=(og_hg, og_ml, x, w_out, g, b), riders=riders, copies=_gather_copies, ride_shapes=_gather_shapes(riders))


def _ffn_up(x1, wg, wu, riders=()):
    t = x1.shape[0]
    tm = _tile(t, DENSE_ROWS)

    def body(x_ref, wg_ref, wu_ref, hg_ref, up_ref, a_ref):
        xv = x_ref[...]
        hg = _bdot_nt(xv, wg_ref[...])
        up = _bdot_nt(xv, wu_ref[...])
        hg_ref[...] = hg.astype(ACT)
        up_ref[...] = up.astype(ACT)
        a_ref[...] = (hg * _sigmoid(hg) * up).astype(ACT)

    return _riding_call(
        body, "ffn_up", t // tm,
        in_specs=[_rows(tm, D_MODEL), _resident((D_FF, D_MODEL)), _resident((D_FF, D_MODEL))],
        out_specs=[_rows(tm, D_FF), _rows(tm, D_FF), _rows(tm, D_FF)],
        out_shape=[jax.ShapeDtypeStruct((t, D_FF), ACT), jax.ShapeDtypeStruct((t, D_FF), ACT), jax.ShapeDtypeStruct((t, D_FF), ACT)],
        scratch_shapes=[], operands=(x1, wg, wu), riders=riders, copies=_gather_copies, ride_shapes=_gather_shapes(riders))


def _ffn_down_ln2(a, x1, wd, g, b):
    t = x1.shape[0]
    tm = _tile(t, LIGHT_ROWS)

    def body(a_ref, x_ref, w_ref, g_ref, bb_ref, x2_ref, xh_ref, rs_ref, x2b_ref):
        ffn = _bdot(a_ref[...], w_ref[...])
        y, xh, rstd = _ln_fwd(ALPHA * x_ref[...] + ffn, g_ref[...], bb_ref[...])
        x2_ref[...] = y
        x2b_ref[...] = y.astype(ACT)
        xh_ref[...] = xh.astype(ACT)
        rs_ref[...] = rstd

    return pl.pallas_call(
        body, name="ffn_down_ln2", grid=(t // tm,),
        in_specs=[_rows(tm, D_FF), _rows(tm, D_MODEL), _resident((D_FF, D_MODEL)), _const((1, D_MODEL)), _const((1, D_MODEL))],
        out_specs=[_rows(tm, D_MODEL), _rows(tm, D_MODEL), _rows(tm, 1), _rows(tm, D_MODEL)],
        out_shape=[jax.ShapeDtypeStruct((t, D_MODEL), f32), jax.ShapeDtypeStruct((t, D_MODEL), ACT), jax.ShapeDtypeStruct((t, 1), f32),
                   jax.ShapeDtypeStruct((t, D_MODEL), ACT)],
        compiler_params=_cparams(1, arbitrary=False),
    )(a, x1, wd, g, b)


def _head_loss_bwd(x2, xh2, rs2, p, tgt, w_pg, b_pg, w_pp, g2):
    t = x2.shape[0]
    tm = _tile(t, LIGHT_ROWS)

    def body(x_ref, xh_ref, rs_ref, p_ref, t_ref, wg_ref, bg_ref, wp_ref, g_ref,
             dr_ref, de_ref, dz_ref, loss_ref, dbg_ref, dg2_ref, db2_ref):
        @pl.when(pl.program_id(0) == 0)
        def _():
            loss_ref[...] = jnp.zeros_like(loss_ref)
            dbg_ref[...] = jnp.zeros_like(dbg_ref)
            dg2_ref[...] = jnp.zeros_like(dg2_ref)
            db2_ref[...] = jnp.zeros_like(db2_ref)

        x2v = x_ref[...]
        z = _bdot(x2v, wg_ref[...]) + bg_ref[...]
        e = _bdot(p_ref[...], wp_ref[...])
        sg = _sigmoid(z)
        diff = x2v + sg * e - t_ref[...]
        loss_ref[...] += 0.5 * jnp.sum(jnp.mean(diff * diff, axis=-1, keepdims=True), axis=0, keepdims=True)
        dy = diff * (1.0 / D_MODEL)
        de_ref[...] = (dy * sg).astype(ACT)
        dz = dy * e * (sg * (1.0 - sg))
        dz_ref[...] = dz.astype(ACT)
        dbg_ref[...] += jnp.sum(dz, axis=0, keepdims=True)
        dx2 = dy + _bdot_nt(dz, wg_ref[...])
        xh = xh_ref[...].astype(f32)
        dg2_ref[...] += jnp.sum(dx2 * xh, axis=0, keepdims=True)
        db2_ref[...] += jnp.sum(dx2, axis=0, keepdims=True)
        dr_ref[...] = _ln_bwd(dx2, xh, rs_ref[...], g_ref[...])

    row = jax.ShapeDtypeStruct((1, D_MODEL), f32)
    return pl.pallas_call(
        body, name="head_loss_bwd", grid=(t // tm,),
        in_specs=[_rows(tm, D_MODEL), _rows(tm, D_MODEL), _rows(tm, 1), _rows(tm, PLE), _rows(tm, D_MODEL),
                  _resident((D_MODEL, D_MODEL)), _const((1, D_MODEL)), _resident((PLE, D_MODEL)), _const((1, D_MODEL))],
        out_specs=[_rows(tm, D_MODEL), _rows(tm, D_MODEL), _rows(tm, D_MODEL), _const((1, 1)), _const((1, D_MODEL)), _const((1, D_MODEL)), _const((1, D_MODEL))],
        out_shape=[jax.ShapeDtypeStruct((t, D_MODEL), f32), jax.ShapeDtypeStruct((t, D_MODEL), ACT), jax.ShapeDtypeStruct((t, D_MODEL), ACT),
                   jax.ShapeDtypeStruct((1, 1), f32), row, row, row],
        compiler_params=_cparams(1),
    )(x2, xh2, rs2, p, tgt, w_pg, b_pg, w_pp, g2)


def _ffn_bwd(dr2, hg, up, xh1, rs1, wd, wg, wu, g1, w_out):
    t = dr2.shape[0]
    tm = _tile(t, DENSE_ROWS // 2)

    def body(dr_ref, hg_ref, up_ref, xh_ref, rs_ref, wd_ref, wg_ref, wu_ref, g_ref, wo_ref,
             dr1_ref, dhg_ref, dup_ref, dg1_ref, db1_ref, doghg_ref, dogml_ref):
        @pl.when(pl.program_id(0) == 0)
        def _():
            dg1_ref[...] = jnp.zeros_like(dg1_ref)
            db1_ref[...] = jnp.zeros_like(db1_ref)

        dr2v = dr_ref[...]
        da = _bdot_nt(dr2v, wd_ref[...])
        hgv = hg_ref[...].astype(f32)
        sg = _sigmoid(hgv)
        dhg = da * up_ref[...].astype(f32) * (sg * (1.0 + hgv * (1.0 - sg)))
        dup = da * (hgv * sg)
        dhg_ref[...] = dhg.astype(ACT)
        dup_ref[...] = dup.astype(ACT)
        dx1 = ALPHA * dr2v + _bdot(dhg, wg_ref[...]) + _bdot(dup, wu_ref[...])
        xh = xh_ref[...].astype(f32)
        dg1_ref[...] += jnp.sum(dx1 * xh, axis=0, keepdims=True)
        db1_ref[...] += jnp.sum(dx1, axis=0, keepdims=True)
        dr1 = _ln_bwd(dx1, xh, rs_ref[...], g_ref[...])
        dr1_ref[...] = dr1
        dog = _bdot_nt(dr1, wo_ref[...])
        doghg_ref[...] = dog[:, 0:MIX_W]
        dogml_ref[...] = dog[:, MIX_W:2 * MIX_W]

    row = jax.ShapeDtypeStruct((1, D_MODEL), f32)
    return pl.pallas_call(
        body, name="ffn_bwd", grid=(t // tm,),
        in_specs=[_rows(tm, D_MODEL), _rows(tm, D_FF), _rows(tm, D_FF), _rows(tm, D_MODEL), _rows(tm, 1),
                  _resident((D_FF, D_MODEL)), _resident((D_FF, D_MODEL)), _resident((D_FF, D_MODEL)), _const((1, D_MODEL)),
                  _resident((D_MODEL, D_MODEL))],
        out_specs=[_rows(tm, D_MODEL), _rows(tm, D_FF), _rows(tm, D_FF), _const((1, D_MODEL)), _const((1, D_MODEL)),
                   _rows(tm, MIX_W), _rows(tm, MIX_W)],
        out_shape=[jax.ShapeDtypeStruct((t, D_MODEL), f32), jax.ShapeDtypeStruct((t, D_FF), ACT), jax.ShapeDtypeStruct((t, D_FF), ACT), row, row,
                   jax.ShapeDtypeStruct((t, MIX_W), f32), jax.ShapeDtypeStruct((t, MIX_W), f32)],
        compiler_params=_cparams(1),
    )(dr2, hg, up, xh1, rs1, wd, wg, wu, g1, w_out)


def _inproj_bwd(dr1, du_hg, dqk, dmv, dmo, dgt, w_hg, w_ml):
    t = dr1.shape[0]
    tm = _tile(t, DENSE_ROWS)

    def body(dr_ref, dhg_ref, dqk_ref, dmv_ref, dmo_ref, dgt_ref, whg_ref, wml_ref, gx_ref, dml_ref):
        dml = jnp.concatenate([dqk_ref[...], dmv_ref[...], dmo_ref[...], dgt_ref[...]], axis=-1).astype(ACT)
        dml_ref[...] = dml
        gx_ref[...] = ALPHA * dr_ref[...] + _bdot(dhg_ref[...], whg_ref[...]) + _bdot(dml, wml_ref[...])

    return pl.pallas_call(
        body, name="inproj_bwd", grid=(t // tm,),
        in_specs=[_rows(tm, D_MODEL), _rows(tm, U_HG), _rows(tm, MIX_W), _rows(tm, MIX_W), _rows(tm, MIX_W), _rows(tm, 128),
                  _resident((U_HG, D_MODEL)), _resident((U_ML, D_MODEL))],
        out_specs=[_rows(tm, D_MODEL), _rows(tm, U_ML)],
        out_shape=[jax.ShapeDtypeStruct((t, D_MODEL), f32), jax.ShapeDtypeStruct((t, U_ML), ACT)],
        compiler_params=_cparams(1, arbitrary=False),
    )(dr1, du_hg, dqk, dmv, dmo, dgt, w_hg, w_ml)


def _wgrad(a, b, name, tk=None, tn=None, colsum=False, low=False):
    t, kdim = a.shape
    n = b.shape[1]
    tk = tk or kdim
    tn = tn or n
    tt = _tile(t, WGRAD_ROWS)
    nt = t // tt
    assert not (colsum and low) and (not colsum or tn == n)

    def body(a_ref, b_ref, o_ref, *s_ref):
        @pl.when(pl.program_id(2) == 0)
        def _():
            o_ref[...] = jnp.zeros_like(o_ref)
            if colsum:
                s_ref[0][...] = jnp.zeros_like(s_ref[0])

        av = a_ref[...]
        o_ref[...] += _bdot_tn(av, b_ref[...])
        if colsum:
            s_ref[0][...] += jnp.sum(av.astype(f32), axis=0, keepdims=True)
        if low:
            @pl.when(pl.program_id(2) == nt - 1)
            def _():
                s_ref[0][...] = o_ref[...].astype(bf16)

    out_specs = [pl.BlockSpec((tk, tn), lambda i, j, s: (i, j))]
    out_shape = [jax.ShapeDtypeStruct((kdim, n), f32)]
    if colsum:
        out_specs.append(pl.BlockSpec((1, tk), lambda i, j, s: (0, i)))
        out_shape.append(jax.ShapeDtypeStruct((1, kdim), f32))
    if low:
        out_specs.append(pl.BlockSpec((tk, tn), lambda i, j, s: (i, j)))
        out_shape.append(jax.ShapeDtypeStruct((kdim, n), bf16))
    res = pl.pallas_call(
        body, name=name, grid=(kdim // tk, n // tn, t // tt),
        in_specs=[pl.BlockSpec((tt, tk), lambda i, j, s: (s, i)), pl.BlockSpec((tt, tn), lambda i, j, s: (s, j))],
        out_specs=out_specs, out_shape=out_shape,
        compiler_params=_cparams(3),
    )(a, b)
    return res if (colsum or low) else res[0]


_TRANSPOSED = {"w_in", "w_ffn_gate", "w_ffn_up"}
_COL_SPLIT = {"ple_w_proj"}
_SCATTER_PLAN = (("w_ffn_gate", "w_ffn_up"), ("w_ffn_down", "w_out", "ple_w_gate", "ple_w_proj"))
_RIDE_PLAN = {"inproj": ("w_ffn_gate",), "hgrn2_fwd": ("w_ffn_up",), "mlstm_fwd": ("w_out",),
              "outproj_ln1": ("ple_w_gate", "ple_w_proj"), "ffn_up": ("w_ffn_down",)}


def _from_chip_major(a, col_split):
    if col_split:
        return a.transpose(1, 0, 2).reshape(a.shape[1], 4 * a.shape[2])
    return a.reshape(4 * a.shape[1], a.shape[2])


def _local_step(x, p, tgt, w_in_b, b_in, logits, conv_w, conv_b, hg_gn, ml_gn, w_out_b, ln1_g, ln1_b,
                wg_b, wu_b, wd_b, ln2_g, ln2_b, w_pp_b, w_pg_b, b_pg, early_hook=None, late_shards=None):
    pad_w = U_HG + U_ML - PROJ_W
    w_hg = w_in_b[:U_HG]
    w_ml = jnp.pad(w_in_b[U_HG:], ((0, pad_w), (0, 0)))
    bb_hg = b_in[:, :U_HG]
    bb_ml = jnp.pad(b_in[:, U_HG:], ((0, 0), (0, pad_w)))

    late = dict(w_out=w_out_b, w_ffn_gate=wg_b, w_ffn_up=wu_b, w_ffn_down=wd_b, ple_w_proj=w_pp_b, ple_w_gate=w_pg_b)

    def riders_of(call):
        return [late_shards[k] for k in _RIDE_PLAN[call]] if late_shards is not None else ()

    def arrived(call, got):
        for k, g in zip(_RIDE_PLAN[call], got):
            late[k] = _from_chip_major(g, k in _COL_SPLIT)

    (u_hg, u_ml, xb), got = _inproj(x, w_hg, w_ml, bb_hg, bb_ml, riders_of("inproj"))
    arrived("inproj", got)
    (og_hg, sst, hg_b, hg_a, hg_o), got = _hgrn2_fwd(u_hg, logits, hg_gn, riders_of("hgrn2_fwd"))
    arrived("hgrn2_fwd", got)
    pre, qkc = _conv_fwd(u_ml, conv_w, conv_b)
    (og_ml, cst, nst, mst), got = _mlstm_fwd(qkc, u_ml, ml_gn, riders_of("mlstm_fwd"))
    arrived("mlstm_fwd", got)
    (x1, xh1, rs1, x1b), got = _outproj_ln1(og_hg, og_ml, x, late["w_out"], ln1_g, ln1_b, riders_of("outproj_ln1"))
    arrived("outproj_ln1", got)
    (hgp, up, act), got = _ffn_up(x1b, late["w_ffn_gate"], late["w_ffn_up"], riders_of("ffn_up"))
    arrived("ffn_up", got)
    w_out_b, wg_b, wu_b, wd_b = late["w_out"], late["w_ffn_gate"], late["w_ffn_up"], late["w_ffn_down"]
    w_pp_b, w_pg_b = late["ple_w_proj"], late["ple_w_gate"]
    x2, xh2, rs2, x2b = _ffn_down_ln2(act, x1, wd_b, ln2_g, ln2_b)
    dr2, de, dz, loss, d_bpg, d_ln2g, d_ln2b = _head_loss_bwd(x2, xh2, rs2, p, tgt, w_pg_b, b_pg, w_pp_b, ln2_g)
    dr1, dhg, dup, d_ln1g, d_ln1b, dog_hg, dog_ml = _ffn_bwd(dr2, hgp, up, xh1, rs1, wd_b, wg_b, wu_b, ln1_g, w_out_b)

    d_wo_a, lo_wo_a = _wgrad(og_hg, dr1, "wgrad_out_hg", low=True)
    d_wo_b, lo_wo_b = _wgrad(og_ml, dr1, "wgrad_out_ml", low=True)
    d_wg, lo_wg = _wgrad(dhg, x1b, "wgrad_ffn_gate", tk=D_FF // 2, low=True)
    d_wu, lo_wu = _wgrad(dup, x1b, "wgrad_ffn_up", tk=D_FF // 2, low=True)
    d_wd, lo_wd = _wgrad(act, dr2, "wgrad_ffn_down", tk=D_FF // 2, low=True)
    d_wpp, lo_wpp = _wgrad(p, de, "wgrad_ple_proj", low=True)
    d_wpg, lo_wpg = _wgrad(x2b, dz, "wgrad_ple_gate", low=True)
    early = dict(w_out=jnp.concatenate([d_wo_a, d_wo_b], axis=0), w_ffn_gate=d_wg, w_ffn_up=d_wu, w_ffn_down=d_wd,
                 ple_w_proj=d_wpp, ple_w_gate=d_wpg)
    early_low = dict(w_out=jnp.concatenate([lo_wo_a, lo_wo_b], axis=0), w_ffn_gate=lo_wg, w_ffn_up=lo_wu, w_ffn_down=lo_wd,
                     ple_w_proj=lo_wpp, ple_w_gate=lo_wpg)
    ride_hg, ride_ml = early_hook(early_low) if early_hook is not None else ((), ())

    (du_hg, d_logits, d_hg_gn), got_hg = _hgrn2_bwd(u_hg, logits, hg_gn, sst, hg_b, hg_a, hg_o, dog_hg, ride_hg)
    (dqkc, dmv, dmo, dgt, d_ml_gn), got_ml = _mlstm_bwd(qkc, u_ml, ml_gn, cst, nst, mst, dog_ml, ride_ml)
    dqk, d_conv_w, d_conv_b = _conv_bwd(u_ml, conv_w, pre, dqkc)
    grad_x, du_ml = _inproj_bwd(dr1, du_hg, dqk, dmv, dmo, dgt, w_hg, w_ml)

    dw_hg, db_hg = _wgrad(du_hg, xb, "wgrad_in_hg", tk=U_HG // 2, colsum=True)
    dw_ml, db_ml = _wgrad(du_ml, xb, "wgrad_in_ml", colsum=True)
    d_w_in = jnp.concatenate([dw_hg, dw_ml[:PROJ_W - U_HG]], axis=0)
    d_b_in = jnp.concatenate([db_hg, db_ml[:, :PROJ_W - U_HG]], axis=1)

    grads = dict(w_in=d_w_in, b_in=d_b_in, hg_lb_logits=d_logits, ml_conv_w=d_conv_w, ml_conv_b=d_conv_b,
                 hg_norm_g=d_hg_gn, ml_norm_g=d_ml_gn, ln1_g=d_ln1g, ln1_b=d_ln1b, ln2_g=d_ln2g, ln2_b=d_ln2b,
                 ple_b_gate=d_bpg, **early)
    return loss, grad_x, grads, (list(got_hg), list(got_ml))


_ANY = pl.BlockSpec(memory_space=pltpu.HBM)
_MESH = pl.DeviceIdType.MESH


def _my_place():
    return lax.axis_index("x"), lax.axis_index("y"), lax.axis_index("c")


def _other_chips(x, y):
    return [(1 - x, y), (x, 1 - y), (1 - x, 1 - y)]


_VMEM = pl.BlockSpec(memory_space=pltpu.VMEM)
_EX_ROWS = 32


def _pair_reduce_cols(p, name):
    s, r, c = p.shape
    hc = c // 2

    def body(p_ref, o_ref, other, send_sem, recv_sem):
        x, y, cc = _my_place()

        def run(mine_lo, theirs_lo):
            cp = pltpu.make_async_remote_copy(src_ref=p_ref.at[pl.ds(0, s), pl.ds(0, r), pl.ds(theirs_lo, hc)], dst_ref=other,
                                              send_sem=send_sem, recv_sem=recv_sem, device_id=(x, y, 1 - cc), device_id_type=_MESH)
            cp.start()
            cp.wait()
            for slot in range(s):
                o_ref[slot] = (p_ref[slot, :, mine_lo:mine_lo + hc] + other[slot]).astype(bf16)

        @pl.when(cc == 0)
        def _():
            run(0, hc)

        @pl.when(cc == 1)
        def _():
            run(hc, 0)

    return pl.pallas_call(
        body, name=name, in_specs=[_VMEM], out_specs=_VMEM,
        out_shape=jax.ShapeDtypeStruct((s, r, hc), bf16),
        scratch_shapes=[pltpu.VMEM((s, r, hc), f32), pltpu.SemaphoreType.DMA, pltpu.SemaphoreType.DMA],
        compiler_params=pltpu.CompilerParams(vmem_limit_bytes=VMEM_LIMIT),
    )(p)


def _reduce_adamw_cols(rcv, w, m, v, name):
    s, r, hc = rcv.shape

    def body(r_ref, w_ref, m_ref, v_ref, g_ref, d_ref, nm_ref, nv_ref, halves, send_sem, recv_sem):
        x, y, cc = _my_place()
        acc = r_ref[0].astype(f32)
        for slot in range(1, s):
            acc = acc + r_ref[slot].astype(f32)
        halves[cc] = acc
        cp = pltpu.make_async_remote_copy(src_ref=halves.at[cc], dst_ref=halves.at[cc], send_sem=send_sem, recv_sem=recv_sem,
                                          device_id=(x, y, 1 - cc), device_id_type=_MESH)
        cp.start()
        cp.wait()
        for k in range(2):
            cols = slice(k * hc, (k + 1) * hc)
            g = halves[k]
            nm = B1 * m_ref[:, cols] + (1.0 - B1) * g
            nv = B2 * v_ref[:, cols] + (1.0 - B2) * (g * g)
            g_ref[:, cols] = g
            nm_ref[:, cols] = nm
            nv_ref[:, cols] = nv
            d_ref[:, cols] = -LR * ((nm / (1.0 - B1 ** STEP)) / (jnp.sqrt(nv / (1.0 - B2 ** STEP)) + EPS_ADAM) + WD * w_ref[:, cols])

    return pl.pallas_call(
        body, name=name, in_specs=[_VMEM] * 4, out_specs=[_VMEM] * 4,
        out_shape=[jax.ShapeDtypeStruct((r, 2 * hc), f32)] * 4,
        scratch_shapes=[pltpu.VMEM((2, r, hc), f32), pltpu.SemaphoreType.DMA, pltpu.SemaphoreType.DMA],
        compiler_params=pltpu.CompilerParams(vmem_limit_bytes=VMEM_LIMIT),
    )(rcv, w, m, v)


def _reduce_adamw(rcv, w, m, v, name):
    s, r, c = rcv.shape
    rows_per = _EX_ROWS
    half = r // 2
    steps = half // rows_per

    def body(r_ref, w_ref, m_ref, v_ref, g_ref, d_ref, nm_ref, nv_ref, mine, theirs, send_sems, recv_sems):
        x, y, cc = _my_place()

        def swap(k):
            rs = pl.ds(k * half, half)
            return pltpu.make_async_remote_copy(src_ref=mine.at[rs], dst_ref=theirs.at[rs], send_sem=send_sems.at[k], recv_sem=recv_sems.at[k],
                                                device_id=(x, y, 1 - cc), device_id_type=_MESH)

        def chip_sum(i, carry):
            rs = pl.ds(pl.multiple_of(i * rows_per, rows_per), rows_per)
            acc = r_ref[0, rs, :].astype(f32)
            for slot in range(1, s):
                acc = acc + r_ref[slot, rs, :].astype(f32)
            mine[rs, :] = acc
            return carry

        def update(i, carry):
            rs = pl.ds(pl.multiple_of(i * rows_per, rows_per), rows_per)
            g = mine[rs, :] + theirs[rs, :]
            nm = B1 * m_ref[rs, :] + (1.0 - B1) * g
            nv = B2 * v_ref[rs, :] + (1.0 - B2) * (g * g)
            g_ref[rs, :] = g
            nm_ref[rs, :] = nm
            nv_ref[rs, :] = nv
            d_ref[rs, :] = -LR * ((nm / (1.0 - B1 ** STEP)) / (jnp.sqrt(nv / (1.0 - B2 ** STEP)) + EPS_ADAM) + WD * w_ref[rs, :])
            return carry

        lax.fori_loop(0, steps, chip_sum, 0)
        swap(0).start()
        lax.fori_loop(steps, 2 * steps, chip_sum, 0)
        swap(1).start()
        swap(0).wait()
        lax.fori_loop(0, steps, update, 0)
        swap(1).wait()
        lax.fori_loop(steps, 2 * steps, update, 0)

    return pl.pallas_call(
        body, name=name, in_specs=[_VMEM] * 4, out_specs=[_VMEM] * 4,
        out_shape=[jax.ShapeDtypeStruct((r, c), f32)] * 4,
        scratch_shapes=[pltpu.VMEM((r, c), f32), pltpu.VMEM((r, c), f32), pltpu.SemaphoreType.DMA((2,)), pltpu.SemaphoreType.DMA((2,))],
        compiler_params=pltpu.CompilerParams(vmem_limit_bytes=VMEM_LIMIT),
    )(rcv, w, m, v)


def _gather_copies(ins, outs, send_sems, recv_sems, local_sems):
    x, y, c = _my_place()
    me = 2 * x + y
    local, outgoing, incoming = [], [], []
    for a in range(len(ins)):
        local.append(pltpu.make_async_copy(ins[a], outs[a].at[me], local_sems.at[a]))
        for j, (px, py) in enumerate(_other_chips(x, y)):
            sems = dict(send_sem=send_sems.at[3 * a + j], recv_sem=recv_sems.at[3 * a + j], device_id=(px, py, c), device_id_type=_MESH)
            outgoing.append(pltpu.make_async_remote_copy(src_ref=ins[a], dst_ref=outs[a].at[me], **sems))
            incoming.append(pltpu.make_async_remote_copy(src_ref=ins[a], dst_ref=outs[a].at[2 * px + py], **sems))
    return local, outgoing, incoming


def _gather_first(block, taps, name):
    r, c = block.shape
    hc = c // 2

    def body(in_ref, tap_in, out_ref, tap_out, send_sems, recv_sems):
        x, y, cc = _my_place()
        me = 2 * x + y
        sibling = (x, y, 1 - cc)
        chips = _other_chips(x, y)
        out_ref[me] = in_ref[...]
        tap_out[me] = tap_in[...]

        def run(mine, theirs):
            def ici(j, chip):
                px, py = chips[j]
                src = in_ref.at[pl.ds(0, r), pl.ds(mine, hc)] if chip is None else out_ref.at[chip, pl.ds(0, r), pl.ds(mine, hc)]
                dst = out_ref.at[me if chip is None else chip, pl.ds(0, r), pl.ds(mine, hc)]
                return pltpu.make_async_remote_copy(src_ref=src, dst_ref=dst, send_sem=send_sems.at[j], recv_sem=recv_sems.at[j],
                                                    device_id=(px, py, cc), device_id_type=_MESH)

            def d2d(j, lo):
                px, py = chips[j]
                blk = out_ref.at[2 * px + py, pl.ds(0, r), pl.ds(lo, hc)]
                return pltpu.make_async_remote_copy(src_ref=blk, dst_ref=blk, send_sem=send_sems.at[3 + j], recv_sem=recv_sems.at[3 + j],
                                                    device_id=sibling, device_id_type=_MESH)

            def tap(j, chip):
                px, py = chips[j]
                return pltpu.make_async_remote_copy(src_ref=tap_in, dst_ref=tap_out.at[me if chip is None else chip],
                                                    send_sem=send_sems.at[6 + j], recv_sem=recv_sems.at[6 + j],
                                                    device_id=(px, py, cc), device_id_type=_MESH)

            for j in range(3):
                ici(j, None).start()
                tap(j, None).start()
            for j, (px, py) in enumerate(chips):
                ici(j, 2 * px + py).wait_recv()
                d2d(j, mine).start()
            for j, (px, py) in enumerate(chips):
                d2d(j, theirs).wait_recv()
                tap(j, 2 * px + py).wait_recv()
            for j in range(3):
                ici(j, None).wait_send()
                d2d(j, mine).wait_send()
                tap(j, None).wait_send()

        @pl.when(cc == 0)
        def _():
            run(0, hc)

        @pl.when(cc == 1)
        def _():
            run(hc, 0)

    return pl.pallas_call(
        body, name=name, in_specs=[_VMEM, _VMEM], out_specs=[_VMEM, _VMEM],
        out_shape=[jax.ShapeDtypeStruct((4, r, c), block.dtype), jax.ShapeDtypeStruct((4,) + taps.shape, taps.dtype)],
        scratch_shapes=[pltpu.SemaphoreType.DMA((9,)), pltpu.SemaphoreType.DMA((9,))],
        compiler_params=pltpu.CompilerParams(vmem_limit_bytes=VMEM_LIMIT),
    )(block, taps)


def _riding_call(body, name, nsteps, in_specs, out_specs, out_shape, scratch_shapes, operands, riders, copies, ride_shapes):
    nr, n_in, n_out, n_scr = len(riders), len(in_specs), len(out_specs), len(scratch_shapes)

    def wrapped(*refs):
        ins, ride_in = refs[:n_in], refs[n_in:n_in + nr]
        outs, ride_out = refs[n_in + nr:n_in + nr + n_out], refs[n_in + nr + n_out:n_in + 2 * nr + n_out]
        scratch, sems = refs[n_in + 2 * nr + n_out:n_in + 2 * nr + n_out + n_scr], refs[n_in + 2 * nr + n_out + n_scr:]
        if nr:
            @pl.when(pl.program_id(0) == 0)
            def _():
                local, outgoing, _ = copies(ride_in, ride_out, *sems)
                for cp in local + outgoing:
                    cp.start()

        body(*ins, *outs, *scratch)
        if nr:
            @pl.when(pl.program_id(0) == nsteps - 1)
            def _():
                local, outgoing, incoming = copies(ride_in, ride_out, *sems)
                for cp in incoming:
                    cp.wait_recv()
                for cp in outgoing:
                    cp.wait_send()
                for cp in local:
                    cp.wait()

    hbm = pl.BlockSpec(memory_space=pltpu.HBM)
    sems = [pltpu.SemaphoreType.DMA((3 * nr,)), pltpu.SemaphoreType.DMA((3 * nr,)), pltpu.SemaphoreType.DMA((nr,))] if nr else []
    res = pl.pallas_call(
        wrapped, name=name, grid=(nsteps,),
        in_specs=list(in_specs) + [hbm] * nr, out_specs=list(out_specs) + [hbm] * nr,
        out_shape=list(out_shape) + list(ride_shapes),
        scratch_shapes=list(scratch_shapes) + sems,
        compiler_params=_cparams(1),
    )(*operands, *riders)
    return list(res[:n_out]), list(res[n_out:])


def _gather_shapes(riders):
    return [jax.ShapeDtypeStruct((4,) + r.shape, r.dtype) for r in riders]


def _scatter_copies(ins, outs, send_sems, recv_sems, local_sems):
    x, y, c = _my_place()
    me = 2 * x + y
    local, outgoing, incoming = [], [], []
    for a in range(len(ins)):
        local.append(pltpu.make_async_copy(ins[a].at[me], outs[a].at[me], local_sems.at[a]))
        for j, (px, py) in enumerate(_other_chips(x, y)):
            sems = dict(send_sem=send_sems.at[3 * a + j], recv_sem=recv_sems.at[3 * a + j], device_id=(px, py, c), device_id_type=_MESH)
            outgoing.append(pltpu.make_async_remote_copy(src_ref=ins[a].at[2 * px + py], dst_ref=outs[a].at[me], **sems))
            incoming.append(pltpu.make_async_remote_copy(src_ref=ins[a].at[2 * px + py], dst_ref=outs[a].at[2 * px + py], **sems))
    return local, outgoing, incoming


def _scatter_chips(pieces, name):
    n = len(pieces)

    def body(*refs):
        local, outgoing, incoming = _scatter_copies(refs[:n], refs[n:2 * n], *refs[2 * n:])
        for cp in local + outgoing:
            cp.start()
        for cp in incoming:
            cp.wait_recv()
        for cp in outgoing:
            cp.wait_send()
        for cp in local:
            cp.wait()

    return pl.pallas_call(
        body, name=name,
        in_specs=[_ANY] * n, out_specs=[_ANY] * n,
        out_shape=[jax.ShapeDtypeStruct(s.shape, s.dtype) for s in pieces],
        scratch_shapes=[pltpu.SemaphoreType.DMA((3 * n,)), pltpu.SemaphoreType.DMA((3 * n,)), pltpu.SemaphoreType.DMA((n,))],
    )(*pieces)


def _gather_all(block, name):
    def body(in_ref, out_ref, send_sems, recv_sems, local_sem):
        x, y, c = _my_place()
        me = 4 * x + 2 * y + c
        cp = pltpu.make_async_copy(in_ref, out_ref.at[me], local_sem)
        cp.start()
        peers = []
        for dx in range(2):
            for dy in range(2):
                for dc in range(2):
                    if dx or dy or dc:
                        peers.append((1 - x if dx else x, 1 - y if dy else y, 1 - c if dc else c))
        for j, pr in enumerate(peers):
            pltpu.make_async_remote_copy(src_ref=in_ref, dst_ref=out_ref.at[me], send_sem=send_sems.at[j], recv_sem=recv_sems.at[j],
                                         device_id=pr, device_id_type=_MESH).start()
        for j, (px, py, pc) in enumerate(peers):
            pltpu.make_async_remote_copy(src_ref=in_ref, dst_ref=out_ref.at[4 * px + 2 * py + pc], send_sem=send_sems.at[j], recv_sem=recv_sems.at[j],
                                         device_id=(px, py, pc), device_id_type=_MESH).wait()
        cp.wait()

    return pl.pallas_call(
        body, name=name,
        in_specs=[_ANY], out_specs=_ANY,
        out_shape=jax.ShapeDtypeStruct((8,) + block.shape, block.dtype),
        scratch_shapes=[pltpu.SemaphoreType.DMA((7,)), pltpu.SemaphoreType.DMA((7,)), pltpu.SemaphoreType.DMA],
    )(block)


def _row_tile(r, c):
    best = r
    for cand in range(16, r + 1, 16):
        if r % cand == 0 and cand * c * 4 <= (1 << 20):
            best = cand
    return best if best * c * 4 <= (4 << 20) else r


def _sum_slots(parts, name):
    n, r, c = parts.shape
    tr = _row_tile(r, c)

    def body(p_ref, o_ref):
        acc = p_ref[0].astype(f32)
        for s in range(1, n):
            acc = acc + p_ref[s].astype(f32)
        o_ref[...] = acc

    return pl.pallas_call(
        body, name=name, grid=(r // tr,),
        in_specs=[pl.BlockSpec((n, tr, c), lambda i: (0, i, 0))],
        out_specs=pl.BlockSpec((tr, c), lambda i: (i, 0)),
        out_shape=jax.ShapeDtypeStruct((r, c), f32),
        compiler_params=_cparams(1, arbitrary=False),
    )(parts)


def _adamw(parts, w, m, v, name):
    n, r, c = parts.shape
    tr = _row_tile(r, c)
    tc = c
    if tr == r and r * c * 4 > (1 << 20) and c % 256 == 0:
        tc = 256

    def body(p_ref, w_ref, m_ref, v_ref, g_ref, d_ref, nm_ref, nv_ref):
        g = p_ref[0]
        for s in range(1, n):
            g = g + p_ref[s]
        nm = B1 * m_ref[...] + (1.0 - B1) * g
        nv = B2 * v_ref[...] + (1.0 - B2) * (g * g)
        m_hat = nm / (1.0 - B1 ** STEP)
        v_hat = nv / (1.0 - B2 ** STEP)
        g_ref[...] = g
        nm_ref[...] = nm
        nv_ref[...] = nv
        d_ref[...] = -LR * (m_hat / (jnp.sqrt(v_hat) + EPS_ADAM) + WD * w_ref[...])

    blk = pl.BlockSpec((tr, tc), lambda i, j: (i, j))
    return pl.pallas_call(
        body, name=name, grid=(r // tr, c // tc),
        in_specs=[pl.BlockSpec((n, tr, tc), lambda i, j: (0, i, j)), blk, blk, blk],
        out_specs=[blk] * 4,
        out_shape=[jax.ShapeDtypeStruct((r, c), f32)] * 4,
        compiler_params=_cparams(2, arbitrary=False),
    )(parts, w, m, v)


_BIG = ["w_in", "w_out", "w_ffn_gate", "w_ffn_up", "w_ffn_down", "ple_w_proj", "ple_w_gate"]
_SMALL = ["b_in", "hg_lb_logits", "ml_conv_w", "ml_conv_b", "hg_norm_g", "ml_norm_g", "ln1_g", "ln1_b", "ln2_g", "ln2_b", "ple_b_gate"]
_ORDER = ["w_in", "b_in", "hg_lb_logits", "ml_conv_w", "ml_conv_b", "hg_norm_g", "ml_norm_g", "w_out", "ln1_g", "ln1_b",
          "w_ffn_gate", "w_ffn_up", "w_ffn_down", "ln2_g", "ln2_b", "ple_w_proj", "ple_w_gate", "ple_b_gate"]
_PACK_ROWS, _PACK_COLS = 16, 1024


def _pack(arrays):
    flat = jnp.concatenate([a.reshape(-1) for a in arrays])
    return jnp.pad(flat, (0, _PACK_ROWS * _PACK_COLS - flat.shape[0])).reshape(_PACK_ROWS, _PACK_COLS)


def _unpack(pack, shapes):
    flat = pack.reshape(-1)
    out, off = [], 0
    for s in shapes:
        size = 1
        for d in s:
            size *= d
        out.append(flat[off:off + size].reshape(s))
        off += size
    return out


def _to_chip_major(g, col_split):
    if col_split:
        k, n = g.shape
        return g.reshape(k, 4, n // 4).transpose(1, 0, 2)
    k, n = g.shape
    return g.reshape(4, k // 4, n)


def kernel(x, p, w_in, b_in, hg_lb_logits, ml_conv_w, ml_conv_b, hg_norm_g, ml_norm_g, w_out, ln1_g, ln1_b, w_ffn_gate, w_ffn_up, w_ffn_down, ln2_g, ln2_b, ple_w_proj, ple_w_gate, ple_b_gate, loss_target, m_w_in, m_b_in, m_hg_lb_logits, m_ml_conv_w, m_ml_conv_b, m_hg_norm_g, m_ml_norm_g, m_w_out, m_ln1_g, m_ln1_b, m_w_ffn_gate, m_w_ffn_up, m_w_ffn_down, m_ln2_g, m_ln2_b, m_ple_w_proj, m_ple_w_gate, m_ple_b_gate, v_w_in, v_b_in, v_hg_lb_logits, v_ml_conv_w, v_ml_conv_b, v_hg_norm_g, v_ml_norm_g, v_w_out, v_ln1_g, v_ln1_b, v_w_ffn_gate, v_w_ffn_up, v_w_ffn_down, v_ln2_g, v_ln2_b, v_ple_w_proj, v_ple_w_gate, v_ple_b_gate):
    args = dict(locals())
    wts = {k: args[k] for k in _ORDER}
    mom = {k: args["m_" + k] for k in _ORDER}
    var = {k: args["v_" + k] for k in _ORDER}
    two_d = lambda a: a.reshape(a.shape[-2], a.shape[-1])
    block = lambda k, a: jnp.swapaxes(two_d(a), 0, 1) if k in _TRANSPOSED else two_d(a)
    unblock = lambda k, a: (jnp.swapaxes(a, 0, 1) if k in _TRANSPOSED else a).reshape(wts[k].shape)

    shards = {k: block(k, wts[k]).astype(bf16) for k in _BIG}
    w_in_blocks, taps = _gather_first(shards["w_in"], two_d(ml_conv_w), "gather_w_in")
    w_in_full = _from_chip_major(w_in_blocks, False)
    conv_w_full = _from_chip_major(taps, True)

    early_keys = _BIG[1:]
    loss, grad_x, grads, (got_hg, got_ml) = _local_step(
        x[0], p[0, 0], loss_target[0], w_in_full, b_in, hg_lb_logits, conv_w_full, ml_conv_b, hg_norm_g, ml_norm_g,
        None, ln1_g, ln1_b, None, None, None, ln2_g, ln2_b, None, None, ple_b_gate,
        early_hook=lambda low: tuple([_to_chip_major(low[k], k in _COL_SPLIT) for k in names] for names in _SCATTER_PLAN),
        late_shards={k: shards[k] for k in early_keys})

    out_g, out_d, out_m, out_v = {}, {}, {}, {}

    def finish(k, g, d, nm, nv):
        out_g[k], out_d[k], out_m[k], out_v[k] = unblock(k, g), unblock(k, d), unblock(k, nm), unblock(k, nv)

    for names, got in zip(_SCATTER_PLAN, (got_hg, got_ml)):
        for k, rcv in zip(names, got):
            finish(k, *_reduce_adamw(rcv, block(k, wts[k]), block(k, mom[k]), block(k, var[k]), "reduce_adamw_" + k))

    core_sums = _pair_reduce_cols(grads["w_in"][None], "pair_reduce_w_in")[0]
    rcv = _scatter_chips([_to_chip_major(core_sums, False)], "scatter_grad_w_in")[0]
    finish("w_in", *_reduce_adamw_cols(rcv, block("w_in", wts["w_in"]), block("w_in", mom["w_in"]), block("w_in", var["w_in"]),
                                       "reduce_adamw_w_in"))

    small_shapes = [(1, PROJ_W), (2, MIX_W), (CONV_K, MIX_W)] + [(1, MIX_W)] * 3 + [(1, D_MODEL)] * 5 + [(1, 1)]
    contrib = _pack([grads[k] for k in _SMALL] + [loss])
    summed = _sum_slots(_gather_all(contrib, "gather_small"), "sum_small")
    small = _unpack(summed, small_shapes)
    loss_total = small[-1].reshape(())
    gsm = dict(zip(_SMALL, small[:-1]))
    place = 2 * lax.axis_index("x") + lax.axis_index("y")
    conv_cols = ml_conv_w.shape[-1]
    gsm["ml_conv_w"] = lax.dynamic_slice(gsm["ml_conv_w"], (0, place * conv_cols), (CONV_K, conv_cols))
    own_shapes = [wts[k].shape for k in _SMALL]
    g_pack = _pack([gsm[k] for k in _SMALL])
    res = _adamw(g_pack[None], _pack([wts[k] for k in _SMALL]), _pack([mom[k] for k in _SMALL]), _pack([var[k] for k in _SMALL]), "adamw_small")
    for dst, pack in zip((out_g, out_d, out_m, out_v), res):
        for k, a in zip(_SMALL, _unpack(pack, own_shapes)):
            dst[k] = a

    outs = [loss_total, grad_x[None]]
    for group in (out_g, out_d, out_m, out_v):
        outs += [group[k] for k in _ORDER]
    return tuple(outs)
```

```python
import jax
import jax.numpy as jnp
from jax import lax
from jax.experimental import pallas as pl
from jax.experimental.pallas import tpu as pltpu

f32 = jnp.float32
bf16 = jnp.bfloat16

D_MODEL = 1024
HEADS = 4
HEAD_W = 128
MIX_W = HEADS * HEAD_W
ML_DQK = 64
PROJ_W = 3592
U_HG = 4 * MIX_W
U_ML = 3 * MIX_W + 128
D_FF = 2816
PLE = 256
CHUNK = 128
SUB = 16
EXP_CAP = 80.0
CONV_K = 4
HALO = 8
ALPHA = float(2.0 ** 0.25)
LN_EPS = 1e-5
RMS_EPS = 1e-6
NEG = -1e30
LR, B1, B2, EPS_ADAM, WD, STEP = 0.001, 0.9, 0.999, 1e-08, 0.01, 10
VMEM_LIMIT = 56 * 1024 * 1024
MIXER_ROWS = 512
DENSE_ROWS = 512
LIGHT_ROWS = 1024
WGRAD_ROWS = 2048


def _cparams(n_axes, arbitrary=True):
    sem = ("arbitrary",) * n_axes if arbitrary else ("parallel",) * n_axes
    return pltpu.CompilerParams(dimension_semantics=sem, vmem_limit_bytes=VMEM_LIMIT)


ACT = bf16


def _mx(a):
    return a.astype(ACT)


def _bdot(a, b):
    return jnp.dot(_mx(a), _mx(b), preferred_element_type=f32)


def _bdot_nt(a, b):
    return lax.dot_general(_mx(a), _mx(b), (((1,), (1,)), ((), ())), preferred_element_type=f32)


def _bdot_tn(a, b):
    return lax.dot_general(_mx(a), _mx(b), (((0,), (0,)), ((), ())), preferred_element_type=f32)


def _split3(x):
    hi = x.astype(bf16)
    r1 = x - hi.astype(f32)
    mid = r1.astype(bf16)
    lo = (r1 - mid.astype(f32)).astype(bf16)
    return hi, mid, lo


def _dot3(a, b, dims):
    a_hi = a.astype(bf16)
    a_lo = (a - a_hi.astype(f32)).astype(bf16)
    b_hi = b.astype(bf16)
    b_lo = (b - b_hi.astype(f32)).astype(bf16)
    dn = (dims, ((), ()))
    return (lax.dot_general(a_hi, b_hi, dn, preferred_element_type=f32) + lax.dot_general(a_hi, b_lo, dn, preferred_element_type=f32)
            + lax.dot_general(a_lo, b_hi, dn, preferred_element_type=f32))


def _lane_sum(x):
    hi = x.astype(bf16)
    lo = (x - hi.astype(f32)).astype(bf16)
    ones = jnp.ones((x.shape[1], 128), bf16)
    return jnp.dot(hi, ones, preferred_element_type=f32) + jnp.dot(lo, ones, preferred_element_type=f32)


def _lane_dot(x, row):
    return _dot3(x, jnp.broadcast_to(row, (128, row.shape[1])), ((1,), (1,)))


def _sel_dot(sel, x):
    sb = sel.astype(bf16)
    return sum(jnp.dot(sb, part, preferred_element_type=f32) for part in _split3(x))


def _sel_dot_nt(sel, x):
    sb = sel.astype(bf16)
    return sum(lax.dot_general(sb, part, (((1,), (1,)), ((), ())), preferred_element_type=f32) for part in _split3(x))


def _sigmoid(x):
    return 1.0 / (1.0 + jnp.exp(-x))


def _log_sigmoid(x):
    return jnp.minimum(x, 0.0) - jnp.log(1.0 + jnp.exp(-jnp.abs(x)))


def _tri(n, upper=False):
    r = lax.broadcasted_iota(jnp.int32, (n, n), 0)
    c = lax.broadcasted_iota(jnp.int32, (n, n), 1)
    return (c >= r) if upper else (c <= r)


def _rows(tm, n, col=0):
    return pl.BlockSpec((tm, n), lambda i, _c=col: (i, _c))


def _rows_rev(tm, n, nb, col=0):
    return pl.BlockSpec((tm, n), lambda i, _c=col, _nb=nb: (_nb - 1 - i, _c))


def _const(shape):
    return pl.BlockSpec(shape, lambda i, _n=len(shape): (0,) * _n)


def _resident(shape):
    return pl.BlockSpec(shape, lambda i, _n=len(shape): (0,) * _n, pipeline_mode=pl.Buffered(1))


def _tile(t, want):
    return want if t % want == 0 else t


def _inproj(x, w_hg, w_ml, b_hg, b_ml, riders=()):
    t = x.shape[0]
    tm = _tile(t, DENSE_ROWS)

    def body(x_ref, whg_ref, wml_ref, bhg_ref, bml_ref, uhg_ref, uml_ref, xb_ref):
        xb = _mx(x_ref[...])
        xb_ref[...] = xb
        uhg_ref[...] = _bdot_nt(xb, whg_ref[...]) + bhg_ref[...]
        uml_ref[...] = _bdot_nt(xb, wml_ref[...]) + bml_ref[...]

    return _riding_call(
        body, "inproj", t // tm,
        in_specs=[_rows(tm, D_MODEL), _resident((U_HG, D_MODEL)), _resident((U_ML, D_MODEL)), _const((1, U_HG)), _const((1, U_ML))],
        out_specs=[_rows(tm, U_HG), _rows(tm, U_ML), _rows(tm, D_MODEL)],
        out_shape=[jax.ShapeDtypeStruct((t, U_HG), f32), jax.ShapeDtypeStruct((t, U_ML), f32), jax.ShapeDtypeStruct((t, D_MODEL), ACT)],
        scratch_shapes=[], operands=(x, w_hg, w_ml, b_hg, b_ml), riders=riders, copies=_gather_copies, ride_shapes=_gather_shapes(riders))


def _hg_gates(hq, hf, lb, tri, b=None):
    s = _sigmoid(hf)
    om = 1.0 - lb
    f = lb + om * s
    k = om * (1.0 - s)
    sq = _sigmoid(hq)
    q = hq * sq
    if b is None:
        b = _sel_dot(tri, jnp.log(f))
    return q, sq, s, f, k, b


def _hg_scores(q, k, b, tril_mask, a=None):
    qts, kts, eqs, eks, rows = [], [], [], [], []
    for i in range(CHUNK // SUB):
        lo = i * SUB
        ref = jnp.zeros_like(b[0:1]) if i == 0 else b[lo - 1:lo]
        eq = jnp.exp(b[lo:lo + SUB] - ref)
        ek = jnp.exp(jnp.minimum(ref - b, EXP_CAP))
        qt = q[lo:lo + SUB] * eq
        kt = k * ek
        if a is None:
            rows.append(_bdot_nt(qt, kt))
        qts.append(qt); kts.append(kt); eqs.append(eq); eks.append(ek)
    if a is None:
        a = jnp.where(tril_mask, jnp.concatenate(rows, axis=0), 0.0)
    return a, qts, kts, eqs, eks


def _head_rms(o, gn, on_mxu=False):
    ms = _lane_sum(o * o) * (1.0 / o.shape[1]) if on_mxu else jnp.mean(o * o, axis=-1, keepdims=True)
    rstd = lax.rsqrt(ms + RMS_EPS)
    oh = o * rstd
    return oh, rstd, oh * gn


def _lower_bound(logit_ref):
    lg = logit_ref[...]
    return _sigmoid(lg[0:1] - lg[1:2])


def _hgrn2_fwd(u_hg, logits, gn, riders=()):
    t = u_hg.shape[0]
    tb = _tile(t, MIXER_ROWS)
    nc_blk = tb // CHUNK

    def body(u_ref, lg_ref, gn_ref, og_ref, sst_ref, b_ref, a_ref, o_ref, st_ref):
        @pl.when(pl.program_id(0) == 0)
        def _():
            st_ref[...] = jnp.zeros_like(st_ref)

        lb_all = _lower_bound(lg_ref)
        tril_mask = _tri(CHUNK)
        tri = tril_mask.astype(f32)

        def chunk(c, carry):
            r0 = pl.multiple_of(c * CHUNK, CHUNK)
            rows = pl.ds(r0, CHUNK)
            heads = range(HEADS)
            cols = [slice(h * HEAD_W, (h + 1) * HEAD_W) for h in heads]
            hv = [u_ref[rows, 2 * MIX_W + h * HEAD_W:2 * MIX_W + (h + 1) * HEAD_W] for h in heads]
            gts = [_hg_gates(u_ref[rows, h * HEAD_W:(h + 1) * HEAD_W], u_ref[rows, MIX_W + h * HEAD_W:MIX_W + (h + 1) * HEAD_W],
                             lb_all[:, cols[h]], tri) for h in heads]
            q = [g[0] for g in gts]
            k = [g[4] for g in gts]
            b = [g[5] for g in gts]
            a = [_hg_scores(q[h], k[h], b[h], tril_mask)[0] for h in heads]
            st = [st_ref[h] for h in heads]
            bl = [b[h][CHUNK - 1:CHUNK] for h in heads]
            o = [_bdot(a[h], hv[h]) + _bdot_nt(q[h] * jnp.exp(b[h]), st[h]) for h in heads]
            new_st = [st[h] * jnp.exp(bl[h]) + _bdot_tn(hv[h], k[h] * jnp.exp(bl[h] - b[h])) for h in heads]
            for h in heads:
                sst_ref[c, h] = st[h]
                st_ref[h] = new_st[h]
                b_ref[rows, cols[h]] = b[h]
                a_ref[rows, cols[h]] = a[h].astype(ACT)
                o_ref[rows, cols[h]] = o[h]
                hgate = u_ref[rows, 3 * MIX_W + h * HEAD_W:3 * MIX_W + (h + 1) * HEAD_W]
                _, _, y = _head_rms(o[h], gn_ref[:, cols[h]])
                og_ref[rows, cols[h]] = (y * (hgate * _sigmoid(hgate))).astype(ACT)
            return carry

        lax.fori_loop(0, nc_blk, chunk, 0, unroll=True)

    assert CHUNK == HEAD_W
    return _riding_call(
        body, "hgrn2_fwd", t // tb,
        in_specs=[_rows(tb, U_HG), _const((2, MIX_W)), _const((1, MIX_W))],
        out_specs=[_rows(tb, MIX_W), pl.BlockSpec((nc_blk, HEADS, HEAD_W, HEAD_W), lambda i: (i, 0, 0, 0)),
                   _rows(tb, MIX_W), _rows(tb, MIX_W), _rows(tb, MIX_W)],
        out_shape=[jax.ShapeDtypeStruct((t, MIX_W), ACT), jax.ShapeDtypeStruct((t // CHUNK, HEADS, HEAD_W, HEAD_W), f32),
                   jax.ShapeDtypeStruct((t, MIX_W), f32), jax.ShapeDtypeStruct((t, MIX_W), ACT), jax.ShapeDtypeStruct((t, MIX_W), f32)],
        scratch_shapes=[pltpu.VMEM((HEADS, HEAD_W, HEAD_W), f32)],
        operands=(u_hg, logits, gn), riders=riders, copies=_gather_copies, ride_shapes=_gather_shapes(riders))


def _hgrn2_bwd(u_hg, logits, gn, sst, bcum, scores, o_raw, dog, riders=()):
    t = u_hg.shape[0]
    tb = _tile(t, MIXER_ROWS)
    nb = t // tb
    nc_blk = tb // CHUNK

    def body(u_ref, lg_ref, gn_ref, sst_ref, b_ref, a_ref, o_ref, dog_ref, du_ref, dlg_ref, dgn_ref, dst_ref):
        @pl.when(pl.program_id(0) == 0)
        def _():
            dst_ref[...] = jnp.zeros_like(dst_ref)
            dlg_ref[...] = jnp.zeros_like(dlg_ref)
            dgn_ref[...] = jnp.zeros_like(dgn_ref)

        lb_all = _lower_bound(lg_ref)
        tril_mask = _tri(CHUNK)
        tri = tril_mask.astype(f32)
        triu = _tri(CHUNK, upper=True).astype(f32)

        def chunk(j, carry):
            c = nc_blk - 1 - j
            r0 = pl.multiple_of(c * CHUNK, CHUNK)
            rows = pl.ds(r0, CHUNK)
            heads = range(HEADS)
            nsub = CHUNK // SUB
            cols = [slice(h * HEAD_W, (h + 1) * HEAD_W) for h in heads]
            hq = [u_ref[rows, h * HEAD_W:(h + 1) * HEAD_W] for h in heads]
            hf = [u_ref[rows, MIX_W + h * HEAD_W:MIX_W + (h + 1) * HEAD_W] for h in heads]
            hv = [u_ref[rows, 2 * MIX_W + h * HEAD_W:2 * MIX_W + (h + 1) * HEAD_W] for h in heads]
            lb = [lb_all[:, cols[h]] for h in heads]
            gts = [_hg_gates(hq[h], hf[h], lb[h], tri, b=b_ref[rows, cols[h]]) for h in heads]
            q, sq, s, f, k, b = ([g[n] for g in gts] for n in range(6))
            scs = [_hg_scores(q[h], k[h], b[h], tril_mask, a=a_ref[rows, cols[h]]) for h in heads]
            a, qts, kts, eqs, eks = ([sc[n] for sc in scs] for n in range(5))
            st = [sst_ref[c, h] for h in heads]
            dst = [dst_ref[h] for h in heads]
            bl = [b[h][CHUNK - 1:CHUNK] for h in heads]
            eb = [jnp.exp(b[h]) for h in heads]
            qh = [q[h] * eb[h] for h in heads]
            ekl = [jnp.exp(bl[h] - b[h]) for h in heads]
            kh = [k[h] * ekl[h] for h in heads]
            o = [o_ref[rows, cols[h]] for h in heads]
            do = []
            for h in heads:
                hgate = u_ref[rows, 3 * MIX_W + h * HEAD_W:3 * MIX_W + (h + 1) * HEAD_W]
                gnh = gn_ref[:, cols[h]]
                oh, rstd, y = _head_rms(o[h], gnh)
                sg = _sigmoid(hgate)
                dogh = dog_ref[rows, cols[h]]
                dy = dogh * (hgate * sg)
                du_ref[rows, 3 * MIX_W + h * HEAD_W:3 * MIX_W + (h + 1) * HEAD_W] = (dogh * y * (sg * (1.0 + hgate * (1.0 - sg)))).astype(ACT)
                dgn_ref[:, cols[h]] += jnp.sum(dy * oh, axis=0, keepdims=True)
                doh = dy * gnh
                do.append(rstd * (doh - oh * jnp.mean(doh * oh, axis=-1, keepdims=True)))
            da = [jnp.where(tril_mask, _bdot_nt(do[h], hv[h]), 0.0) for h in heads]
            dv = [_bdot_tn(a[h], do[h]) + _bdot_nt(kh[h], dst[h]) for h in heads]
            dq = [_bdot(do[h], st[h]) * eb[h] for h in heads]
            dk = [_bdot(hv[h], dst[h]) * ekl[h] for h in heads]
            d_last = [jnp.sum(k[h] * dk[h], axis=0, keepdims=True) + jnp.exp(bl[h]) * jnp.sum(dst[h] * st[h], axis=0, keepdims=True)
                      for h in heads]
            d_b = [q[h] * dq[h] - k[h] * dk[h] for h in heads]
            dqs = [[] for _ in heads]
            q_dq = [[] for _ in heads]
            for i in range(nsub):
                for h in heads:
                    da_i = _mx(da[h][i * SUB:(i + 1) * SUB])
                    q_r, k_r = _mx(qts[h][i]), _mx(kts[h][i])
                    g_q = jnp.dot(da_i, k_r, preferred_element_type=f32)
                    g_k = lax.dot_general(da_i, q_r, (((0,), (0,)), ((), ())), preferred_element_type=f32)
                    dqs[h].append(g_q * eqs[h][i])
                    q_dq[h].append(q_r.astype(f32) * g_q)
                    dk[h] = dk[h] + g_k * eks[h][i]
                    d_b[h] = d_b[h] - k_r.astype(f32) * g_k
            for h in heads:
                dq[h] = dq[h] + jnp.concatenate(dqs[h], axis=0)
                d_b[h] = d_b[h] + jnp.concatenate(q_dq[h], axis=0)
                dst_ref[h] = dst[h] * jnp.exp(bl[h]) + _bdot_tn(do[h], qh[h])
            dg = [_sel_dot(triu, d_b[h]) + d_last[h] for h in heads]
            for h in heads:
                dfk = dg[h] / f[h] - dk[h]
                du_ref[rows, h * HEAD_W:(h + 1) * HEAD_W] = (dq[h] * (sq[h] * (1.0 + hq[h] * (1.0 - sq[h])))).astype(ACT)
                du_ref[rows, MIX_W + h * HEAD_W:MIX_W + (h + 1) * HEAD_W] = ((1.0 - lb[h]) * dfk * s[h] * (1.0 - s[h])).astype(ACT)
                du_ref[rows, 2 * MIX_W + h * HEAD_W:2 * MIX_W + (h + 1) * HEAD_W] = dv[h].astype(ACT)
                dlb = jnp.sum((1.0 - s[h]) * dfk, axis=0, keepdims=True) * (lb[h] * (1.0 - lb[h]))
                dlg_ref[0:1, cols[h]] += dlb
                dlg_ref[1:2, cols[h]] -= dlb
            return carry

        lax.fori_loop(0, nc_blk, chunk, 0, unroll=True)

    rev = _rows_rev(tb, MIX_W, nb)
    return _riding_call(
        body, "hgrn2_bwd", nb,
        in_specs=[_rows_rev(tb, U_HG, nb), _const((2, MIX_W)), _const((1, MIX_W)),
                  pl.BlockSpec((nc_blk, HEADS, HEAD_W, HEAD_W), lambda i: (nb - 1 - i, 0, 0, 0)), rev, rev, rev, rev],
        out_specs=[_rows_rev(tb, U_HG, nb), _const((2, MIX_W)), _const((1, MIX_W))],
        out_shape=[jax.ShapeDtypeStruct((t, U_HG), ACT), jax.ShapeDtypeStruct((2, MIX_W), f32), jax.ShapeDtypeStruct((1, MIX_W), f32)],
        scratch_shapes=[pltpu.VMEM((HEADS, HEAD_W, HEAD_W), f32)],
        operands=(u_hg, logits, gn, sst, bcum, scores, o_raw, dog), riders=riders, copies=_scatter_copies,
        ride_shapes=[jax.ShapeDtypeStruct(r.shape, r.dtype) for r in riders])


def _conv_fwd(u_ml, w, b):
    t = u_ml.shape[0]
    tm = _tile(t, LIGHT_ROWS)

    def body(x_ref, w_ref, b_ref, pre_ref, act_ref, xbuf):
        @pl.when(pl.program_id(0) == 0)
        def _():
            xbuf[...] = jnp.zeros_like(xbuf)

        xbuf[0:HALO, :] = xbuf[tm:tm + HALO, :]
        xbuf[HALO:HALO + tm, :] = x_ref[...]
        pre = b_ref[...] + jnp.zeros((tm, MIX_W), f32)
        for kk in range(CONV_K):
            off = HALO - (CONV_K - 1) + kk
            pre = pre + w_ref[kk:kk + 1, :] * xbuf[off:off + tm, :]
        pre_ref[...] = pre
        act_ref[...] = pre * _sigmoid(pre)

    return pl.pallas_call(
        body, name="conv_fwd", grid=(t // tm,),
        in_specs=[_rows(tm, MIX_W), _const((CONV_K, MIX_W)), _const((1, MIX_W))],
        out_specs=[_rows(tm, MIX_W), _rows(tm, MIX_W)],
        out_shape=[jax.ShapeDtypeStruct((t, MIX_W), f32)] * 2,
        scratch_shapes=[pltpu.VMEM((tm + HALO, MIX_W), f32)],
        compiler_params=_cparams(1),
    )(u_ml, w, b)


def _conv_bwd(u_ml, w, pre, dact):
    t = u_ml.shape[0]
    tm = _tile(t, LIGHT_ROWS)
    nb = t // tm
    hb = tm // HALO

    def body(x_ref, halo_ref, w_ref, pre_ref, dact_ref, dx_ref, dw_ref, db_ref, dbuf, xbuf):
        i = pl.program_id(0)

        @pl.when(i == 0)
        def _():
            dbuf[...] = jnp.zeros_like(dbuf)
            dw_ref[...] = jnp.zeros_like(dw_ref)
            db_ref[...] = jnp.zeros_like(db_ref)

        p = pre_ref[...]
        sg = _sigmoid(p)
        dpre = dact_ref[...] * (sg * (1.0 + p * (1.0 - sg)))
        dbuf[tm:tm + HALO, :] = dbuf[0:HALO, :]
        dbuf[0:tm, :] = dpre
        has_prev = (i < nb - 1).astype(f32)
        xbuf[0:HALO, :] = halo_ref[...] * has_prev
        xbuf[HALO:HALO + tm, :] = x_ref[...]
        dx = jnp.zeros((tm, MIX_W), f32)
        for kk in range(CONV_K):
            back = CONV_K - 1 - kk
            dx = dx + w_ref[kk:kk + 1, :] * dbuf[back:back + tm, :]
            off = HALO - (CONV_K - 1) + kk
            dw_ref[kk:kk + 1, :] += jnp.sum(dpre * xbuf[off:off + tm, :], axis=0, keepdims=True)
        dx_ref[...] = dx.astype(ACT)
        db_ref[...] += jnp.sum(dpre, axis=0, keepdims=True)

    return pl.pallas_call(
        body, name="conv_bwd", grid=(nb,),
        in_specs=[_rows_rev(tm, MIX_W, nb),
                  pl.BlockSpec((HALO, MIX_W), lambda i: (jnp.maximum((nb - 1 - i) * hb - 1, 0), 0)),
                  _const((CONV_K, MIX_W)), _rows_rev(tm, MIX_W, nb), _rows_rev(tm, MIX_W, nb)],
        out_specs=[_rows_rev(tm, MIX_W, nb), _const((CONV_K, MIX_W)), _const((1, MIX_W))],
        out_shape=[jax.ShapeDtypeStruct((t, MIX_W), ACT), jax.ShapeDtypeStruct((CONV_K, MIX_W), f32), jax.ShapeDtypeStruct((1, MIX_W), f32)],
        scratch_shapes=[pltpu.VMEM((tm + HALO, MIX_W), f32), pltpu.VMEM((tm + HALO, MIX_W), f32)],
        compiler_params=_cparams(1),
    )(u_ml, u_ml, w, pre, dact)


def _lane_pick(x, lane):
    idx = lax.broadcasted_iota(jnp.int32, x.shape, 1)
    return jnp.sum(jnp.where(idx == lane, x, 0.0), axis=-1, keepdims=True)


def _ml_gate_forms(gates, tri):
    lf = _log_sigmoid(gates)
    gc = _sel_dot(tri, lf)
    lane = lax.broadcasted_iota(jnp.int32, gates.shape, 1)
    mixed = jnp.where(lane < HEADS, gates, gc)
    sel = (lax.broadcasted_iota(jnp.int32, (8, 128), 0) == lax.broadcasted_iota(jnp.int32, (8, 128), 1)).astype(f32)
    rowsf = _sel_dot_nt(sel, mixed)
    return gc, rowsf


def _ml_chunk(q, k, v, gates, gc, rowsf, c_st, n_st, m_st, tril_mask):
    hs = range(HEADS)
    g_col = [_lane_pick(gc, HEADS + h) for h in hs]
    ig_col = [_lane_pick(gates, h) for h in hs]
    dmat = [jnp.where(tril_mask, g_col[h] - rowsf[HEADS + h:HEADS + h + 1, :] + rowsf[h:h + 1, :], NEG) for h in hs]
    m_inter = [g_col[h] + m_st[h] for h in hs]
    m_t = [jnp.maximum(m_inter[h], jnp.max(dmat[h], axis=-1, keepdims=True)) for h in hs]
    wi = [jnp.exp(dmat[h] - m_t[h]) for h in hs]
    wo = [jnp.exp(m_inter[h] - m_t[h]) for h in hs]
    qk = [_bdot_nt(q[h], k[h]) * wi[h] for h in hs]
    num = [_bdot(qk[h], v[h]) + wo[h] * _bdot(q[h], c_st[h]) for h in hs]
    den = [_lane_sum(qk[h]) + wo[h] * _lane_dot(q[h], n_st[h]) for h in hs]
    floor = [jnp.exp(-m_t[h]) for h in hs]
    z = [jnp.maximum(jnp.abs(den[h]), floor[h]) for h in hs]
    g_last = [g_col[h][CHUNK - 1:CHUNK] for h in hs]
    a_col = [g_last[h] - g_col[h] + ig_col[h] for h in hs]
    m_new = [jnp.maximum(g_last[h] + m_st[h], jnp.max(a_col[h], axis=0, keepdims=True)) for h in hs]
    ws = [jnp.exp(a_col[h] - m_new[h]) for h in hs]
    w_old = [jnp.exp(g_last[h] + m_st[h] - m_new[h]) for h in hs]
    return dict(wi=wi, wo=wo, qk=qk, num=num, den=den, z=z, floor=floor, ws=ws, w_old=w_old, m_new=m_new)


def _mlstm_fwd(qkc, u_ml, gn, riders=()):
    t = qkc.shape[0]
    tb = _tile(t, MIXER_ROWS)
    nc_blk = tb // CHUNK

    def body(qk_ref, v_ref, mo_ref, gt_ref, gn_ref, og_ref, cst_ref, nst_ref, mst_ref, c_sc, n_sc, m_sc):
        @pl.when(pl.program_id(0) == 0)
        def _():
            c_sc[...] = jnp.zeros_like(c_sc)
            n_sc[...] = jnp.zeros_like(n_sc)
            m_sc[...] = jnp.zeros_like(m_sc)

        tril_mask = _tri(CHUNK)
        tri = tril_mask.astype(f32)

        def chunk(c, carry):
            r0 = pl.multiple_of(c * CHUNK, CHUNK)
            rows = pl.ds(r0, CHUNK)
            gates = gt_ref[rows, :]
            gc, rowsf = _ml_gate_forms(gates, tri)
            hs = range(HEADS)
            q = [qk_ref[rows, h * ML_DQK:(h + 1) * ML_DQK] * (ML_DQK ** -0.5) for h in hs]
            k = [qk_ref[rows, HEADS * ML_DQK + h * ML_DQK:HEADS * ML_DQK + (h + 1) * ML_DQK] for h in hs]
            v = [v_ref[rows, h * HEAD_W:(h + 1) * HEAD_W] for h in hs]
            c_st = [c_sc[h] for h in hs]
            n_st = [n_sc[h] for h in hs]
            m_full = [m_sc[h] for h in hs]
            r = _ml_chunk(q, k, v, gates, gc, rowsf, c_st, n_st, [m[:, 0:1] for m in m_full], tril_mask)
            ksc = [k[h] * r["ws"][h] for h in hs]
            new_c = [r["w_old"][h] * c_st[h] + _bdot_tn(ksc[h], v[h]) for h in hs]
            for h in hs:
                cs = slice(h * HEAD_W, (h + 1) * HEAD_W)
                cst_ref[c, h] = c_st[h]
                nst_ref[c, h] = n_st[h]
                mst_ref[c, h] = m_full[h]
                c_sc[h] = new_c[h]
                n_sc[h] = r["w_old"][h] * n_st[h] + jnp.sum(ksc[h], axis=0, keepdims=True)
                m_sc[h] = r["m_new"][h] + jnp.zeros((1, 128), f32)
                _, _, y = _head_rms(r["num"][h] / r["z"][h], gn_ref[:, cs], on_mxu=True)
                og_ref[rows, cs] = (y * _sigmoid(mo_ref[rows, h * HEAD_W:(h + 1) * HEAD_W])).astype(ACT)
            return carry

        lax.fori_loop(0, nc_blk, chunk, 0, unroll=True)

    nchunks = t // CHUNK
    return _riding_call(
        body, "mlstm_fwd", t // tb,
        in_specs=[_rows(tb, MIX_W), _rows(tb, MIX_W, 1), _rows(tb, MIX_W, 2), _rows(tb, 128, 12), _const((1, MIX_W))],
        out_specs=[_rows(tb, MIX_W),
                   pl.BlockSpec((nc_blk, HEADS, ML_DQK, HEAD_W), lambda i: (i, 0, 0, 0)),
                   pl.BlockSpec((nc_blk, HEADS, 1, ML_DQK), lambda i: (i, 0, 0, 0)),
                   pl.BlockSpec((nc_blk, HEADS, 1, 128), lambda i: (i, 0, 0, 0))],
        out_shape=[jax.ShapeDtypeStruct((t, MIX_W), ACT),
                   jax.ShapeDtypeStruct((nchunks, HEADS, ML_DQK, HEAD_W), f32),
                   jax.ShapeDtypeStruct((nchunks, HEADS, 1, ML_DQK), f32),
                   jax.ShapeDtypeStruct((nchunks, HEADS, 1, 128), f32)],
        scratch_shapes=[pltpu.VMEM((HEADS, ML_DQK, HEAD_W), f32), pltpu.VMEM((HEADS, 1, ML_DQK), f32), pltpu.VMEM((HEADS, 1, 128), f32)],
        operands=(qkc, u_ml, u_ml, u_ml, gn), riders=riders, copies=_gather_copies, ride_shapes=_gather_shapes(riders))


def _mlstm_bwd(qkc, u_ml, gn, cst, nst, mst, dog, riders=()):
    t = qkc.shape[0]
    tb = _tile(t, MIXER_ROWS)
    nb = t // tb
    nc_blk = tb // CHUNK

    def body(qk_ref, v_ref, mo_ref, gt_ref, gn_ref, cst_ref, nst_ref, mst_ref, dog_ref,
             dqk_ref, dv_ref, dmo_ref, dgt_ref, dgn_ref, dc_sc, dn_sc):
        @pl.when(pl.program_id(0) == 0)
        def _():
            dc_sc[...] = jnp.zeros_like(dc_sc)
            dn_sc[...] = jnp.zeros_like(dn_sc)
            dgn_ref[...] = jnp.zeros_like(dgn_ref)

        tril_mask = _tri(CHUNK)
        tri = tril_mask.astype(f32)
        triu = _tri(CHUNK, upper=True).astype(f32)
        lane = lax.broadcasted_iota(jnp.int32, (CHUNK, 128), 1)

        def chunk(j, carry):
            c = nc_blk - 1 - j
            r0 = pl.multiple_of(c * CHUNK, CHUNK)
            rows = pl.ds(r0, CHUNK)
            gates = gt_ref[rows, :]
            gc, rowsf = _ml_gate_forms(gates, tri)
            dg_mat = jnp.zeros((CHUNK, 128), f32)
            dig_mat = jnp.zeros((CHUNK, 128), f32)
            dlast_row = jnp.zeros((1, 128), f32)
            hs = range(HEADS)
            cols = [slice(h * HEAD_W, (h + 1) * HEAD_W) for h in hs]
            q = [qk_ref[rows, h * ML_DQK:(h + 1) * ML_DQK] * (ML_DQK ** -0.5) for h in hs]
            k = [qk_ref[rows, HEADS * ML_DQK + h * ML_DQK:HEADS * ML_DQK + (h + 1) * ML_DQK] for h in hs]
            v = [v_ref[rows, h * HEAD_W:(h + 1) * HEAD_W] for h in hs]
            c_st = [cst_ref[c, h] for h in hs]
            n_st = [nst_ref[c, h] for h in hs]
            m_st = [mst_ref[c, h][:, 0:1] for h in hs]
            dc = [dc_sc[h] for h in hs]
            dn = [dn_sc[h] for h in hs]
            r = _ml_chunk(q, k, v, gates, gc, rowsf, c_st, n_st, m_st, tril_mask)
            z, wi, wo, ws, w_old, den = r["z"], r["wi"], r["wo"], r["ws"], r["w_old"], r["den"]
            hh = [r["num"][h] / z[h] for h in hs]
            dh = []
            for h in hs:
                gnh = gn_ref[:, cols[h]]
                oh, rstd, y = _head_rms(hh[h], gnh, on_mxu=True)
                sg = _sigmoid(mo_ref[rows, h * HEAD_W:(h + 1) * HEAD_W])
                dogh = dog_ref[rows, cols[h]]
                dy = dogh * sg
                dmo_ref[rows, cols[h]] = (dogh * y * (sg * (1.0 - sg))).astype(ACT)
                dgn_ref[:, cols[h]] += jnp.sum(dy * oh, axis=0, keepdims=True)
                doh = dy * gnh
                dh.append(rstd * (doh - oh * (_lane_sum(doh * oh) * (1.0 / HEAD_W))))
            dnum = [dh[h] / z[h] for h in hs]
            dz = [-_lane_sum(dh[h] * hh[h]) / z[h] for h in hs]
            dden = [jnp.where(jnp.abs(den[h]) > r["floor"][h], dz[h] * jnp.sign(den[h]), 0.0) for h in hs]
            dsw = [(_bdot_nt(dnum[h], v[h]) + dden[h]) * wi[h] for h in hs]
            dq = [_bdot(dsw[h], k[h]) + wo[h] * (_bdot_nt(dnum[h], c_st[h]) + dden[h][:, :ML_DQK] * n_st[h]) for h in hs]
            dk_state = [ws[h] * (_bdot_nt(v[h], dc[h]) + dn[h]) for h in hs]
            dk = [_bdot_tn(dsw[h], q[h]) + dk_state[h] for h in hs]
            dv = [_bdot_tn(r["qk"][h], dnum[h]) + ws[h] * _bdot(k[h], dc[h]) for h in hs]
            woq = [wo[h] * q[h] for h in hs]
            new_dc = [w_old[h] * dc[h] + _bdot_tn(woq[h], dnum[h]) for h in hs]
            for h in hs:
                dv_ref[rows, cols[h]] = dv[h].astype(ACT)
                dc_sc[h] = new_dc[h]
                dn_sc[h] = w_old[h] * dn[h] + jnp.sum(woq[h] * dden[h][:, :ML_DQK], axis=0, keepdims=True)
                d_last = (jnp.sum(jnp.sum(k[h] * dk_state[h], axis=0, keepdims=True), axis=-1, keepdims=True)
                          + w_old[h] * (jnp.sum(jnp.sum(dc[h] * c_st[h], axis=0, keepdims=True), axis=-1, keepdims=True)
                                        + jnp.sum(dn[h] * n_st[h], axis=-1, keepdims=True)))
                kdk = _lane_sum(k[h] * dk[h])
                qdq = _lane_sum(q[h] * dq[h])
                dg_mat = dg_mat + jnp.where(lane == HEADS + h, qdq - kdk, 0.0)
                dlast_row = dlast_row + jnp.where(lane[0:1] == HEADS + h, d_last, 0.0)
                dig_mat = dig_mat + jnp.where(lane == h, kdk, 0.0)
                dqk_ref[rows, h * ML_DQK:(h + 1) * ML_DQK] = dq[h] * (ML_DQK ** -0.5)
                dqk_ref[rows, HEADS * ML_DQK + h * ML_DQK:HEADS * ML_DQK + (h + 1) * ML_DQK] = dk[h]
            dlf = _sel_dot(triu, dg_mat) + dlast_row
            dgt_ref[rows, :] = (dig_mat + dlf * _sigmoid(-gates)).astype(ACT)
            return carry

        lax.fori_loop(0, nc_blk, chunk, 0, unroll=True)

    st4 = lambda a, b: pl.BlockSpec((nc_blk, HEADS, a, b), lambda i: (nb - 1 - i, 0, 0, 0))
    return _riding_call(
        body, "mlstm_bwd", nb,
        in_specs=[_rows_rev(tb, MIX_W, nb), _rows_rev(tb, MIX_W, nb, 1), _rows_rev(tb, MIX_W, nb, 2), _rows_rev(tb, 128, nb, 12),
                  _const((1, MIX_W)), st4(ML_DQK, HEAD_W), st4(1, ML_DQK), st4(1, 128), _rows_rev(tb, MIX_W, nb)],
        out_specs=[_rows_rev(tb, MIX_W, nb), _rows_rev(tb, MIX_W, nb), _rows_rev(tb, MIX_W, nb), _rows_rev(tb, 128, nb), _const((1, MIX_W))],
        out_shape=[jax.ShapeDtypeStruct((t, MIX_W), f32), jax.ShapeDtypeStruct((t, MIX_W), ACT), jax.ShapeDtypeStruct((t, MIX_W), ACT),
                   jax.ShapeDtypeStruct((t, 128), ACT), jax.ShapeDtypeStruct((1, MIX_W), f32)],
        scratch_shapes=[pltpu.VMEM((HEADS, ML_DQK, HEAD_W), f32), pltpu.VMEM((HEADS, 1, ML_DQK), f32)],
        operands=(qkc, u_ml, u_ml, u_ml, gn, cst, nst, mst, dog), riders=riders, copies=_scatter_copies,
        ride_shapes=[jax.ShapeDtypeStruct(r.shape, r.dtype) for r in riders])


def _ln_fwd(r, g, b):
    mu = jnp.mean(r, axis=-1, keepdims=True)
    xc = r - mu
    rstd = lax.rsqrt(jnp.mean(xc * xc, axis=-1, keepdims=True) + LN_EPS)
    xh = xc * rstd
    return xh * g + b, xh, rstd


def _ln_bwd(dy, xh, rstd, g):
    dxh = dy * g
    return rstd * (dxh - jnp.mean(dxh, axis=-1, keepdims=True) - xh * jnp.mean(dxh * xh, axis=-1, keepdims=True))


def _outproj_ln1(og_hg, og_ml, x, w_out, g, b, riders=()):
    t = x.shape[0]
    tm = _tile(t, LIGHT_ROWS)

    def body(a_ref, b_ref, x_ref, w_ref, g_ref, bb_ref, x1_ref, xh_ref, rs_ref, x1b_ref):
        mix = _bdot(a_ref[...], w_ref[0:MIX_W, :]) + _bdot(b_ref[...], w_ref[MIX_W:2 * MIX_W, :])
        y, xh, rstd = _ln_fwd(ALPHA * x_ref[...] + mix, g_ref[...], bb_ref[...])
        x1_ref[...] = y
        x1b_ref[...] = y.astype(ACT)
        xh_ref[...] = xh.astype(ACT)
        rs_ref[...] = rstd

    return _riding_call(
        body, "outproj_ln1", t // tm,
        in_specs=[_rows(tm, MIX_W), _rows(tm, MIX_W), _rows(tm, D_MODEL), _resident((D_MODEL, D_MODEL)), _const((1, D_MODEL)), _const((1, D_MODEL))],
        out_specs=[_rows(tm, D_MODEL), _rows(tm, D_MODEL), _rows(tm, 1), _rows(tm, D_MODEL)],
        out_shape=[jax.ShapeDtypeStruct((t, D_MODEL), f32), jax.ShapeDtypeStruct((t, D_MODEL), ACT), jax.ShapeDtypeStruct((t, 1), f32),
                   jax.ShapeDtypeStruct((t, D_MODEL), ACT)],
        scratch_shapes=[], operands=(og_hg, og_ml, x, w_out, g, b), riders=riders, copies=_gather_copies, ride_shapes=_gather_shapes(riders))


def _ffn_up(x1, wg, wu, riders=()):
    t = x1.shape[0]
    tm = _tile(t, DENSE_ROWS)

    def body(x_ref, wg_ref, wu_ref, hg_ref, up_ref, a_ref):
        xv = x_ref[...]
        hg = _bdot_nt(xv, wg_ref[...])
        up = _bdot_nt(xv, wu_ref[...])
        hg_ref[...] = hg.astype(ACT)
        up_ref[...] = up.astype(ACT)
        a_ref[...] = (hg * _sigmoid(hg) * up).astype(ACT)

    return _riding_call(
        body, "ffn_up", t // tm,
        in_specs=[_rows(tm, D_MODEL), _resident((D_FF, D_MODEL)), _resident((D_FF, D_MODEL))],
        out_specs=[_rows(tm, D_FF), _rows(tm, D_FF), _rows(tm, D_FF)],
        out_shape=[jax.ShapeDtypeStruct((t, D_FF), ACT), jax.ShapeDtypeStruct((t, D_FF), ACT), jax.ShapeDtypeStruct((t, D_FF), ACT)],
        scratch_shapes=[], operands=(x1, wg, wu), riders=riders, copies=_gather_copies, ride_shapes=_gather_shapes(riders))


def _ffn_down_ln2(a, x1, wd, g, b):
    t = x1.shape[0]
    tm = _tile(t, LIGHT_ROWS)

    def body(a_ref, x_ref, w_ref, g_ref, bb_ref, x2_ref, xh_ref, rs_ref, x2b_ref):
        ffn = _bdot(a_ref[...], w_ref[...])
        y, xh, rstd = _ln_fwd(ALPHA * x_ref[...] + ffn, g_ref[...], bb_ref[...])
        x2_ref[...] = y
        x2b_ref[...] = y.astype(ACT)
        xh_ref[...] = xh.astype(ACT)
        rs_ref[...] = rstd

    return pl.pallas_call(
        body, name="ffn_down_ln2", grid=(t // tm,),
        in_specs=[_rows(tm, D_FF), _rows(tm, D_MODEL), _resident((D_FF, D_MODEL)), _const((1, D_MODEL)), _const((1, D_MODEL))],
        out_specs=[_rows(tm, D_MODEL), _rows(tm, D_MODEL), _rows(tm, 1), _rows(tm, D_MODEL)],
        out_shape=[jax.ShapeDtypeStruct((t, D_MODEL), f32), jax.ShapeDtypeStruct((t, D_MODEL), ACT), jax.ShapeDtypeStruct((t, 1), f32),
                   jax.ShapeDtypeStruct((t, D_MODEL), ACT)],
        compiler_params=_cparams(1, arbitrary=False),
    )(a, x1, wd, g, b)


def _head_loss_bwd(x2, xh2, rs2, p, tgt, w_pg, b_pg, w_pp, g2):
    t = x2.shape[0]
    tm = _tile(t, LIGHT_ROWS)

    def body(x_ref, xh_ref, rs_ref, p_ref, t_ref, wg_ref, bg_ref, wp_ref, g_ref,
             dr_ref, de_ref, dz_ref, loss_ref, dbg_ref, dg2_ref, db2_ref):
        @pl.when(pl.program_id(0) == 0)
        def _():
            loss_ref[...] = jnp.zeros_like(loss_ref)
            dbg_ref[...] = jnp.zeros_like(dbg_ref)
            dg2_ref[...] = jnp.zeros_like(dg2_ref)
            db2_ref[...] = jnp.zeros_like(db2_ref)

        x2v = x_ref[...]
        z = _bdot(x2v, wg_ref[...]) + bg_ref[...]
        e = _bdot(p_ref[...], wp_ref[...])
        sg = _sigmoid(z)
        diff = x2v + sg * e - t_ref[...]
        loss_ref[...] += 0.5 * jnp.sum(jnp.mean(diff * diff, axis=-1, keepdims=True), axis=0, keepdims=True)
        dy = diff * (1.0 / D_MODEL)
        de_ref[...] = (dy * sg).astype(ACT)
        dz = dy * e * (sg * (1.0 - sg))
        dz_ref[...] = dz.astype(ACT)
        dbg_ref[...] += jnp.sum(dz, axis=0, keepdims=True)
        dx2 = dy + _bdot_nt(dz, wg_ref[...])
        xh = xh_ref[...].astype(f32)
        dg2_ref[...] += jnp.sum(dx2 * xh, axis=0, keepdims=True)
        db2_ref[...] += jnp.sum(dx2, axis=0, keepdims=True)
        dr_ref[...] = _ln_bwd(dx2, xh, rs_ref[...], g_ref[...])

    row = jax.ShapeDtypeStruct((1, D_MODEL), f32)
    return pl.pallas_call(
        body, name="head_loss_bwd", grid=(t // tm,),
        in_specs=[_rows(tm, D_MODEL), _rows(tm, D_MODEL), _rows(tm, 1), _rows(tm, PLE), _rows(tm, D_MODEL),
                  _resident((D_MODEL, D_MODEL)), _const((1, D_MODEL)), _resident((PLE, D_MODEL)), _const((1, D_MODEL))],
        out_specs=[_rows(tm, D_MODEL), _rows(tm, D_MODEL), _rows(tm, D_MODEL), _const((1, 1)), _const((1, D_MODEL)), _const((1, D_MODEL)), _const((1, D_MODEL))],
        out_shape=[jax.ShapeDtypeStruct((t, D_MODEL), f32), jax.ShapeDtypeStruct((t, D_MODEL), ACT), jax.ShapeDtypeStruct((t, D_MODEL), ACT),
                   jax.ShapeDtypeStruct((1, 1), f32), row, row, row],
        compiler_params=_cparams(1),
    )(x2, xh2, rs2, p, tgt, w_pg, b_pg, w_pp, g2)


def _ffn_bwd(dr2, hg, up, xh1, rs1, wd, wg, wu, g1, w_out):
    t = dr2.shape[0]
    tm = _tile(t, DENSE_ROWS // 2)

    def body(dr_ref, hg_ref, up_ref, xh_ref, rs_ref, wd_ref, wg_ref, wu_ref, g_ref, wo_ref,
             dr1_ref, dhg_ref, dup_ref, dg1_ref, db1_ref, doghg_ref, dogml_ref):
        @pl.when(pl.program_id(0) == 0)
        def _():
            dg1_ref[...] = jnp.zeros_like(dg1_ref)
            db1_ref[...] = jnp.zeros_like(db1_ref)

        dr2v = dr_ref[...]
        da = _bdot_nt(dr2v, wd_ref[...])
        hgv = hg_ref[...].astype(f32)
        sg = _sigmoid(hgv)
        dhg = da * up_ref[...].astype(f32) * (sg * (1.0 + hgv * (1.0 - sg)))
        dup = da * (hgv * sg)
        dhg_ref[...] = dhg.astype(ACT)
        dup_ref[...] = dup.astype(ACT)
        dx1 = ALPHA * dr2v + _bdot(dhg, wg_ref[...]) + _bdot(dup, wu_ref[...])
        xh = xh_ref[...].astype(f32)
        dg1_ref[...] += jnp.sum(dx1 * xh, axis=0, keepdims=True)
        db1_ref[...] += jnp.sum(dx1, axis=0, keepdims=True)
        dr1 = _ln_bwd(dx1, xh, rs_ref[...], g_ref[...])
        dr1_ref[...] = dr1
        dog = _bdot_nt(dr1, wo_ref[...])
        doghg_ref[...] = dog[:, 0:MIX_W]
        dogml_ref[...] = dog[:, MIX_W:2 * MIX_W]

    row = jax.ShapeDtypeStruct((1, D_MODEL), f32)
    return pl.pallas_call(
        body, name="ffn_bwd", grid=(t // tm,),
        in_specs=[_rows(tm, D_MODEL), _rows(tm, D_FF), _rows(tm, D_FF), _rows(tm, D_MODEL), _rows(tm, 1),
                  _resident((D_FF, D_MODEL)), _resident((D_FF, D_MODEL)), _resident((D_FF, D_MODEL)), _const((1, D_MODEL)),
                  _resident((D_MODEL, D_MODEL))],
        out_specs=[_rows(tm, D_MODEL), _rows(tm, D_FF), _rows(tm, D_FF), _const((1, D_MODEL)), _const((1, D_MODEL)),
                   _rows(tm, MIX_W), _rows(tm, MIX_W)],
        out_shape=[jax.ShapeDtypeStruct((t, D_MODEL), f32), jax.ShapeDtypeStruct((t, D_FF), ACT), jax.ShapeDtypeStruct((t, D_FF), ACT), row, row,
                   jax.ShapeDtypeStruct((t, MIX_W), f32), jax.ShapeDtypeStruct((t, MIX_W), f32)],
        compiler_params=_cparams(1),
    )(dr2, hg, up, xh1, rs1, wd, wg, wu, g1, w_out)


def _inproj_bwd(dr1, du_hg, dqk, dmv, dmo, dgt, w_hg, w_ml):
    t = dr1.shape[0]
    tm = _tile(t, DENSE_ROWS)

    def body(dr_ref, dhg_ref, dqk_ref, dmv_ref, dmo_ref, dgt_ref, whg_ref, wml_ref, gx_ref, dml_ref):
        dml = jnp.concatenate([dqk_ref[...], dmv_ref[...], dmo_ref[...], dgt_ref[...]], axis=-1).astype(ACT)
        dml_ref[...] = dml
        gx_ref[...] = ALPHA * dr_ref[...] + _bdot(dhg_ref[...], whg_ref[...]) + _bdot(dml, wml_ref[...])

    return pl.pallas_call(
        body, name="inproj_bwd", grid=(t // tm,),
        in_specs=[_rows(tm, D_MODEL), _rows(tm, U_HG), _rows(tm, MIX_W), _rows(tm, MIX_W), _rows(tm, MIX_W), _rows(tm, 128),
                  _resident((U_HG, D_MODEL)), _resident((U_ML, D_MODEL))],
        out_specs=[_rows(tm, D_MODEL), _rows(tm, U_ML)],
        out_shape=[jax.ShapeDtypeStruct((t, D_MODEL), f32), jax.ShapeDtypeStruct((t, U_ML), ACT)],
        compiler_params=_cparams(1, arbitrary=False),
    )(dr1, du_hg, dqk, dmv, dmo, dgt, w_hg, w_ml)


def _wgrad(a, b, name, tk=None, tn=None, colsum=False, low=False):
    t, kdim = a.shape
    n = b.shape[1]
    tk = tk or kdim
    tn = tn or n
    tt = _tile(t, WGRAD_ROWS)
    nt = t // tt
    assert not (colsum and low) and (not colsum or tn == n)

    def body(a_ref, b_ref, o_ref, *s_ref):
        @pl.when(pl.program_id(2) == 0)
        def _():
            o_ref[...] = jnp.zeros_like(o_ref)
            if colsum:
                s_ref[0][...] = jnp.zeros_like(s_ref[0])

        av = a_ref[...]
        o_ref[...] += _bdot_tn(av, b_ref[...])
        if colsum:
            s_ref[0][...] += jnp.sum(av.astype(f32), axis=0, keepdims=True)
        if low:
            @pl.when(pl.program_id(2) == nt - 1)
            def _():
                s_ref[0][...] = o_ref[...].astype(bf16)

    out_specs = [pl.BlockSpec((tk, tn), lambda i, j, s: (i, j))]
    out_shape = [jax.ShapeDtypeStruct((kdim, n), f32)]
    if colsum:
        out_specs.append(pl.BlockSpec((1, tk), lambda i, j, s: (0, i)))
        out_shape.append(jax.ShapeDtypeStruct((1, kdim), f32))
    if low:
        out_specs.append(pl.BlockSpec((tk, tn), lambda i, j, s: (i, j)))
        out_shape.append(jax.ShapeDtypeStruct((kdim, n), bf16))
    res = pl.pallas_call(
        body, name=name, grid=(kdim // tk, n // tn, t // tt),
        in_specs=[pl.BlockSpec((tt, tk), lambda i, j, s: (s, i)), pl.BlockSpec((tt, tn), lambda i, j, s: (s, j))],
        out_specs=out_specs, out_shape=out_shape,
        compiler_params=_cparams(3),
    )(a, b)
    return res if (colsum or low) else res[0]


_TRANSPOSED = {"w_in", "w_ffn_gate", "w_ffn_up"}
_COL_SPLIT = {"ple_w_proj"}
_SCATTER_PLAN = (("w_ffn_gate", "w_ffn_up"), ("w_ffn_down", "w_out", "ple_w_gate", "ple_w_proj"))
_RIDE_PLAN = {"inproj": ("w_ffn_gate",), "hgrn2_fwd": ("w_ffn_up",), "mlstm_fwd": ("w_out",),
              "outproj_ln1": ("ple_w_gate", "ple_w_proj"), "ffn_up": ("w_ffn_down",)}


def _from_chip_major(a, col_split):
    if col_split:
        return a.transpose(1, 0, 2).reshape(a.shape[1], 4 * a.shape[2])
    return a.reshape(4 * a.shape[1], a.shape[2])


def _local_step(x, p, tgt, w_in_b, b_in, logits, conv_w, conv_b, hg_gn, ml_gn, w_out_b, ln1_g, ln1_b,
                wg_b, wu_b, wd_b, ln2_g, ln2_b, w_pp_b, w_pg_b, b_pg, early_hook=None, late_shards=None):
    pad_w = U_HG + U_ML - PROJ_W
    w_hg = w_in_b[:U_HG]
    w_ml = jnp.pad(w_in_b[U_HG:], ((0, pad_w), (0, 0)))
    bb_hg = b_in[:, :U_HG]
    bb_ml = jnp.pad(b_in[:, U_HG:], ((0, 0), (0, pad_w)))

    late = dict(w_out=w_out_b, w_ffn_gate=wg_b, w_ffn_up=wu_b, w_ffn_down=wd_b, ple_w_proj=w_pp_b, ple_w_gate=w_pg_b)

    def riders_of(call):
        return [late_shards[k] for k in _RIDE_PLAN[call]] if late_shards is not None else ()

    def arrived(call, got):
        for k, g in zip(_RIDE_PLAN[call], got):
            late[k] = _from_chip_major(g, k in _COL_SPLIT)

    (u_hg, u_ml, xb), got = _inproj(x, w_hg, w_ml, bb_hg, bb_ml, riders_of("inproj"))
    arrived("inproj", got)
    (og_hg, sst, hg_b, hg_a, hg_o), got = _hgrn2_fwd(u_hg, logits, hg_gn, riders_of("hgrn2_fwd"))
    arrived("hgrn2_fwd", got)
    pre, qkc = _conv_fwd(u_ml, conv_w, conv_b)
    (og_ml, cst, nst, mst), got = _mlstm_fwd(qkc, u_ml, ml_gn, riders_of("mlstm_fwd"))
    arrived("mlstm_fwd", got)
    (x1, xh1, rs1, x1b), got = _outproj_ln1(og_hg, og_ml, x, late["w_out"], ln1_g, ln1_b, riders_of("outproj_ln1"))
    arrived("outproj_ln1", got)
    (hgp, up, act), got = _ffn_up(x1b, late["w_ffn_gate"], late["w_ffn_up"], riders_of("ffn_up"))
    arrived("ffn_up", got)
    w_out_b, wg_b, wu_b, wd_b = late["w_out"], late["w_ffn_gate"], late["w_ffn_up"], late["w_ffn_down"]
    w_pp_b, w_pg_b = late["ple_w_proj"], late["ple_w_gate"]
    x2, xh2, rs2, x2b = _ffn_down_ln2(act, x1, wd_b, ln2_g, ln2_b)
    dr2, de, dz, loss, d_bpg, d_ln2g, d_ln2b = _head_loss_bwd(x2, xh2, rs2, p, tgt, w_pg_b, b_pg, w_pp_b, ln2_g)
    dr1, dhg, dup, d_ln1g, d_ln1b, dog_hg, dog_ml = _ffn_bwd(dr2, hgp, up, xh1, rs1, wd_b, wg_b, wu_b, ln1_g, w_out_b)

    d_wo_a, lo_wo_a = _wgrad(og_hg, dr1, "wgrad_out_hg", low=True)
    d_wo_b, lo_wo_b = _wgrad(og_ml, dr1, "wgrad_out_ml", low=True)
    d_wg, lo_wg = _wgrad(dhg, x1b, "wgrad_ffn_gate", tk=D_FF // 2, low=True)
    d_wu, lo_wu = _wgrad(dup, x1b, "wgrad_ffn_up", tk=D_FF // 2, low=True)
    d_wd, lo_wd = _wgrad(act, dr2, "wgrad_ffn_down", tk=D_FF // 2, low=True)
    d_wpp, lo_wpp = _wgrad(p, de, "wgrad_ple_proj", low=True)
    d_wpg, lo_wpg = _wgrad(x2b, dz, "wgrad_ple_gate", low=True)
    early = dict(w_out=jnp.concatenate([d_wo_a, d_wo_b], axis=0), w_ffn_gate=d_wg, w_ffn_up=d_wu, w_ffn_down=d_wd,
                 ple_w_proj=d_wpp, ple_w_gate=d_wpg)
    early_low = dict(w_out=jnp.concatenate([lo_wo_a, lo_wo_b], axis=0), w_ffn_gate=lo_wg, w_ffn_up=lo_wu, w_ffn_down=lo_wd,
                     ple_w_proj=lo_wpp, ple_w_gate=lo_wpg)
    ride_hg, ride_ml = early_hook(early_low) if early_hook is not None else ((), ())

    (du_hg, d_logits, d_hg_gn), got_hg = _hgrn2_bwd(u_hg, logits, hg_gn, sst, hg_b, hg_a, hg_o, dog_hg, ride_hg)
    (dqkc, dmv, dmo, dgt, d_ml_gn), got_ml = _mlstm_bwd(qkc, u_ml, ml_gn, cst, nst, mst, dog_ml, ride_ml)
    dqk, d_conv_w, d_conv_b = _conv_bwd(u_ml, conv_w, pre, dqkc)
    grad_x, du_ml = _inproj_bwd(dr1, du_hg, dqk, dmv, dmo, dgt, w_hg, w_ml)

    dw_hg, db_hg = _wgrad(du_hg, xb, "wgrad_in_hg", tk=U_HG // 2, colsum=True)
    dw_ml, db_ml = _wgrad(du_ml, xb, "wgrad_in_ml", colsum=True)
    d_w_in = jnp.concatenate([dw_hg, dw_ml[:PROJ_W - U_HG]], axis=0)
    d_b_in = jnp.concatenate([db_hg, db_ml[:, :PROJ_W - U_HG]], axis=1)

    grads = dict(w_in=d_w_in, b_in=d_b_in, hg_lb_logits=d_logits, ml_conv_w=d_conv_w, ml_conv_b=d_conv_b,
                 hg_norm_g=d_hg_gn, ml_norm_g=d_ml_gn, ln1_g=d_ln1g, ln1_b=d_ln1b, ln2_g=d_ln2g, ln2_b=d_ln2b,
                 ple_b_gate=d_bpg, **early)
    return loss, grad_x, grads, (list(got_hg), list(got_ml))


_ANY = pl.BlockSpec(memory_space=pltpu.HBM)
_MESH = pl.DeviceIdType.MESH


def _my_place():
    return lax.axis_index("x"), lax.axis_index("y"), lax.axis_index("c")


def _other_chips(x, y):
    return [(1 - x, y), (x, 1 - y), (1 - x, 1 - y)]


_VMEM = pl.BlockSpec(memory_space=pltpu.VMEM)
_EX_ROWS = 32


def _pair_reduce_cols(p, name):
    s, r, c = p.shape
    hc = c // 2

    def body(p_ref, o_ref, other, send_sem, recv_sem):
        x, y, cc = _my_place()

        def run(mine_lo, theirs_lo):
            cp = pltpu.make_async_remote_copy(src_ref=p_ref.at[pl.ds(0, s), pl.ds(0, r), pl.ds(theirs_lo, hc)], dst_ref=other,
                                              send_sem=send_sem, recv_sem=recv_sem, device_id=(x, y, 1 - cc), device_id_type=_MESH)
            cp.start()
            cp.wait()
            for slot in range(s):
                o_ref[slot] = (p_ref[slot, :, mine_lo:mine_lo + hc] + other[slot]).astype(bf16)

        @pl.when(cc == 0)
        def _():
            run(0, hc)

        @pl.when(cc == 1)
        def _():
            run(hc, 0)

    return pl.pallas_call(
        body, name=name, in_specs=[_VMEM], out_specs=_VMEM,
        out_shape=jax.ShapeDtypeStruct((s, r, hc), bf16),
        scratch_shapes=[pltpu.VMEM((s, r, hc), f32), pltpu.SemaphoreType.DMA, pltpu.SemaphoreType.DMA],
        compiler_params=pltpu.CompilerParams(vmem_limit_bytes=VMEM_LIMIT),
    )(p)


def _reduce_adamw_cols(rcv, w, m, v, name):
    s, r, hc = rcv.shape

    def body(r_ref, w_ref, m_ref, v_ref, g_ref, d_ref, nm_ref, nv_ref, halves, send_sem, recv_sem):
        x, y, cc = _my_place()
        acc = r_ref[0].astype(f32)
        for slot in range(1, s):
            acc = acc + r_ref[slot].astype(f32)
        halves[cc] = acc
        cp = pltpu.make_async_remote_copy(src_ref=halves.at[cc], dst_ref=halves.at[cc], send_sem=send_sem, recv_sem=recv_sem,
                                          device_id=(x, y, 1 - cc), device_id_type=_MESH)
        cp.start()
        cp.wait()
        for k in range(2):
            cols = slice(k * hc, (k + 1) * hc)
            g = halves[k]
            nm = B1 * m_ref[:, cols] + (1.0 - B1) * g
            nv = B2 * v_ref[:, cols] + (1.0 - B2) * (g * g)
            g_ref[:, cols] = g
            nm_ref[:, cols] = nm
            nv_ref[:, cols] = nv
            d_ref[:, cols] = -LR * ((nm / (1.0 - B1 ** STEP)) / (jnp.sqrt(nv / (1.0 - B2 ** STEP)) + EPS_ADAM) + WD * w_ref[:, cols])

    return pl.pallas_call(
        body, name=name, in_specs=[_VMEM] * 4, out_specs=[_VMEM] * 4,
        out_shape=[jax.ShapeDtypeStruct((r, 2 * hc), f32)] * 4,
        scratch_shapes=[pltpu.VMEM((2, r, hc), f32), pltpu.SemaphoreType.DMA, pltpu.SemaphoreType.DMA],
        compiler_params=pltpu.CompilerParams(vmem_limit_bytes=VMEM_LIMIT),
    )(rcv, w, m, v)


def _reduce_adamw(rcv, w, m, v, name):
    s, r, c = rcv.shape
    rows_per = _EX_ROWS
    half = r // 2
    steps = half // rows_per

    def body(r_ref, w_ref, m_ref, v_ref, g_ref, d_ref, nm_ref, nv_ref, mine, theirs, send_sems, recv_sems):
        x, y, cc = _my_place()

        def swap(k):
            rs = pl.ds(k * half, half)
            return pltpu.make_async_remote_copy(src_ref=mine.at[rs], dst_ref=theirs.at[rs], send_sem=send_sems.at[k], recv_sem=recv_sems.at[k],
                                                device_id=(x, y, 1 - cc), device_id_type=_MESH)

        def chip_sum(i, carry):
            rs = pl.ds(pl.multiple_of(i * rows_per, rows_per), rows_per)
            acc = r_ref[0, rs, :].astype(f32)
            for slot in range(1, s):
                acc = acc + r_ref[slot, rs, :].astype(f32)
            mine[rs, :] = acc
            return carry

        def update(i, carry):
            rs = pl.ds(pl.multiple_of(i * rows_per, rows_per), rows_per)
            g = mine[rs, :] + theirs[rs, :]
            nm = B1 * m_ref[rs, :] + (1.0 - B1) * g
            nv = B2 * v_ref[rs, :] + (1.0 - B2) * (g * g)
            g_ref[rs, :] = g
            nm_ref[rs, :] = nm
            nv_ref[rs, :] = nv
            d_ref[rs, :] = -LR * ((nm / (1.0 - B1 ** STEP)) / (jnp.sqrt(nv / (1.0 - B2 ** STEP)) + EPS_ADAM) + WD * w_ref[rs, :])
            return carry

        lax.fori_loop(0, steps, chip_sum, 0)
        swap(0).start()
        lax.fori_loop(steps, 2 * steps, chip_sum, 0)
        swap(1).start()
        swap(0).wait()
        lax.fori_loop(0, steps, update, 0)
        swap(1).wait()
        lax.fori_loop(steps, 2 * steps, update, 0)

    return pl.pallas_call(
        body, name=name, in_specs=[_VMEM] * 4, out_specs=[_VMEM] * 4,
        out_shape=[jax.ShapeDtypeStruct((r, c), f32)] * 4,
        scratch_shapes=[pltpu.VMEM((r, c), f32), pltpu.VMEM((r, c), f32), pltpu.SemaphoreType.DMA((2,)), pltpu.SemaphoreType.DMA((2,))],
        compiler_params=pltpu.CompilerParams(vmem_limit_bytes=VMEM_LIMIT),
    )(rcv, w, m, v)


def _gather_copies(ins, outs, send_sems, recv_sems, local_sems):
    x, y, c = _my_place()
    me = 2 * x + y
    local, outgoing, incoming = [], [], []
    for a in range(len(ins)):
        local.append(pltpu.make_async_copy(ins[a], outs[a].at[me], local_sems.at[a]))
        for j, (px, py) in enumerate(_other_chips(x, y)):
            sems = dict(send_sem=send_sems.at[3 * a + j], recv_sem=recv_sems.at[3 * a + j], device_id=(px, py, c), device_id_type=_MESH)
            outgoing.append(pltpu.make_async_remote_copy(src_ref=ins[a], dst_ref=outs[a].at[me], **sems))
            incoming.append(pltpu.make_async_remote_copy(src_ref=ins[a], dst_ref=outs[a].at[2 * px + py], **sems))
    return local, outgoing, incoming


def _gather_first(block, taps, name):
    r, c = block.shape
    hc = c // 2

    def body(in_ref, tap_in, out_ref, tap_out, send_sems, recv_sems):
        x, y, cc = _my_place()
        me = 2 * x + y
        sibling = (x, y, 1 - cc)
        chips = _other_chips(x, y)
        out_ref[me] = in_ref[...]
        tap_out[me] = tap_in[...]

        def run(mine, theirs):
            def ici(j, chip):
                px, py = chips[j]
                src = in_ref.at[pl.ds(0, r), pl.ds(mine, hc)] if chip is None else out_ref.at[chip, pl.ds(0, r), pl.ds(mine, hc)]
                dst = out_ref.at[me if chip is None else chip, pl.ds(0, r), pl.ds(mine, hc)]
                return pltpu.make_async_remote_copy(src_ref=src, dst_ref=dst, send_sem=send_sems.at[j], recv_sem=recv_sems.at[j],
                                                    device_id=(px, py, cc), device_id_type=_MESH)

            def d2d(j, lo):
                px, py = chips[j]
                blk = out_ref.at[2 * px + py, pl.ds(0, r), pl.ds(lo, hc)]
                return pltpu.make_async_remote_copy(src_ref=blk, dst_ref=blk, send_sem=send_sems.at[3 + j], recv_sem=recv_sems.at[3 + j],
                                                    device_id=sibling, device_id_type=_MESH)

            def tap(j, chip):
                px, py = chips[j]
                return pltpu.make_async_remote_copy(src_ref=tap_in, dst_ref=tap_out.at[me if chip is None else chip],
                                                    send_sem=send_sems.at[6 + j], recv_sem=recv_sems.at[6 + j],
                                                    device_id=(px, py, cc), device_id_type=_MESH)

            for j in range(3):
                ici(j, None).start()
                tap(j, None).start()
            for j, (px, py) in enumerate(chips):
                ici(j, 2 * px + py).wait_recv()
                d2d(j, mine).start()
            for j, (px, py) in enumerate(chips):
                d2d(j, theirs).wait_recv()
                tap(j, 2 * px + py).wait_recv()
            for j in range(3):
                ici(j, None).wait_send()
                d2d(j, mine).wait_send()
                tap(j, None).wait_send()

        @pl.when(cc == 0)
        def _():
            run(0, hc)

        @pl.when(cc == 1)
        def _():
            run(hc, 0)

    return pl.pallas_call(
        body, name=name, in_specs=[_VMEM, _VMEM], out_specs=[_VMEM, _VMEM],
        out_shape=[jax.ShapeDtypeStruct((4, r, c), block.dtype), jax.ShapeDtypeStruct((4,) + taps.shape, taps.dtype)],
        scratch_shapes=[pltpu.SemaphoreType.DMA((9,)), pltpu.SemaphoreType.DMA((9,))],
        compiler_params=pltpu.CompilerParams(vmem_limit_bytes=VMEM_LIMIT),
    )(block, taps)


def _riding_call(body, name, nsteps, in_specs, out_specs, out_shape, scratch_shapes, operands, riders, copies, ride_shapes):
    nr, n_in, n_out, n_scr = len(riders), len(in_specs), len(out_specs), len(scratch_shapes)

    def wrapped(*refs):
        ins, ride_in = refs[:n_in], refs[n_in:n_in + nr]
        outs, ride_out = refs[n_in + nr:n_in + nr + n_out], refs[n_in + nr + n_out:n_in + 2 * nr + n_out]
        scratch, sems = refs[n_in + 2 * nr + n_out:n_in + 2 * nr + n_out + n_scr], refs[n_in + 2 * nr + n_out + n_scr:]
        if nr:
            @pl.when(pl.program_id(0) == 0)
            def _():
                local, outgoing, _ = copies(ride_in, ride_out, *sems)
                for cp in local + outgoing:
                    cp.start()

        body(*ins, *outs, *scratch)
        if nr:
            @pl.when(pl.program_id(0) == nsteps - 1)
            def _():
                local, outgoing, incoming = copies(ride_in, ride_out, *sems)
                for cp in incoming:
                    cp.wait_recv()
                for cp in outgoing:
                    cp.wait_send()
                for cp in local:
                    cp.wait()

    hbm = pl.BlockSpec(memory_space=pltpu.HBM)
    sems = [pltpu.SemaphoreType.DMA((3 * nr,)), pltpu.SemaphoreType.DMA((3 * nr,)), pltpu.SemaphoreType.DMA((nr,))] if nr else []
    res = pl.pallas_call(
        wrapped, name=name, grid=(nsteps,),
        in_specs=list(in_specs) + [hbm] * nr, out_specs=list(out_specs) + [hbm] * nr,
        out_shape=list(out_shape) + list(ride_shapes),
        scratch_shapes=list(scratch_shapes) + sems,
        compiler_params=_cparams(1),
    )(*operands, *riders)
    return list(res[:n_out]), list(res[n_out:])


def _gather_shapes(riders):
    return [jax.ShapeDtypeStruct((4,) + r.shape, r.dtype) for r in riders]


def _scatter_copies(ins, outs, send_sems, recv_sems, local_sems):
    x, y, c = _my_place()
    me = 2 * x + y
    local, outgoing, incoming = [], [], []
    for a in range(len(ins)):
        local.append(pltpu.make_async_copy(ins[a].at[me], outs[a].at[me], local_sems.at[a]))
        for j, (px, py) in enumerate(_other_chips(x, y)):
            sems = dict(send_sem=send_sems.at[3 * a + j], recv_sem=recv_sems.at[3 * a + j], device_id=(px, py, c), device_id_type=_MESH)
            outgoing.append(pltpu.make_async_remote_copy(src_ref=ins[a].at[2 * px + py], dst_ref=outs[a].at[me], **sems))
            incoming.append(pltpu.make_async_remote_copy(src_ref=ins[a].at[2 * px + py], dst_ref=outs[a].at[2 * px + py], **sems))
    return local, outgoing, incoming


def _scatter_chips_gather_all(piece, block, name):
    def body(p_ref, b_ref, sc_ref, ga_ref, send_sems, recv_sems, local_sems, g_send, g_recv, g_local):
        x, y, c = _my_place()
        me = 4 * x + 2 * y + c
        own = pltpu.make_async_copy(b_ref, ga_ref.at[me], g_local)
        own.start()
        peers = []
        for dx in range(2):
            for dy in range(2):
                for dc in range(2):
                    if dx or dy or dc:
                        peers.append((1 - x if dx else x, 1 - y if dy else y, 1 - c if dc else c))
        for j, pr in enumerate(peers):
            pltpu.make_async_remote_copy(src_ref=b_ref, dst_ref=ga_ref.at[me], send_sem=g_send.at[j], recv_sem=g_recv.at[j],
                                         device_id=pr, device_id_type=_MESH).start()
        local, outgoing, incoming = _scatter_copies([p_ref], [sc_ref], send_sems, recv_sems, local_sems)
        for cp in local + outgoing:
            cp.start()
        for cp in incoming:
            cp.wait_recv()
        for cp in outgoing:
            cp.wait_send()
        for cp in local:
            cp.wait()
        for j, (px, py, pc) in enumerate(peers):
            pltpu.make_async_remote_copy(src_ref=b_ref, dst_ref=ga_ref.at[4 * px + 2 * py + pc], send_sem=g_send.at[j], recv_sem=g_recv.at[j],
                                         device_id=(px, py, pc), device_id_type=_MESH).wait()
        own.wait()

    return pl.pallas_call(
        body, name=name,
        in_specs=[_ANY, _ANY], out_specs=[_ANY, _ANY],
        out_shape=[jax.ShapeDtypeStruct(piece.shape, piece.dtype), jax.ShapeDtypeStruct((8,) + block.shape, block.dtype)],
        scratch_shapes=[pltpu.SemaphoreType.DMA((3,)), pltpu.SemaphoreType.DMA((3,)), pltpu.SemaphoreType.DMA((1,)),
                        pltpu.SemaphoreType.DMA((7,)), pltpu.SemaphoreType.DMA((7,)), pltpu.SemaphoreType.DMA],
    )(piece, block)


def _row_tile(r, c):
    best = r
    for cand in range(16, r + 1, 16):
        if r % cand == 0 and cand * c * 4 <= (1 << 20):
            best = cand
    return best if best * c * 4 <= (4 << 20) else r


def _sum_slots(parts, name):
    n, r, c = parts.shape
    tr = _row_tile(r, c)

    def body(p_ref, o_ref):
        acc = p_ref[0].astype(f32)
        for s in range(1, n):
            acc = acc + p_ref[s].astype(f32)
        o_ref[...] = acc

    return pl.pallas_call(
        body, name=name, grid=(r // tr,),
        in_specs=[pl.BlockSpec((n, tr, c), lambda i: (0, i, 0))],
        out_specs=pl.BlockSpec((tr, c), lambda i: (i, 0)),
        out_shape=jax.ShapeDtypeStruct((r, c), f32),
        compiler_params=_cparams(1, arbitrary=False),
    )(parts)


def _adamw(parts, w, m, v, name):
    n, r, c = parts.shape
    tr = _row_tile(r, c)
    tc = c
    if tr == r and r * c * 4 > (1 << 20) and c % 256 == 0:
        tc = 256

    def body(p_ref, w_ref, m_ref, v_ref, g_ref, d_ref, nm_ref, nv_ref):
        g = p_ref[0]
        for s in range(1, n):
            g = g + p_ref[s]
        nm = B1 * m_ref[...] + (1.0 - B1) * g
        nv = B2 * v_ref[...] + (1.0 - B2) * (g * g)
        m_hat = nm / (1.0 - B1 ** STEP)
        v_hat = nv / (1.0 - B2 ** STEP)
        g_ref[...] = g
        nm_ref[...] = nm
        nv_ref[...] = nv
        d_ref[...] = -LR * (m_hat / (jnp.sqrt(v_hat) + EPS_ADAM) + WD * w_ref[...])

    blk = pl.BlockSpec((tr, tc), lambda i, j: (i, j))
    return pl.pallas_call(
        body, name=name, grid=(r // tr, c // tc),
        in_specs=[pl.BlockSpec((n, tr, tc), lambda i, j: (0, i, j)), blk, blk, blk],
        out_specs=[blk] * 4,
        out_shape=[jax.ShapeDtypeStruct((r, c), f32)] * 4,
        compiler_params=_cparams(2, arbitrary=False),
    )(parts, w, m, v)


_BIG = ["w_in", "w_out", "w_ffn_gate", "w_ffn_up", "w_ffn_down", "ple_w_proj", "ple_w_gate"]
_SMALL = ["b_in", "hg_lb_logits", "ml_conv_w", "ml_conv_b", "hg_norm_g", "ml_norm_g", "ln1_g", "ln1_b", "ln2_g", "ln2_b", "ple_b_gate"]
_ORDER = ["w_in", "b_in", "hg_lb_logits", "ml_conv_w", "ml_conv_b", "hg_norm_g", "ml_norm_g", "w_out", "ln1_g", "ln1_b",
          "w_ffn_gate", "w_ffn_up", "w_ffn_down", "ln2_g", "ln2_b", "ple_w_proj", "ple_w_gate", "ple_b_gate"]
_PACK_ROWS, _PACK_COLS = 16, 1024


def _pack(arrays):
    flat = jnp.concatenate([a.reshape(-1) for a in arrays])
    return jnp.pad(flat, (0, _PACK_ROWS * _PACK_COLS - flat.shape[0])).reshape(_PACK_ROWS, _PACK_COLS)


def _unpack(pack, shapes):
    flat = pack.reshape(-1)
    out, off = [], 0
    for s in shapes:
        size = 1
        for d in s:
            size *= d
        out.append(flat[off:off + size].reshape(s))
        off += size
    return out


def _to_chip_major(g, col_split):
    if col_split:
        k, n = g.shape
        return g.reshape(k, 4, n // 4).transpose(1, 0, 2)
    k, n = g.shape
    return g.reshape(4, k // 4, n)


def kernel(x, p, w_in, b_in, hg_lb_logits, ml_conv_w, ml_conv_b, hg_norm_g, ml_norm_g, w_out, ln1_g, ln1_b, w_ffn_gate, w_ffn_up, w_ffn_down, ln2_g, ln2_b, ple_w_proj, ple_w_gate, ple_b_gate, loss_target, m_w_in, m_b_in, m_hg_lb_logits, m_ml_conv_w, m_ml_conv_b, m_hg_norm_g, m_ml_norm_g, m_w_out, m_ln1_g, m_ln1_b, m_w_ffn_gate, m_w_ffn_up, m_w_ffn_down, m_ln2_g, m_ln2_b, m_ple_w_proj, m_ple_w_gate, m_ple_b_gate, v_w_in, v_b_in, v_hg_lb_logits, v_ml_conv_w, v_ml_conv_b, v_hg_norm_g, v_ml_norm_g, v_w_out, v_ln1_g, v_ln1_b, v_w_ffn_gate, v_w_ffn_up, v_w_ffn_down, v_ln2_g, v_ln2_b, v_ple_w_proj, v_ple_w_gate, v_ple_b_gate):
    args = dict(locals())
    wts = {k: args[k] for k in _ORDER}
    mom = {k: args["m_" + k] for k in _ORDER}
    var = {k: args["v_" + k] for k in _ORDER}
    two_d = lambda a: a.reshape(a.shape[-2], a.shape[-1])
    block = lambda k, a: jnp.swapaxes(two_d(a), 0, 1) if k in _TRANSPOSED else two_d(a)
    unblock = lambda k, a: (jnp.swapaxes(a, 0, 1) if k in _TRANSPOSED else a).reshape(wts[k].shape)

    shards = {k: block(k, wts[k]).astype(bf16) for k in _BIG}
    w_in_blocks, taps = _gather_first(shards["w_in"], two_d(ml_conv_w), "gather_w_in")
    w_in_full = _from_chip_major(w_in_blocks, False)
    conv_w_full = _from_chip_major(taps, True)

    early_keys = _BIG[1:]
    loss, grad_x, grads, (got_hg, got_ml) = _local_step(
        x[0], p[0, 0], loss_target[0], w_in_full, b_in, hg_lb_logits, conv_w_full, ml_conv_b, hg_norm_g, ml_norm_g,
        None, ln1_g, ln1_b, None, None, None, ln2_g, ln2_b, None, None, ple_b_gate,
        early_hook=lambda low: tuple([_to_chip_major(low[k], k in _COL_SPLIT) for k in names] for names in _SCATTER_PLAN),
        late_shards={k: shards[k] for k in early_keys})

    out_g, out_d, out_m, out_v = {}, {}, {}, {}

    def finish(k, g, d, nm, nv):
        out_g[k], out_d[k], out_m[k], out_v[k] = unblock(k, g), unblock(k, d), unblock(k, nm), unblock(k, nv)

    for names, got in zip(_SCATTER_PLAN, (got_hg, got_ml)):
        for k, rcv in zip(names, got):
            finish(k, *_reduce_adamw(rcv, block(k, wts[k]), block(k, mom[k]), block(k, var[k]), "reduce_adamw_" + k))

    core_sums = _pair_reduce_cols(grads["w_in"][None], "pair_reduce_w_in")[0]
    contrib = _pack([grads[k] for k in _SMALL] + [loss])
    rcv, gathered = _scatter_chips_gather_all(_to_chip_major(core_sums, False), contrib, "scatter_grad_w_in_gather_small")
    finish("w_in", *_reduce_adamw_cols(rcv, block("w_in", wts["w_in"]), block("w_in", mom["w_in"]), block("w_in", var["w_in"]),
                                       "reduce_adamw_w_in"))

    small_shapes = [(1, PROJ_W), (2, MIX_W), (CONV_K, MIX_W)] + [(1, MIX_W)] * 3 + [(1, D_MODEL)] * 5 + [(1, 1)]
    summed = _sum_slots(gathered, "sum_small")
    small = _unpack(summed, small_shapes)
    loss_total = small[-1].reshape(())
    gsm = dict(zip(_SMALL, small[:-1]))
    place = 2 * lax.axis_index("x") + lax.axis_index("y")
    conv_cols = ml_conv_w.shape[-1]
    gsm["ml_conv_w"] = lax.dynamic_slice(gsm["ml_conv_w"], (0, place * conv_cols), (CONV_K, conv_cols))
    own_shapes = [wts[k].shape for k in _SMALL]
    g_pack = _pack([gsm[k] for k in _SMALL])
    res = _adamw(g_pack[None], _pack([wts[k] for k in _SMALL]), _pack([mom[k] for k in _SMALL]), _pack([var[k] for k in _SMALL]), "adamw_small")
    for dst, pack in zip((out_g, out_d, out_m, out_v), res):
        for k, a in zip(_SMALL, _unpack(pack, own_shapes)):
            dst[k] = a

    outs = [loss_total, grad_x[None]]
    for group in (out_g, out_d, out_m, out_v):
        outs += [group[k] for k in _ORDER]
    return tuple(outs)
```
